```python
import jax, jax.numpy as jnp
from jax import lax
import numpy as np

D_MODEL = 1024
BATCH = 8
SEQ = 4096
DEPTH = 1

MEM_LEN = 256
D_MIX = D_MODEL
POOL_WINDOWS = (2, 4, 8, 16)
N_POOL_GROUPS = len(POOL_WINDOWS)
D_POOL = D_MIX // 4
POOL_GROUP_DIM = D_POOL // N_POOL_GROUPS
D_FOX = D_MIX - D_POOL
FOX_HEAD_DIM = 64
FOX_HEADS = D_FOX // FOX_HEAD_DIM
Q_BLOCK = 128
XA_HEADS = 4
XA_HEAD_DIM = D_MODEL // XA_HEADS
D_FF = 4 * D_MODEL
CONV_WIDTH = 3
NORM_EPS = 1e-6
D_IN = D_POOL + 3 * D_FOX + FOX_HEADS

kernel_name = 'hybrid_pool_fox_memxattn_convffn'


def rms_norm(x, g):
    xf = x.astype(jnp.float32)
    y = xf * lax.rsqrt(jnp.mean(xf * xf, axis=-1, keepdims=True) + NORM_EPS)
    return (y * g.astype(jnp.float32)).astype(x.dtype)


def pool_mixer(u, w_pool, pool_scale):
    b, s, _ = u.shape
    ug = u.astype(jnp.float32).reshape(b, s, N_POOL_GROUPS, POOL_GROUP_DIM)
    csum = jnp.pad(jnp.cumsum(ug, axis=1), ((0, 0), (1, 0), (0, 0), (0, 0)))
    t1 = jnp.arange(1, s + 1, dtype=jnp.float32)
    pooled = []
    for g, w in enumerate(POOL_WINDOWS):
        c = csum[:, :, g]
        lower = jnp.concatenate([jnp.zeros((b, w - 1, POOL_GROUP_DIM), c.dtype), c[:, :s + 1 - w]], axis=1)
        win_sum = c[:, 1:] - lower
        count = jnp.minimum(t1, float(w))[None, :, None]
        pooled.append(win_sum / count)
    pooled = jnp.stack(pooled, axis=2)
    diff = (pooled - ug).astype(u.dtype)
    mixed = jnp.einsum('bsgc,gcd->bsgd', diff, w_pool)
    return (mixed * pool_scale).reshape(b, s, D_POOL)


def forgetting_attention(q, k, v, log_f):
    b, s, h, dh = q.shape
    cum = jnp.cumsum(log_f, axis=1).transpose(0, 2, 1)
    scale = dh ** -0.5
    outs = []
    for i in range(s // Q_BLOCK):
        q0, q1 = i * Q_BLOCK, (i + 1) * Q_BLOCK
        qb, kb, vb = q[:, q0:q1], k[:, :q1], v[:, :q1]
        logits = jnp.einsum('bqhd,bkhd->bhqk', qb, kb).astype(jnp.float32) * scale
        decay = cum[:, :, q0:q1, None] - cum[:, :, None, :q1]
        mask = (q0 + jnp.arange(Q_BLOCK))[:, None] >= jnp.arange(q1)[None, :]
        logits = jnp.where(mask, logits + decay, -jnp.inf)
        probs = jax.nn.softmax(logits, axis=-1)
        outs.append(jnp.einsum('bhqk,bkhd->bqhd', probs.astype(v.dtype), vb))
    return jnp.concatenate(outs, axis=1)


def memory_cross_attention(h, mem_n, w_xq, w_xkv, w_xo):
    b, s, _ = h.shape
    m = mem_n.shape[1]
    q = (h @ w_xq).reshape(b, s, XA_HEADS, XA_HEAD_DIM)
    kv = mem_n @ w_xkv
    k = kv[..., :D_MODEL].reshape(b, m, XA_HEADS, XA_HEAD_DIM)
    v = kv[..., D_MODEL:].reshape(b, m, XA_HEADS, XA_HEAD_DIM)
    logits = jnp.einsum('bshd,bmhd->bhsm', q, k).astype(jnp.float32) * (XA_HEAD_DIM ** -0.5)
    probs = jax.nn.softmax(logits, axis=-1)
    out = jnp.einsum('bhsm,bmhd->bshd', probs.astype(v.dtype), v).reshape(b, s, D_MODEL)
    return out @ w_xo


def conv_gated_mlp(h, w_up, conv_w, conv_b, w_down):
    hid = h @ w_up
    hid = lax.conv_general_dilated(
        hid, conv_w, window_strides=(1,), padding=[(CONV_WIDTH - 1, 0)],
        dimension_numbers=('NWC', 'WIO', 'NWC'), feature_group_count=2 * D_FF) + conv_b
    gate, up = hid[..., :D_FF], hid[..., D_FF:]
    return (jax.nn.gelu(gate, approximate=True) * up) @ w_down


def _fwd_setup_inputs(seed: int = 0) -> dict:
    key = jax.random.key(seed)
    ks = jax.random.split(key, 22)
    nrm = lambda k, shape, s: jax.random.normal(k, shape, jnp.float32) * s
    gain = lambda k: 1.0 + 0.1 * jax.random.normal(k, (DEPTH, D_MODEL), jnp.float32)
    return {
        'x': jax.random.normal(ks[0], (BATCH, SEQ, D_MODEL), jnp.float32),
        'mem': jax.random.normal(ks[1], (BATCH, MEM_LEN, D_MODEL), jnp.float32),
        'norm_mix_pre': gain(ks[2]),
        'norm_mix_post': gain(ks[3]),
        'w_in': nrm(ks[4], (DEPTH, D_MODEL, D_IN), D_MODEL ** -0.5),
        'b_forget': jax.random.uniform(ks[5], (DEPTH, FOX_HEADS), jnp.float32, 1.0, 6.0),
        'w_pool': nrm(ks[6], (DEPTH, N_POOL_GROUPS, POOL_GROUP_DIM, POOL_GROUP_DIM), POOL_GROUP_DIM ** -0.5),
        'pool_scale': 1.0 + 0.1 * jax.random.normal(ks[7], (DEPTH, N_POOL_GROUPS, POOL_GROUP_DIM), jnp.float32),
        'w_mix_out': nrm(ks[8], (DEPTH, D_MIX, D_MODEL), D_MIX ** -0.5),
        'norm_mem': gain(ks[9]),
        'norm_xa_pre': gain(ks[10]),
        'norm_xa_post': gain(ks[11]),
        'w_xq': nrm(ks[12], (DEPTH, D_MODEL, D_MODEL), D_MODEL ** -0.5),
        'w_xkv': nrm(ks[13], (DEPTH, D_MODEL, 2 * D_MODEL), D_MODEL ** -0.5),
        'w_xo': nrm(ks[14], (DEPTH, D_MODEL, D_MODEL), D_MODEL ** -0.5),
        'norm_ffn_pre': gain(ks[15]),
        'norm_ffn_post': gain(ks[16]),
        'w_up': nrm(ks[17], (DEPTH, D_MODEL, 2 * D_FF), D_MODEL ** -0.5),
        'conv_w': nrm(ks[18], (DEPTH, CONV_WIDTH, 1, 2 * D_FF), CONV_WIDTH ** -0.5),
        'conv_b': nrm(ks[19], (DEPTH, 2 * D_FF), 0.02),
        'w_down': nrm(ks[20], (DEPTH, D_FF, D_MODEL), D_FF ** -0.5),
    }


def _fwd_reference(x, mem, norm_mix_pre, norm_mix_post, w_in, b_forget, w_pool, pool_scale, w_mix_out,
              norm_mem, norm_xa_pre, norm_xa_post, w_xq, w_xkv, w_xo,
              norm_ffn_pre, norm_ffn_post, w_up, conv_w, conv_b, w_down):
    b, s, _ = x.shape
    for l in range(DEPTH):
        h = rms_norm(x, norm_mix_pre[l])
        proj = h @ w_in[l]
        u_pool = proj[..., :D_POOL]
        o = D_POOL
        q = proj[..., o:o + D_FOX].reshape(b, s, FOX_HEADS, FOX_HEAD_DIM)
        k = proj[..., o + D_FOX:o + 2 * D_FOX].reshape(b, s, FOX_HEADS, FOX_HEAD_DIM)
        v = proj[..., o + 2 * D_FOX:o + 3 * D_FOX].reshape(b, s, FOX_HEADS, FOX_HEAD_DIM)
        f_logit = proj[..., o + 3 * D_FOX:].astype(jnp.float32) + b_forget[l].astype(jnp.float32)
        log_f = jax.nn.log_sigmoid(f_logit)
        y_pool = pool_mixer(u_pool, w_pool[l], pool_scale[l])
        y_fox = forgetting_attention(q, k, v, log_f).reshape(b, s, D_FOX)
        y = jnp.concatenate([y_pool, y_fox], axis=-1) @ w_mix_out[l]
        x = x + rms_norm(y, norm_mix_post[l])
        h = rms_norm(x, norm_xa_pre[l])
        mem_n = rms_norm(mem, norm_mem[l])
        y = memory_cross_attention(h, mem_n, w_xq[l], w_xkv[l], w_xo[l])
        x = x + rms_norm(y, norm_xa_post[l])
        h = rms_norm(x, norm_ffn_pre[l])
        y = conv_gated_mlp(h, w_up[l], conv_w[l], conv_b[l], w_down[l])
        x = x + rms_norm(y, norm_ffn_post[l])
    return x


import jax as _jax
import jax.numpy as _jnp

TWIN_FORMAT = 'train_step'
FWD_PARAMS = ['x', 'mem', 'norm_mix_pre', 'norm_mix_post', 'w_in', 'b_forget', 'w_pool', 'pool_scale', 'w_mix_out', 'norm_mem', 'norm_xa_pre', 'norm_xa_post', 'w_xq', 'w_xkv', 'w_xo', 'norm_ffn_pre', 'norm_ffn_post', 'w_up', 'conv_w', 'conv_b', 'w_down']
TWIN_WEIGHTS = ['norm_mix_pre', 'norm_mix_post', 'w_in', 'b_forget', 'w_pool', 'pool_scale', 'w_mix_out', 'norm_mem', 'norm_xa_pre', 'norm_xa_post', 'w_xq', 'w_xkv', 'w_xo', 'norm_ffn_pre', 'norm_ffn_post', 'w_up', 'conv_w', 'conv_b', 'w_down']
TWIN_DIFF_INPUT = 'x'
TWIN_INPUTS = ['x', 'mem', 'norm_mix_pre', 'norm_mix_post', 'w_in', 'b_forget', 'w_pool', 'pool_scale', 'w_mix_out', 'norm_mem', 'norm_xa_pre', 'norm_xa_post', 'w_xq', 'w_xkv', 'w_xo', 'norm_ffn_pre', 'norm_ffn_post', 'w_up', 'conv_w', 'conv_b', 'w_down', 'loss_target', 'm_norm_mix_pre', 'm_norm_mix_post', 'm_w_in', 'm_b_forget', 'm_w_pool', 'm_pool_scale', 'm_w_mix_out', 'm_norm_mem', 'm_norm_xa_pre', 'm_norm_xa_post', 'm_w_xq', 'm_w_xkv', 'm_w_xo', 'm_norm_ffn_pre', 'm_norm_ffn_post', 'm_w_up', 'm_conv_w', 'm_conv_b', 'm_w_down', 'v_norm_mix_pre', 'v_norm_mix_post', 'v_w_in', 'v_b_forget', 'v_w_pool', 'v_pool_scale', 'v_w_mix_out', 'v_norm_mem', 'v_norm_xa_pre', 'v_norm_xa_post', 'v_w_xq', 'v_w_xkv', 'v_w_xo', 'v_norm_ffn_pre', 'v_norm_ffn_post', 'v_w_up', 'v_conv_w', 'v_conv_b', 'v_w_down']
TWIN_OUTPUTS = ['loss', 'grad_x', 'grad_norm_mix_pre', 'grad_norm_mix_post', 'grad_w_in', 'grad_b_forget', 'grad_w_pool', 'grad_pool_scale', 'grad_w_mix_out', 'grad_norm_mem', 'grad_norm_xa_pre', 'grad_norm_xa_post', 'grad_w_xq', 'grad_w_xkv', 'grad_w_xo', 'grad_norm_ffn_pre', 'grad_norm_ffn_post', 'grad_w_up', 'grad_conv_w', 'grad_conv_b', 'grad_w_down', 'delta_norm_mix_pre', 'delta_norm_mix_post', 'delta_w_in', 'delta_b_forget', 'delta_w_pool', 'delta_pool_scale', 'delta_w_mix_out', 'delta_norm_mem', 'delta_norm_xa_pre', 'delta_norm_xa_post', 'delta_w_xq', 'delta_w_xkv', 'delta_w_xo', 'delta_norm_ffn_pre', 'delta_norm_ffn_post', 'delta_w_up', 'delta_conv_w', 'delta_conv_b', 'delta_w_down', 'new_m_norm_mix_pre', 'new_m_norm_mix_post', 'new_m_w_in', 'new_m_b_forget', 'new_m_w_pool', 'new_m_pool_scale', 'new_m_w_mix_out', 'new_m_norm_mem', 'new_m_norm_xa_pre', 'new_m_norm_xa_post', 'new_m_w_xq', 'new_m_w_xkv', 'new_m_w_xo', 'new_m_norm_ffn_pre', 'new_m_norm_ffn_post', 'new_m_w_up', 'new_m_conv_w', 'new_m_conv_b', 'new_m_w_down', 'new_v_norm_mix_pre', 'new_v_norm_mix_post', 'new_v_w_in', 'new_v_b_forget', 'new_v_w_pool', 'new_v_pool_scale', 'new_v_w_mix_out', 'new_v_norm_mem', 'new_v_norm_xa_pre', 'new_v_norm_xa_post', 'new_v_w_xq', 'new_v_w_xkv', 'new_v_w_xo', 'new_v_norm_ffn_pre', 'new_v_norm_ffn_post', 'new_v_w_up', 'new_v_conv_w', 'new_v_conv_b', 'new_v_w_down']
TWIN_LEAF_KINDS = {'loss': 'loss', 'grad_x': 'grad_x', 'grad_norm_mix_pre': 'grad_w', 'grad_norm_mix_post': 'grad_w', 'grad_w_in': 'grad_w', 'grad_b_forget': 'grad_w', 'grad_w_pool': 'grad_w', 'grad_pool_scale': 'grad_w', 'grad_w_mix_out': 'grad_w', 'grad_norm_mem': 'grad_w', 'grad_norm_xa_pre': 'grad_w', 'grad_norm_xa_post': 'grad_w', 'grad_w_xq': 'grad_w', 'grad_w_xkv': 'grad_w', 'grad_w_xo': 'grad_w', 'grad_norm_ffn_pre': 'grad_w', 'grad_norm_ffn_post': 'grad_w', 'grad_w_up': 'grad_w', 'grad_conv_w': 'grad_w', 'grad_conv_b': 'grad_w', 'grad_w_down': 'grad_w', 'delta_norm_mix_pre': 'delta_w', 'delta_norm_mix_post': 'delta_w', 'delta_w_in': 'delta_w', 'delta_b_forget': 'delta_w', 'delta_w_pool': 'delta_w', 'delta_pool_scale': 'delta_w', 'delta_w_mix_out': 'delta_w', 'delta_norm_mem': 'delta_w', 'delta_norm_xa_pre': 'delta_w', 'delta_norm_xa_post': 'delta_w', 'delta_w_xq': 'delta_w', 'delta_w_xkv': 'delta_w', 'delta_w_xo': 'delta_w', 'delta_norm_ffn_pre': 'delta_w', 'delta_norm_ffn_post': 'delta_w', 'delta_w_up': 'delta_w', 'delta_conv_w': 'delta_w', 'delta_conv_b': 'delta_w', 'delta_w_down': 'delta_w', 'new_m_norm_mix_pre': 'new_m', 'new_m_norm_mix_post': 'new_m', 'new_m_w_in': 'new_m', 'new_m_b_forget': 'new_m', 'new_m_w_pool': 'new_m', 'new_m_pool_scale': 'new_m', 'new_m_w_mix_out': 'new_m', 'new_m_norm_mem': 'new_m', 'new_m_norm_xa_pre': 'new_m', 'new_m_norm_xa_post': 'new_m', 'new_m_w_xq': 'new_m', 'new_m_w_xkv': 'new_m', 'new_m_w_xo': 'new_m', 'new_m_norm_ffn_pre': 'new_m', 'new_m_norm_ffn_post': 'new_m', 'new_m_w_up': 'new_m', 'new_m_conv_w': 'new_m', 'new_m_conv_b': 'new_m', 'new_m_w_down': 'new_m', 'new_v_norm_mix_pre': 'new_v', 'new_v_norm_mix_post': 'new_v', 'new_v_w_in': 'new_v', 'new_v_b_forget': 'new_v', 'new_v_w_pool': 'new_v', 'new_v_pool_scale': 'new_v', 'new_v_w_mix_out': 'new_v', 'new_v_norm_mem': 'new_v', 'new_v_norm_xa_pre': 'new_v', 'new_v_norm_xa_post': 'new_v', 'new_v_w_xq': 'new_v', 'new_v_w_xkv': 'new_v', 'new_v_w_xo': 'new_v', 'new_v_norm_ffn_pre': 'new_v', 'new_v_norm_ffn_post': 'new_v', 'new_v_w_up': 'new_v', 'new_v_conv_w': 'new_v', 'new_v_conv_b': 'new_v', 'new_v_w_down': 'new_v'}


def _forward(args):
    return _fwd_reference(*[args[k] for k in FWD_PARAMS])


def _output_shape():
    def fwd():
        inp = _fwd_setup_inputs(0)
        return _fwd_reference(*[inp[k] for k in FWD_PARAMS])
    out = _jax.eval_shape(fwd)
    return out.shape, out.dtype

N_MICROBATCH = 1
ADAM_LR = 0.001
ADAM_B1 = 0.9
ADAM_B2 = 0.999
ADAM_EPS = 1e-08
ADAM_WD = 0.01
ADAM_STEP = 10
PER_EXAMPLE_BATCH_AXIS = {'x': 0, 'mem': 0, 'loss_target': 0}
SHARED_INPUTS = []
_WEIGHT_DTYPES = {'norm_mix_pre': _jnp.float32, 'norm_mix_post': _jnp.float32, 'w_in': _jnp.float32, 'b_forget': _jnp.float32, 'w_pool': _jnp.float32, 'pool_scale': _jnp.float32, 'w_mix_out': _jnp.float32, 'norm_mem': _jnp.float32, 'norm_xa_pre': _jnp.float32, 'norm_xa_post': _jnp.float32, 'w_xq': _jnp.float32, 'w_xkv': _jnp.float32, 'w_xo': _jnp.float32, 'norm_ffn_pre': _jnp.float32, 'norm_ffn_post': _jnp.float32, 'w_up': _jnp.float32, 'conv_w': _jnp.float32, 'conv_b': _jnp.float32, 'w_down': _jnp.float32}
MOMENT_SCALE = {'norm_mix_pre': 9.350893e-01, 'norm_mix_post': 3.235758e+01, 'w_in': 6.046166e-01, 'b_forget': 1.588713e+00, 'w_pool': 2.310045e+00, 'pool_scale': 3.330878e+00, 'w_mix_out': 1.364573e+00, 'norm_mem': 2.272437e+00, 'norm_xa_pre': 7.342365e-01, 'norm_xa_post': 3.269725e+01, 'w_xq': 8.390341e-01, 'w_xkv': 1.441401e+00, 'w_xo': 1.954564e+00, 'norm_ffn_pre': 1.280708e+00, 'norm_ffn_post': 3.224637e+01, 'w_up': 4.761459e-01, 'conv_w': 5.707758e-01, 'conv_b': 1.465860e+00, 'w_down': 1.275745e+00}


def _to_microbatches(a, axis):
    t = _jnp.moveaxis(a, axis, 0)
    t = t.reshape((N_MICROBATCH, t.shape[0] // N_MICROBATCH) + t.shape[1:])
    return _jnp.moveaxis(t, 1, axis + 1)


def setup_inputs(seed: int = 0) -> dict:
    inp = _fwd_setup_inputs(seed)
    key = _jax.random.fold_in(_jax.random.key(seed), 7919)
    shape, _ = _output_shape()
    out = dict(inp)
    out["loss_target"] = _jax.random.normal(_jax.random.fold_in(key, 0), shape, _jnp.float32)
    for i, name in enumerate(TWIN_WEIGHTS):
        w = inp[name].astype(_jnp.float32)
        if MOMENT_SCALE is None:
            s = _jnp.sqrt(_jnp.mean(_jnp.square(w)) + 1e-30)
        else:
            s = MOMENT_SCALE[name]
        km, kv = _jax.random.split(_jax.random.fold_in(key, i + 1))
        out[name] = w
        out["m_" + name] = s * _jax.random.normal(km, w.shape, _jnp.float32)
        out["v_" + name] = (s * s) * _jax.random.uniform(kv, w.shape, _jnp.float32, 0.5, 1.5)
    if N_MICROBATCH > 1:
        for name, axis in PER_EXAMPLE_BATCH_AXIS.items():
            out[name] = _to_microbatches(out[name], axis)
    return {'x': out['x'], 'mem': out['mem'], 'norm_mix_pre': out['norm_mix_pre'], 'norm_mix_post': out['norm_mix_post'], 'w_in': out['w_in'], 'b_forget': out['b_forget'], 'w_pool': out['w_pool'], 'pool_scale': out['pool_scale'], 'w_mix_out': out['w_mix_out'], 'norm_mem': out['norm_mem'], 'norm_xa_pre': out['norm_xa_pre'], 'norm_xa_post': out['norm_xa_post'], 'w_xq': out['w_xq'], 'w_xkv': out['w_xkv'], 'w_xo': out['w_xo'], 'norm_ffn_pre': out['norm_ffn_pre'], 'norm_ffn_post': out['norm_ffn_post'], 'w_up': out['w_up'], 'conv_w': out['conv_w'], 'conv_b': out['conv_b'], 'w_down': out['w_down'], 'loss_target': out['loss_target'], 'm_norm_mix_pre': out['m_norm_mix_pre'], 'm_norm_mix_post': out['m_norm_mix_post'], 'm_w_in': out['m_w_in'], 'm_b_forget': out['m_b_forget'], 'm_w_pool': out['m_w_pool'], 'm_pool_scale': out['m_pool_scale'], 'm_w_mix_out': out['m_w_mix_out'], 'm_norm_mem': out['m_norm_mem'], 'm_norm_xa_pre': out['m_norm_xa_pre'], 'm_norm_xa_post': out['m_norm_xa_post'], 'm_w_xq': out['m_w_xq'], 'm_w_xkv': out['m_w_xkv'], 'm_w_xo': out['m_w_xo'], 'm_norm_ffn_pre': out['m_norm_ffn_pre'], 'm_norm_ffn_post': out['m_norm_ffn_post'], 'm_w_up': out['m_w_up'], 'm_conv_w': out['m_conv_w'], 'm_conv_b': out['m_conv_b'], 'm_w_down': out['m_w_down'], 'v_norm_mix_pre': out['v_norm_mix_pre'], 'v_norm_mix_post': out['v_norm_mix_post'], 'v_w_in': out['v_w_in'], 'v_b_forget': out['v_b_forget'], 'v_w_pool': out['v_w_pool'], 'v_pool_scale': out['v_pool_scale'], 'v_w_mix_out': out['v_w_mix_out'], 'v_norm_mem': out['v_norm_mem'], 'v_norm_xa_pre': out['v_norm_xa_pre'], 'v_norm_xa_post': out['v_norm_xa_post'], 'v_w_xq': out['v_w_xq'], 'v_w_xkv': out['v_w_xkv'], 'v_w_xo': out['v_w_xo'], 'v_norm_ffn_pre': out['v_norm_ffn_pre'], 'v_norm_ffn_post': out['v_norm_ffn_post'], 'v_w_up': out['v_w_up'], 'v_conv_w': out['v_conv_w'], 'v_conv_b': out['v_conv_b'], 'v_w_down': out['v_w_down']}


def _loss(weights, diff, rest, loss_target):
    with _jax.named_scope("forward"):
        args = {**rest, TWIN_DIFF_INPUT: diff, **{k: w.astype(_WEIGHT_DTYPES[k]) for k, w in weights.items()}}
        y = _forward(args)
    with _jax.named_scope("loss_head"):
        err = _jnp.square(y.astype(_jnp.float32) - loss_target)
        return 0.5 * _jnp.sum(_jnp.mean(err, axis=-1)) if err.ndim else 0.5 * err


def _adamw(w, g, m, v):
    m = ADAM_B1 * m + (1.0 - ADAM_B1) * g
    v = ADAM_B2 * v + (1.0 - ADAM_B2) * _jnp.square(g)
    m_hat = m / (1.0 - ADAM_B1 ** ADAM_STEP)
    v_hat = v / (1.0 - ADAM_B2 ** ADAM_STEP)
    delta = -ADAM_LR * (m_hat / (_jnp.sqrt(v_hat) + ADAM_EPS) + ADAM_WD * w)
    return delta, m, v


def reference(x, mem, norm_mix_pre, norm_mix_post, w_in, b_forget, w_pool, pool_scale, w_mix_out, norm_mem, norm_xa_pre, norm_xa_post, w_xq, w_xkv, w_xo, norm_ffn_pre, norm_ffn_post, w_up, conv_w, conv_b, w_down, loss_target, m_norm_mix_pre, m_norm_mix_post, m_w_in, m_b_forget, m_w_pool, m_pool_scale, m_w_mix_out, m_norm_mem, m_norm_xa_pre, m_norm_xa_post, m_w_xq, m_w_xkv, m_w_xo, m_norm_ffn_pre, m_norm_ffn_post, m_w_up, m_conv_w, m_conv_b, m_w_down, v_norm_mix_pre, v_norm_mix_post, v_w_in, v_b_forget, v_w_pool, v_pool_scale, v_w_mix_out, v_norm_mem, v_norm_xa_pre, v_norm_xa_post, v_w_xq, v_w_xkv, v_w_xo, v_norm_ffn_pre, v_norm_ffn_post, v_w_up, v_conv_w, v_conv_b, v_w_down):
    given = dict(x=x, mem=mem, norm_mix_pre=norm_mix_pre, norm_mix_post=norm_mix_post, w_in=w_in, b_forget=b_forget, w_pool=w_pool, pool_scale=pool_scale, w_mix_out=w_mix_out, norm_mem=norm_mem, norm_xa_pre=norm_xa_pre, norm_xa_post=norm_xa_post, w_xq=w_xq, w_xkv=w_xkv, w_xo=w_xo, norm_ffn_pre=norm_ffn_pre, norm_ffn_post=norm_ffn_post, w_up=w_up, conv_w=conv_w, conv_b=conv_b, w_down=w_down, loss_target=loss_target, m_norm_mix_pre=m_norm_mix_pre, m_norm_mix_post=m_norm_mix_post, m_w_in=m_w_in, m_b_forget=m_b_forget, m_w_pool=m_w_pool, m_pool_scale=m_pool_scale, m_w_mix_out=m_w_mix_out, m_norm_mem=m_norm_mem, m_norm_xa_pre=m_norm_xa_pre, m_norm_xa_post=m_norm_xa_post, m_w_xq=m_w_xq, m_w_xkv=m_w_xkv, m_w_xo=m_w_xo, m_norm_ffn_pre=m_norm_ffn_pre, m_norm_ffn_post=m_norm_ffn_post, m_w_up=m_w_up, m_conv_w=m_conv_w, m_conv_b=m_conv_b, m_w_down=m_w_down, v_norm_mix_pre=v_norm_mix_pre, v_norm_mix_post=v_norm_mix_post, v_w_in=v_w_in, v_b_forget=v_b_forget, v_w_pool=v_w_pool, v_pool_scale=v_pool_scale, v_w_mix_out=v_w_mix_out, v_norm_mem=v_norm_mem, v_norm_xa_pre=v_norm_xa_pre, v_norm_xa_post=v_norm_xa_post, v_w_xq=v_w_xq, v_w_xkv=v_w_xkv, v_w_xo=v_w_xo, v_norm_ffn_pre=v_norm_ffn_pre, v_norm_ffn_post=v_norm_ffn_post, v_w_up=v_w_up, v_conv_w=v_conv_w, v_conv_b=v_conv_b, v_w_down=v_w_down)
    weights = {n: given[n] for n in TWIN_WEIGHTS}
    shared = {n: given[n] for n in SHARED_INPUTS}
    per_example = {n: given[n] for n in ['x', 'mem']}
    grad_fn = _jax.value_and_grad(_loss, argnums=(0, 1))

    def one_microbatch(ex, loss_target):
        ex = dict(ex)
        diff = ex.pop(TWIN_DIFF_INPUT)
        return grad_fn(weights, diff, {**shared, **ex}, loss_target)

    if N_MICROBATCH == 1:
        loss, (grad_w, grad_x) = one_microbatch(per_example, given["loss_target"])
    else:
        def body(carry, xs):
            loss_sum, grad_sum = carry
            l_k, (gw_k, gx_k) = one_microbatch(xs[0], xs[1])
            with _jax.named_scope("update"):
                return (loss_sum + l_k, _jax.tree.map(_jnp.add, grad_sum, gw_k)), gx_k

        init = (_jnp.zeros((), _jnp.float32), _jax.tree.map(_jnp.zeros_like, weights))
        (loss, grad_w), grad_x = _jax.lax.scan(body, init, (per_example, given["loss_target"]))
    with _jax.named_scope("update"):
        delta_w, new_m, new_v = {}, {}, {}
        for n in TWIN_WEIGHTS:
            delta_w[n], new_m[n], new_v[n] = _adamw(weights[n], grad_w[n], given["m_" + n], given["v_" + n])
    return (loss, grad_x, *[grad_w[n] for n in TWIN_WEIGHTS], *[delta_w[n] for n in TWIN_WEIGHTS],
            *[new_m[n] for n in TWIN_WEIGHTS], *[new_v[n] for n in TWIN_WEIGHTS])
```

```python
import functools

import jax
import jax.numpy as jnp
from jax import lax
from jax.experimental import pallas as pl
from jax.experimental.pallas import tpu as pltpu

F32 = jnp.float32
BF16 = jnp.bfloat16
MESH = pl.DeviceIdType.MESH
ANY = pl.BlockSpec(memory_space=pl.ANY)
VMEM_SPEC = pl.BlockSpec(memory_space=pltpu.VMEM)

S = 4096
D = 1024
MEM = 256
D_POOL = 256
HEADS = 12
DH = 64
D_FOX = HEADS * DH
D_IN = D_POOL + 3 * D_FOX + HEADS
F_OFF = D_POOL + 3 * D_FOX
Q_OFF, K_OFF, V_OFF = D_POOL, D_POOL + D_FOX, D_POOL + 2 * D_FOX
XA_HEADS = 4
XA_DH = 256
D_FF = 4096
EPS = 1e-6
N_CHIPS = 4
ADAM_LR, ADAM_B1, ADAM_B2, ADAM_EPS, ADAM_WD, ADAM_STEP = 0.001, 0.9, 0.999, 1e-08, 0.01, 10

LANES = 128
SUBLANES = 8
D_IN_PAD = 21 * LANES
TR = 512
NEG = -1e30
VMEM_LIMIT = 52 * 1024 * 1024

NN = (((1,), (0,)), ((), ()))
NT = (((1,), (1,)), ((), ()))
TN = (((0,), (0,)), ((), ()))


def _dot(a, b, dims=NN):
    return lax.dot_general(a, b, dims, preferred_element_type=F32)


def _params(sem):
    return pltpu.CompilerParams(dimension_semantics=sem, vmem_limit_bytes=VMEM_LIMIT)


def _split3(x):
    hi = x.astype(BF16)
    r = x - hi.astype(F32)
    mid = r.astype(BF16)
    lo = (r - mid.astype(F32)).astype(BF16)
    return hi, mid, lo


def _split3_f32(x):
    hi = x.astype(BF16).astype(F32)
    r = x - hi
    mid = r.astype(BF16).astype(F32)
    return hi, mid, r - mid


def _lane_iota(shape):
    return lax.broadcasted_iota(jnp.int32, shape, len(shape) - 1)


def _row_iota(shape):
    return lax.broadcasted_iota(jnp.int32, shape, len(shape) - 2)


def _mm(name, a, b, a_spec, b_spec, out_shape, out_spec, grid, dims, acc_shape):
    nk = grid[2]

    def body(a_ref, b_ref, o_ref, *scr):
        p = _dot(a_ref[...], b_ref[...], dims)
        if nk == 1:
            o_ref[...] = p.astype(o_ref.dtype)
        else:
            acc = scr[0]
            k = pl.program_id(2)

            @pl.when(k == 0)
            def _():
                acc[...] = p

            @pl.when(k > 0)
            def _():
                acc[...] += p

            @pl.when(k == nk - 1)
            def _():
                o_ref[...] = acc[...].astype(o_ref.dtype)

    return pl.pallas_call(
        body, name=name, grid=grid, in_specs=[a_spec, b_spec], out_specs=out_spec, out_shape=out_shape,
        scratch_shapes=[pltpu.VMEM(acc_shape, F32)] if nk > 1 else [],
        compiler_params=_params(("parallel", "parallel", "arbitrary")),
    )(a, b)


def _mm_nn(name, a, b, out_dtype, tm, tn):
    m, k = a.shape
    n = b.shape[1]
    return _mm(name, a, b, pl.BlockSpec((tm, k), lambda i, j, kk: (i, 0)), pl.BlockSpec((k, tn), lambda i, j, kk: (0, j)),
               jax.ShapeDtypeStruct((m, n), out_dtype), pl.BlockSpec((tm, tn), lambda i, j, kk: (i, j)),
               (m // tm, n // tn, 1), NN, (tm, tn))


def _mm_nt(name, a, b, out_dtype, tm, tn):
    m, k = a.shape
    n = b.shape[0]
    return _mm(name, a, b, pl.BlockSpec((tm, k), lambda i, j, kk: (i, 0)), pl.BlockSpec((tn, k), lambda i, j, kk: (j, 0)),
               jax.ShapeDtypeStruct((m, n), out_dtype), pl.BlockSpec((tm, tn), lambda i, j, kk: (i, j)),
               (m // tm, n // tn, 1), NT, (tm, tn))


def _mm_tn(name, a, b, tka, tn):
    t, ka = a.shape
    n = b.shape[1]
    return _mm(name, a, b, pl.BlockSpec((t, tka), lambda i, j, kk: (0, i)), pl.BlockSpec((t, tn), lambda i, j, kk: (0, j)),
               jax.ShapeDtypeStruct((ka, n), F32), pl.BlockSpec((tka, tn), lambda i, j, kk: (i, j)),
               (ka // tka, n // tn, 1), TN, (tka, tn))


def _rms(x, g):
    r = lax.rsqrt(jnp.mean(x * x, axis=-1, keepdims=True) + EPS)
    return x * r * g


def _rms_bwd(x, g, dy):
    r = lax.rsqrt(jnp.mean(x * x, axis=-1, keepdims=True) + EPS)
    xh = x * r
    dxh = dy * g
    dx = r * (dxh - xh * jnp.mean(dxh * xh, axis=-1, keepdims=True))
    return dx, jnp.sum(dy * xh, axis=0, keepdims=True)


def _row_spec(tr, width):
    return pl.BlockSpec((tr, width), lambda i: (i, 0))


def _vec_spec(width):
    return pl.BlockSpec((1, width), lambda i: (0, 0))


def _norm_fwd(name, x, g):
    rows, width = x.shape
    tr = min(TR, rows)

    def body(x_ref, g_ref, h_ref):
        h_ref[...] = _rms(x_ref[...], g_ref[...]).astype(BF16)

    return pl.pallas_call(
        body, name=name, grid=(rows // tr,), in_specs=[_row_spec(tr, width), _vec_spec(width)],
        out_specs=_row_spec(tr, width), out_shape=jax.ShapeDtypeStruct((rows, width), BF16),
        compiler_params=_params(("parallel",)),
    )(x, g)


def _resid_norm(name, xp, y, g_post, g_pre):
    def body(xp_ref, y_ref, gpost_ref, gpre_ref, xn_ref, h_ref):
        xn = xp_ref[...] + _rms(y_ref[...], gpost_ref[...])
        xn_ref[...] = xn
        h_ref[...] = _rms(xn, gpre_ref[...]).astype(BF16)

    return pl.pallas_call(
        body, name=name, grid=(S // TR,), in_specs=[_row_spec(TR, D), _row_spec(TR, D), _vec_spec(D), _vec_spec(D)],
        out_specs=[_row_spec(TR, D), _row_spec(TR, D)],
        out_shape=[jax.ShapeDtypeStruct((S, D), F32), jax.ShapeDtypeStruct((S, D), BF16)],
        compiler_params=_params(("parallel",)),
    )(xp, y, g_post, g_pre)


def _loss_bwd(x3, y3, g_post, target):
    def body(x_ref, y_ref, g_ref, t_ref, dres_ref, dy_ref, dg_ref, loss_ref):
        i = pl.program_id(0)

        @pl.when(i == 0)
        def _():
            dg_ref[...] = jnp.zeros_like(dg_ref)
            loss_ref[...] = jnp.zeros_like(loss_ref)

        y = y_ref[...]
        g = g_ref[...]
        e = x_ref[...] + _rms(y, g) - t_ref[...]
        loss_ref[...] += jnp.sum(e * e, axis=0, keepdims=True) * (0.5 / D)
        dres = e * (1.0 / D)
        dres_ref[...] = dres
        dy, dg = _rms_bwd(y, g, dres)
        dy_ref[...] = dy.astype(BF16)
        dg_ref[...] += dg

    return pl.pallas_call(
        body, name="loss_bwd", grid=(S // TR,),
        in_specs=[_row_spec(TR, D), _row_spec(TR, D), _vec_spec(D), _row_spec(TR, D)],
        out_specs=[_row_spec(TR, D), _row_spec(TR, D), _vec_spec(D), _vec_spec(D)],
        out_shape=[jax.ShapeDtypeStruct((S, D), F32), jax.ShapeDtypeStruct((S, D), BF16),
                   jax.ShapeDtypeStruct((1, D), F32), jax.ShapeDtypeStruct((1, D), F32)],
        compiler_params=_params(("arbitrary",)),
    )(x3, y3, g_post, target)


def _mid_bwd(name, dres, xcur, g_pre, dh, yprev, g_post):
    def body(dres_ref, x_ref, gpre_ref, dh_ref, y_ref, gpost_ref, dx_ref, dy_ref, dgpre_ref, dgpost_ref):
        i = pl.program_id(0)

        @pl.when(i == 0)
        def _():
            dgpre_ref[...] = jnp.zeros_like(dgpre_ref)
            dgpost_ref[...] = jnp.zeros_like(dgpost_ref)

        dxn, dgpre = _rms_bwd(x_ref[...], gpre_ref[...], dh_ref[...])
        dx = dres_ref[...] + dxn
        dx_ref[...] = dx
        dy, dgpost = _rms_bwd(y_ref[...], gpost_ref[...], dx)
        dy_ref[...] = dy.astype(BF16)
        dgpre_ref[...] += dgpre
        dgpost_ref[...] += dgpost

    return pl.pallas_call(
        body, name=name, grid=(S // TR,),
        in_specs=[_row_spec(TR, D), _row_spec(TR, D), _vec_spec(D), _row_spec(TR, D), _row_spec(TR, D), _vec_spec(D)],
        out_specs=[_row_spec(TR, D), _row_spec(TR, D), _vec_spec(D), _vec_spec(D)],
        out_shape=[jax.ShapeDtypeStruct((S, D), F32), jax.ShapeDtypeStruct((S, D), BF16),
                   jax.ShapeDtypeStruct((1, D), F32), jax.ShapeDtypeStruct((1, D), F32)],
        compiler_params=_params(("arbitrary",)),
    )(dres, xcur, g_pre, dh, yprev, g_post)


def _first_bwd(dres, x, g, dh):
    def body(dres_ref, x_ref, g_ref, dh_ref, dx_ref, dg_ref):
        i = pl.program_id(0)

        @pl.when(i == 0)
        def _():
            dg_ref[...] = jnp.zeros_like(dg_ref)

        dxn, dg = _rms_bwd(x_ref[...], g_ref[...], dh_ref[...])
        dx_ref[...] = dres_ref[...] + dxn
        dg_ref[...] += dg

    return pl.pallas_call(
        body, name="first_bwd", grid=(S // TR,),
        in_specs=[_row_spec(TR, D), _row_spec(TR, D), _vec_spec(D), _row_spec(TR, D)],
        out_specs=[_row_spec(TR, D), _vec_spec(D)],
        out_shape=[jax.ShapeDtypeStruct((S, D), F32), jax.ShapeDtypeStruct((1, D), F32)],
        compiler_params=_params(("arbitrary",)),
    )(dres, x, g, dh)


def _gain_bwd(name, x, g, dy):
    rows, width = x.shape

    def body(x_ref, g_ref, dy_ref, dg_ref):
        _, dg = _rms_bwd(x_ref[...], g_ref[...], dy_ref[...])
        dg_ref[...] = dg

    return pl.pallas_call(
        body, name=name, grid=(1,), in_specs=[_row_spec(rows, width), _vec_spec(width), _row_spec(rows, width)],
        out_specs=_vec_spec(width), out_shape=jax.ShapeDtypeStruct((1, width), F32),
        compiler_params=_params(("arbitrary",)),
    )(x, g, dy)


CUM_Q = DH
CUM_K = DH + 3
LSE_Q = DH + 6
DEN_V = DH
DELTA = DH + 1
PREP_TR = 256
BQ = 512
BK = 512


def _head_block(ref, off, h):
    start = off + DH * h
    base = (start // LANES) * LANES
    blk = ref[:, base:base + LANES]
    return pltpu.roll(blk, DH, 1) if start % LANES else blk


def _cumsum_rows(x, tri, carry):
    hi, mid, lo = _split3(x)
    return _dot(tri, hi) + _dot(tri, mid) + _dot(tri, lo) + carry


def _fox_prep(proj, bf_pad):
    tr = PREP_TR

    def body(proj_ref, bf_ref, qa_ref, ka_ref, va_ref, carry_ref):
        i = pl.program_id(0)

        @pl.when(i == 0)
        def _():
            carry_ref[...] = jnp.zeros_like(carry_ref)

        lane = _lane_iota((tr, LANES))
        z = proj_ref[:, F_OFF:F_OFF + LANES] + bf_ref[...]
        log_f = jnp.minimum(z, 0.0) - jnp.log(1.0 + jnp.exp(-jnp.abs(z)))
        log_f = jnp.where(lane < HEADS, log_f, 0.0)
        tri = jnp.where(_row_iota((tr, tr)) >= _lane_iota((tr, tr)), 1.0, 0.0).astype(BF16)
        cum = _cumsum_rows(log_f, tri, carry_ref[0:1, :])
        carry_ref[0:1, :] = cum[tr - 1:tr, :]

        ones_q = jnp.where((lane >= CUM_K) & (lane < CUM_K + 3), 1.0, 0.0)
        ones_k = jnp.where(((lane >= CUM_Q) & (lane < CUM_Q + 3)) | ((lane >= LSE_Q) & (lane < LSE_Q + 3)), 1.0, 0.0)
        aug_v = jnp.where(lane == DEN_V, 1.0, jnp.where((lane >= DELTA) & (lane < DELTA + 3), -1.0, 0.0))
        for h in range(HEADS):
            c_hi, c_mid, c_lo = _split3_f32(cum[:, h:h + 1])
            aug_q = jnp.where(lane == CUM_Q, c_hi, jnp.where(lane == CUM_Q + 1, c_mid, jnp.where(lane == CUM_Q + 2, c_lo, ones_q)))
            aug_k = jnp.where(lane == CUM_K, -c_hi, jnp.where(lane == CUM_K + 1, -c_mid, jnp.where(lane == CUM_K + 2, -c_lo, ones_k)))
            qa_ref[h] = jnp.where(lane < DH, _head_block(proj_ref, Q_OFF, h) * (DH ** -0.5), aug_q).astype(BF16)
            ka_ref[h] = jnp.where(lane < DH, _head_block(proj_ref, K_OFF, h), aug_k).astype(BF16)
            va_ref[h] = jnp.where(lane < DH, _head_block(proj_ref, V_OFF, h), aug_v).astype(BF16)

    head_spec = pl.BlockSpec((HEADS, tr, LANES), lambda i: (0, i, 0))
    head_shape = jax.ShapeDtypeStruct((HEADS, S, LANES), BF16)
    return pl.pallas_call(
        body, name="fox_prep", grid=(S // tr,), in_specs=[_row_spec(tr, D_IN_PAD), _vec_spec(LANES)],
        out_specs=[head_spec] * 3, out_shape=[head_shape] * 3, scratch_shapes=[pltpu.VMEM((SUBLANES, LANES), F32)],
        compiler_params=_params(("arbitrary",)),
    )(proj, bf_pad)


def _fox_fwd(qa, ka, va):
    nq = S // BQ

    def body(qa_ref, ka_ref, va_ref, y_ref, qab_ref, m_scr, acc_scr):
        i = pl.program_id(1)
        lane = _lane_iota((BQ, LANES))
        causal = _row_iota((BQ, BK)) >= _lane_iota((BQ, BK))
        outs = []
        for hh in range(2):
            q = qa_ref[hh]
            m_scr[...] = jnp.full_like(m_scr, NEG)
            acc_scr[...] = jnp.zeros_like(acc_scr)

            def step(j, masked):
                rows = pl.ds(pl.multiple_of(j * BK, BK), BK)
                s = _dot(q, ka_ref[hh, rows, :], NT)
                if masked:
                    s = jnp.where(causal, s, NEG)
                m_prev = m_scr[...]
                m_new = jnp.maximum(m_prev, jnp.max(s, axis=1, keepdims=True))
                p = jnp.exp(s - jnp.tile(m_new, (1, BK // LANES)))
                acc_scr[...] = jnp.exp(m_prev - m_new) * acc_scr[...] + _dot(p.astype(BF16), va_ref[hh, rows, :])
                m_scr[...] = m_new

            def full_step(j, carry):
                step(j, False)
                return carry

            lax.fori_loop(0, i, full_step, 0)
            step(i, True)
            acc = acc_scr[...]
            den = jnp.broadcast_to(acc[:, DEN_V:DEN_V + 1], (BQ, LANES))
            outs.append(acc * (1.0 / den))
            n_hi, n_mid, n_lo = _split3(-(m_scr[...] + jnp.log(den)))
            qab_ref[hh] = jnp.where(lane == LSE_Q, n_hi, jnp.where(lane == LSE_Q + 1, n_mid, jnp.where(lane == LSE_Q + 2, n_lo, q)))
        y_ref[...] = jnp.where(lane < DH, outs[0], pltpu.roll(outs[1], DH, 1)).astype(BF16)

    pair_rows = pl.BlockSpec((2, BQ, LANES), lambda p, i: (p, i, 0))
    pair_all = pl.BlockSpec((2, S, LANES), lambda p, i: (p, 0, 0))
    return pl.pallas_call(
        body, name="fox_fwd", grid=(HEADS // 2, nq), in_specs=[pair_rows, pair_all, pair_all],
        out_specs=[pl.BlockSpec((BQ, LANES), lambda p, i: (i, p)), pair_rows],
        out_shape=[jax.ShapeDtypeStruct((S, D_FOX), BF16), jax.ShapeDtypeStruct((HEADS, S, LANES), BF16)],
        scratch_shapes=[pltpu.VMEM((BQ, LANES), F32), pltpu.VMEM((BQ, LANES), F32)],
        compiler_params=_params(("parallel", "arbitrary")),
    )(qa, ka, va)


def _fox_bwd_prep(dycat, ycat):
    def body(d_ref, y_ref, doa_ref):
        lane = _lane_iota((TR, LANES))
        do = d_ref[...]
        prod = do * y_ref[...].astype(F32)
        low = lane < DH
        deltas = (jnp.sum(jnp.where(low, prod, 0.0), axis=1, keepdims=True),
                  jnp.sum(jnp.where(low, 0.0, prod), axis=1, keepdims=True))
        for hh in range(2):
            d_hi, d_mid, d_lo = _split3_f32(deltas[hh])
            aug = jnp.where(lane == DELTA, d_hi, jnp.where(lane == DELTA + 1, d_mid, jnp.where(lane == DELTA + 2, d_lo, 0.0)))
            do_h = do if hh == 0 else pltpu.roll(do, DH, 1)
            doa_ref[hh] = jnp.where(low, do_h, aug).astype(BF16)

    col = D_POOL // LANES
    blk = pl.BlockSpec((TR, LANES), lambda p, i: (i, col + p))
    return pl.pallas_call(
        body, name="fox_bwd_prep", grid=(HEADS // 2, S // TR), in_specs=[blk, blk],
        out_specs=pl.BlockSpec((2, TR, LANES), lambda p, i: (p, i, 0)),
        out_shape=jax.ShapeDtypeStruct((HEADS, S, LANES), BF16),
        compiler_params=_params(("parallel", "parallel")),
    )(dycat, ycat)


def _fox_bwd(qab, doa, ka, va):
    nk = S // BK

    def body(qab_ref, doa_ref, ka_ref, va_ref, dqa_ref, dka_ref, dva_ref):
        j = pl.program_id(1)

        @pl.when(j == 0)
        def _():
            dqa_ref[...] = jnp.zeros_like(dqa_ref)

        causal = _row_iota((BQ, BK)) >= _lane_iota((BQ, BK))
        for hh in range(2):
            kb = ka_ref[hh]
            vb = va_ref[hh]
            dka_ref[hh] = jnp.zeros((BK, LANES), F32)
            dva_ref[hh] = jnp.zeros((BK, LANES), F32)

            def step(i, masked):
                rows = pl.ds(pl.multiple_of(i * BQ, BQ), BQ)
                q = qab_ref[hh, rows, :]
                do = doa_ref[hh, rows, :]
                s = _dot(q, kb, NT)
                if masked:
                    s = jnp.where(causal, s, NEG)
                p = jnp.exp(s)
                ds = p * _dot(do, vb, NT)
                pb = p.astype(BF16)
                dsb = ds.astype(BF16)
                dva_ref[hh] += _dot(pb, do, TN)
                dka_ref[hh] += _dot(dsb, q, TN)
                dqa_ref[hh, rows, :] += _dot(dsb, kb)

            def full_step(i, carry):
                step(i, False)
                return carry

            step(j, True)
            lax.fori_loop(j + 1, nk, full_step, 0)

    pair_all = pl.BlockSpec((2, S, LANES), lambda p, j: (p, 0, 0))
    pair_rows = pl.BlockSpec((2, BK, LANES), lambda p, j: (p, j, 0))
    shape = jax.ShapeDtypeStruct((HEADS, S, LANES), F32)
    return pl.pallas_call(
        body, name="fox_bwd", grid=(HEADS // 2, nk), in_specs=[pair_all, pair_all, pair_rows, pair_rows],
        out_specs=[pair_all, pair_rows, pair_rows], out_shape=[shape] * 3,
        compiler_params=_params(("arbitrary", "arbitrary")),
    )(qab, doa, ka, va)


def _fox_bwd_post(dqa, dka, dva, du, proj, bf_pad):
    tr = PREP_TR
    nt = S // tr

    def body(dqa_ref, dka_ref, dva_ref, du_ref, z_ref, bf_ref, dp_ref, dbf_ref, carry_ref):
        i = pl.program_id(0)

        @pl.when(i == 0)
        def _():
            carry_ref[...] = jnp.zeros_like(carry_ref)
            dbf_ref[...] = jnp.zeros_like(dbf_ref)

        lane = _lane_iota((tr, LANES))
        dcum = jnp.zeros((tr, LANES), F32)
        for h in range(HEADS):
            dc = dqa_ref[h][:, CUM_Q:CUM_Q + 1] - dka_ref[h][:, CUM_K:CUM_K + 1]
            dcum = jnp.where(lane == h, dc, dcum)
        tri = jnp.where(_lane_iota((tr, tr)) >= _row_iota((tr, tr)), 1.0, 0.0).astype(BF16)
        dlog_f = _cumsum_rows(dcum, tri, carry_ref[0:1, :])
        carry_ref[0:1, :] = dlog_f[0:1, :]
        z = z_ref[...] + bf_ref[...]
        df = jnp.where(lane < HEADS, dlog_f / (1.0 + jnp.exp(z)), 0.0)
        dbf_ref[...] += jnp.sum(df, axis=0, keepdims=True)

        dp_ref[:, 0:D_POOL] = du_ref[...].astype(BF16)
        low = lane < DH
        for ref, off, scale in ((dqa_ref, Q_OFF, DH ** -0.5), (dka_ref, K_OFF, 1.0), (dva_ref, V_OFF, 1.0)):
            for p in range(HEADS // 2):
                blk = jnp.where(low, ref[2 * p], pltpu.roll(ref[2 * p + 1], DH, 1))
                dp_ref[:, off + LANES * p:off + LANES * (p + 1)] = (blk * scale).astype(BF16)
        dp_ref[:, F_OFF:F_OFF + LANES] = df.astype(BF16)

    head_spec = pl.BlockSpec((HEADS, tr, LANES), lambda i: (0, nt - 1 - i, 0))
    return pl.pallas_call(
        body, name="fox_bwd_post", grid=(nt,),
        in_specs=[head_spec, head_spec, head_spec, pl.BlockSpec((tr, D_POOL), lambda i: (nt - 1 - i, 0)),
                  pl.BlockSpec((tr, LANES), lambda i: (nt - 1 - i, F_OFF // LANES)), _vec_spec(LANES)],
        out_specs=[pl.BlockSpec((tr, D_IN_PAD), lambda i: (nt - 1 - i, 0)), _vec_spec(LANES)],
        out_shape=[jax.ShapeDtypeStruct((S, D_IN_PAD), BF16), jax.ShapeDtypeStruct((1, LANES), F32)],
        scratch_shapes=[pltpu.VMEM((SUBLANES, LANES), F32)],
        compiler_params=_params(("arbitrary",)),
    )(dqa, dka, dva, du, proj, bf_pad)


POOL_HALO = 16


def _by_group(lane, a2, a4, a8, a16):
    return jnp.where(lane < 64, a2, jnp.where(lane < 128, a4, jnp.where(lane < 192, a8, a16)))


def _window_count(lane, t):
    return jnp.minimum(t + 1, _by_group(lane, 2, 4, 8, 16)).astype(F32)


def _pool_diff(u, halo, first, tile):
    n = TR + POOL_HALO
    ext = jnp.concatenate([jnp.where(first, 0.0, halo), u], axis=0)
    s2 = ext + pltpu.roll(ext, 1, 0)
    s4 = s2 + pltpu.roll(s2, 2, 0)
    s8 = s4 + pltpu.roll(s4, 4, 0)
    s16 = s8 + pltpu.roll(s8, 8, 0)
    lane = _lane_iota((n, D_POOL))
    win = _by_group(lane, s2, s4, s8, s16)[POOL_HALO:]
    lane = _lane_iota((TR, D_POOL))
    t = tile * TR + _row_iota((TR, D_POOL))
    return win / _window_count(lane, t) - u


def _prev_halo(rows, width, col):
    per = TR // rows
    return pl.BlockSpec((rows, width), lambda i: (jnp.maximum(i * per - 1, 0), col))


def _next_halo(rows, width, col):
    per = TR // rows
    return pl.BlockSpec((rows, width), lambda i: (jnp.minimum((i + 1) * per, S // rows - 1), col))


def _pool_fwd(proj, w_bd, ps):
    def body(u_ref, halo_ref, w_ref, ps_ref, y_ref):
        i = pl.program_id(0)
        diff = _pool_diff(u_ref[...], halo_ref[...], i == 0, i)
        y_ref[...] = (_dot(diff.astype(BF16), w_ref[...]) * ps_ref[...]).astype(BF16)

    return pl.pallas_call(
        body, name="pool_fwd", grid=(S // TR,),
        in_specs=[_row_spec(TR, D_POOL), _prev_halo(POOL_HALO, D_POOL, 0),
                  pl.BlockSpec((D_POOL, D_POOL), lambda i: (0, 0)), _vec_spec(D_POOL)],
        out_specs=_row_spec(TR, D_POOL), out_shape=jax.ShapeDtypeStruct((S, D_POOL), BF16),
        compiler_params=_params(("parallel",)),
    )(proj, proj, w_bd, ps)


def _pool_bwd(proj, dycat, w_bd, w_bd_t, ps):
    nt = S // TR
    n = TR + POOL_HALO

    def body(u_ref, halo_ref, dy_ref, dyn_ref, w_ref, wt_ref, ps_ref, du_ref, dw_ref, dps_ref):
        i = pl.program_id(0)

        @pl.when(i == 0)
        def _():
            dw_ref[...] = jnp.zeros_like(dw_ref)
            dps_ref[...] = jnp.zeros_like(dps_ref)

        diff = _pool_diff(u_ref[...], halo_ref[...], i == 0, i).astype(BF16)
        dy = dy_ref[...]
        dps_ref[...] += jnp.sum(dy * _dot(diff, w_ref[...]), axis=0, keepdims=True)
        dy_ext = jnp.concatenate([dy, jnp.where(i == nt - 1, 0.0, dyn_ref[...])], axis=0)
        dmixed = (dy_ext * ps_ref[...]).astype(BF16)
        ddiff = _dot(dmixed, wt_ref[...])
        dw_ref[...] += _dot(diff, dmixed[:TR], TN)
        lane = _lane_iota((n, D_POOL))
        t = i * TR + _row_iota((n, D_POOL))
        e = ddiff / _window_count(lane, t)
        f2 = e + pltpu.roll(e, n - 1, 0)
        f4 = f2 + pltpu.roll(f2, n - 2, 0)
        f8 = f4 + pltpu.roll(f4, n - 4, 0)
        f16 = f8 + pltpu.roll(f8, n - 8, 0)
        du_ref[...] = _by_group(lane, f2, f4, f8, f16)[:TR] - ddiff[:TR]

    mat = pl.BlockSpec((D_POOL, D_POOL), lambda i: (0, 0))
    return pl.pallas_call(
        body, name="pool_bwd", grid=(nt,),
        in_specs=[_row_spec(TR, D_POOL), _prev_halo(POOL_HALO, D_POOL, 0), _row_spec(TR, D_POOL),
                  _next_halo(POOL_HALO, D_POOL, 0), mat, mat, _vec_spec(D_POOL)],
        out_specs=[_row_spec(TR, D_POOL), mat, _vec_spec(D_POOL)],
        out_shape=[jax.ShapeDtypeStruct((S, D_POOL), F32), jax.ShapeDtypeStruct((D_POOL, D_POOL), F32),
                   jax.ShapeDtypeStruct((1, D_POOL), F32)],
        compiler_params=_params(("arbitrary",)),
    )(proj, proj, dycat, dycat, w_bd, w_bd_t, ps)


def _xa_probs(q, k):
    s = _dot(q, k, NT) * (XA_DH ** -0.5)
    e = jnp.exp(s - jnp.max(s, axis=-1, keepdims=True))
    return e * (1.0 / jnp.sum(e, axis=-1, keepdims=True))


def _xattn_fwd(qx, kv):
    def body(q_ref, kv_ref, o_ref):
        for h in range(XA_HEADS):
            cols = slice(XA_DH * h, XA_DH * (h + 1))
            vcols = slice(D + XA_DH * h, D + XA_DH * (h + 1))
            p = _xa_probs(q_ref[:, cols], kv_ref[:, cols])
            o_ref[:, cols] = _dot(p.astype(BF16), kv_ref[:, vcols]).astype(BF16)

    return pl.pallas_call(
        body, name="xattn_fwd", grid=(S // TR,),
        in_specs=[_row_spec(TR, D), pl.BlockSpec((MEM, 2 * D), lambda i: (0, 0))],
        out_specs=_row_spec(TR, D), out_shape=jax.ShapeDtypeStruct((S, D), BF16),
        compiler_params=_params(("parallel",)),
    )(qx, kv)


def _xattn_bwd(qx, kv, dxo):
    def body(q_ref, kv_ref, do_ref, dq_ref, dkv_ref):
        i = pl.program_id(0)

        @pl.when(i == 0)
        def _():
            dkv_ref[...] = jnp.zeros_like(dkv_ref)

        for h in range(XA_HEADS):
            cols = slice(XA_DH * h, XA_DH * (h + 1))
            vcols = slice(D + XA_DH * h, D + XA_DH * (h + 1))
            q = q_ref[:, cols]
            k = kv_ref[:, cols]
            do = do_ref[:, cols]
            p = _xa_probs(q, k)
            dkv_ref[:, vcols] += _dot(p.astype(BF16), do, TN)
            dp = _dot(do, kv_ref[:, vcols], NT)
            ds = (p * (dp - jnp.sum(p * dp, axis=-1, keepdims=True)) * (XA_DH ** -0.5)).astype(BF16)
            dq_ref[:, cols] = _dot(ds, k).astype(BF16)
            dkv_ref[:, cols] += _dot(ds, q, TN)

    kv_spec = pl.BlockSpec((MEM, 2 * D), lambda i: (0, 0))
    return pl.pallas_call(
        body, name="xattn_bwd", grid=(S // TR,), in_specs=[_row_spec(TR, D), kv_spec, _row_spec(TR, D)],
        out_specs=[_row_spec(TR, D), kv_spec],
        out_shape=[jax.ShapeDtypeStruct((S, D), BF16), jax.ShapeDtypeStruct((MEM, 2 * D), F32)],
        compiler_params=_params(("arbitrary",)),
    )(qx, kv, dxo)


CONV_HALO = SUBLANES
TC = 512
GELU_K = 0.7978845608028654
GELU_C = 0.044715


def _conv3(ext, w, rows):
    h0 = ext[CONV_HALO:CONV_HALO + rows]
    h1 = pltpu.roll(ext, 1, 0)[CONV_HALO:CONV_HALO + rows]
    h2 = pltpu.roll(ext, 2, 0)[CONV_HALO:CONV_HALO + rows]
    return w[2:3] * h0 + w[1:2] * h1 + w[0:1] * h2 + w[3:4], (h2, h1, h0)


def _conv_specs():
    main = pl.BlockSpec((2, TR, TC), lambda j, i: (0, i, j))
    per = TR // CONV_HALO
    prev = pl.BlockSpec((2, CONV_HALO, TC), lambda j, i: (0, jnp.maximum(i * per - 1, 0), j))
    nxt = pl.BlockSpec((2, CONV_HALO, TC), lambda j, i: (0, jnp.minimum((i + 1) * per, S // CONV_HALO - 1), j))
    par = pl.BlockSpec((2, SUBLANES, TC), lambda j, i: (0, 0, j))
    return main, prev, nxt, par


def _convgate_fwd(hid, cwb):
    def body(h_ref, hp_ref, w_ref, act_ref):
        i = pl.program_id(1)
        c = []
        for g in range(2):
            ext = jnp.concatenate([jnp.where(i == 0, 0.0, hp_ref[g]), h_ref[g]], axis=0)
            c.append(_conv3(ext, w_ref[g], TR)[0])
        gate, up = c
        act_ref[...] = (jax.nn.gelu(gate, approximate=True) * up).astype(BF16)

    main, prev, _, par = _conv_specs()
    return pl.pallas_call(
        body, name="convgate_fwd", grid=(D_FF // TC, S // TR), in_specs=[main, prev, par],
        out_specs=pl.BlockSpec((TR, TC), lambda j, i: (i, j)), out_shape=jax.ShapeDtypeStruct((S, D_FF), BF16),
        compiler_params=_params(("parallel", "parallel")),
    )(hid, hid, cwb)


def _convgate_bwd(hid, dact, cwb):
    nr = S // TR
    n = TR + CONV_HALO

    def body(h_ref, hp_ref, hn_ref, da_ref, dan_ref, w_ref, dh_ref, dw_ref):
        i = pl.program_id(1)

        @pl.when(i == 0)
        def _():
            dw_ref[...] = jnp.zeros_like(dw_ref)

        da = jnp.concatenate([da_ref[...], jnp.where(i == nr - 1, 0.0, dan_ref[...])], axis=0)
        c, taps = [], []
        for g in range(2):
            ext = jnp.concatenate([jnp.where(i == 0, 0.0, hp_ref[g]), h_ref[g], hn_ref[g]], axis=0)
            cg, tg = _conv3(ext, w_ref[g], n)
            c.append(cg)
            taps.append(tg)
        gate, up = c
        th = jnp.tanh(GELU_K * (gate + GELU_C * gate * gate * gate))
        gelu = 0.5 * gate * (1.0 + th)
        dgelu = 0.5 * (1.0 + th) + 0.5 * gate * (1.0 - th * th) * GELU_K * (1.0 + 3.0 * GELU_C * gate * gate)
        for g, dc in enumerate((da * up * dgelu, da * gelu)):
            w = w_ref[g]
            dh = w[2:3] * dc[:TR] + w[1:2] * pltpu.roll(dc, n - 1, 0)[:TR] + w[0:1] * pltpu.roll(dc, n - 2, 0)[:TR]
            dh_ref[g] = dh.astype(BF16)
            dcm = dc[:TR]
            for r in range(3):
                dw_ref[g, r:r + 1, :] += jnp.sum(dcm * taps[g][r][:TR], axis=0, keepdims=True)
            dw_ref[g, 3:4, :] += jnp.sum(dcm, axis=0, keepdims=True)

    main, prev, nxt, par = _conv_specs()
    per = TR // CONV_HALO
    return pl.pallas_call(
        body, name="convgate_bwd", grid=(D_FF // TC, nr),
        in_specs=[main, prev, nxt, pl.BlockSpec((TR, TC), lambda j, i: (i, j)),
                  pl.BlockSpec((CONV_HALO, TC), lambda j, i: (jnp.minimum((i + 1) * per, S // CONV_HALO - 1), j)), par],
        out_specs=[main, par],
        out_shape=[jax.ShapeDtypeStruct((2, S, D_FF), BF16), jax.ShapeDtypeStruct((2, SUBLANES, D_FF), F32)],
        compiler_params=_params(("parallel", "arbitrary")),
    )(hid, hid, hid, dact, dact, cwb)


def _adamw(name, w, g, m, v):
    rows, cols = w.shape
    tr = min(_row_tile(rows), 256)

    def body(w_ref, g_ref, m_ref, v_ref, d_ref, nm_ref, nv_ref):
        g = g_ref[...]
        m = ADAM_B1 * m_ref[...] + (1.0 - ADAM_B1) * g
        v = ADAM_B2 * v_ref[...] + (1.0 - ADAM_B2) * (g * g)
        nm_ref[...] = m
        nv_ref[...] = v
        m_hat = m / (1.0 - ADAM_B1 ** ADAM_STEP)
        v_hat = v / (1.0 - ADAM_B2 ** ADAM_STEP)
        d_ref[...] = -ADAM_LR * (m_hat / (jnp.sqrt(v_hat) + ADAM_EPS) + ADAM_WD * w_ref[...])

    spec = _row_spec(tr, cols)
    shape = jax.ShapeDtypeStruct((rows, cols), F32)
    return pl.pallas_call(
        body, name=name, grid=(rows // tr,), in_specs=[spec] * 4, out_specs=[spec] * 3, out_shape=[shape] * 3,
        compiler_params=_params(("parallel",)),
    )(w, g, m, v)


def _flat_rows(shape):
    n_elems = 1
    for d in shape:
        n_elems *= d
    for cols in (2048, 1024, 512, 256, 128):
        if n_elems % (cols * SUBLANES) == 0:
            return n_elems // cols, cols
    return n_elems // shape[-1], shape[-1]


def _row_tile(rows):
    for tr in (512, 256, 128, 64, 32, 16, 8):
        if rows % tr == 0:
            return tr
    raise ValueError(rows)


def _add2(name, a, b):
    rows, cols = a.shape
    tr = _row_tile(rows)

    def body(a_ref, b_ref, o_ref):
        o_ref[...] = a_ref[...] + b_ref[...]

    spec = _row_spec(tr, cols)
    return pl.pallas_call(
        body, name=name, grid=(rows // tr,), in_specs=[spec, spec], out_specs=spec,
        out_shape=jax.ShapeDtypeStruct((rows, cols), F32), compiler_params=_params(("parallel",)),
    )(a, b)


def _sum_slots(name, a):
    _, rows, cols = a.shape
    tr = _row_tile(rows)

    def body(a_ref, o_ref):
        o_ref[...] = ((a_ref[0] + a_ref[1]) + a_ref[2]) + a_ref[3]

    return pl.pallas_call(
        body, name=name, grid=(rows // tr,), in_specs=[pl.BlockSpec((N_CHIPS, tr, cols), lambda i: (0, i, 0))],
        out_specs=_row_spec(tr, cols), out_shape=jax.ShapeDtypeStruct((rows, cols), F32),
        compiler_params=_params(("parallel",)),
    )(a)


CHIP_FLIPS = ((1, 0), (0, 1), (1, 1))


def _place():
    x, y, c = lax.axis_index("x"), lax.axis_index("y"), lax.axis_index("c")
    return x, y, c, 2 * x + y


def _remote(src, dst, sems_s, sems_r, k, dev):
    return pltpu.make_async_remote_copy(src_ref=src, dst_ref=dst, send_sem=sems_s.at[k], recv_sem=sems_r.at[k],
                                        device_id=dev, device_id_type=MESH)


def _comm_call(name, body, ins, out_shapes, n_remote, n_local):
    return pl.pallas_call(
        body, name=name, in_specs=[ANY] * len(ins), out_specs=[ANY] * len(out_shapes), out_shape=out_shapes,
        scratch_shapes=[pltpu.SemaphoreType.DMA((n_remote,)), pltpu.SemaphoreType.DMA((n_remote,)),
                        pltpu.SemaphoreType.DMA((max(n_local, 1),))],
    )(*ins)


def _all_gather_weights(halved, whole):
    nh, nw = len(halved), len(whole)
    n_remote = 6 * nh + 3 * nw

    def body(*refs):
        ins, outs = refs[:nh + nw], refs[nh + nw:2 * (nh + nw)]
        sems_s, sems_r, sems_l = refs[2 * (nh + nw):]
        x, y, c, me = _place()
        sibling = (x, y, 1 - c)
        local, first, passed = [], [], []
        for k in range(nh + nw):
            cp = pltpu.make_async_copy(ins[k], outs[k].at[me], sems_l.at[k])
            cp.start()
            local.append(cp)
        for k in range(nh):
            for f, (fx, fy) in enumerate(CHIP_FLIPS):
                cp = _remote(ins[k].at[c], outs[k].at[me, c], sems_s, sems_r, 6 * k + f, (x ^ fx, y ^ fy, c))
                cp.start()
                first.append(cp)
        for k in range(nw):
            for f, (fx, fy) in enumerate(CHIP_FLIPS):
                cp = _remote(ins[nh + k], outs[nh + k].at[me], sems_s, sems_r, 6 * nh + 3 * k + f, (x ^ fx, y ^ fy, c))
                cp.start()
                first.append(cp)
        for k in range(nh):
            for f, (fx, fy) in enumerate(CHIP_FLIPS):
                other = 2 * (x ^ fx) + (y ^ fy)
                first[3 * k + f].wait_recv()
                cp = _remote(outs[k].at[other, c], outs[k].at[other, c], sems_s, sems_r, 6 * k + 3 + f, sibling)
                cp.start()
                passed.append(cp)
        for cp in first[3 * nh:]:
            cp.wait_recv()
        for cp in passed:
            cp.wait_recv()
        for cp in first + passed:
            cp.wait_send()
        for cp in local:
            cp.wait()

    shapes = [jax.ShapeDtypeStruct((N_CHIPS,) + a.shape, a.dtype) for a in list(halved) + list(whole)]
    return _comm_call("all_gather_weights", body, list(halved) + list(whole), shapes, n_remote, nh + nw)


def _swap_halves(gs):
    n = len(gs)

    def body(*refs):
        ins, outs = refs[:n], refs[n:2 * n]
        sems_s, sems_r, _ = refs[2 * n:]
        x, y, c, _ = _place()
        copies = [_remote(ins[k].at[:, 1 - c], outs[k], sems_s, sems_r, k, (x, y, 1 - c)) for k in range(n)]
        for cp in copies:
            cp.start()
        for cp in copies:
            cp.wait()

    shapes = [jax.ShapeDtypeStruct((g.shape[0],) + g.shape[2:], g.dtype) for g in gs]
    return _comm_call("swap_halves", body, gs, shapes, n, 0)


def _scatter_chips(ps):
    n = len(ps)

    def body(*refs):
        ins, outs = refs[:n], refs[n:2 * n]
        sems_s, sems_r, sems_l = refs[2 * n:]
        x, y, c, me = _place()
        copies = []
        for k in range(n):
            cp = pltpu.make_async_copy(ins[k].at[me], outs[k].at[me], sems_l.at[k])
            cp.start()
            copies.append(cp)
            for f, (fx, fy) in enumerate(CHIP_FLIPS):
                other = 2 * (x ^ fx) + (y ^ fy)
                cp = _remote(ins[k].at[other], outs[k].at[me], sems_s, sems_r, 3 * k + f, (x ^ fx, y ^ fy, c))
                cp.start()
                copies.append(cp)
        for cp in copies:
            cp.wait()

    shapes = [jax.ShapeDtypeStruct(p.shape, p.dtype) for p in ps]
    return _comm_call("scatter_chips", body, ps, shapes, 3 * n, n)


def _join_halves(rs):
    n = len(rs)

    def body(*refs):
        ins, outs = refs[:n], refs[n:2 * n]
        sems_s, sems_r, sems_l = refs[2 * n:]
        x, y, c, _ = _place()
        copies = []
        for k in range(n):
            copies.append(pltpu.make_async_copy(ins[k], outs[k].at[c], sems_l.at[k]))
            copies.append(_remote(ins[k], outs[k].at[c], sems_s, sems_r, k, (x, y, 1 - c)))
        for cp in copies:
            cp.start()
        for cp in copies:
            cp.wait()

    shapes = [jax.ShapeDtypeStruct((2,) + r.shape, r.dtype) for r in rs]
    return _comm_call("join_halves", body, rs, shapes, n, n)


def _all_reduce_small(buf):
    rows = buf.shape[0]
    n_dev = 8

    def body(in_ref, out_ref, gather, sems_s, sems_r):
        x, y, c, _ = _place()
        me = 4 * x + 2 * y + c
        gather[me] = in_ref[...]
        copies = []
        for o in range(1, n_dev):
            dev = (x ^ (o >> 2), y ^ ((o >> 1) & 1), c ^ (o & 1))
            copies.append(_remote(in_ref, gather.at[me], sems_s, sems_r, o - 1, dev))
        for cp in copies:
            cp.start()
        for cp in copies:
            cp.wait()
        acc = gather[0]
        for d in range(1, n_dev):
            acc = acc + gather[d]
        out_ref[...] = acc

    return pl.pallas_call(
        body, name="all_reduce_small", in_specs=[VMEM_SPEC], out_specs=VMEM_SPEC,
        out_shape=jax.ShapeDtypeStruct((rows, LANES), F32),
        scratch_shapes=[pltpu.VMEM((n_dev, rows, LANES), F32), pltpu.SemaphoreType.DMA((n_dev - 1,)),
                        pltpu.SemaphoreType.DMA((n_dev - 1,))],
        compiler_params=pltpu.CompilerParams(vmem_limit_bytes=VMEM_LIMIT),
    )(buf)


def _local_step(x, mem, target, p):
    h1 = _norm_fwd("norm_mix_pre", x, p["norm_mix_pre"])
    proj = _mm_nn("in_proj", h1, p["w_in"], F32, 512, 896)
    qa, ka, va = _fox_prep(proj, p["bf_pad"])
    y_fox, qab = _fox_fwd(qa, ka, va)
    y_pool = _pool_fwd(proj, p["w_pool_bd"], p["pool_scale"])
    ycat = jnp.concatenate([y_pool, y_fox], axis=1)
    y1 = _mm_nn("mix_out", ycat, p["w_mix_out"], F32, 512, 1024)
    x2, h2 = _resid_norm("resid_mix", x, y1, p["norm_mix_post"], p["norm_xa_pre"])
    qx = _mm_nn("xq", h2, p["w_xq"], BF16, 512, 1024)
    mem_n = _norm_fwd("norm_mem", mem, p["norm_mem"])
    kv = _mm(
        "xkv", mem_n, p["w_xkv"], pl.BlockSpec((MEM, D), lambda i, j, k: (0, 0)),
        pl.BlockSpec((None, D, 512), lambda i, j, k: (j, 0, 0)), jax.ShapeDtypeStruct((MEM, 2 * D), BF16),
        pl.BlockSpec((MEM, 512), lambda i, j, k: (0, j)), (1, N_CHIPS, 1), NN, (MEM, 512))
    xo = _xattn_fwd(qx, kv)
    y2 = _mm_nn("xo", xo, p["w_xo"], F32, 512, 1024)
    x3, h3 = _resid_norm("resid_xa", x2, y2, p["norm_xa_post"], p["norm_ffn_pre"])
    hid = _mm(
        "up_proj", h3, p["w_up"], pl.BlockSpec((1024, D), lambda i, j, k: (i, 0)),
        pl.BlockSpec((None, D, 512), lambda i, j, k: (j // 4, 0, j % 4)), jax.ShapeDtypeStruct((2, S, D_FF), F32),
        pl.BlockSpec((None, 1024, 512), lambda i, j, k: (j // 8, i, j % 8)), (S // 1024, 16, 1), NN, (1024, 512))
    act = _convgate_fwd(hid, p["cwb"])
    y3 = _mm_nn("down_proj", act, p["w_down"], F32, 512, 512)

    g = {}
    dres, dy3, g["norm_ffn_post"], loss_cols = _loss_bwd(x3, y3, p["norm_ffn_post"], target)
    dact = _mm_nt("d_act", dy3, p["w_down"], F32, 512, 1024)
    g["w_down"] = _mm_tn("dw_down", act, dy3, 512, 512)
    dhid, dcwb = _convgate_bwd(hid, dact, p["cwb"])
    dh3 = _mm(
        "d_h3", dhid, p["w_up"], pl.BlockSpec((None, 512, 2048), lambda i, j, k: (k // 2, i, k % 2)),
        pl.BlockSpec((None, 512, 2048), lambda i, j, k: (k, j, 0)), jax.ShapeDtypeStruct((S, D), F32),
        pl.BlockSpec((512, 512), lambda i, j, k: (i, j)), (S // 512, 2, N_CHIPS), NT, (512, 512))
    g["w_up"] = _mm(
        "dw_up", h3, dhid, pl.BlockSpec((S, 512), lambda i, j, k: (0, i)),
        pl.BlockSpec((None, S, 512), lambda i, j, k: (j // 8, 0, j % 8)), jax.ShapeDtypeStruct((N_CHIPS, D, 2048), F32),
        pl.BlockSpec((None, 512, 512), lambda i, j, k: (j // 4, i, j % 4)), (2, 16, 1), TN, (512, 512))
    dres, dy2, g["norm_ffn_pre"], g["norm_xa_post"] = _mid_bwd("bwd_ffn_xa", dres, x3, p["norm_ffn_pre"], dh3, y2, p["norm_xa_post"])
    dxo = _mm_nt("d_xo", dy2, p["w_xo"], BF16, 512, 1024)
    g["w_xo"] = _mm_tn("dw_xo", xo, dy2, 512, 512)
    dqx, dkv = _xattn_bwd(qx, kv, dxo)
    dkv = dkv.astype(BF16)
    dh2 = _mm_nt("d_h2", dqx, p["w_xq"], F32, 512, 1024)
    g["w_xq"] = _mm_tn("dw_xq", h2, dqx, 512, 512)
    dmem_n = _mm(
        "d_mem", dkv, p["w_xkv"], pl.BlockSpec((MEM, 512), lambda i, j, k: (0, k)),
        pl.BlockSpec((None, D, 512), lambda i, j, k: (k, 0, 0)), jax.ShapeDtypeStruct((MEM, D), F32),
        pl.BlockSpec((MEM, D), lambda i, j, k: (0, 0)), (1, 1, N_CHIPS), NT, (MEM, D))
    g["w_xkv"] = _mm(
        "dw_xkv", mem_n, dkv, pl.BlockSpec((MEM, D), lambda i, j, k: (0, 0)),
        pl.BlockSpec((MEM, 512), lambda i, j, k: (0, j)), jax.ShapeDtypeStruct((N_CHIPS, D, 512), F32),
        pl.BlockSpec((None, D, 512), lambda i, j, k: (j, 0, 0)), (1, N_CHIPS, 1), TN, (D, 512))
    g["norm_mem"] = _gain_bwd("dg_mem", mem, p["norm_mem"], dmem_n)
    dres, dy1, g["norm_xa_pre"], g["norm_mix_post"] = _mid_bwd("bwd_xa_mix", dres, x2, p["norm_xa_pre"], dh2, y1, p["norm_mix_post"])
    dycat = _mm_nt("d_ycat", dy1, p["w_mix_out"], F32, 512, 1024)
    g["w_mix_out"] = _mm_tn("dw_mix_out", ycat, dy1, 512, 512)
    doa = _fox_bwd_prep(dycat, ycat)
    dqa, dka, dva = _fox_bwd(qab, doa, ka, va)
    du, g["w_pool_full"], g["pool_scale"] = _pool_bwd(proj, dycat, p["w_pool_bd"], p["w_pool_bd_t"], p["pool_scale"])
    dproj, g["bf_pad"] = _fox_bwd_post(dqa, dka, dva, du, proj, p["bf_pad"])
    dh1 = _mm_nt("d_h1", dproj, p["w_in"], F32, 512, 512)
    g["w_in"] = _mm_tn("dw_in", h1, dproj, 512, 896)
    grad_x, g["norm_mix_pre"] = _first_bwd(dres, x, p["norm_mix_pre"], dh1)
    g["cwb"] = dcwb
    return grad_x, g, loss_cols


BIG = ("w_in", "w_mix_out", "w_xq", "w_xkv", "w_xo", "w_up", "w_down")
ROW_SHARDED = ("w_mix_out", "w_xq", "w_xo", "w_down")
SMALL = ("norm_mix_pre", "norm_mix_post", "b_forget", "w_pool", "pool_scale", "norm_mem", "norm_xa_pre", "norm_xa_post",
         "norm_ffn_pre", "norm_ffn_post", "conv_b")
ORDER = ("norm_mix_pre", "norm_mix_post", "w_in", "b_forget", "w_pool", "pool_scale", "w_mix_out", "norm_mem", "norm_xa_pre",
         "norm_xa_post", "w_xq", "w_xkv", "w_xo", "norm_ffn_pre", "norm_ffn_post", "w_up", "conv_w", "conv_b", "w_down")
SLOT = SUBLANES * LANES


def _pack(parts):
    rows, offs, off = [], [], 0
    for a in parts:
        flat = a.reshape(-1).astype(F32)
        n = -(-flat.shape[0] // SLOT) * SLOT
        rows.append(jnp.pad(flat, (0, n - flat.shape[0])).reshape(n // LANES, LANES))
        offs.append(off)
        off += n // LANES
    return jnp.concatenate(rows, axis=0), offs


def _unpack(buf, off, like):
    n = like.size
    rows = -(-n // LANES)
    return buf[off:off + rows].reshape(-1)[:n].reshape(like.shape)


def _whole_params(w, full, conv_w_full):
    w_in_full = jnp.pad(jnp.concatenate(list(full["w_in"]), axis=1), ((0, 0), (0, D_IN_PAD - D_IN)))
    w_pool_bd = jnp.zeros((D_POOL, D_POOL), F32)
    for gi in range(4):
        w_pool_bd = w_pool_bd.at[64 * gi:64 * (gi + 1), 64 * gi:64 * (gi + 1)].set(w["w_pool"][0, gi])
    cw2 = conv_w_full.reshape(3, 2, D_FF).transpose(1, 0, 2)
    cwb = jnp.concatenate([cw2, w["conv_b"].reshape(1, 2, D_FF).transpose(1, 0, 2), jnp.zeros((2, 4, D_FF), F32)], axis=1)
    p = {n: w[n] for n in ("norm_mix_pre", "norm_mix_post", "norm_mem", "norm_xa_pre", "norm_xa_post", "norm_ffn_pre",
                           "norm_ffn_post")}
    p.update(
        w_in=w_in_full, bf_pad=jnp.pad(w["b_forget"], ((0, 0), (0, LANES - HEADS))),
        w_pool_bd=w_pool_bd.astype(BF16), w_pool_bd_t=w_pool_bd.T.astype(BF16), pool_scale=w["pool_scale"].reshape(1, D_POOL),
        w_mix_out=full["w_mix_out"].reshape(D, D), w_xq=full["w_xq"].reshape(D, D), w_xkv=full["w_xkv"],
        w_xo=full["w_xo"].reshape(D, D), w_up=full["w_up"], cwb=cwb, w_down=full["w_down"].reshape(D_FF, D))
    return p


def kernel(x, mem, norm_mix_pre, norm_mix_post, w_in, b_forget, w_pool, pool_scale, w_mix_out, norm_mem, norm_xa_pre, norm_xa_post, w_xq, w_xkv, w_xo, norm_ffn_pre, norm_ffn_post, w_up, conv_w, conv_b, w_down, loss_target, m_norm_mix_pre, m_norm_mix_post, m_w_in, m_b_forget, m_w_pool, m_pool_scale, m_w_mix_out, m_norm_mem, m_norm_xa_pre, m_norm_xa_post, m_w_xq, m_w_xkv, m_w_xo, m_norm_ffn_pre, m_norm_ffn_post, m_w_up, m_conv_w, m_conv_b, m_w_down, v_norm_mix_pre, v_norm_mix_post, v_w_in, v_b_forget, v_w_pool, v_pool_scale, v_w_mix_out, v_norm_mem, v_norm_xa_pre, v_norm_xa_post, v_w_xq, v_w_xkv, v_w_xo, v_norm_ffn_pre, v_norm_ffn_post, v_w_up, v_conv_w, v_conv_b, v_w_down):
    w = dict(norm_mix_pre=norm_mix_pre, norm_mix_post=norm_mix_post, w_in=w_in, b_forget=b_forget, w_pool=w_pool,
             pool_scale=pool_scale, w_mix_out=w_mix_out, norm_mem=norm_mem, norm_xa_pre=norm_xa_pre, norm_xa_post=norm_xa_post,
             w_xq=w_xq, w_xkv=w_xkv, w_xo=w_xo, norm_ffn_pre=norm_ffn_pre, norm_ffn_post=norm_ffn_post, w_up=w_up,
             conv_w=conv_w, conv_b=conv_b, w_down=w_down)
    m = dict(norm_mix_pre=m_norm_mix_pre, norm_mix_post=m_norm_mix_post, w_in=m_w_in, b_forget=m_b_forget, w_pool=m_w_pool,
             pool_scale=m_pool_scale, w_mix_out=m_w_mix_out, norm_mem=m_norm_mem, norm_xa_pre=m_norm_xa_pre,
             norm_xa_post=m_norm_xa_post, w_xq=m_w_xq, w_xkv=m_w_xkv, w_xo=m_w_xo, norm_ffn_pre=m_norm_ffn_pre,
             norm_ffn_post=m_norm_ffn_post, w_up=m_w_up, conv_w=m_conv_w, conv_b=m_conv_b, w_down=m_w_down)
    v = dict(norm_mix_pre=v_norm_mix_pre, norm_mix_post=v_norm_mix_post, w_in=v_w_in, b_forget=v_b_forget, w_pool=v_w_pool,
             pool_scale=v_pool_scale, w_mix_out=v_w_mix_out, norm_mem=v_norm_mem, norm_xa_pre=v_norm_xa_pre,
             norm_xa_post=v_norm_xa_post, w_xq=v_w_xq, w_xkv=v_w_xkv, w_xo=v_w_xo, norm_ffn_pre=v_norm_ffn_pre,
             norm_ffn_post=v_norm_ffn_post, w_up=v_w_up, conv_w=v_conv_w, conv_b=v_conv_b, w_down=v_w_down)
    chip = 2 * lax.axis_index("x") + lax.axis_index("y")

    shard2d = {n: w[n][0] for n in BIG}
    halved = [shard2d[n].astype(BF16).reshape(2, shard2d[n].shape[0] // 2, shard2d[n].shape[1]) for n in BIG]
    gathered = _all_gather_weights(halved, [conv_w.reshape(3, -1)])
    full = {n: a.reshape((N_CHIPS,) + shard2d[n].shape) for n, a in zip(BIG, gathered)}
    conv_w_full = jnp.transpose(gathered[-1], (1, 0, 2)).reshape(3, 2 * D_FF)

    p = _whole_params(w, full, conv_w_full)

    grad_x, g, loss_cols = _local_step(x[0], mem[0], loss_target[0], p)

    gw_in = g["w_in"][:, :D_IN]
    stacked = dict(g)
    stacked["w_in"] = jnp.stack([gw_in[:, 643 * j:643 * (j + 1)] for j in range(N_CHIPS)])
    for n in ROW_SHARDED:
        stacked[n] = g[n].reshape((N_CHIPS,) + shard2d[n].shape)
    views = [stacked[n].reshape(N_CHIPS, 2, shard2d[n].shape[0] // 2, shard2d[n].shape[1]) for n in BIG]
    from_sibling = _swap_halves(views)
    c = lax.axis_index("c")
    partial = []
    for n, view, other in zip(BIG, views, from_sibling):
        mine = lax.dynamic_index_in_dim(view, c, axis=1, keepdims=False)
        rows, cols = _flat_rows(mine.shape)
        partial.append(_add2("chip_sum_" + n, mine.reshape(rows, cols), other.reshape(rows, cols)).reshape(mine.shape))
    by_chip = _scatter_chips(partial)
    reduced = []
    for n, a in zip(BIG, by_chip):
        rows, cols = _flat_rows(a.shape[1:])
        reduced.append(_sum_slots("mesh_sum_" + n, a.reshape(N_CHIPS, rows, cols)).reshape(a.shape[1:]))
    joined = _join_halves(reduced)
    grads = {n: a.reshape(shard2d[n].shape) for n, a in zip(BIG, joined)}

    gw_pool = jnp.stack([g["w_pool_full"][64 * gi:64 * (gi + 1), 64 * gi:64 * (gi + 1)] for gi in range(4)])
    dcwb = g["cwb"]
    g_conv_w = dcwb[:, 0:3, :].transpose(1, 0, 2).reshape(3, 2 * D_FF)
    g_conv_b = dcwb[:, 3, :].reshape(2 * D_FF)
    small_g = dict(norm_mix_pre=g["norm_mix_pre"], norm_mix_post=g["norm_mix_post"], b_forget=g["bf_pad"][:, :HEADS],
                   w_pool=gw_pool, pool_scale=g["pool_scale"], norm_mem=g["norm_mem"], norm_xa_pre=g["norm_xa_pre"],
                   norm_xa_post=g["norm_xa_post"], norm_ffn_pre=g["norm_ffn_pre"], norm_ffn_post=g["norm_ffn_post"],
                   conv_b=g_conv_b)
    buf, offs = _pack([small_g[n] for n in SMALL] + [g_conv_w, loss_cols])
    buf = _all_reduce_small(buf)
    for n, off in zip(SMALL, offs):
        grads[n] = _unpack(buf, off, w[n])
    g_conv_w = _unpack(buf, offs[len(SMALL)], conv_w_full)
    grads["conv_w"] = lax.dynamic_slice_in_dim(g_conv_w, chip * (2 * D_FF // N_CHIPS), 2 * D_FF // N_CHIPS, axis=1).reshape(conv_w.shape)
    loss = jnp.sum(_unpack(buf, offs[len(SMALL) + 1], loss_cols))

    delta, new_m, new_v = {}, {}, {}
    for n in BIG:
        d, nm, nv = _adamw("adamw_" + n, shard2d[n], grads[n], m[n][0], v[n][0])
        delta[n], new_m[n], new_v[n] = d[None], nm[None], nv[None]
        grads[n] = grads[n][None]
    small_names = SMALL + ("conv_w",)
    packed = [_pack([d[n] for n in small_names])[0] for d in (w, grads, m, v)]
    offs = _pack([w[n] for n in small_names])[1]
    d, nm, nv = _adamw("adamw_small", *packed)
    for n, off in zip(small_names, offs):
        delta[n], new_m[n], new_v[n] = _unpack(d, off, w[n]), _unpack(nm, off, w[n]), _unpack(nv, off, w[n])

    return (loss, grad_x[None], *[grads[n] for n in ORDER], *[delta[n] for n in ORDER], *[new_m[n] for n in ORDER],
            *[new_v[n] for n in ORDER])
```

```python
import functools

import jax
import jax.numpy as jnp
from jax import lax
from jax.experimental import pallas as pl
from jax.experimental.pallas import tpu as pltpu

F32 = jnp.float32
BF16 = jnp.bfloat16
MESH = pl.DeviceIdType.MESH
ANY = pl.BlockSpec(memory_space=pl.ANY)
VMEM_SPEC = pl.BlockSpec(memory_space=pltpu.VMEM)

S = 4096
D = 1024
MEM = 256
D_POOL = 256
HEADS = 12
DH = 64
D_FOX = HEADS * DH
D_IN = D_POOL + 3 * D_FOX + HEADS
F_OFF = D_POOL + 3 * D_FOX
Q_OFF, K_OFF, V_OFF = D_POOL, D_POOL + D_FOX, D_POOL + 2 * D_FOX
XA_HEADS = 4
XA_DH = 256
D_FF = 4096
EPS = 1e-6
N_CHIPS = 4
ADAM_LR, ADAM_B1, ADAM_B2, ADAM_EPS, ADAM_WD, ADAM_STEP = 0.001, 0.9, 0.999, 1e-08, 0.01, 10

LANES = 128
SUBLANES = 8
D_IN_PAD = 21 * LANES
TR = 512
TILE_BYTES = 2 * 1024 * 1024
NEG = -1e30
VMEM_LIMIT = 52 * 1024 * 1024

NN = (((1,), (0,)), ((), ()))
NT = (((1,), (1,)), ((), ()))
TN = (((0,), (0,)), ((), ()))


def _dot(a, b, dims=NN):
    return lax.dot_general(a, b, dims, preferred_element_type=F32)


def _params(sem):
    return pltpu.CompilerParams(dimension_semantics=sem, vmem_limit_bytes=VMEM_LIMIT)


def _split3(x):
    hi = x.astype(BF16)
    r = x - hi.astype(F32)
    mid = r.astype(BF16)
    lo = (r - mid.astype(F32)).astype(BF16)
    return hi, mid, lo


def _split3_f32(x):
    hi = x.astype(BF16).astype(F32)
    r = x - hi
    mid = r.astype(BF16).astype(F32)
    return hi, mid, r - mid


def _lane_iota(shape):
    return lax.broadcasted_iota(jnp.int32, shape, len(shape) - 1)


def _row_iota(shape):
    return lax.broadcasted_iota(jnp.int32, shape, len(shape) - 2)


def _mm(name, a, b, a_spec, b_spec, out_shape, out_spec, grid, dims, acc_shape):
    nk = grid[2]

    def body(a_ref, b_ref, o_ref, *scr):
        p = _dot(a_ref[...], b_ref[...], dims)
        if nk == 1:
            o_ref[...] = p.astype(o_ref.dtype)
        else:
            acc = scr[0]
            k = pl.program_id(2)

            @pl.when(k == 0)
            def _():
                acc[...] = p

            @pl.when(k > 0)
            def _():
                acc[...] += p

            @pl.when(k == nk - 1)
            def _():
                o_ref[...] = acc[...].astype(o_ref.dtype)

    return pl.pallas_call(
        body, name=name, grid=grid, in_specs=[a_spec, b_spec], out_specs=out_spec, out_shape=out_shape,
        scratch_shapes=[pltpu.VMEM(acc_shape, F32)] if nk > 1 else [],
        compiler_params=_params(("parallel", "parallel", "arbitrary")),
    )(a, b)


def _mm_nn(name, a, b, out_dtype, tm, tn):
    m, k = a.shape
    n = b.shape[1]
    return _mm(name, a, b, pl.BlockSpec((tm, k), lambda i, j, kk: (i, 0)), pl.BlockSpec((k, tn), lambda i, j, kk: (0, j)),
               jax.ShapeDtypeStruct((m, n), out_dtype), pl.BlockSpec((tm, tn), lambda i, j, kk: (i, j)),
               (m // tm, n // tn, 1), NN, (tm, tn))


def _mm_nt(name, a, b, out_dtype, tm, tn):
    m, k = a.shape
    n = b.shape[0]
    return _mm(name, a, b, pl.BlockSpec((tm, k), lambda i, j, kk: (i, 0)), pl.BlockSpec((tn, k), lambda i, j, kk: (j, 0)),
               jax.ShapeDtypeStruct((m, n), out_dtype), pl.BlockSpec((tm, tn), lambda i, j, kk: (i, j)),
               (m // tm, n // tn, 1), NT, (tm, tn))


def _mm_tn(name, a, b, tka, tn):
    t, ka = a.shape
    n = b.shape[1]
    return _mm(name, a, b, pl.BlockSpec((t, tka), lambda i, j, kk: (0, i)), pl.BlockSpec((t, tn), lambda i, j, kk: (0, j)),
               jax.ShapeDtypeStruct((ka, n), F32), pl.BlockSpec((tka, tn), lambda i, j, kk: (i, j)),
               (ka // tka, n // tn, 1), TN, (tka, tn))


def _rms(x, g):
    r = lax.rsqrt(jnp.mean(x * x, axis=-1, keepdims=True) + EPS)
    return x * r * g


def _rms_bwd(x, g, dy):
    r = lax.rsqrt(jnp.mean(x * x, axis=-1, keepdims=True) + EPS)
    xh = x * r
    dxh = dy * g
    dx = r * (dxh - xh * jnp.mean(dxh * xh, axis=-1, keepdims=True))
    return dx, jnp.sum(dy * xh, axis=0, keepdims=True)


def _row_spec(tr, width):
    return pl.BlockSpec((tr, width), lambda i: (i, 0))


def _vec_spec(width):
    return pl.BlockSpec((1, width), lambda i: (0, 0))


def _norm_fwd(name, x, g):
    rows, width = x.shape
    tr = min(TR, rows)

    def body(x_ref, g_ref, h_ref):
        h_ref[...] = _rms(x_ref[...], g_ref[...]).astype(BF16)

    return pl.pallas_call(
        body, name=name, grid=(rows // tr,), in_specs=[_row_spec(tr, width), _vec_spec(width)],
        out_specs=_row_spec(tr, width), out_shape=jax.ShapeDtypeStruct((rows, width), BF16),
        compiler_params=_params(("parallel",)),
    )(x, g)


def _resid_norm(name, xp, y, g_post, g_pre):
    def body(xp_ref, y_ref, gpost_ref, gpre_ref, xn_ref, h_ref):
        xn = xp_ref[...] + _rms(y_ref[...], gpost_ref[...])
        xn_ref[...] = xn
        h_ref[...] = _rms(xn, gpre_ref[...]).astype(BF16)

    return pl.pallas_call(
        body, name=name, grid=(S // TR,), in_specs=[_row_spec(TR, D), _row_spec(TR, D), _vec_spec(D), _vec_spec(D)],
        out_specs=[_row_spec(TR, D), _row_spec(TR, D)],
        out_shape=[jax.ShapeDtypeStruct((S, D), F32), jax.ShapeDtypeStruct((S, D), BF16)],
        compiler_params=_params(("parallel",)),
    )(xp, y, g_post, g_pre)


def _loss_bwd(x3, y3, g_post, target):
    def body(x_ref, y_ref, g_ref, t_ref, dres_ref, dy_ref, dg_ref, loss_ref):
        i = pl.program_id(0)

        @pl.when(i == 0)
        def _():
            dg_ref[...] = jnp.zeros_like(dg_ref)
            loss_ref[...] = jnp.zeros_like(loss_ref)

        y = y_ref[...]
        g = g_ref[...]
        e = x_ref[...] + _rms(y, g) - t_ref[...]
        loss_ref[...] += jnp.sum(e * e, axis=0, keepdims=True) * (0.5 / D)
        dres = e * (1.0 / D)
        dres_ref[...] = dres
        dy, dg = _rms_bwd(y, g, dres)
        dy_ref[...] = dy.astype(BF16)
        dg_ref[...] += dg

    return pl.pallas_call(
        body, name="loss_bwd", grid=(S // TR,),
        in_specs=[_row_spec(TR, D), _row_spec(TR, D), _vec_spec(D), _row_spec(TR, D)],
        out_specs=[_row_spec(TR, D), _row_spec(TR, D), _vec_spec(D), _vec_spec(D)],
        out_shape=[jax.ShapeDtypeStruct((S, D), F32), jax.ShapeDtypeStruct((S, D), BF16),
                   jax.ShapeDtypeStruct((1, D), F32), jax.ShapeDtypeStruct((1, D), F32)],
        compiler_params=_params(("arbitrary",)),
    )(x3, y3, g_post, target)


def _mid_bwd(name, dres, xcur, g_pre, dh, yprev, g_post):
    def body(dres_ref, x_ref, gpre_ref, dh_ref, y_ref, gpost_ref, dx_ref, dy_ref, dgpre_ref, dgpost_ref):
        i = pl.program_id(0)

        @pl.when(i == 0)
        def _():
            dgpre_ref[...] = jnp.zeros_like(dgpre_ref)
            dgpost_ref[...] = jnp.zeros_like(dgpost_ref)

        dxn, dgpre = _rms_bwd(x_ref[...], gpre_ref[...], dh_ref[...])
        dx = dres_ref[...] + dxn
        dx_ref[...] = dx
        dy, dgpost = _rms_bwd(y_ref[...], gpost_ref[...], dx)
        dy_ref[...] = dy.astype(BF16)
        dgpre_ref[...] += dgpre
        dgpost_ref[...] += dgpost

    return pl.pallas_call(
        body, name=name, grid=(S // TR,),
        in_specs=[_row_spec(TR, D), _row_spec(TR, D), _vec_spec(D), _row_spec(TR, D), _row_spec(TR, D), _vec_spec(D)],
        out_specs=[_row_spec(TR, D), _row_spec(TR, D), _vec_spec(D), _vec_spec(D)],
        out_shape=[jax.ShapeDtypeStruct((S, D), F32), jax.ShapeDtypeStruct((S, D), BF16),
                   jax.ShapeDtypeStruct((1, D), F32), jax.ShapeDtypeStruct((1, D), F32)],
        compiler_params=_params(("arbitrary",)),
    )(dres, xcur, g_pre, dh, yprev, g_post)


def _first_bwd(dres, x, g, dh):
    def body(dres_ref, x_ref, g_ref, dh_ref, dx_ref, dg_ref):
        i = pl.program_id(0)

        @pl.when(i == 0)
        def _():
            dg_ref[...] = jnp.zeros_like(dg_ref)

        dxn, dg = _rms_bwd(x_ref[...], g_ref[...], dh_ref[...])
        dx_ref[...] = dres_ref[...] + dxn
        dg_ref[...] += dg

    return pl.pallas_call(
        body, name="first_bwd", grid=(S // TR,),
        in_specs=[_row_spec(TR, D), _row_spec(TR, D), _vec_spec(D), _row_spec(TR, D)],
        out_specs=[_row_spec(TR, D), _vec_spec(D)],
        out_shape=[jax.ShapeDtypeStruct((S, D), F32), jax.ShapeDtypeStruct((1, D), F32)],
        compiler_params=_params(("arbitrary",)),
    )(dres, x, g, dh)


def _gain_bwd(name, x, g, dy):
    rows, width = x.shape

    def body(x_ref, g_ref, dy_ref, dg_ref):
        _, dg = _rms_bwd(x_ref[...], g_ref[...], dy_ref[...])
        dg_ref[...] = dg

    return pl.pallas_call(
        body, name=name, grid=(1,), in_specs=[_row_spec(rows, width), _vec_spec(width), _row_spec(rows, width)],
        out_specs=_vec_spec(width), out_shape=jax.ShapeDtypeStruct((1, width), F32),
        compiler_params=_params(("arbitrary",)),
    )(x, g, dy)


CUM_Q = DH
CUM_K = DH + 3
LSE_Q = DH + 6
DEN_V = DH
DELTA = DH + 1
PREP_TR = 256
BQ = 512
BK = 512


def _head_block(ref, off, h):
    start = off + DH * h
    base = (start // LANES) * LANES
    blk = ref[:, base:base + LANES]
    return pltpu.roll(blk, DH, 1) if start % LANES else blk


def _cumsum_rows(x, tri, carry):
    hi, mid, lo = _split3(x)
    return _dot(tri, hi) + _dot(tri, mid) + _dot(tri, lo) + carry


def _fox_prep(proj, bf_pad):
    tr = PREP_TR

    def body(proj_ref, bf_ref, qa_ref, ka_ref, va_ref, carry_ref):
        i = pl.program_id(0)

        @pl.when(i == 0)
        def _():
            carry_ref[...] = jnp.zeros_like(carry_ref)

        lane = _lane_iota((tr, LANES))
        z = proj_ref[:, F_OFF:F_OFF + LANES] + bf_ref[...]
        log_f = jnp.minimum(z, 0.0) - jnp.log(1.0 + jnp.exp(-jnp.abs(z)))
        log_f = jnp.where(lane < HEADS, log_f, 0.0)
        tri = jnp.where(_row_iota((tr, tr)) >= _lane_iota((tr, tr)), 1.0, 0.0).astype(BF16)
        cum = _cumsum_rows(log_f, tri, carry_ref[0:1, :])
        carry_ref[0:1, :] = cum[tr - 1:tr, :]

        ones_q = jnp.where((lane >= CUM_K) & (lane < CUM_K + 3), 1.0, 0.0)
        ones_k = jnp.where(((lane >= CUM_Q) & (lane < CUM_Q + 3)) | ((lane >= LSE_Q) & (lane < LSE_Q + 3)), 1.0, 0.0)
        aug_v = jnp.where(lane == DEN_V, 1.0, jnp.where((lane >= DELTA) & (lane < DELTA + 3), -1.0, 0.0))
        for h in range(HEADS):
            c_hi, c_mid, c_lo = _split3_f32(cum[:, h:h + 1])
            aug_q = jnp.where(lane == CUM_Q, c_hi, jnp.where(lane == CUM_Q + 1, c_mid, jnp.where(lane == CUM_Q + 2, c_lo, ones_q)))
            aug_k = jnp.where(lane == CUM_K, -c_hi, jnp.where(lane == CUM_K + 1, -c_mid, jnp.where(lane == CUM_K + 2, -c_lo, ones_k)))
            qa_ref[h] = jnp.where(lane < DH, _head_block(proj_ref, Q_OFF, h) * (DH ** -0.5), aug_q).astype(BF16)
            ka_ref[h] = jnp.where(lane < DH, _head_block(proj_ref, K_OFF, h), aug_k).astype(BF16)
            va_ref[h] = jnp.where(lane < DH, _head_block(proj_ref, V_OFF, h), aug_v).astype(BF16)

    head_spec = pl.BlockSpec((HEADS, tr, LANES), lambda i: (0, i, 0))
    head_shape = jax.ShapeDtypeStruct((HEADS, S, LANES), BF16)
    return pl.pallas_call(
        body, name="fox_prep", grid=(S // tr,), in_specs=[_row_spec(tr, D_IN_PAD), _vec_spec(LANES)],
        out_specs=[head_spec] * 3, out_shape=[head_shape] * 3, scratch_shapes=[pltpu.VMEM((SUBLANES, LANES), F32)],
        compiler_params=_params(("arbitrary",)),
    )(proj, bf_pad)


def _fox_fwd(qa, ka, va):
    nq = S // BQ

    def body(qa_ref, ka_ref, va_ref, y_ref, qab_ref, m_scr, acc_scr):
        i = pl.program_id(1)
        lane = _lane_iota((BQ, LANES))
        causal = _row_iota((BQ, BK)) >= _lane_iota((BQ, BK))
        outs = []
        for hh in range(2):
            q = qa_ref[hh]
            m_scr[...] = jnp.full_like(m_scr, NEG)
            acc_scr[...] = jnp.zeros_like(acc_scr)

            def step(j, masked):
                rows = pl.ds(pl.multiple_of(j * BK, BK), BK)
                s = _dot(q, ka_ref[hh, rows, :], NT)
                if masked:
                    s = jnp.where(causal, s, NEG)
                m_prev = m_scr[...]
                m_new = jnp.maximum(m_prev, jnp.max(s, axis=1, keepdims=True))
                p = jnp.exp(s - jnp.tile(m_new, (1, BK // LANES)))
                acc_scr[...] = jnp.exp(m_prev - m_new) * acc_scr[...] + _dot(p.astype(BF16), va_ref[hh, rows, :])
                m_scr[...] = m_new

            def full_step(j, carry):
                step(j, False)
                return carry

            lax.fori_loop(0, i, full_step, 0)
            step(i, True)
            acc = acc_scr[...]
            den = jnp.broadcast_to(acc[:, DEN_V:DEN_V + 1], (BQ, LANES))
            outs.append(acc * (1.0 / den))
            n_hi, n_mid, n_lo = _split3(-(m_scr[...] + jnp.log(den)))
            qab_ref[hh] = jnp.where(lane == LSE_Q, n_hi, jnp.where(lane == LSE_Q + 1, n_mid, jnp.where(lane == LSE_Q + 2, n_lo, q)))
        y_ref[...] = jnp.where(lane < DH, outs[0], pltpu.roll(outs[1], DH, 1)).astype(BF16)

    pair_rows = pl.BlockSpec((2, BQ, LANES), lambda p, i: (p, i, 0))
    pair_all = pl.BlockSpec((2, S, LANES), lambda p, i: (p, 0, 0))
    return pl.pallas_call(
        body, name="fox_fwd", grid=(HEADS // 2, nq), in_specs=[pair_rows, pair_all, pair_all],
        out_specs=[pl.BlockSpec((BQ, LANES), lambda p, i: (i, p)), pair_rows],
        out_shape=[jax.ShapeDtypeStruct((S, D_FOX), BF16), jax.ShapeDtypeStruct((HEADS, S, LANES), BF16)],
        scratch_shapes=[pltpu.VMEM((BQ, LANES), F32), pltpu.VMEM((BQ, LANES), F32)],
        compiler_params=_params(("parallel", "arbitrary")),
    )(qa, ka, va)


def _fox_bwd_prep(dycat, ycat):
    def body(d_ref, y_ref, doa_ref):
        lane = _lane_iota((TR, LANES))
        do = d_ref[...]
        prod = do * y_ref[...].astype(F32)
        low = lane < DH
        deltas = (jnp.sum(jnp.where(low, prod, 0.0), axis=1, keepdims=True),
                  jnp.sum(jnp.where(low, 0.0, prod), axis=1, keepdims=True))
        for hh in range(2):
            d_hi, d_mid, d_lo = _split3_f32(deltas[hh])
            aug = jnp.where(lane == DELTA, d_hi, jnp.where(lane == DELTA + 1, d_mid, jnp.where(lane == DELTA + 2, d_lo, 0.0)))
            do_h = do if hh == 0 else pltpu.roll(do, DH, 1)
            doa_ref[hh] = jnp.where(low, do_h, aug).astype(BF16)

    col = D_POOL // LANES
    blk = pl.BlockSpec((TR, LANES), lambda p, i: (i, col + p))
    return pl.pallas_call(
        body, name="fox_bwd_prep", grid=(HEADS // 2, S // TR), in_specs=[blk, blk],
        out_specs=pl.BlockSpec((2, TR, LANES), lambda p, i: (p, i, 0)),
        out_shape=jax.ShapeDtypeStruct((HEADS, S, LANES), BF16),
        compiler_params=_params(("parallel", "parallel")),
    )(dycat, ycat)


def _fox_bwd(qab, doa, ka, va):
    nk = S // BK

    def body(qab_ref, doa_ref, ka_ref, va_ref, dqa_ref, dka_ref, dva_ref):
        j = pl.program_id(1)

        @pl.when(j == 0)
        def _():
            dqa_ref[...] = jnp.zeros_like(dqa_ref)

        causal = _row_iota((BQ, BK)) >= _lane_iota((BQ, BK))
        for hh in range(2):
            kb = ka_ref[hh]
            vb = va_ref[hh]
            dka_ref[hh] = jnp.zeros((BK, LANES), F32)
            dva_ref[hh] = jnp.zeros((BK, LANES), F32)

            def step(i, masked):
                rows = pl.ds(pl.multiple_of(i * BQ, BQ), BQ)
                q = qab_ref[hh, rows, :]
                do = doa_ref[hh, rows, :]
                s = _dot(q, kb, NT)
                if masked:
                    s = jnp.where(causal, s, NEG)
                p = jnp.exp(s)
                ds = p * _dot(do, vb, NT)
                pb = p.astype(BF16)
                dsb = ds.astype(BF16)
                dva_ref[hh] += _dot(pb, do, TN)
                dka_ref[hh] += _dot(dsb, q, TN)
                dqa_ref[hh, rows, :] += _dot(dsb, kb)

            def full_step(i, carry):
                step(i, False)
                return carry

            step(j, True)
            lax.fori_loop(j + 1, nk, full_step, 0)

    pair_all = pl.BlockSpec((2, S, LANES), lambda p, j: (p, 0, 0))
    pair_rows = pl.BlockSpec((2, BK, LANES), lambda p, j: (p, j, 0))
    shape = jax.ShapeDtypeStruct((HEADS, S, LANES), F32)
    return pl.pallas_call(
        body, name="fox_bwd", grid=(HEADS // 2, nk), in_specs=[pair_all, pair_all, pair_rows, pair_rows],
        out_specs=[pair_all, pair_rows, pair_rows], out_shape=[shape] * 3,
        compiler_params=_params(("arbitrary", "arbitrary")),
    )(qab, doa, ka, va)


def _fox_bwd_post(dqa, dka, dva, du, proj, bf_pad):
    tr = PREP_TR
    nt = S // tr

    def body(dqa_ref, dka_ref, dva_ref, du_ref, z_ref, bf_ref, dp_ref, dbf_ref, carry_ref):
        i = pl.program_id(0)

        @pl.when(i == 0)
        def _():
            carry_ref[...] = jnp.zeros_like(carry_ref)
            dbf_ref[...] = jnp.zeros_like(dbf_ref)

        lane = _lane_iota((tr, LANES))
        dcum = jnp.zeros((tr, LANES), F32)
        for h in range(HEADS):
            dc = dqa_ref[h][:, CUM_Q:CUM_Q + 1] - dka_ref[h][:, CUM_K:CUM_K + 1]
            dcum = jnp.where(lane == h, dc, dcum)
        tri = jnp.where(_lane_iota((tr, tr)) >= _row_iota((tr, tr)), 1.0, 0.0).astype(BF16)
        dlog_f = _cumsum_rows(dcum, tri, carry_ref[0:1, :])
        carry_ref[0:1, :] = dlog_f[0:1, :]
        z = z_ref[...] + bf_ref[...]
        df = jnp.where(lane < HEADS, dlog_f / (1.0 + jnp.exp(z)), 0.0)
        dbf_ref[...] += jnp.sum(df, axis=0, keepdims=True)

        dp_ref[:, 0:D_POOL] = du_ref[...].astype(BF16)
        low = lane < DH
        for ref, off, scale in ((dqa_ref, Q_OFF, DH ** -0.5), (dka_ref, K_OFF, 1.0), (dva_ref, V_OFF, 1.0)):
            for p in range(HEADS // 2):
                blk = jnp.where(low, ref[2 * p], pltpu.roll(ref[2 * p + 1], DH, 1))
                dp_ref[:, off + LANES * p:off + LANES * (p + 1)] = (blk * scale).astype(BF16)
        dp_ref[:, F_OFF:F_OFF + LANES] = df.astype(BF16)

    head_spec = pl.BlockSpec((HEADS, tr, LANES), lambda i: (0, nt - 1 - i, 0))
    return pl.pallas_call(
        body, name="fox_bwd_post", grid=(nt,),
        in_specs=[head_spec, head_spec, head_spec, pl.BlockSpec((tr, D_POOL), lambda i: (nt - 1 - i, 0)),
                  pl.BlockSpec((tr, LANES), lambda i: (nt - 1 - i, F_OFF // LANES)), _vec_spec(LANES)],
        out_specs=[pl.BlockSpec((tr, D_IN_PAD), lambda i: (nt - 1 - i, 0)), _vec_spec(LANES)],
        out_shape=[jax.ShapeDtypeStruct((S, D_IN_PAD), BF16), jax.ShapeDtypeStruct((1, LANES), F32)],
        scratch_shapes=[pltpu.VMEM((SUBLANES, LANES), F32)],
        compiler_params=_params(("arbitrary",)),
    )(dqa, dka, dva, du, proj, bf_pad)


POOL_HALO = 16


def _by_group(lane, a2, a4, a8, a16):
    return jnp.where(lane < 64, a2, jnp.where(lane < 128, a4, jnp.where(lane < 192, a8, a16)))


def _window_count(lane, t):
    return jnp.minimum(t + 1, _by_group(lane, 2, 4, 8, 16)).astype(F32)


def _pool_diff(u, halo, first, tile):
    n = TR + POOL_HALO
    ext = jnp.concatenate([jnp.where(first, 0.0, halo), u], axis=0)
    s2 = ext + pltpu.roll(ext, 1, 0)
    s4 = s2 + pltpu.roll(s2, 2, 0)
    s8 = s4 + pltpu.roll(s4, 4, 0)
    s16 = s8 + pltpu.roll(s8, 8, 0)
    lane = _lane_iota((n, D_POOL))
    win = _by_group(lane, s2, s4, s8, s16)[POOL_HALO:]
    lane = _lane_iota((TR, D_POOL))
    t = tile * TR + _row_iota((TR, D_POOL))
    return win / _window_count(lane, t) - u


def _prev_halo(rows, width, col):
    per = TR // rows
    return pl.BlockSpec((rows, width), lambda i: (jnp.maximum(i * per - 1, 0), col))


def _next_halo(rows, width, col):
    per = TR // rows
    return pl.BlockSpec((rows, width), lambda i: (jnp.minimum((i + 1) * per, S // rows - 1), col))


def _pool_fwd(proj, w_bd, ps):
    def body(u_ref, halo_ref, w_ref, ps_ref, y_ref):
        i = pl.program_id(0)
        diff = _pool_diff(u_ref[...], halo_ref[...], i == 0, i)
        y_ref[...] = (_dot(diff.astype(BF16), w_ref[...]) * ps_ref[...]).astype(BF16)

    return pl.pallas_call(
        body, name="pool_fwd", grid=(S // TR,),
        in_specs=[_row_spec(TR, D_POOL), _prev_halo(POOL_HALO, D_POOL, 0),
                  pl.BlockSpec((D_POOL, D_POOL), lambda i: (0, 0)), _vec_spec(D_POOL)],
        out_specs=_row_spec(TR, D_POOL), out_shape=jax.ShapeDtypeStruct((S, D_POOL), BF16),
        compiler_params=_params(("parallel",)),
    )(proj, proj, w_bd, ps)


def _pool_bwd(proj, dycat, w_bd, w_bd_t, ps):
    nt = S // TR
    n = TR + POOL_HALO

    def body(u_ref, halo_ref, dy_ref, dyn_ref, w_ref, wt_ref, ps_ref, du_ref, dw_ref, dps_ref):
        i = pl.program_id(0)

        @pl.when(i == 0)
        def _():
            dw_ref[...] = jnp.zeros_like(dw_ref)
            dps_ref[...] = jnp.zeros_like(dps_ref)

        diff = _pool_diff(u_ref[...], halo_ref[...], i == 0, i).astype(BF16)
        dy = dy_ref[...]
        dps_ref[...] += jnp.sum(dy * _dot(diff, w_ref[...]), axis=0, keepdims=True)
        dy_ext = jnp.concatenate([dy, jnp.where(i == nt - 1, 0.0, dyn_ref[...])], axis=0)
        dmixed = (dy_ext * ps_ref[...]).astype(BF16)
        ddiff = _dot(dmixed, wt_ref[...])
        dw_ref[...] += _dot(diff, dmixed[:TR], TN)
        lane = _lane_iota((n, D_POOL))
        t = i * TR + _row_iota((n, D_POOL))
        e = ddiff / _window_count(lane, t)
        f2 = e + pltpu.roll(e, n - 1, 0)
        f4 = f2 + pltpu.roll(f2, n - 2, 0)
        f8 = f4 + pltpu.roll(f4, n - 4, 0)
        f16 = f8 + pltpu.roll(f8, n - 8, 0)
        du_ref[...] = _by_group(lane, f2, f4, f8, f16)[:TR] - ddiff[:TR]

    mat = pl.BlockSpec((D_POOL, D_POOL), lambda i: (0, 0))
    return pl.pallas_call(
        body, name="pool_bwd", grid=(nt,),
        in_specs=[_row_spec(TR, D_POOL), _prev_halo(POOL_HALO, D_POOL, 0), _row_spec(TR, D_POOL),
                  _next_halo(POOL_HALO, D_POOL, 0), mat, mat, _vec_spec(D_POOL)],
        out_specs=[_row_spec(TR, D_POOL), mat, _vec_spec(D_POOL)],
        out_shape=[jax.ShapeDtypeStruct((S, D_POOL), F32), jax.ShapeDtypeStruct((D_POOL, D_POOL), F32),
                   jax.ShapeDtypeStruct((1, D_POOL), F32)],
        compiler_params=_params(("arbitrary",)),
    )(proj, proj, dycat, dycat, w_bd, w_bd_t, ps)


def _xa_probs(q, k):
    s = _dot(q, k, NT) * (XA_DH ** -0.5)
    e = jnp.exp(s - jnp.max(s, axis=-1, keepdims=True))
    return e * (1.0 / jnp.sum(e, axis=-1, keepdims=True))


def _xattn_fwd(qx, kv):
    def body(q_ref, kv_ref, o_ref):
        for h in range(XA_HEADS):
            cols = slice(XA_DH * h, XA_DH * (h + 1))
            vcols = slice(D + XA_DH * h, D + XA_DH * (h + 1))
            p = _xa_probs(q_ref[:, cols], kv_ref[:, cols])
            o_ref[:, cols] = _dot(p.astype(BF16), kv_ref[:, vcols]).astype(BF16)

    return pl.pallas_call(
        body, name="xattn_fwd", grid=(S // TR,),
        in_specs=[_row_spec(TR, D), pl.BlockSpec((MEM, 2 * D), lambda i: (0, 0))],
        out_specs=_row_spec(TR, D), out_shape=jax.ShapeDtypeStruct((S, D), BF16),
        compiler_params=_params(("parallel",)),
    )(qx, kv)


def _xattn_bwd(qx, kv, dxo):
    def body(q_ref, kv_ref, do_ref, dq_ref, dkv_ref):
        i = pl.program_id(0)

        @pl.when(i == 0)
        def _():
            dkv_ref[...] = jnp.zeros_like(dkv_ref)

        for h in range(XA_HEADS):
            cols = slice(XA_DH * h, XA_DH * (h + 1))
            vcols = slice(D + XA_DH * h, D + XA_DH * (h + 1))
            q = q_ref[:, cols]
            k = kv_ref[:, cols]
            do = do_ref[:, cols]
            p = _xa_probs(q, k)
            dkv_ref[:, vcols] += _dot(p.astype(BF16), do, TN)
            dp = _dot(do, kv_ref[:, vcols], NT)
            ds = (p * (dp - jnp.sum(p * dp, axis=-1, keepdims=True)) * (XA_DH ** -0.5)).astype(BF16)
            dq_ref[:, cols] = _dot(ds, k).astype(BF16)
            dkv_ref[:, cols] += _dot(ds, q, TN)

    kv_spec = pl.BlockSpec((MEM, 2 * D), lambda i: (0, 0))
    return pl.pallas_call(
        body, name="xattn_bwd", grid=(S // TR,), in_specs=[_row_spec(TR, D), kv_spec, _row_spec(TR, D)],
        out_specs=[_row_spec(TR, D), kv_spec],
        out_shape=[jax.ShapeDtypeStruct((S, D), BF16), jax.ShapeDtypeStruct((MEM, 2 * D), F32)],
        compiler_params=_params(("arbitrary",)),
    )(qx, kv, dxo)


CONV_HALO = SUBLANES
TC = 512
GELU_K = 0.7978845608028654
GELU_C = 0.044715


def _conv3(ext, w, rows):
    h0 = ext[CONV_HALO:CONV_HALO + rows]
    h1 = pltpu.roll(ext, 1, 0)[CONV_HALO:CONV_HALO + rows]
    h2 = pltpu.roll(ext, 2, 0)[CONV_HALO:CONV_HALO + rows]
    return w[2:3] * h0 + w[1:2] * h1 + w[0:1] * h2 + w[3:4], (h2, h1, h0)


def _conv_specs():
    main = pl.BlockSpec((2, TR, TC), lambda j, i: (0, i, j))
    per = TR // CONV_HALO
    prev = pl.BlockSpec((2, CONV_HALO, TC), lambda j, i: (0, jnp.maximum(i * per - 1, 0), j))
    nxt = pl.BlockSpec((2, CONV_HALO, TC), lambda j, i: (0, jnp.minimum((i + 1) * per, S // CONV_HALO - 1), j))
    par = pl.BlockSpec((2, SUBLANES, TC), lambda j, i: (0, 0, j))
    return main, prev, nxt, par


def _convgate_fwd(hid, cwb):
    def body(h_ref, hp_ref, w_ref, act_ref):
        i = pl.program_id(1)
        c = []
        for g in range(2):
            ext = jnp.concatenate([jnp.where(i == 0, 0.0, hp_ref[g]), h_ref[g]], axis=0)
            c.append(_conv3(ext, w_ref[g], TR)[0])
        gate, up = c
        act_ref[...] = (jax.nn.gelu(gate, approximate=True) * up).astype(BF16)

    main, prev, _, par = _conv_specs()
    return pl.pallas_call(
        body, name="convgate_fwd", grid=(D_FF // TC, S // TR), in_specs=[main, prev, par],
        out_specs=pl.BlockSpec((TR, TC), lambda j, i: (i, j)), out_shape=jax.ShapeDtypeStruct((S, D_FF), BF16),
        compiler_params=_params(("parallel", "parallel")),
    )(hid, hid, cwb)


def _convgate_bwd(hid, dact, cwb):
    nr = S // TR
    n = TR + CONV_HALO

    def body(h_ref, hp_ref, hn_ref, da_ref, dan_ref, w_ref, dh_ref, dw_ref):
        i = pl.program_id(1)

        @pl.when(i == 0)
        def _():
            dw_ref[...] = jnp.zeros_like(dw_ref)

        da = jnp.concatenate([da_ref[...], jnp.where(i == nr - 1, 0.0, dan_ref[...])], axis=0)
        c, taps = [], []
        for g in range(2):
            ext = jnp.concatenate([jnp.where(i == 0, 0.0, hp_ref[g]), h_ref[g], hn_ref[g]], axis=0)
            cg, tg = _conv3(ext, w_ref[g], n)
            c.append(cg)
            taps.append(tg)
        gate, up = c
        th = jnp.tanh(GELU_K * (gate + GELU_C * gate * gate * gate))
        gelu = 0.5 * gate * (1.0 + th)
        dgelu = 0.5 * (1.0 + th) + 0.5 * gate * (1.0 - th * th) * GELU_K * (1.0 + 3.0 * GELU_C * gate * gate)
        for g, dc in enumerate((da * up * dgelu, da * gelu)):
            w = w_ref[g]
            dh = w[2:3] * dc[:TR] + w[1:2] * pltpu.roll(dc, n - 1, 0)[:TR] + w[0:1] * pltpu.roll(dc, n - 2, 0)[:TR]
            dh_ref[g] = dh.astype(BF16)
            dcm = dc[:TR]
            for r in range(3):
                dw_ref[g, r:r + 1, :] += jnp.sum(dcm * taps[g][r][:TR], axis=0, keepdims=True)
            dw_ref[g, 3:4, :] += jnp.sum(dcm, axis=0, keepdims=True)

    main, prev, nxt, par = _conv_specs()
    per = TR // CONV_HALO
    return pl.pallas_call(
        body, name="convgate_bwd", grid=(D_FF // TC, nr),
        in_specs=[main, prev, nxt, pl.BlockSpec((TR, TC), lambda j, i: (i, j)),
                  pl.BlockSpec((CONV_HALO, TC), lambda j, i: (jnp.minimum((i + 1) * per, S // CONV_HALO - 1), j)), par],
        out_specs=[main, par],
        out_shape=[jax.ShapeDtypeStruct((2, S, D_FF), BF16), jax.ShapeDtypeStruct((2, SUBLANES, D_FF), F32)],
        compiler_params=_params(("parallel", "arbitrary")),
    )(hid, hid, hid, dact, dact, cwb)


def _adam_update(w, g, m, v):
    m = ADAM_B1 * m + (1.0 - ADAM_B1) * g
    v = ADAM_B2 * v + (1.0 - ADAM_B2) * (g * g)
    m_hat = m / (1.0 - ADAM_B1 ** ADAM_STEP)
    v_hat = v / (1.0 - ADAM_B2 ** ADAM_STEP)
    return -ADAM_LR * (m_hat / (jnp.sqrt(v_hat) + ADAM_EPS) + ADAM_WD * w), m, v


def _row_tile(rows, cols, itemsize=4, target=TILE_BYTES):
    tr = SUBLANES
    while rows % (2 * tr) == 0 and 2 * tr * cols * itemsize <= target:
        tr *= 2
    assert rows % tr == 0, (rows, tr)
    return tr


def _adamw(name, w, g, m, v):
    rows, cols = w.shape
    tr = _row_tile(rows, cols, target=TILE_BYTES // 2)

    def body(w_ref, g_ref, m_ref, v_ref, d_ref, nm_ref, nv_ref):
        d_ref[...], nm_ref[...], nv_ref[...] = _adam_update(w_ref[...], g_ref[...], m_ref[...], v_ref[...])

    spec = _row_spec(tr, cols)
    shape = jax.ShapeDtypeStruct((rows, cols), F32)
    return pl.pallas_call(
        body, name=name, grid=(rows // tr,), in_specs=[spec] * 4, out_specs=[spec] * 3, out_shape=[shape] * 3,
        compiler_params=_params(("parallel",)),
    )(w, g, m, v)


def _adamw_halves(name, core, w, g_mine, g_sibling, m, v):
    rows, cols = w.shape
    half = rows // 2
    tr = _row_tile(half, cols, target=TILE_BYTES // 2)
    per = half // tr

    def body(core_ref, w_ref, gm_ref, gs_ref, m_ref, v_ref, g_ref, d_ref, nm_ref, nv_ref):
        g = jnp.where(pl.program_id(0) // per == core_ref[0], gm_ref[...], gs_ref[...])
        g_ref[...] = g
        d_ref[...], nm_ref[...], nv_ref[...] = _adam_update(w_ref[...], g, m_ref[...], v_ref[...])

    spec = pl.BlockSpec((tr, cols), lambda i, core_ref: (i, 0))
    half_spec = pl.BlockSpec((tr, cols), lambda i, core_ref: (i % per, 0))
    shape = jax.ShapeDtypeStruct((rows, cols), F32)
    return pl.pallas_call(
        body, name=name, out_shape=[shape] * 4,
        grid_spec=pltpu.PrefetchScalarGridSpec(
            num_scalar_prefetch=1, grid=(rows // tr,), in_specs=[spec, half_spec, half_spec, spec, spec], out_specs=[spec] * 4),
        compiler_params=_params(("parallel",)),
    )(core, w, g_mine, g_sibling, m, v)


def _chip_sum(name, core, g, other):
    _, _, half, cols = g.shape
    tr = _row_tile(half, cols)

    def body(core_ref, g_ref, o_ref, p_ref):
        p_ref[...] = (g_ref[...] + o_ref[...]).astype(BF16)

    spec = pl.BlockSpec((None, tr, cols), lambda j, i, core_ref: (j, i, 0))
    return pl.pallas_call(
        body, name=name, out_shape=jax.ShapeDtypeStruct((N_CHIPS, half, cols), BF16),
        grid_spec=pltpu.PrefetchScalarGridSpec(
            num_scalar_prefetch=1, grid=(N_CHIPS, half // tr),
            in_specs=[pl.BlockSpec((None, None, tr, cols), lambda j, i, core_ref: (j, core_ref[0], i, 0)), spec],
            out_specs=spec),
        compiler_params=_params(("parallel", "parallel")),
    )(core, g, other)


def _mesh_sum(name, chip, received, own):
    _, half, cols = received.shape
    tr = _row_tile(half, cols, itemsize=2 * N_CHIPS)

    def body(chip_ref, r_ref, own_ref, o_ref):
        acc = None
        for j in range(N_CHIPS):
            term = jnp.where(chip_ref[0] == j, own_ref[...], r_ref[j]).astype(F32)
            acc = term if acc is None else acc + term
        o_ref[...] = acc

    return pl.pallas_call(
        body, name=name, out_shape=jax.ShapeDtypeStruct((half, cols), F32),
        grid_spec=pltpu.PrefetchScalarGridSpec(
            num_scalar_prefetch=1, grid=(half // tr,),
            in_specs=[pl.BlockSpec((N_CHIPS, tr, cols), lambda i, chip_ref: (0, i, 0)),
                      pl.BlockSpec((None, tr, cols), lambda i, chip_ref: (chip_ref[0], i, 0))],
            out_specs=pl.BlockSpec((tr, cols), lambda i, chip_ref: (i, 0))),
        compiler_params=_params(("parallel",)),
    )(chip, received, own)


CHIP_FLIPS = ((1, 0), (0, 1), (1, 1))


def _place():
    x, y, c = lax.axis_index("x"), lax.axis_index("y"), lax.axis_index("c")
    return x, y, c, 2 * x + y


def _remote(src, dst, sems_s, sems_r, k, dev):
    return pltpu.make_async_remote_copy(src_ref=src, dst_ref=dst, send_sem=sems_s.at[k], recv_sem=sems_r.at[k],
                                        device_id=dev, device_id_type=MESH)


def _comm_call(name, body, ins, out_shapes, n_remote):
    return pl.pallas_call(
        body, name=name, in_specs=[ANY] * len(ins), out_specs=[ANY] * len(out_shapes), out_shape=out_shapes,
        scratch_shapes=[pltpu.SemaphoreType.DMA((n_remote,)), pltpu.SemaphoreType.DMA((n_remote,))],
    )(*ins)


def _all_gather_weights(halved, whole):
    nh, nw = len(halved), len(whole)
    n_arr = nh + nw
    n_remote = 7 * nh + 4 * nw

    def body(*refs):
        ins, outs = refs[:n_arr], refs[n_arr:2 * n_arr]
        sems_s, sems_r = refs[2 * n_arr:]
        x, y, c, me = _place()
        sibling = (x, y, 1 - c)
        own, first, passed = [], [], []
        for k in range(n_arr):
            cp = _remote(ins[k], outs[k].at[me], sems_s, sems_r, k, sibling)
            cp.start()
            own.append(cp)
        for k in range(n_arr):
            for f, (fx, fy) in enumerate(CHIP_FLIPS):
                src, dst = (ins[k].at[c], outs[k].at[me, c]) if k < nh else (ins[k], outs[k].at[me])
                cp = _remote(src, dst, sems_s, sems_r, n_arr + 3 * k + f, (x ^ fx, y ^ fy, c))
                cp.start()
                first.append(cp)
        for k in range(nh):
            for f, (fx, fy) in enumerate(CHIP_FLIPS):
                other = 2 * (x ^ fx) + (y ^ fy)
                first[3 * k + f].wait_recv()
                cp = _remote(outs[k].at[other, c], outs[k].at[other, c], sems_s, sems_r, 4 * n_arr + 3 * k + f, sibling)
                cp.start()
                passed.append(cp)
        for cp in first[3 * nh:] + passed + own:
            cp.wait_recv()
        for cp in first + passed + own:
            cp.wait_send()

    shapes = [jax.ShapeDtypeStruct((N_CHIPS,) + a.shape, a.dtype) for a in list(halved) + list(whole)]
    return _comm_call("all_gather_weights", body, list(halved) + list(whole), shapes, n_remote)


def _swap_halves(gs):
    n = len(gs)

    def body(*refs):
        ins, outs = refs[:n], refs[n:2 * n]
        sems_s, sems_r = refs[2 * n:]
        x, y, c, _ = _place()
        copies = [_remote(ins[k].at[:, 1 - c], outs[k], sems_s, sems_r, k, (x, y, 1 - c)) for k in range(n)]
        for cp in copies:
            cp.start()
        for cp in copies:
            cp.wait()

    shapes = [jax.ShapeDtypeStruct((g.shape[0],) + g.shape[2:], g.dtype) for g in gs]
    return _comm_call("swap_halves", body, gs, shapes, n)


def _scatter_chips(ps):
    n = len(ps)

    def body(*refs):
        ins, outs = refs[:n], refs[n:2 * n]
        sems_s, sems_r = refs[2 * n:]
        x, y, c, me = _place()
        copies = []
        for k in range(n):
            for f, (fx, fy) in enumerate(CHIP_FLIPS):
                other = 2 * (x ^ fx) + (y ^ fy)
                cp = _remote(ins[k].at[other], outs[k].at[me], sems_s, sems_r, 3 * k + f, (x ^ fx, y ^ fy, c))
                cp.start()
                copies.append(cp)
        for cp in copies:
            cp.wait()

    shapes = [jax.ShapeDtypeStruct(p.shape, p.dtype) for p in ps]
    return _comm_call("scatter_chips", body, ps, shapes, 3 * n)


def _swap_reduced(rs):
    n = len(rs)

    def body(*refs):
        ins, outs = refs[:n], refs[n:2 * n]
        sems_s, sems_r = refs[2 * n:]
        x, y, c, _ = _place()
        copies = [_remote(ins[k], outs[k], sems_s, sems_r, k, (x, y, 1 - c)) for k in range(n)]
        for cp in copies:
            cp.start()
        for cp in copies:
            cp.wait()

    shapes = [jax.ShapeDtypeStruct(r.shape, r.dtype) for r in rs]
    return _comm_call("swap_reduced", body, rs, shapes, n)


def _all_reduce_small(buf):
    rows = buf.shape[0]
    n_dev = 8

    def body(in_ref, out_ref, gather, sems_s, sems_r):
        x, y, c, _ = _place()
        me = 4 * x + 2 * y + c
        gather[me] = in_ref[...]
        copies = []
        for o in range(1, n_dev):
            dev = (x ^ (o >> 2), y ^ ((o >> 1) & 1), c ^ (o & 1))
            copies.append(_remote(in_ref, gather.at[me], sems_s, sems_r, o - 1, dev))
        for cp in copies:
            cp.start()
        for cp in copies:
            cp.wait()
        acc = gather[0]
        for d in range(1, n_dev):
            acc = acc + gather[d]
        out_ref[...] = acc

    return pl.pallas_call(
        body, name="all_reduce_small", in_specs=[VMEM_SPEC], out_specs=VMEM_SPEC,
        out_shape=jax.ShapeDtypeStruct((rows, LANES), F32),
        scratch_shapes=[pltpu.VMEM((n_dev, rows, LANES), F32), pltpu.SemaphoreType.DMA((n_dev - 1,)),
                        pltpu.SemaphoreType.DMA((n_dev - 1,))],
        compiler_params=pltpu.CompilerParams(vmem_limit_bytes=VMEM_LIMIT),
    )(buf)


def _local_step(x, mem, target, p):
    h1 = _norm_fwd("norm_mix_pre", x, p["norm_mix_pre"])
    proj = _mm_nn("in_proj", h1, p["w_in"], F32, 512, 896)
    qa, ka, va = _fox_prep(proj, p["bf_pad"])
    y_fox, qab = _fox_fwd(qa, ka, va)
    y_pool = _pool_fwd(proj, p["w_pool_bd"], p["pool_scale"])
    ycat = jnp.concatenate([y_pool, y_fox], axis=1)
    y1 = _mm_nn("mix_out", ycat, p["w_mix_out"], F32, 512, 1024)
    x2, h2 = _resid_norm("resid_mix", x, y1, p["norm_mix_post"], p["norm_xa_pre"])
    qx = _mm_nn("xq", h2, p["w_xq"], BF16, 512, 1024)
    mem_n = _norm_fwd("norm_mem", mem, p["norm_mem"])
    kv = _mm(
        "xkv", mem_n, p["w_xkv"], pl.BlockSpec((MEM, D), lambda i, j, k: (0, 0)),
        pl.BlockSpec((None, D, 512), lambda i, j, k: (j, 0, 0)), jax.ShapeDtypeStruct((MEM, 2 * D), BF16),
        pl.BlockSpec((MEM, 512), lambda i, j, k: (0, j)), (1, N_CHIPS, 1), NN, (MEM, 512))
    xo = _xattn_fwd(qx, kv)
    y2 = _mm_nn("xo", xo, p["w_xo"], F32, 512, 1024)
    x3, h3 = _resid_norm("resid_xa", x2, y2, p["norm_xa_post"], p["norm_ffn_pre"])
    hid = _mm(
        "up_proj", h3, p["w_up"], pl.BlockSpec((1024, D), lambda i, j, k: (i, 0)),
        pl.BlockSpec((None, D, 512), lambda i, j, k: (j // 4, 0, j % 4)), jax.ShapeDtypeStruct((2, S, D_FF), F32),
        pl.BlockSpec((None, 1024, 512), lambda i, j, k: (j // 8, i, j % 8)), (S // 1024, 16, 1), NN, (1024, 512))
    act = _convgate_fwd(hid, p["cwb"])
    y3 = _mm_nn("down_proj", act, p["w_down"], F32, 512, 512)

    g = {}
    dres, dy3, g["norm_ffn_post"], loss_cols = _loss_bwd(x3, y3, p["norm_ffn_post"], target)
    dact = _mm_nt("d_act", dy3, p["w_down"], F32, 512, 1024)
    g["w_down"] = _mm_tn("dw_down", act, dy3, 512, 512)
    dhid, dcwb = _convgate_bwd(hid, dact, p["cwb"])
    dh3 = _mm(
        "d_h3", dhid, p["w_up"], pl.BlockSpec((None, 512, 2048), lambda i, j, k: (k // 2, i, k % 2)),
        pl.BlockSpec((None, 512, 2048), lambda i, j, k: (k, j, 0)), jax.ShapeDtypeStruct((S, D), F32),
        pl.BlockSpec((512, 512), lambda i, j, k: (i, j)), (S // 512, 2, N_CHIPS), NT, (512, 512))
    g["w_up"] = _mm(
        "dw_up", h3, dhid, pl.BlockSpec((S, 512), lambda i, j, k: (0, i)),
        pl.BlockSpec((None, S, 512), lambda i, j, k: (j // 8, 0, j % 8)), jax.ShapeDtypeStruct((N_CHIPS, D, 2048), F32),
        pl.BlockSpec((None, 512, 512), lambda i, j, k: (j // 4, i, j % 4)), (2, 16, 1), TN, (512, 512))
    dres, dy2, g["norm_ffn_pre"], g["norm_xa_post"] = _mid_bwd("bwd_ffn_xa", dres, x3, p["norm_ffn_pre"], dh3, y2, p["norm_xa_post"])
    dxo = _mm_nt("d_xo", dy2, p["w_xo"], BF16, 512, 1024)
    g["w_xo"] = _mm_tn("dw_xo", xo, dy2, 512, 512)
    dqx, dkv = _xattn_bwd(qx, kv, dxo)
    dkv = dkv.astype(BF16)
    dh2 = _mm_nt("d_h2", dqx, p["w_xq"], F32, 512, 1024)
    g["w_xq"] = _mm_tn("dw_xq", h2, dqx, 512, 512)
    dmem_n = _mm(
        "d_mem", dkv, p["w_xkv"], pl.BlockSpec((MEM, 512), lambda i, j, k: (0, k)),
        pl.BlockSpec((None, D, 512), lambda i, j, k: (k, 0, 0)), jax.ShapeDtypeStruct((MEM, D), F32),
        pl.BlockSpec((MEM, D), lambda i, j, k: (0, 0)), (1, 1, N_CHIPS), NT, (MEM, D))
    g["w_xkv"] = _mm(
        "dw_xkv", mem_n, dkv, pl.BlockSpec((MEM, D), lambda i, j, k: (0, 0)),
        pl.BlockSpec((MEM, 512), lambda i, j, k: (0, j)), jax.ShapeDtypeStruct((N_CHIPS, D, 512), F32),
        pl.BlockSpec((None, D, 512), lambda i, j, k: (j, 0, 0)), (1, N_CHIPS, 1), TN, (D, 512))
    g["norm_mem"] = _gain_bwd("dg_mem", mem, p["norm_mem"], dmem_n)
    dres, dy1, g["norm_xa_pre"], g["norm_mix_post"] = _mid_bwd("bwd_xa_mix", dres, x2, p["norm_xa_pre"], dh2, y1, p["norm_mix_post"])
    dycat = _mm_nt("d_ycat", dy1, p["w_mix_out"], F32, 512, 1024)
    g["w_mix_out"] = _mm_tn("dw_mix_out", ycat, dy1, 512, 512)
    doa = _fox_bwd_prep(dycat, ycat)
    dqa, dka, dva = _fox_bwd(qab, doa, ka, va)
    du, g["w_pool_full"], g["pool_scale"] = _pool_bwd(proj, dycat, p["w_pool_bd"], p["w_pool_bd_t"], p["pool_scale"])
    dproj, g["bf_pad"] = _fox_bwd_post(dqa, dka, dva, du, proj, p["bf_pad"])
    dh1 = _mm_nt("d_h1", dproj, p["w_in"], F32, 512, 512)
    g["w_in"] = _mm_tn("dw_in", h1, dproj, 512, 896)
    grad_x, g["norm_mix_pre"] = _first_bwd(dres, x, p["norm_mix_pre"], dh1)
    g["cwb"] = dcwb
    return grad_x, g, loss_cols


BIG = ("w_in", "w_mix_out", "w_xq", "w_xkv", "w_xo", "w_up", "w_down")
ROW_SHARDED = ("w_mix_out", "w_xq", "w_xo", "w_down")
SMALL = ("norm_mix_pre", "norm_mix_post", "b_forget", "w_pool", "pool_scale", "norm_mem", "norm_xa_pre", "norm_xa_post",
         "norm_ffn_pre", "norm_ffn_post", "conv_b")
ORDER = ("norm_mix_pre", "norm_mix_post", "w_in", "b_forget", "w_pool", "pool_scale", "w_mix_out", "norm_mem", "norm_xa_pre",
         "norm_xa_post", "w_xq", "w_xkv", "w_xo", "norm_ffn_pre", "norm_ffn_post", "w_up", "conv_w", "conv_b", "w_down")
SLOT = SUBLANES * LANES


def _pack(parts):
    rows, offs, off = [], [], 0
    for a in parts:
        flat = a.reshape(-1).astype(F32)
        n = -(-flat.shape[0] // SLOT) * SLOT
        rows.append(jnp.pad(flat, (0, n - flat.shape[0])).reshape(n // LANES, LANES))
        offs.append(off)
        off += n // LANES
    return jnp.concatenate(rows, axis=0), offs


def _unpack(buf, off, like):
    n = like.size
    rows = -(-n // LANES)
    return buf[off:off + rows].reshape(-1)[:n].reshape(like.shape)


def _whole_params(w, full, conv_w_full):
    w_in_full = jnp.pad(jnp.concatenate(list(full["w_in"]), axis=1), ((0, 0), (0, D_IN_PAD - D_IN)))
    w_pool_bd = jnp.zeros((D_POOL, D_POOL), F32)
    for gi in range(4):
        w_pool_bd = w_pool_bd.at[64 * gi:64 * (gi + 1), 64 * gi:64 * (gi + 1)].set(w["w_pool"][0, gi])
    cw2 = conv_w_full.reshape(3, 2, D_FF).transpose(1, 0, 2)
    cwb = jnp.concatenate([cw2, w["conv_b"].reshape(1, 2, D_FF).transpose(1, 0, 2), jnp.zeros((2, 4, D_FF), F32)], axis=1)
    p = {n: w[n] for n in ("norm_mix_pre", "norm_mix_post", "norm_mem", "norm_xa_pre", "norm_xa_post", "norm_ffn_pre",
                           "norm_ffn_post")}
    p.update(
        w_in=w_in_full, bf_pad=jnp.pad(w["b_forget"], ((0, 0), (0, LANES - HEADS))),
        w_pool_bd=w_pool_bd.astype(BF16), w_pool_bd_t=w_pool_bd.T.astype(BF16), pool_scale=w["pool_scale"].reshape(1, D_POOL),
        w_mix_out=full["w_mix_out"].reshape(D, D), w_xq=full["w_xq"].reshape(D, D), w_xkv=full["w_xkv"],
        w_xo=full["w_xo"].reshape(D, D), w_up=full["w_up"], cwb=cwb, w_down=full["w_down"].reshape(D_FF, D))
    return p


def kernel(x, mem, norm_mix_pre, norm_mix_post, w_in, b_forget, w_pool, pool_scale, w_mix_out, norm_mem, norm_xa_pre, norm_xa_post, w_xq, w_xkv, w_xo, norm_ffn_pre, norm_ffn_post, w_up, conv_w, conv_b, w_down, loss_target, m_norm_mix_pre, m_norm_mix_post, m_w_in, m_b_forget, m_w_pool, m_pool_scale, m_w_mix_out, m_norm_mem, m_norm_xa_pre, m_norm_xa_post, m_w_xq, m_w_xkv, m_w_xo, m_norm_ffn_pre, m_norm_ffn_post, m_w_up, m_conv_w, m_conv_b, m_w_down, v_norm_mix_pre, v_norm_mix_post, v_w_in, v_b_forget, v_w_pool, v_pool_scale, v_w_mix_out, v_norm_mem, v_norm_xa_pre, v_norm_xa_post, v_w_xq, v_w_xkv, v_w_xo, v_norm_ffn_pre, v_norm_ffn_post, v_w_up, v_conv_w, v_conv_b, v_w_down):
    w = dict(norm_mix_pre=norm_mix_pre, norm_mix_post=norm_mix_post, w_in=w_in, b_forget=b_forget, w_pool=w_pool,
             pool_scale=pool_scale, w_mix_out=w_mix_out, norm_mem=norm_mem, norm_xa_pre=norm_xa_pre, norm_xa_post=norm_xa_post,
             w_xq=w_xq, w_xkv=w_xkv, w_xo=w_xo, norm_ffn_pre=norm_ffn_pre, norm_ffn_post=norm_ffn_post, w_up=w_up,
             conv_w=conv_w, conv_b=conv_b, w_down=w_down)
    m = dict(norm_mix_pre=m_norm_mix_pre, norm_mix_post=m_norm_mix_post, w_in=m_w_in, b_forget=m_b_forget, w_pool=m_w_pool,
             pool_scale=m_pool_scale, w_mix_out=m_w_mix_out, norm_mem=m_norm_mem, norm_xa_pre=m_norm_xa_pre,
             norm_xa_post=m_norm_xa_post, w_xq=m_w_xq, w_xkv=m_w_xkv, w_xo=m_w_xo, norm_ffn_pre=m_norm_ffn_pre,
             norm_ffn_post=m_norm_ffn_post, w_up=m_w_up, conv_w=m_conv_w, conv_b=m_conv_b, w_down=m_w_down)
    v = dict(norm_mix_pre=v_norm_mix_pre, norm_mix_post=v_norm_mix_post, w_in=v_w_in, b_forget=v_b_forget, w_pool=v_w_pool,
             pool_scale=v_pool_scale, w_mix_out=v_w_mix_out, norm_mem=v_norm_mem, norm_xa_pre=v_norm_xa_pre,
             norm_xa_post=v_norm_xa_post, w_xq=v_w_xq, w_xkv=v_w_xkv, w_xo=v_w_xo, norm_ffn_pre=v_norm_ffn_pre,
             norm_ffn_post=v_norm_ffn_post, w_up=v_w_up, conv_w=v_conv_w, conv_b=v_conv_b, w_down=v_w_down)
    chip = 2 * lax.axis_index("x") + lax.axis_index("y")

    shard2d = {n: w[n][0] for n in BIG}
    halved = [shard2d[n].astype(BF16).reshape(2, shard2d[n].shape[0] // 2, shard2d[n].shape[1]) for n in BIG]
    gathered = _all_gather_weights(halved, [conv_w.reshape(3, -1)])
    full = {n: a.reshape((N_CHIPS,) + shard2d[n].shape) for n, a in zip(BIG, gathered)}
    conv_w_full = jnp.transpose(gathered[-1], (1, 0, 2)).reshape(3, 2 * D_FF)

    p = _whole_params(w, full, conv_w_full)

    grad_x, g, loss_cols = _local_step(x[0], mem[0], loss_target[0], p)

    gw_in = g["w_in"][:, :D_IN]
    stacked = dict(g)
    stacked["w_in"] = jnp.stack([gw_in[:, 643 * j:643 * (j + 1)] for j in range(N_CHIPS)])
    for n in ROW_SHARDED:
        stacked[n] = g[n].reshape((N_CHIPS,) + shard2d[n].shape)
    views = [stacked[n].reshape(N_CHIPS, 2, shard2d[n].shape[0] // 2, shard2d[n].shape[1]) for n in BIG]
    from_sibling = _swap_halves(views)
    core_id = lax.axis_index("c").astype(jnp.int32).reshape(1)
    chip_id = chip.astype(jnp.int32).reshape(1)
    partial = [_chip_sum("chip_sum_" + n, core_id, view, other) for n, view, other in zip(BIG, views, from_sibling)]
    received = _scatter_chips(partial)
    reduced = [_mesh_sum("mesh_sum_" + n, chip_id, r, own) for n, r, own in zip(BIG, received, partial)]
    reduced_sibling = _swap_reduced(reduced)
    grads = {}

    gw_pool = jnp.stack([g["w_pool_full"][64 * gi:64 * (gi + 1), 64 * gi:64 * (gi + 1)] for gi in range(4)])
    dcwb = g["cwb"]
    g_conv_w = dcwb[:, 0:3, :].transpose(1, 0, 2).reshape(3, 2 * D_FF)
    g_conv_b = dcwb[:, 3, :].reshape(2 * D_FF)
    small_g = dict(norm_mix_pre=g["norm_mix_pre"], norm_mix_post=g["norm_mix_post"], b_forget=g["bf_pad"][:, :HEADS],
                   w_pool=gw_pool, pool_scale=g["pool_scale"], norm_mem=g["norm_mem"], norm_xa_pre=g["norm_xa_pre"],
                   norm_xa_post=g["norm_xa_post"], norm_ffn_pre=g["norm_ffn_pre"], norm_ffn_post=g["norm_ffn_post"],
                   conv_b=g_conv_b)
    buf, offs = _pack([small_g[n] for n in SMALL] + [g_conv_w, loss_cols])
    buf = _all_reduce_small(buf)
    for n, off in zip(SMALL, offs):
        grads[n] = _unpack(buf, off, w[n])
    g_conv_w = _unpack(buf, offs[len(SMALL)], conv_w_full)
    grads["conv_w"] = lax.dynamic_slice_in_dim(g_conv_w, chip * (2 * D_FF // N_CHIPS), 2 * D_FF // N_CHIPS, axis=1).reshape(conv_w.shape)
    loss = jnp.sum(_unpack(buf, offs[len(SMALL) + 1], loss_cols))

    delta, new_m, new_v = {}, {}, {}
    for n, g_mine, g_sibling in zip(BIG, reduced, reduced_sibling):
        gn, d, nm, nv = _adamw_halves("adamw_" + n, core_id, shard2d[n], g_mine, g_sibling, m[n][0], v[n][0])
        grads[n], delta[n], new_m[n], new_v[n] = gn[None], d[None], nm[None], nv[None]
    small_names = SMALL + ("conv_w",)
    packed = [_pack([d[n] for n in small_names])[0] for d in (w, grads, m, v)]
    offs = _pack([w[n] for n in small_names])[1]
    d, nm, nv = _adamw("adamw_small", *packed)
    for n, off in zip(small_names, offs):
        delta[n], new_m[n], new_v[n] = _unpack(d, off, w[n]), _unpack(nm, off, w[n]), _unpack(nv, off, w[n])

    return (loss, grad_x[None], *[grads[n] for n in ORDER], *[delta[n] for n in ORDER], *[new_m[n] for n in ORDER],
            *[new_v[n] for n in ORDER])
```

```python
import functools

import jax
import jax.numpy as jnp
from jax import lax
from jax.experimental import pallas as pl
from jax.experimental.pallas import tpu as pltpu

F32 = jnp.float32
BF16 = jnp.bfloat16
MESH = pl.DeviceIdType.MESH
ANY = pl.BlockSpec(memory_space=pl.ANY)
VMEM_SPEC = pl.BlockSpec(memory_space=pltpu.VMEM)

S = 4096
D = 1024
MEM = 256
D_POOL = 256
HEADS = 12
DH = 64
D_FOX = HEADS * DH
D_IN = D_POOL + 3 * D_FOX + HEADS
F_OFF = D_POOL + 3 * D_FOX
Q_OFF, K_OFF, V_OFF = D_POOL, D_POOL + D_FOX, D_POOL + 2 * D_FOX
XA_HEADS = 4
XA_DH = 256
D_FF = 4096
EPS = 1e-6
N_CHIPS = 4
ADAM_LR, ADAM_B1, ADAM_B2, ADAM_EPS, ADAM_WD, ADAM_STEP = 0.001, 0.9, 0.999, 1e-08, 0.01, 10

LANES = 128
SUBLANES = 8
D_IN_PAD = 21 * LANES
TR = 512
TILE_BYTES = 2 * 1024 * 1024
NEG = -1e30
VMEM_LIMIT = 52 * 1024 * 1024

NN = (((1,), (0,)), ((), ()))
NT = (((1,), (1,)), ((), ()))
TN = (((0,), (0,)), ((), ()))


def _dot(a, b, dims=NN):
    return lax.dot_general(a, b, dims, preferred_element_type=F32)


def _params(sem):
    return pltpu.CompilerParams(dimension_semantics=sem, vmem_limit_bytes=VMEM_LIMIT)


def _split3(x):
    hi = x.astype(BF16)
    r = x - hi.astype(F32)
    mid = r.astype(BF16)
    lo = (r - mid.astype(F32)).astype(BF16)
    return hi, mid, lo


def _split3_f32(x):
    hi = x.astype(BF16).astype(F32)
    r = x - hi
    mid = r.astype(BF16).astype(F32)
    return hi, mid, r - mid


def _lane_iota(shape):
    return lax.broadcasted_iota(jnp.int32, shape, len(shape) - 1)


def _row_iota(shape):
    return lax.broadcasted_iota(jnp.int32, shape, len(shape) - 2)


def _mm(name, a, b, a_spec, b_spec, out_shape, out_spec, grid, dims, acc_shape):
    nk = grid[2]

    def body(a_ref, b_ref, o_ref, *scr):
        p = _dot(a_ref[...], b_ref[...], dims)
        if nk == 1:
            o_ref[...] = p.astype(o_ref.dtype)
        else:
            acc = scr[0]
            k = pl.program_id(2)

            @pl.when(k == 0)
            def _():
                acc[...] = p

            @pl.when(k > 0)
            def _():
                acc[...] += p

            @pl.when(k == nk - 1)
            def _():
                o_ref[...] = acc[...].astype(o_ref.dtype)

    return pl.pallas_call(
        body, name=name, grid=grid, in_specs=[a_spec, b_spec], out_specs=out_spec, out_shape=out_shape,
        scratch_shapes=[pltpu.VMEM(acc_shape, F32)] if nk > 1 else [],
        compiler_params=_params(("parallel", "parallel", "arbitrary")),
    )(a, b)


def _mm_nn(name, a, b, out_dtype, tm, tn):
    m, k = a.shape
    n = b.shape[1]
    return _mm(name, a, b, pl.BlockSpec((tm, k), lambda i, j, kk: (i, 0)), pl.BlockSpec((k, tn), lambda i, j, kk: (0, j)),
               jax.ShapeDtypeStruct((m, n), out_dtype), pl.BlockSpec((tm, tn), lambda i, j, kk: (i, j)),
               (m // tm, n // tn, 1), NN, (tm, tn))


def _mm_nt(name, a, b, out_dtype, tm, tn):
    m, k = a.shape
    n = b.shape[0]
    return _mm(name, a, b, pl.BlockSpec((tm, k), lambda i, j, kk: (i, 0)), pl.BlockSpec((tn, k), lambda i, j, kk: (j, 0)),
               jax.ShapeDtypeStruct((m, n), out_dtype), pl.BlockSpec((tm, tn), lambda i, j, kk: (i, j)),
               (m // tm, n // tn, 1), NT, (tm, tn))


def _mm_tn(name, a, b, tka, tn):
    t, ka = a.shape
    n = b.shape[1]
    return _mm(name, a, b, pl.BlockSpec((t, tka), lambda i, j, kk: (0, i)), pl.BlockSpec((t, tn), lambda i, j, kk: (0, j)),
               jax.ShapeDtypeStruct((ka, n), F32), pl.BlockSpec((tka, tn), lambda i, j, kk: (i, j)),
               (ka // tka, n // tn, 1), TN, (tka, tn))


def _rms(x, g):
    r = lax.rsqrt(jnp.mean(x * x, axis=-1, keepdims=True) + EPS)
    return x * r * g


def _rms_bwd(x, g, dy):
    r = lax.rsqrt(jnp.mean(x * x, axis=-1, keepdims=True) + EPS)
    xh = x * r
    dxh = dy * g
    dx = r * (dxh - xh * jnp.mean(dxh * xh, axis=-1, keepdims=True))
    return dx, jnp.sum(dy * xh, axis=0, keepdims=True)


def _row_spec(tr, width):
    return pl.BlockSpec((tr, width), lambda i: (i, 0))


def _vec_spec(width):
    return pl.BlockSpec((1, width), lambda i: (0, 0))


def _norm_fwd(name, x, g):
    rows, width = x.shape
    tr = min(TR, rows)

    def body(x_ref, g_ref, h_ref):
        h_ref[...] = _rms(x_ref[...], g_ref[...]).astype(BF16)

    return pl.pallas_call(
        body, name=name, grid=(rows // tr,), in_specs=[_row_spec(tr, width), _vec_spec(width)],
        out_specs=_row_spec(tr, width), out_shape=jax.ShapeDtypeStruct((rows, width), BF16),
        compiler_params=_params(("parallel",)),
    )(x, g)


def _resid_norm(name, xp, y, g_post, g_pre):
    def body(xp_ref, y_ref, gpost_ref, gpre_ref, xn_ref, h_ref):
        xn = xp_ref[...] + _rms(y_ref[...], gpost_ref[...])
        xn_ref[...] = xn
        h_ref[...] = _rms(xn, gpre_ref[...]).astype(BF16)

    return pl.pallas_call(
        body, name=name, grid=(S // TR,), in_specs=[_row_spec(TR, D), _row_spec(TR, D), _vec_spec(D), _vec_spec(D)],
        out_specs=[_row_spec(TR, D), _row_spec(TR, D)],
        out_shape=[jax.ShapeDtypeStruct((S, D), F32), jax.ShapeDtypeStruct((S, D), BF16)],
        compiler_params=_params(("parallel",)),
    )(xp, y, g_post, g_pre)


def _loss_bwd(x3, y3, g_post, target):
    def body(x_ref, y_ref, g_ref, t_ref, dres_ref, dy_ref, dg_ref, loss_ref):
        i = pl.program_id(0)

        @pl.when(i == 0)
        def _():
            dg_ref[...] = jnp.zeros_like(dg_ref)
            loss_ref[...] = jnp.zeros_like(loss_ref)

        y = y_ref[...]
        g = g_ref[...]
        e = x_ref[...] + _rms(y, g) - t_ref[...]
        loss_ref[...] += jnp.sum(e * e, axis=0, keepdims=True) * (0.5 / D)
        dres = e * (1.0 / D)
        dres_ref[...] = dres
        dy, dg = _rms_bwd(y, g, dres)
        dy_ref[...] = dy.astype(BF16)
        dg_ref[...] += dg

    return pl.pallas_call(
        body, name="loss_bwd", grid=(S // TR,),
        in_specs=[_row_spec(TR, D), _row_spec(TR, D), _vec_spec(D), _row_spec(TR, D)],
        out_specs=[_row_spec(TR, D), _row_spec(TR, D), _vec_spec(D), _vec_spec(D)],
        out_shape=[jax.ShapeDtypeStruct((S, D), F32), jax.ShapeDtypeStruct((S, D), BF16),
                   jax.ShapeDtypeStruct((1, D), F32), jax.ShapeDtypeStruct((1, D), F32)],
        compiler_params=_params(("arbitrary",)),
    )(x3, y3, g_post, target)


def _mid_bwd(name, dres, xcur, g_pre, dh, yprev, g_post):
    def body(dres_ref, x_ref, gpre_ref, dh_ref, y_ref, gpost_ref, dx_ref, dy_ref, dgpre_ref, dgpost_ref):
        i = pl.program_id(0)

        @pl.when(i == 0)
        def _():
            dgpre_ref[...] = jnp.zeros_like(dgpre_ref)
            dgpost_ref[...] = jnp.zeros_like(dgpost_ref)

        dxn, dgpre = _rms_bwd(x_ref[...], gpre_ref[...], dh_ref[...])
        dx = dres_ref[...] + dxn
        dx_ref[...] = dx
        dy, dgpost = _rms_bwd(y_ref[...], gpost_ref[...], dx)
        dy_ref[...] = dy.astype(BF16)
        dgpre_ref[...] += dgpre
        dgpost_ref[...] += dgpost

    return pl.pallas_call(
        body, name=name, grid=(S // TR,),
        in_specs=[_row_spec(TR, D), _row_spec(TR, D), _vec_spec(D), _row_spec(TR, D), _row_spec(TR, D), _vec_spec(D)],
        out_specs=[_row_spec(TR, D), _row_spec(TR, D), _vec_spec(D), _vec_spec(D)],
        out_shape=[jax.ShapeDtypeStruct((S, D), F32), jax.ShapeDtypeStruct((S, D), BF16),
                   jax.ShapeDtypeStruct((1, D), F32), jax.ShapeDtypeStruct((1, D), F32)],
        compiler_params=_params(("arbitrary",)),
    )(dres, xcur, g_pre, dh, yprev, g_post)


def _first_bwd(dres, x, g, dh):
    def body(dres_ref, x_ref, g_ref, dh_ref, dx_ref, dg_ref):
        i = pl.program_id(0)

        @pl.when(i == 0)
        def _():
            dg_ref[...] = jnp.zeros_like(dg_ref)

        dxn, dg = _rms_bwd(x_ref[...], g_ref[...], dh_ref[...])
        dx_ref[...] = dres_ref[...] + dxn
        dg_ref[...] += dg

    return pl.pallas_call(
        body, name="first_bwd", grid=(S // TR,),
        in_specs=[_row_spec(TR, D), _row_spec(TR, D), _vec_spec(D), _row_spec(TR, D)],
        out_specs=[_row_spec(TR, D), _vec_spec(D)],
        out_shape=[jax.ShapeDtypeStruct((S, D), F32), jax.ShapeDtypeStruct((1, D), F32)],
        compiler_params=_params(("arbitrary",)),
    )(dres, x, g, dh)


def _gain_bwd(name, x, g, dy):
    rows, width = x.shape

    def body(x_ref, g_ref, dy_ref, dg_ref):
        _, dg = _rms_bwd(x_ref[...], g_ref[...], dy_ref[...])
        dg_ref[...] = dg

    return pl.pallas_call(
        body, name=name, grid=(1,), in_specs=[_row_spec(rows, width), _vec_spec(width), _row_spec(rows, width)],
        out_specs=_vec_spec(width), out_shape=jax.ShapeDtypeStruct((1, width), F32),
        compiler_params=_params(("arbitrary",)),
    )(x, g, dy)


CUM_Q = DH
CUM_K = DH + 3
LSE_Q = DH + 6
DEN_V = DH
DELTA = DH + 1
PREP_TR = 256
BQ = 512
BK = 512


def _head_block(ref, off, h):
    start = off + DH * h
    base = (start // LANES) * LANES
    blk = ref[:, base:base + LANES]
    return pltpu.roll(blk, DH, 1) if start % LANES else blk


def _cumsum_rows(x, tri, carry):
    hi, mid, lo = _split3(x)
    return _dot(tri, hi) + _dot(tri, mid) + _dot(tri, lo) + carry


def _fox_prep(proj, bf_pad):
    tr = PREP_TR

    def body(proj_ref, bf_ref, qa_ref, ka_ref, va_ref, carry_ref):
        i = pl.program_id(0)

        @pl.when(i == 0)
        def _():
            carry_ref[...] = jnp.zeros_like(carry_ref)

        lane = _lane_iota((tr, LANES))
        z = proj_ref[:, F_OFF:F_OFF + LANES] + bf_ref[...]
        log_f = jnp.minimum(z, 0.0) - jnp.log(1.0 + jnp.exp(-jnp.abs(z)))
        log_f = jnp.where(lane < HEADS, log_f, 0.0)
        tri = jnp.where(_row_iota((tr, tr)) >= _lane_iota((tr, tr)), 1.0, 0.0).astype(BF16)
        cum = _cumsum_rows(log_f, tri, carry_ref[0:1, :])
        carry_ref[0:1, :] = cum[tr - 1:tr, :]

        ones_q = jnp.where((lane >= CUM_K) & (lane < CUM_K + 3), 1.0, 0.0)
        ones_k = jnp.where(((lane >= CUM_Q) & (lane < CUM_Q + 3)) | ((lane >= LSE_Q) & (lane < LSE_Q + 3)), 1.0, 0.0)
        aug_v = jnp.where(lane == DEN_V, 1.0, jnp.where((lane >= DELTA) & (lane < DELTA + 3), -1.0, 0.0))
        for h in range(HEADS):
            c_hi, c_mid, c_lo = _split3_f32(cum[:, h:h + 1])
            aug_q = jnp.where(lane == CUM_Q, c_hi, jnp.where(lane == CUM_Q + 1, c_mid, jnp.where(lane == CUM_Q + 2, c_lo, ones_q)))
            aug_k = jnp.where(lane == CUM_K, -c_hi, jnp.where(lane == CUM_K + 1, -c_mid, jnp.where(lane == CUM_K + 2, -c_lo, ones_k)))
            qa_ref[h] = jnp.where(lane < DH, _head_block(proj_ref, Q_OFF, h) * (DH ** -0.5), aug_q).astype(BF16)
            ka_ref[h] = jnp.where(lane < DH, _head_block(proj_ref, K_OFF, h), aug_k).astype(BF16)
            va_ref[h] = jnp.where(lane < DH, _head_block(proj_ref, V_OFF, h), aug_v).astype(BF16)

    head_spec = pl.BlockSpec((HEADS, tr, LANES), lambda i: (0, i, 0))
    head_shape = jax.ShapeDtypeStruct((HEADS, S, LANES), BF16)
    return pl.pallas_call(
        body, name="fox_prep", grid=(S // tr,), in_specs=[_row_spec(tr, D_IN_PAD), _vec_spec(LANES)],
        out_specs=[head_spec] * 3, out_shape=[head_shape] * 3, scratch_shapes=[pltpu.VMEM((SUBLANES, LANES), F32)],
        compiler_params=_params(("arbitrary",)),
    )(proj, bf_pad)


def _hosted(ex, refs, n_blocked_in, n_blocked_out, first, forward_at, last):
    n = len(ex.ins)
    own_in = refs[:n_blocked_in]
    ex_in = refs[n_blocked_in:n_blocked_in + n]
    own_out = refs[n_blocked_in + n:n_blocked_in + n + n_blocked_out]
    ex_out = refs[n_blocked_in + n + n_blocked_out:n_blocked_in + 2 * n + n_blocked_out]
    rest = refs[n_blocked_in + 2 * n + n_blocked_out:]
    args = (ex_in, ex_out, rest[-2], rest[-1])

    def begin():
        @pl.when(first)
        def _():
            ex.start(*args)

        @pl.when(forward_at)
        def _():
            ex.forward(*args)

    def end():
        @pl.when(last)
        def _():
            ex.finish(*args)

    return own_in, own_out, rest[:-2], begin, end


def _fox_fwd(qa, ka, va, ex):
    nq = S // BQ
    n_pairs = HEADS // 2

    def body(*refs):
        p_id, i = pl.program_id(0), pl.program_id(1)
        (qa_ref, ka_ref, va_ref), (y_ref, qab_ref), (m_scr, acc_scr), begin, end = _hosted(
            ex, refs, 3, 2, (p_id == 0) & (i == 0), (p_id == n_pairs - 1) & (i == 0), (p_id == n_pairs - 1) & (i == nq - 1))
        begin()
        lane = _lane_iota((BQ, LANES))
        causal = _row_iota((BQ, BK)) >= _lane_iota((BQ, BK))
        outs = []
        for hh in range(2):
            q = qa_ref[hh]
            m_scr[...] = jnp.full_like(m_scr, NEG)
            acc_scr[...] = jnp.zeros_like(acc_scr)

            def step(j, masked):
                rows = pl.ds(pl.multiple_of(j * BK, BK), BK)
                s = _dot(q, ka_ref[hh, rows, :], NT)
                if masked:
                    s = jnp.where(causal, s, NEG)
                m_prev = m_scr[...]
                m_new = jnp.maximum(m_prev, jnp.max(s, axis=1, keepdims=True))
                p = jnp.exp(s - jnp.tile(m_new, (1, BK // LANES)))
                acc_scr[...] = jnp.exp(m_prev - m_new) * acc_scr[...] + _dot(p.astype(BF16), va_ref[hh, rows, :])
                m_scr[...] = m_new

            def full_step(j, carry):
                step(j, False)
                return carry

            lax.fori_loop(0, i, full_step, 0)
            step(i, True)
            acc = acc_scr[...]
            den = jnp.broadcast_to(acc[:, DEN_V:DEN_V + 1], (BQ, LANES))
            outs.append(acc * (1.0 / den))
            n_hi, n_mid, n_lo = _split3(-(m_scr[...] + jnp.log(den)))
            qab_ref[hh] = jnp.where(lane == LSE_Q, n_hi, jnp.where(lane == LSE_Q + 1, n_mid, jnp.where(lane == LSE_Q + 2, n_lo, q)))
        y_ref[...] = jnp.where(lane < DH, outs[0], pltpu.roll(outs[1], DH, 1)).astype(BF16)
        end()

    pair_rows = pl.BlockSpec((2, BQ, LANES), lambda p, i: (p, i, 0))
    pair_all = pl.BlockSpec((2, S, LANES), lambda p, i: (p, 0, 0))
    n = len(ex.ins)
    res = pl.pallas_call(
        body, name="fox_fwd", grid=(n_pairs, nq), in_specs=[pair_rows, pair_all, pair_all] + [ANY] * n,
        out_specs=[pl.BlockSpec((BQ, LANES), lambda p, i: (i, p)), pair_rows] + [ANY] * n,
        out_shape=[jax.ShapeDtypeStruct((S, D_FOX), BF16), jax.ShapeDtypeStruct((HEADS, S, LANES), BF16)] + ex.out_shapes,
        scratch_shapes=[pltpu.VMEM((BQ, LANES), F32), pltpu.VMEM((BQ, LANES), F32)] + ex.scratch(),
        compiler_params=_params(("arbitrary", "arbitrary")),
    )(qa, ka, va, *ex.ins)
    return res[0], res[1], res[2:]


def _fox_bwd_prep(dycat, ycat):
    def body(d_ref, y_ref, doa_ref):
        lane = _lane_iota((TR, LANES))
        do = d_ref[...]
        prod = do * y_ref[...].astype(F32)
        low = lane < DH
        deltas = (jnp.sum(jnp.where(low, prod, 0.0), axis=1, keepdims=True),
                  jnp.sum(jnp.where(low, 0.0, prod), axis=1, keepdims=True))
        for hh in range(2):
            d_hi, d_mid, d_lo = _split3_f32(deltas[hh])
            aug = jnp.where(lane == DELTA, d_hi, jnp.where(lane == DELTA + 1, d_mid, jnp.where(lane == DELTA + 2, d_lo, 0.0)))
            do_h = do if hh == 0 else pltpu.roll(do, DH, 1)
            doa_ref[hh] = jnp.where(low, do_h, aug).astype(BF16)

    col = D_POOL // LANES
    blk = pl.BlockSpec((TR, LANES), lambda p, i: (i, col + p))
    return pl.pallas_call(
        body, name="fox_bwd_prep", grid=(HEADS // 2, S // TR), in_specs=[blk, blk],
        out_specs=pl.BlockSpec((2, TR, LANES), lambda p, i: (p, i, 0)),
        out_shape=jax.ShapeDtypeStruct((HEADS, S, LANES), BF16),
        compiler_params=_params(("parallel", "parallel")),
    )(dycat, ycat)


def _fox_bwd(qab, doa, ka, va, ex):
    nk = S // BK
    n_pairs = HEADS // 2

    def body(*refs):
        p_id, j = pl.program_id(0), pl.program_id(1)
        (qab_ref, doa_ref, ka_ref, va_ref), (dqa_ref, dka_ref, dva_ref), _, begin, end = _hosted(
            ex, refs, 4, 3, (p_id == 0) & (j == 0), (p_id == n_pairs - 1) & (j == 0), (p_id == n_pairs - 1) & (j == nk - 1))
        begin()

        @pl.when(j == 0)
        def _():
            dqa_ref[...] = jnp.zeros_like(dqa_ref)

        causal = _row_iota((BQ, BK)) >= _lane_iota((BQ, BK))
        for hh in range(2):
            kb = ka_ref[hh]
            vb = va_ref[hh]
            dka_ref[hh] = jnp.zeros((BK, LANES), F32)
            dva_ref[hh] = jnp.zeros((BK, LANES), F32)

            def step(i, masked):
                rows = pl.ds(pl.multiple_of(i * BQ, BQ), BQ)
                q = qab_ref[hh, rows, :]
                do = doa_ref[hh, rows, :]
                s = _dot(q, kb, NT)
                if masked:
                    s = jnp.where(causal, s, NEG)
                p = jnp.exp(s)
                ds = p * _dot(do, vb, NT)
                pb = p.astype(BF16)
                dsb = ds.astype(BF16)
                dva_ref[hh] += _dot(pb, do, TN)
                dka_ref[hh] += _dot(dsb, q, TN)
                dqa_ref[hh, rows, :] += _dot(dsb, kb)

            def full_step(i, carry):
                step(i, False)
                return carry

            step(j, True)
            lax.fori_loop(j + 1, nk, full_step, 0)
        end()

    pair_all = pl.BlockSpec((2, S, LANES), lambda p, j: (p, 0, 0))
    pair_rows = pl.BlockSpec((2, BK, LANES), lambda p, j: (p, j, 0))
    shape = jax.ShapeDtypeStruct((HEADS, S, LANES), F32)
    n = len(ex.ins)
    res = pl.pallas_call(
        body, name="fox_bwd", grid=(n_pairs, nk), in_specs=[pair_all, pair_all, pair_rows, pair_rows] + [ANY] * n,
        out_specs=[pair_all, pair_rows, pair_rows] + [ANY] * n, out_shape=[shape] * 3 + ex.out_shapes,
        scratch_shapes=ex.scratch(), compiler_params=_params(("arbitrary", "arbitrary")),
    )(qab, doa, ka, va, *ex.ins)
    return res[0], res[1], res[2], res[3:]


def _fox_bwd_post(dqa, dka, dva, du, proj, bf_pad):
    tr = PREP_TR
    nt = S // tr

    def body(dqa_ref, dka_ref, dva_ref, du_ref, z_ref, bf_ref, dp_ref, dbf_ref, carry_ref):
        i = pl.program_id(0)

        @pl.when(i == 0)
        def _():
            carry_ref[...] = jnp.zeros_like(carry_ref)
            dbf_ref[...] = jnp.zeros_like(dbf_ref)

        lane = _lane_iota((tr, LANES))
        dcum = jnp.zeros((tr, LANES), F32)
        for h in range(HEADS):
            dc = dqa_ref[h][:, CUM_Q:CUM_Q + 1] - dka_ref[h][:, CUM_K:CUM_K + 1]
            dcum = jnp.where(lane == h, dc, dcum)
        tri = jnp.where(_lane_iota((tr, tr)) >= _row_iota((tr, tr)), 1.0, 0.0).astype(BF16)
        dlog_f = _cumsum_rows(dcum, tri, carry_ref[0:1, :])
        carry_ref[0:1, :] = dlog_f[0:1, :]
        z = z_ref[...] + bf_ref[...]
        df = jnp.where(lane < HEADS, dlog_f / (1.0 + jnp.exp(z)), 0.0)
        dbf_ref[...] += jnp.sum(df, axis=0, keepdims=True)

        dp_ref[:, 0:D_POOL] = du_ref[...].astype(BF16)
        low = lane < DH
        for ref, off, scale in ((dqa_ref, Q_OFF, DH ** -0.5), (dka_ref, K_OFF, 1.0), (dva_ref, V_OFF, 1.0)):
            for p in range(HEADS // 2):
                blk = jnp.where(low, ref[2 * p], pltpu.roll(ref[2 * p + 1], DH, 1))
                dp_ref[:, off + LANES * p:off + LANES * (p + 1)] = (blk * scale).astype(BF16)
        dp_ref[:, F_OFF:F_OFF + LANES] = df.astype(BF16)

    head_spec = pl.BlockSpec((HEADS, tr, LANES), lambda i: (0, nt - 1 - i, 0))
    return pl.pallas_call(
        body, name="fox_bwd_post", grid=(nt,),
        in_specs=[head_spec, head_spec, head_spec, pl.BlockSpec((tr, D_POOL), lambda i: (nt - 1 - i, 0)),
                  pl.BlockSpec((tr, LANES), lambda i: (nt - 1 - i, F_OFF // LANES)), _vec_spec(LANES)],
        out_specs=[pl.BlockSpec((tr, D_IN_PAD), lambda i: (nt - 1 - i, 0)), _vec_spec(LANES)],
        out_shape=[jax.ShapeDtypeStruct((S, D_IN_PAD), BF16), jax.ShapeDtypeStruct((1, LANES), F32)],
        scratch_shapes=[pltpu.VMEM((SUBLANES, LANES), F32)],
        compiler_params=_params(("arbitrary",)),
    )(dqa, dka, dva, du, proj, bf_pad)


POOL_HALO = 16


def _by_group(lane, a2, a4, a8, a16):
    return jnp.where(lane < 64, a2, jnp.where(lane < 128, a4, jnp.where(lane < 192, a8, a16)))


def _window_count(lane, t):
    return jnp.minimum(t + 1, _by_group(lane, 2, 4, 8, 16)).astype(F32)


def _pool_diff(u, halo, first, tile):
    n = TR + POOL_HALO
    ext = jnp.concatenate([jnp.where(first, 0.0, halo), u], axis=0)
    s2 = ext + pltpu.roll(ext, 1, 0)
    s4 = s2 + pltpu.roll(s2, 2, 0)
    s8 = s4 + pltpu.roll(s4, 4, 0)
    s16 = s8 + pltpu.roll(s8, 8, 0)
    lane = _lane_iota((n, D_POOL))
    win = _by_group(lane, s2, s4, s8, s16)[POOL_HALO:]
    lane = _lane_iota((TR, D_POOL))
    t = tile * TR + _row_iota((TR, D_POOL))
    return win / _window_count(lane, t) - u


def _prev_halo(rows, width, col):
    per = TR // rows
    return pl.BlockSpec((rows, width), lambda i: (jnp.maximum(i * per - 1, 0), col))


def _next_halo(rows, width, col):
    per = TR // rows
    return pl.BlockSpec((rows, width), lambda i: (jnp.minimum((i + 1) * per, S // rows - 1), col))


def _pool_fwd(proj, w_bd, ps):
    def body(u_ref, halo_ref, w_ref, ps_ref, y_ref):
        i = pl.program_id(0)
        diff = _pool_diff(u_ref[...], halo_ref[...], i == 0, i)
        y_ref[...] = (_dot(diff.astype(BF16), w_ref[...]) * ps_ref[...]).astype(BF16)

    return pl.pallas_call(
        body, name="pool_fwd", grid=(S // TR,),
        in_specs=[_row_spec(TR, D_POOL), _prev_halo(POOL_HALO, D_POOL, 0),
                  pl.BlockSpec((D_POOL, D_POOL), lambda i: (0, 0)), _vec_spec(D_POOL)],
        out_specs=_row_spec(TR, D_POOL), out_shape=jax.ShapeDtypeStruct((S, D_POOL), BF16),
        compiler_params=_params(("parallel",)),
    )(proj, proj, w_bd, ps)


def _pool_bwd(proj, dycat, w_bd, w_bd_t, ps):
    nt = S // TR
    n = TR + POOL_HALO

    def body(u_ref, halo_ref, dy_ref, dyn_ref, w_ref, wt_ref, ps_ref, du_ref, dw_ref, dps_ref):
        i = pl.program_id(0)

        @pl.when(i == 0)
        def _():
            dw_ref[...] = jnp.zeros_like(dw_ref)
            dps_ref[...] = jnp.zeros_like(dps_ref)

        diff = _pool_diff(u_ref[...], halo_ref[...], i == 0, i).astype(BF16)
        dy = dy_ref[...]
        dps_ref[...] += jnp.sum(dy * _dot(diff, w_ref[...]), axis=0, keepdims=True)
        dy_ext = jnp.concatenate([dy, jnp.where(i == nt - 1, 0.0, dyn_ref[...])], axis=0)
        dmixed = (dy_ext * ps_ref[...]).astype(BF16)
        ddiff = _dot(dmixed, wt_ref[...])
        dw_ref[...] += _dot(diff, dmixed[:TR], TN)
        lane = _lane_iota((n, D_POOL))
        t = i * TR + _row_iota((n, D_POOL))
        e = ddiff / _window_count(lane, t)
        f2 = e + pltpu.roll(e, n - 1, 0)
        f4 = f2 + pltpu.roll(f2, n - 2, 0)
        f8 = f4 + pltpu.roll(f4, n - 4, 0)
        f16 = f8 + pltpu.roll(f8, n - 8, 0)
        du_ref[...] = _by_group(lane, f2, f4, f8, f16)[:TR] - ddiff[:TR]

    mat = pl.BlockSpec((D_POOL, D_POOL), lambda i: (0, 0))
    return pl.pallas_call(
        body, name="pool_bwd", grid=(nt,),
        in_specs=[_row_spec(TR, D_POOL), _prev_halo(POOL_HALO, D_POOL, 0), _row_spec(TR, D_POOL),
                  _next_halo(POOL_HALO, D_POOL, 0), mat, mat, _vec_spec(D_POOL)],
        out_specs=[_row_spec(TR, D_POOL), mat, _vec_spec(D_POOL)],
        out_shape=[jax.ShapeDtypeStruct((S, D_POOL), F32), jax.ShapeDtypeStruct((D_POOL, D_POOL), F32),
                   jax.ShapeDtypeStruct((1, D_POOL), F32)],
        compiler_params=_params(("arbitrary",)),
    )(proj, proj, dycat, dycat, w_bd, w_bd_t, ps)


def _xa_probs(q, k):
    s = _dot(q, k, NT) * (XA_DH ** -0.5)
    e = jnp.exp(s - jnp.max(s, axis=-1, keepdims=True))
    return e * (1.0 / jnp.sum(e, axis=-1, keepdims=True))


def _xattn_fwd(qx, kv):
    def body(q_ref, kv_ref, o_ref):
        for h in range(XA_HEADS):
            cols = slice(XA_DH * h, XA_DH * (h + 1))
            vcols = slice(D + XA_DH * h, D + XA_DH * (h + 1))
            p = _xa_probs(q_ref[:, cols], kv_ref[:, cols])
            o_ref[:, cols] = _dot(p.astype(BF16), kv_ref[:, vcols]).astype(BF16)

    return pl.pallas_call(
        body, name="xattn_fwd", grid=(S // TR,),
        in_specs=[_row_spec(TR, D), pl.BlockSpec((MEM, 2 * D), lambda i: (0, 0))],
        out_specs=_row_spec(TR, D), out_shape=jax.ShapeDtypeStruct((S, D), BF16),
        compiler_params=_params(("parallel",)),
    )(qx, kv)


def _xattn_bwd(qx, kv, dxo):
    def body(q_ref, kv_ref, do_ref, dq_ref, dkv_ref):
        i = pl.program_id(0)

        @pl.when(i == 0)
        def _():
            dkv_ref[...] = jnp.zeros_like(dkv_ref)

        for h in range(XA_HEADS):
            cols = slice(XA_DH * h, XA_DH * (h + 1))
            vcols = slice(D + XA_DH * h, D + XA_DH * (h + 1))
            q = q_ref[:, cols]
            k = kv_ref[:, cols]
            do = do_ref[:, cols]
            p = _xa_probs(q, k)
            dkv_ref[:, vcols] += _dot(p.astype(BF16), do, TN)
            dp = _dot(do, kv_ref[:, vcols], NT)
            ds = (p * (dp - jnp.sum(p * dp, axis=-1, keepdims=True)) * (XA_DH ** -0.5)).astype(BF16)
            dq_ref[:, cols] = _dot(ds, k).astype(BF16)
            dkv_ref[:, cols] += _dot(ds, q, TN)

    kv_spec = pl.BlockSpec((MEM, 2 * D), lambda i: (0, 0))
    return pl.pallas_call(
        body, name="xattn_bwd", grid=(S // TR,), in_specs=[_row_spec(TR, D), kv_spec, _row_spec(TR, D)],
        out_specs=[_row_spec(TR, D), kv_spec],
        out_shape=[jax.ShapeDtypeStruct((S, D), BF16), jax.ShapeDtypeStruct((MEM, 2 * D), F32)],
        compiler_params=_params(("arbitrary",)),
    )(qx, kv, dxo)


CONV_HALO = SUBLANES
TC = 512
GELU_K = 0.7978845608028654
GELU_C = 0.044715


def _conv3(ext, w, rows):
    h0 = ext[CONV_HALO:CONV_HALO + rows]
    h1 = pltpu.roll(ext, 1, 0)[CONV_HALO:CONV_HALO + rows]
    h2 = pltpu.roll(ext, 2, 0)[CONV_HALO:CONV_HALO + rows]
    return w[2:3] * h0 + w[1:2] * h1 + w[0:1] * h2 + w[3:4], (h2, h1, h0)


def _conv_specs():
    main = pl.BlockSpec((2, TR, TC), lambda j, i: (0, i, j))
    per = TR // CONV_HALO
    prev = pl.BlockSpec((2, CONV_HALO, TC), lambda j, i: (0, jnp.maximum(i * per - 1, 0), j))
    nxt = pl.BlockSpec((2, CONV_HALO, TC), lambda j, i: (0, jnp.minimum((i + 1) * per, S // CONV_HALO - 1), j))
    par = pl.BlockSpec((2, SUBLANES, TC), lambda j, i: (0, 0, j))
    return main, prev, nxt, par


def _convgate_fwd(hid, cwb):
    def body(h_ref, hp_ref, w_ref, act_ref):
        i = pl.program_id(1)
        c = []
        for g in range(2):
            ext = jnp.concatenate([jnp.where(i == 0, 0.0, hp_ref[g]), h_ref[g]], axis=0)
            c.append(_conv3(ext, w_ref[g], TR)[0])
        gate, up = c
        act_ref[...] = (jax.nn.gelu(gate, approximate=True) * up).astype(BF16)

    main, prev, _, par = _conv_specs()
    return pl.pallas_call(
        body, name="convgate_fwd", grid=(D_FF // TC, S // TR), in_specs=[main, prev, par],
        out_specs=pl.BlockSpec((TR, TC), lambda j, i: (i, j)), out_shape=jax.ShapeDtypeStruct((S, D_FF), BF16),
        compiler_params=_params(("parallel", "parallel")),
    )(hid, hid, cwb)


def _convgate_bwd(hid, dact, cwb):
    nr = S // TR
    n = TR + CONV_HALO

    def body(h_ref, hp_ref, hn_ref, da_ref, dan_ref, w_ref, dh_ref, dw_ref):
        i = pl.program_id(1)

        @pl.when(i == 0)
        def _():
            dw_ref[...] = jnp.zeros_like(dw_ref)

        da = jnp.concatenate([da_ref[...], jnp.where(i == nr - 1, 0.0, dan_ref[...])], axis=0)
        c, taps = [], []
        for g in range(2):
            ext = jnp.concatenate([jnp.where(i == 0, 0.0, hp_ref[g]), h_ref[g], hn_ref[g]], axis=0)
            cg, tg = _conv3(ext, w_ref[g], n)
            c.append(cg)
            taps.append(tg)
        gate, up = c
        th = jnp.tanh(GELU_K * (gate + GELU_C * gate * gate * gate))
        gelu = 0.5 * gate * (1.0 + th)
        dgelu = 0.5 * (1.0 + th) + 0.5 * gate * (1.0 - th * th) * GELU_K * (1.0 + 3.0 * GELU_C * gate * gate)
        for g, dc in enumerate((da * up * dgelu, da * gelu)):
            w = w_ref[g]
            dh = w[2:3] * dc[:TR] + w[1:2] * pltpu.roll(dc, n - 1, 0)[:TR] + w[0:1] * pltpu.roll(dc, n - 2, 0)[:TR]
            dh_ref[g] = dh.astype(BF16)
            dcm = dc[:TR]
            for r in range(3):
                dw_ref[g, r:r + 1, :] += jnp.sum(dcm * taps[g][r][:TR], axis=0, keepdims=True)
            dw_ref[g, 3:4, :] += jnp.sum(dcm, axis=0, keepdims=True)

    main, prev, nxt, par = _conv_specs()
    per = TR // CONV_HALO
    return pl.pallas_call(
        body, name="convgate_bwd", grid=(D_FF // TC, nr),
        in_specs=[main, prev, nxt, pl.BlockSpec((TR, TC), lambda j, i: (i, j)),
                  pl.BlockSpec((CONV_HALO, TC), lambda j, i: (jnp.minimum((i + 1) * per, S // CONV_HALO - 1), j)), par],
        out_specs=[main, par],
        out_shape=[jax.ShapeDtypeStruct((2, S, D_FF), BF16), jax.ShapeDtypeStruct((2, SUBLANES, D_FF), F32)],
        compiler_params=_params(("parallel", "arbitrary")),
    )(hid, hid, hid, dact, dact, cwb)


def _adam_update(w, g, m, v):
    m = ADAM_B1 * m + (1.0 - ADAM_B1) * g
    v = ADAM_B2 * v + (1.0 - ADAM_B2) * (g * g)
    m_hat = m / (1.0 - ADAM_B1 ** ADAM_STEP)
    v_hat = v / (1.0 - ADAM_B2 ** ADAM_STEP)
    return -ADAM_LR * (m_hat / (jnp.sqrt(v_hat) + ADAM_EPS) + ADAM_WD * w), m, v


def _row_tile(rows, cols, itemsize=4, target=TILE_BYTES):
    tr = SUBLANES
    while rows % (2 * tr) == 0 and 2 * tr * cols * itemsize <= target:
        tr *= 2
    assert rows % tr == 0, (rows, tr)
    return tr


def _adamw(name, w, g, m, v):
    rows, cols = w.shape
    tr = _row_tile(rows, cols, target=TILE_BYTES // 2)

    def body(w_ref, g_ref, m_ref, v_ref, d_ref, nm_ref, nv_ref):
        d_ref[...], nm_ref[...], nv_ref[...] = _adam_update(w_ref[...], g_ref[...], m_ref[...], v_ref[...])

    spec = _row_spec(tr, cols)
    shape = jax.ShapeDtypeStruct((rows, cols), F32)
    return pl.pallas_call(
        body, name=name, grid=(rows // tr,), in_specs=[spec] * 4, out_specs=[spec] * 3, out_shape=[shape] * 3,
        compiler_params=_params(("parallel",)),
    )(w, g, m, v)


def _adamw_halves(name, core, w, g_mine, g_sibling, m, v):
    rows, cols = w.shape
    half = rows // 2
    tr = _row_tile(half, cols, target=TILE_BYTES // 2)
    per = half // tr

    def body(core_ref, w_ref, gm_ref, gs_ref, m_ref, v_ref, g_ref, d_ref, nm_ref, nv_ref):
        g = jnp.where(pl.program_id(0) // per == core_ref[0], gm_ref[...], gs_ref[...])
        g_ref[...] = g
        d_ref[...], nm_ref[...], nv_ref[...] = _adam_update(w_ref[...], g, m_ref[...], v_ref[...])

    spec = pl.BlockSpec((tr, cols), lambda i, core_ref: (i, 0))
    half_spec = pl.BlockSpec((tr, cols), lambda i, core_ref: (i % per, 0))
    shape = jax.ShapeDtypeStruct((rows, cols), F32)
    return pl.pallas_call(
        body, name=name, out_shape=[shape] * 4,
        grid_spec=pltpu.PrefetchScalarGridSpec(
            num_scalar_prefetch=1, grid=(rows // tr,), in_specs=[spec, half_spec, half_spec, spec, spec], out_specs=[spec] * 4),
        compiler_params=_params(("parallel",)),
    )(core, w, g_mine, g_sibling, m, v)


def _chip_sum(name, core, g, other):
    _, _, half, cols = g.shape
    tr = _row_tile(half, cols)

    def body(core_ref, g_ref, o_ref, p_ref):
        p_ref[...] = (g_ref[...] + o_ref[...]).astype(BF16)

    spec = pl.BlockSpec((None, tr, cols), lambda j, i, core_ref: (j, i, 0))
    return pl.pallas_call(
        body, name=name, out_shape=jax.ShapeDtypeStruct((N_CHIPS, half, cols), BF16),
        grid_spec=pltpu.PrefetchScalarGridSpec(
            num_scalar_prefetch=1, grid=(N_CHIPS, half // tr),
            in_specs=[pl.BlockSpec((None, None, tr, cols), lambda j, i, core_ref: (j, core_ref[0], i, 0)), spec],
            out_specs=spec),
        compiler_params=_params(("parallel", "parallel")),
    )(core, g, other)


def _mesh_sum(name, chip, received, own):
    _, half, cols = received.shape
    tr = _row_tile(half, cols, itemsize=2 * N_CHIPS)

    def body(chip_ref, r_ref, own_ref, o_ref):
        acc = None
        for j in range(N_CHIPS):
            term = jnp.where(chip_ref[0] == j, own_ref[...], r_ref[j]).astype(F32)
            acc = term if acc is None else acc + term
        o_ref[...] = acc

    return pl.pallas_call(
        body, name=name, out_shape=jax.ShapeDtypeStruct((half, cols), F32),
        grid_spec=pltpu.PrefetchScalarGridSpec(
            num_scalar_prefetch=1, grid=(half // tr,),
            in_specs=[pl.BlockSpec((N_CHIPS, tr, cols), lambda i, chip_ref: (0, i, 0)),
                      pl.BlockSpec((None, tr, cols), lambda i, chip_ref: (chip_ref[0], i, 0))],
            out_specs=pl.BlockSpec((tr, cols), lambda i, chip_ref: (i, 0))),
        compiler_params=_params(("parallel",)),
    )(chip, received, own)


CHIP_FLIPS = ((1, 0), (0, 1), (1, 1))


def _place():
    x, y, c = lax.axis_index("x"), lax.axis_index("y"), lax.axis_index("c")
    return x, y, c, 2 * x + y


def _remote(src, dst, sems_s, sems_r, k, dev):
    return pltpu.make_async_remote_copy(src_ref=src, dst_ref=dst, send_sem=sems_s.at[k], recv_sem=sems_r.at[k],
                                        device_id=dev, device_id_type=MESH)


def _comm_call(name, body, ins, out_shapes, n_remote):
    return pl.pallas_call(
        body, name=name, in_specs=[ANY] * len(ins), out_specs=[ANY] * len(out_shapes), out_shape=out_shapes,
        scratch_shapes=[pltpu.SemaphoreType.DMA((n_remote,)), pltpu.SemaphoreType.DMA((n_remote,))],
    )(*ins)


class _Exchange:
    def __init__(self, ins, out_shapes, n_sems, start, forward, finish):
        self.ins, self.out_shapes, self.n_sems = list(ins), list(out_shapes), n_sems
        self.start, self.forward, self.finish = start, forward, finish

    def scratch(self):
        return [pltpu.SemaphoreType.DMA((self.n_sems,)), pltpu.SemaphoreType.DMA((self.n_sems,))]

    def run(self, name):
        n = len(self.ins)

        def body(*refs):
            args = (refs[:n], refs[n:2 * n]) + tuple(refs[2 * n:])
            self.start(*args)
            self.forward(*args)
            self.finish(*args)

        return pl.pallas_call(
            body, name=name, in_specs=[ANY] * n, out_specs=[ANY] * n, out_shape=self.out_shapes, scratch_shapes=self.scratch(),
        )(*self.ins)


def _all_gather_weights(halved, whole):
    nh, nw = len(halved), len(whole)
    n_arr = nh + nw

    def copies(ins, outs, sems_s, sems_r):
        x, y, c, me = _place()
        sibling = (x, y, 1 - c)
        own = [_remote(ins[k], outs[k].at[me], sems_s, sems_r, k, sibling) for k in range(n_arr)]
        first, passed = [], []
        for k in range(n_arr):
            for f, (fx, fy) in enumerate(CHIP_FLIPS):
                src, dst = (ins[k].at[c], outs[k].at[me, c]) if k < nh else (ins[k], outs[k].at[me])
                first.append(_remote(src, dst, sems_s, sems_r, n_arr + 3 * k + f, (x ^ fx, y ^ fy, c)))
        for k in range(nh):
            for f, (fx, fy) in enumerate(CHIP_FLIPS):
                landed = outs[k].at[2 * (x ^ fx) + (y ^ fy), c]
                passed.append(_remote(landed, landed, sems_s, sems_r, 4 * n_arr + 3 * k + f, sibling))
        return own, first, passed

    def start(*refs):
        own, first, _ = copies(*refs)
        for cp in own + first:
            cp.start()

    def forward(*refs):
        _, first, passed = copies(*refs)
        for arrived, cp in zip(first, passed):
            arrived.wait_recv()
            cp.start()

    def finish(*refs):
        own, first, passed = copies(*refs)
        for cp in first[3 * nh:] + passed + own:
            cp.wait_recv()
        for cp in first + passed + own:
            cp.wait_send()

    shapes = [jax.ShapeDtypeStruct((N_CHIPS,) + a.shape, a.dtype) for a in list(halved) + list(whole)]
    return _Exchange(list(halved) + list(whole), shapes, 7 * nh + 4 * nw, start, forward, finish)


def _swap_halves(name, gs):
    n = len(gs)

    def body(*refs):
        ins, outs = refs[:n], refs[n:2 * n]
        sems_s, sems_r = refs[2 * n:]
        x, y, c, _ = _place()
        copies = [_remote(ins[k].at[:, 1 - c], outs[k], sems_s, sems_r, k, (x, y, 1 - c)) for k in range(n)]
        for cp in copies:
            cp.start()
        for cp in copies:
            cp.wait()

    shapes = [jax.ShapeDtypeStruct((g.shape[0],) + g.shape[2:], g.dtype) for g in gs]
    return _comm_call(name, body, gs, shapes, n)


def _scatter_chips(ps):
    n = len(ps)

    def copies(ins, outs, sems_s, sems_r):
        x, y, c, me = _place()
        return [_remote(ins[k].at[2 * (x ^ fx) + (y ^ fy)], outs[k].at[me], sems_s, sems_r, 3 * k + f, (x ^ fx, y ^ fy, c))
                for k in range(n) for f, (fx, fy) in enumerate(CHIP_FLIPS)]

    def start(*refs):
        for cp in copies(*refs):
            cp.start()

    def forward(*refs):
        pass

    def finish(*refs):
        for cp in copies(*refs):
            cp.wait()

    shapes = [jax.ShapeDtypeStruct(p.shape, p.dtype) for p in ps]
    return _Exchange(ps, shapes, 3 * n, start, forward, finish)


def _swap_reduced(rs):
    n = len(rs)

    def body(*refs):
        ins, outs = refs[:n], refs[n:2 * n]
        sems_s, sems_r = refs[2 * n:]
        x, y, c, _ = _place()
        copies = [_remote(ins[k], outs[k], sems_s, sems_r, k, (x, y, 1 - c)) for k in range(n)]
        for cp in copies:
            cp.start()
        for cp in copies:
            cp.wait()

    shapes = [jax.ShapeDtypeStruct(r.shape, r.dtype) for r in rs]
    return _comm_call("swap_reduced", body, rs, shapes, n)


def _all_reduce_small(buf):
    rows = buf.shape[0]
    n_dev = 8

    def body(in_ref, out_ref, gather, sems_s, sems_r):
        x, y, c, _ = _place()
        me = 4 * x + 2 * y + c
        gather[me] = in_ref[...]
        copies = []
        for o in range(1, n_dev):
            dev = (x ^ (o >> 2), y ^ ((o >> 1) & 1), c ^ (o & 1))
            copies.append(_remote(in_ref, gather.at[me], sems_s, sems_r, o - 1, dev))
        for cp in copies:
            cp.start()
        for cp in copies:
            cp.wait()
        acc = gather[0]
        for d in range(1, n_dev):
            acc = acc + gather[d]
        out_ref[...] = acc

    return pl.pallas_call(
        body, name="all_reduce_small", in_specs=[VMEM_SPEC], out_specs=VMEM_SPEC,
        out_shape=jax.ShapeDtypeStruct((rows, LANES), F32),
        scratch_shapes=[pltpu.VMEM((n_dev, rows, LANES), F32), pltpu.SemaphoreType.DMA((n_dev - 1,)),
                        pltpu.SemaphoreType.DMA((n_dev - 1,))],
        compiler_params=pltpu.CompilerParams(vmem_limit_bytes=VMEM_LIMIT),
    )(buf)


def _no_copies(*refs):
    pass


class _NoComm:
    def gather_rest(self, p):
        return _Exchange([], [], 1, _no_copies, _no_copies, _no_copies)

    def weights_landed(self, p, landed):
        pass

    def scatter_early(self, g):
        return _Exchange([], [], 1, _no_copies, _no_copies, _no_copies)

    def scatter_landed(self, landed):
        pass


def _local_step(x, mem, target, p, comm):
    h1 = _norm_fwd("norm_mix_pre", x, p["norm_mix_pre"])
    proj = _mm_nn("in_proj", h1, p["w_in"], F32, 512, 896)
    qa, ka, va = _fox_prep(proj, p["bf_pad"])
    y_fox, qab, landed = _fox_fwd(qa, ka, va, comm.gather_rest(p))
    comm.weights_landed(p, landed)
    y_pool = _pool_fwd(proj, p["w_pool_bd"], p["pool_scale"])
    ycat = jnp.concatenate([y_pool, y_fox], axis=1)
    y1 = _mm_nn("mix_out", ycat, p["w_mix_out"], F32, 512, 1024)
    x2, h2 = _resid_norm("resid_mix", x, y1, p["norm_mix_post"], p["norm_xa_pre"])
    qx = _mm_nn("xq", h2, p["w_xq"], BF16, 512, 1024)
    mem_n = _norm_fwd("norm_mem", mem, p["norm_mem"])
    kv = _mm(
        "xkv", mem_n, p["w_xkv"], pl.BlockSpec((MEM, D), lambda i, j, k: (0, 0)),
        pl.BlockSpec((None, D, 512), lambda i, j, k: (j, 0, 0)), jax.ShapeDtypeStruct((MEM, 2 * D), BF16),
        pl.BlockSpec((MEM, 512), lambda i, j, k: (0, j)), (1, N_CHIPS, 1), NN, (MEM, 512))
    xo = _xattn_fwd(qx, kv)
    y2 = _mm_nn("xo", xo, p["w_xo"], F32, 512, 1024)
    x3, h3 = _resid_norm("resid_xa", x2, y2, p["norm_xa_post"], p["norm_ffn_pre"])
    hid = _mm(
        "up_proj", h3, p["w_up"], pl.BlockSpec((1024, D), lambda i, j, k: (i, 0)),
        pl.BlockSpec((None, D, 512), lambda i, j, k: (j // 4, 0, j % 4)), jax.ShapeDtypeStruct((2, S, D_FF), F32),
        pl.BlockSpec((None, 1024, 512), lambda i, j, k: (j // 8, i, j % 8)), (S // 1024, 16, 1), NN, (1024, 512))
    act = _convgate_fwd(hid, p["cwb"])
    y3 = _mm_nn("down_proj", act, p["w_down"], F32, 512, 512)

    g = {}
    dres, dy3, g["norm_ffn_post"], loss_cols = _loss_bwd(x3, y3, p["norm_ffn_post"], target)
    dact = _mm_nt("d_act", dy3, p["w_down"], F32, 512, 1024)
    g["w_down"] = _mm_tn("dw_down", act, dy3, 512, 512)
    dhid, dcwb = _convgate_bwd(hid, dact, p["cwb"])
    dh3 = _mm(
        "d_h3", dhid, p["w_up"], pl.BlockSpec((None, 512, 2048), lambda i, j, k: (k // 2, i, k % 2)),
        pl.BlockSpec((None, 512, 2048), lambda i, j, k: (k, j, 0)), jax.ShapeDtypeStruct((S, D), F32),
        pl.BlockSpec((512, 512), lambda i, j, k: (i, j)), (S // 512, 2, N_CHIPS), NT, (512, 512))
    g["w_up"] = _mm(
        "dw_up", h3, dhid, pl.BlockSpec((S, 512), lambda i, j, k: (0, i)),
        pl.BlockSpec((None, S, 512), lambda i, j, k: (j // 8, 0, j % 8)), jax.ShapeDtypeStruct((N_CHIPS, D, 2048), F32),
        pl.BlockSpec((None, 512, 512), lambda i, j, k: (j // 4, i, j % 4)), (2, 16, 1), TN, (512, 512))
    dres, dy2, g["norm_ffn_pre"], g["norm_xa_post"] = _mid_bwd("bwd_ffn_xa", dres, x3, p["norm_ffn_pre"], dh3, y2, p["norm_xa_post"])
    dxo = _mm_nt("d_xo", dy2, p["w_xo"], BF16, 512, 1024)
    g["w_xo"] = _mm_tn("dw_xo", xo, dy2, 512, 512)
    dqx, dkv = _xattn_bwd(qx, kv, dxo)
    dkv = dkv.astype(BF16)
    dh2 = _mm_nt("d_h2", dqx, p["w_xq"], F32, 512, 1024)
    g["w_xq"] = _mm_tn("dw_xq", h2, dqx, 512, 512)
    dmem_n = _mm(
        "d_mem", dkv, p["w_xkv"], pl.BlockSpec((MEM, 512), lambda i, j, k: (0, k)),
        pl.BlockSpec((None, D, 512), lambda i, j, k: (k, 0, 0)), jax.ShapeDtypeStruct((MEM, D), F32),
        pl.BlockSpec((MEM, D), lambda i, j, k: (0, 0)), (1, 1, N_CHIPS), NT, (MEM, D))
    g["w_xkv"] = _mm(
        "dw_xkv", mem_n, dkv, pl.BlockSpec((MEM, D), lambda i, j, k: (0, 0)),
        pl.BlockSpec((MEM, 512), lambda i, j, k: (0, j)), jax.ShapeDtypeStruct((N_CHIPS, D, 512), F32),
        pl.BlockSpec((None, D, 512), lambda i, j, k: (j, 0, 0)), (1, N_CHIPS, 1), TN, (D, 512))
    g["norm_mem"] = _gain_bwd("dg_mem", mem, p["norm_mem"], dmem_n)
    dres, dy1, g["norm_xa_pre"], g["norm_mix_post"] = _mid_bwd("bwd_xa_mix", dres, x2, p["norm_xa_pre"], dh2, y1, p["norm_mix_post"])
    dycat = _mm_nt("d_ycat", dy1, p["w_mix_out"], F32, 512, 1024)
    g["w_mix_out"] = _mm_tn("dw_mix_out", ycat, dy1, 512, 512)
    doa = _fox_bwd_prep(dycat, ycat)
    dqa, dka, dva, landed = _fox_bwd(qab, doa, ka, va, comm.scatter_early(g))
    comm.scatter_landed(landed)
    du, g["w_pool_full"], g["pool_scale"] = _pool_bwd(proj, dycat, p["w_pool_bd"], p["w_pool_bd_t"], p["pool_scale"])
    dproj, g["bf_pad"] = _fox_bwd_post(dqa, dka, dva, du, proj, p["bf_pad"])
    dh1 = _mm_nt("d_h1", dproj, p["w_in"], F32, 512, 512)
    g["w_in"] = _mm_tn("dw_in", h1, dproj, 512, 896)
    grad_x, g["norm_mix_pre"] = _first_bwd(dres, x, p["norm_mix_pre"], dh1)
    g["cwb"] = dcwb
    return grad_x, g, loss_cols


BIG = ("w_in", "w_mix_out", "w_xq", "w_xkv", "w_xo", "w_up", "w_down")
ROW_SHARDED = ("w_mix_out", "w_xq", "w_xo", "w_down")
SMALL = ("norm_mix_pre", "norm_mix_post", "b_forget", "w_pool", "pool_scale", "norm_mem", "norm_xa_pre", "norm_xa_post",
         "norm_ffn_pre", "norm_ffn_post", "conv_b")
ORDER = ("norm_mix_pre", "norm_mix_post", "w_in", "b_forget", "w_pool", "pool_scale", "w_mix_out", "norm_mem", "norm_xa_pre",
         "norm_xa_post", "w_xq", "w_xkv", "w_xo", "norm_ffn_pre", "norm_ffn_post", "w_up", "conv_w", "conv_b", "w_down")
SLOT = SUBLANES * LANES


def _pack(parts):
    rows, offs, off = [], [], 0
    for a in parts:
        flat = a.reshape(-1).astype(F32)
        n = -(-flat.shape[0] // SLOT) * SLOT
        rows.append(jnp.pad(flat, (0, n - flat.shape[0])).reshape(n // LANES, LANES))
        offs.append(off)
        off += n // LANES
    return jnp.concatenate(rows, axis=0), offs


def _unpack(buf, off, like):
    n = like.size
    rows = -(-n // LANES)
    return buf[off:off + rows].reshape(-1)[:n].reshape(like.shape)


FIRST = ("w_in", "w_mix_out")
REST = ("w_xq", "w_xkv", "w_xo", "w_up", "w_down")


def _first_params(w, full):
    w_in_full = jnp.pad(jnp.concatenate(list(full["w_in"]), axis=1), ((0, 0), (0, D_IN_PAD - D_IN)))
    w_pool_bd = jnp.zeros((D_POOL, D_POOL), F32)
    for gi in range(4):
        w_pool_bd = w_pool_bd.at[64 * gi:64 * (gi + 1), 64 * gi:64 * (gi + 1)].set(w["w_pool"][0, gi])
    p = {n: w[n] for n in ("norm_mix_pre", "norm_mix_post", "norm_mem", "norm_xa_pre", "norm_xa_post", "norm_ffn_pre",
                           "norm_ffn_post")}
    p.update(
        w_in=w_in_full, bf_pad=jnp.pad(w["b_forget"], ((0, 0), (0, LANES - HEADS))),
        w_pool_bd=w_pool_bd.astype(BF16), w_pool_bd_t=w_pool_bd.T.astype(BF16), pool_scale=w["pool_scale"].reshape(1, D_POOL),
        w_mix_out=full["w_mix_out"].reshape(D, D))
    return p


def _rest_params(w, full, conv_w_full):
    cw2 = conv_w_full.reshape(3, 2, D_FF).transpose(1, 0, 2)
    cwb = jnp.concatenate([cw2, w["conv_b"].reshape(1, 2, D_FF).transpose(1, 0, 2), jnp.zeros((2, 4, D_FF), F32)], axis=1)
    return dict(w_xq=full["w_xq"].reshape(D, D), w_xkv=full["w_xkv"], w_xo=full["w_xo"].reshape(D, D), w_up=full["w_up"],
                cwb=cwb, w_down=full["w_down"].reshape(D_FF, D))


def _whole_params(w, full, conv_w_full):
    p = _first_params(w, full)
    p.update(_rest_params(w, full, conv_w_full))
    return p


def _halved(a):
    return a.reshape(a.shape[:-2] + (2, a.shape[-2] // 2, a.shape[-1]))


class _StepComm:
    def __init__(self, w, shard2d, conv_w, core_id):
        self.w, self.shard2d, self.conv_w, self.core_id = w, shard2d, conv_w, core_id
        self.early = ("w_mix_out",) + REST
        self.partial = self.received = None

    def gather_rest(self, p):
        return _all_gather_weights([_halved(self.shard2d[n].astype(BF16)) for n in REST], [self.conv_w.reshape(3, -1)])

    def weights_landed(self, p, landed):
        full = {n: a.reshape((N_CHIPS,) + self.shard2d[n].shape) for n, a in zip(REST, landed)}
        conv_w_full = jnp.transpose(landed[-1], (1, 0, 2)).reshape(3, 2 * D_FF)
        p.update(_rest_params(self.w, full, conv_w_full))

    def scatter_early(self, g):
        views = [_halved(g[n].reshape((N_CHIPS,) + self.shard2d[n].shape)) for n in self.early]
        from_sibling = _swap_halves("swap_halves", views)
        self.partial = [_chip_sum("chip_sum_" + n, self.core_id, view, other)
                        for n, view, other in zip(self.early, views, from_sibling)]
        return _scatter_chips(self.partial)

    def scatter_landed(self, landed):
        self.received = list(landed)


def kernel(x, mem, norm_mix_pre, norm_mix_post, w_in, b_forget, w_pool, pool_scale, w_mix_out, norm_mem, norm_xa_pre, norm_xa_post, w_xq, w_xkv, w_xo, norm_ffn_pre, norm_ffn_post, w_up, conv_w, conv_b, w_down, loss_target, m_norm_mix_pre, m_norm_mix_post, m_w_in, m_b_forget, m_w_pool, m_pool_scale, m_w_mix_out, m_norm_mem, m_norm_xa_pre, m_norm_xa_post, m_w_xq, m_w_xkv, m_w_xo, m_norm_ffn_pre, m_norm_ffn_post, m_w_up, m_conv_w, m_conv_b, m_w_down, v_norm_mix_pre, v_norm_mix_post, v_w_in, v_b_forget, v_w_pool, v_pool_scale, v_w_mix_out, v_norm_mem, v_norm_xa_pre, v_norm_xa_post, v_w_xq, v_w_xkv, v_w_xo, v_norm_ffn_pre, v_norm_ffn_post, v_w_up, v_conv_w, v_conv_b, v_w_down):
    w = dict(norm_mix_pre=norm_mix_pre, norm_mix_post=norm_mix_post, w_in=w_in, b_forget=b_forget, w_pool=w_pool,
             pool_scale=pool_scale, w_mix_out=w_mix_out, norm_mem=norm_mem, norm_xa_pre=norm_xa_pre, norm_xa_post=norm_xa_post,
             w_xq=w_xq, w_xkv=w_xkv, w_xo=w_xo, norm_ffn_pre=norm_ffn_pre, norm_ffn_post=norm_ffn_post, w_up=w_up,
             conv_w=conv_w, conv_b=conv_b, w_down=w_down)
    m = dict(norm_mix_pre=m_norm_mix_pre, norm_mix_post=m_norm_mix_post, w_in=m_w_in, b_forget=m_b_forget, w_pool=m_w_pool,
             pool_scale=m_pool_scale, w_mix_out=m_w_mix_out, norm_mem=m_norm_mem, norm_xa_pre=m_norm_xa_pre,
             norm_xa_post=m_norm_xa_post, w_xq=m_w_xq, w_xkv=m_w_xkv, w_xo=m_w_xo, norm_ffn_pre=m_norm_ffn_pre,
             norm_ffn_post=m_norm_ffn_post, w_up=m_w_up, conv_w=m_conv_w, conv_b=m_conv_b, w_down=m_w_down)
    v = dict(norm_mix_pre=v_norm_mix_pre, norm_mix_post=v_norm_mix_post, w_in=v_w_in, b_forget=v_b_forget, w_pool=v_w_pool,
             pool_scale=v_pool_scale, w_mix_out=v_w_mix_out, norm_mem=v_norm_mem, norm_xa_pre=v_norm_xa_pre,
             norm_xa_post=v_norm_xa_post, w_xq=v_w_xq, w_xkv=v_w_xkv, w_xo=v_w_xo, norm_ffn_pre=v_norm_ffn_pre,
             norm_ffn_post=v_norm_ffn_post, w_up=v_w_up, conv_w=v_conv_w, conv_b=v_conv_b, w_down=v_w_down)
    chip = 2 * lax.axis_index("x") + lax.axis_index("y")

    core_id = lax.axis_index("c").astype(jnp.int32).reshape(1)
    chip_id = chip.astype(jnp.int32).reshape(1)

    shard2d = {n: w[n][0] for n in BIG}
    gathered = _all_gather_weights([_halved(shard2d[n].astype(BF16)) for n in FIRST], []).run("all_gather_first")
    p = _first_params(w, {n: a.reshape((N_CHIPS,) + shard2d[n].shape) for n, a in zip(FIRST, gathered)})

    comm = _StepComm(w, shard2d, conv_w, core_id)
    grad_x, g, loss_cols = _local_step(x[0], mem[0], loss_target[0], p, comm)

    gw_in = g["w_in"][:, :D_IN]
    view_in = _halved(jnp.stack([gw_in[:, 643 * j:643 * (j + 1)] for j in range(N_CHIPS)]))
    partial_in = _chip_sum("chip_sum_w_in", core_id, view_in, _swap_halves("swap_halves_w_in", [view_in])[0])
    received_in = _scatter_chips([partial_in]).run("scatter_w_in")[0]
    names = ("w_in",) + comm.early
    reduced = [_mesh_sum("mesh_sum_" + n, chip_id, r, own)
               for n, r, own in zip(names, [received_in] + comm.received, [partial_in] + comm.partial)]
    reduced_sibling = _swap_reduced(reduced)
    grads = {}

    gw_pool = jnp.stack([g["w_pool_full"][64 * gi:64 * (gi + 1), 64 * gi:64 * (gi + 1)] for gi in range(4)])
    dcwb = g["cwb"]
    g_conv_w = dcwb[:, 0:3, :].transpose(1, 0, 2).reshape(3, 2 * D_FF)
    g_conv_b = dcwb[:, 3, :].reshape(2 * D_FF)
    small_g = dict(norm_mix_pre=g["norm_mix_pre"], norm_mix_post=g["norm_mix_post"], b_forget=g["bf_pad"][:, :HEADS],
                   w_pool=gw_pool, pool_scale=g["pool_scale"], norm_mem=g["norm_mem"], norm_xa_pre=g["norm_xa_pre"],
                   norm_xa_post=g["norm_xa_post"], norm_ffn_pre=g["norm_ffn_pre"], norm_ffn_post=g["norm_ffn_post"],
                   conv_b=g_conv_b)
    buf, offs = _pack([small_g[n] for n in SMALL] + [g_conv_w, loss_cols])
    buf = _all_reduce_small(buf)
    for n, off in zip(SMALL, offs):
        grads[n] = _unpack(buf, off, w[n])
    g_conv_w = _unpack(buf, offs[len(SMALL)], g_conv_w)
    grads["conv_w"] = lax.dynamic_slice_in_dim(g_conv_w, chip * (2 * D_FF // N_CHIPS), 2 * D_FF // N_CHIPS, axis=1).reshape(conv_w.shape)
    loss = jnp.sum(_unpack(buf, offs[len(SMALL) + 1], loss_cols))

    delta, new_m, new_v = {}, {}, {}
    for n, g_mine, g_sibling in zip(names, reduced, reduced_sibling):
        gn, d, nm, nv = _adamw_halves("adamw_" + n, core_id, shard2d[n], g_mine, g_sibling, m[n][0], v[n][0])
        grads[n], delta[n], new_m[n], new_v[n] = gn[None], d[None], nm[None], nv[None]
    small_names = SMALL + ("conv_w",)
    packed = [_pack([d[n] for n in small_names])[0] for d in (w, grads, m, v)]
    offs = _pack([w[n] for n in small_names])[1]
    d, nm, nv = _adamw("adamw_small", *packed)
    for n, off in zip(small_names, offs):
        delta[n], new_m[n], new_v[n] = _unpack(d, off, w[n]), _unpack(nm, off, w[n]), _unpack(nv, off, w[n])

    return (loss, grad_x[None], *[grads[n] for n in ORDER], *[delta[n] for n in ORDER], *[new_m[n] for n in ORDER],
            *[new_v[n] for n in ORDER])
```

```python
import functools

import jax
import jax.numpy as jnp
from jax import lax
from jax.experimental import pallas as pl
from jax.experimental.pallas import tpu as pltpu

F32 = jnp.float32
BF16 = jnp.bfloat16
MESH = pl.DeviceIdType.MESH
ANY = pl.BlockSpec(memory_space=pl.ANY)
VMEM_SPEC = pl.BlockSpec(memory_space=pltpu.VMEM)

S = 4096
D = 1024
MEM = 256
D_POOL = 256
HEADS = 12
DH = 64
D_FOX = HEADS * DH
D_IN = D_POOL + 3 * D_FOX + HEADS
F_OFF = D_POOL + 3 * D_FOX
Q_OFF, K_OFF, V_OFF = D_POOL, D_POOL + D_FOX, D_POOL + 2 * D_FOX
XA_HEADS = 4
XA_DH = 256
D_FF = 4096
EPS = 1e-6
N_CHIPS = 4
ADAM_LR, ADAM_B1, ADAM_B2, ADAM_EPS, ADAM_WD, ADAM_STEP = 0.001, 0.9, 0.999, 1e-08, 0.01, 10

LANES = 128
SUBLANES = 8
D_IN_PAD = 21 * LANES
TR = 512
TILE_BYTES = 2 * 1024 * 1024
NEG = -1e30
VMEM_LIMIT = 52 * 1024 * 1024

NN = (((1,), (0,)), ((), ()))
NT = (((1,), (1,)), ((), ()))
TN = (((0,), (0,)), ((), ()))


def _dot(a, b, dims=NN):
    return lax.dot_general(a, b, dims, preferred_element_type=F32)


def _params(sem):
    return pltpu.CompilerParams(dimension_semantics=sem, vmem_limit_bytes=VMEM_LIMIT)


def _split3(x):
    hi = x.astype(BF16)
    r = x - hi.astype(F32)
    mid = r.astype(BF16)
    lo = (r - mid.astype(F32)).astype(BF16)
    return hi, mid, lo


def _split3_f32(x):
    hi = x.astype(BF16).astype(F32)
    r = x - hi
    mid = r.astype(BF16).astype(F32)
    return hi, mid, r - mid


def _lane_iota(shape):
    return lax.broadcasted_iota(jnp.int32, shape, len(shape) - 1)


def _row_iota(shape):
    return lax.broadcasted_iota(jnp.int32, shape, len(shape) - 2)


def _mm(name, a, b, a_spec, b_spec, out_shape, out_spec, grid, dims, acc_shape, ex=None):
    nk = grid[2]
    if ex is not None:
        return _mm_hosting(name, a, b, a_spec, b_spec, out_shape, out_spec, grid, dims, ex)

    def body(a_ref, b_ref, o_ref, *scr):
        p = _dot(a_ref[...], b_ref[...], dims)
        if nk == 1:
            o_ref[...] = p.astype(o_ref.dtype)
        else:
            acc = scr[0]
            k = pl.program_id(2)

            @pl.when(k == 0)
            def _():
                acc[...] = p

            @pl.when(k > 0)
            def _():
                acc[...] += p

            @pl.when(k == nk - 1)
            def _():
                o_ref[...] = acc[...].astype(o_ref.dtype)

    return pl.pallas_call(
        body, name=name, grid=grid, in_specs=[a_spec, b_spec], out_specs=out_spec, out_shape=out_shape,
        scratch_shapes=[pltpu.VMEM(acc_shape, F32)] if nk > 1 else [],
        compiler_params=_params(("parallel", "parallel", "arbitrary")),
    )(a, b)


def _mm_hosting(name, a, b, a_spec, b_spec, out_shape, out_spec, grid, dims, ex):
    assert grid[2] == 1
    n = len(ex.ins)

    def body(*refs):
        i, j = pl.program_id(0), pl.program_id(1)
        first = (i == 0) & (j == 0)
        (a_ref, b_ref), (o_ref,), _, begin, end = _hosted(
            ex, refs, 2, 1, first, first, (i == grid[0] - 1) & (j == grid[1] - 1))
        begin()
        o_ref[...] = _dot(a_ref[...], b_ref[...], dims).astype(o_ref.dtype)
        end()

    res = pl.pallas_call(
        body, name=name, grid=grid, in_specs=[a_spec, b_spec] + [ANY] * n, out_specs=[out_spec] + [ANY] * n,
        out_shape=[out_shape] + ex.out_shapes, scratch_shapes=ex.scratch(),
        compiler_params=_params(("arbitrary", "arbitrary", "arbitrary")),
    )(a, b, *ex.ins)
    return res[0], res[1:]


def _mm_nn(name, a, b, out_dtype, tm, tn):
    m, k = a.shape
    n = b.shape[1]
    return _mm(name, a, b, pl.BlockSpec((tm, k), lambda i, j, kk: (i, 0)), pl.BlockSpec((k, tn), lambda i, j, kk: (0, j)),
               jax.ShapeDtypeStruct((m, n), out_dtype), pl.BlockSpec((tm, tn), lambda i, j, kk: (i, j)),
               (m // tm, n // tn, 1), NN, (tm, tn))


def _mm_nt(name, a, b, out_dtype, tm, tn, ex=None):
    m, k = a.shape
    n = b.shape[0]
    return _mm(name, a, b, pl.BlockSpec((tm, k), lambda i, j, kk: (i, 0)), pl.BlockSpec((tn, k), lambda i, j, kk: (j, 0)),
               jax.ShapeDtypeStruct((m, n), out_dtype), pl.BlockSpec((tm, tn), lambda i, j, kk: (i, j)),
               (m // tm, n // tn, 1), NT, (tm, tn), ex)


def _mm_tn(name, a, b, tka, tn):
    t, ka = a.shape
    n = b.shape[1]
    return _mm(name, a, b, pl.BlockSpec((t, tka), lambda i, j, kk: (0, i)), pl.BlockSpec((t, tn), lambda i, j, kk: (0, j)),
               jax.ShapeDtypeStruct((ka, n), F32), pl.BlockSpec((tka, tn), lambda i, j, kk: (i, j)),
               (ka // tka, n // tn, 1), TN, (tka, tn))


def _d_h3(dhid, w_up):
    tm = tn = 512
    shard = 2 * D_FF // N_CHIPS
    per_plane = D_FF // shard

    def body(a_ref, b_ref, o_ref):
        acc = None
        for k in range(N_CHIPS):
            cols = slice(shard * (k % per_plane), shard * (k % per_plane + 1))
            part = _dot(a_ref[k // per_plane, :, cols], b_ref[k], NT)
            acc = part if acc is None else acc + part
        o_ref[...] = acc

    return pl.pallas_call(
        body, name="d_h3", grid=(S // tm, D // tn),
        in_specs=[pl.BlockSpec((2, tm, D_FF), lambda i, j: (0, i, 0)), pl.BlockSpec((N_CHIPS, tn, shard), lambda i, j: (0, j, 0))],
        out_specs=pl.BlockSpec((tm, tn), lambda i, j: (i, j)), out_shape=jax.ShapeDtypeStruct((S, D), F32),
        compiler_params=_params(("parallel", "parallel")),
    )(dhid, w_up)


def _rms(x, g):
    r = lax.rsqrt(jnp.mean(x * x, axis=-1, keepdims=True) + EPS)
    return x * r * g


def _rms_bwd(x, g, dy):
    r = lax.rsqrt(jnp.mean(x * x, axis=-1, keepdims=True) + EPS)
    xh = x * r
    dxh = dy * g
    dx = r * (dxh - xh * jnp.mean(dxh * xh, axis=-1, keepdims=True))
    return dx, jnp.sum(dy * xh, axis=0, keepdims=True)


def _row_spec(tr, width):
    return pl.BlockSpec((tr, width), lambda i: (i, 0))


def _vec_spec(width):
    return pl.BlockSpec((1, width), lambda i: (0, 0))


def _norm_fwd(name, x, g):
    rows, width = x.shape
    tr = min(TR, rows)

    def body(x_ref, g_ref, h_ref):
        h_ref[...] = _rms(x_ref[...], g_ref[...]).astype(BF16)

    return pl.pallas_call(
        body, name=name, grid=(rows // tr,), in_specs=[_row_spec(tr, width), _vec_spec(width)],
        out_specs=_row_spec(tr, width), out_shape=jax.ShapeDtypeStruct((rows, width), BF16),
        compiler_params=_params(("parallel",)),
    )(x, g)


def _resid_norm(name, xp, y, g_post, g_pre):
    def body(xp_ref, y_ref, gpost_ref, gpre_ref, xn_ref, h_ref):
        xn = xp_ref[...] + _rms(y_ref[...], gpost_ref[...])
        xn_ref[...] = xn
        h_ref[...] = _rms(xn, gpre_ref[...]).astype(BF16)

    return pl.pallas_call(
        body, name=name, grid=(S // TR,), in_specs=[_row_spec(TR, D), _row_spec(TR, D), _vec_spec(D), _vec_spec(D)],
        out_specs=[_row_spec(TR, D), _row_spec(TR, D)],
        out_shape=[jax.ShapeDtypeStruct((S, D), F32), jax.ShapeDtypeStruct((S, D), BF16)],
        compiler_params=_params(("parallel",)),
    )(xp, y, g_post, g_pre)


def _loss_bwd(x3, y3, g_post, target):
    def body(x_ref, y_ref, g_ref, t_ref, dres_ref, dy_ref, dg_ref, loss_ref):
        i = pl.program_id(0)

        @pl.when(i == 0)
        def _():
            dg_ref[...] = jnp.zeros_like(dg_ref)
            loss_ref[...] = jnp.zeros_like(loss_ref)

        y = y_ref[...]
        g = g_ref[...]
        e = x_ref[...] + _rms(y, g) - t_ref[...]
        loss_ref[...] += jnp.sum(e * e, axis=0, keepdims=True) * (0.5 / D)
        dres = e * (1.0 / D)
        dres_ref[...] = dres
        dy, dg = _rms_bwd(y, g, dres)
        dy_ref[...] = dy.astype(BF16)
        dg_ref[...] += dg

    return pl.pallas_call(
        body, name="loss_bwd", grid=(S // TR,),
        in_specs=[_row_spec(TR, D), _row_spec(TR, D), _vec_spec(D), _row_spec(TR, D)],
        out_specs=[_row_spec(TR, D), _row_spec(TR, D), _vec_spec(D), _vec_spec(D)],
        out_shape=[jax.ShapeDtypeStruct((S, D), F32), jax.ShapeDtypeStruct((S, D), BF16),
                   jax.ShapeDtypeStruct((1, D), F32), jax.ShapeDtypeStruct((1, D), F32)],
        compiler_params=_params(("arbitrary",)),
    )(x3, y3, g_post, target)


def _mid_bwd(name, dres, xcur, g_pre, dh, yprev, g_post):
    def body(dres_ref, x_ref, gpre_ref, dh_ref, y_ref, gpost_ref, dx_ref, dy_ref, dgpre_ref, dgpost_ref):
        i = pl.program_id(0)

        @pl.when(i == 0)
        def _():
            dgpre_ref[...] = jnp.zeros_like(dgpre_ref)
            dgpost_ref[...] = jnp.zeros_like(dgpost_ref)

        dxn, dgpre = _rms_bwd(x_ref[...], gpre_ref[...], dh_ref[...])
        dx = dres_ref[...] + dxn
        dx_ref[...] = dx
        dy, dgpost = _rms_bwd(y_ref[...], gpost_ref[...], dx)
        dy_ref[...] = dy.astype(BF16)
        dgpre_ref[...] += dgpre
        dgpost_ref[...] += dgpost

    return pl.pallas_call(
        body, name=name, grid=(S // TR,),
        in_specs=[_row_spec(TR, D), _row_spec(TR, D), _vec_spec(D), _row_spec(TR, D), _row_spec(TR, D), _vec_spec(D)],
        out_specs=[_row_spec(TR, D), _row_spec(TR, D), _vec_spec(D), _vec_spec(D)],
        out_shape=[jax.ShapeDtypeStruct((S, D), F32), jax.ShapeDtypeStruct((S, D), BF16),
                   jax.ShapeDtypeStruct((1, D), F32), jax.ShapeDtypeStruct((1, D), F32)],
        compiler_params=_params(("arbitrary",)),
    )(dres, xcur, g_pre, dh, yprev, g_post)


def _first_bwd(dres, x, g, dh):
    def body(dres_ref, x_ref, g_ref, dh_ref, dx_ref, dg_ref):
        i = pl.program_id(0)

        @pl.when(i == 0)
        def _():
            dg_ref[...] = jnp.zeros_like(dg_ref)

        dxn, dg = _rms_bwd(x_ref[...], g_ref[...], dh_ref[...])
        dx_ref[...] = dres_ref[...] + dxn
        dg_ref[...] += dg

    return pl.pallas_call(
        body, name="first_bwd", grid=(S // TR,),
        in_specs=[_row_spec(TR, D), _row_spec(TR, D), _vec_spec(D), _row_spec(TR, D)],
        out_specs=[_row_spec(TR, D), _vec_spec(D)],
        out_shape=[jax.ShapeDtypeStruct((S, D), F32), jax.ShapeDtypeStruct((1, D), F32)],
        compiler_params=_params(("arbitrary",)),
    )(dres, x, g, dh)


def _gain_bwd(name, x, g, dy):
    rows, width = x.shape

    def body(x_ref, g_ref, dy_ref, dg_ref):
        _, dg = _rms_bwd(x_ref[...], g_ref[...], dy_ref[...])
        dg_ref[...] = dg

    return pl.pallas_call(
        body, name=name, grid=(1,), in_specs=[_row_spec(rows, width), _vec_spec(width), _row_spec(rows, width)],
        out_specs=_vec_spec(width), out_shape=jax.ShapeDtypeStruct((1, width), F32),
        compiler_params=_params(("arbitrary",)),
    )(x, g, dy)


CUM_Q = DH
CUM_K = DH + 3
LSE_Q = DH + 6
DEN_V = DH
DELTA = DH + 1
PREP_TR = 256
FOX_FWD_BLOCK = 1024
FOX_BWD_BLOCK = 512


def _head_block(ref, off, h):
    start = off + DH * h
    base = (start // LANES) * LANES
    blk = ref[:, base:base + LANES]
    return pltpu.roll(blk, DH, 1) if start % LANES else blk


def _cumsum_rows(x, tri, carry):
    hi, mid, lo = _split3(x)
    return _dot(tri, hi) + _dot(tri, mid) + _dot(tri, lo) + carry


def _fox_prep(proj, bf_pad):
    tr = PREP_TR

    def body(proj_ref, bf_ref, qa_ref, ka_ref, va_ref, carry_ref):
        i = pl.program_id(0)

        @pl.when(i == 0)
        def _():
            carry_ref[...] = jnp.zeros_like(carry_ref)

        lane = _lane_iota((tr, LANES))
        z = proj_ref[:, F_OFF:F_OFF + LANES] + bf_ref[...]
        log_f = jnp.minimum(z, 0.0) - jnp.log(1.0 + jnp.exp(-jnp.abs(z)))
        log_f = jnp.where(lane < HEADS, log_f, 0.0)
        tri = jnp.where(_row_iota((tr, tr)) >= _lane_iota((tr, tr)), 1.0, 0.0).astype(BF16)
        cum = _cumsum_rows(log_f, tri, carry_ref[0:1, :])
        carry_ref[0:1, :] = cum[tr - 1:tr, :]

        ones_q = jnp.where((lane >= CUM_K) & (lane < CUM_K + 3), 1.0, 0.0)
        ones_k = jnp.where(((lane >= CUM_Q) & (lane < CUM_Q + 3)) | ((lane >= LSE_Q) & (lane < LSE_Q + 3)), 1.0, 0.0)
        aug_v = jnp.where(lane == DEN_V, 1.0, jnp.where((lane >= DELTA) & (lane < DELTA + 3), -1.0, 0.0))
        for h in range(HEADS):
            c_hi, c_mid, c_lo = _split3_f32(cum[:, h:h + 1])
            aug_q = jnp.where(lane == CUM_Q, c_hi, jnp.where(lane == CUM_Q + 1, c_mid, jnp.where(lane == CUM_Q + 2, c_lo, ones_q)))
            aug_k = jnp.where(lane == CUM_K, -c_hi, jnp.where(lane == CUM_K + 1, -c_mid, jnp.where(lane == CUM_K + 2, -c_lo, ones_k)))
            qa_ref[h] = jnp.where(lane < DH, _head_block(proj_ref, Q_OFF, h) * (DH ** -0.5), aug_q).astype(BF16)
            ka_ref[h] = jnp.where(lane < DH, _head_block(proj_ref, K_OFF, h), aug_k).astype(BF16)
            va_ref[h] = jnp.where(lane < DH, _head_block(proj_ref, V_OFF, h), aug_v).astype(BF16)

    head_spec = pl.BlockSpec((HEADS, tr, LANES), lambda i: (0, i, 0))
    head_shape = jax.ShapeDtypeStruct((HEADS, S, LANES), BF16)
    return pl.pallas_call(
        body, name="fox_prep", grid=(S // tr,), in_specs=[_row_spec(tr, D_IN_PAD), _vec_spec(LANES)],
        out_specs=[head_spec] * 3, out_shape=[head_shape] * 3, scratch_shapes=[pltpu.VMEM((SUBLANES, LANES), F32)],
        compiler_params=_params(("arbitrary",)),
    )(proj, bf_pad)


def _hosted(ex, refs, n_blocked_in, n_blocked_out, first, forward_at, last):
    n = len(ex.ins)
    own_in = refs[:n_blocked_in]
    ex_in = refs[n_blocked_in:n_blocked_in + n]
    own_out = refs[n_blocked_in + n:n_blocked_in + n + n_blocked_out]
    ex_out = refs[n_blocked_in + n + n_blocked_out:n_blocked_in + 2 * n + n_blocked_out]
    rest = refs[n_blocked_in + 2 * n + n_blocked_out:]
    args = (ex_in, ex_out, rest[-2], rest[-1])

    def begin():
        @pl.when(first)
        def _():
            ex.start(*args)

        @pl.when(forward_at)
        def _():
            ex.forward(*args)

    def end():
        @pl.when(last)
        def _():
            ex.finish(*args)

    return own_in, own_out, rest[:-2], begin, end


def _fox_fwd(qa, ka, va, ex):
    BQ = BK = FOX_FWD_BLOCK
    nq = S // BQ
    n_pairs = HEADS // 2

    def body(*refs):
        p_id, i = pl.program_id(0), pl.program_id(1)
        (qa_ref, ka_ref, va_ref), (y_ref, qab_ref), (m_scr, acc_scr), begin, end = _hosted(
            ex, refs, 3, 2, (p_id == 0) & (i == 0), (p_id == n_pairs - 1) & (i == 0), (p_id == n_pairs - 1) & (i == nq - 1))
        begin()
        lane = _lane_iota((BQ, LANES))
        causal = _row_iota((BQ, BK)) >= _lane_iota((BQ, BK))
        m_scr[...] = jnp.full_like(m_scr, NEG)
        acc_scr[...] = jnp.zeros_like(acc_scr)

        def step(j, masked):
            rows = pl.ds(pl.multiple_of(j * BK, BK), BK)
            for hh in range(2):
                s = _dot(qa_ref[hh], ka_ref[hh, rows, :], NT)
                if masked:
                    s = jnp.where(causal, s, NEG)
                m_prev = m_scr[hh]
                m_new = jnp.maximum(m_prev, jnp.max(s, axis=1, keepdims=True))
                p = jnp.exp(s - jnp.tile(m_new, (1, BK // LANES)))
                acc_scr[hh] = jnp.exp(m_prev - m_new) * acc_scr[hh] + _dot(p.astype(BF16), va_ref[hh, rows, :])
                m_scr[hh] = m_new

        def full_step(j, carry):
            step(j, False)
            return carry

        lax.fori_loop(0, i, full_step, 0)
        step(i, True)
        outs = []
        for hh in range(2):
            acc = acc_scr[hh]
            den = jnp.broadcast_to(acc[:, DEN_V:DEN_V + 1], (BQ, LANES))
            outs.append(acc * (1.0 / den))
            n_hi, n_mid, n_lo = _split3(-(m_scr[hh] + jnp.log(den)))
            qab_ref[hh] = jnp.where(lane == LSE_Q, n_hi,
                                    jnp.where(lane == LSE_Q + 1, n_mid, jnp.where(lane == LSE_Q + 2, n_lo, qa_ref[hh])))
        y_ref[...] = jnp.where(lane < DH, outs[0], pltpu.roll(outs[1], DH, 1)).astype(BF16)
        end()

    pair_rows = pl.BlockSpec((2, BQ, LANES), lambda p, i: (p, i, 0))
    pair_all = pl.BlockSpec((2, S, LANES), lambda p, i: (p, 0, 0))
    n = len(ex.ins)
    res = pl.pallas_call(
        body, name="fox_fwd", grid=(n_pairs, nq), in_specs=[pair_rows, pair_all, pair_all] + [ANY] * n,
        out_specs=[pl.BlockSpec((BQ, LANES), lambda p, i: (i, p)), pair_rows] + [ANY] * n,
        out_shape=[jax.ShapeDtypeStruct((S, D_FOX), BF16), jax.ShapeDtypeStruct((HEADS, S, LANES), BF16)] + ex.out_shapes,
        scratch_shapes=[pltpu.VMEM((2, BQ, LANES), F32), pltpu.VMEM((2, BQ, LANES), F32)] + ex.scratch(),
        compiler_params=_params(("arbitrary", "arbitrary")),
    )(qa, ka, va, *ex.ins)
    return res[0], res[1], res[2:]


def _fox_bwd_prep(dycat, ycat):
    def body(d_ref, y_ref, doa_ref):
        lane = _lane_iota((TR, LANES))
        do = d_ref[...]
        prod = do * y_ref[...].astype(F32)
        low = lane < DH
        deltas = (jnp.sum(jnp.where(low, prod, 0.0), axis=1, keepdims=True),
                  jnp.sum(jnp.where(low, 0.0, prod), axis=1, keepdims=True))
        for hh in range(2):
            d_hi, d_mid, d_lo = _split3_f32(deltas[hh])
            aug = jnp.where(lane == DELTA, d_hi, jnp.where(lane == DELTA + 1, d_mid, jnp.where(lane == DELTA + 2, d_lo, 0.0)))
            do_h = do if hh == 0 else pltpu.roll(do, DH, 1)
            doa_ref[hh] = jnp.where(low, do_h, aug).astype(BF16)

    col = D_POOL // LANES
    blk = pl.BlockSpec((TR, LANES), lambda p, i: (i, col + p))
    return pl.pallas_call(
        body, name="fox_bwd_prep", grid=(HEADS // 2, S // TR), in_specs=[blk, blk],
        out_specs=pl.BlockSpec((2, TR, LANES), lambda p, i: (p, i, 0)),
        out_shape=jax.ShapeDtypeStruct((HEADS, S, LANES), BF16),
        compiler_params=_params(("parallel", "parallel")),
    )(dycat, ycat)


def _fox_bwd(qab, doa, ka, va, ex):
    BQ = BK = FOX_BWD_BLOCK
    nk = S // BK
    n_pairs = HEADS // 2

    def body(*refs):
        p_id, j = pl.program_id(0), pl.program_id(1)
        (qab_ref, doa_ref, ka_ref, va_ref), (dqa_ref, dka_ref, dva_ref), _, begin, end = _hosted(
            ex, refs, 4, 3, (p_id == 0) & (j == 0), (p_id == n_pairs - 1) & (j == 0), (p_id == n_pairs - 1) & (j == nk - 1))
        begin()

        @pl.when(j == 0)
        def _():
            dqa_ref[...] = jnp.zeros_like(dqa_ref)

        causal = _row_iota((BQ, BK)) >= _lane_iota((BQ, BK))
        dka_ref[...] = jnp.zeros_like(dka_ref)
        dva_ref[...] = jnp.zeros_like(dva_ref)

        def step(i, masked):
            rows = pl.ds(pl.multiple_of(i * BQ, BQ), BQ)
            for hh in range(2):
                kb = ka_ref[hh]
                q = qab_ref[hh, rows, :]
                do = doa_ref[hh, rows, :]
                s = _dot(q, kb, NT)
                if masked:
                    s = jnp.where(causal, s, NEG)
                p = jnp.exp(s)
                ds = p * _dot(do, va_ref[hh], NT)
                pb = p.astype(BF16)
                dsb = ds.astype(BF16)
                dva_ref[hh] += _dot(pb, do, TN)
                dka_ref[hh] += _dot(dsb, q, TN)
                dqa_ref[hh, rows, :] += _dot(dsb, kb)

        def full_step(i, carry):
            step(i, False)
            return carry

        step(j, True)
        lax.fori_loop(j + 1, nk, full_step, 0)
        end()

    pair_all = pl.BlockSpec((2, S, LANES), lambda p, j: (p, 0, 0))
    pair_rows = pl.BlockSpec((2, BK, LANES), lambda p, j: (p, j, 0))
    shape = jax.ShapeDtypeStruct((HEADS, S, LANES), F32)
    n = len(ex.ins)
    res = pl.pallas_call(
        body, name="fox_bwd", grid=(n_pairs, nk), in_specs=[pair_all, pair_all, pair_rows, pair_rows] + [ANY] * n,
        out_specs=[pair_all, pair_rows, pair_rows] + [ANY] * n, out_shape=[shape] * 3 + ex.out_shapes,
        scratch_shapes=ex.scratch(), compiler_params=_params(("arbitrary", "arbitrary")),
    )(qab, doa, ka, va, *ex.ins)
    return res[0], res[1], res[2], res[3:]


def _fox_bwd_post(dqa, dka, dva, du, proj, bf_pad):
    tr = PREP_TR
    nt = S // tr

    def body(dqa_ref, dka_ref, dva_ref, du_ref, z_ref, bf_ref, dp_ref, dbf_ref, carry_ref):
        i = pl.program_id(0)

        @pl.when(i == 0)
        def _():
            carry_ref[...] = jnp.zeros_like(carry_ref)
            dbf_ref[...] = jnp.zeros_like(dbf_ref)

        lane = _lane_iota((tr, LANES))
        dcum = jnp.zeros((tr, LANES), F32)
        for h in range(HEADS):
            dc = dqa_ref[h][:, CUM_Q:CUM_Q + 1] - dka_ref[h][:, CUM_K:CUM_K + 1]
            dcum = jnp.where(lane == h, dc, dcum)
        tri = jnp.where(_lane_iota((tr, tr)) >= _row_iota((tr, tr)), 1.0, 0.0).astype(BF16)
        dlog_f = _cumsum_rows(dcum, tri, carry_ref[0:1, :])
        carry_ref[0:1, :] = dlog_f[0:1, :]
        z = z_ref[...] + bf_ref[...]
        df = jnp.where(lane < HEADS, dlog_f / (1.0 + jnp.exp(z)), 0.0)
        dbf_ref[...] += jnp.sum(df, axis=0, keepdims=True)

        dp_ref[:, 0:D_POOL] = du_ref[...].astype(BF16)
        low = lane < DH
        for ref, off, scale in ((dqa_ref, Q_OFF, DH ** -0.5), (dka_ref, K_OFF, 1.0), (dva_ref, V_OFF, 1.0)):
            for p in range(HEADS // 2):
                blk = jnp.where(low, ref[2 * p], pltpu.roll(ref[2 * p + 1], DH, 1))
                dp_ref[:, off + LANES * p:off + LANES * (p + 1)] = (blk * scale).astype(BF16)
        dp_ref[:, F_OFF:F_OFF + LANES] = df.astype(BF16)

    head_spec = pl.BlockSpec((HEADS, tr, LANES), lambda i: (0, nt - 1 - i, 0))
    return pl.pallas_call(
        body, name="fox_bwd_post", grid=(nt,),
        in_specs=[head_spec, head_spec, head_spec, pl.BlockSpec((tr, D_POOL), lambda i: (nt - 1 - i, 0)),
                  pl.BlockSpec((tr, LANES), lambda i: (nt - 1 - i, F_OFF // LANES)), _vec_spec(LANES)],
        out_specs=[pl.BlockSpec((tr, D_IN_PAD), lambda i: (nt - 1 - i, 0)), _vec_spec(LANES)],
        out_shape=[jax.ShapeDtypeStruct((S, D_IN_PAD), BF16), jax.ShapeDtypeStruct((1, LANES), F32)],
        scratch_shapes=[pltpu.VMEM((SUBLANES, LANES), F32)],
        compiler_params=_params(("arbitrary",)),
    )(dqa, dka, dva, du, proj, bf_pad)


POOL_HALO = 16


def _by_group(lane, a2, a4, a8, a16):
    return jnp.where(lane < 64, a2, jnp.where(lane < 128, a4, jnp.where(lane < 192, a8, a16)))


def _window_count(lane, t):
    return jnp.minimum(t + 1, _by_group(lane, 2, 4, 8, 16)).astype(F32)


def _pool_diff(u, halo, first, tile):
    n = TR + POOL_HALO
    ext = jnp.concatenate([jnp.where(first, 0.0, halo), u], axis=0)
    s2 = ext + pltpu.roll(ext, 1, 0)
    s4 = s2 + pltpu.roll(s2, 2, 0)
    s8 = s4 + pltpu.roll(s4, 4, 0)
    s16 = s8 + pltpu.roll(s8, 8, 0)
    lane = _lane_iota((n, D_POOL))
    win = _by_group(lane, s2, s4, s8, s16)[POOL_HALO:]
    lane = _lane_iota((TR, D_POOL))
    t = tile * TR + _row_iota((TR, D_POOL))
    return win / _window_count(lane, t) - u


def _prev_halo(rows, width, col):
    per = TR // rows
    return pl.BlockSpec((rows, width), lambda i: (jnp.maximum(i * per - 1, 0), col))


def _next_halo(rows, width, col):
    per = TR // rows
    return pl.BlockSpec((rows, width), lambda i: (jnp.minimum((i + 1) * per, S // rows - 1), col))


def _pool_fwd(proj, w_bd, ps):
    def body(u_ref, halo_ref, w_ref, ps_ref, y_ref):
        i = pl.program_id(0)
        diff = _pool_diff(u_ref[...], halo_ref[...], i == 0, i)
        y_ref[...] = (_dot(diff.astype(BF16), w_ref[...]) * ps_ref[...]).astype(BF16)

    return pl.pallas_call(
        body, name="pool_fwd", grid=(S // TR,),
        in_specs=[_row_spec(TR, D_POOL), _prev_halo(POOL_HALO, D_POOL, 0),
                  pl.BlockSpec((D_POOL, D_POOL), lambda i: (0, 0)), _vec_spec(D_POOL)],
        out_specs=_row_spec(TR, D_POOL), out_shape=jax.ShapeDtypeStruct((S, D_POOL), BF16),
        compiler_params=_params(("parallel",)),
    )(proj, proj, w_bd, ps)


def _pool_bwd(proj, dycat, w_bd, w_bd_t, ps):
    nt = S // TR
    n = TR + POOL_HALO

    def body(u_ref, halo_ref, dy_ref, dyn_ref, w_ref, wt_ref, ps_ref, du_ref, dw_ref, dps_ref):
        i = pl.program_id(0)

        @pl.when(i == 0)
        def _():
            dw_ref[...] = jnp.zeros_like(dw_ref)
            dps_ref[...] = jnp.zeros_like(dps_ref)

        diff = _pool_diff(u_ref[...], halo_ref[...], i == 0, i).astype(BF16)
        dy = dy_ref[...]
        dps_ref[...] += jnp.sum(dy * _dot(diff, w_ref[...]), axis=0, keepdims=True)
        dy_ext = jnp.concatenate([dy, jnp.where(i == nt - 1, 0.0, dyn_ref[...])], axis=0)
        dmixed = (dy_ext * ps_ref[...]).astype(BF16)
        ddiff = _dot(dmixed, wt_ref[...])
        dw_ref[...] += _dot(diff, dmixed[:TR], TN)
        lane = _lane_iota((n, D_POOL))
        t = i * TR + _row_iota((n, D_POOL))
        e = ddiff / _window_count(lane, t)
        f2 = e + pltpu.roll(e, n - 1, 0)
        f4 = f2 + pltpu.roll(f2, n - 2, 0)
        f8 = f4 + pltpu.roll(f4, n - 4, 0)
        f16 = f8 + pltpu.roll(f8, n - 8, 0)
        du_ref[...] = _by_group(lane, f2, f4, f8, f16)[:TR] - ddiff[:TR]

    mat = pl.BlockSpec((D_POOL, D_POOL), lambda i: (0, 0))
    return pl.pallas_call(
        body, name="pool_bwd", grid=(nt,),
        in_specs=[_row_spec(TR, D_POOL), _prev_halo(POOL_HALO, D_POOL, 0), _row_spec(TR, D_POOL),
                  _next_halo(POOL_HALO, D_POOL, 0), mat, mat, _vec_spec(D_POOL)],
        out_specs=[_row_spec(TR, D_POOL), mat, _vec_spec(D_POOL)],
        out_shape=[jax.ShapeDtypeStruct((S, D_POOL), F32), jax.ShapeDtypeStruct((D_POOL, D_POOL), F32),
                   jax.ShapeDtypeStruct((1, D_POOL), F32)],
        compiler_params=_params(("arbitrary",)),
    )(proj, proj, dycat, dycat, w_bd, w_bd_t, ps)


def _xa_probs(q, k):
    s = _dot(q, k, NT) * (XA_DH ** -0.5)
    e = jnp.exp(s - jnp.max(s, axis=-1, keepdims=True))
    return e * (1.0 / jnp.sum(e, axis=-1, keepdims=True))


def _xattn_fwd(qx, kv):
    def body(q_ref, kv_ref, o_ref):
        for h in range(XA_HEADS):
            cols = slice(XA_DH * h, XA_DH * (h + 1))
            vcols = slice(D + XA_DH * h, D + XA_DH * (h + 1))
            p = _xa_probs(q_ref[:, cols], kv_ref[:, cols])
            o_ref[:, cols] = _dot(p.astype(BF16), kv_ref[:, vcols]).astype(BF16)

    return pl.pallas_call(
        body, name="xattn_fwd", grid=(S // TR,),
        in_specs=[_row_spec(TR, D), pl.BlockSpec((MEM, 2 * D), lambda i: (0, 0))],
        out_specs=_row_spec(TR, D), out_shape=jax.ShapeDtypeStruct((S, D), BF16),
        compiler_params=_params(("parallel",)),
    )(qx, kv)


def _xattn_bwd(qx, kv, dxo):
    def body(q_ref, kv_ref, do_ref, dq_ref, dkv_ref):
        i = pl.program_id(0)

        @pl.when(i == 0)
        def _():
            dkv_ref[...] = jnp.zeros_like(dkv_ref)

        for h in range(XA_HEADS):
            cols = slice(XA_DH * h, XA_DH * (h + 1))
            vcols = slice(D + XA_DH * h, D + XA_DH * (h + 1))
            q = q_ref[:, cols]
            k = kv_ref[:, cols]
            do = do_ref[:, cols]
            p = _xa_probs(q, k)
            dkv_ref[:, vcols] += _dot(p.astype(BF16), do, TN)
            dp = _dot(do, kv_ref[:, vcols], NT)
            ds = (p * (dp - jnp.sum(p * dp, axis=-1, keepdims=True)) * (XA_DH ** -0.5)).astype(BF16)
            dq_ref[:, cols] = _dot(ds, k).astype(BF16)
            dkv_ref[:, cols] += _dot(ds, q, TN)

    kv_spec = pl.BlockSpec((MEM, 2 * D), lambda i: (0, 0))
    return pl.pallas_call(
        body, name="xattn_bwd", grid=(S // TR,), in_specs=[_row_spec(TR, D), kv_spec, _row_spec(TR, D)],
        out_specs=[_row_spec(TR, D), kv_spec],
        out_shape=[jax.ShapeDtypeStruct((S, D), BF16), jax.ShapeDtypeStruct((MEM, 2 * D), F32)],
        compiler_params=_params(("arbitrary",)),
    )(qx, kv, dxo)


CONV_HALO = SUBLANES
TC = 512
GELU_K = 0.7978845608028654
GELU_C = 0.044715


def _conv3(ext, w, rows):
    h0 = ext[CONV_HALO:CONV_HALO + rows]
    h1 = pltpu.roll(ext, 1, 0)[CONV_HALO:CONV_HALO + rows]
    h2 = pltpu.roll(ext, 2, 0)[CONV_HALO:CONV_HALO + rows]
    return w[2:3] * h0 + w[1:2] * h1 + w[0:1] * h2 + w[3:4], (h2, h1, h0)


def _conv_specs():
    main = pl.BlockSpec((2, TR, TC), lambda j, i: (0, i, j))
    per = TR // CONV_HALO
    prev = pl.BlockSpec((2, CONV_HALO, TC), lambda j, i: (0, jnp.maximum(i * per - 1, 0), j))
    nxt = pl.BlockSpec((2, CONV_HALO, TC), lambda j, i: (0, jnp.minimum((i + 1) * per, S // CONV_HALO - 1), j))
    par = pl.BlockSpec((2, SUBLANES, TC), lambda j, i: (0, 0, j))
    return main, prev, nxt, par


def _convgate_fwd(hid, cwb):
    def body(h_ref, hp_ref, w_ref, act_ref):
        i = pl.program_id(1)
        c = []
        for g in range(2):
            ext = jnp.concatenate([jnp.where(i == 0, 0.0, hp_ref[g]), h_ref[g]], axis=0)
            c.append(_conv3(ext, w_ref[g], TR)[0])
        gate, up = c
        act_ref[...] = (jax.nn.gelu(gate, approximate=True) * up).astype(BF16)

    main, prev, _, par = _conv_specs()
    return pl.pallas_call(
        body, name="convgate_fwd", grid=(D_FF // TC, S // TR), in_specs=[main, prev, par],
        out_specs=pl.BlockSpec((TR, TC), lambda j, i: (i, j)), out_shape=jax.ShapeDtypeStruct((S, D_FF), BF16),
        compiler_params=_params(("parallel", "parallel")),
    )(hid, hid, cwb)


def _convgate_bwd(hid, dact, cwb):
    nr = S // TR
    n = TR + CONV_HALO

    def body(h_ref, hp_ref, hn_ref, da_ref, dan_ref, w_ref, dh_ref, dw_ref):
        i = pl.program_id(1)

        @pl.when(i == 0)
        def _():
            dw_ref[...] = jnp.zeros_like(dw_ref)

        da = jnp.concatenate([da_ref[...], jnp.where(i == nr - 1, 0.0, dan_ref[...])], axis=0)
        c, taps = [], []
        for g in range(2):
            ext = jnp.concatenate([jnp.where(i == 0, 0.0, hp_ref[g]), h_ref[g], hn_ref[g]], axis=0)
            cg, tg = _conv3(ext, w_ref[g], n)
            c.append(cg)
            taps.append(tg)
        gate, up = c
        th = jnp.tanh(GELU_K * (gate + GELU_C * gate * gate * gate))
        gelu = 0.5 * gate * (1.0 + th)
        dgelu = 0.5 * (1.0 + th) + 0.5 * gate * (1.0 - th * th) * GELU_K * (1.0 + 3.0 * GELU_C * gate * gate)
        for g, dc in enumerate((da * up * dgelu, da * gelu)):
            w = w_ref[g]
            dh = w[2:3] * dc[:TR] + w[1:2] * pltpu.roll(dc, n - 1, 0)[:TR] + w[0:1] * pltpu.roll(dc, n - 2, 0)[:TR]
            dh_ref[g] = dh.astype(BF16)
            dcm = dc[:TR]
            for r in range(3):
                dw_ref[g, r:r + 1, :] += jnp.sum(dcm * taps[g][r][:TR], axis=0, keepdims=True)
            dw_ref[g, 3:4, :] += jnp.sum(dcm, axis=0, keepdims=True)

    main, prev, nxt, par = _conv_specs()
    per = TR // CONV_HALO
    return pl.pallas_call(
        body, name="convgate_bwd", grid=(D_FF // TC, nr),
        in_specs=[main, prev, nxt, pl.BlockSpec((TR, TC), lambda j, i: (i, j)),
                  pl.BlockSpec((CONV_HALO, TC), lambda j, i: (jnp.minimum((i + 1) * per, S // CONV_HALO - 1), j)), par],
        out_specs=[main, par],
        out_shape=[jax.ShapeDtypeStruct((2, S, D_FF), BF16), jax.ShapeDtypeStruct((2, SUBLANES, D_FF), F32)],
        compiler_params=_params(("parallel", "arbitrary")),
    )(hid, hid, hid, dact, dact, cwb)


def _adam_update(w, g, m, v):
    m = ADAM_B1 * m + (1.0 - ADAM_B1) * g
    v = ADAM_B2 * v + (1.0 - ADAM_B2) * (g * g)
    m_hat = m / (1.0 - ADAM_B1 ** ADAM_STEP)
    v_hat = v / (1.0 - ADAM_B2 ** ADAM_STEP)
    return -ADAM_LR * (m_hat / (jnp.sqrt(v_hat) + ADAM_EPS) + ADAM_WD * w), m, v


def _row_tile(rows, cols, itemsize=4, target=TILE_BYTES):
    tr = SUBLANES
    while rows % (2 * tr) == 0 and 2 * tr * cols * itemsize <= target:
        tr *= 2
    assert rows % tr == 0, (rows, tr)
    return tr


def _adamw(name, w, g, m, v):
    rows, cols = w.shape
    tr = rows if rows * cols * 4 <= TILE_BYTES // 2 else _row_tile(rows, cols, target=TILE_BYTES // 2)

    def body(w_ref, g_ref, m_ref, v_ref, d_ref, nm_ref, nv_ref):
        d_ref[...], nm_ref[...], nv_ref[...] = _adam_update(w_ref[...], g_ref[...], m_ref[...], v_ref[...])

    spec = _row_spec(tr, cols)
    shape = jax.ShapeDtypeStruct((rows, cols), F32)
    return pl.pallas_call(
        body, name=name, grid=(rows // tr,), in_specs=[spec] * 4, out_specs=[spec] * 3, out_shape=[shape] * 3,
        compiler_params=_params(("parallel",)),
    )(w, g, m, v)


def _adamw_halves(name, core, w, g_mine, g_sibling, m, v):
    rows, cols = w.shape
    half = rows // 2
    tr = _row_tile(half, cols, target=TILE_BYTES // 2)
    per = half // tr

    def body(core_ref, w_ref, gm_ref, gs_ref, m_ref, v_ref, g_ref, d_ref, nm_ref, nv_ref):
        g = jnp.where(pl.program_id(0) // per == core_ref[0], gm_ref[...], gs_ref[...])
        g_ref[...] = g
        d_ref[...], nm_ref[...], nv_ref[...] = _adam_update(w_ref[...], g, m_ref[...], v_ref[...])

    spec = pl.BlockSpec((tr, cols), lambda i, core_ref: (i, 0))
    half_spec = pl.BlockSpec((tr, cols), lambda i, core_ref: (i % per, 0))
    shape = jax.ShapeDtypeStruct((rows, cols), F32)
    return pl.pallas_call(
        body, name=name, out_shape=[shape] * 4,
        grid_spec=pltpu.PrefetchScalarGridSpec(
            num_scalar_prefetch=1, grid=(rows // tr,), in_specs=[spec, half_spec, half_spec, spec, spec], out_specs=[spec] * 4),
        compiler_params=_params(("parallel",)),
    )(core, w, g_mine, g_sibling, m, v)


def _chip_sum(name, core, g, other):
    _, _, half, cols = g.shape
    tr = _row_tile(half, cols)

    def body(core_ref, g_ref, o_ref, p_ref):
        p_ref[...] = (g_ref[...] + o_ref[...]).astype(BF16)

    spec = pl.BlockSpec((None, tr, cols), lambda j, i, core_ref: (j, i, 0))
    return pl.pallas_call(
        body, name=name, out_shape=jax.ShapeDtypeStruct((N_CHIPS, half, cols), BF16),
        grid_spec=pltpu.PrefetchScalarGridSpec(
            num_scalar_prefetch=1, grid=(N_CHIPS, half // tr),
            in_specs=[pl.BlockSpec((None, None, tr, cols), lambda j, i, core_ref: (j, core_ref[0], i, 0)), spec],
            out_specs=spec),
        compiler_params=_params(("parallel", "parallel")),
    )(core, g, other)


def _mesh_sum(name, chip, received, own):
    _, half, cols = received.shape
    tr = _row_tile(half, cols, itemsize=2 * N_CHIPS)

    def body(chip_ref, r_ref, own_ref, o_ref):
        acc = None
        for j in range(N_CHIPS):
            term = jnp.where(chip_ref[0] == j, own_ref[...], r_ref[j]).astype(F32)
            acc = term if acc is None else acc + term
        o_ref[...] = acc

    return pl.pallas_call(
        body, name=name, out_shape=jax.ShapeDtypeStruct((half, cols), F32),
        grid_spec=pltpu.PrefetchScalarGridSpec(
            num_scalar_prefetch=1, grid=(half // tr,),
            in_specs=[pl.BlockSpec((N_CHIPS, tr, cols), lambda i, chip_ref: (0, i, 0)),
                      pl.BlockSpec((None, tr, cols), lambda i, chip_ref: (chip_ref[0], i, 0))],
            out_specs=pl.BlockSpec((tr, cols), lambda i, chip_ref: (i, 0))),
        compiler_params=_params(("parallel",)),
    )(chip, received, own)


CHIP_FLIPS = ((1, 0), (0, 1), (1, 1))


def _place():
    x, y, c = lax.axis_index("x"), lax.axis_index("y"), lax.axis_index("c")
    return x, y, c, 2 * x + y


def _remote(src, dst, sems_s, sems_r, k, dev):
    return pltpu.make_async_remote_copy(src_ref=src, dst_ref=dst, send_sem=sems_s.at[k], recv_sem=sems_r.at[k],
                                        device_id=dev, device_id_type=MESH)


def _comm_call(name, body, ins, out_shapes, n_remote):
    return pl.pallas_call(
        body, name=name, in_specs=[ANY] * len(ins), out_specs=[ANY] * len(out_shapes), out_shape=out_shapes,
        scratch_shapes=[pltpu.SemaphoreType.DMA((n_remote,)), pltpu.SemaphoreType.DMA((n_remote,))],
    )(*ins)


class _Exchange:
    def __init__(self, ins, out_shapes, n_sems, start, forward, finish):
        self.ins, self.out_shapes, self.n_sems = list(ins), list(out_shapes), n_sems
        self.start, self.forward, self.finish = start, forward, finish

    def scratch(self):
        return [pltpu.SemaphoreType.DMA((self.n_sems,)), pltpu.SemaphoreType.DMA((self.n_sems,))]

    def run(self, name):
        n = len(self.ins)

        def body(*refs):
            args = (refs[:n], refs[n:2 * n]) + tuple(refs[2 * n:])
            self.start(*args)
            self.forward(*args)
            self.finish(*args)

        return pl.pallas_call(
            body, name=name, in_specs=[ANY] * n, out_specs=[ANY] * n, out_shape=self.out_shapes, scratch_shapes=self.scratch(),
        )(*self.ins)


def _all_gather_weights(halved, whole):
    nh, nw = len(halved), len(whole)
    n_arr = nh + nw

    def copies(ins, outs, sems_s, sems_r):
        x, y, c, me = _place()
        sibling = (x, y, 1 - c)
        own = [_remote(ins[k], outs[k].at[me], sems_s, sems_r, k, sibling) for k in range(n_arr)]
        first, passed = [], []
        for k in range(n_arr):
            for f, (fx, fy) in enumerate(CHIP_FLIPS):
                src, dst = (ins[k].at[c], outs[k].at[me, c]) if k < nh else (ins[k], outs[k].at[me])
                first.append(_remote(src, dst, sems_s, sems_r, n_arr + 3 * k + f, (x ^ fx, y ^ fy, c)))
        for k in range(nh):
            for f, (fx, fy) in enumerate(CHIP_FLIPS):
                landed = outs[k].at[2 * (x ^ fx) + (y ^ fy), c]
                passed.append(_remote(landed, landed, sems_s, sems_r, 4 * n_arr + 3 * k + f, sibling))
        return own, first, passed

    def start(*refs):
        own, first, _ = copies(*refs)
        for cp in own + first:
            cp.start()

    def forward(*refs):
        _, first, passed = copies(*refs)
        for arrived, cp in zip(first, passed):
            arrived.wait_recv()
            cp.start()

    def finish(*refs):
        own, first, passed = copies(*refs)
        for cp in first[3 * nh:] + passed + own:
            cp.wait_recv()
        for cp in first + passed + own:
            cp.wait_send()

    shapes = [jax.ShapeDtypeStruct((N_CHIPS,) + a.shape, a.dtype) for a in list(halved) + list(whole)]
    return _Exchange(list(halved) + list(whole), shapes, 7 * nh + 4 * nw, start, forward, finish)


def _swap_halves(name, gs):
    n = len(gs)

    def body(*refs):
        ins, outs = refs[:n], refs[n:2 * n]
        sems_s, sems_r = refs[2 * n:]
        x, y, c, _ = _place()
        copies = [_remote(ins[k].at[:, 1 - c], outs[k], sems_s, sems_r, k, (x, y, 1 - c)) for k in range(n)]
        for cp in copies:
            cp.start()
        for cp in copies:
            cp.wait()

    shapes = [jax.ShapeDtypeStruct((g.shape[0],) + g.shape[2:], g.dtype) for g in gs]
    return _comm_call(name, body, gs, shapes, n)


def _scatter_chips(ps):
    n = len(ps)

    def copies(ins, outs, sems_s, sems_r):
        x, y, c, me = _place()
        return [_remote(ins[k].at[2 * (x ^ fx) + (y ^ fy)], outs[k].at[me], sems_s, sems_r, 3 * k + f, (x ^ fx, y ^ fy, c))
                for k in range(n) for f, (fx, fy) in enumerate(CHIP_FLIPS)]

    def start(*refs):
        for cp in copies(*refs):
            cp.start()

    def forward(*refs):
        pass

    def finish(*refs):
        for cp in copies(*refs):
            cp.wait()

    shapes = [jax.ShapeDtypeStruct(p.shape, p.dtype) for p in ps]
    return _Exchange(ps, shapes, 3 * n, start, forward, finish)


def _swap_reduced(rs):
    n = len(rs)

    def body(*refs):
        ins, outs = refs[:n], refs[n:2 * n]
        sems_s, sems_r = refs[2 * n:]
        x, y, c, _ = _place()
        copies = [_remote(ins[k], outs[k], sems_s, sems_r, k, (x, y, 1 - c)) for k in range(n)]
        for cp in copies:
            cp.start()
        for cp in copies:
            cp.wait()

    shapes = [jax.ShapeDtypeStruct(r.shape, r.dtype) for r in rs]
    return _comm_call("swap_reduced", body, rs, shapes, n)


def _all_reduce_small(buf):
    rows = buf.shape[0]
    n_dev = 8

    def body(in_ref, out_ref, gather, sems_s, sems_r):
        x, y, c, _ = _place()
        me = 4 * x + 2 * y + c
        gather[me] = in_ref[...]
        copies = []
        for o in range(1, n_dev):
            dev = (x ^ (o >> 2), y ^ ((o >> 1) & 1), c ^ (o & 1))
            copies.append(_remote(in_ref, gather.at[me], sems_s, sems_r, o - 1, dev))
        for cp in copies:
            cp.start()
        for cp in copies:
            cp.wait()
        acc = gather[0]
        for d in range(1, n_dev):
            acc = acc + gather[d]
        out_ref[...] = acc

    return pl.pallas_call(
        body, name="all_reduce_small", in_specs=[VMEM_SPEC], out_specs=VMEM_SPEC,
        out_shape=jax.ShapeDtypeStruct((rows, LANES), F32),
        scratch_shapes=[pltpu.VMEM((n_dev, rows, LANES), F32), pltpu.SemaphoreType.DMA((n_dev - 1,)),
                        pltpu.SemaphoreType.DMA((n_dev - 1,))],
        compiler_params=pltpu.CompilerParams(vmem_limit_bytes=VMEM_LIMIT),
    )(buf)


def _no_copies(*refs):
    pass


class _NoComm:
    def gather_rest(self, p):
        return _Exchange([], [], 1, _no_copies, _no_copies, _no_copies)

    def weights_landed(self, p, landed):
        pass

    def scatter_early(self, g):
        return _Exchange([], [], 1, _no_copies, _no_copies, _no_copies)

    def scatter_landed(self, landed):
        pass

    def scatter_late(self, g):
        return _Exchange([], [], 1, _no_copies, _no_copies, _no_copies)

    def late_landed(self, landed):
        pass


def _local_step(x, mem, target, p, comm):
    h1 = _norm_fwd("norm_mix_pre", x, p["norm_mix_pre"])
    proj = _mm_nn("in_proj", h1, p["w_in"], F32, 1024, 896)
    qa, ka, va = _fox_prep(proj, p["bf_pad"])
    y_fox, qab, landed = _fox_fwd(qa, ka, va, comm.gather_rest(p))
    comm.weights_landed(p, landed)
    y_pool = _pool_fwd(proj, p["w_pool_bd"], p["pool_scale"])
    ycat = jnp.concatenate([y_pool, y_fox], axis=1)
    y1 = _mm_nn("mix_out", ycat, p["w_mix_out"], F32, 1024, 1024)
    x2, h2 = _resid_norm("resid_mix", x, y1, p["norm_mix_post"], p["norm_xa_pre"])
    qx = _mm_nn("xq", h2, p["w_xq"], BF16, 1024, 1024)
    mem_n = _norm_fwd("norm_mem", mem, p["norm_mem"])
    kv = _mm(
        "xkv", mem_n, p["w_xkv"], pl.BlockSpec((MEM, D), lambda i, j, k: (0, 0)),
        pl.BlockSpec((None, D, 512), lambda i, j, k: (j, 0, 0)), jax.ShapeDtypeStruct((MEM, 2 * D), BF16),
        pl.BlockSpec((MEM, 512), lambda i, j, k: (0, j)), (1, N_CHIPS, 1), NN, (MEM, 512))
    xo = _xattn_fwd(qx, kv)
    y2 = _mm_nn("xo", xo, p["w_xo"], F32, 1024, 1024)
    x3, h3 = _resid_norm("resid_xa", x2, y2, p["norm_xa_post"], p["norm_ffn_pre"])
    hid = _mm(
        "up_proj", h3, p["w_up"], pl.BlockSpec((1024, D), lambda i, j, k: (i, 0)),
        pl.BlockSpec((None, D, 1024), lambda i, j, k: (j // 2, 0, j % 2)), jax.ShapeDtypeStruct((2, S, D_FF), F32),
        pl.BlockSpec((None, 1024, 1024), lambda i, j, k: (j // 4, i, j % 4)), (S // 1024, 8, 1), NN, (1024, 1024))
    act = _convgate_fwd(hid, p["cwb"])
    y3 = _mm_nn("down_proj", act, p["w_down"], F32, 512, 512)

    g = {}
    dres, dy3, g["norm_ffn_post"], loss_cols = _loss_bwd(x3, y3, p["norm_ffn_post"], target)
    dact = _mm_nt("d_act", dy3, p["w_down"], F32, 1024, 1024)
    g["w_down"] = _mm_tn("dw_down", act, dy3, 512, 512)
    dhid, dcwb = _convgate_bwd(hid, dact, p["cwb"])
    dh3 = _d_h3(dhid, p["w_up"])
    g["w_up"] = _mm(
        "dw_up", h3, dhid, pl.BlockSpec((S, 512), lambda i, j, k: (0, i)),
        pl.BlockSpec((None, S, 512), lambda i, j, k: (j // 8, 0, j % 8)), jax.ShapeDtypeStruct((N_CHIPS, D, 2048), F32),
        pl.BlockSpec((None, 512, 512), lambda i, j, k: (j // 4, i, j % 4)), (2, 16, 1), TN, (512, 512))
    dres, dy2, g["norm_ffn_pre"], g["norm_xa_post"] = _mid_bwd("bwd_ffn_xa", dres, x3, p["norm_ffn_pre"], dh3, y2, p["norm_xa_post"])
    dxo = _mm_nt("d_xo", dy2, p["w_xo"], BF16, 1024, 1024)
    g["w_xo"] = _mm_tn("dw_xo", xo, dy2, 512, 512)
    dqx, dkv = _xattn_bwd(qx, kv, dxo)
    dkv = dkv.astype(BF16)
    dh2 = _mm_nt("d_h2", dqx, p["w_xq"], F32, 1024, 1024)
    g["w_xq"] = _mm_tn("dw_xq", h2, dqx, 512, 512)
    dmem_n = _mm(
        "d_mem", dkv, p["w_xkv"], pl.BlockSpec((MEM, 512), lambda i, j, k: (0, k)),
        pl.BlockSpec((None, D, 512), lambda i, j, k: (k, 0, 0)), jax.ShapeDtypeStruct((MEM, D), F32),
        pl.BlockSpec((MEM, D), lambda i, j, k: (0, 0)), (1, 1, N_CHIPS), NT, (MEM, D))
    g["w_xkv"] = _mm(
        "dw_xkv", mem_n, dkv, pl.BlockSpec((MEM, D), lambda i, j, k: (0, 0)),
        pl.BlockSpec((MEM, 512), lambda i, j, k: (0, j)), jax.ShapeDtypeStruct((N_CHIPS, D, 512), F32),
        pl.BlockSpec((None, D, 512), lambda i, j, k: (j, 0, 0)), (1, N_CHIPS, 1), TN, (D, 512))
    g["norm_mem"] = _gain_bwd("dg_mem", mem, p["norm_mem"], dmem_n)
    dres, dy1, g["norm_xa_pre"], g["norm_mix_post"] = _mid_bwd("bwd_xa_mix", dres, x2, p["norm_xa_pre"], dh2, y1, p["norm_mix_post"])
    dycat = _mm_nt("d_ycat", dy1, p["w_mix_out"], F32, 1024, 1024)
    g["w_mix_out"] = _mm_tn("dw_mix_out", ycat, dy1, 512, 512)
    doa = _fox_bwd_prep(dycat, ycat)
    dqa, dka, dva, landed = _fox_bwd(qab, doa, ka, va, comm.scatter_early(g))
    comm.scatter_landed(landed)
    du, g["w_pool_full"], g["pool_scale"] = _pool_bwd(proj, dycat, p["w_pool_bd"], p["w_pool_bd_t"], p["pool_scale"])
    dproj, g["bf_pad"] = _fox_bwd_post(dqa, dka, dva, du, proj, p["bf_pad"])
    g["w_in"] = _mm_tn("dw_in", h1, dproj, 512, 896)
    dh1, landed = _mm_nt("d_h1", dproj, p["w_in"], F32, 1024, 1024, comm.scatter_late(g))
    comm.late_landed(landed)
    grad_x, g["norm_mix_pre"] = _first_bwd(dres, x, p["norm_mix_pre"], dh1)
    g["cwb"] = dcwb
    return grad_x, g, loss_cols


BIG = ("w_in", "w_mix_out", "w_xq", "w_xkv", "w_xo", "w_up", "w_down")
ROW_SHARDED = ("w_mix_out", "w_xq", "w_xo", "w_down")
SMALL = ("norm_mix_pre", "norm_mix_post", "b_forget", "w_pool", "pool_scale", "norm_mem", "norm_xa_pre", "norm_xa_post",
         "norm_ffn_pre", "norm_ffn_post", "conv_b")
ORDER = ("norm_mix_pre", "norm_mix_post", "w_in", "b_forget", "w_pool", "pool_scale", "w_mix_out", "norm_mem", "norm_xa_pre",
         "norm_xa_post", "w_xq", "w_xkv", "w_xo", "norm_ffn_pre", "norm_ffn_post", "w_up", "conv_w", "conv_b", "w_down")
SLOT = SUBLANES * LANES


def _pack(parts):
    rows, offs, off = [], [], 0
    for a in parts:
        flat = a.reshape(-1).astype(F32)
        n = -(-flat.shape[0] // SLOT) * SLOT
        rows.append(jnp.pad(flat, (0, n - flat.shape[0])).reshape(n // LANES, LANES))
        offs.append(off)
        off += n // LANES
    return jnp.concatenate(rows, axis=0), offs


def _unpack(buf, off, like):
    n = like.size
    rows = -(-n // LANES)
    return buf[off:off + rows].reshape(-1)[:n].reshape(like.shape)


FIRST = ("w_in", "w_mix_out")
REST = ("w_xq", "w_xkv", "w_xo", "w_up", "w_down")


def _first_params(w, full):
    w_in_full = jnp.pad(jnp.concatenate(list(full["w_in"]), axis=1), ((0, 0), (0, D_IN_PAD - D_IN)))
    w_pool_bd = jnp.zeros((D_POOL, D_POOL), F32)
    for gi in range(4):
        w_pool_bd = w_pool_bd.at[64 * gi:64 * (gi + 1), 64 * gi:64 * (gi + 1)].set(w["w_pool"][0, gi])
    p = {n: w[n] for n in ("norm_mix_pre", "norm_mix_post", "norm_mem", "norm_xa_pre", "norm_xa_post", "norm_ffn_pre",
                           "norm_ffn_post")}
    p.update(
        w_in=w_in_full, bf_pad=jnp.pad(w["b_forget"], ((0, 0), (0, LANES - HEADS))),
        w_pool_bd=w_pool_bd.astype(BF16), w_pool_bd_t=w_pool_bd.T.astype(BF16), pool_scale=w["pool_scale"].reshape(1, D_POOL),
        w_mix_out=full["w_mix_out"].reshape(D, D))
    return p


def _rest_params(w, full, conv_w_full):
    cw2 = conv_w_full.reshape(3, 2, D_FF).transpose(1, 0, 2)
    cwb = jnp.concatenate([cw2, w["conv_b"].reshape(1, 2, D_FF).transpose(1, 0, 2), jnp.zeros((2, 4, D_FF), F32)], axis=1)
    return dict(w_xq=full["w_xq"].reshape(D, D), w_xkv=full["w_xkv"], w_xo=full["w_xo"].reshape(D, D), w_up=full["w_up"],
                cwb=cwb, w_down=full["w_down"].reshape(D_FF, D))


def _whole_params(w, full, conv_w_full):
    p = _first_params(w, full)
    p.update(_rest_params(w, full, conv_w_full))
    return p


def _halved(a):
    return a.reshape(a.shape[:-2] + (2, a.shape[-2] // 2, a.shape[-1]))


class _StepComm:
    def __init__(self, w, shard2d, conv_w, core_id):
        self.w, self.shard2d, self.conv_w, self.core_id = w, shard2d, conv_w, core_id
        self.early = ("w_mix_out",) + REST
        self.partial = self.received = None

    def gather_rest(self, p):
        return _all_gather_weights([_halved(self.shard2d[n].astype(BF16)) for n in REST], [self.conv_w.reshape(3, -1)])

    def weights_landed(self, p, landed):
        full = {n: a.reshape((N_CHIPS,) + self.shard2d[n].shape) for n, a in zip(REST, landed)}
        conv_w_full = jnp.transpose(landed[-1], (1, 0, 2)).reshape(3, 2 * D_FF)
        p.update(_rest_params(self.w, full, conv_w_full))

    def scatter_early(self, g):
        views = [_halved(g[n].reshape((N_CHIPS,) + self.shard2d[n].shape)) for n in self.early]
        from_sibling = _swap_halves("swap_halves", views)
        self.partial = [_chip_sum("chip_sum_" + n, self.core_id, view, other)
                        for n, view, other in zip(self.early, views, from_sibling)]
        return _scatter_chips(self.partial)

    def scatter_landed(self, landed):
        self.received = list(landed)

    def scatter_late(self, g):
        gw_in = g["w_in"][:, :D_IN]
        cols = D_IN // N_CHIPS
        view = _halved(jnp.stack([gw_in[:, cols * j:cols * (j + 1)] for j in range(N_CHIPS)]))
        self.partial_in = _chip_sum("chip_sum_w_in", self.core_id, view, _swap_halves("swap_halves_w_in", [view])[0])
        return _scatter_chips([self.partial_in])

    def late_landed(self, landed):
        self.received_in = landed[0]


def kernel(x, mem, norm_mix_pre, norm_mix_post, w_in, b_forget, w_pool, pool_scale, w_mix_out, norm_mem, norm_xa_pre, norm_xa_post, w_xq, w_xkv, w_xo, norm_ffn_pre, norm_ffn_post, w_up, conv_w, conv_b, w_down, loss_target, m_norm_mix_pre, m_norm_mix_post, m_w_in, m_b_forget, m_w_pool, m_pool_scale, m_w_mix_out, m_norm_mem, m_norm_xa_pre, m_norm_xa_post, m_w_xq, m_w_xkv, m_w_xo, m_norm_ffn_pre, m_norm_ffn_post, m_w_up, m_conv_w, m_conv_b, m_w_down, v_norm_mix_pre, v_norm_mix_post, v_w_in, v_b_forget, v_w_pool, v_pool_scale, v_w_mix_out, v_norm_mem, v_norm_xa_pre, v_norm_xa_post, v_w_xq, v_w_xkv, v_w_xo, v_norm_ffn_pre, v_norm_ffn_post, v_w_up, v_conv_w, v_conv_b, v_w_down):
    w = dict(norm_mix_pre=norm_mix_pre, norm_mix_post=norm_mix_post, w_in=w_in, b_forget=b_forget, w_pool=w_pool,
             pool_scale=pool_scale, w_mix_out=w_mix_out, norm_mem=norm_mem, norm_xa_pre=norm_xa_pre, norm_xa_post=norm_xa_post,
             w_xq=w_xq, w_xkv=w_xkv, w_xo=w_xo, norm_ffn_pre=norm_ffn_pre, norm_ffn_post=norm_ffn_post, w_up=w_up,
             conv_w=conv_w, conv_b=conv_b, w_down=w_down)
    m = dict(norm_mix_pre=m_norm_mix_pre, norm_mix_post=m_norm_mix_post, w_in=m_w_in, b_forget=m_b_forget, w_pool=m_w_pool,
             pool_scale=m_pool_scale, w_mix_out=m_w_mix_out, norm_mem=m_norm_mem, norm_xa_pre=m_norm_xa_pre,
             norm_xa_post=m_norm_xa_post, w_xq=m_w_xq, w_xkv=m_w_xkv, w_xo=m_w_xo, norm_ffn_pre=m_norm_ffn_pre,
             norm_ffn_post=m_norm_ffn_post, w_up=m_w_up, conv_w=m_conv_w, conv_b=m_conv_b, w_down=m_w_down)
    v = dict(norm_mix_pre=v_norm_mix_pre, norm_mix_post=v_norm_mix_post, w_in=v_w_in, b_forget=v_b_forget, w_pool=v_w_pool,
             pool_scale=v_pool_scale, w_mix_out=v_w_mix_out, norm_mem=v_norm_mem, norm_xa_pre=v_norm_xa_pre,
             norm_xa_post=v_norm_xa_post, w_xq=v_w_xq, w_xkv=v_w_xkv, w_xo=v_w_xo, norm_ffn_pre=v_norm_ffn_pre,
             norm_ffn_post=v_norm_ffn_post, w_up=v_w_up, conv_w=v_conv_w, conv_b=v_conv_b, w_down=v_w_down)
    chip = 2 * lax.axis_index("x") + lax.axis_index("y")

    core_id = lax.axis_index("c").astype(jnp.int32).reshape(1)
    chip_id = chip.astype(jnp.int32).reshape(1)

    shard2d = {n: w[n][0] for n in BIG}
    gathered = _all_gather_weights([_halved(shard2d[n].astype(BF16)) for n in FIRST], []).run("all_gather_first")
    p = _first_params(w, {n: a.reshape((N_CHIPS,) + shard2d[n].shape) for n, a in zip(FIRST, gathered)})

    comm = _StepComm(w, shard2d, conv_w, core_id)
    grad_x, g, loss_cols = _local_step(x[0], mem[0], loss_target[0], p, comm)

    names = ("w_in",) + comm.early
    reduced = [_mesh_sum("mesh_sum_" + n, chip_id, r, own)
               for n, r, own in zip(names, [comm.received_in] + comm.received, [comm.partial_in] + comm.partial)]
    reduced_sibling = _swap_reduced(reduced)
    grads = {}

    gw_pool = jnp.stack([g["w_pool_full"][64 * gi:64 * (gi + 1), 64 * gi:64 * (gi + 1)] for gi in range(4)])
    dcwb = g["cwb"]
    g_conv_w = dcwb[:, 0:3, :].transpose(1, 0, 2).reshape(3, 2 * D_FF)
    g_conv_b = dcwb[:, 3, :].reshape(2 * D_FF)
    small_g = dict(norm_mix_pre=g["norm_mix_pre"], norm_mix_post=g["norm_mix_post"], b_forget=g["bf_pad"][:, :HEADS],
                   w_pool=gw_pool, pool_scale=g["pool_scale"], norm_mem=g["norm_mem"], norm_xa_pre=g["norm_xa_pre"],
                   norm_xa_post=g["norm_xa_post"], norm_ffn_pre=g["norm_ffn_pre"], norm_ffn_post=g["norm_ffn_post"],
                   conv_b=g_conv_b)
    buf, offs = _pack([small_g[n] for n in SMALL] + [g_conv_w, loss_cols])
    buf = _all_reduce_small(buf)
    for n, off in zip(SMALL, offs):
        grads[n] = _unpack(buf, off, w[n])
    g_conv_w = _unpack(buf, offs[len(SMALL)], g_conv_w)
    grads["conv_w"] = lax.dynamic_slice_in_dim(g_conv_w, chip * (2 * D_FF // N_CHIPS), 2 * D_FF // N_CHIPS, axis=1).reshape(conv_w.shape)
    loss = jnp.sum(_unpack(buf, offs[len(SMALL) + 1], loss_cols))

    delta, new_m, new_v = {}, {}, {}
    for n, g_mine, g_sibling in zip(names, reduced, reduced_sibling):
        gn, d, nm, nv = _adamw_halves("adamw_" + n, core_id, shard2d[n], g_mine, g_sibling, m[n][0], v[n][0])
        grads[n], delta[n], new_m[n], new_v[n] = gn[None], d[None], nm[None], nv[None]
    small_names = SMALL + ("conv_w",)
    packed = [_pack([d[n] for n in small_names])[0] for d in (w, grads, m, v)]
    offs = _pack([w[n] for n in small_names])[1]
    d, nm, nv = _adamw("adamw_small", *packed)
    for n, off in zip(small_names, offs):
        delta[n], new_m[n], new_v[n] = _unpack(d, off, w[n]), _unpack(nm, off, w[n]), _unpack(nv, off, w[n])

    return (loss, grad_x[None], *[grads[n] for n in ORDER], *[delta[n] for n in ORDER], *[new_m[n] for n in ORDER],
            *[new_v[n] for n in ORDER])
```

```python
import functools

import jax
import jax.numpy as jnp
from jax import lax
from jax.experimental import pallas as pl
from jax.experimental.pallas import tpu as pltpu

F32 = jnp.float32
BF16 = jnp.bfloat16
MESH = pl.DeviceIdType.MESH
ANY = pl.BlockSpec(memory_space=pl.ANY)
VMEM_SPEC = pl.BlockSpec(memory_space=pltpu.VMEM)

S = 4096
D = 1024
MEM = 256
D_POOL = 256
HEADS = 12
DH = 64
D_FOX = HEADS * DH
D_IN = D_POOL + 3 * D_FOX + HEADS
F_OFF = D_POOL + 3 * D_FOX
Q_OFF, K_OFF, V_OFF = D_POOL, D_POOL + D_FOX, D_POOL + 2 * D_FOX
XA_HEADS = 4
XA_DH = 256
D_FF = 4096
EPS = 1e-6
N_CHIPS = 4
ADAM_LR, ADAM_B1, ADAM_B2, ADAM_EPS, ADAM_WD, ADAM_STEP = 0.001, 0.9, 0.999, 1e-08, 0.01, 10

LANES = 128
SUBLANES = 8
D_IN_PAD = 21 * LANES
TR = 512
TILE_BYTES = 2 * 1024 * 1024
NEG = -1e30
VMEM_LIMIT = 52 * 1024 * 1024

NN = (((1,), (0,)), ((), ()))
NT = (((1,), (1,)), ((), ()))
TN = (((0,), (0,)), ((), ()))


def _dot(a, b, dims=NN):
    return lax.dot_general(a, b, dims, preferred_element_type=F32)


def _params(sem):
    return pltpu.CompilerParams(dimension_semantics=sem, vmem_limit_bytes=VMEM_LIMIT)


def _split3(x):
    hi = x.astype(BF16)
    r = x - hi.astype(F32)
    mid = r.astype(BF16)
    lo = (r - mid.astype(F32)).astype(BF16)
    return hi, mid, lo


def _split3_f32(x):
    hi = x.astype(BF16).astype(F32)
    r = x - hi
    mid = r.astype(BF16).astype(F32)
    return hi, mid, r - mid


def _lane_iota(shape):
    return lax.broadcasted_iota(jnp.int32, shape, len(shape) - 1)


def _row_iota(shape):
    return lax.broadcasted_iota(jnp.int32, shape, len(shape) - 2)


def _mm(name, a, b, a_spec, b_spec, out_shape, out_spec, grid, dims, acc_shape, ex=None):
    nk = grid[2]
    if ex is not None:
        return _mm_hosting(name, a, b, a_spec, b_spec, out_shape, out_spec, grid, dims, ex)

    def body(a_ref, b_ref, o_ref, *scr):
        p = _dot(a_ref[...], b_ref[...], dims)
        if nk == 1:
            o_ref[...] = p.astype(o_ref.dtype)
        else:
            acc = scr[0]
            k = pl.program_id(2)

            @pl.when(k == 0)
            def _():
                acc[...] = p

            @pl.when(k > 0)
            def _():
                acc[...] += p

            @pl.when(k == nk - 1)
            def _():
                o_ref[...] = acc[...].astype(o_ref.dtype)

    return pl.pallas_call(
        body, name=name, grid=grid, in_specs=[a_spec, b_spec], out_specs=out_spec, out_shape=out_shape,
        scratch_shapes=[pltpu.VMEM(acc_shape, F32)] if nk > 1 else [],
        compiler_params=_params(("parallel", "parallel", "arbitrary")),
    )(a, b)


def _mm_hosting(name, a, b, a_spec, b_spec, out_shape, out_spec, grid, dims, ex):
    assert grid[2] == 1
    n = len(ex.ins)

    def body(*refs):
        i, j = pl.program_id(0), pl.program_id(1)
        first = (i == 0) & (j == 0)
        (a_ref, b_ref), (o_ref,), _, begin, end = _hosted(
            ex, refs, 2, 1, first, first, (i == grid[0] - 1) & (j == grid[1] - 1))
        begin()
        o_ref[...] = _dot(a_ref[...], b_ref[...], dims).astype(o_ref.dtype)
        end()

    res = pl.pallas_call(
        body, name=name, grid=grid, in_specs=[a_spec, b_spec] + [ANY] * n, out_specs=[out_spec] + [ANY] * n,
        out_shape=[out_shape] + ex.out_shapes, scratch_shapes=ex.scratch(),
        compiler_params=_params(("arbitrary", "arbitrary", "arbitrary")),
    )(a, b, *ex.ins)
    return res[0], res[1:]


def _mm_nn(name, a, b, out_dtype, tm, tn):
    m, k = a.shape
    n = b.shape[1]
    return _mm(name, a, b, pl.BlockSpec((tm, k), lambda i, j, kk: (i, 0)), pl.BlockSpec((k, tn), lambda i, j, kk: (0, j)),
               jax.ShapeDtypeStruct((m, n), out_dtype), pl.BlockSpec((tm, tn), lambda i, j, kk: (i, j)),
               (m // tm, n // tn, 1), NN, (tm, tn))


def _mm_nt(name, a, b, out_dtype, tm, tn, ex=None):
    m, k = a.shape
    n = b.shape[0]
    return _mm(name, a, b, pl.BlockSpec((tm, k), lambda i, j, kk: (i, 0)), pl.BlockSpec((tn, k), lambda i, j, kk: (j, 0)),
               jax.ShapeDtypeStruct((m, n), out_dtype), pl.BlockSpec((tm, tn), lambda i, j, kk: (i, j)),
               (m // tm, n // tn, 1), NT, (tm, tn), ex)


def _mm_tn(name, a, b, tka, tn):
    t, ka = a.shape
    n = b.shape[1]
    return _mm(name, a, b, pl.BlockSpec((t, tka), lambda i, j, kk: (0, i)), pl.BlockSpec((t, tn), lambda i, j, kk: (0, j)),
               jax.ShapeDtypeStruct((ka, n), F32), pl.BlockSpec((tka, tn), lambda i, j, kk: (i, j)),
               (ka // tka, n // tn, 1), TN, (tka, tn))


def _d_h3(dhid, w_up, ex):
    tm = tn = 512
    shard = 2 * D_FF // N_CHIPS
    per_plane = D_FF // shard
    grid = (S // tm, D // tn)
    n = len(ex.ins)

    def body(*refs):
        i, j = pl.program_id(0), pl.program_id(1)
        first = (i == 0) & (j == 0)
        (a_ref, b_ref), (o_ref,), _, begin, end = _hosted(ex, refs, 2, 1, first, first, (i == grid[0] - 1) & (j == grid[1] - 1))
        begin()
        acc = None
        for k in range(N_CHIPS):
            cols = slice(shard * (k % per_plane), shard * (k % per_plane + 1))
            part = _dot(a_ref[k // per_plane, :, cols], b_ref[k], NT)
            acc = part if acc is None else acc + part
        o_ref[...] = acc
        end()

    res = pl.pallas_call(
        body, name="d_h3", grid=grid,
        in_specs=[pl.BlockSpec((2, tm, D_FF), lambda i, j: (0, i, 0)),
                  pl.BlockSpec((N_CHIPS, tn, shard), lambda i, j: (0, j, 0))] + [ANY] * n,
        out_specs=[pl.BlockSpec((tm, tn), lambda i, j: (i, j))] + [ANY] * n,
        out_shape=[jax.ShapeDtypeStruct((S, D), F32)] + ex.out_shapes, scratch_shapes=ex.scratch(),
        compiler_params=_params(("arbitrary", "arbitrary")),
    )(dhid, w_up, *ex.ins)
    return res[0], res[1:]


def _rms(x, g):
    r = lax.rsqrt(jnp.mean(x * x, axis=-1, keepdims=True) + EPS)
    return x * r * g


def _rms_bwd(x, g, dy):
    r = lax.rsqrt(jnp.mean(x * x, axis=-1, keepdims=True) + EPS)
    xh = x * r
    dxh = dy * g
    dx = r * (dxh - xh * jnp.mean(dxh * xh, axis=-1, keepdims=True))
    return dx, jnp.sum(dy * xh, axis=0, keepdims=True)


def _row_spec(tr, width):
    return pl.BlockSpec((tr, width), lambda i: (i, 0))


def _vec_spec(width):
    return pl.BlockSpec((1, width), lambda i: (0, 0))


def _norm_fwd(name, x, g):
    rows, width = x.shape
    tr = min(TR, rows)

    def body(x_ref, g_ref, h_ref):
        h_ref[...] = _rms(x_ref[...], g_ref[...]).astype(BF16)

    return pl.pallas_call(
        body, name=name, grid=(rows // tr,), in_specs=[_row_spec(tr, width), _vec_spec(width)],
        out_specs=_row_spec(tr, width), out_shape=jax.ShapeDtypeStruct((rows, width), BF16),
        compiler_params=_params(("parallel",)),
    )(x, g)


def _resid_norm(name, xp, y, g_post, g_pre):
    def body(xp_ref, y_ref, gpost_ref, gpre_ref, xn_ref, h_ref):
        xn = xp_ref[...] + _rms(y_ref[...], gpost_ref[...])
        xn_ref[...] = xn
        h_ref[...] = _rms(xn, gpre_ref[...]).astype(BF16)

    return pl.pallas_call(
        body, name=name, grid=(S // TR,), in_specs=[_row_spec(TR, D), _row_spec(TR, D), _vec_spec(D), _vec_spec(D)],
        out_specs=[_row_spec(TR, D), _row_spec(TR, D)],
        out_shape=[jax.ShapeDtypeStruct((S, D), F32), jax.ShapeDtypeStruct((S, D), BF16)],
        compiler_params=_params(("parallel",)),
    )(xp, y, g_post, g_pre)


def _loss_bwd(x3, y3, g_post, target):
    def body(x_ref, y_ref, g_ref, t_ref, dres_ref, dy_ref, dg_ref, loss_ref):
        i = pl.program_id(0)

        @pl.when(i == 0)
        def _():
            dg_ref[...] = jnp.zeros_like(dg_ref)
            loss_ref[...] = jnp.zeros_like(loss_ref)

        y = y_ref[...]
        g = g_ref[...]
        e = x_ref[...] + _rms(y, g) - t_ref[...]
        loss_ref[...] += jnp.sum(e * e, axis=0, keepdims=True) * (0.5 / D)
        dres = e * (1.0 / D)
        dres_ref[...] = dres
        dy, dg = _rms_bwd(y, g, dres)
        dy_ref[...] = dy.astype(BF16)
        dg_ref[...] += dg

    return pl.pallas_call(
        body, name="loss_bwd", grid=(S // TR,),
        in_specs=[_row_spec(TR, D), _row_spec(TR, D), _vec_spec(D), _row_spec(TR, D)],
        out_specs=[_row_spec(TR, D), _row_spec(TR, D), _vec_spec(D), _vec_spec(D)],
        out_shape=[jax.ShapeDtypeStruct((S, D), F32), jax.ShapeDtypeStruct((S, D), BF16),
                   jax.ShapeDtypeStruct((1, D), F32), jax.ShapeDtypeStruct((1, D), F32)],
        compiler_params=_params(("arbitrary",)),
    )(x3, y3, g_post, target)


def _mid_bwd(name, dres, xcur, g_pre, dh, yprev, g_post):
    def body(dres_ref, x_ref, gpre_ref, dh_ref, y_ref, gpost_ref, dx_ref, dy_ref, dgpre_ref, dgpost_ref):
        i = pl.program_id(0)

        @pl.when(i == 0)
        def _():
            dgpre_ref[...] = jnp.zeros_like(dgpre_ref)
            dgpost_ref[...] = jnp.zeros_like(dgpost_ref)

        dxn, dgpre = _rms_bwd(x_ref[...], gpre_ref[...], dh_ref[...])
        dx = dres_ref[...] + dxn
        dx_ref[...] = dx
        dy, dgpost = _rms_bwd(y_ref[...], gpost_ref[...], dx)
        dy_ref[...] = dy.astype(BF16)
        dgpre_ref[...] += dgpre
        dgpost_ref[...] += dgpost

    return pl.pallas_call(
        body, name=name, grid=(S // TR,),
        in_specs=[_row_spec(TR, D), _row_spec(TR, D), _vec_spec(D), _row_spec(TR, D), _row_spec(TR, D), _vec_spec(D)],
        out_specs=[_row_spec(TR, D), _row_spec(TR, D), _vec_spec(D), _vec_spec(D)],
        out_shape=[jax.ShapeDtypeStruct((S, D), F32), jax.ShapeDtypeStruct((S, D), BF16),
                   jax.ShapeDtypeStruct((1, D), F32), jax.ShapeDtypeStruct((1, D), F32)],
        compiler_params=_params(("arbitrary",)),
    )(dres, xcur, g_pre, dh, yprev, g_post)


def _first_bwd(dres, x, g, dh):
    def body(dres_ref, x_ref, g_ref, dh_ref, dx_ref, dg_ref):
        i = pl.program_id(0)

        @pl.when(i == 0)
        def _():
            dg_ref[...] = jnp.zeros_like(dg_ref)

        dxn, dg = _rms_bwd(x_ref[...], g_ref[...], dh_ref[...])
        dx_ref[...] = dres_ref[...] + dxn
        dg_ref[...] += dg

    return pl.pallas_call(
        body, name="first_bwd", grid=(S // TR,),
        in_specs=[_row_spec(TR, D), _row_spec(TR, D), _vec_spec(D), _row_spec(TR, D)],
        out_specs=[_row_spec(TR, D), _vec_spec(D)],
        out_shape=[jax.ShapeDtypeStruct((S, D), F32), jax.ShapeDtypeStruct((1, D), F32)],
        compiler_params=_params(("arbitrary",)),
    )(dres, x, g, dh)


def _gain_bwd(name, x, g, dy):
    rows, width = x.shape

    def body(x_ref, g_ref, dy_ref, dg_ref):
        _, dg = _rms_bwd(x_ref[...], g_ref[...], dy_ref[...])
        dg_ref[...] = dg

    return pl.pallas_call(
        body, name=name, grid=(1,), in_specs=[_row_spec(rows, width), _vec_spec(width), _row_spec(rows, width)],
        out_specs=_vec_spec(width), out_shape=jax.ShapeDtypeStruct((1, width), F32),
        compiler_params=_params(("arbitrary",)),
    )(x, g, dy)


CUM_Q = DH
CUM_K = DH + 3
LSE_Q = DH + 6
DEN_V = DH
DELTA = DH + 1
PREP_TR = 256
FOX_FWD_BLOCK = 1024
FOX_BWD_BLOCK = 512


def _head_block(ref, off, h):
    start = off + DH * h
    base = (start // LANES) * LANES
    blk = ref[:, base:base + LANES]
    return pltpu.roll(blk, DH, 1) if start % LANES else blk


def _cumsum_rows(x, tri, carry):
    hi, mid, lo = _split3(x)
    return _dot(tri, hi) + _dot(tri, mid) + _dot(tri, lo) + carry


def _fox_prep(proj, bf_pad):
    tr = PREP_TR

    def body(proj_ref, bf_ref, qa_ref, ka_ref, va_ref, carry_ref):
        i = pl.program_id(0)

        @pl.when(i == 0)
        def _():
            carry_ref[...] = jnp.zeros_like(carry_ref)

        lane = _lane_iota((tr, LANES))
        z = proj_ref[:, F_OFF:F_OFF + LANES] + bf_ref[...]
        log_f = jnp.minimum(z, 0.0) - jnp.log(1.0 + jnp.exp(-jnp.abs(z)))
        log_f = jnp.where(lane < HEADS, log_f, 0.0)
        tri = jnp.where(_row_iota((tr, tr)) >= _lane_iota((tr, tr)), 1.0, 0.0).astype(BF16)
        cum = _cumsum_rows(log_f, tri, carry_ref[0:1, :])
        carry_ref[0:1, :] = cum[tr - 1:tr, :]

        ones_q = jnp.where((lane >= CUM_K) & (lane < CUM_K + 3), 1.0, 0.0)
        ones_k = jnp.where(((lane >= CUM_Q) & (lane < CUM_Q + 3)) | ((lane >= LSE_Q) & (lane < LSE_Q + 3)), 1.0, 0.0)
        aug_v = jnp.where(lane == DEN_V, 1.0, jnp.where((lane >= DELTA) & (lane < DELTA + 3), -1.0, 0.0))
        for h in range(HEADS):
            c_hi, c_mid, c_lo = _split3_f32(cum[:, h:h + 1])
            aug_q = jnp.where(lane == CUM_Q, c_hi, jnp.where(lane == CUM_Q + 1, c_mid, jnp.where(lane == CUM_Q + 2, c_lo, ones_q)))
            aug_k = jnp.where(lane == CUM_K, -c_hi, jnp.where(lane == CUM_K + 1, -c_mid, jnp.where(lane == CUM_K + 2, -c_lo, ones_k)))
            qa_ref[h] = jnp.where(lane < DH, _head_block(proj_ref, Q_OFF, h) * (DH ** -0.5), aug_q).astype(BF16)
            ka_ref[h] = jnp.where(lane < DH, _head_block(proj_ref, K_OFF, h), aug_k).astype(BF16)
            va_ref[h] = jnp.where(lane < DH, _head_block(proj_ref, V_OFF, h), aug_v).astype(BF16)

    head_spec = pl.BlockSpec((HEADS, tr, LANES), lambda i: (0, i, 0))
    head_shape = jax.ShapeDtypeStruct((HEADS, S, LANES), BF16)
    return pl.pallas_call(
        body, name="fox_prep", grid=(S // tr,), in_specs=[_row_spec(tr, D_IN_PAD), _vec_spec(LANES)],
        out_specs=[head_spec] * 3, out_shape=[head_shape] * 3, scratch_shapes=[pltpu.VMEM((SUBLANES, LANES), F32)],
        compiler_params=_params(("arbitrary",)),
    )(proj, bf_pad)


def _hosted(ex, refs, n_blocked_in, n_blocked_out, first, forward_at, last):
    n = len(ex.ins)
    own_in = refs[:n_blocked_in]
    ex_in = refs[n_blocked_in:n_blocked_in + n]
    own_out = refs[n_blocked_in + n:n_blocked_in + n + n_blocked_out]
    ex_out = refs[n_blocked_in + n + n_blocked_out:n_blocked_in + 2 * n + n_blocked_out]
    rest = refs[n_blocked_in + 2 * n + n_blocked_out:]
    args = (ex_in, ex_out, rest[-2], rest[-1])

    def begin():
        @pl.when(first)
        def _():
            ex.start(*args)

        @pl.when(forward_at)
        def _():
            ex.forward(*args)

    def end():
        @pl.when(last)
        def _():
            ex.finish(*args)

    return own_in, own_out, rest[:-2], begin, end


def _fox_fwd(qa, ka, va, ex):
    BQ = BK = FOX_FWD_BLOCK
    nq = S // BQ
    n_pairs = HEADS // 2

    def body(*refs):
        p_id, i = pl.program_id(0), pl.program_id(1)
        (qa_ref, ka_ref, va_ref), (y_ref, qab_ref), (m_scr, acc_scr), begin, end = _hosted(
            ex, refs, 3, 2, (p_id == 0) & (i == 0), (p_id == n_pairs - 1) & (i == 0), (p_id == n_pairs - 1) & (i == nq - 1))
        begin()
        lane = _lane_iota((BQ, LANES))
        causal = _row_iota((BQ, BK)) >= _lane_iota((BQ, BK))
        m_scr[...] = jnp.full_like(m_scr, NEG)
        acc_scr[...] = jnp.zeros_like(acc_scr)

        def step(j, masked):
            rows = pl.ds(pl.multiple_of(j * BK, BK), BK)
            for hh in range(2):
                s = _dot(qa_ref[hh], ka_ref[hh, rows, :], NT)
                if masked:
                    s = jnp.where(causal, s, NEG)
                m_prev = m_scr[hh]
                m_new = jnp.maximum(m_prev, jnp.max(s, axis=1, keepdims=True))
                p = jnp.exp(s - jnp.tile(m_new, (1, BK // LANES)))
                acc_scr[hh] = jnp.exp(m_prev - m_new) * acc_scr[hh] + _dot(p.astype(BF16), va_ref[hh, rows, :])
                m_scr[hh] = m_new

        def full_step(j, carry):
            step(j, False)
            return carry

        lax.fori_loop(0, i, full_step, 0)
        step(i, True)
        outs = []
        for hh in range(2):
            acc = acc_scr[hh]
            den = jnp.broadcast_to(acc[:, DEN_V:DEN_V + 1], (BQ, LANES))
            outs.append(acc * (1.0 / den))
            n_hi, n_mid, n_lo = _split3(-(m_scr[hh] + jnp.log(den)))
            qab_ref[hh] = jnp.where(lane == LSE_Q, n_hi,
                                    jnp.where(lane == LSE_Q + 1, n_mid, jnp.where(lane == LSE_Q + 2, n_lo, qa_ref[hh])))
        y_ref[...] = jnp.where(lane < DH, outs[0], pltpu.roll(outs[1], DH, 1)).astype(BF16)
        end()

    pair_rows = pl.BlockSpec((2, BQ, LANES), lambda p, i: (p, i, 0))
    pair_all = pl.BlockSpec((2, S, LANES), lambda p, i: (p, 0, 0))
    n = len(ex.ins)
    res = pl.pallas_call(
        body, name="fox_fwd", grid=(n_pairs, nq), in_specs=[pair_rows, pair_all, pair_all] + [ANY] * n,
        out_specs=[pl.BlockSpec((BQ, LANES), lambda p, i: (i, D_POOL // LANES + p)), pair_rows] + [ANY] * n,
        out_shape=[jax.ShapeDtypeStruct((S, D), BF16), jax.ShapeDtypeStruct((HEADS, S, LANES), BF16)] + ex.out_shapes,
        scratch_shapes=[pltpu.VMEM((2, BQ, LANES), F32), pltpu.VMEM((2, BQ, LANES), F32)] + ex.scratch(),
        compiler_params=_params(("arbitrary", "arbitrary")),
    )(qa, ka, va, *ex.ins)
    return res[0], res[1], res[2:]


def _d_ycat(dy1, w_mix_out, ycat):
    def body(dy_ref, w_ref, y_ref, dp_ref, doa_ref):
        d = _dot(dy_ref[...], w_ref[...], NT)
        dp_ref[...] = d[:, :D_POOL]
        lane = _lane_iota((TR, LANES))
        low = lane < DH
        for p in range(HEADS // 2):
            cols = slice(D_POOL + LANES * p, D_POOL + LANES * (p + 1))
            do = d[:, cols]
            prod = do * y_ref[:, cols].astype(F32)
            deltas = (jnp.sum(jnp.where(low, prod, 0.0), axis=1, keepdims=True),
                      jnp.sum(jnp.where(low, 0.0, prod), axis=1, keepdims=True))
            for hh in range(2):
                d_hi, d_mid, d_lo = _split3_f32(deltas[hh])
                aug = jnp.where(lane == DELTA, d_hi, jnp.where(lane == DELTA + 1, d_mid, jnp.where(lane == DELTA + 2, d_lo, 0.0)))
                do_h = do if hh == 0 else pltpu.roll(do, DH, 1)
                doa_ref[2 * p + hh] = jnp.where(low, do_h, aug).astype(BF16)

    return pl.pallas_call(
        body, name="d_ycat", grid=(S // TR,),
        in_specs=[_row_spec(TR, D), pl.BlockSpec((D, D), lambda i: (0, 0)), _row_spec(TR, D)],
        out_specs=[_row_spec(TR, D_POOL), pl.BlockSpec((HEADS, TR, LANES), lambda i: (0, i, 0))],
        out_shape=[jax.ShapeDtypeStruct((S, D_POOL), F32), jax.ShapeDtypeStruct((HEADS, S, LANES), BF16)],
        compiler_params=_params(("parallel",)),
    )(dy1, w_mix_out, ycat)


def _fox_bwd(qab, doa, ka, va, ex):
    BQ = BK = FOX_BWD_BLOCK
    nk = S // BK
    n_pairs = HEADS // 2

    def body(*refs):
        p_id, j = pl.program_id(0), pl.program_id(1)
        (qab_ref, doa_ref, ka_ref, va_ref), (dqa_ref, dka_ref, dva_ref), _, begin, end = _hosted(
            ex, refs, 4, 3, (p_id == 0) & (j == 0), (p_id == n_pairs - 1) & (j == 0), (p_id == n_pairs - 1) & (j == nk - 1))
        begin()

        @pl.when(j == 0)
        def _():
            dqa_ref[...] = jnp.zeros_like(dqa_ref)

        causal = _row_iota((BQ, BK)) >= _lane_iota((BQ, BK))
        dka_ref[...] = jnp.zeros_like(dka_ref)
        dva_ref[...] = jnp.zeros_like(dva_ref)

        def step(i, masked):
            rows = pl.ds(pl.multiple_of(i * BQ, BQ), BQ)
            for hh in range(2):
                kb = ka_ref[hh]
                q = qab_ref[hh, rows, :]
                do = doa_ref[hh, rows, :]
                s = _dot(q, kb, NT)
                if masked:
                    s = jnp.where(causal, s, NEG)
                p = jnp.exp(s)
                ds = p * _dot(do, va_ref[hh], NT)
                pb = p.astype(BF16)
                dsb = ds.astype(BF16)
                dva_ref[hh] += _dot(pb, do, TN)
                dka_ref[hh] += _dot(dsb, q, TN)
                dqa_ref[hh, rows, :] += _dot(dsb, kb)

        def full_step(i, carry):
            step(i, False)
            return carry

        step(j, True)
        lax.fori_loop(j + 1, nk, full_step, 0)
        end()

    pair_all = pl.BlockSpec((2, S, LANES), lambda p, j: (p, 0, 0))
    pair_rows = pl.BlockSpec((2, BK, LANES), lambda p, j: (p, j, 0))
    shape = jax.ShapeDtypeStruct((HEADS, S, LANES), F32)
    n = len(ex.ins)
    res = pl.pallas_call(
        body, name="fox_bwd", grid=(n_pairs, nk), in_specs=[pair_all, pair_all, pair_rows, pair_rows] + [ANY] * n,
        out_specs=[pair_all, pair_rows, pair_rows] + [ANY] * n, out_shape=[shape] * 3 + ex.out_shapes,
        scratch_shapes=ex.scratch(), compiler_params=_params(("arbitrary", "arbitrary")),
    )(qab, doa, ka, va, *ex.ins)
    return res[0], res[1], res[2], res[3:]


def _fox_bwd_post(dqa, dka, dva, du, proj, bf_pad):
    tr = PREP_TR
    nt = S // tr

    def body(dqa_ref, dka_ref, dva_ref, du_ref, z_ref, bf_ref, dp_ref, dbf_ref, carry_ref):
        i = pl.program_id(0)

        @pl.when(i == 0)
        def _():
            carry_ref[...] = jnp.zeros_like(carry_ref)
            dbf_ref[...] = jnp.zeros_like(dbf_ref)

        lane = _lane_iota((tr, LANES))
        dcum = jnp.zeros((tr, LANES), F32)
        for h in range(HEADS):
            dc = dqa_ref[h][:, CUM_Q:CUM_Q + 1] - dka_ref[h][:, CUM_K:CUM_K + 1]
            dcum = jnp.where(lane == h, dc, dcum)
        tri = jnp.where(_lane_iota((tr, tr)) >= _row_iota((tr, tr)), 1.0, 0.0).astype(BF16)
        dlog_f = _cumsum_rows(dcum, tri, carry_ref[0:1, :])
        carry_ref[0:1, :] = dlog_f[0:1, :]
        z = z_ref[...] + bf_ref[...]
        df = jnp.where(lane < HEADS, dlog_f / (1.0 + jnp.exp(z)), 0.0)
        dbf_ref[...] += jnp.sum(df, axis=0, keepdims=True)

        dp_ref[:, 0:D_POOL] = du_ref[...].astype(BF16)
        low = lane < DH
        for ref, off, scale in ((dqa_ref, Q_OFF, DH ** -0.5), (dka_ref, K_OFF, 1.0), (dva_ref, V_OFF, 1.0)):
            for p in range(HEADS // 2):
                blk = jnp.where(low, ref[2 * p], pltpu.roll(ref[2 * p + 1], DH, 1))
                dp_ref[:, off + LANES * p:off + LANES * (p + 1)] = (blk * scale).astype(BF16)
        dp_ref[:, F_OFF:F_OFF + LANES] = df.astype(BF16)

    head_spec = pl.BlockSpec((HEADS, tr, LANES), lambda i: (0, nt - 1 - i, 0))
    return pl.pallas_call(
        body, name="fox_bwd_post", grid=(nt,),
        in_specs=[head_spec, head_spec, head_spec, pl.BlockSpec((tr, D_POOL), lambda i: (nt - 1 - i, 0)),
                  pl.BlockSpec((tr, LANES), lambda i: (nt - 1 - i, F_OFF // LANES)), _vec_spec(LANES)],
        out_specs=[pl.BlockSpec((tr, D_IN_PAD), lambda i: (nt - 1 - i, 0)), _vec_spec(LANES)],
        out_shape=[jax.ShapeDtypeStruct((S, D_IN_PAD), BF16), jax.ShapeDtypeStruct((1, LANES), F32)],
        scratch_shapes=[pltpu.VMEM((SUBLANES, LANES), F32)],
        compiler_params=_params(("arbitrary",)),
    )(dqa, dka, dva, du, proj, bf_pad)


POOL_HALO = 16


def _by_group(lane, a2, a4, a8, a16):
    return jnp.where(lane < 64, a2, jnp.where(lane < 128, a4, jnp.where(lane < 192, a8, a16)))


def _window_count(lane, t):
    return jnp.minimum(t + 1, _by_group(lane, 2, 4, 8, 16)).astype(F32)


def _pool_diff(u, halo, first, tile):
    n = TR + POOL_HALO
    ext = jnp.concatenate([jnp.where(first, 0.0, halo), u], axis=0)
    s2 = ext + pltpu.roll(ext, 1, 0)
    s4 = s2 + pltpu.roll(s2, 2, 0)
    s8 = s4 + pltpu.roll(s4, 4, 0)
    s16 = s8 + pltpu.roll(s8, 8, 0)
    lane = _lane_iota((n, D_POOL))
    win = _by_group(lane, s2, s4, s8, s16)[POOL_HALO:]
    lane = _lane_iota((TR, D_POOL))
    t = tile * TR + _row_iota((TR, D_POOL))
    return win / _window_count(lane, t) - u


def _prev_halo(rows, width, col):
    per = TR // rows
    return pl.BlockSpec((rows, width), lambda i: (jnp.maximum(i * per - 1, 0), col))


def _next_halo(rows, width, col):
    per = TR // rows
    return pl.BlockSpec((rows, width), lambda i: (jnp.minimum((i + 1) * per, S // rows - 1), col))


def _pool_fwd(proj, w_bd, ps, ycat):
    def body(u_ref, halo_ref, w_ref, ps_ref, ycat_ref, y_ref):
        i = pl.program_id(0)
        diff = _pool_diff(u_ref[...], halo_ref[...], i == 0, i)
        y_ref[...] = (_dot(diff.astype(BF16), w_ref[...]) * ps_ref[...]).astype(BF16)

    return pl.pallas_call(
        body, name="pool_fwd", grid=(S // TR,),
        in_specs=[_row_spec(TR, D_POOL), _prev_halo(POOL_HALO, D_POOL, 0),
                  pl.BlockSpec((D_POOL, D_POOL), lambda i: (0, 0)), _vec_spec(D_POOL), ANY],
        out_specs=_row_spec(TR, D_POOL), out_shape=jax.ShapeDtypeStruct((S, D), BF16), input_output_aliases={4: 0},
        compiler_params=_params(("parallel",)),
    )(proj, proj, w_bd, ps, ycat)


def _pool_bwd(proj, dycat, w_bd, w_bd_t, ps):
    nt = S // TR
    n = TR + POOL_HALO

    def body(u_ref, halo_ref, dy_ref, dyn_ref, w_ref, wt_ref, ps_ref, du_ref, dw_ref, dps_ref):
        i = pl.program_id(0)

        @pl.when(i == 0)
        def _():
            dw_ref[...] = jnp.zeros_like(dw_ref)
            dps_ref[...] = jnp.zeros_like(dps_ref)

        diff = _pool_diff(u_ref[...], halo_ref[...], i == 0, i).astype(BF16)
        dy = dy_ref[...]
        dps_ref[...] += jnp.sum(dy * _dot(diff, w_ref[...]), axis=0, keepdims=True)
        dy_ext = jnp.concatenate([dy, jnp.where(i == nt - 1, 0.0, dyn_ref[...])], axis=0)
        dmixed = (dy_ext * ps_ref[...]).astype(BF16)
        ddiff = _dot(dmixed, wt_ref[...])
        dw_ref[...] += _dot(diff, dmixed[:TR], TN)
        lane = _lane_iota((n, D_POOL))
        t = i * TR + _row_iota((n, D_POOL))
        e = ddiff / _window_count(lane, t)
        f2 = e + pltpu.roll(e, n - 1, 0)
        f4 = f2 + pltpu.roll(f2, n - 2, 0)
        f8 = f4 + pltpu.roll(f4, n - 4, 0)
        f16 = f8 + pltpu.roll(f8, n - 8, 0)
        du_ref[...] = _by_group(lane, f2, f4, f8, f16)[:TR] - ddiff[:TR]

    mat = pl.BlockSpec((D_POOL, D_POOL), lambda i: (0, 0))
    return pl.pallas_call(
        body, name="pool_bwd", grid=(nt,),
        in_specs=[_row_spec(TR, D_POOL), _prev_halo(POOL_HALO, D_POOL, 0), _row_spec(TR, D_POOL),
                  _next_halo(POOL_HALO, D_POOL, 0), mat, mat, _vec_spec(D_POOL)],
        out_specs=[_row_spec(TR, D_POOL), mat, _vec_spec(D_POOL)],
        out_shape=[jax.ShapeDtypeStruct((S, D_POOL), F32), jax.ShapeDtypeStruct((D_POOL, D_POOL), F32),
                   jax.ShapeDtypeStruct((1, D_POOL), F32)],
        compiler_params=_params(("arbitrary",)),
    )(proj, proj, dycat, dycat, w_bd, w_bd_t, ps)


def _xa_probs(q, k):
    s = _dot(q, k, NT) * (XA_DH ** -0.5)
    e = jnp.exp(s - jnp.max(s, axis=-1, keepdims=True))
    return e * (1.0 / jnp.sum(e, axis=-1, keepdims=True))


def _xattn_fwd(qx, kv):
    def body(q_ref, kv_ref, o_ref):
        for h in range(XA_HEADS):
            cols = slice(XA_DH * h, XA_DH * (h + 1))
            vcols = slice(D + XA_DH * h, D + XA_DH * (h + 1))
            p = _xa_probs(q_ref[:, cols], kv_ref[:, cols])
            o_ref[:, cols] = _dot(p.astype(BF16), kv_ref[:, vcols]).astype(BF16)

    return pl.pallas_call(
        body, name="xattn_fwd", grid=(S // TR,),
        in_specs=[_row_spec(TR, D), pl.BlockSpec((MEM, 2 * D), lambda i: (0, 0))],
        out_specs=_row_spec(TR, D), out_shape=jax.ShapeDtypeStruct((S, D), BF16),
        compiler_params=_params(("parallel",)),
    )(qx, kv)


def _xattn_bwd(qx, kv, dxo):
    def body(q_ref, kv_ref, do_ref, dq_ref, dkv_ref):
        i = pl.program_id(0)

        @pl.when(i == 0)
        def _():
            dkv_ref[...] = jnp.zeros_like(dkv_ref)

        for h in range(XA_HEADS):
            cols = slice(XA_DH * h, XA_DH * (h + 1))
            vcols = slice(D + XA_DH * h, D + XA_DH * (h + 1))
            q = q_ref[:, cols]
            k = kv_ref[:, cols]
            do = do_ref[:, cols]
            p = _xa_probs(q, k)
            dkv_ref[:, vcols] += _dot(p.astype(BF16), do, TN)
            dp = _dot(do, kv_ref[:, vcols], NT)
            ds = (p * (dp - jnp.sum(p * dp, axis=-1, keepdims=True)) * (XA_DH ** -0.5)).astype(BF16)
            dq_ref[:, cols] = _dot(ds, k).astype(BF16)
            dkv_ref[:, cols] += _dot(ds, q, TN)

    kv_spec = pl.BlockSpec((MEM, 2 * D), lambda i: (0, 0))
    return pl.pallas_call(
        body, name="xattn_bwd", grid=(S // TR,), in_specs=[_row_spec(TR, D), kv_spec, _row_spec(TR, D)],
        out_specs=[_row_spec(TR, D), kv_spec],
        out_shape=[jax.ShapeDtypeStruct((S, D), BF16), jax.ShapeDtypeStruct((MEM, 2 * D), F32)],
        compiler_params=_params(("arbitrary",)),
    )(qx, kv, dxo)


CONV_HALO = SUBLANES
TC = 512
GELU_K = 0.7978845608028654
GELU_C = 0.044715


def _conv3(ext, w, rows):
    h0 = ext[CONV_HALO:CONV_HALO + rows]
    h1 = pltpu.roll(ext, 1, 0)[CONV_HALO:CONV_HALO + rows]
    h2 = pltpu.roll(ext, 2, 0)[CONV_HALO:CONV_HALO + rows]
    return w[2:3] * h0 + w[1:2] * h1 + w[0:1] * h2 + w[3:4], (h2, h1, h0)


def _conv_specs():
    main = pl.BlockSpec((2, TR, TC), lambda j, i: (0, i, j))
    per = TR // CONV_HALO
    prev = pl.BlockSpec((2, CONV_HALO, TC), lambda j, i: (0, jnp.maximum(i * per - 1, 0), j))
    nxt = pl.BlockSpec((2, CONV_HALO, TC), lambda j, i: (0, jnp.minimum((i + 1) * per, S // CONV_HALO - 1), j))
    par = pl.BlockSpec((2, SUBLANES, TC), lambda j, i: (0, 0, j))
    return main, prev, nxt, par


def _convgate_fwd(hid, cwb):
    def body(h_ref, hp_ref, w_ref, act_ref):
        i = pl.program_id(1)
        c = []
        for g in range(2):
            ext = jnp.concatenate([jnp.where(i == 0, 0.0, hp_ref[g]), h_ref[g]], axis=0)
            c.append(_conv3(ext, w_ref[g], TR)[0])
        gate, up = c
        act_ref[...] = (jax.nn.gelu(gate, approximate=True) * up).astype(BF16)

    main, prev, _, par = _conv_specs()
    return pl.pallas_call(
        body, name="convgate_fwd", grid=(D_FF // TC, S // TR), in_specs=[main, prev, par],
        out_specs=pl.BlockSpec((TR, TC), lambda j, i: (i, j)), out_shape=jax.ShapeDtypeStruct((S, D_FF), BF16),
        compiler_params=_params(("parallel", "parallel")),
    )(hid, hid, cwb)


def _convgate_bwd(hid, dact, cwb):
    nr = S // TR
    n = TR + CONV_HALO

    def body(h_ref, hp_ref, hn_ref, da_ref, dan_ref, w_ref, dh_ref, dw_ref):
        i = pl.program_id(1)

        @pl.when(i == 0)
        def _():
            dw_ref[...] = jnp.zeros_like(dw_ref)

        da = jnp.concatenate([da_ref[...], jnp.where(i == nr - 1, 0.0, dan_ref[...])], axis=0)
        c, taps = [], []
        for g in range(2):
            ext = jnp.concatenate([jnp.where(i == 0, 0.0, hp_ref[g]), h_ref[g], hn_ref[g]], axis=0)
            cg, tg = _conv3(ext, w_ref[g], n)
            c.append(cg)
            taps.append(tg)
        gate, up = c
        th = jnp.tanh(GELU_K * (gate + GELU_C * gate * gate * gate))
        gelu = 0.5 * gate * (1.0 + th)
        dgelu = 0.5 * (1.0 + th) + 0.5 * gate * (1.0 - th * th) * GELU_K * (1.0 + 3.0 * GELU_C * gate * gate)
        for g, dc in enumerate((da * up * dgelu, da * gelu)):
            w = w_ref[g]
            dh = w[2:3] * dc[:TR] + w[1:2] * pltpu.roll(dc, n - 1, 0)[:TR] + w[0:1] * pltpu.roll(dc, n - 2, 0)[:TR]
            dh_ref[g] = dh.astype(BF16)
            dcm = dc[:TR]
            for r in range(3):
                dw_ref[g, r:r + 1, :] += jnp.sum(dcm * taps[g][r][:TR], axis=0, keepdims=True)
            dw_ref[g, 3:4, :] += jnp.sum(dcm, axis=0, keepdims=True)

    main, prev, nxt, par = _conv_specs()
    per = TR // CONV_HALO
    return pl.pallas_call(
        body, name="convgate_bwd", grid=(D_FF // TC, nr),
        in_specs=[main, prev, nxt, pl.BlockSpec((TR, TC), lambda j, i: (i, j)),
                  pl.BlockSpec((CONV_HALO, TC), lambda j, i: (jnp.minimum((i + 1) * per, S // CONV_HALO - 1), j)), par],
        out_specs=[main, par],
        out_shape=[jax.ShapeDtypeStruct((2, S, D_FF), BF16), jax.ShapeDtypeStruct((2, SUBLANES, D_FF), F32)],
        compiler_params=_params(("parallel", "arbitrary")),
    )(hid, hid, hid, dact, dact, cwb)


def _adam_update(w, g, m, v):
    m = ADAM_B1 * m + (1.0 - ADAM_B1) * g
    v = ADAM_B2 * v + (1.0 - ADAM_B2) * (g * g)
    m_hat = m / (1.0 - ADAM_B1 ** ADAM_STEP)
    v_hat = v / (1.0 - ADAM_B2 ** ADAM_STEP)
    return -ADAM_LR * (m_hat / (jnp.sqrt(v_hat) + ADAM_EPS) + ADAM_WD * w), m, v


def _row_tile(rows, cols, itemsize=4, target=TILE_BYTES):
    tr = SUBLANES
    while rows % (2 * tr) == 0 and 2 * tr * cols * itemsize <= target:
        tr *= 2
    assert rows % tr == 0, (rows, tr)
    return tr


def _adamw(name, w, g, m, v):
    rows, cols = w.shape
    tr = rows if rows * cols * 4 <= TILE_BYTES // 2 else _row_tile(rows, cols, target=TILE_BYTES // 2)

    def body(w_ref, g_ref, m_ref, v_ref, d_ref, nm_ref, nv_ref):
        d_ref[...], nm_ref[...], nv_ref[...] = _adam_update(w_ref[...], g_ref[...], m_ref[...], v_ref[...])

    spec = _row_spec(tr, cols)
    shape = jax.ShapeDtypeStruct((rows, cols), F32)
    return pl.pallas_call(
        body, name=name, grid=(rows // tr,), in_specs=[spec] * 4, out_specs=[spec] * 3, out_shape=[shape] * 3,
        compiler_params=_params(("parallel",)),
    )(w, g, m, v)


def _adamw_halves(name, core, w, g_mine, g_sibling, m, v):
    rows, cols = w.shape
    half = rows // 2
    tr = _row_tile(half, cols, target=TILE_BYTES // 2)
    per = half // tr

    def body(core_ref, w_ref, gm_ref, gs_ref, m_ref, v_ref, g_ref, d_ref, nm_ref, nv_ref):
        g = jnp.where(pl.program_id(0) // per == core_ref[0], gm_ref[...], gs_ref[...])
        g_ref[...] = g
        d_ref[...], nm_ref[...], nv_ref[...] = _adam_update(w_ref[...], g, m_ref[...], v_ref[...])

    spec = pl.BlockSpec((tr, cols), lambda i, core_ref: (i, 0))
    half_spec = pl.BlockSpec((tr, cols), lambda i, core_ref: (i % per, 0))
    shape = jax.ShapeDtypeStruct((rows, cols), F32)
    return pl.pallas_call(
        body, name=name, out_shape=[shape] * 4,
        grid_spec=pltpu.PrefetchScalarGridSpec(
            num_scalar_prefetch=1, grid=(rows // tr,), in_specs=[spec, half_spec, half_spec, spec, spec], out_specs=[spec] * 4),
        compiler_params=_params(("parallel",)),
    )(core, w, g_mine, g_sibling, m, v)


def _chip_sum(name, core, g, other):
    _, _, half, cols = g.shape
    tr = _row_tile(half, cols)

    def body(core_ref, g_ref, o_ref, p_ref):
        p_ref[...] = (g_ref[...] + o_ref[...]).astype(BF16)

    spec = pl.BlockSpec((None, tr, cols), lambda j, i, core_ref: (j, i, 0))
    return pl.pallas_call(
        body, name=name, out_shape=jax.ShapeDtypeStruct((N_CHIPS, half, cols), BF16),
        grid_spec=pltpu.PrefetchScalarGridSpec(
            num_scalar_prefetch=1, grid=(N_CHIPS, half // tr),
            in_specs=[pl.BlockSpec((None, None, tr, cols), lambda j, i, core_ref: (j, core_ref[0], i, 0)), spec],
            out_specs=spec),
        compiler_params=_params(("parallel", "parallel")),
    )(core, g, other)


def _mesh_sum(name, chip, received, own):
    _, half, cols = received.shape
    tr = _row_tile(half, cols, itemsize=2 * N_CHIPS)

    def body(chip_ref, r_ref, own_ref, o_ref):
        acc = None
        for j in range(N_CHIPS):
            term = jnp.where(chip_ref[0] == j, own_ref[...], r_ref[j]).astype(F32)
            acc = term if acc is None else acc + term
        o_ref[...] = acc

    return pl.pallas_call(
        body, name=name, out_shape=jax.ShapeDtypeStruct((half, cols), F32),
        grid_spec=pltpu.PrefetchScalarGridSpec(
            num_scalar_prefetch=1, grid=(half // tr,),
            in_specs=[pl.BlockSpec((N_CHIPS, tr, cols), lambda i, chip_ref: (0, i, 0)),
                      pl.BlockSpec((None, tr, cols), lambda i, chip_ref: (chip_ref[0], i, 0))],
            out_specs=pl.BlockSpec((tr, cols), lambda i, chip_ref: (i, 0))),
        compiler_params=_params(("parallel",)),
    )(chip, received, own)


CHIP_FLIPS = ((1, 0), (0, 1), (1, 1))


def _place():
    x, y, c = lax.axis_index("x"), lax.axis_index("y"), lax.axis_index("c")
    return x, y, c, 2 * x + y


def _remote(src, dst, sems_s, sems_r, k, dev):
    return pltpu.make_async_remote_copy(src_ref=src, dst_ref=dst, send_sem=sems_s.at[k], recv_sem=sems_r.at[k],
                                        device_id=dev, device_id_type=MESH)


def _comm_call(name, body, ins, out_shapes, n_remote):
    return pl.pallas_call(
        body, name=name, in_specs=[ANY] * len(ins), out_specs=[ANY] * len(out_shapes), out_shape=out_shapes,
        scratch_shapes=[pltpu.SemaphoreType.DMA((n_remote,)), pltpu.SemaphoreType.DMA((n_remote,))],
    )(*ins)


class _Exchange:
    def __init__(self, ins, out_shapes, n_sems, start, forward, finish):
        self.ins, self.out_shapes, self.n_sems = list(ins), list(out_shapes), n_sems
        self.start, self.forward, self.finish = start, forward, finish

    def scratch(self):
        return [pltpu.SemaphoreType.DMA((self.n_sems,)), pltpu.SemaphoreType.DMA((self.n_sems,))]

    def run(self, name):
        n = len(self.ins)

        def body(*refs):
            args = (refs[:n], refs[n:2 * n]) + tuple(refs[2 * n:])
            self.start(*args)
            self.forward(*args)
            self.finish(*args)

        return pl.pallas_call(
            body, name=name, in_specs=[ANY] * n, out_specs=[ANY] * n, out_shape=self.out_shapes, scratch_shapes=self.scratch(),
        )(*self.ins)


def _all_gather_weights(halved, whole):
    nh, nw = len(halved), len(whole)
    n_arr = nh + nw

    def copies(ins, outs, sems_s, sems_r):
        x, y, c, me = _place()
        sibling = (x, y, 1 - c)
        own = [_remote(ins[k], outs[k].at[me], sems_s, sems_r, k, sibling) for k in range(n_arr)]
        first, passed = [], []
        for k in range(n_arr):
            for f, (fx, fy) in enumerate(CHIP_FLIPS):
                src, dst = (ins[k].at[c], outs[k].at[me, c]) if k < nh else (ins[k], outs[k].at[me])
                first.append(_remote(src, dst, sems_s, sems_r, n_arr + 3 * k + f, (x ^ fx, y ^ fy, c)))
        for k in range(nh):
            for f, (fx, fy) in enumerate(CHIP_FLIPS):
                landed = outs[k].at[2 * (x ^ fx) + (y ^ fy), c]
                passed.append(_remote(landed, landed, sems_s, sems_r, 4 * n_arr + 3 * k + f, sibling))
        return own, first, passed

    def start(*refs):
        own, first, _ = copies(*refs)
        for cp in own + first:
            cp.start()

    def forward(*refs):
        _, first, passed = copies(*refs)
        for arrived, cp in zip(first, passed):
            arrived.wait_recv()
            cp.start()

    def finish(*refs):
        own, first, passed = copies(*refs)
        for cp in first[3 * nh:] + passed + own:
            cp.wait_recv()
        for cp in first + passed + own:
            cp.wait_send()

    shapes = [jax.ShapeDtypeStruct((N_CHIPS,) + a.shape, a.dtype) for a in list(halved) + list(whole)]
    return _Exchange(list(halved) + list(whole), shapes, 7 * nh + 4 * nw, start, forward, finish)


def _swap_halves(gs):
    n = len(gs)

    def copies(ins, outs, sems_s, sems_r):
        x, y, c, _ = _place()
        return [_remote(ins[k].at[:, 1 - c], outs[k], sems_s, sems_r, k, (x, y, 1 - c)) for k in range(n)]

    def start(*refs):
        for cp in copies(*refs):
            cp.start()

    def finish(*refs):
        for cp in copies(*refs):
            cp.wait()

    shapes = [jax.ShapeDtypeStruct((g.shape[0],) + g.shape[2:], g.dtype) for g in gs]
    return _Exchange(gs, shapes, n, start, _no_copies, finish)


def _scatter_chips(ps):
    n = len(ps)

    def copies(ins, outs, sems_s, sems_r):
        x, y, c, me = _place()
        return [_remote(ins[k].at[2 * (x ^ fx) + (y ^ fy)], outs[k].at[me], sems_s, sems_r, 3 * k + f, (x ^ fx, y ^ fy, c))
                for k in range(n) for f, (fx, fy) in enumerate(CHIP_FLIPS)]

    def start(*refs):
        for cp in copies(*refs):
            cp.start()

    def forward(*refs):
        pass

    def finish(*refs):
        for cp in copies(*refs):
            cp.wait()

    shapes = [jax.ShapeDtypeStruct(p.shape, p.dtype) for p in ps]
    return _Exchange(ps, shapes, 3 * n, start, forward, finish)


def _swap_reduced(rs):
    n = len(rs)

    def body(*refs):
        ins, outs = refs[:n], refs[n:2 * n]
        sems_s, sems_r = refs[2 * n:]
        x, y, c, _ = _place()
        copies = [_remote(ins[k], outs[k], sems_s, sems_r, k, (x, y, 1 - c)) for k in range(n)]
        for cp in copies:
            cp.start()
        for cp in copies:
            cp.wait()

    shapes = [jax.ShapeDtypeStruct(r.shape, r.dtype) for r in rs]
    return _comm_call("swap_reduced", body, rs, shapes, n)


def _all_reduce_small(buf):
    rows = buf.shape[0]
    n_dev = 8

    def body(in_ref, out_ref, gather, sems_s, sems_r):
        x, y, c, _ = _place()
        me = 4 * x + 2 * y + c
        gather[me] = in_ref[...]
        copies = []
        for o in range(1, n_dev):
            dev = (x ^ (o >> 2), y ^ ((o >> 1) & 1), c ^ (o & 1))
            copies.append(_remote(in_ref, gather.at[me], sems_s, sems_r, o - 1, dev))
        for cp in copies:
            cp.start()
        for cp in copies:
            cp.wait()
        acc = gather[0]
        for d in range(1, n_dev):
            acc = acc + gather[d]
        out_ref[...] = acc

    return pl.pallas_call(
        body, name="all_reduce_small", in_specs=[VMEM_SPEC], out_specs=VMEM_SPEC,
        out_shape=jax.ShapeDtypeStruct((rows, LANES), F32),
        scratch_shapes=[pltpu.VMEM((n_dev, rows, LANES), F32), pltpu.SemaphoreType.DMA((n_dev - 1,)),
                        pltpu.SemaphoreType.DMA((n_dev - 1,))],
        compiler_params=pltpu.CompilerParams(vmem_limit_bytes=VMEM_LIMIT),
    )(buf)


def _no_copies(*refs):
    pass


def _no_exchange():
    return _Exchange([], [], 1, _no_copies, _no_copies, _no_copies)


class _NoComm:
    def gather_rest(self, p):
        return _no_exchange()

    def weights_landed(self, p, landed):
        pass

    def swap_first(self, g):
        return _no_exchange()

    def first_swapped(self, landed):
        pass

    def scatter_early(self, g):
        return _no_exchange()

    def scatter_landed(self, landed):
        pass

    def scatter_late(self, g):
        return _no_exchange()

    def late_landed(self, landed):
        pass


def _local_step(x, mem, target, p, comm):
    h1 = _norm_fwd("norm_mix_pre", x, p["norm_mix_pre"])
    proj = _mm_nn("in_proj", h1, p["w_in"], F32, 1024, 896)
    qa, ka, va = _fox_prep(proj, p["bf_pad"])
    ycat, qab, landed = _fox_fwd(qa, ka, va, comm.gather_rest(p))
    comm.weights_landed(p, landed)
    ycat = _pool_fwd(proj, p["w_pool_bd"], p["pool_scale"], ycat)
    y1 = _mm_nn("mix_out", ycat, p["w_mix_out"], F32, 1024, 1024)
    x2, h2 = _resid_norm("resid_mix", x, y1, p["norm_mix_post"], p["norm_xa_pre"])
    qx = _mm_nn("xq", h2, p["w_xq"], BF16, 1024, 1024)
    mem_n = _norm_fwd("norm_mem", mem, p["norm_mem"])
    kv = _mm(
        "xkv", mem_n, p["w_xkv"], pl.BlockSpec((MEM, D), lambda i, j, k: (0, 0)),
        pl.BlockSpec((None, D, 512), lambda i, j, k: (j, 0, 0)), jax.ShapeDtypeStruct((MEM, 2 * D), BF16),
        pl.BlockSpec((MEM, 512), lambda i, j, k: (0, j)), (1, N_CHIPS, 1), NN, (MEM, 512))
    xo = _xattn_fwd(qx, kv)
    y2 = _mm_nn("xo", xo, p["w_xo"], F32, 1024, 1024)
    x3, h3 = _resid_norm("resid_xa", x2, y2, p["norm_xa_post"], p["norm_ffn_pre"])
    hid = _mm(
        "up_proj", h3, p["w_up"], pl.BlockSpec((1024, D), lambda i, j, k: (i, 0)),
        pl.BlockSpec((None, D, 1024), lambda i, j, k: (j // 2, 0, j % 2)), jax.ShapeDtypeStruct((2, S, D_FF), F32),
        pl.BlockSpec((None, 1024, 1024), lambda i, j, k: (j // 4, i, j % 4)), (S // 1024, 8, 1), NN, (1024, 1024))
    act = _convgate_fwd(hid, p["cwb"])
    y3 = _mm_nn("down_proj", act, p["w_down"], F32, 512, 512)

    g = {}
    dres, dy3, g["norm_ffn_post"], loss_cols = _loss_bwd(x3, y3, p["norm_ffn_post"], target)
    dact = _mm_nt("d_act", dy3, p["w_down"], F32, 1024, 1024)
    g["w_down"] = _mm_tn("dw_down", act, dy3, 512, 512)
    dhid, dcwb = _convgate_bwd(hid, dact, p["cwb"])
    g["w_up"] = _mm(
        "dw_up", h3, dhid, pl.BlockSpec((S, 512), lambda i, j, k: (0, i)),
        pl.BlockSpec((None, S, 512), lambda i, j, k: (j // 8, 0, j % 8)), jax.ShapeDtypeStruct((N_CHIPS, D, 2048), F32),
        pl.BlockSpec((None, 512, 512), lambda i, j, k: (j // 4, i, j % 4)), (2, 16, 1), TN, (512, 512))
    dh3, landed = _d_h3(dhid, p["w_up"], comm.swap_first(g))
    comm.first_swapped(landed)
    dres, dy2, g["norm_ffn_pre"], g["norm_xa_post"] = _mid_bwd("bwd_ffn_xa", dres, x3, p["norm_ffn_pre"], dh3, y2, p["norm_xa_post"])
    dxo = _mm_nt("d_xo", dy2, p["w_xo"], BF16, 1024, 1024)
    g["w_xo"] = _mm_tn("dw_xo", xo, dy2, 512, 512)
    dqx, dkv = _xattn_bwd(qx, kv, dxo)
    dkv = dkv.astype(BF16)
    dh2 = _mm_nt("d_h2", dqx, p["w_xq"], F32, 1024, 1024)
    g["w_xq"] = _mm_tn("dw_xq", h2, dqx, 512, 512)
    dmem_n = _mm(
        "d_mem", dkv, p["w_xkv"], pl.BlockSpec((MEM, 512), lambda i, j, k: (0, k)),
        pl.BlockSpec((None, D, 512), lambda i, j, k: (k, 0, 0)), jax.ShapeDtypeStruct((MEM, D), F32),
        pl.BlockSpec((MEM, D), lambda i, j, k: (0, 0)), (1, 1, N_CHIPS), NT, (MEM, D))
    g["w_xkv"] = _mm(
        "dw_xkv", mem_n, dkv, pl.BlockSpec((MEM, D), lambda i, j, k: (0, 0)),
        pl.BlockSpec((MEM, 512), lambda i, j, k: (0, j)), jax.ShapeDtypeStruct((N_CHIPS, D, 512), F32),
        pl.BlockSpec((None, D, 512), lambda i, j, k: (j, 0, 0)), (1, N_CHIPS, 1), TN, (D, 512))
    g["norm_mem"] = _gain_bwd("dg_mem", mem, p["norm_mem"], dmem_n)
    dres, dy1, g["norm_xa_pre"], g["norm_mix_post"] = _mid_bwd("bwd_xa_mix", dres, x2, p["norm_xa_pre"], dh2, y1, p["norm_mix_post"])
    dy_pool, doa = _d_ycat(dy1, p["w_mix_out"], ycat)
    g["w_mix_out"] = _mm_tn("dw_mix_out", ycat, dy1, 512, 512)
    dqa, dka, dva, landed = _fox_bwd(qab, doa, ka, va, comm.scatter_early(g))
    comm.scatter_landed(landed)
    du, g["w_pool_full"], g["pool_scale"] = _pool_bwd(proj, dy_pool, p["w_pool_bd"], p["w_pool_bd_t"], p["pool_scale"])
    dproj, g["bf_pad"] = _fox_bwd_post(dqa, dka, dva, du, proj, p["bf_pad"])
    g["w_in"] = _mm_tn("dw_in", h1, dproj, 512, 896)
    dh1, landed = _mm_nt("d_h1", dproj, p["w_in"], F32, 1024, 1024, comm.scatter_late(g))
    comm.late_landed(landed)
    grad_x, g["norm_mix_pre"] = _first_bwd(dres, x, p["norm_mix_pre"], dh1)
    g["cwb"] = dcwb
    return grad_x, g, loss_cols


BIG = ("w_in", "w_mix_out", "w_xq", "w_xkv", "w_xo", "w_up", "w_down")
ROW_SHARDED = ("w_mix_out", "w_xq", "w_xo", "w_down")
SMALL = ("norm_mix_pre", "norm_mix_post", "b_forget", "w_pool", "pool_scale", "norm_mem", "norm_xa_pre", "norm_xa_post",
         "norm_ffn_pre", "norm_ffn_post", "conv_b")
ORDER = ("norm_mix_pre", "norm_mix_post", "w_in", "b_forget", "w_pool", "pool_scale", "w_mix_out", "norm_mem", "norm_xa_pre",
         "norm_xa_post", "w_xq", "w_xkv", "w_xo", "norm_ffn_pre", "norm_ffn_post", "w_up", "conv_w", "conv_b", "w_down")
SLOT = SUBLANES * LANES


def _pack(parts):
    rows, offs, off = [], [], 0
    for a in parts:
        flat = a.reshape(-1).astype(F32)
        n = -(-flat.shape[0] // SLOT) * SLOT
        rows.append(jnp.pad(flat, (0, n - flat.shape[0])).reshape(n // LANES, LANES))
        offs.append(off)
        off += n // LANES
    return jnp.concatenate(rows, axis=0), offs


def _unpack(buf, off, like):
    n = like.size
    rows = -(-n // LANES)
    return buf[off:off + rows].reshape(-1)[:n].reshape(like.shape)


FIRST = ("w_in", "w_mix_out")
REST = ("w_xq", "w_xkv", "w_xo", "w_up", "w_down")


def _first_params(w, full):
    w_in_full = jnp.pad(jnp.concatenate(list(full["w_in"]), axis=1), ((0, 0), (0, D_IN_PAD - D_IN)))
    w_pool_bd = jnp.zeros((D_POOL, D_POOL), F32)
    for gi in range(4):
        w_pool_bd = w_pool_bd.at[64 * gi:64 * (gi + 1), 64 * gi:64 * (gi + 1)].set(w["w_pool"][0, gi])
    p = {n: w[n] for n in ("norm_mix_pre", "norm_mix_post", "norm_mem", "norm_xa_pre", "norm_xa_post", "norm_ffn_pre",
                           "norm_ffn_post")}
    p.update(
        w_in=w_in_full, bf_pad=jnp.pad(w["b_forget"], ((0, 0), (0, LANES - HEADS))),
        w_pool_bd=w_pool_bd.astype(BF16), w_pool_bd_t=w_pool_bd.T.astype(BF16), pool_scale=w["pool_scale"].reshape(1, D_POOL),
        w_mix_out=full["w_mix_out"].reshape(D, D))
    return p


def _rest_params(w, full, conv_w_full):
    cw2 = conv_w_full.reshape(3, 2, D_FF).transpose(1, 0, 2)
    cwb = jnp.concatenate([cw2, w["conv_b"].reshape(1, 2, D_FF).transpose(1, 0, 2), jnp.zeros((2, 4, D_FF), F32)], axis=1)
    return dict(w_xq=full["w_xq"].reshape(D, D), w_xkv=full["w_xkv"], w_xo=full["w_xo"].reshape(D, D), w_up=full["w_up"],
                cwb=cwb, w_down=full["w_down"].reshape(D_FF, D))


def _whole_params(w, full, conv_w_full):
    p = _first_params(w, full)
    p.update(_rest_params(w, full, conv_w_full))
    return p


def _halved(a):
    return a.reshape(a.shape[:-2] + (2, a.shape[-2] // 2, a.shape[-1]))


class _StepComm:
    def __init__(self, w, shard2d, conv_w, core_id):
        self.w, self.shard2d, self.conv_w, self.core_id = w, shard2d, conv_w, core_id
        self.early = ("w_mix_out",) + REST
        self.partial = self.received = None

    def gather_rest(self, p):
        return _all_gather_weights([_halved(self.shard2d[n].astype(BF16)) for n in REST], [self.conv_w.reshape(3, -1)])

    def weights_landed(self, p, landed):
        full = {n: a.reshape((N_CHIPS,) + self.shard2d[n].shape) for n, a in zip(REST, landed)}
        conv_w_full = jnp.transpose(landed[-1], (1, 0, 2)).reshape(3, 2 * D_FF)
        p.update(_rest_params(self.w, full, conv_w_full))

    def _view(self, g, n):
        return _halved(g[n].reshape((N_CHIPS,) + self.shard2d[n].shape))

    def swap_first(self, g):
        self.first = ("w_up", "w_down")
        return _swap_halves([self._view(g, n) for n in self.first])

    def first_swapped(self, landed):
        self.from_sibling = dict(zip(self.first, landed))

    def scatter_early(self, g):
        others = [n for n in self.early if n not in self.first]
        self.from_sibling.update(zip(others, _swap_halves([self._view(g, n) for n in others]).run("swap_halves")))
        self.partial = [_chip_sum("chip_sum_" + n, self.core_id, self._view(g, n), self.from_sibling[n]) for n in self.early]
        return _scatter_chips(self.partial)

    def scatter_landed(self, landed):
        self.received = list(landed)

    def scatter_late(self, g):
        gw_in = g["w_in"][:, :D_IN]
        cols = D_IN // N_CHIPS
        view = _halved(jnp.stack([gw_in[:, cols * j:cols * (j + 1)] for j in range(N_CHIPS)]))
        self.partial_in = _chip_sum("chip_sum_w_in", self.core_id, view, _swap_halves([view]).run("swap_halves_w_in")[0])
        return _scatter_chips([self.partial_in])

    def late_landed(self, landed):
        self.received_in = landed[0]


def kernel(x, mem, norm_mix_pre, norm_mix_post, w_in, b_forget, w_pool, pool_scale, w_mix_out, norm_mem, norm_xa_pre, norm_xa_post, w_xq, w_xkv, w_xo, norm_ffn_pre, norm_ffn_post, w_up, conv_w, conv_b, w_down, loss_target, m_norm_mix_pre, m_norm_mix_post, m_w_in, m_b_forget, m_w_pool, m_pool_scale, m_w_mix_out, m_norm_mem, m_norm_xa_pre, m_norm_xa_post, m_w_xq, m_w_xkv, m_w_xo, m_norm_ffn_pre, m_norm_ffn_post, m_w_up, m_conv_w, m_conv_b, m_w_down, v_norm_mix_pre, v_norm_mix_post, v_w_in, v_b_forget, v_w_pool, v_pool_scale, v_w_mix_out, v_norm_mem, v_norm_xa_pre, v_norm_xa_post, v_w_xq, v_w_xkv, v_w_xo, v_norm_ffn_pre, v_norm_ffn_post, v_w_up, v_conv_w, v_conv_b, v_w_down):
    w = dict(norm_mix_pre=norm_mix_pre, norm_mix_post=norm_mix_post, w_in=w_in, b_forget=b_forget, w_pool=w_pool,
             pool_scale=pool_scale, w_mix_out=w_mix_out, norm_mem=norm_mem, norm_xa_pre=norm_xa_pre, norm_xa_post=norm_xa_post,
             w_xq=w_xq, w_xkv=w_xkv, w_xo=w_xo, norm_ffn_pre=norm_ffn_pre, norm_ffn_post=norm_ffn_post, w_up=w_up,
             conv_w=conv_w, conv_b=conv_b, w_down=w_down)
    m = dict(norm_mix_pre=m_norm_mix_pre, norm_mix_post=m_norm_mix_post, w_in=m_w_in, b_forget=m_b_forget, w_pool=m_w_pool,
             pool_scale=m_pool_scale, w_mix_out=m_w_mix_out, norm_mem=m_norm_mem, norm_xa_pre=m_norm_xa_pre,
             norm_xa_post=m_norm_xa_post, w_xq=m_w_xq, w_xkv=m_w_xkv, w_xo=m_w_xo, norm_ffn_pre=m_norm_ffn_pre,
             norm_ffn_post=m_norm_ffn_post, w_up=m_w_up, conv_w=m_conv_w, conv_b=m_conv_b, w_down=m_w_down)
    v = dict(norm_mix_pre=v_norm_mix_pre, norm_mix_post=v_norm_mix_post, w_in=v_w_in, b_forget=v_b_forget, w_pool=v_w_pool,
             pool_scale=v_pool_scale, w_mix_out=v_w_mix_out, norm_mem=v_norm_mem, norm_xa_pre=v_norm_xa_pre,
             norm_xa_post=v_norm_xa_post, w_xq=v_w_xq, w_xkv=v_w_xkv, w_xo=v_w_xo, norm_ffn_pre=v_norm_ffn_pre,
             norm_ffn_post=v_norm_ffn_post, w_up=v_w_up, conv_w=v_conv_w, conv_b=v_conv_b, w_down=v_w_down)
    chip = 2 * lax.axis_index("x") + lax.axis_index("y")

    core_id = lax.axis_index("c").astype(jnp.int32).reshape(1)
    chip_id = chip.astype(jnp.int32).reshape(1)

    shard2d = {n: w[n][0] for n in BIG}
    gathered = _all_gather_weights([_halved(shard2d[n].astype(BF16)) for n in FIRST], []).run("all_gather_first")
    p = _first_params(w, {n: a.reshape((N_CHIPS,) + shard2d[n].shape) for n, a in zip(FIRST, gathered)})

    comm = _StepComm(w, shard2d, conv_w, core_id)
    grad_x, g, loss_cols = _local_step(x[0], mem[0], loss_target[0], p, comm)

    names = ("w_in",) + comm.early
    reduced = [_mesh_sum("mesh_sum_" + n, chip_id, r, own)
               for n, r, own in zip(names, [comm.received_in] + comm.received, [comm.partial_in] + comm.partial)]
    reduced_sibling = _swap_reduced(reduced)
    grads = {}

    gw_pool = jnp.stack([g["w_pool_full"][64 * gi:64 * (gi + 1), 64 * gi:64 * (gi + 1)] for gi in range(4)])
    dcwb = g["cwb"]
    g_conv_w = dcwb[:, 0:3, :].transpose(1, 0, 2).reshape(3, 2 * D_FF)
    g_conv_b = dcwb[:, 3, :].reshape(2 * D_FF)
    small_g = dict(norm_mix_pre=g["norm_mix_pre"], norm_mix_post=g["norm_mix_post"], b_forget=g["bf_pad"][:, :HEADS],
                   w_pool=gw_pool, pool_scale=g["pool_scale"], norm_mem=g["norm_mem"], norm_xa_pre=g["norm_xa_pre"],
                   norm_xa_post=g["norm_xa_post"], norm_ffn_pre=g["norm_ffn_pre"], norm_ffn_post=g["norm_ffn_post"],
                   conv_b=g_conv_b)
    buf, offs = _pack([small_g[n] for n in SMALL] + [g_conv_w, loss_cols])
    buf = _all_reduce_small(buf)
    for n, off in zip(SMALL, offs):
        grads[n] = _unpack(buf, off, w[n])
    g_conv_w = _unpack(buf, offs[len(SMALL)], g_conv_w)
    grads["conv_w"] = lax.dynamic_slice_in_dim(g_conv_w, chip * (2 * D_FF // N_CHIPS), 2 * D_FF // N_CHIPS, axis=1).reshape(conv_w.shape)
    loss = jnp.sum(_unpack(buf, offs[len(SMALL) + 1], loss_cols))

    delta, new_m, new_v = {}, {}, {}
    for n, g_mine, g_sibling in zip(names, reduced, reduced_sibling):
        gn, d, nm, nv = _adamw_halves("adamw_" + n, core_id, shard2d[n], g_mine, g_sibling, m[n][0], v[n][0])
        grads[n], delta[n], new_m[n], new_v[n] = gn[None], d[None], nm[None], nv[None]
    small_names = SMALL + ("conv_w",)
    packed = [_pack([d[n] for n in small_names])[0] for d in (w, grads, m, v)]
    offs = _pack([w[n] for n in small_names])[1]
    d, nm, nv = _adamw("adamw_small", *packed)
    for n, off in zip(small_names, offs):
        delta[n], new_m[n], new_v[n] = _unpack(d, off, w[n]), _unpack(nm, off, w[n]), _unpack(nv, off, w[n])

    return (loss, grad_x[None], *[grads[n] for n in ORDER], *[delta[n] for n in ORDER], *[new_m[n] for n in ORDER],
            *[new_v[n] for n in ORDER])
```

```python
import functools

import jax
import jax.numpy as jnp
from jax import lax
from jax.experimental import pallas as pl
from jax.experimental.pallas import tpu as pltpu

F32 = jnp.float32
BF16 = jnp.bfloat16
MESH = pl.DeviceIdType.MESH
ANY = pl.BlockSpec(memory_space=pl.ANY)
VMEM_SPEC = pl.BlockSpec(memory_space=pltpu.VMEM)

S = 4096
D = 1024
MEM = 256
D_POOL = 256
HEADS = 12
DH = 64
D_FOX = HEADS * DH
D_IN = D_POOL + 3 * D_FOX + HEADS
F_OFF = D_POOL + 3 * D_FOX
Q_OFF, K_OFF, V_OFF = D_POOL, D_POOL + D_FOX, D_POOL + 2 * D_FOX
XA_HEADS = 4
XA_DH = 256
D_FF = 4096
EPS = 1e-6
N_CHIPS = 4
ADAM_LR, ADAM_B1, ADAM_B2, ADAM_EPS, ADAM_WD, ADAM_STEP = 0.001, 0.9, 0.999, 1e-08, 0.01, 10

LANES = 128
SUBLANES = 8
D_IN_PAD = 21 * LANES
TR = 512
TILE_BYTES = 2 * 1024 * 1024
NEG = -1e30
VMEM_LIMIT = 52 * 1024 * 1024

NN = (((1,), (0,)), ((), ()))
NT = (((1,), (1,)), ((), ()))
TN = (((0,), (0,)), ((), ()))


def _dot(a, b, dims=NN):
    return lax.dot_general(a, b, dims, preferred_element_type=F32)


def _params(sem):
    return pltpu.CompilerParams(dimension_semantics=sem, vmem_limit_bytes=VMEM_LIMIT)


def _split3(x):
    hi = x.astype(BF16)
    r = x - hi.astype(F32)
    mid = r.astype(BF16)
    lo = (r - mid.astype(F32)).astype(BF16)
    return hi, mid, lo


def _split3_f32(x):
    hi = x.astype(BF16).astype(F32)
    r = x - hi
    mid = r.astype(BF16).astype(F32)
    return hi, mid, r - mid


def _lane_iota(shape):
    return lax.broadcasted_iota(jnp.int32, shape, len(shape) - 1)


def _row_iota(shape):
    return lax.broadcasted_iota(jnp.int32, shape, len(shape) - 2)


def _mm(name, a, b, a_spec, b_spec, out_shape, out_spec, grid, dims, acc_shape, ex=None):
    nk = grid[2]
    if ex is not None:
        return _mm_hosting(name, a, b, a_spec, b_spec, out_shape, out_spec, grid, dims, ex)

    def body(a_ref, b_ref, o_ref, *scr):
        p = _dot(a_ref[...], b_ref[...], dims)
        if nk == 1:
            o_ref[...] = p.astype(o_ref.dtype)
        else:
            acc = scr[0]
            k = pl.program_id(2)

            @pl.when(k == 0)
            def _():
                acc[...] = p

            @pl.when(k > 0)
            def _():
                acc[...] += p

            @pl.when(k == nk - 1)
            def _():
                o_ref[...] = acc[...].astype(o_ref.dtype)

    return pl.pallas_call(
        body, name=name, grid=grid, in_specs=[a_spec, b_spec], out_specs=out_spec, out_shape=out_shape,
        scratch_shapes=[pltpu.VMEM(acc_shape, F32)] if nk > 1 else [],
        compiler_params=_params(("parallel", "parallel", "arbitrary")),
    )(a, b)


def _mm_hosting(name, a, b, a_spec, b_spec, out_shape, out_spec, grid, dims, ex):
    assert grid[2] == 1
    n = len(ex.ins)

    def body(*refs):
        i, j = pl.program_id(0), pl.program_id(1)
        first = (i == 0) & (j == 0)
        (a_ref, b_ref), (o_ref,), _, begin, end = _hosted(
            ex, refs, 2, 1, first, first, (i == grid[0] - 1) & (j == grid[1] - 1))
        begin()
        o_ref[...] = _dot(a_ref[...], b_ref[...], dims).astype(o_ref.dtype)
        end()

    res = pl.pallas_call(
        body, name=name, grid=grid, in_specs=[a_spec, b_spec] + [ANY] * n, out_specs=[out_spec] + [ANY] * n,
        out_shape=[out_shape] + ex.out_shapes, scratch_shapes=ex.scratch(),
        compiler_params=_params(("arbitrary", "arbitrary", "arbitrary")),
    )(a, b, *ex.ins)
    return res[0], res[1:]


def _mm_nn(name, a, b, out_dtype, tm, tn):
    m, k = a.shape
    n = b.shape[1]
    return _mm(name, a, b, pl.BlockSpec((tm, k), lambda i, j, kk: (i, 0)), pl.BlockSpec((k, tn), lambda i, j, kk: (0, j)),
               jax.ShapeDtypeStruct((m, n), out_dtype), pl.BlockSpec((tm, tn), lambda i, j, kk: (i, j)),
               (m // tm, n // tn, 1), NN, (tm, tn))


def _mm_nt(name, a, b, out_dtype, tm, tn, ex=None):
    m, k = a.shape
    n = b.shape[0]
    return _mm(name, a, b, pl.BlockSpec((tm, k), lambda i, j, kk: (i, 0)), pl.BlockSpec((tn, k), lambda i, j, kk: (j, 0)),
               jax.ShapeDtypeStruct((m, n), out_dtype), pl.BlockSpec((tm, tn), lambda i, j, kk: (i, j)),
               (m // tm, n // tn, 1), NT, (tm, tn), ex)


def _mm_tn(name, a, b, tka, tn, ex=None):
    t, ka = a.shape
    n = b.shape[1]
    return _mm(name, a, b, pl.BlockSpec((t, tka), lambda i, j, kk: (0, i)), pl.BlockSpec((t, tn), lambda i, j, kk: (0, j)),
               jax.ShapeDtypeStruct((ka, n), F32), pl.BlockSpec((tka, tn), lambda i, j, kk: (i, j)),
               (ka // tka, n // tn, 1), TN, (tka, tn), ex)


def _d_h3(dhid, w_up, ex):
    tm = tn = 512
    shard = 2 * D_FF // N_CHIPS
    per_plane = D_FF // shard
    grid = (S // tm, D // tn)
    n = len(ex.ins)

    def body(*refs):
        i, j = pl.program_id(0), pl.program_id(1)
        first = (i == 0) & (j == 0)
        (a_ref, b_ref), (o_ref,), _, begin, end = _hosted(ex, refs, 2, 1, first, first, (i == grid[0] - 1) & (j == grid[1] - 1))
        begin()
        acc = None
        for k in range(N_CHIPS):
            cols = slice(shard * (k % per_plane), shard * (k % per_plane + 1))
            part = _dot(a_ref[k // per_plane, :, cols], b_ref[k], NT)
            acc = part if acc is None else acc + part
        o_ref[...] = acc
        end()

    res = pl.pallas_call(
        body, name="d_h3", grid=grid,
        in_specs=[pl.BlockSpec((2, tm, D_FF), lambda i, j: (0, i, 0)),
                  pl.BlockSpec((N_CHIPS, tn, shard), lambda i, j: (0, j, 0))] + [ANY] * n,
        out_specs=[pl.BlockSpec((tm, tn), lambda i, j: (i, j))] + [ANY] * n,
        out_shape=[jax.ShapeDtypeStruct((S, D), F32)] + ex.out_shapes, scratch_shapes=ex.scratch(),
        compiler_params=_params(("arbitrary", "arbitrary")),
    )(dhid, w_up, *ex.ins)
    return res[0], res[1:]


def _rms(x, g):
    r = lax.rsqrt(jnp.mean(x * x, axis=-1, keepdims=True) + EPS)
    return x * r * g


def _rms_bwd(x, g, dy):
    r = lax.rsqrt(jnp.mean(x * x, axis=-1, keepdims=True) + EPS)
    xh = x * r
    dxh = dy * g
    dx = r * (dxh - xh * jnp.mean(dxh * xh, axis=-1, keepdims=True))
    return dx, jnp.sum(dy * xh, axis=0, keepdims=True)


def _row_spec(tr, width):
    return pl.BlockSpec((tr, width), lambda i: (i, 0))


def _vec_spec(width):
    return pl.BlockSpec((1, width), lambda i: (0, 0))


def _norm_fwd(name, x, g):
    rows, width = x.shape
    tr = min(TR, rows)

    def body(x_ref, g_ref, h_ref):
        h_ref[...] = _rms(x_ref[...], g_ref[...]).astype(BF16)

    return pl.pallas_call(
        body, name=name, grid=(rows // tr,), in_specs=[_row_spec(tr, width), _vec_spec(width)],
        out_specs=_row_spec(tr, width), out_shape=jax.ShapeDtypeStruct((rows, width), BF16),
        compiler_params=_params(("parallel",)),
    )(x, g)


def _resid_norm(name, xp, y, g_post, g_pre):
    def body(xp_ref, y_ref, gpost_ref, gpre_ref, xn_ref, h_ref):
        xn = xp_ref[...] + _rms(y_ref[...], gpost_ref[...])
        xn_ref[...] = xn
        h_ref[...] = _rms(xn, gpre_ref[...]).astype(BF16)

    return pl.pallas_call(
        body, name=name, grid=(S // TR,), in_specs=[_row_spec(TR, D), _row_spec(TR, D), _vec_spec(D), _vec_spec(D)],
        out_specs=[_row_spec(TR, D), _row_spec(TR, D)],
        out_shape=[jax.ShapeDtypeStruct((S, D), F32), jax.ShapeDtypeStruct((S, D), BF16)],
        compiler_params=_params(("parallel",)),
    )(xp, y, g_post, g_pre)


def _loss_bwd(x3, y3, g_post, target):
    def body(x_ref, y_ref, g_ref, t_ref, dres_ref, dy_ref, dg_ref, loss_ref):
        i = pl.program_id(0)

        @pl.when(i == 0)
        def _():
            dg_ref[...] = jnp.zeros_like(dg_ref)
            loss_ref[...] = jnp.zeros_like(loss_ref)

        y = y_ref[...]
        g = g_ref[...]
        e = x_ref[...] + _rms(y, g) - t_ref[...]
        loss_ref[...] += jnp.sum(e * e, axis=0, keepdims=True) * (0.5 / D)
        dres = e * (1.0 / D)
        dres_ref[...] = dres
        dy, dg = _rms_bwd(y, g, dres)
        dy_ref[...] = dy.astype(BF16)
        dg_ref[...] += dg

    return pl.pallas_call(
        body, name="loss_bwd", grid=(S // TR,),
        in_specs=[_row_spec(TR, D), _row_spec(TR, D), _vec_spec(D), _row_spec(TR, D)],
        out_specs=[_row_spec(TR, D), _row_spec(TR, D), _vec_spec(D), _vec_spec(D)],
        out_shape=[jax.ShapeDtypeStruct((S, D), F32), jax.ShapeDtypeStruct((S, D), BF16),
                   jax.ShapeDtypeStruct((1, D), F32), jax.ShapeDtypeStruct((1, D), F32)],
        compiler_params=_params(("arbitrary",)),
    )(x3, y3, g_post, target)


def _mid_bwd(name, dres, xcur, g_pre, dh, yprev, g_post):
    def body(dres_ref, x_ref, gpre_ref, dh_ref, y_ref, gpost_ref, dx_ref, dy_ref, dgpre_ref, dgpost_ref):
        i = pl.program_id(0)

        @pl.when(i == 0)
        def _():
            dgpre_ref[...] = jnp.zeros_like(dgpre_ref)
            dgpost_ref[...] = jnp.zeros_like(dgpost_ref)

        dxn, dgpre = _rms_bwd(x_ref[...], gpre_ref[...], dh_ref[...])
        dx = dres_ref[...] + dxn
        dx_ref[...] = dx
        dy, dgpost = _rms_bwd(y_ref[...], gpost_ref[...], dx)
        dy_ref[...] = dy.astype(BF16)
        dgpre_ref[...] += dgpre
        dgpost_ref[...] += dgpost

    return pl.pallas_call(
        body, name=name, grid=(S // TR,),
        in_specs=[_row_spec(TR, D), _row_spec(TR, D), _vec_spec(D), _row_spec(TR, D), _row_spec(TR, D), _vec_spec(D)],
        out_specs=[_row_spec(TR, D), _row_spec(TR, D), _vec_spec(D), _vec_spec(D)],
        out_shape=[jax.ShapeDtypeStruct((S, D), F32), jax.ShapeDtypeStruct((S, D), BF16),
                   jax.ShapeDtypeStruct((1, D), F32), jax.ShapeDtypeStruct((1, D), F32)],
        compiler_params=_params(("arbitrary",)),
    )(dres, xcur, g_pre, dh, yprev, g_post)


def _first_bwd(dres, x, g, dh):
    def body(dres_ref, x_ref, g_ref, dh_ref, dx_ref, dg_ref):
        i = pl.program_id(0)

        @pl.when(i == 0)
        def _():
            dg_ref[...] = jnp.zeros_like(dg_ref)

        dxn, dg = _rms_bwd(x_ref[...], g_ref[...], dh_ref[...])
        dx_ref[...] = dres_ref[...] + dxn
        dg_ref[...] += dg

    return pl.pallas_call(
        body, name="first_bwd", grid=(S // TR,),
        in_specs=[_row_spec(TR, D), _row_spec(TR, D), _vec_spec(D), _row_spec(TR, D)],
        out_specs=[_row_spec(TR, D), _vec_spec(D)],
        out_shape=[jax.ShapeDtypeStruct((S, D), F32), jax.ShapeDtypeStruct((1, D), F32)],
        compiler_params=_params(("arbitrary",)),
    )(dres, x, g, dh)


def _gain_bwd(name, x, g, dy):
    rows, width = x.shape

    def body(x_ref, g_ref, dy_ref, dg_ref):
        _, dg = _rms_bwd(x_ref[...], g_ref[...], dy_ref[...])
        dg_ref[...] = dg

    return pl.pallas_call(
        body, name=name, grid=(1,), in_specs=[_row_spec(rows, width), _vec_spec(width), _row_spec(rows, width)],
        out_specs=_vec_spec(width), out_shape=jax.ShapeDtypeStruct((1, width), F32),
        compiler_params=_params(("arbitrary",)),
    )(x, g, dy)


CUM_Q = DH
CUM_K = DH + 3
LSE_Q = DH + 6
DEN_V = DH
DELTA = DH + 1
PREP_TR = 256
FOX_FWD_BLOCK = 1024
FOX_BWD_BLOCK = 512


def _head_block(ref, off, h):
    start = off + DH * h
    base = (start // LANES) * LANES
    blk = ref[:, base:base + LANES]
    return pltpu.roll(blk, DH, 1) if start % LANES else blk


def _cumsum_rows(x, tri, carry):
    hi, mid, lo = _split3(x)
    return _dot(tri, hi) + _dot(tri, mid) + _dot(tri, lo) + carry


def _fox_prep(proj, bf_pad):
    tr = PREP_TR

    def body(proj_ref, bf_ref, qa_ref, ka_ref, va_ref, carry_ref):
        i = pl.program_id(0)

        @pl.when(i == 0)
        def _():
            carry_ref[...] = jnp.zeros_like(carry_ref)

        lane = _lane_iota((tr, LANES))
        z = proj_ref[:, F_OFF:F_OFF + LANES] + bf_ref[...]
        log_f = jnp.minimum(z, 0.0) - jnp.log(1.0 + jnp.exp(-jnp.abs(z)))
        log_f = jnp.where(lane < HEADS, log_f, 0.0)
        tri = jnp.where(_row_iota((tr, tr)) >= _lane_iota((tr, tr)), 1.0, 0.0).astype(BF16)
        cum = _cumsum_rows(log_f, tri, carry_ref[0:1, :])
        carry_ref[0:1, :] = cum[tr - 1:tr, :]

        ones_q = jnp.where((lane >= CUM_K) & (lane < CUM_K + 3), 1.0, 0.0)
        ones_k = jnp.where(((lane >= CUM_Q) & (lane < CUM_Q + 3)) | ((lane >= LSE_Q) & (lane < LSE_Q + 3)), 1.0, 0.0)
        aug_v = jnp.where(lane == DEN_V, 1.0, jnp.where((lane >= DELTA) & (lane < DELTA + 3), -1.0, 0.0))
        for h in range(HEADS):
            c_hi, c_mid, c_lo = _split3_f32(cum[:, h:h + 1])
            aug_q = jnp.where(lane == CUM_Q, c_hi, jnp.where(lane == CUM_Q + 1, c_mid, jnp.where(lane == CUM_Q + 2, c_lo, ones_q)))
            aug_k = jnp.where(lane == CUM_K, -c_hi, jnp.where(lane == CUM_K + 1, -c_mid, jnp.where(lane == CUM_K + 2, -c_lo, ones_k)))
            qa_ref[h] = jnp.where(lane < DH, _head_block(proj_ref, Q_OFF, h) * (DH ** -0.5), aug_q).astype(BF16)
            ka_ref[h] = jnp.where(lane < DH, _head_block(proj_ref, K_OFF, h), aug_k).astype(BF16)
            va_ref[h] = jnp.where(lane < DH, _head_block(proj_ref, V_OFF, h), aug_v).astype(BF16)

    head_spec = pl.BlockSpec((HEADS, tr, LANES), lambda i: (0, i, 0))
    head_shape = jax.ShapeDtypeStruct((HEADS, S, LANES), BF16)
    return pl.pallas_call(
        body, name="fox_prep", grid=(S // tr,), in_specs=[_row_spec(tr, D_IN_PAD), _vec_spec(LANES)],
        out_specs=[head_spec] * 3, out_shape=[head_shape] * 3, scratch_shapes=[pltpu.VMEM((SUBLANES, LANES), F32)],
        compiler_params=_params(("arbitrary",)),
    )(proj, bf_pad)


def _hosted(ex, refs, n_blocked_in, n_blocked_out, first, forward_at, last):
    n = len(ex.ins)
    own_in = refs[:n_blocked_in]
    ex_in = refs[n_blocked_in:n_blocked_in + n]
    own_out = refs[n_blocked_in + n:n_blocked_in + n + n_blocked_out]
    ex_out = refs[n_blocked_in + n + n_blocked_out:n_blocked_in + 2 * n + n_blocked_out]
    rest = refs[n_blocked_in + 2 * n + n_blocked_out:]
    args = (ex_in, ex_out, rest[-2], rest[-1])

    def begin():
        @pl.when(first)
        def _():
            ex.start(*args)

        @pl.when(forward_at)
        def _():
            ex.forward(*args)

    def end():
        @pl.when(last)
        def _():
            ex.finish(*args)

    return own_in, own_out, rest[:-2], begin, end


def _fox_fwd(qa, ka, va, ex):
    BQ = BK = FOX_FWD_BLOCK
    nq = S // BQ
    n_pairs = HEADS // 2

    def body(*refs):
        p_id, i = pl.program_id(0), pl.program_id(1)
        (qa_ref, ka_ref, va_ref), (y_ref, qab_ref), (m_scr, acc_scr), begin, end = _hosted(
            ex, refs, 3, 2, (p_id == 0) & (i == 0), (p_id == n_pairs - 1) & (i == 0), (p_id == n_pairs - 1) & (i == nq - 1))
        begin()
        lane = _lane_iota((BQ, LANES))
        causal = _row_iota((BQ, BK)) >= _lane_iota((BQ, BK))
        m_scr[...] = jnp.full_like(m_scr, NEG)
        acc_scr[...] = jnp.zeros_like(acc_scr)

        def step(j, masked):
            rows = pl.ds(pl.multiple_of(j * BK, BK), BK)
            for hh in range(2):
                s = _dot(qa_ref[hh], ka_ref[hh, rows, :], NT)
                if masked:
                    s = jnp.where(causal, s, NEG)
                m_prev = m_scr[hh]
                m_new = jnp.maximum(m_prev, jnp.max(s, axis=1, keepdims=True))
                p = jnp.exp(s - jnp.tile(m_new, (1, BK // LANES)))
                acc_scr[hh] = jnp.exp(m_prev - m_new) * acc_scr[hh] + _dot(p.astype(BF16), va_ref[hh, rows, :])
                m_scr[hh] = m_new

        def full_step(j, carry):
            step(j, False)
            return carry

        lax.fori_loop(0, i, full_step, 0)
        step(i, True)
        outs = []
        for hh in range(2):
            acc = acc_scr[hh]
            den = jnp.broadcast_to(acc[:, DEN_V:DEN_V + 1], (BQ, LANES))
            outs.append(acc * (1.0 / den))
            n_hi, n_mid, n_lo = _split3(-(m_scr[hh] + jnp.log(den)))
            qab_ref[hh] = jnp.where(lane == LSE_Q, n_hi,
                                    jnp.where(lane == LSE_Q + 1, n_mid, jnp.where(lane == LSE_Q + 2, n_lo, qa_ref[hh])))
        y_ref[...] = jnp.where(lane < DH, outs[0], pltpu.roll(outs[1], DH, 1)).astype(BF16)
        end()

    pair_rows = pl.BlockSpec((2, BQ, LANES), lambda p, i: (p, i, 0))
    pair_all = pl.BlockSpec((2, S, LANES), lambda p, i: (p, 0, 0))
    n = len(ex.ins)
    res = pl.pallas_call(
        body, name="fox_fwd", grid=(n_pairs, nq), in_specs=[pair_rows, pair_all, pair_all] + [ANY] * n,
        out_specs=[pl.BlockSpec((BQ, LANES), lambda p, i: (i, D_POOL // LANES + p)), pair_rows] + [ANY] * n,
        out_shape=[jax.ShapeDtypeStruct((S, D), BF16), jax.ShapeDtypeStruct((HEADS, S, LANES), BF16)] + ex.out_shapes,
        scratch_shapes=[pltpu.VMEM((2, BQ, LANES), F32), pltpu.VMEM((2, BQ, LANES), F32)] + ex.scratch(),
        compiler_params=_params(("arbitrary", "arbitrary")),
    )(qa, ka, va, *ex.ins)
    return res[0], res[1], res[2:]


def _d_ycat(dy1, w_mix_out, ycat):
    def body(dy_ref, w_ref, y_ref, dp_ref, doa_ref):
        d = _dot(dy_ref[...], w_ref[...], NT)
        dp_ref[...] = d[:, :D_POOL]
        lane = _lane_iota((TR, LANES))
        low = lane < DH
        for p in range(HEADS // 2):
            cols = slice(D_POOL + LANES * p, D_POOL + LANES * (p + 1))
            do = d[:, cols]
            prod = do * y_ref[:, cols].astype(F32)
            deltas = (jnp.sum(jnp.where(low, prod, 0.0), axis=1, keepdims=True),
                      jnp.sum(jnp.where(low, 0.0, prod), axis=1, keepdims=True))
            for hh in range(2):
                d_hi, d_mid, d_lo = _split3_f32(deltas[hh])
                aug = jnp.where(lane == DELTA, d_hi, jnp.where(lane == DELTA + 1, d_mid, jnp.where(lane == DELTA + 2, d_lo, 0.0)))
                do_h = do if hh == 0 else pltpu.roll(do, DH, 1)
                doa_ref[2 * p + hh] = jnp.where(low, do_h, aug).astype(BF16)

    return pl.pallas_call(
        body, name="d_ycat", grid=(S // TR,),
        in_specs=[_row_spec(TR, D), pl.BlockSpec((D, D), lambda i: (0, 0)), _row_spec(TR, D)],
        out_specs=[_row_spec(TR, D_POOL), pl.BlockSpec((HEADS, TR, LANES), lambda i: (0, i, 0))],
        out_shape=[jax.ShapeDtypeStruct((S, D_POOL), F32), jax.ShapeDtypeStruct((HEADS, S, LANES), BF16)],
        compiler_params=_params(("parallel",)),
    )(dy1, w_mix_out, ycat)


def _fox_bwd(qab, doa, ka, va, ex):
    BQ = BK = FOX_BWD_BLOCK
    nk = S // BK
    n_pairs = HEADS // 2

    def body(*refs):
        p_id, j = pl.program_id(0), pl.program_id(1)
        (qab_ref, doa_ref, ka_ref, va_ref), (dqa_ref, dka_ref, dva_ref), _, begin, end = _hosted(
            ex, refs, 4, 3, (p_id == 0) & (j == 0), (p_id == n_pairs - 1) & (j == 0), (p_id == n_pairs - 1) & (j == nk - 1))
        begin()

        @pl.when(j == 0)
        def _():
            dqa_ref[...] = jnp.zeros_like(dqa_ref)

        causal = _row_iota((BQ, BK)) >= _lane_iota((BQ, BK))
        dka_ref[...] = jnp.zeros_like(dka_ref)
        dva_ref[...] = jnp.zeros_like(dva_ref)

        def step(i, masked):
            rows = pl.ds(pl.multiple_of(i * BQ, BQ), BQ)
            for hh in range(2):
                kb = ka_ref[hh]
                q = qab_ref[hh, rows, :]
                do = doa_ref[hh, rows, :]
                s = _dot(q, kb, NT)
                if masked:
                    s = jnp.where(causal, s, NEG)
                p = jnp.exp(s)
                ds = p * _dot(do, va_ref[hh], NT)
                pb = p.astype(BF16)
                dsb = ds.astype(BF16)
                dva_ref[hh] += _dot(pb, do, TN)
                dka_ref[hh] += _dot(dsb, q, TN)
                dqa_ref[hh, rows, :] += _dot(dsb, kb)

        def full_step(i, carry):
            step(i, False)
            return carry

        step(j, True)
        lax.fori_loop(j + 1, nk, full_step, 0)
        end()

    pair_all = pl.BlockSpec((2, S, LANES), lambda p, j: (p, 0, 0))
    pair_rows = pl.BlockSpec((2, BK, LANES), lambda p, j: (p, j, 0))
    shape = jax.ShapeDtypeStruct((HEADS, S, LANES), F32)
    n = len(ex.ins)
    res = pl.pallas_call(
        body, name="fox_bwd", grid=(n_pairs, nk), in_specs=[pair_all, pair_all, pair_rows, pair_rows] + [ANY] * n,
        out_specs=[pair_all, pair_rows, pair_rows] + [ANY] * n, out_shape=[shape] * 3 + ex.out_shapes,
        scratch_shapes=ex.scratch(), compiler_params=_params(("arbitrary", "arbitrary")),
    )(qab, doa, ka, va, *ex.ins)
    return res[0], res[1], res[2], res[3:]


def _fox_bwd_post(dqa, dka, dva, du, proj, bf_pad):
    tr = PREP_TR
    nt = S // tr

    def body(dqa_ref, dka_ref, dva_ref, du_ref, z_ref, bf_ref, dp_ref, dbf_ref, carry_ref):
        i = pl.program_id(0)

        @pl.when(i == 0)
        def _():
            carry_ref[...] = jnp.zeros_like(carry_ref)
            dbf_ref[...] = jnp.zeros_like(dbf_ref)

        lane = _lane_iota((tr, LANES))
        dcum = jnp.zeros((tr, LANES), F32)
        for h in range(HEADS):
            dc = dqa_ref[h][:, CUM_Q:CUM_Q + 1] - dka_ref[h][:, CUM_K:CUM_K + 1]
            dcum = jnp.where(lane == h, dc, dcum)
        tri = jnp.where(_lane_iota((tr, tr)) >= _row_iota((tr, tr)), 1.0, 0.0).astype(BF16)
        dlog_f = _cumsum_rows(dcum, tri, carry_ref[0:1, :])
        carry_ref[0:1, :] = dlog_f[0:1, :]
        z = z_ref[...] + bf_ref[...]
        df = jnp.where(lane < HEADS, dlog_f / (1.0 + jnp.exp(z)), 0.0)
        dbf_ref[...] += jnp.sum(df, axis=0, keepdims=True)

        dp_ref[:, 0:D_POOL] = du_ref[...].astype(BF16)
        low = lane < DH
        for ref, off, scale in ((dqa_ref, Q_OFF, DH ** -0.5), (dka_ref, K_OFF, 1.0), (dva_ref, V_OFF, 1.0)):
            for p in range(HEADS // 2):
                blk = jnp.where(low, ref[2 * p], pltpu.roll(ref[2 * p + 1], DH, 1))
                dp_ref[:, off + LANES * p:off + LANES * (p + 1)] = (blk * scale).astype(BF16)
        dp_ref[:, F_OFF:F_OFF + LANES] = df.astype(BF16)

    head_spec = pl.BlockSpec((HEADS, tr, LANES), lambda i: (0, nt - 1 - i, 0))
    return pl.pallas_call(
        body, name="fox_bwd_post", grid=(nt,),
        in_specs=[head_spec, head_spec, head_spec, pl.BlockSpec((tr, D_POOL), lambda i: (nt - 1 - i, 0)),
                  pl.BlockSpec((tr, LANES), lambda i: (nt - 1 - i, F_OFF // LANES)), _vec_spec(LANES)],
        out_specs=[pl.BlockSpec((tr, D_IN_PAD), lambda i: (nt - 1 - i, 0)), _vec_spec(LANES)],
        out_shape=[jax.ShapeDtypeStruct((S, D_IN_PAD), BF16), jax.ShapeDtypeStruct((1, LANES), F32)],
        scratch_shapes=[pltpu.VMEM((SUBLANES, LANES), F32)],
        compiler_params=_params(("arbitrary",)),
    )(dqa, dka, dva, du, proj, bf_pad)


POOL_HALO = 16


def _by_group(lane, a2, a4, a8, a16):
    return jnp.where(lane < 64, a2, jnp.where(lane < 128, a4, jnp.where(lane < 192, a8, a16)))


def _window_count(lane, t):
    return jnp.minimum(t + 1, _by_group(lane, 2, 4, 8, 16)).astype(F32)


def _pool_diff(u, halo, first, tile):
    n = TR + POOL_HALO
    ext = jnp.concatenate([jnp.where(first, 0.0, halo), u], axis=0)
    s2 = ext + pltpu.roll(ext, 1, 0)
    s4 = s2 + pltpu.roll(s2, 2, 0)
    s8 = s4 + pltpu.roll(s4, 4, 0)
    s16 = s8 + pltpu.roll(s8, 8, 0)
    lane = _lane_iota((n, D_POOL))
    win = _by_group(lane, s2, s4, s8, s16)[POOL_HALO:]
    lane = _lane_iota((TR, D_POOL))
    t = tile * TR + _row_iota((TR, D_POOL))
    return win / _window_count(lane, t) - u


def _prev_halo(rows, width, col):
    per = TR // rows
    return pl.BlockSpec((rows, width), lambda i: (jnp.maximum(i * per - 1, 0), col))


def _next_halo(rows, width, col):
    per = TR // rows
    return pl.BlockSpec((rows, width), lambda i: (jnp.minimum((i + 1) * per, S // rows - 1), col))


def _pool_fwd(proj, w_bd, ps, ycat):
    def body(u_ref, halo_ref, w_ref, ps_ref, ycat_ref, y_ref):
        i = pl.program_id(0)
        diff = _pool_diff(u_ref[...], halo_ref[...], i == 0, i)
        y_ref[...] = (_dot(diff.astype(BF16), w_ref[...]) * ps_ref[...]).astype(BF16)

    return pl.pallas_call(
        body, name="pool_fwd", grid=(S // TR,),
        in_specs=[_row_spec(TR, D_POOL), _prev_halo(POOL_HALO, D_POOL, 0),
                  pl.BlockSpec((D_POOL, D_POOL), lambda i: (0, 0)), _vec_spec(D_POOL), ANY],
        out_specs=_row_spec(TR, D_POOL), out_shape=jax.ShapeDtypeStruct((S, D), BF16), input_output_aliases={4: 0},
        compiler_params=_params(("parallel",)),
    )(proj, proj, w_bd, ps, ycat)


def _pool_bwd(proj, dycat, w_bd, w_bd_t, ps):
    nt = S // TR
    n = TR + POOL_HALO

    def body(u_ref, halo_ref, dy_ref, dyn_ref, w_ref, wt_ref, ps_ref, du_ref, dw_ref, dps_ref):
        i = pl.program_id(0)

        @pl.when(i == 0)
        def _():
            dw_ref[...] = jnp.zeros_like(dw_ref)
            dps_ref[...] = jnp.zeros_like(dps_ref)

        diff = _pool_diff(u_ref[...], halo_ref[...], i == 0, i).astype(BF16)
        dy = dy_ref[...]
        dps_ref[...] += jnp.sum(dy * _dot(diff, w_ref[...]), axis=0, keepdims=True)
        dy_ext = jnp.concatenate([dy, jnp.where(i == nt - 1, 0.0, dyn_ref[...])], axis=0)
        dmixed = (dy_ext * ps_ref[...]).astype(BF16)
        ddiff = _dot(dmixed, wt_ref[...])
        dw_ref[...] += _dot(diff, dmixed[:TR], TN)
        lane = _lane_iota((n, D_POOL))
        t = i * TR + _row_iota((n, D_POOL))
        e = ddiff / _window_count(lane, t)
        f2 = e + pltpu.roll(e, n - 1, 0)
        f4 = f2 + pltpu.roll(f2, n - 2, 0)
        f8 = f4 + pltpu.roll(f4, n - 4, 0)
        f16 = f8 + pltpu.roll(f8, n - 8, 0)
        du_ref[...] = _by_group(lane, f2, f4, f8, f16)[:TR] - ddiff[:TR]

    mat = pl.BlockSpec((D_POOL, D_POOL), lambda i: (0, 0))
    return pl.pallas_call(
        body, name="pool_bwd", grid=(nt,),
        in_specs=[_row_spec(TR, D_POOL), _prev_halo(POOL_HALO, D_POOL, 0), _row_spec(TR, D_POOL),
                  _next_halo(POOL_HALO, D_POOL, 0), mat, mat, _vec_spec(D_POOL)],
        out_specs=[_row_spec(TR, D_POOL), mat, _vec_spec(D_POOL)],
        out_shape=[jax.ShapeDtypeStruct((S, D_POOL), F32), jax.ShapeDtypeStruct((D_POOL, D_POOL), F32),
                   jax.ShapeDtypeStruct((1, D_POOL), F32)],
        compiler_params=_params(("arbitrary",)),
    )(proj, proj, dycat, dycat, w_bd, w_bd_t, ps)


def _xa_probs(q, k):
    s = _dot(q, k, NT) * (XA_DH ** -0.5)
    e = jnp.exp(s - jnp.max(s, axis=-1, keepdims=True))
    return e * (1.0 / jnp.sum(e, axis=-1, keepdims=True))


def _xattn_fwd(qx, kv):
    def body(q_ref, kv_ref, o_ref):
        for h in range(XA_HEADS):
            cols = slice(XA_DH * h, XA_DH * (h + 1))
            vcols = slice(D + XA_DH * h, D + XA_DH * (h + 1))
            p = _xa_probs(q_ref[:, cols], kv_ref[:, cols])
            o_ref[:, cols] = _dot(p.astype(BF16), kv_ref[:, vcols]).astype(BF16)

    return pl.pallas_call(
        body, name="xattn_fwd", grid=(S // TR,),
        in_specs=[_row_spec(TR, D), pl.BlockSpec((MEM, 2 * D), lambda i: (0, 0))],
        out_specs=_row_spec(TR, D), out_shape=jax.ShapeDtypeStruct((S, D), BF16),
        compiler_params=_params(("parallel",)),
    )(qx, kv)


def _xattn_bwd(qx, kv, dxo):
    def body(q_ref, kv_ref, do_ref, dq_ref, dkv_ref):
        i = pl.program_id(0)

        @pl.when(i == 0)
        def _():
            dkv_ref[...] = jnp.zeros_like(dkv_ref)

        for h in range(XA_HEADS):
            cols = slice(XA_DH * h, XA_DH * (h + 1))
            vcols = slice(D + XA_DH * h, D + XA_DH * (h + 1))
            q = q_ref[:, cols]
            k = kv_ref[:, cols]
            do = do_ref[:, cols]
            p = _xa_probs(q, k)
            dkv_ref[:, vcols] += _dot(p.astype(BF16), do, TN)
            dp = _dot(do, kv_ref[:, vcols], NT)
            ds = (p * (dp - jnp.sum(p * dp, axis=-1, keepdims=True)) * (XA_DH ** -0.5)).astype(BF16)
            dq_ref[:, cols] = _dot(ds, k).astype(BF16)
            dkv_ref[:, cols] += _dot(ds, q, TN)

    kv_spec = pl.BlockSpec((MEM, 2 * D), lambda i: (0, 0))
    return pl.pallas_call(
        body, name="xattn_bwd", grid=(S // TR,), in_specs=[_row_spec(TR, D), kv_spec, _row_spec(TR, D)],
        out_specs=[_row_spec(TR, D), kv_spec],
        out_shape=[jax.ShapeDtypeStruct((S, D), BF16), jax.ShapeDtypeStruct((MEM, 2 * D), F32)],
        compiler_params=_params(("arbitrary",)),
    )(qx, kv, dxo)


CONV_HALO = SUBLANES
TC = 512
GELU_K = 0.7978845608028654
GELU_C = 0.044715


def _conv3(ext, w, rows):
    h0 = ext[CONV_HALO:CONV_HALO + rows]
    h1 = pltpu.roll(ext, 1, 0)[CONV_HALO:CONV_HALO + rows]
    h2 = pltpu.roll(ext, 2, 0)[CONV_HALO:CONV_HALO + rows]
    return w[2:3] * h0 + w[1:2] * h1 + w[0:1] * h2 + w[3:4], (h2, h1, h0)


def _conv_specs():
    main = pl.BlockSpec((2, TR, TC), lambda j, i: (0, i, j))
    per = TR // CONV_HALO
    prev = pl.BlockSpec((2, CONV_HALO, TC), lambda j, i: (0, jnp.maximum(i * per - 1, 0), j))
    nxt = pl.BlockSpec((2, CONV_HALO, TC), lambda j, i: (0, jnp.minimum((i + 1) * per, S // CONV_HALO - 1), j))
    par = pl.BlockSpec((2, SUBLANES, TC), lambda j, i: (0, 0, j))
    return main, prev, nxt, par


def _convgate_fwd(hid, cwb):
    def body(h_ref, hp_ref, w_ref, act_ref):
        i = pl.program_id(1)
        c = []
        for g in range(2):
            ext = jnp.concatenate([jnp.where(i == 0, 0.0, hp_ref[g]), h_ref[g]], axis=0)
            c.append(_conv3(ext, w_ref[g], TR)[0])
        gate, up = c
        act_ref[...] = (jax.nn.gelu(gate, approximate=True) * up).astype(BF16)

    main, prev, _, par = _conv_specs()
    return pl.pallas_call(
        body, name="convgate_fwd", grid=(D_FF // TC, S // TR), in_specs=[main, prev, par],
        out_specs=pl.BlockSpec((TR, TC), lambda j, i: (i, j)), out_shape=jax.ShapeDtypeStruct((S, D_FF), BF16),
        compiler_params=_params(("parallel", "parallel")),
    )(hid, hid, cwb)


def _convgate_bwd(hid, dact, cwb):
    nr = S // TR
    n = TR + CONV_HALO

    def body(h_ref, hp_ref, hn_ref, da_ref, dan_ref, w_ref, dh_ref, dw_ref):
        i = pl.program_id(1)

        @pl.when(i == 0)
        def _():
            dw_ref[...] = jnp.zeros_like(dw_ref)

        da = jnp.concatenate([da_ref[...], jnp.where(i == nr - 1, 0.0, dan_ref[...])], axis=0)
        c, taps = [], []
        for g in range(2):
            ext = jnp.concatenate([jnp.where(i == 0, 0.0, hp_ref[g]), h_ref[g], hn_ref[g]], axis=0)
            cg, tg = _conv3(ext, w_ref[g], n)
            c.append(cg)
            taps.append(tg)
        gate, up = c
        th = jnp.tanh(GELU_K * (gate + GELU_C * gate * gate * gate))
        gelu = 0.5 * gate * (1.0 + th)
        dgelu = 0.5 * (1.0 + th) + 0.5 * gate * (1.0 - th * th) * GELU_K * (1.0 + 3.0 * GELU_C * gate * gate)
        for g, dc in enumerate((da * up * dgelu, da * gelu)):
            w = w_ref[g]
            dh = w[2:3] * dc[:TR] + w[1:2] * pltpu.roll(dc, n - 1, 0)[:TR] + w[0:1] * pltpu.roll(dc, n - 2, 0)[:TR]
            dh_ref[g] = dh.astype(BF16)
            dcm = dc[:TR]
            for r in range(3):
                dw_ref[g, r:r + 1, :] += jnp.sum(dcm * taps[g][r][:TR], axis=0, keepdims=True)
            dw_ref[g, 3:4, :] += jnp.sum(dcm, axis=0, keepdims=True)

    main, prev, nxt, par = _conv_specs()
    per = TR // CONV_HALO
    return pl.pallas_call(
        body, name="convgate_bwd", grid=(D_FF // TC, nr),
        in_specs=[main, prev, nxt, pl.BlockSpec((TR, TC), lambda j, i: (i, j)),
                  pl.BlockSpec((CONV_HALO, TC), lambda j, i: (jnp.minimum((i + 1) * per, S // CONV_HALO - 1), j)), par],
        out_specs=[main, par],
        out_shape=[jax.ShapeDtypeStruct((2, S, D_FF), BF16), jax.ShapeDtypeStruct((2, SUBLANES, D_FF), F32)],
        compiler_params=_params(("parallel", "arbitrary")),
    )(hid, hid, hid, dact, dact, cwb)


def _adam_update(w, g, m, v):
    m = ADAM_B1 * m + (1.0 - ADAM_B1) * g
    v = ADAM_B2 * v + (1.0 - ADAM_B2) * (g * g)
    m_hat = m / (1.0 - ADAM_B1 ** ADAM_STEP)
    v_hat = v / (1.0 - ADAM_B2 ** ADAM_STEP)
    return -ADAM_LR * (m_hat / (jnp.sqrt(v_hat) + ADAM_EPS) + ADAM_WD * w), m, v


def _row_tile(rows, cols, itemsize=4, target=TILE_BYTES):
    tr = SUBLANES
    while rows % (2 * tr) == 0 and 2 * tr * cols * itemsize <= target:
        tr *= 2
    assert rows % tr == 0, (rows, tr)
    return tr


def _adamw(name, w, g, m, v):
    rows, cols = w.shape
    tr = rows if rows * cols * 4 <= TILE_BYTES // 2 else _row_tile(rows, cols, target=TILE_BYTES // 2)

    def body(w_ref, g_ref, m_ref, v_ref, d_ref, nm_ref, nv_ref):
        d_ref[...], nm_ref[...], nv_ref[...] = _adam_update(w_ref[...], g_ref[...], m_ref[...], v_ref[...])

    spec = _row_spec(tr, cols)
    shape = jax.ShapeDtypeStruct((rows, cols), F32)
    return pl.pallas_call(
        body, name=name, grid=(rows // tr,), in_specs=[spec] * 4, out_specs=[spec] * 3, out_shape=[shape] * 3,
        compiler_params=_params(("parallel",)),
    )(w, g, m, v)


def _adamw_halves(name, core, w, g_mine, g_sibling, m, v, ex=None):
    rows, cols = w.shape
    half = rows // 2
    tr = _row_tile(half, cols, target=TILE_BYTES // 2)
    per = half // tr
    steps = rows // tr
    hosted = ex if ex is not None else _no_exchange()
    n = len(hosted.ins)

    def body(core_ref, *refs):
        i = pl.program_id(0)
        (w_ref, gm_ref, gs_ref, m_ref, v_ref), (g_ref, d_ref, nm_ref, nv_ref), _, begin, end = _hosted(
            hosted, refs, 5, 4, i == 0, i == 0, i == steps - 1)
        begin()
        g = jnp.where(i // per == core_ref[0], gm_ref[...], gs_ref[...])
        g_ref[...] = g
        d_ref[...], nm_ref[...], nv_ref[...] = _adam_update(w_ref[...], g, m_ref[...], v_ref[...])
        end()

    spec = pl.BlockSpec((tr, cols), lambda i, core_ref: (i, 0))
    half_spec = pl.BlockSpec((tr, cols), lambda i, core_ref: (i % per, 0))
    shape = jax.ShapeDtypeStruct((rows, cols), F32)
    res = pl.pallas_call(
        body, name=name, out_shape=[shape] * 4 + hosted.out_shapes,
        grid_spec=pltpu.PrefetchScalarGridSpec(
            num_scalar_prefetch=1, grid=(steps,), in_specs=[spec, half_spec, half_spec, spec, spec] + [ANY] * n,
            out_specs=[spec] * 4 + [ANY] * n, scratch_shapes=hosted.scratch()),
        compiler_params=_params(("arbitrary",)),
    )(core, w, g_mine, g_sibling, m, v, *hosted.ins)
    return res[:4], res[4:]


def _chip_sum(name, core, g, other):
    _, _, half, cols = g.shape
    tr = _row_tile(half, cols)

    def body(core_ref, g_ref, o_ref, p_ref):
        p_ref[...] = (g_ref[...] + o_ref[...]).astype(BF16)

    spec = pl.BlockSpec((None, tr, cols), lambda j, i, core_ref: (j, i, 0))
    return pl.pallas_call(
        body, name=name, out_shape=jax.ShapeDtypeStruct((N_CHIPS, half, cols), BF16),
        grid_spec=pltpu.PrefetchScalarGridSpec(
            num_scalar_prefetch=1, grid=(N_CHIPS, half // tr),
            in_specs=[pl.BlockSpec((None, None, tr, cols), lambda j, i, core_ref: (j, core_ref[0], i, 0)), spec],
            out_specs=spec),
        compiler_params=_params(("parallel", "parallel")),
    )(core, g, other)


def _mesh_sum(name, chip, received, own):
    _, half, cols = received.shape
    tr = _row_tile(half, cols, itemsize=2 * N_CHIPS)

    def body(chip_ref, r_ref, own_ref, o_ref):
        acc = None
        for j in range(N_CHIPS):
            term = jnp.where(chip_ref[0] == j, own_ref[...], r_ref[j]).astype(F32)
            acc = term if acc is None else acc + term
        o_ref[...] = acc

    return pl.pallas_call(
        body, name=name, out_shape=jax.ShapeDtypeStruct((half, cols), F32),
        grid_spec=pltpu.PrefetchScalarGridSpec(
            num_scalar_prefetch=1, grid=(half // tr,),
            in_specs=[pl.BlockSpec((N_CHIPS, tr, cols), lambda i, chip_ref: (0, i, 0)),
                      pl.BlockSpec((None, tr, cols), lambda i, chip_ref: (chip_ref[0], i, 0))],
            out_specs=pl.BlockSpec((tr, cols), lambda i, chip_ref: (i, 0))),
        compiler_params=_params(("parallel",)),
    )(chip, received, own)


CHIP_FLIPS = ((1, 0), (0, 1), (1, 1))


def _place():
    x, y, c = lax.axis_index("x"), lax.axis_index("y"), lax.axis_index("c")
    return x, y, c, 2 * x + y


def _remote(src, dst, sems_s, sems_r, k, dev):
    return pltpu.make_async_remote_copy(src_ref=src, dst_ref=dst, send_sem=sems_s.at[k], recv_sem=sems_r.at[k],
                                        device_id=dev, device_id_type=MESH)


class _Exchange:
    def __init__(self, ins, out_shapes, n_sems, start, forward, finish):
        self.ins, self.out_shapes, self.n_sems = list(ins), list(out_shapes), n_sems
        self.start, self.forward, self.finish = start, forward, finish

    def scratch(self):
        return [pltpu.SemaphoreType.DMA((self.n_sems,)), pltpu.SemaphoreType.DMA((self.n_sems,))]

    def run(self, name):
        n = len(self.ins)

        def body(*refs):
            args = (refs[:n], refs[n:2 * n]) + tuple(refs[2 * n:])
            self.start(*args)
            self.forward(*args)
            self.finish(*args)

        return pl.pallas_call(
            body, name=name, in_specs=[ANY] * n, out_specs=[ANY] * n, out_shape=self.out_shapes, scratch_shapes=self.scratch(),
        )(*self.ins)


def _all_gather_weights(halved, whole):
    nh, nw = len(halved), len(whole)
    n_arr = nh + nw

    def copies(ins, outs, sems_s, sems_r):
        x, y, c, me = _place()
        sibling = (x, y, 1 - c)
        own = [_remote(ins[k], outs[k].at[me], sems_s, sems_r, k, sibling) for k in range(n_arr)]
        first, passed = [], []
        for k in range(n_arr):
            for f, (fx, fy) in enumerate(CHIP_FLIPS):
                src, dst = (ins[k].at[c], outs[k].at[me, c]) if k < nh else (ins[k], outs[k].at[me])
                first.append(_remote(src, dst, sems_s, sems_r, n_arr + 3 * k + f, (x ^ fx, y ^ fy, c)))
        for k in range(nh):
            for f, (fx, fy) in enumerate(CHIP_FLIPS):
                landed = outs[k].at[2 * (x ^ fx) + (y ^ fy), c]
                passed.append(_remote(landed, landed, sems_s, sems_r, 4 * n_arr + 3 * k + f, sibling))
        return own, first, passed

    def start(*refs):
        own, first, _ = copies(*refs)
        for cp in own + first:
            cp.start()

    def forward(*refs):
        _, first, passed = copies(*refs)
        for arrived, cp in zip(first, passed):
            arrived.wait_recv()
            cp.start()

    def finish(*refs):
        own, first, passed = copies(*refs)
        for cp in first[3 * nh:] + passed + own:
            cp.wait_recv()
        for cp in first + passed + own:
            cp.wait_send()

    shapes = [jax.ShapeDtypeStruct((N_CHIPS,) + a.shape, a.dtype) for a in list(halved) + list(whole)]
    return _Exchange(list(halved) + list(whole), shapes, 7 * nh + 4 * nw, start, forward, finish)


def _swap_halves(gs):
    n = len(gs)

    def copies(ins, outs, sems_s, sems_r):
        x, y, c, _ = _place()
        return [_remote(ins[k].at[:, 1 - c], outs[k], sems_s, sems_r, k, (x, y, 1 - c)) for k in range(n)]

    def start(*refs):
        for cp in copies(*refs):
            cp.start()

    def finish(*refs):
        for cp in copies(*refs):
            cp.wait()

    shapes = [jax.ShapeDtypeStruct((g.shape[0],) + g.shape[2:], g.dtype) for g in gs]
    return _Exchange(gs, shapes, n, start, _no_copies, finish)


def _scatter_chips(ps):
    n = len(ps)

    def copies(ins, outs, sems_s, sems_r):
        x, y, c, me = _place()
        return [_remote(ins[k].at[2 * (x ^ fx) + (y ^ fy)], outs[k].at[me], sems_s, sems_r, 3 * k + f, (x ^ fx, y ^ fy, c))
                for k in range(n) for f, (fx, fy) in enumerate(CHIP_FLIPS)]

    def start(*refs):
        for cp in copies(*refs):
            cp.start()

    def forward(*refs):
        pass

    def finish(*refs):
        for cp in copies(*refs):
            cp.wait()

    shapes = [jax.ShapeDtypeStruct(p.shape, p.dtype) for p in ps]
    return _Exchange(ps, shapes, 3 * n, start, forward, finish)


def _swap_reduced(rs):
    n = len(rs)

    def copies(ins, outs, sems_s, sems_r):
        x, y, c, _ = _place()
        return [_remote(ins[k], outs[k], sems_s, sems_r, k, (x, y, 1 - c)) for k in range(n)]

    def start(*refs):
        for cp in copies(*refs):
            cp.start()

    def finish(*refs):
        for cp in copies(*refs):
            cp.wait()

    return _Exchange(rs, [jax.ShapeDtypeStruct(r.shape, r.dtype) for r in rs], n, start, _no_copies, finish)


N_DEV = 8


def _gather_small(buf):
    def copies(ins, outs, sems_s, sems_r):
        x, y, c, _ = _place()
        me = 4 * x + 2 * y + c
        return [_remote(ins[0], outs[0].at[me], sems_s, sems_r, o - 1, (x ^ (o >> 2), y ^ ((o >> 1) & 1), c ^ (o & 1)))
                for o in range(1, N_DEV)]

    def start(*refs):
        for cp in copies(*refs):
            cp.start()

    def finish(*refs):
        for cp in copies(*refs):
            cp.wait()

    return _Exchange([buf], [jax.ShapeDtypeStruct((N_DEV,) + buf.shape, buf.dtype)], N_DEV - 1, start, _no_copies, finish)


def _sum_devices(place, gathered, own):
    rows = own.shape[0]

    def body(place_ref, g_ref, own_ref, o_ref):
        acc = None
        for d in range(N_DEV):
            term = jnp.where(place_ref[0] == d, own_ref[...], g_ref[d])
            acc = term if acc is None else acc + term
        o_ref[...] = acc

    return pl.pallas_call(
        body, name="sum_devices", out_shape=jax.ShapeDtypeStruct((rows, LANES), F32),
        grid_spec=pltpu.PrefetchScalarGridSpec(
            num_scalar_prefetch=1, grid=(1,),
            in_specs=[pl.BlockSpec((N_DEV, rows, LANES), lambda i, place_ref: (0, 0, 0)),
                      pl.BlockSpec((rows, LANES), lambda i, place_ref: (0, 0))],
            out_specs=pl.BlockSpec((rows, LANES), lambda i, place_ref: (0, 0))),
        compiler_params=_params(("arbitrary",)),
    )(place, gathered, own)


def _no_copies(*refs):
    pass


def _no_exchange():
    return _Exchange([], [], 1, _no_copies, _no_copies, _no_copies)


class _NoComm:
    def gather_rest(self, p):
        return _no_exchange()

    def weights_landed(self, p, landed):
        pass

    def swap_first(self, g):
        return _no_exchange()

    def first_swapped(self, landed):
        pass

    def scatter_early(self, g):
        return _no_exchange()

    def scatter_landed(self, landed):
        pass

    def swap_reduced_early(self):
        return _no_exchange()

    def reduced_landed(self, landed):
        pass

    def scatter_late(self, g):
        return _no_exchange()

    def late_landed(self, landed):
        pass


def _local_step(x, mem, target, p, comm):
    h1 = _norm_fwd("norm_mix_pre", x, p["norm_mix_pre"])
    proj = _mm_nn("in_proj", h1, p["w_in"], F32, 1024, 896)
    qa, ka, va = _fox_prep(proj, p["bf_pad"])
    ycat, qab, landed = _fox_fwd(qa, ka, va, comm.gather_rest(p))
    comm.weights_landed(p, landed)
    ycat = _pool_fwd(proj, p["w_pool_bd"], p["pool_scale"], ycat)
    y1 = _mm_nn("mix_out", ycat, p["w_mix_out"], F32, 1024, 1024)
    x2, h2 = _resid_norm("resid_mix", x, y1, p["norm_mix_post"], p["norm_xa_pre"])
    qx = _mm_nn("xq", h2, p["w_xq"], BF16, 1024, 1024)
    mem_n = _norm_fwd("norm_mem", mem, p["norm_mem"])
    kv = _mm(
        "xkv", mem_n, p["w_xkv"], pl.BlockSpec((MEM, D), lambda i, j, k: (0, 0)),
        pl.BlockSpec((None, D, 512), lambda i, j, k: (j, 0, 0)), jax.ShapeDtypeStruct((MEM, 2 * D), BF16),
        pl.BlockSpec((MEM, 512), lambda i, j, k: (0, j)), (1, N_CHIPS, 1), NN, (MEM, 512))
    xo = _xattn_fwd(qx, kv)
    y2 = _mm_nn("xo", xo, p["w_xo"], F32, 1024, 1024)
    x3, h3 = _resid_norm("resid_xa", x2, y2, p["norm_xa_post"], p["norm_ffn_pre"])
    hid = _mm(
        "up_proj", h3, p["w_up"], pl.BlockSpec((1024, D), lambda i, j, k: (i, 0)),
        pl.BlockSpec((None, D, 1024), lambda i, j, k: (j // 2, 0, j % 2)), jax.ShapeDtypeStruct((2, S, D_FF), F32),
        pl.BlockSpec((None, 1024, 1024), lambda i, j, k: (j // 4, i, j % 4)), (S // 1024, 8, 1), NN, (1024, 1024))
    act = _convgate_fwd(hid, p["cwb"])
    y3 = _mm_nn("down_proj", act, p["w_down"], F32, 512, 512)

    g = {}
    dres, dy3, g["norm_ffn_post"], loss_cols = _loss_bwd(x3, y3, p["norm_ffn_post"], target)
    dact = _mm_nt("d_act", dy3, p["w_down"], F32, 1024, 1024)
    g["w_down"] = _mm_tn("dw_down", act, dy3, 512, 512)
    dhid, dcwb = _convgate_bwd(hid, dact, p["cwb"])
    g["w_up"] = _mm(
        "dw_up", h3, dhid, pl.BlockSpec((S, 512), lambda i, j, k: (0, i)),
        pl.BlockSpec((None, S, 512), lambda i, j, k: (j // 8, 0, j % 8)), jax.ShapeDtypeStruct((N_CHIPS, D, 2048), F32),
        pl.BlockSpec((None, 512, 512), lambda i, j, k: (j // 4, i, j % 4)), (2, 16, 1), TN, (512, 512))
    dh3, landed = _d_h3(dhid, p["w_up"], comm.swap_first(g))
    comm.first_swapped(landed)
    dres, dy2, g["norm_ffn_pre"], g["norm_xa_post"] = _mid_bwd("bwd_ffn_xa", dres, x3, p["norm_ffn_pre"], dh3, y2, p["norm_xa_post"])
    dxo = _mm_nt("d_xo", dy2, p["w_xo"], BF16, 1024, 1024)
    g["w_xo"] = _mm_tn("dw_xo", xo, dy2, 512, 512)
    dqx, dkv = _xattn_bwd(qx, kv, dxo)
    dkv = dkv.astype(BF16)
    dh2 = _mm_nt("d_h2", dqx, p["w_xq"], F32, 1024, 1024)
    g["w_xq"] = _mm_tn("dw_xq", h2, dqx, 512, 512)
    dmem_n = _mm(
        "d_mem", dkv, p["w_xkv"], pl.BlockSpec((MEM, 512), lambda i, j, k: (0, k)),
        pl.BlockSpec((None, D, 512), lambda i, j, k: (k, 0, 0)), jax.ShapeDtypeStruct((MEM, D), F32),
        pl.BlockSpec((MEM, D), lambda i, j, k: (0, 0)), (1, 1, N_CHIPS), NT, (MEM, D))
    g["w_xkv"] = _mm(
        "dw_xkv", mem_n, dkv, pl.BlockSpec((MEM, D), lambda i, j, k: (0, 0)),
        pl.BlockSpec((MEM, 512), lambda i, j, k: (0, j)), jax.ShapeDtypeStruct((N_CHIPS, D, 512), F32),
        pl.BlockSpec((None, D, 512), lambda i, j, k: (j, 0, 0)), (1, N_CHIPS, 1), TN, (D, 512))
    g["norm_mem"] = _gain_bwd("dg_mem", mem, p["norm_mem"], dmem_n)
    dres, dy1, g["norm_xa_pre"], g["norm_mix_post"] = _mid_bwd("bwd_xa_mix", dres, x2, p["norm_xa_pre"], dh2, y1, p["norm_mix_post"])
    dy_pool, doa = _d_ycat(dy1, p["w_mix_out"], ycat)
    g["w_mix_out"] = _mm_tn("dw_mix_out", ycat, dy1, 512, 512)
    dqa, dka, dva, landed = _fox_bwd(qab, doa, ka, va, comm.scatter_early(g))
    comm.scatter_landed(landed)
    du, g["w_pool_full"], g["pool_scale"] = _pool_bwd(proj, dy_pool, p["w_pool_bd"], p["w_pool_bd_t"], p["pool_scale"])
    dproj, g["bf_pad"] = _fox_bwd_post(dqa, dka, dva, du, proj, p["bf_pad"])
    g["w_in"], landed = _mm_tn("dw_in", h1, dproj, 512, 896, comm.swap_reduced_early())
    comm.reduced_landed(landed)
    dh1, landed = _mm_nt("d_h1", dproj, p["w_in"], F32, 1024, 1024, comm.scatter_late(g))
    comm.late_landed(landed)
    grad_x, g["norm_mix_pre"] = _first_bwd(dres, x, p["norm_mix_pre"], dh1)
    g["cwb"] = dcwb
    return grad_x, g, loss_cols


BIG = ("w_in", "w_mix_out", "w_xq", "w_xkv", "w_xo", "w_up", "w_down")
ROW_SHARDED = ("w_mix_out", "w_xq", "w_xo", "w_down")
SMALL = ("norm_mix_pre", "norm_mix_post", "b_forget", "w_pool", "pool_scale", "norm_mem", "norm_xa_pre", "norm_xa_post",
         "norm_ffn_pre", "norm_ffn_post", "conv_b")
ORDER = ("norm_mix_pre", "norm_mix_post", "w_in", "b_forget", "w_pool", "pool_scale", "w_mix_out", "norm_mem", "norm_xa_pre",
         "norm_xa_post", "w_xq", "w_xkv", "w_xo", "norm_ffn_pre", "norm_ffn_post", "w_up", "conv_w", "conv_b", "w_down")
SLOT = SUBLANES * LANES


def _pack(parts):
    rows, offs, off = [], [], 0
    for a in parts:
        flat = a.reshape(-1).astype(F32)
        n = -(-flat.shape[0] // SLOT) * SLOT
        rows.append(jnp.pad(flat, (0, n - flat.shape[0])).reshape(n // LANES, LANES))
        offs.append(off)
        off += n // LANES
    return jnp.concatenate(rows, axis=0), offs


def _unpack(buf, off, like):
    n = like.size
    rows = -(-n // LANES)
    return buf[off:off + rows].reshape(-1)[:n].reshape(like.shape)


FIRST = ("w_in", "w_mix_out")
REST = ("w_xq", "w_xkv", "w_xo", "w_up", "w_down")


def _first_params(w, full):
    w_in_full = jnp.pad(jnp.concatenate(list(full["w_in"]), axis=1), ((0, 0), (0, D_IN_PAD - D_IN)))
    w_pool_bd = jnp.zeros((D_POOL, D_POOL), F32)
    for gi in range(4):
        w_pool_bd = w_pool_bd.at[64 * gi:64 * (gi + 1), 64 * gi:64 * (gi + 1)].set(w["w_pool"][0, gi])
    p = {n: w[n] for n in ("norm_mix_pre", "norm_mix_post", "norm_mem", "norm_xa_pre", "norm_xa_post", "norm_ffn_pre",
                           "norm_ffn_post")}
    p.update(
        w_in=w_in_full, bf_pad=jnp.pad(w["b_forget"], ((0, 0), (0, LANES - HEADS))),
        w_pool_bd=w_pool_bd.astype(BF16), w_pool_bd_t=w_pool_bd.T.astype(BF16), pool_scale=w["pool_scale"].reshape(1, D_POOL),
        w_mix_out=full["w_mix_out"].reshape(D, D))
    return p


def _rest_params(w, full, conv_w_full):
    cw2 = conv_w_full.reshape(3, 2, D_FF).transpose(1, 0, 2)
    cwb = jnp.concatenate([cw2, w["conv_b"].reshape(1, 2, D_FF).transpose(1, 0, 2), jnp.zeros((2, 4, D_FF), F32)], axis=1)
    return dict(w_xq=full["w_xq"].reshape(D, D), w_xkv=full["w_xkv"], w_xo=full["w_xo"].reshape(D, D), w_up=full["w_up"],
                cwb=cwb, w_down=full["w_down"].reshape(D_FF, D))


def _whole_params(w, full, conv_w_full):
    p = _first_params(w, full)
    p.update(_rest_params(w, full, conv_w_full))
    return p


def _halved(a):
    return a.reshape(a.shape[:-2] + (2, a.shape[-2] // 2, a.shape[-1]))


class _StepComm:
    def __init__(self, w, shard2d, conv_w, core_id, chip_id):
        self.w, self.shard2d, self.conv_w, self.core_id, self.chip_id = w, shard2d, conv_w, core_id, chip_id
        self.early = ("w_mix_out",) + REST
        self.partial = self.received = None

    def gather_rest(self, p):
        return _all_gather_weights([_halved(self.shard2d[n].astype(BF16)) for n in REST], [self.conv_w.reshape(3, -1)])

    def weights_landed(self, p, landed):
        full = {n: a.reshape((N_CHIPS,) + self.shard2d[n].shape) for n, a in zip(REST, landed)}
        conv_w_full = jnp.transpose(landed[-1], (1, 0, 2)).reshape(3, 2 * D_FF)
        p.update(_rest_params(self.w, full, conv_w_full))

    def _view(self, g, n):
        return _halved(g[n].reshape((N_CHIPS,) + self.shard2d[n].shape))

    def swap_first(self, g):
        self.first = ("w_up", "w_down")
        return _swap_halves([self._view(g, n) for n in self.first])

    def first_swapped(self, landed):
        self.from_sibling = dict(zip(self.first, landed))

    def scatter_early(self, g):
        others = [n for n in self.early if n not in self.first]
        self.from_sibling.update(zip(others, _swap_halves([self._view(g, n) for n in others]).run("swap_halves")))
        self.partial = [_chip_sum("chip_sum_" + n, self.core_id, self._view(g, n), self.from_sibling[n]) for n in self.early]
        return _scatter_chips(self.partial)

    def scatter_landed(self, landed):
        self.received = list(landed)

    def swap_reduced_early(self):
        self.reduced = [_mesh_sum("mesh_sum_" + n, self.chip_id, r, own)
                        for n, r, own in zip(self.early, self.received, self.partial)]
        return _swap_reduced(self.reduced)

    def reduced_landed(self, landed):
        self.reduced_sibling = list(landed)

    def scatter_late(self, g):
        gw_in = g["w_in"][:, :D_IN]
        cols = D_IN // N_CHIPS
        view = _halved(jnp.stack([gw_in[:, cols * j:cols * (j + 1)] for j in range(N_CHIPS)]))
        self.partial_in = _chip_sum("chip_sum_w_in", self.core_id, view, _swap_halves([view]).run("swap_halves_w_in")[0])
        return _scatter_chips([self.partial_in])

    def late_landed(self, landed):
        self.received_in = landed[0]


def kernel(x, mem, norm_mix_pre, norm_mix_post, w_in, b_forget, w_pool, pool_scale, w_mix_out, norm_mem, norm_xa_pre, norm_xa_post, w_xq, w_xkv, w_xo, norm_ffn_pre, norm_ffn_post, w_up, conv_w, conv_b, w_down, loss_target, m_norm_mix_pre, m_norm_mix_post, m_w_in, m_b_forget, m_w_pool, m_pool_scale, m_w_mix_out, m_norm_mem, m_norm_xa_pre, m_norm_xa_post, m_w_xq, m_w_xkv, m_w_xo, m_norm_ffn_pre, m_norm_ffn_post, m_w_up, m_conv_w, m_conv_b, m_w_down, v_norm_mix_pre, v_norm_mix_post, v_w_in, v_b_forget, v_w_pool, v_pool_scale, v_w_mix_out, v_norm_mem, v_norm_xa_pre, v_norm_xa_post, v_w_xq, v_w_xkv, v_w_xo, v_norm_ffn_pre, v_norm_ffn_post, v_w_up, v_conv_w, v_conv_b, v_w_down):
    w = dict(norm_mix_pre=norm_mix_pre, norm_mix_post=norm_mix_post, w_in=w_in, b_forget=b_forget, w_pool=w_pool,
             pool_scale=pool_scale, w_mix_out=w_mix_out, norm_mem=norm_mem, norm_xa_pre=norm_xa_pre, norm_xa_post=norm_xa_post,
             w_xq=w_xq, w_xkv=w_xkv, w_xo=w_xo, norm_ffn_pre=norm_ffn_pre, norm_ffn_post=norm_ffn_post, w_up=w_up,
             conv_w=conv_w, conv_b=conv_b, w_down=w_down)
    m = dict(norm_mix_pre=m_norm_mix_pre, norm_mix_post=m_norm_mix_post, w_in=m_w_in, b_forget=m_b_forget, w_pool=m_w_pool,
             pool_scale=m_pool_scale, w_mix_out=m_w_mix_out, norm_mem=m_norm_mem, norm_xa_pre=m_norm_xa_pre,
             norm_xa_post=m_norm_xa_post, w_xq=m_w_xq, w_xkv=m_w_xkv, w_xo=m_w_xo, norm_ffn_pre=m_norm_ffn_pre,
             norm_ffn_post=m_norm_ffn_post, w_up=m_w_up, conv_w=m_conv_w, conv_b=m_conv_b, w_down=m_w_down)
    v = dict(norm_mix_pre=v_norm_mix_pre, norm_mix_post=v_norm_mix_post, w_in=v_w_in, b_forget=v_b_forget, w_pool=v_w_pool,
             pool_scale=v_pool_scale, w_mix_out=v_w_mix_out, norm_mem=v_norm_mem, norm_xa_pre=v_norm_xa_pre,
             norm_xa_post=v_norm_xa_post, w_xq=v_w_xq, w_xkv=v_w_xkv, w_xo=v_w_xo, norm_ffn_pre=v_norm_ffn_pre,
             norm_ffn_post=v_norm_ffn_post, w_up=v_w_up, conv_w=v_conv_w, conv_b=v_conv_b, w_down=v_w_down)
    chip = 2 * lax.axis_index("x") + lax.axis_index("y")

    core_id = lax.axis_index("c").astype(jnp.int32).reshape(1)
    chip_id = chip.astype(jnp.int32).reshape(1)

    shard2d = {n: w[n][0] for n in BIG}
    gathered = _all_gather_weights([_halved(shard2d[n].astype(BF16)) for n in FIRST], []).run("all_gather_first")
    p = _first_params(w, {n: a.reshape((N_CHIPS,) + shard2d[n].shape) for n, a in zip(FIRST, gathered)})

    comm = _StepComm(w, shard2d, conv_w, core_id, chip_id)
    grad_x, g, loss_cols = _local_step(x[0], mem[0], loss_target[0], p, comm)

    reduced_in = _mesh_sum("mesh_sum_w_in", chip_id, comm.received_in, comm.partial_in)
    names = ("w_in",) + comm.early
    reduced = [reduced_in] + comm.reduced
    reduced_sibling = list(_swap_reduced([reduced_in]).run("swap_reduced_w_in")) + comm.reduced_sibling
    grads = {}

    gw_pool = jnp.stack([g["w_pool_full"][64 * gi:64 * (gi + 1), 64 * gi:64 * (gi + 1)] for gi in range(4)])
    dcwb = g["cwb"]
    g_conv_w = dcwb[:, 0:3, :].transpose(1, 0, 2).reshape(3, 2 * D_FF)
    g_conv_b = dcwb[:, 3, :].reshape(2 * D_FF)
    small_g = dict(norm_mix_pre=g["norm_mix_pre"], norm_mix_post=g["norm_mix_post"], b_forget=g["bf_pad"][:, :HEADS],
                   w_pool=gw_pool, pool_scale=g["pool_scale"], norm_mem=g["norm_mem"], norm_xa_pre=g["norm_xa_pre"],
                   norm_xa_post=g["norm_xa_post"], norm_ffn_pre=g["norm_ffn_pre"], norm_ffn_post=g["norm_ffn_post"],
                   conv_b=g_conv_b)
    local_buf, offs = _pack([small_g[n] for n in SMALL] + [g_conv_w, loss_cols])

    delta, new_m, new_v = {}, {}, {}
    gathered = None
    for n, g_mine, g_sibling in zip(names, reduced, list(reduced_sibling)):
        ex = _gather_small(local_buf) if n == "w_up" else None
        (gn, d, nm, nv), landed = _adamw_halves("adamw_" + n, core_id, shard2d[n], g_mine, g_sibling, m[n][0], v[n][0], ex)
        grads[n], delta[n], new_m[n], new_v[n] = gn[None], d[None], nm[None], nv[None]
        if ex is not None:
            gathered = landed[0]
    place = (2 * chip + lax.axis_index("c")).astype(jnp.int32).reshape(1)
    buf = _sum_devices(place, gathered, local_buf)
    for n, off in zip(SMALL, offs):
        grads[n] = _unpack(buf, off, w[n])
    g_conv_w = _unpack(buf, offs[len(SMALL)], g_conv_w)
    grads["conv_w"] = lax.dynamic_slice_in_dim(g_conv_w, chip * (2 * D_FF // N_CHIPS), 2 * D_FF // N_CHIPS, axis=1).reshape(conv_w.shape)
    loss = jnp.sum(_unpack(buf, offs[len(SMALL) + 1], loss_cols))
    small_names = SMALL + ("conv_w",)
    packed = [_pack([d[n] for n in small_names])[0] for d in (w, grads, m, v)]
    offs = _pack([w[n] for n in small_names])[1]
    d, nm, nv = _adamw("adamw_small", *packed)
    for n, off in zip(small_names, offs):
        delta[n], new_m[n], new_v[n] = _unpack(d, off, w[n]), _unpack(nm, off, w[n]), _unpack(nv, off, w[n])

    return (loss, grad_x[None], *[grads[n] for n in ORDER], *[delta[n] for n in ORDER], *[new_m[n] for n in ORDER],
            *[new_v[n] for n in ORDER])
```

```python
import functools

import jax
import jax.numpy as jnp
from jax import lax
from jax.experimental import pallas as pl
from jax.experimental.pallas import tpu as pltpu

F32 = jnp.float32
BF16 = jnp.bfloat16
MESH = pl.DeviceIdType.MESH
ANY = pl.BlockSpec(memory_space=pl.ANY)
VMEM_SPEC = pl.BlockSpec(memory_space=pltpu.VMEM)

S = 4096
D = 1024
MEM = 256
D_POOL = 256
HEADS = 12
DH = 64
D_FOX = HEADS * DH
D_IN = D_POOL + 3 * D_FOX + HEADS
F_OFF = D_POOL + 3 * D_FOX
Q_OFF, K_OFF, V_OFF = D_POOL, D_POOL + D_FOX, D_POOL + 2 * D_FOX
XA_HEADS = 4
XA_DH = 256
D_FF = 4096
EPS = 1e-6
N_CHIPS = 4
ADAM_LR, ADAM_B1, ADAM_B2, ADAM_EPS, ADAM_WD, ADAM_STEP = 0.001, 0.9, 0.999, 1e-08, 0.01, 10

LANES = 128
SUBLANES = 8
D_IN_PAD = 21 * LANES
TR = 512
TILE_BYTES = 2 * 1024 * 1024
NEG = -1e30
VMEM_LIMIT = 52 * 1024 * 1024

NN = (((1,), (0,)), ((), ()))
NT = (((1,), (1,)), ((), ()))
TN = (((0,), (0,)), ((), ()))


def _dot(a, b, dims=NN):
    return lax.dot_general(a, b, dims, preferred_element_type=F32)


def _params(sem):
    return pltpu.CompilerParams(dimension_semantics=sem, vmem_limit_bytes=VMEM_LIMIT)


def _split3(x):
    hi = x.astype(BF16)
    r = x - hi.astype(F32)
    mid = r.astype(BF16)
    lo = (r - mid.astype(F32)).astype(BF16)
    return hi, mid, lo


def _split3_f32(x):
    hi = x.astype(BF16).astype(F32)
    r = x - hi
    mid = r.astype(BF16).astype(F32)
    return hi, mid, r - mid


def _lane_iota(shape):
    return lax.broadcasted_iota(jnp.int32, shape, len(shape) - 1)


def _row_iota(shape):
    return lax.broadcasted_iota(jnp.int32, shape, len(shape) - 2)


def _mm(name, a, b, a_spec, b_spec, out_shape, out_spec, grid, dims, acc_shape, ex=None):
    nk = grid[2]
    if ex is not None:
        return _mm_hosting(name, a, b, a_spec, b_spec, out_shape, out_spec, grid, dims, ex)

    def body(a_ref, b_ref, o_ref, *scr):
        p = _dot(a_ref[...], b_ref[...], dims)
        if nk == 1:
            o_ref[...] = p.astype(o_ref.dtype)
        else:
            acc = scr[0]
            k = pl.program_id(2)

            @pl.when(k == 0)
            def _():
                acc[...] = p

            @pl.when(k > 0)
            def _():
                acc[...] += p

            @pl.when(k == nk - 1)
            def _():
                o_ref[...] = acc[...].astype(o_ref.dtype)

    return pl.pallas_call(
        body, name=name, grid=grid, in_specs=[a_spec, b_spec], out_specs=out_spec, out_shape=out_shape,
        scratch_shapes=[pltpu.VMEM(acc_shape, F32)] if nk > 1 else [],
        compiler_params=_params(("parallel", "parallel", "arbitrary")),
    )(a, b)


def _mm_hosting(name, a, b, a_spec, b_spec, out_shape, out_spec, grid, dims, ex):
    assert grid[2] == 1
    n = len(ex.ins)

    def body(*refs):
        i, j = pl.program_id(0), pl.program_id(1)
        first = (i == 0) & (j == 0)
        (a_ref, b_ref), (o_ref,), _, begin, end = _hosted(
            ex, refs, 2, 1, first, first, (i == grid[0] - 1) & (j == grid[1] - 1))
        begin()
        o_ref[...] = _dot(a_ref[...], b_ref[...], dims).astype(o_ref.dtype)
        end()

    res = pl.pallas_call(
        body, name=name, grid=grid, in_specs=[a_spec, b_spec] + [ANY] * n, out_specs=[out_spec] + [ANY] * n,
        out_shape=[out_shape] + ex.out_shapes, scratch_shapes=ex.scratch(),
        compiler_params=_params(("arbitrary", "arbitrary", "arbitrary")),
    )(a, b, *ex.ins)
    return res[0], res[1:]


def _mm_nn(name, a, b, out_dtype, tm, tn):
    m, k = a.shape
    n = b.shape[1]
    return _mm(name, a, b, pl.BlockSpec((tm, k), lambda i, j, kk: (i, 0)), pl.BlockSpec((k, tn), lambda i, j, kk: (0, j)),
               jax.ShapeDtypeStruct((m, n), out_dtype), pl.BlockSpec((tm, tn), lambda i, j, kk: (i, j)),
               (m // tm, n // tn, 1), NN, (tm, tn))


def _mm_nt(name, a, b, out_dtype, tm, tn, ex=None):
    m, k = a.shape
    n = b.shape[0]
    return _mm(name, a, b, pl.BlockSpec((tm, k), lambda i, j, kk: (i, 0)), pl.BlockSpec((tn, k), lambda i, j, kk: (j, 0)),
               jax.ShapeDtypeStruct((m, n), out_dtype), pl.BlockSpec((tm, tn), lambda i, j, kk: (i, j)),
               (m // tm, n // tn, 1), NT, (tm, tn), ex)


def _mm_tn(name, a, b, tka, tn, ex=None):
    t, ka = a.shape
    n = b.shape[1]
    return _mm(name, a, b, pl.BlockSpec((t, tka), lambda i, j, kk: (0, i)), pl.BlockSpec((t, tn), lambda i, j, kk: (0, j)),
               jax.ShapeDtypeStruct((ka, n), F32), pl.BlockSpec((tka, tn), lambda i, j, kk: (i, j)),
               (ka // tka, n // tn, 1), TN, (tka, tn), ex)


def _d_h3(dhid, w_up, ex):
    tm = tn = 512
    shard = 2 * D_FF // N_CHIPS
    per_plane = D_FF // shard
    grid = (S // tm, D // tn)
    n = len(ex.ins)

    def body(*refs):
        i, j = pl.program_id(0), pl.program_id(1)
        first = (i == 0) & (j == 0)
        (a_ref, b_ref), (o_ref,), _, begin, end = _hosted(ex, refs, 2, 1, first, first, (i == grid[0] - 1) & (j == grid[1] - 1))
        begin()
        acc = None
        for k in range(N_CHIPS):
            cols = slice(shard * (k % per_plane), shard * (k % per_plane + 1))
            part = _dot(a_ref[k // per_plane, :, cols], b_ref[k], NT)
            acc = part if acc is None else acc + part
        o_ref[...] = acc
        end()

    res = pl.pallas_call(
        body, name="d_h3", grid=grid,
        in_specs=[pl.BlockSpec((2, tm, D_FF), lambda i, j: (0, i, 0)),
                  pl.BlockSpec((N_CHIPS, tn, shard), lambda i, j: (0, j, 0))] + [ANY] * n,
        out_specs=[pl.BlockSpec((tm, tn), lambda i, j: (i, j))] + [ANY] * n,
        out_shape=[jax.ShapeDtypeStruct((S, D), F32)] + ex.out_shapes, scratch_shapes=ex.scratch(),
        compiler_params=_params(("arbitrary", "arbitrary")),
    )(dhid, w_up, *ex.ins)
    return res[0], res[1:]


def _rms(x, g):
    r = lax.rsqrt(jnp.mean(x * x, axis=-1, keepdims=True) + EPS)
    return x * r * g


def _rms_bwd(x, g, dy):
    r = lax.rsqrt(jnp.mean(x * x, axis=-1, keepdims=True) + EPS)
    xh = x * r
    dxh = dy * g
    dx = r * (dxh - xh * jnp.mean(dxh * xh, axis=-1, keepdims=True))
    return dx, jnp.sum(dy * xh, axis=0, keepdims=True)


def _row_spec(tr, width):
    return pl.BlockSpec((tr, width), lambda i: (i, 0))


def _vec_spec(width):
    return pl.BlockSpec((1, width), lambda i: (0, 0))


def _norm_fwd(name, x, g):
    rows, width = x.shape
    tr = min(TR, rows)

    def body(x_ref, g_ref, h_ref):
        h_ref[...] = _rms(x_ref[...], g_ref[...]).astype(BF16)

    return pl.pallas_call(
        body, name=name, grid=(rows // tr,), in_specs=[_row_spec(tr, width), _vec_spec(width)],
        out_specs=_row_spec(tr, width), out_shape=jax.ShapeDtypeStruct((rows, width), BF16),
        compiler_params=_params(("parallel",)),
    )(x, g)


def _proj_resid_norm(name, a, w, xp, g_post, g_pre, w_next=None):
    def body(a_ref, w_ref, xp_ref, gpost_ref, gpre_ref, *rest):
        y_ref, xn_ref, h_ref = rest[-3:] if w_next is None else rest[1:4]
        y = _dot(a_ref[...], w_ref[...])
        y_ref[...] = y
        xn = xp_ref[...] + _rms(y, gpost_ref[...])
        xn_ref[...] = xn
        h = _rms(xn, gpre_ref[...]).astype(BF16)
        h_ref[...] = h
        if w_next is not None:
            rest[4][...] = _dot(h, rest[0][...]).astype(BF16)

    mat = pl.BlockSpec((D, D), lambda i: (0, 0))
    more = [] if w_next is None else [w_next]
    return pl.pallas_call(
        body, name=name, grid=(S // TR,),
        in_specs=[_row_spec(TR, D), mat, _row_spec(TR, D), _vec_spec(D), _vec_spec(D)] + [mat] * len(more),
        out_specs=[_row_spec(TR, D)] * (3 + len(more)),
        out_shape=[jax.ShapeDtypeStruct((S, D), F32), jax.ShapeDtypeStruct((S, D), F32), jax.ShapeDtypeStruct((S, D), BF16)]
        + [jax.ShapeDtypeStruct((S, D), BF16)] * len(more),
        compiler_params=_params(("parallel",)),
    )(a, w, xp, g_post, g_pre, *more)


def _down_loss_bwd(act, w_down, x3, g_post, target):
    def body(a_ref, w_ref, x_ref, g_ref, t_ref, dres_ref, dy_ref, dg_ref, loss_ref):
        i = pl.program_id(0)

        @pl.when(i == 0)
        def _():
            dg_ref[...] = jnp.zeros_like(dg_ref)
            loss_ref[...] = jnp.zeros_like(loss_ref)

        y = _dot(a_ref[...], w_ref[...])
        g = g_ref[...]
        e = x_ref[...] + _rms(y, g) - t_ref[...]
        loss_ref[...] += jnp.sum(e * e, axis=0, keepdims=True) * (0.5 / D)
        dres = e * (1.0 / D)
        dres_ref[...] = dres
        dy, dg = _rms_bwd(y, g, dres)
        dy_ref[...] = dy.astype(BF16)
        dg_ref[...] += dg

    return pl.pallas_call(
        body, name="down_loss_bwd", grid=(S // TR,),
        in_specs=[_row_spec(TR, D_FF), pl.BlockSpec((D_FF, D), lambda i: (0, 0)), _row_spec(TR, D), _vec_spec(D),
                  _row_spec(TR, D)],
        out_specs=[_row_spec(TR, D), _row_spec(TR, D), _vec_spec(D), _vec_spec(D)],
        out_shape=[jax.ShapeDtypeStruct((S, D), F32), jax.ShapeDtypeStruct((S, D), BF16),
                   jax.ShapeDtypeStruct((1, D), F32), jax.ShapeDtypeStruct((1, D), F32)],
        compiler_params=_params(("arbitrary",)),
    )(act, w_down, x3, g_post, target)


def _mid_bwd(name, dres, xcur, g_pre, dh, yprev, g_post):
    def body(dres_ref, x_ref, gpre_ref, dh_ref, y_ref, gpost_ref, dx_ref, dy_ref, dgpre_ref, dgpost_ref):
        i = pl.program_id(0)

        @pl.when(i == 0)
        def _():
            dgpre_ref[...] = jnp.zeros_like(dgpre_ref)
            dgpost_ref[...] = jnp.zeros_like(dgpost_ref)

        dxn, dgpre = _rms_bwd(x_ref[...], gpre_ref[...], dh_ref[...])
        dx = dres_ref[...] + dxn
        dx_ref[...] = dx
        dy, dgpost = _rms_bwd(y_ref[...], gpost_ref[...], dx)
        dy_ref[...] = dy.astype(BF16)
        dgpre_ref[...] += dgpre
        dgpost_ref[...] += dgpost

    return pl.pallas_call(
        body, name=name, grid=(S // TR,),
        in_specs=[_row_spec(TR, D), _row_spec(TR, D), _vec_spec(D), _row_spec(TR, D), _row_spec(TR, D), _vec_spec(D)],
        out_specs=[_row_spec(TR, D), _row_spec(TR, D), _vec_spec(D), _vec_spec(D)],
        out_shape=[jax.ShapeDtypeStruct((S, D), F32), jax.ShapeDtypeStruct((S, D), BF16),
                   jax.ShapeDtypeStruct((1, D), F32), jax.ShapeDtypeStruct((1, D), F32)],
        compiler_params=_params(("arbitrary",)),
    )(dres, xcur, g_pre, dh, yprev, g_post)


def _first_bwd(dres, x, g, dh):
    def body(dres_ref, x_ref, g_ref, dh_ref, dx_ref, dg_ref):
        i = pl.program_id(0)

        @pl.when(i == 0)
        def _():
            dg_ref[...] = jnp.zeros_like(dg_ref)

        dxn, dg = _rms_bwd(x_ref[...], g_ref[...], dh_ref[...])
        dx_ref[...] = dres_ref[...] + dxn
        dg_ref[...] += dg

    return pl.pallas_call(
        body, name="first_bwd", grid=(S // TR,),
        in_specs=[_row_spec(TR, D), _row_spec(TR, D), _vec_spec(D), _row_spec(TR, D)],
        out_specs=[_row_spec(TR, D), _vec_spec(D)],
        out_shape=[jax.ShapeDtypeStruct((S, D), F32), jax.ShapeDtypeStruct((1, D), F32)],
        compiler_params=_params(("arbitrary",)),
    )(dres, x, g, dh)


def _gain_bwd(name, x, g, dy):
    rows, width = x.shape

    def body(x_ref, g_ref, dy_ref, dg_ref):
        _, dg = _rms_bwd(x_ref[...], g_ref[...], dy_ref[...])
        dg_ref[...] = dg

    return pl.pallas_call(
        body, name=name, grid=(1,), in_specs=[_row_spec(rows, width), _vec_spec(width), _row_spec(rows, width)],
        out_specs=_vec_spec(width), out_shape=jax.ShapeDtypeStruct((1, width), F32),
        compiler_params=_params(("arbitrary",)),
    )(x, g, dy)


CUM_Q = DH
CUM_K = DH + 3
LSE_Q = DH + 6
DEN_V = DH
DELTA = DH + 1
PREP_TR = 256
FOX_FWD_BLOCK = 1024
FOX_BWD_BLOCK = 512


def _head_block(ref, off, h):
    start = off + DH * h
    base = (start // LANES) * LANES
    blk = ref[:, base:base + LANES]
    return pltpu.roll(blk, DH, 1) if start % LANES else blk


def _cumsum_rows(x, tri, carry):
    hi, mid, lo = _split3(x)
    return _dot(tri, hi) + _dot(tri, mid) + _dot(tri, lo) + carry


def _fox_prep(proj, bf_pad):
    tr = PREP_TR

    def body(proj_ref, bf_ref, qa_ref, ka_ref, va_ref, carry_ref):
        i = pl.program_id(0)

        @pl.when(i == 0)
        def _():
            carry_ref[...] = jnp.zeros_like(carry_ref)

        lane = _lane_iota((tr, LANES))
        z = proj_ref[:, F_OFF:F_OFF + LANES] + bf_ref[...]
        log_f = jnp.minimum(z, 0.0) - jnp.log(1.0 + jnp.exp(-jnp.abs(z)))
        log_f = jnp.where(lane < HEADS, log_f, 0.0)
        tri = jnp.where(_row_iota((tr, tr)) >= _lane_iota((tr, tr)), 1.0, 0.0).astype(BF16)
        cum = _cumsum_rows(log_f, tri, carry_ref[0:1, :])
        carry_ref[0:1, :] = cum[tr - 1:tr, :]

        ones_q = jnp.where((lane >= CUM_K) & (lane < CUM_K + 3), 1.0, 0.0)
        ones_k = jnp.where(((lane >= CUM_Q) & (lane < CUM_Q + 3)) | ((lane >= LSE_Q) & (lane < LSE_Q + 3)), 1.0, 0.0)
        aug_v = jnp.where(lane == DEN_V, 1.0, jnp.where((lane >= DELTA) & (lane < DELTA + 3), -1.0, 0.0))
        for h in range(HEADS):
            c_hi, c_mid, c_lo = _split3_f32(jnp.broadcast_to(cum[:, h:h + 1], (tr, LANES)))
            aug_q = jnp.where(lane == CUM_Q, c_hi, jnp.where(lane == CUM_Q + 1, c_mid, jnp.where(lane == CUM_Q + 2, c_lo, ones_q)))
            aug_k = jnp.where(lane == CUM_K, -c_hi, jnp.where(lane == CUM_K + 1, -c_mid, jnp.where(lane == CUM_K + 2, -c_lo, ones_k)))
            qa_ref[h] = jnp.where(lane < DH, _head_block(proj_ref, Q_OFF, h) * (DH ** -0.5), aug_q).astype(BF16)
            ka_ref[h] = jnp.where(lane < DH, _head_block(proj_ref, K_OFF, h), aug_k).astype(BF16)
            va_ref[h] = jnp.where(lane < DH, _head_block(proj_ref, V_OFF, h), aug_v).astype(BF16)

    head_spec = pl.BlockSpec((HEADS, tr, LANES), lambda i: (0, i, 0))
    head_shape = jax.ShapeDtypeStruct((HEADS, S, LANES), BF16)
    return pl.pallas_call(
        body, name="fox_prep", grid=(S // tr,), in_specs=[_row_spec(tr, D_IN_PAD), _vec_spec(LANES)],
        out_specs=[head_spec] * 3, out_shape=[head_shape] * 3, scratch_shapes=[pltpu.VMEM((SUBLANES, LANES), F32)],
        compiler_params=_params(("arbitrary",)),
    )(proj, bf_pad)


def _hosted(ex, refs, n_blocked_in, n_blocked_out, first, forward_at, last):
    n = len(ex.ins)
    own_in = refs[:n_blocked_in]
    ex_in = refs[n_blocked_in:n_blocked_in + n]
    own_out = refs[n_blocked_in + n:n_blocked_in + n + n_blocked_out]
    ex_out = refs[n_blocked_in + n + n_blocked_out:n_blocked_in + 2 * n + n_blocked_out]
    rest = refs[n_blocked_in + 2 * n + n_blocked_out:]
    args = (ex_in, ex_out, rest[-2], rest[-1])

    def begin():
        @pl.when(first)
        def _():
            ex.start(*args)

        @pl.when(forward_at)
        def _():
            ex.forward(*args)

    def end():
        @pl.when(last)
        def _():
            ex.finish(*args)

    return own_in, own_out, rest[:-2], begin, end


def _fox_fwd(qa, ka, va, ex):
    BQ = BK = FOX_FWD_BLOCK
    nq = S // BQ
    n_pairs = HEADS // 2

    def body(*refs):
        p_id, i = pl.program_id(0), pl.program_id(1)
        (qa_ref, ka_ref, va_ref), (y_ref, qab_ref), (m_scr, acc_scr), begin, end = _hosted(
            ex, refs, 3, 2, (p_id == 0) & (i == 0), (p_id == n_pairs - 1) & (i == 0), (p_id == n_pairs - 1) & (i == nq - 1))
        begin()
        lane = _lane_iota((BQ, LANES))
        causal = _row_iota((BQ, BK)) >= _lane_iota((BQ, BK))
        m_scr[...] = jnp.full_like(m_scr, NEG)
        acc_scr[...] = jnp.zeros_like(acc_scr)

        def step(j, masked):
            rows = pl.ds(pl.multiple_of(j * BK, BK), BK)
            for hh in range(2):
                s = _dot(qa_ref[hh], ka_ref[hh, rows, :], NT)
                if masked:
                    s = jnp.where(causal, s, NEG)
                m_prev = m_scr[hh]
                m_new = jnp.maximum(m_prev, jnp.max(s, axis=1, keepdims=True))
                p = jnp.exp(s - jnp.tile(m_new, (1, BK // LANES)))
                acc_scr[hh] = jnp.exp(m_prev - m_new) * acc_scr[hh] + _dot(p.astype(BF16), va_ref[hh, rows, :])
                m_scr[hh] = m_new

        def full_step(j, carry):
            step(j, False)
            return carry

        lax.fori_loop(0, i, full_step, 0)
        step(i, True)
        outs = []
        for hh in range(2):
            acc = acc_scr[hh]
            den = jnp.broadcast_to(acc[:, DEN_V:DEN_V + 1], (BQ, LANES))
            outs.append(acc * (1.0 / den))
            n_hi, n_mid, n_lo = _split3(-(m_scr[hh] + jnp.log(den)))
            qab_ref[hh] = jnp.where(lane == LSE_Q, n_hi,
                                    jnp.where(lane == LSE_Q + 1, n_mid, jnp.where(lane == LSE_Q + 2, n_lo, qa_ref[hh])))
        y_ref[...] = jnp.where(lane < DH, outs[0], pltpu.roll(outs[1], DH, 1)).astype(BF16)
        end()

    pair_rows = pl.BlockSpec((2, BQ, LANES), lambda p, i: (p, i, 0))
    pair_all = pl.BlockSpec((2, S, LANES), lambda p, i: (p, 0, 0))
    n = len(ex.ins)
    res = pl.pallas_call(
        body, name="fox_fwd", grid=(n_pairs, nq), in_specs=[pair_rows, pair_all, pair_all] + [ANY] * n,
        out_specs=[pl.BlockSpec((BQ, LANES), lambda p, i: (i, D_POOL // LANES + p)), pair_rows] + [ANY] * n,
        out_shape=[jax.ShapeDtypeStruct((S, D), BF16), jax.ShapeDtypeStruct((HEADS, S, LANES), BF16)] + ex.out_shapes,
        scratch_shapes=[pltpu.VMEM((2, BQ, LANES), F32), pltpu.VMEM((2, BQ, LANES), F32)] + ex.scratch(),
        compiler_params=_params(("arbitrary", "arbitrary")),
    )(qa, ka, va, *ex.ins)
    return res[0], res[1], res[2:]


def _bwd_xa_mix(dqx, w_xq, dres, x2, g_pre, y1, g_post, w_mix_out, ycat):
    def body(dq_ref, wq_ref, dres_ref, x_ref, gpre_ref, y_ref, gpost_ref, wm_ref, ycat_ref,
             dx_ref, dy_ref, dgpre_ref, dgpost_ref, dp_ref, doa_ref):
        i = pl.program_id(0)

        @pl.when(i == 0)
        def _():
            dgpre_ref[...] = jnp.zeros_like(dgpre_ref)
            dgpost_ref[...] = jnp.zeros_like(dgpost_ref)

        dxn, dgpre = _rms_bwd(x_ref[...], gpre_ref[...], _dot(dq_ref[...], wq_ref[...], NT))
        dx = dres_ref[...] + dxn
        dx_ref[...] = dx
        dy, dgpost = _rms_bwd(y_ref[...], gpost_ref[...], dx)
        dy = dy.astype(BF16)
        dy_ref[...] = dy
        dgpre_ref[...] += dgpre
        dgpost_ref[...] += dgpost

        d = _dot(dy, wm_ref[...], NT)
        dp_ref[...] = d[:, :D_POOL]
        lane = _lane_iota((TR, LANES))
        low = lane < DH
        for p in range(HEADS // 2):
            cols = slice(D_POOL + LANES * p, D_POOL + LANES * (p + 1))
            do = d[:, cols]
            prod = do * ycat_ref[:, cols].astype(F32)
            deltas = (jnp.sum(jnp.where(low, prod, 0.0), axis=1, keepdims=True),
                      jnp.sum(jnp.where(low, 0.0, prod), axis=1, keepdims=True))
            for hh in range(2):
                d_hi, d_mid, d_lo = _split3_f32(deltas[hh])
                aug = jnp.where(lane == DELTA, d_hi, jnp.where(lane == DELTA + 1, d_mid, jnp.where(lane == DELTA + 2, d_lo, 0.0)))
                do_h = do if hh == 0 else pltpu.roll(do, DH, 1)
                doa_ref[2 * p + hh] = jnp.where(low, do_h, aug).astype(BF16)

    mat = pl.BlockSpec((D, D), lambda i: (0, 0))
    return pl.pallas_call(
        body, name="bwd_xa_mix", grid=(S // TR,),
        in_specs=[_row_spec(TR, D), mat, _row_spec(TR, D), _row_spec(TR, D), _vec_spec(D), _row_spec(TR, D), _vec_spec(D), mat,
                  _row_spec(TR, D)],
        out_specs=[_row_spec(TR, D), _row_spec(TR, D), _vec_spec(D), _vec_spec(D), _row_spec(TR, D_POOL),
                   pl.BlockSpec((HEADS, TR, LANES), lambda i: (0, i, 0))],
        out_shape=[jax.ShapeDtypeStruct((S, D), F32), jax.ShapeDtypeStruct((S, D), BF16), jax.ShapeDtypeStruct((1, D), F32),
                   jax.ShapeDtypeStruct((1, D), F32), jax.ShapeDtypeStruct((S, D_POOL), F32),
                   jax.ShapeDtypeStruct((HEADS, S, LANES), BF16)],
        compiler_params=_params(("arbitrary",)),
    )(dqx, w_xq, dres, x2, g_pre, y1, g_post, w_mix_out, ycat)


def _fox_bwd(qab, doa, ka, va, ex):
    BQ = BK = FOX_BWD_BLOCK
    nk = S // BK
    n_pairs = HEADS // 2

    def body(*refs):
        p_id, j = pl.program_id(0), pl.program_id(1)
        (qab_ref, doa_ref, ka_ref, va_ref), (dqa_ref, dka_ref, dva_ref), _, begin, end = _hosted(
            ex, refs, 4, 3, (p_id == 0) & (j == 0), (p_id == n_pairs - 1) & (j == 0), (p_id == n_pairs - 1) & (j == nk - 1))
        begin()

        @pl.when(j == 0)
        def _():
            dqa_ref[...] = jnp.zeros_like(dqa_ref)

        causal = _row_iota((BQ, BK)) >= _lane_iota((BQ, BK))
        dka_ref[...] = jnp.zeros_like(dka_ref)
        dva_ref[...] = jnp.zeros_like(dva_ref)

        def step(i, masked):
            rows = pl.ds(pl.multiple_of(i * BQ, BQ), BQ)
            for hh in range(2):
                kb = ka_ref[hh]
                q = qab_ref[hh, rows, :]
                do = doa_ref[hh, rows, :]
                s = _dot(q, kb, NT)
                if masked:
                    s = jnp.where(causal, s, NEG)
                p = jnp.exp(s)
                ds = p * _dot(do, va_ref[hh], NT)
                pb = p.astype(BF16)
                dsb = ds.astype(BF16)
                dva_ref[hh] += _dot(pb, do, TN)
                dka_ref[hh] += _dot(dsb, q, TN)
                dqa_ref[hh, rows, :] += _dot(dsb, kb)

        def full_step(i, carry):
            step(i, False)
            return carry

        step(j, True)
        lax.fori_loop(j + 1, nk, full_step, 0)
        end()

    pair_all = pl.BlockSpec((2, S, LANES), lambda p, j: (p, 0, 0))
    pair_rows = pl.BlockSpec((2, BK, LANES), lambda p, j: (p, j, 0))
    shape = jax.ShapeDtypeStruct((HEADS, S, LANES), F32)
    n = len(ex.ins)
    res = pl.pallas_call(
        body, name="fox_bwd", grid=(n_pairs, nk), in_specs=[pair_all, pair_all, pair_rows, pair_rows] + [ANY] * n,
        out_specs=[pair_all, pair_rows, pair_rows] + [ANY] * n, out_shape=[shape] * 3 + ex.out_shapes,
        scratch_shapes=ex.scratch(), compiler_params=_params(("arbitrary", "arbitrary")),
    )(qab, doa, ka, va, *ex.ins)
    return res[0], res[1], res[2], res[3:]


def _fox_bwd_post(dqa, dka, dva, du, proj, bf_pad):
    tr = PREP_TR
    nt = S // tr

    def body(dqa_ref, dka_ref, dva_ref, du_ref, z_ref, bf_ref, dp_ref, dbf_ref, carry_ref):
        i = pl.program_id(0)

        @pl.when(i == 0)
        def _():
            carry_ref[...] = jnp.zeros_like(carry_ref)
            dbf_ref[...] = jnp.zeros_like(dbf_ref)

        lane = _lane_iota((tr, LANES))
        dcum = jnp.zeros((tr, LANES), F32)
        for h in range(HEADS):
            dc = dqa_ref[h][:, CUM_Q:CUM_Q + 1] - dka_ref[h][:, CUM_K:CUM_K + 1]
            dcum = jnp.where(lane == h, dc, dcum)
        tri = jnp.where(_lane_iota((tr, tr)) >= _row_iota((tr, tr)), 1.0, 0.0).astype(BF16)
        dlog_f = _cumsum_rows(dcum, tri, carry_ref[0:1, :])
        carry_ref[0:1, :] = dlog_f[0:1, :]
        z = z_ref[...] + bf_ref[...]
        df = jnp.where(lane < HEADS, dlog_f / (1.0 + jnp.exp(z)), 0.0)
        dbf_ref[...] += jnp.sum(df, axis=0, keepdims=True)

        dp_ref[:, 0:D_POOL] = du_ref[...].astype(BF16)
        low = lane < DH
        for ref, off, scale in ((dqa_ref, Q_OFF, DH ** -0.5), (dka_ref, K_OFF, 1.0), (dva_ref, V_OFF, 1.0)):
            for p in range(HEADS // 2):
                blk = jnp.where(low, ref[2 * p], pltpu.roll(ref[2 * p + 1], DH, 1))
                dp_ref[:, off + LANES * p:off + LANES * (p + 1)] = (blk * scale).astype(BF16)
        dp_ref[:, F_OFF:F_OFF + LANES] = df.astype(BF16)

    head_spec = pl.BlockSpec((HEADS, tr, LANES), lambda i: (0, nt - 1 - i, 0))
    return pl.pallas_call(
        body, name="fox_bwd_post", grid=(nt,),
        in_specs=[head_spec, head_spec, head_spec, pl.BlockSpec((tr, D_POOL), lambda i: (nt - 1 - i, 0)),
                  pl.BlockSpec((tr, LANES), lambda i: (nt - 1 - i, F_OFF // LANES)), _vec_spec(LANES)],
        out_specs=[pl.BlockSpec((tr, D_IN_PAD), lambda i: (nt - 1 - i, 0)), _vec_spec(LANES)],
        out_shape=[jax.ShapeDtypeStruct((S, D_IN_PAD), BF16), jax.ShapeDtypeStruct((1, LANES), F32)],
        scratch_shapes=[pltpu.VMEM((SUBLANES, LANES), F32)],
        compiler_params=_params(("arbitrary",)),
    )(dqa, dka, dva, du, proj, bf_pad)


POOL_HALO = 16


def _by_group(lane, a2, a4, a8, a16):
    return jnp.where(lane < 64, a2, jnp.where(lane < 128, a4, jnp.where(lane < 192, a8, a16)))


def _window_count(lane, t):
    return jnp.minimum(t + 1, _by_group(lane, 2, 4, 8, 16)).astype(F32)


def _pool_diff(u, halo, first, tile):
    n = TR + POOL_HALO
    ext = jnp.concatenate([jnp.where(first, 0.0, halo), u], axis=0)
    s2 = ext + pltpu.roll(ext, 1, 0)
    s4 = s2 + pltpu.roll(s2, 2, 0)
    s8 = s4 + pltpu.roll(s4, 4, 0)
    s16 = s8 + pltpu.roll(s8, 8, 0)
    lane = _lane_iota((n, D_POOL))
    win = _by_group(lane, s2, s4, s8, s16)[POOL_HALO:]
    lane = _lane_iota((TR, D_POOL))
    t = tile * TR + _row_iota((TR, D_POOL))
    return win / _window_count(lane, t) - u


def _prev_halo(rows, width, col):
    per = TR // rows
    return pl.BlockSpec((rows, width), lambda i: (jnp.maximum(i * per - 1, 0), col))


def _next_halo(rows, width, col):
    per = TR // rows
    return pl.BlockSpec((rows, width), lambda i: (jnp.minimum((i + 1) * per, S // rows - 1), col))


def _pool_fwd(proj, w_bd, ps, ycat):
    def body(u_ref, halo_ref, w_ref, ps_ref, ycat_ref, y_ref):
        i = pl.program_id(0)
        diff = _pool_diff(u_ref[...], halo_ref[...], i == 0, i)
        y_ref[...] = (_dot(diff.astype(BF16), w_ref[...]) * ps_ref[...]).astype(BF16)

    return pl.pallas_call(
        body, name="pool_fwd", grid=(S // TR,),
        in_specs=[_row_spec(TR, D_POOL), _prev_halo(POOL_HALO, D_POOL, 0),
                  pl.BlockSpec((D_POOL, D_POOL), lambda i: (0, 0)), _vec_spec(D_POOL), ANY],
        out_specs=_row_spec(TR, D_POOL), out_shape=jax.ShapeDtypeStruct((S, D), BF16), input_output_aliases={4: 0},
        compiler_params=_params(("parallel",)),
    )(proj, proj, w_bd, ps, ycat)


def _pool_bwd(proj, dycat, w_bd, w_bd_t, ps):
    nt = S // TR
    n = TR + POOL_HALO

    def body(u_ref, halo_ref, dy_ref, dyn_ref, w_ref, wt_ref, ps_ref, du_ref, dw_ref, dps_ref):
        i = pl.program_id(0)

        @pl.when(i == 0)
        def _():
            dw_ref[...] = jnp.zeros_like(dw_ref)
            dps_ref[...] = jnp.zeros_like(dps_ref)

        diff = _pool_diff(u_ref[...], halo_ref[...], i == 0, i).astype(BF16)
        dy = dy_ref[...]
        dps_ref[...] += jnp.sum(dy * _dot(diff, w_ref[...]), axis=0, keepdims=True)
        dy_ext = jnp.concatenate([dy, jnp.where(i == nt - 1, 0.0, dyn_ref[...])], axis=0)
        dmixed = (dy_ext * ps_ref[...]).astype(BF16)
        ddiff = _dot(dmixed, wt_ref[...])
        dw_ref[...] += _dot(diff, dmixed[:TR], TN)
        lane = _lane_iota((n, D_POOL))
        t = i * TR + _row_iota((n, D_POOL))
        e = ddiff / _window_count(lane, t)
        f2 = e + pltpu.roll(e, n - 1, 0)
        f4 = f2 + pltpu.roll(f2, n - 2, 0)
        f8 = f4 + pltpu.roll(f4, n - 4, 0)
        f16 = f8 + pltpu.roll(f8, n - 8, 0)
        du_ref[...] = _by_group(lane, f2, f4, f8, f16)[:TR] - ddiff[:TR]

    mat = pl.BlockSpec((D_POOL, D_POOL), lambda i: (0, 0))
    return pl.pallas_call(
        body, name="pool_bwd", grid=(nt,),
        in_specs=[_row_spec(TR, D_POOL), _prev_halo(POOL_HALO, D_POOL, 0), _row_spec(TR, D_POOL),
                  _next_halo(POOL_HALO, D_POOL, 0), mat, mat, _vec_spec(D_POOL)],
        out_specs=[_row_spec(TR, D_POOL), mat, _vec_spec(D_POOL)],
        out_shape=[jax.ShapeDtypeStruct((S, D_POOL), F32), jax.ShapeDtypeStruct((D_POOL, D_POOL), F32),
                   jax.ShapeDtypeStruct((1, D_POOL), F32)],
        compiler_params=_params(("arbitrary",)),
    )(proj, proj, dycat, dycat, w_bd, w_bd_t, ps)


def _xa_probs(q, k):
    s = _dot(q, k, NT) * (XA_DH ** -0.5)
    e = jnp.exp(s - jnp.max(s, axis=-1, keepdims=True))
    return e * (1.0 / jnp.sum(e, axis=-1, keepdims=True))


def _xattn_fwd(qx, kv):
    def body(q_ref, kv_ref, o_ref):
        for h in range(XA_HEADS):
            cols = slice(XA_DH * h, XA_DH * (h + 1))
            vcols = slice(D + XA_DH * h, D + XA_DH * (h + 1))
            p = _xa_probs(q_ref[:, cols], kv_ref[:, cols])
            o_ref[:, cols] = _dot(p.astype(BF16), kv_ref[:, vcols]).astype(BF16)

    return pl.pallas_call(
        body, name="xattn_fwd", grid=(S // TR,),
        in_specs=[_row_spec(TR, D), pl.BlockSpec((MEM, 2 * D), lambda i: (0, 0))],
        out_specs=_row_spec(TR, D), out_shape=jax.ShapeDtypeStruct((S, D), BF16),
        compiler_params=_params(("parallel",)),
    )(qx, kv)


def _xattn_bwd(qx, kv, dxo):
    def body(q_ref, kv_ref, do_ref, dq_ref, dkv_ref):
        i = pl.program_id(0)

        @pl.when(i == 0)
        def _():
            dkv_ref[...] = jnp.zeros_like(dkv_ref)

        for h in range(XA_HEADS):
            cols = slice(XA_DH * h, XA_DH * (h + 1))
            vcols = slice(D + XA_DH * h, D + XA_DH * (h + 1))
            q = q_ref[:, cols]
            k = kv_ref[:, cols]
            do = do_ref[:, cols]
            p = _xa_probs(q, k)
            dkv_ref[:, vcols] += _dot(p.astype(BF16), do, TN)
            dp = _dot(do, kv_ref[:, vcols], NT)
            ds = (p * (dp - jnp.sum(p * dp, axis=-1, keepdims=True)) * (XA_DH ** -0.5)).astype(BF16)
            dq_ref[:, cols] = _dot(ds, k).astype(BF16)
            dkv_ref[:, cols] += _dot(ds, q, TN)

    kv_spec = pl.BlockSpec((MEM, 2 * D), lambda i: (0, 0))
    return pl.pallas_call(
        body, name="xattn_bwd", grid=(S // TR,), in_specs=[_row_spec(TR, D), kv_spec, _row_spec(TR, D)],
        out_specs=[_row_spec(TR, D), kv_spec],
        out_shape=[jax.ShapeDtypeStruct((S, D), BF16), jax.ShapeDtypeStruct((MEM, 2 * D), F32)],
        compiler_params=_params(("arbitrary",)),
    )(qx, kv, dxo)


CONV_HALO = SUBLANES
TC = 512
GELU_K = 0.7978845608028654
GELU_C = 0.044715


def _conv3(ext, w, rows):
    h0 = ext[CONV_HALO:CONV_HALO + rows]
    h1 = pltpu.roll(ext, 1, 0)[CONV_HALO:CONV_HALO + rows]
    h2 = pltpu.roll(ext, 2, 0)[CONV_HALO:CONV_HALO + rows]
    return w[2:3] * h0 + w[1:2] * h1 + w[0:1] * h2 + w[3:4], (h2, h1, h0)


def _conv_specs():
    main = pl.BlockSpec((2, TR, TC), lambda j, i: (0, i, j))
    per = TR // CONV_HALO
    prev = pl.BlockSpec((2, CONV_HALO, TC), lambda j, i: (0, jnp.maximum(i * per - 1, 0), j))
    nxt = pl.BlockSpec((2, CONV_HALO, TC), lambda j, i: (0, jnp.minimum((i + 1) * per, S // CONV_HALO - 1), j))
    par = pl.BlockSpec((2, SUBLANES, TC), lambda j, i: (0, 0, j))
    return main, prev, nxt, par


def _convgate_fwd(hid, cwb):
    def body(h_ref, hp_ref, w_ref, act_ref):
        i = pl.program_id(1)
        c = []
        for g in range(2):
            ext = jnp.concatenate([jnp.where(i == 0, 0.0, hp_ref[g]), h_ref[g]], axis=0)
            c.append(_conv3(ext, w_ref[g], TR)[0])
        gate, up = c
        act_ref[...] = (jax.nn.gelu(gate, approximate=True) * up).astype(BF16)

    main, prev, _, par = _conv_specs()
    return pl.pallas_call(
        body, name="convgate_fwd", grid=(D_FF // TC, S // TR), in_specs=[main, prev, par],
        out_specs=pl.BlockSpec((TR, TC), lambda j, i: (i, j)), out_shape=jax.ShapeDtypeStruct((S, D_FF), BF16),
        compiler_params=_params(("parallel", "parallel")),
    )(hid, hid, cwb)


def _convgate_bwd(hid, dact, cwb):
    nr = S // TR
    n = TR + CONV_HALO

    def body(h_ref, hp_ref, hn_ref, da_ref, dan_ref, w_ref, dh_ref, dw_ref):
        i = pl.program_id(1)

        @pl.when(i == 0)
        def _():
            dw_ref[...] = jnp.zeros_like(dw_ref)

        da = jnp.concatenate([da_ref[...], jnp.where(i == nr - 1, 0.0, dan_ref[...])], axis=0)
        c, taps = [], []
        for g in range(2):
            ext = jnp.concatenate([jnp.where(i == 0, 0.0, hp_ref[g]), h_ref[g], hn_ref[g]], axis=0)
            cg, tg = _conv3(ext, w_ref[g], n)
            c.append(cg)
            taps.append(tg)
        gate, up = c
        th = jnp.tanh(GELU_K * (gate + GELU_C * gate * gate * gate))
        gelu = 0.5 * gate * (1.0 + th)
        dgelu = 0.5 * (1.0 + th) + 0.5 * gate * (1.0 - th * th) * GELU_K * (1.0 + 3.0 * GELU_C * gate * gate)
        for g, dc in enumerate((da * up * dgelu, da * gelu)):
            w = w_ref[g]
            dh = w[2:3] * dc[:TR] + w[1:2] * pltpu.roll(dc, n - 1, 0)[:TR] + w[0:1] * pltpu.roll(dc, n - 2, 0)[:TR]
            dh_ref[g] = dh.astype(BF16)
            dcm = dc[:TR]
            for r in range(3):
                dw_ref[g, r:r + 1, :] += jnp.sum(dcm * taps[g][r][:TR], axis=0, keepdims=True)
            dw_ref[g, 3:4, :] += jnp.sum(dcm, axis=0, keepdims=True)

    main, prev, nxt, par = _conv_specs()
    per = TR // CONV_HALO
    return pl.pallas_call(
        body, name="convgate_bwd", grid=(D_FF // TC, nr),
        in_specs=[main, prev, nxt, pl.BlockSpec((TR, TC), lambda j, i: (i, j)),
                  pl.BlockSpec((CONV_HALO, TC), lambda j, i: (jnp.minimum((i + 1) * per, S // CONV_HALO - 1), j)), par],
        out_specs=[main, par],
        out_shape=[jax.ShapeDtypeStruct((2, S, D_FF), BF16), jax.ShapeDtypeStruct((2, SUBLANES, D_FF), F32)],
        compiler_params=_params(("parallel", "arbitrary")),
    )(hid, hid, hid, dact, dact, cwb)


def _adam_update(w, g, m, v):
    m = ADAM_B1 * m + (1.0 - ADAM_B1) * g
    v = ADAM_B2 * v + (1.0 - ADAM_B2) * (g * g)
    m_hat = m / (1.0 - ADAM_B1 ** ADAM_STEP)
    v_hat = v / (1.0 - ADAM_B2 ** ADAM_STEP)
    return -ADAM_LR * (m_hat / (jnp.sqrt(v_hat) + ADAM_EPS) + ADAM_WD * w), m, v


def _row_tile(rows, cols, itemsize=4, target=TILE_BYTES):
    tr = SUBLANES
    while rows % (2 * tr) == 0 and 2 * tr * cols * itemsize <= target:
        tr *= 2
    assert rows % tr == 0, (rows, tr)
    return tr


def _adamw(name, w, g, m, v):
    rows, cols = w.shape
    tr = rows if rows * cols * 4 <= TILE_BYTES // 2 else _row_tile(rows, cols, target=TILE_BYTES // 2)

    def body(w_ref, g_ref, m_ref, v_ref, d_ref, nm_ref, nv_ref):
        d_ref[...], nm_ref[...], nv_ref[...] = _adam_update(w_ref[...], g_ref[...], m_ref[...], v_ref[...])

    spec = _row_spec(tr, cols)
    shape = jax.ShapeDtypeStruct((rows, cols), F32)
    return pl.pallas_call(
        body, name=name, grid=(rows // tr,), in_specs=[spec] * 4, out_specs=[spec] * 3, out_shape=[shape] * 3,
        compiler_params=_params(("parallel",)),
    )(w, g, m, v)


def _adamw_halves(name, core, w, g_mine, g_sibling, m, v):
    rows, cols = w.shape
    half = rows // 2
    tr = _row_tile(half, cols, target=TILE_BYTES // 2)
    per = half // tr

    def body(core_ref, w_ref, gm_ref, gs_ref, m_ref, v_ref, g_ref, d_ref, nm_ref, nv_ref):
        g = jnp.where(pl.program_id(0) // per == core_ref[0], gm_ref[...], gs_ref[...])
        g_ref[...] = g
        d_ref[...], nm_ref[...], nv_ref[...] = _adam_update(w_ref[...], g, m_ref[...], v_ref[...])

    spec = pl.BlockSpec((tr, cols), lambda i, core_ref: (i, 0))
    half_spec = pl.BlockSpec((tr, cols), lambda i, core_ref: (i % per, 0))
    shape = jax.ShapeDtypeStruct((rows, cols), F32)
    return pl.pallas_call(
        body, name=name, out_shape=[shape] * 4,
        grid_spec=pltpu.PrefetchScalarGridSpec(
            num_scalar_prefetch=1, grid=(rows // tr,), in_specs=[spec, half_spec, half_spec, spec, spec], out_specs=[spec] * 4),
        compiler_params=_params(("parallel",)),
    )(core, w, g_mine, g_sibling, m, v)


def _chip_sum(name, core, g, other):
    _, _, half, cols = g.shape
    tr = _row_tile(half, cols)

    def body(core_ref, g_ref, o_ref, p_ref):
        p_ref[...] = (g_ref[...] + o_ref[...]).astype(BF16)

    spec = pl.BlockSpec((None, tr, cols), lambda j, i, core_ref: (j, i, 0))
    return pl.pallas_call(
        body, name=name, out_shape=jax.ShapeDtypeStruct((N_CHIPS, half, cols), BF16),
        grid_spec=pltpu.PrefetchScalarGridSpec(
            num_scalar_prefetch=1, grid=(N_CHIPS, half // tr),
            in_specs=[pl.BlockSpec((None, None, tr, cols), lambda j, i, core_ref: (j, core_ref[0], i, 0)), spec],
            out_specs=spec),
        compiler_params=_params(("parallel", "parallel")),
    )(core, g, other)


def _mesh_sum(name, chip, received, own):
    _, half, cols = received.shape
    tr = _row_tile(half, cols, itemsize=2 * N_CHIPS)

    def body(chip_ref, r_ref, own_ref, o_ref):
        acc = None
        for j in range(N_CHIPS):
            term = jnp.where(chip_ref[0] == j, own_ref[...], r_ref[j]).astype(F32)
            acc = term if acc is None else acc + term
        o_ref[...] = acc

    return pl.pallas_call(
        body, name=name, out_shape=jax.ShapeDtypeStruct((half, cols), F32),
        grid_spec=pltpu.PrefetchScalarGridSpec(
            num_scalar_prefetch=1, grid=(half // tr,),
            in_specs=[pl.BlockSpec((N_CHIPS, tr, cols), lambda i, chip_ref: (0, i, 0)),
                      pl.BlockSpec((None, tr, cols), lambda i, chip_ref: (chip_ref[0], i, 0))],
            out_specs=pl.BlockSpec((tr, cols), lambda i, chip_ref: (i, 0))),
        compiler_params=_params(("parallel",)),
    )(chip, received, own)


CHIP_FLIPS = ((1, 0), (0, 1), (1, 1))


def _place():
    x, y, c = lax.axis_index("x"), lax.axis_index("y"), lax.axis_index("c")
    return x, y, c, 2 * x + y


def _remote(src, dst, sems_s, sems_r, k, dev):
    return pltpu.make_async_remote_copy(src_ref=src, dst_ref=dst, send_sem=sems_s.at[k], recv_sem=sems_r.at[k],
                                        device_id=dev, device_id_type=MESH)


class _Exchange:
    def __init__(self, ins, out_shapes, n_sems, start, forward, finish):
        self.ins, self.out_shapes, self.n_sems = list(ins), list(out_shapes), n_sems
        self.start, self.forward, self.finish = start, forward, finish

    def scratch(self):
        return [pltpu.SemaphoreType.DMA((self.n_sems,)), pltpu.SemaphoreType.DMA((self.n_sems,))]

    def run(self, name):
        n = len(self.ins)

        def body(*refs):
            args = (refs[:n], refs[n:2 * n]) + tuple(refs[2 * n:])
            self.start(*args)
            self.forward(*args)
            self.finish(*args)

        return pl.pallas_call(
            body, name=name, in_specs=[ANY] * n, out_specs=[ANY] * n, out_shape=self.out_shapes, scratch_shapes=self.scratch(),
        )(*self.ins)


def _all_gather_weights(halved, whole):
    nh, nw = len(halved), len(whole)
    n_arr = nh + nw

    def copies(ins, outs, sems_s, sems_r):
        x, y, c, me = _place()
        sibling = (x, y, 1 - c)
        own = [_remote(ins[k], outs[k].at[me], sems_s, sems_r, k, sibling) for k in range(n_arr)]
        first, passed = [], []
        for k in range(n_arr):
            for f, (fx, fy) in enumerate(CHIP_FLIPS):
                src, dst = (ins[k].at[c], outs[k].at[me, c]) if k < nh else (ins[k], outs[k].at[me])
                first.append(_remote(src, dst, sems_s, sems_r, n_arr + 3 * k + f, (x ^ fx, y ^ fy, c)))
        for k in range(nh):
            for f, (fx, fy) in enumerate(CHIP_FLIPS):
                landed = outs[k].at[2 * (x ^ fx) + (y ^ fy), c]
                passed.append(_remote(landed, landed, sems_s, sems_r, 4 * n_arr + 3 * k + f, sibling))
        return own, first, passed

    def start(*refs):
        own, first, _ = copies(*refs)
        for cp in own + first:
            cp.start()

    def forward(*refs):
        _, first, passed = copies(*refs)
        for arrived, cp in zip(first, passed):
            arrived.wait_recv()
            cp.start()

    def finish(*refs):
        own, first, passed = copies(*refs)
        for cp in first[3 * nh:] + passed + own:
            cp.wait_recv()
        for cp in first + passed + own:
            cp.wait_send()

    shapes = [jax.ShapeDtypeStruct((N_CHIPS,) + a.shape, a.dtype) for a in list(halved) + list(whole)]
    return _Exchange(list(halved) + list(whole), shapes, 7 * nh + 4 * nw, start, forward, finish)


def _swap_halves(gs):
    n = len(gs)

    def copies(ins, outs, sems_s, sems_r):
        x, y, c, _ = _place()
        return [_remote(ins[k].at[:, 1 - c], outs[k], sems_s, sems_r, k, (x, y, 1 - c)) for k in range(n)]

    def start(*refs):
        for cp in copies(*refs):
            cp.start()

    def finish(*refs):
        for cp in copies(*refs):
            cp.wait()

    shapes = [jax.ShapeDtypeStruct((g.shape[0],) + g.shape[2:], g.dtype) for g in gs]
    return _Exchange(gs, shapes, n, start, _no_copies, finish)


def _scatter_chips(ps):
    n = len(ps)

    def copies(ins, outs, sems_s, sems_r):
        x, y, c, me = _place()
        return [_remote(ins[k].at[2 * (x ^ fx) + (y ^ fy)], outs[k].at[me], sems_s, sems_r, 3 * k + f, (x ^ fx, y ^ fy, c))
                for k in range(n) for f, (fx, fy) in enumerate(CHIP_FLIPS)]

    def start(*refs):
        for cp in copies(*refs):
            cp.start()

    def forward(*refs):
        pass

    def finish(*refs):
        for cp in copies(*refs):
            cp.wait()

    shapes = [jax.ShapeDtypeStruct(p.shape, p.dtype) for p in ps]
    return _Exchange(ps, shapes, 3 * n, start, forward, finish)


def _swap_reduced(rs):
    n = len(rs)

    def copies(ins, outs, sems_s, sems_r):
        x, y, c, _ = _place()
        return [_remote(ins[k], outs[k], sems_s, sems_r, k, (x, y, 1 - c)) for k in range(n)]

    def start(*refs):
        for cp in copies(*refs):
            cp.start()

    def finish(*refs):
        for cp in copies(*refs):
            cp.wait()

    return _Exchange(rs, [jax.ShapeDtypeStruct(r.shape, r.dtype) for r in rs], n, start, _no_copies, finish)


N_DEV = 8


def _gather_small(buf):
    def copies(ins, outs, sems_s, sems_r):
        x, y, c, _ = _place()
        me = 4 * x + 2 * y + c
        return [_remote(ins[0], outs[0].at[me], sems_s, sems_r, o - 1, (x ^ (o >> 2), y ^ ((o >> 1) & 1), c ^ (o & 1)))
                for o in range(1, N_DEV)]

    def start(*refs):
        for cp in copies(*refs):
            cp.start()

    def finish(*refs):
        for cp in copies(*refs):
            cp.wait()

    return _Exchange([buf], [jax.ShapeDtypeStruct((N_DEV,) + buf.shape, buf.dtype)], N_DEV - 1, start, _no_copies, finish)


def _sum_devices(place, gathered, own):
    rows = own.shape[0]

    def body(place_ref, g_ref, own_ref, o_ref):
        acc = None
        for d in range(N_DEV):
            term = jnp.where(place_ref[0] == d, own_ref[...], g_ref[d])
            acc = term if acc is None else acc + term
        o_ref[...] = acc

    return pl.pallas_call(
        body, name="sum_devices", out_shape=jax.ShapeDtypeStruct((rows, LANES), F32),
        grid_spec=pltpu.PrefetchScalarGridSpec(
            num_scalar_prefetch=1, grid=(1,),
            in_specs=[pl.BlockSpec((N_DEV, rows, LANES), lambda i, place_ref: (0, 0, 0)),
                      pl.BlockSpec((rows, LANES), lambda i, place_ref: (0, 0))],
            out_specs=pl.BlockSpec((rows, LANES), lambda i, place_ref: (0, 0))),
        compiler_params=_params(("arbitrary",)),
    )(place, gathered, own)


def _no_copies(*refs):
    pass


def _no_exchange():
    return _Exchange([], [], 1, _no_copies, _no_copies, _no_copies)


class _NoComm:
    def gather_rest(self, p):
        return _no_exchange()

    def weights_landed(self, p, landed):
        pass

    def swap_first(self, g):
        return _no_exchange()

    def first_swapped(self, landed):
        pass

    def scatter_early(self, g):
        return _no_exchange()

    def scatter_landed(self, landed):
        pass

    def swap_reduced_early(self):
        return _no_exchange()

    def reduced_landed(self, landed):
        pass

    def scatter_late(self, g):
        return _no_exchange()

    def late_landed(self, landed):
        pass


def _local_step(x, mem, target, p, comm):
    h1 = _norm_fwd("norm_mix_pre", x, p["norm_mix_pre"])
    proj = _mm_nn("in_proj", h1, p["w_in"], F32, 1024, 896)
    qa, ka, va = _fox_prep(proj, p["bf_pad"])
    ycat, qab, landed = _fox_fwd(qa, ka, va, comm.gather_rest(p))
    comm.weights_landed(p, landed)
    ycat = _pool_fwd(proj, p["w_pool_bd"], p["pool_scale"], ycat)
    y1, x2, h2, qx = _proj_resid_norm("mix_out", ycat, p["w_mix_out"], x, p["norm_mix_post"], p["norm_xa_pre"], p["w_xq"])
    mem_n = _norm_fwd("norm_mem", mem, p["norm_mem"])
    kv = _mm(
        "xkv", mem_n, p["w_xkv"], pl.BlockSpec((MEM, D), lambda i, j, k: (0, 0)),
        pl.BlockSpec((None, D, 512), lambda i, j, k: (j, 0, 0)), jax.ShapeDtypeStruct((MEM, 2 * D), BF16),
        pl.BlockSpec((MEM, 512), lambda i, j, k: (0, j)), (1, N_CHIPS, 1), NN, (MEM, 512))
    xo = _xattn_fwd(qx, kv)
    y2, x3, h3 = _proj_resid_norm("xo", xo, p["w_xo"], x2, p["norm_xa_post"], p["norm_ffn_pre"])
    hid = _mm(
        "up_proj", h3, p["w_up"], pl.BlockSpec((1024, D), lambda i, j, k: (i, 0)),
        pl.BlockSpec((None, D, 1024), lambda i, j, k: (j // 2, 0, j % 2)), jax.ShapeDtypeStruct((2, S, D_FF), F32),
        pl.BlockSpec((None, 1024, 1024), lambda i, j, k: (j // 4, i, j % 4)), (S // 1024, 8, 1), NN, (1024, 1024))
    act = _convgate_fwd(hid, p["cwb"])

    g = {}
    dres, dy3, g["norm_ffn_post"], loss_cols = _down_loss_bwd(act, p["w_down"], x3, p["norm_ffn_post"], target)
    dact = _mm_nt("d_act", dy3, p["w_down"], F32, 1024, 1024)
    g["w_down"] = _mm_tn("dw_down", act, dy3, 512, 512)
    dhid, dcwb = _convgate_bwd(hid, dact, p["cwb"])
    g["w_up"] = _mm(
        "dw_up", h3, dhid, pl.BlockSpec((S, 512), lambda i, j, k: (0, i)),
        pl.BlockSpec((None, S, 512), lambda i, j, k: (j // 8, 0, j % 8)), jax.ShapeDtypeStruct((N_CHIPS, D, 2048), F32),
        pl.BlockSpec((None, 512, 512), lambda i, j, k: (j // 4, i, j % 4)), (2, 16, 1), TN, (512, 512))
    dh3, landed = _d_h3(dhid, p["w_up"], comm.swap_first(g))
    comm.first_swapped(landed)
    dres, dy2, g["norm_ffn_pre"], g["norm_xa_post"] = _mid_bwd("bwd_ffn_xa", dres, x3, p["norm_ffn_pre"], dh3, y2, p["norm_xa_post"])
    dxo = _mm_nt("d_xo", dy2, p["w_xo"], BF16, 1024, 1024)
    g["w_xo"] = _mm_tn("dw_xo", xo, dy2, 512, 512)
    dqx, dkv = _xattn_bwd(qx, kv, dxo)
    dkv = dkv.astype(BF16)
    g["w_xq"] = _mm_tn("dw_xq", h2, dqx, 512, 512)
    dmem_n = _mm(
        "d_mem", dkv, p["w_xkv"], pl.BlockSpec((MEM, 512), lambda i, j, k: (0, k)),
        pl.BlockSpec((None, D, 512), lambda i, j, k: (k, 0, 0)), jax.ShapeDtypeStruct((MEM, D), F32),
        pl.BlockSpec((MEM, D), lambda i, j, k: (0, 0)), (1, 1, N_CHIPS), NT, (MEM, D))
    g["w_xkv"] = _mm(
        "dw_xkv", mem_n, dkv, pl.BlockSpec((MEM, D), lambda i, j, k: (0, 0)),
        pl.BlockSpec((MEM, 512), lambda i, j, k: (0, j)), jax.ShapeDtypeStruct((N_CHIPS, D, 512), F32),
        pl.BlockSpec((None, D, 512), lambda i, j, k: (j, 0, 0)), (1, N_CHIPS, 1), TN, (D, 512))
    g["norm_mem"] = _gain_bwd("dg_mem", mem, p["norm_mem"], dmem_n)
    dres, dy1, g["norm_xa_pre"], g["norm_mix_post"], dy_pool, doa = _bwd_xa_mix(
        dqx, p["w_xq"], dres, x2, p["norm_xa_pre"], y1, p["norm_mix_post"], p["w_mix_out"], ycat)
    g["w_mix_out"] = _mm_tn("dw_mix_out", ycat, dy1, 512, 512)
    dqa, dka, dva, landed = _fox_bwd(qab, doa, ka, va, comm.scatter_early(g))
    comm.scatter_landed(landed)
    du, g["w_pool_full"], g["pool_scale"] = _pool_bwd(proj, dy_pool, p["w_pool_bd"], p["w_pool_bd_t"], p["pool_scale"])
    dproj, g["bf_pad"] = _fox_bwd_post(dqa, dka, dva, du, proj, p["bf_pad"])
    g["w_in"], landed = _mm_tn("dw_in", h1, dproj, 512, 896, comm.swap_reduced_early())
    comm.reduced_landed(landed)
    dh1, landed = _mm_nt("d_h1", dproj, p["w_in"], F32, 1024, 1024, comm.scatter_late(g))
    comm.late_landed(landed)
    grad_x, g["norm_mix_pre"] = _first_bwd(dres, x, p["norm_mix_pre"], dh1)
    g["cwb"] = dcwb
    return grad_x, g, loss_cols


BIG = ("w_in", "w_mix_out", "w_xq", "w_xkv", "w_xo", "w_up", "w_down")
ROW_SHARDED = ("w_mix_out", "w_xq", "w_xo", "w_down")
SMALL = ("norm_mix_pre", "norm_mix_post", "b_forget", "w_pool", "pool_scale", "norm_mem", "norm_xa_pre", "norm_xa_post",
         "norm_ffn_pre", "norm_ffn_post", "conv_b")
ORDER = ("norm_mix_pre", "norm_mix_post", "w_in", "b_forget", "w_pool", "pool_scale", "w_mix_out", "norm_mem", "norm_xa_pre",
         "norm_xa_post", "w_xq", "w_xkv", "w_xo", "norm_ffn_pre", "norm_ffn_post", "w_up", "conv_w", "conv_b", "w_down")
SLOT = SUBLANES * LANES


def _pack(parts):
    rows, offs, off = [], [], 0
    for a in parts:
        flat = a.reshape(-1).astype(F32)
        n = -(-flat.shape[0] // SLOT) * SLOT
        rows.append(jnp.pad(flat, (0, n - flat.shape[0])).reshape(n // LANES, LANES))
        offs.append(off)
        off += n // LANES
    return jnp.concatenate(rows, axis=0), offs


def _unpack(buf, off, like):
    n = like.size
    rows = -(-n // LANES)
    return buf[off:off + rows].reshape(-1)[:n].reshape(like.shape)


FIRST = ("w_in",)
REST = ("w_mix_out", "w_xq", "w_xkv", "w_xo", "w_up", "w_down")


def _first_params(w, full):
    w_in_full = jnp.pad(jnp.concatenate(list(full["w_in"]), axis=1), ((0, 0), (0, D_IN_PAD - D_IN)))
    w_pool_bd = jnp.zeros((D_POOL, D_POOL), F32)
    for gi in range(4):
        w_pool_bd = w_pool_bd.at[64 * gi:64 * (gi + 1), 64 * gi:64 * (gi + 1)].set(w["w_pool"][0, gi])
    p = {n: w[n] for n in ("norm_mix_pre", "norm_mix_post", "norm_mem", "norm_xa_pre", "norm_xa_post", "norm_ffn_pre",
                           "norm_ffn_post")}
    p.update(
        w_in=w_in_full, bf_pad=jnp.pad(w["b_forget"], ((0, 0), (0, LANES - HEADS))),
        w_pool_bd=w_pool_bd.astype(BF16), w_pool_bd_t=w_pool_bd.T.astype(BF16), pool_scale=w["pool_scale"].reshape(1, D_POOL))
    return p


def _rest_params(w, full, conv_w_full):
    cw2 = conv_w_full.reshape(3, 2, D_FF).transpose(1, 0, 2)
    cwb = jnp.concatenate([cw2, w["conv_b"].reshape(1, 2, D_FF).transpose(1, 0, 2), jnp.zeros((2, 4, D_FF), F32)], axis=1)
    return dict(w_mix_out=full["w_mix_out"].reshape(D, D), w_xq=full["w_xq"].reshape(D, D), w_xkv=full["w_xkv"],
                w_xo=full["w_xo"].reshape(D, D), w_up=full["w_up"], cwb=cwb, w_down=full["w_down"].reshape(D_FF, D))


def _whole_params(w, full, conv_w_full):
    p = _first_params(w, full)
    p.update(_rest_params(w, full, conv_w_full))
    return p


def _halved(a):
    return a.reshape(a.shape[:-2] + (2, a.shape[-2] // 2, a.shape[-1]))


class _StepComm:
    def __init__(self, w, shard2d, conv_w, core_id, chip_id):
        self.w, self.shard2d, self.conv_w, self.core_id, self.chip_id = w, shard2d, conv_w, core_id, chip_id
        self.early = REST
        self.partial = self.received = None

    def gather_rest(self, p):
        return _all_gather_weights([_halved(self.shard2d[n].astype(BF16)) for n in REST], [self.conv_w.reshape(3, -1)])

    def weights_landed(self, p, landed):
        full = {n: a.reshape((N_CHIPS,) + self.shard2d[n].shape) for n, a in zip(REST, landed)}
        conv_w_full = jnp.transpose(landed[-1], (1, 0, 2)).reshape(3, 2 * D_FF)
        p.update(_rest_params(self.w, full, conv_w_full))

    def _view(self, g, n):
        return _halved(g[n].reshape((N_CHIPS,) + self.shard2d[n].shape))

    def swap_first(self, g):
        self.first = ("w_up", "w_down")
        return _swap_halves([self._view(g, n) for n in self.first])

    def first_swapped(self, landed):
        self.from_sibling = dict(zip(self.first, landed))

    def scatter_early(self, g):
        others = [n for n in self.early if n not in self.first]
        self.from_sibling.update(zip(others, _swap_halves([self._view(g, n) for n in others]).run("swap_halves")))
        self.partial = [_chip_sum("chip_sum_" + n, self.core_id, self._view(g, n), self.from_sibling[n]) for n in self.early]
        return _scatter_chips(self.partial)

    def scatter_landed(self, landed):
        self.received = list(landed)

    def swap_reduced_early(self):
        self.reduced = [_mesh_sum("mesh_sum_" + n, self.chip_id, r, own)
                        for n, r, own in zip(self.early, self.received, self.partial)]
        return _swap_reduced(self.reduced)

    def reduced_landed(self, landed):
        self.reduced_sibling = list(landed)

    def scatter_late(self, g):
        gw_in = g["w_in"][:, :D_IN]
        cols = D_IN // N_CHIPS
        view = _halved(jnp.stack([gw_in[:, cols * j:cols * (j + 1)] for j in range(N_CHIPS)]))
        self.partial_in = _chip_sum("chip_sum_w_in", self.core_id, view, _swap_halves([view]).run("swap_halves_w_in")[0])
        return _scatter_chips([self.partial_in])

    def late_landed(self, landed):
        self.received_in = landed[0]


def kernel(x, mem, norm_mix_pre, norm_mix_post, w_in, b_forget, w_pool, pool_scale, w_mix_out, norm_mem, norm_xa_pre, norm_xa_post, w_xq, w_xkv, w_xo, norm_ffn_pre, norm_ffn_post, w_up, conv_w, conv_b, w_down, loss_target, m_norm_mix_pre, m_norm_mix_post, m_w_in, m_b_forget, m_w_pool, m_pool_scale, m_w_mix_out, m_norm_mem, m_norm_xa_pre, m_norm_xa_post, m_w_xq, m_w_xkv, m_w_xo, m_norm_ffn_pre, m_norm_ffn_post, m_w_up, m_conv_w, m_conv_b, m_w_down, v_norm_mix_pre, v_norm_mix_post, v_w_in, v_b_forget, v_w_pool, v_pool_scale, v_w_mix_out, v_norm_mem, v_norm_xa_pre, v_norm_xa_post, v_w_xq, v_w_xkv, v_w_xo, v_norm_ffn_pre, v_norm_ffn_post, v_w_up, v_conv_w, v_conv_b, v_w_down):
    w = dict(norm_mix_pre=norm_mix_pre, norm_mix_post=norm_mix_post, w_in=w_in, b_forget=b_forget, w_pool=w_pool,
             pool_scale=pool_scale, w_mix_out=w_mix_out, norm_mem=norm_mem, norm_xa_pre=norm_xa_pre, norm_xa_post=norm_xa_post,
             w_xq=w_xq, w_xkv=w_xkv, w_xo=w_xo, norm_ffn_pre=norm_ffn_pre, norm_ffn_post=norm_ffn_post, w_up=w_up,
             conv_w=conv_w, conv_b=conv_b, w_down=w_down)
    m = dict(norm_mix_pre=m_norm_mix_pre, norm_mix_post=m_norm_mix_post, w_in=m_w_in, b_forget=m_b_forget, w_pool=m_w_pool,
             pool_scale=m_pool_scale, w_mix_out=m_w_mix_out, norm_mem=m_norm_mem, norm_xa_pre=m_norm_xa_pre,
             norm_xa_post=m_norm_xa_post, w_xq=m_w_xq, w_xkv=m_w_xkv, w_xo=m_w_xo, norm_ffn_pre=m_norm_ffn_pre,
             norm_ffn_post=m_norm_ffn_post, w_up=m_w_up, conv_w=m_conv_w, conv_b=m_conv_b, w_down=m_w_down)
    v = dict(norm_mix_pre=v_norm_mix_pre, norm_mix_post=v_norm_mix_post, w_in=v_w_in, b_forget=v_b_forget, w_pool=v_w_pool,
             pool_scale=v_pool_scale, w_mix_out=v_w_mix_out, norm_mem=v_norm_mem, norm_xa_pre=v_norm_xa_pre,
             norm_xa_post=v_norm_xa_post, w_xq=v_w_xq, w_xkv=v_w_xkv, w_xo=v_w_xo, norm_ffn_pre=v_norm_ffn_pre,
             norm_ffn_post=v_norm_ffn_post, w_up=v_w_up, conv_w=v_conv_w, conv_b=v_conv_b, w_down=v_w_down)
    chip = 2 * lax.axis_index("x") + lax.axis_index("y")

    core_id = lax.axis_index("c").astype(jnp.int32).reshape(1)
    chip_id = chip.astype(jnp.int32).reshape(1)

    shard2d = {n: w[n][0] for n in BIG}
    gathered = _all_gather_weights([_halved(shard2d[n].astype(BF16)) for n in FIRST], []).run("all_gather_first")
    p = _first_params(w, {n: a.reshape((N_CHIPS,) + shard2d[n].shape) for n, a in zip(FIRST, gathered)})

    comm = _StepComm(w, shard2d, conv_w, core_id, chip_id)
    grad_x, g, loss_cols = _local_step(x[0], mem[0], loss_target[0], p, comm)

    reduced_in = _mesh_sum("mesh_sum_w_in", chip_id, comm.received_in, comm.partial_in)
    names = ("w_in",) + comm.early
    reduced = [reduced_in] + comm.reduced
    reduced_sibling = list(_swap_reduced([reduced_in]).run("swap_reduced_w_in")) + comm.reduced_sibling
    grads = {}

    gw_pool = jnp.stack([g["w_pool_full"][64 * gi:64 * (gi + 1), 64 * gi:64 * (gi + 1)] for gi in range(4)])
    dcwb = g["cwb"]
    g_conv_w = dcwb[:, 0:3, :].transpose(1, 0, 2).reshape(3, 2 * D_FF)
    g_conv_b = dcwb[:, 3, :].reshape(2 * D_FF)
    small_g = dict(norm_mix_pre=g["norm_mix_pre"], norm_mix_post=g["norm_mix_post"], b_forget=g["bf_pad"][:, :HEADS],
                   w_pool=gw_pool, pool_scale=g["pool_scale"], norm_mem=g["norm_mem"], norm_xa_pre=g["norm_xa_pre"],
                   norm_xa_post=g["norm_xa_post"], norm_ffn_pre=g["norm_ffn_pre"], norm_ffn_post=g["norm_ffn_post"],
                   conv_b=g_conv_b)
    local_buf, offs = _pack([small_g[n] for n in SMALL] + [g_conv_w, loss_cols])

    delta, new_m, new_v = {}, {}, {}
    for n, g_mine, g_sibling in zip(names, reduced, reduced_sibling):
        gn, d, nm, nv = _adamw_halves("adamw_" + n, core_id, shard2d[n], g_mine, g_sibling, m[n][0], v[n][0])
        grads[n], delta[n], new_m[n], new_v[n] = gn[None], d[None], nm[None], nv[None]
    place = (2 * chip + lax.axis_index("c")).astype(jnp.int32).reshape(1)
    buf = _sum_devices(place, _gather_small(local_buf).run("gather_small")[0], local_buf)
    for n, off in zip(SMALL, offs):
        grads[n] = _unpack(buf, off, w[n])
    g_conv_w = _unpack(buf, offs[len(SMALL)], g_conv_w)
    grads["conv_w"] = lax.dynamic_slice_in_dim(g_conv_w, chip * (2 * D_FF // N_CHIPS), 2 * D_FF // N_CHIPS, axis=1).reshape(conv_w.shape)
    loss = jnp.sum(_unpack(buf, offs[len(SMALL) + 1], loss_cols))
    small_names = SMALL + ("conv_w",)
    packed = [_pack([d[n] for n in small_names])[0] for d in (w, grads, m, v)]
    offs = _pack([w[n] for n in small_names])[1]
    d, nm, nv = _adamw("adamw_small", *packed)
    for n, off in zip(small_names, offs):
        delta[n], new_m[n], new_v[n] = _unpack(d, off, w[n]), _unpack(nm, off, w[n]), _unpack(nv, off, w[n])

    return (loss, grad_x[None], *[grads[n] for n in ORDER], *[delta[n] for n in ORDER], *[new_m[n] for n in ORDER],
            *[new_v[n] for n in ORDER])
```

```python
import functools

import jax
import jax.numpy as jnp
from jax import lax
from jax.experimental import pallas as pl
from jax.experimental.pallas import tpu as pltpu

F32 = jnp.float32
BF16 = jnp.bfloat16
MESH = pl.DeviceIdType.MESH
ANY = pl.BlockSpec(memory_space=pl.ANY)
VMEM_SPEC = pl.BlockSpec(memory_space=pltpu.VMEM)

S = 4096
D = 1024
MEM = 256
D_POOL = 256
HEADS = 12
DH = 64
D_FOX = HEADS * DH
D_IN = D_POOL + 3 * D_FOX + HEADS
F_OFF = D_POOL + 3 * D_FOX
Q_OFF, K_OFF, V_OFF = D_POOL, D_POOL + D_FOX, D_POOL + 2 * D_FOX
XA_HEADS = 4
XA_DH = 256
D_FF = 4096
EPS = 1e-6
N_CHIPS = 4
ADAM_LR, ADAM_B1, ADAM_B2, ADAM_EPS, ADAM_WD, ADAM_STEP = 0.001, 0.9, 0.999, 1e-08, 0.01, 10

LANES = 128
SUBLANES = 8
D_IN_PAD = 21 * LANES
TR = 512
TILE_BYTES = 2 * 1024 * 1024
NEG = -1e30
VMEM_LIMIT = 52 * 1024 * 1024

NN = (((1,), (0,)), ((), ()))
NT = (((1,), (1,)), ((), ()))
TN = (((0,), (0,)), ((), ()))


def _dot(a, b, dims=NN):
    return lax.dot_general(a, b, dims, preferred_element_type=F32)


def _params(sem):
    return pltpu.CompilerParams(dimension_semantics=sem, vmem_limit_bytes=VMEM_LIMIT)


def _split3(x):
    hi = x.astype(BF16)
    r = x - hi.astype(F32)
    mid = r.astype(BF16)
    lo = (r - mid.astype(F32)).astype(BF16)
    return hi, mid, lo


def _split3_f32(x):
    hi = x.astype(BF16).astype(F32)
    r = x - hi
    mid = r.astype(BF16).astype(F32)
    return hi, mid, r - mid


def _lane_iota(shape):
    return lax.broadcasted_iota(jnp.int32, shape, len(shape) - 1)


def _row_iota(shape):
    return lax.broadcasted_iota(jnp.int32, shape, len(shape) - 2)


def _mm(name, a, b, a_spec, b_spec, out_shape, out_spec, grid, dims, acc_shape, ex=None):
    nk = grid[2]
    if ex is not None:
        return _mm_hosting(name, a, b, a_spec, b_spec, out_shape, out_spec, grid, dims, ex)

    def body(a_ref, b_ref, o_ref, *scr):
        p = _dot(a_ref[...], b_ref[...], dims)
        if nk == 1:
            o_ref[...] = p.astype(o_ref.dtype)
        else:
            acc = scr[0]
            k = pl.program_id(2)

            @pl.when(k == 0)
            def _():
                acc[...] = p

            @pl.when(k > 0)
            def _():
                acc[...] += p

            @pl.when(k == nk - 1)
            def _():
                o_ref[...] = acc[...].astype(o_ref.dtype)

    return pl.pallas_call(
        body, name=name, grid=grid, in_specs=[a_spec, b_spec], out_specs=out_spec, out_shape=out_shape,
        scratch_shapes=[pltpu.VMEM(acc_shape, F32)] if nk > 1 else [],
        compiler_params=_params(("parallel", "parallel", "arbitrary")),
    )(a, b)


def _mm_hosting(name, a, b, a_spec, b_spec, out_shape, out_spec, grid, dims, ex):
    assert grid[2] == 1
    n = len(ex.ins)

    def body(*refs):
        i, j = pl.program_id(0), pl.program_id(1)
        first = (i == 0) & (j == 0)
        (a_ref, b_ref), (o_ref,), _, begin, end = _hosted(
            ex, refs, 2, 1, first, first, (i == grid[0] - 1) & (j == grid[1] - 1))
        begin()
        o_ref[...] = _dot(a_ref[...], b_ref[...], dims).astype(o_ref.dtype)
        end()

    res = pl.pallas_call(
        body, name=name, grid=grid, in_specs=[a_spec, b_spec] + [ANY] * n, out_specs=[out_spec] + [ANY] * n,
        out_shape=[out_shape] + ex.out_shapes, scratch_shapes=ex.scratch(),
        compiler_params=_params(("arbitrary", "arbitrary", "arbitrary")),
    )(a, b, *ex.ins)
    return res[0], res[1:]


def _mm_nn(name, a, b, out_dtype, tm, tn):
    m, k = a.shape
    n = b.shape[1]
    return _mm(name, a, b, pl.BlockSpec((tm, k), lambda i, j, kk: (i, 0)), pl.BlockSpec((k, tn), lambda i, j, kk: (0, j)),
               jax.ShapeDtypeStruct((m, n), out_dtype), pl.BlockSpec((tm, tn), lambda i, j, kk: (i, j)),
               (m // tm, n // tn, 1), NN, (tm, tn))


def _mm_nt(name, a, b, out_dtype, tm, tn, ex=None):
    m, k = a.shape
    n = b.shape[0]
    return _mm(name, a, b, pl.BlockSpec((tm, k), lambda i, j, kk: (i, 0)), pl.BlockSpec((tn, k), lambda i, j, kk: (j, 0)),
               jax.ShapeDtypeStruct((m, n), out_dtype), pl.BlockSpec((tm, tn), lambda i, j, kk: (i, j)),
               (m // tm, n // tn, 1), NT, (tm, tn), ex)


def _mm_tn(name, a, b, tka, tn, ex=None):
    t, ka = a.shape
    n = b.shape[1]
    return _mm(name, a, b, pl.BlockSpec((t, tka), lambda i, j, kk: (0, i)), pl.BlockSpec((t, tn), lambda i, j, kk: (0, j)),
               jax.ShapeDtypeStruct((ka, n), F32), pl.BlockSpec((tka, tn), lambda i, j, kk: (i, j)),
               (ka // tka, n // tn, 1), TN, (tka, tn), ex)


def _d_h3(dhid, w_up, ex):
    tm = tn = 512
    shard = 2 * D_FF // N_CHIPS
    per_plane = D_FF // shard
    grid = (S // tm, D // tn)
    n = len(ex.ins)

    def body(*refs):
        i, j = pl.program_id(0), pl.program_id(1)
        first = (i == 0) & (j == 0)
        (a_ref, b_ref), (o_ref,), _, begin, end = _hosted(ex, refs, 2, 1, first, first, (i == grid[0] - 1) & (j == grid[1] - 1))
        begin()
        acc = None
        for k in range(N_CHIPS):
            cols = slice(shard * (k % per_plane), shard * (k % per_plane + 1))
            part = _dot(a_ref[k // per_plane, :, cols], b_ref[k], NT)
            acc = part if acc is None else acc + part
        o_ref[...] = acc
        end()

    res = pl.pallas_call(
        body, name="d_h3", grid=grid,
        in_specs=[pl.BlockSpec((2, tm, D_FF), lambda i, j: (0, i, 0)),
                  pl.BlockSpec((N_CHIPS, tn, shard), lambda i, j: (0, j, 0))] + [ANY] * n,
        out_specs=[pl.BlockSpec((tm, tn), lambda i, j: (i, j))] + [ANY] * n,
        out_shape=[jax.ShapeDtypeStruct((S, D), F32)] + ex.out_shapes, scratch_shapes=ex.scratch(),
        compiler_params=_params(("arbitrary", "arbitrary")),
    )(dhid, w_up, *ex.ins)
    return res[0], res[1:]


def _rms(x, g):
    r = lax.rsqrt(jnp.mean(x * x, axis=-1, keepdims=True) + EPS)
    return x * r * g


def _rms_bwd(x, g, dy):
    r = lax.rsqrt(jnp.mean(x * x, axis=-1, keepdims=True) + EPS)
    xh = x * r
    dxh = dy * g
    dx = r * (dxh - xh * jnp.mean(dxh * xh, axis=-1, keepdims=True))
    return dx, jnp.sum(dy * xh, axis=0, keepdims=True)


def _row_spec(tr, width):
    return pl.BlockSpec((tr, width), lambda i: (i, 0))


def _vec_spec(width):
    return pl.BlockSpec((1, width), lambda i: (0, 0))


def _norm_fwd(name, x, g, ex=None):
    rows, width = x.shape
    tr = min(TR, rows)
    steps = rows // tr
    hosted = ex if ex is not None else _no_exchange()
    n = len(hosted.ins)

    def body(*refs):
        i = pl.program_id(0)
        (x_ref, g_ref), (h_ref,), _, begin, end = _hosted(hosted, refs, 2, 1, i == 0, i == 0, i == steps - 1)
        begin()
        h_ref[...] = _rms(x_ref[...], g_ref[...]).astype(BF16)
        end()

    res = pl.pallas_call(
        body, name=name, grid=(steps,), in_specs=[_row_spec(tr, width), _vec_spec(width)] + [ANY] * n,
        out_specs=[_row_spec(tr, width)] + [ANY] * n,
        out_shape=[jax.ShapeDtypeStruct((rows, width), BF16)] + hosted.out_shapes, scratch_shapes=hosted.scratch(),
        compiler_params=_params(("arbitrary",)),
    )(x, g, *hosted.ins)
    return res[0] if ex is None else (res[0], res[1:])


def _proj_resid_norm(name, a, w, xp, g_post, g_pre, w_next=None):
    def body(a_ref, w_ref, xp_ref, gpost_ref, gpre_ref, *rest):
        y_ref, xn_ref, h_ref = rest[-3:] if w_next is None else rest[1:4]
        y = _dot(a_ref[...], w_ref[...])
        y_ref[...] = y
        xn = xp_ref[...] + _rms(y, gpost_ref[...])
        xn_ref[...] = xn
        h = _rms(xn, gpre_ref[...]).astype(BF16)
        h_ref[...] = h
        if w_next is not None:
            rest[4][...] = _dot(h, rest[0][...]).astype(BF16)

    mat = pl.BlockSpec((D, D), lambda i: (0, 0))
    more = [] if w_next is None else [w_next]
    return pl.pallas_call(
        body, name=name, grid=(S // TR,),
        in_specs=[_row_spec(TR, D), mat, _row_spec(TR, D), _vec_spec(D), _vec_spec(D)] + [mat] * len(more),
        out_specs=[_row_spec(TR, D)] * (3 + len(more)),
        out_shape=[jax.ShapeDtypeStruct((S, D), F32), jax.ShapeDtypeStruct((S, D), F32), jax.ShapeDtypeStruct((S, D), BF16)]
        + [jax.ShapeDtypeStruct((S, D), BF16)] * len(more),
        compiler_params=_params(("parallel",)),
    )(a, w, xp, g_post, g_pre, *more)


def _down_loss_bwd(act, w_down, x3, g_post, target):
    def body(a_ref, w_ref, x_ref, g_ref, t_ref, dres_ref, dy_ref, dg_ref, loss_ref):
        i = pl.program_id(0)

        @pl.when(i == 0)
        def _():
            dg_ref[...] = jnp.zeros_like(dg_ref)
            loss_ref[...] = jnp.zeros_like(loss_ref)

        y = _dot(a_ref[...], w_ref[...])
        g = g_ref[...]
        e = x_ref[...] + _rms(y, g) - t_ref[...]
        loss_ref[...] += jnp.sum(e * e, axis=0, keepdims=True) * (0.5 / D)
        dres = e * (1.0 / D)
        dres_ref[...] = dres
        dy, dg = _rms_bwd(y, g, dres)
        dy_ref[...] = dy.astype(BF16)
        dg_ref[...] += dg

    return pl.pallas_call(
        body, name="down_loss_bwd", grid=(S // TR,),
        in_specs=[_row_spec(TR, D_FF), pl.BlockSpec((D_FF, D), lambda i: (0, 0)), _row_spec(TR, D), _vec_spec(D),
                  _row_spec(TR, D)],
        out_specs=[_row_spec(TR, D), _row_spec(TR, D), _vec_spec(D), _vec_spec(D)],
        out_shape=[jax.ShapeDtypeStruct((S, D), F32), jax.ShapeDtypeStruct((S, D), BF16),
                   jax.ShapeDtypeStruct((1, D), F32), jax.ShapeDtypeStruct((1, D), F32)],
        compiler_params=_params(("arbitrary",)),
    )(act, w_down, x3, g_post, target)


def _mid_bwd(name, dres, xcur, g_pre, dh, yprev, g_post):
    def body(dres_ref, x_ref, gpre_ref, dh_ref, y_ref, gpost_ref, dx_ref, dy_ref, dgpre_ref, dgpost_ref):
        i = pl.program_id(0)

        @pl.when(i == 0)
        def _():
            dgpre_ref[...] = jnp.zeros_like(dgpre_ref)
            dgpost_ref[...] = jnp.zeros_like(dgpost_ref)

        dxn, dgpre = _rms_bwd(x_ref[...], gpre_ref[...], dh_ref[...])
        dx = dres_ref[...] + dxn
        dx_ref[...] = dx
        dy, dgpost = _rms_bwd(y_ref[...], gpost_ref[...], dx)
        dy_ref[...] = dy.astype(BF16)
        dgpre_ref[...] += dgpre
        dgpost_ref[...] += dgpost

    return pl.pallas_call(
        body, name=name, grid=(S // TR,),
        in_specs=[_row_spec(TR, D), _row_spec(TR, D), _vec_spec(D), _row_spec(TR, D), _row_spec(TR, D), _vec_spec(D)],
        out_specs=[_row_spec(TR, D), _row_spec(TR, D), _vec_spec(D), _vec_spec(D)],
        out_shape=[jax.ShapeDtypeStruct((S, D), F32), jax.ShapeDtypeStruct((S, D), BF16),
                   jax.ShapeDtypeStruct((1, D), F32), jax.ShapeDtypeStruct((1, D), F32)],
        compiler_params=_params(("arbitrary",)),
    )(dres, xcur, g_pre, dh, yprev, g_post)


def _first_bwd(dres, x, g, dh):
    def body(dres_ref, x_ref, g_ref, dh_ref, dx_ref, dg_ref):
        i = pl.program_id(0)

        @pl.when(i == 0)
        def _():
            dg_ref[...] = jnp.zeros_like(dg_ref)

        dxn, dg = _rms_bwd(x_ref[...], g_ref[...], dh_ref[...])
        dx_ref[...] = dres_ref[...] + dxn
        dg_ref[...] += dg

    return pl.pallas_call(
        body, name="first_bwd", grid=(S // TR,),
        in_specs=[_row_spec(TR, D), _row_spec(TR, D), _vec_spec(D), _row_spec(TR, D)],
        out_specs=[_row_spec(TR, D), _vec_spec(D)],
        out_shape=[jax.ShapeDtypeStruct((S, D), F32), jax.ShapeDtypeStruct((1, D), F32)],
        compiler_params=_params(("arbitrary",)),
    )(dres, x, g, dh)


def _gain_bwd(name, x, g, dy):
    rows, width = x.shape

    def body(x_ref, g_ref, dy_ref, dg_ref):
        _, dg = _rms_bwd(x_ref[...], g_ref[...], dy_ref[...])
        dg_ref[...] = dg

    return pl.pallas_call(
        body, name=name, grid=(1,), in_specs=[_row_spec(rows, width), _vec_spec(width), _row_spec(rows, width)],
        out_specs=_vec_spec(width), out_shape=jax.ShapeDtypeStruct((1, width), F32),
        compiler_params=_params(("arbitrary",)),
    )(x, g, dy)


CUM_Q = DH
CUM_K = DH + 3
LSE_Q = DH + 6
DEN_V = DH
DELTA = DH + 1
PREP_TR = 256
FOX_FWD_BLOCK = 1024
FOX_BWD_BLOCK = 512


def _head_block(ref, off, h):
    start = off + DH * h
    base = (start // LANES) * LANES
    blk = ref[:, base:base + LANES]
    return pltpu.roll(blk, DH, 1) if start % LANES else blk


def _cumsum_rows(x, tri, carry):
    hi, mid, lo = _split3(x)
    return _dot(tri, hi) + _dot(tri, mid) + _dot(tri, lo) + carry


def _fox_prep(proj, bf_pad):
    tr = PREP_TR

    def body(proj_ref, bf_ref, qa_ref, ka_ref, va_ref, carry_ref):
        i = pl.program_id(0)

        @pl.when(i == 0)
        def _():
            carry_ref[...] = jnp.zeros_like(carry_ref)

        lane = _lane_iota((tr, LANES))
        z = proj_ref[:, F_OFF:F_OFF + LANES] + bf_ref[...]
        log_f = jnp.minimum(z, 0.0) - jnp.log(1.0 + jnp.exp(-jnp.abs(z)))
        log_f = jnp.where(lane < HEADS, log_f, 0.0)
        tri = jnp.where(_row_iota((tr, tr)) >= _lane_iota((tr, tr)), 1.0, 0.0).astype(BF16)
        cum = _cumsum_rows(log_f, tri, carry_ref[0:1, :])
        carry_ref[0:1, :] = cum[tr - 1:tr, :]

        ones_q = jnp.where((lane >= CUM_K) & (lane < CUM_K + 3), 1.0, 0.0)
        ones_k = jnp.where(((lane >= CUM_Q) & (lane < CUM_Q + 3)) | ((lane >= LSE_Q) & (lane < LSE_Q + 3)), 1.0, 0.0)
        aug_v = jnp.where(lane == DEN_V, 1.0, jnp.where((lane >= DELTA) & (lane < DELTA + 3), -1.0, 0.0))
        for h in range(HEADS):
            c_hi, c_mid, c_lo = _split3_f32(jnp.broadcast_to(cum[:, h:h + 1], (tr, LANES)))
            aug_q = jnp.where(lane == CUM_Q, c_hi, jnp.where(lane == CUM_Q + 1, c_mid, jnp.where(lane == CUM_Q + 2, c_lo, ones_q)))
            aug_k = jnp.where(lane == CUM_K, -c_hi, jnp.where(lane == CUM_K + 1, -c_mid, jnp.where(lane == CUM_K + 2, -c_lo, ones_k)))
            qa_ref[h] = jnp.where(lane < DH, _head_block(proj_ref, Q_OFF, h) * (DH ** -0.5), aug_q).astype(BF16)
            ka_ref[h] = jnp.where(lane < DH, _head_block(proj_ref, K_OFF, h), aug_k).astype(BF16)
            va_ref[h] = jnp.where(lane < DH, _head_block(proj_ref, V_OFF, h), aug_v).astype(BF16)

    head_spec = pl.BlockSpec((HEADS, tr, LANES), lambda i: (0, i, 0))
    head_shape = jax.ShapeDtypeStruct((HEADS, S, LANES), BF16)
    return pl.pallas_call(
        body, name="fox_prep", grid=(S // tr,), in_specs=[_row_spec(tr, D_IN_PAD), _vec_spec(LANES)],
        out_specs=[head_spec] * 3, out_shape=[head_shape] * 3, scratch_shapes=[pltpu.VMEM((SUBLANES, LANES), F32)],
        compiler_params=_params(("arbitrary",)),
    )(proj, bf_pad)


def _hosted(ex, refs, n_blocked_in, n_blocked_out, first, forward_at, last):
    n = len(ex.ins)
    own_in = refs[:n_blocked_in]
    ex_in = refs[n_blocked_in:n_blocked_in + n]
    own_out = refs[n_blocked_in + n:n_blocked_in + n + n_blocked_out]
    ex_out = refs[n_blocked_in + n + n_blocked_out:n_blocked_in + 2 * n + n_blocked_out]
    rest = refs[n_blocked_in + 2 * n + n_blocked_out:]
    args = (ex_in, ex_out, rest[-2], rest[-1])

    def begin():
        @pl.when(first)
        def _():
            ex.start(*args)

        @pl.when(forward_at)
        def _():
            ex.forward(*args)

    def end():
        @pl.when(last)
        def _():
            ex.finish(*args)

    return own_in, own_out, rest[:-2], begin, end


def _fox_fwd(qa, ka, va, ex):
    BQ = BK = FOX_FWD_BLOCK
    nq = S // BQ
    n_pairs = HEADS // 2

    def body(*refs):
        p_id, i = pl.program_id(0), pl.program_id(1)
        (qa_ref, ka_ref, va_ref), (y_ref, qab_ref), (m_scr, acc_scr), begin, end = _hosted(
            ex, refs, 3, 2, (p_id == 0) & (i == 0), (p_id == n_pairs - 1) & (i == 0), (p_id == n_pairs - 1) & (i == nq - 1))
        begin()
        lane = _lane_iota((BQ, LANES))
        causal = _row_iota((BQ, BK)) >= _lane_iota((BQ, BK))
        m_scr[...] = jnp.full_like(m_scr, NEG)
        acc_scr[...] = jnp.zeros_like(acc_scr)

        def step(j, masked):
            rows = pl.ds(pl.multiple_of(j * BK, BK), BK)
            for hh in range(2):
                s = _dot(qa_ref[hh], ka_ref[hh, rows, :], NT)
                if masked:
                    s = jnp.where(causal, s, NEG)
                m_prev = m_scr[hh]
                m_new = jnp.maximum(m_prev, jnp.max(s, axis=1, keepdims=True))
                p = jnp.exp(s - jnp.tile(m_new, (1, BK // LANES)))
                acc_scr[hh] = jnp.exp(m_prev - m_new) * acc_scr[hh] + _dot(p.astype(BF16), va_ref[hh, rows, :])
                m_scr[hh] = m_new

        def full_step(j, carry):
            step(j, False)
            return carry

        lax.fori_loop(0, i, full_step, 0)
        step(i, True)
        outs = []
        for hh in range(2):
            acc = acc_scr[hh]
            den = jnp.broadcast_to(acc[:, DEN_V:DEN_V + 1], (BQ, LANES))
            outs.append(acc * (1.0 / den))
            n_hi, n_mid, n_lo = _split3(-(m_scr[hh] + jnp.log(den)))
            qab_ref[hh] = jnp.where(lane == LSE_Q, n_hi,
                                    jnp.where(lane == LSE_Q + 1, n_mid, jnp.where(lane == LSE_Q + 2, n_lo, qa_ref[hh])))
        y_ref[...] = jnp.where(lane < DH, outs[0], pltpu.roll(outs[1], DH, 1)).astype(BF16)
        end()

    pair_rows = pl.BlockSpec((2, BQ, LANES), lambda p, i: (p, i, 0))
    pair_all = pl.BlockSpec((2, S, LANES), lambda p, i: (p, 0, 0))
    n = len(ex.ins)
    res = pl.pallas_call(
        body, name="fox_fwd", grid=(n_pairs, nq), in_specs=[pair_rows, pair_all, pair_all] + [ANY] * n,
        out_specs=[pl.BlockSpec((BQ, LANES), lambda p, i: (i, D_POOL // LANES + p)), pair_rows] + [ANY] * n,
        out_shape=[jax.ShapeDtypeStruct((S, D), BF16), jax.ShapeDtypeStruct((HEADS, S, LANES), BF16)] + ex.out_shapes,
        scratch_shapes=[pltpu.VMEM((2, BQ, LANES), F32), pltpu.VMEM((2, BQ, LANES), F32)] + ex.scratch(),
        compiler_params=_params(("arbitrary", "arbitrary")),
    )(qa, ka, va, *ex.ins)
    return res[0], res[1], res[2:]


def _bwd_xa_mix(dqx, w_xq, dres, x2, g_pre, y1, g_post, w_mix_out, ycat, ex):
    steps = S // TR
    n = len(ex.ins)

    def body(*refs):
        i = pl.program_id(0)
        ((dq_ref, wq_ref, dres_ref, x_ref, gpre_ref, y_ref, gpost_ref, wm_ref, ycat_ref),
         (dx_ref, dy_ref, dgpre_ref, dgpost_ref, dp_ref, doa_ref), _, begin, end) = _hosted(
            ex, refs, 9, 6, i == 0, i == 0, i == steps - 1)
        begin()

        @pl.when(i == 0)
        def _():
            dgpre_ref[...] = jnp.zeros_like(dgpre_ref)
            dgpost_ref[...] = jnp.zeros_like(dgpost_ref)

        dxn, dgpre = _rms_bwd(x_ref[...], gpre_ref[...], _dot(dq_ref[...], wq_ref[...], NT))
        dx = dres_ref[...] + dxn
        dx_ref[...] = dx
        dy, dgpost = _rms_bwd(y_ref[...], gpost_ref[...], dx)
        dy = dy.astype(BF16)
        dy_ref[...] = dy
        dgpre_ref[...] += dgpre
        dgpost_ref[...] += dgpost

        d = _dot(dy, wm_ref[...], NT)
        dp_ref[...] = d[:, :D_POOL]
        lane = _lane_iota((TR, LANES))
        low = lane < DH
        for p in range(HEADS // 2):
            cols = slice(D_POOL + LANES * p, D_POOL + LANES * (p + 1))
            do = d[:, cols]
            prod = do * ycat_ref[:, cols].astype(F32)
            deltas = (jnp.sum(jnp.where(low, prod, 0.0), axis=1, keepdims=True),
                      jnp.sum(jnp.where(low, 0.0, prod), axis=1, keepdims=True))
            for hh in range(2):
                d_hi, d_mid, d_lo = _split3_f32(deltas[hh])
                aug = jnp.where(lane == DELTA, d_hi, jnp.where(lane == DELTA + 1, d_mid, jnp.where(lane == DELTA + 2, d_lo, 0.0)))
                do_h = do if hh == 0 else pltpu.roll(do, DH, 1)
                doa_ref[2 * p + hh] = jnp.where(low, do_h, aug).astype(BF16)
        end()

    mat = pl.BlockSpec((D, D), lambda i: (0, 0))
    res = pl.pallas_call(
        body, name="bwd_xa_mix", grid=(steps,),
        in_specs=[_row_spec(TR, D), mat, _row_spec(TR, D), _row_spec(TR, D), _vec_spec(D), _row_spec(TR, D), _vec_spec(D), mat,
                  _row_spec(TR, D)] + [ANY] * n,
        out_specs=[_row_spec(TR, D), _row_spec(TR, D), _vec_spec(D), _vec_spec(D), _row_spec(TR, D_POOL),
                   pl.BlockSpec((HEADS, TR, LANES), lambda i: (0, i, 0))] + [ANY] * n,
        out_shape=[jax.ShapeDtypeStruct((S, D), F32), jax.ShapeDtypeStruct((S, D), BF16), jax.ShapeDtypeStruct((1, D), F32),
                   jax.ShapeDtypeStruct((1, D), F32), jax.ShapeDtypeStruct((S, D_POOL), F32),
                   jax.ShapeDtypeStruct((HEADS, S, LANES), BF16)] + ex.out_shapes,
        scratch_shapes=ex.scratch(), compiler_params=_params(("arbitrary",)),
    )(dqx, w_xq, dres, x2, g_pre, y1, g_post, w_mix_out, ycat, *ex.ins)
    return res[:6], res[6:]


def _fox_bwd(qab, doa, ka, va, ex):
    BQ = BK = FOX_BWD_BLOCK
    nk = S // BK
    n_pairs = HEADS // 2

    def body(*refs):
        p_id, j = pl.program_id(0), pl.program_id(1)
        (qab_ref, doa_ref, ka_ref, va_ref), (dqa_ref, dka_ref, dva_ref), _, begin, end = _hosted(
            ex, refs, 4, 3, (p_id == 0) & (j == 0), (p_id == n_pairs - 1) & (j == 0), (p_id == n_pairs - 1) & (j == nk - 1))
        begin()

        @pl.when(j == 0)
        def _():
            dqa_ref[...] = jnp.zeros_like(dqa_ref)

        causal = _row_iota((BQ, BK)) >= _lane_iota((BQ, BK))
        dka_ref[...] = jnp.zeros_like(dka_ref)
        dva_ref[...] = jnp.zeros_like(dva_ref)

        def step(i, masked):
            rows = pl.ds(pl.multiple_of(i * BQ, BQ), BQ)
            for hh in range(2):
                kb = ka_ref[hh]
                q = qab_ref[hh, rows, :]
                do = doa_ref[hh, rows, :]
                s = _dot(q, kb, NT)
                if masked:
                    s = jnp.where(causal, s, NEG)
                p = jnp.exp(s)
                ds = p * _dot(do, va_ref[hh], NT)
                pb = p.astype(BF16)
                dsb = ds.astype(BF16)
                dva_ref[hh] += _dot(pb, do, TN)
                dka_ref[hh] += _dot(dsb, q, TN)
                dqa_ref[hh, rows, :] += _dot(dsb, kb)

        def full_step(i, carry):
            step(i, False)
            return carry

        step(j, True)
        lax.fori_loop(j + 1, nk, full_step, 0)
        end()

    pair_all = pl.BlockSpec((2, S, LANES), lambda p, j: (p, 0, 0))
    pair_rows = pl.BlockSpec((2, BK, LANES), lambda p, j: (p, j, 0))
    shape = jax.ShapeDtypeStruct((HEADS, S, LANES), F32)
    n = len(ex.ins)
    res = pl.pallas_call(
        body, name="fox_bwd", grid=(n_pairs, nk), in_specs=[pair_all, pair_all, pair_rows, pair_rows] + [ANY] * n,
        out_specs=[pair_all, pair_rows, pair_rows] + [ANY] * n, out_shape=[shape] * 3 + ex.out_shapes,
        scratch_shapes=ex.scratch(), compiler_params=_params(("arbitrary", "arbitrary")),
    )(qab, doa, ka, va, *ex.ins)
    return res[0], res[1], res[2], res[3:]


def _fox_bwd_post(dqa, dka, dva, du, proj, bf_pad):
    tr = PREP_TR
    nt = S // tr

    def body(dqa_ref, dka_ref, dva_ref, du_ref, z_ref, bf_ref, dp_ref, dbf_ref, carry_ref):
        i = pl.program_id(0)

        @pl.when(i == 0)
        def _():
            carry_ref[...] = jnp.zeros_like(carry_ref)
            dbf_ref[...] = jnp.zeros_like(dbf_ref)

        lane = _lane_iota((tr, LANES))
        dcum = jnp.zeros((tr, LANES), F32)
        for h in range(HEADS):
            dc = dqa_ref[h][:, CUM_Q:CUM_Q + 1] - dka_ref[h][:, CUM_K:CUM_K + 1]
            dcum = jnp.where(lane == h, dc, dcum)
        tri = jnp.where(_lane_iota((tr, tr)) >= _row_iota((tr, tr)), 1.0, 0.0).astype(BF16)
        dlog_f = _cumsum_rows(dcum, tri, carry_ref[0:1, :])
        carry_ref[0:1, :] = dlog_f[0:1, :]
        z = z_ref[...] + bf_ref[...]
        df = jnp.where(lane < HEADS, dlog_f / (1.0 + jnp.exp(z)), 0.0)
        dbf_ref[...] += jnp.sum(df, axis=0, keepdims=True)

        dp_ref[:, 0:D_POOL] = du_ref[...].astype(BF16)
        low = lane < DH
        for ref, off, scale in ((dqa_ref, Q_OFF, DH ** -0.5), (dka_ref, K_OFF, 1.0), (dva_ref, V_OFF, 1.0)):
            for p in range(HEADS // 2):
                blk = jnp.where(low, ref[2 * p], pltpu.roll(ref[2 * p + 1], DH, 1))
                dp_ref[:, off + LANES * p:off + LANES * (p + 1)] = (blk * scale).astype(BF16)
        dp_ref[:, F_OFF:F_OFF + LANES] = df.astype(BF16)

    head_spec = pl.BlockSpec((HEADS, tr, LANES), lambda i: (0, nt - 1 - i, 0))
    return pl.pallas_call(
        body, name="fox_bwd_post", grid=(nt,),
        in_specs=[head_spec, head_spec, head_spec, pl.BlockSpec((tr, D_POOL), lambda i: (nt - 1 - i, 0)),
                  pl.BlockSpec((tr, LANES), lambda i: (nt - 1 - i, F_OFF // LANES)), _vec_spec(LANES)],
        out_specs=[pl.BlockSpec((tr, D_IN_PAD), lambda i: (nt - 1 - i, 0)), _vec_spec(LANES)],
        out_shape=[jax.ShapeDtypeStruct((S, D_IN_PAD), BF16), jax.ShapeDtypeStruct((1, LANES), F32)],
        scratch_shapes=[pltpu.VMEM((SUBLANES, LANES), F32)],
        compiler_params=_params(("arbitrary",)),
    )(dqa, dka, dva, du, proj, bf_pad)


POOL_HALO = 16


def _by_group(lane, a2, a4, a8, a16):
    return jnp.where(lane < 64, a2, jnp.where(lane < 128, a4, jnp.where(lane < 192, a8, a16)))


def _window_count(lane, t):
    return jnp.minimum(t + 1, _by_group(lane, 2, 4, 8, 16)).astype(F32)


def _pool_diff(u, halo, first, tile):
    n = TR + POOL_HALO
    ext = jnp.concatenate([jnp.where(first, 0.0, halo), u], axis=0)
    s2 = ext + pltpu.roll(ext, 1, 0)
    s4 = s2 + pltpu.roll(s2, 2, 0)
    s8 = s4 + pltpu.roll(s4, 4, 0)
    s16 = s8 + pltpu.roll(s8, 8, 0)
    lane = _lane_iota((n, D_POOL))
    win = _by_group(lane, s2, s4, s8, s16)[POOL_HALO:]
    lane = _lane_iota((TR, D_POOL))
    t = tile * TR + _row_iota((TR, D_POOL))
    return win / _window_count(lane, t) - u


def _prev_halo(rows, width, col):
    per = TR // rows
    return pl.BlockSpec((rows, width), lambda i: (jnp.maximum(i * per - 1, 0), col))


def _next_halo(rows, width, col):
    per = TR // rows
    return pl.BlockSpec((rows, width), lambda i: (jnp.minimum((i + 1) * per, S // rows - 1), col))


def _pool_fwd(proj, w_bd, ps, ycat):
    def body(u_ref, halo_ref, w_ref, ps_ref, ycat_ref, y_ref):
        i = pl.program_id(0)
        diff = _pool_diff(u_ref[...], halo_ref[...], i == 0, i)
        y_ref[...] = (_dot(diff.astype(BF16), w_ref[...]) * ps_ref[...]).astype(BF16)

    return pl.pallas_call(
        body, name="pool_fwd", grid=(S // TR,),
        in_specs=[_row_spec(TR, D_POOL), _prev_halo(POOL_HALO, D_POOL, 0),
                  pl.BlockSpec((D_POOL, D_POOL), lambda i: (0, 0)), _vec_spec(D_POOL), ANY],
        out_specs=_row_spec(TR, D_POOL), out_shape=jax.ShapeDtypeStruct((S, D), BF16), input_output_aliases={4: 0},
        compiler_params=_params(("parallel",)),
    )(proj, proj, w_bd, ps, ycat)


def _pool_bwd(proj, dycat, w_bd, w_bd_t, ps):
    nt = S // TR
    n = TR + POOL_HALO

    def body(u_ref, halo_ref, dy_ref, dyn_ref, w_ref, wt_ref, ps_ref, du_ref, dw_ref, dps_ref):
        i = pl.program_id(0)

        @pl.when(i == 0)
        def _():
            dw_ref[...] = jnp.zeros_like(dw_ref)
            dps_ref[...] = jnp.zeros_like(dps_ref)

        diff = _pool_diff(u_ref[...], halo_ref[...], i == 0, i).astype(BF16)
        dy = dy_ref[...]
        dps_ref[...] += jnp.sum(dy * _dot(diff, w_ref[...]), axis=0, keepdims=True)
        dy_ext = jnp.concatenate([dy, jnp.where(i == nt - 1, 0.0, dyn_ref[...])], axis=0)
        dmixed = (dy_ext * ps_ref[...]).astype(BF16)
        ddiff = _dot(dmixed, wt_ref[...])
        dw_ref[...] += _dot(diff, dmixed[:TR], TN)
        lane = _lane_iota((n, D_POOL))
        t = i * TR + _row_iota((n, D_POOL))
        e = ddiff / _window_count(lane, t)
        f2 = e + pltpu.roll(e, n - 1, 0)
        f4 = f2 + pltpu.roll(f2, n - 2, 0)
        f8 = f4 + pltpu.roll(f4, n - 4, 0)
        f16 = f8 + pltpu.roll(f8, n - 8, 0)
        du_ref[...] = _by_group(lane, f2, f4, f8, f16)[:TR] - ddiff[:TR]

    mat = pl.BlockSpec((D_POOL, D_POOL), lambda i: (0, 0))
    return pl.pallas_call(
        body, name="pool_bwd", grid=(nt,),
        in_specs=[_row_spec(TR, D_POOL), _prev_halo(POOL_HALO, D_POOL, 0), _row_spec(TR, D_POOL),
                  _next_halo(POOL_HALO, D_POOL, 0), mat, mat, _vec_spec(D_POOL)],
        out_specs=[_row_spec(TR, D_POOL), mat, _vec_spec(D_POOL)],
        out_shape=[jax.ShapeDtypeStruct((S, D_POOL), F32), jax.ShapeDtypeStruct((D_POOL, D_POOL), F32),
                   jax.ShapeDtypeStruct((1, D_POOL), F32)],
        compiler_params=_params(("arbitrary",)),
    )(proj, proj, dycat, dycat, w_bd, w_bd_t, ps)


def _xa_probs(q, k):
    s = _dot(q, k, NT) * (XA_DH ** -0.5)
    e = jnp.exp(s - jnp.max(s, axis=-1, keepdims=True))
    return e * (1.0 / jnp.sum(e, axis=-1, keepdims=True))


def _xattn_fwd(qx, kv):
    def body(q_ref, kv_ref, o_ref):
        for h in range(XA_HEADS):
            cols = slice(XA_DH * h, XA_DH * (h + 1))
            vcols = slice(D + XA_DH * h, D + XA_DH * (h + 1))
            p = _xa_probs(q_ref[:, cols], kv_ref[:, cols])
            o_ref[:, cols] = _dot(p.astype(BF16), kv_ref[:, vcols]).astype(BF16)

    return pl.pallas_call(
        body, name="xattn_fwd", grid=(S // TR,),
        in_specs=[_row_spec(TR, D), pl.BlockSpec((MEM, 2 * D), lambda i: (0, 0))],
        out_specs=_row_spec(TR, D), out_shape=jax.ShapeDtypeStruct((S, D), BF16),
        compiler_params=_params(("parallel",)),
    )(qx, kv)


def _xattn_bwd(qx, kv, dxo):
    def body(q_ref, kv_ref, do_ref, dq_ref, dkv_ref):
        i = pl.program_id(0)

        @pl.when(i == 0)
        def _():
            dkv_ref[...] = jnp.zeros_like(dkv_ref)

        for h in range(XA_HEADS):
            cols = slice(XA_DH * h, XA_DH * (h + 1))
            vcols = slice(D + XA_DH * h, D + XA_DH * (h + 1))
            q = q_ref[:, cols]
            k = kv_ref[:, cols]
            do = do_ref[:, cols]
            p = _xa_probs(q, k)
            dkv_ref[:, vcols] += _dot(p.astype(BF16), do, TN)
            dp = _dot(do, kv_ref[:, vcols], NT)
            ds = (p * (dp - jnp.sum(p * dp, axis=-1, keepdims=True)) * (XA_DH ** -0.5)).astype(BF16)
            dq_ref[:, cols] = _dot(ds, k).astype(BF16)
            dkv_ref[:, cols] += _dot(ds, q, TN)

    kv_spec = pl.BlockSpec((MEM, 2 * D), lambda i: (0, 0))
    return pl.pallas_call(
        body, name="xattn_bwd", grid=(S // TR,), in_specs=[_row_spec(TR, D), kv_spec, _row_spec(TR, D)],
        out_specs=[_row_spec(TR, D), kv_spec],
        out_shape=[jax.ShapeDtypeStruct((S, D), BF16), jax.ShapeDtypeStruct((MEM, 2 * D), F32)],
        compiler_params=_params(("arbitrary",)),
    )(qx, kv, dxo)


CONV_HALO = SUBLANES
TC = 512
GELU_K = 0.7978845608028654
GELU_C = 0.044715


def _conv3(ext, w, rows):
    h0 = ext[CONV_HALO:CONV_HALO + rows]
    h1 = pltpu.roll(ext, 1, 0)[CONV_HALO:CONV_HALO + rows]
    h2 = pltpu.roll(ext, 2, 0)[CONV_HALO:CONV_HALO + rows]
    return w[2:3] * h0 + w[1:2] * h1 + w[0:1] * h2 + w[3:4], (h2, h1, h0)


def _conv_specs():
    main = pl.BlockSpec((2, TR, TC), lambda j, i: (0, i, j))
    per = TR // CONV_HALO
    prev = pl.BlockSpec((2, CONV_HALO, TC), lambda j, i: (0, jnp.maximum(i * per - 1, 0), j))
    nxt = pl.BlockSpec((2, CONV_HALO, TC), lambda j, i: (0, jnp.minimum((i + 1) * per, S // CONV_HALO - 1), j))
    par = pl.BlockSpec((2, SUBLANES, TC), lambda j, i: (0, 0, j))
    return main, prev, nxt, par


def _convgate_fwd(hid, cwb):
    def body(h_ref, hp_ref, w_ref, act_ref):
        i = pl.program_id(1)
        c = []
        for g in range(2):
            ext = jnp.concatenate([jnp.where(i == 0, 0.0, hp_ref[g]), h_ref[g]], axis=0)
            c.append(_conv3(ext, w_ref[g], TR)[0])
        gate, up = c
        act_ref[...] = (jax.nn.gelu(gate, approximate=True) * up).astype(BF16)

    main, prev, _, par = _conv_specs()
    return pl.pallas_call(
        body, name="convgate_fwd", grid=(D_FF // TC, S // TR), in_specs=[main, prev, par],
        out_specs=pl.BlockSpec((TR, TC), lambda j, i: (i, j)), out_shape=jax.ShapeDtypeStruct((S, D_FF), BF16),
        compiler_params=_params(("parallel", "parallel")),
    )(hid, hid, cwb)


def _convgate_bwd(hid, dact, cwb):
    nr = S // TR
    n = TR + CONV_HALO

    def body(h_ref, hp_ref, hn_ref, da_ref, dan_ref, w_ref, dh_ref, dw_ref):
        i = pl.program_id(1)

        @pl.when(i == 0)
        def _():
            dw_ref[...] = jnp.zeros_like(dw_ref)

        da = jnp.concatenate([da_ref[...], jnp.where(i == nr - 1, 0.0, dan_ref[...])], axis=0)
        c, taps = [], []
        for g in range(2):
            ext = jnp.concatenate([jnp.where(i == 0, 0.0, hp_ref[g]), h_ref[g], hn_ref[g]], axis=0)
            cg, tg = _conv3(ext, w_ref[g], n)
            c.append(cg)
            taps.append(tg)
        gate, up = c
        th = jnp.tanh(GELU_K * (gate + GELU_C * gate * gate * gate))
        gelu = 0.5 * gate * (1.0 + th)
        dgelu = 0.5 * (1.0 + th) + 0.5 * gate * (1.0 - th * th) * GELU_K * (1.0 + 3.0 * GELU_C * gate * gate)
        for g, dc in enumerate((da * up * dgelu, da * gelu)):
            w = w_ref[g]
            dh = w[2:3] * dc[:TR] + w[1:2] * pltpu.roll(dc, n - 1, 0)[:TR] + w[0:1] * pltpu.roll(dc, n - 2, 0)[:TR]
            dh_ref[g] = dh.astype(BF16)
            dcm = dc[:TR]
            for r in range(3):
                dw_ref[g, r:r + 1, :] += jnp.sum(dcm * taps[g][r][:TR], axis=0, keepdims=True)
            dw_ref[g, 3:4, :] += jnp.sum(dcm, axis=0, keepdims=True)

    main, prev, nxt, par = _conv_specs()
    per = TR // CONV_HALO
    return pl.pallas_call(
        body, name="convgate_bwd", grid=(D_FF // TC, nr),
        in_specs=[main, prev, nxt, pl.BlockSpec((TR, TC), lambda j, i: (i, j)),
                  pl.BlockSpec((CONV_HALO, TC), lambda j, i: (jnp.minimum((i + 1) * per, S // CONV_HALO - 1), j)), par],
        out_specs=[main, par],
        out_shape=[jax.ShapeDtypeStruct((2, S, D_FF), BF16), jax.ShapeDtypeStruct((2, SUBLANES, D_FF), F32)],
        compiler_params=_params(("parallel", "arbitrary")),
    )(hid, hid, hid, dact, dact, cwb)


def _adam_update(w, g, m, v):
    m = ADAM_B1 * m + (1.0 - ADAM_B1) * g
    v = ADAM_B2 * v + (1.0 - ADAM_B2) * (g * g)
    m_hat = m / (1.0 - ADAM_B1 ** ADAM_STEP)
    v_hat = v / (1.0 - ADAM_B2 ** ADAM_STEP)
    return -ADAM_LR * (m_hat / (jnp.sqrt(v_hat) + ADAM_EPS) + ADAM_WD * w), m, v


def _row_tile(rows, cols, itemsize=4, target=TILE_BYTES):
    tr = SUBLANES
    while rows % (2 * tr) == 0 and 2 * tr * cols * itemsize <= target:
        tr *= 2
    assert rows % tr == 0, (rows, tr)
    return tr


def _adamw(name, w, g, m, v):
    rows, cols = w.shape
    tr = rows if rows * cols * 4 <= TILE_BYTES // 2 else _row_tile(rows, cols, target=TILE_BYTES // 2)

    def body(w_ref, g_ref, m_ref, v_ref, d_ref, nm_ref, nv_ref):
        d_ref[...], nm_ref[...], nv_ref[...] = _adam_update(w_ref[...], g_ref[...], m_ref[...], v_ref[...])

    spec = _row_spec(tr, cols)
    shape = jax.ShapeDtypeStruct((rows, cols), F32)
    return pl.pallas_call(
        body, name=name, grid=(rows // tr,), in_specs=[spec] * 4, out_specs=[spec] * 3, out_shape=[shape] * 3,
        compiler_params=_params(("parallel",)),
    )(w, g, m, v)


def _adamw_halves(name, core, w, g_mine, g_sibling, m, v):
    rows, cols = w.shape
    half = rows // 2
    tr = _row_tile(half, cols, target=TILE_BYTES // 2)
    per = half // tr

    def body(core_ref, w_ref, gm_ref, gs_ref, m_ref, v_ref, g_ref, d_ref, nm_ref, nv_ref):
        g = jnp.where(pl.program_id(0) // per == core_ref[0], gm_ref[...], gs_ref[...])
        g_ref[...] = g
        d_ref[...], nm_ref[...], nv_ref[...] = _adam_update(w_ref[...], g, m_ref[...], v_ref[...])

    spec = pl.BlockSpec((tr, cols), lambda i, core_ref: (i, 0))
    half_spec = pl.BlockSpec((tr, cols), lambda i, core_ref: (i % per, 0))
    shape = jax.ShapeDtypeStruct((rows, cols), F32)
    return pl.pallas_call(
        body, name=name, out_shape=[shape] * 4,
        grid_spec=pltpu.PrefetchScalarGridSpec(
            num_scalar_prefetch=1, grid=(rows // tr,), in_specs=[spec, half_spec, half_spec, spec, spec], out_specs=[spec] * 4),
        compiler_params=_params(("parallel",)),
    )(core, w, g_mine, g_sibling, m, v)


def _chip_sum(name, core, g, other):
    _, _, half, cols = g.shape
    tr = _row_tile(half, cols)

    def body(core_ref, g_ref, o_ref, p_ref):
        p_ref[...] = (g_ref[...] + o_ref[...]).astype(BF16)

    spec = pl.BlockSpec((None, tr, cols), lambda j, i, core_ref: (j, i, 0))
    return pl.pallas_call(
        body, name=name, out_shape=jax.ShapeDtypeStruct((N_CHIPS, half, cols), BF16),
        grid_spec=pltpu.PrefetchScalarGridSpec(
            num_scalar_prefetch=1, grid=(N_CHIPS, half // tr),
            in_specs=[pl.BlockSpec((None, None, tr, cols), lambda j, i, core_ref: (j, core_ref[0], i, 0)), spec],
            out_specs=spec),
        compiler_params=_params(("parallel", "parallel")),
    )(core, g, other)


def _mesh_sum(name, chip, received, own):
    _, half, cols = received.shape
    tr = _row_tile(half, cols, itemsize=2 * N_CHIPS)

    def body(chip_ref, r_ref, own_ref, o_ref):
        acc = None
        for j in range(N_CHIPS):
            term = jnp.where(chip_ref[0] == j, own_ref[...], r_ref[j]).astype(F32)
            acc = term if acc is None else acc + term
        o_ref[...] = acc

    return pl.pallas_call(
        body, name=name, out_shape=jax.ShapeDtypeStruct((half, cols), F32),
        grid_spec=pltpu.PrefetchScalarGridSpec(
            num_scalar_prefetch=1, grid=(half // tr,),
            in_specs=[pl.BlockSpec((N_CHIPS, tr, cols), lambda i, chip_ref: (0, i, 0)),
                      pl.BlockSpec((None, tr, cols), lambda i, chip_ref: (chip_ref[0], i, 0))],
            out_specs=pl.BlockSpec((tr, cols), lambda i, chip_ref: (i, 0))),
        compiler_params=_params(("parallel",)),
    )(chip, received, own)


CHIP_FLIPS = ((1, 0), (0, 1), (1, 1))


def _place():
    x, y, c = lax.axis_index("x"), lax.axis_index("y"), lax.axis_index("c")
    return x, y, c, 2 * x + y


def _remote(src, dst, sems_s, sems_r, k, dev):
    return pltpu.make_async_remote_copy(src_ref=src, dst_ref=dst, send_sem=sems_s.at[k], recv_sem=sems_r.at[k],
                                        device_id=dev, device_id_type=MESH)


class _Exchange:
    def __init__(self, ins, out_shapes, n_sems, start, forward, finish):
        self.ins, self.out_shapes, self.n_sems = list(ins), list(out_shapes), n_sems
        self.start, self.forward, self.finish = start, forward, finish

    def scratch(self):
        return [pltpu.SemaphoreType.DMA((self.n_sems,)), pltpu.SemaphoreType.DMA((self.n_sems,))]

    def run(self, name):
        n = len(self.ins)

        def body(*refs):
            args = (refs[:n], refs[n:2 * n]) + tuple(refs[2 * n:])
            self.start(*args)
            self.forward(*args)
            self.finish(*args)

        return pl.pallas_call(
            body, name=name, in_specs=[ANY] * n, out_specs=[ANY] * n, out_shape=self.out_shapes, scratch_shapes=self.scratch(),
        )(*self.ins)


def _all_gather_weights(halved, whole):
    nh, nw = len(halved), len(whole)
    n_arr = nh + nw

    def copies(ins, outs, sems_s, sems_r):
        x, y, c, me = _place()
        sibling = (x, y, 1 - c)
        own = [_remote(ins[k], outs[k].at[me], sems_s, sems_r, k, sibling) for k in range(n_arr)]
        first, passed = [], []
        for k in range(n_arr):
            for f, (fx, fy) in enumerate(CHIP_FLIPS):
                src, dst = (ins[k].at[c], outs[k].at[me, c]) if k < nh else (ins[k], outs[k].at[me])
                first.append(_remote(src, dst, sems_s, sems_r, n_arr + 3 * k + f, (x ^ fx, y ^ fy, c)))
        for k in range(nh):
            for f, (fx, fy) in enumerate(CHIP_FLIPS):
                landed = outs[k].at[2 * (x ^ fx) + (y ^ fy), c]
                passed.append(_remote(landed, landed, sems_s, sems_r, 4 * n_arr + 3 * k + f, sibling))
        return own, first, passed

    def start(*refs):
        own, first, _ = copies(*refs)
        for cp in own + first:
            cp.start()

    def forward(*refs):
        _, first, passed = copies(*refs)
        for arrived, cp in zip(first, passed):
            arrived.wait_recv()
            cp.start()

    def finish(*refs):
        own, first, passed = copies(*refs)
        for cp in first[3 * nh:] + passed + own:
            cp.wait_recv()
        for cp in first + passed + own:
            cp.wait_send()

    shapes = [jax.ShapeDtypeStruct((N_CHIPS,) + a.shape, a.dtype) for a in list(halved) + list(whole)]
    return _Exchange(list(halved) + list(whole), shapes, 7 * nh + 4 * nw, start, forward, finish)


def _swap_halves(gs):
    n = len(gs)

    def copies(ins, outs, sems_s, sems_r):
        x, y, c, _ = _place()
        return [_remote(ins[k].at[:, 1 - c], outs[k], sems_s, sems_r, k, (x, y, 1 - c)) for k in range(n)]

    def start(*refs):
        for cp in copies(*refs):
            cp.start()

    def finish(*refs):
        for cp in copies(*refs):
            cp.wait()

    shapes = [jax.ShapeDtypeStruct((g.shape[0],) + g.shape[2:], g.dtype) for g in gs]
    return _Exchange(gs, shapes, n, start, _no_copies, finish)


def _scatter_chips(ps):
    n = len(ps)

    def copies(ins, outs, sems_s, sems_r):
        x, y, c, me = _place()
        return [_remote(ins[k].at[2 * (x ^ fx) + (y ^ fy)], outs[k].at[me], sems_s, sems_r, 3 * k + f, (x ^ fx, y ^ fy, c))
                for k in range(n) for f, (fx, fy) in enumerate(CHIP_FLIPS)]

    def start(*refs):
        for cp in copies(*refs):
            cp.start()

    def forward(*refs):
        pass

    def finish(*refs):
        for cp in copies(*refs):
            cp.wait()

    shapes = [jax.ShapeDtypeStruct(p.shape, p.dtype) for p in ps]
    return _Exchange(ps, shapes, 3 * n, start, forward, finish)


def _swap_reduced(rs):
    n = len(rs)

    def copies(ins, outs, sems_s, sems_r):
        x, y, c, _ = _place()
        return [_remote(ins[k], outs[k], sems_s, sems_r, k, (x, y, 1 - c)) for k in range(n)]

    def start(*refs):
        for cp in copies(*refs):
            cp.start()

    def finish(*refs):
        for cp in copies(*refs):
            cp.wait()

    return _Exchange(rs, [jax.ShapeDtypeStruct(r.shape, r.dtype) for r in rs], n, start, _no_copies, finish)


N_DEV = 8


def _gather_small(buf):
    def copies(ins, outs, sems_s, sems_r):
        x, y, c, _ = _place()
        me = 4 * x + 2 * y + c
        return [_remote(ins[0], outs[0].at[me], sems_s, sems_r, o - 1, (x ^ (o >> 2), y ^ ((o >> 1) & 1), c ^ (o & 1)))
                for o in range(1, N_DEV)]

    def start(*refs):
        for cp in copies(*refs):
            cp.start()

    def finish(*refs):
        for cp in copies(*refs):
            cp.wait()

    return _Exchange([buf], [jax.ShapeDtypeStruct((N_DEV,) + buf.shape, buf.dtype)], N_DEV - 1, start, _no_copies, finish)


def _sum_devices(place, gathered, own):
    rows = own.shape[0]

    def body(place_ref, g_ref, own_ref, o_ref):
        acc = None
        for d in range(N_DEV):
            term = jnp.where(place_ref[0] == d, own_ref[...], g_ref[d])
            acc = term if acc is None else acc + term
        o_ref[...] = acc

    return pl.pallas_call(
        body, name="sum_devices", out_shape=jax.ShapeDtypeStruct((rows, LANES), F32),
        grid_spec=pltpu.PrefetchScalarGridSpec(
            num_scalar_prefetch=1, grid=(1,),
            in_specs=[pl.BlockSpec((N_DEV, rows, LANES), lambda i, place_ref: (0, 0, 0)),
                      pl.BlockSpec((rows, LANES), lambda i, place_ref: (0, 0))],
            out_specs=pl.BlockSpec((rows, LANES), lambda i, place_ref: (0, 0))),
        compiler_params=_params(("arbitrary",)),
    )(place, gathered, own)


def _no_copies(*refs):
    pass


def _no_exchange():
    return _Exchange([], [], 1, _no_copies, _no_copies, _no_copies)


class _NoComm:
    def gather_first(self):
        return _no_exchange()

    def first_landed(self, p, landed):
        pass

    def gather_rest(self, p):
        return _no_exchange()

    def weights_landed(self, p, landed):
        pass

    def swap_first(self, g):
        return _no_exchange()

    def first_swapped(self, landed):
        pass

    def swap_second(self, g):
        return _no_exchange()

    def second_swapped(self, landed):
        pass

    def scatter_early(self, g):
        return _no_exchange()

    def scatter_landed(self, landed):
        pass

    def swap_reduced_early(self):
        return _no_exchange()

    def reduced_landed(self, landed):
        pass

    def scatter_late(self, g):
        return _no_exchange()

    def late_landed(self, landed):
        pass


def _local_step(x, mem, target, p, comm):
    h1, landed = _norm_fwd("norm_mix_pre", x, p["norm_mix_pre"], comm.gather_first())
    comm.first_landed(p, landed)
    proj = _mm_nn("in_proj", h1, p["w_in"], F32, 1024, 896)
    qa, ka, va = _fox_prep(proj, p["bf_pad"])
    ycat, qab, landed = _fox_fwd(qa, ka, va, comm.gather_rest(p))
    comm.weights_landed(p, landed)
    ycat = _pool_fwd(proj, p["w_pool_bd"], p["pool_scale"], ycat)
    y1, x2, h2, qx = _proj_resid_norm("mix_out", ycat, p["w_mix_out"], x, p["norm_mix_post"], p["norm_xa_pre"], p["w_xq"])
    mem_n = _norm_fwd("norm_mem", mem, p["norm_mem"])
    kv = _mm(
        "xkv", mem_n, p["w_xkv"], pl.BlockSpec((MEM, D), lambda i, j, k: (0, 0)),
        pl.BlockSpec((None, D, 512), lambda i, j, k: (j, 0, 0)), jax.ShapeDtypeStruct((MEM, 2 * D), BF16),
        pl.BlockSpec((MEM, 512), lambda i, j, k: (0, j)), (1, N_CHIPS, 1), NN, (MEM, 512))
    xo = _xattn_fwd(qx, kv)
    y2, x3, h3 = _proj_resid_norm("xo", xo, p["w_xo"], x2, p["norm_xa_post"], p["norm_ffn_pre"])
    hid = _mm(
        "up_proj", h3, p["w_up"], pl.BlockSpec((1024, D), lambda i, j, k: (i, 0)),
        pl.BlockSpec((None, D, 1024), lambda i, j, k: (j // 2, 0, j % 2)), jax.ShapeDtypeStruct((2, S, D_FF), F32),
        pl.BlockSpec((None, 1024, 1024), lambda i, j, k: (j // 4, i, j % 4)), (S // 1024, 8, 1), NN, (1024, 1024))
    act = _convgate_fwd(hid, p["cwb"])

    g = {}
    dres, dy3, g["norm_ffn_post"], loss_cols = _down_loss_bwd(act, p["w_down"], x3, p["norm_ffn_post"], target)
    dact = _mm_nt("d_act", dy3, p["w_down"], F32, 1024, 1024)
    g["w_down"] = _mm_tn("dw_down", act, dy3, 512, 512)
    dhid, dcwb = _convgate_bwd(hid, dact, p["cwb"])
    g["w_up"] = _mm(
        "dw_up", h3, dhid, pl.BlockSpec((S, 512), lambda i, j, k: (0, i)),
        pl.BlockSpec((None, S, 512), lambda i, j, k: (j // 8, 0, j % 8)), jax.ShapeDtypeStruct((N_CHIPS, D, 2048), F32),
        pl.BlockSpec((None, 512, 512), lambda i, j, k: (j // 4, i, j % 4)), (2, 16, 1), TN, (512, 512))
    dh3, landed = _d_h3(dhid, p["w_up"], comm.swap_first(g))
    comm.first_swapped(landed)
    dres, dy2, g["norm_ffn_pre"], g["norm_xa_post"] = _mid_bwd("bwd_ffn_xa", dres, x3, p["norm_ffn_pre"], dh3, y2, p["norm_xa_post"])
    dxo = _mm_nt("d_xo", dy2, p["w_xo"], BF16, 1024, 1024)
    g["w_xo"] = _mm_tn("dw_xo", xo, dy2, 512, 512)
    dqx, dkv = _xattn_bwd(qx, kv, dxo)
    dkv = dkv.astype(BF16)
    g["w_xq"] = _mm_tn("dw_xq", h2, dqx, 512, 512)
    dmem_n = _mm(
        "d_mem", dkv, p["w_xkv"], pl.BlockSpec((MEM, 512), lambda i, j, k: (0, k)),
        pl.BlockSpec((None, D, 512), lambda i, j, k: (k, 0, 0)), jax.ShapeDtypeStruct((MEM, D), F32),
        pl.BlockSpec((MEM, D), lambda i, j, k: (0, 0)), (1, 1, N_CHIPS), NT, (MEM, D))
    g["w_xkv"] = _mm(
        "dw_xkv", mem_n, dkv, pl.BlockSpec((MEM, D), lambda i, j, k: (0, 0)),
        pl.BlockSpec((MEM, 512), lambda i, j, k: (0, j)), jax.ShapeDtypeStruct((N_CHIPS, D, 512), F32),
        pl.BlockSpec((None, D, 512), lambda i, j, k: (j, 0, 0)), (1, N_CHIPS, 1), TN, (D, 512))
    g["norm_mem"] = _gain_bwd("dg_mem", mem, p["norm_mem"], dmem_n)
    (dres, dy1, g["norm_xa_pre"], g["norm_mix_post"], dy_pool, doa), landed = _bwd_xa_mix(
        dqx, p["w_xq"], dres, x2, p["norm_xa_pre"], y1, p["norm_mix_post"], p["w_mix_out"], ycat, comm.swap_second(g))
    comm.second_swapped(landed)
    g["w_mix_out"] = _mm_tn("dw_mix_out", ycat, dy1, 512, 512)
    dqa, dka, dva, landed = _fox_bwd(qab, doa, ka, va, comm.scatter_early(g))
    comm.scatter_landed(landed)
    du, g["w_pool_full"], g["pool_scale"] = _pool_bwd(proj, dy_pool, p["w_pool_bd"], p["w_pool_bd_t"], p["pool_scale"])
    dproj, g["bf_pad"] = _fox_bwd_post(dqa, dka, dva, du, proj, p["bf_pad"])
    g["w_in"], landed = _mm_tn("dw_in", h1, dproj, 512, 896, comm.swap_reduced_early())
    comm.reduced_landed(landed)
    dh1, landed = _mm_nt("d_h1", dproj, p["w_in"], F32, 1024, 1024, comm.scatter_late(g))
    comm.late_landed(landed)
    grad_x, g["norm_mix_pre"] = _first_bwd(dres, x, p["norm_mix_pre"], dh1)
    g["cwb"] = dcwb
    return grad_x, g, loss_cols


BIG = ("w_in", "w_mix_out", "w_xq", "w_xkv", "w_xo", "w_up", "w_down")
ROW_SHARDED = ("w_mix_out", "w_xq", "w_xo", "w_down")
SMALL = ("norm_mix_pre", "norm_mix_post", "b_forget", "w_pool", "pool_scale", "norm_mem", "norm_xa_pre", "norm_xa_post",
         "norm_ffn_pre", "norm_ffn_post", "conv_b")
ORDER = ("norm_mix_pre", "norm_mix_post", "w_in", "b_forget", "w_pool", "pool_scale", "w_mix_out", "norm_mem", "norm_xa_pre",
         "norm_xa_post", "w_xq", "w_xkv", "w_xo", "norm_ffn_pre", "norm_ffn_post", "w_up", "conv_w", "conv_b", "w_down")
SLOT = SUBLANES * LANES


def _pack(parts):
    rows, offs, off = [], [], 0
    for a in parts:
        flat = a.reshape(-1).astype(F32)
        n = -(-flat.shape[0] // SLOT) * SLOT
        rows.append(jnp.pad(flat, (0, n - flat.shape[0])).reshape(n // LANES, LANES))
        offs.append(off)
        off += n // LANES
    return jnp.concatenate(rows, axis=0), offs


def _unpack(buf, off, like):
    n = like.size
    rows = -(-n // LANES)
    return buf[off:off + rows].reshape(-1)[:n].reshape(like.shape)


FIRST = ("w_in",)
REST = ("w_mix_out", "w_xq", "w_xkv", "w_xo", "w_up", "w_down")


def _local_params(w):
    w_pool_bd = jnp.zeros((D_POOL, D_POOL), F32)
    for gi in range(4):
        w_pool_bd = w_pool_bd.at[64 * gi:64 * (gi + 1), 64 * gi:64 * (gi + 1)].set(w["w_pool"][0, gi])
    p = {n: w[n] for n in ("norm_mix_pre", "norm_mix_post", "norm_mem", "norm_xa_pre", "norm_xa_post", "norm_ffn_pre",
                           "norm_ffn_post")}
    p.update(
        bf_pad=jnp.pad(w["b_forget"], ((0, 0), (0, LANES - HEADS))),
        w_pool_bd=w_pool_bd.astype(BF16), w_pool_bd_t=w_pool_bd.T.astype(BF16), pool_scale=w["pool_scale"].reshape(1, D_POOL))
    return p


def _w_in_param(stacked):
    return jnp.pad(jnp.concatenate(list(stacked), axis=1), ((0, 0), (0, D_IN_PAD - D_IN)))


def _rest_params(w, full, conv_w_full):
    cw2 = conv_w_full.reshape(3, 2, D_FF).transpose(1, 0, 2)
    cwb = jnp.concatenate([cw2, w["conv_b"].reshape(1, 2, D_FF).transpose(1, 0, 2), jnp.zeros((2, 4, D_FF), F32)], axis=1)
    return dict(w_mix_out=full["w_mix_out"].reshape(D, D), w_xq=full["w_xq"].reshape(D, D), w_xkv=full["w_xkv"],
                w_xo=full["w_xo"].reshape(D, D), w_up=full["w_up"], cwb=cwb, w_down=full["w_down"].reshape(D_FF, D))


def _whole_params(w, full, conv_w_full):
    p = _local_params(w)
    p.update(_rest_params(w, full, conv_w_full), w_in=_w_in_param(full["w_in"]))
    return p


def _halved(a):
    return a.reshape(a.shape[:-2] + (2, a.shape[-2] // 2, a.shape[-1]))


class _StepComm:
    def __init__(self, w, shard2d, conv_w, core_id, chip_id):
        self.w, self.shard2d, self.conv_w, self.core_id, self.chip_id = w, shard2d, conv_w, core_id, chip_id
        self.first, self.second = ("w_up", "w_down"), ("w_xq", "w_xkv", "w_xo")
        self.early = self.first + self.second
        self.late = ("w_in", "w_mix_out")

    def gather_first(self):
        return _all_gather_weights([_halved(self.shard2d[n].astype(BF16)) for n in FIRST], [])

    def first_landed(self, p, landed):
        p["w_in"] = _w_in_param(landed[0].reshape((N_CHIPS,) + self.shard2d["w_in"].shape))

    def gather_rest(self, p):
        return _all_gather_weights([_halved(self.shard2d[n].astype(BF16)) for n in REST], [self.conv_w.reshape(3, -1)])

    def weights_landed(self, p, landed):
        full = {n: a.reshape((N_CHIPS,) + self.shard2d[n].shape) for n, a in zip(REST, landed)}
        conv_w_full = jnp.transpose(landed[-1], (1, 0, 2)).reshape(3, 2 * D_FF)
        p.update(_rest_params(self.w, full, conv_w_full))

    def _view(self, g, n):
        return _halved(g[n].reshape((N_CHIPS,) + self.shard2d[n].shape))

    def swap_first(self, g):
        return _swap_halves([self._view(g, n) for n in self.first])

    def first_swapped(self, landed):
        self.from_sibling = dict(zip(self.first, landed))

    def swap_second(self, g):
        return _swap_halves([self._view(g, n) for n in self.second])

    def second_swapped(self, landed):
        self.from_sibling.update(zip(self.second, landed))

    def scatter_early(self, g):
        self.partial = [_chip_sum("chip_sum_" + n, self.core_id, self._view(g, n), self.from_sibling[n]) for n in self.early]
        return _scatter_chips(self.partial)

    def scatter_landed(self, landed):
        self.received = list(landed)

    def swap_reduced_early(self):
        self.reduced = [_mesh_sum("mesh_sum_" + n, self.chip_id, r, own)
                        for n, r, own in zip(self.early, self.received, self.partial)]
        return _swap_reduced(self.reduced)

    def reduced_landed(self, landed):
        self.reduced_sibling = list(landed)

    def scatter_late(self, g):
        gw_in = g["w_in"][:, :D_IN]
        cols = D_IN // N_CHIPS
        views = [_halved(jnp.stack([gw_in[:, cols * j:cols * (j + 1)] for j in range(N_CHIPS)])), self._view(g, "w_mix_out")]
        from_sibling = _swap_halves(views).run("swap_halves_late")
        self.partial_late = [_chip_sum("chip_sum_" + n, self.core_id, view, other)
                             for n, view, other in zip(self.late, views, from_sibling)]
        return _scatter_chips(self.partial_late)

    def late_landed(self, landed):
        self.received_late = list(landed)


def kernel(x, mem, norm_mix_pre, norm_mix_post, w_in, b_forget, w_pool, pool_scale, w_mix_out, norm_mem, norm_xa_pre, norm_xa_post, w_xq, w_xkv, w_xo, norm_ffn_pre, norm_ffn_post, w_up, conv_w, conv_b, w_down, loss_target, m_norm_mix_pre, m_norm_mix_post, m_w_in, m_b_forget, m_w_pool, m_pool_scale, m_w_mix_out, m_norm_mem, m_norm_xa_pre, m_norm_xa_post, m_w_xq, m_w_xkv, m_w_xo, m_norm_ffn_pre, m_norm_ffn_post, m_w_up, m_conv_w, m_conv_b, m_w_down, v_norm_mix_pre, v_norm_mix_post, v_w_in, v_b_forget, v_w_pool, v_pool_scale, v_w_mix_out, v_norm_mem, v_norm_xa_pre, v_norm_xa_post, v_w_xq, v_w_xkv, v_w_xo, v_norm_ffn_pre, v_norm_ffn_post, v_w_up, v_conv_w, v_conv_b, v_w_down):
    w = dict(norm_mix_pre=norm_mix_pre, norm_mix_post=norm_mix_post, w_in=w_in, b_forget=b_forget, w_pool=w_pool,
             pool_scale=pool_scale, w_mix_out=w_mix_out, norm_mem=norm_mem, norm_xa_pre=norm_xa_pre, norm_xa_post=norm_xa_post,
             w_xq=w_xq, w_xkv=w_xkv, w_xo=w_xo, norm_ffn_pre=norm_ffn_pre, norm_ffn_post=norm_ffn_post, w_up=w_up,
             conv_w=conv_w, conv_b=conv_b, w_down=w_down)
    m = dict(norm_mix_pre=m_norm_mix_pre, norm_mix_post=m_norm_mix_post, w_in=m_w_in, b_forget=m_b_forget, w_pool=m_w_pool,
             pool_scale=m_pool_scale, w_mix_out=m_w_mix_out, norm_mem=m_norm_mem, norm_xa_pre=m_norm_xa_pre,
             norm_xa_post=m_norm_xa_post, w_xq=m_w_xq, w_xkv=m_w_xkv, w_xo=m_w_xo, norm_ffn_pre=m_norm_ffn_pre,
             norm_ffn_post=m_norm_ffn_post, w_up=m_w_up, conv_w=m_conv_w, conv_b=m_conv_b, w_down=m_w_down)
    v = dict(norm_mix_pre=v_norm_mix_pre, norm_mix_post=v_norm_mix_post, w_in=v_w_in, b_forget=v_b_forget, w_pool=v_w_pool,
             pool_scale=v_pool_scale, w_mix_out=v_w_mix_out, norm_mem=v_norm_mem, norm_xa_pre=v_norm_xa_pre,
             norm_xa_post=v_norm_xa_post, w_xq=v_w_xq, w_xkv=v_w_xkv, w_xo=v_w_xo, norm_ffn_pre=v_norm_ffn_pre,
             norm_ffn_post=v_norm_ffn_post, w_up=v_w_up, conv_w=v_conv_w, conv_b=v_conv_b, w_down=v_w_down)
    chip = 2 * lax.axis_index("x") + lax.axis_index("y")

    core_id = lax.axis_index("c").astype(jnp.int32).reshape(1)
    chip_id = chip.astype(jnp.int32).reshape(1)

    shard2d = {n: w[n][0] for n in BIG}
    p = _local_params(w)
    comm = _StepComm(w, shard2d, conv_w, core_id, chip_id)
    grad_x, g, loss_cols = _local_step(x[0], mem[0], loss_target[0], p, comm)

    reduced_late = [_mesh_sum("mesh_sum_" + n, chip_id, r, own)
                    for n, r, own in zip(comm.late, comm.received_late, comm.partial_late)]
    names = comm.late + comm.early
    reduced = reduced_late + comm.reduced
    reduced_sibling = list(_swap_reduced(reduced_late).run("swap_reduced_late")) + comm.reduced_sibling
    grads = {}

    gw_pool = jnp.stack([g["w_pool_full"][64 * gi:64 * (gi + 1), 64 * gi:64 * (gi + 1)] for gi in range(4)])
    dcwb = g["cwb"]
    g_conv_w = dcwb[:, 0:3, :].transpose(1, 0, 2).reshape(3, 2 * D_FF)
    g_conv_b = dcwb[:, 3, :].reshape(2 * D_FF)
    small_g = dict(norm_mix_pre=g["norm_mix_pre"], norm_mix_post=g["norm_mix_post"], b_forget=g["bf_pad"][:, :HEADS],
                   w_pool=gw_pool, pool_scale=g["pool_scale"], norm_mem=g["norm_mem"], norm_xa_pre=g["norm_xa_pre"],
                   norm_xa_post=g["norm_xa_post"], norm_ffn_pre=g["norm_ffn_pre"], norm_ffn_post=g["norm_ffn_post"],
                   conv_b=g_conv_b)
    local_buf, offs = _pack([small_g[n] for n in SMALL] + [g_conv_w, loss_cols])

    delta, new_m, new_v = {}, {}, {}
    for n, g_mine, g_sibling in zip(names, reduced, reduced_sibling):
        gn, d, nm, nv = _adamw_halves("adamw_" + n, core_id, shard2d[n], g_mine, g_sibling, m[n][0], v[n][0])
        grads[n], delta[n], new_m[n], new_v[n] = gn[None], d[None], nm[None], nv[None]
    place = (2 * chip + lax.axis_index("c")).astype(jnp.int32).reshape(1)
    buf = _sum_devices(place, _gather_small(local_buf).run("gather_small")[0], local_buf)
    for n, off in zip(SMALL, offs):
        grads[n] = _unpack(buf, off, w[n])
    g_conv_w = _unpack(buf, offs[len(SMALL)], g_conv_w)
    grads["conv_w"] = lax.dynamic_slice_in_dim(g_conv_w, chip * (2 * D_FF // N_CHIPS), 2 * D_FF // N_CHIPS, axis=1).reshape(conv_w.shape)
    loss = jnp.sum(_unpack(buf, offs[len(SMALL) + 1], loss_cols))
    small_names = SMALL + ("conv_w",)
    packed = [_pack([d[n] for n in small_names])[0] for d in (w, grads, m, v)]
    offs = _pack([w[n] for n in small_names])[1]
    d, nm, nv = _adamw("adamw_small", *packed)
    for n, off in zip(small_names, offs):
        delta[n], new_m[n], new_v[n] = _unpack(d, off, w[n]), _unpack(nm, off, w[n]), _unpack(nv, off, w[n])

    return (loss, grad_x[None], *[grads[n] for n in ORDER], *[delta[n] for n in ORDER], *[new_m[n] for n in ORDER],
            *[new_v[n] for n in ORDER])
```

```python
import functools

import jax
import jax.numpy as jnp
from jax import lax
from jax.experimental import pallas as pl
from jax.experimental.pallas import tpu as pltpu

F32 = jnp.float32
BF16 = jnp.bfloat16
MESH = pl.DeviceIdType.MESH
ANY = pl.BlockSpec(memory_space=pl.ANY)
VMEM_SPEC = pl.BlockSpec(memory_space=pltpu.VMEM)

S = 4096
D = 1024
MEM = 256
D_POOL = 256
HEADS = 12
DH = 64
D_FOX = HEADS * DH
D_IN = D_POOL + 3 * D_FOX + HEADS
F_OFF = D_POOL + 3 * D_FOX
Q_OFF, K_OFF, V_OFF = D_POOL, D_POOL + D_FOX, D_POOL + 2 * D_FOX
XA_HEADS = 4
XA_DH = 256
D_FF = 4096
EPS = 1e-6
N_CHIPS = 4
ADAM_LR, ADAM_B1, ADAM_B2, ADAM_EPS, ADAM_WD, ADAM_STEP = 0.001, 0.9, 0.999, 1e-08, 0.01, 10

LANES = 128
SUBLANES = 8
D_IN_PAD = 21 * LANES
TR = 512
TILE_BYTES = 2 * 1024 * 1024
NEG = -1e30
VMEM_LIMIT = 52 * 1024 * 1024

NN = (((1,), (0,)), ((), ()))
NT = (((1,), (1,)), ((), ()))
TN = (((0,), (0,)), ((), ()))


def _dot(a, b, dims=NN):
    return lax.dot_general(a, b, dims, preferred_element_type=F32)


def _params(sem):
    return pltpu.CompilerParams(dimension_semantics=sem, vmem_limit_bytes=VMEM_LIMIT)


def _split3(x):
    hi = x.astype(BF16)
    r = x - hi.astype(F32)
    mid = r.astype(BF16)
    lo = (r - mid.astype(F32)).astype(BF16)
    return hi, mid, lo


def _split3_f32(x):
    hi = x.astype(BF16).astype(F32)
    r = x - hi
    mid = r.astype(BF16).astype(F32)
    return hi, mid, r - mid


def _lane_iota(shape):
    return lax.broadcasted_iota(jnp.int32, shape, len(shape) - 1)


def _row_iota(shape):
    return lax.broadcasted_iota(jnp.int32, shape, len(shape) - 2)


def _mm(name, a, b, a_spec, b_spec, out_shape, out_spec, grid, dims, acc_shape, ex=None):
    nk = grid[2]
    if ex is not None:
        return _mm_hosting(name, a, b, a_spec, b_spec, out_shape, out_spec, grid, dims, ex)

    def body(a_ref, b_ref, o_ref, *scr):
        p = _dot(a_ref[...], b_ref[...], dims)
        if nk == 1:
            o_ref[...] = p.astype(o_ref.dtype)
        else:
            acc = scr[0]
            k = pl.program_id(2)

            @pl.when(k == 0)
            def _():
                acc[...] = p

            @pl.when(k > 0)
            def _():
                acc[...] += p

            @pl.when(k == nk - 1)
            def _():
                o_ref[...] = acc[...].astype(o_ref.dtype)

    return pl.pallas_call(
        body, name=name, grid=grid, in_specs=[a_spec, b_spec], out_specs=out_spec, out_shape=out_shape,
        scratch_shapes=[pltpu.VMEM(acc_shape, F32)] if nk > 1 else [],
        compiler_params=_params(("parallel", "parallel", "arbitrary")),
    )(a, b)


def _mm_hosting(name, a, b, a_spec, b_spec, out_shape, out_spec, grid, dims, ex):
    assert grid[2] == 1
    n = len(ex.ins)

    def body(*refs):
        i, j = pl.program_id(0), pl.program_id(1)
        first = (i == 0) & (j == 0)
        (a_ref, b_ref), (o_ref,), _, begin, end = _hosted(
            ex, refs, 2, 1, first, first, (i == grid[0] - 1) & (j == grid[1] - 1))
        begin()
        o_ref[...] = _dot(a_ref[...], b_ref[...], dims).astype(o_ref.dtype)
        end()

    res = pl.pallas_call(
        body, name=name, grid=grid, in_specs=[a_spec, b_spec] + [ANY] * n, out_specs=[out_spec] + [ANY] * n,
        out_shape=[out_shape] + ex.out_shapes, scratch_shapes=ex.scratch(),
        compiler_params=_params(("arbitrary", "arbitrary", "arbitrary")),
    )(a, b, *ex.ins)
    return res[0], res[1:]


def _mm_nn(name, a, b, out_dtype, tm, tn):
    m, k = a.shape
    n = b.shape[1]
    return _mm(name, a, b, pl.BlockSpec((tm, k), lambda i, j, kk: (i, 0)), pl.BlockSpec((k, tn), lambda i, j, kk: (0, j)),
               jax.ShapeDtypeStruct((m, n), out_dtype), pl.BlockSpec((tm, tn), lambda i, j, kk: (i, j)),
               (m // tm, n // tn, 1), NN, (tm, tn))


def _mm_nt(name, a, b, out_dtype, tm, tn, ex=None):
    m, k = a.shape
    n = b.shape[0]
    return _mm(name, a, b, pl.BlockSpec((tm, k), lambda i, j, kk: (i, 0)), pl.BlockSpec((tn, k), lambda i, j, kk: (j, 0)),
               jax.ShapeDtypeStruct((m, n), out_dtype), pl.BlockSpec((tm, tn), lambda i, j, kk: (i, j)),
               (m // tm, n // tn, 1), NT, (tm, tn), ex)


def _mm_tn(name, a, b, tka, tn, ex=None):
    t, ka = a.shape
    n = b.shape[1]
    return _mm(name, a, b, pl.BlockSpec((t, tka), lambda i, j, kk: (0, i)), pl.BlockSpec((t, tn), lambda i, j, kk: (0, j)),
               jax.ShapeDtypeStruct((ka, n), F32), pl.BlockSpec((tka, tn), lambda i, j, kk: (i, j)),
               (ka // tka, n // tn, 1), TN, (tka, tn), ex)


def _d_h3(dhid, w_up, ex):
    tm = tn = 512
    shard = 2 * D_FF // N_CHIPS
    per_plane = D_FF // shard
    grid = (S // tm, D // tn)
    n = len(ex.ins)

    def body(*refs):
        i, j = pl.program_id(0), pl.program_id(1)
        first = (i == 0) & (j == 0)
        (a_ref, b_ref), (o_ref,), _, begin, end = _hosted(ex, refs, 2, 1, first, first, (i == grid[0] - 1) & (j == grid[1] - 1))
        begin()
        acc = None
        for k in range(N_CHIPS):
            cols = slice(shard * (k % per_plane), shard * (k % per_plane + 1))
            part = _dot(a_ref[k // per_plane, :, cols], b_ref[k], NT)
            acc = part if acc is None else acc + part
        o_ref[...] = acc
        end()

    res = pl.pallas_call(
        body, name="d_h3", grid=grid,
        in_specs=[pl.BlockSpec((2, tm, D_FF), lambda i, j: (0, i, 0)),
                  pl.BlockSpec((N_CHIPS, tn, shard), lambda i, j: (0, j, 0))] + [ANY] * n,
        out_specs=[pl.BlockSpec((tm, tn), lambda i, j: (i, j))] + [ANY] * n,
        out_shape=[jax.ShapeDtypeStruct((S, D), F32)] + ex.out_shapes, scratch_shapes=ex.scratch(),
        compiler_params=_params(("arbitrary", "arbitrary")),
    )(dhid, w_up, *ex.ins)
    return res[0], res[1:]


def _rms(x, g):
    r = lax.rsqrt(jnp.mean(x * x, axis=-1, keepdims=True) + EPS)
    return x * r * g


def _rms_bwd(x, g, dy):
    r = lax.rsqrt(jnp.mean(x * x, axis=-1, keepdims=True) + EPS)
    xh = x * r
    dxh = dy * g
    dx = r * (dxh - xh * jnp.mean(dxh * xh, axis=-1, keepdims=True))
    return dx, jnp.sum(dy * xh, axis=0, keepdims=True)


def _row_spec(tr, width):
    return pl.BlockSpec((tr, width), lambda i: (i, 0))


def _vec_spec(width):
    return pl.BlockSpec((1, width), lambda i: (0, 0))


def _norm_fwd(name, x, g, ex=None):
    rows, width = x.shape
    tr = min(TR, rows)
    steps = rows // tr
    hosted = ex if ex is not None else _no_exchange()
    n = len(hosted.ins)

    def body(*refs):
        i = pl.program_id(0)
        (x_ref, g_ref), (h_ref,), _, begin, end = _hosted(hosted, refs, 2, 1, i == 0, i == 0, i == steps - 1)
        begin()
        h_ref[...] = _rms(x_ref[...], g_ref[...]).astype(BF16)
        end()

    res = pl.pallas_call(
        body, name=name, grid=(steps,), in_specs=[_row_spec(tr, width), _vec_spec(width)] + [ANY] * n,
        out_specs=[_row_spec(tr, width)] + [ANY] * n,
        out_shape=[jax.ShapeDtypeStruct((rows, width), BF16)] + hosted.out_shapes, scratch_shapes=hosted.scratch(),
        compiler_params=_params(("arbitrary",)),
    )(x, g, *hosted.ins)
    return res[0] if ex is None else (res[0], res[1:])


def _proj_resid_norm(name, a, w, xp, g_post, g_pre, w_next=None):
    def body(a_ref, w_ref, xp_ref, gpost_ref, gpre_ref, *rest):
        y_ref, xn_ref, h_ref = rest[-3:] if w_next is None else rest[1:4]
        y = _dot(a_ref[...], w_ref[...])
        y_ref[...] = y
        xn = xp_ref[...] + _rms(y, gpost_ref[...])
        xn_ref[...] = xn
        h = _rms(xn, gpre_ref[...]).astype(BF16)
        h_ref[...] = h
        if w_next is not None:
            rest[4][...] = _dot(h, rest[0][...]).astype(BF16)

    mat = pl.BlockSpec((D, D), lambda i: (0, 0))
    more = [] if w_next is None else [w_next]
    return pl.pallas_call(
        body, name=name, grid=(S // TR,),
        in_specs=[_row_spec(TR, D), mat, _row_spec(TR, D), _vec_spec(D), _vec_spec(D)] + [mat] * len(more),
        out_specs=[_row_spec(TR, D)] * (3 + len(more)),
        out_shape=[jax.ShapeDtypeStruct((S, D), F32), jax.ShapeDtypeStruct((S, D), F32), jax.ShapeDtypeStruct((S, D), BF16)]
        + [jax.ShapeDtypeStruct((S, D), BF16)] * len(more),
        compiler_params=_params(("parallel",)),
    )(a, w, xp, g_post, g_pre, *more)


def _down_loss_bwd(act, w_down, x3, g_post, target):
    def body(a_ref, w_ref, x_ref, g_ref, t_ref, dres_ref, dy_ref, dg_ref, loss_ref):
        i = pl.program_id(0)

        @pl.when(i == 0)
        def _():
            dg_ref[...] = jnp.zeros_like(dg_ref)
            loss_ref[...] = jnp.zeros_like(loss_ref)

        y = _dot(a_ref[...], w_ref[...])
        g = g_ref[...]
        e = x_ref[...] + _rms(y, g) - t_ref[...]
        loss_ref[...] += jnp.sum(e * e, axis=0, keepdims=True) * (0.5 / D)
        dres = e * (1.0 / D)
        dres_ref[...] = dres
        dy, dg = _rms_bwd(y, g, dres)
        dy_ref[...] = dy.astype(BF16)
        dg_ref[...] += dg

    return pl.pallas_call(
        body, name="down_loss_bwd", grid=(S // TR,),
        in_specs=[_row_spec(TR, D_FF), pl.BlockSpec((D_FF, D), lambda i: (0, 0)), _row_spec(TR, D), _vec_spec(D),
                  _row_spec(TR, D)],
        out_specs=[_row_spec(TR, D), _row_spec(TR, D), _vec_spec(D), _vec_spec(D)],
        out_shape=[jax.ShapeDtypeStruct((S, D), F32), jax.ShapeDtypeStruct((S, D), BF16),
                   jax.ShapeDtypeStruct((1, D), F32), jax.ShapeDtypeStruct((1, D), F32)],
        compiler_params=_params(("arbitrary",)),
    )(act, w_down, x3, g_post, target)


def _mid_bwd(name, dres, xcur, g_pre, dh, yprev, g_post):
    def body(dres_ref, x_ref, gpre_ref, dh_ref, y_ref, gpost_ref, dx_ref, dy_ref, dgpre_ref, dgpost_ref):
        i = pl.program_id(0)

        @pl.when(i == 0)
        def _():
            dgpre_ref[...] = jnp.zeros_like(dgpre_ref)
            dgpost_ref[...] = jnp.zeros_like(dgpost_ref)

        dxn, dgpre = _rms_bwd(x_ref[...], gpre_ref[...], dh_ref[...])
        dx = dres_ref[...] + dxn
        dx_ref[...] = dx
        dy, dgpost = _rms_bwd(y_ref[...], gpost_ref[...], dx)
        dy_ref[...] = dy.astype(BF16)
        dgpre_ref[...] += dgpre
        dgpost_ref[...] += dgpost

    return pl.pallas_call(
        body, name=name, grid=(S // TR,),
        in_specs=[_row_spec(TR, D), _row_spec(TR, D), _vec_spec(D), _row_spec(TR, D), _row_spec(TR, D), _vec_spec(D)],
        out_specs=[_row_spec(TR, D), _row_spec(TR, D), _vec_spec(D), _vec_spec(D)],
        out_shape=[jax.ShapeDtypeStruct((S, D), F32), jax.ShapeDtypeStruct((S, D), BF16),
                   jax.ShapeDtypeStruct((1, D), F32), jax.ShapeDtypeStruct((1, D), F32)],
        compiler_params=_params(("arbitrary",)),
    )(dres, xcur, g_pre, dh, yprev, g_post)


def _first_bwd(dres, x, g, dh):
    def body(dres_ref, x_ref, g_ref, dh_ref, dx_ref, dg_ref):
        i = pl.program_id(0)

        @pl.when(i == 0)
        def _():
            dg_ref[...] = jnp.zeros_like(dg_ref)

        dxn, dg = _rms_bwd(x_ref[...], g_ref[...], dh_ref[...])
        dx_ref[...] = dres_ref[...] + dxn
        dg_ref[...] += dg

    return pl.pallas_call(
        body, name="first_bwd", grid=(S // TR,),
        in_specs=[_row_spec(TR, D), _row_spec(TR, D), _vec_spec(D), _row_spec(TR, D)],
        out_specs=[_row_spec(TR, D), _vec_spec(D)],
        out_shape=[jax.ShapeDtypeStruct((S, D), F32), jax.ShapeDtypeStruct((1, D), F32)],
        compiler_params=_params(("arbitrary",)),
    )(dres, x, g, dh)


def _gain_bwd(name, x, g, dy):
    rows, width = x.shape

    def body(x_ref, g_ref, dy_ref, dg_ref):
        _, dg = _rms_bwd(x_ref[...], g_ref[...], dy_ref[...])
        dg_ref[...] = dg

    return pl.pallas_call(
        body, name=name, grid=(1,), in_specs=[_row_spec(rows, width), _vec_spec(width), _row_spec(rows, width)],
        out_specs=_vec_spec(width), out_shape=jax.ShapeDtypeStruct((1, width), F32),
        compiler_params=_params(("arbitrary",)),
    )(x, g, dy)


CUM_Q = DH
CUM_K = DH + 3
LSE_Q = DH + 6
DEN_V = DH
DELTA = DH + 1
PREP_TR = 256
FOX_FWD_BLOCK = 1024
FOX_BWD_BLOCK = 512


def _at(lane_of_even_head, h):
    return (lane_of_even_head + DH * (h % 2)) % LANES


def _data_lanes(lane, h):
    return lane >= DH if h % 2 else lane < DH


def _pair_block(ref, off, h):
    base = ((off + DH * h) // LANES) * LANES
    return ref[:, base:base + LANES]


def _cumsum_rows(x, tri, carry):
    hi, mid, lo = _split3(x)
    return _dot(tri, hi) + _dot(tri, mid) + _dot(tri, lo) + carry


def _fox_prep(proj, bf_pad):
    tr = PREP_TR

    def body(proj_ref, bf_ref, qa_ref, ka_ref, va_ref, carry_ref):
        i = pl.program_id(0)

        @pl.when(i == 0)
        def _():
            carry_ref[...] = jnp.zeros_like(carry_ref)

        lane = _lane_iota((tr, LANES))
        z = proj_ref[:, F_OFF:F_OFF + LANES] + bf_ref[...]
        log_f = jnp.minimum(z, 0.0) - jnp.log(1.0 + jnp.exp(-jnp.abs(z)))
        log_f = jnp.where(lane < HEADS, log_f, 0.0)
        tri = jnp.where(_row_iota((tr, tr)) >= _lane_iota((tr, tr)), 1.0, 0.0).astype(BF16)
        cum = _cumsum_rows(log_f, tri, carry_ref[0:1, :])
        carry_ref[0:1, :] = cum[tr - 1:tr, :]

        def between(first, h):
            return (lane >= _at(first, h)) & (lane < _at(first, h) + 3)

        ones_q = [jnp.where(between(CUM_K, h), 1.0, 0.0) for h in range(2)]
        ones_k = [jnp.where(between(CUM_Q, h) | between(LSE_Q, h), 1.0, 0.0) for h in range(2)]
        aug_v = [jnp.where(lane == _at(DEN_V, h), 1.0, jnp.where(between(DELTA, h), -1.0, 0.0)) for h in range(2)]
        for h in range(HEADS):
            c_hi, c_mid, c_lo = _split3_f32(jnp.broadcast_to(cum[:, h:h + 1], (tr, LANES)))
            cq, ck = _at(CUM_Q, h), _at(CUM_K, h)
            aug_q = jnp.where(lane == cq, c_hi, jnp.where(lane == cq + 1, c_mid, jnp.where(lane == cq + 2, c_lo, ones_q[h % 2])))
            aug_k = jnp.where(lane == ck, -c_hi, jnp.where(lane == ck + 1, -c_mid, jnp.where(lane == ck + 2, -c_lo, ones_k[h % 2])))
            data = _data_lanes(lane, h)
            qa_ref[h] = jnp.where(data, _pair_block(proj_ref, Q_OFF, h) * (DH ** -0.5), aug_q).astype(BF16)
            ka_ref[h] = jnp.where(data, _pair_block(proj_ref, K_OFF, h), aug_k).astype(BF16)
            va_ref[h] = jnp.where(data, _pair_block(proj_ref, V_OFF, h), aug_v[h % 2]).astype(BF16)

    head_spec = pl.BlockSpec((HEADS, tr, LANES), lambda i: (0, i, 0))
    head_shape = jax.ShapeDtypeStruct((HEADS, S, LANES), BF16)
    return pl.pallas_call(
        body, name="fox_prep", grid=(S // tr,), in_specs=[_row_spec(tr, D_IN_PAD), _vec_spec(LANES)],
        out_specs=[head_spec] * 3, out_shape=[head_shape] * 3, scratch_shapes=[pltpu.VMEM((SUBLANES, LANES), F32)],
        compiler_params=_params(("arbitrary",)),
    )(proj, bf_pad)


def _hosted(ex, refs, n_blocked_in, n_blocked_out, first, forward_at, last):
    n = len(ex.ins)
    own_in = refs[:n_blocked_in]
    ex_in = refs[n_blocked_in:n_blocked_in + n]
    own_out = refs[n_blocked_in + n:n_blocked_in + n + n_blocked_out]
    ex_out = refs[n_blocked_in + n + n_blocked_out:n_blocked_in + 2 * n + n_blocked_out]
    rest = refs[n_blocked_in + 2 * n + n_blocked_out:]
    args = (ex_in, ex_out, rest[-2], rest[-1])

    def begin():
        @pl.when(first)
        def _():
            ex.start(*args)

        @pl.when(forward_at)
        def _():
            ex.forward(*args)

    def end():
        @pl.when(last)
        def _():
            ex.finish(*args)

    return own_in, own_out, rest[:-2], begin, end


def _fox_fwd(qa, ka, va, ex):
    BQ = BK = FOX_FWD_BLOCK
    nq = S // BQ
    n_pairs = HEADS // 2

    def body(*refs):
        p_id, i = pl.program_id(0), pl.program_id(1)
        (qa_ref, ka_ref, va_ref), (y_ref, qab_ref), (m_scr, acc_scr), begin, end = _hosted(
            ex, refs, 3, 2, (p_id == 0) & (i == 0), (p_id == n_pairs - 1) & (i == 0), (p_id == n_pairs - 1) & (i == nq - 1))
        begin()
        lane = _lane_iota((BQ, LANES))
        causal = _row_iota((BQ, BK)) >= _lane_iota((BQ, BK))
        m_scr[...] = jnp.full_like(m_scr, NEG)
        acc_scr[...] = jnp.zeros_like(acc_scr)

        def step(j, masked):
            rows = pl.ds(pl.multiple_of(j * BK, BK), BK)
            for hh in range(2):
                s = _dot(qa_ref[hh], ka_ref[hh, rows, :], NT)
                if masked:
                    s = jnp.where(causal, s, NEG)
                m_prev = m_scr[hh]
                m_new = jnp.maximum(m_prev, jnp.max(s, axis=1, keepdims=True))
                p = jnp.exp(s - jnp.tile(m_new, (1, BK // LANES)))
                acc_scr[hh] = jnp.exp(m_prev - m_new) * acc_scr[hh] + _dot(p.astype(BF16), va_ref[hh, rows, :])
                m_scr[hh] = m_new

        def full_step(j, carry):
            step(j, False)
            return carry

        lax.fori_loop(0, i, full_step, 0)
        step(i, True)
        outs = []
        for hh in range(2):
            acc = acc_scr[hh]
            den_lane, lse_lane = _at(DEN_V, hh), _at(LSE_Q, hh)
            den = jnp.broadcast_to(acc[:, den_lane:den_lane + 1], (BQ, LANES))
            outs.append(acc * (1.0 / den))
            n_hi, n_mid, n_lo = _split3(-(m_scr[hh] + jnp.log(den)))
            qab_ref[hh] = jnp.where(lane == lse_lane, n_hi,
                                    jnp.where(lane == lse_lane + 1, n_mid, jnp.where(lane == lse_lane + 2, n_lo, qa_ref[hh])))
        y_ref[...] = jnp.where(lane < DH, outs[0], outs[1]).astype(BF16)
        end()

    pair_rows = pl.BlockSpec((2, BQ, LANES), lambda p, i: (p, i, 0))
    pair_all = pl.BlockSpec((2, S, LANES), lambda p, i: (p, 0, 0))
    n = len(ex.ins)
    res = pl.pallas_call(
        body, name="fox_fwd", grid=(n_pairs, nq), in_specs=[pair_rows, pair_all, pair_all] + [ANY] * n,
        out_specs=[pl.BlockSpec((BQ, LANES), lambda p, i: (i, D_POOL // LANES + p)), pair_rows] + [ANY] * n,
        out_shape=[jax.ShapeDtypeStruct((S, D), BF16), jax.ShapeDtypeStruct((HEADS, S, LANES), BF16)] + ex.out_shapes,
        scratch_shapes=[pltpu.VMEM((2, BQ, LANES), F32), pltpu.VMEM((2, BQ, LANES), F32)] + ex.scratch(),
        compiler_params=_params(("arbitrary", "arbitrary")),
    )(qa, ka, va, *ex.ins)
    return res[0], res[1], res[2:]


def _bwd_xa_mix(dqx, w_xq, dres, x2, g_pre, y1, g_post, w_mix_out, ycat, ex):
    steps = S // TR
    n = len(ex.ins)

    def body(*refs):
        i = pl.program_id(0)
        ((dq_ref, wq_ref, dres_ref, x_ref, gpre_ref, y_ref, gpost_ref, wm_ref, ycat_ref),
         (dx_ref, dy_ref, dgpre_ref, dgpost_ref, dp_ref, doa_ref), _, begin, end) = _hosted(
            ex, refs, 9, 6, i == 0, i == 0, i == steps - 1)
        begin()

        @pl.when(i == 0)
        def _():
            dgpre_ref[...] = jnp.zeros_like(dgpre_ref)
            dgpost_ref[...] = jnp.zeros_like(dgpost_ref)

        dxn, dgpre = _rms_bwd(x_ref[...], gpre_ref[...], _dot(dq_ref[...], wq_ref[...], NT))
        dx = dres_ref[...] + dxn
        dx_ref[...] = dx
        dy, dgpost = _rms_bwd(y_ref[...], gpost_ref[...], dx)
        dy = dy.astype(BF16)
        dy_ref[...] = dy
        dgpre_ref[...] += dgpre
        dgpost_ref[...] += dgpost

        d = _dot(dy, wm_ref[...], NT)
        dp_ref[...] = d[:, :D_POOL]
        lane = _lane_iota((TR, LANES))
        low = lane < DH
        for p in range(HEADS // 2):
            cols = slice(D_POOL + LANES * p, D_POOL + LANES * (p + 1))
            do = d[:, cols]
            prod = do * ycat_ref[:, cols].astype(F32)
            deltas = (jnp.sum(jnp.where(low, prod, 0.0), axis=1, keepdims=True),
                      jnp.sum(jnp.where(low, 0.0, prod), axis=1, keepdims=True))
            for hh in range(2):
                d_hi, d_mid, d_lo = _split3_f32(deltas[hh])
                dl = _at(DELTA, hh)
                aug = jnp.where(lane == dl, d_hi, jnp.where(lane == dl + 1, d_mid, jnp.where(lane == dl + 2, d_lo, 0.0)))
                doa_ref[2 * p + hh] = jnp.where(_data_lanes(lane, hh), do, aug).astype(BF16)
        end()

    mat = pl.BlockSpec((D, D), lambda i: (0, 0))
    res = pl.pallas_call(
        body, name="bwd_xa_mix", grid=(steps,),
        in_specs=[_row_spec(TR, D), mat, _row_spec(TR, D), _row_spec(TR, D), _vec_spec(D), _row_spec(TR, D), _vec_spec(D), mat,
                  _row_spec(TR, D)] + [ANY] * n,
        out_specs=[_row_spec(TR, D), _row_spec(TR, D), _vec_spec(D), _vec_spec(D), _row_spec(TR, D_POOL),
                   pl.BlockSpec((HEADS, TR, LANES), lambda i: (0, i, 0))] + [ANY] * n,
        out_shape=[jax.ShapeDtypeStruct((S, D), F32), jax.ShapeDtypeStruct((S, D), BF16), jax.ShapeDtypeStruct((1, D), F32),
                   jax.ShapeDtypeStruct((1, D), F32), jax.ShapeDtypeStruct((S, D_POOL), F32),
                   jax.ShapeDtypeStruct((HEADS, S, LANES), BF16)] + ex.out_shapes,
        scratch_shapes=ex.scratch(), compiler_params=_params(("arbitrary",)),
    )(dqx, w_xq, dres, x2, g_pre, y1, g_post, w_mix_out, ycat, *ex.ins)
    return res[:6], res[6:]


def _fox_bwd(qab, doa, ka, va, ex):
    BQ = BK = FOX_BWD_BLOCK
    nk = S // BK
    n_pairs = HEADS // 2

    def body(*refs):
        p_id, j = pl.program_id(0), pl.program_id(1)
        (qab_ref, doa_ref, ka_ref, va_ref), (dqa_ref, dka_ref, dva_ref), _, begin, end = _hosted(
            ex, refs, 4, 3, (p_id == 0) & (j == 0), (p_id == n_pairs - 1) & (j == 0), (p_id == n_pairs - 1) & (j == nk - 1))
        begin()

        @pl.when(j == 0)
        def _():
            dqa_ref[...] = jnp.zeros_like(dqa_ref)

        causal = _row_iota((BQ, BK)) >= _lane_iota((BQ, BK))
        dka_ref[...] = jnp.zeros_like(dka_ref)
        dva_ref[...] = jnp.zeros_like(dva_ref)

        def step(i, masked):
            rows = pl.ds(pl.multiple_of(i * BQ, BQ), BQ)
            for hh in range(2):
                kb = ka_ref[hh]
                q = qab_ref[hh, rows, :]
                do = doa_ref[hh, rows, :]
                s = _dot(q, kb, NT)
                if masked:
                    s = jnp.where(causal, s, NEG)
                p = jnp.exp(s)
                ds = p * _dot(do, va_ref[hh], NT)
                pb = p.astype(BF16)
                dsb = ds.astype(BF16)
                dva_ref[hh] += _dot(pb, do, TN)
                dka_ref[hh] += _dot(dsb, q, TN)
                dqa_ref[hh, rows, :] += _dot(dsb, kb)

        def full_step(i, carry):
            step(i, False)
            return carry

        step(j, True)
        lax.fori_loop(j + 1, nk, full_step, 0)
        end()

    pair_all = pl.BlockSpec((2, S, LANES), lambda p, j: (p, 0, 0))
    pair_rows = pl.BlockSpec((2, BK, LANES), lambda p, j: (p, j, 0))
    shape = jax.ShapeDtypeStruct((HEADS, S, LANES), F32)
    n = len(ex.ins)
    res = pl.pallas_call(
        body, name="fox_bwd", grid=(n_pairs, nk), in_specs=[pair_all, pair_all, pair_rows, pair_rows] + [ANY] * n,
        out_specs=[pair_all, pair_rows, pair_rows] + [ANY] * n, out_shape=[shape] * 3 + ex.out_shapes,
        scratch_shapes=ex.scratch(), compiler_params=_params(("arbitrary", "arbitrary")),
    )(qab, doa, ka, va, *ex.ins)
    return res[0], res[1], res[2], res[3:]


def _fox_bwd_post(dqa, dka, dva, du, proj, bf_pad):
    tr = PREP_TR
    nt = S // tr

    def body(dqa_ref, dka_ref, dva_ref, du_ref, z_ref, bf_ref, dp_ref, dbf_ref, carry_ref):
        i = pl.program_id(0)

        @pl.when(i == 0)
        def _():
            carry_ref[...] = jnp.zeros_like(carry_ref)
            dbf_ref[...] = jnp.zeros_like(dbf_ref)

        lane = _lane_iota((tr, LANES))
        dcum = jnp.zeros((tr, LANES), F32)
        for h in range(HEADS):
            cq, ck = _at(CUM_Q, h), _at(CUM_K, h)
            dc = dqa_ref[h][:, cq:cq + 1] - dka_ref[h][:, ck:ck + 1]
            dcum = jnp.where(lane == h, dc, dcum)
        tri = jnp.where(_lane_iota((tr, tr)) >= _row_iota((tr, tr)), 1.0, 0.0).astype(BF16)
        dlog_f = _cumsum_rows(dcum, tri, carry_ref[0:1, :])
        carry_ref[0:1, :] = dlog_f[0:1, :]
        z = z_ref[...] + bf_ref[...]
        df = jnp.where(lane < HEADS, dlog_f / (1.0 + jnp.exp(z)), 0.0)
        dbf_ref[...] += jnp.sum(df, axis=0, keepdims=True)

        dp_ref[:, 0:D_POOL] = du_ref[...].astype(BF16)
        low = lane < DH
        for ref, off, scale in ((dqa_ref, Q_OFF, DH ** -0.5), (dka_ref, K_OFF, 1.0), (dva_ref, V_OFF, 1.0)):
            for p in range(HEADS // 2):
                blk = jnp.where(low, ref[2 * p], ref[2 * p + 1])
                dp_ref[:, off + LANES * p:off + LANES * (p + 1)] = (blk * scale).astype(BF16)
        dp_ref[:, F_OFF:F_OFF + LANES] = df.astype(BF16)

    head_spec = pl.BlockSpec((HEADS, tr, LANES), lambda i: (0, nt - 1 - i, 0))
    return pl.pallas_call(
        body, name="fox_bwd_post", grid=(nt,),
        in_specs=[head_spec, head_spec, head_spec, pl.BlockSpec((tr, D_POOL), lambda i: (nt - 1 - i, 0)),
                  pl.BlockSpec((tr, LANES), lambda i: (nt - 1 - i, F_OFF // LANES)), _vec_spec(LANES)],
        out_specs=[pl.BlockSpec((tr, D_IN_PAD), lambda i: (nt - 1 - i, 0)), _vec_spec(LANES)],
        out_shape=[jax.ShapeDtypeStruct((S, D_IN_PAD), BF16), jax.ShapeDtypeStruct((1, LANES), F32)],
        scratch_shapes=[pltpu.VMEM((SUBLANES, LANES), F32)],
        compiler_params=_params(("arbitrary",)),
    )(dqa, dka, dva, du, proj, bf_pad)


POOL_HALO = 16


def _by_group(lane, a2, a4, a8, a16):
    return jnp.where(lane < 64, a2, jnp.where(lane < 128, a4, jnp.where(lane < 192, a8, a16)))


def _window_count(lane, t):
    return jnp.minimum(t + 1, _by_group(lane, 2, 4, 8, 16)).astype(F32)


def _pool_diff(u, halo, first, tile):
    n = TR + POOL_HALO
    ext = jnp.concatenate([jnp.where(first, 0.0, halo), u], axis=0)
    s2 = ext + pltpu.roll(ext, 1, 0)
    s4 = s2 + pltpu.roll(s2, 2, 0)
    s8 = s4 + pltpu.roll(s4, 4, 0)
    s16 = s8 + pltpu.roll(s8, 8, 0)
    lane = _lane_iota((n, D_POOL))
    win = _by_group(lane, s2, s4, s8, s16)[POOL_HALO:]
    lane = _lane_iota((TR, D_POOL))
    t = tile * TR + _row_iota((TR, D_POOL))
    return win / _window_count(lane, t) - u


def _prev_halo(rows, width, col):
    per = TR // rows
    return pl.BlockSpec((rows, width), lambda i: (jnp.maximum(i * per - 1, 0), col))


def _next_halo(rows, width, col):
    per = TR // rows
    return pl.BlockSpec((rows, width), lambda i: (jnp.minimum((i + 1) * per, S // rows - 1), col))


def _pool_fwd(proj, w_bd, ps, ycat):
    def body(u_ref, halo_ref, w_ref, ps_ref, ycat_ref, y_ref):
        i = pl.program_id(0)
        diff = _pool_diff(u_ref[...], halo_ref[...], i == 0, i)
        y_ref[...] = (_dot(diff.astype(BF16), w_ref[...]) * ps_ref[...]).astype(BF16)

    return pl.pallas_call(
        body, name="pool_fwd", grid=(S // TR,),
        in_specs=[_row_spec(TR, D_POOL), _prev_halo(POOL_HALO, D_POOL, 0),
                  pl.BlockSpec((D_POOL, D_POOL), lambda i: (0, 0)), _vec_spec(D_POOL), ANY],
        out_specs=_row_spec(TR, D_POOL), out_shape=jax.ShapeDtypeStruct((S, D), BF16), input_output_aliases={4: 0},
        compiler_params=_params(("parallel",)),
    )(proj, proj, w_bd, ps, ycat)


def _pool_bwd(proj, dycat, w_bd, w_bd_t, ps):
    nt = S // TR
    n = TR + POOL_HALO

    def body(u_ref, halo_ref, dy_ref, dyn_ref, w_ref, wt_ref, ps_ref, du_ref, dw_ref, dps_ref):
        i = pl.program_id(0)

        @pl.when(i == 0)
        def _():
            dw_ref[...] = jnp.zeros_like(dw_ref)
            dps_ref[...] = jnp.zeros_like(dps_ref)

        diff = _pool_diff(u_ref[...], halo_ref[...], i == 0, i).astype(BF16)
        dy = dy_ref[...]
        dps_ref[...] += jnp.sum(dy * _dot(diff, w_ref[...]), axis=0, keepdims=True)
        dy_ext = jnp.concatenate([dy, jnp.where(i == nt - 1, 0.0, dyn_ref[...])], axis=0)
        dmixed = (dy_ext * ps_ref[...]).astype(BF16)
        ddiff = _dot(dmixed, wt_ref[...])
        dw_ref[...] += _dot(diff, dmixed[:TR], TN)
        lane = _lane_iota((n, D_POOL))
        t = i * TR + _row_iota((n, D_POOL))
        e = ddiff / _window_count(lane, t)
        f2 = e + pltpu.roll(e, n - 1, 0)
        f4 = f2 + pltpu.roll(f2, n - 2, 0)
        f8 = f4 + pltpu.roll(f4, n - 4, 0)
        f16 = f8 + pltpu.roll(f8, n - 8, 0)
        du_ref[...] = _by_group(lane, f2, f4, f8, f16)[:TR] - ddiff[:TR]

    mat = pl.BlockSpec((D_POOL, D_POOL), lambda i: (0, 0))
    return pl.pallas_call(
        body, name="pool_bwd", grid=(nt,),
        in_specs=[_row_spec(TR, D_POOL), _prev_halo(POOL_HALO, D_POOL, 0), _row_spec(TR, D_POOL),
                  _next_halo(POOL_HALO, D_POOL, 0), mat, mat, _vec_spec(D_POOL)],
        out_specs=[_row_spec(TR, D_POOL), mat, _vec_spec(D_POOL)],
        out_shape=[jax.ShapeDtypeStruct((S, D_POOL), F32), jax.ShapeDtypeStruct((D_POOL, D_POOL), F32),
                   jax.ShapeDtypeStruct((1, D_POOL), F32)],
        compiler_params=_params(("arbitrary",)),
    )(proj, proj, dycat, dycat, w_bd, w_bd_t, ps)


def _xa_probs(q, k):
    s = _dot(q, k, NT) * (XA_DH ** -0.5)
    e = jnp.exp(s - jnp.max(s, axis=-1, keepdims=True))
    return e * (1.0 / jnp.sum(e, axis=-1, keepdims=True))


def _xattn_fwd(qx, kv):
    def body(q_ref, kv_ref, o_ref):
        for h in range(XA_HEADS):
            cols = slice(XA_DH * h, XA_DH * (h + 1))
            vcols = slice(D + XA_DH * h, D + XA_DH * (h + 1))
            p = _xa_probs(q_ref[:, cols], kv_ref[:, cols])
            o_ref[:, cols] = _dot(p.astype(BF16), kv_ref[:, vcols]).astype(BF16)

    return pl.pallas_call(
        body, name="xattn_fwd", grid=(S // TR,),
        in_specs=[_row_spec(TR, D), pl.BlockSpec((MEM, 2 * D), lambda i: (0, 0))],
        out_specs=_row_spec(TR, D), out_shape=jax.ShapeDtypeStruct((S, D), BF16),
        compiler_params=_params(("parallel",)),
    )(qx, kv)


def _xattn_bwd(qx, kv, dxo):
    def body(q_ref, kv_ref, do_ref, dq_ref, dkv_ref):
        i = pl.program_id(0)

        @pl.when(i == 0)
        def _():
            dkv_ref[...] = jnp.zeros_like(dkv_ref)

        for h in range(XA_HEADS):
            cols = slice(XA_DH * h, XA_DH * (h + 1))
            vcols = slice(D + XA_DH * h, D + XA_DH * (h + 1))
            q = q_ref[:, cols]
            k = kv_ref[:, cols]
            do = do_ref[:, cols]
            p = _xa_probs(q, k)
            dkv_ref[:, vcols] += _dot(p.astype(BF16), do, TN)
            dp = _dot(do, kv_ref[:, vcols], NT)
            ds = (p * (dp - jnp.sum(p * dp, axis=-1, keepdims=True)) * (XA_DH ** -0.5)).astype(BF16)
            dq_ref[:, cols] = _dot(ds, k).astype(BF16)
            dkv_ref[:, cols] += _dot(ds, q, TN)

    kv_spec = pl.BlockSpec((MEM, 2 * D), lambda i: (0, 0))
    return pl.pallas_call(
        body, name="xattn_bwd", grid=(S // TR,), in_specs=[_row_spec(TR, D), kv_spec, _row_spec(TR, D)],
        out_specs=[_row_spec(TR, D), kv_spec],
        out_shape=[jax.ShapeDtypeStruct((S, D), BF16), jax.ShapeDtypeStruct((MEM, 2 * D), F32)],
        compiler_params=_params(("arbitrary",)),
    )(qx, kv, dxo)


CONV_HALO = SUBLANES
TC = 512
GELU_K = 0.7978845608028654
GELU_C = 0.044715


def _conv3(ext, w, rows):
    h0 = ext[CONV_HALO:CONV_HALO + rows]
    h1 = pltpu.roll(ext, 1, 0)[CONV_HALO:CONV_HALO + rows]
    h2 = pltpu.roll(ext, 2, 0)[CONV_HALO:CONV_HALO + rows]
    return w[2:3] * h0 + w[1:2] * h1 + w[0:1] * h2 + w[3:4], (h2, h1, h0)


def _conv_specs():
    main = pl.BlockSpec((2, TR, TC), lambda j, i: (0, i, j))
    per = TR // CONV_HALO
    prev = pl.BlockSpec((2, CONV_HALO, TC), lambda j, i: (0, jnp.maximum(i * per - 1, 0), j))
    nxt = pl.BlockSpec((2, CONV_HALO, TC), lambda j, i: (0, jnp.minimum((i + 1) * per, S // CONV_HALO - 1), j))
    par = pl.BlockSpec((2, SUBLANES, TC), lambda j, i: (0, 0, j))
    return main, prev, nxt, par


def _convgate_fwd(hid, cwb):
    def body(h_ref, hp_ref, w_ref, act_ref):
        i = pl.program_id(1)
        c = []
        for g in range(2):
            ext = jnp.concatenate([jnp.where(i == 0, 0.0, hp_ref[g]), h_ref[g]], axis=0)
            c.append(_conv3(ext, w_ref[g], TR)[0])
        gate, up = c
        act_ref[...] = (jax.nn.gelu(gate, approximate=True) * up).astype(BF16)

    main, prev, _, par = _conv_specs()
    return pl.pallas_call(
        body, name="convgate_fwd", grid=(D_FF // TC, S // TR), in_specs=[main, prev, par],
        out_specs=pl.BlockSpec((TR, TC), lambda j, i: (i, j)), out_shape=jax.ShapeDtypeStruct((S, D_FF), BF16),
        compiler_params=_params(("parallel", "parallel")),
    )(hid, hid, cwb)


def _convgate_bwd(hid, dact, cwb):
    nr = S // TR
    n = TR + CONV_HALO

    def body(h_ref, hp_ref, hn_ref, da_ref, dan_ref, w_ref, dh_ref, dw_ref):
        i = pl.program_id(1)

        @pl.when(i == 0)
        def _():
            dw_ref[...] = jnp.zeros_like(dw_ref)

        da = jnp.concatenate([da_ref[...], jnp.where(i == nr - 1, 0.0, dan_ref[...])], axis=0)
        c, taps = [], []
        for g in range(2):
            ext = jnp.concatenate([jnp.where(i == 0, 0.0, hp_ref[g]), h_ref[g], hn_ref[g]], axis=0)
            cg, tg = _conv3(ext, w_ref[g], n)
            c.append(cg)
            taps.append(tg)
        gate, up = c
        th = jnp.tanh(GELU_K * (gate + GELU_C * gate * gate * gate))
        gelu = 0.5 * gate * (1.0 + th)
        dgelu = 0.5 * (1.0 + th) + 0.5 * gate * (1.0 - th * th) * GELU_K * (1.0 + 3.0 * GELU_C * gate * gate)
        for g, dc in enumerate((da * up * dgelu, da * gelu)):
            w = w_ref[g]
            dh = w[2:3] * dc[:TR] + w[1:2] * pltpu.roll(dc, n - 1, 0)[:TR] + w[0:1] * pltpu.roll(dc, n - 2, 0)[:TR]
            dh_ref[g] = dh.astype(BF16)
            dcm = dc[:TR]
            for r in range(3):
                dw_ref[g, r:r + 1, :] += jnp.sum(dcm * taps[g][r][:TR], axis=0, keepdims=True)
            dw_ref[g, 3:4, :] += jnp.sum(dcm, axis=0, keepdims=True)

    main, prev, nxt, par = _conv_specs()
    per = TR // CONV_HALO
    return pl.pallas_call(
        body, name="convgate_bwd", grid=(D_FF // TC, nr),
        in_specs=[main, prev, nxt, pl.BlockSpec((TR, TC), lambda j, i: (i, j)),
                  pl.BlockSpec((CONV_HALO, TC), lambda j, i: (jnp.minimum((i + 1) * per, S // CONV_HALO - 1), j)), par],
        out_specs=[main, par],
        out_shape=[jax.ShapeDtypeStruct((2, S, D_FF), BF16), jax.ShapeDtypeStruct((2, SUBLANES, D_FF), F32)],
        compiler_params=_params(("parallel", "arbitrary")),
    )(hid, hid, hid, dact, dact, cwb)


def _adam_update(w, g, m, v):
    m = ADAM_B1 * m + (1.0 - ADAM_B1) * g
    v = ADAM_B2 * v + (1.0 - ADAM_B2) * (g * g)
    m_hat = m / (1.0 - ADAM_B1 ** ADAM_STEP)
    v_hat = v / (1.0 - ADAM_B2 ** ADAM_STEP)
    return -ADAM_LR * (m_hat / (jnp.sqrt(v_hat) + ADAM_EPS) + ADAM_WD * w), m, v


def _row_tile(rows, cols, itemsize=4, target=TILE_BYTES):
    tr = SUBLANES
    while rows % (2 * tr) == 0 and 2 * tr * cols * itemsize <= target:
        tr *= 2
    assert rows % tr == 0, (rows, tr)
    return tr


def _adamw(name, w, g, m, v):
    rows, cols = w.shape
    tr = rows if rows * cols * 4 <= TILE_BYTES // 2 else _row_tile(rows, cols, target=TILE_BYTES // 2)

    def body(w_ref, g_ref, m_ref, v_ref, d_ref, nm_ref, nv_ref):
        d_ref[...], nm_ref[...], nv_ref[...] = _adam_update(w_ref[...], g_ref[...], m_ref[...], v_ref[...])

    spec = _row_spec(tr, cols)
    shape = jax.ShapeDtypeStruct((rows, cols), F32)
    return pl.pallas_call(
        body, name=name, grid=(rows // tr,), in_specs=[spec] * 4, out_specs=[spec] * 3, out_shape=[shape] * 3,
        compiler_params=_params(("parallel",)),
    )(w, g, m, v)


def _adamw_halves(name, core, w, g_mine, g_sibling, m, v):
    rows, cols = w.shape
    half = rows // 2
    tr = _row_tile(half, cols, target=TILE_BYTES // 2)
    per = half // tr

    def body(core_ref, w_ref, gm_ref, gs_ref, m_ref, v_ref, g_ref, d_ref, nm_ref, nv_ref):
        g = jnp.where(pl.program_id(0) // per == core_ref[0], gm_ref[...], gs_ref[...])
        g_ref[...] = g
        d_ref[...], nm_ref[...], nv_ref[...] = _adam_update(w_ref[...], g, m_ref[...], v_ref[...])

    spec = pl.BlockSpec((tr, cols), lambda i, core_ref: (i, 0))
    half_spec = pl.BlockSpec((tr, cols), lambda i, core_ref: (i % per, 0))
    shape = jax.ShapeDtypeStruct((rows, cols), F32)
    return pl.pallas_call(
        body, name=name, out_shape=[shape] * 4,
        grid_spec=pltpu.PrefetchScalarGridSpec(
            num_scalar_prefetch=1, grid=(rows // tr,), in_specs=[spec, half_spec, half_spec, spec, spec], out_specs=[spec] * 4),
        compiler_params=_params(("parallel",)),
    )(core, w, g_mine, g_sibling, m, v)


def _chip_sum(name, core, g, other):
    _, _, half, cols = g.shape
    tr = _row_tile(half, cols)

    def body(core_ref, g_ref, o_ref, p_ref):
        p_ref[...] = (g_ref[...] + o_ref[...]).astype(BF16)

    spec = pl.BlockSpec((None, tr, cols), lambda j, i, core_ref: (j, i, 0))
    return pl.pallas_call(
        body, name=name, out_shape=jax.ShapeDtypeStruct((N_CHIPS, half, cols), BF16),
        grid_spec=pltpu.PrefetchScalarGridSpec(
            num_scalar_prefetch=1, grid=(N_CHIPS, half // tr),
            in_specs=[pl.BlockSpec((None, None, tr, cols), lambda j, i, core_ref: (j, core_ref[0], i, 0)), spec],
            out_specs=spec),
        compiler_params=_params(("parallel", "parallel")),
    )(core, g, other)


def _mesh_sum(name, chip, received, own):
    _, half, cols = received.shape
    tr = _row_tile(half, cols, itemsize=2 * N_CHIPS)

    def body(chip_ref, r_ref, own_ref, o_ref):
        acc = None
        for j in range(N_CHIPS):
            term = jnp.where(chip_ref[0] == j, own_ref[...], r_ref[j]).astype(F32)
            acc = term if acc is None else acc + term
        o_ref[...] = acc

    return pl.pallas_call(
        body, name=name, out_shape=jax.ShapeDtypeStruct((half, cols), F32),
        grid_spec=pltpu.PrefetchScalarGridSpec(
            num_scalar_prefetch=1, grid=(half // tr,),
            in_specs=[pl.BlockSpec((N_CHIPS, tr, cols), lambda i, chip_ref: (0, i, 0)),
                      pl.BlockSpec((None, tr, cols), lambda i, chip_ref: (chip_ref[0], i, 0))],
            out_specs=pl.BlockSpec((tr, cols), lambda i, chip_ref: (i, 0))),
        compiler_params=_params(("parallel",)),
    )(chip, received, own)


CHIP_FLIPS = ((1, 0), (0, 1), (1, 1))


def _place():
    x, y, c = lax.axis_index("x"), lax.axis_index("y"), lax.axis_index("c")
    return x, y, c, 2 * x + y


def _remote(src, dst, sems_s, sems_r, k, dev):
    return pltpu.make_async_remote_copy(src_ref=src, dst_ref=dst, send_sem=sems_s.at[k], recv_sem=sems_r.at[k],
                                        device_id=dev, device_id_type=MESH)


class _Exchange:
    def __init__(self, ins, out_shapes, n_sems, start, forward, finish):
        self.ins, self.out_shapes, self.n_sems = list(ins), list(out_shapes), n_sems
        self.start, self.forward, self.finish = start, forward, finish

    def scratch(self):
        return [pltpu.SemaphoreType.DMA((self.n_sems,)), pltpu.SemaphoreType.DMA((self.n_sems,))]

    def run(self, name):
        n = len(self.ins)

        def body(*refs):
            args = (refs[:n], refs[n:2 * n]) + tuple(refs[2 * n:])
            self.start(*args)
            self.forward(*args)
            self.finish(*args)

        return pl.pallas_call(
            body, name=name, in_specs=[ANY] * n, out_specs=[ANY] * n, out_shape=self.out_shapes, scratch_shapes=self.scratch(),
        )(*self.ins)


def _all_gather_weights(halved, whole):
    nh, nw = len(halved), len(whole)
    n_arr = nh + nw

    def copies(ins, outs, sems_s, sems_r):
        x, y, c, me = _place()
        sibling = (x, y, 1 - c)
        own = [_remote(ins[k], outs[k].at[me], sems_s, sems_r, k, sibling) for k in range(n_arr)]
        first, passed = [], []
        for k in range(n_arr):
            for f, (fx, fy) in enumerate(CHIP_FLIPS):
                src, dst = (ins[k].at[c], outs[k].at[me, c]) if k < nh else (ins[k], outs[k].at[me])
                first.append(_remote(src, dst, sems_s, sems_r, n_arr + 3 * k + f, (x ^ fx, y ^ fy, c)))
        for k in range(nh):
            for f, (fx, fy) in enumerate(CHIP_FLIPS):
                landed = outs[k].at[2 * (x ^ fx) + (y ^ fy), c]
                passed.append(_remote(landed, landed, sems_s, sems_r, 4 * n_arr + 3 * k + f, sibling))
        return own, first, passed

    def start(*refs):
        own, first, _ = copies(*refs)
        for cp in own + first:
            cp.start()

    def forward(*refs):
        _, first, passed = copies(*refs)
        for arrived, cp in zip(first, passed):
            arrived.wait_recv()
            cp.start()

    def finish(*refs):
        own, first, passed = copies(*refs)
        for cp in first[3 * nh:] + passed + own:
            cp.wait_recv()
        for cp in first + passed + own:
            cp.wait_send()

    shapes = [jax.ShapeDtypeStruct((N_CHIPS,) + a.shape, a.dtype) for a in list(halved) + list(whole)]
    return _Exchange(list(halved) + list(whole), shapes, 7 * nh + 4 * nw, start, forward, finish)


def _swap_halves(gs):
    n = len(gs)

    def copies(ins, outs, sems_s, sems_r):
        x, y, c, _ = _place()
        return [_remote(ins[k].at[:, 1 - c], outs[k], sems_s, sems_r, k, (x, y, 1 - c)) for k in range(n)]

    def start(*refs):
        for cp in copies(*refs):
            cp.start()

    def finish(*refs):
        for cp in copies(*refs):
            cp.wait()

    shapes = [jax.ShapeDtypeStruct((g.shape[0],) + g.shape[2:], g.dtype) for g in gs]
    return _Exchange(gs, shapes, n, start, _no_copies, finish)


def _scatter_chips(ps):
    n = len(ps)

    def copies(ins, outs, sems_s, sems_r):
        x, y, c, me = _place()
        return [_remote(ins[k].at[2 * (x ^ fx) + (y ^ fy)], outs[k].at[me], sems_s, sems_r, 3 * k + f, (x ^ fx, y ^ fy, c))
                for k in range(n) for f, (fx, fy) in enumerate(CHIP_FLIPS)]

    def start(*refs):
        for cp in copies(*refs):
            cp.start()

    def forward(*refs):
        pass

    def finish(*refs):
        for cp in copies(*refs):
            cp.wait()

    shapes = [jax.ShapeDtypeStruct(p.shape, p.dtype) for p in ps]
    return _Exchange(ps, shapes, 3 * n, start, forward, finish)


def _swap_reduced(rs):
    n = len(rs)

    def copies(ins, outs, sems_s, sems_r):
        x, y, c, _ = _place()
        return [_remote(ins[k], outs[k], sems_s, sems_r, k, (x, y, 1 - c)) for k in range(n)]

    def start(*refs):
        for cp in copies(*refs):
            cp.start()

    def finish(*refs):
        for cp in copies(*refs):
            cp.wait()

    return _Exchange(rs, [jax.ShapeDtypeStruct(r.shape, r.dtype) for r in rs], n, start, _no_copies, finish)


N_DEV = 8


def _gather_small(buf):
    def copies(ins, outs, sems_s, sems_r):
        x, y, c, _ = _place()
        me = 4 * x + 2 * y + c
        return [_remote(ins[0], outs[0].at[me], sems_s, sems_r, o - 1, (x ^ (o >> 2), y ^ ((o >> 1) & 1), c ^ (o & 1)))
                for o in range(1, N_DEV)]

    def start(*refs):
        for cp in copies(*refs):
            cp.start()

    def finish(*refs):
        for cp in copies(*refs):
            cp.wait()

    return _Exchange([buf], [jax.ShapeDtypeStruct((N_DEV,) + buf.shape, buf.dtype)], N_DEV - 1, start, _no_copies, finish)


def _sum_devices(place, gathered, own):
    rows = own.shape[0]

    def body(place_ref, g_ref, own_ref, o_ref):
        acc = None
        for d in range(N_DEV):
            term = jnp.where(place_ref[0] == d, own_ref[...], g_ref[d])
            acc = term if acc is None else acc + term
        o_ref[...] = acc

    return pl.pallas_call(
        body, name="sum_devices", out_shape=jax.ShapeDtypeStruct((rows, LANES), F32),
        grid_spec=pltpu.PrefetchScalarGridSpec(
            num_scalar_prefetch=1, grid=(1,),
            in_specs=[pl.BlockSpec((N_DEV, rows, LANES), lambda i, place_ref: (0, 0, 0)),
                      pl.BlockSpec((rows, LANES), lambda i, place_ref: (0, 0))],
            out_specs=pl.BlockSpec((rows, LANES), lambda i, place_ref: (0, 0))),
        compiler_params=_params(("arbitrary",)),
    )(place, gathered, own)


def _no_copies(*refs):
    pass


def _no_exchange():
    return _Exchange([], [], 1, _no_copies, _no_copies, _no_copies)


class _NoComm:
    def gather_first(self):
        return _no_exchange()

    def first_landed(self, p, landed):
        pass

    def gather_rest(self, p):
        return _no_exchange()

    def weights_landed(self, p, landed):
        pass

    def swap_first(self, g):
        return _no_exchange()

    def first_swapped(self, landed):
        pass

    def swap_second(self, g):
        return _no_exchange()

    def second_swapped(self, landed):
        pass

    def scatter_early(self, g):
        return _no_exchange()

    def scatter_landed(self, landed):
        pass

    def swap_reduced_early(self):
        return _no_exchange()

    def reduced_landed(self, landed):
        pass

    def scatter_late(self, g):
        return _no_exchange()

    def late_landed(self, landed):
        pass


def _local_step(x, mem, target, p, comm):
    h1, landed = _norm_fwd("norm_mix_pre", x, p["norm_mix_pre"], comm.gather_first())
    comm.first_landed(p, landed)
    proj = _mm_nn("in_proj", h1, p["w_in"], F32, 1024, 896)
    qa, ka, va = _fox_prep(proj, p["bf_pad"])
    ycat, qab, landed = _fox_fwd(qa, ka, va, comm.gather_rest(p))
    comm.weights_landed(p, landed)
    ycat = _pool_fwd(proj, p["w_pool_bd"], p["pool_scale"], ycat)
    y1, x2, h2, qx = _proj_resid_norm("mix_out", ycat, p["w_mix_out"], x, p["norm_mix_post"], p["norm_xa_pre"], p["w_xq"])
    mem_n = _norm_fwd("norm_mem", mem, p["norm_mem"])
    kv = _mm(
        "xkv", mem_n, p["w_xkv"], pl.BlockSpec((MEM, D), lambda i, j, k: (0, 0)),
        pl.BlockSpec((None, D, 512), lambda i, j, k: (j, 0, 0)), jax.ShapeDtypeStruct((MEM, 2 * D), BF16),
        pl.BlockSpec((MEM, 512), lambda i, j, k: (0, j)), (1, N_CHIPS, 1), NN, (MEM, 512))
    xo = _xattn_fwd(qx, kv)
    y2, x3, h3 = _proj_resid_norm("xo", xo, p["w_xo"], x2, p["norm_xa_post"], p["norm_ffn_pre"])
    hid = _mm(
        "up_proj", h3, p["w_up"], pl.BlockSpec((1024, D), lambda i, j, k: (i, 0)),
        pl.BlockSpec((None, D, 1024), lambda i, j, k: (j // 2, 0, j % 2)), jax.ShapeDtypeStruct((2, S, D_FF), F32),
        pl.BlockSpec((None, 1024, 1024), lambda i, j, k: (j // 4, i, j % 4)), (S // 1024, 8, 1), NN, (1024, 1024))
    act = _convgate_fwd(hid, p["cwb"])

    g = {}
    dres, dy3, g["norm_ffn_post"], loss_cols = _down_loss_bwd(act, p["w_down"], x3, p["norm_ffn_post"], target)
    dact = _mm_nt("d_act", dy3, p["w_down"], F32, 1024, 1024)
    g["w_down"] = _mm_tn("dw_down", act, dy3, 512, 512)
    dhid, dcwb = _convgate_bwd(hid, dact, p["cwb"])
    g["w_up"] = _mm(
        "dw_up", h3, dhid, pl.BlockSpec((S, 512), lambda i, j, k: (0, i)),
        pl.BlockSpec((None, S, 512), lambda i, j, k: (j // 8, 0, j % 8)), jax.ShapeDtypeStruct((N_CHIPS, D, 2048), F32),
        pl.BlockSpec((None, 512, 512), lambda i, j, k: (j // 4, i, j % 4)), (2, 16, 1), TN, (512, 512))
    dh3, landed = _d_h3(dhid, p["w_up"], comm.swap_first(g))
    comm.first_swapped(landed)
    dres, dy2, g["norm_ffn_pre"], g["norm_xa_post"] = _mid_bwd("bwd_ffn_xa", dres, x3, p["norm_ffn_pre"], dh3, y2, p["norm_xa_post"])
    dxo = _mm_nt("d_xo", dy2, p["w_xo"], BF16, 1024, 1024)
    g["w_xo"] = _mm_tn("dw_xo", xo, dy2, 512, 512)
    dqx, dkv = _xattn_bwd(qx, kv, dxo)
    dkv = dkv.astype(BF16)
    g["w_xq"] = _mm_tn("dw_xq", h2, dqx, 512, 512)
    dmem_n = _mm(
        "d_mem", dkv, p["w_xkv"], pl.BlockSpec((MEM, 512), lambda i, j, k: (0, k)),
        pl.BlockSpec((None, D, 512), lambda i, j, k: (k, 0, 0)), jax.ShapeDtypeStruct((MEM, D), F32),
        pl.BlockSpec((MEM, D), lambda i, j, k: (0, 0)), (1, 1, N_CHIPS), NT, (MEM, D))
    g["w_xkv"] = _mm(
        "dw_xkv", mem_n, dkv, pl.BlockSpec((MEM, D), lambda i, j, k: (0, 0)),
        pl.BlockSpec((MEM, 512), lambda i, j, k: (0, j)), jax.ShapeDtypeStruct((N_CHIPS, D, 512), F32),
        pl.BlockSpec((None, D, 512), lambda i, j, k: (j, 0, 0)), (1, N_CHIPS, 1), TN, (D, 512))
    g["norm_mem"] = _gain_bwd("dg_mem", mem, p["norm_mem"], dmem_n)
    (dres, dy1, g["norm_xa_pre"], g["norm_mix_post"], dy_pool, doa), landed = _bwd_xa_mix(
        dqx, p["w_xq"], dres, x2, p["norm_xa_pre"], y1, p["norm_mix_post"], p["w_mix_out"], ycat, comm.swap_second(g))
    comm.second_swapped(landed)
    g["w_mix_out"] = _mm_tn("dw_mix_out", ycat, dy1, 512, 512)
    dqa, dka, dva, landed = _fox_bwd(qab, doa, ka, va, comm.scatter_early(g))
    comm.scatter_landed(landed)
    du, g["w_pool_full"], g["pool_scale"] = _pool_bwd(proj, dy_pool, p["w_pool_bd"], p["w_pool_bd_t"], p["pool_scale"])
    dproj, g["bf_pad"] = _fox_bwd_post(dqa, dka, dva, du, proj, p["bf_pad"])
    g["w_in"], landed = _mm_tn("dw_in", h1, dproj, 512, 896, comm.swap_reduced_early())
    comm.reduced_landed(landed)
    dh1, landed = _mm_nt("d_h1", dproj, p["w_in"], F32, 1024, 1024, comm.scatter_late(g))
    comm.late_landed(landed)
    grad_x, g["norm_mix_pre"] = _first_bwd(dres, x, p["norm_mix_pre"], dh1)
    g["cwb"] = dcwb
    return grad_x, g, loss_cols


BIG = ("w_in", "w_mix_out", "w_xq", "w_xkv", "w_xo", "w_up", "w_down")
ROW_SHARDED = ("w_mix_out", "w_xq", "w_xo", "w_down")
SMALL = ("norm_mix_pre", "norm_mix_post", "b_forget", "w_pool", "pool_scale", "norm_mem", "norm_xa_pre", "norm_xa_post",
         "norm_ffn_pre", "norm_ffn_post", "conv_b")
ORDER = ("norm_mix_pre", "norm_mix_post", "w_in", "b_forget", "w_pool", "pool_scale", "w_mix_out", "norm_mem", "norm_xa_pre",
         "norm_xa_post", "w_xq", "w_xkv", "w_xo", "norm_ffn_pre", "norm_ffn_post", "w_up", "conv_w", "conv_b", "w_down")
SLOT = SUBLANES * LANES


def _pack(parts):
    rows, offs, off = [], [], 0
    for a in parts:
        flat = a.reshape(-1).astype(F32)
        n = -(-flat.shape[0] // SLOT) * SLOT
        rows.append(jnp.pad(flat, (0, n - flat.shape[0])).reshape(n // LANES, LANES))
        offs.append(off)
        off += n // LANES
    return jnp.concatenate(rows, axis=0), offs


def _unpack(buf, off, like):
    n = like.size
    rows = -(-n // LANES)
    return buf[off:off + rows].reshape(-1)[:n].reshape(like.shape)


FIRST = ("w_in",)
REST = ("w_mix_out", "w_xq", "w_xkv", "w_xo", "w_up", "w_down")


def _local_params(w):
    w_pool_bd = jnp.zeros((D_POOL, D_POOL), F32)
    for gi in range(4):
        w_pool_bd = w_pool_bd.at[64 * gi:64 * (gi + 1), 64 * gi:64 * (gi + 1)].set(w["w_pool"][0, gi])
    p = {n: w[n] for n in ("norm_mix_pre", "norm_mix_post", "norm_mem", "norm_xa_pre", "norm_xa_post", "norm_ffn_pre",
                           "norm_ffn_post")}
    p.update(
        bf_pad=jnp.pad(w["b_forget"], ((0, 0), (0, LANES - HEADS))),
        w_pool_bd=w_pool_bd.astype(BF16), w_pool_bd_t=w_pool_bd.T.astype(BF16), pool_scale=w["pool_scale"].reshape(1, D_POOL))
    return p


def _w_in_param(stacked):
    return jnp.pad(jnp.concatenate(list(stacked), axis=1), ((0, 0), (0, D_IN_PAD - D_IN)))


def _rest_params(w, full, conv_w_full):
    cw2 = conv_w_full.reshape(3, 2, D_FF).transpose(1, 0, 2)
    cwb = jnp.concatenate([cw2, w["conv_b"].reshape(1, 2, D_FF).transpose(1, 0, 2), jnp.zeros((2, 4, D_FF), F32)], axis=1)
    return dict(w_mix_out=full["w_mix_out"].reshape(D, D), w_xq=full["w_xq"].reshape(D, D), w_xkv=full["w_xkv"],
                w_xo=full["w_xo"].reshape(D, D), w_up=full["w_up"], cwb=cwb, w_down=full["w_down"].reshape(D_FF, D))


def _whole_params(w, full, conv_w_full):
    p = _local_params(w)
    p.update(_rest_params(w, full, conv_w_full), w_in=_w_in_param(full["w_in"]))
    return p


def _halved(a):
    return a.reshape(a.shape[:-2] + (2, a.shape[-2] // 2, a.shape[-1]))


class _StepComm:
    def __init__(self, w, shard2d, conv_w, core_id, chip_id):
        self.w, self.shard2d, self.conv_w, self.core_id, self.chip_id = w, shard2d, conv_w, core_id, chip_id
        self.first, self.second = ("w_up", "w_down"), ("w_xq", "w_xkv", "w_xo")
        self.early = self.first + self.second
        self.late = ("w_in", "w_mix_out")

    def gather_first(self):
        return _all_gather_weights([_halved(self.shard2d[n].astype(BF16)) for n in FIRST], [])

    def first_landed(self, p, landed):
        p["w_in"] = _w_in_param(landed[0].reshape((N_CHIPS,) + self.shard2d["w_in"].shape))

    def gather_rest(self, p):
        return _all_gather_weights([_halved(self.shard2d[n].astype(BF16)) for n in REST], [self.conv_w.reshape(3, -1)])

    def weights_landed(self, p, landed):
        full = {n: a.reshape((N_CHIPS,) + self.shard2d[n].shape) for n, a in zip(REST, landed)}
        conv_w_full = jnp.transpose(landed[-1], (1, 0, 2)).reshape(3, 2 * D_FF)
        p.update(_rest_params(self.w, full, conv_w_full))

    def _view(self, g, n):
        return _halved(g[n].reshape((N_CHIPS,) + self.shard2d[n].shape))

    def swap_first(self, g):
        return _swap_halves([self._view(g, n) for n in self.first])

    def first_swapped(self, landed):
        self.from_sibling = dict(zip(self.first, landed))

    def swap_second(self, g):
        return _swap_halves([self._view(g, n) for n in self.second])

    def second_swapped(self, landed):
        self.from_sibling.update(zip(self.second, landed))

    def scatter_early(self, g):
        self.partial = [_chip_sum("chip_sum_" + n, self.core_id, self._view(g, n), self.from_sibling[n]) for n in self.early]
        return _scatter_chips(self.partial)

    def scatter_landed(self, landed):
        self.received = list(landed)

    def swap_reduced_early(self):
        self.reduced = [_mesh_sum("mesh_sum_" + n, self.chip_id, r, own)
                        for n, r, own in zip(self.early, self.received, self.partial)]
        return _swap_reduced(self.reduced)

    def reduced_landed(self, landed):
        self.reduced_sibling = list(landed)

    def scatter_late(self, g):
        gw_in = g["w_in"][:, :D_IN]
        cols = D_IN // N_CHIPS
        views = [_halved(jnp.stack([gw_in[:, cols * j:cols * (j + 1)] for j in range(N_CHIPS)])), self._view(g, "w_mix_out")]
        from_sibling = _swap_halves(views).run("swap_halves_late")
        self.partial_late = [_chip_sum("chip_sum_" + n, self.core_id, view, other)
                             for n, view, other in zip(self.late, views, from_sibling)]
        return _scatter_chips(self.partial_late)

    def late_landed(self, landed):
        self.received_late = list(landed)


def kernel(x, mem, norm_mix_pre, norm_mix_post, w_in, b_forget, w_pool, pool_scale, w_mix_out, norm_mem, norm_xa_pre, norm_xa_post, w_xq, w_xkv, w_xo, norm_ffn_pre, norm_ffn_post, w_up, conv_w, conv_b, w_down, loss_target, m_norm_mix_pre, m_norm_mix_post, m_w_in, m_b_forget, m_w_pool, m_pool_scale, m_w_mix_out, m_norm_mem, m_norm_xa_pre, m_norm_xa_post, m_w_xq, m_w_xkv, m_w_xo, m_norm_ffn_pre, m_norm_ffn_post, m_w_up, m_conv_w, m_conv_b, m_w_down, v_norm_mix_pre, v_norm_mix_post, v_w_in, v_b_forget, v_w_pool, v_pool_scale, v_w_mix_out, v_norm_mem, v_norm_xa_pre, v_norm_xa_post, v_w_xq, v_w_xkv, v_w_xo, v_norm_ffn_pre, v_norm_ffn_post, v_w_up, v_conv_w, v_conv_b, v_w_down):
    w = dict(norm_mix_pre=norm_mix_pre, norm_mix_post=norm_mix_post, w_in=w_in, b_forget=b_forget, w_pool=w_pool,
             pool_scale=pool_scale, w_mix_out=w_mix_out, norm_mem=norm_mem, norm_xa_pre=norm_xa_pre, norm_xa_post=norm_xa_post,
             w_xq=w_xq, w_xkv=w_xkv, w_xo=w_xo, norm_ffn_pre=norm_ffn_pre, norm_ffn_post=norm_ffn_post, w_up=w_up,
             conv_w=conv_w, conv_b=conv_b, w_down=w_down)
    m = dict(norm_mix_pre=m_norm_mix_pre, norm_mix_post=m_norm_mix_post, w_in=m_w_in, b_forget=m_b_forget, w_pool=m_w_pool,
             pool_scale=m_pool_scale, w_mix_out=m_w_mix_out, norm_mem=m_norm_mem, norm_xa_pre=m_norm_xa_pre,
             norm_xa_post=m_norm_xa_post, w_xq=m_w_xq, w_xkv=m_w_xkv, w_xo=m_w_xo, norm_ffn_pre=m_norm_ffn_pre,
             norm_ffn_post=m_norm_ffn_post, w_up=m_w_up, conv_w=m_conv_w, conv_b=m_conv_b, w_down=m_w_down)
    v = dict(norm_mix_pre=v_norm_mix_pre, norm_mix_post=v_norm_mix_post, w_in=v_w_in, b_forget=v_b_forget, w_pool=v_w_pool,
             pool_scale=v_pool_scale, w_mix_out=v_w_mix_out, norm_mem=v_norm_mem, norm_xa_pre=v_norm_xa_pre,
             norm_xa_post=v_norm_xa_post, w_xq=v_w_xq, w_xkv=v_w_xkv, w_xo=v_w_xo, norm_ffn_pre=v_norm_ffn_pre,
             norm_ffn_post=v_norm_ffn_post, w_up=v_w_up, conv_w=v_conv_w, conv_b=v_conv_b, w_down=v_w_down)
    chip = 2 * lax.axis_index("x") + lax.axis_index("y")

    core_id = lax.axis_index("c").astype(jnp.int32).reshape(1)
    chip_id = chip.astype(jnp.int32).reshape(1)

    shard2d = {n: w[n][0] for n in BIG}
    p = _local_params(w)
    comm = _StepComm(w, shard2d, conv_w, core_id, chip_id)
    grad_x, g, loss_cols = _local_step(x[0], mem[0], loss_target[0], p, comm)

    reduced_late = [_mesh_sum("mesh_sum_" + n, chip_id, r, own)
                    for n, r, own in zip(comm.late, comm.received_late, comm.partial_late)]
    names = comm.late + comm.early
    reduced = reduced_late + comm.reduced
    reduced_sibling = list(_swap_reduced(reduced_late).run("swap_reduced_late")) + comm.reduced_sibling
    grads = {}

    gw_pool = jnp.stack([g["w_pool_full"][64 * gi:64 * (gi + 1), 64 * gi:64 * (gi + 1)] for gi in range(4)])
    dcwb = g["cwb"]
    g_conv_w = dcwb[:, 0:3, :].transpose(1, 0, 2).reshape(3, 2 * D_FF)
    g_conv_b = dcwb[:, 3, :].reshape(2 * D_FF)
    small_g = dict(norm_mix_pre=g["norm_mix_pre"], norm_mix_post=g["norm_mix_post"], b_forget=g["bf_pad"][:, :HEADS],
                   w_pool=gw_pool, pool_scale=g["pool_scale"], norm_mem=g["norm_mem"], norm_xa_pre=g["norm_xa_pre"],
                   norm_xa_post=g["norm_xa_post"], norm_ffn_pre=g["norm_ffn_pre"], norm_ffn_post=g["norm_ffn_post"],
                   conv_b=g_conv_b)
    local_buf, offs = _pack([small_g[n] for n in SMALL] + [g_conv_w, loss_cols])

    delta, new_m, new_v = {}, {}, {}
    for n, g_mine, g_sibling in zip(names, reduced, reduced_sibling):
        gn, d, nm, nv = _adamw_halves("adamw_" + n, core_id, shard2d[n], g_mine, g_sibling, m[n][0], v[n][0])
        grads[n], delta[n], new_m[n], new_v[n] = gn[None], d[None], nm[None], nv[None]
    place = (2 * chip + lax.axis_index("c")).astype(jnp.int32).reshape(1)
    buf = _sum_devices(place, _gather_small(local_buf).run("gather_small")[0], local_buf)
    for n, off in zip(SMALL, offs):
        grads[n] = _unpack(buf, off, w[n])
    g_conv_w = _unpack(buf, offs[len(SMALL)], g_conv_w)
    grads["conv_w"] = lax.dynamic_slice_in_dim(g_conv_w, chip * (2 * D_FF // N_CHIPS), 2 * D_FF // N_CHIPS, axis=1).reshape(conv_w.shape)
    loss = jnp.sum(_unpack(buf, offs[len(SMALL) + 1], loss_cols))
    small_names = SMALL + ("conv_w",)
    packed = [_pack([d[n] for n in small_names])[0] for d in (w, grads, m, v)]
    offs = _pack([w[n] for n in small_names])[1]
    d, nm, nv = _adamw("adamw_small", *packed)
    for n, off in zip(small_names, offs):
        delta[n], new_m[n], new_v[n] = _unpack(d, off, w[n]), _unpack(nm, off, w[n]), _unpack(nv, off, w[n])

    return (loss, grad_x[None], *[grads[n] for n in ORDER], *[delta[n] for n in ORDER], *[new_m[n] for n in ORDER],
            *[new_v[n] for n in ORDER])
```

```python
import functools

import jax
import jax.numpy as jnp
import numpy as np
from jax import lax
from jax.experimental import pallas as pl
from jax.experimental.pallas import tpu as pltpu

F32 = jnp.float32
BF16 = jnp.bfloat16
MESH = pl.DeviceIdType.MESH
ANY = pl.BlockSpec(memory_space=pl.ANY)
VMEM_SPEC = pl.BlockSpec(memory_space=pltpu.VMEM)

S = 4096
D = 1024
MEM = 256
D_POOL = 256
HEADS = 12
DH = 64
D_FOX = HEADS * DH
D_IN = D_POOL + 3 * D_FOX + HEADS
F_OFF = D_POOL + 3 * D_FOX
Q_OFF, K_OFF, V_OFF = D_POOL, D_POOL + D_FOX, D_POOL + 2 * D_FOX
XA_HEADS = 4
XA_DH = 256
D_FF = 4096
EPS = 1e-6
N_CHIPS = 4
ADAM_LR, ADAM_B1, ADAM_B2, ADAM_EPS, ADAM_WD, ADAM_STEP = 0.001, 0.9, 0.999, 1e-08, 0.01, 10

LANES = 128
SUBLANES = 8
D_IN_PAD = 21 * LANES
TR = 512
TILE_BYTES = 2 * 1024 * 1024
NEG = -1e30
VMEM_LIMIT = 52 * 1024 * 1024

NN = (((1,), (0,)), ((), ()))
NT = (((1,), (1,)), ((), ()))
TN = (((0,), (0,)), ((), ()))


def _dot(a, b, dims=NN):
    return lax.dot_general(a, b, dims, preferred_element_type=F32)


def _params(sem):
    return pltpu.CompilerParams(dimension_semantics=sem, vmem_limit_bytes=VMEM_LIMIT)


def _split3(x):
    hi = x.astype(BF16)
    r = x - hi.astype(F32)
    mid = r.astype(BF16)
    lo = (r - mid.astype(F32)).astype(BF16)
    return hi, mid, lo


def _split3_f32(x):
    hi = x.astype(BF16).astype(F32)
    r = x - hi
    mid = r.astype(BF16).astype(F32)
    return hi, mid, r - mid


def _lane_iota(shape):
    return lax.broadcasted_iota(jnp.int32, shape, len(shape) - 1)


def _row_iota(shape):
    return lax.broadcasted_iota(jnp.int32, shape, len(shape) - 2)


def _mm(name, a, b, a_spec, b_spec, out_shape, out_spec, grid, dims, acc_shape, ex=None):
    nk = grid[2]
    if ex is not None:
        return _mm_hosting(name, a, b, a_spec, b_spec, out_shape, out_spec, grid, dims, ex)

    def body(a_ref, b_ref, o_ref, *scr):
        p = _dot(a_ref[...], b_ref[...], dims)
        if nk == 1:
            o_ref[...] = p.astype(o_ref.dtype)
        else:
            acc = scr[0]
            k = pl.program_id(2)

            @pl.when(k == 0)
            def _():
                acc[...] = p

            @pl.when(k > 0)
            def _():
                acc[...] += p

            @pl.when(k == nk - 1)
            def _():
                o_ref[...] = acc[...].astype(o_ref.dtype)

    return pl.pallas_call(
        body, name=name, grid=grid, in_specs=[a_spec, b_spec], out_specs=out_spec, out_shape=out_shape,
        scratch_shapes=[pltpu.VMEM(acc_shape, F32)] if nk > 1 else [],
        compiler_params=_params(("parallel", "parallel", "arbitrary")),
    )(a, b)


def _mm_hosting(name, a, b, a_spec, b_spec, out_shape, out_spec, grid, dims, ex):
    assert grid[2] == 1
    n = len(ex.ins)

    def body(*refs):
        i, j = pl.program_id(0), pl.program_id(1)
        first = (i == 0) & (j == 0)
        (a_ref, b_ref), (o_ref,), _, begin, end = _hosted(
            ex, refs, 2, 1, first, first, (i == grid[0] - 1) & (j == grid[1] - 1))
        begin()
        o_ref[...] = _dot(a_ref[...], b_ref[...], dims).astype(o_ref.dtype)
        end()

    res = pl.pallas_call(
        body, name=name, grid=grid, in_specs=[a_spec, b_spec] + [ANY] * n, out_specs=[out_spec] + [ANY] * n,
        out_shape=[out_shape] + ex.out_shapes, scratch_shapes=ex.scratch(),
        compiler_params=_params(("arbitrary", "arbitrary", "arbitrary")),
    )(a, b, *ex.ins)
    return res[0], res[1:]


def _mm_nn(name, a, b, out_dtype, tm, tn):
    m, k = a.shape
    n = b.shape[1]
    return _mm(name, a, b, pl.BlockSpec((tm, k), lambda i, j, kk: (i, 0)), pl.BlockSpec((k, tn), lambda i, j, kk: (0, j)),
               jax.ShapeDtypeStruct((m, n), out_dtype), pl.BlockSpec((tm, tn), lambda i, j, kk: (i, j)),
               (m // tm, n // tn, 1), NN, (tm, tn))


def _mm_nt(name, a, b, out_dtype, tm, tn, ex=None):
    m, k = a.shape
    n = b.shape[0]
    return _mm(name, a, b, pl.BlockSpec((tm, k), lambda i, j, kk: (i, 0)), pl.BlockSpec((tn, k), lambda i, j, kk: (j, 0)),
               jax.ShapeDtypeStruct((m, n), out_dtype), pl.BlockSpec((tm, tn), lambda i, j, kk: (i, j)),
               (m // tm, n // tn, 1), NT, (tm, tn), ex)


def _mm_tn(name, a, b, tka, tn, ex=None):
    t, ka = a.shape
    n = b.shape[1]
    return _mm(name, a, b, pl.BlockSpec((t, tka), lambda i, j, kk: (0, i)), pl.BlockSpec((t, tn), lambda i, j, kk: (0, j)),
               jax.ShapeDtypeStruct((ka, n), F32), pl.BlockSpec((tka, tn), lambda i, j, kk: (i, j)),
               (ka // tka, n // tn, 1), TN, (tka, tn), ex)


def _d_h3(dhid, w_up, ex):
    tm = tn = 512
    shard = 2 * D_FF // N_CHIPS
    per_plane = D_FF // shard
    grid = (S // tm, D // tn)
    n = len(ex.ins)

    def body(*refs):
        i, j = pl.program_id(0), pl.program_id(1)
        first = (i == 0) & (j == 0)
        (a_ref, b_ref), (o_ref,), _, begin, end = _hosted(ex, refs, 2, 1, first, first, (i == grid[0] - 1) & (j == grid[1] - 1))
        begin()
        acc = None
        for k in range(N_CHIPS):
            cols = slice(shard * (k % per_plane), shard * (k % per_plane + 1))
            part = _dot(a_ref[k // per_plane, :, cols], b_ref[k], NT)
            acc = part if acc is None else acc + part
        o_ref[...] = acc
        end()

    res = pl.pallas_call(
        body, name="d_h3", grid=grid,
        in_specs=[pl.BlockSpec((2, tm, D_FF), lambda i, j: (0, i, 0)),
                  pl.BlockSpec((N_CHIPS, tn, shard), lambda i, j: (0, j, 0))] + [ANY] * n,
        out_specs=[pl.BlockSpec((tm, tn), lambda i, j: (i, j))] + [ANY] * n,
        out_shape=[jax.ShapeDtypeStruct((S, D), F32)] + ex.out_shapes, scratch_shapes=ex.scratch(),
        compiler_params=_params(("arbitrary", "arbitrary")),
    )(dhid, w_up, *ex.ins)
    return res[0], res[1:]


def _rms(x, g):
    r = lax.rsqrt(jnp.mean(x * x, axis=-1, keepdims=True) + EPS)
    return x * r * g


def _rms_bwd(x, g, dy):
    r = lax.rsqrt(jnp.mean(x * x, axis=-1, keepdims=True) + EPS)
    xh = x * r
    dxh = dy * g
    dx = r * (dxh - xh * jnp.mean(dxh * xh, axis=-1, keepdims=True))
    return dx, jnp.sum(dy * xh, axis=0, keepdims=True)


def _row_spec(tr, width):
    return pl.BlockSpec((tr, width), lambda i: (i, 0))


def _vec_spec(width):
    return pl.BlockSpec((1, width), lambda i: (0, 0))


def _norm_fwd(name, x, g, ex=None):
    rows, width = x.shape
    tr = min(TR, rows)
    steps = rows // tr
    hosted = ex if ex is not None else _no_exchange()
    n = len(hosted.ins)

    def body(*refs):
        i = pl.program_id(0)
        (x_ref, g_ref), (h_ref,), _, begin, end = _hosted(hosted, refs, 2, 1, i == 0, i == 0, i == steps - 1)
        begin()
        h_ref[...] = _rms(x_ref[...], g_ref[...]).astype(BF16)
        end()

    res = pl.pallas_call(
        body, name=name, grid=(steps,), in_specs=[_row_spec(tr, width), _vec_spec(width)] + [ANY] * n,
        out_specs=[_row_spec(tr, width)] + [ANY] * n,
        out_shape=[jax.ShapeDtypeStruct((rows, width), BF16)] + hosted.out_shapes, scratch_shapes=hosted.scratch(),
        compiler_params=_params(("arbitrary",)),
    )(x, g, *hosted.ins)
    return res[0] if ex is None else (res[0], res[1:])


def _proj_resid_norm(name, a, w, xp, g_post, g_pre, w_next=None):
    def body(a_ref, w_ref, xp_ref, gpost_ref, gpre_ref, *rest):
        y_ref, xn_ref, h_ref = rest[-3:] if w_next is None else rest[1:4]
        y = _dot(a_ref[...], w_ref[...])
        y_ref[...] = y
        xn = xp_ref[...] + _rms(y, gpost_ref[...])
        xn_ref[...] = xn
        h = _rms(xn, gpre_ref[...]).astype(BF16)
        h_ref[...] = h
        if w_next is not None:
            rest[4][...] = _dot(h, rest[0][...]).astype(BF16)

    mat = pl.BlockSpec((D, D), lambda i: (0, 0))
    more = [] if w_next is None else [w_next]
    return pl.pallas_call(
        body, name=name, grid=(S // TR,),
        in_specs=[_row_spec(TR, D), mat, _row_spec(TR, D), _vec_spec(D), _vec_spec(D)] + [mat] * len(more),
        out_specs=[_row_spec(TR, D)] * (3 + len(more)),
        out_shape=[jax.ShapeDtypeStruct((S, D), F32), jax.ShapeDtypeStruct((S, D), F32), jax.ShapeDtypeStruct((S, D), BF16)]
        + [jax.ShapeDtypeStruct((S, D), BF16)] * len(more),
        compiler_params=_params(("parallel",)),
    )(a, w, xp, g_post, g_pre, *more)


def _down_loss_bwd(act, w_down, x3, g_post, target):
    def body(a_ref, w_ref, x_ref, g_ref, t_ref, dres_ref, dy_ref, dg_ref, loss_ref):
        i = pl.program_id(0)

        @pl.when(i == 0)
        def _():
            dg_ref[...] = jnp.zeros_like(dg_ref)
            loss_ref[...] = jnp.zeros_like(loss_ref)

        y = _dot(a_ref[...], w_ref[...])
        g = g_ref[...]
        e = x_ref[...] + _rms(y, g) - t_ref[...]
        loss_ref[...] += jnp.sum(e * e, axis=0, keepdims=True) * (0.5 / D)
        dres = e * (1.0 / D)
        dres_ref[...] = dres
        dy, dg = _rms_bwd(y, g, dres)
        dy_ref[...] = dy.astype(BF16)
        dg_ref[...] += dg

    return pl.pallas_call(
        body, name="down_loss_bwd", grid=(S // TR,),
        in_specs=[_row_spec(TR, D_FF), pl.BlockSpec((D_FF, D), lambda i: (0, 0)), _row_spec(TR, D), _vec_spec(D),
                  _row_spec(TR, D)],
        out_specs=[_row_spec(TR, D), _row_spec(TR, D), _vec_spec(D), _vec_spec(D)],
        out_shape=[jax.ShapeDtypeStruct((S, D), F32), jax.ShapeDtypeStruct((S, D), BF16),
                   jax.ShapeDtypeStruct((1, D), F32), jax.ShapeDtypeStruct((1, D), F32)],
        compiler_params=_params(("arbitrary",)),
    )(act, w_down, x3, g_post, target)


def _mid_bwd(name, dres, xcur, g_pre, dh, yprev, g_post):
    def body(dres_ref, x_ref, gpre_ref, dh_ref, y_ref, gpost_ref, dx_ref, dy_ref, dgpre_ref, dgpost_ref):
        i = pl.program_id(0)

        @pl.when(i == 0)
        def _():
            dgpre_ref[...] = jnp.zeros_like(dgpre_ref)
            dgpost_ref[...] = jnp.zeros_like(dgpost_ref)

        dxn, dgpre = _rms_bwd(x_ref[...], gpre_ref[...], dh_ref[...])
        dx = dres_ref[...] + dxn
        dx_ref[...] = dx
        dy, dgpost = _rms_bwd(y_ref[...], gpost_ref[...], dx)
        dy_ref[...] = dy.astype(BF16)
        dgpre_ref[...] += dgpre
        dgpost_ref[...] += dgpost

    return pl.pallas_call(
        body, name=name, grid=(S // TR,),
        in_specs=[_row_spec(TR, D), _row_spec(TR, D), _vec_spec(D), _row_spec(TR, D), _row_spec(TR, D), _vec_spec(D)],
        out_specs=[_row_spec(TR, D), _row_spec(TR, D), _vec_spec(D), _vec_spec(D)],
        out_shape=[jax.ShapeDtypeStruct((S, D), F32), jax.ShapeDtypeStruct((S, D), BF16),
                   jax.ShapeDtypeStruct((1, D), F32), jax.ShapeDtypeStruct((1, D), F32)],
        compiler_params=_params(("arbitrary",)),
    )(dres, xcur, g_pre, dh, yprev, g_post)


def _first_bwd(dres, x, g, dh):
    def body(dres_ref, x_ref, g_ref, dh_ref, dx_ref, dg_ref):
        i = pl.program_id(0)

        @pl.when(i == 0)
        def _():
            dg_ref[...] = jnp.zeros_like(dg_ref)

        dxn, dg = _rms_bwd(x_ref[...], g_ref[...], dh_ref[...])
        dx_ref[...] = dres_ref[...] + dxn
        dg_ref[...] += dg

    return pl.pallas_call(
        body, name="first_bwd", grid=(S // TR,),
        in_specs=[_row_spec(TR, D), _row_spec(TR, D), _vec_spec(D), _row_spec(TR, D)],
        out_specs=[_row_spec(TR, D), _vec_spec(D)],
        out_shape=[jax.ShapeDtypeStruct((S, D), F32), jax.ShapeDtypeStruct((1, D), F32)],
        compiler_params=_params(("arbitrary",)),
    )(dres, x, g, dh)


def _gain_bwd(name, x, g, dy):
    rows, width = x.shape

    def body(x_ref, g_ref, dy_ref, dg_ref):
        _, dg = _rms_bwd(x_ref[...], g_ref[...], dy_ref[...])
        dg_ref[...] = dg

    return pl.pallas_call(
        body, name=name, grid=(1,), in_specs=[_row_spec(rows, width), _vec_spec(width), _row_spec(rows, width)],
        out_specs=_vec_spec(width), out_shape=jax.ShapeDtypeStruct((1, width), F32),
        compiler_params=_params(("arbitrary",)),
    )(x, g, dy)


CUM_Q = DH
CUM_K = DH + 3
LSE_Q = DH + 6
BOTH_ONE = DH + 9
DEN_V = DH
DELTA = DH + 1
PREP_TR = 256
FOX_FWD_BLOCK = 1024
FOX_BWD_BLOCK = 512


def _at(lane_of_even_head, h):
    return (lane_of_even_head + DH * (h % 2)) % LANES


def _data_lanes(lane, h):
    return lane >= DH if h % 2 else lane < DH


def _pair_block(ref, off, h):
    base = ((off + DH * h) // LANES) * LANES
    return ref[:, base:base + LANES]


def _cumsum_rows(x, tri, carry):
    hi, mid, lo = _split3(x)
    return _dot(tri, hi) + _dot(tri, mid) + _dot(tri, lo) + carry


def _fox_prep(proj, bf_pad):
    tr = PREP_TR

    def body(proj_ref, bf_ref, qa_ref, ka_ref, va_ref, carry_ref):
        i = pl.program_id(0)

        @pl.when(i == 0)
        def _():
            carry_ref[...] = jnp.zeros_like(carry_ref)

        lane = _lane_iota((tr, LANES))
        z = proj_ref[:, F_OFF:F_OFF + LANES] + bf_ref[...]
        log_f = jnp.minimum(z, 0.0) - jnp.log(1.0 + jnp.exp(-jnp.abs(z)))
        log_f = jnp.where(lane < HEADS, log_f, 0.0)
        tri = jnp.where(_row_iota((tr, tr)) >= _lane_iota((tr, tr)), 1.0, 0.0).astype(BF16)
        cum = _cumsum_rows(log_f, tri, carry_ref[0:1, :])
        carry_ref[0:1, :] = cum[tr - 1:tr, :]

        def between(first, h):
            return (lane >= _at(first, h)) & (lane < _at(first, h) + 3)

        ones_q = [jnp.where(between(CUM_K, h) | (lane == _at(BOTH_ONE, h)), 1.0, 0.0) for h in range(2)]
        ones_k = [jnp.where(between(CUM_Q, h) | between(LSE_Q, h) | (lane == _at(BOTH_ONE, h)), 1.0, 0.0) for h in range(2)]
        aug_v = [jnp.where(lane == _at(DEN_V, h), 1.0, jnp.where(between(DELTA, h), -1.0, 0.0)) for h in range(2)]
        for h in range(HEADS):
            c_hi, c_mid, c_lo = _split3_f32(jnp.broadcast_to(cum[:, h:h + 1], (tr, LANES)))
            cq, ck = _at(CUM_Q, h), _at(CUM_K, h)
            aug_q = jnp.where(lane == cq, c_hi, jnp.where(lane == cq + 1, c_mid, jnp.where(lane == cq + 2, c_lo, ones_q[h % 2])))
            aug_k = jnp.where(lane == ck, -c_hi, jnp.where(lane == ck + 1, -c_mid, jnp.where(lane == ck + 2, -c_lo, ones_k[h % 2])))
            data = _data_lanes(lane, h)
            qa_ref[h] = jnp.where(data, _pair_block(proj_ref, Q_OFF, h) * (DH ** -0.5), aug_q).astype(BF16)
            ka_ref[h] = jnp.where(data, _pair_block(proj_ref, K_OFF, h), aug_k).astype(BF16)
            va_ref[h] = jnp.where(data, _pair_block(proj_ref, V_OFF, h), aug_v[h % 2]).astype(BF16)

    head_spec = pl.BlockSpec((HEADS, tr, LANES), lambda i: (0, i, 0))
    head_shape = jax.ShapeDtypeStruct((HEADS, S, LANES), BF16)
    return pl.pallas_call(
        body, name="fox_prep", grid=(S // tr,), in_specs=[_row_spec(tr, D_IN_PAD), _vec_spec(LANES)],
        out_specs=[head_spec] * 3, out_shape=[head_shape] * 3, scratch_shapes=[pltpu.VMEM((SUBLANES, LANES), F32)],
        compiler_params=_params(("arbitrary",)),
    )(proj, bf_pad)


def _hosted(ex, refs, n_blocked_in, n_blocked_out, first, forward_at, last):
    n = len(ex.ins)
    own_in = refs[:n_blocked_in]
    ex_in = refs[n_blocked_in:n_blocked_in + n]
    own_out = refs[n_blocked_in + n:n_blocked_in + n + n_blocked_out]
    ex_out = refs[n_blocked_in + n + n_blocked_out:n_blocked_in + 2 * n + n_blocked_out]
    rest = refs[n_blocked_in + 2 * n + n_blocked_out:]
    args = (ex_in, ex_out, rest[-2], rest[-1])

    def begin():
        @pl.when(first)
        def _():
            ex.start(*args)

        @pl.when(forward_at)
        def _():
            ex.forward(*args)

    def end():
        @pl.when(last)
        def _():
            ex.finish(*args)

    return own_in, own_out, rest[:-2], begin, end


def _fox_fwd(qa, ka, va, ex):
    BQ = BK = FOX_FWD_BLOCK
    nq = S // BQ
    n_pairs = HEADS // 2

    def body(*refs):
        p_id, i = pl.program_id(0), pl.program_id(1)
        (qa_ref, ka_ref, va_ref), (y_ref, qab_ref), (m_scr, acc_scr), begin, end = _hosted(
            ex, refs, 3, 2, (p_id == 0) & (i == 0), (p_id == n_pairs - 1) & (i == 0), (p_id == n_pairs - 1) & (i == nq - 1))
        begin()
        lane = _lane_iota((BQ, LANES))
        causal = _row_iota((BQ, BK)) >= _lane_iota((BQ, BK))
        m_scr[...] = jnp.full_like(m_scr, NEG)
        acc_scr[...] = jnp.zeros_like(acc_scr)

        def step(j, masked):
            rows = pl.ds(pl.multiple_of(j * BK, BK), BK)
            for hh in range(2):
                s = _dot(qa_ref[hh], ka_ref[hh, rows, :], NT)
                if masked:
                    s = jnp.where(causal, s, NEG)
                m_prev = m_scr[hh]
                m_new = jnp.maximum(m_prev, jnp.max(s, axis=1, keepdims=True))
                p = jnp.exp(s - jnp.tile(m_new, (1, BK // LANES)))
                acc_scr[hh] = jnp.exp(m_prev - m_new) * acc_scr[hh] + _dot(p.astype(BF16), va_ref[hh, rows, :])
                m_scr[hh] = m_new

        def full_step(j, carry):
            step(j, False)
            return carry

        lax.fori_loop(0, i, full_step, 0)
        step(i, True)
        outs = []
        for hh in range(2):
            acc = acc_scr[hh]
            den_lane, lse_lane = _at(DEN_V, hh), _at(LSE_Q, hh)
            den = jnp.broadcast_to(acc[:, den_lane:den_lane + 1], (BQ, LANES))
            outs.append(acc * (1.0 / den))
            n_hi, n_mid, n_lo = _split3(-(m_scr[hh] + jnp.log(den)))
            qab_ref[hh] = jnp.where(lane == lse_lane, n_hi,
                                    jnp.where(lane == lse_lane + 1, n_mid, jnp.where(lane == lse_lane + 2, n_lo, qa_ref[hh])))
        y_ref[...] = jnp.where(lane < DH, outs[0], outs[1]).astype(BF16)
        end()

    pair_rows = pl.BlockSpec((2, BQ, LANES), lambda p, i: (p, i, 0))
    pair_all = pl.BlockSpec((2, S, LANES), lambda p, i: (p, 0, 0))
    n = len(ex.ins)
    res = pl.pallas_call(
        body, name="fox_fwd", grid=(n_pairs, nq), in_specs=[pair_rows, pair_all, pair_all] + [ANY] * n,
        out_specs=[pl.BlockSpec((BQ, LANES), lambda p, i: (i, D_POOL // LANES + p)), pair_rows] + [ANY] * n,
        out_shape=[jax.ShapeDtypeStruct((S, D), BF16), jax.ShapeDtypeStruct((HEADS, S, LANES), BF16)] + ex.out_shapes,
        scratch_shapes=[pltpu.VMEM((2, BQ, LANES), F32), pltpu.VMEM((2, BQ, LANES), F32)] + ex.scratch(),
        compiler_params=_params(("arbitrary", "arbitrary")),
    )(qa, ka, va, *ex.ins)
    return res[0], res[1], res[2:]


def _bwd_xa_mix(dqx, w_xq, dres, x2, g_pre, y1, g_post, w_mix_out, ycat, ex):
    steps = S // TR
    n = len(ex.ins)

    def body(*refs):
        i = pl.program_id(0)
        ((dq_ref, wq_ref, dres_ref, x_ref, gpre_ref, y_ref, gpost_ref, wm_ref, ycat_ref),
         (dx_ref, dy_ref, dgpre_ref, dgpost_ref, dp_ref, doa_ref), _, begin, end) = _hosted(
            ex, refs, 9, 6, i == 0, i == 0, i == steps - 1)
        begin()

        @pl.when(i == 0)
        def _():
            dgpre_ref[...] = jnp.zeros_like(dgpre_ref)
            dgpost_ref[...] = jnp.zeros_like(dgpost_ref)

        dxn, dgpre = _rms_bwd(x_ref[...], gpre_ref[...], _dot(dq_ref[...], wq_ref[...], NT))
        dx = dres_ref[...] + dxn
        dx_ref[...] = dx
        dy, dgpost = _rms_bwd(y_ref[...], gpost_ref[...], dx)
        dy = dy.astype(BF16)
        dy_ref[...] = dy
        dgpre_ref[...] += dgpre
        dgpost_ref[...] += dgpost

        d = _dot(dy, wm_ref[...], NT)
        dp_ref[...] = d[:, :D_POOL]
        lane = _lane_iota((TR, LANES))
        low = lane < DH
        for p in range(HEADS // 2):
            cols = slice(D_POOL + LANES * p, D_POOL + LANES * (p + 1))
            do = d[:, cols]
            prod = do * ycat_ref[:, cols].astype(F32)
            deltas = (jnp.sum(jnp.where(low, prod, 0.0), axis=1, keepdims=True),
                      jnp.sum(jnp.where(low, 0.0, prod), axis=1, keepdims=True))
            for hh in range(2):
                d_hi, d_mid, d_lo = _split3_f32(deltas[hh])
                dl = _at(DELTA, hh)
                aug = jnp.where(lane == dl, d_hi, jnp.where(lane == dl + 1, d_mid, jnp.where(lane == dl + 2, d_lo, 0.0)))
                doa_ref[2 * p + hh] = jnp.where(_data_lanes(lane, hh), do, aug).astype(BF16)
        end()

    mat = pl.BlockSpec((D, D), lambda i: (0, 0))
    res = pl.pallas_call(
        body, name="bwd_xa_mix", grid=(steps,),
        in_specs=[_row_spec(TR, D), mat, _row_spec(TR, D), _row_spec(TR, D), _vec_spec(D), _row_spec(TR, D), _vec_spec(D), mat,
                  _row_spec(TR, D)] + [ANY] * n,
        out_specs=[_row_spec(TR, D), _row_spec(TR, D), _vec_spec(D), _vec_spec(D), _row_spec(TR, D_POOL),
                   pl.BlockSpec((HEADS, TR, LANES), lambda i: (0, i, 0))] + [ANY] * n,
        out_shape=[jax.ShapeDtypeStruct((S, D), F32), jax.ShapeDtypeStruct((S, D), BF16), jax.ShapeDtypeStruct((1, D), F32),
                   jax.ShapeDtypeStruct((1, D), F32), jax.ShapeDtypeStruct((S, D_POOL), F32),
                   jax.ShapeDtypeStruct((HEADS, S, LANES), BF16)] + ex.out_shapes,
        scratch_shapes=ex.scratch(), compiler_params=_params(("arbitrary",)),
    )(dqx, w_xq, dres, x2, g_pre, y1, g_post, w_mix_out, ycat, *ex.ins)
    return res[:6], res[6:]


def _fox_bwd(qab, doa, ka, va, ex):
    BQ = BK = FOX_BWD_BLOCK
    nk = S // BK
    n_pairs = HEADS // 2

    def body(*refs):
        p_id, j = pl.program_id(0), pl.program_id(1)
        (qab_ref, doa_ref, ka_ref, va_ref), (dqa_ref, dka_ref, dva_ref), _, begin, end = _hosted(
            ex, refs, 4, 3, (p_id == 0) & (j == 0), (p_id == n_pairs - 1) & (j == 0), (p_id == n_pairs - 1) & (j == nk - 1))
        begin()

        @pl.when(j == 0)
        def _():
            dqa_ref[...] = jnp.zeros_like(dqa_ref)

        causal = _row_iota((BQ, BK)) >= _lane_iota((BQ, BK))
        dka_ref[...] = jnp.zeros_like(dka_ref)
        dva_ref[...] = jnp.zeros_like(dva_ref)

        def step(i, masked):
            rows = pl.ds(pl.multiple_of(i * BQ, BQ), BQ)
            for hh in range(2):
                kb = ka_ref[hh]
                q = qab_ref[hh, rows, :]
                do = doa_ref[hh, rows, :]
                s = _dot(q, kb, NT)
                if masked:
                    s = jnp.where(causal, s, NEG)
                p = jnp.exp(s)
                ds = p * _dot(do, va_ref[hh], NT)
                pb = p.astype(BF16)
                dsb = ds.astype(BF16)
                dva_ref[hh] += _dot(pb, do, TN)
                dka_ref[hh] += _dot(dsb, q, TN)
                dqa_ref[hh, rows, :] += _dot(dsb, kb)

        def full_step(i, carry):
            step(i, False)
            return carry

        step(j, True)
        lax.fori_loop(j + 1, nk, full_step, 0)
        end()

    pair_all = pl.BlockSpec((2, S, LANES), lambda p, j: (p, 0, 0))
    pair_rows = pl.BlockSpec((2, BK, LANES), lambda p, j: (p, j, 0))
    shape = jax.ShapeDtypeStruct((HEADS, S, LANES), F32)
    n = len(ex.ins)
    res = pl.pallas_call(
        body, name="fox_bwd", grid=(n_pairs, nk), in_specs=[pair_all, pair_all, pair_rows, pair_rows] + [ANY] * n,
        out_specs=[pair_all, pair_rows, pair_rows] + [ANY] * n, out_shape=[shape] * 3 + ex.out_shapes,
        scratch_shapes=ex.scratch(), compiler_params=_params(("arbitrary", "arbitrary")),
    )(qab, doa, ka, va, *ex.ins)
    return res[0], res[1], res[2], res[3:]


def _fox_bwd_post(dqa, dka, dva, du, proj, bf_pad):
    tr = PREP_TR
    nt = S // tr

    pick = np.zeros((HEADS * LANES, LANES), np.float32)
    for h in range(HEADS):
        pick[LANES * h + _at(BOTH_ONE, h), h] = 1.0

    def body(dqa_ref, dka_ref, dva_ref, du_ref, z_ref, bf_ref, pick_ref, dp_ref, dbf_ref, carry_ref):
        i = pl.program_id(0)

        @pl.when(i == 0)
        def _():
            carry_ref[...] = jnp.zeros_like(carry_ref)
            dbf_ref[...] = jnp.zeros_like(dbf_ref)

        lane = _lane_iota((tr, LANES))
        diff = jnp.concatenate([dqa_ref[h] - dka_ref[h] for h in range(HEADS)], axis=1)
        hi = diff.astype(BF16)
        dcum = _dot(hi, pick_ref[...]) + _dot((diff - hi.astype(F32)).astype(BF16), pick_ref[...])
        tri =jnp.where(_lane_iota((tr, tr)) >= _row_iota((tr, tr)), 1.0, 0.0).astype(BF16)
        dlog_f = _cumsum_rows(dcum, tri, carry_ref[0:1, :])
        carry_ref[0:1, :] = dlog_f[0:1, :]
        z = z_ref[...] + bf_ref[...]
        df = jnp.where(lane < HEADS, dlog_f / (1.0 + jnp.exp(z)), 0.0)
        dbf_ref[...] += jnp.sum(df, axis=0, keepdims=True)

        dp_ref[:, 0:D_POOL] = du_ref[...].astype(BF16)
        low = lane < DH
        for ref, off, scale in ((dqa_ref, Q_OFF, DH ** -0.5), (dka_ref, K_OFF, 1.0), (dva_ref, V_OFF, 1.0)):
            for p in range(HEADS // 2):
                blk = jnp.where(low, ref[2 * p], ref[2 * p + 1])
                dp_ref[:, off + LANES * p:off + LANES * (p + 1)] = (blk * scale).astype(BF16)
        dp_ref[:, F_OFF:F_OFF + LANES] = df.astype(BF16)

    head_spec = pl.BlockSpec((HEADS, tr, LANES), lambda i: (0, nt - 1 - i, 0))
    return pl.pallas_call(
        body, name="fox_bwd_post", grid=(nt,),
        in_specs=[head_spec, head_spec, head_spec, pl.BlockSpec((tr, D_POOL), lambda i: (nt - 1 - i, 0)),
                  pl.BlockSpec((tr, LANES), lambda i: (nt - 1 - i, F_OFF // LANES)), _vec_spec(LANES),
                  pl.BlockSpec(pick.shape, lambda i: (0, 0))],
        out_specs=[pl.BlockSpec((tr, D_IN_PAD), lambda i: (nt - 1 - i, 0)), _vec_spec(LANES)],
        out_shape=[jax.ShapeDtypeStruct((S, D_IN_PAD), BF16), jax.ShapeDtypeStruct((1, LANES), F32)],
        scratch_shapes=[pltpu.VMEM((SUBLANES, LANES), F32)],
        compiler_params=_params(("arbitrary",)),
    )(dqa, dka, dva, du, proj, bf_pad, jnp.asarray(pick, BF16))


POOL_HALO = 16


def _by_group(lane, a2, a4, a8, a16):
    return jnp.where(lane < 64, a2, jnp.where(lane < 128, a4, jnp.where(lane < 192, a8, a16)))


def _window_count(lane, t):
    return jnp.minimum(t + 1, _by_group(lane, 2, 4, 8, 16)).astype(F32)


def _pool_diff(u, halo, first, tile):
    n = TR + POOL_HALO
    ext = jnp.concatenate([jnp.where(first, 0.0, halo), u], axis=0)
    s2 = ext + pltpu.roll(ext, 1, 0)
    s4 = s2 + pltpu.roll(s2, 2, 0)
    s8 = s4 + pltpu.roll(s4, 4, 0)
    s16 = s8 + pltpu.roll(s8, 8, 0)
    lane = _lane_iota((n, D_POOL))
    win = _by_group(lane, s2, s4, s8, s16)[POOL_HALO:]
    lane = _lane_iota((TR, D_POOL))
    t = tile * TR + _row_iota((TR, D_POOL))
    return win / _window_count(lane, t) - u


def _prev_halo(rows, width, col):
    per = TR // rows
    return pl.BlockSpec((rows, width), lambda i: (jnp.maximum(i * per - 1, 0), col))


def _next_halo(rows, width, col):
    per = TR // rows
    return pl.BlockSpec((rows, width), lambda i: (jnp.minimum((i + 1) * per, S // rows - 1), col))


def _pool_fwd(proj, w_bd, ps, ycat):
    def body(u_ref, halo_ref, w_ref, ps_ref, ycat_ref, y_ref):
        i = pl.program_id(0)
        diff = _pool_diff(u_ref[...], halo_ref[...], i == 0, i)
        y_ref[...] = (_dot(diff.astype(BF16), w_ref[...]) * ps_ref[...]).astype(BF16)

    return pl.pallas_call(
        body, name="pool_fwd", grid=(S // TR,),
        in_specs=[_row_spec(TR, D_POOL), _prev_halo(POOL_HALO, D_POOL, 0),
                  pl.BlockSpec((D_POOL, D_POOL), lambda i: (0, 0)), _vec_spec(D_POOL), ANY],
        out_specs=_row_spec(TR, D_POOL), out_shape=jax.ShapeDtypeStruct((S, D), BF16), input_output_aliases={4: 0},
        compiler_params=_params(("parallel",)),
    )(proj, proj, w_bd, ps, ycat)


def _pool_bwd(proj, dycat, w_bd, w_bd_t, ps):
    nt = S // TR
    n = TR + POOL_HALO

    def body(u_ref, halo_ref, dy_ref, dyn_ref, w_ref, wt_ref, ps_ref, du_ref, dw_ref, dps_ref):
        i = pl.program_id(0)

        @pl.when(i == 0)
        def _():
            dw_ref[...] = jnp.zeros_like(dw_ref)
            dps_ref[...] = jnp.zeros_like(dps_ref)

        diff = _pool_diff(u_ref[...], halo_ref[...], i == 0, i).astype(BF16)
        dy = dy_ref[...]
        dps_ref[...] += jnp.sum(dy * _dot(diff, w_ref[...]), axis=0, keepdims=True)
        dy_ext = jnp.concatenate([dy, jnp.where(i == nt - 1, 0.0, dyn_ref[...])], axis=0)
        dmixed = (dy_ext * ps_ref[...]).astype(BF16)
        ddiff = _dot(dmixed, wt_ref[...])
        dw_ref[...] += _dot(diff, dmixed[:TR], TN)
        lane = _lane_iota((n, D_POOL))
        t = i * TR + _row_iota((n, D_POOL))
        e = ddiff / _window_count(lane, t)
        f2 = e + pltpu.roll(e, n - 1, 0)
        f4 = f2 + pltpu.roll(f2, n - 2, 0)
        f8 = f4 + pltpu.roll(f4, n - 4, 0)
        f16 = f8 + pltpu.roll(f8, n - 8, 0)
        du_ref[...] = _by_group(lane, f2, f4, f8, f16)[:TR] - ddiff[:TR]

    mat = pl.BlockSpec((D_POOL, D_POOL), lambda i: (0, 0))
    return pl.pallas_call(
        body, name="pool_bwd", grid=(nt,),
        in_specs=[_row_spec(TR, D_POOL), _prev_halo(POOL_HALO, D_POOL, 0), _row_spec(TR, D_POOL),
                  _next_halo(POOL_HALO, D_POOL, 0), mat, mat, _vec_spec(D_POOL)],
        out_specs=[_row_spec(TR, D_POOL), mat, _vec_spec(D_POOL)],
        out_shape=[jax.ShapeDtypeStruct((S, D_POOL), F32), jax.ShapeDtypeStruct((D_POOL, D_POOL), F32),
                   jax.ShapeDtypeStruct((1, D_POOL), F32)],
        compiler_params=_params(("arbitrary",)),
    )(proj, proj, dycat, dycat, w_bd, w_bd_t, ps)


def _xa_probs(q, k):
    s = _dot(q, k, NT) * (XA_DH ** -0.5)
    e = jnp.exp(s - jnp.max(s, axis=-1, keepdims=True))
    return e * (1.0 / jnp.sum(e, axis=-1, keepdims=True))


def _xattn_fwd(qx, kv):
    def body(q_ref, kv_ref, o_ref):
        for h in range(XA_HEADS):
            cols = slice(XA_DH * h, XA_DH * (h + 1))
            vcols = slice(D + XA_DH * h, D + XA_DH * (h + 1))
            p = _xa_probs(q_ref[:, cols], kv_ref[:, cols])
            o_ref[:, cols] = _dot(p.astype(BF16), kv_ref[:, vcols]).astype(BF16)

    return pl.pallas_call(
        body, name="xattn_fwd", grid=(S // TR,),
        in_specs=[_row_spec(TR, D), pl.BlockSpec((MEM, 2 * D), lambda i: (0, 0))],
        out_specs=_row_spec(TR, D), out_shape=jax.ShapeDtypeStruct((S, D), BF16),
        compiler_params=_params(("parallel",)),
    )(qx, kv)


def _xattn_bwd(qx, kv, dxo):
    def body(q_ref, kv_ref, do_ref, dq_ref, dkv_ref):
        i = pl.program_id(0)

        @pl.when(i == 0)
        def _():
            dkv_ref[...] = jnp.zeros_like(dkv_ref)

        for h in range(XA_HEADS):
            cols = slice(XA_DH * h, XA_DH * (h + 1))
            vcols = slice(D + XA_DH * h, D + XA_DH * (h + 1))
            q = q_ref[:, cols]
            k = kv_ref[:, cols]
            do = do_ref[:, cols]
            p = _xa_probs(q, k)
            dkv_ref[:, vcols] += _dot(p.astype(BF16), do, TN)
            dp = _dot(do, kv_ref[:, vcols], NT)
            ds = (p * (dp - jnp.sum(p * dp, axis=-1, keepdims=True)) * (XA_DH ** -0.5)).astype(BF16)
            dq_ref[:, cols] = _dot(ds, k).astype(BF16)
            dkv_ref[:, cols] += _dot(ds, q, TN)

    kv_spec = pl.BlockSpec((MEM, 2 * D), lambda i: (0, 0))
    return pl.pallas_call(
        body, name="xattn_bwd", grid=(S // TR,), in_specs=[_row_spec(TR, D), kv_spec, _row_spec(TR, D)],
        out_specs=[_row_spec(TR, D), kv_spec],
        out_shape=[jax.ShapeDtypeStruct((S, D), BF16), jax.ShapeDtypeStruct((MEM, 2 * D), F32)],
        compiler_params=_params(("arbitrary",)),
    )(qx, kv, dxo)


CONV_HALO = SUBLANES
TC = 512
GELU_K = 0.7978845608028654
GELU_C = 0.044715


def _conv3(ext, w, rows):
    h0 = ext[CONV_HALO:CONV_HALO + rows]
    h1 = pltpu.roll(ext, 1, 0)[CONV_HALO:CONV_HALO + rows]
    h2 = pltpu.roll(ext, 2, 0)[CONV_HALO:CONV_HALO + rows]
    return w[2:3] * h0 + w[1:2] * h1 + w[0:1] * h2 + w[3:4], (h2, h1, h0)


def _conv_specs():
    main = pl.BlockSpec((2, TR, TC), lambda j, i: (0, i, j))
    per = TR // CONV_HALO
    prev = pl.BlockSpec((2, CONV_HALO, TC), lambda j, i: (0, jnp.maximum(i * per - 1, 0), j))
    nxt = pl.BlockSpec((2, CONV_HALO, TC), lambda j, i: (0, jnp.minimum((i + 1) * per, S // CONV_HALO - 1), j))
    par = pl.BlockSpec((2, SUBLANES, TC), lambda j, i: (0, 0, j))
    return main, prev, nxt, par


def _convgate_fwd(hid, cwb):
    def body(h_ref, hp_ref, w_ref, act_ref):
        i = pl.program_id(1)
        c = []
        for g in range(2):
            ext = jnp.concatenate([jnp.where(i == 0, 0.0, hp_ref[g]), h_ref[g]], axis=0)
            c.append(_conv3(ext, w_ref[g], TR)[0])
        gate, up = c
        act_ref[...] = (jax.nn.gelu(gate, approximate=True) * up).astype(BF16)

    main, prev, _, par = _conv_specs()
    return pl.pallas_call(
        body, name="convgate_fwd", grid=(D_FF // TC, S // TR), in_specs=[main, prev, par],
        out_specs=pl.BlockSpec((TR, TC), lambda j, i: (i, j)), out_shape=jax.ShapeDtypeStruct((S, D_FF), BF16),
        compiler_params=_params(("parallel", "parallel")),
    )(hid, hid, cwb)


def _convgate_bwd(hid, dact, cwb):
    nr = S // TR
    n = TR + CONV_HALO

    def body(h_ref, hp_ref, hn_ref, da_ref, dan_ref, w_ref, dh_ref, dw_ref):
        i = pl.program_id(1)

        @pl.when(i == 0)
        def _():
            dw_ref[...] = jnp.zeros_like(dw_ref)

        da = jnp.concatenate([da_ref[...], jnp.where(i == nr - 1, 0.0, dan_ref[...])], axis=0)
        c, taps = [], []
        for g in range(2):
            ext = jnp.concatenate([jnp.where(i == 0, 0.0, hp_ref[g]), h_ref[g], hn_ref[g]], axis=0)
            cg, tg = _conv3(ext, w_ref[g], n)
            c.append(cg)
            taps.append(tg)
        gate, up = c
        th = jnp.tanh(GELU_K * (gate + GELU_C * gate * gate * gate))
        gelu = 0.5 * gate * (1.0 + th)
        dgelu = 0.5 * (1.0 + th) + 0.5 * gate * (1.0 - th * th) * GELU_K * (1.0 + 3.0 * GELU_C * gate * gate)
        for g, dc in enumerate((da * up * dgelu, da * gelu)):
            w = w_ref[g]
            dh = w[2:3] * dc[:TR] + w[1:2] * pltpu.roll(dc, n - 1, 0)[:TR] + w[0:1] * pltpu.roll(dc, n - 2, 0)[:TR]
            dh_ref[g] = dh.astype(BF16)
            dcm = dc[:TR]
            for r in range(3):
                dw_ref[g, r:r + 1, :] += jnp.sum(dcm * taps[g][r][:TR], axis=0, keepdims=True)
            dw_ref[g, 3:4, :] += jnp.sum(dcm, axis=0, keepdims=True)

    main, prev, nxt, par = _conv_specs()
    per = TR // CONV_HALO
    return pl.pallas_call(
        body, name="convgate_bwd", grid=(D_FF // TC, nr),
        in_specs=[main, prev, nxt, pl.BlockSpec((TR, TC), lambda j, i: (i, j)),
                  pl.BlockSpec((CONV_HALO, TC), lambda j, i: (jnp.minimum((i + 1) * per, S // CONV_HALO - 1), j)), par],
        out_specs=[main, par],
        out_shape=[jax.ShapeDtypeStruct((2, S, D_FF), BF16), jax.ShapeDtypeStruct((2, SUBLANES, D_FF), F32)],
        compiler_params=_params(("parallel", "arbitrary")),
    )(hid, hid, hid, dact, dact, cwb)


def _adam_update(w, g, m, v):
    m = ADAM_B1 * m + (1.0 - ADAM_B1) * g
    v = ADAM_B2 * v + (1.0 - ADAM_B2) * (g * g)
    m_hat = m / (1.0 - ADAM_B1 ** ADAM_STEP)
    v_hat = v / (1.0 - ADAM_B2 ** ADAM_STEP)
    return -ADAM_LR * (m_hat / (jnp.sqrt(v_hat) + ADAM_EPS) + ADAM_WD * w), m, v


def _row_tile(rows, cols, itemsize=4, target=TILE_BYTES):
    tr = SUBLANES
    while rows % (2 * tr) == 0 and 2 * tr * cols * itemsize <= target:
        tr *= 2
    assert rows % tr == 0, (rows, tr)
    return tr


def _adamw(name, w, g, m, v):
    rows, cols = w.shape
    tr = rows if rows * cols * 4 <= TILE_BYTES // 2 else _row_tile(rows, cols, target=TILE_BYTES // 2)

    def body(w_ref, g_ref, m_ref, v_ref, d_ref, nm_ref, nv_ref):
        d_ref[...], nm_ref[...], nv_ref[...] = _adam_update(w_ref[...], g_ref[...], m_ref[...], v_ref[...])

    spec = _row_spec(tr, cols)
    shape = jax.ShapeDtypeStruct((rows, cols), F32)
    return pl.pallas_call(
        body, name=name, grid=(rows // tr,), in_specs=[spec] * 4, out_specs=[spec] * 3, out_shape=[shape] * 3,
        compiler_params=_params(("parallel",)),
    )(w, g, m, v)


def _adamw_halves(name, core, w, g_mine, g_sibling, m, v):
    rows, cols = w.shape
    half = rows // 2
    tr = _row_tile(half, cols, target=TILE_BYTES // 2)
    per = half // tr

    def body(core_ref, w_ref, gm_ref, gs_ref, m_ref, v_ref, g_ref, d_ref, nm_ref, nv_ref):
        g = jnp.where(pl.program_id(0) // per == core_ref[0], gm_ref[...], gs_ref[...])
        g_ref[...] = g
        d_ref[...], nm_ref[...], nv_ref[...] = _adam_update(w_ref[...], g, m_ref[...], v_ref[...])

    spec = pl.BlockSpec((tr, cols), lambda i, core_ref: (i, 0))
    half_spec = pl.BlockSpec((tr, cols), lambda i, core_ref: (i % per, 0))
    shape = jax.ShapeDtypeStruct((rows, cols), F32)
    return pl.pallas_call(
        body, name=name, out_shape=[shape] * 4,
        grid_spec=pltpu.PrefetchScalarGridSpec(
            num_scalar_prefetch=1, grid=(rows // tr,), in_specs=[spec, half_spec, half_spec, spec, spec], out_specs=[spec] * 4),
        compiler_params=_params(("parallel",)),
    )(core, w, g_mine, g_sibling, m, v)


def _chip_sum(name, core, g, other):
    _, _, half, cols = g.shape
    tr = _row_tile(half, cols)

    def body(core_ref, g_ref, o_ref, p_ref):
        p_ref[...] = (g_ref[...] + o_ref[...]).astype(BF16)

    spec = pl.BlockSpec((None, tr, cols), lambda j, i, core_ref: (j, i, 0))
    return pl.pallas_call(
        body, name=name, out_shape=jax.ShapeDtypeStruct((N_CHIPS, half, cols), BF16),
        grid_spec=pltpu.PrefetchScalarGridSpec(
            num_scalar_prefetch=1, grid=(N_CHIPS, half // tr),
            in_specs=[pl.BlockSpec((None, None, tr, cols), lambda j, i, core_ref: (j, core_ref[0], i, 0)), spec],
            out_specs=spec),
        compiler_params=_params(("parallel", "parallel")),
    )(core, g, other)


def _mesh_sum(name, chip, received, own):
    _, half, cols = received.shape
    tr = _row_tile(half, cols, itemsize=2 * N_CHIPS)

    def body(chip_ref, r_ref, own_ref, o_ref):
        acc = None
        for j in range(N_CHIPS):
            term = jnp.where(chip_ref[0] == j, own_ref[...], r_ref[j]).astype(F32)
            acc = term if acc is None else acc + term
        o_ref[...] = acc

    return pl.pallas_call(
        body, name=name, out_shape=jax.ShapeDtypeStruct((half, cols), F32),
        grid_spec=pltpu.PrefetchScalarGridSpec(
            num_scalar_prefetch=1, grid=(half // tr,),
            in_specs=[pl.BlockSpec((N_CHIPS, tr, cols), lambda i, chip_ref: (0, i, 0)),
                      pl.BlockSpec((None, tr, cols), lambda i, chip_ref: (chip_ref[0], i, 0))],
            out_specs=pl.BlockSpec((tr, cols), lambda i, chip_ref: (i, 0))),
        compiler_params=_params(("parallel",)),
    )(chip, received, own)


CHIP_FLIPS = ((1, 0), (0, 1), (1, 1))


def _place():
    x, y, c = lax.axis_index("x"), lax.axis_index("y"), lax.axis_index("c")
    return x, y, c, 2 * x + y


def _remote(src, dst, sems_s, sems_r, k, dev):
    return pltpu.make_async_remote_copy(src_ref=src, dst_ref=dst, send_sem=sems_s.at[k], recv_sem=sems_r.at[k],
                                        device_id=dev, device_id_type=MESH)


class _Exchange:
    def __init__(self, ins, out_shapes, n_sems, start, forward, finish):
        self.ins, self.out_shapes, self.n_sems = list(ins), list(out_shapes), n_sems
        self.start, self.forward, self.finish = start, forward, finish

    def scratch(self):
        return [pltpu.SemaphoreType.DMA((self.n_sems,)), pltpu.SemaphoreType.DMA((self.n_sems,))]

    def run(self, name):
        n = len(self.ins)

        def body(*refs):
            args = (refs[:n], refs[n:2 * n]) + tuple(refs[2 * n:])
            self.start(*args)
            self.forward(*args)
            self.finish(*args)

        return pl.pallas_call(
            body, name=name, in_specs=[ANY] * n, out_specs=[ANY] * n, out_shape=self.out_shapes, scratch_shapes=self.scratch(),
        )(*self.ins)


def _all_gather_weights(halved, whole):
    nh, nw = len(halved), len(whole)
    n_arr = nh + nw

    def copies(ins, outs, sems_s, sems_r):
        x, y, c, me = _place()
        sibling = (x, y, 1 - c)
        own = [_remote(ins[k], outs[k].at[me], sems_s, sems_r, k, sibling) for k in range(n_arr)]
        first, passed = [], []
        for k in range(n_arr):
            for f, (fx, fy) in enumerate(CHIP_FLIPS):
                src, dst = (ins[k].at[c], outs[k].at[me, c]) if k < nh else (ins[k], outs[k].at[me])
                first.append(_remote(src, dst, sems_s, sems_r, n_arr + 3 * k + f, (x ^ fx, y ^ fy, c)))
        for k in range(nh):
            for f, (fx, fy) in enumerate(CHIP_FLIPS):
                landed = outs[k].at[2 * (x ^ fx) + (y ^ fy), c]
                passed.append(_remote(landed, landed, sems_s, sems_r, 4 * n_arr + 3 * k + f, sibling))
        return own, first, passed

    def start(*refs):
        own, first, _ = copies(*refs)
        for cp in own + first:
            cp.start()

    def forward(*refs):
        _, first, passed = copies(*refs)
        for arrived, cp in zip(first, passed):
            arrived.wait_recv()
            cp.start()

    def finish(*refs):
        own, first, passed = copies(*refs)
        for cp in first[3 * nh:] + passed + own:
            cp.wait_recv()
        for cp in first + passed + own:
            cp.wait_send()

    shapes = [jax.ShapeDtypeStruct((N_CHIPS,) + a.shape, a.dtype) for a in list(halved) + list(whole)]
    return _Exchange(list(halved) + list(whole), shapes, 7 * nh + 4 * nw, start, forward, finish)


def _swap_halves(gs):
    n = len(gs)

    def copies(ins, outs, sems_s, sems_r):
        x, y, c, _ = _place()
        return [_remote(ins[k].at[:, 1 - c], outs[k], sems_s, sems_r, k, (x, y, 1 - c)) for k in range(n)]

    def start(*refs):
        for cp in copies(*refs):
            cp.start()

    def finish(*refs):
        for cp in copies(*refs):
            cp.wait()

    shapes = [jax.ShapeDtypeStruct((g.shape[0],) + g.shape[2:], g.dtype) for g in gs]
    return _Exchange(gs, shapes, n, start, _no_copies, finish)


def _scatter_chips(ps):
    n = len(ps)

    def copies(ins, outs, sems_s, sems_r):
        x, y, c, me = _place()
        return [_remote(ins[k].at[2 * (x ^ fx) + (y ^ fy)], outs[k].at[me], sems_s, sems_r, 3 * k + f, (x ^ fx, y ^ fy, c))
                for k in range(n) for f, (fx, fy) in enumerate(CHIP_FLIPS)]

    def start(*refs):
        for cp in copies(*refs):
            cp.start()

    def forward(*refs):
        pass

    def finish(*refs):
        for cp in copies(*refs):
            cp.wait()

    shapes = [jax.ShapeDtypeStruct(p.shape, p.dtype) for p in ps]
    return _Exchange(ps, shapes, 3 * n, start, forward, finish)


def _swap_reduced(rs):
    n = len(rs)

    def copies(ins, outs, sems_s, sems_r):
        x, y, c, _ = _place()
        return [_remote(ins[k], outs[k], sems_s, sems_r, k, (x, y, 1 - c)) for k in range(n)]

    def start(*refs):
        for cp in copies(*refs):
            cp.start()

    def finish(*refs):
        for cp in copies(*refs):
            cp.wait()

    return _Exchange(rs, [jax.ShapeDtypeStruct(r.shape, r.dtype) for r in rs], n, start, _no_copies, finish)


N_DEV = 8


def _gather_small(buf):
    def copies(ins, outs, sems_s, sems_r):
        x, y, c, _ = _place()
        me = 4 * x + 2 * y + c
        return [_remote(ins[0], outs[0].at[me], sems_s, sems_r, o - 1, (x ^ (o >> 2), y ^ ((o >> 1) & 1), c ^ (o & 1)))
                for o in range(1, N_DEV)]

    def start(*refs):
        for cp in copies(*refs):
            cp.start()

    def finish(*refs):
        for cp in copies(*refs):
            cp.wait()

    return _Exchange([buf], [jax.ShapeDtypeStruct((N_DEV,) + buf.shape, buf.dtype)], N_DEV - 1, start, _no_copies, finish)


def _sum_devices(place, gathered, own):
    rows = own.shape[0]

    def body(place_ref, g_ref, own_ref, o_ref):
        acc = None
        for d in range(N_DEV):
            term = jnp.where(place_ref[0] == d, own_ref[...], g_ref[d])
            acc = term if acc is None else acc + term
        o_ref[...] = acc

    return pl.pallas_call(
        body, name="sum_devices", out_shape=jax.ShapeDtypeStruct((rows, LANES), F32),
        grid_spec=pltpu.PrefetchScalarGridSpec(
            num_scalar_prefetch=1, grid=(1,),
            in_specs=[pl.BlockSpec((N_DEV, rows, LANES), lambda i, place_ref: (0, 0, 0)),
                      pl.BlockSpec((rows, LANES), lambda i, place_ref: (0, 0))],
            out_specs=pl.BlockSpec((rows, LANES), lambda i, place_ref: (0, 0))),
        compiler_params=_params(("arbitrary",)),
    )(place, gathered, own)


def _no_copies(*refs):
    pass


def _no_exchange():
    return _Exchange([], [], 1, _no_copies, _no_copies, _no_copies)


class _NoComm:
    def gather_first(self):
        return _no_exchange()

    def first_landed(self, p, landed):
        pass

    def gather_rest(self, p):
        return _no_exchange()

    def weights_landed(self, p, landed):
        pass

    def swap_first(self, g):
        return _no_exchange()

    def first_swapped(self, landed):
        pass

    def swap_second(self, g):
        return _no_exchange()

    def second_swapped(self, landed):
        pass

    def scatter_early(self, g):
        return _no_exchange()

    def scatter_landed(self, landed):
        pass

    def swap_reduced_early(self):
        return _no_exchange()

    def reduced_landed(self, landed):
        pass

    def scatter_late(self, g):
        return _no_exchange()

    def late_landed(self, landed):
        pass


def _local_step(x, mem, target, p, comm):
    h1, landed = _norm_fwd("norm_mix_pre", x, p["norm_mix_pre"], comm.gather_first())
    comm.first_landed(p, landed)
    proj = _mm_nn("in_proj", h1, p["w_in"], F32, 1024, 896)
    qa, ka, va = _fox_prep(proj, p["bf_pad"])
    ycat, qab, landed = _fox_fwd(qa, ka, va, comm.gather_rest(p))
    comm.weights_landed(p, landed)
    ycat = _pool_fwd(proj, p["w_pool_bd"], p["pool_scale"], ycat)
    y1, x2, h2, qx = _proj_resid_norm("mix_out", ycat, p["w_mix_out"], x, p["norm_mix_post"], p["norm_xa_pre"], p["w_xq"])
    mem_n = _norm_fwd("norm_mem", mem, p["norm_mem"])
    kv = _mm(
        "xkv", mem_n, p["w_xkv"], pl.BlockSpec((MEM, D), lambda i, j, k: (0, 0)),
        pl.BlockSpec((None, D, 512), lambda i, j, k: (j, 0, 0)), jax.ShapeDtypeStruct((MEM, 2 * D), BF16),
        pl.BlockSpec((MEM, 512), lambda i, j, k: (0, j)), (1, N_CHIPS, 1), NN, (MEM, 512))
    xo = _xattn_fwd(qx, kv)
    y2, x3, h3 = _proj_resid_norm("xo", xo, p["w_xo"], x2, p["norm_xa_post"], p["norm_ffn_pre"])
    hid = _mm(
        "up_proj", h3, p["w_up"], pl.BlockSpec((1024, D), lambda i, j, k: (i, 0)),
        pl.BlockSpec((None, D, 1024), lambda i, j, k: (j // 2, 0, j % 2)), jax.ShapeDtypeStruct((2, S, D_FF), F32),
        pl.BlockSpec((None, 1024, 1024), lambda i, j, k: (j // 4, i, j % 4)), (S // 1024, 8, 1), NN, (1024, 1024))
    act = _convgate_fwd(hid, p["cwb"])

    g = {}
    dres, dy3, g["norm_ffn_post"], loss_cols = _down_loss_bwd(act, p["w_down"], x3, p["norm_ffn_post"], target)
    dact = _mm_nt("d_act", dy3, p["w_down"], F32, 1024, 1024)
    g["w_down"] = _mm_tn("dw_down", act, dy3, 512, 512)
    dhid, dcwb = _convgate_bwd(hid, dact, p["cwb"])
    g["w_up"] = _mm(
        "dw_up", h3, dhid, pl.BlockSpec((S, 512), lambda i, j, k: (0, i)),
        pl.BlockSpec((None, S, 512), lambda i, j, k: (j // 8, 0, j % 8)), jax.ShapeDtypeStruct((N_CHIPS, D, 2048), F32),
        pl.BlockSpec((None, 512, 512), lambda i, j, k: (j // 4, i, j % 4)), (2, 16, 1), TN, (512, 512))
    dh3, landed = _d_h3(dhid, p["w_up"], comm.swap_first(g))
    comm.first_swapped(landed)
    dres, dy2, g["norm_ffn_pre"], g["norm_xa_post"] = _mid_bwd("bwd_ffn_xa", dres, x3, p["norm_ffn_pre"], dh3, y2, p["norm_xa_post"])
    dxo = _mm_nt("d_xo", dy2, p["w_xo"], BF16, 1024, 1024)
    g["w_xo"] = _mm_tn("dw_xo", xo, dy2, 512, 512)
    dqx, dkv = _xattn_bwd(qx, kv, dxo)
    dkv = dkv.astype(BF16)
    g["w_xq"] = _mm_tn("dw_xq", h2, dqx, 512, 512)
    dmem_n = _mm(
        "d_mem", dkv, p["w_xkv"], pl.BlockSpec((MEM, 512), lambda i, j, k: (0, k)),
        pl.BlockSpec((None, D, 512), lambda i, j, k: (k, 0, 0)), jax.ShapeDtypeStruct((MEM, D), F32),
        pl.BlockSpec((MEM, D), lambda i, j, k: (0, 0)), (1, 1, N_CHIPS), NT, (MEM, D))
    g["w_xkv"] = _mm(
        "dw_xkv", mem_n, dkv, pl.BlockSpec((MEM, D), lambda i, j, k: (0, 0)),
        pl.BlockSpec((MEM, 512), lambda i, j, k: (0, j)), jax.ShapeDtypeStruct((N_CHIPS, D, 512), F32),
        pl.BlockSpec((None, D, 512), lambda i, j, k: (j, 0, 0)), (1, N_CHIPS, 1), TN, (D, 512))
    g["norm_mem"] = _gain_bwd("dg_mem", mem, p["norm_mem"], dmem_n)
    (dres, dy1, g["norm_xa_pre"], g["norm_mix_post"], dy_pool, doa), landed = _bwd_xa_mix(
        dqx, p["w_xq"], dres, x2, p["norm_xa_pre"], y1, p["norm_mix_post"], p["w_mix_out"], ycat, comm.swap_second(g))
    comm.second_swapped(landed)
    g["w_mix_out"] = _mm_tn("dw_mix_out", ycat, dy1, 512, 512)
    dqa, dka, dva, landed = _fox_bwd(qab, doa, ka, va, comm.scatter_early(g))
    comm.scatter_landed(landed)
    du, g["w_pool_full"], g["pool_scale"] = _pool_bwd(proj, dy_pool, p["w_pool_bd"], p["w_pool_bd_t"], p["pool_scale"])
    dproj, g["bf_pad"] = _fox_bwd_post(dqa, dka, dva, du, proj, p["bf_pad"])
    g["w_in"], landed = _mm_tn("dw_in", h1, dproj, 512, 896, comm.swap_reduced_early())
    comm.reduced_landed(landed)
    dh1, landed = _mm_nt("d_h1", dproj, p["w_in"], F32, 1024, 1024, comm.scatter_late(g))
    comm.late_landed(landed)
    grad_x, g["norm_mix_pre"] = _first_bwd(dres, x, p["norm_mix_pre"], dh1)
    g["cwb"] = dcwb
    return grad_x, g, loss_cols


BIG = ("w_in", "w_mix_out", "w_xq", "w_xkv", "w_xo", "w_up", "w_down")
ROW_SHARDED = ("w_mix_out", "w_xq", "w_xo", "w_down")
SMALL = ("norm_mix_pre", "norm_mix_post", "b_forget", "w_pool", "pool_scale", "norm_mem", "norm_xa_pre", "norm_xa_post",
         "norm_ffn_pre", "norm_ffn_post", "conv_b")
ORDER = ("norm_mix_pre", "norm_mix_post", "w_in", "b_forget", "w_pool", "pool_scale", "w_mix_out", "norm_mem", "norm_xa_pre",
         "norm_xa_post", "w_xq", "w_xkv", "w_xo", "norm_ffn_pre", "norm_ffn_post", "w_up", "conv_w", "conv_b", "w_down")
SLOT = SUBLANES * LANES


def _pack(parts):
    rows, offs, off = [], [], 0
    for a in parts:
        flat = a.reshape(-1).astype(F32)
        n = -(-flat.shape[0] // SLOT) * SLOT
        rows.append(jnp.pad(flat, (0, n - flat.shape[0])).reshape(n // LANES, LANES))
        offs.append(off)
        off += n // LANES
    return jnp.concatenate(rows, axis=0), offs


def _unpack(buf, off, like):
    n = like.size
    rows = -(-n // LANES)
    return buf[off:off + rows].reshape(-1)[:n].reshape(like.shape)


FIRST = ("w_in",)
REST = ("w_mix_out", "w_xq", "w_xkv", "w_xo", "w_up", "w_down")


def _local_params(w):
    w_pool_bd = jnp.zeros((D_POOL, D_POOL), F32)
    for gi in range(4):
        w_pool_bd = w_pool_bd.at[64 * gi:64 * (gi + 1), 64 * gi:64 * (gi + 1)].set(w["w_pool"][0, gi])
    p = {n: w[n] for n in ("norm_mix_pre", "norm_mix_post", "norm_mem", "norm_xa_pre", "norm_xa_post", "norm_ffn_pre",
                           "norm_ffn_post")}
    p.update(
        bf_pad=jnp.pad(w["b_forget"], ((0, 0), (0, LANES - HEADS))),
        w_pool_bd=w_pool_bd.astype(BF16), w_pool_bd_t=w_pool_bd.T.astype(BF16), pool_scale=w["pool_scale"].reshape(1, D_POOL))
    return p


def _w_in_param(stacked):
    return jnp.pad(jnp.concatenate(list(stacked), axis=1), ((0, 0), (0, D_IN_PAD - D_IN)))


def _rest_params(w, full, conv_w_full):
    cw2 = conv_w_full.reshape(3, 2, D_FF).transpose(1, 0, 2)
    cwb = jnp.concatenate([cw2, w["conv_b"].reshape(1, 2, D_FF).transpose(1, 0, 2), jnp.zeros((2, 4, D_FF), F32)], axis=1)
    return dict(w_mix_out=full["w_mix_out"].reshape(D, D), w_xq=full["w_xq"].reshape(D, D), w_xkv=full["w_xkv"],
                w_xo=full["w_xo"].reshape(D, D), w_up=full["w_up"], cwb=cwb, w_down=full["w_down"].reshape(D_FF, D))


def _whole_params(w, full, conv_w_full):
    p = _local_params(w)
    p.update(_rest_params(w, full, conv_w_full), w_in=_w_in_param(full["w_in"]))
    return p


def _halved(a):
    return a.reshape(a.shape[:-2] + (2, a.shape[-2] // 2, a.shape[-1]))


class _StepComm:
    def __init__(self, w, shard2d, conv_w, core_id, chip_id):
        self.w, self.shard2d, self.conv_w, self.core_id, self.chip_id = w, shard2d, conv_w, core_id, chip_id
        self.first, self.second = ("w_up", "w_down"), ("w_xq", "w_xkv", "w_xo")
        self.early = self.first + self.second
        self.late = ("w_in", "w_mix_out")

    def gather_first(self):
        return _all_gather_weights([_halved(self.shard2d[n].astype(BF16)) for n in FIRST], [])

    def first_landed(self, p, landed):
        p["w_in"] = _w_in_param(landed[0].reshape((N_CHIPS,) + self.shard2d["w_in"].shape))

    def gather_rest(self, p):
        return _all_gather_weights([_halved(self.shard2d[n].astype(BF16)) for n in REST], [self.conv_w.reshape(3, -1)])

    def weights_landed(self, p, landed):
        full = {n: a.reshape((N_CHIPS,) + self.shard2d[n].shape) for n, a in zip(REST, landed)}
        conv_w_full = jnp.transpose(landed[-1], (1, 0, 2)).reshape(3, 2 * D_FF)
        p.update(_rest_params(self.w, full, conv_w_full))

    def _view(self, g, n):
        return _halved(g[n].reshape((N_CHIPS,) + self.shard2d[n].shape))

    def swap_first(self, g):
        return _swap_halves([self._view(g, n) for n in self.first])

    def first_swapped(self, landed):
        self.from_sibling = dict(zip(self.first, landed))

    def swap_second(self, g):
        return _swap_halves([self._view(g, n) for n in self.second])

    def second_swapped(self, landed):
        self.from_sibling.update(zip(self.second, landed))

    def scatter_early(self, g):
        self.partial = [_chip_sum("chip_sum_" + n, self.core_id, self._view(g, n), self.from_sibling[n]) for n in self.early]
        return _scatter_chips(self.partial)

    def scatter_landed(self, landed):
        self.received = list(landed)

    def swap_reduced_early(self):
        self.reduced = [_mesh_sum("mesh_sum_" + n, self.chip_id, r, own)
                        for n, r, own in zip(self.early, self.received, self.partial)]
        return _swap_reduced(self.reduced)

    def reduced_landed(self, landed):
        self.reduced_sibling = list(landed)

    def scatter_late(self, g):
        gw_in = g["w_in"][:, :D_IN]
        cols = D_IN // N_CHIPS
        views = [_halved(jnp.stack([gw_in[:, cols * j:cols * (j + 1)] for j in range(N_CHIPS)])), self._view(g, "w_mix_out")]
        from_sibling = _swap_halves(views).run("swap_halves_late")
        self.partial_late = [_chip_sum("chip_sum_" + n, self.core_id, view, other)
                             for n, view, other in zip(self.late, views, from_sibling)]
        return _scatter_chips(self.partial_late)

    def late_landed(self, landed):
        self.received_late = list(landed)


def kernel(x, mem, norm_mix_pre, norm_mix_post, w_in, b_forget, w_pool, pool_scale, w_mix_out, norm_mem, norm_xa_pre, norm_xa_post, w_xq, w_xkv, w_xo, norm_ffn_pre, norm_ffn_post, w_up, conv_w, conv_b, w_down, loss_target, m_norm_mix_pre, m_norm_mix_post, m_w_in, m_b_forget, m_w_pool, m_pool_scale, m_w_mix_out, m_norm_mem, m_norm_xa_pre, m_norm_xa_post, m_w_xq, m_w_xkv, m_w_xo, m_norm_ffn_pre, m_norm_ffn_post, m_w_up, m_conv_w, m_conv_b, m_w_down, v_norm_mix_pre, v_norm_mix_post, v_w_in, v_b_forget, v_w_pool, v_pool_scale, v_w_mix_out, v_norm_mem, v_norm_xa_pre, v_norm_xa_post, v_w_xq, v_w_xkv, v_w_xo, v_norm_ffn_pre, v_norm_ffn_post, v_w_up, v_conv_w, v_conv_b, v_w_down):
    w = dict(norm_mix_pre=norm_mix_pre, norm_mix_post=norm_mix_post, w_in=w_in, b_forget=b_forget, w_pool=w_pool,
             pool_scale=pool_scale, w_mix_out=w_mix_out, norm_mem=norm_mem, norm_xa_pre=norm_xa_pre, norm_xa_post=norm_xa_post,
             w_xq=w_xq, w_xkv=w_xkv, w_xo=w_xo, norm_ffn_pre=norm_ffn_pre, norm_ffn_post=norm_ffn_post, w_up=w_up,
             conv_w=conv_w, conv_b=conv_b, w_down=w_down)
    m = dict(norm_mix_pre=m_norm_mix_pre, norm_mix_post=m_norm_mix_post, w_in=m_w_in, b_forget=m_b_forget, w_pool=m_w_pool,
             pool_scale=m_pool_scale, w_mix_out=m_w_mix_out, norm_mem=m_norm_mem, norm_xa_pre=m_norm_xa_pre,
             norm_xa_post=m_norm_xa_post, w_xq=m_w_xq, w_xkv=m_w_xkv, w_xo=m_w_xo, norm_ffn_pre=m_norm_ffn_pre,
             norm_ffn_post=m_norm_ffn_post, w_up=m_w_up, conv_w=m_conv_w, conv_b=m_conv_b, w_down=m_w_down)
    v = dict(norm_mix_pre=v_norm_mix_pre, norm_mix_post=v_norm_mix_post, w_in=v_w_in, b_forget=v_b_forget, w_pool=v_w_pool,
             pool_scale=v_pool_scale, w_mix_out=v_w_mix_out, norm_mem=v_norm_mem, norm_xa_pre=v_norm_xa_pre,
             norm_xa_post=v_norm_xa_post, w_xq=v_w_xq, w_xkv=v_w_xkv, w_xo=v_w_xo, norm_ffn_pre=v_norm_ffn_pre,
             norm_ffn_post=v_norm_ffn_post, w_up=v_w_up, conv_w=v_conv_w, conv_b=v_conv_b, w_down=v_w_down)
    chip = 2 * lax.axis_index("x") + lax.axis_index("y")

    core_id = lax.axis_index("c").astype(jnp.int32).reshape(1)
    chip_id = chip.astype(jnp.int32).reshape(1)

    shard2d = {n: w[n][0] for n in BIG}
    p = _local_params(w)
    comm = _StepComm(w, shard2d, conv_w, core_id, chip_id)
    grad_x, g, loss_cols = _local_step(x[0], mem[0], loss_target[0], p, comm)

    reduced_late = [_mesh_sum("mesh_sum_" + n, chip_id, r, own)
                    for n, r, own in zip(comm.late, comm.received_late, comm.partial_late)]
    names = comm.late + comm.early
    reduced = reduced_late + comm.reduced
    reduced_sibling = list(_swap_reduced(reduced_late).run("swap_reduced_late")) + comm.reduced_sibling
    grads = {}

    gw_pool = jnp.stack([g["w_pool_full"][64 * gi:64 * (gi + 1), 64 * gi:64 * (gi + 1)] for gi in range(4)])
    dcwb = g["cwb"]
    g_conv_w = dcwb[:, 0:3, :].transpose(1, 0, 2).reshape(3, 2 * D_FF)
    g_conv_b = dcwb[:, 3, :].reshape(2 * D_FF)
    small_g = dict(norm_mix_pre=g["norm_mix_pre"], norm_mix_post=g["norm_mix_post"], b_forget=g["bf_pad"][:, :HEADS],
                   w_pool=gw_pool, pool_scale=g["pool_scale"], norm_mem=g["norm_mem"], norm_xa_pre=g["norm_xa_pre"],
                   norm_xa_post=g["norm_xa_post"], norm_ffn_pre=g["norm_ffn_pre"], norm_ffn_post=g["norm_ffn_post"],
                   conv_b=g_conv_b)
    local_buf, offs = _pack([small_g[n] for n in SMALL] + [g_conv_w, loss_cols])

    delta, new_m, new_v = {}, {}, {}
    for n, g_mine, g_sibling in zip(names, reduced, reduced_sibling):
        gn, d, nm, nv = _adamw_halves("adamw_" + n, core_id, shard2d[n], g_mine, g_sibling, m[n][0], v[n][0])
        grads[n], delta[n], new_m[n], new_v[n] = gn[None], d[None], nm[None], nv[None]
    place = (2 * chip + lax.axis_index("c")).astype(jnp.int32).reshape(1)
    buf = _sum_devices(place, _gather_small(local_buf).run("gather_small")[0], local_buf)
    for n, off in zip(SMALL, offs):
        grads[n] = _unpack(buf, off, w[n])
    g_conv_w = _unpack(buf, offs[len(SMALL)], g_conv_w)
    grads["conv_w"] = lax.dynamic_slice_in_dim(g_conv_w, chip * (2 * D_FF // N_CHIPS), 2 * D_FF // N_CHIPS, axis=1).reshape(conv_w.shape)
    loss = jnp.sum(_unpack(buf, offs[len(SMALL) + 1], loss_cols))
    small_names = SMALL + ("conv_w",)
    packed = [_pack([d[n] for n in small_names])[0] for d in (w, grads, m, v)]
    offs = _pack([w[n] for n in small_names])[1]
    d, nm, nv = _adamw("adamw_small", *packed)
    for n, off in zip(small_names, offs):
        delta[n], new_m[n], new_v[n] = _unpack(d, off, w[n]), _unpack(nm, off, w[n]), _unpack(nv, off, w[n])

    return (loss, grad_x[None], *[grads[n] for n in ORDER], *[delta[n] for n in ORDER], *[new_m[n] for n in ORDER],
            *[new_v[n] for n in ORDER])
```

```python
import functools

import jax
import jax.numpy as jnp
import numpy as np
from jax import lax
from jax.experimental import pallas as pl
from jax.experimental.pallas import tpu as pltpu

F32 = jnp.float32
BF16 = jnp.bfloat16
MESH = pl.DeviceIdType.MESH
ANY = pl.BlockSpec(memory_space=pl.ANY)
VMEM_SPEC = pl.BlockSpec(memory_space=pltpu.VMEM)

S = 4096
D = 1024
MEM = 256
D_POOL = 256
HEADS = 12
DH = 64
D_FOX = HEADS * DH
D_IN = D_POOL + 3 * D_FOX + HEADS
F_OFF = D_POOL + 3 * D_FOX
Q_OFF, K_OFF, V_OFF = D_POOL, D_POOL + D_FOX, D_POOL + 2 * D_FOX
XA_HEADS = 4
XA_DH = 256
D_FF = 4096
EPS = 1e-6
N_CHIPS = 4
ADAM_LR, ADAM_B1, ADAM_B2, ADAM_EPS, ADAM_WD, ADAM_STEP = 0.001, 0.9, 0.999, 1e-08, 0.01, 10

LANES = 128
SUBLANES = 8
D_IN_PAD = 21 * LANES
TR = 512
TILE_BYTES = 2 * 1024 * 1024
NEG = -1e30
VMEM_LIMIT = 52 * 1024 * 1024

NN = (((1,), (0,)), ((), ()))
NT = (((1,), (1,)), ((), ()))
TN = (((0,), (0,)), ((), ()))


def _dot(a, b, dims=NN):
    return lax.dot_general(a, b, dims, preferred_element_type=F32)


def _params(sem):
    return pltpu.CompilerParams(dimension_semantics=sem, vmem_limit_bytes=VMEM_LIMIT)


def _split3(x):
    hi = x.astype(BF16)
    r = x - hi.astype(F32)
    mid = r.astype(BF16)
    lo = (r - mid.astype(F32)).astype(BF16)
    return hi, mid, lo


def _split3_f32(x):
    hi = x.astype(BF16).astype(F32)
    r = x - hi
    mid = r.astype(BF16).astype(F32)
    return hi, mid, r - mid


def _lane_iota(shape):
    return lax.broadcasted_iota(jnp.int32, shape, len(shape) - 1)


def _row_iota(shape):
    return lax.broadcasted_iota(jnp.int32, shape, len(shape) - 2)


def _mm(name, a, b, a_spec, b_spec, out_shape, out_spec, grid, dims, acc_shape, ex=None):
    nk = grid[2]
    if ex is not None:
        return _mm_hosting(name, a, b, a_spec, b_spec, out_shape, out_spec, grid, dims, ex)

    def body(a_ref, b_ref, o_ref, *scr):
        p = _dot(a_ref[...], b_ref[...], dims)
        if nk == 1:
            o_ref[...] = p.astype(o_ref.dtype)
        else:
            acc = scr[0]
            k = pl.program_id(2)

            @pl.when(k == 0)
            def _():
                acc[...] = p

            @pl.when(k > 0)
            def _():
                acc[...] += p

            @pl.when(k == nk - 1)
            def _():
                o_ref[...] = acc[...].astype(o_ref.dtype)

    return pl.pallas_call(
        body, name=name, grid=grid, in_specs=[a_spec, b_spec], out_specs=out_spec, out_shape=out_shape,
        scratch_shapes=[pltpu.VMEM(acc_shape, F32)] if nk > 1 else [],
        compiler_params=_params(("parallel", "parallel", "arbitrary")),
    )(a, b)


def _mm_hosting(name, a, b, a_spec, b_spec, out_shape, out_spec, grid, dims, ex):
    assert grid[2] == 1
    n = len(ex.ins)

    def body(*refs):
        i, j = pl.program_id(0), pl.program_id(1)
        first = (i == 0) & (j == 0)
        (a_ref, b_ref), (o_ref,), _, begin, end = _hosted(
            ex, refs, 2, 1, first, first, (i == grid[0] - 1) & (j == grid[1] - 1))
        begin()
        o_ref[...] = _dot(a_ref[...], b_ref[...], dims).astype(o_ref.dtype)
        end()

    res = pl.pallas_call(
        body, name=name, grid=grid, in_specs=[a_spec, b_spec] + [ANY] * n, out_specs=[out_spec] + [ANY] * n,
        out_shape=[out_shape] + ex.out_shapes, scratch_shapes=ex.scratch(),
        compiler_params=_params(("arbitrary", "arbitrary", "arbitrary")),
    )(a, b, *ex.ins)
    return res[0], res[1:]


def _mm_nn(name, a, b, out_dtype, tm, tn):
    m, k = a.shape
    n = b.shape[1]
    return _mm(name, a, b, pl.BlockSpec((tm, k), lambda i, j, kk: (i, 0)), pl.BlockSpec((k, tn), lambda i, j, kk: (0, j)),
               jax.ShapeDtypeStruct((m, n), out_dtype), pl.BlockSpec((tm, tn), lambda i, j, kk: (i, j)),
               (m // tm, n // tn, 1), NN, (tm, tn))


def _mm_nt(name, a, b, out_dtype, tm, tn, ex=None):
    m, k = a.shape
    n = b.shape[0]
    return _mm(name, a, b, pl.BlockSpec((tm, k), lambda i, j, kk: (i, 0)), pl.BlockSpec((tn, k), lambda i, j, kk: (j, 0)),
               jax.ShapeDtypeStruct((m, n), out_dtype), pl.BlockSpec((tm, tn), lambda i, j, kk: (i, j)),
               (m // tm, n // tn, 1), NT, (tm, tn), ex)


def _mm_tn(name, a, b, tka, tn, ex=None):
    t, ka = a.shape
    n = b.shape[1]
    return _mm(name, a, b, pl.BlockSpec((t, tka), lambda i, j, kk: (0, i)), pl.BlockSpec((t, tn), lambda i, j, kk: (0, j)),
               jax.ShapeDtypeStruct((ka, n), F32), pl.BlockSpec((tka, tn), lambda i, j, kk: (i, j)),
               (ka // tka, n // tn, 1), TN, (tka, tn), ex)


def _d_h3(dhid, w_up, ex):
    tm = tn = 512
    shard = 2 * D_FF // N_CHIPS
    per_plane = D_FF // shard
    grid = (S // tm, D // tn)
    n = len(ex.ins)

    def body(*refs):
        i, j = pl.program_id(0), pl.program_id(1)
        first = (i == 0) & (j == 0)
        (a_ref, b_ref), (o_ref,), _, begin, end = _hosted(ex, refs, 2, 1, first, first, (i == grid[0] - 1) & (j == grid[1] - 1))
        begin()
        acc = None
        for k in range(N_CHIPS):
            cols = slice(shard * (k % per_plane), shard * (k % per_plane + 1))
            part = _dot(a_ref[k // per_plane, :, cols], b_ref[k], NT)
            acc = part if acc is None else acc + part
        o_ref[...] = acc
        end()

    res = pl.pallas_call(
        body, name="d_h3", grid=grid,
        in_specs=[pl.BlockSpec((2, tm, D_FF), lambda i, j: (0, i, 0)),
                  pl.BlockSpec((N_CHIPS, tn, shard), lambda i, j: (0, j, 0))] + [ANY] * n,
        out_specs=[pl.BlockSpec((tm, tn), lambda i, j: (i, j))] + [ANY] * n,
        out_shape=[jax.ShapeDtypeStruct((S, D), F32)] + ex.out_shapes, scratch_shapes=ex.scratch(),
        compiler_params=_params(("arbitrary", "arbitrary")),
    )(dhid, w_up, *ex.ins)
    return res[0], res[1:]


def _rms(x, g):
    r = lax.rsqrt(jnp.mean(x * x, axis=-1, keepdims=True) + EPS)
    return x * r * g


def _rms_bwd(x, g, dy):
    r = lax.rsqrt(jnp.mean(x * x, axis=-1, keepdims=True) + EPS)
    xh = x * r
    dxh = dy * g
    dx = r * (dxh - xh * jnp.mean(dxh * xh, axis=-1, keepdims=True))
    return dx, jnp.sum(dy * xh, axis=0, keepdims=True)


def _row_spec(tr, width):
    return pl.BlockSpec((tr, width), lambda i: (i, 0))


def _vec_spec(width):
    return pl.BlockSpec((1, width), lambda i: (0, 0))


def _norm_fwd(name, x, g, ex=None):
    rows, width = x.shape
    tr = min(TR, rows)
    steps = rows // tr
    hosted = ex if ex is not None else _no_exchange()
    n = len(hosted.ins)

    def body(*refs):
        i = pl.program_id(0)
        (x_ref, g_ref), (h_ref,), _, begin, end = _hosted(hosted, refs, 2, 1, i == 0, i == 0, i == steps - 1)
        begin()
        h_ref[...] = _rms(x_ref[...], g_ref[...]).astype(BF16)
        end()

    res = pl.pallas_call(
        body, name=name, grid=(steps,), in_specs=[_row_spec(tr, width), _vec_spec(width)] + [ANY] * n,
        out_specs=[_row_spec(tr, width)] + [ANY] * n,
        out_shape=[jax.ShapeDtypeStruct((rows, width), BF16)] + hosted.out_shapes, scratch_shapes=hosted.scratch(),
        compiler_params=_params(("arbitrary",)),
    )(x, g, *hosted.ins)
    return res[0] if ex is None else (res[0], res[1:])


def _proj_resid_norm(name, a, w, xp, g_post, g_pre, w_next=None):
    def body(a_ref, w_ref, xp_ref, gpost_ref, gpre_ref, *rest):
        y_ref, xn_ref, h_ref = rest[-3:] if w_next is None else rest[1:4]
        y = _dot(a_ref[...], w_ref[...])
        y_ref[...] = y
        xn = xp_ref[...] + _rms(y, gpost_ref[...])
        xn_ref[...] = xn
        h = _rms(xn, gpre_ref[...]).astype(BF16)
        h_ref[...] = h
        if w_next is not None:
            rest[4][...] = _dot(h, rest[0][...]).astype(BF16)

    mat = pl.BlockSpec((D, D), lambda i: (0, 0))
    more = [] if w_next is None else [w_next]
    return pl.pallas_call(
        body, name=name, grid=(S // TR,),
        in_specs=[_row_spec(TR, D), mat, _row_spec(TR, D), _vec_spec(D), _vec_spec(D)] + [mat] * len(more),
        out_specs=[_row_spec(TR, D)] * (3 + len(more)),
        out_shape=[jax.ShapeDtypeStruct((S, D), F32), jax.ShapeDtypeStruct((S, D), F32), jax.ShapeDtypeStruct((S, D), BF16)]
        + [jax.ShapeDtypeStruct((S, D), BF16)] * len(more),
        compiler_params=_params(("parallel",)),
    )(a, w, xp, g_post, g_pre, *more)


def _down_loss_bwd(act, w_down, x3, g_post, target):
    def body(a_ref, w_ref, x_ref, g_ref, t_ref, dres_ref, dy_ref, dg_ref, loss_ref):
        i = pl.program_id(0)

        @pl.when(i == 0)
        def _():
            dg_ref[...] = jnp.zeros_like(dg_ref)
            loss_ref[...] = jnp.zeros_like(loss_ref)

        y = _dot(a_ref[...], w_ref[...])
        g = g_ref[...]
        e = x_ref[...] + _rms(y, g) - t_ref[...]
        loss_ref[...] += jnp.sum(e * e, axis=0, keepdims=True) * (0.5 / D)
        dres = e * (1.0 / D)
        dres_ref[...] = dres
        dy, dg = _rms_bwd(y, g, dres)
        dy_ref[...] = dy.astype(BF16)
        dg_ref[...] += dg

    return pl.pallas_call(
        body, name="down_loss_bwd", grid=(S // TR,),
        in_specs=[_row_spec(TR, D_FF), pl.BlockSpec((D_FF, D), lambda i: (0, 0)), _row_spec(TR, D), _vec_spec(D),
                  _row_spec(TR, D)],
        out_specs=[_row_spec(TR, D), _row_spec(TR, D), _vec_spec(D), _vec_spec(D)],
        out_shape=[jax.ShapeDtypeStruct((S, D), F32), jax.ShapeDtypeStruct((S, D), BF16),
                   jax.ShapeDtypeStruct((1, D), F32), jax.ShapeDtypeStruct((1, D), F32)],
        compiler_params=_params(("arbitrary",)),
    )(act, w_down, x3, g_post, target)


def _mid_bwd(name, dres, xcur, g_pre, dh, yprev, g_post, w):
    def body(dres_ref, x_ref, gpre_ref, dh_ref, y_ref, gpost_ref, w_ref, dx_ref, dy_ref, da_ref, dgpre_ref, dgpost_ref):
        i = pl.program_id(0)

        @pl.when(i == 0)
        def _():
            dgpre_ref[...] = jnp.zeros_like(dgpre_ref)
            dgpost_ref[...] = jnp.zeros_like(dgpost_ref)

        dxn, dgpre = _rms_bwd(x_ref[...], gpre_ref[...], dh_ref[...])
        dx = dres_ref[...] + dxn
        dx_ref[...] = dx
        dy, dgpost = _rms_bwd(y_ref[...], gpost_ref[...], dx)
        dy = dy.astype(BF16)
        dy_ref[...] = dy
        da_ref[...] = _dot(dy, w_ref[...], NT).astype(BF16)
        dgpre_ref[...] += dgpre
        dgpost_ref[...] += dgpost

    return pl.pallas_call(
        body, name=name, grid=(S // TR,),
        in_specs=[_row_spec(TR, D), _row_spec(TR, D), _vec_spec(D), _row_spec(TR, D), _row_spec(TR, D), _vec_spec(D),
                  pl.BlockSpec((D, D), lambda i: (0, 0))],
        out_specs=[_row_spec(TR, D), _row_spec(TR, D), _row_spec(TR, D), _vec_spec(D), _vec_spec(D)],
        out_shape=[jax.ShapeDtypeStruct((S, D), F32), jax.ShapeDtypeStruct((S, D), BF16), jax.ShapeDtypeStruct((S, D), BF16),
                   jax.ShapeDtypeStruct((1, D), F32), jax.ShapeDtypeStruct((1, D), F32)],
        compiler_params=_params(("arbitrary",)),
    )(dres, xcur, g_pre, dh, yprev, g_post, w)


def _first_bwd(dres, x, g, dh):
    def body(dres_ref, x_ref, g_ref, dh_ref, dx_ref, dg_ref):
        i = pl.program_id(0)

        @pl.when(i == 0)
        def _():
            dg_ref[...] = jnp.zeros_like(dg_ref)

        dxn, dg = _rms_bwd(x_ref[...], g_ref[...], dh_ref[...])
        dx_ref[...] = dres_ref[...] + dxn
        dg_ref[...] += dg

    return pl.pallas_call(
        body, name="first_bwd", grid=(S // TR,),
        in_specs=[_row_spec(TR, D), _row_spec(TR, D), _vec_spec(D), _row_spec(TR, D)],
        out_specs=[_row_spec(TR, D), _vec_spec(D)],
        out_shape=[jax.ShapeDtypeStruct((S, D), F32), jax.ShapeDtypeStruct((1, D), F32)],
        compiler_params=_params(("arbitrary",)),
    )(dres, x, g, dh)


def _gain_bwd(name, x, g, dy):
    rows, width = x.shape

    def body(x_ref, g_ref, dy_ref, dg_ref):
        _, dg = _rms_bwd(x_ref[...], g_ref[...], dy_ref[...])
        dg_ref[...] = dg

    return pl.pallas_call(
        body, name=name, grid=(1,), in_specs=[_row_spec(rows, width), _vec_spec(width), _row_spec(rows, width)],
        out_specs=_vec_spec(width), out_shape=jax.ShapeDtypeStruct((1, width), F32),
        compiler_params=_params(("arbitrary",)),
    )(x, g, dy)


CUM_Q = DH
CUM_K = DH + 3
LSE_Q = DH + 6
BOTH_ONE = DH + 9
DEN_V = DH
DELTA = DH + 1
PREP_TR = 256
PIECE_LANES = 16
FOX_FWD_BLOCK = 1024
FOX_BWD_BLOCK = 512


def _at(lane_of_even_head, h):
    return (lane_of_even_head + DH * (h % 2)) % LANES


def _data_lanes(lane, h):
    return lane >= DH if h % 2 else lane < DH


def _pair_block(ref, off, h):
    base = ((off + DH * h) // LANES) * LANES
    return ref[:, base:base + LANES]


def _cumsum_rows(x, tri, carry):
    hi, mid, lo = _split3(x)
    return _dot(tri, hi) + _dot(tri, mid) + _dot(tri, lo) + carry


def _fox_prep(proj, bf_pad):
    tr = PREP_TR

    place_q = np.zeros((LANES, HEADS * LANES), np.float32)
    place_k = np.zeros((LANES, HEADS * LANES), np.float32)
    for h in range(HEADS):
        for piece in range(3):
            place_q[PIECE_LANES * piece + h, LANES * h + _at(CUM_Q, h) + piece] = 1.0
            place_k[PIECE_LANES * piece + h, LANES * h + _at(CUM_K, h) + piece] = -1.0

    def body(proj_ref, bf_ref, pq_ref, pk_ref, qa_ref, ka_ref, va_ref, carry_ref):
        i = pl.program_id(0)

        @pl.when(i == 0)
        def _():
            carry_ref[...] = jnp.zeros_like(carry_ref)

        lane = _lane_iota((tr, LANES))
        z = proj_ref[:, F_OFF:F_OFF + LANES] + bf_ref[...]
        log_f = jnp.minimum(z, 0.0) - jnp.log(1.0 + jnp.exp(-jnp.abs(z)))
        log_f = jnp.where(lane < HEADS, log_f, 0.0)
        tri = jnp.where(_row_iota((tr, tr)) >= _lane_iota((tr, tr)), 1.0, 0.0).astype(BF16)
        cum = _cumsum_rows(log_f, tri, carry_ref[0:1, :])
        carry_ref[0:1, :] = cum[tr - 1:tr, :]
        c_hi, c_mid, c_lo = _split3_f32(cum)
        pieces = (c_hi + pltpu.roll(c_mid, PIECE_LANES, 1) + pltpu.roll(c_lo, 2 * PIECE_LANES, 1)).astype(BF16)
        cum_q = _dot(pieces, pq_ref[...])
        cum_k = _dot(pieces, pk_ref[...])

        def between(first, h):
            return (lane >= _at(first, h)) & (lane < _at(first, h) + 3)

        ones_q = [jnp.where(between(CUM_K, h) | (lane == _at(BOTH_ONE, h)), 1.0, 0.0) for h in range(2)]
        ones_k = [jnp.where(between(CUM_Q, h) | between(LSE_Q, h) | (lane == _at(BOTH_ONE, h)), 1.0, 0.0) for h in range(2)]
        aug_v = [jnp.where(lane == _at(DEN_V, h), 1.0, jnp.where(between(DELTA, h), -1.0, 0.0)) for h in range(2)]
        for h in range(HEADS):
            mine = slice(LANES * h, LANES * (h + 1))
            data = _data_lanes(lane, h)
            qa_ref[h] = jnp.where(data, _pair_block(proj_ref, Q_OFF, h) * (DH ** -0.5), cum_q[:, mine] + ones_q[h % 2]).astype(BF16)
            ka_ref[h] = jnp.where(data, _pair_block(proj_ref, K_OFF, h), cum_k[:, mine] + ones_k[h % 2]).astype(BF16)
            va_ref[h] = jnp.where(data, _pair_block(proj_ref, V_OFF, h), aug_v[h % 2]).astype(BF16)

    head_spec = pl.BlockSpec((HEADS, tr, LANES), lambda i: (0, i, 0))
    head_shape = jax.ShapeDtypeStruct((HEADS, S, LANES), BF16)
    place_spec = pl.BlockSpec(place_q.shape, lambda i: (0, 0))
    return pl.pallas_call(
        body, name="fox_prep", grid=(S // tr,), in_specs=[_row_spec(tr, D_IN_PAD), _vec_spec(LANES), place_spec, place_spec],
        out_specs=[head_spec] * 3, out_shape=[head_shape] * 3, scratch_shapes=[pltpu.VMEM((SUBLANES, LANES), F32)],
        compiler_params=_params(("arbitrary",)),
    )(proj, bf_pad, jnp.asarray(place_q, BF16), jnp.asarray(place_k, BF16))


def _hosted(ex, refs, n_blocked_in, n_blocked_out, first, forward_at, last):
    n = len(ex.ins)
    own_in = refs[:n_blocked_in]
    ex_in = refs[n_blocked_in:n_blocked_in + n]
    own_out = refs[n_blocked_in + n:n_blocked_in + n + n_blocked_out]
    ex_out = refs[n_blocked_in + n + n_blocked_out:n_blocked_in + 2 * n + n_blocked_out]
    rest = refs[n_blocked_in + 2 * n + n_blocked_out:]
    args = (ex_in, ex_out, rest[-2], rest[-1])

    def begin():
        @pl.when(first)
        def _():
            ex.start(*args)

        @pl.when(forward_at)
        def _():
            ex.forward(*args)

    def end():
        @pl.when(last)
        def _():
            ex.finish(*args)

    return own_in, own_out, rest[:-2], begin, end


def _fox_fwd(qa, ka, va, ex):
    BQ = BK = FOX_FWD_BLOCK
    nq = S // BQ
    n_pairs = HEADS // 2

    def body(*refs):
        p_id, i = pl.program_id(0), pl.program_id(1)
        (qa_ref, ka_ref, va_ref), (y_ref, qab_ref), (m_scr, acc_scr), begin, end = _hosted(
            ex, refs, 3, 2, (p_id == 0) & (i == 0), (p_id == n_pairs - 1) & (i == 0), (p_id == n_pairs - 1) & (i == nq - 1))
        begin()
        lane = _lane_iota((BQ, LANES))
        causal = _row_iota((BQ, BK)) >= _lane_iota((BQ, BK))
        m_scr[...] = jnp.full_like(m_scr, NEG)
        acc_scr[...] = jnp.zeros_like(acc_scr)

        def step(j, masked):
            rows = pl.ds(pl.multiple_of(j * BK, BK), BK)
            for hh in range(2):
                s = _dot(qa_ref[hh], ka_ref[hh, rows, :], NT)
                if masked:
                    s = jnp.where(causal, s, NEG)
                m_prev = m_scr[hh]
                m_new = jnp.maximum(m_prev, jnp.max(s, axis=1, keepdims=True))
                p = jnp.exp(s - jnp.tile(m_new, (1, BK // LANES)))
                acc_scr[hh] = jnp.exp(m_prev - m_new) * acc_scr[hh] + _dot(p.astype(BF16), va_ref[hh, rows, :])
                m_scr[hh] = m_new

        def full_step(j, carry):
            step(j, False)
            return carry

        lax.fori_loop(0, i, full_step, 0)
        step(i, True)
        outs = []
        for hh in range(2):
            acc = acc_scr[hh]
            den_lane, lse_lane = _at(DEN_V, hh), _at(LSE_Q, hh)
            den = jnp.broadcast_to(acc[:, den_lane:den_lane + 1], (BQ, LANES))
            outs.append(acc * (1.0 / den))
            n_hi, n_mid, n_lo = _split3(-(m_scr[hh] + jnp.log(den)))
            qab_ref[hh] = jnp.where(lane == lse_lane, n_hi,
                                    jnp.where(lane == lse_lane + 1, n_mid, jnp.where(lane == lse_lane + 2, n_lo, qa_ref[hh])))
        y_ref[...] = jnp.where(lane < DH, outs[0], outs[1]).astype(BF16)
        end()

    pair_rows = pl.BlockSpec((2, BQ, LANES), lambda p, i: (p, i, 0))
    pair_all = pl.BlockSpec((2, S, LANES), lambda p, i: (p, 0, 0))
    n = len(ex.ins)
    res = pl.pallas_call(
        body, name="fox_fwd", grid=(n_pairs, nq), in_specs=[pair_rows, pair_all, pair_all] + [ANY] * n,
        out_specs=[pl.BlockSpec((BQ, LANES), lambda p, i: (i, D_POOL // LANES + p)), pair_rows] + [ANY] * n,
        out_shape=[jax.ShapeDtypeStruct((S, D), BF16), jax.ShapeDtypeStruct((HEADS, S, LANES), BF16)] + ex.out_shapes,
        scratch_shapes=[pltpu.VMEM((2, BQ, LANES), F32), pltpu.VMEM((2, BQ, LANES), F32)] + ex.scratch(),
        compiler_params=_params(("arbitrary", "arbitrary")),
    )(qa, ka, va, *ex.ins)
    return res[0], res[1], res[2:]


def _bwd_xa_mix(dqx, w_xq, dres, x2, g_pre, y1, g_post, w_mix_out, ycat, ex):
    steps = S // TR
    n = len(ex.ins)

    def body(*refs):
        i = pl.program_id(0)
        ((dq_ref, wq_ref, dres_ref, x_ref, gpre_ref, y_ref, gpost_ref, wm_ref, ycat_ref),
         (dx_ref, dy_ref, dgpre_ref, dgpost_ref, dp_ref, doa_ref), _, begin, end) = _hosted(
            ex, refs, 9, 6, i == 0, i == 0, i == steps - 1)
        begin()

        @pl.when(i == 0)
        def _():
            dgpre_ref[...] = jnp.zeros_like(dgpre_ref)
            dgpost_ref[...] = jnp.zeros_like(dgpost_ref)

        dxn, dgpre = _rms_bwd(x_ref[...], gpre_ref[...], _dot(dq_ref[...], wq_ref[...], NT))
        dx = dres_ref[...] + dxn
        dx_ref[...] = dx
        dy, dgpost = _rms_bwd(y_ref[...], gpost_ref[...], dx)
        dy = dy.astype(BF16)
        dy_ref[...] = dy
        dgpre_ref[...] += dgpre
        dgpost_ref[...] += dgpost

        d = _dot(dy, wm_ref[...], NT)
        dp_ref[...] = d[:, :D_POOL]
        lane = _lane_iota((TR, LANES))
        low = lane < DH
        for p in range(HEADS // 2):
            cols = slice(D_POOL + LANES * p, D_POOL + LANES * (p + 1))
            do = d[:, cols]
            prod = do * ycat_ref[:, cols].astype(F32)
            deltas = (jnp.sum(jnp.where(low, prod, 0.0), axis=1, keepdims=True),
                      jnp.sum(jnp.where(low, 0.0, prod), axis=1, keepdims=True))
            for hh in range(2):
                d_hi, d_mid, d_lo = _split3_f32(deltas[hh])
                dl = _at(DELTA, hh)
                aug = jnp.where(lane == dl, d_hi, jnp.where(lane == dl + 1, d_mid, jnp.where(lane == dl + 2, d_lo, 0.0)))
                doa_ref[2 * p + hh] = jnp.where(_data_lanes(lane, hh), do, aug).astype(BF16)
        end()

    mat = pl.BlockSpec((D, D), lambda i: (0, 0))
    res = pl.pallas_call(
        body, name="bwd_xa_mix", grid=(steps,),
        in_specs=[_row_spec(TR, D), mat, _row_spec(TR, D), _row_spec(TR, D), _vec_spec(D), _row_spec(TR, D), _vec_spec(D), mat,
                  _row_spec(TR, D)] + [ANY] * n,
        out_specs=[_row_spec(TR, D), _row_spec(TR, D), _vec_spec(D), _vec_spec(D), _row_spec(TR, D_POOL),
                   pl.BlockSpec((HEADS, TR, LANES), lambda i: (0, i, 0))] + [ANY] * n,
        out_shape=[jax.ShapeDtypeStruct((S, D), F32), jax.ShapeDtypeStruct((S, D), BF16), jax.ShapeDtypeStruct((1, D), F32),
                   jax.ShapeDtypeStruct((1, D), F32), jax.ShapeDtypeStruct((S, D_POOL), F32),
                   jax.ShapeDtypeStruct((HEADS, S, LANES), BF16)] + ex.out_shapes,
        scratch_shapes=ex.scratch(), compiler_params=_params(("arbitrary",)),
    )(dqx, w_xq, dres, x2, g_pre, y1, g_post, w_mix_out, ycat, *ex.ins)
    return res[:6], res[6:]


def _fox_bwd(qab, doa, ka, va, ex):
    BQ = BK = FOX_BWD_BLOCK
    nk = S // BK
    n_pairs = HEADS // 2

    def body(*refs):
        p_id, j = pl.program_id(0), pl.program_id(1)
        (qab_ref, doa_ref, ka_ref, va_ref), (dqa_ref, dka_ref, dva_ref), _, begin, end = _hosted(
            ex, refs, 4, 3, (p_id == 0) & (j == 0), (p_id == n_pairs - 1) & (j == 0), (p_id == n_pairs - 1) & (j == nk - 1))
        begin()

        @pl.when(j == 0)
        def _():
            dqa_ref[...] = jnp.zeros_like(dqa_ref)

        causal = _row_iota((BQ, BK)) >= _lane_iota((BQ, BK))
        dka_ref[...] = jnp.zeros_like(dka_ref)
        dva_ref[...] = jnp.zeros_like(dva_ref)

        def step(i, masked):
            rows = pl.ds(pl.multiple_of(i * BQ, BQ), BQ)
            for hh in range(2):
                kb = ka_ref[hh]
                q = qab_ref[hh, rows, :]
                do = doa_ref[hh, rows, :]
                s = _dot(q, kb, NT)
                if masked:
                    s = jnp.where(causal, s, NEG)
                p = jnp.exp(s)
                ds = p * _dot(do, va_ref[hh], NT)
                pb = p.astype(BF16)
                dsb = ds.astype(BF16)
                dva_ref[hh] += _dot(pb, do, TN)
                dka_ref[hh] += _dot(dsb, q, TN)
                dqa_ref[hh, rows, :] += _dot(dsb, kb)

        def full_step(i, carry):
            step(i, False)
            return carry

        step(j, True)
        lax.fori_loop(j + 1, nk, full_step, 0)
        end()

    pair_all = pl.BlockSpec((2, S, LANES), lambda p, j: (p, 0, 0))
    pair_rows = pl.BlockSpec((2, BK, LANES), lambda p, j: (p, j, 0))
    shape = jax.ShapeDtypeStruct((HEADS, S, LANES), F32)
    n = len(ex.ins)
    res = pl.pallas_call(
        body, name="fox_bwd", grid=(n_pairs, nk), in_specs=[pair_all, pair_all, pair_rows, pair_rows] + [ANY] * n,
        out_specs=[pair_all, pair_rows, pair_rows] + [ANY] * n, out_shape=[shape] * 3 + ex.out_shapes,
        scratch_shapes=ex.scratch(), compiler_params=_params(("arbitrary", "arbitrary")),
    )(qab, doa, ka, va, *ex.ins)
    return res[0], res[1], res[2], res[3:]


def _fox_bwd_post(dqa, dka, dva, du, proj, bf_pad):
    tr = PREP_TR
    nt = S // tr

    pick = np.zeros((HEADS * LANES, LANES), np.float32)
    for h in range(HEADS):
        pick[LANES * h + _at(BOTH_ONE, h), h] = 1.0

    def body(dqa_ref, dka_ref, dva_ref, du_ref, z_ref, bf_ref, pick_ref, dp_ref, dbf_ref, carry_ref):
        i = pl.program_id(0)

        @pl.when(i == 0)
        def _():
            carry_ref[...] = jnp.zeros_like(carry_ref)
            dbf_ref[...] = jnp.zeros_like(dbf_ref)

        lane = _lane_iota((tr, LANES))
        diff = jnp.concatenate([dqa_ref[h] - dka_ref[h] for h in range(HEADS)], axis=1)
        hi = diff.astype(BF16)
        dcum = _dot(hi, pick_ref[...]) + _dot((diff - hi.astype(F32)).astype(BF16), pick_ref[...])
        tri =jnp.where(_lane_iota((tr, tr)) >= _row_iota((tr, tr)), 1.0, 0.0).astype(BF16)
        dlog_f = _cumsum_rows(dcum, tri, carry_ref[0:1, :])
        carry_ref[0:1, :] = dlog_f[0:1, :]
        z = z_ref[...] + bf_ref[...]
        df = jnp.where(lane < HEADS, dlog_f / (1.0 + jnp.exp(z)), 0.0)
        dbf_ref[...] += jnp.sum(df, axis=0, keepdims=True)

        dp_ref[:, 0:D_POOL] = du_ref[...].astype(BF16)
        low = lane < DH
        for ref, off, scale in ((dqa_ref, Q_OFF, DH ** -0.5), (dka_ref, K_OFF, 1.0), (dva_ref, V_OFF, 1.0)):
            for p in range(HEADS // 2):
                blk = jnp.where(low, ref[2 * p], ref[2 * p + 1])
                dp_ref[:, off + LANES * p:off + LANES * (p + 1)] = (blk * scale).astype(BF16)
        dp_ref[:, F_OFF:F_OFF + LANES] = df.astype(BF16)

    head_spec = pl.BlockSpec((HEADS, tr, LANES), lambda i: (0, nt - 1 - i, 0))
    return pl.pallas_call(
        body, name="fox_bwd_post", grid=(nt,),
        in_specs=[head_spec, head_spec, head_spec, pl.BlockSpec((tr, D_POOL), lambda i: (nt - 1 - i, 0)),
                  pl.BlockSpec((tr, LANES), lambda i: (nt - 1 - i, F_OFF // LANES)), _vec_spec(LANES),
                  pl.BlockSpec(pick.shape, lambda i: (0, 0))],
        out_specs=[pl.BlockSpec((tr, D_IN_PAD), lambda i: (nt - 1 - i, 0)), _vec_spec(LANES)],
        out_shape=[jax.ShapeDtypeStruct((S, D_IN_PAD), BF16), jax.ShapeDtypeStruct((1, LANES), F32)],
        scratch_shapes=[pltpu.VMEM((SUBLANES, LANES), F32)],
        compiler_params=_params(("arbitrary",)),
    )(dqa, dka, dva, du, proj, bf_pad, jnp.asarray(pick, BF16))


POOL_HALO = 16


def _by_group(lane, a2, a4, a8, a16):
    return jnp.where(lane < 64, a2, jnp.where(lane < 128, a4, jnp.where(lane < 192, a8, a16)))


def _window_count(lane, t):
    return jnp.minimum(t + 1, _by_group(lane, 2, 4, 8, 16)).astype(F32)


def _pool_diff(u, halo, first, tile):
    n = TR + POOL_HALO
    ext = jnp.concatenate([jnp.where(first, 0.0, halo), u], axis=0)
    s2 = ext + pltpu.roll(ext, 1, 0)
    s4 = s2 + pltpu.roll(s2, 2, 0)
    s8 = s4 + pltpu.roll(s4, 4, 0)
    s16 = s8 + pltpu.roll(s8, 8, 0)
    lane = _lane_iota((n, D_POOL))
    win = _by_group(lane, s2, s4, s8, s16)[POOL_HALO:]
    lane = _lane_iota((TR, D_POOL))
    t = tile * TR + _row_iota((TR, D_POOL))
    return win / _window_count(lane, t) - u


def _prev_halo(rows, width, col):
    per = TR // rows
    return pl.BlockSpec((rows, width), lambda i: (jnp.maximum(i * per - 1, 0), col))


def _next_halo(rows, width, col):
    per = TR // rows
    return pl.BlockSpec((rows, width), lambda i: (jnp.minimum((i + 1) * per, S // rows - 1), col))


def _pool_fwd(proj, w_bd, ps, ycat):
    def body(u_ref, halo_ref, w_ref, ps_ref, ycat_ref, y_ref):
        i = pl.program_id(0)
        diff = _pool_diff(u_ref[...], halo_ref[...], i == 0, i)
        y_ref[...] = (_dot(diff.astype(BF16), w_ref[...]) * ps_ref[...]).astype(BF16)

    return pl.pallas_call(
        body, name="pool_fwd", grid=(S // TR,),
        in_specs=[_row_spec(TR, D_POOL), _prev_halo(POOL_HALO, D_POOL, 0),
                  pl.BlockSpec((D_POOL, D_POOL), lambda i: (0, 0)), _vec_spec(D_POOL), ANY],
        out_specs=_row_spec(TR, D_POOL), out_shape=jax.ShapeDtypeStruct((S, D), BF16), input_output_aliases={4: 0},
        compiler_params=_params(("parallel",)),
    )(proj, proj, w_bd, ps, ycat)


def _pool_bwd(proj, dycat, w_bd, w_bd_t, ps):
    nt = S // TR
    n = TR + POOL_HALO

    def body(u_ref, halo_ref, dy_ref, dyn_ref, w_ref, wt_ref, ps_ref, du_ref, dw_ref, dps_ref):
        i = pl.program_id(0)

        @pl.when(i == 0)
        def _():
            dw_ref[...] = jnp.zeros_like(dw_ref)
            dps_ref[...] = jnp.zeros_like(dps_ref)

        diff = _pool_diff(u_ref[...], halo_ref[...], i == 0, i).astype(BF16)
        dy = dy_ref[...]
        dps_ref[...] += jnp.sum(dy * _dot(diff, w_ref[...]), axis=0, keepdims=True)
        dy_ext = jnp.concatenate([dy, jnp.where(i == nt - 1, 0.0, dyn_ref[...])], axis=0)
        dmixed = (dy_ext * ps_ref[...]).astype(BF16)
        ddiff = _dot(dmixed, wt_ref[...])
        dw_ref[...] += _dot(diff, dmixed[:TR], TN)
        lane = _lane_iota((n, D_POOL))
        t = i * TR + _row_iota((n, D_POOL))
        e = ddiff / _window_count(lane, t)
        f2 = e + pltpu.roll(e, n - 1, 0)
        f4 = f2 + pltpu.roll(f2, n - 2, 0)
        f8 = f4 + pltpu.roll(f4, n - 4, 0)
        f16 = f8 + pltpu.roll(f8, n - 8, 0)
        du_ref[...] = _by_group(lane, f2, f4, f8, f16)[:TR] - ddiff[:TR]

    mat = pl.BlockSpec((D_POOL, D_POOL), lambda i: (0, 0))
    return pl.pallas_call(
        body, name="pool_bwd", grid=(nt,),
        in_specs=[_row_spec(TR, D_POOL), _prev_halo(POOL_HALO, D_POOL, 0), _row_spec(TR, D_POOL),
                  _next_halo(POOL_HALO, D_POOL, 0), mat, mat, _vec_spec(D_POOL)],
        out_specs=[_row_spec(TR, D_POOL), mat, _vec_spec(D_POOL)],
        out_shape=[jax.ShapeDtypeStruct((S, D_POOL), F32), jax.ShapeDtypeStruct((D_POOL, D_POOL), F32),
                   jax.ShapeDtypeStruct((1, D_POOL), F32)],
        compiler_params=_params(("arbitrary",)),
    )(proj, proj, dycat, dycat, w_bd, w_bd_t, ps)


def _xa_probs(q, k):
    s = _dot(q, k, NT) * (XA_DH ** -0.5)
    e = jnp.exp(s - jnp.max(s, axis=-1, keepdims=True))
    return e * (1.0 / jnp.sum(e, axis=-1, keepdims=True))


def _xattn_fwd(qx, kv):
    def body(q_ref, kv_ref, o_ref):
        for h in range(XA_HEADS):
            cols = slice(XA_DH * h, XA_DH * (h + 1))
            vcols = slice(D + XA_DH * h, D + XA_DH * (h + 1))
            p = _xa_probs(q_ref[:, cols], kv_ref[:, cols])
            o_ref[:, cols] = _dot(p.astype(BF16), kv_ref[:, vcols]).astype(BF16)

    return pl.pallas_call(
        body, name="xattn_fwd", grid=(S // TR,),
        in_specs=[_row_spec(TR, D), pl.BlockSpec((MEM, 2 * D), lambda i: (0, 0))],
        out_specs=_row_spec(TR, D), out_shape=jax.ShapeDtypeStruct((S, D), BF16),
        compiler_params=_params(("parallel",)),
    )(qx, kv)


def _xattn_bwd(qx, kv, dxo):
    def body(q_ref, kv_ref, do_ref, dq_ref, dkv_ref):
        i = pl.program_id(0)

        @pl.when(i == 0)
        def _():
            dkv_ref[...] = jnp.zeros_like(dkv_ref)

        for h in range(XA_HEADS):
            cols = slice(XA_DH * h, XA_DH * (h + 1))
            vcols = slice(D + XA_DH * h, D + XA_DH * (h + 1))
            q = q_ref[:, cols]
            k = kv_ref[:, cols]
            do = do_ref[:, cols]
            p = _xa_probs(q, k)
            dkv_ref[:, vcols] += _dot(p.astype(BF16), do, TN)
            dp = _dot(do, kv_ref[:, vcols], NT)
            ds = (p * (dp - jnp.sum(p * dp, axis=-1, keepdims=True)) * (XA_DH ** -0.5)).astype(BF16)
            dq_ref[:, cols] = _dot(ds, k).astype(BF16)
            dkv_ref[:, cols] += _dot(ds, q, TN)

    kv_spec = pl.BlockSpec((MEM, 2 * D), lambda i: (0, 0))
    return pl.pallas_call(
        body, name="xattn_bwd", grid=(S // TR,), in_specs=[_row_spec(TR, D), kv_spec, _row_spec(TR, D)],
        out_specs=[_row_spec(TR, D), kv_spec],
        out_shape=[jax.ShapeDtypeStruct((S, D), BF16), jax.ShapeDtypeStruct((MEM, 2 * D), F32)],
        compiler_params=_params(("arbitrary",)),
    )(qx, kv, dxo)


CONV_HALO = SUBLANES
TC = 512
GELU_K = 0.7978845608028654
GELU_C = 0.044715


def _conv3(ext, w, rows):
    h0 = ext[CONV_HALO:CONV_HALO + rows]
    h1 = pltpu.roll(ext, 1, 0)[CONV_HALO:CONV_HALO + rows]
    h2 = pltpu.roll(ext, 2, 0)[CONV_HALO:CONV_HALO + rows]
    return w[2:3] * h0 + w[1:2] * h1 + w[0:1] * h2 + w[3:4], (h2, h1, h0)


def _conv_specs():
    main = pl.BlockSpec((2, TR, TC), lambda j, i: (0, i, j))
    per = TR // CONV_HALO
    prev = pl.BlockSpec((2, CONV_HALO, TC), lambda j, i: (0, jnp.maximum(i * per - 1, 0), j))
    nxt = pl.BlockSpec((2, CONV_HALO, TC), lambda j, i: (0, jnp.minimum((i + 1) * per, S // CONV_HALO - 1), j))
    par = pl.BlockSpec((2, SUBLANES, TC), lambda j, i: (0, 0, j))
    return main, prev, nxt, par


def _convgate_fwd(hid, cwb):
    def body(h_ref, hp_ref, w_ref, act_ref):
        i = pl.program_id(1)
        c = []
        for g in range(2):
            ext = jnp.concatenate([jnp.where(i == 0, 0.0, hp_ref[g]), h_ref[g]], axis=0)
            c.append(_conv3(ext, w_ref[g], TR)[0])
        gate, up = c
        act_ref[...] = (jax.nn.gelu(gate, approximate=True) * up).astype(BF16)

    main, prev, _, par = _conv_specs()
    return pl.pallas_call(
        body, name="convgate_fwd", grid=(D_FF // TC, S // TR), in_specs=[main, prev, par],
        out_specs=pl.BlockSpec((TR, TC), lambda j, i: (i, j)), out_shape=jax.ShapeDtypeStruct((S, D_FF), BF16),
        compiler_params=_params(("parallel", "parallel")),
    )(hid, hid, cwb)


def _convgate_bwd(hid, dact, cwb):
    nr = S // TR
    n = TR + CONV_HALO

    def body(h_ref, hp_ref, hn_ref, da_ref, dan_ref, w_ref, dh_ref, dw_ref):
        i = pl.program_id(1)

        @pl.when(i == 0)
        def _():
            dw_ref[...] = jnp.zeros_like(dw_ref)

        da = jnp.concatenate([da_ref[...], jnp.where(i == nr - 1, 0.0, dan_ref[...])], axis=0)
        c, taps = [], []
        for g in range(2):
            ext = jnp.concatenate([jnp.where(i == 0, 0.0, hp_ref[g]), h_ref[g], hn_ref[g]], axis=0)
            cg, tg = _conv3(ext, w_ref[g], n)
            c.append(cg)
            taps.append(tg)
        gate, up = c
        th = jnp.tanh(GELU_K * (gate + GELU_C * gate * gate * gate))
        gelu = 0.5 * gate * (1.0 + th)
        dgelu = 0.5 * (1.0 + th) + 0.5 * gate * (1.0 - th * th) * GELU_K * (1.0 + 3.0 * GELU_C * gate * gate)
        for g, dc in enumerate((da * up * dgelu, da * gelu)):
            w = w_ref[g]
            dh = w[2:3] * dc[:TR] + w[1:2] * pltpu.roll(dc, n - 1, 0)[:TR] + w[0:1] * pltpu.roll(dc, n - 2, 0)[:TR]
            dh_ref[g] = dh.astype(BF16)
            dcm = dc[:TR]
            for r in range(3):
                dw_ref[g, r:r + 1, :] += jnp.sum(dcm * taps[g][r][:TR], axis=0, keepdims=True)
            dw_ref[g, 3:4, :] += jnp.sum(dcm, axis=0, keepdims=True)

    main, prev, nxt, par = _conv_specs()
    per = TR // CONV_HALO
    return pl.pallas_call(
        body, name="convgate_bwd", grid=(D_FF // TC, nr),
        in_specs=[main, prev, nxt, pl.BlockSpec((TR, TC), lambda j, i: (i, j)),
                  pl.BlockSpec((CONV_HALO, TC), lambda j, i: (jnp.minimum((i + 1) * per, S // CONV_HALO - 1), j)), par],
        out_specs=[main, par],
        out_shape=[jax.ShapeDtypeStruct((2, S, D_FF), BF16), jax.ShapeDtypeStruct((2, SUBLANES, D_FF), F32)],
        compiler_params=_params(("parallel", "arbitrary")),
    )(hid, hid, hid, dact, dact, cwb)


def _adam_update(w, g, m, v):
    m = ADAM_B1 * m + (1.0 - ADAM_B1) * g
    v = ADAM_B2 * v + (1.0 - ADAM_B2) * (g * g)
    m_hat = m / (1.0 - ADAM_B1 ** ADAM_STEP)
    v_hat = v / (1.0 - ADAM_B2 ** ADAM_STEP)
    return -ADAM_LR * (m_hat / (jnp.sqrt(v_hat) + ADAM_EPS) + ADAM_WD * w), m, v


def _row_tile(rows, cols, itemsize=4, target=TILE_BYTES):
    tr = SUBLANES
    while rows % (2 * tr) == 0 and 2 * tr * cols * itemsize <= target:
        tr *= 2
    assert rows % tr == 0, (rows, tr)
    return tr


def _adamw(name, w, g, m, v):
    rows, cols = w.shape
    tr = rows if rows * cols * 4 <= TILE_BYTES // 2 else _row_tile(rows, cols, target=TILE_BYTES // 2)

    def body(w_ref, g_ref, m_ref, v_ref, d_ref, nm_ref, nv_ref):
        d_ref[...], nm_ref[...], nv_ref[...] = _adam_update(w_ref[...], g_ref[...], m_ref[...], v_ref[...])

    spec = _row_spec(tr, cols)
    shape = jax.ShapeDtypeStruct((rows, cols), F32)
    return pl.pallas_call(
        body, name=name, grid=(rows // tr,), in_specs=[spec] * 4, out_specs=[spec] * 3, out_shape=[shape] * 3,
        compiler_params=_params(("parallel",)),
    )(w, g, m, v)


def _adamw_halves(name, core, w, g_mine, g_sibling, m, v):
    rows, cols = w.shape
    half = rows // 2
    tr = _row_tile(half, cols, target=TILE_BYTES // 2)
    per = half // tr

    def body(core_ref, w_ref, gm_ref, gs_ref, m_ref, v_ref, g_ref, d_ref, nm_ref, nv_ref):
        g = jnp.where(pl.program_id(0) // per == core_ref[0], gm_ref[...], gs_ref[...])
        g_ref[...] = g
        d_ref[...], nm_ref[...], nv_ref[...] = _adam_update(w_ref[...], g, m_ref[...], v_ref[...])

    spec = pl.BlockSpec((tr, cols), lambda i, core_ref: (i, 0))
    half_spec = pl.BlockSpec((tr, cols), lambda i, core_ref: (i % per, 0))
    shape = jax.ShapeDtypeStruct((rows, cols), F32)
    return pl.pallas_call(
        body, name=name, out_shape=[shape] * 4,
        grid_spec=pltpu.PrefetchScalarGridSpec(
            num_scalar_prefetch=1, grid=(rows // tr,), in_specs=[spec, half_spec, half_spec, spec, spec], out_specs=[spec] * 4),
        compiler_params=_params(("parallel",)),
    )(core, w, g_mine, g_sibling, m, v)


def _chip_sum(name, core, g, other):
    _, _, half, cols = g.shape
    tr = _row_tile(half, cols)

    def body(core_ref, g_ref, o_ref, p_ref):
        p_ref[...] = (g_ref[...] + o_ref[...]).astype(BF16)

    spec = pl.BlockSpec((None, tr, cols), lambda j, i, core_ref: (j, i, 0))
    return pl.pallas_call(
        body, name=name, out_shape=jax.ShapeDtypeStruct((N_CHIPS, half, cols), BF16),
        grid_spec=pltpu.PrefetchScalarGridSpec(
            num_scalar_prefetch=1, grid=(N_CHIPS, half // tr),
            in_specs=[pl.BlockSpec((None, None, tr, cols), lambda j, i, core_ref: (j, core_ref[0], i, 0)), spec],
            out_specs=spec),
        compiler_params=_params(("parallel", "parallel")),
    )(core, g, other)


def _mesh_sum(name, chip, received, own):
    _, half, cols = received.shape
    tr = _row_tile(half, cols, itemsize=2 * N_CHIPS)

    def body(chip_ref, r_ref, own_ref, o_ref):
        acc = None
        for j in range(N_CHIPS):
            term = jnp.where(chip_ref[0] == j, own_ref[...], r_ref[j]).astype(F32)
            acc = term if acc is None else acc + term
        o_ref[...] = acc

    return pl.pallas_call(
        body, name=name, out_shape=jax.ShapeDtypeStruct((half, cols), F32),
        grid_spec=pltpu.PrefetchScalarGridSpec(
            num_scalar_prefetch=1, grid=(half // tr,),
            in_specs=[pl.BlockSpec((N_CHIPS, tr, cols), lambda i, chip_ref: (0, i, 0)),
                      pl.BlockSpec((None, tr, cols), lambda i, chip_ref: (chip_ref[0], i, 0))],
            out_specs=pl.BlockSpec((tr, cols), lambda i, chip_ref: (i, 0))),
        compiler_params=_params(("parallel",)),
    )(chip, received, own)


CHIP_FLIPS = ((1, 0), (0, 1), (1, 1))


def _place():
    x, y, c = lax.axis_index("x"), lax.axis_index("y"), lax.axis_index("c")
    return x, y, c, 2 * x + y


def _remote(src, dst, sems_s, sems_r, k, dev):
    return pltpu.make_async_remote_copy(src_ref=src, dst_ref=dst, send_sem=sems_s.at[k], recv_sem=sems_r.at[k],
                                        device_id=dev, device_id_type=MESH)


class _Exchange:
    def __init__(self, ins, out_shapes, n_sems, start, forward, finish):
        self.ins, self.out_shapes, self.n_sems = list(ins), list(out_shapes), n_sems
        self.start, self.forward, self.finish = start, forward, finish

    def scratch(self):
        return [pltpu.SemaphoreType.DMA((self.n_sems,)), pltpu.SemaphoreType.DMA((self.n_sems,))]

    def run(self, name):
        n = len(self.ins)

        def body(*refs):
            args = (refs[:n], refs[n:2 * n]) + tuple(refs[2 * n:])
            self.start(*args)
            self.forward(*args)
            self.finish(*args)

        return pl.pallas_call(
            body, name=name, in_specs=[ANY] * n, out_specs=[ANY] * n, out_shape=self.out_shapes, scratch_shapes=self.scratch(),
        )(*self.ins)


def _all_gather_weights(halved, whole):
    nh, nw = len(halved), len(whole)
    n_arr = nh + nw

    def copies(ins, outs, sems_s, sems_r):
        x, y, c, me = _place()
        sibling = (x, y, 1 - c)
        own = [_remote(ins[k], outs[k].at[me], sems_s, sems_r, k, sibling) for k in range(n_arr)]
        first, passed = [], []
        for k in range(n_arr):
            for f, (fx, fy) in enumerate(CHIP_FLIPS):
                src, dst = (ins[k].at[c], outs[k].at[me, c]) if k < nh else (ins[k], outs[k].at[me])
                first.append(_remote(src, dst, sems_s, sems_r, n_arr + 3 * k + f, (x ^ fx, y ^ fy, c)))
        for k in range(nh):
            for f, (fx, fy) in enumerate(CHIP_FLIPS):
                landed = outs[k].at[2 * (x ^ fx) + (y ^ fy), c]
                passed.append(_remote(landed, landed, sems_s, sems_r, 4 * n_arr + 3 * k + f, sibling))
        return own, first, passed

    def start(*refs):
        own, first, _ = copies(*refs)
        for cp in own + first:
            cp.start()

    def forward(*refs):
        _, first, passed = copies(*refs)
        for arrived, cp in zip(first, passed):
            arrived.wait_recv()
            cp.start()

    def finish(*refs):
        own, first, passed = copies(*refs)
        for cp in first[3 * nh:] + passed + own:
            cp.wait_recv()
        for cp in first + passed + own:
            cp.wait_send()

    shapes = [jax.ShapeDtypeStruct((N_CHIPS,) + a.shape, a.dtype) for a in list(halved) + list(whole)]
    return _Exchange(list(halved) + list(whole), shapes, 7 * nh + 4 * nw, start, forward, finish)


def _swap_halves(gs):
    n = len(gs)

    def copies(ins, outs, sems_s, sems_r):
        x, y, c, _ = _place()
        return [_remote(ins[k].at[:, 1 - c], outs[k], sems_s, sems_r, k, (x, y, 1 - c)) for k in range(n)]

    def start(*refs):
        for cp in copies(*refs):
            cp.start()

    def finish(*refs):
        for cp in copies(*refs):
            cp.wait()

    shapes = [jax.ShapeDtypeStruct((g.shape[0],) + g.shape[2:], g.dtype) for g in gs]
    return _Exchange(gs, shapes, n, start, _no_copies, finish)


def _scatter_chips(ps):
    n = len(ps)

    def copies(ins, outs, sems_s, sems_r):
        x, y, c, me = _place()
        return [_remote(ins[k].at[2 * (x ^ fx) + (y ^ fy)], outs[k].at[me], sems_s, sems_r, 3 * k + f, (x ^ fx, y ^ fy, c))
                for k in range(n) for f, (fx, fy) in enumerate(CHIP_FLIPS)]

    def start(*refs):
        for cp in copies(*refs):
            cp.start()

    def forward(*refs):
        pass

    def finish(*refs):
        for cp in copies(*refs):
            cp.wait()

    shapes = [jax.ShapeDtypeStruct(p.shape, p.dtype) for p in ps]
    return _Exchange(ps, shapes, 3 * n, start, forward, finish)


def _swap_reduced(rs):
    n = len(rs)

    def copies(ins, outs, sems_s, sems_r):
        x, y, c, _ = _place()
        return [_remote(ins[k], outs[k], sems_s, sems_r, k, (x, y, 1 - c)) for k in range(n)]

    def start(*refs):
        for cp in copies(*refs):
            cp.start()

    def finish(*refs):
        for cp in copies(*refs):
            cp.wait()

    return _Exchange(rs, [jax.ShapeDtypeStruct(r.shape, r.dtype) for r in rs], n, start, _no_copies, finish)


N_DEV = 8


def _gather_small(buf):
    def copies(ins, outs, sems_s, sems_r):
        x, y, c, _ = _place()
        me = 4 * x + 2 * y + c
        return [_remote(ins[0], outs[0].at[me], sems_s, sems_r, o - 1, (x ^ (o >> 2), y ^ ((o >> 1) & 1), c ^ (o & 1)))
                for o in range(1, N_DEV)]

    def start(*refs):
        for cp in copies(*refs):
            cp.start()

    def finish(*refs):
        for cp in copies(*refs):
            cp.wait()

    return _Exchange([buf], [jax.ShapeDtypeStruct((N_DEV,) + buf.shape, buf.dtype)], N_DEV - 1, start, _no_copies, finish)


def _sum_devices(place, gathered, own):
    rows = own.shape[0]

    def body(place_ref, g_ref, own_ref, o_ref):
        acc = None
        for d in range(N_DEV):
            term = jnp.where(place_ref[0] == d, own_ref[...], g_ref[d])
            acc = term if acc is None else acc + term
        o_ref[...] = acc

    return pl.pallas_call(
        body, name="sum_devices", out_shape=jax.ShapeDtypeStruct((rows, LANES), F32),
        grid_spec=pltpu.PrefetchScalarGridSpec(
            num_scalar_prefetch=1, grid=(1,),
            in_specs=[pl.BlockSpec((N_DEV, rows, LANES), lambda i, place_ref: (0, 0, 0)),
                      pl.BlockSpec((rows, LANES), lambda i, place_ref: (0, 0))],
            out_specs=pl.BlockSpec((rows, LANES), lambda i, place_ref: (0, 0))),
        compiler_params=_params(("arbitrary",)),
    )(place, gathered, own)


def _no_copies(*refs):
    pass


def _no_exchange():
    return _Exchange([], [], 1, _no_copies, _no_copies, _no_copies)


class _NoComm:
    def gather_first(self):
        return _no_exchange()

    def first_landed(self, p, landed):
        pass

    def gather_rest(self, p):
        return _no_exchange()

    def weights_landed(self, p, landed):
        pass

    def swap_first(self, g):
        return _no_exchange()

    def first_swapped(self, landed):
        pass

    def swap_second(self, g):
        return _no_exchange()

    def second_swapped(self, landed):
        pass

    def scatter_early(self, g):
        return _no_exchange()

    def scatter_landed(self, landed):
        pass

    def swap_reduced_early(self):
        return _no_exchange()

    def reduced_landed(self, landed):
        pass

    def scatter_late(self, g):
        return _no_exchange()

    def late_landed(self, landed):
        pass


def _local_step(x, mem, target, p, comm):
    h1, landed = _norm_fwd("norm_mix_pre", x, p["norm_mix_pre"], comm.gather_first())
    comm.first_landed(p, landed)
    proj = _mm_nn("in_proj", h1, p["w_in"], F32, 1024, 896)
    qa, ka, va = _fox_prep(proj, p["bf_pad"])
    ycat, qab, landed = _fox_fwd(qa, ka, va, comm.gather_rest(p))
    comm.weights_landed(p, landed)
    ycat = _pool_fwd(proj, p["w_pool_bd"], p["pool_scale"], ycat)
    y1, x2, h2, qx = _proj_resid_norm("mix_out", ycat, p["w_mix_out"], x, p["norm_mix_post"], p["norm_xa_pre"], p["w_xq"])
    mem_n = _norm_fwd("norm_mem", mem, p["norm_mem"])
    kv = _mm(
        "xkv", mem_n, p["w_xkv"], pl.BlockSpec((MEM, D), lambda i, j, k: (0, 0)),
        pl.BlockSpec((None, D, 512), lambda i, j, k: (j, 0, 0)), jax.ShapeDtypeStruct((MEM, 2 * D), BF16),
        pl.BlockSpec((MEM, 512), lambda i, j, k: (0, j)), (1, N_CHIPS, 1), NN, (MEM, 512))
    xo = _xattn_fwd(qx, kv)
    y2, x3, h3 = _proj_resid_norm("xo", xo, p["w_xo"], x2, p["norm_xa_post"], p["norm_ffn_pre"])
    hid = _mm(
        "up_proj", h3, p["w_up"], pl.BlockSpec((1024, D), lambda i, j, k: (i, 0)),
        pl.BlockSpec((None, D, 1024), lambda i, j, k: (j // 2, 0, j % 2)), jax.ShapeDtypeStruct((2, S, D_FF), F32),
        pl.BlockSpec((None, 1024, 1024), lambda i, j, k: (j // 4, i, j % 4)), (S // 1024, 8, 1), NN, (1024, 1024))
    act = _convgate_fwd(hid, p["cwb"])

    g = {}
    dres, dy3, g["norm_ffn_post"], loss_cols = _down_loss_bwd(act, p["w_down"], x3, p["norm_ffn_post"], target)
    dact = _mm_nt("d_act", dy3, p["w_down"], F32, 1024, 1024)
    g["w_down"] = _mm_tn("dw_down", act, dy3, 512, 512)
    dhid, dcwb = _convgate_bwd(hid, dact, p["cwb"])
    g["w_up"] = _mm(
        "dw_up", h3, dhid, pl.BlockSpec((S, 512), lambda i, j, k: (0, i)),
        pl.BlockSpec((None, S, 512), lambda i, j, k: (j // 8, 0, j % 8)), jax.ShapeDtypeStruct((N_CHIPS, D, 2048), F32),
        pl.BlockSpec((None, 512, 512), lambda i, j, k: (j // 4, i, j % 4)), (2, 16, 1), TN, (512, 512))
    dh3, landed = _d_h3(dhid, p["w_up"], comm.swap_first(g))
    comm.first_swapped(landed)
    dres, dy2, dxo, g["norm_ffn_pre"], g["norm_xa_post"] = _mid_bwd(
        "bwd_ffn_xa", dres, x3, p["norm_ffn_pre"], dh3, y2, p["norm_xa_post"], p["w_xo"])
    g["w_xo"] = _mm_tn("dw_xo", xo, dy2, 512, 512)
    dqx, dkv = _xattn_bwd(qx, kv, dxo)
    dkv = dkv.astype(BF16)
    g["w_xq"] = _mm_tn("dw_xq", h2, dqx, 512, 512)
    dmem_n = _mm(
        "d_mem", dkv, p["w_xkv"], pl.BlockSpec((MEM, 512), lambda i, j, k: (0, k)),
        pl.BlockSpec((None, D, 512), lambda i, j, k: (k, 0, 0)), jax.ShapeDtypeStruct((MEM, D), F32),
        pl.BlockSpec((MEM, D), lambda i, j, k: (0, 0)), (1, 1, N_CHIPS), NT, (MEM, D))
    g["w_xkv"] = _mm(
        "dw_xkv", mem_n, dkv, pl.BlockSpec((MEM, D), lambda i, j, k: (0, 0)),
        pl.BlockSpec((MEM, 512), lambda i, j, k: (0, j)), jax.ShapeDtypeStruct((N_CHIPS, D, 512), F32),
        pl.BlockSpec((None, D, 512), lambda i, j, k: (j, 0, 0)), (1, N_CHIPS, 1), TN, (D, 512))
    g["norm_mem"] = _gain_bwd("dg_mem", mem, p["norm_mem"], dmem_n)
    (dres, dy1, g["norm_xa_pre"], g["norm_mix_post"], dy_pool, doa), landed = _bwd_xa_mix(
        dqx, p["w_xq"], dres, x2, p["norm_xa_pre"], y1, p["norm_mix_post"], p["w_mix_out"], ycat, comm.swap_second(g))
    comm.second_swapped(landed)
    g["w_mix_out"] = _mm_tn("dw_mix_out", ycat, dy1, 512, 512)
    dqa, dka, dva, landed = _fox_bwd(qab, doa, ka, va, comm.scatter_early(g))
    comm.scatter_landed(landed)
    du, g["w_pool_full"], g["pool_scale"] = _pool_bwd(proj, dy_pool, p["w_pool_bd"], p["w_pool_bd_t"], p["pool_scale"])
    dproj, g["bf_pad"] = _fox_bwd_post(dqa, dka, dva, du, proj, p["bf_pad"])
    g["w_in"], landed = _mm_tn("dw_in", h1, dproj, 512, 896, comm.swap_reduced_early())
    comm.reduced_landed(landed)
    dh1, landed = _mm_nt("d_h1", dproj, p["w_in"], F32, 1024, 1024, comm.scatter_late(g))
    comm.late_landed(landed)
    grad_x, g["norm_mix_pre"] = _first_bwd(dres, x, p["norm_mix_pre"], dh1)
    g["cwb"] = dcwb
    return grad_x, g, loss_cols


BIG = ("w_in", "w_mix_out", "w_xq", "w_xkv", "w_xo", "w_up", "w_down")
ROW_SHARDED = ("w_mix_out", "w_xq", "w_xo", "w_down")
SMALL = ("norm_mix_pre", "norm_mix_post", "b_forget", "w_pool", "pool_scale", "norm_mem", "norm_xa_pre", "norm_xa_post",
         "norm_ffn_pre", "norm_ffn_post", "conv_b")
ORDER = ("norm_mix_pre", "norm_mix_post", "w_in", "b_forget", "w_pool", "pool_scale", "w_mix_out", "norm_mem", "norm_xa_pre",
         "norm_xa_post", "w_xq", "w_xkv", "w_xo", "norm_ffn_pre", "norm_ffn_post", "w_up", "conv_w", "conv_b", "w_down")
SLOT = SUBLANES * LANES


def _pack(parts):
    rows, offs, off = [], [], 0
    for a in parts:
        flat = a.reshape(-1).astype(F32)
        n = -(-flat.shape[0] // SLOT) * SLOT
        rows.append(jnp.pad(flat, (0, n - flat.shape[0])).reshape(n // LANES, LANES))
        offs.append(off)
        off += n // LANES
    return jnp.concatenate(rows, axis=0), offs


def _unpack(buf, off, like):
    n = like.size
    rows = -(-n // LANES)
    return buf[off:off + rows].reshape(-1)[:n].reshape(like.shape)


FIRST = ("w_in",)
REST = ("w_mix_out", "w_xq", "w_xkv", "w_xo", "w_up", "w_down")


def _local_params(w):
    w_pool_bd = jnp.zeros((D_POOL, D_POOL), F32)
    for gi in range(4):
        w_pool_bd = w_pool_bd.at[64 * gi:64 * (gi + 1), 64 * gi:64 * (gi + 1)].set(w["w_pool"][0, gi])
    p = {n: w[n] for n in ("norm_mix_pre", "norm_mix_post", "norm_mem", "norm_xa_pre", "norm_xa_post", "norm_ffn_pre",
                           "norm_ffn_post")}
    p.update(
        bf_pad=jnp.pad(w["b_forget"], ((0, 0), (0, LANES - HEADS))),
        w_pool_bd=w_pool_bd.astype(BF16), w_pool_bd_t=w_pool_bd.T.astype(BF16), pool_scale=w["pool_scale"].reshape(1, D_POOL))
    return p


def _w_in_param(stacked):
    return jnp.pad(jnp.concatenate(list(stacked), axis=1), ((0, 0), (0, D_IN_PAD - D_IN)))


def _rest_params(w, full, conv_w_full):
    cw2 = conv_w_full.reshape(3, 2, D_FF).transpose(1, 0, 2)
    cwb = jnp.concatenate([cw2, w["conv_b"].reshape(1, 2, D_FF).transpose(1, 0, 2), jnp.zeros((2, 4, D_FF), F32)], axis=1)
    return dict(w_mix_out=full["w_mix_out"].reshape(D, D), w_xq=full["w_xq"].reshape(D, D), w_xkv=full["w_xkv"],
                w_xo=full["w_xo"].reshape(D, D), w_up=full["w_up"], cwb=cwb, w_down=full["w_down"].reshape(D_FF, D))


def _whole_params(w, full, conv_w_full):
    p = _local_params(w)
    p.update(_rest_params(w, full, conv_w_full), w_in=_w_in_param(full["w_in"]))
    return p


def _halved(a):
    return a.reshape(a.shape[:-2] + (2, a.shape[-2] // 2, a.shape[-1]))


class _StepComm:
    def __init__(self, w, shard2d, conv_w, core_id, chip_id):
        self.w, self.shard2d, self.conv_w, self.core_id, self.chip_id = w, shard2d, conv_w, core_id, chip_id
        self.first, self.second = ("w_up", "w_down"), ("w_xq", "w_xkv", "w_xo")
        self.early = self.first + self.second
        self.late = ("w_in", "w_mix_out")

    def gather_first(self):
        return _all_gather_weights([_halved(self.shard2d[n].astype(BF16)) for n in FIRST], [])

    def first_landed(self, p, landed):
        p["w_in"] = _w_in_param(landed[0].reshape((N_CHIPS,) + self.shard2d["w_in"].shape))

    def gather_rest(self, p):
        return _all_gather_weights([_halved(self.shard2d[n].astype(BF16)) for n in REST], [self.conv_w.reshape(3, -1)])

    def weights_landed(self, p, landed):
        full = {n: a.reshape((N_CHIPS,) + self.shard2d[n].shape) for n, a in zip(REST, landed)}
        conv_w_full = jnp.transpose(landed[-1], (1, 0, 2)).reshape(3, 2 * D_FF)
        p.update(_rest_params(self.w, full, conv_w_full))

    def _view(self, g, n):
        return _halved(g[n].reshape((N_CHIPS,) + self.shard2d[n].shape))

    def swap_first(self, g):
        return _swap_halves([self._view(g, n) for n in self.first])

    def first_swapped(self, landed):
        self.from_sibling = dict(zip(self.first, landed))

    def swap_second(self, g):
        return _swap_halves([self._view(g, n) for n in self.second])

    def second_swapped(self, landed):
        self.from_sibling.update(zip(self.second, landed))

    def scatter_early(self, g):
        self.partial = [_chip_sum("chip_sum_" + n, self.core_id, self._view(g, n), self.from_sibling[n]) for n in self.early]
        return _scatter_chips(self.partial)

    def scatter_landed(self, landed):
        self.received = list(landed)

    def swap_reduced_early(self):
        self.reduced = [_mesh_sum("mesh_sum_" + n, self.chip_id, r, own)
                        for n, r, own in zip(self.early, self.received, self.partial)]
        return _swap_reduced(self.reduced)

    def reduced_landed(self, landed):
        self.reduced_sibling = list(landed)

    def scatter_late(self, g):
        gw_in = g["w_in"][:, :D_IN]
        cols = D_IN // N_CHIPS
        views = [_halved(jnp.stack([gw_in[:, cols * j:cols * (j + 1)] for j in range(N_CHIPS)])), self._view(g, "w_mix_out")]
        from_sibling = _swap_halves(views).run("swap_halves_late")
        self.partial_late = [_chip_sum("chip_sum_" + n, self.core_id, view, other)
                             for n, view, other in zip(self.late, views, from_sibling)]
        return _scatter_chips(self.partial_late)

    def late_landed(self, landed):
        self.received_late = list(landed)


def kernel(x, mem, norm_mix_pre, norm_mix_post, w_in, b_forget, w_pool, pool_scale, w_mix_out, norm_mem, norm_xa_pre, norm_xa_post, w_xq, w_xkv, w_xo, norm_ffn_pre, norm_ffn_post, w_up, conv_w, conv_b, w_down, loss_target, m_norm_mix_pre, m_norm_mix_post, m_w_in, m_b_forget, m_w_pool, m_pool_scale, m_w_mix_out, m_norm_mem, m_norm_xa_pre, m_norm_xa_post, m_w_xq, m_w_xkv, m_w_xo, m_norm_ffn_pre, m_norm_ffn_post, m_w_up, m_conv_w, m_conv_b, m_w_down, v_norm_mix_pre, v_norm_mix_post, v_w_in, v_b_forget, v_w_pool, v_pool_scale, v_w_mix_out, v_norm_mem, v_norm_xa_pre, v_norm_xa_post, v_w_xq, v_w_xkv, v_w_xo, v_norm_ffn_pre, v_norm_ffn_post, v_w_up, v_conv_w, v_conv_b, v_w_down):
    w = dict(norm_mix_pre=norm_mix_pre, norm_mix_post=norm_mix_post, w_in=w_in, b_forget=b_forget, w_pool=w_pool,
             pool_scale=pool_scale, w_mix_out=w_mix_out, norm_mem=norm_mem, norm_xa_pre=norm_xa_pre, norm_xa_post=norm_xa_post,
             w_xq=w_xq, w_xkv=w_xkv, w_xo=w_xo, norm_ffn_pre=norm_ffn_pre, norm_ffn_post=norm_ffn_post, w_up=w_up,
             conv_w=conv_w, conv_b=conv_b, w_down=w_down)
    m = dict(norm_mix_pre=m_norm_mix_pre, norm_mix_post=m_norm_mix_post, w_in=m_w_in, b_forget=m_b_forget, w_pool=m_w_pool,
             pool_scale=m_pool_scale, w_mix_out=m_w_mix_out, norm_mem=m_norm_mem, norm_xa_pre=m_norm_xa_pre,
             norm_xa_post=m_norm_xa_post, w_xq=m_w_xq, w_xkv=m_w_xkv, w_xo=m_w_xo, norm_ffn_pre=m_norm_ffn_pre,
             norm_ffn_post=m_norm_ffn_post, w_up=m_w_up, conv_w=m_conv_w, conv_b=m_conv_b, w_down=m_w_down)
    v = dict(norm_mix_pre=v_norm_mix_pre, norm_mix_post=v_norm_mix_post, w_in=v_w_in, b_forget=v_b_forget, w_pool=v_w_pool,
             pool_scale=v_pool_scale, w_mix_out=v_w_mix_out, norm_mem=v_norm_mem, norm_xa_pre=v_norm_xa_pre,
             norm_xa_post=v_norm_xa_post, w_xq=v_w_xq, w_xkv=v_w_xkv, w_xo=v_w_xo, norm_ffn_pre=v_norm_ffn_pre,
             norm_ffn_post=v_norm_ffn_post, w_up=v_w_up, conv_w=v_conv_w, conv_b=v_conv_b, w_down=v_w_down)
    chip = 2 * lax.axis_index("x") + lax.axis_index("y")

    core_id = lax.axis_index("c").astype(jnp.int32).reshape(1)
    chip_id = chip.astype(jnp.int32).reshape(1)

    shard2d = {n: w[n][0] for n in BIG}
    p = _local_params(w)
    comm = _StepComm(w, shard2d, conv_w, core_id, chip_id)
    grad_x, g, loss_cols = _local_step(x[0], mem[0], loss_target[0], p, comm)

    reduced_late = [_mesh_sum("mesh_sum_" + n, chip_id, r, own)
                    for n, r, own in zip(comm.late, comm.received_late, comm.partial_late)]
    names = comm.late + comm.early
    reduced = reduced_late + comm.reduced
    reduced_sibling = list(_swap_reduced(reduced_late).run("swap_reduced_late")) + comm.reduced_sibling
    grads = {}

    gw_pool = jnp.stack([g["w_pool_full"][64 * gi:64 * (gi + 1), 64 * gi:64 * (gi + 1)] for gi in range(4)])
    dcwb = g["cwb"]
    g_conv_w = dcwb[:, 0:3, :].transpose(1, 0, 2).reshape(3, 2 * D_FF)
    g_conv_b = dcwb[:, 3, :].reshape(2 * D_FF)
    small_g = dict(norm_mix_pre=g["norm_mix_pre"], norm_mix_post=g["norm_mix_post"], b_forget=g["bf_pad"][:, :HEADS],
                   w_pool=gw_pool, pool_scale=g["pool_scale"], norm_mem=g["norm_mem"], norm_xa_pre=g["norm_xa_pre"],
                   norm_xa_post=g["norm_xa_post"], norm_ffn_pre=g["norm_ffn_pre"], norm_ffn_post=g["norm_ffn_post"],
                   conv_b=g_conv_b)
    local_buf, offs = _pack([small_g[n] for n in SMALL] + [g_conv_w, loss_cols])

    delta, new_m, new_v = {}, {}, {}
    for n, g_mine, g_sibling in zip(names, reduced, reduced_sibling):
        gn, d, nm, nv = _adamw_halves("adamw_" + n, core_id, shard2d[n], g_mine, g_sibling, m[n][0], v[n][0])
        grads[n], delta[n], new_m[n], new_v[n] = gn[None], d[None], nm[None], nv[None]
    place = (2 * chip + lax.axis_index("c")).astype(jnp.int32).reshape(1)
    buf = _sum_devices(place, _gather_small(local_buf).run("gather_small")[0], local_buf)
    for n, off in zip(SMALL, offs):
        grads[n] = _unpack(buf, off, w[n])
    g_conv_w = _unpack(buf, offs[len(SMALL)], g_conv_w)
    grads["conv_w"] = lax.dynamic_slice_in_dim(g_conv_w, chip * (2 * D_FF // N_CHIPS), 2 * D_FF // N_CHIPS, axis=1).reshape(conv_w.shape)
    loss = jnp.sum(_unpack(buf, offs[len(SMALL) + 1], loss_cols))
    small_names = SMALL + ("conv_w",)
    packed = [_pack([d[n] for n in small_names])[0] for d in (w, grads, m, v)]
    offs = _pack([w[n] for n in small_names])[1]
    d, nm, nv = _adamw("adamw_small", *packed)
    for n, off in zip(small_names, offs):
        delta[n], new_m[n], new_v[n] = _unpack(d, off, w[n]), _unpack(nm, off, w[n]), _unpack(nv, off, w[n])

    return (loss, grad_x[None], *[grads[n] for n in ORDER], *[delta[n] for n in ORDER], *[new_m[n] for n in ORDER],
            *[new_v[n] for n in ORDER])
```

```python
import functools

import jax
import jax.numpy as jnp
import numpy as np
from jax import lax
from jax.experimental import pallas as pl
from jax.experimental.pallas import tpu as pltpu

F32 = jnp.float32
BF16 = jnp.bfloat16
MESH = pl.DeviceIdType.MESH
ANY = pl.BlockSpec(memory_space=pl.ANY)
VMEM_SPEC = pl.BlockSpec(memory_space=pltpu.VMEM)

S = 4096
D = 1024
MEM = 256
D_POOL = 256
HEADS = 12
DH = 64
D_FOX = HEADS * DH
D_IN = D_POOL + 3 * D_FOX + HEADS
F_OFF = D_POOL + 3 * D_FOX
Q_OFF, K_OFF, V_OFF = D_POOL, D_POOL + D_FOX, D_POOL + 2 * D_FOX
XA_HEADS = 4
XA_DH = 256
D_FF = 4096
EPS = 1e-6
N_CHIPS = 4
ADAM_LR, ADAM_B1, ADAM_B2, ADAM_EPS, ADAM_WD, ADAM_STEP = 0.001, 0.9, 0.999, 1e-08, 0.01, 10

LANES = 128
SUBLANES = 8
D_IN_PAD = 21 * LANES
TR = 512
TILE_BYTES = 2 * 1024 * 1024
NEG = -1e30
VMEM_LIMIT = 52 * 1024 * 1024

NN = (((1,), (0,)), ((), ()))
NT = (((1,), (1,)), ((), ()))
TN = (((0,), (0,)), ((), ()))


def _dot(a, b, dims=NN):
    return lax.dot_general(a, b, dims, preferred_element_type=F32)


def _params(sem):
    return pltpu.CompilerParams(dimension_semantics=sem, vmem_limit_bytes=VMEM_LIMIT)


def _split3(x):
    hi = x.astype(BF16)
    r = x - hi.astype(F32)
    mid = r.astype(BF16)
    lo = (r - mid.astype(F32)).astype(BF16)
    return hi, mid, lo


def _split3_f32(x):
    hi = x.astype(BF16).astype(F32)
    r = x - hi
    mid = r.astype(BF16).astype(F32)
    return hi, mid, r - mid


def _lane_iota(shape):
    return lax.broadcasted_iota(jnp.int32, shape, len(shape) - 1)


def _row_iota(shape):
    return lax.broadcasted_iota(jnp.int32, shape, len(shape) - 2)


def _mm(name, a, b, a_spec, b_spec, out_shape, out_spec, grid, dims, acc_shape, ex=None):
    nk = grid[2]
    if ex is not None:
        return _mm_hosting(name, a, b, a_spec, b_spec, out_shape, out_spec, grid, dims, ex)

    def body(a_ref, b_ref, o_ref, *scr):
        p = _dot(a_ref[...], b_ref[...], dims)
        if nk == 1:
            o_ref[...] = p.astype(o_ref.dtype)
        else:
            acc = scr[0]
            k = pl.program_id(2)

            @pl.when(k == 0)
            def _():
                acc[...] = p

            @pl.when(k > 0)
            def _():
                acc[...] += p

            @pl.when(k == nk - 1)
            def _():
                o_ref[...] = acc[...].astype(o_ref.dtype)

    return pl.pallas_call(
        body, name=name, grid=grid, in_specs=[a_spec, b_spec], out_specs=out_spec, out_shape=out_shape,
        scratch_shapes=[pltpu.VMEM(acc_shape, F32)] if nk > 1 else [],
        compiler_params=_params(("parallel", "parallel", "arbitrary")),
    )(a, b)


def _mm_hosting(name, a, b, a_spec, b_spec, out_shape, out_spec, grid, dims, ex):
    assert grid[2] == 1
    n = len(ex.ins)

    def body(*refs):
        i, j = pl.program_id(0), pl.program_id(1)
        first = (i == 0) & (j == 0)
        (a_ref, b_ref), (o_ref,), _, begin, end = _hosted(
            ex, refs, 2, 1, first, first, (i == grid[0] - 1) & (j == grid[1] - 1))
        begin()
        o_ref[...] = _dot(a_ref[...], b_ref[...], dims).astype(o_ref.dtype)
        end()

    res = pl.pallas_call(
        body, name=name, grid=grid, in_specs=[a_spec, b_spec] + [ANY] * n, out_specs=[out_spec] + [ANY] * n,
        out_shape=[out_shape] + ex.out_shapes, scratch_shapes=ex.scratch(),
        compiler_params=_params(("arbitrary", "arbitrary", "arbitrary")),
    )(a, b, *ex.ins)
    return res[0], res[1:]


def _mm_nn(name, a, b, out_dtype, tm, tn):
    m, k = a.shape
    n = b.shape[1]
    return _mm(name, a, b, pl.BlockSpec((tm, k), lambda i, j, kk: (i, 0)), pl.BlockSpec((k, tn), lambda i, j, kk: (0, j)),
               jax.ShapeDtypeStruct((m, n), out_dtype), pl.BlockSpec((tm, tn), lambda i, j, kk: (i, j)),
               (m // tm, n // tn, 1), NN, (tm, tn))


def _mm_nt(name, a, b, out_dtype, tm, tn, ex=None):
    m, k = a.shape
    n = b.shape[0]
    return _mm(name, a, b, pl.BlockSpec((tm, k), lambda i, j, kk: (i, 0)), pl.BlockSpec((tn, k), lambda i, j, kk: (j, 0)),
               jax.ShapeDtypeStruct((m, n), out_dtype), pl.BlockSpec((tm, tn), lambda i, j, kk: (i, j)),
               (m // tm, n // tn, 1), NT, (tm, tn), ex)


def _mm_tn(name, a, b, tka, tn, ex=None):
    t, ka = a.shape
    n = b.shape[1]
    return _mm(name, a, b, pl.BlockSpec((t, tka), lambda i, j, kk: (0, i)), pl.BlockSpec((t, tn), lambda i, j, kk: (0, j)),
               jax.ShapeDtypeStruct((ka, n), F32), pl.BlockSpec((tka, tn), lambda i, j, kk: (i, j)),
               (ka // tka, n // tn, 1), TN, (tka, tn), ex)


def _d_h3(dhid, w_up, ex):
    tm = tn = 512
    shard = 2 * D_FF // N_CHIPS
    per_plane = D_FF // shard
    grid = (S // tm, D // tn)
    n = len(ex.ins)

    def body(*refs):
        i, j = pl.program_id(0), pl.program_id(1)
        first = (i == 0) & (j == 0)
        (a_ref, b_ref), (o_ref,), _, begin, end = _hosted(ex, refs, 2, 1, first, first, (i == grid[0] - 1) & (j == grid[1] - 1))
        begin()
        acc = None
        for k in range(N_CHIPS):
            cols = slice(shard * (k % per_plane), shard * (k % per_plane + 1))
            part = _dot(a_ref[k // per_plane, :, cols], b_ref[k], NT)
            acc = part if acc is None else acc + part
        o_ref[...] = acc
        end()

    res = pl.pallas_call(
        body, name="d_h3", grid=grid,
        in_specs=[pl.BlockSpec((2, tm, D_FF), lambda i, j: (0, i, 0)),
                  pl.BlockSpec((N_CHIPS, tn, shard), lambda i, j: (0, j, 0))] + [ANY] * n,
        out_specs=[pl.BlockSpec((tm, tn), lambda i, j: (i, j))] + [ANY] * n,
        out_shape=[jax.ShapeDtypeStruct((S, D), F32)] + ex.out_shapes, scratch_shapes=ex.scratch(),
        compiler_params=_params(("arbitrary", "arbitrary")),
    )(dhid, w_up, *ex.ins)
    return res[0], res[1:]


def _rms(x, g):
    r = lax.rsqrt(jnp.mean(x * x, axis=-1, keepdims=True) + EPS)
    return x * r * g


def _rms_bwd(x, g, dy):
    r = lax.rsqrt(jnp.mean(x * x, axis=-1, keepdims=True) + EPS)
    xh = x * r
    dxh = dy * g
    dx = r * (dxh - xh * jnp.mean(dxh * xh, axis=-1, keepdims=True))
    return dx, jnp.sum(dy * xh, axis=0, keepdims=True)


def _row_spec(tr, width):
    return pl.BlockSpec((tr, width), lambda i: (i, 0))


def _vec_spec(width):
    return pl.BlockSpec((1, width), lambda i: (0, 0))


def _norm_fwd(name, x, g, ex=None):
    rows, width = x.shape
    tr = min(TR, rows)
    steps = rows // tr
    hosted = ex if ex is not None else _no_exchange()
    n = len(hosted.ins)

    def body(*refs):
        i = pl.program_id(0)
        (x_ref, g_ref), (h_ref,), _, begin, end = _hosted(hosted, refs, 2, 1, i == 0, i == 0, i == steps - 1)
        begin()
        h_ref[...] = _rms(x_ref[...], g_ref[...]).astype(BF16)
        end()

    res = pl.pallas_call(
        body, name=name, grid=(steps,), in_specs=[_row_spec(tr, width), _vec_spec(width)] + [ANY] * n,
        out_specs=[_row_spec(tr, width)] + [ANY] * n,
        out_shape=[jax.ShapeDtypeStruct((rows, width), BF16)] + hosted.out_shapes, scratch_shapes=hosted.scratch(),
        compiler_params=_params(("arbitrary",)),
    )(x, g, *hosted.ins)
    return res[0] if ex is None else (res[0], res[1:])


def _proj_resid_norm(name, a, w, xp, g_post, g_pre, w_next=None):
    def body(a_ref, w_ref, xp_ref, gpost_ref, gpre_ref, *rest):
        y_ref, xn_ref, h_ref = rest[-3:] if w_next is None else rest[1:4]
        y = _dot(a_ref[...], w_ref[...])
        y_ref[...] = y
        xn = xp_ref[...] + _rms(y, gpost_ref[...])
        xn_ref[...] = xn
        h = _rms(xn, gpre_ref[...]).astype(BF16)
        h_ref[...] = h
        if w_next is not None:
            rest[4][...] = _dot(h, rest[0][...]).astype(BF16)

    mat = pl.BlockSpec((D, D), lambda i: (0, 0))
    more = [] if w_next is None else [w_next]
    return pl.pallas_call(
        body, name=name, grid=(S // TR,),
        in_specs=[_row_spec(TR, D), mat, _row_spec(TR, D), _vec_spec(D), _vec_spec(D)] + [mat] * len(more),
        out_specs=[_row_spec(TR, D)] * (3 + len(more)),
        out_shape=[jax.ShapeDtypeStruct((S, D), F32), jax.ShapeDtypeStruct((S, D), F32), jax.ShapeDtypeStruct((S, D), BF16)]
        + [jax.ShapeDtypeStruct((S, D), BF16)] * len(more),
        compiler_params=_params(("parallel",)),
    )(a, w, xp, g_post, g_pre, *more)


def _down_loss_bwd(act, w_down, x3, g_post, target):
    def body(a_ref, w_ref, x_ref, g_ref, t_ref, dres_ref, dy_ref, dg_ref, loss_ref):
        i = pl.program_id(0)

        @pl.when(i == 0)
        def _():
            dg_ref[...] = jnp.zeros_like(dg_ref)
            loss_ref[...] = jnp.zeros_like(loss_ref)

        y = _dot(a_ref[...], w_ref[...])
        g = g_ref[...]
        e = x_ref[...] + _rms(y, g) - t_ref[...]
        loss_ref[...] += jnp.sum(e * e, axis=0, keepdims=True) * (0.5 / D)
        dres = e * (1.0 / D)
        dres_ref[...] = dres
        dy, dg = _rms_bwd(y, g, dres)
        dy_ref[...] = dy.astype(BF16)
        dg_ref[...] += dg

    return pl.pallas_call(
        body, name="down_loss_bwd", grid=(S // TR,),
        in_specs=[_row_spec(TR, D_FF), pl.BlockSpec((D_FF, D), lambda i: (0, 0)), _row_spec(TR, D), _vec_spec(D),
                  _row_spec(TR, D)],
        out_specs=[_row_spec(TR, D), _row_spec(TR, D), _vec_spec(D), _vec_spec(D)],
        out_shape=[jax.ShapeDtypeStruct((S, D), F32), jax.ShapeDtypeStruct((S, D), BF16),
                   jax.ShapeDtypeStruct((1, D), F32), jax.ShapeDtypeStruct((1, D), F32)],
        compiler_params=_params(("arbitrary",)),
    )(act, w_down, x3, g_post, target)


def _mid_bwd(name, dres, xcur, g_pre, dh, yprev, g_post, w):
    def body(dres_ref, x_ref, gpre_ref, dh_ref, y_ref, gpost_ref, w_ref, dx_ref, dy_ref, da_ref, dgpre_ref, dgpost_ref):
        i = pl.program_id(0)

        @pl.when(i == 0)
        def _():
            dgpre_ref[...] = jnp.zeros_like(dgpre_ref)
            dgpost_ref[...] = jnp.zeros_like(dgpost_ref)

        dxn, dgpre = _rms_bwd(x_ref[...], gpre_ref[...], dh_ref[...])
        dx = dres_ref[...] + dxn
        dx_ref[...] = dx
        dy, dgpost = _rms_bwd(y_ref[...], gpost_ref[...], dx)
        dy = dy.astype(BF16)
        dy_ref[...] = dy
        da_ref[...] = _dot(dy, w_ref[...], NT).astype(BF16)
        dgpre_ref[...] += dgpre
        dgpost_ref[...] += dgpost

    return pl.pallas_call(
        body, name=name, grid=(S // TR,),
        in_specs=[_row_spec(TR, D), _row_spec(TR, D), _vec_spec(D), _row_spec(TR, D), _row_spec(TR, D), _vec_spec(D),
                  pl.BlockSpec((D, D), lambda i: (0, 0))],
        out_specs=[_row_spec(TR, D), _row_spec(TR, D), _row_spec(TR, D), _vec_spec(D), _vec_spec(D)],
        out_shape=[jax.ShapeDtypeStruct((S, D), F32), jax.ShapeDtypeStruct((S, D), BF16), jax.ShapeDtypeStruct((S, D), BF16),
                   jax.ShapeDtypeStruct((1, D), F32), jax.ShapeDtypeStruct((1, D), F32)],
        compiler_params=_params(("arbitrary",)),
    )(dres, xcur, g_pre, dh, yprev, g_post, w)


def _first_bwd(dres, x, g, dh):
    def body(dres_ref, x_ref, g_ref, dh_ref, dx_ref, dg_ref):
        i = pl.program_id(0)

        @pl.when(i == 0)
        def _():
            dg_ref[...] = jnp.zeros_like(dg_ref)

        dxn, dg = _rms_bwd(x_ref[...], g_ref[...], dh_ref[...])
        dx_ref[...] = dres_ref[...] + dxn
        dg_ref[...] += dg

    return pl.pallas_call(
        body, name="first_bwd", grid=(S // TR,),
        in_specs=[_row_spec(TR, D), _row_spec(TR, D), _vec_spec(D), _row_spec(TR, D)],
        out_specs=[_row_spec(TR, D), _vec_spec(D)],
        out_shape=[jax.ShapeDtypeStruct((S, D), F32), jax.ShapeDtypeStruct((1, D), F32)],
        compiler_params=_params(("arbitrary",)),
    )(dres, x, g, dh)


def _gain_bwd(name, x, g, dy):
    rows, width = x.shape

    def body(x_ref, g_ref, dy_ref, dg_ref):
        _, dg = _rms_bwd(x_ref[...], g_ref[...], dy_ref[...])
        dg_ref[...] = dg

    return pl.pallas_call(
        body, name=name, grid=(1,), in_specs=[_row_spec(rows, width), _vec_spec(width), _row_spec(rows, width)],
        out_specs=_vec_spec(width), out_shape=jax.ShapeDtypeStruct((1, width), F32),
        compiler_params=_params(("arbitrary",)),
    )(x, g, dy)


CUM_Q = DH
CUM_K = DH + 3
LSE_Q = DH + 6
BOTH_ONE = DH + 9
DEN_V = DH
DELTA = DH + 1
PREP_TR = 256
PIECE_LANES = 16
FOX_FWD_BLOCK = 1024
FOX_BWD_BLOCK = 512


def _at(lane_of_even_head, h):
    return (lane_of_even_head + DH * (h % 2)) % LANES


def _data_lanes(lane, h):
    return lane >= DH if h % 2 else lane < DH


def _pair_block(ref, off, h):
    base = ((off + DH * h) // LANES) * LANES
    return ref[:, base:base + LANES]


def _cumsum_rows(x, tri, carry):
    hi, mid, lo = _split3(x)
    return _dot(tri, hi) + _dot(tri, mid) + _dot(tri, lo) + carry


def _in_proj(h1, w_in, bf_pad):
    tr = TR

    place_q = np.zeros((LANES, HEADS * LANES), np.float32)
    place_k = np.zeros((LANES, HEADS * LANES), np.float32)
    for h in range(HEADS):
        for piece in range(3):
            place_q[PIECE_LANES * piece + h, LANES * h + _at(CUM_Q, h) + piece] = 1.0
            place_k[PIECE_LANES * piece + h, LANES * h + _at(CUM_K, h) + piece] = -1.0

    def body(h_ref, w_ref, bf_ref, pq_ref, pk_ref, qa_ref, ka_ref, va_ref, u_ref, z_ref, carry_ref):
        i = pl.program_id(0)

        @pl.when(i == 0)
        def _():
            carry_ref[...] = jnp.zeros_like(carry_ref)

        proj = _dot(h_ref[...], w_ref[...])
        u_ref[...] = proj[:, :D_POOL]
        z_ref[...] = proj[:, F_OFF:F_OFF + LANES]
        lane = _lane_iota((tr, LANES))
        z = proj[:, F_OFF:F_OFF + LANES] + bf_ref[...]
        log_f = jnp.minimum(z, 0.0) - jnp.log(1.0 + jnp.exp(-jnp.abs(z)))
        log_f = jnp.where(lane < HEADS, log_f, 0.0)
        tri = jnp.where(_row_iota((tr, tr)) >= _lane_iota((tr, tr)), 1.0, 0.0).astype(BF16)
        cum = _cumsum_rows(log_f, tri, carry_ref[0:1, :])
        carry_ref[0:1, :] = cum[tr - 1:tr, :]
        c_hi, c_mid, c_lo = _split3_f32(cum)
        pieces = (c_hi + pltpu.roll(c_mid, PIECE_LANES, 1) + pltpu.roll(c_lo, 2 * PIECE_LANES, 1)).astype(BF16)
        cum_q = _dot(pieces, pq_ref[...])
        cum_k = _dot(pieces, pk_ref[...])

        def between(first, h):
            return (lane >= _at(first, h)) & (lane < _at(first, h) + 3)

        ones_q = [jnp.where(between(CUM_K, h) | (lane == _at(BOTH_ONE, h)), 1.0, 0.0) for h in range(2)]
        ones_k = [jnp.where(between(CUM_Q, h) | between(LSE_Q, h) | (lane == _at(BOTH_ONE, h)), 1.0, 0.0) for h in range(2)]
        aug_v = [jnp.where(lane == _at(DEN_V, h), 1.0, jnp.where(between(DELTA, h), -1.0, 0.0)) for h in range(2)]
        for h in range(HEADS):
            mine = slice(LANES * h, LANES * (h + 1))
            data = _data_lanes(lane, h)
            qa_ref[h] = jnp.where(data, _pair_block(proj, Q_OFF, h) * (DH ** -0.5), cum_q[:, mine] + ones_q[h % 2]).astype(BF16)
            ka_ref[h] = jnp.where(data, _pair_block(proj, K_OFF, h), cum_k[:, mine] + ones_k[h % 2]).astype(BF16)
            va_ref[h] = jnp.where(data, _pair_block(proj, V_OFF, h), aug_v[h % 2]).astype(BF16)

    head_spec = pl.BlockSpec((HEADS, tr, LANES), lambda i: (0, i, 0))
    head_shape = jax.ShapeDtypeStruct((HEADS, S, LANES), BF16)
    place_spec = pl.BlockSpec(place_q.shape, lambda i: (0, 0))
    return pl.pallas_call(
        body, name="in_proj", grid=(S // tr,),
        in_specs=[_row_spec(tr, D), pl.BlockSpec((D, D_IN_PAD), lambda i: (0, 0)), _vec_spec(LANES), place_spec, place_spec],
        out_specs=[head_spec] * 3 + [_row_spec(tr, D_POOL), _row_spec(tr, LANES)],
        out_shape=[head_shape] * 3 + [jax.ShapeDtypeStruct((S, D_POOL), F32), jax.ShapeDtypeStruct((S, LANES), F32)],
        scratch_shapes=[pltpu.VMEM((SUBLANES, LANES), F32)], compiler_params=_params(("arbitrary",)),
    )(h1, w_in, bf_pad, jnp.asarray(place_q, BF16), jnp.asarray(place_k, BF16))


def _hosted(ex, refs, n_blocked_in, n_blocked_out, first, forward_at, last):
    n = len(ex.ins)
    own_in = refs[:n_blocked_in]
    ex_in = refs[n_blocked_in:n_blocked_in + n]
    own_out = refs[n_blocked_in + n:n_blocked_in + n + n_blocked_out]
    ex_out = refs[n_blocked_in + n + n_blocked_out:n_blocked_in + 2 * n + n_blocked_out]
    rest = refs[n_blocked_in + 2 * n + n_blocked_out:]
    args = (ex_in, ex_out, rest[-2], rest[-1])

    def begin():
        @pl.when(first)
        def _():
            ex.start(*args)

        @pl.when(forward_at)
        def _():
            ex.forward(*args)

    def end():
        @pl.when(last)
        def _():
            ex.finish(*args)

    return own_in, own_out, rest[:-2], begin, end


def _fox_fwd(qa, ka, va, ex):
    BQ = BK = FOX_FWD_BLOCK
    nq = S // BQ
    n_pairs = HEADS // 2

    def body(*refs):
        p_id, i = pl.program_id(0), pl.program_id(1)
        (qa_ref, ka_ref, va_ref), (y_ref, qab_ref), (m_scr, acc_scr), begin, end = _hosted(
            ex, refs, 3, 2, (p_id == 0) & (i == 0), (p_id == n_pairs - 1) & (i == 0), (p_id == n_pairs - 1) & (i == nq - 1))
        begin()
        lane = _lane_iota((BQ, LANES))
        causal = _row_iota((BQ, BK)) >= _lane_iota((BQ, BK))
        m_scr[...] = jnp.full_like(m_scr, NEG)
        acc_scr[...] = jnp.zeros_like(acc_scr)

        def step(j, masked):
            rows = pl.ds(pl.multiple_of(j * BK, BK), BK)
            for hh in range(2):
                s = _dot(qa_ref[hh], ka_ref[hh, rows, :], NT)
                if masked:
                    s = jnp.where(causal, s, NEG)
                m_prev = m_scr[hh]
                m_new = jnp.maximum(m_prev, jnp.max(s, axis=1, keepdims=True))
                p = jnp.exp(s - jnp.tile(m_new, (1, BK // LANES)))
                acc_scr[hh] = jnp.exp(m_prev - m_new) * acc_scr[hh] + _dot(p.astype(BF16), va_ref[hh, rows, :])
                m_scr[hh] = m_new

        def full_step(j, carry):
            step(j, False)
            return carry

        lax.fori_loop(0, i, full_step, 0)
        step(i, True)
        outs = []
        for hh in range(2):
            acc = acc_scr[hh]
            den_lane, lse_lane = _at(DEN_V, hh), _at(LSE_Q, hh)
            den = jnp.broadcast_to(acc[:, den_lane:den_lane + 1], (BQ, LANES))
            outs.append(acc * (1.0 / den))
            n_hi, n_mid, n_lo = _split3(-(m_scr[hh] + jnp.log(den)))
            qab_ref[hh] = jnp.where(lane == lse_lane, n_hi,
                                    jnp.where(lane == lse_lane + 1, n_mid, jnp.where(lane == lse_lane + 2, n_lo, qa_ref[hh])))
        y_ref[...] = jnp.where(lane < DH, outs[0], outs[1]).astype(BF16)
        end()

    pair_rows = pl.BlockSpec((2, BQ, LANES), lambda p, i: (p, i, 0))
    pair_all = pl.BlockSpec((2, S, LANES), lambda p, i: (p, 0, 0))
    n = len(ex.ins)
    res = pl.pallas_call(
        body, name="fox_fwd", grid=(n_pairs, nq), in_specs=[pair_rows, pair_all, pair_all] + [ANY] * n,
        out_specs=[pl.BlockSpec((BQ, LANES), lambda p, i: (i, D_POOL // LANES + p)), pair_rows] + [ANY] * n,
        out_shape=[jax.ShapeDtypeStruct((S, D), BF16), jax.ShapeDtypeStruct((HEADS, S, LANES), BF16)] + ex.out_shapes,
        scratch_shapes=[pltpu.VMEM((2, BQ, LANES), F32), pltpu.VMEM((2, BQ, LANES), F32)] + ex.scratch(),
        compiler_params=_params(("arbitrary", "arbitrary")),
    )(qa, ka, va, *ex.ins)
    return res[0], res[1], res[2:]


def _bwd_xa_mix(dqx, w_xq, dres, x2, g_pre, y1, g_post, w_mix_out, ycat, ex):
    steps = S // TR
    n = len(ex.ins)

    def body(*refs):
        i = pl.program_id(0)
        ((dq_ref, wq_ref, dres_ref, x_ref, gpre_ref, y_ref, gpost_ref, wm_ref, ycat_ref),
         (dx_ref, dy_ref, dgpre_ref, dgpost_ref, dp_ref, doa_ref), _, begin, end) = _hosted(
            ex, refs, 9, 6, i == 0, i == 0, i == steps - 1)
        begin()

        @pl.when(i == 0)
        def _():
            dgpre_ref[...] = jnp.zeros_like(dgpre_ref)
            dgpost_ref[...] = jnp.zeros_like(dgpost_ref)

        dxn, dgpre = _rms_bwd(x_ref[...], gpre_ref[...], _dot(dq_ref[...], wq_ref[...], NT))
        dx = dres_ref[...] + dxn
        dx_ref[...] = dx
        dy, dgpost = _rms_bwd(y_ref[...], gpost_ref[...], dx)
        dy = dy.astype(BF16)
        dy_ref[...] = dy
        dgpre_ref[...] += dgpre
        dgpost_ref[...] += dgpost

        d = _dot(dy, wm_ref[...], NT)
        dp_ref[...] = d[:, :D_POOL]
        lane = _lane_iota((TR, LANES))
        low = lane < DH
        for p in range(HEADS // 2):
            cols = slice(D_POOL + LANES * p, D_POOL + LANES * (p + 1))
            do = d[:, cols]
            prod = do * ycat_ref[:, cols].astype(F32)
            deltas = (jnp.sum(jnp.where(low, prod, 0.0), axis=1, keepdims=True),
                      jnp.sum(jnp.where(low, 0.0, prod), axis=1, keepdims=True))
            for hh in range(2):
                d_hi, d_mid, d_lo = _split3_f32(deltas[hh])
                dl = _at(DELTA, hh)
                aug = jnp.where(lane == dl, d_hi, jnp.where(lane == dl + 1, d_mid, jnp.where(lane == dl + 2, d_lo, 0.0)))
                doa_ref[2 * p + hh] = jnp.where(_data_lanes(lane, hh), do, aug).astype(BF16)
        end()

    mat = pl.BlockSpec((D, D), lambda i: (0, 0))
    res = pl.pallas_call(
        body, name="bwd_xa_mix", grid=(steps,),
        in_specs=[_row_spec(TR, D), mat, _row_spec(TR, D), _row_spec(TR, D), _vec_spec(D), _row_spec(TR, D), _vec_spec(D), mat,
                  _row_spec(TR, D)] + [ANY] * n,
        out_specs=[_row_spec(TR, D), _row_spec(TR, D), _vec_spec(D), _vec_spec(D), _row_spec(TR, D_POOL),
                   pl.BlockSpec((HEADS, TR, LANES), lambda i: (0, i, 0))] + [ANY] * n,
        out_shape=[jax.ShapeDtypeStruct((S, D), F32), jax.ShapeDtypeStruct((S, D), BF16), jax.ShapeDtypeStruct((1, D), F32),
                   jax.ShapeDtypeStruct((1, D), F32), jax.ShapeDtypeStruct((S, D_POOL), F32),
                   jax.ShapeDtypeStruct((HEADS, S, LANES), BF16)] + ex.out_shapes,
        scratch_shapes=ex.scratch(), compiler_params=_params(("arbitrary",)),
    )(dqx, w_xq, dres, x2, g_pre, y1, g_post, w_mix_out, ycat, *ex.ins)
    return res[:6], res[6:]


def _fox_bwd(qab, doa, ka, va, ex):
    BQ = BK = FOX_BWD_BLOCK
    nk = S // BK
    n_pairs = HEADS // 2

    def body(*refs):
        p_id, j = pl.program_id(0), pl.program_id(1)
        (qab_ref, doa_ref, ka_ref, va_ref), (dqa_ref, dka_ref, dva_ref), _, begin, end = _hosted(
            ex, refs, 4, 3, (p_id == 0) & (j == 0), (p_id == n_pairs - 1) & (j == 0), (p_id == n_pairs - 1) & (j == nk - 1))
        begin()

        @pl.when(j == 0)
        def _():
            dqa_ref[...] = jnp.zeros_like(dqa_ref)

        causal = _row_iota((BQ, BK)) >= _lane_iota((BQ, BK))
        dka_ref[...] = jnp.zeros_like(dka_ref)
        dva_ref[...] = jnp.zeros_like(dva_ref)

        def step(i, masked):
            rows = pl.ds(pl.multiple_of(i * BQ, BQ), BQ)
            for hh in range(2):
                kb = ka_ref[hh]
                q = qab_ref[hh, rows, :]
                do = doa_ref[hh, rows, :]
                s = _dot(q, kb, NT)
                if masked:
                    s = jnp.where(causal, s, NEG)
                p = jnp.exp(s)
                ds = p * _dot(do, va_ref[hh], NT)
                pb = p.astype(BF16)
                dsb = ds.astype(BF16)
                dva_ref[hh] += _dot(pb, do, TN)
                dka_ref[hh] += _dot(dsb, q, TN)
                dqa_ref[hh, rows, :] += _dot(dsb, kb)

        def full_step(i, carry):
            step(i, False)
            return carry

        step(j, True)
        lax.fori_loop(j + 1, nk, full_step, 0)
        end()

    pair_all = pl.BlockSpec((2, S, LANES), lambda p, j: (p, 0, 0))
    pair_rows = pl.BlockSpec((2, BK, LANES), lambda p, j: (p, j, 0))
    shape = jax.ShapeDtypeStruct((HEADS, S, LANES), F32)
    n = len(ex.ins)
    res = pl.pallas_call(
        body, name="fox_bwd", grid=(n_pairs, nk), in_specs=[pair_all, pair_all, pair_rows, pair_rows] + [ANY] * n,
        out_specs=[pair_all, pair_rows, pair_rows] + [ANY] * n, out_shape=[shape] * 3 + ex.out_shapes,
        scratch_shapes=ex.scratch(), compiler_params=_params(("arbitrary", "arbitrary")),
    )(qab, doa, ka, va, *ex.ins)
    return res[0], res[1], res[2], res[3:]


def _fox_bwd_post(dqa, dka, dva, du, proj, bf_pad):
    tr = PREP_TR
    nt = S // tr

    pick = np.zeros((HEADS * LANES, LANES), np.float32)
    for h in range(HEADS):
        pick[LANES * h + _at(BOTH_ONE, h), h] = 1.0

    def body(dqa_ref, dka_ref, dva_ref, du_ref, z_ref, bf_ref, pick_ref, dp_ref, dbf_ref, carry_ref):
        i = pl.program_id(0)

        @pl.when(i == 0)
        def _():
            carry_ref[...] = jnp.zeros_like(carry_ref)
            dbf_ref[...] = jnp.zeros_like(dbf_ref)

        lane = _lane_iota((tr, LANES))
        diff = jnp.concatenate([dqa_ref[h] - dka_ref[h] for h in range(HEADS)], axis=1)
        hi = diff.astype(BF16)
        dcum = _dot(hi, pick_ref[...]) + _dot((diff - hi.astype(F32)).astype(BF16), pick_ref[...])
        tri =jnp.where(_lane_iota((tr, tr)) >= _row_iota((tr, tr)), 1.0, 0.0).astype(BF16)
        dlog_f = _cumsum_rows(dcum, tri, carry_ref[0:1, :])
        carry_ref[0:1, :] = dlog_f[0:1, :]
        z = z_ref[...] + bf_ref[...]
        df = jnp.where(lane < HEADS, dlog_f / (1.0 + jnp.exp(z)), 0.0)
        dbf_ref[...] += jnp.sum(df, axis=0, keepdims=True)

        dp_ref[:, 0:D_POOL] = du_ref[...].astype(BF16)
        low = lane < DH
        for ref, off, scale in ((dqa_ref, Q_OFF, DH ** -0.5), (dka_ref, K_OFF, 1.0), (dva_ref, V_OFF, 1.0)):
            for p in range(HEADS // 2):
                blk = jnp.where(low, ref[2 * p], ref[2 * p + 1])
                dp_ref[:, off + LANES * p:off + LANES * (p + 1)] = (blk * scale).astype(BF16)
        dp_ref[:, F_OFF:F_OFF + LANES] = df.astype(BF16)

    head_spec = pl.BlockSpec((HEADS, tr, LANES), lambda i: (0, nt - 1 - i, 0))
    return pl.pallas_call(
        body, name="fox_bwd_post", grid=(nt,),
        in_specs=[head_spec, head_spec, head_spec, pl.BlockSpec((tr, D_POOL), lambda i: (nt - 1 - i, 0)),
                  pl.BlockSpec((tr, LANES), lambda i: (nt - 1 - i, 0)), _vec_spec(LANES),
                  pl.BlockSpec(pick.shape, lambda i: (0, 0))],
        out_specs=[pl.BlockSpec((tr, D_IN_PAD), lambda i: (nt - 1 - i, 0)), _vec_spec(LANES)],
        out_shape=[jax.ShapeDtypeStruct((S, D_IN_PAD), BF16), jax.ShapeDtypeStruct((1, LANES), F32)],
        scratch_shapes=[pltpu.VMEM((SUBLANES, LANES), F32)],
        compiler_params=_params(("arbitrary",)),
    )(dqa, dka, dva, du, proj, bf_pad, jnp.asarray(pick, BF16))


POOL_HALO = 16


def _by_group(lane, a2, a4, a8, a16):
    return jnp.where(lane < 64, a2, jnp.where(lane < 128, a4, jnp.where(lane < 192, a8, a16)))


def _window_count(lane, t):
    return jnp.minimum(t + 1, _by_group(lane, 2, 4, 8, 16)).astype(F32)


def _pool_diff(u, halo, first, tile):
    n = TR + POOL_HALO
    ext = jnp.concatenate([jnp.where(first, 0.0, halo), u], axis=0)
    s2 = ext + pltpu.roll(ext, 1, 0)
    s4 = s2 + pltpu.roll(s2, 2, 0)
    s8 = s4 + pltpu.roll(s4, 4, 0)
    s16 = s8 + pltpu.roll(s8, 8, 0)
    lane = _lane_iota((n, D_POOL))
    win = _by_group(lane, s2, s4, s8, s16)[POOL_HALO:]
    lane = _lane_iota((TR, D_POOL))
    t = tile * TR + _row_iota((TR, D_POOL))
    return win / _window_count(lane, t) - u


def _prev_halo(rows, width, col):
    per = TR // rows
    return pl.BlockSpec((rows, width), lambda i: (jnp.maximum(i * per - 1, 0), col))


def _next_halo(rows, width, col):
    per = TR // rows
    return pl.BlockSpec((rows, width), lambda i: (jnp.minimum((i + 1) * per, S // rows - 1), col))


def _pool_fwd(proj, w_bd, ps, ycat):
    def body(u_ref, halo_ref, w_ref, ps_ref, ycat_ref, y_ref):
        i = pl.program_id(0)
        diff = _pool_diff(u_ref[...], halo_ref[...], i == 0, i)
        y_ref[...] = (_dot(diff.astype(BF16), w_ref[...]) * ps_ref[...]).astype(BF16)

    return pl.pallas_call(
        body, name="pool_fwd", grid=(S // TR,),
        in_specs=[_row_spec(TR, D_POOL), _prev_halo(POOL_HALO, D_POOL, 0),
                  pl.BlockSpec((D_POOL, D_POOL), lambda i: (0, 0)), _vec_spec(D_POOL), ANY],
        out_specs=_row_spec(TR, D_POOL), out_shape=jax.ShapeDtypeStruct((S, D), BF16), input_output_aliases={4: 0},
        compiler_params=_params(("parallel",)),
    )(proj, proj, w_bd, ps, ycat)


def _pool_bwd(proj, dycat, w_bd, w_bd_t, ps):
    nt = S // TR
    n = TR + POOL_HALO

    def body(u_ref, halo_ref, dy_ref, dyn_ref, w_ref, wt_ref, ps_ref, du_ref, dw_ref, dps_ref):
        i = pl.program_id(0)

        @pl.when(i == 0)
        def _():
            dw_ref[...] = jnp.zeros_like(dw_ref)
            dps_ref[...] = jnp.zeros_like(dps_ref)

        diff = _pool_diff(u_ref[...], halo_ref[...], i == 0, i).astype(BF16)
        dy = dy_ref[...]
        dps_ref[...] += jnp.sum(dy * _dot(diff, w_ref[...]), axis=0, keepdims=True)
        dy_ext = jnp.concatenate([dy, jnp.where(i == nt - 1, 0.0, dyn_ref[...])], axis=0)
        dmixed = (dy_ext * ps_ref[...]).astype(BF16)
        ddiff = _dot(dmixed, wt_ref[...])
        dw_ref[...] += _dot(diff, dmixed[:TR], TN)
        lane = _lane_iota((n, D_POOL))
        t = i * TR + _row_iota((n, D_POOL))
        e = ddiff / _window_count(lane, t)
        f2 = e + pltpu.roll(e, n - 1, 0)
        f4 = f2 + pltpu.roll(f2, n - 2, 0)
        f8 = f4 + pltpu.roll(f4, n - 4, 0)
        f16 = f8 + pltpu.roll(f8, n - 8, 0)
        du_ref[...] = _by_group(lane, f2, f4, f8, f16)[:TR] - ddiff[:TR]

    mat = pl.BlockSpec((D_POOL, D_POOL), lambda i: (0, 0))
    return pl.pallas_call(
        body, name="pool_bwd", grid=(nt,),
        in_specs=[_row_spec(TR, D_POOL), _prev_halo(POOL_HALO, D_POOL, 0), _row_spec(TR, D_POOL),
                  _next_halo(POOL_HALO, D_POOL, 0), mat, mat, _vec_spec(D_POOL)],
        out_specs=[_row_spec(TR, D_POOL), mat, _vec_spec(D_POOL)],
        out_shape=[jax.ShapeDtypeStruct((S, D_POOL), F32), jax.ShapeDtypeStruct((D_POOL, D_POOL), F32),
                   jax.ShapeDtypeStruct((1, D_POOL), F32)],
        compiler_params=_params(("arbitrary",)),
    )(proj, proj, dycat, dycat, w_bd, w_bd_t, ps)


def _xa_probs(q, k):
    s = _dot(q, k, NT) * (XA_DH ** -0.5)
    e = jnp.exp(s - jnp.max(s, axis=-1, keepdims=True))
    return e * (1.0 / jnp.sum(e, axis=-1, keepdims=True))


def _xattn_fwd(qx, kv):
    def body(q_ref, kv_ref, o_ref):
        for h in range(XA_HEADS):
            cols = slice(XA_DH * h, XA_DH * (h + 1))
            vcols = slice(D + XA_DH * h, D + XA_DH * (h + 1))
            p = _xa_probs(q_ref[:, cols], kv_ref[:, cols])
            o_ref[:, cols] = _dot(p.astype(BF16), kv_ref[:, vcols]).astype(BF16)

    return pl.pallas_call(
        body, name="xattn_fwd", grid=(S // TR,),
        in_specs=[_row_spec(TR, D), pl.BlockSpec((MEM, 2 * D), lambda i: (0, 0))],
        out_specs=_row_spec(TR, D), out_shape=jax.ShapeDtypeStruct((S, D), BF16),
        compiler_params=_params(("parallel",)),
    )(qx, kv)


def _xattn_bwd(qx, kv, dxo):
    def body(q_ref, kv_ref, do_ref, dq_ref, dkv_ref):
        i = pl.program_id(0)

        @pl.when(i == 0)
        def _():
            dkv_ref[...] = jnp.zeros_like(dkv_ref)

        for h in range(XA_HEADS):
            cols = slice(XA_DH * h, XA_DH * (h + 1))
            vcols = slice(D + XA_DH * h, D + XA_DH * (h + 1))
            q = q_ref[:, cols]
            k = kv_ref[:, cols]
            do = do_ref[:, cols]
            p = _xa_probs(q, k)
            dkv_ref[:, vcols] += _dot(p.astype(BF16), do, TN)
            dp = _dot(do, kv_ref[:, vcols], NT)
            ds = (p * (dp - jnp.sum(p * dp, axis=-1, keepdims=True)) * (XA_DH ** -0.5)).astype(BF16)
            dq_ref[:, cols] = _dot(ds, k).astype(BF16)
            dkv_ref[:, cols] += _dot(ds, q, TN)

    kv_spec = pl.BlockSpec((MEM, 2 * D), lambda i: (0, 0))
    return pl.pallas_call(
        body, name="xattn_bwd", grid=(S // TR,), in_specs=[_row_spec(TR, D), kv_spec, _row_spec(TR, D)],
        out_specs=[_row_spec(TR, D), kv_spec],
        out_shape=[jax.ShapeDtypeStruct((S, D), BF16), jax.ShapeDtypeStruct((MEM, 2 * D), F32)],
        compiler_params=_params(("arbitrary",)),
    )(qx, kv, dxo)


CONV_HALO = SUBLANES
TC = 512
GELU_K = 0.7978845608028654
GELU_C = 0.044715


def _conv3(ext, w, rows):
    h0 = ext[CONV_HALO:CONV_HALO + rows]
    h1 = pltpu.roll(ext, 1, 0)[CONV_HALO:CONV_HALO + rows]
    h2 = pltpu.roll(ext, 2, 0)[CONV_HALO:CONV_HALO + rows]
    return w[2:3] * h0 + w[1:2] * h1 + w[0:1] * h2 + w[3:4], (h2, h1, h0)


def _conv_specs():
    main = pl.BlockSpec((2, TR, TC), lambda j, i: (0, i, j))
    per = TR // CONV_HALO
    prev = pl.BlockSpec((2, CONV_HALO, TC), lambda j, i: (0, jnp.maximum(i * per - 1, 0), j))
    nxt = pl.BlockSpec((2, CONV_HALO, TC), lambda j, i: (0, jnp.minimum((i + 1) * per, S // CONV_HALO - 1), j))
    par = pl.BlockSpec((2, SUBLANES, TC), lambda j, i: (0, 0, j))
    return main, prev, nxt, par


def _convgate_fwd(hid, cwb):
    def body(h_ref, hp_ref, w_ref, act_ref):
        i = pl.program_id(1)
        c = []
        for g in range(2):
            ext = jnp.concatenate([jnp.where(i == 0, 0.0, hp_ref[g]), h_ref[g]], axis=0)
            c.append(_conv3(ext, w_ref[g], TR)[0])
        gate, up = c
        act_ref[...] = (jax.nn.gelu(gate, approximate=True) * up).astype(BF16)

    main, prev, _, par = _conv_specs()
    return pl.pallas_call(
        body, name="convgate_fwd", grid=(D_FF // TC, S // TR), in_specs=[main, prev, par],
        out_specs=pl.BlockSpec((TR, TC), lambda j, i: (i, j)), out_shape=jax.ShapeDtypeStruct((S, D_FF), BF16),
        compiler_params=_params(("parallel", "parallel")),
    )(hid, hid, cwb)


def _convgate_bwd(hid, dact, cwb):
    nr = S // TR
    n = TR + CONV_HALO

    def body(h_ref, hp_ref, hn_ref, da_ref, dan_ref, w_ref, dh_ref, dw_ref):
        i = pl.program_id(1)

        @pl.when(i == 0)
        def _():
            dw_ref[...] = jnp.zeros_like(dw_ref)

        da = jnp.concatenate([da_ref[...], jnp.where(i == nr - 1, 0.0, dan_ref[...])], axis=0)
        c, taps = [], []
        for g in range(2):
            ext = jnp.concatenate([jnp.where(i == 0, 0.0, hp_ref[g]), h_ref[g], hn_ref[g]], axis=0)
            cg, tg = _conv3(ext, w_ref[g], n)
            c.append(cg)
            taps.append(tg)
        gate, up = c
        th = jnp.tanh(GELU_K * (gate + GELU_C * gate * gate * gate))
        gelu = 0.5 * gate * (1.0 + th)
        dgelu = 0.5 * (1.0 + th) + 0.5 * gate * (1.0 - th * th) * GELU_K * (1.0 + 3.0 * GELU_C * gate * gate)
        for g, dc in enumerate((da * up * dgelu, da * gelu)):
            w = w_ref[g]
            dh = w[2:3] * dc[:TR] + w[1:2] * pltpu.roll(dc, n - 1, 0)[:TR] + w[0:1] * pltpu.roll(dc, n - 2, 0)[:TR]
            dh_ref[g] = dh.astype(BF16)
            dcm = dc[:TR]
            for r in range(3):
                dw_ref[g, r:r + 1, :] += jnp.sum(dcm * taps[g][r][:TR], axis=0, keepdims=True)
            dw_ref[g, 3:4, :] += jnp.sum(dcm, axis=0, keepdims=True)

    main, prev, nxt, par = _conv_specs()
    per = TR // CONV_HALO
    return pl.pallas_call(
        body, name="convgate_bwd", grid=(D_FF // TC, nr),
        in_specs=[main, prev, nxt, pl.BlockSpec((TR, TC), lambda j, i: (i, j)),
                  pl.BlockSpec((CONV_HALO, TC), lambda j, i: (jnp.minimum((i + 1) * per, S // CONV_HALO - 1), j)), par],
        out_specs=[main, par],
        out_shape=[jax.ShapeDtypeStruct((2, S, D_FF), BF16), jax.ShapeDtypeStruct((2, SUBLANES, D_FF), F32)],
        compiler_params=_params(("parallel", "arbitrary")),
    )(hid, hid, hid, dact, dact, cwb)


def _adam_update(w, g, m, v):
    m = ADAM_B1 * m + (1.0 - ADAM_B1) * g
    v = ADAM_B2 * v + (1.0 - ADAM_B2) * (g * g)
    m_hat = m / (1.0 - ADAM_B1 ** ADAM_STEP)
    v_hat = v / (1.0 - ADAM_B2 ** ADAM_STEP)
    return -ADAM_LR * (m_hat / (jnp.sqrt(v_hat) + ADAM_EPS) + ADAM_WD * w), m, v


def _row_tile(rows, cols, itemsize=4, target=TILE_BYTES):
    tr = SUBLANES
    while rows % (2 * tr) == 0 and 2 * tr * cols * itemsize <= target:
        tr *= 2
    assert rows % tr == 0, (rows, tr)
    return tr


def _adamw(name, w, g, m, v):
    rows, cols = w.shape
    tr = rows if rows * cols * 4 <= TILE_BYTES // 2 else _row_tile(rows, cols, target=TILE_BYTES // 2)

    def body(w_ref, g_ref, m_ref, v_ref, d_ref, nm_ref, nv_ref):
        d_ref[...], nm_ref[...], nv_ref[...] = _adam_update(w_ref[...], g_ref[...], m_ref[...], v_ref[...])

    spec = _row_spec(tr, cols)
    shape = jax.ShapeDtypeStruct((rows, cols), F32)
    return pl.pallas_call(
        body, name=name, grid=(rows // tr,), in_specs=[spec] * 4, out_specs=[spec] * 3, out_shape=[shape] * 3,
        compiler_params=_params(("parallel",)),
    )(w, g, m, v)


def _adamw_halves(name, core, w, g_mine, g_sibling, m, v):
    rows, cols = w.shape
    half = rows // 2
    tr = _row_tile(half, cols, target=TILE_BYTES // 2)
    per = half // tr

    def body(core_ref, w_ref, gm_ref, gs_ref, m_ref, v_ref, g_ref, d_ref, nm_ref, nv_ref):
        g = jnp.where(pl.program_id(0) // per == core_ref[0], gm_ref[...], gs_ref[...])
        g_ref[...] = g
        d_ref[...], nm_ref[...], nv_ref[...] = _adam_update(w_ref[...], g, m_ref[...], v_ref[...])

    spec = pl.BlockSpec((tr, cols), lambda i, core_ref: (i, 0))
    half_spec = pl.BlockSpec((tr, cols), lambda i, core_ref: (i % per, 0))
    shape = jax.ShapeDtypeStruct((rows, cols), F32)
    return pl.pallas_call(
        body, name=name, out_shape=[shape] * 4,
        grid_spec=pltpu.PrefetchScalarGridSpec(
            num_scalar_prefetch=1, grid=(rows // tr,), in_specs=[spec, half_spec, half_spec, spec, spec], out_specs=[spec] * 4),
        compiler_params=_params(("parallel",)),
    )(core, w, g_mine, g_sibling, m, v)


def _chip_sum(name, core, g, other):
    _, _, half, cols = g.shape
    tr = _row_tile(half, cols)

    def body(core_ref, g_ref, o_ref, p_ref):
        p_ref[...] = (g_ref[...] + o_ref[...]).astype(BF16)

    spec = pl.BlockSpec((None, tr, cols), lambda j, i, core_ref: (j, i, 0))
    return pl.pallas_call(
        body, name=name, out_shape=jax.ShapeDtypeStruct((N_CHIPS, half, cols), BF16),
        grid_spec=pltpu.PrefetchScalarGridSpec(
            num_scalar_prefetch=1, grid=(N_CHIPS, half // tr),
            in_specs=[pl.BlockSpec((None, None, tr, cols), lambda j, i, core_ref: (j, core_ref[0], i, 0)), spec],
            out_specs=spec),
        compiler_params=_params(("parallel", "parallel")),
    )(core, g, other)


def _mesh_sum(name, chip, received, own):
    _, half, cols = received.shape
    tr = _row_tile(half, cols, itemsize=2 * N_CHIPS)

    def body(chip_ref, r_ref, own_ref, o_ref):
        acc = None
        for j in range(N_CHIPS):
            term = jnp.where(chip_ref[0] == j, own_ref[...], r_ref[j]).astype(F32)
            acc = term if acc is None else acc + term
        o_ref[...] = acc

    return pl.pallas_call(
        body, name=name, out_shape=jax.ShapeDtypeStruct((half, cols), F32),
        grid_spec=pltpu.PrefetchScalarGridSpec(
            num_scalar_prefetch=1, grid=(half // tr,),
            in_specs=[pl.BlockSpec((N_CHIPS, tr, cols), lambda i, chip_ref: (0, i, 0)),
                      pl.BlockSpec((None, tr, cols), lambda i, chip_ref: (chip_ref[0], i, 0))],
            out_specs=pl.BlockSpec((tr, cols), lambda i, chip_ref: (i, 0))),
        compiler_params=_params(("parallel",)),
    )(chip, received, own)


CHIP_FLIPS = ((1, 0), (0, 1), (1, 1))


def _place():
    x, y, c = lax.axis_index("x"), lax.axis_index("y"), lax.axis_index("c")
    return x, y, c, 2 * x + y


def _remote(src, dst, sems_s, sems_r, k, dev):
    return pltpu.make_async_remote_copy(src_ref=src, dst_ref=dst, send_sem=sems_s.at[k], recv_sem=sems_r.at[k],
                                        device_id=dev, device_id_type=MESH)


class _Exchange:
    def __init__(self, ins, out_shapes, n_sems, start, forward, finish):
        self.ins, self.out_shapes, self.n_sems = list(ins), list(out_shapes), n_sems
        self.start, self.forward, self.finish = start, forward, finish

    def scratch(self):
        return [pltpu.SemaphoreType.DMA((self.n_sems,)), pltpu.SemaphoreType.DMA((self.n_sems,))]

    def run(self, name):
        n = len(self.ins)

        def body(*refs):
            args = (refs[:n], refs[n:2 * n]) + tuple(refs[2 * n:])
            self.start(*args)
            self.forward(*args)
            self.finish(*args)

        return pl.pallas_call(
            body, name=name, in_specs=[ANY] * n, out_specs=[ANY] * n, out_shape=self.out_shapes, scratch_shapes=self.scratch(),
        )(*self.ins)


def _all_gather_weights(halved, whole):
    nh, nw = len(halved), len(whole)
    n_arr = nh + nw

    def copies(ins, outs, sems_s, sems_r):
        x, y, c, me = _place()
        sibling = (x, y, 1 - c)
        own = [_remote(ins[k], outs[k].at[me], sems_s, sems_r, k, sibling) for k in range(n_arr)]
        first, passed = [], []
        for k in range(n_arr):
            for f, (fx, fy) in enumerate(CHIP_FLIPS):
                src, dst = (ins[k].at[c], outs[k].at[me, c]) if k < nh else (ins[k], outs[k].at[me])
                first.append(_remote(src, dst, sems_s, sems_r, n_arr + 3 * k + f, (x ^ fx, y ^ fy, c)))
        for k in range(nh):
            for f, (fx, fy) in enumerate(CHIP_FLIPS):
                landed = outs[k].at[2 * (x ^ fx) + (y ^ fy), c]
                passed.append(_remote(landed, landed, sems_s, sems_r, 4 * n_arr + 3 * k + f, sibling))
        return own, first, passed

    def start(*refs):
        own, first, _ = copies(*refs)
        for cp in own + first:
            cp.start()

    def forward(*refs):
        _, first, passed = copies(*refs)
        for arrived, cp in zip(first, passed):
            arrived.wait_recv()
            cp.start()

    def finish(*refs):
        own, first, passed = copies(*refs)
        for cp in first[3 * nh:] + passed + own:
            cp.wait_recv()
        for cp in first + passed + own:
            cp.wait_send()

    shapes = [jax.ShapeDtypeStruct((N_CHIPS,) + a.shape, a.dtype) for a in list(halved) + list(whole)]
    return _Exchange(list(halved) + list(whole), shapes, 7 * nh + 4 * nw, start, forward, finish)


def _swap_halves(gs):
    n = len(gs)

    def copies(ins, outs, sems_s, sems_r):
        x, y, c, _ = _place()
        return [_remote(ins[k].at[:, 1 - c], outs[k], sems_s, sems_r, k, (x, y, 1 - c)) for k in range(n)]

    def start(*refs):
        for cp in copies(*refs):
            cp.start()

    def finish(*refs):
        for cp in copies(*refs):
            cp.wait()

    shapes = [jax.ShapeDtypeStruct((g.shape[0],) + g.shape[2:], g.dtype) for g in gs]
    return _Exchange(gs, shapes, n, start, _no_copies, finish)


def _scatter_chips(ps):
    n = len(ps)

    def copies(ins, outs, sems_s, sems_r):
        x, y, c, me = _place()
        return [_remote(ins[k].at[2 * (x ^ fx) + (y ^ fy)], outs[k].at[me], sems_s, sems_r, 3 * k + f, (x ^ fx, y ^ fy, c))
                for k in range(n) for f, (fx, fy) in enumerate(CHIP_FLIPS)]

    def start(*refs):
        for cp in copies(*refs):
            cp.start()

    def forward(*refs):
        pass

    def finish(*refs):
        for cp in copies(*refs):
            cp.wait()

    shapes = [jax.ShapeDtypeStruct(p.shape, p.dtype) for p in ps]
    return _Exchange(ps, shapes, 3 * n, start, forward, finish)


def _swap_reduced(rs):
    n = len(rs)

    def copies(ins, outs, sems_s, sems_r):
        x, y, c, _ = _place()
        return [_remote(ins[k], outs[k], sems_s, sems_r, k, (x, y, 1 - c)) for k in range(n)]

    def start(*refs):
        for cp in copies(*refs):
            cp.start()

    def finish(*refs):
        for cp in copies(*refs):
            cp.wait()

    return _Exchange(rs, [jax.ShapeDtypeStruct(r.shape, r.dtype) for r in rs], n, start, _no_copies, finish)


N_DEV = 8


def _gather_small(buf):
    def copies(ins, outs, sems_s, sems_r):
        x, y, c, _ = _place()
        me = 4 * x + 2 * y + c
        return [_remote(ins[0], outs[0].at[me], sems_s, sems_r, o - 1, (x ^ (o >> 2), y ^ ((o >> 1) & 1), c ^ (o & 1)))
                for o in range(1, N_DEV)]

    def start(*refs):
        for cp in copies(*refs):
            cp.start()

    def finish(*refs):
        for cp in copies(*refs):
            cp.wait()

    return _Exchange([buf], [jax.ShapeDtypeStruct((N_DEV,) + buf.shape, buf.dtype)], N_DEV - 1, start, _no_copies, finish)


def _sum_devices(place, gathered, own):
    rows = own.shape[0]

    def body(place_ref, g_ref, own_ref, o_ref):
        acc = None
        for d in range(N_DEV):
            term = jnp.where(place_ref[0] == d, own_ref[...], g_ref[d])
            acc = term if acc is None else acc + term
        o_ref[...] = acc

    return pl.pallas_call(
        body, name="sum_devices", out_shape=jax.ShapeDtypeStruct((rows, LANES), F32),
        grid_spec=pltpu.PrefetchScalarGridSpec(
            num_scalar_prefetch=1, grid=(1,),
            in_specs=[pl.BlockSpec((N_DEV, rows, LANES), lambda i, place_ref: (0, 0, 0)),
                      pl.BlockSpec((rows, LANES), lambda i, place_ref: (0, 0))],
            out_specs=pl.BlockSpec((rows, LANES), lambda i, place_ref: (0, 0))),
        compiler_params=_params(("arbitrary",)),
    )(place, gathered, own)


def _no_copies(*refs):
    pass


def _no_exchange():
    return _Exchange([], [], 1, _no_copies, _no_copies, _no_copies)


class _NoComm:
    def gather_first(self):
        return _no_exchange()

    def first_landed(self, p, landed):
        pass

    def gather_rest(self, p):
        return _no_exchange()

    def weights_landed(self, p, landed):
        pass

    def swap_first(self, g):
        return _no_exchange()

    def first_swapped(self, landed):
        pass

    def swap_second(self, g):
        return _no_exchange()

    def second_swapped(self, landed):
        pass

    def scatter_early(self, g):
        return _no_exchange()

    def scatter_landed(self, landed):
        pass

    def swap_reduced_early(self):
        return _no_exchange()

    def reduced_landed(self, landed):
        pass

    def scatter_late(self, g):
        return _no_exchange()

    def late_landed(self, landed):
        pass


def _local_step(x, mem, target, p, comm):
    h1, landed = _norm_fwd("norm_mix_pre", x, p["norm_mix_pre"], comm.gather_first())
    comm.first_landed(p, landed)
    qa, ka, va, u, z = _in_proj(h1, p["w_in"], p["bf_pad"])
    ycat, qab, landed = _fox_fwd(qa, ka, va, comm.gather_rest(p))
    comm.weights_landed(p, landed)
    ycat = _pool_fwd(u, p["w_pool_bd"], p["pool_scale"], ycat)
    y1, x2, h2, qx = _proj_resid_norm("mix_out", ycat, p["w_mix_out"], x, p["norm_mix_post"], p["norm_xa_pre"], p["w_xq"])
    mem_n = _norm_fwd("norm_mem", mem, p["norm_mem"])
    kv = _mm(
        "xkv", mem_n, p["w_xkv"], pl.BlockSpec((MEM, D), lambda i, j, k: (0, 0)),
        pl.BlockSpec((None, D, 512), lambda i, j, k: (j, 0, 0)), jax.ShapeDtypeStruct((MEM, 2 * D), BF16),
        pl.BlockSpec((MEM, 512), lambda i, j, k: (0, j)), (1, N_CHIPS, 1), NN, (MEM, 512))
    xo = _xattn_fwd(qx, kv)
    y2, x3, h3 = _proj_resid_norm("xo", xo, p["w_xo"], x2, p["norm_xa_post"], p["norm_ffn_pre"])
    hid = _mm(
        "up_proj", h3, p["w_up"], pl.BlockSpec((1024, D), lambda i, j, k: (i, 0)),
        pl.BlockSpec((None, D, 1024), lambda i, j, k: (j // 2, 0, j % 2)), jax.ShapeDtypeStruct((2, S, D_FF), F32),
        pl.BlockSpec((None, 1024, 1024), lambda i, j, k: (j // 4, i, j % 4)), (S // 1024, 8, 1), NN, (1024, 1024))
    act = _convgate_fwd(hid, p["cwb"])

    g = {}
    dres, dy3, g["norm_ffn_post"], loss_cols = _down_loss_bwd(act, p["w_down"], x3, p["norm_ffn_post"], target)
    dact = _mm_nt("d_act", dy3, p["w_down"], F32, 1024, 1024)
    g["w_down"] = _mm_tn("dw_down", act, dy3, 512, 512)
    dhid, dcwb = _convgate_bwd(hid, dact, p["cwb"])
    g["w_up"] = _mm(
        "dw_up", h3, dhid, pl.BlockSpec((S, 512), lambda i, j, k: (0, i)),
        pl.BlockSpec((None, S, 512), lambda i, j, k: (j // 8, 0, j % 8)), jax.ShapeDtypeStruct((N_CHIPS, D, 2048), F32),
        pl.BlockSpec((None, 512, 512), lambda i, j, k: (j // 4, i, j % 4)), (2, 16, 1), TN, (512, 512))
    dh3, landed = _d_h3(dhid, p["w_up"], comm.swap_first(g))
    comm.first_swapped(landed)
    dres, dy2, dxo, g["norm_ffn_pre"], g["norm_xa_post"] = _mid_bwd(
        "bwd_ffn_xa", dres, x3, p["norm_ffn_pre"], dh3, y2, p["norm_xa_post"], p["w_xo"])
    g["w_xo"] = _mm_tn("dw_xo", xo, dy2, 512, 512)
    dqx, dkv = _xattn_bwd(qx, kv, dxo)
    dkv = dkv.astype(BF16)
    g["w_xq"] = _mm_tn("dw_xq", h2, dqx, 512, 512)
    dmem_n = _mm(
        "d_mem", dkv, p["w_xkv"], pl.BlockSpec((MEM, 512), lambda i, j, k: (0, k)),
        pl.BlockSpec((None, D, 512), lambda i, j, k: (k, 0, 0)), jax.ShapeDtypeStruct((MEM, D), F32),
        pl.BlockSpec((MEM, D), lambda i, j, k: (0, 0)), (1, 1, N_CHIPS), NT, (MEM, D))
    g["w_xkv"] = _mm(
        "dw_xkv", mem_n, dkv, pl.BlockSpec((MEM, D), lambda i, j, k: (0, 0)),
        pl.BlockSpec((MEM, 512), lambda i, j, k: (0, j)), jax.ShapeDtypeStruct((N_CHIPS, D, 512), F32),
        pl.BlockSpec((None, D, 512), lambda i, j, k: (j, 0, 0)), (1, N_CHIPS, 1), TN, (D, 512))
    g["norm_mem"] = _gain_bwd("dg_mem", mem, p["norm_mem"], dmem_n)
    (dres, dy1, g["norm_xa_pre"], g["norm_mix_post"], dy_pool, doa), landed = _bwd_xa_mix(
        dqx, p["w_xq"], dres, x2, p["norm_xa_pre"], y1, p["norm_mix_post"], p["w_mix_out"], ycat, comm.swap_second(g))
    comm.second_swapped(landed)
    g["w_mix_out"] = _mm_tn("dw_mix_out", ycat, dy1, 512, 512)
    dqa, dka, dva, landed = _fox_bwd(qab, doa, ka, va, comm.scatter_early(g))
    comm.scatter_landed(landed)
    du, g["w_pool_full"], g["pool_scale"] = _pool_bwd(u, dy_pool, p["w_pool_bd"], p["w_pool_bd_t"], p["pool_scale"])
    dproj, g["bf_pad"] = _fox_bwd_post(dqa, dka, dva, du, z, p["bf_pad"])
    g["w_in"], landed = _mm_tn("dw_in", h1, dproj, 512, 896, comm.swap_reduced_early())
    comm.reduced_landed(landed)
    dh1, landed = _mm_nt("d_h1", dproj, p["w_in"], F32, 1024, 1024, comm.scatter_late(g))
    comm.late_landed(landed)
    grad_x, g["norm_mix_pre"] = _first_bwd(dres, x, p["norm_mix_pre"], dh1)
    g["cwb"] = dcwb
    return grad_x, g, loss_cols


BIG = ("w_in", "w_mix_out", "w_xq", "w_xkv", "w_xo", "w_up", "w_down")
ROW_SHARDED = ("w_mix_out", "w_xq", "w_xo", "w_down")
SMALL = ("norm_mix_pre", "norm_mix_post", "b_forget", "w_pool", "pool_scale", "norm_mem", "norm_xa_pre", "norm_xa_post",
         "norm_ffn_pre", "norm_ffn_post", "conv_b")
ORDER = ("norm_mix_pre", "norm_mix_post", "w_in", "b_forget", "w_pool", "pool_scale", "w_mix_out", "norm_mem", "norm_xa_pre",
         "norm_xa_post", "w_xq", "w_xkv", "w_xo", "norm_ffn_pre", "norm_ffn_post", "w_up", "conv_w", "conv_b", "w_down")
SLOT = SUBLANES * LANES


def _pack(parts):
    rows, offs, off = [], [], 0
    for a in parts:
        flat = a.reshape(-1).astype(F32)
        n = -(-flat.shape[0] // SLOT) * SLOT
        rows.append(jnp.pad(flat, (0, n - flat.shape[0])).reshape(n // LANES, LANES))
        offs.append(off)
        off += n // LANES
    return jnp.concatenate(rows, axis=0), offs


def _unpack(buf, off, like):
    n = like.size
    rows = -(-n // LANES)
    return buf[off:off + rows].reshape(-1)[:n].reshape(like.shape)


FIRST = ("w_in",)
REST = ("w_mix_out", "w_xq", "w_xkv", "w_xo", "w_up", "w_down")


def _local_params(w):
    w_pool_bd = jnp.zeros((D_POOL, D_POOL), F32)
    for gi in range(4):
        w_pool_bd = w_pool_bd.at[64 * gi:64 * (gi + 1), 64 * gi:64 * (gi + 1)].set(w["w_pool"][0, gi])
    p = {n: w[n] for n in ("norm_mix_pre", "norm_mix_post", "norm_mem", "norm_xa_pre", "norm_xa_post", "norm_ffn_pre",
                           "norm_ffn_post")}
    p.update(
        bf_pad=jnp.pad(w["b_forget"], ((0, 0), (0, LANES - HEADS))),
        w_pool_bd=w_pool_bd.astype(BF16), w_pool_bd_t=w_pool_bd.T.astype(BF16), pool_scale=w["pool_scale"].reshape(1, D_POOL))
    return p


def _w_in_param(stacked):
    return jnp.pad(jnp.concatenate(list(stacked), axis=1), ((0, 0), (0, D_IN_PAD - D_IN)))


def _rest_params(w, full, conv_w_full):
    cw2 = conv_w_full.reshape(3, 2, D_FF).transpose(1, 0, 2)
    cwb = jnp.concatenate([cw2, w["conv_b"].reshape(1, 2, D_FF).transpose(1, 0, 2), jnp.zeros((2, 4, D_FF), F32)], axis=1)
    return dict(w_mix_out=full["w_mix_out"].reshape(D, D), w_xq=full["w_xq"].reshape(D, D), w_xkv=full["w_xkv"],
                w_xo=full["w_xo"].reshape(D, D), w_up=full["w_up"], cwb=cwb, w_down=full["w_down"].reshape(D_FF, D))


def _whole_params(w, full, conv_w_full):
    p = _local_params(w)
    p.update(_rest_params(w, full, conv_w_full), w_in=_w_in_param(full["w_in"]))
    return p


def _halved(a):
    return a.reshape(a.shape[:-2] + (2, a.shape[-2] // 2, a.shape[-1]))


class _StepComm:
    def __init__(self, w, shard2d, conv_w, core_id, chip_id):
        self.w, self.shard2d, self.conv_w, self.core_id, self.chip_id = w, shard2d, conv_w, core_id, chip_id
        self.first, self.second = ("w_up", "w_down"), ("w_xq", "w_xkv", "w_xo")
        self.early = self.first + self.second
        self.late = ("w_in", "w_mix_out")

    def gather_first(self):
        return _all_gather_weights([_halved(self.shard2d[n].astype(BF16)) for n in FIRST], [])

    def first_landed(self, p, landed):
        p["w_in"] = _w_in_param(landed[0].reshape((N_CHIPS,) + self.shard2d["w_in"].shape))

    def gather_rest(self, p):
        return _all_gather_weights([_halved(self.shard2d[n].astype(BF16)) for n in REST], [self.conv_w.reshape(3, -1)])

    def weights_landed(self, p, landed):
        full = {n: a.reshape((N_CHIPS,) + self.shard2d[n].shape) for n, a in zip(REST, landed)}
        conv_w_full = jnp.transpose(landed[-1], (1, 0, 2)).reshape(3, 2 * D_FF)
        p.update(_rest_params(self.w, full, conv_w_full))

    def _view(self, g, n):
        return _halved(g[n].reshape((N_CHIPS,) + self.shard2d[n].shape))

    def swap_first(self, g):
        return _swap_halves([self._view(g, n) for n in self.first])

    def first_swapped(self, landed):
        self.from_sibling = dict(zip(self.first, landed))

    def swap_second(self, g):
        return _swap_halves([self._view(g, n) for n in self.second])

    def second_swapped(self, landed):
        self.from_sibling.update(zip(self.second, landed))

    def scatter_early(self, g):
        self.partial = [_chip_sum("chip_sum_" + n, self.core_id, self._view(g, n), self.from_sibling[n]) for n in self.early]
        return _scatter_chips(self.partial)

    def scatter_landed(self, landed):
        self.received = list(landed)

    def swap_reduced_early(self):
        self.reduced = [_mesh_sum("mesh_sum_" + n, self.chip_id, r, own)
                        for n, r, own in zip(self.early, self.received, self.partial)]
        return _swap_reduced(self.reduced)

    def reduced_landed(self, landed):
        self.reduced_sibling = list(landed)

    def scatter_late(self, g):
        gw_in = g["w_in"][:, :D_IN]
        cols = D_IN // N_CHIPS
        views = [_halved(jnp.stack([gw_in[:, cols * j:cols * (j + 1)] for j in range(N_CHIPS)])), self._view(g, "w_mix_out")]
        from_sibling = _swap_halves(views).run("swap_halves_late")
        self.partial_late = [_chip_sum("chip_sum_" + n, self.core_id, view, other)
                             for n, view, other in zip(self.late, views, from_sibling)]
        return _scatter_chips(self.partial_late)

    def late_landed(self, landed):
        self.received_late = list(landed)


def kernel(x, mem, norm_mix_pre, norm_mix_post, w_in, b_forget, w_pool, pool_scale, w_mix_out, norm_mem, norm_xa_pre, norm_xa_post, w_xq, w_xkv, w_xo, norm_ffn_pre, norm_ffn_post, w_up, conv_w, conv_b, w_down, loss_target, m_norm_mix_pre, m_norm_mix_post, m_w_in, m_b_forget, m_w_pool, m_pool_scale, m_w_mix_out, m_norm_mem, m_norm_xa_pre, m_norm_xa_post, m_w_xq, m_w_xkv, m_w_xo, m_norm_ffn_pre, m_norm_ffn_post, m_w_up, m_conv_w, m_conv_b, m_w_down, v_norm_mix_pre, v_norm_mix_post, v_w_in, v_b_forget, v_w_pool, v_pool_scale, v_w_mix_out, v_norm_mem, v_norm_xa_pre, v_norm_xa_post, v_w_xq, v_w_xkv, v_w_xo, v_norm_ffn_pre, v_norm_ffn_post, v_w_up, v_conv_w, v_conv_b, v_w_down):
    w = dict(norm_mix_pre=norm_mix_pre, norm_mix_post=norm_mix_post, w_in=w_in, b_forget=b_forget, w_pool=w_pool,
             pool_scale=pool_scale, w_mix_out=w_mix_out, norm_mem=norm_mem, norm_xa_pre=norm_xa_pre, norm_xa_post=norm_xa_post,
             w_xq=w_xq, w_xkv=w_xkv, w_xo=w_xo, norm_ffn_pre=norm_ffn_pre, norm_ffn_post=norm_ffn_post, w_up=w_up,
             conv_w=conv_w, conv_b=conv_b, w_down=w_down)
    m = dict(norm_mix_pre=m_norm_mix_pre, norm_mix_post=m_norm_mix_post, w_in=m_w_in, b_forget=m_b_forget, w_pool=m_w_pool,
             pool_scale=m_pool_scale, w_mix_out=m_w_mix_out, norm_mem=m_norm_mem, norm_xa_pre=m_norm_xa_pre,
             norm_xa_post=m_norm_xa_post, w_xq=m_w_xq, w_xkv=m_w_xkv, w_xo=m_w_xo, norm_ffn_pre=m_norm_ffn_pre,
             norm_ffn_post=m_norm_ffn_post, w_up=m_w_up, conv_w=m_conv_w, conv_b=m_conv_b, w_down=m_w_down)
    v = dict(norm_mix_pre=v_norm_mix_pre, norm_mix_post=v_norm_mix_post, w_in=v_w_in, b_forget=v_b_forget, w_pool=v_w_pool,
             pool_scale=v_pool_scale, w_mix_out=v_w_mix_out, norm_mem=v_norm_mem, norm_xa_pre=v_norm_xa_pre,
             norm_xa_post=v_norm_xa_post, w_xq=v_w_xq, w_xkv=v_w_xkv, w_xo=v_w_xo, norm_ffn_pre=v_norm_ffn_pre,
             norm_ffn_post=v_norm_ffn_post, w_up=v_w_up, conv_w=v_conv_w, conv_b=v_conv_b, w_down=v_w_down)
    chip = 2 * lax.axis_index("x") + lax.axis_index("y")

    core_id = lax.axis_index("c").astype(jnp.int32).reshape(1)
    chip_id = chip.astype(jnp.int32).reshape(1)

    shard2d = {n: w[n][0] for n in BIG}
    p = _local_params(w)
    comm = _StepComm(w, shard2d, conv_w, core_id, chip_id)
    grad_x, g, loss_cols = _local_step(x[0], mem[0], loss_target[0], p, comm)

    reduced_late = [_mesh_sum("mesh_sum_" + n, chip_id, r, own)
                    for n, r, own in zip(comm.late, comm.received_late, comm.partial_late)]
    names = comm.late + comm.early
    reduced = reduced_late + comm.reduced
    reduced_sibling = list(_swap_reduced(reduced_late).run("swap_reduced_late")) + comm.reduced_sibling
    grads = {}

    gw_pool = jnp.stack([g["w_pool_full"][64 * gi:64 * (gi + 1), 64 * gi:64 * (gi + 1)] for gi in range(4)])
    dcwb = g["cwb"]
    g_conv_w = dcwb[:, 0:3, :].transpose(1, 0, 2).reshape(3, 2 * D_FF)
    g_conv_b = dcwb[:, 3, :].reshape(2 * D_FF)
    small_g = dict(norm_mix_pre=g["norm_mix_pre"], norm_mix_post=g["norm_mix_post"], b_forget=g["bf_pad"][:, :HEADS],
                   w_pool=gw_pool, pool_scale=g["pool_scale"], norm_mem=g["norm_mem"], norm_xa_pre=g["norm_xa_pre"],
                   norm_xa_post=g["norm_xa_post"], norm_ffn_pre=g["norm_ffn_pre"], norm_ffn_post=g["norm_ffn_post"],
                   conv_b=g_conv_b)
    local_buf, offs = _pack([small_g[n] for n in SMALL] + [g_conv_w, loss_cols])

    delta, new_m, new_v = {}, {}, {}
    for n, g_mine, g_sibling in zip(names, reduced, reduced_sibling):
        gn, d, nm, nv = _adamw_halves("adamw_" + n, core_id, shard2d[n], g_mine, g_sibling, m[n][0], v[n][0])
        grads[n], delta[n], new_m[n], new_v[n] = gn[None], d[None], nm[None], nv[None]
    place = (2 * chip + lax.axis_index("c")).astype(jnp.int32).reshape(1)
    buf = _sum_devices(place, _gather_small(local_buf).run("gather_small")[0], local_buf)
    for n, off in zip(SMALL, offs):
        grads[n] = _unpack(buf, off, w[n])
    g_conv_w = _unpack(buf, offs[len(SMALL)], g_conv_w)
    grads["conv_w"] = lax.dynamic_slice_in_dim(g_conv_w, chip * (2 * D_FF // N_CHIPS), 2 * D_FF // N_CHIPS, axis=1).reshape(conv_w.shape)
    loss = jnp.sum(_unpack(buf, offs[len(SMALL) + 1], loss_cols))
    small_names = SMALL + ("conv_w",)
    packed = [_pack([d[n] for n in small_names])[0] for d in (w, grads, m, v)]
    offs = _pack([w[n] for n in small_names])[1]
    d, nm, nv = _adamw("adamw_small", *packed)
    for n, off in zip(small_names, offs):
        delta[n], new_m[n], new_v[n] = _unpack(d, off, w[n]), _unpack(nm, off, w[n]), _unpack(nv, off, w[n])

    return (loss, grad_x[None], *[grads[n] for n in ORDER], *[delta[n] for n in ORDER], *[new_m[n] for n in ORDER],
            *[new_v[n] for n in ORDER])
```

```python
import functools

import jax
import jax.numpy as jnp
import numpy as np
from jax import lax
from jax.experimental import pallas as pl
from jax.experimental.pallas import tpu as pltpu

F32 = jnp.float32
BF16 = jnp.bfloat16
MESH = pl.DeviceIdType.MESH
ANY = pl.BlockSpec(memory_space=pl.ANY)
VMEM_SPEC = pl.BlockSpec(memory_space=pltpu.VMEM)

S = 4096
D = 1024
MEM = 256
D_POOL = 256
HEADS = 12
DH = 64
D_FOX = HEADS * DH
D_IN = D_POOL + 3 * D_FOX + HEADS
F_OFF = D_POOL + 3 * D_FOX
Q_OFF, K_OFF, V_OFF = D_POOL, D_POOL + D_FOX, D_POOL + 2 * D_FOX
XA_HEADS = 4
XA_DH = 256
D_FF = 4096
EPS = 1e-6
N_CHIPS = 4
ADAM_LR, ADAM_B1, ADAM_B2, ADAM_EPS, ADAM_WD, ADAM_STEP = 0.001, 0.9, 0.999, 1e-08, 0.01, 10

LANES = 128
SUBLANES = 8
D_IN_PAD = 21 * LANES
TR = 512
TILE_BYTES = 2 * 1024 * 1024
NEG = -1e30
VMEM_LIMIT = 52 * 1024 * 1024

NN = (((1,), (0,)), ((), ()))
NT = (((1,), (1,)), ((), ()))
TN = (((0,), (0,)), ((), ()))


def _dot(a, b, dims=NN):
    return lax.dot_general(a, b, dims, preferred_element_type=F32)


def _params(sem):
    return pltpu.CompilerParams(dimension_semantics=sem, vmem_limit_bytes=VMEM_LIMIT)


def _split3(x):
    hi = x.astype(BF16)
    r = x - hi.astype(F32)
    mid = r.astype(BF16)
    lo = (r - mid.astype(F32)).astype(BF16)
    return hi, mid, lo


def _split3_f32(x):
    hi = x.astype(BF16).astype(F32)
    r = x - hi
    mid = r.astype(BF16).astype(F32)
    return hi, mid, r - mid


def _lane_iota(shape):
    return lax.broadcasted_iota(jnp.int32, shape, len(shape) - 1)


def _row_iota(shape):
    return lax.broadcasted_iota(jnp.int32, shape, len(shape) - 2)


def _mm(name, a, b, a_spec, b_spec, out_shape, out_spec, grid, dims, acc_shape, ex=None):
    nk = grid[2]
    if ex is not None:
        return _mm_hosting(name, a, b, a_spec, b_spec, out_shape, out_spec, grid, dims, ex)

    def body(a_ref, b_ref, o_ref, *scr):
        p = _dot(a_ref[...], b_ref[...], dims)
        if nk == 1:
            o_ref[...] = p.astype(o_ref.dtype)
        else:
            acc = scr[0]
            k = pl.program_id(2)

            @pl.when(k == 0)
            def _():
                acc[...] = p

            @pl.when(k > 0)
            def _():
                acc[...] += p

            @pl.when(k == nk - 1)
            def _():
                o_ref[...] = acc[...].astype(o_ref.dtype)

    return pl.pallas_call(
        body, name=name, grid=grid, in_specs=[a_spec, b_spec], out_specs=out_spec, out_shape=out_shape,
        scratch_shapes=[pltpu.VMEM(acc_shape, F32)] if nk > 1 else [],
        compiler_params=_params(("parallel", "parallel", "arbitrary")),
    )(a, b)


def _mm_hosting(name, a, b, a_spec, b_spec, out_shape, out_spec, grid, dims, ex):
    assert grid[2] == 1
    n = len(ex.ins)

    def body(*refs):
        i, j = pl.program_id(0), pl.program_id(1)
        first = (i == 0) & (j == 0)
        (a_ref, b_ref), (o_ref,), _, begin, end = _hosted(
            ex, refs, 2, 1, first, first, (i == grid[0] - 1) & (j == grid[1] - 1))
        begin()
        o_ref[...] = _dot(a_ref[...], b_ref[...], dims).astype(o_ref.dtype)
        end()

    res = pl.pallas_call(
        body, name=name, grid=grid, in_specs=[a_spec, b_spec] + [ANY] * n, out_specs=[out_spec] + [ANY] * n,
        out_shape=[out_shape] + ex.out_shapes, scratch_shapes=ex.scratch(),
        compiler_params=_params(("arbitrary", "arbitrary", "arbitrary")),
    )(a, b, *ex.ins)
    return res[0], res[1:]


def _mm_nn(name, a, b, out_dtype, tm, tn):
    m, k = a.shape
    n = b.shape[1]
    return _mm(name, a, b, pl.BlockSpec((tm, k), lambda i, j, kk: (i, 0)), pl.BlockSpec((k, tn), lambda i, j, kk: (0, j)),
               jax.ShapeDtypeStruct((m, n), out_dtype), pl.BlockSpec((tm, tn), lambda i, j, kk: (i, j)),
               (m // tm, n // tn, 1), NN, (tm, tn))


def _mm_nt(name, a, b, out_dtype, tm, tn, ex=None):
    m, k = a.shape
    n = b.shape[0]
    return _mm(name, a, b, pl.BlockSpec((tm, k), lambda i, j, kk: (i, 0)), pl.BlockSpec((tn, k), lambda i, j, kk: (j, 0)),
               jax.ShapeDtypeStruct((m, n), out_dtype), pl.BlockSpec((tm, tn), lambda i, j, kk: (i, j)),
               (m // tm, n // tn, 1), NT, (tm, tn), ex)


def _mm_tn(name, a, b, tka, tn, ex=None):
    t, ka = a.shape
    n = b.shape[1]
    return _mm(name, a, b, pl.BlockSpec((t, tka), lambda i, j, kk: (0, i)), pl.BlockSpec((t, tn), lambda i, j, kk: (0, j)),
               jax.ShapeDtypeStruct((ka, n), F32), pl.BlockSpec((tka, tn), lambda i, j, kk: (i, j)),
               (ka // tka, n // tn, 1), TN, (tka, tn), ex)


def _d_h3(dhid, w_up, ex):
    tm = tn = 512
    shard = 2 * D_FF // N_CHIPS
    per_plane = D_FF // shard
    grid = (S // tm, D // tn)
    n = len(ex.ins)

    def body(*refs):
        i, j = pl.program_id(0), pl.program_id(1)
        first = (i == 0) & (j == 0)
        (a_ref, b_ref), (o_ref,), _, begin, end = _hosted(ex, refs, 2, 1, first, first, (i == grid[0] - 1) & (j == grid[1] - 1))
        begin()
        acc = None
        for k in range(N_CHIPS):
            cols = slice(shard * (k % per_plane), shard * (k % per_plane + 1))
            part = _dot(a_ref[k // per_plane, :, cols], b_ref[k], NT)
            acc = part if acc is None else acc + part
        o_ref[...] = acc
        end()

    res = pl.pallas_call(
        body, name="d_h3", grid=grid,
        in_specs=[pl.BlockSpec((2, tm, D_FF), lambda i, j: (0, i, 0)),
                  pl.BlockSpec((N_CHIPS, tn, shard), lambda i, j: (0, j, 0))] + [ANY] * n,
        out_specs=[pl.BlockSpec((tm, tn), lambda i, j: (i, j))] + [ANY] * n,
        out_shape=[jax.ShapeDtypeStruct((S, D), F32)] + ex.out_shapes, scratch_shapes=ex.scratch(),
        compiler_params=_params(("arbitrary", "arbitrary")),
    )(dhid, w_up, *ex.ins)
    return res[0], res[1:]


def _rms(x, g):
    r = lax.rsqrt(jnp.mean(x * x, axis=-1, keepdims=True) + EPS)
    return x * r * g


def _rms_bwd(x, g, dy):
    r = lax.rsqrt(jnp.mean(x * x, axis=-1, keepdims=True) + EPS)
    xh = x * r
    dxh = dy * g
    dx = r * (dxh - xh * jnp.mean(dxh * xh, axis=-1, keepdims=True))
    return dx, jnp.sum(dy * xh, axis=0, keepdims=True)


def _row_spec(tr, width):
    return pl.BlockSpec((tr, width), lambda i: (i, 0))


def _vec_spec(width):
    return pl.BlockSpec((1, width), lambda i: (0, 0))


def _norm_fwd(name, x, g, ex=None):
    rows, width = x.shape
    tr = min(TR, rows)
    steps = rows // tr
    hosted = ex if ex is not None else _no_exchange()
    n = len(hosted.ins)

    def body(*refs):
        i = pl.program_id(0)
        (x_ref, g_ref), (h_ref,), _, begin, end = _hosted(hosted, refs, 2, 1, i == 0, i == 0, i == steps - 1)
        begin()
        h_ref[...] = _rms(x_ref[...], g_ref[...]).astype(BF16)
        end()

    res = pl.pallas_call(
        body, name=name, grid=(steps,), in_specs=[_row_spec(tr, width), _vec_spec(width)] + [ANY] * n,
        out_specs=[_row_spec(tr, width)] + [ANY] * n,
        out_shape=[jax.ShapeDtypeStruct((rows, width), BF16)] + hosted.out_shapes, scratch_shapes=hosted.scratch(),
        compiler_params=_params(("arbitrary",)),
    )(x, g, *hosted.ins)
    return res[0] if ex is None else (res[0], res[1:])


def _proj_resid_norm(name, a, w, xp, g_post, g_pre, w_next=None):
    def body(a_ref, w_ref, xp_ref, gpost_ref, gpre_ref, *rest):
        y_ref, xn_ref, h_ref = rest[-3:] if w_next is None else rest[1:4]
        y = _dot(a_ref[...], w_ref[...])
        y_ref[...] = y
        xn = xp_ref[...] + _rms(y, gpost_ref[...])
        xn_ref[...] = xn
        h = _rms(xn, gpre_ref[...]).astype(BF16)
        h_ref[...] = h
        if w_next is not None:
            rest[4][...] = _dot(h, rest[0][...]).astype(BF16)

    mat = pl.BlockSpec((D, D), lambda i: (0, 0))
    more = [] if w_next is None else [w_next]
    return pl.pallas_call(
        body, name=name, grid=(S // TR,),
        in_specs=[_row_spec(TR, D), mat, _row_spec(TR, D), _vec_spec(D), _vec_spec(D)] + [mat] * len(more),
        out_specs=[_row_spec(TR, D)] * (3 + len(more)),
        out_shape=[jax.ShapeDtypeStruct((S, D), F32), jax.ShapeDtypeStruct((S, D), F32), jax.ShapeDtypeStruct((S, D), BF16)]
        + [jax.ShapeDtypeStruct((S, D), BF16)] * len(more),
        compiler_params=_params(("parallel",)),
    )(a, w, xp, g_post, g_pre, *more)


def _down_loss_bwd(act, w_down, x3, g_post, target):
    def body(a_ref, w_ref, x_ref, g_ref, t_ref, dres_ref, dy_ref, dg_ref, loss_ref):
        i = pl.program_id(0)

        @pl.when(i == 0)
        def _():
            dg_ref[...] = jnp.zeros_like(dg_ref)
            loss_ref[...] = jnp.zeros_like(loss_ref)

        y = _dot(a_ref[...], w_ref[...])
        g = g_ref[...]
        e = x_ref[...] + _rms(y, g) - t_ref[...]
        loss_ref[...] += jnp.sum(e * e, axis=0, keepdims=True) * (0.5 / D)
        dres = e * (1.0 / D)
        dres_ref[...] = dres
        dy, dg = _rms_bwd(y, g, dres)
        dy_ref[...] = dy.astype(BF16)
        dg_ref[...] += dg

    return pl.pallas_call(
        body, name="down_loss_bwd", grid=(S // TR,),
        in_specs=[_row_spec(TR, D_FF), pl.BlockSpec((D_FF, D), lambda i: (0, 0)), _row_spec(TR, D), _vec_spec(D),
                  _row_spec(TR, D)],
        out_specs=[_row_spec(TR, D), _row_spec(TR, D), _vec_spec(D), _vec_spec(D)],
        out_shape=[jax.ShapeDtypeStruct((S, D), F32), jax.ShapeDtypeStruct((S, D), BF16),
                   jax.ShapeDtypeStruct((1, D), F32), jax.ShapeDtypeStruct((1, D), F32)],
        compiler_params=_params(("arbitrary",)),
    )(act, w_down, x3, g_post, target)


def _mid_bwd(name, dres, xcur, g_pre, dh, yprev, g_post, w):
    def body(dres_ref, x_ref, gpre_ref, dh_ref, y_ref, gpost_ref, w_ref, dx_ref, dy_ref, da_ref, dgpre_ref, dgpost_ref):
        i = pl.program_id(0)

        @pl.when(i == 0)
        def _():
            dgpre_ref[...] = jnp.zeros_like(dgpre_ref)
            dgpost_ref[...] = jnp.zeros_like(dgpost_ref)

        dxn, dgpre = _rms_bwd(x_ref[...], gpre_ref[...], dh_ref[...])
        dx = dres_ref[...] + dxn
        dx_ref[...] = dx
        dy, dgpost = _rms_bwd(y_ref[...], gpost_ref[...], dx)
        dy = dy.astype(BF16)
        dy_ref[...] = dy
        da_ref[...] = _dot(dy, w_ref[...], NT).astype(BF16)
        dgpre_ref[...] += dgpre
        dgpost_ref[...] += dgpost

    return pl.pallas_call(
        body, name=name, grid=(S // TR,),
        in_specs=[_row_spec(TR, D), _row_spec(TR, D), _vec_spec(D), _row_spec(TR, D), _row_spec(TR, D), _vec_spec(D),
                  pl.BlockSpec((D, D), lambda i: (0, 0))],
        out_specs=[_row_spec(TR, D), _row_spec(TR, D), _row_spec(TR, D), _vec_spec(D), _vec_spec(D)],
        out_shape=[jax.ShapeDtypeStruct((S, D), F32), jax.ShapeDtypeStruct((S, D), BF16), jax.ShapeDtypeStruct((S, D), BF16),
                   jax.ShapeDtypeStruct((1, D), F32), jax.ShapeDtypeStruct((1, D), F32)],
        compiler_params=_params(("arbitrary",)),
    )(dres, xcur, g_pre, dh, yprev, g_post, w)


def _first_bwd(dres, x, g, dh):
    def body(dres_ref, x_ref, g_ref, dh_ref, dx_ref, dg_ref):
        i = pl.program_id(0)

        @pl.when(i == 0)
        def _():
            dg_ref[...] = jnp.zeros_like(dg_ref)

        dxn, dg = _rms_bwd(x_ref[...], g_ref[...], dh_ref[...])
        dx_ref[...] = dres_ref[...] + dxn
        dg_ref[...] += dg

    return pl.pallas_call(
        body, name="first_bwd", grid=(S // TR,),
        in_specs=[_row_spec(TR, D), _row_spec(TR, D), _vec_spec(D), _row_spec(TR, D)],
        out_specs=[_row_spec(TR, D), _vec_spec(D)],
        out_shape=[jax.ShapeDtypeStruct((S, D), F32), jax.ShapeDtypeStruct((1, D), F32)],
        compiler_params=_params(("arbitrary",)),
    )(dres, x, g, dh)


def _gain_bwd(name, x, g, dy):
    rows, width = x.shape

    def body(x_ref, g_ref, dy_ref, dg_ref):
        _, dg = _rms_bwd(x_ref[...], g_ref[...], dy_ref[...])
        dg_ref[...] = dg

    return pl.pallas_call(
        body, name=name, grid=(1,), in_specs=[_row_spec(rows, width), _vec_spec(width), _row_spec(rows, width)],
        out_specs=_vec_spec(width), out_shape=jax.ShapeDtypeStruct((1, width), F32),
        compiler_params=_params(("arbitrary",)),
    )(x, g, dy)


CUM_Q = DH
CUM_K = DH + 3
LSE_Q = DH + 6
BOTH_ONE = DH + 9
DEN_V = DH
DELTA = DH + 1
PREP_TR = 256
PIECE_LANES = 16
FOX_FWD_BLOCK = 1024
FOX_BWD_BLOCK = 512


def _at(lane_of_even_head, h):
    return (lane_of_even_head + DH * (h % 2)) % LANES


def _data_lanes(lane, h):
    return lane >= DH if h % 2 else lane < DH


def _pair_block(ref, off, h):
    base = ((off + DH * h) // LANES) * LANES
    return ref[:, base:base + LANES]


def _cumsum_rows(x, tri, carry):
    hi, mid, lo = _split3(x)
    return _dot(tri, hi) + _dot(tri, mid) + _dot(tri, lo) + carry


def _in_proj(h1, w_in, bf_pad):
    tr = TR

    place_q = np.zeros((LANES, HEADS * LANES), np.float32)
    place_k = np.zeros((LANES, HEADS * LANES), np.float32)
    for h in range(HEADS):
        for piece in range(3):
            place_q[PIECE_LANES * piece + h, LANES * h + _at(CUM_Q, h) + piece] = 1.0
            place_k[PIECE_LANES * piece + h, LANES * h + _at(CUM_K, h) + piece] = -1.0

    def body(h_ref, w_ref, bf_ref, pq_ref, pk_ref, qa_ref, ka_ref, va_ref, u_ref, z_ref, carry_ref):
        i = pl.program_id(0)

        @pl.when(i == 0)
        def _():
            carry_ref[...] = jnp.zeros_like(carry_ref)

        proj = _dot(h_ref[...], w_ref[...])
        u_ref[...] = proj[:, :D_POOL]
        z_ref[...] = proj[:, F_OFF:F_OFF + LANES]
        lane = _lane_iota((tr, LANES))
        z = proj[:, F_OFF:F_OFF + LANES] + bf_ref[...]
        log_f = jnp.minimum(z, 0.0) - jnp.log(1.0 + jnp.exp(-jnp.abs(z)))
        log_f = jnp.where(lane < HEADS, log_f, 0.0)
        tri = jnp.where(_row_iota((tr, tr)) >= _lane_iota((tr, tr)), 1.0, 0.0).astype(BF16)
        cum = _cumsum_rows(log_f, tri, carry_ref[0:1, :])
        carry_ref[0:1, :] = cum[tr - 1:tr, :]
        c_hi, c_mid, c_lo = _split3_f32(cum)
        pieces = (c_hi + pltpu.roll(c_mid, PIECE_LANES, 1) + pltpu.roll(c_lo, 2 * PIECE_LANES, 1)).astype(BF16)
        cum_q = _dot(pieces, pq_ref[...])
        cum_k = _dot(pieces, pk_ref[...])

        def between(first, h):
            return (lane >= _at(first, h)) & (lane < _at(first, h) + 3)

        ones_q = [jnp.where(between(CUM_K, h) | (lane == _at(BOTH_ONE, h)), 1.0, 0.0) for h in range(2)]
        ones_k = [jnp.where(between(CUM_Q, h) | between(LSE_Q, h) | (lane == _at(BOTH_ONE, h)), 1.0, 0.0) for h in range(2)]
        aug_v = [jnp.where(lane == _at(DEN_V, h), 1.0, jnp.where(between(DELTA, h), -1.0, 0.0)) for h in range(2)]
        for h in range(HEADS):
            mine = slice(LANES * h, LANES * (h + 1))
            data = _data_lanes(lane, h)
            qa_ref[h] = jnp.where(data, _pair_block(proj, Q_OFF, h) * (DH ** -0.5), cum_q[:, mine] + ones_q[h % 2]).astype(BF16)
            ka_ref[h] = jnp.where(data, _pair_block(proj, K_OFF, h), cum_k[:, mine] + ones_k[h % 2]).astype(BF16)
            va_ref[h] = jnp.where(data, _pair_block(proj, V_OFF, h), aug_v[h % 2]).astype(BF16)

    head_spec = pl.BlockSpec((HEADS, tr, LANES), lambda i: (0, i, 0))
    head_shape = jax.ShapeDtypeStruct((HEADS, S, LANES), BF16)
    place_spec = pl.BlockSpec(place_q.shape, lambda i: (0, 0))
    return pl.pallas_call(
        body, name="in_proj", grid=(S // tr,),
        in_specs=[_row_spec(tr, D), pl.BlockSpec((D, D_IN_PAD), lambda i: (0, 0)), _vec_spec(LANES), place_spec, place_spec],
        out_specs=[head_spec] * 3 + [_row_spec(tr, D_POOL), _row_spec(tr, LANES)],
        out_shape=[head_shape] * 3 + [jax.ShapeDtypeStruct((S, D_POOL), F32), jax.ShapeDtypeStruct((S, LANES), F32)],
        scratch_shapes=[pltpu.VMEM((SUBLANES, LANES), F32)], compiler_params=_params(("arbitrary",)),
    )(h1, w_in, bf_pad, jnp.asarray(place_q, BF16), jnp.asarray(place_k, BF16))


def _hosted(ex, refs, n_blocked_in, n_blocked_out, first, forward_at, last):
    n = len(ex.ins)
    own_in = refs[:n_blocked_in]
    ex_in = refs[n_blocked_in:n_blocked_in + n]
    own_out = refs[n_blocked_in + n:n_blocked_in + n + n_blocked_out]
    ex_out = refs[n_blocked_in + n + n_blocked_out:n_blocked_in + 2 * n + n_blocked_out]
    rest = refs[n_blocked_in + 2 * n + n_blocked_out:]
    args = (ex_in, ex_out, rest[-2], rest[-1])

    def begin():
        @pl.when(first)
        def _():
            ex.start(*args)

        @pl.when(forward_at)
        def _():
            ex.forward(*args)

    def end():
        @pl.when(last)
        def _():
            ex.finish(*args)

    return own_in, own_out, rest[:-2], begin, end


def _fox_fwd(qa, ka, va, ex):
    BQ = BK = FOX_FWD_BLOCK
    nq = S // BQ
    n_pairs = HEADS // 2

    def body(*refs):
        p_id, i = pl.program_id(0), pl.program_id(1)
        (qa_ref, ka_ref, va_ref), (y_ref, qab_ref), (m_scr, acc_scr), begin, end = _hosted(
            ex, refs, 3, 2, (p_id == 0) & (i == 0), (p_id == n_pairs - 1) & (i == 0), (p_id == n_pairs - 1) & (i == nq - 1))
        begin()
        lane = _lane_iota((BQ, LANES))
        causal = _row_iota((BQ, BK)) >= _lane_iota((BQ, BK))
        m_scr[...] = jnp.full_like(m_scr, NEG)
        acc_scr[...] = jnp.zeros_like(acc_scr)

        def step(j, masked):
            rows = pl.ds(pl.multiple_of(j * BK, BK), BK)
            for hh in range(2):
                s = _dot(qa_ref[hh], ka_ref[hh, rows, :], NT)
                if masked:
                    s = jnp.where(causal, s, NEG)
                m_prev = m_scr[hh]
                m_new = jnp.maximum(m_prev, jnp.max(s, axis=1, keepdims=True))
                p = jnp.exp(s - jnp.tile(m_new, (1, BK // LANES)))
                acc_scr[hh] = jnp.exp(m_prev - m_new) * acc_scr[hh] + _dot(p.astype(BF16), va_ref[hh, rows, :])
                m_scr[hh] = m_new

        def full_step(j, carry):
            step(j, False)
            return carry

        lax.fori_loop(0, i, full_step, 0)
        step(i, True)
        outs = []
        for hh in range(2):
            acc = acc_scr[hh]
            den_lane, lse_lane = _at(DEN_V, hh), _at(LSE_Q, hh)
            den = jnp.broadcast_to(acc[:, den_lane:den_lane + 1], (BQ, LANES))
            outs.append(acc * (1.0 / den))
            n_hi, n_mid, n_lo = _split3(-(m_scr[hh] + jnp.log(den)))
            qab_ref[hh] = jnp.where(lane == lse_lane, n_hi,
                                    jnp.where(lane == lse_lane + 1, n_mid, jnp.where(lane == lse_lane + 2, n_lo, qa_ref[hh])))
        y_ref[...] = jnp.where(lane < DH, outs[0], outs[1]).astype(BF16)
        end()

    pair_rows = pl.BlockSpec((2, BQ, LANES), lambda p, i: (p, i, 0))
    pair_all = pl.BlockSpec((2, S, LANES), lambda p, i: (p, 0, 0))
    n = len(ex.ins)
    res = pl.pallas_call(
        body, name="fox_fwd", grid=(n_pairs, nq), in_specs=[pair_rows, pair_all, pair_all] + [ANY] * n,
        out_specs=[pl.BlockSpec((BQ, LANES), lambda p, i: (i, D_POOL // LANES + p)), pair_rows] + [ANY] * n,
        out_shape=[jax.ShapeDtypeStruct((S, D), BF16), jax.ShapeDtypeStruct((HEADS, S, LANES), BF16)] + ex.out_shapes,
        scratch_shapes=[pltpu.VMEM((2, BQ, LANES), F32), pltpu.VMEM((2, BQ, LANES), F32)] + ex.scratch(),
        compiler_params=_params(("arbitrary", "arbitrary")),
    )(qa, ka, va, *ex.ins)
    return res[0], res[1], res[2:]


def _bwd_xa_mix(dqx, w_xq, dres, x2, g_pre, y1, g_post, w_mix_out, ycat, ex):
    steps = S // TR
    n = len(ex.ins)

    def body(*refs):
        i = pl.program_id(0)
        ((dq_ref, wq_ref, dres_ref, x_ref, gpre_ref, y_ref, gpost_ref, wm_ref, ycat_ref),
         (dx_ref, dy_ref, dgpre_ref, dgpost_ref, dp_ref, doa_ref), _, begin, end) = _hosted(
            ex, refs, 9, 6, i == 0, i == 0, i == steps - 1)
        begin()

        @pl.when(i == 0)
        def _():
            dgpre_ref[...] = jnp.zeros_like(dgpre_ref)
            dgpost_ref[...] = jnp.zeros_like(dgpost_ref)

        dxn, dgpre = _rms_bwd(x_ref[...], gpre_ref[...], _dot(dq_ref[...], wq_ref[...], NT))
        dx = dres_ref[...] + dxn
        dx_ref[...] = dx
        dy, dgpost = _rms_bwd(y_ref[...], gpost_ref[...], dx)
        dy = dy.astype(BF16)
        dy_ref[...] = dy
        dgpre_ref[...] += dgpre
        dgpost_ref[...] += dgpost

        d = _dot(dy, wm_ref[...], NT)
        dp_ref[...] = d[:, :D_POOL]
        lane = _lane_iota((TR, LANES))
        low = lane < DH
        for p in range(HEADS // 2):
            cols = slice(D_POOL + LANES * p, D_POOL + LANES * (p + 1))
            do = d[:, cols]
            prod = do * ycat_ref[:, cols].astype(F32)
            deltas = (jnp.sum(jnp.where(low, prod, 0.0), axis=1, keepdims=True),
                      jnp.sum(jnp.where(low, 0.0, prod), axis=1, keepdims=True))
            for hh in range(2):
                d_hi, d_mid, d_lo = _split3_f32(deltas[hh])
                dl = _at(DELTA, hh)
                aug = jnp.where(lane == dl, d_hi, jnp.where(lane == dl + 1, d_mid, jnp.where(lane == dl + 2, d_lo, 0.0)))
                doa_ref[2 * p + hh] = jnp.where(_data_lanes(lane, hh), do, aug).astype(BF16)
        end()

    mat = pl.BlockSpec((D, D), lambda i: (0, 0))
    res = pl.pallas_call(
        body, name="bwd_xa_mix", grid=(steps,),
        in_specs=[_row_spec(TR, D), mat, _row_spec(TR, D), _row_spec(TR, D), _vec_spec(D), _row_spec(TR, D), _vec_spec(D), mat,
                  _row_spec(TR, D)] + [ANY] * n,
        out_specs=[_row_spec(TR, D), _row_spec(TR, D), _vec_spec(D), _vec_spec(D), _row_spec(TR, D_POOL),
                   pl.BlockSpec((HEADS, TR, LANES), lambda i: (0, i, 0))] + [ANY] * n,
        out_shape=[jax.ShapeDtypeStruct((S, D), F32), jax.ShapeDtypeStruct((S, D), BF16), jax.ShapeDtypeStruct((1, D), F32),
                   jax.ShapeDtypeStruct((1, D), F32), jax.ShapeDtypeStruct((S, D_POOL), F32),
                   jax.ShapeDtypeStruct((HEADS, S, LANES), BF16)] + ex.out_shapes,
        scratch_shapes=ex.scratch(), compiler_params=_params(("arbitrary",)),
    )(dqx, w_xq, dres, x2, g_pre, y1, g_post, w_mix_out, ycat, *ex.ins)
    return res[:6], res[6:]


def _fox_bwd(qab, doa, ka, va, ex):
    BQ = BK = FOX_BWD_BLOCK
    nk = S // BK
    n_pairs = HEADS // 2

    def body(*refs):
        p_id, j = pl.program_id(0), pl.program_id(1)
        (qab_ref, doa_ref, ka_ref, va_ref), (dqa_ref, dka_ref, dva_ref), _, begin, end = _hosted(
            ex, refs, 4, 3, (p_id == 0) & (j == 0), (p_id == n_pairs - 1) & (j == 0), (p_id == n_pairs - 1) & (j == nk - 1))
        begin()

        @pl.when(j == 0)
        def _():
            dqa_ref[...] = jnp.zeros_like(dqa_ref)

        causal = _row_iota((BQ, BK)) >= _lane_iota((BQ, BK))
        dka_ref[...] = jnp.zeros_like(dka_ref)
        dva_ref[...] = jnp.zeros_like(dva_ref)

        def step(i, masked):
            rows = pl.ds(pl.multiple_of(i * BQ, BQ), BQ)
            for hh in range(2):
                kb = ka_ref[hh]
                q = qab_ref[hh, rows, :]
                do = doa_ref[hh, rows, :]
                s = _dot(q, kb, NT)
                if masked:
                    s = jnp.where(causal, s, NEG)
                p = jnp.exp(s)
                ds = p * _dot(do, va_ref[hh], NT)
                pb = p.astype(BF16)
                dsb = ds.astype(BF16)
                dva_ref[hh] += _dot(pb, do, TN)
                dka_ref[hh] += _dot(dsb, q, TN)
                dqa_ref[hh, rows, :] += _dot(dsb, kb)

        def full_step(i, carry):
            step(i, False)
            return carry

        step(j, True)
        lax.fori_loop(j + 1, nk, full_step, 0)
        end()

    pair_all = pl.BlockSpec((2, S, LANES), lambda p, j: (p, 0, 0))
    pair_rows = pl.BlockSpec((2, BK, LANES), lambda p, j: (p, j, 0))
    shape = jax.ShapeDtypeStruct((HEADS, S, LANES), F32)
    n = len(ex.ins)
    res = pl.pallas_call(
        body, name="fox_bwd", grid=(n_pairs, nk), in_specs=[pair_all, pair_all, pair_rows, pair_rows] + [ANY] * n,
        out_specs=[pair_all, pair_rows, pair_rows] + [ANY] * n, out_shape=[shape] * 3 + ex.out_shapes,
        scratch_shapes=ex.scratch(), compiler_params=_params(("arbitrary", "arbitrary")),
    )(qab, doa, ka, va, *ex.ins)
    return res[0], res[1], res[2], res[3:]


def _fox_bwd_post(dqa, dka, dva, du, proj, bf_pad):
    tr = PREP_TR
    nt = S // tr

    pick = np.zeros((HEADS * LANES, LANES), np.float32)
    for h in range(HEADS):
        pick[LANES * h + _at(BOTH_ONE, h), h] = 1.0

    def body(dqa_ref, dka_ref, dva_ref, du_ref, z_ref, bf_ref, pick_ref, dp_ref, dbf_ref, carry_ref):
        i = pl.program_id(0)

        @pl.when(i == 0)
        def _():
            carry_ref[...] = jnp.zeros_like(carry_ref)
            dbf_ref[...] = jnp.zeros_like(dbf_ref)

        lane = _lane_iota((tr, LANES))
        diff = jnp.concatenate([dqa_ref[h] - dka_ref[h] for h in range(HEADS)], axis=1)
        hi = diff.astype(BF16)
        dcum = _dot(hi, pick_ref[...]) + _dot((diff - hi.astype(F32)).astype(BF16), pick_ref[...])
        tri =jnp.where(_lane_iota((tr, tr)) >= _row_iota((tr, tr)), 1.0, 0.0).astype(BF16)
        dlog_f = _cumsum_rows(dcum, tri, carry_ref[0:1, :])
        carry_ref[0:1, :] = dlog_f[0:1, :]
        z = z_ref[...] + bf_ref[...]
        df = jnp.where(lane < HEADS, dlog_f / (1.0 + jnp.exp(z)), 0.0)
        dbf_ref[...] += jnp.sum(df, axis=0, keepdims=True)

        dp_ref[:, 0:D_POOL] = du_ref[...].astype(BF16)
        low = lane < DH
        for ref, off, scale in ((dqa_ref, Q_OFF, DH ** -0.5), (dka_ref, K_OFF, 1.0), (dva_ref, V_OFF, 1.0)):
            for p in range(HEADS // 2):
                blk = jnp.where(low, ref[2 * p], ref[2 * p + 1])
                dp_ref[:, off + LANES * p:off + LANES * (p + 1)] = (blk * scale).astype(BF16)
        dp_ref[:, F_OFF:F_OFF + LANES] = df.astype(BF16)

    head_spec = pl.BlockSpec((HEADS, tr, LANES), lambda i: (0, nt - 1 - i, 0))
    return pl.pallas_call(
        body, name="fox_bwd_post", grid=(nt,),
        in_specs=[head_spec, head_spec, head_spec, pl.BlockSpec((tr, D_POOL), lambda i: (nt - 1 - i, 0)),
                  pl.BlockSpec((tr, LANES), lambda i: (nt - 1 - i, 0)), _vec_spec(LANES),
                  pl.BlockSpec(pick.shape, lambda i: (0, 0))],
        out_specs=[pl.BlockSpec((tr, D_IN_PAD), lambda i: (nt - 1 - i, 0)), _vec_spec(LANES)],
        out_shape=[jax.ShapeDtypeStruct((S, D_IN_PAD), BF16), jax.ShapeDtypeStruct((1, LANES), F32)],
        scratch_shapes=[pltpu.VMEM((SUBLANES, LANES), F32)],
        compiler_params=_params(("arbitrary",)),
    )(dqa, dka, dva, du, proj, bf_pad, jnp.asarray(pick, BF16))


POOL_HALO = 16


def _by_group(lane, a2, a4, a8, a16):
    return jnp.where(lane < 64, a2, jnp.where(lane < 128, a4, jnp.where(lane < 192, a8, a16)))


def _window_count(lane, t):
    return jnp.minimum(t + 1, _by_group(lane, 2, 4, 8, 16)).astype(F32)


def _pool_diff(u, halo, first, tile):
    n = TR + POOL_HALO
    ext = jnp.concatenate([jnp.where(first, 0.0, halo), u], axis=0)
    s2 = ext + pltpu.roll(ext, 1, 0)
    s4 = s2 + pltpu.roll(s2, 2, 0)
    s8 = s4 + pltpu.roll(s4, 4, 0)
    s16 = s8 + pltpu.roll(s8, 8, 0)
    lane = _lane_iota((n, D_POOL))
    win = _by_group(lane, s2, s4, s8, s16)[POOL_HALO:]
    lane = _lane_iota((TR, D_POOL))
    t = tile * TR + _row_iota((TR, D_POOL))
    return win / _window_count(lane, t) - u


def _prev_halo(rows, width, col):
    per = TR // rows
    return pl.BlockSpec((rows, width), lambda i: (jnp.maximum(i * per - 1, 0), col))


def _next_halo(rows, width, col):
    per = TR // rows
    return pl.BlockSpec((rows, width), lambda i: (jnp.minimum((i + 1) * per, S // rows - 1), col))


def _pool_fwd(proj, w_bd, ps, ycat):
    def body(u_ref, halo_ref, w_ref, ps_ref, ycat_ref, y_ref):
        i = pl.program_id(0)
        diff = _pool_diff(u_ref[...], halo_ref[...], i == 0, i)
        y_ref[...] = (_dot(diff.astype(BF16), w_ref[...]) * ps_ref[...]).astype(BF16)

    return pl.pallas_call(
        body, name="pool_fwd", grid=(S // TR,),
        in_specs=[_row_spec(TR, D_POOL), _prev_halo(POOL_HALO, D_POOL, 0),
                  pl.BlockSpec((D_POOL, D_POOL), lambda i: (0, 0)), _vec_spec(D_POOL), ANY],
        out_specs=_row_spec(TR, D_POOL), out_shape=jax.ShapeDtypeStruct((S, D), BF16), input_output_aliases={4: 0},
        compiler_params=_params(("parallel",)),
    )(proj, proj, w_bd, ps, ycat)


def _pool_bwd(proj, dycat, w_bd, w_bd_t, ps):
    nt = S // TR
    n = TR + POOL_HALO

    def body(u_ref, halo_ref, dy_ref, dyn_ref, w_ref, wt_ref, ps_ref, du_ref, dw_ref, dps_ref):
        i = pl.program_id(0)

        @pl.when(i == 0)
        def _():
            dw_ref[...] = jnp.zeros_like(dw_ref)
            dps_ref[...] = jnp.zeros_like(dps_ref)

        diff = _pool_diff(u_ref[...], halo_ref[...], i == 0, i).astype(BF16)
        dy = dy_ref[...]
        dps_ref[...] += jnp.sum(dy * _dot(diff, w_ref[...]), axis=0, keepdims=True)
        dy_ext = jnp.concatenate([dy, jnp.where(i == nt - 1, 0.0, dyn_ref[...])], axis=0)
        dmixed = (dy_ext * ps_ref[...]).astype(BF16)
        ddiff = _dot(dmixed, wt_ref[...])
        dw_ref[...] += _dot(diff, dmixed[:TR], TN)
        lane = _lane_iota((n, D_POOL))
        t = i * TR + _row_iota((n, D_POOL))
        e = ddiff / _window_count(lane, t)
        f2 = e + pltpu.roll(e, n - 1, 0)
        f4 = f2 + pltpu.roll(f2, n - 2, 0)
        f8 = f4 + pltpu.roll(f4, n - 4, 0)
        f16 = f8 + pltpu.roll(f8, n - 8, 0)
        du_ref[...] = _by_group(lane, f2, f4, f8, f16)[:TR] - ddiff[:TR]

    mat = pl.BlockSpec((D_POOL, D_POOL), lambda i: (0, 0))
    return pl.pallas_call(
        body, name="pool_bwd", grid=(nt,),
        in_specs=[_row_spec(TR, D_POOL), _prev_halo(POOL_HALO, D_POOL, 0), _row_spec(TR, D_POOL),
                  _next_halo(POOL_HALO, D_POOL, 0), mat, mat, _vec_spec(D_POOL)],
        out_specs=[_row_spec(TR, D_POOL), mat, _vec_spec(D_POOL)],
        out_shape=[jax.ShapeDtypeStruct((S, D_POOL), F32), jax.ShapeDtypeStruct((D_POOL, D_POOL), F32),
                   jax.ShapeDtypeStruct((1, D_POOL), F32)],
        compiler_params=_params(("arbitrary",)),
    )(proj, proj, dycat, dycat, w_bd, w_bd_t, ps)


def _xa_probs(q, k):
    s = _dot(q, k, NT) * (XA_DH ** -0.5)
    e = jnp.exp(s - jnp.max(s, axis=-1, keepdims=True))
    return e * (1.0 / jnp.sum(e, axis=-1, keepdims=True))


def _xattn_fwd(qx, kv):
    def body(q_ref, kv_ref, o_ref):
        for h in range(XA_HEADS):
            cols = slice(XA_DH * h, XA_DH * (h + 1))
            vcols = slice(D + XA_DH * h, D + XA_DH * (h + 1))
            p = _xa_probs(q_ref[:, cols], kv_ref[:, cols])
            o_ref[:, cols] = _dot(p.astype(BF16), kv_ref[:, vcols]).astype(BF16)

    return pl.pallas_call(
        body, name="xattn_fwd", grid=(S // TR,),
        in_specs=[_row_spec(TR, D), pl.BlockSpec((MEM, 2 * D), lambda i: (0, 0))],
        out_specs=_row_spec(TR, D), out_shape=jax.ShapeDtypeStruct((S, D), BF16),
        compiler_params=_params(("parallel",)),
    )(qx, kv)


def _xattn_bwd(qx, kv, dxo):
    def body(q_ref, kv_ref, do_ref, dq_ref, dkv_ref):
        i = pl.program_id(0)

        @pl.when(i == 0)
        def _():
            dkv_ref[...] = jnp.zeros_like(dkv_ref)

        for h in range(XA_HEADS):
            cols = slice(XA_DH * h, XA_DH * (h + 1))
            vcols = slice(D + XA_DH * h, D + XA_DH * (h + 1))
            q = q_ref[:, cols]
            k = kv_ref[:, cols]
            do = do_ref[:, cols]
            p = _xa_probs(q, k)
            dkv_ref[:, vcols] += _dot(p.astype(BF16), do, TN)
            dp = _dot(do, kv_ref[:, vcols], NT)
            ds = (p * (dp - jnp.sum(p * dp, axis=-1, keepdims=True)) * (XA_DH ** -0.5)).astype(BF16)
            dq_ref[:, cols] = _dot(ds, k).astype(BF16)
            dkv_ref[:, cols] += _dot(ds, q, TN)

    kv_spec = pl.BlockSpec((MEM, 2 * D), lambda i: (0, 0))
    return pl.pallas_call(
        body, name="xattn_bwd", grid=(S // TR,), in_specs=[_row_spec(TR, D), kv_spec, _row_spec(TR, D)],
        out_specs=[_row_spec(TR, D), kv_spec],
        out_shape=[jax.ShapeDtypeStruct((S, D), BF16), jax.ShapeDtypeStruct((MEM, 2 * D), F32)],
        compiler_params=_params(("arbitrary",)),
    )(qx, kv, dxo)


CONV_HALO = SUBLANES
TC = 512
GELU_K = 0.7978845608028654
GELU_C = 0.044715


def _conv3(ext, w, rows):
    h0 = ext[CONV_HALO:CONV_HALO + rows]
    h1 = pltpu.roll(ext, 1, 0)[CONV_HALO:CONV_HALO + rows]
    h2 = pltpu.roll(ext, 2, 0)[CONV_HALO:CONV_HALO + rows]
    return w[2:3] * h0 + w[1:2] * h1 + w[0:1] * h2 + w[3:4], (h2, h1, h0)


def _conv_specs():
    main = pl.BlockSpec((2, TR, TC), lambda j, i: (0, i, j))
    per = TR // CONV_HALO
    prev = pl.BlockSpec((2, CONV_HALO, TC), lambda j, i: (0, jnp.maximum(i * per - 1, 0), j))
    nxt = pl.BlockSpec((2, CONV_HALO, TC), lambda j, i: (0, jnp.minimum((i + 1) * per, S // CONV_HALO - 1), j))
    par = pl.BlockSpec((2, SUBLANES, TC), lambda j, i: (0, 0, j))
    return main, prev, nxt, par


def _convgate_fwd(hid, cwb):
    def body(h_ref, hp_ref, w_ref, act_ref):
        i = pl.program_id(1)
        c = []
        for g in range(2):
            ext = jnp.concatenate([jnp.where(i == 0, 0.0, hp_ref[g]), h_ref[g]], axis=0)
            c.append(_conv3(ext, w_ref[g], TR)[0])
        gate, up = c
        act_ref[...] = (jax.nn.gelu(gate, approximate=True) * up).astype(BF16)

    main, prev, _, par = _conv_specs()
    return pl.pallas_call(
        body, name="convgate_fwd", grid=(D_FF // TC, S // TR), in_specs=[main, prev, par],
        out_specs=pl.BlockSpec((TR, TC), lambda j, i: (i, j)), out_shape=jax.ShapeDtypeStruct((S, D_FF), BF16),
        compiler_params=_params(("parallel", "parallel")),
    )(hid, hid, cwb)


def _convgate_bwd(hid, dact, cwb):
    nr = S // TR
    n = TR + CONV_HALO

    def body(h_ref, hp_ref, hn_ref, da_ref, dan_ref, w_ref, dh_ref, dw_ref):
        i = pl.program_id(1)

        @pl.when(i == 0)
        def _():
            dw_ref[...] = jnp.zeros_like(dw_ref)

        da = jnp.concatenate([da_ref[...], jnp.where(i == nr - 1, 0.0, dan_ref[...])], axis=0)
        c, taps = [], []
        for g in range(2):
            ext = jnp.concatenate([jnp.where(i == 0, 0.0, hp_ref[g]), h_ref[g], hn_ref[g]], axis=0)
            cg, tg = _conv3(ext, w_ref[g], n)
            c.append(cg)
            taps.append(tg)
        gate, up = c
        th = jnp.tanh(GELU_K * (gate + GELU_C * gate * gate * gate))
        gelu = 0.5 * gate * (1.0 + th)
        dgelu = 0.5 * (1.0 + th) + 0.5 * gate * (1.0 - th * th) * GELU_K * (1.0 + 3.0 * GELU_C * gate * gate)
        for g, dc in enumerate((da * up * dgelu, da * gelu)):
            w = w_ref[g]
            dh = w[2:3] * dc[:TR] + w[1:2] * pltpu.roll(dc, n - 1, 0)[:TR] + w[0:1] * pltpu.roll(dc, n - 2, 0)[:TR]
            dh_ref[g] = dh.astype(BF16)
            dcm = dc[:TR]
            for r in range(3):
                dw_ref[g, r:r + 1, :] += jnp.sum(dcm * taps[g][r][:TR], axis=0, keepdims=True)
            dw_ref[g, 3:4, :] += jnp.sum(dcm, axis=0, keepdims=True)

    main, prev, nxt, par = _conv_specs()
    per = TR // CONV_HALO
    return pl.pallas_call(
        body, name="convgate_bwd", grid=(D_FF // TC, nr),
        in_specs=[main, prev, nxt, pl.BlockSpec((TR, TC), lambda j, i: (i, j)),
                  pl.BlockSpec((CONV_HALO, TC), lambda j, i: (jnp.minimum((i + 1) * per, S // CONV_HALO - 1), j)), par],
        out_specs=[main, par],
        out_shape=[jax.ShapeDtypeStruct((2, S, D_FF), BF16), jax.ShapeDtypeStruct((2, SUBLANES, D_FF), F32)],
        compiler_params=_params(("parallel", "arbitrary")),
    )(hid, hid, hid, dact, dact, cwb)


def _adam_update(w, g, m, v):
    m = ADAM_B1 * m + (1.0 - ADAM_B1) * g
    v = ADAM_B2 * v + (1.0 - ADAM_B2) * (g * g)
    m_hat = m / (1.0 - ADAM_B1 ** ADAM_STEP)
    v_hat = v / (1.0 - ADAM_B2 ** ADAM_STEP)
    return -ADAM_LR * (m_hat / (jnp.sqrt(v_hat) + ADAM_EPS) + ADAM_WD * w), m, v


def _row_tile(rows, cols, itemsize=4, target=TILE_BYTES):
    tr = SUBLANES
    while rows % (2 * tr) == 0 and 2 * tr * cols * itemsize <= target:
        tr *= 2
    assert rows % tr == 0, (rows, tr)
    return tr


def _adamw(name, w, g, m, v):
    rows, cols = w.shape
    tr = rows if rows * cols * 4 <= TILE_BYTES // 2 else _row_tile(rows, cols, target=TILE_BYTES // 2)

    def body(w_ref, g_ref, m_ref, v_ref, d_ref, nm_ref, nv_ref):
        d_ref[...], nm_ref[...], nv_ref[...] = _adam_update(w_ref[...], g_ref[...], m_ref[...], v_ref[...])

    spec = _row_spec(tr, cols)
    shape = jax.ShapeDtypeStruct((rows, cols), F32)
    return pl.pallas_call(
        body, name=name, grid=(rows // tr,), in_specs=[spec] * 4, out_specs=[spec] * 3, out_shape=[shape] * 3,
        compiler_params=_params(("parallel",)),
    )(w, g, m, v)


def _adamw_halves(name, core, w, g_mine, g_sibling, m, v):
    rows, cols = w.shape
    half = rows // 2
    tr = _row_tile(half, cols, target=TILE_BYTES // 2)
    per = half // tr

    def body(core_ref, w_ref, gm_ref, gs_ref, m_ref, v_ref, g_ref, d_ref, nm_ref, nv_ref):
        g = jnp.where(pl.program_id(0) // per == core_ref[0], gm_ref[...], gs_ref[...])
        g_ref[...] = g
        d_ref[...], nm_ref[...], nv_ref[...] = _adam_update(w_ref[...], g, m_ref[...], v_ref[...])

    spec = pl.BlockSpec((tr, cols), lambda i, core_ref: (i, 0))
    half_spec = pl.BlockSpec((tr, cols), lambda i, core_ref: (i % per, 0))
    shape = jax.ShapeDtypeStruct((rows, cols), F32)
    return pl.pallas_call(
        body, name=name, out_shape=[shape] * 4,
        grid_spec=pltpu.PrefetchScalarGridSpec(
            num_scalar_prefetch=1, grid=(rows // tr,), in_specs=[spec, half_spec, half_spec, spec, spec], out_specs=[spec] * 4),
        compiler_params=_params(("parallel",)),
    )(core, w, g_mine, g_sibling, m, v)


def _chip_sum(name, core, g, other):
    _, _, half, cols = g.shape
    tr = _row_tile(half, cols)

    def body(core_ref, g_ref, o_ref, p_ref):
        p_ref[...] = (g_ref[...] + o_ref[...]).astype(BF16)

    spec = pl.BlockSpec((None, tr, cols), lambda j, i, core_ref: (j, i, 0))
    return pl.pallas_call(
        body, name=name, out_shape=jax.ShapeDtypeStruct((N_CHIPS, half, cols), BF16),
        grid_spec=pltpu.PrefetchScalarGridSpec(
            num_scalar_prefetch=1, grid=(N_CHIPS, half // tr),
            in_specs=[pl.BlockSpec((None, None, tr, cols), lambda j, i, core_ref: (j, core_ref[0], i, 0)), spec],
            out_specs=spec),
        compiler_params=_params(("parallel", "parallel")),
    )(core, g, other)


def _mesh_sum(name, chip, received, own):
    _, half, cols = received.shape
    tr = _row_tile(half, cols, itemsize=2 * N_CHIPS)

    def body(chip_ref, r_ref, own_ref, o_ref):
        acc = None
        for j in range(N_CHIPS):
            term = jnp.where(chip_ref[0] == j, own_ref[...], r_ref[j]).astype(F32)
            acc = term if acc is None else acc + term
        o_ref[...] = acc

    return pl.pallas_call(
        body, name=name, out_shape=jax.ShapeDtypeStruct((half, cols), F32),
        grid_spec=pltpu.PrefetchScalarGridSpec(
            num_scalar_prefetch=1, grid=(half // tr,),
            in_specs=[pl.BlockSpec((N_CHIPS, tr, cols), lambda i, chip_ref: (0, i, 0)),
                      pl.BlockSpec((None, tr, cols), lambda i, chip_ref: (chip_ref[0], i, 0))],
            out_specs=pl.BlockSpec((tr, cols), lambda i, chip_ref: (i, 0))),
        compiler_params=_params(("parallel",)),
    )(chip, received, own)


CHIP_FLIPS = ((1, 0), (0, 1), (1, 1))


def _place():
    x, y, c = lax.axis_index("x"), lax.axis_index("y"), lax.axis_index("c")
    return x, y, c, 2 * x + y


def _remote(src, dst, sems_s, sems_r, k, dev):
    return pltpu.make_async_remote_copy(src_ref=src, dst_ref=dst, send_sem=sems_s.at[k], recv_sem=sems_r.at[k],
                                        device_id=dev, device_id_type=MESH)


class _Exchange:
    def __init__(self, ins, out_shapes, n_sems, start, forward, finish):
        self.ins, self.out_shapes, self.n_sems = list(ins), list(out_shapes), n_sems
        self.start, self.forward, self.finish = start, forward, finish

    def scratch(self):
        return [pltpu.SemaphoreType.DMA((self.n_sems,)), pltpu.SemaphoreType.DMA((self.n_sems,))]

    def run(self, name):
        n = len(self.ins)

        def body(*refs):
            args = (refs[:n], refs[n:2 * n]) + tuple(refs[2 * n:])
            self.start(*args)
            self.forward(*args)
            self.finish(*args)

        return pl.pallas_call(
            body, name=name, in_specs=[ANY] * n, out_specs=[ANY] * n, out_shape=self.out_shapes, scratch_shapes=self.scratch(),
        )(*self.ins)


def _all_gather_weights(halved, whole):
    nh, nw = len(halved), len(whole)
    n_arr = nh + nw

    def copies(ins, outs, sems_s, sems_r):
        x, y, c, me = _place()
        sibling = (x, y, 1 - c)
        own = [_remote(ins[k], outs[k].at[me], sems_s, sems_r, k, sibling) for k in range(n_arr)]
        first, passed = [], []
        for k in range(n_arr):
            for f, (fx, fy) in enumerate(CHIP_FLIPS):
                src, dst = (ins[k].at[c], outs[k].at[me, c]) if k < nh else (ins[k], outs[k].at[me])
                first.append(_remote(src, dst, sems_s, sems_r, n_arr + 3 * k + f, (x ^ fx, y ^ fy, c)))
        for k in range(nh):
            for f, (fx, fy) in enumerate(CHIP_FLIPS):
                landed = outs[k].at[2 * (x ^ fx) + (y ^ fy), c]
                passed.append(_remote(landed, landed, sems_s, sems_r, 4 * n_arr + 3 * k + f, sibling))
        return own, first, passed

    def start(*refs):
        own, first, _ = copies(*refs)
        for cp in own + first:
            cp.start()

    def forward(*refs):
        _, first, passed = copies(*refs)
        for arrived, cp in zip(first, passed):
            arrived.wait_recv()
            cp.start()

    def finish(*refs):
        own, first, passed = copies(*refs)
        for cp in first[3 * nh:] + passed + own:
            cp.wait_recv()
        for cp in first + passed + own:
            cp.wait_send()

    shapes = [jax.ShapeDtypeStruct((N_CHIPS,) + a.shape, a.dtype) for a in list(halved) + list(whole)]
    return _Exchange(list(halved) + list(whole), shapes, 7 * nh + 4 * nw, start, forward, finish)


def _swap_halves(gs):
    n = len(gs)

    def copies(ins, outs, sems_s, sems_r):
        x, y, c, _ = _place()
        return [_remote(ins[k].at[:, 1 - c], outs[k], sems_s, sems_r, k, (x, y, 1 - c)) for k in range(n)]

    def start(*refs):
        for cp in copies(*refs):
            cp.start()

    def finish(*refs):
        for cp in copies(*refs):
            cp.wait()

    shapes = [jax.ShapeDtypeStruct((g.shape[0],) + g.shape[2:], g.dtype) for g in gs]
    return _Exchange(gs, shapes, n, start, _no_copies, finish)


def _scatter_chips(ps):
    n = len(ps)

    def copies(ins, outs, sems_s, sems_r):
        x, y, c, me = _place()
        return [_remote(ins[k].at[2 * (x ^ fx) + (y ^ fy)], outs[k].at[me], sems_s, sems_r, 3 * k + f, (x ^ fx, y ^ fy, c))
                for k in range(n) for f, (fx, fy) in enumerate(CHIP_FLIPS)]

    def start(*refs):
        for cp in copies(*refs):
            cp.start()

    def forward(*refs):
        pass

    def finish(*refs):
        for cp in copies(*refs):
            cp.wait()

    shapes = [jax.ShapeDtypeStruct(p.shape, p.dtype) for p in ps]
    return _Exchange(ps, shapes, 3 * n, start, forward, finish)


def _swap_reduced(rs):
    n = len(rs)

    def copies(ins, outs, sems_s, sems_r):
        x, y, c, _ = _place()
        return [_remote(ins[k], outs[k], sems_s, sems_r, k, (x, y, 1 - c)) for k in range(n)]

    def start(*refs):
        for cp in copies(*refs):
            cp.start()

    def finish(*refs):
        for cp in copies(*refs):
            cp.wait()

    return _Exchange(rs, [jax.ShapeDtypeStruct(r.shape, r.dtype) for r in rs], n, start, _no_copies, finish)


N_DEV = 8


def _gather_small(buf):
    def copies(ins, outs, sems_s, sems_r):
        x, y, c, _ = _place()
        me = 4 * x + 2 * y + c
        return [_remote(ins[0], outs[0].at[me], sems_s, sems_r, o - 1, (x ^ (o >> 2), y ^ ((o >> 1) & 1), c ^ (o & 1)))
                for o in range(1, N_DEV)]

    def start(*refs):
        for cp in copies(*refs):
            cp.start()

    def finish(*refs):
        for cp in copies(*refs):
            cp.wait()

    return _Exchange([buf], [jax.ShapeDtypeStruct((N_DEV,) + buf.shape, buf.dtype)], N_DEV - 1, start, _no_copies, finish)


def _sum_devices(place, gathered, own):
    rows = own.shape[0]

    def body(place_ref, g_ref, own_ref, o_ref):
        acc = None
        for d in range(N_DEV):
            term = jnp.where(place_ref[0] == d, own_ref[...], g_ref[d])
            acc = term if acc is None else acc + term
        o_ref[...] = acc

    return pl.pallas_call(
        body, name="sum_devices", out_shape=jax.ShapeDtypeStruct((rows, LANES), F32),
        grid_spec=pltpu.PrefetchScalarGridSpec(
            num_scalar_prefetch=1, grid=(1,),
            in_specs=[pl.BlockSpec((N_DEV, rows, LANES), lambda i, place_ref: (0, 0, 0)),
                      pl.BlockSpec((rows, LANES), lambda i, place_ref: (0, 0))],
            out_specs=pl.BlockSpec((rows, LANES), lambda i, place_ref: (0, 0))),
        compiler_params=_params(("arbitrary",)),
    )(place, gathered, own)


def _no_copies(*refs):
    pass


def _no_exchange():
    return _Exchange([], [], 1, _no_copies, _no_copies, _no_copies)


class _NoComm:
    def gather_first(self):
        return _no_exchange()

    def first_landed(self, p, landed):
        pass

    def gather_rest(self, p):
        return _no_exchange()

    def weights_landed(self, p, landed):
        pass

    def gather_last(self):
        return _no_exchange()

    def last_landed(self, p, landed):
        pass

    def swap_first(self, g):
        return _no_exchange()

    def first_swapped(self, landed):
        pass

    def swap_second(self, g):
        return _no_exchange()

    def second_swapped(self, landed):
        pass

    def scatter_early(self, g):
        return _no_exchange()

    def scatter_landed(self, landed):
        pass

    def swap_reduced_early(self):
        return _no_exchange()

    def reduced_landed(self, landed):
        pass

    def scatter_late(self, g):
        return _no_exchange()

    def late_landed(self, landed):
        pass


def _local_step(x, mem, target, p, comm):
    h1, landed = _norm_fwd("norm_mix_pre", x, p["norm_mix_pre"], comm.gather_first())
    comm.first_landed(p, landed)
    qa, ka, va, u, z = _in_proj(h1, p["w_in"], p["bf_pad"])
    ycat, qab, landed = _fox_fwd(qa, ka, va, comm.gather_rest(p))
    comm.weights_landed(p, landed)
    ycat = _pool_fwd(u, p["w_pool_bd"], p["pool_scale"], ycat)
    y1, x2, h2, qx = _proj_resid_norm("mix_out", ycat, p["w_mix_out"], x, p["norm_mix_post"], p["norm_xa_pre"], p["w_xq"])
    mem_n = _norm_fwd("norm_mem", mem, p["norm_mem"])
    kv = _mm(
        "xkv", mem_n, p["w_xkv"], pl.BlockSpec((MEM, D), lambda i, j, k: (0, 0)),
        pl.BlockSpec((None, D, 512), lambda i, j, k: (j, 0, 0)), jax.ShapeDtypeStruct((MEM, 2 * D), BF16),
        pl.BlockSpec((MEM, 512), lambda i, j, k: (0, j)), (1, N_CHIPS, 1), NN, (MEM, 512))
    xo = _xattn_fwd(qx, kv)
    y2, x3, h3 = _proj_resid_norm("xo", xo, p["w_xo"], x2, p["norm_xa_post"], p["norm_ffn_pre"])
    hid, landed = _mm(
        "up_proj", h3, p["w_up"], pl.BlockSpec((1024, D), lambda i, j, k: (i, 0)),
        pl.BlockSpec((None, D, 1024), lambda i, j, k: (j // 2, 0, j % 2)), jax.ShapeDtypeStruct((2, S, D_FF), F32),
        pl.BlockSpec((None, 1024, 1024), lambda i, j, k: (j // 4, i, j % 4)), (S // 1024, 8, 1), NN, (1024, 1024),
        comm.gather_last())
    comm.last_landed(p, landed)
    act = _convgate_fwd(hid, p["cwb"])

    g = {}
    dres, dy3, g["norm_ffn_post"], loss_cols = _down_loss_bwd(act, p["w_down"], x3, p["norm_ffn_post"], target)
    dact = _mm_nt("d_act", dy3, p["w_down"], F32, 1024, 1024)
    g["w_down"] = _mm_tn("dw_down", act, dy3, 512, 512)
    dhid, dcwb = _convgate_bwd(hid, dact, p["cwb"])
    g["w_up"] = _mm(
        "dw_up", h3, dhid, pl.BlockSpec((S, 512), lambda i, j, k: (0, i)),
        pl.BlockSpec((None, S, 512), lambda i, j, k: (j // 8, 0, j % 8)), jax.ShapeDtypeStruct((N_CHIPS, D, 2048), F32),
        pl.BlockSpec((None, 512, 512), lambda i, j, k: (j // 4, i, j % 4)), (2, 16, 1), TN, (512, 512))
    dh3, landed = _d_h3(dhid, p["w_up"], comm.swap_first(g))
    comm.first_swapped(landed)
    dres, dy2, dxo, g["norm_ffn_pre"], g["norm_xa_post"] = _mid_bwd(
        "bwd_ffn_xa", dres, x3, p["norm_ffn_pre"], dh3, y2, p["norm_xa_post"], p["w_xo"])
    g["w_xo"] = _mm_tn("dw_xo", xo, dy2, 512, 512)
    dqx, dkv = _xattn_bwd(qx, kv, dxo)
    dkv = dkv.astype(BF16)
    g["w_xq"] = _mm_tn("dw_xq", h2, dqx, 512, 512)
    dmem_n = _mm(
        "d_mem", dkv, p["w_xkv"], pl.BlockSpec((MEM, 512), lambda i, j, k: (0, k)),
        pl.BlockSpec((None, D, 512), lambda i, j, k: (k, 0, 0)), jax.ShapeDtypeStruct((MEM, D), F32),
        pl.BlockSpec((MEM, D), lambda i, j, k: (0, 0)), (1, 1, N_CHIPS), NT, (MEM, D))
    g["w_xkv"] = _mm(
        "dw_xkv", mem_n, dkv, pl.BlockSpec((MEM, D), lambda i, j, k: (0, 0)),
        pl.BlockSpec((MEM, 512), lambda i, j, k: (0, j)), jax.ShapeDtypeStruct((N_CHIPS, D, 512), F32),
        pl.BlockSpec((None, D, 512), lambda i, j, k: (j, 0, 0)), (1, N_CHIPS, 1), TN, (D, 512))
    g["norm_mem"] = _gain_bwd("dg_mem", mem, p["norm_mem"], dmem_n)
    (dres, dy1, g["norm_xa_pre"], g["norm_mix_post"], dy_pool, doa), landed = _bwd_xa_mix(
        dqx, p["w_xq"], dres, x2, p["norm_xa_pre"], y1, p["norm_mix_post"], p["w_mix_out"], ycat, comm.swap_second(g))
    comm.second_swapped(landed)
    g["w_mix_out"] = _mm_tn("dw_mix_out", ycat, dy1, 512, 512)
    dqa, dka, dva, landed = _fox_bwd(qab, doa, ka, va, comm.scatter_early(g))
    comm.scatter_landed(landed)
    du, g["w_pool_full"], g["pool_scale"] = _pool_bwd(u, dy_pool, p["w_pool_bd"], p["w_pool_bd_t"], p["pool_scale"])
    dproj, g["bf_pad"] = _fox_bwd_post(dqa, dka, dva, du, z, p["bf_pad"])
    g["w_in"], landed = _mm_tn("dw_in", h1, dproj, 512, 896, comm.swap_reduced_early())
    comm.reduced_landed(landed)
    dh1, landed = _mm_nt("d_h1", dproj, p["w_in"], F32, 1024, 1024, comm.scatter_late(g))
    comm.late_landed(landed)
    grad_x, g["norm_mix_pre"] = _first_bwd(dres, x, p["norm_mix_pre"], dh1)
    g["cwb"] = dcwb
    return grad_x, g, loss_cols


BIG = ("w_in", "w_mix_out", "w_xq", "w_xkv", "w_xo", "w_up", "w_down")
ROW_SHARDED = ("w_mix_out", "w_xq", "w_xo", "w_down")
SMALL = ("norm_mix_pre", "norm_mix_post", "b_forget", "w_pool", "pool_scale", "norm_mem", "norm_xa_pre", "norm_xa_post",
         "norm_ffn_pre", "norm_ffn_post", "conv_b")
ORDER = ("norm_mix_pre", "norm_mix_post", "w_in", "b_forget", "w_pool", "pool_scale", "w_mix_out", "norm_mem", "norm_xa_pre",
         "norm_xa_post", "w_xq", "w_xkv", "w_xo", "norm_ffn_pre", "norm_ffn_post", "w_up", "conv_w", "conv_b", "w_down")
SLOT = SUBLANES * LANES


def _pack(parts):
    rows, offs, off = [], [], 0
    for a in parts:
        flat = a.reshape(-1).astype(F32)
        n = -(-flat.shape[0] // SLOT) * SLOT
        rows.append(jnp.pad(flat, (0, n - flat.shape[0])).reshape(n // LANES, LANES))
        offs.append(off)
        off += n // LANES
    return jnp.concatenate(rows, axis=0), offs


def _unpack(buf, off, like):
    n = like.size
    rows = -(-n // LANES)
    return buf[off:off + rows].reshape(-1)[:n].reshape(like.shape)


FIRST = ("w_in",)
REST = ("w_mix_out", "w_xq", "w_xkv", "w_xo", "w_up")
LAST = ("w_down",)


def _local_params(w):
    w_pool_bd = jnp.zeros((D_POOL, D_POOL), F32)
    for gi in range(4):
        w_pool_bd = w_pool_bd.at[64 * gi:64 * (gi + 1), 64 * gi:64 * (gi + 1)].set(w["w_pool"][0, gi])
    p = {n: w[n] for n in ("norm_mix_pre", "norm_mix_post", "norm_mem", "norm_xa_pre", "norm_xa_post", "norm_ffn_pre",
                           "norm_ffn_post")}
    p.update(
        bf_pad=jnp.pad(w["b_forget"], ((0, 0), (0, LANES - HEADS))),
        w_pool_bd=w_pool_bd.astype(BF16), w_pool_bd_t=w_pool_bd.T.astype(BF16), pool_scale=w["pool_scale"].reshape(1, D_POOL))
    return p


def _w_in_param(stacked):
    return jnp.pad(jnp.concatenate(list(stacked), axis=1), ((0, 0), (0, D_IN_PAD - D_IN)))


def _rest_params(w, full, conv_w_full):
    cw2 = conv_w_full.reshape(3, 2, D_FF).transpose(1, 0, 2)
    cwb = jnp.concatenate([cw2, w["conv_b"].reshape(1, 2, D_FF).transpose(1, 0, 2), jnp.zeros((2, 4, D_FF), F32)], axis=1)
    return dict(w_mix_out=full["w_mix_out"].reshape(D, D), w_xq=full["w_xq"].reshape(D, D), w_xkv=full["w_xkv"],
                w_xo=full["w_xo"].reshape(D, D), w_up=full["w_up"], cwb=cwb)


def _whole_params(w, full, conv_w_full):
    p = _local_params(w)
    p.update(_rest_params(w, full, conv_w_full), w_in=_w_in_param(full["w_in"]), w_down=full["w_down"].reshape(D_FF, D))
    return p


def _halved(a):
    return a.reshape(a.shape[:-2] + (2, a.shape[-2] // 2, a.shape[-1]))


class _StepComm:
    def __init__(self, w, shard2d, conv_w, core_id, chip_id):
        self.w, self.shard2d, self.conv_w, self.core_id, self.chip_id = w, shard2d, conv_w, core_id, chip_id
        self.first, self.second = ("w_up", "w_down"), ("w_xq", "w_xkv", "w_xo")
        self.early = self.first + self.second
        self.late = ("w_in", "w_mix_out")

    def gather_first(self):
        return _all_gather_weights([_halved(self.shard2d[n].astype(BF16)) for n in FIRST], [])

    def first_landed(self, p, landed):
        p["w_in"] = _w_in_param(landed[0].reshape((N_CHIPS,) + self.shard2d["w_in"].shape))

    def gather_rest(self, p):
        return _all_gather_weights([_halved(self.shard2d[n].astype(BF16)) for n in REST], [self.conv_w.reshape(3, -1)])

    def weights_landed(self, p, landed):
        full = {n: a.reshape((N_CHIPS,) + self.shard2d[n].shape) for n, a in zip(REST, landed)}
        conv_w_full = jnp.transpose(landed[-1], (1, 0, 2)).reshape(3, 2 * D_FF)
        p.update(_rest_params(self.w, full, conv_w_full))

    def gather_last(self):
        return _all_gather_weights([_halved(self.shard2d[n].astype(BF16)) for n in LAST], [])

    def last_landed(self, p, landed):
        p["w_down"] = landed[0].reshape(D_FF, D)

    def _view(self, g, n):
        return _halved(g[n].reshape((N_CHIPS,) + self.shard2d[n].shape))

    def swap_first(self, g):
        return _swap_halves([self._view(g, n) for n in self.first])

    def first_swapped(self, landed):
        self.from_sibling = dict(zip(self.first, landed))

    def swap_second(self, g):
        return _swap_halves([self._view(g, n) for n in self.second])

    def second_swapped(self, landed):
        self.from_sibling.update(zip(self.second, landed))

    def scatter_early(self, g):
        self.partial = [_chip_sum("chip_sum_" + n, self.core_id, self._view(g, n), self.from_sibling[n]) for n in self.early]
        return _scatter_chips(self.partial)

    def scatter_landed(self, landed):
        self.received = list(landed)

    def swap_reduced_early(self):
        self.reduced = [_mesh_sum("mesh_sum_" + n, self.chip_id, r, own)
                        for n, r, own in zip(self.early, self.received, self.partial)]
        return _swap_reduced(self.reduced)

    def reduced_landed(self, landed):
        self.reduced_sibling = list(landed)

    def scatter_late(self, g):
        gw_in = g["w_in"][:, :D_IN]
        cols = D_IN // N_CHIPS
        views = [_halved(jnp.stack([gw_in[:, cols * j:cols * (j + 1)] for j in range(N_CHIPS)])), self._view(g, "w_mix_out")]
        from_sibling = _swap_halves(views).run("swap_halves_late")
        self.partial_late = [_chip_sum("chip_sum_" + n, self.core_id, view, other)
                             for n, view, other in zip(self.late, views, from_sibling)]
        return _scatter_chips(self.partial_late)

    def late_landed(self, landed):
        self.received_late = list(landed)


def kernel(x, mem, norm_mix_pre, norm_mix_post, w_in, b_forget, w_pool, pool_scale, w_mix_out, norm_mem, norm_xa_pre, norm_xa_post, w_xq, w_xkv, w_xo, norm_ffn_pre, norm_ffn_post, w_up, conv_w, conv_b, w_down, loss_target, m_norm_mix_pre, m_norm_mix_post, m_w_in, m_b_forget, m_w_pool, m_pool_scale, m_w_mix_out, m_norm_mem, m_norm_xa_pre, m_norm_xa_post, m_w_xq, m_w_xkv, m_w_xo, m_norm_ffn_pre, m_norm_ffn_post, m_w_up, m_conv_w, m_conv_b, m_w_down, v_norm_mix_pre, v_norm_mix_post, v_w_in, v_b_forget, v_w_pool, v_pool_scale, v_w_mix_out, v_norm_mem, v_norm_xa_pre, v_norm_xa_post, v_w_xq, v_w_xkv, v_w_xo, v_norm_ffn_pre, v_norm_ffn_post, v_w_up, v_conv_w, v_conv_b, v_w_down):
    w = dict(norm_mix_pre=norm_mix_pre, norm_mix_post=norm_mix_post, w_in=w_in, b_forget=b_forget, w_pool=w_pool,
             pool_scale=pool_scale, w_mix_out=w_mix_out, norm_mem=norm_mem, norm_xa_pre=norm_xa_pre, norm_xa_post=norm_xa_post,
             w_xq=w_xq, w_xkv=w_xkv, w_xo=w_xo, norm_ffn_pre=norm_ffn_pre, norm_ffn_post=norm_ffn_post, w_up=w_up,
             conv_w=conv_w, conv_b=conv_b, w_down=w_down)
    m = dict(norm_mix_pre=m_norm_mix_pre, norm_mix_post=m_norm_mix_post, w_in=m_w_in, b_forget=m_b_forget, w_pool=m_w_pool,
             pool_scale=m_pool_scale, w_mix_out=m_w_mix_out, norm_mem=m_norm_mem, norm_xa_pre=m_norm_xa_pre,
             norm_xa_post=m_norm_xa_post, w_xq=m_w_xq, w_xkv=m_w_xkv, w_xo=m_w_xo, norm_ffn_pre=m_norm_ffn_pre,
             norm_ffn_post=m_norm_ffn_post, w_up=m_w_up, conv_w=m_conv_w, conv_b=m_conv_b, w_down=m_w_down)
    v = dict(norm_mix_pre=v_norm_mix_pre, norm_mix_post=v_norm_mix_post, w_in=v_w_in, b_forget=v_b_forget, w_pool=v_w_pool,
             pool_scale=v_pool_scale, w_mix_out=v_w_mix_out, norm_mem=v_norm_mem, norm_xa_pre=v_norm_xa_pre,
             norm_xa_post=v_norm_xa_post, w_xq=v_w_xq, w_xkv=v_w_xkv, w_xo=v_w_xo, norm_ffn_pre=v_norm_ffn_pre,
             norm_ffn_post=v_norm_ffn_post, w_up=v_w_up, conv_w=v_conv_w, conv_b=v_conv_b, w_down=v_w_down)
    chip = 2 * lax.axis_index("x") + lax.axis_index("y")

    core_id = lax.axis_index("c").astype(jnp.int32).reshape(1)
    chip_id = chip.astype(jnp.int32).reshape(1)

    shard2d = {n: w[n][0] for n in BIG}
    p = _local_params(w)
    comm = _StepComm(w, shard2d, conv_w, core_id, chip_id)
    grad_x, g, loss_cols = _local_step(x[0], mem[0], loss_target[0], p, comm)

    reduced_late = [_mesh_sum("mesh_sum_" + n, chip_id, r, own)
                    for n, r, own in zip(comm.late, comm.received_late, comm.partial_late)]
    names = comm.late + comm.early
    reduced = reduced_late + comm.reduced
    reduced_sibling = list(_swap_reduced(reduced_late).run("swap_reduced_late")) + comm.reduced_sibling
    grads = {}

    gw_pool = jnp.stack([g["w_pool_full"][64 * gi:64 * (gi + 1), 64 * gi:64 * (gi + 1)] for gi in range(4)])
    dcwb = g["cwb"]
    g_conv_w = dcwb[:, 0:3, :].transpose(1, 0, 2).reshape(3, 2 * D_FF)
    g_conv_b = dcwb[:, 3, :].reshape(2 * D_FF)
    small_g = dict(norm_mix_pre=g["norm_mix_pre"], norm_mix_post=g["norm_mix_post"], b_forget=g["bf_pad"][:, :HEADS],
                   w_pool=gw_pool, pool_scale=g["pool_scale"], norm_mem=g["norm_mem"], norm_xa_pre=g["norm_xa_pre"],
                   norm_xa_post=g["norm_xa_post"], norm_ffn_pre=g["norm_ffn_pre"], norm_ffn_post=g["norm_ffn_post"],
                   conv_b=g_conv_b)
    local_buf, offs = _pack([small_g[n] for n in SMALL] + [g_conv_w, loss_cols])

    delta, new_m, new_v = {}, {}, {}
    for n, g_mine, g_sibling in zip(names, reduced, reduced_sibling):
        gn, d, nm, nv = _adamw_halves("adamw_" + n, core_id, shard2d[n], g_mine, g_sibling, m[n][0], v[n][0])
        grads[n], delta[n], new_m[n], new_v[n] = gn[None], d[None], nm[None], nv[None]
    place = (2 * chip + lax.axis_index("c")).astype(jnp.int32).reshape(1)
    buf = _sum_devices(place, _gather_small(local_buf).run("gather_small")[0], local_buf)
    for n, off in zip(SMALL, offs):
        grads[n] = _unpack(buf, off, w[n])
    g_conv_w = _unpack(buf, offs[len(SMALL)], g_conv_w)
    grads["conv_w"] = lax.dynamic_slice_in_dim(g_conv_w, chip * (2 * D_FF // N_CHIPS), 2 * D_FF // N_CHIPS, axis=1).reshape(conv_w.shape)
    loss = jnp.sum(_unpack(buf, offs[len(SMALL) + 1], loss_cols))
    small_names = SMALL + ("conv_w",)
    packed = [_pack([d[n] for n in small_names])[0] for d in (w, grads, m, v)]
    offs = _pack([w[n] for n in small_names])[1]
    d, nm, nv = _adamw("adamw_small", *packed)
    for n, off in zip(small_names, offs):
        delta[n], new_m[n], new_v[n] = _unpack(d, off, w[n]), _unpack(nm, off, w[n]), _unpack(nv, off, w[n])

    return (loss, grad_x[None], *[grads[n] for n in ORDER], *[delta[n] for n in ORDER], *[new_m[n] for n in ORDER],
            *[new_v[n] for n in ORDER])
```

```python
import functools

import jax
import jax.numpy as jnp
import numpy as np
from jax import lax
from jax.experimental import pallas as pl
from jax.experimental.pallas import tpu as pltpu

F32 = jnp.float32
BF16 = jnp.bfloat16
MESH = pl.DeviceIdType.MESH
ANY = pl.BlockSpec(memory_space=pl.ANY)
VMEM_SPEC = pl.BlockSpec(memory_space=pltpu.VMEM)

S = 4096
D = 1024
MEM = 256
D_POOL = 256
HEADS = 12
DH = 64
D_FOX = HEADS * DH
D_IN = D_POOL + 3 * D_FOX + HEADS
F_OFF = D_POOL + 3 * D_FOX
Q_OFF, K_OFF, V_OFF = D_POOL, D_POOL + D_FOX, D_POOL + 2 * D_FOX
XA_HEADS = 4
XA_DH = 256
D_FF = 4096
EPS = 1e-6
N_CHIPS = 4
ADAM_LR, ADAM_B1, ADAM_B2, ADAM_EPS, ADAM_WD, ADAM_STEP = 0.001, 0.9, 0.999, 1e-08, 0.01, 10

LANES = 128
SUBLANES = 8
D_IN_PAD = 21 * LANES
TR = 512
TILE_BYTES = 2 * 1024 * 1024
NEG = -1e30
VMEM_LIMIT = 52 * 1024 * 1024

NN = (((1,), (0,)), ((), ()))
NT = (((1,), (1,)), ((), ()))
TN = (((0,), (0,)), ((), ()))


def _dot(a, b, dims=NN):
    return lax.dot_general(a, b, dims, preferred_element_type=F32)


def _params(sem):
    return pltpu.CompilerParams(dimension_semantics=sem, vmem_limit_bytes=VMEM_LIMIT)


def _split3(x):
    hi = x.astype(BF16)
    r = x - hi.astype(F32)
    mid = r.astype(BF16)
    lo = (r - mid.astype(F32)).astype(BF16)
    return hi, mid, lo


def _split3_f32(x):
    hi = x.astype(BF16).astype(F32)
    r = x - hi
    mid = r.astype(BF16).astype(F32)
    return hi, mid, r - mid


def _lane_iota(shape):
    return lax.broadcasted_iota(jnp.int32, shape, len(shape) - 1)


def _row_iota(shape):
    return lax.broadcasted_iota(jnp.int32, shape, len(shape) - 2)


def _mm(name, a, b, a_spec, b_spec, out_shape, out_spec, grid, dims, acc_shape, ex=None):
    nk = grid[2]
    if ex is not None:
        return _mm_hosting(name, a, b, a_spec, b_spec, out_shape, out_spec, grid, dims, ex)

    def body(a_ref, b_ref, o_ref, *scr):
        p = _dot(a_ref[...], b_ref[...], dims)
        if nk == 1:
            o_ref[...] = p.astype(o_ref.dtype)
        else:
            acc = scr[0]
            k = pl.program_id(2)

            @pl.when(k == 0)
            def _():
                acc[...] = p

            @pl.when(k > 0)
            def _():
                acc[...] += p

            @pl.when(k == nk - 1)
            def _():
                o_ref[...] = acc[...].astype(o_ref.dtype)

    return pl.pallas_call(
        body, name=name, grid=grid, in_specs=[a_spec, b_spec], out_specs=out_spec, out_shape=out_shape,
        scratch_shapes=[pltpu.VMEM(acc_shape, F32)] if nk > 1 else [],
        compiler_params=_params(("parallel", "parallel", "arbitrary")),
    )(a, b)


def _mm_hosting(name, a, b, a_spec, b_spec, out_shape, out_spec, grid, dims, ex):
    assert grid[2] == 1
    n = len(ex.ins)

    def body(*refs):
        i, j = pl.program_id(0), pl.program_id(1)
        last = (i == grid[0] - 1) & (j == grid[1] - 1)
        (a_ref, b_ref), (o_ref,), _, begin, end = _hosted(ex, refs, 2, 1, (i == 0) & (j == 0), last, last)
        begin()
        o_ref[...] = _dot(a_ref[...], b_ref[...], dims).astype(o_ref.dtype)
        end()

    res = pl.pallas_call(
        body, name=name, grid=grid, in_specs=[a_spec, b_spec] + [ANY] * n, out_specs=[out_spec] + [ANY] * n,
        out_shape=[out_shape] + ex.out_shapes, scratch_shapes=ex.scratch(),
        compiler_params=_params(("arbitrary", "arbitrary", "arbitrary")),
    )(a, b, *ex.ins)
    return res[0], res[1:]


def _mm_nn(name, a, b, out_dtype, tm, tn):
    m, k = a.shape
    n = b.shape[1]
    return _mm(name, a, b, pl.BlockSpec((tm, k), lambda i, j, kk: (i, 0)), pl.BlockSpec((k, tn), lambda i, j, kk: (0, j)),
               jax.ShapeDtypeStruct((m, n), out_dtype), pl.BlockSpec((tm, tn), lambda i, j, kk: (i, j)),
               (m // tm, n // tn, 1), NN, (tm, tn))


def _mm_nt(name, a, b, out_dtype, tm, tn, ex=None):
    m, k = a.shape
    n = b.shape[0]
    return _mm(name, a, b, pl.BlockSpec((tm, k), lambda i, j, kk: (i, 0)), pl.BlockSpec((tn, k), lambda i, j, kk: (j, 0)),
               jax.ShapeDtypeStruct((m, n), out_dtype), pl.BlockSpec((tm, tn), lambda i, j, kk: (i, j)),
               (m // tm, n // tn, 1), NT, (tm, tn), ex)


def _mm_tn(name, a, b, tka, tn, ex=None):
    t, ka = a.shape
    n = b.shape[1]
    return _mm(name, a, b, pl.BlockSpec((t, tka), lambda i, j, kk: (0, i)), pl.BlockSpec((t, tn), lambda i, j, kk: (0, j)),
               jax.ShapeDtypeStruct((ka, n), F32), pl.BlockSpec((tka, tn), lambda i, j, kk: (i, j)),
               (ka // tka, n // tn, 1), TN, (tka, tn), ex)


def _d_h3(dhid, w_up, ex):
    tm = tn = 512
    shard = 2 * D_FF // N_CHIPS
    per_plane = D_FF // shard
    grid = (S // tm, D // tn)
    n = len(ex.ins)

    def body(*refs):
        i, j = pl.program_id(0), pl.program_id(1)
        first = (i == 0) & (j == 0)
        (a_ref, b_ref), (o_ref,), _, begin, end = _hosted(ex, refs, 2, 1, first, first, (i == grid[0] - 1) & (j == grid[1] - 1))
        begin()
        acc = None
        for k in range(N_CHIPS):
            cols = slice(shard * (k % per_plane), shard * (k % per_plane + 1))
            part = _dot(a_ref[k // per_plane, :, cols], b_ref[k], NT)
            acc = part if acc is None else acc + part
        o_ref[...] = acc
        end()

    res = pl.pallas_call(
        body, name="d_h3", grid=grid,
        in_specs=[pl.BlockSpec((2, tm, D_FF), lambda i, j: (0, i, 0)),
                  pl.BlockSpec((N_CHIPS, tn, shard), lambda i, j: (0, j, 0))] + [ANY] * n,
        out_specs=[pl.BlockSpec((tm, tn), lambda i, j: (i, j))] + [ANY] * n,
        out_shape=[jax.ShapeDtypeStruct((S, D), F32)] + ex.out_shapes, scratch_shapes=ex.scratch(),
        compiler_params=_params(("arbitrary", "arbitrary")),
    )(dhid, w_up, *ex.ins)
    return res[0], res[1:]


def _rms(x, g):
    r = lax.rsqrt(jnp.mean(x * x, axis=-1, keepdims=True) + EPS)
    return x * r * g


def _rms_bwd(x, g, dy):
    r = lax.rsqrt(jnp.mean(x * x, axis=-1, keepdims=True) + EPS)
    xh = x * r
    dxh = dy * g
    dx = r * (dxh - xh * jnp.mean(dxh * xh, axis=-1, keepdims=True))
    return dx, jnp.sum(dy * xh, axis=0, keepdims=True)


def _row_spec(tr, width):
    return pl.BlockSpec((tr, width), lambda i: (i, 0))


def _vec_spec(width):
    return pl.BlockSpec((1, width), lambda i: (0, 0))


def _norm_fwd(name, x, g, ex=None):
    rows, width = x.shape
    tr = min(TR, rows)
    steps = rows // tr
    hosted = ex if ex is not None else _no_exchange()
    n = len(hosted.ins)

    def body(*refs):
        i = pl.program_id(0)
        (x_ref, g_ref), (h_ref,), _, begin, end = _hosted(hosted, refs, 2, 1, i == 0, i == steps - 1, i == steps - 1)
        begin()
        h_ref[...] = _rms(x_ref[...], g_ref[...]).astype(BF16)
        end()

    res = pl.pallas_call(
        body, name=name, grid=(steps,), in_specs=[_row_spec(tr, width), _vec_spec(width)] + [ANY] * n,
        out_specs=[_row_spec(tr, width)] + [ANY] * n,
        out_shape=[jax.ShapeDtypeStruct((rows, width), BF16)] + hosted.out_shapes, scratch_shapes=hosted.scratch(),
        compiler_params=_params(("arbitrary",)),
    )(x, g, *hosted.ins)
    return res[0] if ex is None else (res[0], res[1:])


def _proj_resid_norm(name, a, w, xp, g_post, g_pre, w_next=None):
    def body(a_ref, w_ref, xp_ref, gpost_ref, gpre_ref, *rest):
        y_ref, xn_ref, h_ref = rest[-3:] if w_next is None else rest[1:4]
        y = _dot(a_ref[...], w_ref[...])
        y_ref[...] = y
        xn = xp_ref[...] + _rms(y, gpost_ref[...])
        xn_ref[...] = xn
        h = _rms(xn, gpre_ref[...]).astype(BF16)
        h_ref[...] = h
        if w_next is not None:
            rest[4][...] = _dot(h, rest[0][...]).astype(BF16)

    mat = pl.BlockSpec((D, D), lambda i: (0, 0))
    more = [] if w_next is None else [w_next]
    return pl.pallas_call(
        body, name=name, grid=(S // TR,),
        in_specs=[_row_spec(TR, D), mat, _row_spec(TR, D), _vec_spec(D), _vec_spec(D)] + [mat] * len(more),
        out_specs=[_row_spec(TR, D)] * (3 + len(more)),
        out_shape=[jax.ShapeDtypeStruct((S, D), F32), jax.ShapeDtypeStruct((S, D), F32), jax.ShapeDtypeStruct((S, D), BF16)]
        + [jax.ShapeDtypeStruct((S, D), BF16)] * len(more),
        compiler_params=_params(("parallel",)),
    )(a, w, xp, g_post, g_pre, *more)


def _down_loss_bwd(act, w_down, x3, g_post, target):
    def body(a_ref, w_ref, x_ref, g_ref, t_ref, dres_ref, dy_ref, dg_ref, loss_ref):
        i = pl.program_id(0)

        @pl.when(i == 0)
        def _():
            dg_ref[...] = jnp.zeros_like(dg_ref)
            loss_ref[...] = jnp.zeros_like(loss_ref)

        y = _dot(a_ref[...], w_ref[...])
        g = g_ref[...]
        e = x_ref[...] + _rms(y, g) - t_ref[...]
        loss_ref[...] += jnp.sum(e * e, axis=0, keepdims=True) * (0.5 / D)
        dres = e * (1.0 / D)
        dres_ref[...] = dres
        dy, dg = _rms_bwd(y, g, dres)
        dy_ref[...] = dy.astype(BF16)
        dg_ref[...] += dg

    return pl.pallas_call(
        body, name="down_loss_bwd", grid=(S // TR,),
        in_specs=[_row_spec(TR, D_FF), pl.BlockSpec((D_FF, D), lambda i: (0, 0)), _row_spec(TR, D), _vec_spec(D),
                  _row_spec(TR, D)],
        out_specs=[_row_spec(TR, D), _row_spec(TR, D), _vec_spec(D), _vec_spec(D)],
        out_shape=[jax.ShapeDtypeStruct((S, D), F32), jax.ShapeDtypeStruct((S, D), BF16),
                   jax.ShapeDtypeStruct((1, D), F32), jax.ShapeDtypeStruct((1, D), F32)],
        compiler_params=_params(("arbitrary",)),
    )(act, w_down, x3, g_post, target)


def _mid_bwd(name, dres, xcur, g_pre, dh, yprev, g_post, w):
    def body(dres_ref, x_ref, gpre_ref, dh_ref, y_ref, gpost_ref, w_ref, dx_ref, dy_ref, da_ref, dgpre_ref, dgpost_ref):
        i = pl.program_id(0)

        @pl.when(i == 0)
        def _():
            dgpre_ref[...] = jnp.zeros_like(dgpre_ref)
            dgpost_ref[...] = jnp.zeros_like(dgpost_ref)

        dxn, dgpre = _rms_bwd(x_ref[...], gpre_ref[...], dh_ref[...])
        dx = dres_ref[...] + dxn
        dx_ref[...] = dx
        dy, dgpost = _rms_bwd(y_ref[...], gpost_ref[...], dx)
        dy = dy.astype(BF16)
        dy_ref[...] = dy
        da_ref[...] = _dot(dy, w_ref[...], NT).astype(BF16)
        dgpre_ref[...] += dgpre
        dgpost_ref[...] += dgpost

    return pl.pallas_call(
        body, name=name, grid=(S // TR,),
        in_specs=[_row_spec(TR, D), _row_spec(TR, D), _vec_spec(D), _row_spec(TR, D), _row_spec(TR, D), _vec_spec(D),
                  pl.BlockSpec((D, D), lambda i: (0, 0))],
        out_specs=[_row_spec(TR, D), _row_spec(TR, D), _row_spec(TR, D), _vec_spec(D), _vec_spec(D)],
        out_shape=[jax.ShapeDtypeStruct((S, D), F32), jax.ShapeDtypeStruct((S, D), BF16), jax.ShapeDtypeStruct((S, D), BF16),
                   jax.ShapeDtypeStruct((1, D), F32), jax.ShapeDtypeStruct((1, D), F32)],
        compiler_params=_params(("arbitrary",)),
    )(dres, xcur, g_pre, dh, yprev, g_post, w)


def _first_bwd(dres, x, g, dh):
    def body(dres_ref, x_ref, g_ref, dh_ref, dx_ref, dg_ref):
        i = pl.program_id(0)

        @pl.when(i == 0)
        def _():
            dg_ref[...] = jnp.zeros_like(dg_ref)

        dxn, dg = _rms_bwd(x_ref[...], g_ref[...], dh_ref[...])
        dx_ref[...] = dres_ref[...] + dxn
        dg_ref[...] += dg

    return pl.pallas_call(
        body, name="first_bwd", grid=(S // TR,),
        in_specs=[_row_spec(TR, D), _row_spec(TR, D), _vec_spec(D), _row_spec(TR, D)],
        out_specs=[_row_spec(TR, D), _vec_spec(D)],
        out_shape=[jax.ShapeDtypeStruct((S, D), F32), jax.ShapeDtypeStruct((1, D), F32)],
        compiler_params=_params(("arbitrary",)),
    )(dres, x, g, dh)


def _gain_bwd(name, x, g, dy):
    rows, width = x.shape

    def body(x_ref, g_ref, dy_ref, dg_ref):
        _, dg = _rms_bwd(x_ref[...], g_ref[...], dy_ref[...])
        dg_ref[...] = dg

    return pl.pallas_call(
        body, name=name, grid=(1,), in_specs=[_row_spec(rows, width), _vec_spec(width), _row_spec(rows, width)],
        out_specs=_vec_spec(width), out_shape=jax.ShapeDtypeStruct((1, width), F32),
        compiler_params=_params(("arbitrary",)),
    )(x, g, dy)


CUM_Q = DH
CUM_K = DH + 3
LSE_Q = DH + 6
BOTH_ONE = DH + 9
DEN_V = DH
DELTA = DH + 1
PREP_TR = 256
PIECE_LANES = 16
FOX_FWD_BLOCK = 1024
FOX_BWD_BLOCK = 512


def _at(lane_of_even_head, h):
    return (lane_of_even_head + DH * (h % 2)) % LANES


def _data_lanes(lane, h):
    return lane >= DH if h % 2 else lane < DH


def _pair_block(ref, off, h):
    base = ((off + DH * h) // LANES) * LANES
    return ref[:, base:base + LANES]


def _cumsum_rows(x, tri, carry):
    hi, mid, lo = _split3(x)
    return _dot(tri, hi) + _dot(tri, mid) + _dot(tri, lo) + carry


def _in_proj(h1, w_in, bf_pad):
    tr = TR

    place_q = np.zeros((LANES, HEADS * LANES), np.float32)
    place_k = np.zeros((LANES, HEADS * LANES), np.float32)
    for h in range(HEADS):
        for piece in range(3):
            place_q[PIECE_LANES * piece + h, LANES * h + _at(CUM_Q, h) + piece] = 1.0
            place_k[PIECE_LANES * piece + h, LANES * h + _at(CUM_K, h) + piece] = -1.0

    def body(h_ref, w_ref, bf_ref, pq_ref, pk_ref, qa_ref, ka_ref, va_ref, u_ref, z_ref, carry_ref):
        i = pl.program_id(0)

        @pl.when(i == 0)
        def _():
            carry_ref[...] = jnp.zeros_like(carry_ref)

        proj = _dot(h_ref[...], w_ref[...])
        u_ref[...] = proj[:, :D_POOL]
        z_ref[...] = proj[:, F_OFF:F_OFF + LANES]
        lane = _lane_iota((tr, LANES))
        z = proj[:, F_OFF:F_OFF + LANES] + bf_ref[...]
        log_f = jnp.minimum(z, 0.0) - jnp.log(1.0 + jnp.exp(-jnp.abs(z)))
        log_f = jnp.where(lane < HEADS, log_f, 0.0)
        tri = jnp.where(_row_iota((tr, tr)) >= _lane_iota((tr, tr)), 1.0, 0.0).astype(BF16)
        cum = _cumsum_rows(log_f, tri, carry_ref[0:1, :])
        carry_ref[0:1, :] = cum[tr - 1:tr, :]
        c_hi, c_mid, c_lo = _split3_f32(cum)
        pieces = (c_hi + pltpu.roll(c_mid, PIECE_LANES, 1) + pltpu.roll(c_lo, 2 * PIECE_LANES, 1)).astype(BF16)
        cum_q = _dot(pieces, pq_ref[...])
        cum_k = _dot(pieces, pk_ref[...])

        def between(first, h):
            return (lane >= _at(first, h)) & (lane < _at(first, h) + 3)

        ones_q = [jnp.where(between(CUM_K, h) | (lane == _at(BOTH_ONE, h)), 1.0, 0.0) for h in range(2)]
        ones_k = [jnp.where(between(CUM_Q, h) | between(LSE_Q, h) | (lane == _at(BOTH_ONE, h)), 1.0, 0.0) for h in range(2)]
        aug_v = [jnp.where(lane == _at(DEN_V, h), 1.0, jnp.where(between(DELTA, h), -1.0, 0.0)) for h in range(2)]
        for h in range(HEADS):
            mine = slice(LANES * h, LANES * (h + 1))
            data = _data_lanes(lane, h)
            qa_ref[h] = jnp.where(data, _pair_block(proj, Q_OFF, h) * (DH ** -0.5), cum_q[:, mine] + ones_q[h % 2]).astype(BF16)
            ka_ref[h] = jnp.where(data, _pair_block(proj, K_OFF, h), cum_k[:, mine] + ones_k[h % 2]).astype(BF16)
            va_ref[h] = jnp.where(data, _pair_block(proj, V_OFF, h), aug_v[h % 2]).astype(BF16)

    head_spec = pl.BlockSpec((HEADS, tr, LANES), lambda i: (0, i, 0))
    head_shape = jax.ShapeDtypeStruct((HEADS, S, LANES), BF16)
    place_spec = pl.BlockSpec(place_q.shape, lambda i: (0, 0))
    return pl.pallas_call(
        body, name="in_proj", grid=(S // tr,),
        in_specs=[_row_spec(tr, D), pl.BlockSpec((D, D_IN_PAD), lambda i: (0, 0)), _vec_spec(LANES), place_spec, place_spec],
        out_specs=[head_spec] * 3 + [_row_spec(tr, D_POOL), _row_spec(tr, LANES)],
        out_shape=[head_shape] * 3 + [jax.ShapeDtypeStruct((S, D_POOL), F32), jax.ShapeDtypeStruct((S, LANES), F32)],
        scratch_shapes=[pltpu.VMEM((SUBLANES, LANES), F32)], compiler_params=_params(("arbitrary",)),
    )(h1, w_in, bf_pad, jnp.asarray(place_q, BF16), jnp.asarray(place_k, BF16))


def _hosted(ex, refs, n_blocked_in, n_blocked_out, first, forward_at, last):
    n = len(ex.ins)
    own_in = refs[:n_blocked_in]
    ex_in = refs[n_blocked_in:n_blocked_in + n]
    own_out = refs[n_blocked_in + n:n_blocked_in + n + n_blocked_out]
    ex_out = refs[n_blocked_in + n + n_blocked_out:n_blocked_in + 2 * n + n_blocked_out]
    rest = refs[n_blocked_in + 2 * n + n_blocked_out:]
    args = (ex_in, ex_out, rest[-2], rest[-1])

    def begin():
        @pl.when(first)
        def _():
            ex.start(*args)

        @pl.when(forward_at)
        def _():
            ex.forward(*args)

    def end():
        @pl.when(last)
        def _():
            ex.finish(*args)

    return own_in, own_out, rest[:-2], begin, end


def _fox_fwd(qa, ka, va, ex):
    BQ = BK = FOX_FWD_BLOCK
    nq = S // BQ
    n_pairs = HEADS // 2

    def body(*refs):
        p_id, i = pl.program_id(0), pl.program_id(1)
        (qa_ref, ka_ref, va_ref), (y_ref, qab_ref), (m_scr, acc_scr), begin, end = _hosted(
            ex, refs, 3, 2, (p_id == 0) & (i == 0), (p_id == n_pairs - 1) & (i == 0), (p_id == n_pairs - 1) & (i == nq - 1))
        begin()
        lane = _lane_iota((BQ, LANES))
        causal = _row_iota((BQ, BK)) >= _lane_iota((BQ, BK))
        m_scr[...] = jnp.full_like(m_scr, NEG)
        acc_scr[...] = jnp.zeros_like(acc_scr)

        def step(j, masked):
            rows = pl.ds(pl.multiple_of(j * BK, BK), BK)
            for hh in range(2):
                s = _dot(qa_ref[hh], ka_ref[hh, rows, :], NT)
                if masked:
                    s = jnp.where(causal, s, NEG)
                m_prev = m_scr[hh]
                m_new = jnp.maximum(m_prev, jnp.max(s, axis=1, keepdims=True))
                p = jnp.exp(s - jnp.tile(m_new, (1, BK // LANES)))
                acc_scr[hh] = jnp.exp(m_prev - m_new) * acc_scr[hh] + _dot(p.astype(BF16), va_ref[hh, rows, :])
                m_scr[hh] = m_new

        def full_step(j, carry):
            step(j, False)
            return carry

        lax.fori_loop(0, i, full_step, 0)
        step(i, True)
        outs = []
        for hh in range(2):
            acc = acc_scr[hh]
            den_lane, lse_lane = _at(DEN_V, hh), _at(LSE_Q, hh)
            den = jnp.broadcast_to(acc[:, den_lane:den_lane + 1], (BQ, LANES))
            outs.append(acc * (1.0 / den))
            n_hi, n_mid, n_lo = _split3(-(m_scr[hh] + jnp.log(den)))
            qab_ref[hh] = jnp.where(lane == lse_lane, n_hi,
                                    jnp.where(lane == lse_lane + 1, n_mid, jnp.where(lane == lse_lane + 2, n_lo, qa_ref[hh])))
        y_ref[...] = jnp.where(lane < DH, outs[0], outs[1]).astype(BF16)
        end()

    pair_rows = pl.BlockSpec((2, BQ, LANES), lambda p, i: (p, i, 0))
    pair_all = pl.BlockSpec((2, S, LANES), lambda p, i: (p, 0, 0))
    n = len(ex.ins)
    res = pl.pallas_call(
        body, name="fox_fwd", grid=(n_pairs, nq), in_specs=[pair_rows, pair_all, pair_all] + [ANY] * n,
        out_specs=[pl.BlockSpec((BQ, LANES), lambda p, i: (i, D_POOL // LANES + p)), pair_rows] + [ANY] * n,
        out_shape=[jax.ShapeDtypeStruct((S, D), BF16), jax.ShapeDtypeStruct((HEADS, S, LANES), BF16)] + ex.out_shapes,
        scratch_shapes=[pltpu.VMEM((2, BQ, LANES), F32), pltpu.VMEM((2, BQ, LANES), F32)] + ex.scratch(),
        compiler_params=_params(("arbitrary", "arbitrary")),
    )(qa, ka, va, *ex.ins)
    return res[0], res[1], res[2:]


def _bwd_xa_mix(dqx, w_xq, dres, x2, g_pre, y1, g_post, w_mix_out, ycat, ex):
    steps = S // TR
    n = len(ex.ins)

    def body(*refs):
        i = pl.program_id(0)
        ((dq_ref, wq_ref, dres_ref, x_ref, gpre_ref, y_ref, gpost_ref, wm_ref, ycat_ref),
         (dx_ref, dy_ref, dgpre_ref, dgpost_ref, dp_ref, doa_ref), _, begin, end) = _hosted(
            ex, refs, 9, 6, i == 0, i == 0, i == steps - 1)
        begin()

        @pl.when(i == 0)
        def _():
            dgpre_ref[...] = jnp.zeros_like(dgpre_ref)
            dgpost_ref[...] = jnp.zeros_like(dgpost_ref)

        dxn, dgpre = _rms_bwd(x_ref[...], gpre_ref[...], _dot(dq_ref[...], wq_ref[...], NT))
        dx = dres_ref[...] + dxn
        dx_ref[...] = dx
        dy, dgpost = _rms_bwd(y_ref[...], gpost_ref[...], dx)
        dy = dy.astype(BF16)
        dy_ref[...] = dy
        dgpre_ref[...] += dgpre
        dgpost_ref[...] += dgpost

        d = _dot(dy, wm_ref[...], NT)
        dp_ref[...] = d[:, :D_POOL]
        lane = _lane_iota((TR, LANES))
        low = lane < DH
        for p in range(HEADS // 2):
            cols = slice(D_POOL + LANES * p, D_POOL + LANES * (p + 1))
            do = d[:, cols]
            prod = do * ycat_ref[:, cols].astype(F32)
            deltas = (jnp.sum(jnp.where(low, prod, 0.0), axis=1, keepdims=True),
                      jnp.sum(jnp.where(low, 0.0, prod), axis=1, keepdims=True))
            for hh in range(2):
                d_hi, d_mid, d_lo = _split3_f32(deltas[hh])
                dl = _at(DELTA, hh)
                aug = jnp.where(lane == dl, d_hi, jnp.where(lane == dl + 1, d_mid, jnp.where(lane == dl + 2, d_lo, 0.0)))
                doa_ref[2 * p + hh] = jnp.where(_data_lanes(lane, hh), do, aug).astype(BF16)
        end()

    mat = pl.BlockSpec((D, D), lambda i: (0, 0))
    res = pl.pallas_call(
        body, name="bwd_xa_mix", grid=(steps,),
        in_specs=[_row_spec(TR, D), mat, _row_spec(TR, D), _row_spec(TR, D), _vec_spec(D), _row_spec(TR, D), _vec_spec(D), mat,
                  _row_spec(TR, D)] + [ANY] * n,
        out_specs=[_row_spec(TR, D), _row_spec(TR, D), _vec_spec(D), _vec_spec(D), _row_spec(TR, D_POOL),
                   pl.BlockSpec((HEADS, TR, LANES), lambda i: (0, i, 0))] + [ANY] * n,
        out_shape=[jax.ShapeDtypeStruct((S, D), F32), jax.ShapeDtypeStruct((S, D), BF16), jax.ShapeDtypeStruct((1, D), F32),
                   jax.ShapeDtypeStruct((1, D), F32), jax.ShapeDtypeStruct((S, D_POOL), F32),
                   jax.ShapeDtypeStruct((HEADS, S, LANES), BF16)] + ex.out_shapes,
        scratch_shapes=ex.scratch(), compiler_params=_params(("arbitrary",)),
    )(dqx, w_xq, dres, x2, g_pre, y1, g_post, w_mix_out, ycat, *ex.ins)
    return res[:6], res[6:]


def _fox_bwd(qab, doa, ka, va, ex):
    BQ = BK = FOX_BWD_BLOCK
    nk = S // BK
    n_pairs = HEADS // 2

    def body(*refs):
        p_id, j = pl.program_id(0), pl.program_id(1)
        (qab_ref, doa_ref, ka_ref, va_ref), (dqa_ref, dka_ref, dva_ref), _, begin, end = _hosted(
            ex, refs, 4, 3, (p_id == 0) & (j == 0), (p_id == n_pairs - 1) & (j == 0), (p_id == n_pairs - 1) & (j == nk - 1))
        begin()

        @pl.when(j == 0)
        def _():
            dqa_ref[...] = jnp.zeros_like(dqa_ref)

        causal = _row_iota((BQ, BK)) >= _lane_iota((BQ, BK))
        dka_ref[...] = jnp.zeros_like(dka_ref)
        dva_ref[...] = jnp.zeros_like(dva_ref)

        def step(i, masked):
            rows = pl.ds(pl.multiple_of(i * BQ, BQ), BQ)
            for hh in range(2):
                kb = ka_ref[hh]
                q = qab_ref[hh, rows, :]
                do = doa_ref[hh, rows, :]
                s = _dot(q, kb, NT)
                if masked:
                    s = jnp.where(causal, s, NEG)
                p = jnp.exp(s)
                ds = p * _dot(do, va_ref[hh], NT)
                pb = p.astype(BF16)
                dsb = ds.astype(BF16)
                dva_ref[hh] += _dot(pb, do, TN)
                dka_ref[hh] += _dot(dsb, q, TN)
                dqa_ref[hh, rows, :] += _dot(dsb, kb)

        def full_step(i, carry):
            step(i, False)
            return carry

        step(j, True)
        lax.fori_loop(j + 1, nk, full_step, 0)
        end()

    pair_all = pl.BlockSpec((2, S, LANES), lambda p, j: (p, 0, 0))
    pair_rows = pl.BlockSpec((2, BK, LANES), lambda p, j: (p, j, 0))
    shape = jax.ShapeDtypeStruct((HEADS, S, LANES), F32)
    n = len(ex.ins)
    res = pl.pallas_call(
        body, name="fox_bwd", grid=(n_pairs, nk), in_specs=[pair_all, pair_all, pair_rows, pair_rows] + [ANY] * n,
        out_specs=[pair_all, pair_rows, pair_rows] + [ANY] * n, out_shape=[shape] * 3 + ex.out_shapes,
        scratch_shapes=ex.scratch(), compiler_params=_params(("arbitrary", "arbitrary")),
    )(qab, doa, ka, va, *ex.ins)
    return res[0], res[1], res[2], res[3:]


def _fox_bwd_post(dqa, dka, dva, du, proj, bf_pad):
    tr = PREP_TR
    nt = S // tr

    pick = np.zeros((HEADS * LANES, LANES), np.float32)
    for h in range(HEADS):
        pick[LANES * h + _at(BOTH_ONE, h), h] = 1.0

    def body(dqa_ref, dka_ref, dva_ref, du_ref, z_ref, bf_ref, pick_ref, dp_ref, dbf_ref, carry_ref):
        i = pl.program_id(0)

        @pl.when(i == 0)
        def _():
            carry_ref[...] = jnp.zeros_like(carry_ref)
            dbf_ref[...] = jnp.zeros_like(dbf_ref)

        lane = _lane_iota((tr, LANES))
        diff = jnp.concatenate([dqa_ref[h] - dka_ref[h] for h in range(HEADS)], axis=1)
        hi = diff.astype(BF16)
        dcum = _dot(hi, pick_ref[...]) + _dot((diff - hi.astype(F32)).astype(BF16), pick_ref[...])
        tri =jnp.where(_lane_iota((tr, tr)) >= _row_iota((tr, tr)), 1.0, 0.0).astype(BF16)
        dlog_f = _cumsum_rows(dcum, tri, carry_ref[0:1, :])
        carry_ref[0:1, :] = dlog_f[0:1, :]
        z = z_ref[...] + bf_ref[...]
        df = jnp.where(lane < HEADS, dlog_f / (1.0 + jnp.exp(z)), 0.0)
        dbf_ref[...] += jnp.sum(df, axis=0, keepdims=True)

        dp_ref[:, 0:D_POOL] = du_ref[...].astype(BF16)
        low = lane < DH
        for ref, off, scale in ((dqa_ref, Q_OFF, DH ** -0.5), (dka_ref, K_OFF, 1.0), (dva_ref, V_OFF, 1.0)):
            for p in range(HEADS // 2):
                blk = jnp.where(low, ref[2 * p], ref[2 * p + 1])
                dp_ref[:, off + LANES * p:off + LANES * (p + 1)] = (blk * scale).astype(BF16)
        dp_ref[:, F_OFF:F_OFF + LANES] = df.astype(BF16)

    head_spec = pl.BlockSpec((HEADS, tr, LANES), lambda i: (0, nt - 1 - i, 0))
    return pl.pallas_call(
        body, name="fox_bwd_post", grid=(nt,),
        in_specs=[head_spec, head_spec, head_spec, pl.BlockSpec((tr, D_POOL), lambda i: (nt - 1 - i, 0)),
                  pl.BlockSpec((tr, LANES), lambda i: (nt - 1 - i, 0)), _vec_spec(LANES),
                  pl.BlockSpec(pick.shape, lambda i: (0, 0))],
        out_specs=[pl.BlockSpec((tr, D_IN_PAD), lambda i: (nt - 1 - i, 0)), _vec_spec(LANES)],
        out_shape=[jax.ShapeDtypeStruct((S, D_IN_PAD), BF16), jax.ShapeDtypeStruct((1, LANES), F32)],
        scratch_shapes=[pltpu.VMEM((SUBLANES, LANES), F32)],
        compiler_params=_params(("arbitrary",)),
    )(dqa, dka, dva, du, proj, bf_pad, jnp.asarray(pick, BF16))


POOL_HALO = 16


def _by_group(lane, a2, a4, a8, a16):
    return jnp.where(lane < 64, a2, jnp.where(lane < 128, a4, jnp.where(lane < 192, a8, a16)))


def _window_count(lane, t):
    return jnp.minimum(t + 1, _by_group(lane, 2, 4, 8, 16)).astype(F32)


def _pool_diff(u, halo, first, tile):
    n = TR + POOL_HALO
    ext = jnp.concatenate([jnp.where(first, 0.0, halo), u], axis=0)
    s2 = ext + pltpu.roll(ext, 1, 0)
    s4 = s2 + pltpu.roll(s2, 2, 0)
    s8 = s4 + pltpu.roll(s4, 4, 0)
    s16 = s8 + pltpu.roll(s8, 8, 0)
    lane = _lane_iota((n, D_POOL))
    win = _by_group(lane, s2, s4, s8, s16)[POOL_HALO:]
    lane = _lane_iota((TR, D_POOL))
    t = tile * TR + _row_iota((TR, D_POOL))
    return win / _window_count(lane, t) - u


def _prev_halo(rows, width, col):
    per = TR // rows
    return pl.BlockSpec((rows, width), lambda i: (jnp.maximum(i * per - 1, 0), col))


def _next_halo(rows, width, col):
    per = TR // rows
    return pl.BlockSpec((rows, width), lambda i: (jnp.minimum((i + 1) * per, S // rows - 1), col))


def _pool_fwd(proj, w_bd, ps, ycat):
    def body(u_ref, halo_ref, w_ref, ps_ref, ycat_ref, y_ref):
        i = pl.program_id(0)
        diff = _pool_diff(u_ref[...], halo_ref[...], i == 0, i)
        y_ref[...] = (_dot(diff.astype(BF16), w_ref[...]) * ps_ref[...]).astype(BF16)

    return pl.pallas_call(
        body, name="pool_fwd", grid=(S // TR,),
        in_specs=[_row_spec(TR, D_POOL), _prev_halo(POOL_HALO, D_POOL, 0),
                  pl.BlockSpec((D_POOL, D_POOL), lambda i: (0, 0)), _vec_spec(D_POOL), ANY],
        out_specs=_row_spec(TR, D_POOL), out_shape=jax.ShapeDtypeStruct((S, D), BF16), input_output_aliases={4: 0},
        compiler_params=_params(("parallel",)),
    )(proj, proj, w_bd, ps, ycat)


def _pool_bwd(proj, dycat, w_bd, w_bd_t, ps):
    nt = S // TR
    n = TR + POOL_HALO

    def body(u_ref, halo_ref, dy_ref, dyn_ref, w_ref, wt_ref, ps_ref, du_ref, dw_ref, dps_ref):
        i = pl.program_id(0)

        @pl.when(i == 0)
        def _():
            dw_ref[...] = jnp.zeros_like(dw_ref)
            dps_ref[...] = jnp.zeros_like(dps_ref)

        diff = _pool_diff(u_ref[...], halo_ref[...], i == 0, i).astype(BF16)
        dy = dy_ref[...]
        dps_ref[...] += jnp.sum(dy * _dot(diff, w_ref[...]), axis=0, keepdims=True)
        dy_ext = jnp.concatenate([dy, jnp.where(i == nt - 1, 0.0, dyn_ref[...])], axis=0)
        dmixed = (dy_ext * ps_ref[...]).astype(BF16)
        ddiff = _dot(dmixed, wt_ref[...])
        dw_ref[...] += _dot(diff, dmixed[:TR], TN)
        lane = _lane_iota((n, D_POOL))
        t = i * TR + _row_iota((n, D_POOL))
        e = ddiff / _window_count(lane, t)
        f2 = e + pltpu.roll(e, n - 1, 0)
        f4 = f2 + pltpu.roll(f2, n - 2, 0)
        f8 = f4 + pltpu.roll(f4, n - 4, 0)
        f16 = f8 + pltpu.roll(f8, n - 8, 0)
        du_ref[...] = _by_group(lane, f2, f4, f8, f16)[:TR] - ddiff[:TR]

    mat = pl.BlockSpec((D_POOL, D_POOL), lambda i: (0, 0))
    return pl.pallas_call(
        body, name="pool_bwd", grid=(nt,),
        in_specs=[_row_spec(TR, D_POOL), _prev_halo(POOL_HALO, D_POOL, 0), _row_spec(TR, D_POOL),
                  _next_halo(POOL_HALO, D_POOL, 0), mat, mat, _vec_spec(D_POOL)],
        out_specs=[_row_spec(TR, D_POOL), mat, _vec_spec(D_POOL)],
        out_shape=[jax.ShapeDtypeStruct((S, D_POOL), F32), jax.ShapeDtypeStruct((D_POOL, D_POOL), F32),
                   jax.ShapeDtypeStruct((1, D_POOL), F32)],
        compiler_params=_params(("arbitrary",)),
    )(proj, proj, dycat, dycat, w_bd, w_bd_t, ps)


def _xa_probs(q, k):
    s = _dot(q, k, NT) * (XA_DH ** -0.5)
    e = jnp.exp(s - jnp.max(s, axis=-1, keepdims=True))
    return e * (1.0 / jnp.sum(e, axis=-1, keepdims=True))


def _xattn_fwd(qx, kv):
    def body(q_ref, kv_ref, o_ref):
        for h in range(XA_HEADS):
            cols = slice(XA_DH * h, XA_DH * (h + 1))
            vcols = slice(D + XA_DH * h, D + XA_DH * (h + 1))
            p = _xa_probs(q_ref[:, cols], kv_ref[:, cols])
            o_ref[:, cols] = _dot(p.astype(BF16), kv_ref[:, vcols]).astype(BF16)

    return pl.pallas_call(
        body, name="xattn_fwd", grid=(S // TR,),
        in_specs=[_row_spec(TR, D), pl.BlockSpec((MEM, 2 * D), lambda i: (0, 0))],
        out_specs=_row_spec(TR, D), out_shape=jax.ShapeDtypeStruct((S, D), BF16),
        compiler_params=_params(("parallel",)),
    )(qx, kv)


def _xattn_bwd(qx, kv, dxo):
    def body(q_ref, kv_ref, do_ref, dq_ref, dkv_ref):
        i = pl.program_id(0)

        @pl.when(i == 0)
        def _():
            dkv_ref[...] = jnp.zeros_like(dkv_ref)

        for h in range(XA_HEADS):
            cols = slice(XA_DH * h, XA_DH * (h + 1))
            vcols = slice(D + XA_DH * h, D + XA_DH * (h + 1))
            q = q_ref[:, cols]
            k = kv_ref[:, cols]
            do = do_ref[:, cols]
            p = _xa_probs(q, k)
            dkv_ref[:, vcols] += _dot(p.astype(BF16), do, TN)
            dp = _dot(do, kv_ref[:, vcols], NT)
            ds = (p * (dp - jnp.sum(p * dp, axis=-1, keepdims=True)) * (XA_DH ** -0.5)).astype(BF16)
            dq_ref[:, cols] = _dot(ds, k).astype(BF16)
            dkv_ref[:, cols] += _dot(ds, q, TN)

    kv_spec = pl.BlockSpec((MEM, 2 * D), lambda i: (0, 0))
    return pl.pallas_call(
        body, name="xattn_bwd", grid=(S // TR,), in_specs=[_row_spec(TR, D), kv_spec, _row_spec(TR, D)],
        out_specs=[_row_spec(TR, D), kv_spec],
        out_shape=[jax.ShapeDtypeStruct((S, D), BF16), jax.ShapeDtypeStruct((MEM, 2 * D), F32)],
        compiler_params=_params(("arbitrary",)),
    )(qx, kv, dxo)


CONV_HALO = SUBLANES
TC = 512
GELU_K = 0.7978845608028654
GELU_C = 0.044715


def _conv3(ext, w, rows):
    h0 = ext[CONV_HALO:CONV_HALO + rows]
    h1 = pltpu.roll(ext, 1, 0)[CONV_HALO:CONV_HALO + rows]
    h2 = pltpu.roll(ext, 2, 0)[CONV_HALO:CONV_HALO + rows]
    return w[2:3] * h0 + w[1:2] * h1 + w[0:1] * h2 + w[3:4], (h2, h1, h0)


def _conv_specs():
    main = pl.BlockSpec((2, TR, TC), lambda j, i: (0, i, j))
    per = TR // CONV_HALO
    prev = pl.BlockSpec((2, CONV_HALO, TC), lambda j, i: (0, jnp.maximum(i * per - 1, 0), j))
    nxt = pl.BlockSpec((2, CONV_HALO, TC), lambda j, i: (0, jnp.minimum((i + 1) * per, S // CONV_HALO - 1), j))
    par = pl.BlockSpec((2, SUBLANES, TC), lambda j, i: (0, 0, j))
    return main, prev, nxt, par


def _convgate_fwd(hid, cwb):
    def body(h_ref, hp_ref, w_ref, act_ref):
        i = pl.program_id(1)
        c = []
        for g in range(2):
            ext = jnp.concatenate([jnp.where(i == 0, 0.0, hp_ref[g]), h_ref[g]], axis=0)
            c.append(_conv3(ext, w_ref[g], TR)[0])
        gate, up = c
        act_ref[...] = (jax.nn.gelu(gate, approximate=True) * up).astype(BF16)

    main, prev, _, par = _conv_specs()
    return pl.pallas_call(
        body, name="convgate_fwd", grid=(D_FF // TC, S // TR), in_specs=[main, prev, par],
        out_specs=pl.BlockSpec((TR, TC), lambda j, i: (i, j)), out_shape=jax.ShapeDtypeStruct((S, D_FF), BF16),
        compiler_params=_params(("parallel", "parallel")),
    )(hid, hid, cwb)


def _convgate_bwd(hid, dact, cwb):
    nr = S // TR
    n = TR + CONV_HALO

    def body(h_ref, hp_ref, hn_ref, da_ref, dan_ref, w_ref, dh_ref, dw_ref):
        i = pl.program_id(1)

        @pl.when(i == 0)
        def _():
            dw_ref[...] = jnp.zeros_like(dw_ref)

        da = jnp.concatenate([da_ref[...], jnp.where(i == nr - 1, 0.0, dan_ref[...])], axis=0)
        c, taps = [], []
        for g in range(2):
            ext = jnp.concatenate([jnp.where(i == 0, 0.0, hp_ref[g]), h_ref[g], hn_ref[g]], axis=0)
            cg, tg = _conv3(ext, w_ref[g], n)
            c.append(cg)
            taps.append(tg)
        gate, up = c
        th = jnp.tanh(GELU_K * (gate + GELU_C * gate * gate * gate))
        gelu = 0.5 * gate * (1.0 + th)
        dgelu = 0.5 * (1.0 + th) + 0.5 * gate * (1.0 - th * th) * GELU_K * (1.0 + 3.0 * GELU_C * gate * gate)
        for g, dc in enumerate((da * up * dgelu, da * gelu)):
            w = w_ref[g]
            dh = w[2:3] * dc[:TR] + w[1:2] * pltpu.roll(dc, n - 1, 0)[:TR] + w[0:1] * pltpu.roll(dc, n - 2, 0)[:TR]
            dh_ref[g] = dh.astype(BF16)
            dcm = dc[:TR]
            for r in range(3):
                dw_ref[g, r:r + 1, :] += jnp.sum(dcm * taps[g][r][:TR], axis=0, keepdims=True)
            dw_ref[g, 3:4, :] += jnp.sum(dcm, axis=0, keepdims=True)

    main, prev, nxt, par = _conv_specs()
    per = TR // CONV_HALO
    return pl.pallas_call(
        body, name="convgate_bwd", grid=(D_FF // TC, nr),
        in_specs=[main, prev, nxt, pl.BlockSpec((TR, TC), lambda j, i: (i, j)),
                  pl.BlockSpec((CONV_HALO, TC), lambda j, i: (jnp.minimum((i + 1) * per, S // CONV_HALO - 1), j)), par],
        out_specs=[main, par],
        out_shape=[jax.ShapeDtypeStruct((2, S, D_FF), BF16), jax.ShapeDtypeStruct((2, SUBLANES, D_FF), F32)],
        compiler_params=_params(("parallel", "arbitrary")),
    )(hid, hid, hid, dact, dact, cwb)


def _adam_update(w, g, m, v):
    m = ADAM_B1 * m + (1.0 - ADAM_B1) * g
    v = ADAM_B2 * v + (1.0 - ADAM_B2) * (g * g)
    m_hat = m / (1.0 - ADAM_B1 ** ADAM_STEP)
    v_hat = v / (1.0 - ADAM_B2 ** ADAM_STEP)
    return -ADAM_LR * (m_hat / (jnp.sqrt(v_hat) + ADAM_EPS) + ADAM_WD * w), m, v


def _row_tile(rows, cols, itemsize=4, target=TILE_BYTES):
    tr = SUBLANES
    while rows % (2 * tr) == 0 and 2 * tr * cols * itemsize <= target:
        tr *= 2
    assert rows % tr == 0, (rows, tr)
    return tr


def _adamw(name, w, g, m, v):
    rows, cols = w.shape
    tr = rows if rows * cols * 4 <= TILE_BYTES // 2 else _row_tile(rows, cols, target=TILE_BYTES // 2)

    def body(w_ref, g_ref, m_ref, v_ref, d_ref, nm_ref, nv_ref):
        d_ref[...], nm_ref[...], nv_ref[...] = _adam_update(w_ref[...], g_ref[...], m_ref[...], v_ref[...])

    spec = _row_spec(tr, cols)
    shape = jax.ShapeDtypeStruct((rows, cols), F32)
    return pl.pallas_call(
        body, name=name, grid=(rows // tr,), in_specs=[spec] * 4, out_specs=[spec] * 3, out_shape=[shape] * 3,
        compiler_params=_params(("parallel",)),
    )(w, g, m, v)


def _adamw_halves(name, core, w, g_mine, g_sibling, m, v):
    rows, cols = w.shape
    half = rows // 2
    tr = _row_tile(half, cols, target=TILE_BYTES // 2)
    per = half // tr

    def body(core_ref, w_ref, gm_ref, gs_ref, m_ref, v_ref, g_ref, d_ref, nm_ref, nv_ref):
        g = jnp.where(pl.program_id(0) // per == core_ref[0], gm_ref[...], gs_ref[...])
        g_ref[...] = g
        d_ref[...], nm_ref[...], nv_ref[...] = _adam_update(w_ref[...], g, m_ref[...], v_ref[...])

    spec = pl.BlockSpec((tr, cols), lambda i, core_ref: (i, 0))
    half_spec = pl.BlockSpec((tr, cols), lambda i, core_ref: (i % per, 0))
    shape = jax.ShapeDtypeStruct((rows, cols), F32)
    return pl.pallas_call(
        body, name=name, out_shape=[shape] * 4,
        grid_spec=pltpu.PrefetchScalarGridSpec(
            num_scalar_prefetch=1, grid=(rows // tr,), in_specs=[spec, half_spec, half_spec, spec, spec], out_specs=[spec] * 4),
        compiler_params=_params(("parallel",)),
    )(core, w, g_mine, g_sibling, m, v)


def _chip_sum(name, core, g, other):
    _, _, half, cols = g.shape
    tr = _row_tile(half, cols)

    def body(core_ref, g_ref, o_ref, p_ref):
        p_ref[...] = (g_ref[...] + o_ref[...]).astype(BF16)

    spec = pl.BlockSpec((None, tr, cols), lambda j, i, core_ref: (j, i, 0))
    return pl.pallas_call(
        body, name=name, out_shape=jax.ShapeDtypeStruct((N_CHIPS, half, cols), BF16),
        grid_spec=pltpu.PrefetchScalarGridSpec(
            num_scalar_prefetch=1, grid=(N_CHIPS, half // tr),
            in_specs=[pl.BlockSpec((None, None, tr, cols), lambda j, i, core_ref: (j, core_ref[0], i, 0)), spec],
            out_specs=spec),
        compiler_params=_params(("parallel", "parallel")),
    )(core, g, other)


def _mesh_sum(name, chip, received, own):
    _, half, cols = received.shape
    tr = _row_tile(half, cols, itemsize=2 * N_CHIPS)

    def body(chip_ref, r_ref, own_ref, o_ref):
        acc = None
        for j in range(N_CHIPS):
            term = jnp.where(chip_ref[0] == j, own_ref[...], r_ref[j]).astype(F32)
            acc = term if acc is None else acc + term
        o_ref[...] = acc

    return pl.pallas_call(
        body, name=name, out_shape=jax.ShapeDtypeStruct((half, cols), F32),
        grid_spec=pltpu.PrefetchScalarGridSpec(
            num_scalar_prefetch=1, grid=(half // tr,),
            in_specs=[pl.BlockSpec((N_CHIPS, tr, cols), lambda i, chip_ref: (0, i, 0)),
                      pl.BlockSpec((None, tr, cols), lambda i, chip_ref: (chip_ref[0], i, 0))],
            out_specs=pl.BlockSpec((tr, cols), lambda i, chip_ref: (i, 0))),
        compiler_params=_params(("parallel",)),
    )(chip, received, own)


CHIP_FLIPS = ((1, 0), (0, 1), (1, 1))


def _place():
    x, y, c = lax.axis_index("x"), lax.axis_index("y"), lax.axis_index("c")
    return x, y, c, 2 * x + y


def _remote(src, dst, sems_s, sems_r, k, dev):
    return pltpu.make_async_remote_copy(src_ref=src, dst_ref=dst, send_sem=sems_s.at[k], recv_sem=sems_r.at[k],
                                        device_id=dev, device_id_type=MESH)


class _Exchange:
    def __init__(self, ins, out_shapes, n_sems, start, forward, finish):
        self.ins, self.out_shapes, self.n_sems = list(ins), list(out_shapes), n_sems
        self.start, self.forward, self.finish = start, forward, finish

    def scratch(self):
        return [pltpu.SemaphoreType.DMA((self.n_sems,)), pltpu.SemaphoreType.DMA((self.n_sems,))]

    def run(self, name):
        n = len(self.ins)

        def body(*refs):
            args = (refs[:n], refs[n:2 * n]) + tuple(refs[2 * n:])
            self.start(*args)
            self.forward(*args)
            self.finish(*args)

        return pl.pallas_call(
            body, name=name, in_specs=[ANY] * n, out_specs=[ANY] * n, out_shape=self.out_shapes, scratch_shapes=self.scratch(),
        )(*self.ins)


def _all_gather_weights(halved, whole):
    nh, nw = len(halved), len(whole)
    n_arr = nh + nw

    def copies(ins, outs, sems_s, sems_r):
        x, y, c, me = _place()
        sibling = (x, y, 1 - c)
        own = [_remote(ins[k], outs[k].at[me], sems_s, sems_r, k, sibling) for k in range(n_arr)]
        first, passed = [], []
        for k in range(n_arr):
            for f, (fx, fy) in enumerate(CHIP_FLIPS):
                src, dst = (ins[k].at[c], outs[k].at[me, c]) if k < nh else (ins[k], outs[k].at[me])
                first.append(_remote(src, dst, sems_s, sems_r, n_arr + 3 * k + f, (x ^ fx, y ^ fy, c)))
        for k in range(nh):
            for f, (fx, fy) in enumerate(CHIP_FLIPS):
                landed = outs[k].at[2 * (x ^ fx) + (y ^ fy), c]
                passed.append(_remote(landed, landed, sems_s, sems_r, 4 * n_arr + 3 * k + f, sibling))
        return own, first, passed

    def start(*refs):
        own, first, _ = copies(*refs)
        for cp in own + first:
            cp.start()

    def forward(*refs):
        _, first, passed = copies(*refs)
        for arrived, cp in zip(first, passed):
            arrived.wait_recv()
            cp.start()

    def finish(*refs):
        own, first, passed = copies(*refs)
        for cp in first[3 * nh:] + passed + own:
            cp.wait_recv()
        for cp in first + passed + own:
            cp.wait_send()

    shapes = [jax.ShapeDtypeStruct((N_CHIPS,) + a.shape, a.dtype) for a in list(halved) + list(whole)]
    return _Exchange(list(halved) + list(whole), shapes, 7 * nh + 4 * nw, start, forward, finish)


def _swap_halves(gs):
    n = len(gs)

    def copies(ins, outs, sems_s, sems_r):
        x, y, c, _ = _place()
        return [_remote(ins[k].at[:, 1 - c], outs[k], sems_s, sems_r, k, (x, y, 1 - c)) for k in range(n)]

    def start(*refs):
        for cp in copies(*refs):
            cp.start()

    def finish(*refs):
        for cp in copies(*refs):
            cp.wait()

    shapes = [jax.ShapeDtypeStruct((g.shape[0],) + g.shape[2:], g.dtype) for g in gs]
    return _Exchange(gs, shapes, n, start, _no_copies, finish)


def _scatter_chips(ps):
    n = len(ps)

    def copies(ins, outs, sems_s, sems_r):
        x, y, c, me = _place()
        return [_remote(ins[k].at[2 * (x ^ fx) + (y ^ fy)], outs[k].at[me], sems_s, sems_r, 3 * k + f, (x ^ fx, y ^ fy, c))
                for k in range(n) for f, (fx, fy) in enumerate(CHIP_FLIPS)]

    def start(*refs):
        for cp in copies(*refs):
            cp.start()

    def forward(*refs):
        pass

    def finish(*refs):
        for cp in copies(*refs):
            cp.wait()

    shapes = [jax.ShapeDtypeStruct(p.shape, p.dtype) for p in ps]
    return _Exchange(ps, shapes, 3 * n, start, forward, finish)


def _swap_reduced(rs):
    n = len(rs)

    def copies(ins, outs, sems_s, sems_r):
        x, y, c, _ = _place()
        return [_remote(ins[k], outs[k], sems_s, sems_r, k, (x, y, 1 - c)) for k in range(n)]

    def start(*refs):
        for cp in copies(*refs):
            cp.start()

    def finish(*refs):
        for cp in copies(*refs):
            cp.wait()

    return _Exchange(rs, [jax.ShapeDtypeStruct(r.shape, r.dtype) for r in rs], n, start, _no_copies, finish)


N_DEV = 8


def _gather_small(buf):
    def copies(ins, outs, sems_s, sems_r):
        x, y, c, _ = _place()
        me = 4 * x + 2 * y + c
        return [_remote(ins[0], outs[0].at[me], sems_s, sems_r, o - 1, (x ^ (o >> 2), y ^ ((o >> 1) & 1), c ^ (o & 1)))
                for o in range(1, N_DEV)]

    def start(*refs):
        for cp in copies(*refs):
            cp.start()

    def finish(*refs):
        for cp in copies(*refs):
            cp.wait()

    return _Exchange([buf], [jax.ShapeDtypeStruct((N_DEV,) + buf.shape, buf.dtype)], N_DEV - 1, start, _no_copies, finish)


def _sum_devices(place, gathered, own):
    rows = own.shape[0]

    def body(place_ref, g_ref, own_ref, o_ref):
        acc = None
        for d in range(N_DEV):
            term = jnp.where(place_ref[0] == d, own_ref[...], g_ref[d])
            acc = term if acc is None else acc + term
        o_ref[...] = acc

    return pl.pallas_call(
        body, name="sum_devices", out_shape=jax.ShapeDtypeStruct((rows, LANES), F32),
        grid_spec=pltpu.PrefetchScalarGridSpec(
            num_scalar_prefetch=1, grid=(1,),
            in_specs=[pl.BlockSpec((N_DEV, rows, LANES), lambda i, place_ref: (0, 0, 0)),
                      pl.BlockSpec((rows, LANES), lambda i, place_ref: (0, 0))],
            out_specs=pl.BlockSpec((rows, LANES), lambda i, place_ref: (0, 0))),
        compiler_params=_params(("arbitrary",)),
    )(place, gathered, own)


def _no_copies(*refs):
    pass


def _no_exchange():
    return _Exchange([], [], 1, _no_copies, _no_copies, _no_copies)


class _NoComm:
    def gather_first(self):
        return _no_exchange()

    def first_landed(self, p, landed):
        pass

    def gather_rest(self, p):
        return _no_exchange()

    def weights_landed(self, p, landed):
        pass

    def gather_last(self):
        return _no_exchange()

    def last_landed(self, p, landed):
        pass

    def swap_first(self, g):
        return _no_exchange()

    def first_swapped(self, landed):
        pass

    def swap_second(self, g):
        return _no_exchange()

    def second_swapped(self, landed):
        pass

    def scatter_early(self, g):
        return _no_exchange()

    def scatter_landed(self, landed):
        pass

    def swap_reduced_early(self):
        return _no_exchange()

    def reduced_landed(self, landed):
        pass

    def scatter_late(self, g):
        return _no_exchange()

    def late_landed(self, landed):
        pass


def _local_step(x, mem, target, p, comm):
    h1, landed = _norm_fwd("norm_mix_pre", x, p["norm_mix_pre"], comm.gather_first())
    comm.first_landed(p, landed)
    qa, ka, va, u, z = _in_proj(h1, p["w_in"], p["bf_pad"])
    ycat, qab, landed = _fox_fwd(qa, ka, va, comm.gather_rest(p))
    comm.weights_landed(p, landed)
    ycat = _pool_fwd(u, p["w_pool_bd"], p["pool_scale"], ycat)
    y1, x2, h2, qx = _proj_resid_norm("mix_out", ycat, p["w_mix_out"], x, p["norm_mix_post"], p["norm_xa_pre"], p["w_xq"])
    mem_n = _norm_fwd("norm_mem", mem, p["norm_mem"])
    kv = _mm(
        "xkv", mem_n, p["w_xkv"], pl.BlockSpec((MEM, D), lambda i, j, k: (0, 0)),
        pl.BlockSpec((None, D, 512), lambda i, j, k: (j, 0, 0)), jax.ShapeDtypeStruct((MEM, 2 * D), BF16),
        pl.BlockSpec((MEM, 512), lambda i, j, k: (0, j)), (1, N_CHIPS, 1), NN, (MEM, 512))
    xo = _xattn_fwd(qx, kv)
    y2, x3, h3 = _proj_resid_norm("xo", xo, p["w_xo"], x2, p["norm_xa_post"], p["norm_ffn_pre"])
    hid, landed = _mm(
        "up_proj", h3, p["w_up"], pl.BlockSpec((1024, D), lambda i, j, k: (i, 0)),
        pl.BlockSpec((None, D, 1024), lambda i, j, k: (j // 2, 0, j % 2)), jax.ShapeDtypeStruct((2, S, D_FF), F32),
        pl.BlockSpec((None, 1024, 1024), lambda i, j, k: (j // 4, i, j % 4)), (S // 1024, 8, 1), NN, (1024, 1024),
        comm.gather_last())
    comm.last_landed(p, landed)
    act = _convgate_fwd(hid, p["cwb"])

    g = {}
    dres, dy3, g["norm_ffn_post"], loss_cols = _down_loss_bwd(act, p["w_down"], x3, p["norm_ffn_post"], target)
    dact = _mm_nt("d_act", dy3, p["w_down"], F32, 1024, 1024)
    g["w_down"] = _mm_tn("dw_down", act, dy3, 512, 512)
    dhid, dcwb = _convgate_bwd(hid, dact, p["cwb"])
    g["w_up"] = _mm(
        "dw_up", h3, dhid, pl.BlockSpec((S, 512), lambda i, j, k: (0, i)),
        pl.BlockSpec((None, S, 512), lambda i, j, k: (j // 8, 0, j % 8)), jax.ShapeDtypeStruct((N_CHIPS, D, 2048), F32),
        pl.BlockSpec((None, 512, 512), lambda i, j, k: (j // 4, i, j % 4)), (2, 16, 1), TN, (512, 512))
    dh3, landed = _d_h3(dhid, p["w_up"], comm.swap_first(g))
    comm.first_swapped(landed)
    dres, dy2, dxo, g["norm_ffn_pre"], g["norm_xa_post"] = _mid_bwd(
        "bwd_ffn_xa", dres, x3, p["norm_ffn_pre"], dh3, y2, p["norm_xa_post"], p["w_xo"])
    g["w_xo"] = _mm_tn("dw_xo", xo, dy2, 512, 512)
    dqx, dkv = _xattn_bwd(qx, kv, dxo)
    dkv = dkv.astype(BF16)
    g["w_xq"] = _mm_tn("dw_xq", h2, dqx, 512, 512)
    dmem_n = _mm(
        "d_mem", dkv, p["w_xkv"], pl.BlockSpec((MEM, 512), lambda i, j, k: (0, k)),
        pl.BlockSpec((None, D, 512), lambda i, j, k: (k, 0, 0)), jax.ShapeDtypeStruct((MEM, D), F32),
        pl.BlockSpec((MEM, D), lambda i, j, k: (0, 0)), (1, 1, N_CHIPS), NT, (MEM, D))
    g["w_xkv"] = _mm(
        "dw_xkv", mem_n, dkv, pl.BlockSpec((MEM, D), lambda i, j, k: (0, 0)),
        pl.BlockSpec((MEM, 512), lambda i, j, k: (0, j)), jax.ShapeDtypeStruct((N_CHIPS, D, 512), F32),
        pl.BlockSpec((None, D, 512), lambda i, j, k: (j, 0, 0)), (1, N_CHIPS, 1), TN, (D, 512))
    g["norm_mem"] = _gain_bwd("dg_mem", mem, p["norm_mem"], dmem_n)
    (dres, dy1, g["norm_xa_pre"], g["norm_mix_post"], dy_pool, doa), landed = _bwd_xa_mix(
        dqx, p["w_xq"], dres, x2, p["norm_xa_pre"], y1, p["norm_mix_post"], p["w_mix_out"], ycat, comm.swap_second(g))
    comm.second_swapped(landed)
    g["w_mix_out"] = _mm_tn("dw_mix_out", ycat, dy1, 512, 512)
    dqa, dka, dva, landed = _fox_bwd(qab, doa, ka, va, comm.scatter_early(g))
    comm.scatter_landed(landed)
    du, g["w_pool_full"], g["pool_scale"] = _pool_bwd(u, dy_pool, p["w_pool_bd"], p["w_pool_bd_t"], p["pool_scale"])
    dproj, g["bf_pad"] = _fox_bwd_post(dqa, dka, dva, du, z, p["bf_pad"])
    g["w_in"], landed = _mm_tn("dw_in", h1, dproj, 512, 896, comm.swap_reduced_early())
    comm.reduced_landed(landed)
    dh1, landed = _mm_nt("d_h1", dproj, p["w_in"], F32, 1024, 1024, comm.scatter_late(g))
    comm.late_landed(landed)
    grad_x, g["norm_mix_pre"] = _first_bwd(dres, x, p["norm_mix_pre"], dh1)
    g["cwb"] = dcwb
    return grad_x, g, loss_cols


BIG = ("w_in", "w_mix_out", "w_xq", "w_xkv", "w_xo", "w_up", "w_down")
ROW_SHARDED = ("w_mix_out", "w_xq", "w_xo", "w_down")
SMALL = ("norm_mix_pre", "norm_mix_post", "b_forget", "w_pool", "pool_scale", "norm_mem", "norm_xa_pre", "norm_xa_post",
         "norm_ffn_pre", "norm_ffn_post", "conv_b")
ORDER = ("norm_mix_pre", "norm_mix_post", "w_in", "b_forget", "w_pool", "pool_scale", "w_mix_out", "norm_mem", "norm_xa_pre",
         "norm_xa_post", "w_xq", "w_xkv", "w_xo", "norm_ffn_pre", "norm_ffn_post", "w_up", "conv_w", "conv_b", "w_down")
SLOT = SUBLANES * LANES


def _pack(parts):
    rows, offs, off = [], [], 0
    for a in parts:
        flat = a.reshape(-1).astype(F32)
        n = -(-flat.shape[0] // SLOT) * SLOT
        rows.append(jnp.pad(flat, (0, n - flat.shape[0])).reshape(n // LANES, LANES))
        offs.append(off)
        off += n // LANES
    return jnp.concatenate(rows, axis=0), offs


def _unpack(buf, off, like):
    n = like.size
    rows = -(-n // LANES)
    return buf[off:off + rows].reshape(-1)[:n].reshape(like.shape)


FIRST = ("w_in",)
REST = ("w_mix_out", "w_xq", "w_xkv", "w_xo", "w_up")
LAST = ("w_down",)


def _local_params(w):
    w_pool_bd = jnp.zeros((D_POOL, D_POOL), F32)
    for gi in range(4):
        w_pool_bd = w_pool_bd.at[64 * gi:64 * (gi + 1), 64 * gi:64 * (gi + 1)].set(w["w_pool"][0, gi])
    p = {n: w[n] for n in ("norm_mix_pre", "norm_mix_post", "norm_mem", "norm_xa_pre", "norm_xa_post", "norm_ffn_pre",
                           "norm_ffn_post")}
    p.update(
        bf_pad=jnp.pad(w["b_forget"], ((0, 0), (0, LANES - HEADS))),
        w_pool_bd=w_pool_bd.astype(BF16), w_pool_bd_t=w_pool_bd.T.astype(BF16), pool_scale=w["pool_scale"].reshape(1, D_POOL))
    return p


def _w_in_param(stacked):
    return jnp.pad(jnp.concatenate(list(stacked), axis=1), ((0, 0), (0, D_IN_PAD - D_IN)))


def _rest_params(w, full, conv_w_full):
    cw2 = conv_w_full.reshape(3, 2, D_FF).transpose(1, 0, 2)
    cwb = jnp.concatenate([cw2, w["conv_b"].reshape(1, 2, D_FF).transpose(1, 0, 2), jnp.zeros((2, 4, D_FF), F32)], axis=1)
    return dict(w_mix_out=full["w_mix_out"].reshape(D, D), w_xq=full["w_xq"].reshape(D, D), w_xkv=full["w_xkv"],
                w_xo=full["w_xo"].reshape(D, D), w_up=full["w_up"], cwb=cwb)


def _whole_params(w, full, conv_w_full):
    p = _local_params(w)
    p.update(_rest_params(w, full, conv_w_full), w_in=_w_in_param(full["w_in"]), w_down=full["w_down"].reshape(D_FF, D))
    return p


def _halved(a):
    return a.reshape(a.shape[:-2] + (2, a.shape[-2] // 2, a.shape[-1]))


class _StepComm:
    def __init__(self, w, shard2d, conv_w, core_id, chip_id):
        self.w, self.shard2d, self.conv_w, self.core_id, self.chip_id = w, shard2d, conv_w, core_id, chip_id
        self.first, self.second = ("w_up", "w_down"), ("w_xq", "w_xkv", "w_xo")
        self.early = self.first + self.second
        self.late = ("w_in", "w_mix_out")

    def gather_first(self):
        return _all_gather_weights([_halved(self.shard2d[n].astype(BF16)) for n in FIRST], [])

    def first_landed(self, p, landed):
        p["w_in"] = _w_in_param(landed[0].reshape((N_CHIPS,) + self.shard2d["w_in"].shape))

    def gather_rest(self, p):
        return _all_gather_weights([_halved(self.shard2d[n].astype(BF16)) for n in REST], [self.conv_w.reshape(3, -1)])

    def weights_landed(self, p, landed):
        full = {n: a.reshape((N_CHIPS,) + self.shard2d[n].shape) for n, a in zip(REST, landed)}
        conv_w_full = jnp.transpose(landed[-1], (1, 0, 2)).reshape(3, 2 * D_FF)
        p.update(_rest_params(self.w, full, conv_w_full))

    def gather_last(self):
        return _all_gather_weights([_halved(self.shard2d[n].astype(BF16)) for n in LAST], [])

    def last_landed(self, p, landed):
        p["w_down"] = landed[0].reshape(D_FF, D)

    def _view(self, g, n):
        return _halved(g[n].reshape((N_CHIPS,) + self.shard2d[n].shape))

    def swap_first(self, g):
        return _swap_halves([self._view(g, n) for n in self.first])

    def first_swapped(self, landed):
        self.from_sibling = dict(zip(self.first, landed))

    def swap_second(self, g):
        return _swap_halves([self._view(g, n) for n in self.second])

    def second_swapped(self, landed):
        self.from_sibling.update(zip(self.second, landed))

    def scatter_early(self, g):
        self.partial = [_chip_sum("chip_sum_" + n, self.core_id, self._view(g, n), self.from_sibling[n]) for n in self.early]
        return _scatter_chips(self.partial)

    def scatter_landed(self, landed):
        self.received = list(landed)

    def swap_reduced_early(self):
        self.reduced = [_mesh_sum("mesh_sum_" + n, self.chip_id, r, own)
                        for n, r, own in zip(self.early, self.received, self.partial)]
        return _swap_reduced(self.reduced)

    def reduced_landed(self, landed):
        self.reduced_sibling = list(landed)

    def scatter_late(self, g):
        gw_in = g["w_in"][:, :D_IN]
        cols = D_IN // N_CHIPS
        views = [_halved(jnp.stack([gw_in[:, cols * j:cols * (j + 1)] for j in range(N_CHIPS)])), self._view(g, "w_mix_out")]
        from_sibling = _swap_halves(views).run("swap_halves_late")
        self.partial_late = [_chip_sum("chip_sum_" + n, self.core_id, view, other)
                             for n, view, other in zip(self.late, views, from_sibling)]
        return _scatter_chips(self.partial_late)

    def late_landed(self, landed):
        self.received_late = list(landed)


def kernel(x, mem, norm_mix_pre, norm_mix_post, w_in, b_forget, w_pool, pool_scale, w_mix_out, norm_mem, norm_xa_pre, norm_xa_post, w_xq, w_xkv, w_xo, norm_ffn_pre, norm_ffn_post, w_up, conv_w, conv_b, w_down, loss_target, m_norm_mix_pre, m_norm_mix_post, m_w_in, m_b_forget, m_w_pool, m_pool_scale, m_w_mix_out, m_norm_mem, m_norm_xa_pre, m_norm_xa_post, m_w_xq, m_w_xkv, m_w_xo, m_norm_ffn_pre, m_norm_ffn_post, m_w_up, m_conv_w, m_conv_b, m_w_down, v_norm_mix_pre, v_norm_mix_post, v_w_in, v_b_forget, v_w_pool, v_pool_scale, v_w_mix_out, v_norm_mem, v_norm_xa_pre, v_norm_xa_post, v_w_xq, v_w_xkv, v_w_xo, v_norm_ffn_pre, v_norm_ffn_post, v_w_up, v_conv_w, v_conv_b, v_w_down):
    w = dict(norm_mix_pre=norm_mix_pre, norm_mix_post=norm_mix_post, w_in=w_in, b_forget=b_forget, w_pool=w_pool,
             pool_scale=pool_scale, w_mix_out=w_mix_out, norm_mem=norm_mem, norm_xa_pre=norm_xa_pre, norm_xa_post=norm_xa_post,
             w_xq=w_xq, w_xkv=w_xkv, w_xo=w_xo, norm_ffn_pre=norm_ffn_pre, norm_ffn_post=norm_ffn_post, w_up=w_up,
             conv_w=conv_w, conv_b=conv_b, w_down=w_down)
    m = dict(norm_mix_pre=m_norm_mix_pre, norm_mix_post=m_norm_mix_post, w_in=m_w_in, b_forget=m_b_forget, w_pool=m_w_pool,
             pool_scale=m_pool_scale, w_mix_out=m_w_mix_out, norm_mem=m_norm_mem, norm_xa_pre=m_norm_xa_pre,
             norm_xa_post=m_norm_xa_post, w_xq=m_w_xq, w_xkv=m_w_xkv, w_xo=m_w_xo, norm_ffn_pre=m_norm_ffn_pre,
             norm_ffn_post=m_norm_ffn_post, w_up=m_w_up, conv_w=m_conv_w, conv_b=m_conv_b, w_down=m_w_down)
    v = dict(norm_mix_pre=v_norm_mix_pre, norm_mix_post=v_norm_mix_post, w_in=v_w_in, b_forget=v_b_forget, w_pool=v_w_pool,
             pool_scale=v_pool_scale, w_mix_out=v_w_mix_out, norm_mem=v_norm_mem, norm_xa_pre=v_norm_xa_pre,
             norm_xa_post=v_norm_xa_post, w_xq=v_w_xq, w_xkv=v_w_xkv, w_xo=v_w_xo, norm_ffn_pre=v_norm_ffn_pre,
             norm_ffn_post=v_norm_ffn_post, w_up=v_w_up, conv_w=v_conv_w, conv_b=v_conv_b, w_down=v_w_down)
    chip = 2 * lax.axis_index("x") + lax.axis_index("y")

    core_id = lax.axis_index("c").astype(jnp.int32).reshape(1)
    chip_id = chip.astype(jnp.int32).reshape(1)

    shard2d = {n: w[n][0] for n in BIG}
    p = _local_params(w)
    comm = _StepComm(w, shard2d, conv_w, core_id, chip_id)
    grad_x, g, loss_cols = _local_step(x[0], mem[0], loss_target[0], p, comm)

    reduced_late = [_mesh_sum("mesh_sum_" + n, chip_id, r, own)
                    for n, r, own in zip(comm.late, comm.received_late, comm.partial_late)]
    names = comm.late + comm.early
    reduced = reduced_late + comm.reduced
    reduced_sibling = list(_swap_reduced(reduced_late).run("swap_reduced_late")) + comm.reduced_sibling
    grads = {}

    gw_pool = jnp.stack([g["w_pool_full"][64 * gi:64 * (gi + 1), 64 * gi:64 * (gi + 1)] for gi in range(4)])
    dcwb = g["cwb"]
    g_conv_w = dcwb[:, 0:3, :].transpose(1, 0, 2).reshape(3, 2 * D_FF)
    g_conv_b = dcwb[:, 3, :].reshape(2 * D_FF)
    small_g = dict(norm_mix_pre=g["norm_mix_pre"], norm_mix_post=g["norm_mix_post"], b_forget=g["bf_pad"][:, :HEADS],
                   w_pool=gw_pool, pool_scale=g["pool_scale"], norm_mem=g["norm_mem"], norm_xa_pre=g["norm_xa_pre"],
                   norm_xa_post=g["norm_xa_post"], norm_ffn_pre=g["norm_ffn_pre"], norm_ffn_post=g["norm_ffn_post"],
                   conv_b=g_conv_b)
    local_buf, offs = _pack([small_g[n] for n in SMALL] + [g_conv_w, loss_cols])

    delta, new_m, new_v = {}, {}, {}
    for n, g_mine, g_sibling in zip(names, reduced, reduced_sibling):
        gn, d, nm, nv = _adamw_halves("adamw_" + n, core_id, shard2d[n], g_mine, g_sibling, m[n][0], v[n][0])
        grads[n], delta[n], new_m[n], new_v[n] = gn[None], d[None], nm[None], nv[None]
    place = (2 * chip + lax.axis_index("c")).astype(jnp.int32).reshape(1)
    buf = _sum_devices(place, _gather_small(local_buf).run("gather_small")[0], local_buf)
    for n, off in zip(SMALL, offs):
        grads[n] = _unpack(buf, off, w[n])
    g_conv_w = _unpack(buf, offs[len(SMALL)], g_conv_w)
    grads["conv_w"] = lax.dynamic_slice_in_dim(g_conv_w, chip * (2 * D_FF // N_CHIPS), 2 * D_FF // N_CHIPS, axis=1).reshape(conv_w.shape)
    loss = jnp.sum(_unpack(buf, offs[len(SMALL) + 1], loss_cols))
    small_names = SMALL + ("conv_w",)
    packed = [_pack([d[n] for n in small_names])[0] for d in (w, grads, m, v)]
    offs = _pack([w[n] for n in small_names])[1]
    d, nm, nv = _adamw("adamw_small", *packed)
    for n, off in zip(small_names, offs):
        delta[n], new_m[n], new_v[n] = _unpack(d, off, w[n]), _unpack(nm, off, w[n]), _unpack(nv, off, w[n])

    return (loss, grad_x[None], *[grads[n] for n in ORDER], *[delta[n] for n in ORDER], *[new_m[n] for n in ORDER],
            *[new_v[n] for n in ORDER])
```

```python
import functools

import jax
import jax.numpy as jnp
import numpy as np
from jax import lax
from jax.experimental import pallas as pl
from jax.experimental.pallas import tpu as pltpu

F32 = jnp.float32
BF16 = jnp.bfloat16
MESH = pl.DeviceIdType.MESH
ANY = pl.BlockSpec(memory_space=pl.ANY)
VMEM_SPEC = pl.BlockSpec(memory_space=pltpu.VMEM)

S = 4096
D = 1024
MEM = 256
D_POOL = 256
HEADS = 12
DH = 64
D_FOX = HEADS * DH
D_IN = D_POOL + 3 * D_FOX + HEADS
F_OFF = D_POOL + 3 * D_FOX
Q_OFF, K_OFF, V_OFF = D_POOL, D_POOL + D_FOX, D_POOL + 2 * D_FOX
XA_HEADS = 4
XA_DH = 256
D_FF = 4096
EPS = 1e-6
N_CHIPS = 4
ADAM_LR, ADAM_B1, ADAM_B2, ADAM_EPS, ADAM_WD, ADAM_STEP = 0.001, 0.9, 0.999, 1e-08, 0.01, 10

LANES = 128
SUBLANES = 8
D_IN_PAD = 21 * LANES
TR = 512
TILE_BYTES = 2 * 1024 * 1024
NEG = -1e30
VMEM_LIMIT = 52 * 1024 * 1024

NN = (((1,), (0,)), ((), ()))
NT = (((1,), (1,)), ((), ()))
TN = (((0,), (0,)), ((), ()))


def _dot(a, b, dims=NN):
    return lax.dot_general(a, b, dims, preferred_element_type=F32)


def _params(sem):
    return pltpu.CompilerParams(dimension_semantics=sem, vmem_limit_bytes=VMEM_LIMIT)


def _split3(x):
    hi = x.astype(BF16)
    r = x - hi.astype(F32)
    mid = r.astype(BF16)
    lo = (r - mid.astype(F32)).astype(BF16)
    return hi, mid, lo


def _split3_f32(x):
    hi = x.astype(BF16).astype(F32)
    r = x - hi
    mid = r.astype(BF16).astype(F32)
    return hi, mid, r - mid


def _lane_iota(shape):
    return lax.broadcasted_iota(jnp.int32, shape, len(shape) - 1)


def _row_iota(shape):
    return lax.broadcasted_iota(jnp.int32, shape, len(shape) - 2)


def _mm(name, a, b, a_spec, b_spec, out_shape, out_spec, grid, dims, acc_shape, ex=None):
    nk = grid[2]
    if ex is not None:
        return _mm_hosting(name, a, b, a_spec, b_spec, out_shape, out_spec, grid, dims, ex)

    def body(a_ref, b_ref, o_ref, *scr):
        p = _dot(a_ref[...], b_ref[...], dims)
        if nk == 1:
            o_ref[...] = p.astype(o_ref.dtype)
        else:
            acc = scr[0]
            k = pl.program_id(2)

            @pl.when(k == 0)
            def _():
                acc[...] = p

            @pl.when(k > 0)
            def _():
                acc[...] += p

            @pl.when(k == nk - 1)
            def _():
                o_ref[...] = acc[...].astype(o_ref.dtype)

    return pl.pallas_call(
        body, name=name, grid=grid, in_specs=[a_spec, b_spec], out_specs=out_spec, out_shape=out_shape,
        scratch_shapes=[pltpu.VMEM(acc_shape, F32)] if nk > 1 else [],
        compiler_params=_params(("parallel", "parallel", "arbitrary")),
    )(a, b)


def _mm_hosting(name, a, b, a_spec, b_spec, out_shape, out_spec, grid, dims, ex):
    assert grid[2] == 1
    n = len(ex.ins)

    def body(*refs):
        i, j = pl.program_id(0), pl.program_id(1)
        last = (i == grid[0] - 1) & (j == grid[1] - 1)
        (a_ref, b_ref), (o_ref,), _, begin, end = _hosted(ex, refs, 2, 1, (i == 0) & (j == 0), last, last)
        begin()
        o_ref[...] = _dot(a_ref[...], b_ref[...], dims).astype(o_ref.dtype)
        end()

    res = pl.pallas_call(
        body, name=name, grid=grid, in_specs=[a_spec, b_spec] + [ANY] * n, out_specs=[out_spec] + [ANY] * n,
        out_shape=[out_shape] + ex.out_shapes, scratch_shapes=ex.scratch(),
        compiler_params=_params(("arbitrary", "arbitrary", "arbitrary")),
    )(a, b, *ex.ins)
    return res[0], res[1:]


def _mm_nn(name, a, b, out_dtype, tm, tn):
    m, k = a.shape
    n = b.shape[1]
    return _mm(name, a, b, pl.BlockSpec((tm, k), lambda i, j, kk: (i, 0)), pl.BlockSpec((k, tn), lambda i, j, kk: (0, j)),
               jax.ShapeDtypeStruct((m, n), out_dtype), pl.BlockSpec((tm, tn), lambda i, j, kk: (i, j)),
               (m // tm, n // tn, 1), NN, (tm, tn))


def _mm_nt(name, a, b, out_dtype, tm, tn, ex=None):
    m, k = a.shape
    n = b.shape[0]
    return _mm(name, a, b, pl.BlockSpec((tm, k), lambda i, j, kk: (i, 0)), pl.BlockSpec((tn, k), lambda i, j, kk: (j, 0)),
               jax.ShapeDtypeStruct((m, n), out_dtype), pl.BlockSpec((tm, tn), lambda i, j, kk: (i, j)),
               (m // tm, n // tn, 1), NT, (tm, tn), ex)


def _mm_tn(name, a, b, tka, tn, ex=None):
    t, ka = a.shape
    n = b.shape[1]
    return _mm(name, a, b, pl.BlockSpec((t, tka), lambda i, j, kk: (0, i)), pl.BlockSpec((t, tn), lambda i, j, kk: (0, j)),
               jax.ShapeDtypeStruct((ka, n), F32), pl.BlockSpec((tka, tn), lambda i, j, kk: (i, j)),
               (ka // tka, n // tn, 1), TN, (tka, tn), ex)


def _d_h3(dhid, w_up, ex):
    tm = tn = 512
    shard = 2 * D_FF // N_CHIPS
    per_plane = D_FF // shard
    grid = (S // tm, D // tn)
    n = len(ex.ins)

    def body(*refs):
        i, j = pl.program_id(0), pl.program_id(1)
        first = (i == 0) & (j == 0)
        (a_ref, b_ref), (o_ref,), _, begin, end = _hosted(ex, refs, 2, 1, first, first, (i == grid[0] - 1) & (j == grid[1] - 1))
        begin()
        acc = None
        for k in range(N_CHIPS):
            cols = slice(shard * (k % per_plane), shard * (k % per_plane + 1))
            part = _dot(a_ref[k // per_plane, :, cols], b_ref[k], NT)
            acc = part if acc is None else acc + part
        o_ref[...] = acc
        end()

    res = pl.pallas_call(
        body, name="d_h3", grid=grid,
        in_specs=[pl.BlockSpec((2, tm, D_FF), lambda i, j: (0, i, 0)),
                  pl.BlockSpec((N_CHIPS, tn, shard), lambda i, j: (0, j, 0))] + [ANY] * n,
        out_specs=[pl.BlockSpec((tm, tn), lambda i, j: (i, j))] + [ANY] * n,
        out_shape=[jax.ShapeDtypeStruct((S, D), F32)] + ex.out_shapes, scratch_shapes=ex.scratch(),
        compiler_params=_params(("arbitrary", "arbitrary")),
    )(dhid, w_up, *ex.ins)
    return res[0], res[1:]


def _rms(x, g):
    r = lax.rsqrt(jnp.mean(x * x, axis=-1, keepdims=True) + EPS)
    return x * r * g


def _rms_bwd(x, g, dy):
    r = lax.rsqrt(jnp.mean(x * x, axis=-1, keepdims=True) + EPS)
    xh = x * r
    dxh = dy * g
    dx = r * (dxh - xh * jnp.mean(dxh * xh, axis=-1, keepdims=True))
    return dx, jnp.sum(dy * xh, axis=0, keepdims=True)


def _row_spec(tr, width):
    return pl.BlockSpec((tr, width), lambda i: (i, 0))


def _vec_spec(width):
    return pl.BlockSpec((1, width), lambda i: (0, 0))


def _norm_fwd(name, x, g, ex=None):
    rows, width = x.shape
    tr = min(TR, rows)
    steps = rows // tr
    hosted = ex if ex is not None else _no_exchange()
    n = len(hosted.ins)

    def body(*refs):
        i = pl.program_id(0)
        (x_ref, g_ref), (h_ref,), _, begin, end = _hosted(hosted, refs, 2, 1, i == 0, i == steps - 1, i == steps - 1)
        begin()
        h_ref[...] = _rms(x_ref[...], g_ref[...]).astype(BF16)
        end()

    res = pl.pallas_call(
        body, name=name, grid=(steps,), in_specs=[_row_spec(tr, width), _vec_spec(width)] + [ANY] * n,
        out_specs=[_row_spec(tr, width)] + [ANY] * n,
        out_shape=[jax.ShapeDtypeStruct((rows, width), BF16)] + hosted.out_shapes, scratch_shapes=hosted.scratch(),
        compiler_params=_params(("arbitrary",)),
    )(x, g, *hosted.ins)
    return res[0] if ex is None else (res[0], res[1:])


def _proj_resid_norm(name, a, w, xp, g_post, g_pre, w_next=None):
    def body(a_ref, w_ref, xp_ref, gpost_ref, gpre_ref, *rest):
        y_ref, xn_ref, h_ref = rest[-3:] if w_next is None else rest[1:4]
        y = _dot(a_ref[...], w_ref[...])
        y_ref[...] = y
        xn = xp_ref[...] + _rms(y, gpost_ref[...])
        xn_ref[...] = xn
        h = _rms(xn, gpre_ref[...]).astype(BF16)
        h_ref[...] = h
        if w_next is not None:
            rest[4][...] = _dot(h, rest[0][...]).astype(BF16)

    mat = pl.BlockSpec((D, D), lambda i: (0, 0))
    more = [] if w_next is None else [w_next]
    return pl.pallas_call(
        body, name=name, grid=(S // TR,),
        in_specs=[_row_spec(TR, D), mat, _row_spec(TR, D), _vec_spec(D), _vec_spec(D)] + [mat] * len(more),
        out_specs=[_row_spec(TR, D)] * (3 + len(more)),
        out_shape=[jax.ShapeDtypeStruct((S, D), F32), jax.ShapeDtypeStruct((S, D), F32), jax.ShapeDtypeStruct((S, D), BF16)]
        + [jax.ShapeDtypeStruct((S, D), BF16)] * len(more),
        compiler_params=_params(("parallel",)),
    )(a, w, xp, g_post, g_pre, *more)


def _down_loss_bwd(act, w_down, x3, g_post, target):
    def body(a_ref, w_ref, x_ref, g_ref, t_ref, dres_ref, dy_ref, dg_ref, loss_ref):
        i = pl.program_id(0)

        @pl.when(i == 0)
        def _():
            dg_ref[...] = jnp.zeros_like(dg_ref)
            loss_ref[...] = jnp.zeros_like(loss_ref)

        y = _dot(a_ref[...], w_ref[...])
        g = g_ref[...]
        e = x_ref[...] + _rms(y, g) - t_ref[...]
        loss_ref[...] += jnp.sum(e * e, axis=0, keepdims=True) * (0.5 / D)
        dres = e * (1.0 / D)
        dres_ref[...] = dres
        dy, dg = _rms_bwd(y, g, dres)
        dy_ref[...] = dy.astype(BF16)
        dg_ref[...] += dg

    return pl.pallas_call(
        body, name="down_loss_bwd", grid=(S // TR,),
        in_specs=[_row_spec(TR, D_FF), pl.BlockSpec((D_FF, D), lambda i: (0, 0)), _row_spec(TR, D), _vec_spec(D),
                  _row_spec(TR, D)],
        out_specs=[_row_spec(TR, D), _row_spec(TR, D), _vec_spec(D), _vec_spec(D)],
        out_shape=[jax.ShapeDtypeStruct((S, D), F32), jax.ShapeDtypeStruct((S, D), BF16),
                   jax.ShapeDtypeStruct((1, D), F32), jax.ShapeDtypeStruct((1, D), F32)],
        compiler_params=_params(("arbitrary",)),
    )(act, w_down, x3, g_post, target)


def _mid_bwd(name, dres, xcur, g_pre, dh, yprev, g_post, w):
    def body(dres_ref, x_ref, gpre_ref, dh_ref, y_ref, gpost_ref, w_ref, dx_ref, dy_ref, da_ref, dgpre_ref, dgpost_ref):
        i = pl.program_id(0)

        @pl.when(i == 0)
        def _():
            dgpre_ref[...] = jnp.zeros_like(dgpre_ref)
            dgpost_ref[...] = jnp.zeros_like(dgpost_ref)

        dxn, dgpre = _rms_bwd(x_ref[...], gpre_ref[...], dh_ref[...])
        dx = dres_ref[...] + dxn
        dx_ref[...] = dx
        dy, dgpost = _rms_bwd(y_ref[...], gpost_ref[...], dx)
        dy = dy.astype(BF16)
        dy_ref[...] = dy
        da_ref[...] = _dot(dy, w_ref[...], NT).astype(BF16)
        dgpre_ref[...] += dgpre
        dgpost_ref[...] += dgpost

    return pl.pallas_call(
        body, name=name, grid=(S // TR,),
        in_specs=[_row_spec(TR, D), _row_spec(TR, D), _vec_spec(D), _row_spec(TR, D), _row_spec(TR, D), _vec_spec(D),
                  pl.BlockSpec((D, D), lambda i: (0, 0))],
        out_specs=[_row_spec(TR, D), _row_spec(TR, D), _row_spec(TR, D), _vec_spec(D), _vec_spec(D)],
        out_shape=[jax.ShapeDtypeStruct((S, D), F32), jax.ShapeDtypeStruct((S, D), BF16), jax.ShapeDtypeStruct((S, D), BF16),
                   jax.ShapeDtypeStruct((1, D), F32), jax.ShapeDtypeStruct((1, D), F32)],
        compiler_params=_params(("arbitrary",)),
    )(dres, xcur, g_pre, dh, yprev, g_post, w)


def _first_bwd(dres, x, g, dh):
    def body(dres_ref, x_ref, g_ref, dh_ref, dx_ref, dg_ref):
        i = pl.program_id(0)

        @pl.when(i == 0)
        def _():
            dg_ref[...] = jnp.zeros_like(dg_ref)

        dxn, dg = _rms_bwd(x_ref[...], g_ref[...], dh_ref[...])
        dx_ref[...] = dres_ref[...] + dxn
        dg_ref[...] += dg

    return pl.pallas_call(
        body, name="first_bwd", grid=(S // TR,),
        in_specs=[_row_spec(TR, D), _row_spec(TR, D), _vec_spec(D), _row_spec(TR, D)],
        out_specs=[_row_spec(TR, D), _vec_spec(D)],
        out_shape=[jax.ShapeDtypeStruct((S, D), F32), jax.ShapeDtypeStruct((1, D), F32)],
        compiler_params=_params(("arbitrary",)),
    )(dres, x, g, dh)


def _gain_bwd(name, x, g, dy):
    rows, width = x.shape

    def body(x_ref, g_ref, dy_ref, dg_ref):
        _, dg = _rms_bwd(x_ref[...], g_ref[...], dy_ref[...])
        dg_ref[...] = dg

    return pl.pallas_call(
        body, name=name, grid=(1,), in_specs=[_row_spec(rows, width), _vec_spec(width), _row_spec(rows, width)],
        out_specs=_vec_spec(width), out_shape=jax.ShapeDtypeStruct((1, width), F32),
        compiler_params=_params(("arbitrary",)),
    )(x, g, dy)


CUM_Q = DH
CUM_K = DH + 3
LSE_Q = DH + 6
BOTH_ONE = DH + 9
DEN_V = DH
DELTA = DH + 1
PREP_TR = 256
PIECE_LANES = 16
FOX_FWD_BLOCK = 1024
FOX_BWD_BLOCK = 512


def _at(lane_of_even_head, h):
    return (lane_of_even_head + DH * (h % 2)) % LANES


def _data_lanes(lane, h):
    return lane >= DH if h % 2 else lane < DH


def _pair_block(ref, off, h):
    base = ((off + DH * h) // LANES) * LANES
    return ref[:, base:base + LANES]


def _cumsum_rows(x, tri, carry):
    hi, mid, lo = _split3(x)
    return _dot(tri, hi) + _dot(tri, mid) + _dot(tri, lo) + carry


def _in_proj(h1, w_in, bf_pad):
    tr = TR

    place_q = np.zeros((LANES, HEADS * LANES), np.float32)
    place_k = np.zeros((LANES, HEADS * LANES), np.float32)
    for h in range(HEADS):
        for piece in range(3):
            place_q[PIECE_LANES * piece + h, LANES * h + _at(CUM_Q, h) + piece] = 1.0
            place_k[PIECE_LANES * piece + h, LANES * h + _at(CUM_K, h) + piece] = -1.0

    def body(h_ref, w_ref, bf_ref, pq_ref, pk_ref, qa_ref, ka_ref, va_ref, u_ref, z_ref, carry_ref):
        i = pl.program_id(0)

        @pl.when(i == 0)
        def _():
            carry_ref[...] = jnp.zeros_like(carry_ref)

        proj = _dot(h_ref[...], w_ref[...])
        u_ref[...] = proj[:, :D_POOL]
        z_ref[...] = proj[:, F_OFF:F_OFF + LANES]
        lane = _lane_iota((tr, LANES))
        z = proj[:, F_OFF:F_OFF + LANES] + bf_ref[...]
        log_f = jnp.minimum(z, 0.0) - jnp.log(1.0 + jnp.exp(-jnp.abs(z)))
        log_f = jnp.where(lane < HEADS, log_f, 0.0)
        tri = jnp.where(_row_iota((tr, tr)) >= _lane_iota((tr, tr)), 1.0, 0.0).astype(BF16)
        cum = _cumsum_rows(log_f, tri, carry_ref[0:1, :])
        carry_ref[0:1, :] = cum[tr - 1:tr, :]
        c_hi, c_mid, c_lo = _split3_f32(cum)
        pieces = (c_hi + pltpu.roll(c_mid, PIECE_LANES, 1) + pltpu.roll(c_lo, 2 * PIECE_LANES, 1)).astype(BF16)
        cum_q = _dot(pieces, pq_ref[...])
        cum_k = _dot(pieces, pk_ref[...])

        def between(first, h):
            return (lane >= _at(first, h)) & (lane < _at(first, h) + 3)

        ones_q = [jnp.where(between(CUM_K, h) | (lane == _at(BOTH_ONE, h)), 1.0, 0.0) for h in range(2)]
        ones_k = [jnp.where(between(CUM_Q, h) | between(LSE_Q, h) | (lane == _at(BOTH_ONE, h)), 1.0, 0.0) for h in range(2)]
        aug_v = [jnp.where(lane == _at(DEN_V, h), 1.0, jnp.where(between(DELTA, h), -1.0, 0.0)) for h in range(2)]
        for h in range(HEADS):
            mine = slice(LANES * h, LANES * (h + 1))
            data = _data_lanes(lane, h)
            qa_ref[h] = jnp.where(data, _pair_block(proj, Q_OFF, h) * (DH ** -0.5), cum_q[:, mine] + ones_q[h % 2]).astype(BF16)
            ka_ref[h] = jnp.where(data, _pair_block(proj, K_OFF, h), cum_k[:, mine] + ones_k[h % 2]).astype(BF16)
            va_ref[h] = jnp.where(data, _pair_block(proj, V_OFF, h), aug_v[h % 2]).astype(BF16)

    head_spec = pl.BlockSpec((HEADS, tr, LANES), lambda i: (0, i, 0))
    head_shape = jax.ShapeDtypeStruct((HEADS, S, LANES), BF16)
    place_spec = pl.BlockSpec(place_q.shape, lambda i: (0, 0))
    return pl.pallas_call(
        body, name="in_proj", grid=(S // tr,),
        in_specs=[_row_spec(tr, D), pl.BlockSpec((D, D_IN_PAD), lambda i: (0, 0)), _vec_spec(LANES), place_spec, place_spec],
        out_specs=[head_spec] * 3 + [_row_spec(tr, D_POOL), _row_spec(tr, LANES)],
        out_shape=[head_shape] * 3 + [jax.ShapeDtypeStruct((S, D_POOL), F32), jax.ShapeDtypeStruct((S, LANES), F32)],
        scratch_shapes=[pltpu.VMEM((SUBLANES, LANES), F32)], compiler_params=_params(("arbitrary",)),
    )(h1, w_in, bf_pad, jnp.asarray(place_q, BF16), jnp.asarray(place_k, BF16))


def _hosted(ex, refs, n_blocked_in, n_blocked_out, first, forward_at, last):
    n = len(ex.ins)
    own_in = refs[:n_blocked_in]
    ex_in = refs[n_blocked_in:n_blocked_in + n]
    own_out = refs[n_blocked_in + n:n_blocked_in + n + n_blocked_out]
    ex_out = refs[n_blocked_in + n + n_blocked_out:n_blocked_in + 2 * n + n_blocked_out]
    rest = refs[n_blocked_in + 2 * n + n_blocked_out:]
    args = (ex_in, ex_out, rest[-2], rest[-1])

    def begin():
        @pl.when(first)
        def _():
            ex.start(*args)

        @pl.when(forward_at)
        def _():
            ex.forward(*args)

    def end():
        @pl.when(last)
        def _():
            ex.finish(*args)

    return own_in, own_out, rest[:-2], begin, end


def _fox_fwd(qa, ka, va, ex):
    BQ = BK = FOX_FWD_BLOCK
    nq = S // BQ
    n_pairs = HEADS // 2

    def body(*refs):
        p_id, i = pl.program_id(0), pl.program_id(1)
        (qa_ref, ka_ref, va_ref), (y_ref, qab_ref), (m_scr, acc_scr), begin, end = _hosted(
            ex, refs, 3, 2, (p_id == 0) & (i == 0), (p_id == n_pairs - 1) & (i == 0), (p_id == n_pairs - 1) & (i == nq - 1))
        begin()
        lane = _lane_iota((BQ, LANES))
        causal = _row_iota((BQ, BK)) >= _lane_iota((BQ, BK))
        m_scr[...] = jnp.full_like(m_scr, NEG)
        acc_scr[...] = jnp.zeros_like(acc_scr)

        def step(j, masked):
            rows = pl.ds(pl.multiple_of(j * BK, BK), BK)
            for hh in range(2):
                s = _dot(qa_ref[hh], ka_ref[hh, rows, :], NT)
                if masked:
                    s = jnp.where(causal, s, NEG)
                m_prev = m_scr[hh]
                m_new = jnp.maximum(m_prev, jnp.max(s, axis=1, keepdims=True))
                p = jnp.exp(s - jnp.tile(m_new, (1, BK // LANES)))
                acc_scr[hh] = jnp.exp(m_prev - m_new) * acc_scr[hh] + _dot(p.astype(BF16), va_ref[hh, rows, :])
                m_scr[hh] = m_new

        def full_step(j, carry):
            step(j, False)
            return carry

        lax.fori_loop(0, i, full_step, 0)
        step(i, True)
        outs = []
        for hh in range(2):
            acc = acc_scr[hh]
            den_lane, lse_lane = _at(DEN_V, hh), _at(LSE_Q, hh)
            den = jnp.broadcast_to(acc[:, den_lane:den_lane + 1], (BQ, LANES))
            outs.append(acc * (1.0 / den))
            n_hi, n_mid, n_lo = _split3(-(m_scr[hh] + jnp.log(den)))
            qab_ref[hh] = jnp.where(lane == lse_lane, n_hi,
                                    jnp.where(lane == lse_lane + 1, n_mid, jnp.where(lane == lse_lane + 2, n_lo, qa_ref[hh])))
        y_ref[...] = jnp.where(lane < DH, outs[0], outs[1]).astype(BF16)
        end()

    pair_rows = pl.BlockSpec((2, BQ, LANES), lambda p, i: (p, i, 0))
    pair_all = pl.BlockSpec((2, S, LANES), lambda p, i: (p, 0, 0))
    n = len(ex.ins)
    res = pl.pallas_call(
        body, name="fox_fwd", grid=(n_pairs, nq), in_specs=[pair_rows, pair_all, pair_all] + [ANY] * n,
        out_specs=[pl.BlockSpec((BQ, LANES), lambda p, i: (i, D_POOL // LANES + p)), pair_rows] + [ANY] * n,
        out_shape=[jax.ShapeDtypeStruct((S, D), BF16), jax.ShapeDtypeStruct((HEADS, S, LANES), BF16)] + ex.out_shapes,
        scratch_shapes=[pltpu.VMEM((2, BQ, LANES), F32), pltpu.VMEM((2, BQ, LANES), F32)] + ex.scratch(),
        compiler_params=_params(("arbitrary", "arbitrary")),
    )(qa, ka, va, *ex.ins)
    return res[0], res[1], res[2:]


def _bwd_xa_mix(dqx, w_xq, dres, x2, g_pre, y1, g_post, w_mix_out, ycat, ex):
    steps = S // TR
    n = len(ex.ins)

    def body(*refs):
        i = pl.program_id(0)
        ((dq_ref, wq_ref, dres_ref, x_ref, gpre_ref, y_ref, gpost_ref, wm_ref, ycat_ref),
         (dx_ref, dy_ref, dgpre_ref, dgpost_ref, dp_ref, doa_ref), _, begin, end) = _hosted(
            ex, refs, 9, 6, i == 0, i == 0, i == steps - 1)
        begin()

        @pl.when(i == 0)
        def _():
            dgpre_ref[...] = jnp.zeros_like(dgpre_ref)
            dgpost_ref[...] = jnp.zeros_like(dgpost_ref)

        dxn, dgpre = _rms_bwd(x_ref[...], gpre_ref[...], _dot(dq_ref[...], wq_ref[...], NT))
        dx = dres_ref[...] + dxn
        dx_ref[...] = dx
        dy, dgpost = _rms_bwd(y_ref[...], gpost_ref[...], dx)
        dy = dy.astype(BF16)
        dy_ref[...] = dy
        dgpre_ref[...] += dgpre
        dgpost_ref[...] += dgpost

        d = _dot(dy, wm_ref[...], NT)
        dp_ref[...] = d[:, :D_POOL]
        lane = _lane_iota((TR, LANES))
        low = lane < DH
        for p in range(HEADS // 2):
            cols = slice(D_POOL + LANES * p, D_POOL + LANES * (p + 1))
            do = d[:, cols]
            prod = do * ycat_ref[:, cols].astype(F32)
            deltas = (jnp.sum(jnp.where(low, prod, 0.0), axis=1, keepdims=True),
                      jnp.sum(jnp.where(low, 0.0, prod), axis=1, keepdims=True))
            for hh in range(2):
                d_hi, d_mid, d_lo = _split3_f32(deltas[hh])
                dl = _at(DELTA, hh)
                aug = jnp.where(lane == dl, d_hi, jnp.where(lane == dl + 1, d_mid, jnp.where(lane == dl + 2, d_lo, 0.0)))
                doa_ref[2 * p + hh] = jnp.where(_data_lanes(lane, hh), do, aug).astype(BF16)
        end()

    mat = pl.BlockSpec((D, D), lambda i: (0, 0))
    res = pl.pallas_call(
        body, name="bwd_xa_mix", grid=(steps,),
        in_specs=[_row_spec(TR, D), mat, _row_spec(TR, D), _row_spec(TR, D), _vec_spec(D), _row_spec(TR, D), _vec_spec(D), mat,
                  _row_spec(TR, D)] + [ANY] * n,
        out_specs=[_row_spec(TR, D), _row_spec(TR, D), _vec_spec(D), _vec_spec(D), _row_spec(TR, D_POOL),
                   pl.BlockSpec((HEADS, TR, LANES), lambda i: (0, i, 0))] + [ANY] * n,
        out_shape=[jax.ShapeDtypeStruct((S, D), F32), jax.ShapeDtypeStruct((S, D), BF16), jax.ShapeDtypeStruct((1, D), F32),
                   jax.ShapeDtypeStruct((1, D), F32), jax.ShapeDtypeStruct((S, D_POOL), F32),
                   jax.ShapeDtypeStruct((HEADS, S, LANES), BF16)] + ex.out_shapes,
        scratch_shapes=ex.scratch(), compiler_params=_params(("arbitrary",)),
    )(dqx, w_xq, dres, x2, g_pre, y1, g_post, w_mix_out, ycat, *ex.ins)
    return res[:6], res[6:]


def _fox_bwd(qab, doa, ka, va, ex):
    BQ = BK = FOX_BWD_BLOCK
    nk = S // BK
    n_pairs = HEADS // 2

    def body(*refs):
        p_id, j = pl.program_id(0), pl.program_id(1)
        (qab_ref, doa_ref, ka_ref, va_ref), (dqa_ref, dka_ref, dva_ref), _, begin, end = _hosted(
            ex, refs, 4, 3, (p_id == 0) & (j == 0), (p_id == n_pairs - 1) & (j == 0), (p_id == n_pairs - 1) & (j == nk - 1))
        begin()

        @pl.when(j == 0)
        def _():
            dqa_ref[...] = jnp.zeros_like(dqa_ref)

        causal = _row_iota((BQ, BK)) >= _lane_iota((BQ, BK))
        dka_ref[...] = jnp.zeros_like(dka_ref)
        dva_ref[...] = jnp.zeros_like(dva_ref)

        def step(i, masked):
            rows = pl.ds(pl.multiple_of(i * BQ, BQ), BQ)
            for hh in range(2):
                kb = ka_ref[hh]
                q = qab_ref[hh, rows, :]
                do = doa_ref[hh, rows, :]
                s = _dot(q, kb, NT)
                if masked:
                    s = jnp.where(causal, s, NEG)
                p = jnp.exp(s)
                ds = p * _dot(do, va_ref[hh], NT)
                pb = p.astype(BF16)
                dsb = ds.astype(BF16)
                dva_ref[hh] += _dot(pb, do, TN)
                dka_ref[hh] += _dot(dsb, q, TN)
                dqa_ref[hh, rows, :] += _dot(dsb, kb)

        def full_step(i, carry):
            step(i, False)
            return carry

        step(j, True)
        lax.fori_loop(j + 1, nk, full_step, 0)
        end()

    pair_all = pl.BlockSpec((2, S, LANES), lambda p, j: (p, 0, 0))
    pair_rows = pl.BlockSpec((2, BK, LANES), lambda p, j: (p, j, 0))
    shape = jax.ShapeDtypeStruct((HEADS, S, LANES), F32)
    n = len(ex.ins)
    res = pl.pallas_call(
        body, name="fox_bwd", grid=(n_pairs, nk), in_specs=[pair_all, pair_all, pair_rows, pair_rows] + [ANY] * n,
        out_specs=[pair_all, pair_rows, pair_rows] + [ANY] * n, out_shape=[shape] * 3 + ex.out_shapes,
        scratch_shapes=ex.scratch(), compiler_params=_params(("arbitrary", "arbitrary")),
    )(qab, doa, ka, va, *ex.ins)
    return res[0], res[1], res[2], res[3:]


def _fox_bwd_post(dqa, dka, dva, du, proj, bf_pad):
    tr = PREP_TR
    nt = S // tr

    pick = np.zeros((HEADS * LANES, LANES), np.float32)
    for h in range(HEADS):
        pick[LANES * h + _at(BOTH_ONE, h), h] = 1.0

    def body(dqa_ref, dka_ref, dva_ref, du_ref, z_ref, bf_ref, pick_ref, dp_ref, dbf_ref, carry_ref):
        i = pl.program_id(0)

        @pl.when(i == 0)
        def _():
            carry_ref[...] = jnp.zeros_like(carry_ref)
            dbf_ref[...] = jnp.zeros_like(dbf_ref)

        lane = _lane_iota((tr, LANES))
        diff = jnp.concatenate([dqa_ref[h] - dka_ref[h] for h in range(HEADS)], axis=1)
        hi = diff.astype(BF16)
        dcum = _dot(hi, pick_ref[...]) + _dot((diff - hi.astype(F32)).astype(BF16), pick_ref[...])
        tri =jnp.where(_lane_iota((tr, tr)) >= _row_iota((tr, tr)), 1.0, 0.0).astype(BF16)
        dlog_f = _cumsum_rows(dcum, tri, carry_ref[0:1, :])
        carry_ref[0:1, :] = dlog_f[0:1, :]
        z = z_ref[...] + bf_ref[...]
        df = jnp.where(lane < HEADS, dlog_f / (1.0 + jnp.exp(z)), 0.0)
        dbf_ref[...] += jnp.sum(df, axis=0, keepdims=True)

        dp_ref[:, 0:D_POOL] = du_ref[...].astype(BF16)
        low = lane < DH
        for ref, off, scale in ((dqa_ref, Q_OFF, DH ** -0.5), (dka_ref, K_OFF, 1.0), (dva_ref, V_OFF, 1.0)):
            for p in range(HEADS // 2):
                blk = jnp.where(low, ref[2 * p], ref[2 * p + 1])
                dp_ref[:, off + LANES * p:off + LANES * (p + 1)] = (blk * scale).astype(BF16)
        dp_ref[:, F_OFF:F_OFF + LANES] = df.astype(BF16)

    head_spec = pl.BlockSpec((HEADS, tr, LANES), lambda i: (0, nt - 1 - i, 0))
    return pl.pallas_call(
        body, name="fox_bwd_post", grid=(nt,),
        in_specs=[head_spec, head_spec, head_spec, pl.BlockSpec((tr, D_POOL), lambda i: (nt - 1 - i, 0)),
                  pl.BlockSpec((tr, LANES), lambda i: (nt - 1 - i, 0)), _vec_spec(LANES),
                  pl.BlockSpec(pick.shape, lambda i: (0, 0))],
        out_specs=[pl.BlockSpec((tr, D_IN_PAD), lambda i: (nt - 1 - i, 0)), _vec_spec(LANES)],
        out_shape=[jax.ShapeDtypeStruct((S, D_IN_PAD), BF16), jax.ShapeDtypeStruct((1, LANES), F32)],
        scratch_shapes=[pltpu.VMEM((SUBLANES, LANES), F32)],
        compiler_params=_params(("arbitrary",)),
    )(dqa, dka, dva, du, proj, bf_pad, jnp.asarray(pick, BF16))


POOL_HALO = 16


def _by_group(lane, a2, a4, a8, a16):
    return jnp.where(lane < 64, a2, jnp.where(lane < 128, a4, jnp.where(lane < 192, a8, a16)))


def _window_count(lane, t):
    return jnp.minimum(t + 1, _by_group(lane, 2, 4, 8, 16)).astype(F32)


def _pool_diff(u, halo, first, tile):
    n = TR + POOL_HALO
    ext = jnp.concatenate([jnp.where(first, 0.0, halo), u], axis=0)
    s2 = ext + pltpu.roll(ext, 1, 0)
    s4 = s2 + pltpu.roll(s2, 2, 0)
    s8 = s4 + pltpu.roll(s4, 4, 0)
    s16 = s8 + pltpu.roll(s8, 8, 0)
    lane = _lane_iota((n, D_POOL))
    win = _by_group(lane, s2, s4, s8, s16)[POOL_HALO:]
    lane = _lane_iota((TR, D_POOL))
    t = tile * TR + _row_iota((TR, D_POOL))
    return win / _window_count(lane, t) - u


def _prev_halo(rows, width, col):
    per = TR // rows
    return pl.BlockSpec((rows, width), lambda i: (jnp.maximum(i * per - 1, 0), col))


def _next_halo(rows, width, col):
    per = TR // rows
    return pl.BlockSpec((rows, width), lambda i: (jnp.minimum((i + 1) * per, S // rows - 1), col))


def _pool_fwd(proj, w_bd, ps, ycat):
    def body(u_ref, halo_ref, w_ref, ps_ref, ycat_ref, y_ref):
        i = pl.program_id(0)
        diff = _pool_diff(u_ref[...], halo_ref[...], i == 0, i)
        y_ref[...] = (_dot(diff.astype(BF16), w_ref[...]) * ps_ref[...]).astype(BF16)

    return pl.pallas_call(
        body, name="pool_fwd", grid=(S // TR,),
        in_specs=[_row_spec(TR, D_POOL), _prev_halo(POOL_HALO, D_POOL, 0),
                  pl.BlockSpec((D_POOL, D_POOL), lambda i: (0, 0)), _vec_spec(D_POOL), ANY],
        out_specs=_row_spec(TR, D_POOL), out_shape=jax.ShapeDtypeStruct((S, D), BF16), input_output_aliases={4: 0},
        compiler_params=_params(("parallel",)),
    )(proj, proj, w_bd, ps, ycat)


def _pool_bwd(proj, dycat, w_bd, w_bd_t, ps):
    nt = S // TR
    n = TR + POOL_HALO

    def body(u_ref, halo_ref, dy_ref, dyn_ref, w_ref, wt_ref, ps_ref, du_ref, dw_ref, dps_ref):
        i = pl.program_id(0)

        @pl.when(i == 0)
        def _():
            dw_ref[...] = jnp.zeros_like(dw_ref)
            dps_ref[...] = jnp.zeros_like(dps_ref)

        diff = _pool_diff(u_ref[...], halo_ref[...], i == 0, i).astype(BF16)
        dy = dy_ref[...]
        dps_ref[...] += jnp.sum(dy * _dot(diff, w_ref[...]), axis=0, keepdims=True)
        dy_ext = jnp.concatenate([dy, jnp.where(i == nt - 1, 0.0, dyn_ref[...])], axis=0)
        dmixed = (dy_ext * ps_ref[...]).astype(BF16)
        ddiff = _dot(dmixed, wt_ref[...])
        dw_ref[...] += _dot(diff, dmixed[:TR], TN)
        lane = _lane_iota((n, D_POOL))
        t = i * TR + _row_iota((n, D_POOL))
        e = ddiff / _window_count(lane, t)
        f2 = e + pltpu.roll(e, n - 1, 0)
        f4 = f2 + pltpu.roll(f2, n - 2, 0)
        f8 = f4 + pltpu.roll(f4, n - 4, 0)
        f16 = f8 + pltpu.roll(f8, n - 8, 0)
        du_ref[...] = _by_group(lane, f2, f4, f8, f16)[:TR] - ddiff[:TR]

    mat = pl.BlockSpec((D_POOL, D_POOL), lambda i: (0, 0))
    return pl.pallas_call(
        body, name="pool_bwd", grid=(nt,),
        in_specs=[_row_spec(TR, D_POOL), _prev_halo(POOL_HALO, D_POOL, 0), _row_spec(TR, D_POOL),
                  _next_halo(POOL_HALO, D_POOL, 0), mat, mat, _vec_spec(D_POOL)],
        out_specs=[_row_spec(TR, D_POOL), mat, _vec_spec(D_POOL)],
        out_shape=[jax.ShapeDtypeStruct((S, D_POOL), F32), jax.ShapeDtypeStruct((D_POOL, D_POOL), F32),
                   jax.ShapeDtypeStruct((1, D_POOL), F32)],
        compiler_params=_params(("arbitrary",)),
    )(proj, proj, dycat, dycat, w_bd, w_bd_t, ps)


def _xa_probs(q, k):
    s = _dot(q, k, NT) * (XA_DH ** -0.5)
    e = jnp.exp(s - jnp.max(s, axis=-1, keepdims=True))
    return e * (1.0 / jnp.sum(e, axis=-1, keepdims=True))


def _xattn_fwd(qx, kv):
    def body(q_ref, kv_ref, o_ref):
        for h in range(XA_HEADS):
            cols = slice(XA_DH * h, XA_DH * (h + 1))
            vcols = slice(D + XA_DH * h, D + XA_DH * (h + 1))
            p = _xa_probs(q_ref[:, cols], kv_ref[:, cols])
            o_ref[:, cols] = _dot(p.astype(BF16), kv_ref[:, vcols]).astype(BF16)

    return pl.pallas_call(
        body, name="xattn_fwd", grid=(S // TR,),
        in_specs=[_row_spec(TR, D), pl.BlockSpec((MEM, 2 * D), lambda i: (0, 0))],
        out_specs=_row_spec(TR, D), out_shape=jax.ShapeDtypeStruct((S, D), BF16),
        compiler_params=_params(("parallel",)),
    )(qx, kv)


def _xattn_bwd(qx, kv, dxo):
    def body(q_ref, kv_ref, do_ref, dq_ref, dkv_ref):
        i = pl.program_id(0)

        @pl.when(i == 0)
        def _():
            dkv_ref[...] = jnp.zeros_like(dkv_ref)

        for h in range(XA_HEADS):
            cols = slice(XA_DH * h, XA_DH * (h + 1))
            vcols = slice(D + XA_DH * h, D + XA_DH * (h + 1))
            q = q_ref[:, cols]
            k = kv_ref[:, cols]
            do = do_ref[:, cols]
            p = _xa_probs(q, k)
            dkv_ref[:, vcols] += _dot(p.astype(BF16), do, TN)
            dp = _dot(do, kv_ref[:, vcols], NT)
            ds = (p * (dp - jnp.sum(p * dp, axis=-1, keepdims=True)) * (XA_DH ** -0.5)).astype(BF16)
            dq_ref[:, cols] = _dot(ds, k).astype(BF16)
            dkv_ref[:, cols] += _dot(ds, q, TN)

    kv_spec = pl.BlockSpec((MEM, 2 * D), lambda i: (0, 0))
    return pl.pallas_call(
        body, name="xattn_bwd", grid=(S // TR,), in_specs=[_row_spec(TR, D), kv_spec, _row_spec(TR, D)],
        out_specs=[_row_spec(TR, D), kv_spec],
        out_shape=[jax.ShapeDtypeStruct((S, D), BF16), jax.ShapeDtypeStruct((MEM, 2 * D), F32)],
        compiler_params=_params(("arbitrary",)),
    )(qx, kv, dxo)


CONV_HALO = SUBLANES
TC = 512
TC_FWD = 1024
GELU_K = 0.7978845608028654
GELU_C = 0.044715


def _conv3(ext, w, rows):
    h0 = ext[CONV_HALO:CONV_HALO + rows]
    h1 = pltpu.roll(ext, 1, 0)[CONV_HALO:CONV_HALO + rows]
    h2 = pltpu.roll(ext, 2, 0)[CONV_HALO:CONV_HALO + rows]
    return w[2:3] * h0 + w[1:2] * h1 + w[0:1] * h2 + w[3:4], (h2, h1, h0)


def _conv_specs(tc):
    main = pl.BlockSpec((2, TR, tc), lambda j, i: (0, i, j))
    per = TR // CONV_HALO
    prev = pl.BlockSpec((2, CONV_HALO, tc), lambda j, i: (0, jnp.maximum(i * per - 1, 0), j))
    nxt = pl.BlockSpec((2, CONV_HALO, tc), lambda j, i: (0, jnp.minimum((i + 1) * per, S // CONV_HALO - 1), j))
    par = pl.BlockSpec((2, SUBLANES, tc), lambda j, i: (0, 0, j))
    return main, prev, nxt, par


def _convgate_fwd(hid, cwb):
    tc = TC_FWD

    def body(h_ref, hp_ref, w_ref, act_ref):
        i = pl.program_id(1)
        c = []
        for g in range(2):
            ext = jnp.concatenate([jnp.where(i == 0, 0.0, hp_ref[g]), h_ref[g]], axis=0)
            c.append(_conv3(ext, w_ref[g], TR)[0])
        gate, up = c
        act_ref[...] = (jax.nn.gelu(gate, approximate=True) * up).astype(BF16)

    main, prev, _, par = _conv_specs(tc)
    return pl.pallas_call(
        body, name="convgate_fwd", grid=(D_FF // tc, S // TR), in_specs=[main, prev, par],
        out_specs=pl.BlockSpec((TR, tc), lambda j, i: (i, j)), out_shape=jax.ShapeDtypeStruct((S, D_FF), BF16),
        compiler_params=_params(("parallel", "parallel")),
    )(hid, hid, cwb)


def _convgate_bwd(hid, dact, cwb):
    nr = S // TR
    n = TR + CONV_HALO

    def body(h_ref, hp_ref, hn_ref, da_ref, dan_ref, w_ref, dh_ref, dw_ref):
        i = pl.program_id(1)

        @pl.when(i == 0)
        def _():
            dw_ref[...] = jnp.zeros_like(dw_ref)

        da = jnp.concatenate([da_ref[...], jnp.where(i == nr - 1, 0.0, dan_ref[...])], axis=0)
        c, taps = [], []
        for g in range(2):
            ext = jnp.concatenate([jnp.where(i == 0, 0.0, hp_ref[g]), h_ref[g], hn_ref[g]], axis=0)
            cg, tg = _conv3(ext, w_ref[g], n)
            c.append(cg)
            taps.append(tg)
        gate, up = c
        th = jnp.tanh(GELU_K * (gate + GELU_C * gate * gate * gate))
        gelu = 0.5 * gate * (1.0 + th)
        dgelu = 0.5 * (1.0 + th) + 0.5 * gate * (1.0 - th * th) * GELU_K * (1.0 + 3.0 * GELU_C * gate * gate)
        for g, dc in enumerate((da * up * dgelu, da * gelu)):
            w = w_ref[g]
            dh = w[2:3] * dc[:TR] + w[1:2] * pltpu.roll(dc, n - 1, 0)[:TR] + w[0:1] * pltpu.roll(dc, n - 2, 0)[:TR]
            dh_ref[g] = dh.astype(BF16)
            dcm = dc[:TR]
            for r in range(3):
                dw_ref[g, r:r + 1, :] += jnp.sum(dcm * taps[g][r][:TR], axis=0, keepdims=True)
            dw_ref[g, 3:4, :] += jnp.sum(dcm, axis=0, keepdims=True)

    main, prev, nxt, par = _conv_specs(TC)
    per = TR // CONV_HALO
    return pl.pallas_call(
        body, name="convgate_bwd", grid=(D_FF // TC, nr),
        in_specs=[main, prev, nxt, pl.BlockSpec((TR, TC), lambda j, i: (i, j)),
                  pl.BlockSpec((CONV_HALO, TC), lambda j, i: (jnp.minimum((i + 1) * per, S // CONV_HALO - 1), j)), par],
        out_specs=[main, par],
        out_shape=[jax.ShapeDtypeStruct((2, S, D_FF), BF16), jax.ShapeDtypeStruct((2, SUBLANES, D_FF), F32)],
        compiler_params=_params(("parallel", "arbitrary")),
    )(hid, hid, hid, dact, dact, cwb)


def _adam_update(w, g, m, v):
    m = ADAM_B1 * m + (1.0 - ADAM_B1) * g
    v = ADAM_B2 * v + (1.0 - ADAM_B2) * (g * g)
    m_hat = m / (1.0 - ADAM_B1 ** ADAM_STEP)
    v_hat = v / (1.0 - ADAM_B2 ** ADAM_STEP)
    return -ADAM_LR * (m_hat / (jnp.sqrt(v_hat) + ADAM_EPS) + ADAM_WD * w), m, v


def _row_tile(rows, cols, itemsize=4, target=TILE_BYTES):
    tr = SUBLANES
    while rows % (2 * tr) == 0 and 2 * tr * cols * itemsize <= target:
        tr *= 2
    assert rows % tr == 0, (rows, tr)
    return tr


def _adamw(name, w, g, m, v):
    rows, cols = w.shape
    tr = rows if rows * cols * 4 <= TILE_BYTES // 2 else _row_tile(rows, cols, target=TILE_BYTES // 2)

    def body(w_ref, g_ref, m_ref, v_ref, d_ref, nm_ref, nv_ref):
        d_ref[...], nm_ref[...], nv_ref[...] = _adam_update(w_ref[...], g_ref[...], m_ref[...], v_ref[...])

    spec = _row_spec(tr, cols)
    shape = jax.ShapeDtypeStruct((rows, cols), F32)
    return pl.pallas_call(
        body, name=name, grid=(rows // tr,), in_specs=[spec] * 4, out_specs=[spec] * 3, out_shape=[shape] * 3,
        compiler_params=_params(("parallel",)),
    )(w, g, m, v)


def _adamw_halves(name, core, w, g_mine, g_sibling, m, v):
    rows, cols = w.shape
    half = rows // 2
    tr = _row_tile(half, cols, target=TILE_BYTES // 2)
    per = half // tr

    def body(core_ref, w_ref, gm_ref, gs_ref, m_ref, v_ref, g_ref, d_ref, nm_ref, nv_ref):
        g = jnp.where(pl.program_id(0) // per == core_ref[0], gm_ref[...], gs_ref[...])
        g_ref[...] = g
        d_ref[...], nm_ref[...], nv_ref[...] = _adam_update(w_ref[...], g, m_ref[...], v_ref[...])

    spec = pl.BlockSpec((tr, cols), lambda i, core_ref: (i, 0))
    half_spec = pl.BlockSpec((tr, cols), lambda i, core_ref: (i % per, 0))
    shape = jax.ShapeDtypeStruct((rows, cols), F32)
    return pl.pallas_call(
        body, name=name, out_shape=[shape] * 4,
        grid_spec=pltpu.PrefetchScalarGridSpec(
            num_scalar_prefetch=1, grid=(rows // tr,), in_specs=[spec, half_spec, half_spec, spec, spec], out_specs=[spec] * 4),
        compiler_params=_params(("parallel",)),
    )(core, w, g_mine, g_sibling, m, v)


def _chip_sum(name, core, g, other):
    _, _, half, cols = g.shape
    tr = _row_tile(half, cols)

    def body(core_ref, g_ref, o_ref, p_ref):
        p_ref[...] = (g_ref[...] + o_ref[...]).astype(BF16)

    spec = pl.BlockSpec((None, tr, cols), lambda j, i, core_ref: (j, i, 0))
    return pl.pallas_call(
        body, name=name, out_shape=jax.ShapeDtypeStruct((N_CHIPS, half, cols), BF16),
        grid_spec=pltpu.PrefetchScalarGridSpec(
            num_scalar_prefetch=1, grid=(N_CHIPS, half // tr),
            in_specs=[pl.BlockSpec((None, None, tr, cols), lambda j, i, core_ref: (j, core_ref[0], i, 0)), spec],
            out_specs=spec),
        compiler_params=_params(("parallel", "parallel")),
    )(core, g, other)


def _mesh_sum(name, chip, received, own):
    _, half, cols = received.shape
    tr = _row_tile(half, cols, itemsize=2 * N_CHIPS)

    def body(chip_ref, r_ref, own_ref, o_ref):
        acc = None
        for j in range(N_CHIPS):
            term = jnp.where(chip_ref[0] == j, own_ref[...], r_ref[j]).astype(F32)
            acc = term if acc is None else acc + term
        o_ref[...] = acc

    return pl.pallas_call(
        body, name=name, out_shape=jax.ShapeDtypeStruct((half, cols), F32),
        grid_spec=pltpu.PrefetchScalarGridSpec(
            num_scalar_prefetch=1, grid=(half // tr,),
            in_specs=[pl.BlockSpec((N_CHIPS, tr, cols), lambda i, chip_ref: (0, i, 0)),
                      pl.BlockSpec((None, tr, cols), lambda i, chip_ref: (chip_ref[0], i, 0))],
            out_specs=pl.BlockSpec((tr, cols), lambda i, chip_ref: (i, 0))),
        compiler_params=_params(("parallel",)),
    )(chip, received, own)


CHIP_FLIPS = ((1, 0), (0, 1), (1, 1))


def _place():
    x, y, c = lax.axis_index("x"), lax.axis_index("y"), lax.axis_index("c")
    return x, y, c, 2 * x + y


def _remote(src, dst, sems_s, sems_r, k, dev):
    return pltpu.make_async_remote_copy(src_ref=src, dst_ref=dst, send_sem=sems_s.at[k], recv_sem=sems_r.at[k],
                                        device_id=dev, device_id_type=MESH)


class _Exchange:
    def __init__(self, ins, out_shapes, n_sems, start, forward, finish):
        self.ins, self.out_shapes, self.n_sems = list(ins), list(out_shapes), n_sems
        self.start, self.forward, self.finish = start, forward, finish

    def scratch(self):
        return [pltpu.SemaphoreType.DMA((self.n_sems,)), pltpu.SemaphoreType.DMA((self.n_sems,))]

    def run(self, name):
        n = len(self.ins)

        def body(*refs):
            args = (refs[:n], refs[n:2 * n]) + tuple(refs[2 * n:])
            self.start(*args)
            self.forward(*args)
            self.finish(*args)

        return pl.pallas_call(
            body, name=name, in_specs=[ANY] * n, out_specs=[ANY] * n, out_shape=self.out_shapes, scratch_shapes=self.scratch(),
        )(*self.ins)


def _all_gather_weights(halved, whole):
    nh, nw = len(halved), len(whole)
    n_arr = nh + nw

    def copies(ins, outs, sems_s, sems_r):
        x, y, c, me = _place()
        sibling = (x, y, 1 - c)
        own = [_remote(ins[k], outs[k].at[me], sems_s, sems_r, k, sibling) for k in range(n_arr)]
        first, passed = [], []
        for k in range(n_arr):
            for f, (fx, fy) in enumerate(CHIP_FLIPS):
                src, dst = (ins[k].at[c], outs[k].at[me, c]) if k < nh else (ins[k], outs[k].at[me])
                first.append(_remote(src, dst, sems_s, sems_r, n_arr + 3 * k + f, (x ^ fx, y ^ fy, c)))
        for k in range(nh):
            for f, (fx, fy) in enumerate(CHIP_FLIPS):
                landed = outs[k].at[2 * (x ^ fx) + (y ^ fy), c]
                passed.append(_remote(landed, landed, sems_s, sems_r, 4 * n_arr + 3 * k + f, sibling))
        return own, first, passed

    def start(*refs):
        own, first, _ = copies(*refs)
        for cp in own + first:
            cp.start()

    def forward(*refs):
        _, first, passed = copies(*refs)
        for arrived, cp in zip(first, passed):
            arrived.wait_recv()
            cp.start()

    def finish(*refs):
        own, first, passed = copies(*refs)
        for cp in first[3 * nh:] + passed + own:
            cp.wait_recv()
        for cp in first + passed + own:
            cp.wait_send()

    shapes = [jax.ShapeDtypeStruct((N_CHIPS,) + a.shape, a.dtype) for a in list(halved) + list(whole)]
    return _Exchange(list(halved) + list(whole), shapes, 7 * nh + 4 * nw, start, forward, finish)


def _swap_halves(gs):
    n = len(gs)

    def copies(ins, outs, sems_s, sems_r):
        x, y, c, _ = _place()
        return [_remote(ins[k].at[:, 1 - c], outs[k], sems_s, sems_r, k, (x, y, 1 - c)) for k in range(n)]

    def start(*refs):
        for cp in copies(*refs):
            cp.start()

    def finish(*refs):
        for cp in copies(*refs):
            cp.wait()

    shapes = [jax.ShapeDtypeStruct((g.shape[0],) + g.shape[2:], g.dtype) for g in gs]
    return _Exchange(gs, shapes, n, start, _no_copies, finish)


def _scatter_chips(ps):
    n = len(ps)

    def copies(ins, outs, sems_s, sems_r):
        x, y, c, me = _place()
        return [_remote(ins[k].at[2 * (x ^ fx) + (y ^ fy)], outs[k].at[me], sems_s, sems_r, 3 * k + f, (x ^ fx, y ^ fy, c))
                for k in range(n) for f, (fx, fy) in enumerate(CHIP_FLIPS)]

    def start(*refs):
        for cp in copies(*refs):
            cp.start()

    def forward(*refs):
        pass

    def finish(*refs):
        for cp in copies(*refs):
            cp.wait()

    shapes = [jax.ShapeDtypeStruct(p.shape, p.dtype) for p in ps]
    return _Exchange(ps, shapes, 3 * n, start, forward, finish)


def _swap_reduced(rs):
    n = len(rs)

    def copies(ins, outs, sems_s, sems_r):
        x, y, c, _ = _place()
        return [_remote(ins[k], outs[k], sems_s, sems_r, k, (x, y, 1 - c)) for k in range(n)]

    def start(*refs):
        for cp in copies(*refs):
            cp.start()

    def finish(*refs):
        for cp in copies(*refs):
            cp.wait()

    return _Exchange(rs, [jax.ShapeDtypeStruct(r.shape, r.dtype) for r in rs], n, start, _no_copies, finish)


N_DEV = 8


def _gather_small(buf):
    def copies(ins, outs, sems_s, sems_r):
        x, y, c, _ = _place()
        me = 4 * x + 2 * y + c
        return [_remote(ins[0], outs[0].at[me], sems_s, sems_r, o - 1, (x ^ (o >> 2), y ^ ((o >> 1) & 1), c ^ (o & 1)))
                for o in range(1, N_DEV)]

    def start(*refs):
        for cp in copies(*refs):
            cp.start()

    def finish(*refs):
        for cp in copies(*refs):
            cp.wait()

    return _Exchange([buf], [jax.ShapeDtypeStruct((N_DEV,) + buf.shape, buf.dtype)], N_DEV - 1, start, _no_copies, finish)


def _sum_devices(place, gathered, own):
    rows = own.shape[0]

    def body(place_ref, g_ref, own_ref, o_ref):
        acc = None
        for d in range(N_DEV):
            term = jnp.where(place_ref[0] == d, own_ref[...], g_ref[d])
            acc = term if acc is None else acc + term
        o_ref[...] = acc

    return pl.pallas_call(
        body, name="sum_devices", out_shape=jax.ShapeDtypeStruct((rows, LANES), F32),
        grid_spec=pltpu.PrefetchScalarGridSpec(
            num_scalar_prefetch=1, grid=(1,),
            in_specs=[pl.BlockSpec((N_DEV, rows, LANES), lambda i, place_ref: (0, 0, 0)),
                      pl.BlockSpec((rows, LANES), lambda i, place_ref: (0, 0))],
            out_specs=pl.BlockSpec((rows, LANES), lambda i, place_ref: (0, 0))),
        compiler_params=_params(("arbitrary",)),
    )(place, gathered, own)


def _no_copies(*refs):
    pass


def _no_exchange():
    return _Exchange([], [], 1, _no_copies, _no_copies, _no_copies)


class _NoComm:
    def gather_first(self):
        return _no_exchange()

    def first_landed(self, p, landed):
        pass

    def gather_rest(self, p):
        return _no_exchange()

    def weights_landed(self, p, landed):
        pass

    def gather_last(self):
        return _no_exchange()

    def last_landed(self, p, landed):
        pass

    def swap_first(self, g):
        return _no_exchange()

    def first_swapped(self, landed):
        pass

    def swap_second(self, g):
        return _no_exchange()

    def second_swapped(self, landed):
        pass

    def scatter_early(self, g):
        return _no_exchange()

    def scatter_landed(self, landed):
        pass

    def swap_reduced_early(self):
        return _no_exchange()

    def reduced_landed(self, landed):
        pass

    def scatter_late(self, g):
        return _no_exchange()

    def late_landed(self, landed):
        pass


def _local_step(x, mem, target, p, comm):
    h1, landed = _norm_fwd("norm_mix_pre", x, p["norm_mix_pre"], comm.gather_first())
    comm.first_landed(p, landed)
    qa, ka, va, u, z = _in_proj(h1, p["w_in"], p["bf_pad"])
    ycat, qab, landed = _fox_fwd(qa, ka, va, comm.gather_rest(p))
    comm.weights_landed(p, landed)
    ycat = _pool_fwd(u, p["w_pool_bd"], p["pool_scale"], ycat)
    y1, x2, h2, qx = _proj_resid_norm("mix_out", ycat, p["w_mix_out"], x, p["norm_mix_post"], p["norm_xa_pre"], p["w_xq"])
    mem_n = _norm_fwd("norm_mem", mem, p["norm_mem"])
    kv = _mm(
        "xkv", mem_n, p["w_xkv"], pl.BlockSpec((MEM, D), lambda i, j, k: (0, 0)),
        pl.BlockSpec((None, D, 512), lambda i, j, k: (j, 0, 0)), jax.ShapeDtypeStruct((MEM, 2 * D), BF16),
        pl.BlockSpec((MEM, 512), lambda i, j, k: (0, j)), (1, N_CHIPS, 1), NN, (MEM, 512))
    xo = _xattn_fwd(qx, kv)
    y2, x3, h3 = _proj_resid_norm("xo", xo, p["w_xo"], x2, p["norm_xa_post"], p["norm_ffn_pre"])
    hid, landed = _mm(
        "up_proj", h3, p["w_up"], pl.BlockSpec((1024, D), lambda i, j, k: (i, 0)),
        pl.BlockSpec((None, D, 1024), lambda i, j, k: (j // 2, 0, j % 2)), jax.ShapeDtypeStruct((2, S, D_FF), F32),
        pl.BlockSpec((None, 1024, 1024), lambda i, j, k: (j // 4, i, j % 4)), (S // 1024, 8, 1), NN, (1024, 1024),
        comm.gather_last())
    comm.last_landed(p, landed)
    act = _convgate_fwd(hid, p["cwb"])

    g = {}
    dres, dy3, g["norm_ffn_post"], loss_cols = _down_loss_bwd(act, p["w_down"], x3, p["norm_ffn_post"], target)
    dact = _mm_nt("d_act", dy3, p["w_down"], F32, 1024, 1024)
    g["w_down"] = _mm_tn("dw_down", act, dy3, 1024, 512)
    dhid, dcwb = _convgate_bwd(hid, dact, p["cwb"])
    g["w_up"] = _mm(
        "dw_up", h3, dhid, pl.BlockSpec((S, D), lambda i, j, k: (0, 0)),
        pl.BlockSpec((None, S, 512), lambda i, j, k: (j // 8, 0, j % 8)), jax.ShapeDtypeStruct((N_CHIPS, D, 2048), F32),
        pl.BlockSpec((None, D, 512), lambda i, j, k: (j // 4, 0, j % 4)), (1, 16, 1), TN, (D, 512))
    dh3, landed = _d_h3(dhid, p["w_up"], comm.swap_first(g))
    comm.first_swapped(landed)
    dres, dy2, dxo, g["norm_ffn_pre"], g["norm_xa_post"] = _mid_bwd(
        "bwd_ffn_xa", dres, x3, p["norm_ffn_pre"], dh3, y2, p["norm_xa_post"], p["w_xo"])
    g["w_xo"] = _mm_tn("dw_xo", xo, dy2, 1024, 512)
    dqx, dkv = _xattn_bwd(qx, kv, dxo)
    dkv = dkv.astype(BF16)
    g["w_xq"] = _mm_tn("dw_xq", h2, dqx, 1024, 512)
    dmem_n = _mm(
        "d_mem", dkv, p["w_xkv"], pl.BlockSpec((MEM, 512), lambda i, j, k: (0, k)),
        pl.BlockSpec((None, D, 512), lambda i, j, k: (k, 0, 0)), jax.ShapeDtypeStruct((MEM, D), F32),
        pl.BlockSpec((MEM, D), lambda i, j, k: (0, 0)), (1, 1, N_CHIPS), NT, (MEM, D))
    g["w_xkv"] = _mm(
        "dw_xkv", mem_n, dkv, pl.BlockSpec((MEM, D), lambda i, j, k: (0, 0)),
        pl.BlockSpec((MEM, 512), lambda i, j, k: (0, j)), jax.ShapeDtypeStruct((N_CHIPS, D, 512), F32),
        pl.BlockSpec((None, D, 512), lambda i, j, k: (j, 0, 0)), (1, N_CHIPS, 1), TN, (D, 512))
    g["norm_mem"] = _gain_bwd("dg_mem", mem, p["norm_mem"], dmem_n)
    (dres, dy1, g["norm_xa_pre"], g["norm_mix_post"], dy_pool, doa), landed = _bwd_xa_mix(
        dqx, p["w_xq"], dres, x2, p["norm_xa_pre"], y1, p["norm_mix_post"], p["w_mix_out"], ycat, comm.swap_second(g))
    comm.second_swapped(landed)
    g["w_mix_out"] = _mm_tn("dw_mix_out", ycat, dy1, 1024, 512)
    dqa, dka, dva, landed = _fox_bwd(qab, doa, ka, va, comm.scatter_early(g))
    comm.scatter_landed(landed)
    du, g["w_pool_full"], g["pool_scale"] = _pool_bwd(u, dy_pool, p["w_pool_bd"], p["w_pool_bd_t"], p["pool_scale"])
    dproj, g["bf_pad"] = _fox_bwd_post(dqa, dka, dva, du, z, p["bf_pad"])
    g["w_in"], landed = _mm_tn("dw_in", h1, dproj, 1024, 896, comm.swap_reduced_early())
    comm.reduced_landed(landed)
    dh1, landed = _mm_nt("d_h1", dproj, p["w_in"], F32, 1024, 1024, comm.scatter_late(g))
    comm.late_landed(landed)
    grad_x, g["norm_mix_pre"] = _first_bwd(dres, x, p["norm_mix_pre"], dh1)
    g["cwb"] = dcwb
    return grad_x, g, loss_cols


BIG = ("w_in", "w_mix_out", "w_xq", "w_xkv", "w_xo", "w_up", "w_down")
ROW_SHARDED = ("w_mix_out", "w_xq", "w_xo", "w_down")
SMALL = ("norm_mix_pre", "norm_mix_post", "b_forget", "w_pool", "pool_scale", "norm_mem", "norm_xa_pre", "norm_xa_post",
         "norm_ffn_pre", "norm_ffn_post", "conv_b")
ORDER = ("norm_mix_pre", "norm_mix_post", "w_in", "b_forget", "w_pool", "pool_scale", "w_mix_out", "norm_mem", "norm_xa_pre",
         "norm_xa_post", "w_xq", "w_xkv", "w_xo", "norm_ffn_pre", "norm_ffn_post", "w_up", "conv_w", "conv_b", "w_down")
SLOT = SUBLANES * LANES


def _pack(parts):
    rows, offs, off = [], [], 0
    for a in parts:
        flat = a.reshape(-1).astype(F32)
        n = -(-flat.shape[0] // SLOT) * SLOT
        rows.append(jnp.pad(flat, (0, n - flat.shape[0])).reshape(n // LANES, LANES))
        offs.append(off)
        off += n // LANES
    return jnp.concatenate(rows, axis=0), offs


def _unpack(buf, off, like):
    n = like.size
    rows = -(-n // LANES)
    return buf[off:off + rows].reshape(-1)[:n].reshape(like.shape)


FIRST = ("w_in",)
REST = ("w_mix_out", "w_xq", "w_xkv", "w_xo", "w_up")
LAST = ("w_down",)


def _local_params(w):
    w_pool_bd = jnp.zeros((D_POOL, D_POOL), F32)
    for gi in range(4):
        w_pool_bd = w_pool_bd.at[64 * gi:64 * (gi + 1), 64 * gi:64 * (gi + 1)].set(w["w_pool"][0, gi])
    p = {n: w[n] for n in ("norm_mix_pre", "norm_mix_post", "norm_mem", "norm_xa_pre", "norm_xa_post", "norm_ffn_pre",
                           "norm_ffn_post")}
    p.update(
        bf_pad=jnp.pad(w["b_forget"], ((0, 0), (0, LANES - HEADS))),
        w_pool_bd=w_pool_bd.astype(BF16), w_pool_bd_t=w_pool_bd.T.astype(BF16), pool_scale=w["pool_scale"].reshape(1, D_POOL))
    return p


def _w_in_param(stacked):
    return jnp.pad(jnp.concatenate(list(stacked), axis=1), ((0, 0), (0, D_IN_PAD - D_IN)))


def _rest_params(w, full, conv_w_full):
    cw2 = conv_w_full.reshape(3, 2, D_FF).transpose(1, 0, 2)
    cwb = jnp.concatenate([cw2, w["conv_b"].reshape(1, 2, D_FF).transpose(1, 0, 2), jnp.zeros((2, 4, D_FF), F32)], axis=1)
    return dict(w_mix_out=full["w_mix_out"].reshape(D, D), w_xq=full["w_xq"].reshape(D, D), w_xkv=full["w_xkv"],
                w_xo=full["w_xo"].reshape(D, D), w_up=full["w_up"], cwb=cwb)


def _whole_params(w, full, conv_w_full):
    p = _local_params(w)
    p.update(_rest_params(w, full, conv_w_full), w_in=_w_in_param(full["w_in"]), w_down=full["w_down"].reshape(D_FF, D))
    return p


def _halved(a):
    return a.reshape(a.shape[:-2] + (2, a.shape[-2] // 2, a.shape[-1]))


class _StepComm:
    def __init__(self, w, shard2d, conv_w, core_id, chip_id):
        self.w, self.shard2d, self.conv_w, self.core_id, self.chip_id = w, shard2d, conv_w, core_id, chip_id
        self.first, self.second = ("w_up", "w_down"), ("w_xq", "w_xkv", "w_xo")
        self.early = self.first + self.second
        self.late = ("w_in", "w_mix_out")

    def gather_first(self):
        return _all_gather_weights([_halved(self.shard2d[n].astype(BF16)) for n in FIRST], [])

    def first_landed(self, p, landed):
        p["w_in"] = _w_in_param(landed[0].reshape((N_CHIPS,) + self.shard2d["w_in"].shape))

    def gather_rest(self, p):
        return _all_gather_weights([_halved(self.shard2d[n].astype(BF16)) for n in REST], [self.conv_w.reshape(3, -1)])

    def weights_landed(self, p, landed):
        full = {n: a.reshape((N_CHIPS,) + self.shard2d[n].shape) for n, a in zip(REST, landed)}
        conv_w_full = jnp.transpose(landed[-1], (1, 0, 2)).reshape(3, 2 * D_FF)
        p.update(_rest_params(self.w, full, conv_w_full))

    def gather_last(self):
        return _all_gather_weights([_halved(self.shard2d[n].astype(BF16)) for n in LAST], [])

    def last_landed(self, p, landed):
        p["w_down"] = landed[0].reshape(D_FF, D)

    def _view(self, g, n):
        return _halved(g[n].reshape((N_CHIPS,) + self.shard2d[n].shape))

    def swap_first(self, g):
        return _swap_halves([self._view(g, n) for n in self.first])

    def first_swapped(self, landed):
        self.from_sibling = dict(zip(self.first, landed))

    def swap_second(self, g):
        return _swap_halves([self._view(g, n) for n in self.second])

    def second_swapped(self, landed):
        self.from_sibling.update(zip(self.second, landed))

    def scatter_early(self, g):
        self.partial = [_chip_sum("chip_sum_" + n, self.core_id, self._view(g, n), self.from_sibling[n]) for n in self.early]
        return _scatter_chips(self.partial)

    def scatter_landed(self, landed):
        self.received = list(landed)

    def swap_reduced_early(self):
        self.reduced = [_mesh_sum("mesh_sum_" + n, self.chip_id, r, own)
                        for n, r, own in zip(self.early, self.received, self.partial)]
        return _swap_reduced(self.reduced)

    def reduced_landed(self, landed):
        self.reduced_sibling = list(landed)

    def scatter_late(self, g):
        gw_in = g["w_in"][:, :D_IN]
        cols = D_IN // N_CHIPS
        views = [_halved(jnp.stack([gw_in[:, cols * j:cols * (j + 1)] for j in range(N_CHIPS)])), self._view(g, "w_mix_out")]
        from_sibling = _swap_halves(views).run("swap_halves_late")
        self.partial_late = [_chip_sum("chip_sum_" + n, self.core_id, view, other)
                             for n, view, other in zip(self.late, views, from_sibling)]
        return _scatter_chips(self.partial_late)

    def late_landed(self, landed):
        self.received_late = list(landed)


def kernel(x, mem, norm_mix_pre, norm_mix_post, w_in, b_forget, w_pool, pool_scale, w_mix_out, norm_mem, norm_xa_pre, norm_xa_post, w_xq, w_xkv, w_xo, norm_ffn_pre, norm_ffn_post, w_up, conv_w, conv_b, w_down, loss_target, m_norm_mix_pre, m_norm_mix_post, m_w_in, m_b_forget, m_w_pool, m_pool_scale, m_w_mix_out, m_norm_mem, m_norm_xa_pre, m_norm_xa_post, m_w_xq, m_w_xkv, m_w_xo, m_norm_ffn_pre, m_norm_ffn_post, m_w_up, m_conv_w, m_conv_b, m_w_down, v_norm_mix_pre, v_norm_mix_post, v_w_in, v_b_forget, v_w_pool, v_pool_scale, v_w_mix_out, v_norm_mem, v_norm_xa_pre, v_norm_xa_post, v_w_xq, v_w_xkv, v_w_xo, v_norm_ffn_pre, v_norm_ffn_post, v_w_up, v_conv_w, v_conv_b, v_w_down):
    w = dict(norm_mix_pre=norm_mix_pre, norm_mix_post=norm_mix_post, w_in=w_in, b_forget=b_forget, w_pool=w_pool,
             pool_scale=pool_scale, w_mix_out=w_mix_out, norm_mem=norm_mem, norm_xa_pre=norm_xa_pre, norm_xa_post=norm_xa_post,
             w_xq=w_xq, w_xkv=w_xkv, w_xo=w_xo, norm_ffn_pre=norm_ffn_pre, norm_ffn_post=norm_ffn_post, w_up=w_up,
             conv_w=conv_w, conv_b=conv_b, w_down=w_down)
    m = dict(norm_mix_pre=m_norm_mix_pre, norm_mix_post=m_norm_mix_post, w_in=m_w_in, b_forget=m_b_forget, w_pool=m_w_pool,
             pool_scale=m_pool_scale, w_mix_out=m_w_mix_out, norm_mem=m_norm_mem, norm_xa_pre=m_norm_xa_pre,
             norm_xa_post=m_norm_xa_post, w_xq=m_w_xq, w_xkv=m_w_xkv, w_xo=m_w_xo, norm_ffn_pre=m_norm_ffn_pre,
             norm_ffn_post=m_norm_ffn_post, w_up=m_w_up, conv_w=m_conv_w, conv_b=m_conv_b, w_down=m_w_down)
    v = dict(norm_mix_pre=v_norm_mix_pre, norm_mix_post=v_norm_mix_post, w_in=v_w_in, b_forget=v_b_forget, w_pool=v_w_pool,
             pool_scale=v_pool_scale, w_mix_out=v_w_mix_out, norm_mem=v_norm_mem, norm_xa_pre=v_norm_xa_pre,
             norm_xa_post=v_norm_xa_post, w_xq=v_w_xq, w_xkv=v_w_xkv, w_xo=v_w_xo, norm_ffn_pre=v_norm_ffn_pre,
             norm_ffn_post=v_norm_ffn_post, w_up=v_w_up, conv_w=v_conv_w, conv_b=v_conv_b, w_down=v_w_down)
    chip = 2 * lax.axis_index("x") + lax.axis_index("y")

    core_id = lax.axis_index("c").astype(jnp.int32).reshape(1)
    chip_id = chip.astype(jnp.int32).reshape(1)

    shard2d = {n: w[n][0] for n in BIG}
    p = _local_params(w)
    comm = _StepComm(w, shard2d, conv_w, core_id, chip_id)
    grad_x, g, loss_cols = _local_step(x[0], mem[0], loss_target[0], p, comm)

    reduced_late = [_mesh_sum("mesh_sum_" + n, chip_id, r, own)
                    for n, r, own in zip(comm.late, comm.received_late, comm.partial_late)]
    names = comm.late + comm.early
    reduced = reduced_late + comm.reduced
    reduced_sibling = list(_swap_reduced(reduced_late).run("swap_reduced_late")) + comm.reduced_sibling
    grads = {}

    gw_pool = jnp.stack([g["w_pool_full"][64 * gi:64 * (gi + 1), 64 * gi:64 * (gi + 1)] for gi in range(4)])
    dcwb = g["cwb"]
    g_conv_w = dcwb[:, 0:3, :].transpose(1, 0, 2).reshape(3, 2 * D_FF)
    g_conv_b = dcwb[:, 3, :].reshape(2 * D_FF)
    small_g = dict(norm_mix_pre=g["norm_mix_pre"], norm_mix_post=g["norm_mix_post"], b_forget=g["bf_pad"][:, :HEADS],
                   w_pool=gw_pool, pool_scale=g["pool_scale"], norm_mem=g["norm_mem"], norm_xa_pre=g["norm_xa_pre"],
                   norm_xa_post=g["norm_xa_post"], norm_ffn_pre=g["norm_ffn_pre"], norm_ffn_post=g["norm_ffn_post"],
                   conv_b=g_conv_b)
    local_buf, offs = _pack([small_g[n] for n in SMALL] + [g_conv_w, loss_cols])

    delta, new_m, new_v = {}, {}, {}
    for n, g_mine, g_sibling in zip(names, reduced, reduced_sibling):
        gn, d, nm, nv = _adamw_halves("adamw_" + n, core_id, shard2d[n], g_mine, g_sibling, m[n][0], v[n][0])
        grads[n], delta[n], new_m[n], new_v[n] = gn[None], d[None], nm[None], nv[None]
    place = (2 * chip + lax.axis_index("c")).astype(jnp.int32).reshape(1)
    buf = _sum_devices(place, _gather_small(local_buf).run("gather_small")[0], local_buf)
    for n, off in zip(SMALL, offs):
        grads[n] = _unpack(buf, off, w[n])
    g_conv_w = _unpack(buf, offs[len(SMALL)], g_conv_w)
    grads["conv_w"] = lax.dynamic_slice_in_dim(g_conv_w, chip * (2 * D_FF // N_CHIPS), 2 * D_FF // N_CHIPS, axis=1).reshape(conv_w.shape)
    loss = jnp.sum(_unpack(buf, offs[len(SMALL) + 1], loss_cols))
    small_names = SMALL + ("conv_w",)
    packed = [_pack([d[n] for n in small_names])[0] for d in (w, grads, m, v)]
    offs = _pack([w[n] for n in small_names])[1]
    d, nm, nv = _adamw("adamw_small", *packed)
    for n, off in zip(small_names, offs):
        delta[n], new_m[n], new_v[n] = _unpack(d, off, w[n]), _unpack(nm, off, w[n]), _unpack(nv, off, w[n])

    return (loss, grad_x[None], *[grads[n] for n in ORDER], *[delta[n] for n in ORDER], *[new_m[n] for n in ORDER],
            *[new_v[n] for n in ORDER])
```

```python
import functools

import jax
import jax.numpy as jnp
import numpy as np
from jax import lax
from jax.experimental import pallas as pl
from jax.experimental.pallas import tpu as pltpu

F32 = jnp.float32
BF16 = jnp.bfloat16
MESH = pl.DeviceIdType.MESH
ANY = pl.BlockSpec(memory_space=pl.ANY)
VMEM_SPEC = pl.BlockSpec(memory_space=pltpu.VMEM)

S = 4096
D = 1024
MEM = 256
D_POOL = 256
HEADS = 12
DH = 64
D_FOX = HEADS * DH
D_IN = D_POOL + 3 * D_FOX + HEADS
F_OFF = D_POOL + 3 * D_FOX
Q_OFF, K_OFF, V_OFF = D_POOL, D_POOL + D_FOX, D_POOL + 2 * D_FOX
XA_HEADS = 4
XA_DH = 256
D_FF = 4096
EPS = 1e-6
N_CHIPS = 4
ADAM_LR, ADAM_B1, ADAM_B2, ADAM_EPS, ADAM_WD, ADAM_STEP = 0.001, 0.9, 0.999, 1e-08, 0.01, 10

LANES = 128
SUBLANES = 8
D_IN_PAD = 21 * LANES
TR = 512
TILE_BYTES = 2 * 1024 * 1024
NEG = -1e30
VMEM_LIMIT = 52 * 1024 * 1024

NN = (((1,), (0,)), ((), ()))
NT = (((1,), (1,)), ((), ()))
TN = (((0,), (0,)), ((), ()))


def _dot(a, b, dims=NN):
    return lax.dot_general(a, b, dims, preferred_element_type=F32)


def _params(sem):
    return pltpu.CompilerParams(dimension_semantics=sem, vmem_limit_bytes=VMEM_LIMIT)


def _split3(x):
    hi = x.astype(BF16)
    r = x - hi.astype(F32)
    mid = r.astype(BF16)
    lo = (r - mid.astype(F32)).astype(BF16)
    return hi, mid, lo


def _split3_f32(x):
    hi = x.astype(BF16).astype(F32)
    r = x - hi
    mid = r.astype(BF16).astype(F32)
    return hi, mid, r - mid


def _lane_iota(shape):
    return lax.broadcasted_iota(jnp.int32, shape, len(shape) - 1)


def _row_iota(shape):
    return lax.broadcasted_iota(jnp.int32, shape, len(shape) - 2)


def _mm(name, a, b, a_spec, b_spec, out_shape, out_spec, grid, dims, acc_shape, ex=None):
    nk = grid[2]
    if ex is not None:
        return _mm_hosting(name, a, b, a_spec, b_spec, out_shape, out_spec, grid, dims, ex)

    def body(a_ref, b_ref, o_ref, *scr):
        p = _dot(a_ref[...], b_ref[...], dims)
        if nk == 1:
            o_ref[...] = p.astype(o_ref.dtype)
        else:
            acc = scr[0]
            k = pl.program_id(2)

            @pl.when(k == 0)
            def _():
                acc[...] = p

            @pl.when(k > 0)
            def _():
                acc[...] += p

            @pl.when(k == nk - 1)
            def _():
                o_ref[...] = acc[...].astype(o_ref.dtype)

    return pl.pallas_call(
        body, name=name, grid=grid, in_specs=[a_spec, b_spec], out_specs=out_spec, out_shape=out_shape,
        scratch_shapes=[pltpu.VMEM(acc_shape, F32)] if nk > 1 else [],
        compiler_params=_params(("parallel", "parallel", "arbitrary")),
    )(a, b)


def _mm_hosting(name, a, b, a_spec, b_spec, out_shape, out_spec, grid, dims, ex):
    assert grid[2] == 1
    n = len(ex.ins)

    def body(*refs):
        i, j = pl.program_id(0), pl.program_id(1)
        last = (i == grid[0] - 1) & (j == grid[1] - 1)
        (a_ref, b_ref), (o_ref,), _, begin, end = _hosted(ex, refs, 2, 1, (i == 0) & (j == 0), last, last)
        begin()
        o_ref[...] = _dot(a_ref[...], b_ref[...], dims).astype(o_ref.dtype)
        end()

    res = pl.pallas_call(
        body, name=name, grid=grid, in_specs=[a_spec, b_spec] + [ANY] * n, out_specs=[out_spec] + [ANY] * n,
        out_shape=[out_shape] + ex.out_shapes, scratch_shapes=ex.scratch(),
        compiler_params=_params(("arbitrary", "arbitrary", "arbitrary")),
    )(a, b, *ex.ins)
    return res[0], res[1:]


def _mm_nn(name, a, b, out_dtype, tm, tn):
    m, k = a.shape
    n = b.shape[1]
    return _mm(name, a, b, pl.BlockSpec((tm, k), lambda i, j, kk: (i, 0)), pl.BlockSpec((k, tn), lambda i, j, kk: (0, j)),
               jax.ShapeDtypeStruct((m, n), out_dtype), pl.BlockSpec((tm, tn), lambda i, j, kk: (i, j)),
               (m // tm, n // tn, 1), NN, (tm, tn))


def _mm_nt(name, a, b, out_dtype, tm, tn, ex=None):
    m, k = a.shape
    n = b.shape[0]
    return _mm(name, a, b, pl.BlockSpec((tm, k), lambda i, j, kk: (i, 0)), pl.BlockSpec((tn, k), lambda i, j, kk: (j, 0)),
               jax.ShapeDtypeStruct((m, n), out_dtype), pl.BlockSpec((tm, tn), lambda i, j, kk: (i, j)),
               (m // tm, n // tn, 1), NT, (tm, tn), ex)


def _mm_tn(name, a, b, tka, tn, ex=None):
    t, ka = a.shape
    n = b.shape[1]
    return _mm(name, a, b, pl.BlockSpec((t, tka), lambda i, j, kk: (0, i)), pl.BlockSpec((t, tn), lambda i, j, kk: (0, j)),
               jax.ShapeDtypeStruct((ka, n), F32), pl.BlockSpec((tka, tn), lambda i, j, kk: (i, j)),
               (ka // tka, n // tn, 1), TN, (tka, tn), ex)


def _d_h3(dhid, w_up, ex):
    tm = tn = 1024
    shard = 2 * D_FF // N_CHIPS
    per_plane = D_FF // shard
    grid = (S // tm, D // tn, N_CHIPS)
    n = len(ex.ins)

    def body(*refs):
        i, j, k = pl.program_id(0), pl.program_id(1), pl.program_id(2)
        first = (i == 0) & (j == 0) & (k == 0)
        last = (i == grid[0] - 1) & (j == grid[1] - 1) & (k == N_CHIPS - 1)
        (a_ref, b_ref), (o_ref,), (acc_ref,), begin, end = _hosted(ex, refs, 2, 1, first, first, last)
        begin()
        part = _dot(a_ref[...], b_ref[...], NT)

        @pl.when(k == 0)
        def _():
            acc_ref[...] = part

        @pl.when(k > 0)
        def _():
            acc_ref[...] += part

        @pl.when(k == N_CHIPS - 1)
        def _():
            o_ref[...] = acc_ref[...]

        end()

    res = pl.pallas_call(
        body, name="d_h3", grid=grid,
        in_specs=[pl.BlockSpec((None, tm, shard), lambda i, j, k: (k // per_plane, i, k % per_plane)),
                  pl.BlockSpec((None, tn, shard), lambda i, j, k: (k, j, 0))] + [ANY] * n,
        out_specs=[pl.BlockSpec((tm, tn), lambda i, j, k: (i, j))] + [ANY] * n,
        out_shape=[jax.ShapeDtypeStruct((S, D), F32)] + ex.out_shapes,
        scratch_shapes=[pltpu.VMEM((tm, tn), F32)] + ex.scratch(),
        compiler_params=_params(("arbitrary", "arbitrary", "arbitrary")),
    )(dhid, w_up, *ex.ins)
    return res[0], res[1:]


def _rms(x, g):
    r = lax.rsqrt(jnp.mean(x * x, axis=-1, keepdims=True) + EPS)
    return x * r * g


def _rms_bwd(x, g, dy):
    r = lax.rsqrt(jnp.mean(x * x, axis=-1, keepdims=True) + EPS)
    xh = x * r
    dxh = dy * g
    dx = r * (dxh - xh * jnp.mean(dxh * xh, axis=-1, keepdims=True))
    return dx, jnp.sum(dy * xh, axis=0, keepdims=True)


def _row_spec(tr, width):
    return pl.BlockSpec((tr, width), lambda i: (i, 0))


def _vec_spec(width):
    return pl.BlockSpec((1, width), lambda i: (0, 0))


def _norm_fwd(name, x, g, ex=None):
    rows, width = x.shape
    tr = min(TR, rows)
    steps = rows // tr
    hosted = ex if ex is not None else _no_exchange()
    n = len(hosted.ins)

    def body(*refs):
        i = pl.program_id(0)
        (x_ref, g_ref), (h_ref,), _, begin, end = _hosted(hosted, refs, 2, 1, i == 0, i == steps - 1, i == steps - 1)
        begin()
        h_ref[...] = _rms(x_ref[...], g_ref[...]).astype(BF16)
        end()

    res = pl.pallas_call(
        body, name=name, grid=(steps,), in_specs=[_row_spec(tr, width), _vec_spec(width)] + [ANY] * n,
        out_specs=[_row_spec(tr, width)] + [ANY] * n,
        out_shape=[jax.ShapeDtypeStruct((rows, width), BF16)] + hosted.out_shapes, scratch_shapes=hosted.scratch(),
        compiler_params=_params(("arbitrary",)),
    )(x, g, *hosted.ins)
    return res[0] if ex is None else (res[0], res[1:])


def _proj_resid_norm(name, a, w, xp, g_post, g_pre, w_next=None):
    def body(a_ref, w_ref, xp_ref, gpost_ref, gpre_ref, *rest):
        y_ref, xn_ref, h_ref = rest[-3:] if w_next is None else rest[1:4]
        y = _dot(a_ref[...], w_ref[...])
        y_ref[...] = y
        xn = xp_ref[...] + _rms(y, gpost_ref[...])
        xn_ref[...] = xn
        h = _rms(xn, gpre_ref[...]).astype(BF16)
        h_ref[...] = h
        if w_next is not None:
            rest[4][...] = _dot(h, rest[0][...]).astype(BF16)

    mat = pl.BlockSpec((D, D), lambda i: (0, 0))
    more = [] if w_next is None else [w_next]
    return pl.pallas_call(
        body, name=name, grid=(S // TR,),
        in_specs=[_row_spec(TR, D), mat, _row_spec(TR, D), _vec_spec(D), _vec_spec(D)] + [mat] * len(more),
        out_specs=[_row_spec(TR, D)] * (3 + len(more)),
        out_shape=[jax.ShapeDtypeStruct((S, D), F32), jax.ShapeDtypeStruct((S, D), F32), jax.ShapeDtypeStruct((S, D), BF16)]
        + [jax.ShapeDtypeStruct((S, D), BF16)] * len(more),
        compiler_params=_params(("parallel",)),
    )(a, w, xp, g_post, g_pre, *more)


def _down_loss_bwd(act, w_down, x3, g_post, target):
    def body(a_ref, w_ref, x_ref, g_ref, t_ref, dres_ref, dy_ref, dg_ref, loss_ref):
        i = pl.program_id(0)

        @pl.when(i == 0)
        def _():
            dg_ref[...] = jnp.zeros_like(dg_ref)
            loss_ref[...] = jnp.zeros_like(loss_ref)

        y = _dot(a_ref[...], w_ref[...])
        g = g_ref[...]
        e = x_ref[...] + _rms(y, g) - t_ref[...]
        loss_ref[...] += jnp.sum(e * e, axis=0, keepdims=True) * (0.5 / D)
        dres = e * (1.0 / D)
        dres_ref[...] = dres
        dy, dg = _rms_bwd(y, g, dres)
        dy_ref[...] = dy.astype(BF16)
        dg_ref[...] += dg

    return pl.pallas_call(
        body, name="down_loss_bwd", grid=(S // TR,),
        in_specs=[_row_spec(TR, D_FF), pl.BlockSpec((D_FF, D), lambda i: (0, 0)), _row_spec(TR, D), _vec_spec(D),
                  _row_spec(TR, D)],
        out_specs=[_row_spec(TR, D), _row_spec(TR, D), _vec_spec(D), _vec_spec(D)],
        out_shape=[jax.ShapeDtypeStruct((S, D), F32), jax.ShapeDtypeStruct((S, D), BF16),
                   jax.ShapeDtypeStruct((1, D), F32), jax.ShapeDtypeStruct((1, D), F32)],
        compiler_params=_params(("arbitrary",)),
    )(act, w_down, x3, g_post, target)


def _mid_bwd(name, dres, xcur, g_pre, dh, yprev, g_post, w):
    def body(dres_ref, x_ref, gpre_ref, dh_ref, y_ref, gpost_ref, w_ref, dx_ref, dy_ref, da_ref, dgpre_ref, dgpost_ref):
        i = pl.program_id(0)

        @pl.when(i == 0)
        def _():
            dgpre_ref[...] = jnp.zeros_like(dgpre_ref)
            dgpost_ref[...] = jnp.zeros_like(dgpost_ref)

        dxn, dgpre = _rms_bwd(x_ref[...], gpre_ref[...], dh_ref[...])
        dx = dres_ref[...] + dxn
        dx_ref[...] = dx
        dy, dgpost = _rms_bwd(y_ref[...], gpost_ref[...], dx)
        dy = dy.astype(BF16)
        dy_ref[...] = dy
        da_ref[...] = _dot(dy, w_ref[...], NT).astype(BF16)
        dgpre_ref[...] += dgpre
        dgpost_ref[...] += dgpost

    return pl.pallas_call(
        body, name=name, grid=(S // TR,),
        in_specs=[_row_spec(TR, D), _row_spec(TR, D), _vec_spec(D), _row_spec(TR, D), _row_spec(TR, D), _vec_spec(D),
                  pl.BlockSpec((D, D), lambda i: (0, 0))],
        out_specs=[_row_spec(TR, D), _row_spec(TR, D), _row_spec(TR, D), _vec_spec(D), _vec_spec(D)],
        out_shape=[jax.ShapeDtypeStruct((S, D), F32), jax.ShapeDtypeStruct((S, D), BF16), jax.ShapeDtypeStruct((S, D), BF16),
                   jax.ShapeDtypeStruct((1, D), F32), jax.ShapeDtypeStruct((1, D), F32)],
        compiler_params=_params(("arbitrary",)),
    )(dres, xcur, g_pre, dh, yprev, g_post, w)


def _first_bwd(dres, x, g, dh):
    def body(dres_ref, x_ref, g_ref, dh_ref, dx_ref, dg_ref):
        i = pl.program_id(0)

        @pl.when(i == 0)
        def _():
            dg_ref[...] = jnp.zeros_like(dg_ref)

        dxn, dg = _rms_bwd(x_ref[...], g_ref[...], dh_ref[...])
        dx_ref[...] = dres_ref[...] + dxn
        dg_ref[...] += dg

    return pl.pallas_call(
        body, name="first_bwd", grid=(S // TR,),
        in_specs=[_row_spec(TR, D), _row_spec(TR, D), _vec_spec(D), _row_spec(TR, D)],
        out_specs=[_row_spec(TR, D), _vec_spec(D)],
        out_shape=[jax.ShapeDtypeStruct((S, D), F32), jax.ShapeDtypeStruct((1, D), F32)],
        compiler_params=_params(("arbitrary",)),
    )(dres, x, g, dh)


def _gain_bwd(name, x, g, dy):
    rows, width = x.shape

    def body(x_ref, g_ref, dy_ref, dg_ref):
        _, dg = _rms_bwd(x_ref[...], g_ref[...], dy_ref[...])
        dg_ref[...] = dg

    return pl.pallas_call(
        body, name=name, grid=(1,), in_specs=[_row_spec(rows, width), _vec_spec(width), _row_spec(rows, width)],
        out_specs=_vec_spec(width), out_shape=jax.ShapeDtypeStruct((1, width), F32),
        compiler_params=_params(("arbitrary",)),
    )(x, g, dy)


CUM_Q = DH
CUM_K = DH + 3
LSE_Q = DH + 6
BOTH_ONE = DH + 9
DEN_V = DH
DELTA = DH + 1
PREP_TR = 256
PIECE_LANES = 16
FOX_FWD_BLOCK = 1024
FOX_BWD_BLOCK = 512


def _at(lane_of_even_head, h):
    return (lane_of_even_head + DH * (h % 2)) % LANES


def _data_lanes(lane, h):
    return lane >= DH if h % 2 else lane < DH


def _pair_block(ref, off, h):
    base = ((off + DH * h) // LANES) * LANES
    return ref[:, base:base + LANES]


def _cumsum_rows(x, tri, carry):
    hi, mid, lo = _split3(x)
    return _dot(tri, hi) + _dot(tri, mid) + _dot(tri, lo) + carry


def _in_proj(h1, w_in, bf_pad):
    tr = TR

    place_q = np.zeros((LANES, HEADS * LANES), np.float32)
    place_k = np.zeros((LANES, HEADS * LANES), np.float32)
    for h in range(HEADS):
        for piece in range(3):
            place_q[PIECE_LANES * piece + h, LANES * h + _at(CUM_Q, h) + piece] = 1.0
            place_k[PIECE_LANES * piece + h, LANES * h + _at(CUM_K, h) + piece] = -1.0

    def body(h_ref, w_ref, bf_ref, pq_ref, pk_ref, qa_ref, ka_ref, va_ref, u_ref, z_ref, carry_ref):
        i = pl.program_id(0)

        @pl.when(i == 0)
        def _():
            carry_ref[...] = jnp.zeros_like(carry_ref)

        proj = _dot(h_ref[...], w_ref[...])
        u_ref[...] = proj[:, :D_POOL]
        z_ref[...] = proj[:, F_OFF:F_OFF + LANES]
        lane = _lane_iota((tr, LANES))
        z = proj[:, F_OFF:F_OFF + LANES] + bf_ref[...]
        log_f = jnp.minimum(z, 0.0) - jnp.log(1.0 + jnp.exp(-jnp.abs(z)))
        log_f = jnp.where(lane < HEADS, log_f, 0.0)
        tri = jnp.where(_row_iota((tr, tr)) >= _lane_iota((tr, tr)), 1.0, 0.0).astype(BF16)
        cum = _cumsum_rows(log_f, tri, carry_ref[0:1, :])
        carry_ref[0:1, :] = cum[tr - 1:tr, :]
        c_hi, c_mid, c_lo = _split3_f32(cum)
        pieces = (c_hi + pltpu.roll(c_mid, PIECE_LANES, 1) + pltpu.roll(c_lo, 2 * PIECE_LANES, 1)).astype(BF16)
        cum_q = _dot(pieces, pq_ref[...])
        cum_k = _dot(pieces, pk_ref[...])

        def between(first, h):
            return (lane >= _at(first, h)) & (lane < _at(first, h) + 3)

        ones_q = [jnp.where(between(CUM_K, h) | (lane == _at(BOTH_ONE, h)), 1.0, 0.0) for h in range(2)]
        ones_k = [jnp.where(between(CUM_Q, h) | between(LSE_Q, h) | (lane == _at(BOTH_ONE, h)), 1.0, 0.0) for h in range(2)]
        aug_v = [jnp.where(lane == _at(DEN_V, h), 1.0, jnp.where(between(DELTA, h), -1.0, 0.0)) for h in range(2)]
        for h in range(HEADS):
            mine = slice(LANES * h, LANES * (h + 1))
            data = _data_lanes(lane, h)
            qa_ref[h] = jnp.where(data, _pair_block(proj, Q_OFF, h) * (DH ** -0.5), cum_q[:, mine] + ones_q[h % 2]).astype(BF16)
            ka_ref[h] = jnp.where(data, _pair_block(proj, K_OFF, h), cum_k[:, mine] + ones_k[h % 2]).astype(BF16)
            va_ref[h] = jnp.where(data, _pair_block(proj, V_OFF, h), aug_v[h % 2]).astype(BF16)

    head_spec = pl.BlockSpec((HEADS, tr, LANES), lambda i: (0, i, 0))
    head_shape = jax.ShapeDtypeStruct((HEADS, S, LANES), BF16)
    place_spec = pl.BlockSpec(place_q.shape, lambda i: (0, 0))
    return pl.pallas_call(
        body, name="in_proj", grid=(S // tr,),
        in_specs=[_row_spec(tr, D), pl.BlockSpec((D, D_IN_PAD), lambda i: (0, 0)), _vec_spec(LANES), place_spec, place_spec],
        out_specs=[head_spec] * 3 + [_row_spec(tr, D_POOL), _row_spec(tr, LANES)],
        out_shape=[head_shape] * 3 + [jax.ShapeDtypeStruct((S, D_POOL), F32), jax.ShapeDtypeStruct((S, LANES), F32)],
        scratch_shapes=[pltpu.VMEM((SUBLANES, LANES), F32)], compiler_params=_params(("arbitrary",)),
    )(h1, w_in, bf_pad, jnp.asarray(place_q, BF16), jnp.asarray(place_k, BF16))


def _hosted(ex, refs, n_blocked_in, n_blocked_out, first, forward_at, last):
    n = len(ex.ins)
    own_in = refs[:n_blocked_in]
    ex_in = refs[n_blocked_in:n_blocked_in + n]
    own_out = refs[n_blocked_in + n:n_blocked_in + n + n_blocked_out]
    ex_out = refs[n_blocked_in + n + n_blocked_out:n_blocked_in + 2 * n + n_blocked_out]
    rest = refs[n_blocked_in + 2 * n + n_blocked_out:]
    args = (ex_in, ex_out, rest[-2], rest[-1])

    def begin():
        @pl.when(first)
        def _():
            ex.start(*args)

        @pl.when(forward_at)
        def _():
            ex.forward(*args)

    def end():
        @pl.when(last)
        def _():
            ex.finish(*args)

    return own_in, own_out, rest[:-2], begin, end


def _fox_fwd(qa, ka, va, ex):
    BQ = BK = FOX_FWD_BLOCK
    nq = S // BQ
    n_pairs = HEADS // 2

    def body(*refs):
        p_id, i = pl.program_id(0), pl.program_id(1)
        (qa_ref, ka_ref, va_ref), (y_ref, qab_ref), (m_scr, acc_scr), begin, end = _hosted(
            ex, refs, 3, 2, (p_id == 0) & (i == 0), (p_id == n_pairs - 1) & (i == 0), (p_id == n_pairs - 1) & (i == nq - 1))
        begin()
        lane = _lane_iota((BQ, LANES))
        causal = _row_iota((BQ, BK)) >= _lane_iota((BQ, BK))
        m_scr[...] = jnp.full_like(m_scr, NEG)
        acc_scr[...] = jnp.zeros_like(acc_scr)

        def step(j, masked):
            rows = pl.ds(pl.multiple_of(j * BK, BK), BK)
            for hh in range(2):
                s = _dot(qa_ref[hh], ka_ref[hh, rows, :], NT)
                if masked:
                    s = jnp.where(causal, s, NEG)
                m_prev = m_scr[hh]
                m_new = jnp.maximum(m_prev, jnp.max(s, axis=1, keepdims=True))
                p = jnp.exp(s - jnp.tile(m_new, (1, BK // LANES)))
                acc_scr[hh] = jnp.exp(m_prev - m_new) * acc_scr[hh] + _dot(p.astype(BF16), va_ref[hh, rows, :])
                m_scr[hh] = m_new

        def full_step(j, carry):
            step(j, False)
            return carry

        lax.fori_loop(0, i, full_step, 0)
        step(i, True)
        outs = []
        for hh in range(2):
            acc = acc_scr[hh]
            den_lane, lse_lane = _at(DEN_V, hh), _at(LSE_Q, hh)
            den = jnp.broadcast_to(acc[:, den_lane:den_lane + 1], (BQ, LANES))
            outs.append(acc * (1.0 / den))
            n_hi, n_mid, n_lo = _split3(-(m_scr[hh] + jnp.log(den)))
            qab_ref[hh] = jnp.where(lane == lse_lane, n_hi,
                                    jnp.where(lane == lse_lane + 1, n_mid, jnp.where(lane == lse_lane + 2, n_lo, qa_ref[hh])))
        y_ref[...] = jnp.where(lane < DH, outs[0], outs[1]).astype(BF16)
        end()

    pair_rows = pl.BlockSpec((2, BQ, LANES), lambda p, i: (p, i, 0))
    pair_all = pl.BlockSpec((2, S, LANES), lambda p, i: (p, 0, 0))
    n = len(ex.ins)
    res = pl.pallas_call(
        body, name="fox_fwd", grid=(n_pairs, nq), in_specs=[pair_rows, pair_all, pair_all] + [ANY] * n,
        out_specs=[pl.BlockSpec((BQ, LANES), lambda p, i: (i, D_POOL // LANES + p)), pair_rows] + [ANY] * n,
        out_shape=[jax.ShapeDtypeStruct((S, D), BF16), jax.ShapeDtypeStruct((HEADS, S, LANES), BF16)] + ex.out_shapes,
        scratch_shapes=[pltpu.VMEM((2, BQ, LANES), F32), pltpu.VMEM((2, BQ, LANES), F32)] + ex.scratch(),
        compiler_params=_params(("arbitrary", "arbitrary")),
    )(qa, ka, va, *ex.ins)
    return res[0], res[1], res[2:]


def _bwd_xa_mix(dqx, w_xq, dres, x2, g_pre, y1, g_post, w_mix_out, ycat, ex):
    steps = S // TR
    n = len(ex.ins)

    def body(*refs):
        i = pl.program_id(0)
        ((dq_ref, wq_ref, dres_ref, x_ref, gpre_ref, y_ref, gpost_ref, wm_ref, ycat_ref),
         (dx_ref, dy_ref, dgpre_ref, dgpost_ref, dp_ref, doa_ref), _, begin, end) = _hosted(
            ex, refs, 9, 6, i == 0, i == 0, i == steps - 1)
        begin()

        @pl.when(i == 0)
        def _():
            dgpre_ref[...] = jnp.zeros_like(dgpre_ref)
            dgpost_ref[...] = jnp.zeros_like(dgpost_ref)

        dxn, dgpre = _rms_bwd(x_ref[...], gpre_ref[...], _dot(dq_ref[...], wq_ref[...], NT))
        dx = dres_ref[...] + dxn
        dx_ref[...] = dx
        dy, dgpost = _rms_bwd(y_ref[...], gpost_ref[...], dx)
        dy = dy.astype(BF16)
        dy_ref[...] = dy
        dgpre_ref[...] += dgpre
        dgpost_ref[...] += dgpost

        d = _dot(dy, wm_ref[...], NT)
        dp_ref[...] = d[:, :D_POOL]
        lane = _lane_iota((TR, LANES))
        low = lane < DH
        for p in range(HEADS // 2):
            cols = slice(D_POOL + LANES * p, D_POOL + LANES * (p + 1))
            do = d[:, cols]
            prod = do * ycat_ref[:, cols].astype(F32)
            deltas = (jnp.sum(jnp.where(low, prod, 0.0), axis=1, keepdims=True),
                      jnp.sum(jnp.where(low, 0.0, prod), axis=1, keepdims=True))
            for hh in range(2):
                d_hi, d_mid, d_lo = _split3_f32(deltas[hh])
                dl = _at(DELTA, hh)
                aug = jnp.where(lane == dl, d_hi, jnp.where(lane == dl + 1, d_mid, jnp.where(lane == dl + 2, d_lo, 0.0)))
                doa_ref[2 * p + hh] = jnp.where(_data_lanes(lane, hh), do, aug).astype(BF16)
        end()

    mat = pl.BlockSpec((D, D), lambda i: (0, 0))
    res = pl.pallas_call(
        body, name="bwd_xa_mix", grid=(steps,),
        in_specs=[_row_spec(TR, D), mat, _row_spec(TR, D), _row_spec(TR, D), _vec_spec(D), _row_spec(TR, D), _vec_spec(D), mat,
                  _row_spec(TR, D)] + [ANY] * n,
        out_specs=[_row_spec(TR, D), _row_spec(TR, D), _vec_spec(D), _vec_spec(D), _row_spec(TR, D_POOL),
                   pl.BlockSpec((HEADS, TR, LANES), lambda i: (0, i, 0))] + [ANY] * n,
        out_shape=[jax.ShapeDtypeStruct((S, D), F32), jax.ShapeDtypeStruct((S, D), BF16), jax.ShapeDtypeStruct((1, D), F32),
                   jax.ShapeDtypeStruct((1, D), F32), jax.ShapeDtypeStruct((S, D_POOL), F32),
                   jax.ShapeDtypeStruct((HEADS, S, LANES), BF16)] + ex.out_shapes,
        scratch_shapes=ex.scratch(), compiler_params=_params(("arbitrary",)),
    )(dqx, w_xq, dres, x2, g_pre, y1, g_post, w_mix_out, ycat, *ex.ins)
    return res[:6], res[6:]


def _fox_bwd(qab, doa, ka, va, ex):
    BQ = BK = FOX_BWD_BLOCK
    nk = S // BK
    n_pairs = HEADS // 2

    def body(*refs):
        p_id, j = pl.program_id(0), pl.program_id(1)
        (qab_ref, doa_ref, ka_ref, va_ref), (dqa_ref, dka_ref, dva_ref), _, begin, end = _hosted(
            ex, refs, 4, 3, (p_id == 0) & (j == 0), (p_id == n_pairs - 1) & (j == 0), (p_id == n_pairs - 1) & (j == nk - 1))
        begin()

        @pl.when(j == 0)
        def _():
            dqa_ref[...] = jnp.zeros_like(dqa_ref)

        causal = _row_iota((BQ, BK)) >= _lane_iota((BQ, BK))
        dka_ref[...] = jnp.zeros_like(dka_ref)
        dva_ref[...] = jnp.zeros_like(dva_ref)

        def step(i, masked):
            rows = pl.ds(pl.multiple_of(i * BQ, BQ), BQ)
            for hh in range(2):
                kb = ka_ref[hh]
                q = qab_ref[hh, rows, :]
                do = doa_ref[hh, rows, :]
                s = _dot(q, kb, NT)
                if masked:
                    s = jnp.where(causal, s, NEG)
                p = jnp.exp(s)
                ds = p * _dot(do, va_ref[hh], NT)
                pb = p.astype(BF16)
                dsb = ds.astype(BF16)
                dva_ref[hh] += _dot(pb, do, TN)
                dka_ref[hh] += _dot(dsb, q, TN)
                dqa_ref[hh, rows, :] += _dot(dsb, kb)

        def full_step(i, carry):
            step(i, False)
            return carry

        step(j, True)
        lax.fori_loop(j + 1, nk, full_step, 0)
        end()

    pair_all = pl.BlockSpec((2, S, LANES), lambda p, j: (p, 0, 0))
    pair_rows = pl.BlockSpec((2, BK, LANES), lambda p, j: (p, j, 0))
    shape = jax.ShapeDtypeStruct((HEADS, S, LANES), F32)
    n = len(ex.ins)
    res = pl.pallas_call(
        body, name="fox_bwd", grid=(n_pairs, nk), in_specs=[pair_all, pair_all, pair_rows, pair_rows] + [ANY] * n,
        out_specs=[pair_all, pair_rows, pair_rows] + [ANY] * n, out_shape=[shape] * 3 + ex.out_shapes,
        scratch_shapes=ex.scratch(), compiler_params=_params(("arbitrary", "arbitrary")),
    )(qab, doa, ka, va, *ex.ins)
    return res[0], res[1], res[2], res[3:]


def _fox_bwd_post(dqa, dka, dva, du, proj, bf_pad):
    tr = PREP_TR
    nt = S // tr

    pick = np.zeros((HEADS * LANES, LANES), np.float32)
    for h in range(HEADS):
        pick[LANES * h + _at(BOTH_ONE, h), h] = 1.0

    def body(dqa_ref, dka_ref, dva_ref, du_ref, z_ref, bf_ref, pick_ref, dp_ref, dbf_ref, carry_ref):
        i = pl.program_id(0)

        @pl.when(i == 0)
        def _():
            carry_ref[...] = jnp.zeros_like(carry_ref)
            dbf_ref[...] = jnp.zeros_like(dbf_ref)

        lane = _lane_iota((tr, LANES))
        diff = jnp.concatenate([dqa_ref[h] - dka_ref[h] for h in range(HEADS)], axis=1)
        hi = diff.astype(BF16)
        dcum = _dot(hi, pick_ref[...]) + _dot((diff - hi.astype(F32)).astype(BF16), pick_ref[...])
        tri =jnp.where(_lane_iota((tr, tr)) >= _row_iota((tr, tr)), 1.0, 0.0).astype(BF16)
        dlog_f = _cumsum_rows(dcum, tri, carry_ref[0:1, :])
        carry_ref[0:1, :] = dlog_f[0:1, :]
        z = z_ref[...] + bf_ref[...]
        df = jnp.where(lane < HEADS, dlog_f / (1.0 + jnp.exp(z)), 0.0)
        dbf_ref[...] += jnp.sum(df, axis=0, keepdims=True)

        dp_ref[:, 0:D_POOL] = du_ref[...].astype(BF16)
        low = lane < DH
        for ref, off, scale in ((dqa_ref, Q_OFF, DH ** -0.5), (dka_ref, K_OFF, 1.0), (dva_ref, V_OFF, 1.0)):
            for p in range(HEADS // 2):
                blk = jnp.where(low, ref[2 * p], ref[2 * p + 1])
                dp_ref[:, off + LANES * p:off + LANES * (p + 1)] = (blk * scale).astype(BF16)
        dp_ref[:, F_OFF:F_OFF + LANES] = df.astype(BF16)

    head_spec = pl.BlockSpec((HEADS, tr, LANES), lambda i: (0, nt - 1 - i, 0))
    return pl.pallas_call(
        body, name="fox_bwd_post", grid=(nt,),
        in_specs=[head_spec, head_spec, head_spec, pl.BlockSpec((tr, D_POOL), lambda i: (nt - 1 - i, 0)),
                  pl.BlockSpec((tr, LANES), lambda i: (nt - 1 - i, 0)), _vec_spec(LANES),
                  pl.BlockSpec(pick.shape, lambda i: (0, 0))],
        out_specs=[pl.BlockSpec((tr, D_IN_PAD), lambda i: (nt - 1 - i, 0)), _vec_spec(LANES)],
        out_shape=[jax.ShapeDtypeStruct((S, D_IN_PAD), BF16), jax.ShapeDtypeStruct((1, LANES), F32)],
        scratch_shapes=[pltpu.VMEM((SUBLANES, LANES), F32)],
        compiler_params=_params(("arbitrary",)),
    )(dqa, dka, dva, du, proj, bf_pad, jnp.asarray(pick, BF16))


POOL_HALO = 16


def _by_group(lane, a2, a4, a8, a16):
    return jnp.where(lane < 64, a2, jnp.where(lane < 128, a4, jnp.where(lane < 192, a8, a16)))


def _window_count(lane, t):
    return jnp.minimum(t + 1, _by_group(lane, 2, 4, 8, 16)).astype(F32)


def _pool_diff(u, halo, first, tile):
    n = TR + POOL_HALO
    ext = jnp.concatenate([jnp.where(first, 0.0, halo), u], axis=0)
    s2 = ext + pltpu.roll(ext, 1, 0)
    s4 = s2 + pltpu.roll(s2, 2, 0)
    s8 = s4 + pltpu.roll(s4, 4, 0)
    s16 = s8 + pltpu.roll(s8, 8, 0)
    lane = _lane_iota((n, D_POOL))
    win = _by_group(lane, s2, s4, s8, s16)[POOL_HALO:]
    lane = _lane_iota((TR, D_POOL))
    t = tile * TR + _row_iota((TR, D_POOL))
    return win / _window_count(lane, t) - u


def _prev_halo(rows, width, col):
    per = TR // rows
    return pl.BlockSpec((rows, width), lambda i: (jnp.maximum(i * per - 1, 0), col))


def _next_halo(rows, width, col):
    per = TR // rows
    return pl.BlockSpec((rows, width), lambda i: (jnp.minimum((i + 1) * per, S // rows - 1), col))


def _pool_fwd(proj, w_bd, ps, ycat):
    def body(u_ref, halo_ref, w_ref, ps_ref, ycat_ref, y_ref):
        i = pl.program_id(0)
        diff = _pool_diff(u_ref[...], halo_ref[...], i == 0, i)
        y_ref[...] = (_dot(diff.astype(BF16), w_ref[...]) * ps_ref[...]).astype(BF16)

    return pl.pallas_call(
        body, name="pool_fwd", grid=(S // TR,),
        in_specs=[_row_spec(TR, D_POOL), _prev_halo(POOL_HALO, D_POOL, 0),
                  pl.BlockSpec((D_POOL, D_POOL), lambda i: (0, 0)), _vec_spec(D_POOL), ANY],
        out_specs=_row_spec(TR, D_POOL), out_shape=jax.ShapeDtypeStruct((S, D), BF16), input_output_aliases={4: 0},
        compiler_params=_params(("parallel",)),
    )(proj, proj, w_bd, ps, ycat)


def _pool_bwd(proj, dycat, w_bd, w_bd_t, ps):
    nt = S // TR
    n = TR + POOL_HALO

    def body(u_ref, halo_ref, dy_ref, dyn_ref, w_ref, wt_ref, ps_ref, du_ref, dw_ref, dps_ref):
        i = pl.program_id(0)

        @pl.when(i == 0)
        def _():
            dw_ref[...] = jnp.zeros_like(dw_ref)
            dps_ref[...] = jnp.zeros_like(dps_ref)

        diff = _pool_diff(u_ref[...], halo_ref[...], i == 0, i).astype(BF16)
        dy = dy_ref[...]
        dps_ref[...] += jnp.sum(dy * _dot(diff, w_ref[...]), axis=0, keepdims=True)
        dy_ext = jnp.concatenate([dy, jnp.where(i == nt - 1, 0.0, dyn_ref[...])], axis=0)
        dmixed = (dy_ext * ps_ref[...]).astype(BF16)
        ddiff = _dot(dmixed, wt_ref[...])
        dw_ref[...] += _dot(diff, dmixed[:TR], TN)
        lane = _lane_iota((n, D_POOL))
        t = i * TR + _row_iota((n, D_POOL))
        e = ddiff / _window_count(lane, t)
        f2 = e + pltpu.roll(e, n - 1, 0)
        f4 = f2 + pltpu.roll(f2, n - 2, 0)
        f8 = f4 + pltpu.roll(f4, n - 4, 0)
        f16 = f8 + pltpu.roll(f8, n - 8, 0)
        du_ref[...] = _by_group(lane, f2, f4, f8, f16)[:TR] - ddiff[:TR]

    mat = pl.BlockSpec((D_POOL, D_POOL), lambda i: (0, 0))
    return pl.pallas_call(
        body, name="pool_bwd", grid=(nt,),
        in_specs=[_row_spec(TR, D_POOL), _prev_halo(POOL_HALO, D_POOL, 0), _row_spec(TR, D_POOL),
                  _next_halo(POOL_HALO, D_POOL, 0), mat, mat, _vec_spec(D_POOL)],
        out_specs=[_row_spec(TR, D_POOL), mat, _vec_spec(D_POOL)],
        out_shape=[jax.ShapeDtypeStruct((S, D_POOL), F32), jax.ShapeDtypeStruct((D_POOL, D_POOL), F32),
                   jax.ShapeDtypeStruct((1, D_POOL), F32)],
        compiler_params=_params(("arbitrary",)),
    )(proj, proj, dycat, dycat, w_bd, w_bd_t, ps)


def _xa_probs(q, k):
    s = _dot(q, k, NT) * (XA_DH ** -0.5)
    e = jnp.exp(s - jnp.max(s, axis=-1, keepdims=True))
    return e * (1.0 / jnp.sum(e, axis=-1, keepdims=True))


def _xattn_fwd(qx, kv):
    def body(q_ref, kv_ref, o_ref):
        for h in range(XA_HEADS):
            cols = slice(XA_DH * h, XA_DH * (h + 1))
            vcols = slice(D + XA_DH * h, D + XA_DH * (h + 1))
            p = _xa_probs(q_ref[:, cols], kv_ref[:, cols])
            o_ref[:, cols] = _dot(p.astype(BF16), kv_ref[:, vcols]).astype(BF16)

    return pl.pallas_call(
        body, name="xattn_fwd", grid=(S // TR,),
        in_specs=[_row_spec(TR, D), pl.BlockSpec((MEM, 2 * D), lambda i: (0, 0))],
        out_specs=_row_spec(TR, D), out_shape=jax.ShapeDtypeStruct((S, D), BF16),
        compiler_params=_params(("parallel",)),
    )(qx, kv)


def _xattn_bwd(qx, kv, dxo):
    def body(q_ref, kv_ref, do_ref, dq_ref, dkv_ref):
        i = pl.program_id(0)

        @pl.when(i == 0)
        def _():
            dkv_ref[...] = jnp.zeros_like(dkv_ref)

        for h in range(XA_HEADS):
            cols = slice(XA_DH * h, XA_DH * (h + 1))
            vcols = slice(D + XA_DH * h, D + XA_DH * (h + 1))
            q = q_ref[:, cols]
            k = kv_ref[:, cols]
            do = do_ref[:, cols]
            p = _xa_probs(q, k)
            dkv_ref[:, vcols] += _dot(p.astype(BF16), do, TN)
            dp = _dot(do, kv_ref[:, vcols], NT)
            ds = (p * (dp - jnp.sum(p * dp, axis=-1, keepdims=True)) * (XA_DH ** -0.5)).astype(BF16)
            dq_ref[:, cols] = _dot(ds, k).astype(BF16)
            dkv_ref[:, cols] += _dot(ds, q, TN)

    kv_spec = pl.BlockSpec((MEM, 2 * D), lambda i: (0, 0))
    return pl.pallas_call(
        body, name="xattn_bwd", grid=(S // TR,), in_specs=[_row_spec(TR, D), kv_spec, _row_spec(TR, D)],
        out_specs=[_row_spec(TR, D), kv_spec],
        out_shape=[jax.ShapeDtypeStruct((S, D), BF16), jax.ShapeDtypeStruct((MEM, 2 * D), F32)],
        compiler_params=_params(("arbitrary",)),
    )(qx, kv, dxo)


CONV_HALO = SUBLANES
TC = 512
TC_FWD = 1024
GELU_K = 0.7978845608028654
GELU_C = 0.044715


def _conv3(ext, w, rows):
    h0 = ext[CONV_HALO:CONV_HALO + rows]
    h1 = pltpu.roll(ext, 1, 0)[CONV_HALO:CONV_HALO + rows]
    h2 = pltpu.roll(ext, 2, 0)[CONV_HALO:CONV_HALO + rows]
    return w[2:3] * h0 + w[1:2] * h1 + w[0:1] * h2 + w[3:4], (h2, h1, h0)


def _conv_specs(tc):
    main = pl.BlockSpec((2, TR, tc), lambda j, i: (0, i, j))
    per = TR // CONV_HALO
    prev = pl.BlockSpec((2, CONV_HALO, tc), lambda j, i: (0, jnp.maximum(i * per - 1, 0), j))
    nxt = pl.BlockSpec((2, CONV_HALO, tc), lambda j, i: (0, jnp.minimum((i + 1) * per, S // CONV_HALO - 1), j))
    par = pl.BlockSpec((2, SUBLANES, tc), lambda j, i: (0, 0, j))
    return main, prev, nxt, par


def _convgate_fwd(hid, cwb):
    tc = TC_FWD

    def body(h_ref, hp_ref, w_ref, act_ref):
        i = pl.program_id(1)
        c = []
        for g in range(2):
            ext = jnp.concatenate([jnp.where(i == 0, 0.0, hp_ref[g]), h_ref[g]], axis=0)
            c.append(_conv3(ext, w_ref[g], TR)[0])
        gate, up = c
        act_ref[...] = (jax.nn.gelu(gate, approximate=True) * up).astype(BF16)

    main, prev, _, par = _conv_specs(tc)
    return pl.pallas_call(
        body, name="convgate_fwd", grid=(D_FF // tc, S // TR), in_specs=[main, prev, par],
        out_specs=pl.BlockSpec((TR, tc), lambda j, i: (i, j)), out_shape=jax.ShapeDtypeStruct((S, D_FF), BF16),
        compiler_params=_params(("parallel", "parallel")),
    )(hid, hid, cwb)


def _convgate_bwd(hid, dact, cwb):
    nr = S // TR
    n = TR + CONV_HALO

    def body(h_ref, hp_ref, hn_ref, da_ref, dan_ref, w_ref, dh_ref, dw_ref):
        i = pl.program_id(1)

        @pl.when(i == 0)
        def _():
            dw_ref[...] = jnp.zeros_like(dw_ref)

        da = jnp.concatenate([da_ref[...], jnp.where(i == nr - 1, 0.0, dan_ref[...])], axis=0)
        c, taps = [], []
        for g in range(2):
            ext = jnp.concatenate([jnp.where(i == 0, 0.0, hp_ref[g]), h_ref[g], hn_ref[g]], axis=0)
            cg, tg = _conv3(ext, w_ref[g], n)
            c.append(cg)
            taps.append(tg)
        gate, up = c
        th = jnp.tanh(GELU_K * (gate + GELU_C * gate * gate * gate))
        gelu = 0.5 * gate * (1.0 + th)
        dgelu = 0.5 * (1.0 + th) + 0.5 * gate * (1.0 - th * th) * GELU_K * (1.0 + 3.0 * GELU_C * gate * gate)
        for g, dc in enumerate((da * up * dgelu, da * gelu)):
            w = w_ref[g]
            dh = w[2:3] * dc[:TR] + w[1:2] * pltpu.roll(dc, n - 1, 0)[:TR] + w[0:1] * pltpu.roll(dc, n - 2, 0)[:TR]
            dh_ref[g] = dh.astype(BF16)
            dcm = dc[:TR]
            for r in range(3):
                dw_ref[g, r:r + 1, :] += jnp.sum(dcm * taps[g][r][:TR], axis=0, keepdims=True)
            dw_ref[g, 3:4, :] += jnp.sum(dcm, axis=0, keepdims=True)

    main, prev, nxt, par = _conv_specs(TC)
    per = TR // CONV_HALO
    return pl.pallas_call(
        body, name="convgate_bwd", grid=(D_FF // TC, nr),
        in_specs=[main, prev, nxt, pl.BlockSpec((TR, TC), lambda j, i: (i, j)),
                  pl.BlockSpec((CONV_HALO, TC), lambda j, i: (jnp.minimum((i + 1) * per, S // CONV_HALO - 1), j)), par],
        out_specs=[main, par],
        out_shape=[jax.ShapeDtypeStruct((2, S, D_FF), BF16), jax.ShapeDtypeStruct((2, SUBLANES, D_FF), F32)],
        compiler_params=_params(("parallel", "arbitrary")),
    )(hid, hid, hid, dact, dact, cwb)


def _adam_update(w, g, m, v):
    m = ADAM_B1 * m + (1.0 - ADAM_B1) * g
    v = ADAM_B2 * v + (1.0 - ADAM_B2) * (g * g)
    m_hat = m / (1.0 - ADAM_B1 ** ADAM_STEP)
    v_hat = v / (1.0 - ADAM_B2 ** ADAM_STEP)
    return -ADAM_LR * (m_hat / (jnp.sqrt(v_hat) + ADAM_EPS) + ADAM_WD * w), m, v


def _row_tile(rows, cols, itemsize=4, target=TILE_BYTES):
    tr = SUBLANES
    while rows % (2 * tr) == 0 and 2 * tr * cols * itemsize <= target:
        tr *= 2
    assert rows % tr == 0, (rows, tr)
    return tr


def _adamw(name, w, g, m, v):
    rows, cols = w.shape
    tr = rows if rows * cols * 4 <= TILE_BYTES // 2 else _row_tile(rows, cols, target=TILE_BYTES // 2)

    def body(w_ref, g_ref, m_ref, v_ref, d_ref, nm_ref, nv_ref):
        d_ref[...], nm_ref[...], nv_ref[...] = _adam_update(w_ref[...], g_ref[...], m_ref[...], v_ref[...])

    spec = _row_spec(tr, cols)
    shape = jax.ShapeDtypeStruct((rows, cols), F32)
    return pl.pallas_call(
        body, name=name, grid=(rows // tr,), in_specs=[spec] * 4, out_specs=[spec] * 3, out_shape=[shape] * 3,
        compiler_params=_params(("parallel",)),
    )(w, g, m, v)


def _adamw_halves(name, core, w, g_mine, g_sibling, m, v):
    rows, cols = w.shape
    half = rows // 2
    tr = _row_tile(half, cols, target=TILE_BYTES // 2)
    per = half // tr

    def body(core_ref, w_ref, gm_ref, gs_ref, m_ref, v_ref, g_ref, d_ref, nm_ref, nv_ref):
        g = jnp.where(pl.program_id(0) // per == core_ref[0], gm_ref[...], gs_ref[...])
        g_ref[...] = g
        d_ref[...], nm_ref[...], nv_ref[...] = _adam_update(w_ref[...], g, m_ref[...], v_ref[...])

    spec = pl.BlockSpec((tr, cols), lambda i, core_ref: (i, 0))
    half_spec = pl.BlockSpec((tr, cols), lambda i, core_ref: (i % per, 0))
    shape = jax.ShapeDtypeStruct((rows, cols), F32)
    return pl.pallas_call(
        body, name=name, out_shape=[shape] * 4,
        grid_spec=pltpu.PrefetchScalarGridSpec(
            num_scalar_prefetch=1, grid=(rows // tr,), in_specs=[spec, half_spec, half_spec, spec, spec], out_specs=[spec] * 4),
        compiler_params=_params(("parallel",)),
    )(core, w, g_mine, g_sibling, m, v)


def _chip_sum(name, core, g, other):
    _, _, half, cols = g.shape
    tr = _row_tile(half, cols)

    def body(core_ref, g_ref, o_ref, p_ref):
        p_ref[...] = (g_ref[...] + o_ref[...]).astype(BF16)

    spec = pl.BlockSpec((None, tr, cols), lambda j, i, core_ref: (j, i, 0))
    return pl.pallas_call(
        body, name=name, out_shape=jax.ShapeDtypeStruct((N_CHIPS, half, cols), BF16),
        grid_spec=pltpu.PrefetchScalarGridSpec(
            num_scalar_prefetch=1, grid=(N_CHIPS, half // tr),
            in_specs=[pl.BlockSpec((None, None, tr, cols), lambda j, i, core_ref: (j, core_ref[0], i, 0)), spec],
            out_specs=spec),
        compiler_params=_params(("parallel", "parallel")),
    )(core, g, other)


def _mesh_sum(name, chip, received, own):
    _, half, cols = received.shape
    tr = _row_tile(half, cols, itemsize=2 * N_CHIPS)

    def body(chip_ref, r_ref, own_ref, o_ref):
        acc = None
        for j in range(N_CHIPS):
            term = jnp.where(chip_ref[0] == j, own_ref[...], r_ref[j]).astype(F32)
            acc = term if acc is None else acc + term
        o_ref[...] = acc

    return pl.pallas_call(
        body, name=name, out_shape=jax.ShapeDtypeStruct((half, cols), F32),
        grid_spec=pltpu.PrefetchScalarGridSpec(
            num_scalar_prefetch=1, grid=(half // tr,),
            in_specs=[pl.BlockSpec((N_CHIPS, tr, cols), lambda i, chip_ref: (0, i, 0)),
                      pl.BlockSpec((None, tr, cols), lambda i, chip_ref: (chip_ref[0], i, 0))],
            out_specs=pl.BlockSpec((tr, cols), lambda i, chip_ref: (i, 0))),
        compiler_params=_params(("parallel",)),
    )(chip, received, own)


CHIP_FLIPS = ((1, 0), (0, 1), (1, 1))


def _place():
    x, y, c = lax.axis_index("x"), lax.axis_index("y"), lax.axis_index("c")
    return x, y, c, 2 * x + y


def _remote(src, dst, sems_s, sems_r, k, dev):
    return pltpu.make_async_remote_copy(src_ref=src, dst_ref=dst, send_sem=sems_s.at[k], recv_sem=sems_r.at[k],
                                        device_id=dev, device_id_type=MESH)


class _Exchange:
    def __init__(self, ins, out_shapes, n_sems, start, forward, finish):
        self.ins, self.out_shapes, self.n_sems = list(ins), list(out_shapes), n_sems
        self.start, self.forward, self.finish = start, forward, finish

    def scratch(self):
        return [pltpu.SemaphoreType.DMA((self.n_sems,)), pltpu.SemaphoreType.DMA((self.n_sems,))]

    def run(self, name):
        n = len(self.ins)

        def body(*refs):
            args = (refs[:n], refs[n:2 * n]) + tuple(refs[2 * n:])
            self.start(*args)
            self.forward(*args)
            self.finish(*args)

        return pl.pallas_call(
            body, name=name, in_specs=[ANY] * n, out_specs=[ANY] * n, out_shape=self.out_shapes, scratch_shapes=self.scratch(),
        )(*self.ins)


def _all_gather_weights(halved, whole):
    nh, nw = len(halved), len(whole)
    n_arr = nh + nw

    def copies(ins, outs, sems_s, sems_r):
        x, y, c, me = _place()
        sibling = (x, y, 1 - c)
        own = [_remote(ins[k], outs[k].at[me], sems_s, sems_r, k, sibling) for k in range(n_arr)]
        first, passed = [], []
        for k in range(n_arr):
            for f, (fx, fy) in enumerate(CHIP_FLIPS):
                src, dst = (ins[k].at[c], outs[k].at[me, c]) if k < nh else (ins[k], outs[k].at[me])
                first.append(_remote(src, dst, sems_s, sems_r, n_arr + 3 * k + f, (x ^ fx, y ^ fy, c)))
        for k in range(nh):
            for f, (fx, fy) in enumerate(CHIP_FLIPS):
                landed = outs[k].at[2 * (x ^ fx) + (y ^ fy), c]
                passed.append(_remote(landed, landed, sems_s, sems_r, 4 * n_arr + 3 * k + f, sibling))
        return own, first, passed

    def start(*refs):
        own, first, _ = copies(*refs)
        for cp in own + first:
            cp.start()

    def forward(*refs):
        _, first, passed = copies(*refs)
        for arrived, cp in zip(first, passed):
            arrived.wait_recv()
            cp.start()

    def finish(*refs):
        own, first, passed = copies(*refs)
        for cp in first[3 * nh:] + passed + own:
            cp.wait_recv()
        for cp in first + passed + own:
            cp.wait_send()

    shapes = [jax.ShapeDtypeStruct((N_CHIPS,) + a.shape, a.dtype) for a in list(halved) + list(whole)]
    return _Exchange(list(halved) + list(whole), shapes, 7 * nh + 4 * nw, start, forward, finish)


def _swap_halves(gs):
    n = len(gs)

    def copies(ins, outs, sems_s, sems_r):
        x, y, c, _ = _place()
        return [_remote(ins[k].at[:, 1 - c], outs[k], sems_s, sems_r, k, (x, y, 1 - c)) for k in range(n)]

    def start(*refs):
        for cp in copies(*refs):
            cp.start()

    def finish(*refs):
        for cp in copies(*refs):
            cp.wait()

    shapes = [jax.ShapeDtypeStruct((g.shape[0],) + g.shape[2:], g.dtype) for g in gs]
    return _Exchange(gs, shapes, n, start, _no_copies, finish)


def _scatter_chips(ps):
    n = len(ps)

    def copies(ins, outs, sems_s, sems_r):
        x, y, c, me = _place()
        return [_remote(ins[k].at[2 * (x ^ fx) + (y ^ fy)], outs[k].at[me], sems_s, sems_r, 3 * k + f, (x ^ fx, y ^ fy, c))
                for k in range(n) for f, (fx, fy) in enumerate(CHIP_FLIPS)]

    def start(*refs):
        for cp in copies(*refs):
            cp.start()

    def forward(*refs):
        pass

    def finish(*refs):
        for cp in copies(*refs):
            cp.wait()

    shapes = [jax.ShapeDtypeStruct(p.shape, p.dtype) for p in ps]
    return _Exchange(ps, shapes, 3 * n, start, forward, finish)


def _swap_reduced(rs):
    n = len(rs)

    def copies(ins, outs, sems_s, sems_r):
        x, y, c, _ = _place()
        return [_remote(ins[k], outs[k], sems_s, sems_r, k, (x, y, 1 - c)) for k in range(n)]

    def start(*refs):
        for cp in copies(*refs):
            cp.start()

    def finish(*refs):
        for cp in copies(*refs):
            cp.wait()

    return _Exchange(rs, [jax.ShapeDtypeStruct(r.shape, r.dtype) for r in rs], n, start, _no_copies, finish)


N_DEV = 8


def _gather_small(buf):
    def copies(ins, outs, sems_s, sems_r):
        x, y, c, _ = _place()
        me = 4 * x + 2 * y + c
        return [_remote(ins[0], outs[0].at[me], sems_s, sems_r, o - 1, (x ^ (o >> 2), y ^ ((o >> 1) & 1), c ^ (o & 1)))
                for o in range(1, N_DEV)]

    def start(*refs):
        for cp in copies(*refs):
            cp.start()

    def finish(*refs):
        for cp in copies(*refs):
            cp.wait()

    return _Exchange([buf], [jax.ShapeDtypeStruct((N_DEV,) + buf.shape, buf.dtype)], N_DEV - 1, start, _no_copies, finish)


def _sum_devices(place, gathered, own):
    rows = own.shape[0]

    def body(place_ref, g_ref, own_ref, o_ref):
        acc = None
        for d in range(N_DEV):
            term = jnp.where(place_ref[0] == d, own_ref[...], g_ref[d])
            acc = term if acc is None else acc + term
        o_ref[...] = acc

    return pl.pallas_call(
        body, name="sum_devices", out_shape=jax.ShapeDtypeStruct((rows, LANES), F32),
        grid_spec=pltpu.PrefetchScalarGridSpec(
            num_scalar_prefetch=1, grid=(1,),
            in_specs=[pl.BlockSpec((N_DEV, rows, LANES), lambda i, place_ref: (0, 0, 0)),
                      pl.BlockSpec((rows, LANES), lambda i, place_ref: (0, 0))],
            out_specs=pl.BlockSpec((rows, LANES), lambda i, place_ref: (0, 0))),
        compiler_params=_params(("arbitrary",)),
    )(place, gathered, own)


def _no_copies(*refs):
    pass


def _no_exchange():
    return _Exchange([], [], 1, _no_copies, _no_copies, _no_copies)


class _NoComm:
    def gather_first(self):
        return _no_exchange()

    def first_landed(self, p, landed):
        pass

    def gather_rest(self, p):
        return _no_exchange()

    def weights_landed(self, p, landed):
        pass

    def gather_last(self):
        return _no_exchange()

    def last_landed(self, p, landed):
        pass

    def swap_first(self, g):
        return _no_exchange()

    def first_swapped(self, landed):
        pass

    def swap_second(self, g):
        return _no_exchange()

    def second_swapped(self, landed):
        pass

    def scatter_early(self, g):
        return _no_exchange()

    def scatter_landed(self, landed):
        pass

    def swap_reduced_early(self):
        return _no_exchange()

    def reduced_landed(self, landed):
        pass

    def scatter_late(self, g):
        return _no_exchange()

    def late_landed(self, landed):
        pass


def _local_step(x, mem, target, p, comm):
    h1, landed = _norm_fwd("norm_mix_pre", x, p["norm_mix_pre"], comm.gather_first())
    comm.first_landed(p, landed)
    qa, ka, va, u, z = _in_proj(h1, p["w_in"], p["bf_pad"])
    ycat, qab, landed = _fox_fwd(qa, ka, va, comm.gather_rest(p))
    comm.weights_landed(p, landed)
    ycat = _pool_fwd(u, p["w_pool_bd"], p["pool_scale"], ycat)
    y1, x2, h2, qx = _proj_resid_norm("mix_out", ycat, p["w_mix_out"], x, p["norm_mix_post"], p["norm_xa_pre"], p["w_xq"])
    mem_n = _norm_fwd("norm_mem", mem, p["norm_mem"])
    kv = _mm(
        "xkv", mem_n, p["w_xkv"], pl.BlockSpec((MEM, D), lambda i, j, k: (0, 0)),
        pl.BlockSpec((None, D, 512), lambda i, j, k: (j, 0, 0)), jax.ShapeDtypeStruct((MEM, 2 * D), BF16),
        pl.BlockSpec((MEM, 512), lambda i, j, k: (0, j)), (1, N_CHIPS, 1), NN, (MEM, 512))
    xo = _xattn_fwd(qx, kv)
    y2, x3, h3 = _proj_resid_norm("xo", xo, p["w_xo"], x2, p["norm_xa_post"], p["norm_ffn_pre"])
    hid, landed = _mm(
        "up_proj", h3, p["w_up"], pl.BlockSpec((2048, D), lambda i, j, k: (i, 0)),
        pl.BlockSpec((None, D, 1024), lambda i, j, k: (j // 2, 0, j % 2)), jax.ShapeDtypeStruct((2, S, D_FF), F32),
        pl.BlockSpec((None, 2048, 1024), lambda i, j, k: (j // 4, i, j % 4)), (S // 2048, 8, 1), NN, (2048, 1024),
        comm.gather_last())
    comm.last_landed(p, landed)
    act = _convgate_fwd(hid, p["cwb"])

    g = {}
    dres, dy3, g["norm_ffn_post"], loss_cols = _down_loss_bwd(act, p["w_down"], x3, p["norm_ffn_post"], target)
    dact = _mm_nt("d_act", dy3, p["w_down"], F32, 1024, 1024)
    g["w_down"] = _mm_tn("dw_down", act, dy3, 1024, 512)
    dhid, dcwb = _convgate_bwd(hid, dact, p["cwb"])
    g["w_up"] = _mm(
        "dw_up", h3, dhid, pl.BlockSpec((S, D), lambda i, j, k: (0, 0)),
        pl.BlockSpec((None, S, 512), lambda i, j, k: (j // 8, 0, j % 8)), jax.ShapeDtypeStruct((N_CHIPS, D, 2048), F32),
        pl.BlockSpec((None, D, 512), lambda i, j, k: (j // 4, 0, j % 4)), (1, 16, 1), TN, (D, 512))
    dh3, landed = _d_h3(dhid, p["w_up"], comm.swap_first(g))
    comm.first_swapped(landed)
    dres, dy2, dxo, g["norm_ffn_pre"], g["norm_xa_post"] = _mid_bwd(
        "bwd_ffn_xa", dres, x3, p["norm_ffn_pre"], dh3, y2, p["norm_xa_post"], p["w_xo"])
    g["w_xo"] = _mm_tn("dw_xo", xo, dy2, 1024, 512)
    dqx, dkv = _xattn_bwd(qx, kv, dxo)
    dkv = dkv.astype(BF16)
    g["w_xq"] = _mm_tn("dw_xq", h2, dqx, 1024, 512)
    dmem_n = _mm(
        "d_mem", dkv, p["w_xkv"], pl.BlockSpec((MEM, 512), lambda i, j, k: (0, k)),
        pl.BlockSpec((None, D, 512), lambda i, j, k: (k, 0, 0)), jax.ShapeDtypeStruct((MEM, D), F32),
        pl.BlockSpec((MEM, D), lambda i, j, k: (0, 0)), (1, 1, N_CHIPS), NT, (MEM, D))
    g["w_xkv"] = _mm(
        "dw_xkv", mem_n, dkv, pl.BlockSpec((MEM, D), lambda i, j, k: (0, 0)),
        pl.BlockSpec((MEM, 512), lambda i, j, k: (0, j)), jax.ShapeDtypeStruct((N_CHIPS, D, 512), F32),
        pl.BlockSpec((None, D, 512), lambda i, j, k: (j, 0, 0)), (1, N_CHIPS, 1), TN, (D, 512))
    g["norm_mem"] = _gain_bwd("dg_mem", mem, p["norm_mem"], dmem_n)
    (dres, dy1, g["norm_xa_pre"], g["norm_mix_post"], dy_pool, doa), landed = _bwd_xa_mix(
        dqx, p["w_xq"], dres, x2, p["norm_xa_pre"], y1, p["norm_mix_post"], p["w_mix_out"], ycat, comm.swap_second(g))
    comm.second_swapped(landed)
    g["w_mix_out"] = _mm_tn("dw_mix_out", ycat, dy1, 1024, 512)
    dqa, dka, dva, landed = _fox_bwd(qab, doa, ka, va, comm.scatter_early(g))
    comm.scatter_landed(landed)
    du, g["w_pool_full"], g["pool_scale"] = _pool_bwd(u, dy_pool, p["w_pool_bd"], p["w_pool_bd_t"], p["pool_scale"])
    dproj, g["bf_pad"] = _fox_bwd_post(dqa, dka, dva, du, z, p["bf_pad"])
    g["w_in"], landed = _mm_tn("dw_in", h1, dproj, 1024, 896, comm.swap_reduced_early())
    comm.reduced_landed(landed)
    dh1, landed = _mm_nt("d_h1", dproj, p["w_in"], F32, 1024, 1024, comm.scatter_late(g))
    comm.late_landed(landed)
    grad_x, g["norm_mix_pre"] = _first_bwd(dres, x, p["norm_mix_pre"], dh1)
    g["cwb"] = dcwb
    return grad_x, g, loss_cols


BIG = ("w_in", "w_mix_out", "w_xq", "w_xkv", "w_xo", "w_up", "w_down")
ROW_SHARDED = ("w_mix_out", "w_xq", "w_xo", "w_down")
SMALL = ("norm_mix_pre", "norm_mix_post", "b_forget", "w_pool", "pool_scale", "norm_mem", "norm_xa_pre", "norm_xa_post",
         "norm_ffn_pre", "norm_ffn_post", "conv_b")
ORDER = ("norm_mix_pre", "norm_mix_post", "w_in", "b_forget", "w_pool", "pool_scale", "w_mix_out", "norm_mem", "norm_xa_pre",
         "norm_xa_post", "w_xq", "w_xkv", "w_xo", "norm_ffn_pre", "norm_ffn_post", "w_up", "conv_w", "conv_b", "w_down")
SLOT = SUBLANES * LANES


def _pack(parts):
    rows, offs, off = [], [], 0
    for a in parts:
        flat = a.reshape(-1).astype(F32)
        n = -(-flat.shape[0] // SLOT) * SLOT
        rows.append(jnp.pad(flat, (0, n - flat.shape[0])).reshape(n // LANES, LANES))
        offs.append(off)
        off += n // LANES
    return jnp.concatenate(rows, axis=0), offs


def _unpack(buf, off, like):
    n = like.size
    rows = -(-n // LANES)
    return buf[off:off + rows].reshape(-1)[:n].reshape(like.shape)


FIRST = ("w_in",)
REST = ("w_mix_out", "w_xq", "w_xkv", "w_xo", "w_up")
LAST = ("w_down",)


def _local_params(w):
    w_pool_bd = jnp.zeros((D_POOL, D_POOL), F32)
    for gi in range(4):
        w_pool_bd = w_pool_bd.at[64 * gi:64 * (gi + 1), 64 * gi:64 * (gi + 1)].set(w["w_pool"][0, gi])
    p = {n: w[n] for n in ("norm_mix_pre", "norm_mix_post", "norm_mem", "norm_xa_pre", "norm_xa_post", "norm_ffn_pre",
                           "norm_ffn_post")}
    p.update(
        bf_pad=jnp.pad(w["b_forget"], ((0, 0), (0, LANES - HEADS))),
        w_pool_bd=w_pool_bd.astype(BF16), w_pool_bd_t=w_pool_bd.T.astype(BF16), pool_scale=w["pool_scale"].reshape(1, D_POOL))
    return p


def _w_in_param(stacked):
    return jnp.pad(jnp.concatenate(list(stacked), axis=1), ((0, 0), (0, D_IN_PAD - D_IN)))


def _rest_params(w, full, conv_w_full):
    cw2 = conv_w_full.reshape(3, 2, D_FF).transpose(1, 0, 2)
    cwb = jnp.concatenate([cw2, w["conv_b"].reshape(1, 2, D_FF).transpose(1, 0, 2), jnp.zeros((2, 4, D_FF), F32)], axis=1)
    return dict(w_mix_out=full["w_mix_out"].reshape(D, D), w_xq=full["w_xq"].reshape(D, D), w_xkv=full["w_xkv"],
                w_xo=full["w_xo"].reshape(D, D), w_up=full["w_up"], cwb=cwb)


def _whole_params(w, full, conv_w_full):
    p = _local_params(w)
    p.update(_rest_params(w, full, conv_w_full), w_in=_w_in_param(full["w_in"]), w_down=full["w_down"].reshape(D_FF, D))
    return p


def _halved(a):
    return a.reshape(a.shape[:-2] + (2, a.shape[-2] // 2, a.shape[-1]))


class _StepComm:
    def __init__(self, w, shard2d, conv_w, core_id, chip_id):
        self.w, self.shard2d, self.conv_w, self.core_id, self.chip_id = w, shard2d, conv_w, core_id, chip_id
        self.first, self.second = ("w_up", "w_down"), ("w_xq", "w_xkv", "w_xo")
        self.early = self.first + self.second
        self.late = ("w_in", "w_mix_out")

    def gather_first(self):
        return _all_gather_weights([_halved(self.shard2d[n].astype(BF16)) for n in FIRST], [])

    def first_landed(self, p, landed):
        p["w_in"] = _w_in_param(landed[0].reshape((N_CHIPS,) + self.shard2d["w_in"].shape))

    def gather_rest(self, p):
        return _all_gather_weights([_halved(self.shard2d[n].astype(BF16)) for n in REST], [self.conv_w.reshape(3, -1)])

    def weights_landed(self, p, landed):
        full = {n: a.reshape((N_CHIPS,) + self.shard2d[n].shape) for n, a in zip(REST, landed)}
        conv_w_full = jnp.transpose(landed[-1], (1, 0, 2)).reshape(3, 2 * D_FF)
        p.update(_rest_params(self.w, full, conv_w_full))

    def gather_last(self):
        return _all_gather_weights([_halved(self.shard2d[n].astype(BF16)) for n in LAST], [])

    def last_landed(self, p, landed):
        p["w_down"] = landed[0].reshape(D_FF, D)

    def _view(self, g, n):
        return _halved(g[n].reshape((N_CHIPS,) + self.shard2d[n].shape))

    def swap_first(self, g):
        return _swap_halves([self._view(g, n) for n in self.first])

    def first_swapped(self, landed):
        self.from_sibling = dict(zip(self.first, landed))

    def swap_second(self, g):
        return _swap_halves([self._view(g, n) for n in self.second])

    def second_swapped(self, landed):
        self.from_sibling.update(zip(self.second, landed))

    def scatter_early(self, g):
        self.partial = [_chip_sum("chip_sum_" + n, self.core_id, self._view(g, n), self.from_sibling[n]) for n in self.early]
        return _scatter_chips(self.partial)

    def scatter_landed(self, landed):
        self.received = list(landed)

    def swap_reduced_early(self):
        self.reduced = [_mesh_sum("mesh_sum_" + n, self.chip_id, r, own)
                        for n, r, own in zip(self.early, self.received, self.partial)]
        return _swap_reduced(self.reduced)

    def reduced_landed(self, landed):
        self.reduced_sibling = list(landed)

    def scatter_late(self, g):
        gw_in = g["w_in"][:, :D_IN]
        cols = D_IN // N_CHIPS
        views = [_halved(jnp.stack([gw_in[:, cols * j:cols * (j + 1)] for j in range(N_CHIPS)])), self._view(g, "w_mix_out")]
        from_sibling = _swap_halves(views).run("swap_halves_late")
        self.partial_late = [_chip_sum("chip_sum_" + n, self.core_id, view, other)
                             for n, view, other in zip(self.late, views, from_sibling)]
        return _scatter_chips(self.partial_late)

    def late_landed(self, landed):
        self.received_late = list(landed)


def kernel(x, mem, norm_mix_pre, norm_mix_post, w_in, b_forget, w_pool, pool_scale, w_mix_out, norm_mem, norm_xa_pre, norm_xa_post, w_xq, w_xkv, w_xo, norm_ffn_pre, norm_ffn_post, w_up, conv_w, conv_b, w_down, loss_target, m_norm_mix_pre, m_norm_mix_post, m_w_in, m_b_forget, m_w_pool, m_pool_scale, m_w_mix_out, m_norm_mem, m_norm_xa_pre, m_norm_xa_post, m_w_xq, m_w_xkv, m_w_xo, m_norm_ffn_pre, m_norm_ffn_post, m_w_up, m_conv_w, m_conv_b, m_w_down, v_norm_mix_pre, v_norm_mix_post, v_w_in, v_b_forget, v_w_pool, v_pool_scale, v_w_mix_out, v_norm_mem, v_norm_xa_pre, v_norm_xa_post, v_w_xq, v_w_xkv, v_w_xo, v_norm_ffn_pre, v_norm_ffn_post, v_w_up, v_conv_w, v_conv_b, v_w_down):
    w = dict(norm_mix_pre=norm_mix_pre, norm_mix_post=norm_mix_post, w_in=w_in, b_forget=b_forget, w_pool=w_pool,
             pool_scale=pool_scale, w_mix_out=w_mix_out, norm_mem=norm_mem, norm_xa_pre=norm_xa_pre, norm_xa_post=norm_xa_post,
             w_xq=w_xq, w_xkv=w_xkv, w_xo=w_xo, norm_ffn_pre=norm_ffn_pre, norm_ffn_post=norm_ffn_post, w_up=w_up,
             conv_w=conv_w, conv_b=conv_b, w_down=w_down)
    m = dict(norm_mix_pre=m_norm_mix_pre, norm_mix_post=m_norm_mix_post, w_in=m_w_in, b_forget=m_b_forget, w_pool=m_w_pool,
             pool_scale=m_pool_scale, w_mix_out=m_w_mix_out, norm_mem=m_norm_mem, norm_xa_pre=m_norm_xa_pre,
             norm_xa_post=m_norm_xa_post, w_xq=m_w_xq, w_xkv=m_w_xkv, w_xo=m_w_xo, norm_ffn_pre=m_norm_ffn_pre,
             norm_ffn_post=m_norm_ffn_post, w_up=m_w_up, conv_w=m_conv_w, conv_b=m_conv_b, w_down=m_w_down)
    v = dict(norm_mix_pre=v_norm_mix_pre, norm_mix_post=v_norm_mix_post, w_in=v_w_in, b_forget=v_b_forget, w_pool=v_w_pool,
             pool_scale=v_pool_scale, w_mix_out=v_w_mix_out, norm_mem=v_norm_mem, norm_xa_pre=v_norm_xa_pre,
             norm_xa_post=v_norm_xa_post, w_xq=v_w_xq, w_xkv=v_w_xkv, w_xo=v_w_xo, norm_ffn_pre=v_norm_ffn_pre,
             norm_ffn_post=v_norm_ffn_post, w_up=v_w_up, conv_w=v_conv_w, conv_b=v_conv_b, w_down=v_w_down)
    chip = 2 * lax.axis_index("x") + lax.axis_index("y")

    core_id = lax.axis_index("c").astype(jnp.int32).reshape(1)
    chip_id = chip.astype(jnp.int32).reshape(1)

    shard2d = {n: w[n][0] for n in BIG}
    p = _local_params(w)
    comm = _StepComm(w, shard2d, conv_w, core_id, chip_id)
    grad_x, g, loss_cols = _local_step(x[0], mem[0], loss_target[0], p, comm)

    reduced_late = [_mesh_sum("mesh_sum_" + n, chip_id, r, own)
                    for n, r, own in zip(comm.late, comm.received_late, comm.partial_late)]
    names = comm.late + comm.early
    reduced = reduced_late + comm.reduced
    reduced_sibling = list(_swap_reduced(reduced_late).run("swap_reduced_late")) + comm.reduced_sibling
    grads = {}

    gw_pool = jnp.stack([g["w_pool_full"][64 * gi:64 * (gi + 1), 64 * gi:64 * (gi + 1)] for gi in range(4)])
    dcwb = g["cwb"]
    g_conv_w = dcwb[:, 0:3, :].transpose(1, 0, 2).reshape(3, 2 * D_FF)
    g_conv_b = dcwb[:, 3, :].reshape(2 * D_FF)
    small_g = dict(norm_mix_pre=g["norm_mix_pre"], norm_mix_post=g["norm_mix_post"], b_forget=g["bf_pad"][:, :HEADS],
                   w_pool=gw_pool, pool_scale=g["pool_scale"], norm_mem=g["norm_mem"], norm_xa_pre=g["norm_xa_pre"],
                   norm_xa_post=g["norm_xa_post"], norm_ffn_pre=g["norm_ffn_pre"], norm_ffn_post=g["norm_ffn_post"],
                   conv_b=g_conv_b)
    local_buf, offs = _pack([small_g[n] for n in SMALL] + [g_conv_w, loss_cols])

    delta, new_m, new_v = {}, {}, {}
    for n, g_mine, g_sibling in zip(names, reduced, reduced_sibling):
        gn, d, nm, nv = _adamw_halves("adamw_" + n, core_id, shard2d[n], g_mine, g_sibling, m[n][0], v[n][0])
        grads[n], delta[n], new_m[n], new_v[n] = gn[None], d[None], nm[None], nv[None]
    place = (2 * chip + lax.axis_index("c")).astype(jnp.int32).reshape(1)
    buf = _sum_devices(place, _gather_small(local_buf).run("gather_small")[0], local_buf)
    for n, off in zip(SMALL, offs):
        grads[n] = _unpack(buf, off, w[n])
    g_conv_w = _unpack(buf, offs[len(SMALL)], g_conv_w)
    grads["conv_w"] = lax.dynamic_slice_in_dim(g_conv_w, chip * (2 * D_FF // N_CHIPS), 2 * D_FF // N_CHIPS, axis=1).reshape(conv_w.shape)
    loss = jnp.sum(_unpack(buf, offs[len(SMALL) + 1], loss_cols))
    small_names = SMALL + ("conv_w",)
    packed = [_pack([d[n] for n in small_names])[0] for d in (w, grads, m, v)]
    offs = _pack([w[n] for n in small_names])[1]
    d, nm, nv = _adamw("adamw_small", *packed)
    for n, off in zip(small_names, offs):
        delta[n], new_m[n], new_v[n] = _unpack(d, off, w[n]), _unpack(nm, off, w[n]), _unpack(nv, off, w[n])

    return (loss, grad_x[None], *[grads[n] for n in ORDER], *[delta[n] for n in ORDER], *[new_m[n] for n in ORDER],
            *[new_v[n] for n in ORDER])
```

```python
import functools

import jax
import jax.numpy as jnp
import numpy as np
from jax import lax
from jax.experimental import pallas as pl
from jax.experimental.pallas import tpu as pltpu

F32 = jnp.float32
BF16 = jnp.bfloat16
MESH = pl.DeviceIdType.MESH
ANY = pl.BlockSpec(memory_space=pl.ANY)
VMEM_SPEC = pl.BlockSpec(memory_space=pltpu.VMEM)

S = 4096
D = 1024
MEM = 256
D_POOL = 256
HEADS = 12
DH = 64
D_FOX = HEADS * DH
D_IN = D_POOL + 3 * D_FOX + HEADS
F_OFF = D_POOL + 3 * D_FOX
Q_OFF, K_OFF, V_OFF = D_POOL, D_POOL + D_FOX, D_POOL + 2 * D_FOX
XA_HEADS = 4
XA_DH = 256
D_FF = 4096
EPS = 1e-6
N_CHIPS = 4
ADAM_LR, ADAM_B1, ADAM_B2, ADAM_EPS, ADAM_WD, ADAM_STEP = 0.001, 0.9, 0.999, 1e-08, 0.01, 10

LANES = 128
SUBLANES = 8
D_IN_PAD = 21 * LANES
TR = 512
TILE_BYTES = 2 * 1024 * 1024
NEG = -1e30
VMEM_LIMIT = 52 * 1024 * 1024

NN = (((1,), (0,)), ((), ()))
NT = (((1,), (1,)), ((), ()))
TN = (((0,), (0,)), ((), ()))


def _dot(a, b, dims=NN):
    return lax.dot_general(a, b, dims, preferred_element_type=F32)


def _params(sem):
    return pltpu.CompilerParams(dimension_semantics=sem, vmem_limit_bytes=VMEM_LIMIT)


def _split3(x):
    hi = x.astype(BF16)
    r = x - hi.astype(F32)
    mid = r.astype(BF16)
    lo = (r - mid.astype(F32)).astype(BF16)
    return hi, mid, lo


def _split3_f32(x):
    hi = x.astype(BF16).astype(F32)
    r = x - hi
    mid = r.astype(BF16).astype(F32)
    return hi, mid, r - mid


def _lane_iota(shape):
    return lax.broadcasted_iota(jnp.int32, shape, len(shape) - 1)


def _row_iota(shape):
    return lax.broadcasted_iota(jnp.int32, shape, len(shape) - 2)


def _mm(name, a, b, a_spec, b_spec, out_shape, out_spec, grid, dims, acc_shape, ex=None):
    nk = grid[2]
    if ex is not None:
        return _mm_hosting(name, a, b, a_spec, b_spec, out_shape, out_spec, grid, dims, ex)

    def body(a_ref, b_ref, o_ref, *scr):
        p = _dot(a_ref[...], b_ref[...], dims)
        if nk == 1:
            o_ref[...] = p.astype(o_ref.dtype)
        else:
            acc = scr[0]
            k = pl.program_id(2)

            @pl.when(k == 0)
            def _():
                acc[...] = p

            @pl.when(k > 0)
            def _():
                acc[...] += p

            @pl.when(k == nk - 1)
            def _():
                o_ref[...] = acc[...].astype(o_ref.dtype)

    return pl.pallas_call(
        body, name=name, grid=grid, in_specs=[a_spec, b_spec], out_specs=out_spec, out_shape=out_shape,
        scratch_shapes=[pltpu.VMEM(acc_shape, F32)] if nk > 1 else [],
        compiler_params=_params(("parallel", "parallel", "arbitrary")),
    )(a, b)


def _mm_hosting(name, a, b, a_spec, b_spec, out_shape, out_spec, grid, dims, ex):
    assert grid[2] == 1
    n = len(ex.ins)

    def body(*refs):
        i, j = pl.program_id(0), pl.program_id(1)
        last = (i == grid[0] - 1) & (j == grid[1] - 1)
        (a_ref, b_ref), (o_ref,), _, begin, end = _hosted(ex, refs, 2, 1, (i == 0) & (j == 0), last, last)
        begin()
        o_ref[...] = _dot(a_ref[...], b_ref[...], dims).astype(o_ref.dtype)
        end()

    res = pl.pallas_call(
        body, name=name, grid=grid, in_specs=[a_spec, b_spec] + [ANY] * n, out_specs=[out_spec] + [ANY] * n,
        out_shape=[out_shape] + ex.out_shapes, scratch_shapes=ex.scratch(),
        compiler_params=_params(("arbitrary", "arbitrary", "arbitrary")),
    )(a, b, *ex.ins)
    return res[0], res[1:]


def _mm_nn(name, a, b, out_dtype, tm, tn):
    m, k = a.shape
    n = b.shape[1]
    return _mm(name, a, b, pl.BlockSpec((tm, k), lambda i, j, kk: (i, 0)), pl.BlockSpec((k, tn), lambda i, j, kk: (0, j)),
               jax.ShapeDtypeStruct((m, n), out_dtype), pl.BlockSpec((tm, tn), lambda i, j, kk: (i, j)),
               (m // tm, n // tn, 1), NN, (tm, tn))


def _mm_nt(name, a, b, out_dtype, tm, tn, ex=None):
    m, k = a.shape
    n = b.shape[0]
    return _mm(name, a, b, pl.BlockSpec((tm, k), lambda i, j, kk: (i, 0)), pl.BlockSpec((tn, k), lambda i, j, kk: (j, 0)),
               jax.ShapeDtypeStruct((m, n), out_dtype), pl.BlockSpec((tm, tn), lambda i, j, kk: (i, j)),
               (m // tm, n // tn, 1), NT, (tm, tn), ex)


def _mm_tn(name, a, b, tka, tn, ex=None):
    t, ka = a.shape
    n = b.shape[1]
    return _mm(name, a, b, pl.BlockSpec((t, tka), lambda i, j, kk: (0, i)), pl.BlockSpec((t, tn), lambda i, j, kk: (0, j)),
               jax.ShapeDtypeStruct((ka, n), F32), pl.BlockSpec((tka, tn), lambda i, j, kk: (i, j)),
               (ka // tka, n // tn, 1), TN, (tka, tn), ex)


def _d_h3(dhid, w_up, ex):
    tm = tn = 1024
    shard = 2 * D_FF // N_CHIPS
    per_plane = D_FF // shard
    grid = (S // tm, D // tn, N_CHIPS)
    n = len(ex.ins)

    def body(*refs):
        i, j, k = pl.program_id(0), pl.program_id(1), pl.program_id(2)
        first = (i == 0) & (j == 0) & (k == 0)
        last = (i == grid[0] - 1) & (j == grid[1] - 1) & (k == N_CHIPS - 1)
        (a_ref, b_ref), (o_ref,), (acc_ref,), begin, end = _hosted(ex, refs, 2, 1, first, first, last)
        begin()
        part = _dot(a_ref[...], b_ref[...], NT)

        @pl.when(k == 0)
        def _():
            acc_ref[...] = part

        @pl.when(k > 0)
        def _():
            acc_ref[...] += part

        @pl.when(k == N_CHIPS - 1)
        def _():
            o_ref[...] = acc_ref[...]

        end()

    res = pl.pallas_call(
        body, name="d_h3", grid=grid,
        in_specs=[pl.BlockSpec((None, tm, shard), lambda i, j, k: (k // per_plane, i, k % per_plane)),
                  pl.BlockSpec((None, tn, shard), lambda i, j, k: (k, j, 0))] + [ANY] * n,
        out_specs=[pl.BlockSpec((tm, tn), lambda i, j, k: (i, j))] + [ANY] * n,
        out_shape=[jax.ShapeDtypeStruct((S, D), F32)] + ex.out_shapes,
        scratch_shapes=[pltpu.VMEM((tm, tn), F32)] + ex.scratch(),
        compiler_params=_params(("arbitrary", "arbitrary", "arbitrary")),
    )(dhid, w_up, *ex.ins)
    return res[0], res[1:]


def _rms(x, g):
    r = lax.rsqrt(jnp.mean(x * x, axis=-1, keepdims=True) + EPS)
    return x * r * g


def _rms_bwd(x, g, dy):
    r = lax.rsqrt(jnp.mean(x * x, axis=-1, keepdims=True) + EPS)
    xh = x * r
    dxh = dy * g
    dx = r * (dxh - xh * jnp.mean(dxh * xh, axis=-1, keepdims=True))
    return dx, jnp.sum(dy * xh, axis=0, keepdims=True)


def _row_spec(tr, width):
    return pl.BlockSpec((tr, width), lambda i: (i, 0))


def _vec_spec(width):
    return pl.BlockSpec((1, width), lambda i: (0, 0))


def _norm_fwd(name, x, g, ex=None):
    rows, width = x.shape
    tr = min(TR, rows)
    steps = rows // tr
    hosted = ex if ex is not None else _no_exchange()
    n = len(hosted.ins)

    def body(*refs):
        i = pl.program_id(0)
        (x_ref, g_ref), (h_ref,), _, begin, end = _hosted(hosted, refs, 2, 1, i == 0, i == steps - 1, i == steps - 1)
        begin()
        h_ref[...] = _rms(x_ref[...], g_ref[...]).astype(BF16)
        end()

    res = pl.pallas_call(
        body, name=name, grid=(steps,), in_specs=[_row_spec(tr, width), _vec_spec(width)] + [ANY] * n,
        out_specs=[_row_spec(tr, width)] + [ANY] * n,
        out_shape=[jax.ShapeDtypeStruct((rows, width), BF16)] + hosted.out_shapes, scratch_shapes=hosted.scratch(),
        compiler_params=_params(("arbitrary",)),
    )(x, g, *hosted.ins)
    return res[0] if ex is None else (res[0], res[1:])


def _proj_resid_norm(name, a, w, xp, g_post, g_pre, w_next=None):
    def body(a_ref, w_ref, xp_ref, gpost_ref, gpre_ref, *rest):
        y_ref, xn_ref, h_ref = rest[-3:] if w_next is None else rest[1:4]
        y = _dot(a_ref[...], w_ref[...])
        y_ref[...] = y
        xn = xp_ref[...] + _rms(y, gpost_ref[...])
        xn_ref[...] = xn
        h = _rms(xn, gpre_ref[...]).astype(BF16)
        h_ref[...] = h
        if w_next is not None:
            rest[4][...] = _dot(h, rest[0][...]).astype(BF16)

    mat = pl.BlockSpec((D, D), lambda i: (0, 0))
    more = [] if w_next is None else [w_next]
    return pl.pallas_call(
        body, name=name, grid=(S // TR,),
        in_specs=[_row_spec(TR, D), mat, _row_spec(TR, D), _vec_spec(D), _vec_spec(D)] + [mat] * len(more),
        out_specs=[_row_spec(TR, D)] * (3 + len(more)),
        out_shape=[jax.ShapeDtypeStruct((S, D), F32), jax.ShapeDtypeStruct((S, D), F32), jax.ShapeDtypeStruct((S, D), BF16)]
        + [jax.ShapeDtypeStruct((S, D), BF16)] * len(more),
        compiler_params=_params(("parallel",)),
    )(a, w, xp, g_post, g_pre, *more)


def _down_loss_bwd(act, w_down, x3, g_post, target):
    def body(a_ref, w_ref, x_ref, g_ref, t_ref, dres_ref, dy_ref, dg_ref, loss_ref):
        i = pl.program_id(0)

        @pl.when(i == 0)
        def _():
            dg_ref[...] = jnp.zeros_like(dg_ref)
            loss_ref[...] = jnp.zeros_like(loss_ref)

        y = _dot(a_ref[...], w_ref[...])
        g = g_ref[...]
        e = x_ref[...] + _rms(y, g) - t_ref[...]
        loss_ref[...] += jnp.sum(e * e, axis=0, keepdims=True) * (0.5 / D)
        dres = e * (1.0 / D)
        dres_ref[...] = dres
        dy, dg = _rms_bwd(y, g, dres)
        dy_ref[...] = dy.astype(BF16)
        dg_ref[...] += dg

    return pl.pallas_call(
        body, name="down_loss_bwd", grid=(S // TR,),
        in_specs=[_row_spec(TR, D_FF), pl.BlockSpec((D_FF, D), lambda i: (0, 0)), _row_spec(TR, D), _vec_spec(D),
                  _row_spec(TR, D)],
        out_specs=[_row_spec(TR, D), _row_spec(TR, D), _vec_spec(D), _vec_spec(D)],
        out_shape=[jax.ShapeDtypeStruct((S, D), F32), jax.ShapeDtypeStruct((S, D), BF16),
                   jax.ShapeDtypeStruct((1, D), F32), jax.ShapeDtypeStruct((1, D), F32)],
        compiler_params=_params(("arbitrary",)),
    )(act, w_down, x3, g_post, target)


def _mid_bwd(name, dres, xcur, g_pre, dh, yprev, g_post, w):
    def body(dres_ref, x_ref, gpre_ref, dh_ref, y_ref, gpost_ref, w_ref, dx_ref, dy_ref, da_ref, dgpre_ref, dgpost_ref):
        i = pl.program_id(0)

        @pl.when(i == 0)
        def _():
            dgpre_ref[...] = jnp.zeros_like(dgpre_ref)
            dgpost_ref[...] = jnp.zeros_like(dgpost_ref)

        dxn, dgpre = _rms_bwd(x_ref[...], gpre_ref[...], dh_ref[...])
        dx = dres_ref[...] + dxn
        dx_ref[...] = dx
        dy, dgpost = _rms_bwd(y_ref[...], gpost_ref[...], dx)
        dy = dy.astype(BF16)
        dy_ref[...] = dy
        da_ref[...] = _dot(dy, w_ref[...], NT).astype(BF16)
        dgpre_ref[...] += dgpre
        dgpost_ref[...] += dgpost

    return pl.pallas_call(
        body, name=name, grid=(S // TR,),
        in_specs=[_row_spec(TR, D), _row_spec(TR, D), _vec_spec(D), _row_spec(TR, D), _row_spec(TR, D), _vec_spec(D),
                  pl.BlockSpec((D, D), lambda i: (0, 0))],
        out_specs=[_row_spec(TR, D), _row_spec(TR, D), _row_spec(TR, D), _vec_spec(D), _vec_spec(D)],
        out_shape=[jax.ShapeDtypeStruct((S, D), F32), jax.ShapeDtypeStruct((S, D), BF16), jax.ShapeDtypeStruct((S, D), BF16),
                   jax.ShapeDtypeStruct((1, D), F32), jax.ShapeDtypeStruct((1, D), F32)],
        compiler_params=_params(("arbitrary",)),
    )(dres, xcur, g_pre, dh, yprev, g_post, w)


def _first_bwd(dres, x, g, dh):
    def body(dres_ref, x_ref, g_ref, dh_ref, dx_ref, dg_ref):
        i = pl.program_id(0)

        @pl.when(i == 0)
        def _():
            dg_ref[...] = jnp.zeros_like(dg_ref)

        dxn, dg = _rms_bwd(x_ref[...], g_ref[...], dh_ref[...])
        dx_ref[...] = dres_ref[...] + dxn
        dg_ref[...] += dg

    return pl.pallas_call(
        body, name="first_bwd", grid=(S // TR,),
        in_specs=[_row_spec(TR, D), _row_spec(TR, D), _vec_spec(D), _row_spec(TR, D)],
        out_specs=[_row_spec(TR, D), _vec_spec(D)],
        out_shape=[jax.ShapeDtypeStruct((S, D), F32), jax.ShapeDtypeStruct((1, D), F32)],
        compiler_params=_params(("arbitrary",)),
    )(dres, x, g, dh)


def _gain_bwd(name, x, g, dy):
    rows, width = x.shape

    def body(x_ref, g_ref, dy_ref, dg_ref):
        _, dg = _rms_bwd(x_ref[...], g_ref[...], dy_ref[...])
        dg_ref[...] = dg

    return pl.pallas_call(
        body, name=name, grid=(1,), in_specs=[_row_spec(rows, width), _vec_spec(width), _row_spec(rows, width)],
        out_specs=_vec_spec(width), out_shape=jax.ShapeDtypeStruct((1, width), F32),
        compiler_params=_params(("arbitrary",)),
    )(x, g, dy)


CUM_Q = DH
CUM_K = DH + 3
LSE_Q = DH + 6
BOTH_ONE = DH + 9
DEN_V = DH
DELTA = DH + 1
PREP_TR = 256
PIECE_LANES = 16
FOX_FWD_BLOCK = 1024
FOX_BWD_BLOCK = 512


def _at(lane_of_even_head, h):
    return (lane_of_even_head + DH * (h % 2)) % LANES


def _data_lanes(lane, h):
    return lane >= DH if h % 2 else lane < DH


def _pair_block(ref, off, h):
    base = ((off + DH * h) // LANES) * LANES
    return ref[:, base:base + LANES]


def _cumsum_rows(x, tri, carry):
    hi, mid, lo = _split3(x)
    return _dot(tri, hi) + _dot(tri, mid) + _dot(tri, lo) + carry


def _in_proj(h1, w_in, bf_pad):
    tr = TR

    place_q = np.zeros((LANES, HEADS * LANES), np.float32)
    place_k = np.zeros((LANES, HEADS * LANES), np.float32)
    for h in range(HEADS):
        for piece in range(3):
            place_q[PIECE_LANES * piece + h, LANES * h + _at(CUM_Q, h) + piece] = 1.0
            place_k[PIECE_LANES * piece + h, LANES * h + _at(CUM_K, h) + piece] = -1.0

    def body(h_ref, w_ref, bf_ref, pq_ref, pk_ref, qa_ref, ka_ref, va_ref, u_ref, z_ref, carry_ref):
        i = pl.program_id(0)

        @pl.when(i == 0)
        def _():
            carry_ref[...] = jnp.zeros_like(carry_ref)

        proj = _dot(h_ref[...], w_ref[...])
        u_ref[...] = proj[:, :D_POOL]
        z_ref[...] = proj[:, F_OFF:F_OFF + LANES]
        lane = _lane_iota((tr, LANES))
        z = proj[:, F_OFF:F_OFF + LANES] + bf_ref[...]
        log_f = jnp.minimum(z, 0.0) - jnp.log(1.0 + jnp.exp(-jnp.abs(z)))
        log_f = jnp.where(lane < HEADS, log_f, 0.0)
        tri = jnp.where(_row_iota((tr, tr)) >= _lane_iota((tr, tr)), 1.0, 0.0).astype(BF16)
        cum = _cumsum_rows(log_f, tri, carry_ref[0:1, :])
        carry_ref[0:1, :] = cum[tr - 1:tr, :]
        c_hi, c_mid, c_lo = _split3_f32(cum)
        pieces = (c_hi + pltpu.roll(c_mid, PIECE_LANES, 1) + pltpu.roll(c_lo, 2 * PIECE_LANES, 1)).astype(BF16)
        cum_q = _dot(pieces, pq_ref[...])
        cum_k = _dot(pieces, pk_ref[...])

        def between(first, h):
            return (lane >= _at(first, h)) & (lane < _at(first, h) + 3)

        ones_q = [jnp.where(between(CUM_K, h) | (lane == _at(BOTH_ONE, h)), 1.0, 0.0) for h in range(2)]
        ones_k = [jnp.where(between(CUM_Q, h) | between(LSE_Q, h) | (lane == _at(BOTH_ONE, h)), 1.0, 0.0) for h in range(2)]
        aug_v = [jnp.where(lane == _at(DEN_V, h), 1.0, jnp.where(between(DELTA, h), -1.0, 0.0)) for h in range(2)]
        for h in range(HEADS):
            mine = slice(LANES * h, LANES * (h + 1))
            data = _data_lanes(lane, h)
            qa_ref[h] = jnp.where(data, _pair_block(proj, Q_OFF, h) * (DH ** -0.5), cum_q[:, mine] + ones_q[h % 2]).astype(BF16)
            ka_ref[h] = jnp.where(data, _pair_block(proj, K_OFF, h), cum_k[:, mine] + ones_k[h % 2]).astype(BF16)
            va_ref[h] = jnp.where(data, _pair_block(proj, V_OFF, h), aug_v[h % 2]).astype(BF16)

    head_spec = pl.BlockSpec((HEADS, tr, LANES), lambda i: (0, i, 0))
    head_shape = jax.ShapeDtypeStruct((HEADS, S, LANES), BF16)
    place_spec = pl.BlockSpec(place_q.shape, lambda i: (0, 0))
    return pl.pallas_call(
        body, name="in_proj", grid=(S // tr,),
        in_specs=[_row_spec(tr, D), pl.BlockSpec((D, D_IN_PAD), lambda i: (0, 0)), _vec_spec(LANES), place_spec, place_spec],
        out_specs=[head_spec] * 3 + [_row_spec(tr, D_POOL), _row_spec(tr, LANES)],
        out_shape=[head_shape] * 3 + [jax.ShapeDtypeStruct((S, D_POOL), F32), jax.ShapeDtypeStruct((S, LANES), F32)],
        scratch_shapes=[pltpu.VMEM((SUBLANES, LANES), F32)], compiler_params=_params(("arbitrary",)),
    )(h1, w_in, bf_pad, jnp.asarray(place_q, BF16), jnp.asarray(place_k, BF16))


def _hosted(ex, refs, n_blocked_in, n_blocked_out, first, forward_at, last):
    n = len(ex.ins)
    own_in = refs[:n_blocked_in]
    ex_in = refs[n_blocked_in:n_blocked_in + n]
    own_out = refs[n_blocked_in + n:n_blocked_in + n + n_blocked_out]
    ex_out = refs[n_blocked_in + n + n_blocked_out:n_blocked_in + 2 * n + n_blocked_out]
    rest = refs[n_blocked_in + 2 * n + n_blocked_out:]
    args = (ex_in, ex_out, rest[-2], rest[-1])

    def begin():
        @pl.when(first)
        def _():
            ex.start(*args)

        @pl.when(forward_at)
        def _():
            ex.forward(*args)

    def end():
        @pl.when(last)
        def _():
            ex.finish(*args)

    return own_in, own_out, rest[:-2], begin, end


def _fox_fwd(qa, ka, va, ex):
    BQ = BK = FOX_FWD_BLOCK
    nq = S // BQ
    n_pairs = HEADS // 2

    def body(*refs):
        p_id, i = pl.program_id(0), pl.program_id(1)
        (qa_ref, ka_ref, va_ref), (y_ref, qab_ref), (m_scr, acc_scr), begin, end = _hosted(
            ex, refs, 3, 2, (p_id == 0) & (i == 0), (p_id == n_pairs - 1) & (i == 0), (p_id == n_pairs - 1) & (i == nq - 1))
        begin()
        lane = _lane_iota((BQ, LANES))
        causal = _row_iota((BQ, BK)) >= _lane_iota((BQ, BK))
        m_scr[...] = jnp.full_like(m_scr, NEG)
        acc_scr[...] = jnp.zeros_like(acc_scr)

        def step(j, masked):
            rows = pl.ds(pl.multiple_of(j * BK, BK), BK)
            for hh in range(2):
                s = _dot(qa_ref[hh], ka_ref[hh, rows, :], NT)
                if masked:
                    s = jnp.where(causal, s, NEG)
                m_prev = m_scr[hh]
                m_new = jnp.maximum(m_prev, jnp.max(s, axis=1, keepdims=True))
                p = jnp.exp(s - jnp.tile(m_new, (1, BK // LANES)))
                acc_scr[hh] = jnp.exp(m_prev - m_new) * acc_scr[hh] + _dot(p.astype(BF16), va_ref[hh, rows, :])
                m_scr[hh] = m_new

        def full_step(j, carry):
            step(j, False)
            return carry

        lax.fori_loop(0, i, full_step, 0)
        step(i, True)
        outs = []
        for hh in range(2):
            acc = acc_scr[hh]
            den_lane, lse_lane = _at(DEN_V, hh), _at(LSE_Q, hh)
            den = jnp.broadcast_to(acc[:, den_lane:den_lane + 1], (BQ, LANES))
            outs.append(acc * (1.0 / den))
            n_hi, n_mid, n_lo = _split3(-(m_scr[hh] + jnp.log(den)))
            qab_ref[hh] = jnp.where(lane == lse_lane, n_hi,
                                    jnp.where(lane == lse_lane + 1, n_mid, jnp.where(lane == lse_lane + 2, n_lo, qa_ref[hh])))
        y_ref[...] = jnp.where(lane < DH, outs[0], outs[1]).astype(BF16)
        end()

    pair_rows = pl.BlockSpec((2, BQ, LANES), lambda p, i: (p, i, 0))
    pair_all = pl.BlockSpec((2, S, LANES), lambda p, i: (p, 0, 0))
    n = len(ex.ins)
    res = pl.pallas_call(
        body, name="fox_fwd", grid=(n_pairs, nq), in_specs=[pair_rows, pair_all, pair_all] + [ANY] * n,
        out_specs=[pl.BlockSpec((BQ, LANES), lambda p, i: (i, D_POOL // LANES + p)), pair_rows] + [ANY] * n,
        out_shape=[jax.ShapeDtypeStruct((S, D), BF16), jax.ShapeDtypeStruct((HEADS, S, LANES), BF16)] + ex.out_shapes,
        scratch_shapes=[pltpu.VMEM((2, BQ, LANES), F32), pltpu.VMEM((2, BQ, LANES), F32)] + ex.scratch(),
        compiler_params=_params(("arbitrary", "arbitrary")),
    )(qa, ka, va, *ex.ins)
    return res[0], res[1], res[2:]


def _bwd_xa_mix(dqx, w_xq, dres, x2, g_pre, y1, g_post, w_mix_out, ycat, ex):
    steps = S // TR
    n = len(ex.ins)

    def body(*refs):
        i = pl.program_id(0)
        ((dq_ref, wq_ref, dres_ref, x_ref, gpre_ref, y_ref, gpost_ref, wm_ref, ycat_ref),
         (dx_ref, dy_ref, dgpre_ref, dgpost_ref, dp_ref, doa_ref), _, begin, end) = _hosted(
            ex, refs, 9, 6, i == 0, i == 0, i == steps - 1)
        begin()

        @pl.when(i == 0)
        def _():
            dgpre_ref[...] = jnp.zeros_like(dgpre_ref)
            dgpost_ref[...] = jnp.zeros_like(dgpost_ref)

        dxn, dgpre = _rms_bwd(x_ref[...], gpre_ref[...], _dot(dq_ref[...], wq_ref[...], NT))
        dx = dres_ref[...] + dxn
        dx_ref[...] = dx
        dy, dgpost = _rms_bwd(y_ref[...], gpost_ref[...], dx)
        dy = dy.astype(BF16)
        dy_ref[...] = dy
        dgpre_ref[...] += dgpre
        dgpost_ref[...] += dgpost

        d = _dot(dy, wm_ref[...], NT)
        dp_ref[...] = d[:, :D_POOL]
        lane = _lane_iota((TR, LANES))
        low = lane < DH
        for p in range(HEADS // 2):
            cols = slice(D_POOL + LANES * p, D_POOL + LANES * (p + 1))
            do = d[:, cols]
            prod = do * ycat_ref[:, cols].astype(F32)
            deltas = (jnp.sum(jnp.where(low, prod, 0.0), axis=1, keepdims=True),
                      jnp.sum(jnp.where(low, 0.0, prod), axis=1, keepdims=True))
            for hh in range(2):
                d_hi, d_mid, d_lo = _split3_f32(deltas[hh])
                dl = _at(DELTA, hh)
                aug = jnp.where(lane == dl, d_hi, jnp.where(lane == dl + 1, d_mid, jnp.where(lane == dl + 2, d_lo, 0.0)))
                doa_ref[2 * p + hh] = jnp.where(_data_lanes(lane, hh), do, aug).astype(BF16)
        end()

    mat = pl.BlockSpec((D, D), lambda i: (0, 0))
    res = pl.pallas_call(
        body, name="bwd_xa_mix", grid=(steps,),
        in_specs=[_row_spec(TR, D), mat, _row_spec(TR, D), _row_spec(TR, D), _vec_spec(D), _row_spec(TR, D), _vec_spec(D), mat,
                  _row_spec(TR, D)] + [ANY] * n,
        out_specs=[_row_spec(TR, D), _row_spec(TR, D), _vec_spec(D), _vec_spec(D), _row_spec(TR, D_POOL),
                   pl.BlockSpec((HEADS, TR, LANES), lambda i: (0, i, 0))] + [ANY] * n,
        out_shape=[jax.ShapeDtypeStruct((S, D), F32), jax.ShapeDtypeStruct((S, D), BF16), jax.ShapeDtypeStruct((1, D), F32),
                   jax.ShapeDtypeStruct((1, D), F32), jax.ShapeDtypeStruct((S, D_POOL), F32),
                   jax.ShapeDtypeStruct((HEADS, S, LANES), BF16)] + ex.out_shapes,
        scratch_shapes=ex.scratch(), compiler_params=_params(("arbitrary",)),
    )(dqx, w_xq, dres, x2, g_pre, y1, g_post, w_mix_out, ycat, *ex.ins)
    return res[:6], res[6:]


def _fox_bwd(qab, doa, ka, va, ex):
    BQ = BK = FOX_BWD_BLOCK
    nk = S // BK
    n_pairs = HEADS // 2

    def body(*refs):
        p_id, j = pl.program_id(0), pl.program_id(1)
        (qab_ref, doa_ref, ka_ref, va_ref), (dqa_ref, dka_ref, dva_ref), _, begin, end = _hosted(
            ex, refs, 4, 3, (p_id == 0) & (j == 0), (p_id == n_pairs - 1) & (j == 0), (p_id == n_pairs - 1) & (j == nk - 1))
        begin()

        @pl.when(j == 0)
        def _():
            dqa_ref[...] = jnp.zeros_like(dqa_ref)

        causal = _row_iota((BQ, BK)) >= _lane_iota((BQ, BK))
        dka_ref[...] = jnp.zeros_like(dka_ref)
        dva_ref[...] = jnp.zeros_like(dva_ref)

        def step(i, masked):
            rows = pl.ds(pl.multiple_of(i * BQ, BQ), BQ)
            for hh in range(2):
                kb = ka_ref[hh]
                q = qab_ref[hh, rows, :]
                do = doa_ref[hh, rows, :]
                s = _dot(q, kb, NT)
                if masked:
                    s = jnp.where(causal, s, NEG)
                p = jnp.exp(s)
                ds = p * _dot(do, va_ref[hh], NT)
                pb = p.astype(BF16)
                dsb = ds.astype(BF16)
                dva_ref[hh] += _dot(pb, do, TN)
                dka_ref[hh] += _dot(dsb, q, TN)
                dqa_ref[hh, rows, :] += _dot(dsb, kb)

        def full_step(i, carry):
            step(i, False)
            return carry

        step(j, True)
        lax.fori_loop(j + 1, nk, full_step, 0)
        end()

    pair_all = pl.BlockSpec((2, S, LANES), lambda p, j: (p, 0, 0))
    pair_rows = pl.BlockSpec((2, BK, LANES), lambda p, j: (p, j, 0))
    shape = jax.ShapeDtypeStruct((HEADS, S, LANES), F32)
    n = len(ex.ins)
    res = pl.pallas_call(
        body, name="fox_bwd", grid=(n_pairs, nk), in_specs=[pair_all, pair_all, pair_rows, pair_rows] + [ANY] * n,
        out_specs=[pair_all, pair_rows, pair_rows] + [ANY] * n, out_shape=[shape] * 3 + ex.out_shapes,
        scratch_shapes=ex.scratch(), compiler_params=_params(("arbitrary", "arbitrary")),
    )(qab, doa, ka, va, *ex.ins)
    return res[0], res[1], res[2], res[3:]


def _fox_bwd_post(dqa, dka, dva, du, proj, bf_pad):
    tr = PREP_TR
    nt = S // tr

    pick = np.zeros((HEADS * LANES, LANES), np.float32)
    for h in range(HEADS):
        pick[LANES * h + _at(BOTH_ONE, h), h] = 1.0

    def body(dqa_ref, dka_ref, dva_ref, du_ref, z_ref, bf_ref, pick_ref, dp_ref, dbf_ref, carry_ref):
        i = pl.program_id(0)

        @pl.when(i == 0)
        def _():
            carry_ref[...] = jnp.zeros_like(carry_ref)
            dbf_ref[...] = jnp.zeros_like(dbf_ref)

        lane = _lane_iota((tr, LANES))
        diff = jnp.concatenate([dqa_ref[h] - dka_ref[h] for h in range(HEADS)], axis=1)
        hi = diff.astype(BF16)
        dcum = _dot(hi, pick_ref[...]) + _dot((diff - hi.astype(F32)).astype(BF16), pick_ref[...])
        tri =jnp.where(_lane_iota((tr, tr)) >= _row_iota((tr, tr)), 1.0, 0.0).astype(BF16)
        dlog_f = _cumsum_rows(dcum, tri, carry_ref[0:1, :])
        carry_ref[0:1, :] = dlog_f[0:1, :]
        z = z_ref[...] + bf_ref[...]
        df = jnp.where(lane < HEADS, dlog_f / (1.0 + jnp.exp(z)), 0.0)
        dbf_ref[...] += jnp.sum(df, axis=0, keepdims=True)

        dp_ref[:, 0:D_POOL] = du_ref[...].astype(BF16)
        low = lane < DH
        for ref, off, scale in ((dqa_ref, Q_OFF, DH ** -0.5), (dka_ref, K_OFF, 1.0), (dva_ref, V_OFF, 1.0)):
            for p in range(HEADS // 2):
                blk = jnp.where(low, ref[2 * p], ref[2 * p + 1])
                dp_ref[:, off + LANES * p:off + LANES * (p + 1)] = (blk * scale).astype(BF16)
        dp_ref[:, F_OFF:F_OFF + LANES] = df.astype(BF16)

    head_spec = pl.BlockSpec((HEADS, tr, LANES), lambda i: (0, nt - 1 - i, 0))
    return pl.pallas_call(
        body, name="fox_bwd_post", grid=(nt,),
        in_specs=[head_spec, head_spec, head_spec, pl.BlockSpec((tr, D_POOL), lambda i: (nt - 1 - i, 0)),
                  pl.BlockSpec((tr, LANES), lambda i: (nt - 1 - i, 0)), _vec_spec(LANES),
                  pl.BlockSpec(pick.shape, lambda i: (0, 0))],
        out_specs=[pl.BlockSpec((tr, D_IN_PAD), lambda i: (nt - 1 - i, 0)), _vec_spec(LANES)],
        out_shape=[jax.ShapeDtypeStruct((S, D_IN_PAD), BF16), jax.ShapeDtypeStruct((1, LANES), F32)],
        scratch_shapes=[pltpu.VMEM((SUBLANES, LANES), F32)],
        compiler_params=_params(("arbitrary",)),
    )(dqa, dka, dva, du, proj, bf_pad, jnp.asarray(pick, BF16))


POOL_HALO = 16


def _by_group(lane, a2, a4, a8, a16):
    return jnp.where(lane < 64, a2, jnp.where(lane < 128, a4, jnp.where(lane < 192, a8, a16)))


def _window_count(lane, t):
    return jnp.minimum(t + 1, _by_group(lane, 2, 4, 8, 16)).astype(F32)


def _pool_diff(u, halo, first, tile):
    n = TR + POOL_HALO
    ext = jnp.concatenate([jnp.where(first, 0.0, halo), u], axis=0)
    s2 = ext + pltpu.roll(ext, 1, 0)
    s4 = s2 + pltpu.roll(s2, 2, 0)
    s8 = s4 + pltpu.roll(s4, 4, 0)
    s16 = s8 + pltpu.roll(s8, 8, 0)
    lane = _lane_iota((n, D_POOL))
    win = _by_group(lane, s2, s4, s8, s16)[POOL_HALO:]
    lane = _lane_iota((TR, D_POOL))
    t = tile * TR + _row_iota((TR, D_POOL))
    return win / _window_count(lane, t) - u


def _prev_halo(rows, width, col):
    per = TR // rows
    return pl.BlockSpec((rows, width), lambda i: (jnp.maximum(i * per - 1, 0), col))


def _next_halo(rows, width, col):
    per = TR // rows
    return pl.BlockSpec((rows, width), lambda i: (jnp.minimum((i + 1) * per, S // rows - 1), col))


def _pool_fwd(proj, w_bd, ps, ycat):
    def body(u_ref, halo_ref, w_ref, ps_ref, ycat_ref, y_ref):
        i = pl.program_id(0)
        diff = _pool_diff(u_ref[...], halo_ref[...], i == 0, i)
        y_ref[...] = (_dot(diff.astype(BF16), w_ref[...]) * ps_ref[...]).astype(BF16)

    return pl.pallas_call(
        body, name="pool_fwd", grid=(S // TR,),
        in_specs=[_row_spec(TR, D_POOL), _prev_halo(POOL_HALO, D_POOL, 0),
                  pl.BlockSpec((D_POOL, D_POOL), lambda i: (0, 0)), _vec_spec(D_POOL), ANY],
        out_specs=_row_spec(TR, D_POOL), out_shape=jax.ShapeDtypeStruct((S, D), BF16), input_output_aliases={4: 0},
        compiler_params=_params(("parallel",)),
    )(proj, proj, w_bd, ps, ycat)


def _pool_bwd(proj, dycat, w_bd, w_bd_t, ps):
    nt = S // TR
    n = TR + POOL_HALO

    def body(u_ref, halo_ref, dy_ref, dyn_ref, w_ref, wt_ref, ps_ref, du_ref, dw_ref, dps_ref):
        i = pl.program_id(0)

        @pl.when(i == 0)
        def _():
            dw_ref[...] = jnp.zeros_like(dw_ref)
            dps_ref[...] = jnp.zeros_like(dps_ref)

        diff = _pool_diff(u_ref[...], halo_ref[...], i == 0, i).astype(BF16)
        dy = dy_ref[...]
        dps_ref[...] += jnp.sum(dy * _dot(diff, w_ref[...]), axis=0, keepdims=True)
        dy_ext = jnp.concatenate([dy, jnp.where(i == nt - 1, 0.0, dyn_ref[...])], axis=0)
        dmixed = (dy_ext * ps_ref[...]).astype(BF16)
        ddiff = _dot(dmixed, wt_ref[...])
        dw_ref[...] += _dot(diff, dmixed[:TR], TN)
        lane = _lane_iota((n, D_POOL))
        t = i * TR + _row_iota((n, D_POOL))
        e = ddiff / _window_count(lane, t)
        f2 = e + pltpu.roll(e, n - 1, 0)
        f4 = f2 + pltpu.roll(f2, n - 2, 0)
        f8 = f4 + pltpu.roll(f4, n - 4, 0)
        f16 = f8 + pltpu.roll(f8, n - 8, 0)
        du_ref[...] = _by_group(lane, f2, f4, f8, f16)[:TR] - ddiff[:TR]

    mat = pl.BlockSpec((D_POOL, D_POOL), lambda i: (0, 0))
    return pl.pallas_call(
        body, name="pool_bwd", grid=(nt,),
        in_specs=[_row_spec(TR, D_POOL), _prev_halo(POOL_HALO, D_POOL, 0), _row_spec(TR, D_POOL),
                  _next_halo(POOL_HALO, D_POOL, 0), mat, mat, _vec_spec(D_POOL)],
        out_specs=[_row_spec(TR, D_POOL), mat, _vec_spec(D_POOL)],
        out_shape=[jax.ShapeDtypeStruct((S, D_POOL), F32), jax.ShapeDtypeStruct((D_POOL, D_POOL), F32),
                   jax.ShapeDtypeStruct((1, D_POOL), F32)],
        compiler_params=_params(("arbitrary",)),
    )(proj, proj, dycat, dycat, w_bd, w_bd_t, ps)


def _xa_probs(q, k):
    s = _dot(q, k, NT) * (XA_DH ** -0.5)
    e = jnp.exp(s - jnp.max(s, axis=-1, keepdims=True))
    return e * (1.0 / jnp.sum(e, axis=-1, keepdims=True))


def _xattn_fwd(qx, kv):
    def body(q_ref, kv_ref, o_ref):
        for h in range(XA_HEADS):
            cols = slice(XA_DH * h, XA_DH * (h + 1))
            vcols = slice(D + XA_DH * h, D + XA_DH * (h + 1))
            p = _xa_probs(q_ref[:, cols], kv_ref[:, cols])
            o_ref[:, cols] = _dot(p.astype(BF16), kv_ref[:, vcols]).astype(BF16)

    return pl.pallas_call(
        body, name="xattn_fwd", grid=(S // TR,),
        in_specs=[_row_spec(TR, D), pl.BlockSpec((MEM, 2 * D), lambda i: (0, 0))],
        out_specs=_row_spec(TR, D), out_shape=jax.ShapeDtypeStruct((S, D), BF16),
        compiler_params=_params(("parallel",)),
    )(qx, kv)


def _xattn_bwd(qx, kv, dxo):
    def body(q_ref, kv_ref, do_ref, dq_ref, dkv_ref):
        i = pl.program_id(0)

        @pl.when(i == 0)
        def _():
            dkv_ref[...] = jnp.zeros_like(dkv_ref)

        for h in range(XA_HEADS):
            cols = slice(XA_DH * h, XA_DH * (h + 1))
            vcols = slice(D + XA_DH * h, D + XA_DH * (h + 1))
            q = q_ref[:, cols]
            k = kv_ref[:, cols]
            do = do_ref[:, cols]
            p = _xa_probs(q, k)
            dkv_ref[:, vcols] += _dot(p.astype(BF16), do, TN)
            dp = _dot(do, kv_ref[:, vcols], NT)
            ds = (p * (dp - jnp.sum(p * dp, axis=-1, keepdims=True)) * (XA_DH ** -0.5)).astype(BF16)
            dq_ref[:, cols] = _dot(ds, k).astype(BF16)
            dkv_ref[:, cols] += _dot(ds, q, TN)

    kv_spec = pl.BlockSpec((MEM, 2 * D), lambda i: (0, 0))
    return pl.pallas_call(
        body, name="xattn_bwd", grid=(S // TR,), in_specs=[_row_spec(TR, D), kv_spec, _row_spec(TR, D)],
        out_specs=[_row_spec(TR, D), kv_spec],
        out_shape=[jax.ShapeDtypeStruct((S, D), BF16), jax.ShapeDtypeStruct((MEM, 2 * D), F32)],
        compiler_params=_params(("arbitrary",)),
    )(qx, kv, dxo)


CONV_HALO = SUBLANES
TC = 512
TC_FWD = 1024
GELU_K = 0.7978845608028654
GELU_C = 0.044715


def _conv3(ext, w, rows):
    h0 = ext[CONV_HALO:CONV_HALO + rows]
    h1 = pltpu.roll(ext, 1, 0)[CONV_HALO:CONV_HALO + rows]
    h2 = pltpu.roll(ext, 2, 0)[CONV_HALO:CONV_HALO + rows]
    return w[2:3] * h0 + w[1:2] * h1 + w[0:1] * h2 + w[3:4], (h2, h1, h0)


def _conv_specs(tc):
    main = pl.BlockSpec((2, TR, tc), lambda j, i: (0, i, j))
    per = TR // CONV_HALO
    prev = pl.BlockSpec((2, CONV_HALO, tc), lambda j, i: (0, jnp.maximum(i * per - 1, 0), j))
    nxt = pl.BlockSpec((2, CONV_HALO, tc), lambda j, i: (0, jnp.minimum((i + 1) * per, S // CONV_HALO - 1), j))
    par = pl.BlockSpec((2, SUBLANES, tc), lambda j, i: (0, 0, j))
    return main, prev, nxt, par


def _convgate_fwd(hid, cwb):
    tc = TC_FWD

    def body(h_ref, hp_ref, w_ref, act_ref):
        i = pl.program_id(1)
        c = []
        for g in range(2):
            ext = jnp.concatenate([jnp.where(i == 0, 0.0, hp_ref[g]), h_ref[g]], axis=0)
            c.append(_conv3(ext, w_ref[g], TR)[0])
        gate, up = c
        act_ref[...] = (jax.nn.gelu(gate, approximate=True) * up).astype(BF16)

    main, prev, _, par = _conv_specs(tc)
    return pl.pallas_call(
        body, name="convgate_fwd", grid=(D_FF // tc, S // TR), in_specs=[main, prev, par],
        out_specs=pl.BlockSpec((TR, tc), lambda j, i: (i, j)), out_shape=jax.ShapeDtypeStruct((S, D_FF), BF16),
        compiler_params=_params(("parallel", "parallel")),
    )(hid, hid, cwb)


def _convgate_bwd(hid, dact, cwb):
    nr = S // TR
    n = TR + CONV_HALO

    def body(h_ref, hp_ref, hn_ref, da_ref, dan_ref, w_ref, dh_ref, dw_ref):
        i = pl.program_id(1)

        @pl.when(i == 0)
        def _():
            dw_ref[...] = jnp.zeros_like(dw_ref)

        da = jnp.concatenate([da_ref[...], jnp.where(i == nr - 1, 0.0, dan_ref[...])], axis=0)
        c, taps = [], []
        for g in range(2):
            ext = jnp.concatenate([jnp.where(i == 0, 0.0, hp_ref[g]), h_ref[g], hn_ref[g]], axis=0)
            cg, tg = _conv3(ext, w_ref[g], n)
            c.append(cg)
            taps.append(tg)
        gate, up = c
        th = jnp.tanh(GELU_K * (gate + GELU_C * gate * gate * gate))
        gelu = 0.5 * gate * (1.0 + th)
        dgelu = 0.5 * (1.0 + th) + 0.5 * gate * (1.0 - th * th) * GELU_K * (1.0 + 3.0 * GELU_C * gate * gate)
        for g, dc in enumerate((da * up * dgelu, da * gelu)):
            w = w_ref[g]
            dh = w[2:3] * dc[:TR] + w[1:2] * pltpu.roll(dc, n - 1, 0)[:TR] + w[0:1] * pltpu.roll(dc, n - 2, 0)[:TR]
            dh_ref[g] = dh.astype(BF16)
            dcm = dc[:TR]
            for r in range(3):
                dw_ref[g, r:r + 1, :] += jnp.sum(dcm * taps[g][r][:TR], axis=0, keepdims=True)
            dw_ref[g, 3:4, :] += jnp.sum(dcm, axis=0, keepdims=True)

    main, prev, nxt, par = _conv_specs(TC)
    per = TR // CONV_HALO
    return pl.pallas_call(
        body, name="convgate_bwd", grid=(D_FF // TC, nr),
        in_specs=[main, prev, nxt, pl.BlockSpec((TR, TC), lambda j, i: (i, j)),
                  pl.BlockSpec((CONV_HALO, TC), lambda j, i: (jnp.minimum((i + 1) * per, S // CONV_HALO - 1), j)), par],
        out_specs=[main, par],
        out_shape=[jax.ShapeDtypeStruct((2, S, D_FF), BF16), jax.ShapeDtypeStruct((2, SUBLANES, D_FF), F32)],
        compiler_params=_params(("parallel", "arbitrary")),
    )(hid, hid, hid, dact, dact, cwb)


def _adam_update(w, g, m, v):
    m = ADAM_B1 * m + (1.0 - ADAM_B1) * g
    v = ADAM_B2 * v + (1.0 - ADAM_B2) * (g * g)
    m_hat = m / (1.0 - ADAM_B1 ** ADAM_STEP)
    v_hat = v / (1.0 - ADAM_B2 ** ADAM_STEP)
    return -ADAM_LR * (m_hat / (jnp.sqrt(v_hat) + ADAM_EPS) + ADAM_WD * w), m, v


def _row_tile(rows, cols, itemsize=4, target=TILE_BYTES):
    tr = SUBLANES
    while rows % (2 * tr) == 0 and 2 * tr * cols * itemsize <= target:
        tr *= 2
    assert rows % tr == 0, (rows, tr)
    return tr


def _adamw(name, w, g, m, v):
    rows, cols = w.shape
    tr = rows if rows * cols * 4 <= TILE_BYTES // 2 else _row_tile(rows, cols, target=TILE_BYTES // 2)

    def body(w_ref, g_ref, m_ref, v_ref, d_ref, nm_ref, nv_ref):
        d_ref[...], nm_ref[...], nv_ref[...] = _adam_update(w_ref[...], g_ref[...], m_ref[...], v_ref[...])

    spec = _row_spec(tr, cols)
    shape = jax.ShapeDtypeStruct((rows, cols), F32)
    return pl.pallas_call(
        body, name=name, grid=(rows // tr,), in_specs=[spec] * 4, out_specs=[spec] * 3, out_shape=[shape] * 3,
        compiler_params=_params(("parallel",)),
    )(w, g, m, v)


def _adamw_halves(name, core, w, g_mine, g_sibling, m, v):
    rows, cols = w.shape
    half = rows // 2
    tr = _row_tile(half, cols, target=TILE_BYTES // 2)
    per = half // tr

    def body(core_ref, w_ref, gm_ref, gs_ref, m_ref, v_ref, g_ref, d_ref, nm_ref, nv_ref):
        g = jnp.where(pl.program_id(0) // per == core_ref[0], gm_ref[...], gs_ref[...])
        g_ref[...] = g
        d_ref[...], nm_ref[...], nv_ref[...] = _adam_update(w_ref[...], g, m_ref[...], v_ref[...])

    spec = pl.BlockSpec((tr, cols), lambda i, core_ref: (i, 0))
    half_spec = pl.BlockSpec((tr, cols), lambda i, core_ref: (i % per, 0))
    shape = jax.ShapeDtypeStruct((rows, cols), F32)
    return pl.pallas_call(
        body, name=name, out_shape=[shape] * 4,
        grid_spec=pltpu.PrefetchScalarGridSpec(
            num_scalar_prefetch=1, grid=(rows // tr,), in_specs=[spec, half_spec, half_spec, spec, spec], out_specs=[spec] * 4),
        compiler_params=_params(("parallel",)),
    )(core, w, g_mine, g_sibling, m, v)


def _adamw_halves_columns(name, core, w, g_mine, g_sibling, m, v):
    cols, _, rows = w.shape
    tl = 2 * LANES
    per = rows // 2 // tl

    def body(core_ref, w_ref, gm_ref, gs_ref, m_ref, v_ref, g_ref, d_ref, nm_ref, nv_ref):
        g = jnp.where(pl.program_id(0) // per == core_ref[0], gm_ref[...], gs_ref[...])
        g_ref[...] = g
        d_ref[...], nm_ref[...], nv_ref[...] = _adam_update(w_ref[...], g, m_ref[...], v_ref[...])

    spec = pl.BlockSpec((cols, 1, tl), lambda i, core_ref: (0, 0, i))
    half_spec = pl.BlockSpec((cols, 1, tl), lambda i, core_ref: (0, 0, i % per))
    shape = jax.ShapeDtypeStruct((cols, 1, rows), F32)
    return pl.pallas_call(
        body, name=name, out_shape=[shape] * 4,
        grid_spec=pltpu.PrefetchScalarGridSpec(
            num_scalar_prefetch=1, grid=(rows // tl,), in_specs=[spec, half_spec, half_spec, spec, spec], out_specs=[spec] * 4),
        compiler_params=_params(("parallel",)),
    )(core, w, g_mine, g_sibling, m, v)


def _chip_sum(name, core, g, other):
    _, _, half, cols = g.shape
    tr = _row_tile(half, cols)

    def body(core_ref, g_ref, o_ref, p_ref):
        p_ref[...] = (g_ref[...] + o_ref[...]).astype(BF16)

    spec = pl.BlockSpec((None, tr, cols), lambda j, i, core_ref: (j, i, 0))
    return pl.pallas_call(
        body, name=name, out_shape=jax.ShapeDtypeStruct((N_CHIPS, half, cols), BF16),
        grid_spec=pltpu.PrefetchScalarGridSpec(
            num_scalar_prefetch=1, grid=(N_CHIPS, half // tr),
            in_specs=[pl.BlockSpec((None, None, tr, cols), lambda j, i, core_ref: (j, core_ref[0], i, 0)), spec],
            out_specs=spec),
        compiler_params=_params(("parallel", "parallel")),
    )(core, g, other)


def _mesh_sum(name, chip, received, own):
    _, half, cols = received.shape
    tr = _row_tile(half, cols, itemsize=2 * N_CHIPS)

    def body(chip_ref, r_ref, own_ref, o_ref):
        acc = None
        for j in range(N_CHIPS):
            term = jnp.where(chip_ref[0] == j, own_ref[...], r_ref[j]).astype(F32)
            acc = term if acc is None else acc + term
        o_ref[...] = acc

    return pl.pallas_call(
        body, name=name, out_shape=jax.ShapeDtypeStruct((half, cols), F32),
        grid_spec=pltpu.PrefetchScalarGridSpec(
            num_scalar_prefetch=1, grid=(half // tr,),
            in_specs=[pl.BlockSpec((N_CHIPS, tr, cols), lambda i, chip_ref: (0, i, 0)),
                      pl.BlockSpec((None, tr, cols), lambda i, chip_ref: (chip_ref[0], i, 0))],
            out_specs=pl.BlockSpec((tr, cols), lambda i, chip_ref: (i, 0))),
        compiler_params=_params(("parallel",)),
    )(chip, received, own)


CHIP_FLIPS = ((1, 0), (0, 1), (1, 1))


def _place():
    x, y, c = lax.axis_index("x"), lax.axis_index("y"), lax.axis_index("c")
    return x, y, c, 2 * x + y


def _remote(src, dst, sems_s, sems_r, k, dev):
    return pltpu.make_async_remote_copy(src_ref=src, dst_ref=dst, send_sem=sems_s.at[k], recv_sem=sems_r.at[k],
                                        device_id=dev, device_id_type=MESH)


class _Exchange:
    def __init__(self, ins, out_shapes, n_sems, start, forward, finish):
        self.ins, self.out_shapes, self.n_sems = list(ins), list(out_shapes), n_sems
        self.start, self.forward, self.finish = start, forward, finish

    def scratch(self):
        return [pltpu.SemaphoreType.DMA((self.n_sems,)), pltpu.SemaphoreType.DMA((self.n_sems,))]

    def run(self, name):
        n = len(self.ins)

        def body(*refs):
            args = (refs[:n], refs[n:2 * n]) + tuple(refs[2 * n:])
            self.start(*args)
            self.forward(*args)
            self.finish(*args)

        return pl.pallas_call(
            body, name=name, in_specs=[ANY] * n, out_specs=[ANY] * n, out_shape=self.out_shapes, scratch_shapes=self.scratch(),
        )(*self.ins)


def _all_gather_weights(halved, whole):
    nh, nw = len(halved), len(whole)
    n_arr = nh + nw

    def copies(ins, outs, sems_s, sems_r):
        x, y, c, me = _place()
        sibling = (x, y, 1 - c)
        own = [_remote(ins[k], outs[k].at[me], sems_s, sems_r, k, sibling) for k in range(n_arr)]
        first, passed = [], []
        for k in range(n_arr):
            for f, (fx, fy) in enumerate(CHIP_FLIPS):
                src, dst = (ins[k].at[c], outs[k].at[me, c]) if k < nh else (ins[k], outs[k].at[me])
                first.append(_remote(src, dst, sems_s, sems_r, n_arr + 3 * k + f, (x ^ fx, y ^ fy, c)))
        for k in range(nh):
            for f, (fx, fy) in enumerate(CHIP_FLIPS):
                landed = outs[k].at[2 * (x ^ fx) + (y ^ fy), c]
                passed.append(_remote(landed, landed, sems_s, sems_r, 4 * n_arr + 3 * k + f, sibling))
        return own, first, passed

    def start(*refs):
        own, first, _ = copies(*refs)
        for cp in own + first:
            cp.start()

    def forward(*refs):
        _, first, passed = copies(*refs)
        for arrived, cp in zip(first, passed):
            arrived.wait_recv()
            cp.start()

    def finish(*refs):
        own, first, passed = copies(*refs)
        for cp in first[3 * nh:] + passed + own:
            cp.wait_recv()
        for cp in first + passed + own:
            cp.wait_send()

    shapes = [jax.ShapeDtypeStruct((N_CHIPS,) + a.shape, a.dtype) for a in list(halved) + list(whole)]
    return _Exchange(list(halved) + list(whole), shapes, 7 * nh + 4 * nw, start, forward, finish)


def _swap_halves(gs):
    n = len(gs)

    def copies(ins, outs, sems_s, sems_r):
        x, y, c, _ = _place()
        return [_remote(ins[k].at[:, 1 - c], outs[k], sems_s, sems_r, k, (x, y, 1 - c)) for k in range(n)]

    def start(*refs):
        for cp in copies(*refs):
            cp.start()

    def finish(*refs):
        for cp in copies(*refs):
            cp.wait()

    shapes = [jax.ShapeDtypeStruct((g.shape[0],) + g.shape[2:], g.dtype) for g in gs]
    return _Exchange(gs, shapes, n, start, _no_copies, finish)


def _scatter_chips(ps):
    n = len(ps)

    def copies(ins, outs, sems_s, sems_r):
        x, y, c, me = _place()
        return [_remote(ins[k].at[2 * (x ^ fx) + (y ^ fy)], outs[k].at[me], sems_s, sems_r, 3 * k + f, (x ^ fx, y ^ fy, c))
                for k in range(n) for f, (fx, fy) in enumerate(CHIP_FLIPS)]

    def start(*refs):
        for cp in copies(*refs):
            cp.start()

    def forward(*refs):
        pass

    def finish(*refs):
        for cp in copies(*refs):
            cp.wait()

    shapes = [jax.ShapeDtypeStruct(p.shape, p.dtype) for p in ps]
    return _Exchange(ps, shapes, 3 * n, start, forward, finish)


def _swap_reduced(rs):
    n = len(rs)

    def copies(ins, outs, sems_s, sems_r):
        x, y, c, _ = _place()
        return [_remote(ins[k], outs[k], sems_s, sems_r, k, (x, y, 1 - c)) for k in range(n)]

    def start(*refs):
        for cp in copies(*refs):
            cp.start()

    def finish(*refs):
        for cp in copies(*refs):
            cp.wait()

    return _Exchange(rs, [jax.ShapeDtypeStruct(r.shape, r.dtype) for r in rs], n, start, _no_copies, finish)


N_DEV = 8


def _gather_small(buf):
    def copies(ins, outs, sems_s, sems_r):
        x, y, c, _ = _place()
        me = 4 * x + 2 * y + c
        return [_remote(ins[0], outs[0].at[me], sems_s, sems_r, o - 1, (x ^ (o >> 2), y ^ ((o >> 1) & 1), c ^ (o & 1)))
                for o in range(1, N_DEV)]

    def start(*refs):
        for cp in copies(*refs):
            cp.start()

    def finish(*refs):
        for cp in copies(*refs):
            cp.wait()

    return _Exchange([buf], [jax.ShapeDtypeStruct((N_DEV,) + buf.shape, buf.dtype)], N_DEV - 1, start, _no_copies, finish)


def _sum_devices(place, gathered, own):
    rows = own.shape[0]

    def body(place_ref, g_ref, own_ref, o_ref):
        acc = None
        for d in range(N_DEV):
            term = jnp.where(place_ref[0] == d, own_ref[...], g_ref[d])
            acc = term if acc is None else acc + term
        o_ref[...] = acc

    return pl.pallas_call(
        body, name="sum_devices", out_shape=jax.ShapeDtypeStruct((rows, LANES), F32),
        grid_spec=pltpu.PrefetchScalarGridSpec(
            num_scalar_prefetch=1, grid=(1,),
            in_specs=[pl.BlockSpec((N_DEV, rows, LANES), lambda i, place_ref: (0, 0, 0)),
                      pl.BlockSpec((rows, LANES), lambda i, place_ref: (0, 0))],
            out_specs=pl.BlockSpec((rows, LANES), lambda i, place_ref: (0, 0))),
        compiler_params=_params(("arbitrary",)),
    )(place, gathered, own)


def _no_copies(*refs):
    pass


def _no_exchange():
    return _Exchange([], [], 1, _no_copies, _no_copies, _no_copies)


class _NoComm:
    def gather_first(self):
        return _no_exchange()

    def first_landed(self, p, landed):
        pass

    def gather_rest(self, p):
        return _no_exchange()

    def weights_landed(self, p, landed):
        pass

    def gather_last(self):
        return _no_exchange()

    def last_landed(self, p, landed):
        pass

    def swap_first(self, g):
        return _no_exchange()

    def first_swapped(self, landed):
        pass

    def swap_second(self, g):
        return _no_exchange()

    def second_swapped(self, landed):
        pass

    def scatter_early(self, g):
        return _no_exchange()

    def scatter_landed(self, landed):
        pass

    def swap_reduced_early(self):
        return _no_exchange()

    def reduced_landed(self, landed):
        pass

    def scatter_late(self, g):
        return _no_exchange()

    def late_landed(self, landed):
        pass


def _local_step(x, mem, target, p, comm):
    h1, landed = _norm_fwd("norm_mix_pre", x, p["norm_mix_pre"], comm.gather_first())
    comm.first_landed(p, landed)
    qa, ka, va, u, z = _in_proj(h1, p["w_in"], p["bf_pad"])
    ycat, qab, landed = _fox_fwd(qa, ka, va, comm.gather_rest(p))
    comm.weights_landed(p, landed)
    ycat = _pool_fwd(u, p["w_pool_bd"], p["pool_scale"], ycat)
    y1, x2, h2, qx = _proj_resid_norm("mix_out", ycat, p["w_mix_out"], x, p["norm_mix_post"], p["norm_xa_pre"], p["w_xq"])
    mem_n = _norm_fwd("norm_mem", mem, p["norm_mem"])
    kv = _mm(
        "xkv", mem_n, p["w_xkv"], pl.BlockSpec((MEM, D), lambda i, j, k: (0, 0)),
        pl.BlockSpec((None, D, 512), lambda i, j, k: (j, 0, 0)), jax.ShapeDtypeStruct((MEM, 2 * D), BF16),
        pl.BlockSpec((MEM, 512), lambda i, j, k: (0, j)), (1, N_CHIPS, 1), NN, (MEM, 512))
    xo = _xattn_fwd(qx, kv)
    y2, x3, h3 = _proj_resid_norm("xo", xo, p["w_xo"], x2, p["norm_xa_post"], p["norm_ffn_pre"])
    hid, landed = _mm(
        "up_proj", h3, p["w_up"], pl.BlockSpec((2048, D), lambda i, j, k: (i, 0)),
        pl.BlockSpec((None, D, 1024), lambda i, j, k: (j // 2, 0, j % 2)), jax.ShapeDtypeStruct((2, S, D_FF), F32),
        pl.BlockSpec((None, 2048, 1024), lambda i, j, k: (j // 4, i, j % 4)), (S // 2048, 8, 1), NN, (2048, 1024),
        comm.gather_last())
    comm.last_landed(p, landed)
    act = _convgate_fwd(hid, p["cwb"])

    g = {}
    dres, dy3, g["norm_ffn_post"], loss_cols = _down_loss_bwd(act, p["w_down"], x3, p["norm_ffn_post"], target)
    dact = _mm_nt("d_act", dy3, p["w_down"], F32, 2048, 1024)
    g["w_down"] = _mm_tn("dw_down", act, dy3, 1024, 512)
    dhid, dcwb = _convgate_bwd(hid, dact, p["cwb"])
    g["w_up"] = _mm(
        "dw_up", h3, dhid, pl.BlockSpec((S, D), lambda i, j, k: (0, 0)),
        pl.BlockSpec((None, S, 512), lambda i, j, k: (j // 8, 0, j % 8)), jax.ShapeDtypeStruct((N_CHIPS, D, 2048), F32),
        pl.BlockSpec((None, D, 512), lambda i, j, k: (j // 4, 0, j % 4)), (1, 16, 1), TN, (D, 512))
    dh3, landed = _d_h3(dhid, p["w_up"], comm.swap_first(g))
    comm.first_swapped(landed)
    dres, dy2, dxo, g["norm_ffn_pre"], g["norm_xa_post"] = _mid_bwd(
        "bwd_ffn_xa", dres, x3, p["norm_ffn_pre"], dh3, y2, p["norm_xa_post"], p["w_xo"])
    g["w_xo"] = _mm_tn("dw_xo", xo, dy2, 1024, 512)
    dqx, dkv = _xattn_bwd(qx, kv, dxo)
    dkv = dkv.astype(BF16)
    g["w_xq"] = _mm_tn("dw_xq", h2, dqx, 1024, 512)
    dmem_n = _mm(
        "d_mem", dkv, p["w_xkv"], pl.BlockSpec((MEM, 512), lambda i, j, k: (0, k)),
        pl.BlockSpec((None, D, 512), lambda i, j, k: (k, 0, 0)), jax.ShapeDtypeStruct((MEM, D), F32),
        pl.BlockSpec((MEM, D), lambda i, j, k: (0, 0)), (1, 1, N_CHIPS), NT, (MEM, D))
    g["w_xkv"] = _mm(
        "dw_xkv", mem_n, dkv, pl.BlockSpec((MEM, D), lambda i, j, k: (0, 0)),
        pl.BlockSpec((MEM, 512), lambda i, j, k: (0, j)), jax.ShapeDtypeStruct((N_CHIPS, D, 512), F32),
        pl.BlockSpec((None, D, 512), lambda i, j, k: (j, 0, 0)), (1, N_CHIPS, 1), TN, (D, 512))
    g["norm_mem"] = _gain_bwd("dg_mem", mem, p["norm_mem"], dmem_n)
    (dres, dy1, g["norm_xa_pre"], g["norm_mix_post"], dy_pool, doa), landed = _bwd_xa_mix(
        dqx, p["w_xq"], dres, x2, p["norm_xa_pre"], y1, p["norm_mix_post"], p["w_mix_out"], ycat, comm.swap_second(g))
    comm.second_swapped(landed)
    g["w_mix_out"] = _mm_tn("dw_mix_out", ycat, dy1, 1024, 512)
    dqa, dka, dva, landed = _fox_bwd(qab, doa, ka, va, comm.scatter_early(g))
    comm.scatter_landed(landed)
    du, g["w_pool_full"], g["pool_scale"] = _pool_bwd(u, dy_pool, p["w_pool_bd"], p["w_pool_bd_t"], p["pool_scale"])
    dproj, g["bf_pad"] = _fox_bwd_post(dqa, dka, dva, du, z, p["bf_pad"])
    g["w_in"], landed = _mm_tn("dw_in", h1, dproj, 1024, 896, comm.swap_reduced_early())
    comm.reduced_landed(landed)
    dh1, landed = _mm_nt("d_h1", dproj, p["w_in"], F32, 1024, 1024, comm.scatter_late(g))
    comm.late_landed(landed)
    grad_x, g["norm_mix_pre"] = _first_bwd(dres, x, p["norm_mix_pre"], dh1)
    g["cwb"] = dcwb
    return grad_x, g, loss_cols


BIG = ("w_in", "w_mix_out", "w_xq", "w_xkv", "w_xo", "w_up", "w_down")
ROW_SHARDED = ("w_mix_out", "w_xq", "w_xo", "w_down")
SMALL = ("norm_mix_pre", "norm_mix_post", "b_forget", "w_pool", "pool_scale", "norm_mem", "norm_xa_pre", "norm_xa_post",
         "norm_ffn_pre", "norm_ffn_post", "conv_b")
ORDER = ("norm_mix_pre", "norm_mix_post", "w_in", "b_forget", "w_pool", "pool_scale", "w_mix_out", "norm_mem", "norm_xa_pre",
         "norm_xa_post", "w_xq", "w_xkv", "w_xo", "norm_ffn_pre", "norm_ffn_post", "w_up", "conv_w", "conv_b", "w_down")
SLOT = SUBLANES * LANES


def _pack(parts):
    rows, offs, off = [], [], 0
    for a in parts:
        flat = a.reshape(-1).astype(F32)
        n = -(-flat.shape[0] // SLOT) * SLOT
        rows.append(jnp.pad(flat, (0, n - flat.shape[0])).reshape(n // LANES, LANES))
        offs.append(off)
        off += n // LANES
    return jnp.concatenate(rows, axis=0), offs


def _unpack(buf, off, like):
    n = like.size
    rows = -(-n // LANES)
    return buf[off:off + rows].reshape(-1)[:n].reshape(like.shape)


FIRST = ("w_in",)
REST = ("w_mix_out", "w_xq", "w_xkv", "w_xo", "w_up")
LAST = ("w_down",)


def _local_params(w):
    w_pool_bd = jnp.zeros((D_POOL, D_POOL), F32)
    for gi in range(4):
        w_pool_bd = w_pool_bd.at[64 * gi:64 * (gi + 1), 64 * gi:64 * (gi + 1)].set(w["w_pool"][0, gi])
    p = {n: w[n] for n in ("norm_mix_pre", "norm_mix_post", "norm_mem", "norm_xa_pre", "norm_xa_post", "norm_ffn_pre",
                           "norm_ffn_post")}
    p.update(
        bf_pad=jnp.pad(w["b_forget"], ((0, 0), (0, LANES - HEADS))),
        w_pool_bd=w_pool_bd.astype(BF16), w_pool_bd_t=w_pool_bd.T.astype(BF16), pool_scale=w["pool_scale"].reshape(1, D_POOL))
    return p


def _w_in_param(stacked):
    return jnp.pad(jnp.concatenate(list(stacked), axis=1), ((0, 0), (0, D_IN_PAD - D_IN)))


def _rest_params(w, full, conv_w_full):
    cw2 = conv_w_full.reshape(3, 2, D_FF).transpose(1, 0, 2)
    cwb = jnp.concatenate([cw2, w["conv_b"].reshape(1, 2, D_FF).transpose(1, 0, 2), jnp.zeros((2, 4, D_FF), F32)], axis=1)
    return dict(w_mix_out=full["w_mix_out"].reshape(D, D), w_xq=full["w_xq"].reshape(D, D), w_xkv=full["w_xkv"],
                w_xo=full["w_xo"].reshape(D, D), w_up=full["w_up"], cwb=cwb)


def _whole_params(w, full, conv_w_full):
    p = _local_params(w)
    p.update(_rest_params(w, full, conv_w_full), w_in=_w_in_param(full["w_in"]), w_down=full["w_down"].reshape(D_FF, D))
    return p


def _halved(a):
    return a.reshape(a.shape[:-2] + (2, a.shape[-2] // 2, a.shape[-1]))


class _StepComm:
    def __init__(self, w, shard2d, conv_w, core_id, chip_id):
        self.w, self.shard2d, self.conv_w, self.core_id, self.chip_id = w, shard2d, conv_w, core_id, chip_id
        self.first, self.second = ("w_up", "w_down"), ("w_xq", "w_xkv", "w_xo")
        self.early = self.first + self.second
        self.late = ("w_in", "w_mix_out")

    def gather_first(self):
        return _all_gather_weights([_halved(self.shard2d[n].astype(BF16)) for n in FIRST], [])

    def first_landed(self, p, landed):
        p["w_in"] = _w_in_param(landed[0].reshape((N_CHIPS,) + self.shard2d["w_in"].shape))

    def gather_rest(self, p):
        return _all_gather_weights([_halved(self.shard2d[n].astype(BF16)) for n in REST], [self.conv_w.reshape(3, -1)])

    def weights_landed(self, p, landed):
        full = {n: a.reshape((N_CHIPS,) + self.shard2d[n].shape) for n, a in zip(REST, landed)}
        conv_w_full = jnp.transpose(landed[-1], (1, 0, 2)).reshape(3, 2 * D_FF)
        p.update(_rest_params(self.w, full, conv_w_full))

    def gather_last(self):
        return _all_gather_weights([_halved(self.shard2d[n].astype(BF16)) for n in LAST], [])

    def last_landed(self, p, landed):
        p["w_down"] = landed[0].reshape(D_FF, D)

    def _view(self, g, n):
        return _halved(g[n].reshape((N_CHIPS,) + self.shard2d[n].shape))

    def swap_first(self, g):
        return _swap_halves([self._view(g, n) for n in self.first])

    def first_swapped(self, landed):
        self.from_sibling = dict(zip(self.first, landed))

    def swap_second(self, g):
        return _swap_halves([self._view(g, n) for n in self.second])

    def second_swapped(self, landed):
        self.from_sibling.update(zip(self.second, landed))

    def scatter_early(self, g):
        self.partial = [_chip_sum("chip_sum_" + n, self.core_id, self._view(g, n), self.from_sibling[n]) for n in self.early]
        return _scatter_chips(self.partial)

    def scatter_landed(self, landed):
        self.received = list(landed)

    def swap_reduced_early(self):
        self.reduced = [_mesh_sum("mesh_sum_" + n, self.chip_id, r, own)
                        for n, r, own in zip(self.early, self.received, self.partial)]
        return _swap_reduced(self.reduced)

    def reduced_landed(self, landed):
        self.reduced_sibling = list(landed)

    def scatter_late(self, g):
        gw_in = g["w_in"][:, :D_IN]
        cols = D_IN // N_CHIPS
        views = [_halved(jnp.stack([gw_in[:, cols * j:cols * (j + 1)] for j in range(N_CHIPS)])), self._view(g, "w_mix_out")]
        from_sibling = _swap_halves(views).run("swap_halves_late")
        self.partial_late = [_chip_sum("chip_sum_" + n, self.core_id, view, other)
                             for n, view, other in zip(self.late, views, from_sibling)]
        return _scatter_chips(self.partial_late)

    def late_landed(self, landed):
        self.received_late = list(landed)


def kernel(x, mem, norm_mix_pre, norm_mix_post, w_in, b_forget, w_pool, pool_scale, w_mix_out, norm_mem, norm_xa_pre, norm_xa_post, w_xq, w_xkv, w_xo, norm_ffn_pre, norm_ffn_post, w_up, conv_w, conv_b, w_down, loss_target, m_norm_mix_pre, m_norm_mix_post, m_w_in, m_b_forget, m_w_pool, m_pool_scale, m_w_mix_out, m_norm_mem, m_norm_xa_pre, m_norm_xa_post, m_w_xq, m_w_xkv, m_w_xo, m_norm_ffn_pre, m_norm_ffn_post, m_w_up, m_conv_w, m_conv_b, m_w_down, v_norm_mix_pre, v_norm_mix_post, v_w_in, v_b_forget, v_w_pool, v_pool_scale, v_w_mix_out, v_norm_mem, v_norm_xa_pre, v_norm_xa_post, v_w_xq, v_w_xkv, v_w_xo, v_norm_ffn_pre, v_norm_ffn_post, v_w_up, v_conv_w, v_conv_b, v_w_down):
    w = dict(norm_mix_pre=norm_mix_pre, norm_mix_post=norm_mix_post, w_in=w_in, b_forget=b_forget, w_pool=w_pool,
             pool_scale=pool_scale, w_mix_out=w_mix_out, norm_mem=norm_mem, norm_xa_pre=norm_xa_pre, norm_xa_post=norm_xa_post,
             w_xq=w_xq, w_xkv=w_xkv, w_xo=w_xo, norm_ffn_pre=norm_ffn_pre, norm_ffn_post=norm_ffn_post, w_up=w_up,
             conv_w=conv_w, conv_b=conv_b, w_down=w_down)
    m = dict(norm_mix_pre=m_norm_mix_pre, norm_mix_post=m_norm_mix_post, w_in=m_w_in, b_forget=m_b_forget, w_pool=m_w_pool,
             pool_scale=m_pool_scale, w_mix_out=m_w_mix_out, norm_mem=m_norm_mem, norm_xa_pre=m_norm_xa_pre,
             norm_xa_post=m_norm_xa_post, w_xq=m_w_xq, w_xkv=m_w_xkv, w_xo=m_w_xo, norm_ffn_pre=m_norm_ffn_pre,
             norm_ffn_post=m_norm_ffn_post, w_up=m_w_up, conv_w=m_conv_w, conv_b=m_conv_b, w_down=m_w_down)
    v = dict(norm_mix_pre=v_norm_mix_pre, norm_mix_post=v_norm_mix_post, w_in=v_w_in, b_forget=v_b_forget, w_pool=v_w_pool,
             pool_scale=v_pool_scale, w_mix_out=v_w_mix_out, norm_mem=v_norm_mem, norm_xa_pre=v_norm_xa_pre,
             norm_xa_post=v_norm_xa_post, w_xq=v_w_xq, w_xkv=v_w_xkv, w_xo=v_w_xo, norm_ffn_pre=v_norm_ffn_pre,
             norm_ffn_post=v_norm_ffn_post, w_up=v_w_up, conv_w=v_conv_w, conv_b=v_conv_b, w_down=v_w_down)
    chip = 2 * lax.axis_index("x") + lax.axis_index("y")

    core_id = lax.axis_index("c").astype(jnp.int32).reshape(1)
    chip_id = chip.astype(jnp.int32).reshape(1)

    shard2d = {n: w[n][0] for n in BIG}
    p = _local_params(w)
    comm = _StepComm(w, shard2d, conv_w, core_id, chip_id)
    grad_x, g, loss_cols = _local_step(x[0], mem[0], loss_target[0], p, comm)

    reduced_late = [_mesh_sum("mesh_sum_" + n, chip_id, r, own)
                    for n, r, own in zip(comm.late, comm.received_late, comm.partial_late)]
    names = comm.late + comm.early
    reduced = reduced_late + comm.reduced
    reduced_sibling = list(_swap_reduced(reduced_late).run("swap_reduced_late")) + comm.reduced_sibling
    grads = {}

    gw_pool = jnp.stack([g["w_pool_full"][64 * gi:64 * (gi + 1), 64 * gi:64 * (gi + 1)] for gi in range(4)])
    dcwb = g["cwb"]
    g_conv_w = dcwb[:, 0:3, :].transpose(1, 0, 2).reshape(3, 2 * D_FF)
    g_conv_b = dcwb[:, 3, :].reshape(2 * D_FF)
    small_g = dict(norm_mix_pre=g["norm_mix_pre"], norm_mix_post=g["norm_mix_post"], b_forget=g["bf_pad"][:, :HEADS],
                   w_pool=gw_pool, pool_scale=g["pool_scale"], norm_mem=g["norm_mem"], norm_xa_pre=g["norm_xa_pre"],
                   norm_xa_post=g["norm_xa_post"], norm_ffn_pre=g["norm_ffn_pre"], norm_ffn_post=g["norm_ffn_post"],
                   conv_b=g_conv_b)
    local_buf, offs = _pack([small_g[n] for n in SMALL] + [g_conv_w, loss_cols])

    delta, new_m, new_v = {}, {}, {}
    for n, g_mine, g_sibling in zip(names, reduced, reduced_sibling):
        if shard2d[n].shape[1] % LANES:
            outs = _adamw_halves_columns("adamw_" + n, core_id, jnp.transpose(w[n], (2, 0, 1)), g_mine.T[:, None, :],
                                         g_sibling.T[:, None, :], jnp.transpose(m[n], (2, 0, 1)), jnp.transpose(v[n], (2, 0, 1)))
            gn, d, nm, nv = (jnp.transpose(o, (1, 2, 0)) for o in outs)
        else:
            gn, d, nm, nv = (o[None] for o in _adamw_halves("adamw_" + n, core_id, shard2d[n], g_mine, g_sibling, m[n][0], v[n][0]))
        grads[n], delta[n], new_m[n], new_v[n] = gn, d, nm, nv
    place = (2 * chip + lax.axis_index("c")).astype(jnp.int32).reshape(1)
    buf = _sum_devices(place, _gather_small(local_buf).run("gather_small")[0], local_buf)
    for n, off in zip(SMALL, offs):
        grads[n] = _unpack(buf, off, w[n])
    g_conv_w = _unpack(buf, offs[len(SMALL)], g_conv_w)
    grads["conv_w"] = lax.dynamic_slice_in_dim(g_conv_w, chip * (2 * D_FF // N_CHIPS), 2 * D_FF // N_CHIPS, axis=1).reshape(conv_w.shape)
    loss = jnp.sum(_unpack(buf, offs[len(SMALL) + 1], loss_cols))
    small_names = SMALL + ("conv_w",)
    packed = [_pack([d[n] for n in small_names])[0] for d in (w, grads, m, v)]
    offs = _pack([w[n] for n in small_names])[1]
    d, nm, nv = _adamw("adamw_small", *packed)
    for n, off in zip(small_names, offs):
        delta[n], new_m[n], new_v[n] = _unpack(d, off, w[n]), _unpack(nm, off, w[n]), _unpack(nv, off, w[n])

    return (loss, grad_x[None], *[grads[n] for n in ORDER], *[delta[n] for n in ORDER], *[new_m[n] for n in ORDER],
            *[new_v[n] for n in ORDER])
```

```python
import functools

import jax
import jax.numpy as jnp
import numpy as np
from jax import lax
from jax.experimental import pallas as pl
from jax.experimental.pallas import tpu as pltpu

F32 = jnp.float32
BF16 = jnp.bfloat16
MESH = pl.DeviceIdType.MESH
ANY = pl.BlockSpec(memory_space=pl.ANY)
VMEM_SPEC = pl.BlockSpec(memory_space=pltpu.VMEM)

S = 4096
D = 1024
MEM = 256
D_POOL = 256
HEADS = 12
DH = 64
D_FOX = HEADS * DH
D_IN = D_POOL + 3 * D_FOX + HEADS
F_OFF = D_POOL + 3 * D_FOX
Q_OFF, K_OFF, V_OFF = D_POOL, D_POOL + D_FOX, D_POOL + 2 * D_FOX
XA_HEADS = 4
XA_DH = 256
D_FF = 4096
EPS = 1e-6
N_CHIPS = 4
ADAM_LR, ADAM_B1, ADAM_B2, ADAM_EPS, ADAM_WD, ADAM_STEP = 0.001, 0.9, 0.999, 1e-08, 0.01, 10

LANES = 128
SUBLANES = 8
D_IN_PAD = 21 * LANES
TR = 512
TILE_BYTES = 2 * 1024 * 1024
NEG = -1e30
VMEM_LIMIT = 52 * 1024 * 1024

NN = (((1,), (0,)), ((), ()))
NT = (((1,), (1,)), ((), ()))
TN = (((0,), (0,)), ((), ()))


def _dot(a, b, dims=NN):
    return lax.dot_general(a, b, dims, preferred_element_type=F32)


def _params(sem):
    return pltpu.CompilerParams(dimension_semantics=sem, vmem_limit_bytes=VMEM_LIMIT)


def _split3(x):
    hi = x.astype(BF16)
    r = x - hi.astype(F32)
    mid = r.astype(BF16)
    lo = (r - mid.astype(F32)).astype(BF16)
    return hi, mid, lo


def _split3_f32(x):
    hi = x.astype(BF16).astype(F32)
    r = x - hi
    mid = r.astype(BF16).astype(F32)
    return hi, mid, r - mid


def _lane_iota(shape):
    return lax.broadcasted_iota(jnp.int32, shape, len(shape) - 1)


def _row_iota(shape):
    return lax.broadcasted_iota(jnp.int32, shape, len(shape) - 2)


def _mm(name, a, b, a_spec, b_spec, out_shape, out_spec, grid, dims, acc_shape, ex=None):
    nk = grid[2]
    if ex is not None:
        return _mm_hosting(name, a, b, a_spec, b_spec, out_shape, out_spec, grid, dims, ex)

    def body(a_ref, b_ref, o_ref, *scr):
        p = _dot(a_ref[...], b_ref[...], dims)
        if nk == 1:
            o_ref[...] = p.astype(o_ref.dtype)
        else:
            acc = scr[0]
            k = pl.program_id(2)

            @pl.when(k == 0)
            def _():
                acc[...] = p

            @pl.when(k > 0)
            def _():
                acc[...] += p

            @pl.when(k == nk - 1)
            def _():
                o_ref[...] = acc[...].astype(o_ref.dtype)

    return pl.pallas_call(
        body, name=name, grid=grid, in_specs=[a_spec, b_spec], out_specs=out_spec, out_shape=out_shape,
        scratch_shapes=[pltpu.VMEM(acc_shape, F32)] if nk > 1 else [],
        compiler_params=_params(("parallel", "parallel", "arbitrary")),
    )(a, b)


def _mm_hosting(name, a, b, a_spec, b_spec, out_shape, out_spec, grid, dims, ex):
    assert grid[2] == 1
    n = len(ex.ins)

    def body(*refs):
        i, j = pl.program_id(0), pl.program_id(1)
        last = (i == grid[0] - 1) & (j == grid[1] - 1)
        (a_ref, b_ref), (o_ref,), _, begin, end = _hosted(ex, refs, 2, 1, (i == 0) & (j == 0), last, last)
        begin()
        o_ref[...] = _dot(a_ref[...], b_ref[...], dims).astype(o_ref.dtype)
        end()

    res = pl.pallas_call(
        body, name=name, grid=grid, in_specs=[a_spec, b_spec] + [ANY] * n, out_specs=[out_spec] + [ANY] * n,
        out_shape=[out_shape] + ex.out_shapes, scratch_shapes=ex.scratch(),
        compiler_params=_params(("arbitrary", "arbitrary", "arbitrary")),
    )(a, b, *ex.ins)
    return res[0], res[1:]


def _mm_nn(name, a, b, out_dtype, tm, tn):
    m, k = a.shape
    n = b.shape[1]
    return _mm(name, a, b, pl.BlockSpec((tm, k), lambda i, j, kk: (i, 0)), pl.BlockSpec((k, tn), lambda i, j, kk: (0, j)),
               jax.ShapeDtypeStruct((m, n), out_dtype), pl.BlockSpec((tm, tn), lambda i, j, kk: (i, j)),
               (m // tm, n // tn, 1), NN, (tm, tn))


def _mm_nt(name, a, b, out_dtype, tm, tn, ex=None):
    m, k = a.shape
    n = b.shape[0]
    return _mm(name, a, b, pl.BlockSpec((tm, k), lambda i, j, kk: (i, 0)), pl.BlockSpec((tn, k), lambda i, j, kk: (j, 0)),
               jax.ShapeDtypeStruct((m, n), out_dtype), pl.BlockSpec((tm, tn), lambda i, j, kk: (i, j)),
               (m // tm, n // tn, 1), NT, (tm, tn), ex)


def _mm_tn(name, a, b, tka, tn, ex=None):
    t, ka = a.shape
    n = b.shape[1]
    return _mm(name, a, b, pl.BlockSpec((t, tka), lambda i, j, kk: (0, i)), pl.BlockSpec((t, tn), lambda i, j, kk: (0, j)),
               jax.ShapeDtypeStruct((ka, n), F32), pl.BlockSpec((tka, tn), lambda i, j, kk: (i, j)),
               (ka // tka, n // tn, 1), TN, (tka, tn), ex)


def _d_h3(dhid, w_up, ex):
    tm = tn = 1024
    shard = 2 * D_FF // N_CHIPS
    per_plane = D_FF // shard
    grid = (S // tm, D // tn, N_CHIPS)
    n = len(ex.ins)

    def body(*refs):
        i, j, k = pl.program_id(0), pl.program_id(1), pl.program_id(2)
        first = (i == 0) & (j == 0) & (k == 0)
        last = (i == grid[0] - 1) & (j == grid[1] - 1) & (k == N_CHIPS - 1)
        (a_ref, b_ref), (o_ref,), (acc_ref,), begin, end = _hosted(ex, refs, 2, 1, first, first, last)
        begin()
        part = _dot(a_ref[...], b_ref[...], NT)

        @pl.when(k == 0)
        def _():
            acc_ref[...] = part

        @pl.when(k > 0)
        def _():
            acc_ref[...] += part

        @pl.when(k == N_CHIPS - 1)
        def _():
            o_ref[...] = acc_ref[...]

        end()

    res = pl.pallas_call(
        body, name="d_h3", grid=grid,
        in_specs=[pl.BlockSpec((None, tm, shard), lambda i, j, k: (k // per_plane, i, k % per_plane)),
                  pl.BlockSpec((None, tn, shard), lambda i, j, k: (k, j, 0))] + [ANY] * n,
        out_specs=[pl.BlockSpec((tm, tn), lambda i, j, k: (i, j))] + [ANY] * n,
        out_shape=[jax.ShapeDtypeStruct((S, D), F32)] + ex.out_shapes,
        scratch_shapes=[pltpu.VMEM((tm, tn), F32)] + ex.scratch(),
        compiler_params=_params(("arbitrary", "arbitrary", "arbitrary")),
    )(dhid, w_up, *ex.ins)
    return res[0], res[1:]


SHARD_IN = D_IN // N_CHIPS
SHARD_IN_PAD = -(-SHARD_IN // SUBLANES) * SUBLANES


def _dw_in(dproj, h1, ex):
    tk = 1024
    nk = S // tk
    half = D // 2
    starts = [SHARD_IN * j // LANES * LANES for j in range(N_CHIPS)]
    shifts = [SHARD_IN * j - s for j, s in enumerate(starts)]
    window = -(-(max(shifts) + SHARD_IN) // LANES) * LANES
    assert starts[-1] + window <= dproj.shape[1]
    n = len(ex.ins)

    def body(*refs):
        k = pl.program_id(0)
        (a_ref, b_ref), (o_ref,), _, begin, end = _hosted(ex, refs, 2, 1, k == 0, k == nk - 1, k == nk - 1)
        begin()

        @pl.when(k == 0)
        def _():
            o_ref[...] = jnp.zeros(o_ref.shape, F32)

        for j in range(N_CHIPS):
            win = a_ref[:, starts[j]:starts[j] + window]
            if shifts[j]:
                win = pltpu.roll(win, window - shifts[j], axis=1)
            part = _dot(win, b_ref[...], TN)
            for h in range(2):
                o_ref[j, h] += part[:SHARD_IN_PAD, h * half:(h + 1) * half]
        end()

    out_shape = (N_CHIPS, 2, SHARD_IN_PAD, half)
    res = pl.pallas_call(
        body, name="dw_in", grid=(nk,),
        in_specs=[pl.BlockSpec((tk, dproj.shape[1]), lambda k: (k, 0)), pl.BlockSpec((tk, D), lambda k: (k, 0))] + [ANY] * n,
        out_specs=[pl.BlockSpec(out_shape, lambda k: (0, 0, 0, 0))] + [ANY] * n,
        out_shape=[jax.ShapeDtypeStruct(out_shape, F32)] + ex.out_shapes,
        scratch_shapes=ex.scratch(),
        compiler_params=_params(("arbitrary",)),
    )(dproj, h1, *ex.ins)
    return res[0], res[1:]


def _rms(x, g):
    r = lax.rsqrt(jnp.mean(x * x, axis=-1, keepdims=True) + EPS)
    return x * r * g


def _rms_bwd(x, g, dy):
    r = lax.rsqrt(jnp.mean(x * x, axis=-1, keepdims=True) + EPS)
    xh = x * r
    dxh = dy * g
    dx = r * (dxh - xh * jnp.mean(dxh * xh, axis=-1, keepdims=True))
    return dx, jnp.sum(dy * xh, axis=0, keepdims=True)


def _row_spec(tr, width):
    return pl.BlockSpec((tr, width), lambda i: (i, 0))


def _vec_spec(width):
    return pl.BlockSpec((1, width), lambda i: (0, 0))


def _norm_fwd(name, x, g, ex=None):
    rows, width = x.shape
    tr = min(TR, rows)
    steps = rows // tr
    hosted = ex if ex is not None else _no_exchange()
    n = len(hosted.ins)

    def body(*refs):
        i = pl.program_id(0)
        (x_ref, g_ref), (h_ref,), _, begin, end = _hosted(hosted, refs, 2, 1, i == 0, i == steps - 1, i == steps - 1)
        begin()
        h_ref[...] = _rms(x_ref[...], g_ref[...]).astype(BF16)
        end()

    res = pl.pallas_call(
        body, name=name, grid=(steps,), in_specs=[_row_spec(tr, width), _vec_spec(width)] + [ANY] * n,
        out_specs=[_row_spec(tr, width)] + [ANY] * n,
        out_shape=[jax.ShapeDtypeStruct((rows, width), BF16)] + hosted.out_shapes, scratch_shapes=hosted.scratch(),
        compiler_params=_params(("arbitrary",)),
    )(x, g, *hosted.ins)
    return res[0] if ex is None else (res[0], res[1:])


def _proj_resid_norm(name, a, w, xp, g_post, g_pre, w_next=None):
    def body(a_ref, w_ref, xp_ref, gpost_ref, gpre_ref, *rest):
        y_ref, xn_ref, h_ref = rest[-3:] if w_next is None else rest[1:4]
        y = _dot(a_ref[...], w_ref[...])
        y_ref[...] = y
        xn = xp_ref[...] + _rms(y, gpost_ref[...])
        xn_ref[...] = xn
        h = _rms(xn, gpre_ref[...]).astype(BF16)
        h_ref[...] = h
        if w_next is not None:
            rest[4][...] = _dot(h, rest[0][...]).astype(BF16)

    mat = pl.BlockSpec((D, D), lambda i: (0, 0))
    more = [] if w_next is None else [w_next]
    return pl.pallas_call(
        body, name=name, grid=(S // TR,),
        in_specs=[_row_spec(TR, D), mat, _row_spec(TR, D), _vec_spec(D), _vec_spec(D)] + [mat] * len(more),
        out_specs=[_row_spec(TR, D)] * (3 + len(more)),
        out_shape=[jax.ShapeDtypeStruct((S, D), F32), jax.ShapeDtypeStruct((S, D), F32), jax.ShapeDtypeStruct((S, D), BF16)]
        + [jax.ShapeDtypeStruct((S, D), BF16)] * len(more),
        compiler_params=_params(("parallel",)),
    )(a, w, xp, g_post, g_pre, *more)


def _down_loss_bwd(act, w_down, x3, g_post, target):
    def body(a_ref, w_ref, x_ref, g_ref, t_ref, dres_ref, dy_ref, dg_ref, loss_ref):
        i = pl.program_id(0)

        @pl.when(i == 0)
        def _():
            dg_ref[...] = jnp.zeros_like(dg_ref)
            loss_ref[...] = jnp.zeros_like(loss_ref)

        y = _dot(a_ref[...], w_ref[...])
        g = g_ref[...]
        e = x_ref[...] + _rms(y, g) - t_ref[...]
        loss_ref[...] += jnp.sum(e * e, axis=0, keepdims=True) * (0.5 / D)
        dres = e * (1.0 / D)
        dres_ref[...] = dres
        dy, dg = _rms_bwd(y, g, dres)
        dy_ref[...] = dy.astype(BF16)
        dg_ref[...] += dg

    return pl.pallas_call(
        body, name="down_loss_bwd", grid=(S // TR,),
        in_specs=[_row_spec(TR, D_FF), pl.BlockSpec((D_FF, D), lambda i: (0, 0)), _row_spec(TR, D), _vec_spec(D),
                  _row_spec(TR, D)],
        out_specs=[_row_spec(TR, D), _row_spec(TR, D), _vec_spec(D), _vec_spec(D)],
        out_shape=[jax.ShapeDtypeStruct((S, D), F32), jax.ShapeDtypeStruct((S, D), BF16),
                   jax.ShapeDtypeStruct((1, D), F32), jax.ShapeDtypeStruct((1, D), F32)],
        compiler_params=_params(("arbitrary",)),
    )(act, w_down, x3, g_post, target)


def _mid_bwd(name, dres, xcur, g_pre, dh, yprev, g_post, w):
    def body(dres_ref, x_ref, gpre_ref, dh_ref, y_ref, gpost_ref, w_ref, dx_ref, dy_ref, da_ref, dgpre_ref, dgpost_ref):
        i = pl.program_id(0)

        @pl.when(i == 0)
        def _():
            dgpre_ref[...] = jnp.zeros_like(dgpre_ref)
            dgpost_ref[...] = jnp.zeros_like(dgpost_ref)

        dxn, dgpre = _rms_bwd(x_ref[...], gpre_ref[...], dh_ref[...])
        dx = dres_ref[...] + dxn
        dx_ref[...] = dx
        dy, dgpost = _rms_bwd(y_ref[...], gpost_ref[...], dx)
        dy = dy.astype(BF16)
        dy_ref[...] = dy
        da_ref[...] = _dot(dy, w_ref[...], NT).astype(BF16)
        dgpre_ref[...] += dgpre
        dgpost_ref[...] += dgpost

    return pl.pallas_call(
        body, name=name, grid=(S // TR,),
        in_specs=[_row_spec(TR, D), _row_spec(TR, D), _vec_spec(D), _row_spec(TR, D), _row_spec(TR, D), _vec_spec(D),
                  pl.BlockSpec((D, D), lambda i: (0, 0))],
        out_specs=[_row_spec(TR, D), _row_spec(TR, D), _row_spec(TR, D), _vec_spec(D), _vec_spec(D)],
        out_shape=[jax.ShapeDtypeStruct((S, D), F32), jax.ShapeDtypeStruct((S, D), BF16), jax.ShapeDtypeStruct((S, D), BF16),
                   jax.ShapeDtypeStruct((1, D), F32), jax.ShapeDtypeStruct((1, D), F32)],
        compiler_params=_params(("arbitrary",)),
    )(dres, xcur, g_pre, dh, yprev, g_post, w)


def _first_bwd(dres, x, g, dh):
    def body(dres_ref, x_ref, g_ref, dh_ref, dx_ref, dg_ref):
        i = pl.program_id(0)

        @pl.when(i == 0)
        def _():
            dg_ref[...] = jnp.zeros_like(dg_ref)

        dxn, dg = _rms_bwd(x_ref[...], g_ref[...], dh_ref[...])
        dx_ref[...] = dres_ref[...] + dxn
        dg_ref[...] += dg

    return pl.pallas_call(
        body, name="first_bwd", grid=(S // TR,),
        in_specs=[_row_spec(TR, D), _row_spec(TR, D), _vec_spec(D), _row_spec(TR, D)],
        out_specs=[_row_spec(TR, D), _vec_spec(D)],
        out_shape=[jax.ShapeDtypeStruct((S, D), F32), jax.ShapeDtypeStruct((1, D), F32)],
        compiler_params=_params(("arbitrary",)),
    )(dres, x, g, dh)


def _gain_bwd(name, x, g, dy):
    rows, width = x.shape

    def body(x_ref, g_ref, dy_ref, dg_ref):
        _, dg = _rms_bwd(x_ref[...], g_ref[...], dy_ref[...])
        dg_ref[...] = dg

    return pl.pallas_call(
        body, name=name, grid=(1,), in_specs=[_row_spec(rows, width), _vec_spec(width), _row_spec(rows, width)],
        out_specs=_vec_spec(width), out_shape=jax.ShapeDtypeStruct((1, width), F32),
        compiler_params=_params(("arbitrary",)),
    )(x, g, dy)


CUM_Q = DH
CUM_K = DH + 3
LSE_Q = DH + 6
BOTH_ONE = DH + 9
DEN_V = DH
DELTA = DH + 1
PREP_TR = 256
PIECE_LANES = 16
FOX_FWD_BLOCK = 1024
FOX_BWD_BLOCK = 512


def _at(lane_of_even_head, h):
    return (lane_of_even_head + DH * (h % 2)) % LANES


def _data_lanes(lane, h):
    return lane >= DH if h % 2 else lane < DH


def _pair_block(ref, off, h):
    base = ((off + DH * h) // LANES) * LANES
    return ref[:, base:base + LANES]


def _cumsum_rows(x, tri, carry):
    hi, mid, lo = _split3(x)
    return _dot(tri, hi) + _dot(tri, mid) + _dot(tri, lo) + carry


def _in_proj(h1, w_in, bf_pad):
    tr = TR

    place_q = np.zeros((LANES, HEADS * LANES), np.float32)
    place_k = np.zeros((LANES, HEADS * LANES), np.float32)
    for h in range(HEADS):
        for piece in range(3):
            place_q[PIECE_LANES * piece + h, LANES * h + _at(CUM_Q, h) + piece] = 1.0
            place_k[PIECE_LANES * piece + h, LANES * h + _at(CUM_K, h) + piece] = -1.0

    def body(h_ref, w_ref, bf_ref, pq_ref, pk_ref, qa_ref, ka_ref, va_ref, u_ref, z_ref, carry_ref):
        i = pl.program_id(0)

        @pl.when(i == 0)
        def _():
            carry_ref[...] = jnp.zeros_like(carry_ref)

        proj = _dot(h_ref[...], w_ref[...])
        u_ref[...] = proj[:, :D_POOL]
        z_ref[...] = proj[:, F_OFF:F_OFF + LANES]
        lane = _lane_iota((tr, LANES))
        z = proj[:, F_OFF:F_OFF + LANES] + bf_ref[...]
        log_f = jnp.minimum(z, 0.0) - jnp.log(1.0 + jnp.exp(-jnp.abs(z)))
        log_f = jnp.where(lane < HEADS, log_f, 0.0)
        tri = jnp.where(_row_iota((tr, tr)) >= _lane_iota((tr, tr)), 1.0, 0.0).astype(BF16)
        cum = _cumsum_rows(log_f, tri, carry_ref[0:1, :])
        carry_ref[0:1, :] = cum[tr - 1:tr, :]
        c_hi, c_mid, c_lo = _split3_f32(cum)
        pieces = (c_hi + pltpu.roll(c_mid, PIECE_LANES, 1) + pltpu.roll(c_lo, 2 * PIECE_LANES, 1)).astype(BF16)
        cum_q = _dot(pieces, pq_ref[...])
        cum_k = _dot(pieces, pk_ref[...])

        def between(first, h):
            return (lane >= _at(first, h)) & (lane < _at(first, h) + 3)

        ones_q = [jnp.where(between(CUM_K, h) | (lane == _at(BOTH_ONE, h)), 1.0, 0.0) for h in range(2)]
        ones_k = [jnp.where(between(CUM_Q, h) | between(LSE_Q, h) | (lane == _at(BOTH_ONE, h)), 1.0, 0.0) for h in range(2)]
        aug_v = [jnp.where(lane == _at(DEN_V, h), 1.0, jnp.where(between(DELTA, h), -1.0, 0.0)) for h in range(2)]
        for h in range(HEADS):
            mine = slice(LANES * h, LANES * (h + 1))
            data = _data_lanes(lane, h)
            qa_ref[h] = jnp.where(data, _pair_block(proj, Q_OFF, h) * (DH ** -0.5), cum_q[:, mine] + ones_q[h % 2]).astype(BF16)
            ka_ref[h] = jnp.where(data, _pair_block(proj, K_OFF, h), cum_k[:, mine] + ones_k[h % 2]).astype(BF16)
            va_ref[h] = jnp.where(data, _pair_block(proj, V_OFF, h), aug_v[h % 2]).astype(BF16)

    head_spec = pl.BlockSpec((HEADS, tr, LANES), lambda i: (0, i, 0))
    head_shape = jax.ShapeDtypeStruct((HEADS, S, LANES), BF16)
    place_spec = pl.BlockSpec(place_q.shape, lambda i: (0, 0))
    return pl.pallas_call(
        body, name="in_proj", grid=(S // tr,),
        in_specs=[_row_spec(tr, D), pl.BlockSpec((D, D_IN_PAD), lambda i: (0, 0)), _vec_spec(LANES), place_spec, place_spec],
        out_specs=[head_spec] * 3 + [_row_spec(tr, D_POOL), _row_spec(tr, LANES)],
        out_shape=[head_shape] * 3 + [jax.ShapeDtypeStruct((S, D_POOL), F32), jax.ShapeDtypeStruct((S, LANES), F32)],
        scratch_shapes=[pltpu.VMEM((SUBLANES, LANES), F32)], compiler_params=_params(("arbitrary",)),
    )(h1, w_in, bf_pad, jnp.asarray(place_q, BF16), jnp.asarray(place_k, BF16))


def _hosted(ex, refs, n_blocked_in, n_blocked_out, first, forward_at, last):
    n = len(ex.ins)
    own_in = refs[:n_blocked_in]
    ex_in = refs[n_blocked_in:n_blocked_in + n]
    own_out = refs[n_blocked_in + n:n_blocked_in + n + n_blocked_out]
    ex_out = refs[n_blocked_in + n + n_blocked_out:n_blocked_in + 2 * n + n_blocked_out]
    rest = refs[n_blocked_in + 2 * n + n_blocked_out:]
    args = (ex_in, ex_out, rest[-2], rest[-1])

    def begin():
        @pl.when(first)
        def _():
            ex.start(*args)

        @pl.when(forward_at)
        def _():
            ex.forward(*args)

    def end():
        @pl.when(last)
        def _():
            ex.finish(*args)

    return own_in, own_out, rest[:-2], begin, end


def _fox_fwd(qa, ka, va, ex):
    BQ = BK = FOX_FWD_BLOCK
    nq = S // BQ
    n_pairs = HEADS // 2

    def body(*refs):
        p_id, i = pl.program_id(0), pl.program_id(1)
        (qa_ref, ka_ref, va_ref), (y_ref, qab_ref), (m_scr, acc_scr), begin, end = _hosted(
            ex, refs, 3, 2, (p_id == 0) & (i == 0), (p_id == n_pairs - 1) & (i == 0), (p_id == n_pairs - 1) & (i == nq - 1))
        begin()
        lane = _lane_iota((BQ, LANES))
        causal = _row_iota((BQ, BK)) >= _lane_iota((BQ, BK))
        m_scr[...] = jnp.full_like(m_scr, NEG)
        acc_scr[...] = jnp.zeros_like(acc_scr)

        def step(j, masked):
            rows = pl.ds(pl.multiple_of(j * BK, BK), BK)
            for hh in range(2):
                s = _dot(qa_ref[hh], ka_ref[hh, rows, :], NT)
                if masked:
                    s = jnp.where(causal, s, NEG)
                m_prev = m_scr[hh]
                m_new = jnp.maximum(m_prev, jnp.max(s, axis=1, keepdims=True))
                p = jnp.exp(s - jnp.tile(m_new, (1, BK // LANES)))
                acc_scr[hh] = jnp.exp(m_prev - m_new) * acc_scr[hh] + _dot(p.astype(BF16), va_ref[hh, rows, :])
                m_scr[hh] = m_new

        def full_step(j, carry):
            step(j, False)
            return carry

        lax.fori_loop(0, i, full_step, 0)
        step(i, True)
        outs = []
        for hh in range(2):
            acc = acc_scr[hh]
            den_lane, lse_lane = _at(DEN_V, hh), _at(LSE_Q, hh)
            den = jnp.broadcast_to(acc[:, den_lane:den_lane + 1], (BQ, LANES))
            outs.append(acc * (1.0 / den))
            n_hi, n_mid, n_lo = _split3(-(m_scr[hh] + jnp.log(den)))
            qab_ref[hh] = jnp.where(lane == lse_lane, n_hi,
                                    jnp.where(lane == lse_lane + 1, n_mid, jnp.where(lane == lse_lane + 2, n_lo, qa_ref[hh])))
        y_ref[...] = jnp.where(lane < DH, outs[0], outs[1]).astype(BF16)
        end()

    pair_rows = pl.BlockSpec((2, BQ, LANES), lambda p, i: (p, i, 0))
    pair_all = pl.BlockSpec((2, S, LANES), lambda p, i: (p, 0, 0))
    n = len(ex.ins)
    res = pl.pallas_call(
        body, name="fox_fwd", grid=(n_pairs, nq), in_specs=[pair_rows, pair_all, pair_all] + [ANY] * n,
        out_specs=[pl.BlockSpec((BQ, LANES), lambda p, i: (i, D_POOL // LANES + p)), pair_rows] + [ANY] * n,
        out_shape=[jax.ShapeDtypeStruct((S, D), BF16), jax.ShapeDtypeStruct((HEADS, S, LANES), BF16)] + ex.out_shapes,
        scratch_shapes=[pltpu.VMEM((2, BQ, LANES), F32), pltpu.VMEM((2, BQ, LANES), F32)] + ex.scratch(),
        compiler_params=_params(("arbitrary", "arbitrary")),
    )(qa, ka, va, *ex.ins)
    return res[0], res[1], res[2:]


def _bwd_xa_mix(dqx, w_xq, dres, x2, g_pre, y1, g_post, w_mix_out, ycat, ex):
    steps = S // TR
    n = len(ex.ins)

    def body(*refs):
        i = pl.program_id(0)
        ((dq_ref, wq_ref, dres_ref, x_ref, gpre_ref, y_ref, gpost_ref, wm_ref, ycat_ref),
         (dx_ref, dy_ref, dgpre_ref, dgpost_ref, dp_ref, doa_ref), _, begin, end) = _hosted(
            ex, refs, 9, 6, i == 0, i == 0, i == steps - 1)
        begin()

        @pl.when(i == 0)
        def _():
            dgpre_ref[...] = jnp.zeros_like(dgpre_ref)
            dgpost_ref[...] = jnp.zeros_like(dgpost_ref)

        dxn, dgpre = _rms_bwd(x_ref[...], gpre_ref[...], _dot(dq_ref[...], wq_ref[...], NT))
        dx = dres_ref[...] + dxn
        dx_ref[...] = dx
        dy, dgpost = _rms_bwd(y_ref[...], gpost_ref[...], dx)
        dy = dy.astype(BF16)
        dy_ref[...] = dy
        dgpre_ref[...] += dgpre
        dgpost_ref[...] += dgpost

        d = _dot(dy, wm_ref[...], NT)
        dp_ref[...] = d[:, :D_POOL]
        lane = _lane_iota((TR, LANES))
        low = lane < DH
        for p in range(HEADS // 2):
            cols = slice(D_POOL + LANES * p, D_POOL + LANES * (p + 1))
            do = d[:, cols]
            prod = do * ycat_ref[:, cols].astype(F32)
            deltas = (jnp.sum(jnp.where(low, prod, 0.0), axis=1, keepdims=True),
                      jnp.sum(jnp.where(low, 0.0, prod), axis=1, keepdims=True))
            for hh in range(2):
                d_hi, d_mid, d_lo = _split3_f32(deltas[hh])
                dl = _at(DELTA, hh)
                aug = jnp.where(lane == dl, d_hi, jnp.where(lane == dl + 1, d_mid, jnp.where(lane == dl + 2, d_lo, 0.0)))
                doa_ref[2 * p + hh] = jnp.where(_data_lanes(lane, hh), do, aug).astype(BF16)
        end()

    mat = pl.BlockSpec((D, D), lambda i: (0, 0))
    res = pl.pallas_call(
        body, name="bwd_xa_mix", grid=(steps,),
        in_specs=[_row_spec(TR, D), mat, _row_spec(TR, D), _row_spec(TR, D), _vec_spec(D), _row_spec(TR, D), _vec_spec(D), mat,
                  _row_spec(TR, D)] + [ANY] * n,
        out_specs=[_row_spec(TR, D), _row_spec(TR, D), _vec_spec(D), _vec_spec(D), _row_spec(TR, D_POOL),
                   pl.BlockSpec((HEADS, TR, LANES), lambda i: (0, i, 0))] + [ANY] * n,
        out_shape=[jax.ShapeDtypeStruct((S, D), F32), jax.ShapeDtypeStruct((S, D), BF16), jax.ShapeDtypeStruct((1, D), F32),
                   jax.ShapeDtypeStruct((1, D), F32), jax.ShapeDtypeStruct((S, D_POOL), F32),
                   jax.ShapeDtypeStruct((HEADS, S, LANES), BF16)] + ex.out_shapes,
        scratch_shapes=ex.scratch(), compiler_params=_params(("arbitrary",)),
    )(dqx, w_xq, dres, x2, g_pre, y1, g_post, w_mix_out, ycat, *ex.ins)
    return res[:6], res[6:]


def _fox_bwd(qab, doa, ka, va, ex):
    BQ = BK = FOX_BWD_BLOCK
    nk = S // BK
    n_pairs = HEADS // 2

    def body(*refs):
        p_id, j = pl.program_id(0), pl.program_id(1)
        (qab_ref, doa_ref, ka_ref, va_ref), (dqa_ref, dka_ref, dva_ref), _, begin, end = _hosted(
            ex, refs, 4, 3, (p_id == 0) & (j == 0), (p_id == n_pairs - 1) & (j == 0), (p_id == n_pairs - 1) & (j == nk - 1))
        begin()

        @pl.when(j == 0)
        def _():
            dqa_ref[...] = jnp.zeros_like(dqa_ref)

        causal = _row_iota((BQ, BK)) >= _lane_iota((BQ, BK))
        dka_ref[...] = jnp.zeros_like(dka_ref)
        dva_ref[...] = jnp.zeros_like(dva_ref)

        def step(i, masked):
            rows = pl.ds(pl.multiple_of(i * BQ, BQ), BQ)
            for hh in range(2):
                kb = ka_ref[hh]
                q = qab_ref[hh, rows, :]
                do = doa_ref[hh, rows, :]
                s = _dot(q, kb, NT)
                if masked:
                    s = jnp.where(causal, s, NEG)
                p = jnp.exp(s)
                ds = p * _dot(do, va_ref[hh], NT)
                pb = p.astype(BF16)
                dsb = ds.astype(BF16)
                dva_ref[hh] += _dot(pb, do, TN)
                dka_ref[hh] += _dot(dsb, q, TN)
                dqa_ref[hh, rows, :] += _dot(dsb, kb)

        def full_step(i, carry):
            step(i, False)
            return carry

        step(j, True)
        lax.fori_loop(j + 1, nk, full_step, 0)
        end()

    pair_all = pl.BlockSpec((2, S, LANES), lambda p, j: (p, 0, 0))
    pair_rows = pl.BlockSpec((2, BK, LANES), lambda p, j: (p, j, 0))
    shape = jax.ShapeDtypeStruct((HEADS, S, LANES), F32)
    n = len(ex.ins)
    res = pl.pallas_call(
        body, name="fox_bwd", grid=(n_pairs, nk), in_specs=[pair_all, pair_all, pair_rows, pair_rows] + [ANY] * n,
        out_specs=[pair_all, pair_rows, pair_rows] + [ANY] * n, out_shape=[shape] * 3 + ex.out_shapes,
        scratch_shapes=ex.scratch(), compiler_params=_params(("arbitrary", "arbitrary")),
    )(qab, doa, ka, va, *ex.ins)
    return res[0], res[1], res[2], res[3:]


def _fox_bwd_post(dqa, dka, dva, du, proj, bf_pad):
    tr = PREP_TR
    nt = S // tr

    pick = np.zeros((HEADS * LANES, LANES), np.float32)
    for h in range(HEADS):
        pick[LANES * h + _at(BOTH_ONE, h), h] = 1.0

    def body(dqa_ref, dka_ref, dva_ref, du_ref, z_ref, bf_ref, pick_ref, dp_ref, dbf_ref, carry_ref):
        i = pl.program_id(0)

        @pl.when(i == 0)
        def _():
            carry_ref[...] = jnp.zeros_like(carry_ref)
            dbf_ref[...] = jnp.zeros_like(dbf_ref)

        lane = _lane_iota((tr, LANES))
        diff = jnp.concatenate([dqa_ref[h] - dka_ref[h] for h in range(HEADS)], axis=1)
        hi = diff.astype(BF16)
        dcum = _dot(hi, pick_ref[...]) + _dot((diff - hi.astype(F32)).astype(BF16), pick_ref[...])
        tri =jnp.where(_lane_iota((tr, tr)) >= _row_iota((tr, tr)), 1.0, 0.0).astype(BF16)
        dlog_f = _cumsum_rows(dcum, tri, carry_ref[0:1, :])
        carry_ref[0:1, :] = dlog_f[0:1, :]
        z = z_ref[...] + bf_ref[...]
        df = jnp.where(lane < HEADS, dlog_f / (1.0 + jnp.exp(z)), 0.0)
        dbf_ref[...] += jnp.sum(df, axis=0, keepdims=True)

        dp_ref[:, 0:D_POOL] = du_ref[...].astype(BF16)
        low = lane < DH
        for ref, off, scale in ((dqa_ref, Q_OFF, DH ** -0.5), (dka_ref, K_OFF, 1.0), (dva_ref, V_OFF, 1.0)):
            for p in range(HEADS // 2):
                blk = jnp.where(low, ref[2 * p], ref[2 * p + 1])
                dp_ref[:, off + LANES * p:off + LANES * (p + 1)] = (blk * scale).astype(BF16)
        dp_ref[:, F_OFF:F_OFF + LANES] = df.astype(BF16)

    head_spec = pl.BlockSpec((HEADS, tr, LANES), lambda i: (0, nt - 1 - i, 0))
    return pl.pallas_call(
        body, name="fox_bwd_post", grid=(nt,),
        in_specs=[head_spec, head_spec, head_spec, pl.BlockSpec((tr, D_POOL), lambda i: (nt - 1 - i, 0)),
                  pl.BlockSpec((tr, LANES), lambda i: (nt - 1 - i, 0)), _vec_spec(LANES),
                  pl.BlockSpec(pick.shape, lambda i: (0, 0))],
        out_specs=[pl.BlockSpec((tr, D_IN_PAD), lambda i: (nt - 1 - i, 0)), _vec_spec(LANES)],
        out_shape=[jax.ShapeDtypeStruct((S, D_IN_PAD), BF16), jax.ShapeDtypeStruct((1, LANES), F32)],
        scratch_shapes=[pltpu.VMEM((SUBLANES, LANES), F32)],
        compiler_params=_params(("arbitrary",)),
    )(dqa, dka, dva, du, proj, bf_pad, jnp.asarray(pick, BF16))


POOL_HALO = 16


def _by_group(lane, a2, a4, a8, a16):
    return jnp.where(lane < 64, a2, jnp.where(lane < 128, a4, jnp.where(lane < 192, a8, a16)))


def _window_count(lane, t):
    return jnp.minimum(t + 1, _by_group(lane, 2, 4, 8, 16)).astype(F32)


def _pool_diff(u, halo, first, tile):
    n = TR + POOL_HALO
    ext = jnp.concatenate([jnp.where(first, 0.0, halo), u], axis=0)
    s2 = ext + pltpu.roll(ext, 1, 0)
    s4 = s2 + pltpu.roll(s2, 2, 0)
    s8 = s4 + pltpu.roll(s4, 4, 0)
    s16 = s8 + pltpu.roll(s8, 8, 0)
    lane = _lane_iota((n, D_POOL))
    win = _by_group(lane, s2, s4, s8, s16)[POOL_HALO:]
    lane = _lane_iota((TR, D_POOL))
    t = tile * TR + _row_iota((TR, D_POOL))
    return win / _window_count(lane, t) - u


def _prev_halo(rows, width, col):
    per = TR // rows
    return pl.BlockSpec((rows, width), lambda i: (jnp.maximum(i * per - 1, 0), col))


def _next_halo(rows, width, col):
    per = TR // rows
    return pl.BlockSpec((rows, width), lambda i: (jnp.minimum((i + 1) * per, S // rows - 1), col))


def _pool_fwd(proj, w_bd, ps, ycat):
    def body(u_ref, halo_ref, w_ref, ps_ref, ycat_ref, y_ref):
        i = pl.program_id(0)
        diff = _pool_diff(u_ref[...], halo_ref[...], i == 0, i)
        y_ref[...] = (_dot(diff.astype(BF16), w_ref[...]) * ps_ref[...]).astype(BF16)

    return pl.pallas_call(
        body, name="pool_fwd", grid=(S // TR,),
        in_specs=[_row_spec(TR, D_POOL), _prev_halo(POOL_HALO, D_POOL, 0),
                  pl.BlockSpec((D_POOL, D_POOL), lambda i: (0, 0)), _vec_spec(D_POOL), ANY],
        out_specs=_row_spec(TR, D_POOL), out_shape=jax.ShapeDtypeStruct((S, D), BF16), input_output_aliases={4: 0},
        compiler_params=_params(("parallel",)),
    )(proj, proj, w_bd, ps, ycat)


def _pool_bwd(proj, dycat, w_bd, w_bd_t, ps):
    nt = S // TR
    n = TR + POOL_HALO

    def body(u_ref, halo_ref, dy_ref, dyn_ref, w_ref, wt_ref, ps_ref, du_ref, dw_ref, dps_ref):
        i = pl.program_id(0)

        @pl.when(i == 0)
        def _():
            dw_ref[...] = jnp.zeros_like(dw_ref)
            dps_ref[...] = jnp.zeros_like(dps_ref)

        diff = _pool_diff(u_ref[...], halo_ref[...], i == 0, i).astype(BF16)
        dy = dy_ref[...]
        dps_ref[...] += jnp.sum(dy * _dot(diff, w_ref[...]), axis=0, keepdims=True)
        dy_ext = jnp.concatenate([dy, jnp.where(i == nt - 1, 0.0, dyn_ref[...])], axis=0)
        dmixed = (dy_ext * ps_ref[...]).astype(BF16)
        ddiff = _dot(dmixed, wt_ref[...])
        dw_ref[...] += _dot(diff, dmixed[:TR], TN)
        lane = _lane_iota((n, D_POOL))
        t = i * TR + _row_iota((n, D_POOL))
        e = ddiff / _window_count(lane, t)
        f2 = e + pltpu.roll(e, n - 1, 0)
        f4 = f2 + pltpu.roll(f2, n - 2, 0)
        f8 = f4 + pltpu.roll(f4, n - 4, 0)
        f16 = f8 + pltpu.roll(f8, n - 8, 0)
        du_ref[...] = _by_group(lane, f2, f4, f8, f16)[:TR] - ddiff[:TR]

    mat = pl.BlockSpec((D_POOL, D_POOL), lambda i: (0, 0))
    return pl.pallas_call(
        body, name="pool_bwd", grid=(nt,),
        in_specs=[_row_spec(TR, D_POOL), _prev_halo(POOL_HALO, D_POOL, 0), _row_spec(TR, D_POOL),
                  _next_halo(POOL_HALO, D_POOL, 0), mat, mat, _vec_spec(D_POOL)],
        out_specs=[_row_spec(TR, D_POOL), mat, _vec_spec(D_POOL)],
        out_shape=[jax.ShapeDtypeStruct((S, D_POOL), F32), jax.ShapeDtypeStruct((D_POOL, D_POOL), F32),
                   jax.ShapeDtypeStruct((1, D_POOL), F32)],
        compiler_params=_params(("arbitrary",)),
    )(proj, proj, dycat, dycat, w_bd, w_bd_t, ps)


def _xa_probs(q, k):
    s = _dot(q, k, NT) * (XA_DH ** -0.5)
    e = jnp.exp(s - jnp.max(s, axis=-1, keepdims=True))
    return e * (1.0 / jnp.sum(e, axis=-1, keepdims=True))


def _xattn_fwd(qx, kv):
    def body(q_ref, kv_ref, o_ref):
        for h in range(XA_HEADS):
            cols = slice(XA_DH * h, XA_DH * (h + 1))
            vcols = slice(D + XA_DH * h, D + XA_DH * (h + 1))
            p = _xa_probs(q_ref[:, cols], kv_ref[:, cols])
            o_ref[:, cols] = _dot(p.astype(BF16), kv_ref[:, vcols]).astype(BF16)

    return pl.pallas_call(
        body, name="xattn_fwd", grid=(S // TR,),
        in_specs=[_row_spec(TR, D), pl.BlockSpec((MEM, 2 * D), lambda i: (0, 0))],
        out_specs=_row_spec(TR, D), out_shape=jax.ShapeDtypeStruct((S, D), BF16),
        compiler_params=_params(("parallel",)),
    )(qx, kv)


def _xattn_bwd(qx, kv, dxo):
    def body(q_ref, kv_ref, do_ref, dq_ref, dkv_ref):
        i = pl.program_id(0)

        @pl.when(i == 0)
        def _():
            dkv_ref[...] = jnp.zeros_like(dkv_ref)

        for h in range(XA_HEADS):
            cols = slice(XA_DH * h, XA_DH * (h + 1))
            vcols = slice(D + XA_DH * h, D + XA_DH * (h + 1))
            q = q_ref[:, cols]
            k = kv_ref[:, cols]
            do = do_ref[:, cols]
            p = _xa_probs(q, k)
            dkv_ref[:, vcols] += _dot(p.astype(BF16), do, TN)
            dp = _dot(do, kv_ref[:, vcols], NT)
            ds = (p * (dp - jnp.sum(p * dp, axis=-1, keepdims=True)) * (XA_DH ** -0.5)).astype(BF16)
            dq_ref[:, cols] = _dot(ds, k).astype(BF16)
            dkv_ref[:, cols] += _dot(ds, q, TN)

    kv_spec = pl.BlockSpec((MEM, 2 * D), lambda i: (0, 0))
    return pl.pallas_call(
        body, name="xattn_bwd", grid=(S // TR,), in_specs=[_row_spec(TR, D), kv_spec, _row_spec(TR, D)],
        out_specs=[_row_spec(TR, D), kv_spec],
        out_shape=[jax.ShapeDtypeStruct((S, D), BF16), jax.ShapeDtypeStruct((MEM, 2 * D), F32)],
        compiler_params=_params(("arbitrary",)),
    )(qx, kv, dxo)


CONV_HALO = SUBLANES
TC = 512
TC_FWD = 1024
GELU_K = 0.7978845608028654
GELU_C = 0.044715


def _conv3(ext, w, rows):
    h0 = ext[CONV_HALO:CONV_HALO + rows]
    h1 = pltpu.roll(ext, 1, 0)[CONV_HALO:CONV_HALO + rows]
    h2 = pltpu.roll(ext, 2, 0)[CONV_HALO:CONV_HALO + rows]
    return w[2:3] * h0 + w[1:2] * h1 + w[0:1] * h2 + w[3:4], (h2, h1, h0)


def _conv_specs(tc):
    main = pl.BlockSpec((2, TR, tc), lambda j, i: (0, i, j))
    per = TR // CONV_HALO
    prev = pl.BlockSpec((2, CONV_HALO, tc), lambda j, i: (0, jnp.maximum(i * per - 1, 0), j))
    nxt = pl.BlockSpec((2, CONV_HALO, tc), lambda j, i: (0, jnp.minimum((i + 1) * per, S // CONV_HALO - 1), j))
    par = pl.BlockSpec((2, SUBLANES, tc), lambda j, i: (0, 0, j))
    return main, prev, nxt, par


def _convgate_fwd(hid, cwb):
    tc = TC_FWD

    def body(h_ref, hp_ref, w_ref, act_ref):
        i = pl.program_id(1)
        c = []
        for g in range(2):
            ext = jnp.concatenate([jnp.where(i == 0, 0.0, hp_ref[g]), h_ref[g]], axis=0)
            c.append(_conv3(ext, w_ref[g], TR)[0])
        gate, up = c
        act_ref[...] = (jax.nn.gelu(gate, approximate=True) * up).astype(BF16)

    main, prev, _, par = _conv_specs(tc)
    return pl.pallas_call(
        body, name="convgate_fwd", grid=(D_FF // tc, S // TR), in_specs=[main, prev, par],
        out_specs=pl.BlockSpec((TR, tc), lambda j, i: (i, j)), out_shape=jax.ShapeDtypeStruct((S, D_FF), BF16),
        compiler_params=_params(("parallel", "parallel")),
    )(hid, hid, cwb)


def _convgate_bwd(hid, dact, cwb):
    nr = S // TR
    n = TR + CONV_HALO

    def body(h_ref, hp_ref, hn_ref, da_ref, dan_ref, w_ref, dh_ref, dw_ref):
        i = pl.program_id(1)

        @pl.when(i == 0)
        def _():
            dw_ref[...] = jnp.zeros_like(dw_ref)

        da = jnp.concatenate([da_ref[...], jnp.where(i == nr - 1, 0.0, dan_ref[...])], axis=0)
        c, taps = [], []
        for g in range(2):
            ext = jnp.concatenate([jnp.where(i == 0, 0.0, hp_ref[g]), h_ref[g], hn_ref[g]], axis=0)
            cg, tg = _conv3(ext, w_ref[g], n)
            c.append(cg)
            taps.append(tg)
        gate, up = c
        th = jnp.tanh(GELU_K * (gate + GELU_C * gate * gate * gate))
        gelu = 0.5 * gate * (1.0 + th)
        dgelu = 0.5 * (1.0 + th) + 0.5 * gate * (1.0 - th * th) * GELU_K * (1.0 + 3.0 * GELU_C * gate * gate)
        for g, dc in enumerate((da * up * dgelu, da * gelu)):
            w = w_ref[g]
            dh = w[2:3] * dc[:TR] + w[1:2] * pltpu.roll(dc, n - 1, 0)[:TR] + w[0:1] * pltpu.roll(dc, n - 2, 0)[:TR]
            dh_ref[g] = dh.astype(BF16)
            dcm = dc[:TR]
            for r in range(3):
                dw_ref[g, r:r + 1, :] += jnp.sum(dcm * taps[g][r][:TR], axis=0, keepdims=True)
            dw_ref[g, 3:4, :] += jnp.sum(dcm, axis=0, keepdims=True)

    main, prev, nxt, par = _conv_specs(TC)
    per = TR // CONV_HALO
    return pl.pallas_call(
        body, name="convgate_bwd", grid=(D_FF // TC, nr),
        in_specs=[main, prev, nxt, pl.BlockSpec((TR, TC), lambda j, i: (i, j)),
                  pl.BlockSpec((CONV_HALO, TC), lambda j, i: (jnp.minimum((i + 1) * per, S // CONV_HALO - 1), j)), par],
        out_specs=[main, par],
        out_shape=[jax.ShapeDtypeStruct((2, S, D_FF), BF16), jax.ShapeDtypeStruct((2, SUBLANES, D_FF), F32)],
        compiler_params=_params(("parallel", "arbitrary")),
    )(hid, hid, hid, dact, dact, cwb)


def _adam_update(w, g, m, v):
    m = ADAM_B1 * m + (1.0 - ADAM_B1) * g
    v = ADAM_B2 * v + (1.0 - ADAM_B2) * (g * g)
    m_hat = m / (1.0 - ADAM_B1 ** ADAM_STEP)
    v_hat = v / (1.0 - ADAM_B2 ** ADAM_STEP)
    return -ADAM_LR * (m_hat / (jnp.sqrt(v_hat) + ADAM_EPS) + ADAM_WD * w), m, v


def _row_tile(rows, cols, itemsize=4, target=TILE_BYTES):
    tr = SUBLANES
    while rows % (2 * tr) == 0 and 2 * tr * cols * itemsize <= target:
        tr *= 2
    assert rows % tr == 0, (rows, tr)
    return rows if rows % (2 * tr) and 16 * tr * cols * itemsize < target else tr


def _adamw(name, w, g, m, v):
    rows, cols = w.shape
    tr = rows if rows * cols * 4 <= TILE_BYTES // 2 else _row_tile(rows, cols, target=TILE_BYTES // 2)

    def body(w_ref, g_ref, m_ref, v_ref, d_ref, nm_ref, nv_ref):
        d_ref[...], nm_ref[...], nv_ref[...] = _adam_update(w_ref[...], g_ref[...], m_ref[...], v_ref[...])

    spec = _row_spec(tr, cols)
    shape = jax.ShapeDtypeStruct((rows, cols), F32)
    return pl.pallas_call(
        body, name=name, grid=(rows // tr,), in_specs=[spec] * 4, out_specs=[spec] * 3, out_shape=[shape] * 3,
        compiler_params=_params(("parallel",)),
    )(w, g, m, v)


def _adamw_halves(name, core, w, g_mine, g_sibling, m, v):
    rows, cols = w.shape
    half = rows // 2
    tr = _row_tile(half, cols, target=TILE_BYTES // 2)
    per = half // tr

    def body(core_ref, w_ref, gm_ref, gs_ref, m_ref, v_ref, g_ref, d_ref, nm_ref, nv_ref):
        g = jnp.where(pl.program_id(0) // per == core_ref[0], gm_ref[...], gs_ref[...])
        g_ref[...] = g
        d_ref[...], nm_ref[...], nv_ref[...] = _adam_update(w_ref[...], g, m_ref[...], v_ref[...])

    spec = pl.BlockSpec((tr, cols), lambda i, core_ref: (i, 0))
    half_spec = pl.BlockSpec((tr, cols), lambda i, core_ref: (i % per, 0))
    shape = jax.ShapeDtypeStruct((rows, cols), F32)
    return pl.pallas_call(
        body, name=name, out_shape=[shape] * 4,
        grid_spec=pltpu.PrefetchScalarGridSpec(
            num_scalar_prefetch=1, grid=(rows // tr,), in_specs=[spec, half_spec, half_spec, spec, spec], out_specs=[spec] * 4),
        compiler_params=_params(("parallel",)),
    )(core, w, g_mine, g_sibling, m, v)


def _adamw_halves_columns(name, core, w, g_mine, g_sibling, m, v):
    cols, _, rows = w.shape
    tl = 2 * LANES
    per = rows // 2 // tl

    def body(core_ref, w_ref, gm_ref, gs_ref, m_ref, v_ref, g_ref, d_ref, nm_ref, nv_ref):
        g = jnp.where(pl.program_id(0) // per == core_ref[0], gm_ref[...], gs_ref[...])
        g_ref[...] = g
        d_ref[...], nm_ref[...], nv_ref[...] = _adam_update(w_ref[...], g, m_ref[...], v_ref[...])

    spec = pl.BlockSpec((cols, 1, tl), lambda i, core_ref: (0, 0, i))
    half_spec = pl.BlockSpec((cols, 1, tl), lambda i, core_ref: (0, 0, i % per))
    shape = jax.ShapeDtypeStruct((cols, 1, rows), F32)
    return pl.pallas_call(
        body, name=name, out_shape=[shape] * 4,
        grid_spec=pltpu.PrefetchScalarGridSpec(
            num_scalar_prefetch=1, grid=(rows // tl,), in_specs=[spec, half_spec, half_spec, spec, spec], out_specs=[spec] * 4),
        compiler_params=_params(("parallel",)),
    )(core, w, g_mine, g_sibling, m, v)


def _chip_sum(name, core, g, other):
    _, _, half, cols = g.shape
    tr = _row_tile(half, cols)

    def body(core_ref, g_ref, o_ref, p_ref):
        p_ref[...] = (g_ref[...] + o_ref[...]).astype(BF16)

    spec = pl.BlockSpec((None, tr, cols), lambda j, i, core_ref: (j, i, 0))
    return pl.pallas_call(
        body, name=name, out_shape=jax.ShapeDtypeStruct((N_CHIPS, half, cols), BF16),
        grid_spec=pltpu.PrefetchScalarGridSpec(
            num_scalar_prefetch=1, grid=(N_CHIPS, half // tr),
            in_specs=[pl.BlockSpec((None, None, tr, cols), lambda j, i, core_ref: (j, core_ref[0], i, 0)), spec],
            out_specs=spec),
        compiler_params=_params(("parallel", "parallel")),
    )(core, g, other)


def _mesh_sum(name, chip, received, own):
    _, half, cols = received.shape
    tr = _row_tile(half, cols, itemsize=2 * N_CHIPS)

    def body(chip_ref, r_ref, own_ref, o_ref):
        acc = None
        for j in range(N_CHIPS):
            term = jnp.where(chip_ref[0] == j, own_ref[...], r_ref[j]).astype(F32)
            acc = term if acc is None else acc + term
        o_ref[...] = acc

    return pl.pallas_call(
        body, name=name, out_shape=jax.ShapeDtypeStruct((half, cols), F32),
        grid_spec=pltpu.PrefetchScalarGridSpec(
            num_scalar_prefetch=1, grid=(half // tr,),
            in_specs=[pl.BlockSpec((N_CHIPS, tr, cols), lambda i, chip_ref: (0, i, 0)),
                      pl.BlockSpec((None, tr, cols), lambda i, chip_ref: (chip_ref[0], i, 0))],
            out_specs=pl.BlockSpec((tr, cols), lambda i, chip_ref: (i, 0))),
        compiler_params=_params(("parallel",)),
    )(chip, received, own)


CHIP_FLIPS = ((1, 0), (0, 1), (1, 1))


def _place():
    x, y, c = lax.axis_index("x"), lax.axis_index("y"), lax.axis_index("c")
    return x, y, c, 2 * x + y


def _remote(src, dst, sems_s, sems_r, k, dev):
    return pltpu.make_async_remote_copy(src_ref=src, dst_ref=dst, send_sem=sems_s.at[k], recv_sem=sems_r.at[k],
                                        device_id=dev, device_id_type=MESH)


class _Exchange:
    def __init__(self, ins, out_shapes, n_sems, start, forward, finish):
        self.ins, self.out_shapes, self.n_sems = list(ins), list(out_shapes), n_sems
        self.start, self.forward, self.finish = start, forward, finish

    def scratch(self):
        return [pltpu.SemaphoreType.DMA((self.n_sems,)), pltpu.SemaphoreType.DMA((self.n_sems,))]

    def run(self, name):
        n = len(self.ins)

        def body(*refs):
            args = (refs[:n], refs[n:2 * n]) + tuple(refs[2 * n:])
            self.start(*args)
            self.forward(*args)
            self.finish(*args)

        return pl.pallas_call(
            body, name=name, in_specs=[ANY] * n, out_specs=[ANY] * n, out_shape=self.out_shapes, scratch_shapes=self.scratch(),
        )(*self.ins)


def _all_gather_weights(halved, whole):
    nh, nw = len(halved), len(whole)
    n_arr = nh + nw

    def copies(ins, outs, sems_s, sems_r):
        x, y, c, me = _place()
        sibling = (x, y, 1 - c)
        own = [_remote(ins[k], outs[k].at[me], sems_s, sems_r, k, sibling) for k in range(n_arr)]
        first, passed = [], []
        for k in range(n_arr):
            for f, (fx, fy) in enumerate(CHIP_FLIPS):
                src, dst = (ins[k].at[c], outs[k].at[me, c]) if k < nh else (ins[k], outs[k].at[me])
                first.append(_remote(src, dst, sems_s, sems_r, n_arr + 3 * k + f, (x ^ fx, y ^ fy, c)))
        for k in range(nh):
            for f, (fx, fy) in enumerate(CHIP_FLIPS):
                landed = outs[k].at[2 * (x ^ fx) + (y ^ fy), c]
                passed.append(_remote(landed, landed, sems_s, sems_r, 4 * n_arr + 3 * k + f, sibling))
        return own, first, passed

    def start(*refs):
        own, first, _ = copies(*refs)
        for cp in own + first:
            cp.start()

    def forward(*refs):
        _, first, passed = copies(*refs)
        for arrived, cp in zip(first, passed):
            arrived.wait_recv()
            cp.start()

    def finish(*refs):
        own, first, passed = copies(*refs)
        for cp in first[3 * nh:] + passed + own:
            cp.wait_recv()
        for cp in first + passed + own:
            cp.wait_send()

    shapes = [jax.ShapeDtypeStruct((N_CHIPS,) + a.shape, a.dtype) for a in list(halved) + list(whole)]
    return _Exchange(list(halved) + list(whole), shapes, 7 * nh + 4 * nw, start, forward, finish)


def _swap_halves(gs):
    n = len(gs)

    def copies(ins, outs, sems_s, sems_r):
        x, y, c, _ = _place()
        return [_remote(ins[k].at[:, 1 - c], outs[k], sems_s, sems_r, k, (x, y, 1 - c)) for k in range(n)]

    def start(*refs):
        for cp in copies(*refs):
            cp.start()

    def finish(*refs):
        for cp in copies(*refs):
            cp.wait()

    shapes = [jax.ShapeDtypeStruct((g.shape[0],) + g.shape[2:], g.dtype) for g in gs]
    return _Exchange(gs, shapes, n, start, _no_copies, finish)


def _scatter_chips(ps):
    n = len(ps)

    def copies(ins, outs, sems_s, sems_r):
        x, y, c, me = _place()
        return [_remote(ins[k].at[2 * (x ^ fx) + (y ^ fy)], outs[k].at[me], sems_s, sems_r, 3 * k + f, (x ^ fx, y ^ fy, c))
                for k in range(n) for f, (fx, fy) in enumerate(CHIP_FLIPS)]

    def start(*refs):
        for cp in copies(*refs):
            cp.start()

    def forward(*refs):
        pass

    def finish(*refs):
        for cp in copies(*refs):
            cp.wait()

    shapes = [jax.ShapeDtypeStruct(p.shape, p.dtype) for p in ps]
    return _Exchange(ps, shapes, 3 * n, start, forward, finish)


def _swap_reduced(rs):
    n = len(rs)

    def copies(ins, outs, sems_s, sems_r):
        x, y, c, _ = _place()
        return [_remote(ins[k], outs[k], sems_s, sems_r, k, (x, y, 1 - c)) for k in range(n)]

    def start(*refs):
        for cp in copies(*refs):
            cp.start()

    def finish(*refs):
        for cp in copies(*refs):
            cp.wait()

    return _Exchange(rs, [jax.ShapeDtypeStruct(r.shape, r.dtype) for r in rs], n, start, _no_copies, finish)


N_DEV = 8


def _gather_small(buf):
    def copies(ins, outs, sems_s, sems_r):
        x, y, c, _ = _place()
        me = 4 * x + 2 * y + c
        return [_remote(ins[0], outs[0].at[me], sems_s, sems_r, o - 1, (x ^ (o >> 2), y ^ ((o >> 1) & 1), c ^ (o & 1)))
                for o in range(1, N_DEV)]

    def start(*refs):
        for cp in copies(*refs):
            cp.start()

    def finish(*refs):
        for cp in copies(*refs):
            cp.wait()

    return _Exchange([buf], [jax.ShapeDtypeStruct((N_DEV,) + buf.shape, buf.dtype)], N_DEV - 1, start, _no_copies, finish)


def _sum_devices(place, gathered, own):
    rows = own.shape[0]

    def body(place_ref, g_ref, own_ref, o_ref):
        acc = None
        for d in range(N_DEV):
            term = jnp.where(place_ref[0] == d, own_ref[...], g_ref[d])
            acc = term if acc is None else acc + term
        o_ref[...] = acc

    return pl.pallas_call(
        body, name="sum_devices", out_shape=jax.ShapeDtypeStruct((rows, LANES), F32),
        grid_spec=pltpu.PrefetchScalarGridSpec(
            num_scalar_prefetch=1, grid=(1,),
            in_specs=[pl.BlockSpec((N_DEV, rows, LANES), lambda i, place_ref: (0, 0, 0)),
                      pl.BlockSpec((rows, LANES), lambda i, place_ref: (0, 0))],
            out_specs=pl.BlockSpec((rows, LANES), lambda i, place_ref: (0, 0))),
        compiler_params=_params(("arbitrary",)),
    )(place, gathered, own)


def _no_copies(*refs):
    pass


def _no_exchange():
    return _Exchange([], [], 1, _no_copies, _no_copies, _no_copies)


class _NoComm:
    def gather_first(self):
        return _no_exchange()

    def first_landed(self, p, landed):
        pass

    def gather_rest(self, p):
        return _no_exchange()

    def weights_landed(self, p, landed):
        pass

    def gather_last(self):
        return _no_exchange()

    def last_landed(self, p, landed):
        pass

    def swap_first(self, g):
        return _no_exchange()

    def first_swapped(self, landed):
        pass

    def swap_second(self, g):
        return _no_exchange()

    def second_swapped(self, landed):
        pass

    def scatter_early(self, g):
        return _no_exchange()

    def scatter_landed(self, landed):
        pass

    def swap_reduced_early(self):
        return _no_exchange()

    def reduced_landed(self, landed):
        pass

    def scatter_late(self, g):
        return _no_exchange()

    def late_landed(self, landed):
        pass


def _local_step(x, mem, target, p, comm):
    h1, landed = _norm_fwd("norm_mix_pre", x, p["norm_mix_pre"], comm.gather_first())
    comm.first_landed(p, landed)
    qa, ka, va, u, z = _in_proj(h1, p["w_in"], p["bf_pad"])
    ycat, qab, landed = _fox_fwd(qa, ka, va, comm.gather_rest(p))
    comm.weights_landed(p, landed)
    ycat = _pool_fwd(u, p["w_pool_bd"], p["pool_scale"], ycat)
    y1, x2, h2, qx = _proj_resid_norm("mix_out", ycat, p["w_mix_out"], x, p["norm_mix_post"], p["norm_xa_pre"], p["w_xq"])
    mem_n = _norm_fwd("norm_mem", mem, p["norm_mem"])
    kv = _mm(
        "xkv", mem_n, p["w_xkv"], pl.BlockSpec((MEM, D), lambda i, j, k: (0, 0)),
        pl.BlockSpec((None, D, 512), lambda i, j, k: (j, 0, 0)), jax.ShapeDtypeStruct((MEM, 2 * D), BF16),
        pl.BlockSpec((MEM, 512), lambda i, j, k: (0, j)), (1, N_CHIPS, 1), NN, (MEM, 512))
    xo = _xattn_fwd(qx, kv)
    y2, x3, h3 = _proj_resid_norm("xo", xo, p["w_xo"], x2, p["norm_xa_post"], p["norm_ffn_pre"])
    hid, landed = _mm(
        "up_proj", h3, p["w_up"], pl.BlockSpec((2048, D), lambda i, j, k: (i, 0)),
        pl.BlockSpec((None, D, 1024), lambda i, j, k: (j // 2, 0, j % 2)), jax.ShapeDtypeStruct((2, S, D_FF), F32),
        pl.BlockSpec((None, 2048, 1024), lambda i, j, k: (j // 4, i, j % 4)), (S // 2048, 8, 1), NN, (2048, 1024),
        comm.gather_last())
    comm.last_landed(p, landed)
    act = _convgate_fwd(hid, p["cwb"])

    g = {}
    dres, dy3, g["norm_ffn_post"], loss_cols = _down_loss_bwd(act, p["w_down"], x3, p["norm_ffn_post"], target)
    dact = _mm_nt("d_act", dy3, p["w_down"], F32, 2048, 1024)
    g["w_down"] = _mm_tn("dw_down", act, dy3, 1024, 512)
    dhid, dcwb = _convgate_bwd(hid, dact, p["cwb"])
    g["w_up"] = _mm(
        "dw_up", h3, dhid, pl.BlockSpec((S, D), lambda i, j, k: (0, 0)),
        pl.BlockSpec((None, S, 512), lambda i, j, k: (j // 8, 0, j % 8)), jax.ShapeDtypeStruct((N_CHIPS, D, 2048), F32),
        pl.BlockSpec((None, D, 512), lambda i, j, k: (j // 4, 0, j % 4)), (1, 16, 1), TN, (D, 512))
    dh3, landed = _d_h3(dhid, p["w_up"], comm.swap_first(g))
    comm.first_swapped(landed)
    dres, dy2, dxo, g["norm_ffn_pre"], g["norm_xa_post"] = _mid_bwd(
        "bwd_ffn_xa", dres, x3, p["norm_ffn_pre"], dh3, y2, p["norm_xa_post"], p["w_xo"])
    g["w_xo"] = _mm_tn("dw_xo", xo, dy2, 1024, 512)
    dqx, dkv = _xattn_bwd(qx, kv, dxo)
    dkv = dkv.astype(BF16)
    g["w_xq"] = _mm_tn("dw_xq", h2, dqx, 1024, 512)
    dmem_n = _mm(
        "d_mem", dkv, p["w_xkv"], pl.BlockSpec((MEM, 512), lambda i, j, k: (0, k)),
        pl.BlockSpec((None, D, 512), lambda i, j, k: (k, 0, 0)), jax.ShapeDtypeStruct((MEM, D), F32),
        pl.BlockSpec((MEM, D), lambda i, j, k: (0, 0)), (1, 1, N_CHIPS), NT, (MEM, D))
    g["w_xkv"] = _mm(
        "dw_xkv", mem_n, dkv, pl.BlockSpec((MEM, D), lambda i, j, k: (0, 0)),
        pl.BlockSpec((MEM, 512), lambda i, j, k: (0, j)), jax.ShapeDtypeStruct((N_CHIPS, D, 512), F32),
        pl.BlockSpec((None, D, 512), lambda i, j, k: (j, 0, 0)), (1, N_CHIPS, 1), TN, (D, 512))
    g["norm_mem"] = _gain_bwd("dg_mem", mem, p["norm_mem"], dmem_n)
    (dres, dy1, g["norm_xa_pre"], g["norm_mix_post"], dy_pool, doa), landed = _bwd_xa_mix(
        dqx, p["w_xq"], dres, x2, p["norm_xa_pre"], y1, p["norm_mix_post"], p["w_mix_out"], ycat, comm.swap_second(g))
    comm.second_swapped(landed)
    g["w_mix_out"] = _mm_tn("dw_mix_out", ycat, dy1, 1024, 512)
    dqa, dka, dva, landed = _fox_bwd(qab, doa, ka, va, comm.scatter_early(g))
    comm.scatter_landed(landed)
    du, g["w_pool_full"], g["pool_scale"] = _pool_bwd(u, dy_pool, p["w_pool_bd"], p["w_pool_bd_t"], p["pool_scale"])
    dproj, g["bf_pad"] = _fox_bwd_post(dqa, dka, dva, du, z, p["bf_pad"])
    g["w_in"], landed = _dw_in(dproj, h1, comm.swap_reduced_early())
    comm.reduced_landed(landed)
    dh1, landed = _mm_nt("d_h1", dproj, p["w_in"], F32, 1024, 1024, comm.scatter_late(g))
    comm.late_landed(landed)
    grad_x, g["norm_mix_pre"] = _first_bwd(dres, x, p["norm_mix_pre"], dh1)
    g["cwb"] = dcwb
    return grad_x, g, loss_cols


BIG = ("w_in", "w_mix_out", "w_xq", "w_xkv", "w_xo", "w_up", "w_down")
ROW_SHARDED = ("w_mix_out", "w_xq", "w_xo", "w_down")
SMALL = ("norm_mix_pre", "norm_mix_post", "b_forget", "w_pool", "pool_scale", "norm_mem", "norm_xa_pre", "norm_xa_post",
         "norm_ffn_pre", "norm_ffn_post", "conv_b")
ORDER = ("norm_mix_pre", "norm_mix_post", "w_in", "b_forget", "w_pool", "pool_scale", "w_mix_out", "norm_mem", "norm_xa_pre",
         "norm_xa_post", "w_xq", "w_xkv", "w_xo", "norm_ffn_pre", "norm_ffn_post", "w_up", "conv_w", "conv_b", "w_down")
SLOT = SUBLANES * LANES


def _pack(parts):
    rows, offs, off = [], [], 0
    for a in parts:
        flat = a.reshape(-1).astype(F32)
        n = -(-flat.shape[0] // SLOT) * SLOT
        rows.append(jnp.pad(flat, (0, n - flat.shape[0])).reshape(n // LANES, LANES))
        offs.append(off)
        off += n // LANES
    return jnp.concatenate(rows, axis=0), offs


def _unpack(buf, off, like):
    n = like.size
    rows = -(-n // LANES)
    return buf[off:off + rows].reshape(-1)[:n].reshape(like.shape)


FIRST = ("w_in",)
REST = ("w_mix_out", "w_xq", "w_xkv", "w_xo", "w_up")
LAST = ("w_down",)


def _local_params(w):
    w_pool_bd = jnp.zeros((D_POOL, D_POOL), F32)
    for gi in range(4):
        w_pool_bd = w_pool_bd.at[64 * gi:64 * (gi + 1), 64 * gi:64 * (gi + 1)].set(w["w_pool"][0, gi])
    p = {n: w[n] for n in ("norm_mix_pre", "norm_mix_post", "norm_mem", "norm_xa_pre", "norm_xa_post", "norm_ffn_pre",
                           "norm_ffn_post")}
    p.update(
        bf_pad=jnp.pad(w["b_forget"], ((0, 0), (0, LANES - HEADS))),
        w_pool_bd=w_pool_bd.astype(BF16), w_pool_bd_t=w_pool_bd.T.astype(BF16), pool_scale=w["pool_scale"].reshape(1, D_POOL))
    return p


def _w_in_param(stacked):
    return jnp.pad(jnp.concatenate(list(stacked), axis=1), ((0, 0), (0, D_IN_PAD - D_IN)))


def _rest_params(w, full, conv_w_full):
    cw2 = conv_w_full.reshape(3, 2, D_FF).transpose(1, 0, 2)
    cwb = jnp.concatenate([cw2, w["conv_b"].reshape(1, 2, D_FF).transpose(1, 0, 2), jnp.zeros((2, 4, D_FF), F32)], axis=1)
    return dict(w_mix_out=full["w_mix_out"].reshape(D, D), w_xq=full["w_xq"].reshape(D, D), w_xkv=full["w_xkv"],
                w_xo=full["w_xo"].reshape(D, D), w_up=full["w_up"], cwb=cwb)


def _whole_params(w, full, conv_w_full):
    p = _local_params(w)
    p.update(_rest_params(w, full, conv_w_full), w_in=_w_in_param(full["w_in"]), w_down=full["w_down"].reshape(D_FF, D))
    return p


def _halved(a):
    return a.reshape(a.shape[:-2] + (2, a.shape[-2] // 2, a.shape[-1]))


class _StepComm:
    def __init__(self, w, shard2d, conv_w, core_id, chip_id):
        self.w, self.shard2d, self.conv_w, self.core_id, self.chip_id = w, shard2d, conv_w, core_id, chip_id
        self.first, self.second = ("w_up", "w_down"), ("w_xq", "w_xkv", "w_xo")
        self.early = self.first + self.second
        self.late = ("w_in", "w_mix_out")

    def gather_first(self):
        return _all_gather_weights([_halved(self.shard2d[n].astype(BF16)) for n in FIRST], [])

    def first_landed(self, p, landed):
        p["w_in"] = _w_in_param(landed[0].reshape((N_CHIPS,) + self.shard2d["w_in"].shape))

    def gather_rest(self, p):
        return _all_gather_weights([_halved(self.shard2d[n].astype(BF16)) for n in REST], [self.conv_w.reshape(3, -1)])

    def weights_landed(self, p, landed):
        full = {n: a.reshape((N_CHIPS,) + self.shard2d[n].shape) for n, a in zip(REST, landed)}
        conv_w_full = jnp.transpose(landed[-1], (1, 0, 2)).reshape(3, 2 * D_FF)
        p.update(_rest_params(self.w, full, conv_w_full))

    def gather_last(self):
        return _all_gather_weights([_halved(self.shard2d[n].astype(BF16)) for n in LAST], [])

    def last_landed(self, p, landed):
        p["w_down"] = landed[0].reshape(D_FF, D)

    def _view(self, g, n):
        return _halved(g[n].reshape((N_CHIPS,) + self.shard2d[n].shape))

    def swap_first(self, g):
        return _swap_halves([self._view(g, n) for n in self.first])

    def first_swapped(self, landed):
        self.from_sibling = dict(zip(self.first, landed))

    def swap_second(self, g):
        return _swap_halves([self._view(g, n) for n in self.second])

    def second_swapped(self, landed):
        self.from_sibling.update(zip(self.second, landed))

    def scatter_early(self, g):
        self.partial = [_chip_sum("chip_sum_" + n, self.core_id, self._view(g, n), self.from_sibling[n]) for n in self.early]
        return _scatter_chips(self.partial)

    def scatter_landed(self, landed):
        self.received = list(landed)

    def swap_reduced_early(self):
        self.reduced = [_mesh_sum("mesh_sum_" + n, self.chip_id, r, own)
                        for n, r, own in zip(self.early, self.received, self.partial)]
        return _swap_reduced(self.reduced)

    def reduced_landed(self, landed):
        self.reduced_sibling = list(landed)

    def scatter_late(self, g):
        views = [g["w_in"], self._view(g, "w_mix_out")]
        from_sibling = _swap_halves(views).run("swap_halves_late")
        self.partial_late = [_chip_sum("chip_sum_" + n, self.core_id, view, other)
                             for n, view, other in zip(self.late, views, from_sibling)]
        return _scatter_chips(self.partial_late)

    def late_landed(self, landed):
        self.received_late = list(landed)


def kernel(x, mem, norm_mix_pre, norm_mix_post, w_in, b_forget, w_pool, pool_scale, w_mix_out, norm_mem, norm_xa_pre, norm_xa_post, w_xq, w_xkv, w_xo, norm_ffn_pre, norm_ffn_post, w_up, conv_w, conv_b, w_down, loss_target, m_norm_mix_pre, m_norm_mix_post, m_w_in, m_b_forget, m_w_pool, m_pool_scale, m_w_mix_out, m_norm_mem, m_norm_xa_pre, m_norm_xa_post, m_w_xq, m_w_xkv, m_w_xo, m_norm_ffn_pre, m_norm_ffn_post, m_w_up, m_conv_w, m_conv_b, m_w_down, v_norm_mix_pre, v_norm_mix_post, v_w_in, v_b_forget, v_w_pool, v_pool_scale, v_w_mix_out, v_norm_mem, v_norm_xa_pre, v_norm_xa_post, v_w_xq, v_w_xkv, v_w_xo, v_norm_ffn_pre, v_norm_ffn_post, v_w_up, v_conv_w, v_conv_b, v_w_down):
    w = dict(norm_mix_pre=norm_mix_pre, norm_mix_post=norm_mix_post, w_in=w_in, b_forget=b_forget, w_pool=w_pool,
             pool_scale=pool_scale, w_mix_out=w_mix_out, norm_mem=norm_mem, norm_xa_pre=norm_xa_pre, norm_xa_post=norm_xa_post,
             w_xq=w_xq, w_xkv=w_xkv, w_xo=w_xo, norm_ffn_pre=norm_ffn_pre, norm_ffn_post=norm_ffn_post, w_up=w_up,
             conv_w=conv_w, conv_b=conv_b, w_down=w_down)
    m = dict(norm_mix_pre=m_norm_mix_pre, norm_mix_post=m_norm_mix_post, w_in=m_w_in, b_forget=m_b_forget, w_pool=m_w_pool,
             pool_scale=m_pool_scale, w_mix_out=m_w_mix_out, norm_mem=m_norm_mem, norm_xa_pre=m_norm_xa_pre,
             norm_xa_post=m_norm_xa_post, w_xq=m_w_xq, w_xkv=m_w_xkv, w_xo=m_w_xo, norm_ffn_pre=m_norm_ffn_pre,
             norm_ffn_post=m_norm_ffn_post, w_up=m_w_up, conv_w=m_conv_w, conv_b=m_conv_b, w_down=m_w_down)
    v = dict(norm_mix_pre=v_norm_mix_pre, norm_mix_post=v_norm_mix_post, w_in=v_w_in, b_forget=v_b_forget, w_pool=v_w_pool,
             pool_scale=v_pool_scale, w_mix_out=v_w_mix_out, norm_mem=v_norm_mem, norm_xa_pre=v_norm_xa_pre,
             norm_xa_post=v_norm_xa_post, w_xq=v_w_xq, w_xkv=v_w_xkv, w_xo=v_w_xo, norm_ffn_pre=v_norm_ffn_pre,
             norm_ffn_post=v_norm_ffn_post, w_up=v_w_up, conv_w=v_conv_w, conv_b=v_conv_b, w_down=v_w_down)
    chip = 2 * lax.axis_index("x") + lax.axis_index("y")

    core_id = lax.axis_index("c").astype(jnp.int32).reshape(1)
    chip_id = chip.astype(jnp.int32).reshape(1)

    shard2d = {n: w[n][0] for n in BIG}
    p = _local_params(w)
    comm = _StepComm(w, shard2d, conv_w, core_id, chip_id)
    grad_x, g, loss_cols = _local_step(x[0], mem[0], loss_target[0], p, comm)

    reduced_late = [_mesh_sum("mesh_sum_" + n, chip_id, r, own)
                    for n, r, own in zip(comm.late, comm.received_late, comm.partial_late)]
    names = comm.late + comm.early
    reduced = reduced_late + comm.reduced
    reduced_sibling = list(_swap_reduced(reduced_late).run("swap_reduced_late")) + comm.reduced_sibling
    grads = {}

    gw_pool = jnp.stack([g["w_pool_full"][64 * gi:64 * (gi + 1), 64 * gi:64 * (gi + 1)] for gi in range(4)])
    dcwb = g["cwb"]
    g_conv_w = dcwb[:, 0:3, :].transpose(1, 0, 2).reshape(3, 2 * D_FF)
    g_conv_b = dcwb[:, 3, :].reshape(2 * D_FF)
    small_g = dict(norm_mix_pre=g["norm_mix_pre"], norm_mix_post=g["norm_mix_post"], b_forget=g["bf_pad"][:, :HEADS],
                   w_pool=gw_pool, pool_scale=g["pool_scale"], norm_mem=g["norm_mem"], norm_xa_pre=g["norm_xa_pre"],
                   norm_xa_post=g["norm_xa_post"], norm_ffn_pre=g["norm_ffn_pre"], norm_ffn_post=g["norm_ffn_post"],
                   conv_b=g_conv_b)
    local_buf, offs = _pack([small_g[n] for n in SMALL] + [g_conv_w, loss_cols])

    delta, new_m, new_v = {}, {}, {}
    for n, g_mine, g_sibling in zip(names, reduced, reduced_sibling):
        cols = shard2d[n].shape[1]
        if cols % LANES:
            outs = _adamw_halves_columns("adamw_" + n, core_id, jnp.transpose(w[n], (2, 0, 1)), g_mine[:cols, None, :],
                                         g_sibling[:cols, None, :], jnp.transpose(m[n], (2, 0, 1)), jnp.transpose(v[n], (2, 0, 1)))
            gn, d, nm, nv = (jnp.transpose(o, (1, 2, 0)) for o in outs)
        else:
            gn, d, nm, nv = (o[None] for o in _adamw_halves("adamw_" + n, core_id, shard2d[n], g_mine, g_sibling, m[n][0], v[n][0]))
        grads[n], delta[n], new_m[n], new_v[n] = gn, d, nm, nv
    place = (2 * chip + lax.axis_index("c")).astype(jnp.int32).reshape(1)
    buf = _sum_devices(place, _gather_small(local_buf).run("gather_small")[0], local_buf)
    for n, off in zip(SMALL, offs):
        grads[n] = _unpack(buf, off, w[n])
    g_conv_w = _unpack(buf, offs[len(SMALL)], g_conv_w)
    grads["conv_w"] = lax.dynamic_slice_in_dim(g_conv_w, chip * (2 * D_FF // N_CHIPS), 2 * D_FF // N_CHIPS, axis=1).reshape(conv_w.shape)
    loss = jnp.sum(_unpack(buf, offs[len(SMALL) + 1], loss_cols))
    small_names = SMALL + ("conv_w",)
    packed = [_pack([d[n] for n in small_names])[0] for d in (w, grads, m, v)]
    offs = _pack([w[n] for n in small_names])[1]
    d, nm, nv = _adamw("adamw_small", *packed)
    for n, off in zip(small_names, offs):
        delta[n], new_m[n], new_v[n] = _unpack(d, off, w[n]), _unpack(nm, off, w[n]), _unpack(nv, off, w[n])

    return (loss, grad_x[None], *[grads[n] for n in ORDER], *[delta[n] for n in ORDER], *[new_m[n] for n in ORDER],
            *[new_v[n] for n in ORDER])
```

```python
import functools

import jax
import jax.numpy as jnp
import numpy as np
from jax import lax
from jax.experimental import pallas as pl
from jax.experimental.pallas import tpu as pltpu

F32 = jnp.float32
BF16 = jnp.bfloat16
MESH = pl.DeviceIdType.MESH
ANY = pl.BlockSpec(memory_space=pl.ANY)
VMEM_SPEC = pl.BlockSpec(memory_space=pltpu.VMEM)

S = 4096
D = 1024
MEM = 256
D_POOL = 256
HEADS = 12
DH = 64
D_FOX = HEADS * DH
D_IN = D_POOL + 3 * D_FOX + HEADS
F_OFF = D_POOL + 3 * D_FOX
Q_OFF, K_OFF, V_OFF = D_POOL, D_POOL + D_FOX, D_POOL + 2 * D_FOX
XA_HEADS = 4
XA_DH = 256
D_FF = 4096
EPS = 1e-6
N_CHIPS = 4
ADAM_LR, ADAM_B1, ADAM_B2, ADAM_EPS, ADAM_WD, ADAM_STEP = 0.001, 0.9, 0.999, 1e-08, 0.01, 10

LANES = 128
SUBLANES = 8
D_IN_PAD = 21 * LANES
TR = 512
TILE_BYTES = 2 * 1024 * 1024
NEG = -1e30
VMEM_LIMIT = 52 * 1024 * 1024

NN = (((1,), (0,)), ((), ()))
NT = (((1,), (1,)), ((), ()))
TN = (((0,), (0,)), ((), ()))


def _dot(a, b, dims=NN):
    return lax.dot_general(a, b, dims, preferred_element_type=F32)


def _params(sem):
    return pltpu.CompilerParams(dimension_semantics=sem, vmem_limit_bytes=VMEM_LIMIT)


def _split3(x):
    hi = x.astype(BF16)
    r = x - hi.astype(F32)
    mid = r.astype(BF16)
    lo = (r - mid.astype(F32)).astype(BF16)
    return hi, mid, lo


def _split3_f32(x):
    hi = x.astype(BF16).astype(F32)
    r = x - hi
    mid = r.astype(BF16).astype(F32)
    return hi, mid, r - mid


def _lane_iota(shape):
    return lax.broadcasted_iota(jnp.int32, shape, len(shape) - 1)


def _row_iota(shape):
    return lax.broadcasted_iota(jnp.int32, shape, len(shape) - 2)


def _mm(name, a, b, a_spec, b_spec, out_shape, out_spec, grid, dims, acc_shape, ex=None):
    nk = grid[2]
    if ex is not None:
        return _mm_hosting(name, a, b, a_spec, b_spec, out_shape, out_spec, grid, dims, ex)

    def body(a_ref, b_ref, o_ref, *scr):
        p = _dot(a_ref[...], b_ref[...], dims)
        if nk == 1:
            o_ref[...] = p.astype(o_ref.dtype)
        else:
            acc = scr[0]
            k = pl.program_id(2)

            @pl.when(k == 0)
            def _():
                acc[...] = p

            @pl.when(k > 0)
            def _():
                acc[...] += p

            @pl.when(k == nk - 1)
            def _():
                o_ref[...] = acc[...].astype(o_ref.dtype)

    return pl.pallas_call(
        body, name=name, grid=grid, in_specs=[a_spec, b_spec], out_specs=out_spec, out_shape=out_shape,
        scratch_shapes=[pltpu.VMEM(acc_shape, F32)] if nk > 1 else [],
        compiler_params=_params(("parallel", "parallel", "arbitrary")),
    )(a, b)


def _mm_hosting(name, a, b, a_spec, b_spec, out_shape, out_spec, grid, dims, ex):
    assert grid[2] == 1
    n = len(ex.ins)

    def body(*refs):
        i, j = pl.program_id(0), pl.program_id(1)
        last = (i == grid[0] - 1) & (j == grid[1] - 1)
        (a_ref, b_ref), (o_ref,), _, begin, end = _hosted(ex, refs, 2, 1, (i == 0) & (j == 0), last, last)
        begin()
        o_ref[...] = _dot(a_ref[...], b_ref[...], dims).astype(o_ref.dtype)
        end()

    res = pl.pallas_call(
        body, name=name, grid=grid, in_specs=[a_spec, b_spec] + [ANY] * n, out_specs=[out_spec] + [ANY] * n,
        out_shape=[out_shape] + ex.out_shapes, scratch_shapes=ex.scratch(),
        compiler_params=_params(("arbitrary", "arbitrary", "arbitrary")),
    )(a, b, *ex.ins)
    return res[0], res[1:]


def _mm_nn(name, a, b, out_dtype, tm, tn):
    m, k = a.shape
    n = b.shape[1]
    return _mm(name, a, b, pl.BlockSpec((tm, k), lambda i, j, kk: (i, 0)), pl.BlockSpec((k, tn), lambda i, j, kk: (0, j)),
               jax.ShapeDtypeStruct((m, n), out_dtype), pl.BlockSpec((tm, tn), lambda i, j, kk: (i, j)),
               (m // tm, n // tn, 1), NN, (tm, tn))


def _mm_nt(name, a, b, out_dtype, tm, tn, ex=None):
    m, k = a.shape
    n = b.shape[0]
    return _mm(name, a, b, pl.BlockSpec((tm, k), lambda i, j, kk: (i, 0)), pl.BlockSpec((tn, k), lambda i, j, kk: (j, 0)),
               jax.ShapeDtypeStruct((m, n), out_dtype), pl.BlockSpec((tm, tn), lambda i, j, kk: (i, j)),
               (m // tm, n // tn, 1), NT, (tm, tn), ex)


def _mm_tn(name, a, b, tka, tn, ex=None):
    t, ka = a.shape
    n = b.shape[1]
    return _mm(name, a, b, pl.BlockSpec((t, tka), lambda i, j, kk: (0, i)), pl.BlockSpec((t, tn), lambda i, j, kk: (0, j)),
               jax.ShapeDtypeStruct((ka, n), F32), pl.BlockSpec((tka, tn), lambda i, j, kk: (i, j)),
               (ka // tka, n // tn, 1), TN, (tka, tn), ex)


def _d_h3(dhid, w_up, ex):
    tm = tn = 1024
    shard = 2 * D_FF // N_CHIPS
    per_plane = D_FF // shard
    grid = (S // tm, D // tn, N_CHIPS)
    n = len(ex.ins)

    def body(*refs):
        i, j, k = pl.program_id(0), pl.program_id(1), pl.program_id(2)
        first = (i == 0) & (j == 0) & (k == 0)
        last = (i == grid[0] - 1) & (j == grid[1] - 1) & (k == N_CHIPS - 1)
        (a_ref, b_ref), (o_ref,), (acc_ref,), begin, end = _hosted(ex, refs, 2, 1, first, first, last)
        begin()
        part = _dot(a_ref[...], b_ref[...], NT)

        @pl.when(k == 0)
        def _():
            acc_ref[...] = part

        @pl.when(k > 0)
        def _():
            acc_ref[...] += part

        @pl.when(k == N_CHIPS - 1)
        def _():
            o_ref[...] = acc_ref[...]

        end()

    res = pl.pallas_call(
        body, name="d_h3", grid=grid,
        in_specs=[pl.BlockSpec((None, tm, shard), lambda i, j, k: (k // per_plane, i, k % per_plane)),
                  pl.BlockSpec((None, tn, shard), lambda i, j, k: (k, j, 0))] + [ANY] * n,
        out_specs=[pl.BlockSpec((tm, tn), lambda i, j, k: (i, j))] + [ANY] * n,
        out_shape=[jax.ShapeDtypeStruct((S, D), F32)] + ex.out_shapes,
        scratch_shapes=[pltpu.VMEM((tm, tn), F32)] + ex.scratch(),
        compiler_params=_params(("arbitrary", "arbitrary", "arbitrary")),
    )(dhid, w_up, *ex.ins)
    return res[0], res[1:]


SHARD_IN = D_IN // N_CHIPS
SHARD_IN_PAD = -(-SHARD_IN // SUBLANES) * SUBLANES


def _dw_in(dproj, h1, ex):
    tk = 1024
    nk = S // tk
    half = D // 2
    starts = [SHARD_IN * j // LANES * LANES for j in range(N_CHIPS)]
    shifts = [SHARD_IN * j - s for j, s in enumerate(starts)]
    window = -(-(max(shifts) + SHARD_IN) // LANES) * LANES
    assert starts[-1] + window <= dproj.shape[1]
    n = len(ex.ins)

    def body(*refs):
        k = pl.program_id(0)
        (a_ref, b_ref), (o_ref,), _, begin, end = _hosted(ex, refs, 2, 1, k == 0, k == nk - 1, k == nk - 1)
        begin()

        @pl.when(k == 0)
        def _():
            o_ref[...] = jnp.zeros(o_ref.shape, F32)

        for j in range(N_CHIPS):
            win = a_ref[:, starts[j]:starts[j] + window]
            if shifts[j]:
                win = pltpu.roll(win, window - shifts[j], axis=1)
            part = _dot(win, b_ref[...], TN)
            for h in range(2):
                o_ref[j, h] += part[:SHARD_IN_PAD, h * half:(h + 1) * half]
        end()

    out_shape = (N_CHIPS, 2, SHARD_IN_PAD, half)
    res = pl.pallas_call(
        body, name="dw_in", grid=(nk,),
        in_specs=[pl.BlockSpec((tk, dproj.shape[1]), lambda k: (k, 0)), pl.BlockSpec((tk, D), lambda k: (k, 0))] + [ANY] * n,
        out_specs=[pl.BlockSpec(out_shape, lambda k: (0, 0, 0, 0))] + [ANY] * n,
        out_shape=[jax.ShapeDtypeStruct(out_shape, F32)] + ex.out_shapes,
        scratch_shapes=ex.scratch(),
        compiler_params=_params(("arbitrary",)),
    )(dproj, h1, *ex.ins)
    return res[0], res[1:]


def _rms(x, g):
    r = lax.rsqrt(jnp.mean(x * x, axis=-1, keepdims=True) + EPS)
    return x * r * g


def _rms_bwd(x, g, dy):
    r = lax.rsqrt(jnp.mean(x * x, axis=-1, keepdims=True) + EPS)
    xh = x * r
    dxh = dy * g
    dx = r * (dxh - xh * jnp.mean(dxh * xh, axis=-1, keepdims=True))
    return dx, jnp.sum(dy * xh, axis=0, keepdims=True)


def _row_spec(tr, width):
    return pl.BlockSpec((tr, width), lambda i: (i, 0))


def _vec_spec(width):
    return pl.BlockSpec((1, width), lambda i: (0, 0))


def _norm_fwd(name, x, g, ex=None):
    rows, width = x.shape
    tr = min(TR, rows)
    steps = rows // tr
    hosted = ex if ex is not None else _no_exchange()
    n = len(hosted.ins)

    def body(*refs):
        i = pl.program_id(0)
        (x_ref, g_ref), (h_ref,), _, begin, end = _hosted(hosted, refs, 2, 1, i == 0, i == steps - 1, i == steps - 1)
        begin()
        h_ref[...] = _rms(x_ref[...], g_ref[...]).astype(BF16)
        end()

    res = pl.pallas_call(
        body, name=name, grid=(steps,), in_specs=[_row_spec(tr, width), _vec_spec(width)] + [ANY] * n,
        out_specs=[_row_spec(tr, width)] + [ANY] * n,
        out_shape=[jax.ShapeDtypeStruct((rows, width), BF16)] + hosted.out_shapes, scratch_shapes=hosted.scratch(),
        compiler_params=_params(("arbitrary",)),
    )(x, g, *hosted.ins)
    return res[0] if ex is None else (res[0], res[1:])


def _proj_resid_norm(name, a, w, xp, g_post, g_pre, w_next=None):
    def body(a_ref, w_ref, xp_ref, gpost_ref, gpre_ref, *rest):
        y_ref, xn_ref, h_ref = rest[-3:] if w_next is None else rest[1:4]
        y = _dot(a_ref[...], w_ref[...])
        y_ref[...] = y
        xn = xp_ref[...] + _rms(y, gpost_ref[...])
        xn_ref[...] = xn
        h = _rms(xn, gpre_ref[...]).astype(BF16)
        h_ref[...] = h
        if w_next is not None:
            rest[4][...] = _dot(h, rest[0][...]).astype(BF16)

    mat = pl.BlockSpec((D, D), lambda i: (0, 0))
    more = [] if w_next is None else [w_next]
    return pl.pallas_call(
        body, name=name, grid=(S // TR,),
        in_specs=[_row_spec(TR, D), mat, _row_spec(TR, D), _vec_spec(D), _vec_spec(D)] + [mat] * len(more),
        out_specs=[_row_spec(TR, D)] * (3 + len(more)),
        out_shape=[jax.ShapeDtypeStruct((S, D), F32), jax.ShapeDtypeStruct((S, D), F32), jax.ShapeDtypeStruct((S, D), BF16)]
        + [jax.ShapeDtypeStruct((S, D), BF16)] * len(more),
        compiler_params=_params(("parallel",)),
    )(a, w, xp, g_post, g_pre, *more)


def _down_loss_bwd(act, w_down, x3, g_post, target):
    def body(a_ref, w_ref, x_ref, g_ref, t_ref, dres_ref, dy_ref, dg_ref, loss_ref):
        i = pl.program_id(0)

        @pl.when(i == 0)
        def _():
            dg_ref[...] = jnp.zeros_like(dg_ref)
            loss_ref[...] = jnp.zeros_like(loss_ref)

        y = _dot(a_ref[...], w_ref[...])
        g = g_ref[...]
        e = x_ref[...] + _rms(y, g) - t_ref[...]
        loss_ref[...] += jnp.sum(e * e, axis=0, keepdims=True) * (0.5 / D)
        dres = e * (1.0 / D)
        dres_ref[...] = dres
        dy, dg = _rms_bwd(y, g, dres)
        dy_ref[...] = dy.astype(BF16)
        dg_ref[...] += dg

    return pl.pallas_call(
        body, name="down_loss_bwd", grid=(S // TR,),
        in_specs=[_row_spec(TR, D_FF), pl.BlockSpec((D_FF, D), lambda i: (0, 0)), _row_spec(TR, D), _vec_spec(D),
                  _row_spec(TR, D)],
        out_specs=[_row_spec(TR, D), _row_spec(TR, D), _vec_spec(D), _vec_spec(D)],
        out_shape=[jax.ShapeDtypeStruct((S, D), F32), jax.ShapeDtypeStruct((S, D), BF16),
                   jax.ShapeDtypeStruct((1, D), F32), jax.ShapeDtypeStruct((1, D), F32)],
        compiler_params=_params(("arbitrary",)),
    )(act, w_down, x3, g_post, target)


def _mid_bwd(name, dres, xcur, g_pre, dh, yprev, g_post, w):
    def body(dres_ref, x_ref, gpre_ref, dh_ref, y_ref, gpost_ref, w_ref, dx_ref, dy_ref, da_ref, dgpre_ref, dgpost_ref):
        i = pl.program_id(0)

        @pl.when(i == 0)
        def _():
            dgpre_ref[...] = jnp.zeros_like(dgpre_ref)
            dgpost_ref[...] = jnp.zeros_like(dgpost_ref)

        dxn, dgpre = _rms_bwd(x_ref[...], gpre_ref[...], dh_ref[...])
        dx = dres_ref[...] + dxn
        dx_ref[...] = dx
        dy, dgpost = _rms_bwd(y_ref[...], gpost_ref[...], dx)
        dy = dy.astype(BF16)
        dy_ref[...] = dy
        da_ref[...] = _dot(dy, w_ref[...], NT).astype(BF16)
        dgpre_ref[...] += dgpre
        dgpost_ref[...] += dgpost

    return pl.pallas_call(
        body, name=name, grid=(S // TR,),
        in_specs=[_row_spec(TR, D), _row_spec(TR, D), _vec_spec(D), _row_spec(TR, D), _row_spec(TR, D), _vec_spec(D),
                  pl.BlockSpec((D, D), lambda i: (0, 0))],
        out_specs=[_row_spec(TR, D), _row_spec(TR, D), _row_spec(TR, D), _vec_spec(D), _vec_spec(D)],
        out_shape=[jax.ShapeDtypeStruct((S, D), F32), jax.ShapeDtypeStruct((S, D), BF16), jax.ShapeDtypeStruct((S, D), BF16),
                   jax.ShapeDtypeStruct((1, D), F32), jax.ShapeDtypeStruct((1, D), F32)],
        compiler_params=_params(("arbitrary",)),
    )(dres, xcur, g_pre, dh, yprev, g_post, w)


def _first_bwd(dres, x, g, dh):
    def body(dres_ref, x_ref, g_ref, dh_ref, dx_ref, dg_ref):
        i = pl.program_id(0)

        @pl.when(i == 0)
        def _():
            dg_ref[...] = jnp.zeros_like(dg_ref)

        dxn, dg = _rms_bwd(x_ref[...], g_ref[...], dh_ref[...])
        dx_ref[...] = dres_ref[...] + dxn
        dg_ref[...] += dg

    return pl.pallas_call(
        body, name="first_bwd", grid=(S // TR,),
        in_specs=[_row_spec(TR, D), _row_spec(TR, D), _vec_spec(D), _row_spec(TR, D)],
        out_specs=[_row_spec(TR, D), _vec_spec(D)],
        out_shape=[jax.ShapeDtypeStruct((S, D), F32), jax.ShapeDtypeStruct((1, D), F32)],
        compiler_params=_params(("arbitrary",)),
    )(dres, x, g, dh)


def _gain_bwd(name, x, g, dy):
    rows, width = x.shape

    def body(x_ref, g_ref, dy_ref, dg_ref):
        _, dg = _rms_bwd(x_ref[...], g_ref[...], dy_ref[...])
        dg_ref[...] = dg

    return pl.pallas_call(
        body, name=name, grid=(1,), in_specs=[_row_spec(rows, width), _vec_spec(width), _row_spec(rows, width)],
        out_specs=_vec_spec(width), out_shape=jax.ShapeDtypeStruct((1, width), F32),
        compiler_params=_params(("arbitrary",)),
    )(x, g, dy)


CUM_Q = DH
CUM_K = DH + 3
LSE_Q = DH + 6
BOTH_ONE = DH + 9
DEN_V = DH
DELTA = DH + 1
PREP_TR = 256
PIECE_LANES = 16
FOX_FWD_BLOCK = 1024
FOX_BWD_BLOCK = 512


def _at(lane_of_even_head, h):
    return (lane_of_even_head + DH * (h % 2)) % LANES


def _data_lanes(lane, h):
    return lane >= DH if h % 2 else lane < DH


def _pair_block(ref, off, h):
    base = ((off + DH * h) // LANES) * LANES
    return ref[:, base:base + LANES]


def _cumsum_rows(x, tri, carry):
    hi, mid, lo = _split3(x)
    return _dot(tri, hi) + _dot(tri, mid) + _dot(tri, lo) + carry


def _in_proj(h1, w_in, bf_pad):
    tr = TR

    place_q = np.zeros((LANES, HEADS * LANES), np.float32)
    place_k = np.zeros((LANES, HEADS * LANES), np.float32)
    for h in range(HEADS):
        for piece in range(3):
            place_q[PIECE_LANES * piece + h, LANES * h + _at(CUM_Q, h) + piece] = 1.0
            place_k[PIECE_LANES * piece + h, LANES * h + _at(CUM_K, h) + piece] = -1.0

    def body(h_ref, w_ref, bf_ref, pq_ref, pk_ref, qa_ref, ka_ref, va_ref, u_ref, z_ref, carry_ref):
        i = pl.program_id(0)

        @pl.when(i == 0)
        def _():
            carry_ref[...] = jnp.zeros_like(carry_ref)

        proj = _dot(h_ref[...], w_ref[...])
        u_ref[...] = proj[:, :D_POOL]
        z_ref[...] = proj[:, F_OFF:F_OFF + LANES]
        lane = _lane_iota((tr, LANES))
        z = proj[:, F_OFF:F_OFF + LANES] + bf_ref[...]
        log_f = jnp.minimum(z, 0.0) - jnp.log(1.0 + jnp.exp(-jnp.abs(z)))
        log_f = jnp.where(lane < HEADS, log_f, 0.0)
        tri = jnp.where(_row_iota((tr, tr)) >= _lane_iota((tr, tr)), 1.0, 0.0).astype(BF16)
        cum = _cumsum_rows(log_f, tri, carry_ref[0:1, :])
        carry_ref[0:1, :] = cum[tr - 1:tr, :]
        c_hi, c_mid, c_lo = _split3_f32(cum)
        pieces = (c_hi + pltpu.roll(c_mid, PIECE_LANES, 1) + pltpu.roll(c_lo, 2 * PIECE_LANES, 1)).astype(BF16)
        cum_q = _dot(pieces, pq_ref[...])
        cum_k = _dot(pieces, pk_ref[...])

        def between(first, h):
            return (lane >= _at(first, h)) & (lane < _at(first, h) + 3)

        ones_q = [jnp.where(between(CUM_K, h) | (lane == _at(BOTH_ONE, h)), 1.0, 0.0) for h in range(2)]
        ones_k = [jnp.where(between(CUM_Q, h) | between(LSE_Q, h) | (lane == _at(BOTH_ONE, h)), 1.0, 0.0) for h in range(2)]
        aug_v = [jnp.where(lane == _at(DEN_V, h), 1.0, jnp.where(between(DELTA, h), -1.0, 0.0)) for h in range(2)]
        for h in range(HEADS):
            mine = slice(LANES * h, LANES * (h + 1))
            data = _data_lanes(lane, h)
            qa_ref[h] = jnp.where(data, _pair_block(proj, Q_OFF, h) * (DH ** -0.5), cum_q[:, mine] + ones_q[h % 2]).astype(BF16)
            ka_ref[h] = jnp.where(data, _pair_block(proj, K_OFF, h), cum_k[:, mine] + ones_k[h % 2]).astype(BF16)
            va_ref[h] = jnp.where(data, _pair_block(proj, V_OFF, h), aug_v[h % 2]).astype(BF16)

    head_spec = pl.BlockSpec((HEADS, tr, LANES), lambda i: (0, i, 0))
    head_shape = jax.ShapeDtypeStruct((HEADS, S, LANES), BF16)
    place_spec = pl.BlockSpec(place_q.shape, lambda i: (0, 0))
    return pl.pallas_call(
        body, name="in_proj", grid=(S // tr,),
        in_specs=[_row_spec(tr, D), pl.BlockSpec((D, D_IN_PAD), lambda i: (0, 0)), _vec_spec(LANES), place_spec, place_spec],
        out_specs=[head_spec] * 3 + [_row_spec(tr, D_POOL), _row_spec(tr, LANES)],
        out_shape=[head_shape] * 3 + [jax.ShapeDtypeStruct((S, D_POOL), F32), jax.ShapeDtypeStruct((S, LANES), F32)],
        scratch_shapes=[pltpu.VMEM((SUBLANES, LANES), F32)], compiler_params=_params(("arbitrary",)),
    )(h1, w_in, bf_pad, jnp.asarray(place_q, BF16), jnp.asarray(place_k, BF16))


def _hosted(ex, refs, n_blocked_in, n_blocked_out, first, forward_at, last):
    n = len(ex.ins)
    own_in = refs[:n_blocked_in]
    ex_in = refs[n_blocked_in:n_blocked_in + n]
    own_out = refs[n_blocked_in + n:n_blocked_in + n + n_blocked_out]
    ex_out = refs[n_blocked_in + n + n_blocked_out:n_blocked_in + 2 * n + n_blocked_out]
    rest = refs[n_blocked_in + 2 * n + n_blocked_out:]
    args = (ex_in, ex_out, rest[-2], rest[-1])

    def begin():
        @pl.when(first)
        def _():
            ex.start(*args)

        @pl.when(forward_at)
        def _():
            ex.forward(*args)

    def end():
        @pl.when(last)
        def _():
            ex.finish(*args)

    return own_in, own_out, rest[:-2], begin, end


def _fox_fwd(qa, ka, va, ex):
    BQ = BK = FOX_FWD_BLOCK
    nq = S // BQ
    n_pairs = HEADS // 2

    def body(*refs):
        p_id, i = pl.program_id(0), pl.program_id(1)
        (qa_ref, ka_ref, va_ref), (y_ref, qab_ref), (m_scr, acc_scr), begin, end = _hosted(
            ex, refs, 3, 2, (p_id == 0) & (i == 0), (p_id == n_pairs - 1) & (i == 0), (p_id == n_pairs - 1) & (i == nq - 1))
        begin()
        lane = _lane_iota((BQ, LANES))
        causal = _row_iota((BQ, BK)) >= _lane_iota((BQ, BK))
        m_scr[...] = jnp.full_like(m_scr, NEG)
        acc_scr[...] = jnp.zeros_like(acc_scr)

        def step(j, masked):
            rows = pl.ds(pl.multiple_of(j * BK, BK), BK)
            for hh in range(2):
                s = _dot(qa_ref[hh], ka_ref[hh, rows, :], NT)
                if masked:
                    s = jnp.where(causal, s, NEG)
                m_prev = m_scr[hh]
                m_new = jnp.maximum(m_prev, jnp.max(s, axis=1, keepdims=True))
                p = jnp.exp(s - jnp.tile(m_new, (1, BK // LANES)))
                acc_scr[hh] = jnp.exp(m_prev - m_new) * acc_scr[hh] + _dot(p.astype(BF16), va_ref[hh, rows, :])
                m_scr[hh] = m_new

        def full_step(j, carry):
            step(j, False)
            return carry

        lax.fori_loop(0, i, full_step, 0)
        step(i, True)
        outs = []
        for hh in range(2):
            acc = acc_scr[hh]
            den_lane, lse_lane = _at(DEN_V, hh), _at(LSE_Q, hh)
            den = jnp.broadcast_to(acc[:, den_lane:den_lane + 1], (BQ, LANES))
            outs.append(acc * (1.0 / den))
            n_hi, n_mid, n_lo = _split3(-(m_scr[hh] + jnp.log(den)))
            qab_ref[hh] = jnp.where(lane == lse_lane, n_hi,
                                    jnp.where(lane == lse_lane + 1, n_mid, jnp.where(lane == lse_lane + 2, n_lo, qa_ref[hh])))
        y_ref[...] = jnp.where(lane < DH, outs[0], outs[1]).astype(BF16)
        end()

    pair_rows = pl.BlockSpec((2, BQ, LANES), lambda p, i: (p, i, 0))
    pair_all = pl.BlockSpec((2, S, LANES), lambda p, i: (p, 0, 0))
    n = len(ex.ins)
    res = pl.pallas_call(
        body, name="fox_fwd", grid=(n_pairs, nq), in_specs=[pair_rows, pair_all, pair_all] + [ANY] * n,
        out_specs=[pl.BlockSpec((BQ, LANES), lambda p, i: (i, D_POOL // LANES + p)), pair_rows] + [ANY] * n,
        out_shape=[jax.ShapeDtypeStruct((S, D), BF16), jax.ShapeDtypeStruct((HEADS, S, LANES), BF16)] + ex.out_shapes,
        scratch_shapes=[pltpu.VMEM((2, BQ, LANES), F32), pltpu.VMEM((2, BQ, LANES), F32)] + ex.scratch(),
        compiler_params=_params(("arbitrary", "arbitrary")),
    )(qa, ka, va, *ex.ins)
    return res[0], res[1], res[2:]


def _bwd_xa_mix(dqx, w_xq, dres, x2, g_pre, y1, g_post, w_mix_out, ycat, ex):
    steps = S // TR
    n = len(ex.ins)

    def body(*refs):
        i = pl.program_id(0)
        ((dq_ref, wq_ref, dres_ref, x_ref, gpre_ref, y_ref, gpost_ref, wm_ref, ycat_ref),
         (dx_ref, dy_ref, dgpre_ref, dgpost_ref, dp_ref, doa_ref), _, begin, end) = _hosted(
            ex, refs, 9, 6, i == 0, i == 0, i == steps - 1)
        begin()

        @pl.when(i == 0)
        def _():
            dgpre_ref[...] = jnp.zeros_like(dgpre_ref)
            dgpost_ref[...] = jnp.zeros_like(dgpost_ref)

        dxn, dgpre = _rms_bwd(x_ref[...], gpre_ref[...], _dot(dq_ref[...], wq_ref[...], NT))
        dx = dres_ref[...] + dxn
        dx_ref[...] = dx
        dy, dgpost = _rms_bwd(y_ref[...], gpost_ref[...], dx)
        dy = dy.astype(BF16)
        dy_ref[...] = dy
        dgpre_ref[...] += dgpre
        dgpost_ref[...] += dgpost

        d = _dot(dy, wm_ref[...], NT)
        dp_ref[...] = d[:, :D_POOL]
        lane = _lane_iota((TR, LANES))
        low = lane < DH
        for p in range(HEADS // 2):
            cols = slice(D_POOL + LANES * p, D_POOL + LANES * (p + 1))
            do = d[:, cols]
            prod = do * ycat_ref[:, cols].astype(F32)
            deltas = (jnp.sum(jnp.where(low, prod, 0.0), axis=1, keepdims=True),
                      jnp.sum(jnp.where(low, 0.0, prod), axis=1, keepdims=True))
            for hh in range(2):
                d_hi, d_mid, d_lo = _split3_f32(deltas[hh])
                dl = _at(DELTA, hh)
                aug = jnp.where(lane == dl, d_hi, jnp.where(lane == dl + 1, d_mid, jnp.where(lane == dl + 2, d_lo, 0.0)))
                doa_ref[2 * p + hh] = jnp.where(_data_lanes(lane, hh), do, aug).astype(BF16)
        end()

    mat = pl.BlockSpec((D, D), lambda i: (0, 0))
    res = pl.pallas_call(
        body, name="bwd_xa_mix", grid=(steps,),
        in_specs=[_row_spec(TR, D), mat, _row_spec(TR, D), _row_spec(TR, D), _vec_spec(D), _row_spec(TR, D), _vec_spec(D), mat,
                  _row_spec(TR, D)] + [ANY] * n,
        out_specs=[_row_spec(TR, D), _row_spec(TR, D), _vec_spec(D), _vec_spec(D), _row_spec(TR, D_POOL),
                   pl.BlockSpec((HEADS, TR, LANES), lambda i: (0, i, 0))] + [ANY] * n,
        out_shape=[jax.ShapeDtypeStruct((S, D), F32), jax.ShapeDtypeStruct((S, D), BF16), jax.ShapeDtypeStruct((1, D), F32),
                   jax.ShapeDtypeStruct((1, D), F32), jax.ShapeDtypeStruct((S, D_POOL), F32),
                   jax.ShapeDtypeStruct((HEADS, S, LANES), BF16)] + ex.out_shapes,
        scratch_shapes=ex.scratch(), compiler_params=_params(("arbitrary",)),
    )(dqx, w_xq, dres, x2, g_pre, y1, g_post, w_mix_out, ycat, *ex.ins)
    return res[:6], res[6:]


def _fox_bwd(qab, doa, ka, va, ex):
    BQ = BK = FOX_BWD_BLOCK
    nk = S // BK
    n_pairs = HEADS // 2

    def body(*refs):
        p_id, j = pl.program_id(0), pl.program_id(1)
        (qab_ref, doa_ref, ka_ref, va_ref), (dqa_ref, dka_ref, dva_ref), _, begin, end = _hosted(
            ex, refs, 4, 3, (p_id == 0) & (j == 0), (p_id == n_pairs - 1) & (j == 0), (p_id == n_pairs - 1) & (j == nk - 1))
        begin()

        @pl.when(j == 0)
        def _():
            dqa_ref[...] = jnp.zeros_like(dqa_ref)

        causal = _row_iota((BQ, BK)) >= _lane_iota((BQ, BK))
        dka_ref[...] = jnp.zeros_like(dka_ref)
        dva_ref[...] = jnp.zeros_like(dva_ref)

        def step(i, masked):
            rows = pl.ds(pl.multiple_of(i * BQ, BQ), BQ)
            for hh in range(2):
                kb = ka_ref[hh]
                q = qab_ref[hh, rows, :]
                do = doa_ref[hh, rows, :]
                s = _dot(q, kb, NT)
                if masked:
                    s = jnp.where(causal, s, NEG)
                p = jnp.exp(s)
                ds = p * _dot(do, va_ref[hh], NT)
                pb = p.astype(BF16)
                dsb = ds.astype(BF16)
                dva_ref[hh] += _dot(pb, do, TN)
                dka_ref[hh] += _dot(dsb, q, TN)
                dqa_ref[hh, rows, :] += _dot(dsb, kb)

        def full_step(i, carry):
            step(i, False)
            return carry

        step(j, True)
        lax.fori_loop(j + 1, nk, full_step, 0)
        end()

    pair_all = pl.BlockSpec((2, S, LANES), lambda p, j: (p, 0, 0))
    pair_rows = pl.BlockSpec((2, BK, LANES), lambda p, j: (p, j, 0))
    shape = jax.ShapeDtypeStruct((HEADS, S, LANES), F32)
    n = len(ex.ins)
    res = pl.pallas_call(
        body, name="fox_bwd", grid=(n_pairs, nk), in_specs=[pair_all, pair_all, pair_rows, pair_rows] + [ANY] * n,
        out_specs=[pair_all, pair_rows, pair_rows] + [ANY] * n, out_shape=[shape] * 3 + ex.out_shapes,
        scratch_shapes=ex.scratch(), compiler_params=_params(("arbitrary", "arbitrary")),
    )(qab, doa, ka, va, *ex.ins)
    return res[0], res[1], res[2], res[3:]


def _fox_bwd_post(dqa, dka, dva, du, proj, bf_pad):
    tr = PREP_TR
    nt = S // tr

    pick = np.zeros((HEADS * LANES, LANES), np.float32)
    for h in range(HEADS):
        pick[LANES * h + _at(BOTH_ONE, h), h] = 1.0

    def body(dqa_ref, dka_ref, dva_ref, du_ref, z_ref, bf_ref, pick_ref, dp_ref, dbf_ref, carry_ref):
        i = pl.program_id(0)

        @pl.when(i == 0)
        def _():
            carry_ref[...] = jnp.zeros_like(carry_ref)
            dbf_ref[...] = jnp.zeros_like(dbf_ref)

        lane = _lane_iota((tr, LANES))
        diff = jnp.concatenate([dqa_ref[h] - dka_ref[h] for h in range(HEADS)], axis=1)
        hi = diff.astype(BF16)
        dcum = _dot(hi, pick_ref[...]) + _dot((diff - hi.astype(F32)).astype(BF16), pick_ref[...])
        tri =jnp.where(_lane_iota((tr, tr)) >= _row_iota((tr, tr)), 1.0, 0.0).astype(BF16)
        dlog_f = _cumsum_rows(dcum, tri, carry_ref[0:1, :])
        carry_ref[0:1, :] = dlog_f[0:1, :]
        z = z_ref[...] + bf_ref[...]
        df = jnp.where(lane < HEADS, dlog_f / (1.0 + jnp.exp(z)), 0.0)
        dbf_ref[...] += jnp.sum(df, axis=0, keepdims=True)

        dp_ref[:, 0:D_POOL] = du_ref[...].astype(BF16)
        low = lane < DH
        for ref, off, scale in ((dqa_ref, Q_OFF, DH ** -0.5), (dka_ref, K_OFF, 1.0), (dva_ref, V_OFF, 1.0)):
            for p in range(HEADS // 2):
                blk = jnp.where(low, ref[2 * p], ref[2 * p + 1])
                dp_ref[:, off + LANES * p:off + LANES * (p + 1)] = (blk * scale).astype(BF16)
        dp_ref[:, F_OFF:F_OFF + LANES] = df.astype(BF16)

    head_spec = pl.BlockSpec((HEADS, tr, LANES), lambda i: (0, nt - 1 - i, 0))
    return pl.pallas_call(
        body, name="fox_bwd_post", grid=(nt,),
        in_specs=[head_spec, head_spec, head_spec, pl.BlockSpec((tr, D_POOL), lambda i: (nt - 1 - i, 0)),
                  pl.BlockSpec((tr, LANES), lambda i: (nt - 1 - i, 0)), _vec_spec(LANES),
                  pl.BlockSpec(pick.shape, lambda i: (0, 0))],
        out_specs=[pl.BlockSpec((tr, D_IN_PAD), lambda i: (nt - 1 - i, 0)), _vec_spec(LANES)],
        out_shape=[jax.ShapeDtypeStruct((S, D_IN_PAD), BF16), jax.ShapeDtypeStruct((1, LANES), F32)],
        scratch_shapes=[pltpu.VMEM((SUBLANES, LANES), F32)],
        compiler_params=_params(("arbitrary",)),
    )(dqa, dka, dva, du, proj, bf_pad, jnp.asarray(pick, BF16))


POOL_HALO = 16


def _by_group(lane, a2, a4, a8, a16):
    return jnp.where(lane < 64, a2, jnp.where(lane < 128, a4, jnp.where(lane < 192, a8, a16)))


def _window_count(lane, t):
    return jnp.minimum(t + 1, _by_group(lane, 2, 4, 8, 16)).astype(F32)


def _pool_diff(u, halo, first, tile):
    n = TR + POOL_HALO
    ext = jnp.concatenate([jnp.where(first, 0.0, halo), u], axis=0)
    s2 = ext + pltpu.roll(ext, 1, 0)
    s4 = s2 + pltpu.roll(s2, 2, 0)
    s8 = s4 + pltpu.roll(s4, 4, 0)
    s16 = s8 + pltpu.roll(s8, 8, 0)
    lane = _lane_iota((n, D_POOL))
    win = _by_group(lane, s2, s4, s8, s16)[POOL_HALO:]
    lane = _lane_iota((TR, D_POOL))
    t = tile * TR + _row_iota((TR, D_POOL))
    return win / _window_count(lane, t) - u


def _prev_halo(rows, width, col):
    per = TR // rows
    return pl.BlockSpec((rows, width), lambda i: (jnp.maximum(i * per - 1, 0), col))


def _next_halo(rows, width, col):
    per = TR // rows
    return pl.BlockSpec((rows, width), lambda i: (jnp.minimum((i + 1) * per, S // rows - 1), col))


def _pool_fwd(proj, w_bd, ps, ycat):
    def body(u_ref, halo_ref, w_ref, ps_ref, ycat_ref, y_ref):
        i = pl.program_id(0)
        diff = _pool_diff(u_ref[...], halo_ref[...], i == 0, i)
        y_ref[...] = (_dot(diff.astype(BF16), w_ref[...]) * ps_ref[...]).astype(BF16)

    return pl.pallas_call(
        body, name="pool_fwd", grid=(S // TR,),
        in_specs=[_row_spec(TR, D_POOL), _prev_halo(POOL_HALO, D_POOL, 0),
                  pl.BlockSpec((D_POOL, D_POOL), lambda i: (0, 0)), _vec_spec(D_POOL), ANY],
        out_specs=_row_spec(TR, D_POOL), out_shape=jax.ShapeDtypeStruct((S, D), BF16), input_output_aliases={4: 0},
        compiler_params=_params(("parallel",)),
    )(proj, proj, w_bd, ps, ycat)


def _pool_bwd(proj, dycat, w_bd, w_bd_t, ps):
    nt = S // TR
    n = TR + POOL_HALO

    def body(u_ref, halo_ref, dy_ref, dyn_ref, w_ref, wt_ref, ps_ref, du_ref, dw_ref, dps_ref):
        i = pl.program_id(0)

        @pl.when(i == 0)
        def _():
            dw_ref[...] = jnp.zeros_like(dw_ref)
            dps_ref[...] = jnp.zeros_like(dps_ref)

        diff = _pool_diff(u_ref[...], halo_ref[...], i == 0, i).astype(BF16)
        dy = dy_ref[...]
        dps_ref[...] += jnp.sum(dy * _dot(diff, w_ref[...]), axis=0, keepdims=True)
        dy_ext = jnp.concatenate([dy, jnp.where(i == nt - 1, 0.0, dyn_ref[...])], axis=0)
        dmixed = (dy_ext * ps_ref[...]).astype(BF16)
        ddiff = _dot(dmixed, wt_ref[...])
        dw_ref[...] += _dot(diff, dmixed[:TR], TN)
        lane = _lane_iota((n, D_POOL))
        t = i * TR + _row_iota((n, D_POOL))
        e = ddiff / _window_count(lane, t)
        f2 = e + pltpu.roll(e, n - 1, 0)
        f4 = f2 + pltpu.roll(f2, n - 2, 0)
        f8 = f4 + pltpu.roll(f4, n - 4, 0)
        f16 = f8 + pltpu.roll(f8, n - 8, 0)
        du_ref[...] = _by_group(lane, f2, f4, f8, f16)[:TR] - ddiff[:TR]

    mat = pl.BlockSpec((D_POOL, D_POOL), lambda i: (0, 0))
    return pl.pallas_call(
        body, name="pool_bwd", grid=(nt,),
        in_specs=[_row_spec(TR, D_POOL), _prev_halo(POOL_HALO, D_POOL, 0), _row_spec(TR, D_POOL),
                  _next_halo(POOL_HALO, D_POOL, 0), mat, mat, _vec_spec(D_POOL)],
        out_specs=[_row_spec(TR, D_POOL), mat, _vec_spec(D_POOL)],
        out_shape=[jax.ShapeDtypeStruct((S, D_POOL), F32), jax.ShapeDtypeStruct((D_POOL, D_POOL), F32),
                   jax.ShapeDtypeStruct((1, D_POOL), F32)],
        compiler_params=_params(("arbitrary",)),
    )(proj, proj, dycat, dycat, w_bd, w_bd_t, ps)


def _xa_probs(q, k):
    s = _dot(q, k, NT) * (XA_DH ** -0.5)
    e = jnp.exp(s - jnp.max(s, axis=-1, keepdims=True))
    return e * (1.0 / jnp.sum(e, axis=-1, keepdims=True))


def _xattn_fwd(qx, kv):
    def body(q_ref, kv_ref, o_ref):
        for h in range(XA_HEADS):
            cols = slice(XA_DH * h, XA_DH * (h + 1))
            vcols = slice(D + XA_DH * h, D + XA_DH * (h + 1))
            p = _xa_probs(q_ref[:, cols], kv_ref[:, cols])
            o_ref[:, cols] = _dot(p.astype(BF16), kv_ref[:, vcols]).astype(BF16)

    return pl.pallas_call(
        body, name="xattn_fwd", grid=(S // TR,),
        in_specs=[_row_spec(TR, D), pl.BlockSpec((MEM, 2 * D), lambda i: (0, 0))],
        out_specs=_row_spec(TR, D), out_shape=jax.ShapeDtypeStruct((S, D), BF16),
        compiler_params=_params(("parallel",)),
    )(qx, kv)


def _xattn_bwd(qx, kv, dxo):
    def body(q_ref, kv_ref, do_ref, dq_ref, dkv_ref):
        i = pl.program_id(0)

        @pl.when(i == 0)
        def _():
            dkv_ref[...] = jnp.zeros_like(dkv_ref)

        for h in range(XA_HEADS):
            cols = slice(XA_DH * h, XA_DH * (h + 1))
            vcols = slice(D + XA_DH * h, D + XA_DH * (h + 1))
            q = q_ref[:, cols]
            k = kv_ref[:, cols]
            do = do_ref[:, cols]
            p = _xa_probs(q, k)
            dkv_ref[:, vcols] += _dot(p.astype(BF16), do, TN)
            dp = _dot(do, kv_ref[:, vcols], NT)
            ds = (p * (dp - jnp.sum(p * dp, axis=-1, keepdims=True)) * (XA_DH ** -0.5)).astype(BF16)
            dq_ref[:, cols] = _dot(ds, k).astype(BF16)
            dkv_ref[:, cols] += _dot(ds, q, TN)

    kv_spec = pl.BlockSpec((MEM, 2 * D), lambda i: (0, 0))
    return pl.pallas_call(
        body, name="xattn_bwd", grid=(S // TR,), in_specs=[_row_spec(TR, D), kv_spec, _row_spec(TR, D)],
        out_specs=[_row_spec(TR, D), kv_spec],
        out_shape=[jax.ShapeDtypeStruct((S, D), BF16), jax.ShapeDtypeStruct((MEM, 2 * D), F32)],
        compiler_params=_params(("arbitrary",)),
    )(qx, kv, dxo)


CONV_HALO = SUBLANES
TC = 512
TC_FWD = 1024
GELU_K = 0.7978845608028654
GELU_C = 0.044715


def _conv3(ext, w, rows):
    h0 = ext[CONV_HALO:CONV_HALO + rows]
    h1 = pltpu.roll(ext, 1, 0)[CONV_HALO:CONV_HALO + rows]
    h2 = pltpu.roll(ext, 2, 0)[CONV_HALO:CONV_HALO + rows]
    return w[2:3] * h0 + w[1:2] * h1 + w[0:1] * h2 + w[3:4], (h2, h1, h0)


def _conv_specs(tc):
    main = pl.BlockSpec((2, TR, tc), lambda j, i: (0, i, j))
    per = TR // CONV_HALO
    prev = pl.BlockSpec((2, CONV_HALO, tc), lambda j, i: (0, jnp.maximum(i * per - 1, 0), j))
    nxt = pl.BlockSpec((2, CONV_HALO, tc), lambda j, i: (0, jnp.minimum((i + 1) * per, S // CONV_HALO - 1), j))
    par = pl.BlockSpec((2, SUBLANES, tc), lambda j, i: (0, 0, j))
    return main, prev, nxt, par


def _convgate_fwd(hid, cwb):
    tc = TC_FWD

    def body(h_ref, hp_ref, w_ref, act_ref):
        i = pl.program_id(1)
        c = []
        for g in range(2):
            ext = jnp.concatenate([jnp.where(i == 0, 0.0, hp_ref[g]), h_ref[g]], axis=0)
            c.append(_conv3(ext, w_ref[g], TR)[0])
        gate, up = c
        act_ref[...] = (jax.nn.gelu(gate, approximate=True) * up).astype(BF16)

    main, prev, _, par = _conv_specs(tc)
    return pl.pallas_call(
        body, name="convgate_fwd", grid=(D_FF // tc, S // TR), in_specs=[main, prev, par],
        out_specs=pl.BlockSpec((TR, tc), lambda j, i: (i, j)), out_shape=jax.ShapeDtypeStruct((S, D_FF), BF16),
        compiler_params=_params(("parallel", "parallel")),
    )(hid, hid, cwb)


def _convgate_bwd(hid, dact, cwb):
    nr = S // TR
    n = TR + CONV_HALO

    def body(h_ref, hp_ref, hn_ref, da_ref, dan_ref, w_ref, dh_ref, dw_ref):
        i = pl.program_id(1)

        @pl.when(i == 0)
        def _():
            dw_ref[...] = jnp.zeros_like(dw_ref)

        da = jnp.concatenate([da_ref[...], jnp.where(i == nr - 1, 0.0, dan_ref[...])], axis=0)
        c, taps = [], []
        for g in range(2):
            ext = jnp.concatenate([jnp.where(i == 0, 0.0, hp_ref[g]), h_ref[g], hn_ref[g]], axis=0)
            cg, tg = _conv3(ext, w_ref[g], n)
            c.append(cg)
            taps.append(tg)
        gate, up = c
        th = jnp.tanh(GELU_K * (gate + GELU_C * gate * gate * gate))
        gelu = 0.5 * gate * (1.0 + th)
        dgelu = 0.5 * (1.0 + th) + 0.5 * gate * (1.0 - th * th) * GELU_K * (1.0 + 3.0 * GELU_C * gate * gate)
        for g, dc in enumerate((da * up * dgelu, da * gelu)):
            w = w_ref[g]
            dh = w[2:3] * dc[:TR] + w[1:2] * pltpu.roll(dc, n - 1, 0)[:TR] + w[0:1] * pltpu.roll(dc, n - 2, 0)[:TR]
            dh_ref[g] = dh.astype(BF16)
            dcm = dc[:TR]
            for r in range(3):
                dw_ref[g, r:r + 1, :] += jnp.sum(dcm * taps[g][r][:TR], axis=0, keepdims=True)
            dw_ref[g, 3:4, :] += jnp.sum(dcm, axis=0, keepdims=True)

    main, prev, nxt, par = _conv_specs(TC)
    per = TR // CONV_HALO
    return pl.pallas_call(
        body, name="convgate_bwd", grid=(D_FF // TC, nr),
        in_specs=[main, prev, nxt, pl.BlockSpec((TR, TC), lambda j, i: (i, j)),
                  pl.BlockSpec((CONV_HALO, TC), lambda j, i: (jnp.minimum((i + 1) * per, S // CONV_HALO - 1), j)), par],
        out_specs=[main, par],
        out_shape=[jax.ShapeDtypeStruct((2, S, D_FF), BF16), jax.ShapeDtypeStruct((2, SUBLANES, D_FF), F32)],
        compiler_params=_params(("parallel", "arbitrary")),
    )(hid, hid, hid, dact, dact, cwb)


def _adam_update(w, g, m, v):
    m = ADAM_B1 * m + (1.0 - ADAM_B1) * g
    v = ADAM_B2 * v + (1.0 - ADAM_B2) * (g * g)
    m_hat = m / (1.0 - ADAM_B1 ** ADAM_STEP)
    v_hat = v / (1.0 - ADAM_B2 ** ADAM_STEP)
    return -ADAM_LR * (m_hat / (jnp.sqrt(v_hat) + ADAM_EPS) + ADAM_WD * w), m, v


def _row_tile(rows, cols, itemsize=4, target=TILE_BYTES):
    tr = SUBLANES
    while rows % (2 * tr) == 0 and 2 * tr * cols * itemsize <= target:
        tr *= 2
    assert rows % tr == 0, (rows, tr)
    return rows if rows % (2 * tr) and 16 * tr * cols * itemsize < target else tr


def _adamw(name, w, g, m, v):
    rows, cols = w.shape
    tr = rows if rows * cols * 4 <= TILE_BYTES // 2 else _row_tile(rows, cols, target=TILE_BYTES // 2)

    def body(w_ref, g_ref, m_ref, v_ref, d_ref, nm_ref, nv_ref):
        d_ref[...], nm_ref[...], nv_ref[...] = _adam_update(w_ref[...], g_ref[...], m_ref[...], v_ref[...])

    spec = _row_spec(tr, cols)
    shape = jax.ShapeDtypeStruct((rows, cols), F32)
    return pl.pallas_call(
        body, name=name, grid=(rows // tr,), in_specs=[spec] * 4, out_specs=[spec] * 3, out_shape=[shape] * 3,
        compiler_params=_params(("parallel",)),
    )(w, g, m, v)


def _adamw_halves(name, core, w, g_mine, g_sibling, m, v):
    rows, cols = w.shape
    half = rows // 2
    tr = _row_tile(half, cols, target=TILE_BYTES // 2)
    per = half // tr

    def body(core_ref, w_ref, gm_ref, gs_ref, m_ref, v_ref, g_ref, d_ref, nm_ref, nv_ref):
        g = jnp.where(pl.program_id(0) // per == core_ref[0], gm_ref[...], gs_ref[...])
        g_ref[...] = g
        d_ref[...], nm_ref[...], nv_ref[...] = _adam_update(w_ref[...], g, m_ref[...], v_ref[...])

    spec = pl.BlockSpec((tr, cols), lambda i, core_ref: (i, 0))
    half_spec = pl.BlockSpec((tr, cols), lambda i, core_ref: (i % per, 0))
    shape = jax.ShapeDtypeStruct((rows, cols), F32)
    return pl.pallas_call(
        body, name=name, out_shape=[shape] * 4,
        grid_spec=pltpu.PrefetchScalarGridSpec(
            num_scalar_prefetch=1, grid=(rows // tr,), in_specs=[spec, half_spec, half_spec, spec, spec], out_specs=[spec] * 4),
        compiler_params=_params(("parallel",)),
    )(core, w, g_mine, g_sibling, m, v)


def _adamw_halves_columns(name, core, w, g_mine, g_sibling, m, v):
    cols, _, rows = w.shape
    tl = 2 * LANES
    per = rows // 2 // tl

    def body(core_ref, w_ref, gm_ref, gs_ref, m_ref, v_ref, g_ref, d_ref, nm_ref, nv_ref):
        g = jnp.where(pl.program_id(0) // per == core_ref[0], gm_ref[...], gs_ref[...])
        g_ref[...] = g
        d_ref[...], nm_ref[...], nv_ref[...] = _adam_update(w_ref[...], g, m_ref[...], v_ref[...])

    spec = pl.BlockSpec((cols, 1, tl), lambda i, core_ref: (0, 0, i))
    half_spec = pl.BlockSpec((cols, 1, tl), lambda i, core_ref: (0, 0, i % per))
    shape = jax.ShapeDtypeStruct((cols, 1, rows), F32)
    return pl.pallas_call(
        body, name=name, out_shape=[shape] * 4,
        grid_spec=pltpu.PrefetchScalarGridSpec(
            num_scalar_prefetch=1, grid=(rows // tl,), in_specs=[spec, half_spec, half_spec, spec, spec], out_specs=[spec] * 4),
        compiler_params=_params(("parallel",)),
    )(core, w, g_mine, g_sibling, m, v)


def _chip_sum(name, core, g, other):
    _, _, half, cols = g.shape
    tr = _row_tile(half, cols)

    def body(core_ref, g_ref, o_ref, p_ref):
        p_ref[...] = (g_ref[...] + o_ref[...]).astype(BF16)

    spec = pl.BlockSpec((None, tr, cols), lambda j, i, core_ref: (j, i, 0))
    return pl.pallas_call(
        body, name=name, out_shape=jax.ShapeDtypeStruct((N_CHIPS, half, cols), BF16),
        grid_spec=pltpu.PrefetchScalarGridSpec(
            num_scalar_prefetch=1, grid=(N_CHIPS, half // tr),
            in_specs=[pl.BlockSpec((None, None, tr, cols), lambda j, i, core_ref: (j, core_ref[0], i, 0)), spec],
            out_specs=spec),
        compiler_params=_params(("parallel", "parallel")),
    )(core, g, other)


def _mesh_sum(name, chip, received, own):
    _, half, cols = received.shape
    tr = _row_tile(half, cols, itemsize=2 * N_CHIPS)

    def body(chip_ref, r_ref, own_ref, o_ref):
        acc = None
        for j in range(N_CHIPS):
            term = jnp.where(chip_ref[0] == j, own_ref[...], r_ref[j]).astype(F32)
            acc = term if acc is None else acc + term
        o_ref[...] = acc

    return pl.pallas_call(
        body, name=name, out_shape=jax.ShapeDtypeStruct((half, cols), F32),
        grid_spec=pltpu.PrefetchScalarGridSpec(
            num_scalar_prefetch=1, grid=(half // tr,),
            in_specs=[pl.BlockSpec((N_CHIPS, tr, cols), lambda i, chip_ref: (0, i, 0)),
                      pl.BlockSpec((None, tr, cols), lambda i, chip_ref: (chip_ref[0], i, 0))],
            out_specs=pl.BlockSpec((tr, cols), lambda i, chip_ref: (i, 0))),
        compiler_params=_params(("parallel",)),
    )(chip, received, own)


CHIP_FLIPS = ((1, 0), (0, 1), (1, 1))


def _place():
    x, y, c = lax.axis_index("x"), lax.axis_index("y"), lax.axis_index("c")
    return x, y, c, 2 * x + y


def _remote(src, dst, sems_s, sems_r, k, dev):
    return pltpu.make_async_remote_copy(src_ref=src, dst_ref=dst, send_sem=sems_s.at[k], recv_sem=sems_r.at[k],
                                        device_id=dev, device_id_type=MESH)


class _Exchange:
    def __init__(self, ins, out_shapes, n_sems, start, forward, finish):
        self.ins, self.out_shapes, self.n_sems = list(ins), list(out_shapes), n_sems
        self.start, self.forward, self.finish = start, forward, finish

    def scratch(self):
        return [pltpu.SemaphoreType.DMA((self.n_sems,)), pltpu.SemaphoreType.DMA((self.n_sems,))]

    def run(self, name):
        n = len(self.ins)

        def body(*refs):
            args = (refs[:n], refs[n:2 * n]) + tuple(refs[2 * n:])
            self.start(*args)
            self.forward(*args)
            self.finish(*args)

        return pl.pallas_call(
            body, name=name, in_specs=[ANY] * n, out_specs=[ANY] * n, out_shape=self.out_shapes, scratch_shapes=self.scratch(),
        )(*self.ins)


def _all_gather_weights(halved, whole):
    nh, nw = len(halved), len(whole)
    n_arr = nh + nw

    def copies(ins, outs, sems_s, sems_r):
        x, y, c, me = _place()
        sibling = (x, y, 1 - c)
        own = [_remote(ins[k], outs[k].at[me], sems_s, sems_r, k, sibling) for k in range(n_arr)]
        first, passed = [], []
        for k in range(n_arr):
            for f, (fx, fy) in enumerate(CHIP_FLIPS):
                src, dst = (ins[k].at[c], outs[k].at[me, c]) if k < nh else (ins[k], outs[k].at[me])
                first.append(_remote(src, dst, sems_s, sems_r, n_arr + 3 * k + f, (x ^ fx, y ^ fy, c)))
        for k in range(nh):
            for f, (fx, fy) in enumerate(CHIP_FLIPS):
                landed = outs[k].at[2 * (x ^ fx) + (y ^ fy), c]
                passed.append(_remote(landed, landed, sems_s, sems_r, 4 * n_arr + 3 * k + f, sibling))
        return own, first, passed

    def start(*refs):
        own, first, _ = copies(*refs)
        for cp in own + first:
            cp.start()

    def forward(*refs):
        _, first, passed = copies(*refs)
        for arrived, cp in zip(first, passed):
            arrived.wait_recv()
            cp.start()

    def finish(*refs):
        own, first, passed = copies(*refs)
        for cp in first[3 * nh:] + passed + own:
            cp.wait_recv()
        for cp in first + passed + own:
            cp.wait_send()

    shapes = [jax.ShapeDtypeStruct((N_CHIPS,) + a.shape, a.dtype) for a in list(halved) + list(whole)]
    return _Exchange(list(halved) + list(whole), shapes, 7 * nh + 4 * nw, start, forward, finish)


def _swap_halves(gs):
    n = len(gs)

    def copies(ins, outs, sems_s, sems_r):
        x, y, c, _ = _place()
        return [_remote(ins[k].at[:, 1 - c], outs[k], sems_s, sems_r, k, (x, y, 1 - c)) for k in range(n)]

    def start(*refs):
        for cp in copies(*refs):
            cp.start()

    def finish(*refs):
        for cp in copies(*refs):
            cp.wait()

    shapes = [jax.ShapeDtypeStruct((g.shape[0],) + g.shape[2:], g.dtype) for g in gs]
    return _Exchange(gs, shapes, n, start, _no_copies, finish)


def _scatter_chips(ps):
    n = len(ps)

    def copies(ins, outs, sems_s, sems_r):
        x, y, c, me = _place()
        return [_remote(ins[k].at[2 * (x ^ fx) + (y ^ fy)], outs[k].at[me], sems_s, sems_r, 3 * k + f, (x ^ fx, y ^ fy, c))
                for k in range(n) for f, (fx, fy) in enumerate(CHIP_FLIPS)]

    def start(*refs):
        for cp in copies(*refs):
            cp.start()

    def forward(*refs):
        pass

    def finish(*refs):
        for cp in copies(*refs):
            cp.wait()

    shapes = [jax.ShapeDtypeStruct(p.shape, p.dtype) for p in ps]
    return _Exchange(ps, shapes, 3 * n, start, forward, finish)


def _swap_reduced(rs):
    n = len(rs)

    def copies(ins, outs, sems_s, sems_r):
        x, y, c, _ = _place()
        return [_remote(ins[k], outs[k], sems_s, sems_r, k, (x, y, 1 - c)) for k in range(n)]

    def start(*refs):
        for cp in copies(*refs):
            cp.start()

    def finish(*refs):
        for cp in copies(*refs):
            cp.wait()

    return _Exchange(rs, [jax.ShapeDtypeStruct(r.shape, r.dtype) for r in rs], n, start, _no_copies, finish)


N_DEV = 8


def _gather_small(buf):
    def copies(ins, outs, sems_s, sems_r):
        x, y, c, chip = _place()
        sibling = (x, y, 1 - c)
        own = _remote(ins[0], outs[0].at[2 * chip + c], sems_s, sems_r, 0, sibling)
        first = [_remote(ins[0], outs[0].at[2 * chip + c], sems_s, sems_r, 1 + f, (x ^ fx, y ^ fy, c))
                 for f, (fx, fy) in enumerate(CHIP_FLIPS)]
        passed = []
        for f, (fx, fy) in enumerate(CHIP_FLIPS):
            landed = outs[0].at[2 * (2 * (x ^ fx) + (y ^ fy)) + c]
            passed.append(_remote(landed, landed, sems_s, sems_r, 4 + f, sibling))
        return own, first, passed

    def start(*refs):
        own, first, _ = copies(*refs)
        for cp in [own] + first:
            cp.start()

    def forward(*refs):
        _, first, passed = copies(*refs)
        for arrived, cp in zip(first, passed):
            arrived.wait_recv()
            cp.start()

    def finish(*refs):
        own, first, passed = copies(*refs)
        for cp in passed + [own]:
            cp.wait_recv()
        for cp in first + passed + [own]:
            cp.wait_send()

    return _Exchange([buf], [jax.ShapeDtypeStruct((N_DEV,) + buf.shape, buf.dtype)], N_DEV - 1, start, forward, finish)


def _sum_devices(place, gathered, own):
    rows = own.shape[0]

    def body(place_ref, g_ref, own_ref, o_ref):
        acc = None
        for d in range(N_DEV):
            term = jnp.where(place_ref[0] == d, own_ref[...], g_ref[d])
            acc = term if acc is None else acc + term
        o_ref[...] = acc

    return pl.pallas_call(
        body, name="sum_devices", out_shape=jax.ShapeDtypeStruct((rows, LANES), F32),
        grid_spec=pltpu.PrefetchScalarGridSpec(
            num_scalar_prefetch=1, grid=(1,),
            in_specs=[pl.BlockSpec((N_DEV, rows, LANES), lambda i, place_ref: (0, 0, 0)),
                      pl.BlockSpec((rows, LANES), lambda i, place_ref: (0, 0))],
            out_specs=pl.BlockSpec((rows, LANES), lambda i, place_ref: (0, 0))),
        compiler_params=_params(("arbitrary",)),
    )(place, gathered, own)


def _no_copies(*refs):
    pass


def _no_exchange():
    return _Exchange([], [], 1, _no_copies, _no_copies, _no_copies)


class _NoComm:
    def gather_first(self):
        return _no_exchange()

    def first_landed(self, p, landed):
        pass

    def gather_rest(self, p):
        return _no_exchange()

    def weights_landed(self, p, landed):
        pass

    def gather_last(self):
        return _no_exchange()

    def last_landed(self, p, landed):
        pass

    def swap_first(self, g):
        return _no_exchange()

    def first_swapped(self, landed):
        pass

    def swap_second(self, g):
        return _no_exchange()

    def second_swapped(self, landed):
        pass

    def scatter_early(self, g):
        return _no_exchange()

    def scatter_landed(self, landed):
        pass

    def swap_reduced_early(self):
        return _no_exchange()

    def reduced_landed(self, landed):
        pass

    def scatter_late(self, g):
        return _no_exchange()

    def late_landed(self, landed):
        pass


def _local_step(x, mem, target, p, comm):
    h1, landed = _norm_fwd("norm_mix_pre", x, p["norm_mix_pre"], comm.gather_first())
    comm.first_landed(p, landed)
    qa, ka, va, u, z = _in_proj(h1, p["w_in"], p["bf_pad"])
    ycat, qab, landed = _fox_fwd(qa, ka, va, comm.gather_rest(p))
    comm.weights_landed(p, landed)
    ycat = _pool_fwd(u, p["w_pool_bd"], p["pool_scale"], ycat)
    y1, x2, h2, qx = _proj_resid_norm("mix_out", ycat, p["w_mix_out"], x, p["norm_mix_post"], p["norm_xa_pre"], p["w_xq"])
    mem_n = _norm_fwd("norm_mem", mem, p["norm_mem"])
    kv = _mm(
        "xkv", mem_n, p["w_xkv"], pl.BlockSpec((MEM, D), lambda i, j, k: (0, 0)),
        pl.BlockSpec((None, D, 512), lambda i, j, k: (j, 0, 0)), jax.ShapeDtypeStruct((MEM, 2 * D), BF16),
        pl.BlockSpec((MEM, 512), lambda i, j, k: (0, j)), (1, N_CHIPS, 1), NN, (MEM, 512))
    xo = _xattn_fwd(qx, kv)
    y2, x3, h3 = _proj_resid_norm("xo", xo, p["w_xo"], x2, p["norm_xa_post"], p["norm_ffn_pre"])
    hid, landed = _mm(
        "up_proj", h3, p["w_up"], pl.BlockSpec((2048, D), lambda i, j, k: (i, 0)),
        pl.BlockSpec((None, D, 1024), lambda i, j, k: (j // 2, 0, j % 2)), jax.ShapeDtypeStruct((2, S, D_FF), F32),
        pl.BlockSpec((None, 2048, 1024), lambda i, j, k: (j // 4, i, j % 4)), (S // 2048, 8, 1), NN, (2048, 1024),
        comm.gather_last())
    comm.last_landed(p, landed)
    act = _convgate_fwd(hid, p["cwb"])

    g = {}
    dres, dy3, g["norm_ffn_post"], loss_cols = _down_loss_bwd(act, p["w_down"], x3, p["norm_ffn_post"], target)
    dact = _mm_nt("d_act", dy3, p["w_down"], F32, 2048, 1024)
    g["w_down"] = _mm_tn("dw_down", act, dy3, 1024, 512)
    dhid, dcwb = _convgate_bwd(hid, dact, p["cwb"])
    g["w_up"] = _mm(
        "dw_up", h3, dhid, pl.BlockSpec((S, D), lambda i, j, k: (0, 0)),
        pl.BlockSpec((None, S, 512), lambda i, j, k: (j // 8, 0, j % 8)), jax.ShapeDtypeStruct((N_CHIPS, D, 2048), F32),
        pl.BlockSpec((None, D, 512), lambda i, j, k: (j // 4, 0, j % 4)), (1, 16, 1), TN, (D, 512))
    dh3, landed = _d_h3(dhid, p["w_up"], comm.swap_first(g))
    comm.first_swapped(landed)
    dres, dy2, dxo, g["norm_ffn_pre"], g["norm_xa_post"] = _mid_bwd(
        "bwd_ffn_xa", dres, x3, p["norm_ffn_pre"], dh3, y2, p["norm_xa_post"], p["w_xo"])
    g["w_xo"] = _mm_tn("dw_xo", xo, dy2, 1024, 512)
    dqx, dkv = _xattn_bwd(qx, kv, dxo)
    dkv = dkv.astype(BF16)
    g["w_xq"] = _mm_tn("dw_xq", h2, dqx, 1024, 512)
    dmem_n = _mm(
        "d_mem", dkv, p["w_xkv"], pl.BlockSpec((MEM, 512), lambda i, j, k: (0, k)),
        pl.BlockSpec((None, D, 512), lambda i, j, k: (k, 0, 0)), jax.ShapeDtypeStruct((MEM, D), F32),
        pl.BlockSpec((MEM, D), lambda i, j, k: (0, 0)), (1, 1, N_CHIPS), NT, (MEM, D))
    g["w_xkv"] = _mm(
        "dw_xkv", mem_n, dkv, pl.BlockSpec((MEM, D), lambda i, j, k: (0, 0)),
        pl.BlockSpec((MEM, 512), lambda i, j, k: (0, j)), jax.ShapeDtypeStruct((N_CHIPS, D, 512), F32),
        pl.BlockSpec((None, D, 512), lambda i, j, k: (j, 0, 0)), (1, N_CHIPS, 1), TN, (D, 512))
    g["norm_mem"] = _gain_bwd("dg_mem", mem, p["norm_mem"], dmem_n)
    (dres, dy1, g["norm_xa_pre"], g["norm_mix_post"], dy_pool, doa), landed = _bwd_xa_mix(
        dqx, p["w_xq"], dres, x2, p["norm_xa_pre"], y1, p["norm_mix_post"], p["w_mix_out"], ycat, comm.swap_second(g))
    comm.second_swapped(landed)
    g["w_mix_out"] = _mm_tn("dw_mix_out", ycat, dy1, 1024, 512)
    dqa, dka, dva, landed = _fox_bwd(qab, doa, ka, va, comm.scatter_early(g))
    comm.scatter_landed(landed)
    du, g["w_pool_full"], g["pool_scale"] = _pool_bwd(u, dy_pool, p["w_pool_bd"], p["w_pool_bd_t"], p["pool_scale"])
    dproj, g["bf_pad"] = _fox_bwd_post(dqa, dka, dva, du, z, p["bf_pad"])
    g["w_in"], landed = _dw_in(dproj, h1, comm.swap_reduced_early())
    comm.reduced_landed(landed)
    dh1, landed = _mm_nt("d_h1", dproj, p["w_in"], F32, 1024, 1024, comm.scatter_late(g))
    comm.late_landed(landed)
    grad_x, g["norm_mix_pre"] = _first_bwd(dres, x, p["norm_mix_pre"], dh1)
    g["cwb"] = dcwb
    return grad_x, g, loss_cols


BIG = ("w_in", "w_mix_out", "w_xq", "w_xkv", "w_xo", "w_up", "w_down")
ROW_SHARDED = ("w_mix_out", "w_xq", "w_xo", "w_down")
SMALL = ("norm_mix_pre", "norm_mix_post", "b_forget", "w_pool", "pool_scale", "norm_mem", "norm_xa_pre", "norm_xa_post",
         "norm_ffn_pre", "norm_ffn_post", "conv_b")
ORDER = ("norm_mix_pre", "norm_mix_post", "w_in", "b_forget", "w_pool", "pool_scale", "w_mix_out", "norm_mem", "norm_xa_pre",
         "norm_xa_post", "w_xq", "w_xkv", "w_xo", "norm_ffn_pre", "norm_ffn_post", "w_up", "conv_w", "conv_b", "w_down")
SLOT = SUBLANES * LANES


def _pack(parts):
    rows, offs, off = [], [], 0
    for a in parts:
        flat = a.reshape(-1).astype(F32)
        n = -(-flat.shape[0] // SLOT) * SLOT
        rows.append(jnp.pad(flat, (0, n - flat.shape[0])).reshape(n // LANES, LANES))
        offs.append(off)
        off += n // LANES
    return jnp.concatenate(rows, axis=0), offs


def _unpack(buf, off, like):
    n = like.size
    rows = -(-n // LANES)
    return buf[off:off + rows].reshape(-1)[:n].reshape(like.shape)


FIRST = ("w_in",)
REST = ("w_mix_out", "w_xq", "w_xkv", "w_xo", "w_up")
LAST = ("w_down",)


def _local_params(w):
    w_pool_bd = jnp.zeros((D_POOL, D_POOL), F32)
    for gi in range(4):
        w_pool_bd = w_pool_bd.at[64 * gi:64 * (gi + 1), 64 * gi:64 * (gi + 1)].set(w["w_pool"][0, gi])
    p = {n: w[n] for n in ("norm_mix_pre", "norm_mix_post", "norm_mem", "norm_xa_pre", "norm_xa_post", "norm_ffn_pre",
                           "norm_ffn_post")}
    p.update(
        bf_pad=jnp.pad(w["b_forget"], ((0, 0), (0, LANES - HEADS))),
        w_pool_bd=w_pool_bd.astype(BF16), w_pool_bd_t=w_pool_bd.T.astype(BF16), pool_scale=w["pool_scale"].reshape(1, D_POOL))
    return p


def _w_in_param(stacked):
    n, rows, cols = stacked.shape
    tr = PREP_TR

    def body(w_ref, o_ref):
        o_ref[...] = jnp.concatenate([w_ref[j] for j in range(n)] + [jnp.zeros((tr, D_IN_PAD - n * cols), BF16)], axis=1)

    return pl.pallas_call(
        body, name="w_in_whole", grid=(rows // tr,), in_specs=[pl.BlockSpec((n, tr, cols), lambda i: (0, i, 0))],
        out_specs=_row_spec(tr, D_IN_PAD), out_shape=jax.ShapeDtypeStruct((rows, D_IN_PAD), BF16),
        compiler_params=_params(("parallel",)),
    )(stacked)


def _rest_params(w, full, conv_w_full):
    cw2 = conv_w_full.reshape(3, 2, D_FF).transpose(1, 0, 2)
    cwb = jnp.concatenate([cw2, w["conv_b"].reshape(1, 2, D_FF).transpose(1, 0, 2), jnp.zeros((2, 4, D_FF), F32)], axis=1)
    return dict(w_mix_out=full["w_mix_out"].reshape(D, D), w_xq=full["w_xq"].reshape(D, D), w_xkv=full["w_xkv"],
                w_xo=full["w_xo"].reshape(D, D), w_up=full["w_up"], cwb=cwb)


def _whole_params(w, full, conv_w_full):
    p = _local_params(w)
    p.update(_rest_params(w, full, conv_w_full), w_in=_w_in_param(full["w_in"]), w_down=full["w_down"].reshape(D_FF, D))
    return p


def _halved(a):
    return a.reshape(a.shape[:-2] + (2, a.shape[-2] // 2, a.shape[-1]))


class _StepComm:
    def __init__(self, w, shard2d, conv_w, core_id, chip_id):
        self.w, self.shard2d, self.conv_w, self.core_id, self.chip_id = w, shard2d, conv_w, core_id, chip_id
        self.first, self.second = ("w_up", "w_down"), ("w_xq", "w_xkv", "w_xo")
        self.early = self.first + self.second
        self.late = ("w_in", "w_mix_out")

    def gather_first(self):
        return _all_gather_weights([_halved(self.shard2d[n].astype(BF16)) for n in FIRST], [])

    def first_landed(self, p, landed):
        p["w_in"] = _w_in_param(landed[0].reshape((N_CHIPS,) + self.shard2d["w_in"].shape))

    def gather_rest(self, p):
        return _all_gather_weights([_halved(self.shard2d[n].astype(BF16)) for n in REST], [self.conv_w.reshape(3, -1)])

    def weights_landed(self, p, landed):
        full = {n: a.reshape((N_CHIPS,) + self.shard2d[n].shape) for n, a in zip(REST, landed)}
        conv_w_full = jnp.transpose(landed[-1], (1, 0, 2)).reshape(3, 2 * D_FF)
        p.update(_rest_params(self.w, full, conv_w_full))

    def gather_last(self):
        return _all_gather_weights([_halved(self.shard2d[n].astype(BF16)) for n in LAST], [])

    def last_landed(self, p, landed):
        p["w_down"] = landed[0].reshape(D_FF, D)

    def _view(self, g, n):
        return _halved(g[n].reshape((N_CHIPS,) + self.shard2d[n].shape))

    def swap_first(self, g):
        return _swap_halves([self._view(g, n) for n in self.first])

    def first_swapped(self, landed):
        self.from_sibling = dict(zip(self.first, landed))

    def swap_second(self, g):
        return _swap_halves([self._view(g, n) for n in self.second])

    def second_swapped(self, landed):
        self.from_sibling.update(zip(self.second, landed))

    def scatter_early(self, g):
        self.partial = [_chip_sum("chip_sum_" + n, self.core_id, self._view(g, n), self.from_sibling[n]) for n in self.early]
        return _scatter_chips(self.partial)

    def scatter_landed(self, landed):
        self.received = list(landed)

    def swap_reduced_early(self):
        self.reduced = [_mesh_sum("mesh_sum_" + n, self.chip_id, r, own)
                        for n, r, own in zip(self.early, self.received, self.partial)]
        return _swap_reduced(self.reduced)

    def reduced_landed(self, landed):
        self.reduced_sibling = list(landed)

    def scatter_late(self, g):
        views = [g["w_in"], self._view(g, "w_mix_out")]
        from_sibling = _swap_halves(views).run("swap_halves_late")
        self.partial_late = [_chip_sum("chip_sum_" + n, self.core_id, view, other)
                             for n, view, other in zip(self.late, views, from_sibling)]
        return _scatter_chips(self.partial_late)

    def late_landed(self, landed):
        self.received_late = list(landed)


def kernel(x, mem, norm_mix_pre, norm_mix_post, w_in, b_forget, w_pool, pool_scale, w_mix_out, norm_mem, norm_xa_pre, norm_xa_post, w_xq, w_xkv, w_xo, norm_ffn_pre, norm_ffn_post, w_up, conv_w, conv_b, w_down, loss_target, m_norm_mix_pre, m_norm_mix_post, m_w_in, m_b_forget, m_w_pool, m_pool_scale, m_w_mix_out, m_norm_mem, m_norm_xa_pre, m_norm_xa_post, m_w_xq, m_w_xkv, m_w_xo, m_norm_ffn_pre, m_norm_ffn_post, m_w_up, m_conv_w, m_conv_b, m_w_down, v_norm_mix_pre, v_norm_mix_post, v_w_in, v_b_forget, v_w_pool, v_pool_scale, v_w_mix_out, v_norm_mem, v_norm_xa_pre, v_norm_xa_post, v_w_xq, v_w_xkv, v_w_xo, v_norm_ffn_pre, v_norm_ffn_post, v_w_up, v_conv_w, v_conv_b, v_w_down):
    w = dict(norm_mix_pre=norm_mix_pre, norm_mix_post=norm_mix_post, w_in=w_in, b_forget=b_forget, w_pool=w_pool,
             pool_scale=pool_scale, w_mix_out=w_mix_out, norm_mem=norm_mem, norm_xa_pre=norm_xa_pre, norm_xa_post=norm_xa_post,
             w_xq=w_xq, w_xkv=w_xkv, w_xo=w_xo, norm_ffn_pre=norm_ffn_pre, norm_ffn_post=norm_ffn_post, w_up=w_up,
             conv_w=conv_w, conv_b=conv_b, w_down=w_down)
    m = dict(norm_mix_pre=m_norm_mix_pre, norm_mix_post=m_norm_mix_post, w_in=m_w_in, b_forget=m_b_forget, w_pool=m_w_pool,
             pool_scale=m_pool_scale, w_mix_out=m_w_mix_out, norm_mem=m_norm_mem, norm_xa_pre=m_norm_xa_pre,
             norm_xa_post=m_norm_xa_post, w_xq=m_w_xq, w_xkv=m_w_xkv, w_xo=m_w_xo, norm_ffn_pre=m_norm_ffn_pre,
             norm_ffn_post=m_norm_ffn_post, w_up=m_w_up, conv_w=m_conv_w, conv_b=m_conv_b, w_down=m_w_down)
    v = dict(norm_mix_pre=v_norm_mix_pre, norm_mix_post=v_norm_mix_post, w_in=v_w_in, b_forget=v_b_forget, w_pool=v_w_pool,
             pool_scale=v_pool_scale, w_mix_out=v_w_mix_out, norm_mem=v_norm_mem, norm_xa_pre=v_norm_xa_pre,
             norm_xa_post=v_norm_xa_post, w_xq=v_w_xq, w_xkv=v_w_xkv, w_xo=v_w_xo, norm_ffn_pre=v_norm_ffn_pre,
             norm_ffn_post=v_norm_ffn_post, w_up=v_w_up, conv_w=v_conv_w, conv_b=v_conv_b, w_down=v_w_down)
    chip = 2 * lax.axis_index("x") + lax.axis_index("y")

    core_id = lax.axis_index("c").astype(jnp.int32).reshape(1)
    chip_id = chip.astype(jnp.int32).reshape(1)

    shard2d = {n: w[n][0] for n in BIG}
    p = _local_params(w)
    comm = _StepComm(w, shard2d, conv_w, core_id, chip_id)
    grad_x, g, loss_cols = _local_step(x[0], mem[0], loss_target[0], p, comm)

    reduced_late = [_mesh_sum("mesh_sum_" + n, chip_id, r, own)
                    for n, r, own in zip(comm.late, comm.received_late, comm.partial_late)]
    names = comm.late + comm.early
    reduced = reduced_late + comm.reduced
    reduced_sibling = list(_swap_reduced(reduced_late).run("swap_reduced_late")) + comm.reduced_sibling
    grads = {}

    gw_pool = jnp.stack([g["w_pool_full"][64 * gi:64 * (gi + 1), 64 * gi:64 * (gi + 1)] for gi in range(4)])
    dcwb = g["cwb"]
    g_conv_w = dcwb[:, 0:3, :].transpose(1, 0, 2).reshape(3, 2 * D_FF)
    g_conv_b = dcwb[:, 3, :].reshape(2 * D_FF)
    small_g = dict(norm_mix_pre=g["norm_mix_pre"], norm_mix_post=g["norm_mix_post"], b_forget=g["bf_pad"][:, :HEADS],
                   w_pool=gw_pool, pool_scale=g["pool_scale"], norm_mem=g["norm_mem"], norm_xa_pre=g["norm_xa_pre"],
                   norm_xa_post=g["norm_xa_post"], norm_ffn_pre=g["norm_ffn_pre"], norm_ffn_post=g["norm_ffn_post"],
                   conv_b=g_conv_b)
    local_buf, offs = _pack([small_g[n] for n in SMALL] + [g_conv_w, loss_cols])

    delta, new_m, new_v = {}, {}, {}
    for n, g_mine, g_sibling in zip(names, reduced, reduced_sibling):
        cols = shard2d[n].shape[1]
        if cols % LANES:
            outs = _adamw_halves_columns("adamw_" + n, core_id, jnp.transpose(w[n], (2, 0, 1)), g_mine[:cols, None, :],
                                         g_sibling[:cols, None, :], jnp.transpose(m[n], (2, 0, 1)), jnp.transpose(v[n], (2, 0, 1)))
            gn, d, nm, nv = (jnp.transpose(o, (1, 2, 0)) for o in outs)
        else:
            gn, d, nm, nv = (o[None] for o in _adamw_halves("adamw_" + n, core_id, shard2d[n], g_mine, g_sibling, m[n][0], v[n][0]))
        grads[n], delta[n], new_m[n], new_v[n] = gn, d, nm, nv
    place = (2 * chip + lax.axis_index("c")).astype(jnp.int32).reshape(1)
    buf = _sum_devices(place, _gather_small(local_buf).run("gather_small")[0], local_buf)
    for n, off in zip(SMALL, offs):
        grads[n] = _unpack(buf, off, w[n])
    g_conv_w = _unpack(buf, offs[len(SMALL)], g_conv_w)
    grads["conv_w"] = lax.dynamic_slice_in_dim(g_conv_w, chip * (2 * D_FF // N_CHIPS), 2 * D_FF // N_CHIPS, axis=1).reshape(conv_w.shape)
    loss = jnp.sum(_unpack(buf, offs[len(SMALL) + 1], loss_cols))
    small_names = SMALL + ("conv_w",)
    packed = [_pack([d[n] for n in small_names])[0] for d in (w, grads, m, v)]
    offs = _pack([w[n] for n in small_names])[1]
    d, nm, nv = _adamw("adamw_small", *packed)
    for n, off in zip(small_names, offs):
        delta[n], new_m[n], new_v[n] = _unpack(d, off, w[n]), _unpack(nm, off, w[n]), _unpack(nv, off, w[n])

    return (loss, grad_x[None], *[grads[n] for n in ORDER], *[delta[n] for n in ORDER], *[new_m[n] for n in ORDER],
            *[new_v[n] for n in ORDER])
```

```python
import functools

import jax
import jax.numpy as jnp
import numpy as np
from jax import lax
from jax.experimental import pallas as pl
from jax.experimental.pallas import tpu as pltpu

F32 = jnp.float32
BF16 = jnp.bfloat16
MESH = pl.DeviceIdType.MESH
ANY = pl.BlockSpec(memory_space=pl.ANY)
VMEM_SPEC = pl.BlockSpec(memory_space=pltpu.VMEM)

S = 4096
D = 1024
MEM = 256
D_POOL = 256
HEADS = 12
DH = 64
D_FOX = HEADS * DH
D_IN = D_POOL + 3 * D_FOX + HEADS
F_OFF = D_POOL + 3 * D_FOX
Q_OFF, K_OFF, V_OFF = D_POOL, D_POOL + D_FOX, D_POOL + 2 * D_FOX
XA_HEADS = 4
XA_DH = 256
D_FF = 4096
EPS = 1e-6
N_CHIPS = 4
ADAM_LR, ADAM_B1, ADAM_B2, ADAM_EPS, ADAM_WD, ADAM_STEP = 0.001, 0.9, 0.999, 1e-08, 0.01, 10

LANES = 128
SUBLANES = 8
D_IN_PAD = 21 * LANES
TR = 512
ROW_PIECES = 4
TILE_BYTES = 2 * 1024 * 1024
NEG = -1e30
VMEM_LIMIT = 52 * 1024 * 1024

NN = (((1,), (0,)), ((), ()))
NT = (((1,), (1,)), ((), ()))
TN = (((0,), (0,)), ((), ()))


def _dot(a, b, dims=NN):
    return lax.dot_general(a, b, dims, preferred_element_type=F32)


def _params(sem):
    return pltpu.CompilerParams(dimension_semantics=sem, vmem_limit_bytes=VMEM_LIMIT)


def _split3(x):
    hi = x.astype(BF16)
    r = x - hi.astype(F32)
    mid = r.astype(BF16)
    lo = (r - mid.astype(F32)).astype(BF16)
    return hi, mid, lo


def _split3_f32(x):
    hi = x.astype(BF16).astype(F32)
    r = x - hi
    mid = r.astype(BF16).astype(F32)
    return hi, mid, r - mid


def _lane_iota(shape):
    return lax.broadcasted_iota(jnp.int32, shape, len(shape) - 1)


def _row_iota(shape):
    return lax.broadcasted_iota(jnp.int32, shape, len(shape) - 2)


def _mm(name, a, b, a_spec, b_spec, out_shape, out_spec, grid, dims, acc_shape, ex=None):
    nk = grid[2]
    if ex is not None:
        return _mm_hosting(name, a, b, a_spec, b_spec, out_shape, out_spec, grid, dims, ex)

    def body(a_ref, b_ref, o_ref, *scr):
        p = _dot(a_ref[...], b_ref[...], dims)
        if nk == 1:
            o_ref[...] = p.astype(o_ref.dtype)
        else:
            acc = scr[0]
            k = pl.program_id(2)

            @pl.when(k == 0)
            def _():
                acc[...] = p

            @pl.when(k > 0)
            def _():
                acc[...] += p

            @pl.when(k == nk - 1)
            def _():
                o_ref[...] = acc[...].astype(o_ref.dtype)

    return pl.pallas_call(
        body, name=name, grid=grid, in_specs=[a_spec, b_spec], out_specs=out_spec, out_shape=out_shape,
        scratch_shapes=[pltpu.VMEM(acc_shape, F32)] if nk > 1 else [],
        compiler_params=_params(("parallel", "parallel", "arbitrary")),
    )(a, b)


def _mm_hosting(name, a, b, a_spec, b_spec, out_shape, out_spec, grid, dims, ex):
    assert grid[2] == 1
    n = len(ex.ins)

    def body(*refs):
        i, j = pl.program_id(0), pl.program_id(1)
        last = (i == grid[0] - 1) & (j == grid[1] - 1)
        (a_ref, b_ref), (o_ref,), _, begin, end = _hosted(ex, refs, 2, 1, (i == 0) & (j == 0), last, last)
        begin()
        o_ref[...] = _dot(a_ref[...], b_ref[...], dims).astype(o_ref.dtype)
        end()

    res = pl.pallas_call(
        body, name=name, grid=grid, in_specs=[a_spec, b_spec] + [ANY] * n, out_specs=[out_spec] + [ANY] * n,
        out_shape=[out_shape] + ex.out_shapes, scratch_shapes=ex.scratch(),
        compiler_params=_params(("arbitrary", "arbitrary", "arbitrary")),
    )(a, b, *ex.ins)
    return res[0], res[1:]


def _mm_nn(name, a, b, out_dtype, tm, tn):
    m, k = a.shape
    n = b.shape[1]
    return _mm(name, a, b, pl.BlockSpec((tm, k), lambda i, j, kk: (i, 0)), pl.BlockSpec((k, tn), lambda i, j, kk: (0, j)),
               jax.ShapeDtypeStruct((m, n), out_dtype), pl.BlockSpec((tm, tn), lambda i, j, kk: (i, j)),
               (m // tm, n // tn, 1), NN, (tm, tn))


def _mm_nt(name, a, b, out_dtype, tm, tn, ex=None):
    m, k = a.shape
    n = b.shape[0]
    return _mm(name, a, b, pl.BlockSpec((tm, k), lambda i, j, kk: (i, 0)), pl.BlockSpec((tn, k), lambda i, j, kk: (j, 0)),
               jax.ShapeDtypeStruct((m, n), out_dtype), pl.BlockSpec((tm, tn), lambda i, j, kk: (i, j)),
               (m // tm, n // tn, 1), NT, (tm, tn), ex)


def _mm_tn(name, a, b, tka, tn, ex=None):
    t, ka = a.shape
    n = b.shape[1]
    return _mm(name, a, b, pl.BlockSpec((t, tka), lambda i, j, kk: (0, i)), pl.BlockSpec((t, tn), lambda i, j, kk: (0, j)),
               jax.ShapeDtypeStruct((ka, n), F32), pl.BlockSpec((tka, tn), lambda i, j, kk: (i, j)),
               (ka // tka, n // tn, 1), TN, (tka, tn), ex)


def _d_h3(dhid, w_up, ex):
    tm = tn = 1024
    shard = 2 * D_FF // N_CHIPS
    per_plane = D_FF // shard
    grid = (S // tm, D // tn, N_CHIPS)
    n = len(ex.ins)

    def body(*refs):
        i, j, k = pl.program_id(0), pl.program_id(1), pl.program_id(2)
        first = (i == 0) & (j == 0) & (k == 0)
        last = (i == grid[0] - 1) & (j == grid[1] - 1) & (k == N_CHIPS - 1)
        (a_ref, b_ref), (o_ref,), (acc_ref,), begin, end = _hosted(ex, refs, 2, 1, first, first, last)
        begin()
        part = _dot(a_ref[...], b_ref[...], NT)

        @pl.when(k == 0)
        def _():
            acc_ref[...] = part

        @pl.when(k > 0)
        def _():
            acc_ref[...] += part

        @pl.when(k == N_CHIPS - 1)
        def _():
            o_ref[...] = acc_ref[...]

        end()

    res = pl.pallas_call(
        body, name="d_h3", grid=grid,
        in_specs=[pl.BlockSpec((None, tm, shard), lambda i, j, k: (k // per_plane, i, k % per_plane)),
                  pl.BlockSpec((None, tn, shard), lambda i, j, k: (k, j, 0))] + [ANY] * n,
        out_specs=[pl.BlockSpec((tm, tn), lambda i, j, k: (i, j))] + [ANY] * n,
        out_shape=[jax.ShapeDtypeStruct((S, D), F32)] + ex.out_shapes,
        scratch_shapes=[pltpu.VMEM((tm, tn), F32)] + ex.scratch(),
        compiler_params=_params(("arbitrary", "arbitrary", "arbitrary")),
    )(dhid, w_up, *ex.ins)
    return res[0], res[1:]


SHARD_IN = D_IN // N_CHIPS
SHARD_IN_PAD = -(-SHARD_IN // SUBLANES) * SUBLANES


def _dw_in(dproj, h1, ex):
    tk = 1024
    nk = S // tk
    half = D // 2
    starts = [SHARD_IN * j // LANES * LANES for j in range(N_CHIPS)]
    shifts = [SHARD_IN * j - s for j, s in enumerate(starts)]
    window = -(-(max(shifts) + SHARD_IN) // LANES) * LANES
    assert starts[-1] + window <= dproj.shape[1]
    n = len(ex.ins)

    def body(*refs):
        k = pl.program_id(0)
        (a_ref, b_ref), (o_ref,), _, begin, end = _hosted(ex, refs, 2, 1, k == 0, k == nk - 1, k == nk - 1)
        begin()

        @pl.when(k == 0)
        def _():
            o_ref[...] = jnp.zeros(o_ref.shape, F32)

        for j in range(N_CHIPS):
            win = a_ref[:, starts[j]:starts[j] + window]
            if shifts[j]:
                win = pltpu.roll(win, window - shifts[j], axis=1)
            part = _dot(win, b_ref[...], TN)
            for h in range(2):
                o_ref[j, h] += part[:SHARD_IN_PAD, h * half:(h + 1) * half]
        end()

    out_shape = (N_CHIPS, 2, SHARD_IN_PAD, half)
    res = pl.pallas_call(
        body, name="dw_in", grid=(nk,),
        in_specs=[pl.BlockSpec((tk, dproj.shape[1]), lambda k: (k, 0)), pl.BlockSpec((tk, D), lambda k: (k, 0))] + [ANY] * n,
        out_specs=[pl.BlockSpec(out_shape, lambda k: (0, 0, 0, 0))] + [ANY] * n,
        out_shape=[jax.ShapeDtypeStruct(out_shape, F32)] + ex.out_shapes,
        scratch_shapes=ex.scratch(),
        compiler_params=_params(("arbitrary",)),
    )(dproj, h1, *ex.ins)
    return res[0], res[1:]


def _rms(x, g):
    r = lax.rsqrt(jnp.mean(x * x, axis=-1, keepdims=True) + EPS)
    return x * r * g


def _rms_bwd(x, g, dy):
    r = lax.rsqrt(jnp.mean(x * x, axis=-1, keepdims=True) + EPS)
    xh = x * r
    dxh = dy * g
    dx = r * (dxh - xh * jnp.mean(dxh * xh, axis=-1, keepdims=True))
    return dx, jnp.sum(dy * xh, axis=0, keepdims=True)


def _row_spec(tr, width):
    return pl.BlockSpec((tr, width), lambda i: (i, 0))


def _vec_spec(width):
    return pl.BlockSpec((1, width), lambda i: (0, 0))


def _norm_fwd(name, x, g, ex=None):
    rows, width = x.shape
    tr = min(TR, rows)
    steps = rows // tr
    hosted = ex if ex is not None else _no_exchange()
    n = len(hosted.ins)

    def body(*refs):
        i = pl.program_id(0)
        (x_ref, g_ref), (h_ref,), _, begin, end = _hosted(hosted, refs, 2, 1, i == 0, i == steps - 1, i == steps - 1)
        begin()
        h_ref[...] = _rms(x_ref[...], g_ref[...]).astype(BF16)
        end()

    res = pl.pallas_call(
        body, name=name, grid=(steps,), in_specs=[_row_spec(tr, width), _vec_spec(width)] + [ANY] * n,
        out_specs=[_row_spec(tr, width)] + [ANY] * n,
        out_shape=[jax.ShapeDtypeStruct((rows, width), BF16)] + hosted.out_shapes, scratch_shapes=hosted.scratch(),
        compiler_params=_params(("arbitrary",)),
    )(x, g, *hosted.ins)
    return res[0] if ex is None else (res[0], res[1:])


def _proj_resid_norm(name, a, w, xp, g_post, g_pre, w_next=None):
    def body(a_ref, w_ref, xp_ref, gpost_ref, gpre_ref, *rest):
        y_ref, xn_ref, h_ref = rest[-3:] if w_next is None else rest[1:4]
        y = _dot(a_ref[...], w_ref[...])
        y_ref[...] = y
        xn = xp_ref[...] + _rms(y, gpost_ref[...])
        xn_ref[...] = xn
        h = _rms(xn, gpre_ref[...]).astype(BF16)
        h_ref[...] = h
        if w_next is not None:
            rest[4][...] = _dot(h, rest[0][...]).astype(BF16)

    mat = pl.BlockSpec((D, D), lambda i: (0, 0))
    more = [] if w_next is None else [w_next]
    return pl.pallas_call(
        body, name=name, grid=(S // TR,),
        in_specs=[_row_spec(TR, D), mat, _row_spec(TR, D), _vec_spec(D), _vec_spec(D)] + [mat] * len(more),
        out_specs=[_row_spec(TR, D)] * (3 + len(more)),
        out_shape=[jax.ShapeDtypeStruct((S, D), F32), jax.ShapeDtypeStruct((S, D), F32), jax.ShapeDtypeStruct((S, D), BF16)]
        + [jax.ShapeDtypeStruct((S, D), BF16)] * len(more),
        compiler_params=_params(("parallel",)),
    )(a, w, xp, g_post, g_pre, *more)


def _down_loss_bwd(act, w_down, x3, g_post, target):
    def body(a_ref, w_ref, x_ref, g_ref, t_ref, dres_ref, dy_ref, dg_ref, loss_ref):
        i = pl.program_id(0)

        @pl.when(i == 0)
        def _():
            dg_ref[...] = jnp.zeros_like(dg_ref)
            loss_ref[...] = jnp.zeros_like(loss_ref)

        g = g_ref[...]
        for r in range(ROW_PIECES):
            rows = slice(r * TR // ROW_PIECES, (r + 1) * TR // ROW_PIECES)
            y = _dot(a_ref[rows, :], w_ref[...])
            e = x_ref[rows, :] + _rms(y, g) - t_ref[rows, :]
            loss_ref[...] += jnp.sum(e * e, axis=0, keepdims=True) * (0.5 / D)
            dres = e * (1.0 / D)
            dres_ref[rows, :] = dres
            dy, dg = _rms_bwd(y, g, dres)
            dy_ref[rows, :] = dy.astype(BF16)
            dg_ref[...] += dg

    return pl.pallas_call(
        body, name="down_loss_bwd", grid=(S // TR,),
        in_specs=[_row_spec(TR, D_FF), pl.BlockSpec((D_FF, D), lambda i: (0, 0)), _row_spec(TR, D), _vec_spec(D),
                  _row_spec(TR, D)],
        out_specs=[_row_spec(TR, D), _row_spec(TR, D), _vec_spec(D), _vec_spec(D)],
        out_shape=[jax.ShapeDtypeStruct((S, D), F32), jax.ShapeDtypeStruct((S, D), BF16),
                   jax.ShapeDtypeStruct((1, D), F32), jax.ShapeDtypeStruct((1, D), F32)],
        compiler_params=_params(("arbitrary",)),
    )(act, w_down, x3, g_post, target)


def _mid_bwd(name, dres, xcur, g_pre, dh, yprev, g_post, w):
    def body(dres_ref, x_ref, gpre_ref, dh_ref, y_ref, gpost_ref, w_ref, dx_ref, dy_ref, da_ref, dgpre_ref, dgpost_ref):
        i = pl.program_id(0)

        @pl.when(i == 0)
        def _():
            dgpre_ref[...] = jnp.zeros_like(dgpre_ref)
            dgpost_ref[...] = jnp.zeros_like(dgpost_ref)

        dxn, dgpre = _rms_bwd(x_ref[...], gpre_ref[...], dh_ref[...])
        dx = dres_ref[...] + dxn
        dx_ref[...] = dx
        dy, dgpost = _rms_bwd(y_ref[...], gpost_ref[...], dx)
        dy = dy.astype(BF16)
        dy_ref[...] = dy
        da_ref[...] = _dot(dy, w_ref[...], NT).astype(BF16)
        dgpre_ref[...] += dgpre
        dgpost_ref[...] += dgpost

    return pl.pallas_call(
        body, name=name, grid=(S // TR,),
        in_specs=[_row_spec(TR, D), _row_spec(TR, D), _vec_spec(D), _row_spec(TR, D), _row_spec(TR, D), _vec_spec(D),
                  pl.BlockSpec((D, D), lambda i: (0, 0))],
        out_specs=[_row_spec(TR, D), _row_spec(TR, D), _row_spec(TR, D), _vec_spec(D), _vec_spec(D)],
        out_shape=[jax.ShapeDtypeStruct((S, D), F32), jax.ShapeDtypeStruct((S, D), BF16), jax.ShapeDtypeStruct((S, D), BF16),
                   jax.ShapeDtypeStruct((1, D), F32), jax.ShapeDtypeStruct((1, D), F32)],
        compiler_params=_params(("arbitrary",)),
    )(dres, xcur, g_pre, dh, yprev, g_post, w)


def _first_bwd(dres, x, g, dh):
    def body(dres_ref, x_ref, g_ref, dh_ref, dx_ref, dg_ref):
        i = pl.program_id(0)

        @pl.when(i == 0)
        def _():
            dg_ref[...] = jnp.zeros_like(dg_ref)

        dxn, dg = _rms_bwd(x_ref[...], g_ref[...], dh_ref[...])
        dx_ref[...] = dres_ref[...] + dxn
        dg_ref[...] += dg

    return pl.pallas_call(
        body, name="first_bwd", grid=(S // TR,),
        in_specs=[_row_spec(TR, D), _row_spec(TR, D), _vec_spec(D), _row_spec(TR, D)],
        out_specs=[_row_spec(TR, D), _vec_spec(D)],
        out_shape=[jax.ShapeDtypeStruct((S, D), F32), jax.ShapeDtypeStruct((1, D), F32)],
        compiler_params=_params(("arbitrary",)),
    )(dres, x, g, dh)


def _gain_bwd(name, x, g, dy):
    rows, width = x.shape

    def body(x_ref, g_ref, dy_ref, dg_ref):
        _, dg = _rms_bwd(x_ref[...], g_ref[...], dy_ref[...])
        dg_ref[...] = dg

    return pl.pallas_call(
        body, name=name, grid=(1,), in_specs=[_row_spec(rows, width), _vec_spec(width), _row_spec(rows, width)],
        out_specs=_vec_spec(width), out_shape=jax.ShapeDtypeStruct((1, width), F32),
        compiler_params=_params(("arbitrary",)),
    )(x, g, dy)


CUM_Q = DH
CUM_K = DH + 3
LSE_Q = DH + 6
BOTH_ONE = DH + 9
DEN_V = DH
DELTA = DH + 1
PREP_TR = 256
PIECE_LANES = 16
FOX_FWD_BLOCK = 1024
FOX_BWD_BLOCK = 512


def _at(lane_of_even_head, h):
    return (lane_of_even_head + DH * (h % 2)) % LANES


def _data_lanes(lane, h):
    return lane >= DH if h % 2 else lane < DH


def _pair_block(ref, off, h):
    base = ((off + DH * h) // LANES) * LANES
    return ref[:, base:base + LANES]


def _cumsum_rows(x, tri, carry):
    hi, mid, lo = _split3(x)
    return _dot(tri, hi) + _dot(tri, mid) + _dot(tri, lo) + carry


def _in_proj(h1, w_in, bf_pad):
    tr = TR

    place_q = np.zeros((LANES, HEADS * LANES), np.float32)
    place_k = np.zeros((LANES, HEADS * LANES), np.float32)
    for h in range(HEADS):
        for piece in range(3):
            place_q[PIECE_LANES * piece + h, LANES * h + _at(CUM_Q, h) + piece] = 1.0
            place_k[PIECE_LANES * piece + h, LANES * h + _at(CUM_K, h) + piece] = -1.0

    def body(h_ref, w_ref, bf_ref, pq_ref, pk_ref, qa_ref, ka_ref, va_ref, u_ref, z_ref, carry_ref):
        i = pl.program_id(0)

        @pl.when(i == 0)
        def _():
            carry_ref[...] = jnp.zeros_like(carry_ref)

        proj = _dot(h_ref[...], w_ref[...])
        u_ref[...] = proj[:, :D_POOL]
        z_ref[...] = proj[:, F_OFF:F_OFF + LANES]
        lane = _lane_iota((tr, LANES))
        z = proj[:, F_OFF:F_OFF + LANES] + bf_ref[...]
        log_f = jnp.minimum(z, 0.0) - jnp.log(1.0 + jnp.exp(-jnp.abs(z)))
        log_f = jnp.where(lane < HEADS, log_f, 0.0)
        tri = jnp.where(_row_iota((tr, tr)) >= _lane_iota((tr, tr)), 1.0, 0.0).astype(BF16)
        cum = _cumsum_rows(log_f, tri, carry_ref[0:1, :])
        carry_ref[0:1, :] = cum[tr - 1:tr, :]
        c_hi, c_mid, c_lo = _split3_f32(cum)
        pieces = (c_hi + pltpu.roll(c_mid, PIECE_LANES, 1) + pltpu.roll(c_lo, 2 * PIECE_LANES, 1)).astype(BF16)
        cum_q = _dot(pieces, pq_ref[...])
        cum_k = _dot(pieces, pk_ref[...])

        def between(first, h):
            return (lane >= _at(first, h)) & (lane < _at(first, h) + 3)

        ones_q = [jnp.where(between(CUM_K, h) | (lane == _at(BOTH_ONE, h)), 1.0, 0.0) for h in range(2)]
        ones_k = [jnp.where(between(CUM_Q, h) | between(LSE_Q, h) | (lane == _at(BOTH_ONE, h)), 1.0, 0.0) for h in range(2)]
        aug_v = [jnp.where(lane == _at(DEN_V, h), 1.0, jnp.where(between(DELTA, h), -1.0, 0.0)) for h in range(2)]
        for h in range(HEADS):
            mine = slice(LANES * h, LANES * (h + 1))
            data = _data_lanes(lane, h)
            qa_ref[h] = jnp.where(data, _pair_block(proj, Q_OFF, h) * (DH ** -0.5), cum_q[:, mine] + ones_q[h % 2]).astype(BF16)
            ka_ref[h] = jnp.where(data, _pair_block(proj, K_OFF, h), cum_k[:, mine] + ones_k[h % 2]).astype(BF16)
            va_ref[h] = jnp.where(data, _pair_block(proj, V_OFF, h), aug_v[h % 2]).astype(BF16)

    head_spec = pl.BlockSpec((HEADS, tr, LANES), lambda i: (0, i, 0))
    head_shape = jax.ShapeDtypeStruct((HEADS, S, LANES), BF16)
    place_spec = pl.BlockSpec(place_q.shape, lambda i: (0, 0))
    return pl.pallas_call(
        body, name="in_proj", grid=(S // tr,),
        in_specs=[_row_spec(tr, D), pl.BlockSpec((D, D_IN_PAD), lambda i: (0, 0)), _vec_spec(LANES), place_spec, place_spec],
        out_specs=[head_spec] * 3 + [_row_spec(tr, D_POOL), _row_spec(tr, LANES)],
        out_shape=[head_shape] * 3 + [jax.ShapeDtypeStruct((S, D_POOL), F32), jax.ShapeDtypeStruct((S, LANES), F32)],
        scratch_shapes=[pltpu.VMEM((SUBLANES, LANES), F32)], compiler_params=_params(("arbitrary",)),
    )(h1, w_in, bf_pad, jnp.asarray(place_q, BF16), jnp.asarray(place_k, BF16))


def _hosted(ex, refs, n_blocked_in, n_blocked_out, first, forward_at, last):
    n = len(ex.ins)
    own_in = refs[:n_blocked_in]
    ex_in = refs[n_blocked_in:n_blocked_in + n]
    own_out = refs[n_blocked_in + n:n_blocked_in + n + n_blocked_out]
    ex_out = refs[n_blocked_in + n + n_blocked_out:n_blocked_in + 2 * n + n_blocked_out]
    rest = refs[n_blocked_in + 2 * n + n_blocked_out:]
    args = (ex_in, ex_out, rest[-2], rest[-1])

    def begin():
        @pl.when(first)
        def _():
            ex.start(*args)

        @pl.when(forward_at)
        def _():
            ex.forward(*args)

    def end():
        @pl.when(last)
        def _():
            ex.finish(*args)

    return own_in, own_out, rest[:-2], begin, end


def _fox_fwd(qa, ka, va, ex):
    BQ = BK = FOX_FWD_BLOCK
    nq = S // BQ
    n_pairs = HEADS // 2

    def body(*refs):
        p_id, i = pl.program_id(0), pl.program_id(1)
        (qa_ref, ka_ref, va_ref), (y_ref, qab_ref), (m_scr, acc_scr), begin, end = _hosted(
            ex, refs, 3, 2, (p_id == 0) & (i == 0), (p_id == n_pairs - 1) & (i == 0), (p_id == n_pairs - 1) & (i == nq - 1))
        begin()
        lane = _lane_iota((BQ, LANES))
        causal = _row_iota((BQ, BK)) >= _lane_iota((BQ, BK))
        m_scr[...] = jnp.full_like(m_scr, NEG)
        acc_scr[...] = jnp.zeros_like(acc_scr)

        def step(j, masked):
            rows = pl.ds(pl.multiple_of(j * BK, BK), BK)
            for hh in range(2):
                s = _dot(qa_ref[hh], ka_ref[hh, rows, :], NT)
                if masked:
                    s = jnp.where(causal, s, NEG)
                m_prev = m_scr[hh]
                m_new = jnp.maximum(m_prev, jnp.max(s, axis=1, keepdims=True))
                p = jnp.exp(s - jnp.tile(m_new, (1, BK // LANES)))
                acc_scr[hh] = jnp.exp(m_prev - m_new) * acc_scr[hh] + _dot(p.astype(BF16), va_ref[hh, rows, :])
                m_scr[hh] = m_new

        def full_step(j, carry):
            step(j, False)
            return carry

        lax.fori_loop(0, i, full_step, 0)
        step(i, True)
        outs = []
        for hh in range(2):
            acc = acc_scr[hh]
            den_lane, lse_lane = _at(DEN_V, hh), _at(LSE_Q, hh)
            den = jnp.broadcast_to(acc[:, den_lane:den_lane + 1], (BQ, LANES))
            outs.append(acc * (1.0 / den))
            n_hi, n_mid, n_lo = _split3(-(m_scr[hh] + jnp.log(den)))
            qab_ref[hh] = jnp.where(lane == lse_lane, n_hi,
                                    jnp.where(lane == lse_lane + 1, n_mid, jnp.where(lane == lse_lane + 2, n_lo, qa_ref[hh])))
        y_ref[...] = jnp.where(lane < DH, outs[0], outs[1]).astype(BF16)
        end()

    pair_rows = pl.BlockSpec((2, BQ, LANES), lambda p, i: (p, i, 0))
    pair_all = pl.BlockSpec((2, S, LANES), lambda p, i: (p, 0, 0))
    n = len(ex.ins)
    res = pl.pallas_call(
        body, name="fox_fwd", grid=(n_pairs, nq), in_specs=[pair_rows, pair_all, pair_all] + [ANY] * n,
        out_specs=[pl.BlockSpec((BQ, LANES), lambda p, i: (i, D_POOL // LANES + p)), pair_rows] + [ANY] * n,
        out_shape=[jax.ShapeDtypeStruct((S, D), BF16), jax.ShapeDtypeStruct((HEADS, S, LANES), BF16)] + ex.out_shapes,
        scratch_shapes=[pltpu.VMEM((2, BQ, LANES), F32), pltpu.VMEM((2, BQ, LANES), F32)] + ex.scratch(),
        compiler_params=_params(("arbitrary", "arbitrary")),
    )(qa, ka, va, *ex.ins)
    return res[0], res[1], res[2:]


def _bwd_xa_mix(dqx, w_xq, dres, x2, g_pre, y1, g_post, w_mix_out, ycat, ex):
    steps = S // TR
    n = len(ex.ins)

    def body(*refs):
        i = pl.program_id(0)
        ((dq_ref, wq_ref, dres_ref, x_ref, gpre_ref, y_ref, gpost_ref, wm_ref, ycat_ref),
         (dx_ref, dy_ref, dgpre_ref, dgpost_ref, dp_ref, doa_ref), _, begin, end) = _hosted(
            ex, refs, 9, 6, i == 0, i == 0, i == steps - 1)
        begin()

        @pl.when(i == 0)
        def _():
            dgpre_ref[...] = jnp.zeros_like(dgpre_ref)
            dgpost_ref[...] = jnp.zeros_like(dgpost_ref)

        dxn, dgpre = _rms_bwd(x_ref[...], gpre_ref[...], _dot(dq_ref[...], wq_ref[...], NT))
        dx = dres_ref[...] + dxn
        dx_ref[...] = dx
        dy, dgpost = _rms_bwd(y_ref[...], gpost_ref[...], dx)
        dy = dy.astype(BF16)
        dy_ref[...] = dy
        dgpre_ref[...] += dgpre
        dgpost_ref[...] += dgpost

        d = _dot(dy, wm_ref[...], NT)
        dp_ref[...] = d[:, :D_POOL]
        lane = _lane_iota((TR, LANES))
        low = lane < DH
        for p in range(HEADS // 2):
            cols = slice(D_POOL + LANES * p, D_POOL + LANES * (p + 1))
            do = d[:, cols]
            prod = do * ycat_ref[:, cols].astype(F32)
            deltas = (jnp.sum(jnp.where(low, prod, 0.0), axis=1, keepdims=True),
                      jnp.sum(jnp.where(low, 0.0, prod), axis=1, keepdims=True))
            for hh in range(2):
                d_hi, d_mid, d_lo = _split3_f32(deltas[hh])
                dl = _at(DELTA, hh)
                aug = jnp.where(lane == dl, d_hi, jnp.where(lane == dl + 1, d_mid, jnp.where(lane == dl + 2, d_lo, 0.0)))
                doa_ref[2 * p + hh] = jnp.where(_data_lanes(lane, hh), do, aug).astype(BF16)
        end()

    mat = pl.BlockSpec((D, D), lambda i: (0, 0))
    res = pl.pallas_call(
        body, name="bwd_xa_mix", grid=(steps,),
        in_specs=[_row_spec(TR, D), mat, _row_spec(TR, D), _row_spec(TR, D), _vec_spec(D), _row_spec(TR, D), _vec_spec(D), mat,
                  _row_spec(TR, D)] + [ANY] * n,
        out_specs=[_row_spec(TR, D), _row_spec(TR, D), _vec_spec(D), _vec_spec(D), _row_spec(TR, D_POOL),
                   pl.BlockSpec((HEADS, TR, LANES), lambda i: (0, i, 0))] + [ANY] * n,
        out_shape=[jax.ShapeDtypeStruct((S, D), F32), jax.ShapeDtypeStruct((S, D), BF16), jax.ShapeDtypeStruct((1, D), F32),
                   jax.ShapeDtypeStruct((1, D), F32), jax.ShapeDtypeStruct((S, D_POOL), F32),
                   jax.ShapeDtypeStruct((HEADS, S, LANES), BF16)] + ex.out_shapes,
        scratch_shapes=ex.scratch(), compiler_params=_params(("arbitrary",)),
    )(dqx, w_xq, dres, x2, g_pre, y1, g_post, w_mix_out, ycat, *ex.ins)
    return res[:6], res[6:]


def _fox_bwd(qab, doa, ka, va, ex):
    BQ = BK = FOX_BWD_BLOCK
    nk = S // BK
    n_pairs = HEADS // 2

    def body(*refs):
        p_id, j = pl.program_id(0), pl.program_id(1)
        (qab_ref, doa_ref, ka_ref, va_ref), (dqa_ref, dka_ref, dva_ref), (dv_ref,), begin, end = _hosted(
            ex, refs, 4, 3, (p_id == 0) & (j == 0), (p_id == n_pairs - 1) & (j == 0), (p_id == n_pairs - 1) & (j == nk - 1))
        begin()

        @pl.when(j == 0)
        def _():
            dqa_ref[...] = jnp.zeros_like(dqa_ref)

        causal = _row_iota((BQ, BK)) >= _lane_iota((BQ, BK))
        dka_ref[...] = jnp.zeros_like(dka_ref)
        dv_ref[...] = jnp.zeros_like(dv_ref)

        def step(i, masked):
            rows = pl.ds(pl.multiple_of(i * BQ, BQ), BQ)
            for hh in range(2):
                kb = ka_ref[hh]
                q = qab_ref[hh, rows, :]
                do = doa_ref[hh, rows, :]
                s = _dot(q, kb, NT)
                if masked:
                    s = jnp.where(causal, s, NEG)
                p = jnp.exp(s)
                ds = p * _dot(do, va_ref[hh], NT)
                pb = p.astype(BF16)
                dsb = ds.astype(BF16)
                dv_ref[hh] += _dot(pb, do, TN)
                dka_ref[hh] += _dot(dsb, q, TN)
                dqa_ref[hh, rows, :] += _dot(dsb, kb)

        def full_step(i, carry):
            step(i, False)
            return carry

        step(j, True)
        lax.fori_loop(j + 1, nk, full_step, 0)
        dva_ref[...] = dv_ref[...].astype(BF16)
        end()

    pair_all = pl.BlockSpec((2, S, LANES), lambda p, j: (p, 0, 0))
    pair_rows = pl.BlockSpec((2, BK, LANES), lambda p, j: (p, j, 0))
    shape = jax.ShapeDtypeStruct((HEADS, S, LANES), F32)
    n = len(ex.ins)
    res = pl.pallas_call(
        body, name="fox_bwd", grid=(n_pairs, nk), in_specs=[pair_all, pair_all, pair_rows, pair_rows] + [ANY] * n,
        out_specs=[pair_all, pair_rows, pair_rows] + [ANY] * n,
        out_shape=[shape, shape, jax.ShapeDtypeStruct((HEADS, S, LANES), BF16)] + ex.out_shapes,
        scratch_shapes=[pltpu.VMEM((2, BK, LANES), F32)] + ex.scratch(), compiler_params=_params(("arbitrary", "arbitrary")),
    )(qab, doa, ka, va, *ex.ins)
    return res[0], res[1], res[2], res[3:]


def _fox_bwd_post(dqa, dka, dva, du, proj, bf_pad):
    tr = PREP_TR
    nt = S // tr

    pick = np.zeros((HEADS * LANES, LANES), np.float32)
    for h in range(HEADS):
        pick[LANES * h + _at(BOTH_ONE, h), h] = 1.0

    def body(dqa_ref, dka_ref, dva_ref, du_ref, z_ref, bf_ref, pick_ref, dp_ref, dbf_ref, carry_ref):
        i = pl.program_id(0)

        @pl.when(i == 0)
        def _():
            carry_ref[...] = jnp.zeros_like(carry_ref)
            dbf_ref[...] = jnp.zeros_like(dbf_ref)

        lane = _lane_iota((tr, LANES))
        diff = jnp.concatenate([dqa_ref[h] - dka_ref[h] for h in range(HEADS)], axis=1)
        hi = diff.astype(BF16)
        dcum = _dot(hi, pick_ref[...]) + _dot((diff - hi.astype(F32)).astype(BF16), pick_ref[...])
        tri =jnp.where(_lane_iota((tr, tr)) >= _row_iota((tr, tr)), 1.0, 0.0).astype(BF16)
        dlog_f = _cumsum_rows(dcum, tri, carry_ref[0:1, :])
        carry_ref[0:1, :] = dlog_f[0:1, :]
        z = z_ref[...] + bf_ref[...]
        df = jnp.where(lane < HEADS, dlog_f / (1.0 + jnp.exp(z)), 0.0)
        dbf_ref[...] += jnp.sum(df, axis=0, keepdims=True)

        dp_ref[:, 0:D_POOL] = du_ref[...].astype(BF16)
        low = lane < DH
        for ref, off, scale in ((dqa_ref, Q_OFF, DH ** -0.5), (dka_ref, K_OFF, 1.0), (dva_ref, V_OFF, 1.0)):
            for p in range(HEADS // 2):
                blk = jnp.where(low, ref[2 * p], ref[2 * p + 1])
                dp_ref[:, off + LANES * p:off + LANES * (p + 1)] = (blk * scale).astype(BF16)
        dp_ref[:, F_OFF:F_OFF + LANES] = df.astype(BF16)

    head_spec = pl.BlockSpec((HEADS, tr, LANES), lambda i: (0, nt - 1 - i, 0))
    return pl.pallas_call(
        body, name="fox_bwd_post", grid=(nt,),
        in_specs=[head_spec, head_spec, head_spec, pl.BlockSpec((tr, D_POOL), lambda i: (nt - 1 - i, 0)),
                  pl.BlockSpec((tr, LANES), lambda i: (nt - 1 - i, 0)), _vec_spec(LANES),
                  pl.BlockSpec(pick.shape, lambda i: (0, 0))],
        out_specs=[pl.BlockSpec((tr, D_IN_PAD), lambda i: (nt - 1 - i, 0)), _vec_spec(LANES)],
        out_shape=[jax.ShapeDtypeStruct((S, D_IN_PAD), BF16), jax.ShapeDtypeStruct((1, LANES), F32)],
        scratch_shapes=[pltpu.VMEM((SUBLANES, LANES), F32)],
        compiler_params=_params(("arbitrary",)),
    )(dqa, dka, dva, du, proj, bf_pad, jnp.asarray(pick, BF16))


POOL_HALO = 16


def _by_group(lane, a2, a4, a8, a16):
    return jnp.where(lane < 64, a2, jnp.where(lane < 128, a4, jnp.where(lane < 192, a8, a16)))


def _window_count(lane, t):
    return jnp.minimum(t + 1, _by_group(lane, 2, 4, 8, 16)).astype(F32)


def _pool_diff(u, halo, first, tile):
    n = TR + POOL_HALO
    ext = jnp.concatenate([jnp.where(first, 0.0, halo), u], axis=0)
    s2 = ext + pltpu.roll(ext, 1, 0)
    s4 = s2 + pltpu.roll(s2, 2, 0)
    s8 = s4 + pltpu.roll(s4, 4, 0)
    s16 = s8 + pltpu.roll(s8, 8, 0)
    lane = _lane_iota((n, D_POOL))
    win = _by_group(lane, s2, s4, s8, s16)[POOL_HALO:]
    lane = _lane_iota((TR, D_POOL))
    t = tile * TR + _row_iota((TR, D_POOL))
    return win / _window_count(lane, t) - u


def _prev_halo(rows, width, col):
    per = TR // rows
    return pl.BlockSpec((rows, width), lambda i: (jnp.maximum(i * per - 1, 0), col))


def _next_halo(rows, width, col):
    per = TR // rows
    return pl.BlockSpec((rows, width), lambda i: (jnp.minimum((i + 1) * per, S // rows - 1), col))


def _pool_fwd(proj, w_bd, ps, ycat):
    def body(u_ref, halo_ref, w_ref, ps_ref, ycat_ref, y_ref):
        i = pl.program_id(0)
        diff = _pool_diff(u_ref[...], halo_ref[...], i == 0, i)
        y_ref[...] = (_dot(diff.astype(BF16), w_ref[...]) * ps_ref[...]).astype(BF16)

    return pl.pallas_call(
        body, name="pool_fwd", grid=(S // TR,),
        in_specs=[_row_spec(TR, D_POOL), _prev_halo(POOL_HALO, D_POOL, 0),
                  pl.BlockSpec((D_POOL, D_POOL), lambda i: (0, 0)), _vec_spec(D_POOL), ANY],
        out_specs=_row_spec(TR, D_POOL), out_shape=jax.ShapeDtypeStruct((S, D), BF16), input_output_aliases={4: 0},
        compiler_params=_params(("parallel",)),
    )(proj, proj, w_bd, ps, ycat)


def _pool_bwd(proj, dycat, w_bd, w_bd_t, ps):
    nt = S // TR
    n = TR + POOL_HALO

    def body(u_ref, halo_ref, dy_ref, dyn_ref, w_ref, wt_ref, ps_ref, du_ref, dw_ref, dps_ref):
        i = pl.program_id(0)

        @pl.when(i == 0)
        def _():
            dw_ref[...] = jnp.zeros_like(dw_ref)
            dps_ref[...] = jnp.zeros_like(dps_ref)

        diff = _pool_diff(u_ref[...], halo_ref[...], i == 0, i).astype(BF16)
        dy = dy_ref[...]
        dps_ref[...] += jnp.sum(dy * _dot(diff, w_ref[...]), axis=0, keepdims=True)
        dy_ext = jnp.concatenate([dy, jnp.where(i == nt - 1, 0.0, dyn_ref[...])], axis=0)
        dmixed = (dy_ext * ps_ref[...]).astype(BF16)
        ddiff = _dot(dmixed, wt_ref[...])
        dw_ref[...] += _dot(diff, dmixed[:TR], TN)
        lane = _lane_iota((n, D_POOL))
        t = i * TR + _row_iota((n, D_POOL))
        e = ddiff / _window_count(lane, t)
        f2 = e + pltpu.roll(e, n - 1, 0)
        f4 = f2 + pltpu.roll(f2, n - 2, 0)
        f8 = f4 + pltpu.roll(f4, n - 4, 0)
        f16 = f8 + pltpu.roll(f8, n - 8, 0)
        du_ref[...] = _by_group(lane, f2, f4, f8, f16)[:TR] - ddiff[:TR]

    mat = pl.BlockSpec((D_POOL, D_POOL), lambda i: (0, 0))
    return pl.pallas_call(
        body, name="pool_bwd", grid=(nt,),
        in_specs=[_row_spec(TR, D_POOL), _prev_halo(POOL_HALO, D_POOL, 0), _row_spec(TR, D_POOL),
                  _next_halo(POOL_HALO, D_POOL, 0), mat, mat, _vec_spec(D_POOL)],
        out_specs=[_row_spec(TR, D_POOL), mat, _vec_spec(D_POOL)],
        out_shape=[jax.ShapeDtypeStruct((S, D_POOL), F32), jax.ShapeDtypeStruct((D_POOL, D_POOL), F32),
                   jax.ShapeDtypeStruct((1, D_POOL), F32)],
        compiler_params=_params(("arbitrary",)),
    )(proj, proj, dycat, dycat, w_bd, w_bd_t, ps)


def _xa_probs(q, k):
    s = _dot(q, k, NT) * (XA_DH ** -0.5)
    e = jnp.exp(s - jnp.max(s, axis=-1, keepdims=True))
    return e * (1.0 / jnp.sum(e, axis=-1, keepdims=True))


def _xattn_fwd(qx, kv):
    def body(q_ref, kv_ref, o_ref):
        for h in range(XA_HEADS):
            cols = slice(XA_DH * h, XA_DH * (h + 1))
            vcols = slice(D + XA_DH * h, D + XA_DH * (h + 1))
            p = _xa_probs(q_ref[:, cols], kv_ref[:, cols])
            o_ref[:, cols] = _dot(p.astype(BF16), kv_ref[:, vcols]).astype(BF16)

    return pl.pallas_call(
        body, name="xattn_fwd", grid=(S // TR,),
        in_specs=[_row_spec(TR, D), pl.BlockSpec((MEM, 2 * D), lambda i: (0, 0))],
        out_specs=_row_spec(TR, D), out_shape=jax.ShapeDtypeStruct((S, D), BF16),
        compiler_params=_params(("parallel",)),
    )(qx, kv)


def _xattn_bwd(qx, kv, dxo):
    def body(q_ref, kv_ref, do_ref, dq_ref, dkv_ref):
        i = pl.program_id(0)

        @pl.when(i == 0)
        def _():
            dkv_ref[...] = jnp.zeros_like(dkv_ref)

        for h in range(XA_HEADS):
            cols = slice(XA_DH * h, XA_DH * (h + 1))
            vcols = slice(D + XA_DH * h, D + XA_DH * (h + 1))
            q = q_ref[:, cols]
            k = kv_ref[:, cols]
            do = do_ref[:, cols]
            p = _xa_probs(q, k)
            dkv_ref[:, vcols] += _dot(p.astype(BF16), do, TN)
            dp = _dot(do, kv_ref[:, vcols], NT)
            ds = (p * (dp - jnp.sum(p * dp, axis=-1, keepdims=True)) * (XA_DH ** -0.5)).astype(BF16)
            dq_ref[:, cols] = _dot(ds, k).astype(BF16)
            dkv_ref[:, cols] += _dot(ds, q, TN)

    kv_spec = pl.BlockSpec((MEM, 2 * D), lambda i: (0, 0))
    return pl.pallas_call(
        body, name="xattn_bwd", grid=(S // TR,), in_specs=[_row_spec(TR, D), kv_spec, _row_spec(TR, D)],
        out_specs=[_row_spec(TR, D), kv_spec],
        out_shape=[jax.ShapeDtypeStruct((S, D), BF16), jax.ShapeDtypeStruct((MEM, 2 * D), F32)],
        compiler_params=_params(("arbitrary",)),
    )(qx, kv, dxo)


CONV_HALO = SUBLANES
TC = 512
TC_FWD = 1024
GELU_K = 0.7978845608028654
GELU_C = 0.044715


def _conv3(ext, w, rows):
    h0 = ext[CONV_HALO:CONV_HALO + rows]
    h1 = pltpu.roll(ext, 1, 0)[CONV_HALO:CONV_HALO + rows]
    h2 = pltpu.roll(ext, 2, 0)[CONV_HALO:CONV_HALO + rows]
    return w[2:3] * h0 + w[1:2] * h1 + w[0:1] * h2 + w[3:4], (h2, h1, h0)


def _conv_specs(tc):
    main = pl.BlockSpec((2, TR, tc), lambda j, i: (0, i, j))
    per = TR // CONV_HALO
    prev = pl.BlockSpec((2, CONV_HALO, tc), lambda j, i: (0, jnp.maximum(i * per - 1, 0), j))
    nxt = pl.BlockSpec((2, CONV_HALO, tc), lambda j, i: (0, jnp.minimum((i + 1) * per, S // CONV_HALO - 1), j))
    par = pl.BlockSpec((2, SUBLANES, tc), lambda j, i: (0, 0, j))
    return main, prev, nxt, par


def _convgate_fwd(hid, cwb):
    tc = TC_FWD

    def body(h_ref, hp_ref, w_ref, act_ref):
        i = pl.program_id(1)
        c = []
        for g in range(2):
            ext = jnp.concatenate([jnp.where(i == 0, 0.0, hp_ref[g]), h_ref[g]], axis=0)
            c.append(_conv3(ext, w_ref[g], TR)[0])
        gate, up = c
        act_ref[...] = (jax.nn.gelu(gate, approximate=True) * up).astype(BF16)

    main, prev, _, par = _conv_specs(tc)
    return pl.pallas_call(
        body, name="convgate_fwd", grid=(D_FF // tc, S // TR), in_specs=[main, prev, par],
        out_specs=pl.BlockSpec((TR, tc), lambda j, i: (i, j)), out_shape=jax.ShapeDtypeStruct((S, D_FF), BF16),
        compiler_params=_params(("parallel", "parallel")),
    )(hid, hid, cwb)


def _convgate_bwd(hid, dact, cwb):
    nr = S // TR
    n = TR + CONV_HALO

    def body(h_ref, hp_ref, hn_ref, da_ref, dan_ref, w_ref, dh_ref, dw_ref):
        i = pl.program_id(1)

        @pl.when(i == 0)
        def _():
            dw_ref[...] = jnp.zeros_like(dw_ref)

        da = jnp.concatenate([da_ref[...], jnp.where(i == nr - 1, 0.0, dan_ref[...])], axis=0)
        c, taps = [], []
        for g in range(2):
            ext = jnp.concatenate([jnp.where(i == 0, 0.0, hp_ref[g]), h_ref[g], hn_ref[g]], axis=0)
            cg, tg = _conv3(ext, w_ref[g], n)
            c.append(cg)
            taps.append(tg)
        gate, up = c
        th = jnp.tanh(GELU_K * (gate + GELU_C * gate * gate * gate))
        gelu = 0.5 * gate * (1.0 + th)
        dgelu = 0.5 * (1.0 + th) + 0.5 * gate * (1.0 - th * th) * GELU_K * (1.0 + 3.0 * GELU_C * gate * gate)
        for g, dc in enumerate((da * up * dgelu, da * gelu)):
            w = w_ref[g]
            dh = w[2:3] * dc[:TR] + w[1:2] * pltpu.roll(dc, n - 1, 0)[:TR] + w[0:1] * pltpu.roll(dc, n - 2, 0)[:TR]
            dh_ref[g] = dh.astype(BF16)
            dcm = dc[:TR]
            for r in range(3):
                dw_ref[g, r:r + 1, :] += jnp.sum(dcm * taps[g][r][:TR], axis=0, keepdims=True)
            dw_ref[g, 3:4, :] += jnp.sum(dcm, axis=0, keepdims=True)

    main, prev, nxt, par = _conv_specs(TC)
    per = TR // CONV_HALO
    return pl.pallas_call(
        body, name="convgate_bwd", grid=(D_FF // TC, nr),
        in_specs=[main, prev, nxt, pl.BlockSpec((TR, TC), lambda j, i: (i, j)),
                  pl.BlockSpec((CONV_HALO, TC), lambda j, i: (jnp.minimum((i + 1) * per, S // CONV_HALO - 1), j)), par],
        out_specs=[main, par],
        out_shape=[jax.ShapeDtypeStruct((2, S, D_FF), BF16), jax.ShapeDtypeStruct((2, SUBLANES, D_FF), F32)],
        compiler_params=_params(("parallel", "arbitrary")),
    )(hid, hid, hid, dact, dact, cwb)


def _adam_update(w, g, m, v):
    m = ADAM_B1 * m + (1.0 - ADAM_B1) * g
    v = ADAM_B2 * v + (1.0 - ADAM_B2) * (g * g)
    m_hat = m / (1.0 - ADAM_B1 ** ADAM_STEP)
    v_hat = v / (1.0 - ADAM_B2 ** ADAM_STEP)
    return -ADAM_LR * (m_hat / (jnp.sqrt(v_hat) + ADAM_EPS) + ADAM_WD * w), m, v


def _row_tile(rows, cols, itemsize=4, target=TILE_BYTES):
    tr = SUBLANES
    while rows % (2 * tr) == 0 and 2 * tr * cols * itemsize <= target:
        tr *= 2
    assert rows % tr == 0, (rows, tr)
    return rows if rows % (2 * tr) and 16 * tr * cols * itemsize < target else tr


def _adamw(name, w, g, m, v):
    rows, cols = w.shape
    tr = rows if rows * cols * 4 <= TILE_BYTES // 2 else _row_tile(rows, cols, target=TILE_BYTES // 2)

    def body(w_ref, g_ref, m_ref, v_ref, d_ref, nm_ref, nv_ref):
        d_ref[...], nm_ref[...], nv_ref[...] = _adam_update(w_ref[...], g_ref[...], m_ref[...], v_ref[...])

    spec = _row_spec(tr, cols)
    shape = jax.ShapeDtypeStruct((rows, cols), F32)
    return pl.pallas_call(
        body, name=name, grid=(rows // tr,), in_specs=[spec] * 4, out_specs=[spec] * 3, out_shape=[shape] * 3,
        compiler_params=_params(("parallel",)),
    )(w, g, m, v)


def _adamw_halves(name, core, w, g_mine, g_sibling, m, v):
    rows, cols = w.shape
    half = rows // 2
    tr = _row_tile(half, cols, target=TILE_BYTES // 2)
    per = half // tr

    def body(core_ref, w_ref, gm_ref, gs_ref, m_ref, v_ref, g_ref, d_ref, nm_ref, nv_ref):
        g = jnp.where(pl.program_id(0) // per == core_ref[0], gm_ref[...], gs_ref[...])
        g_ref[...] = g
        d_ref[...], nm_ref[...], nv_ref[...] = _adam_update(w_ref[...], g, m_ref[...], v_ref[...])

    spec = pl.BlockSpec((tr, cols), lambda i, core_ref: (i, 0))
    half_spec = pl.BlockSpec((tr, cols), lambda i, core_ref: (i % per, 0))
    shape = jax.ShapeDtypeStruct((rows, cols), F32)
    return pl.pallas_call(
        body, name=name, out_shape=[shape] * 4,
        grid_spec=pltpu.PrefetchScalarGridSpec(
            num_scalar_prefetch=1, grid=(rows // tr,), in_specs=[spec, half_spec, half_spec, spec, spec], out_specs=[spec] * 4),
        compiler_params=_params(("parallel",)),
    )(core, w, g_mine, g_sibling, m, v)


def _adamw_halves_columns(name, core, w, g_mine, g_sibling, m, v):
    cols, _, rows = w.shape
    tl = 2 * LANES
    per = rows // 2 // tl

    def body(core_ref, w_ref, gm_ref, gs_ref, m_ref, v_ref, g_ref, d_ref, nm_ref, nv_ref):
        g = jnp.where(pl.program_id(0) // per == core_ref[0], gm_ref[...], gs_ref[...])
        g_ref[...] = g
        d_ref[...], nm_ref[...], nv_ref[...] = _adam_update(w_ref[...], g, m_ref[...], v_ref[...])

    spec = pl.BlockSpec((cols, 1, tl), lambda i, core_ref: (0, 0, i))
    half_spec = pl.BlockSpec((cols, 1, tl), lambda i, core_ref: (0, 0, i % per))
    shape = jax.ShapeDtypeStruct((cols, 1, rows), F32)
    return pl.pallas_call(
        body, name=name, out_shape=[shape] * 4,
        grid_spec=pltpu.PrefetchScalarGridSpec(
            num_scalar_prefetch=1, grid=(rows // tl,), in_specs=[spec, half_spec, half_spec, spec, spec], out_specs=[spec] * 4),
        compiler_params=_params(("parallel",)),
    )(core, w, g_mine, g_sibling, m, v)


def _chip_sum(name, core, g, other):
    _, _, half, cols = g.shape
    tr = _row_tile(half, cols)

    def body(core_ref, g_ref, o_ref, p_ref):
        p_ref[...] = (g_ref[...] + o_ref[...]).astype(BF16)

    spec = pl.BlockSpec((None, tr, cols), lambda j, i, core_ref: (j, i, 0))
    return pl.pallas_call(
        body, name=name, out_shape=jax.ShapeDtypeStruct((N_CHIPS, half, cols), BF16),
        grid_spec=pltpu.PrefetchScalarGridSpec(
            num_scalar_prefetch=1, grid=(N_CHIPS, half // tr),
            in_specs=[pl.BlockSpec((None, None, tr, cols), lambda j, i, core_ref: (j, core_ref[0], i, 0)), spec],
            out_specs=spec),
        compiler_params=_params(("parallel", "parallel")),
    )(core, g, other)


def _mesh_sum(name, chip, received, own):
    _, half, cols = received.shape
    tr = _row_tile(half, cols, itemsize=2 * N_CHIPS)

    def body(chip_ref, r_ref, own_ref, o_ref):
        acc = None
        for j in range(N_CHIPS):
            term = jnp.where(chip_ref[0] == j, own_ref[...], r_ref[j]).astype(F32)
            acc = term if acc is None else acc + term
        o_ref[...] = acc

    return pl.pallas_call(
        body, name=name, out_shape=jax.ShapeDtypeStruct((half, cols), F32),
        grid_spec=pltpu.PrefetchScalarGridSpec(
            num_scalar_prefetch=1, grid=(half // tr,),
            in_specs=[pl.BlockSpec((N_CHIPS, tr, cols), lambda i, chip_ref: (0, i, 0)),
                      pl.BlockSpec((None, tr, cols), lambda i, chip_ref: (chip_ref[0], i, 0))],
            out_specs=pl.BlockSpec((tr, cols), lambda i, chip_ref: (i, 0))),
        compiler_params=_params(("parallel",)),
    )(chip, received, own)


CHIP_FLIPS = ((1, 0), (0, 1), (1, 1))


def _place():
    x, y, c = lax.axis_index("x"), lax.axis_index("y"), lax.axis_index("c")
    return x, y, c, 2 * x + y


def _remote(src, dst, sems_s, sems_r, k, dev):
    return pltpu.make_async_remote_copy(src_ref=src, dst_ref=dst, send_sem=sems_s.at[k], recv_sem=sems_r.at[k],
                                        device_id=dev, device_id_type=MESH)


class _Exchange:
    def __init__(self, ins, out_shapes, n_sems, start, forward, finish):
        self.ins, self.out_shapes, self.n_sems = list(ins), list(out_shapes), n_sems
        self.start, self.forward, self.finish = start, forward, finish

    def scratch(self):
        return [pltpu.SemaphoreType.DMA((self.n_sems,)), pltpu.SemaphoreType.DMA((self.n_sems,))]

    def run(self, name):
        n = len(self.ins)

        def body(*refs):
            args = (refs[:n], refs[n:2 * n]) + tuple(refs[2 * n:])
            self.start(*args)
            self.forward(*args)
            self.finish(*args)

        return pl.pallas_call(
            body, name=name, in_specs=[ANY] * n, out_specs=[ANY] * n, out_shape=self.out_shapes, scratch_shapes=self.scratch(),
        )(*self.ins)


def _all_gather_weights(halved, whole):
    nh, nw = len(halved), len(whole)
    n_arr = nh + nw

    def copies(ins, outs, sems_s, sems_r):
        x, y, c, me = _place()
        sibling = (x, y, 1 - c)
        own = [_remote(ins[k], outs[k].at[me], sems_s, sems_r, k, sibling) for k in range(n_arr)]
        first, passed = [], []
        for k in range(n_arr):
            for f, (fx, fy) in enumerate(CHIP_FLIPS):
                src, dst = (ins[k].at[c], outs[k].at[me, c]) if k < nh else (ins[k], outs[k].at[me])
                first.append(_remote(src, dst, sems_s, sems_r, n_arr + 3 * k + f, (x ^ fx, y ^ fy, c)))
        for k in range(nh):
            for f, (fx, fy) in enumerate(CHIP_FLIPS):
                landed = outs[k].at[2 * (x ^ fx) + (y ^ fy), c]
                passed.append(_remote(landed, landed, sems_s, sems_r, 4 * n_arr + 3 * k + f, sibling))
        return own, first, passed

    def start(*refs):
        own, first, _ = copies(*refs)
        for cp in own + first:
            cp.start()

    def forward(*refs):
        _, first, passed = copies(*refs)
        for arrived, cp in zip(first, passed):
            arrived.wait_recv()
            cp.start()

    def finish(*refs):
        own, first, passed = copies(*refs)
        for cp in first[3 * nh:] + passed + own:
            cp.wait_recv()
        for cp in first + passed + own:
            cp.wait_send()

    shapes = [jax.ShapeDtypeStruct((N_CHIPS,) + a.shape, a.dtype) for a in list(halved) + list(whole)]
    return _Exchange(list(halved) + list(whole), shapes, 7 * nh + 4 * nw, start, forward, finish)


def _swap_halves(gs):
    n = len(gs)

    def copies(ins, outs, sems_s, sems_r):
        x, y, c, _ = _place()
        return [_remote(ins[k].at[:, 1 - c], outs[k], sems_s, sems_r, k, (x, y, 1 - c)) for k in range(n)]

    def start(*refs):
        for cp in copies(*refs):
            cp.start()

    def finish(*refs):
        for cp in copies(*refs):
            cp.wait()

    shapes = [jax.ShapeDtypeStruct((g.shape[0],) + g.shape[2:], g.dtype) for g in gs]
    return _Exchange(gs, shapes, n, start, _no_copies, finish)


def _scatter_chips(ps):
    n = len(ps)

    def copies(ins, outs, sems_s, sems_r):
        x, y, c, me = _place()
        return [_remote(ins[k].at[2 * (x ^ fx) + (y ^ fy)], outs[k].at[me], sems_s, sems_r, 3 * k + f, (x ^ fx, y ^ fy, c))
                for k in range(n) for f, (fx, fy) in enumerate(CHIP_FLIPS)]

    def start(*refs):
        for cp in copies(*refs):
            cp.start()

    def forward(*refs):
        pass

    def finish(*refs):
        for cp in copies(*refs):
            cp.wait()

    shapes = [jax.ShapeDtypeStruct(p.shape, p.dtype) for p in ps]
    return _Exchange(ps, shapes, 3 * n, start, forward, finish)


def _swap_reduced(rs):
    n = len(rs)

    def copies(ins, outs, sems_s, sems_r):
        x, y, c, _ = _place()
        return [_remote(ins[k], outs[k], sems_s, sems_r, k, (x, y, 1 - c)) for k in range(n)]

    def start(*refs):
        for cp in copies(*refs):
            cp.start()

    def finish(*refs):
        for cp in copies(*refs):
            cp.wait()

    return _Exchange(rs, [jax.ShapeDtypeStruct(r.shape, r.dtype) for r in rs], n, start, _no_copies, finish)


N_DEV = 8


def _gather_small(buf):
    def copies(ins, outs, sems_s, sems_r):
        x, y, c, chip = _place()
        sibling = (x, y, 1 - c)
        own = _remote(ins[0], outs[0].at[2 * chip + c], sems_s, sems_r, 0, sibling)
        first = [_remote(ins[0], outs[0].at[2 * chip + c], sems_s, sems_r, 1 + f, (x ^ fx, y ^ fy, c))
                 for f, (fx, fy) in enumerate(CHIP_FLIPS)]
        passed = []
        for f, (fx, fy) in enumerate(CHIP_FLIPS):
            landed = outs[0].at[2 * (2 * (x ^ fx) + (y ^ fy)) + c]
            passed.append(_remote(landed, landed, sems_s, sems_r, 4 + f, sibling))
        return own, first, passed

    def start(*refs):
        own, first, _ = copies(*refs)
        for cp in [own] + first:
            cp.start()

    def forward(*refs):
        _, first, passed = copies(*refs)
        for arrived, cp in zip(first, passed):
            arrived.wait_recv()
            cp.start()

    def finish(*refs):
        own, first, passed = copies(*refs)
        for cp in passed + [own]:
            cp.wait_recv()
        for cp in first + passed + [own]:
            cp.wait_send()

    return _Exchange([buf], [jax.ShapeDtypeStruct((N_DEV,) + buf.shape, buf.dtype)], N_DEV - 1, start, forward, finish)


def _sum_devices(place, gathered, own):
    rows = own.shape[0]

    def body(place_ref, g_ref, own_ref, o_ref):
        acc = None
        for d in range(N_DEV):
            term = jnp.where(place_ref[0] == d, own_ref[...], g_ref[d])
            acc = term if acc is None else acc + term
        o_ref[...] = acc

    return pl.pallas_call(
        body, name="sum_devices", out_shape=jax.ShapeDtypeStruct((rows, LANES), F32),
        grid_spec=pltpu.PrefetchScalarGridSpec(
            num_scalar_prefetch=1, grid=(1,),
            in_specs=[pl.BlockSpec((N_DEV, rows, LANES), lambda i, place_ref: (0, 0, 0)),
                      pl.BlockSpec((rows, LANES), lambda i, place_ref: (0, 0))],
            out_specs=pl.BlockSpec((rows, LANES), lambda i, place_ref: (0, 0))),
        compiler_params=_params(("arbitrary",)),
    )(place, gathered, own)


def _no_copies(*refs):
    pass


def _no_exchange():
    return _Exchange([], [], 1, _no_copies, _no_copies, _no_copies)


class _NoComm:
    def gather_first(self):
        return _no_exchange()

    def first_landed(self, p, landed):
        pass

    def gather_rest(self, p):
        return _no_exchange()

    def weights_landed(self, p, landed):
        pass

    def gather_last(self):
        return _no_exchange()

    def last_landed(self, p, landed):
        pass

    def swap_first(self, g):
        return _no_exchange()

    def first_swapped(self, landed):
        pass

    def swap_second(self, g):
        return _no_exchange()

    def second_swapped(self, landed):
        pass

    def scatter_early(self, g):
        return _no_exchange()

    def scatter_landed(self, landed):
        pass

    def swap_reduced_early(self):
        return _no_exchange()

    def reduced_landed(self, landed):
        pass

    def scatter_late(self, g):
        return _no_exchange()

    def late_landed(self, landed):
        pass


def _local_step(x, mem, target, p, comm):
    h1, landed = _norm_fwd("norm_mix_pre", x, p["norm_mix_pre"], comm.gather_first())
    comm.first_landed(p, landed)
    qa, ka, va, u, z = _in_proj(h1, p["w_in"], p["bf_pad"])
    ycat, qab, landed = _fox_fwd(qa, ka, va, comm.gather_rest(p))
    comm.weights_landed(p, landed)
    ycat = _pool_fwd(u, p["w_pool_bd"], p["pool_scale"], ycat)
    y1, x2, h2, qx = _proj_resid_norm("mix_out", ycat, p["w_mix_out"], x, p["norm_mix_post"], p["norm_xa_pre"], p["w_xq"])
    mem_n = _norm_fwd("norm_mem", mem, p["norm_mem"])
    kv = _mm(
        "xkv", mem_n, p["w_xkv"], pl.BlockSpec((MEM, D), lambda i, j, k: (0, 0)),
        pl.BlockSpec((None, D, 512), lambda i, j, k: (j, 0, 0)), jax.ShapeDtypeStruct((MEM, 2 * D), BF16),
        pl.BlockSpec((MEM, 512), lambda i, j, k: (0, j)), (1, N_CHIPS, 1), NN, (MEM, 512))
    xo = _xattn_fwd(qx, kv)
    y2, x3, h3 = _proj_resid_norm("xo", xo, p["w_xo"], x2, p["norm_xa_post"], p["norm_ffn_pre"])
    hid, landed = _mm(
        "up_proj", h3, p["w_up"], pl.BlockSpec((2048, D), lambda i, j, k: (i, 0)),
        pl.BlockSpec((None, D, 1024), lambda i, j, k: (j // 2, 0, j % 2)), jax.ShapeDtypeStruct((2, S, D_FF), F32),
        pl.BlockSpec((None, 2048, 1024), lambda i, j, k: (j // 4, i, j % 4)), (S // 2048, 8, 1), NN, (2048, 1024),
        comm.gather_last())
    comm.last_landed(p, landed)
    act = _convgate_fwd(hid, p["cwb"])

    g = {}
    dres, dy3, g["norm_ffn_post"], loss_cols = _down_loss_bwd(act, p["w_down"], x3, p["norm_ffn_post"], target)
    dact = _mm_nt("d_act", dy3, p["w_down"], F32, 2048, 1024)
    g["w_down"] = _mm_tn("dw_down", act, dy3, 1024, 512)
    dhid, dcwb = _convgate_bwd(hid, dact, p["cwb"])
    g["w_up"] = _mm(
        "dw_up", h3, dhid, pl.BlockSpec((S, D), lambda i, j, k: (0, 0)),
        pl.BlockSpec((None, S, 512), lambda i, j, k: (j // 8, 0, j % 8)), jax.ShapeDtypeStruct((N_CHIPS, D, 2048), F32),
        pl.BlockSpec((None, D, 512), lambda i, j, k: (j // 4, 0, j % 4)), (1, 16, 1), TN, (D, 512))
    dh3, landed = _d_h3(dhid, p["w_up"], comm.swap_first(g))
    comm.first_swapped(landed)
    dres, dy2, dxo, g["norm_ffn_pre"], g["norm_xa_post"] = _mid_bwd(
        "bwd_ffn_xa", dres, x3, p["norm_ffn_pre"], dh3, y2, p["norm_xa_post"], p["w_xo"])
    g["w_xo"] = _mm_tn("dw_xo", xo, dy2, 1024, 512)
    dqx, dkv = _xattn_bwd(qx, kv, dxo)
    dkv = dkv.astype(BF16)
    g["w_xq"] = _mm_tn("dw_xq", h2, dqx, 1024, 512)
    dmem_n = _mm(
        "d_mem", dkv, p["w_xkv"], pl.BlockSpec((MEM, 512), lambda i, j, k: (0, k)),
        pl.BlockSpec((None, D, 512), lambda i, j, k: (k, 0, 0)), jax.ShapeDtypeStruct((MEM, D), F32),
        pl.BlockSpec((MEM, D), lambda i, j, k: (0, 0)), (1, 1, N_CHIPS), NT, (MEM, D))
    g["w_xkv"] = _mm(
        "dw_xkv", mem_n, dkv, pl.BlockSpec((MEM, D), lambda i, j, k: (0, 0)),
        pl.BlockSpec((MEM, 512), lambda i, j, k: (0, j)), jax.ShapeDtypeStruct((N_CHIPS, D, 512), F32),
        pl.BlockSpec((None, D, 512), lambda i, j, k: (j, 0, 0)), (1, N_CHIPS, 1), TN, (D, 512))
    g["norm_mem"] = _gain_bwd("dg_mem", mem, p["norm_mem"], dmem_n)
    (dres, dy1, g["norm_xa_pre"], g["norm_mix_post"], dy_pool, doa), landed = _bwd_xa_mix(
        dqx, p["w_xq"], dres, x2, p["norm_xa_pre"], y1, p["norm_mix_post"], p["w_mix_out"], ycat, comm.swap_second(g))
    comm.second_swapped(landed)
    g["w_mix_out"] = _mm_tn("dw_mix_out", ycat, dy1, 1024, 512)
    dqa, dka, dva, landed = _fox_bwd(qab, doa, ka, va, comm.scatter_early(g))
    comm.scatter_landed(landed)
    du, g["w_pool_full"], g["pool_scale"] = _pool_bwd(u, dy_pool, p["w_pool_bd"], p["w_pool_bd_t"], p["pool_scale"])
    dproj, g["bf_pad"] = _fox_bwd_post(dqa, dka, dva, du, z, p["bf_pad"])
    g["w_in"], landed = _dw_in(dproj, h1, comm.swap_reduced_early())
    comm.reduced_landed(landed)
    dh1, landed = _mm_nt("d_h1", dproj, p["w_in"], F32, 1024, 1024, comm.scatter_late(g))
    comm.late_landed(landed)
    grad_x, g["norm_mix_pre"] = _first_bwd(dres, x, p["norm_mix_pre"], dh1)
    g["cwb"] = dcwb
    return grad_x, g, loss_cols


BIG = ("w_in", "w_mix_out", "w_xq", "w_xkv", "w_xo", "w_up", "w_down")
ROW_SHARDED = ("w_mix_out", "w_xq", "w_xo", "w_down")
SMALL = ("norm_mix_pre", "norm_mix_post", "b_forget", "w_pool", "pool_scale", "norm_mem", "norm_xa_pre", "norm_xa_post",
         "norm_ffn_pre", "norm_ffn_post", "conv_b")
ORDER = ("norm_mix_pre", "norm_mix_post", "w_in", "b_forget", "w_pool", "pool_scale", "w_mix_out", "norm_mem", "norm_xa_pre",
         "norm_xa_post", "w_xq", "w_xkv", "w_xo", "norm_ffn_pre", "norm_ffn_post", "w_up", "conv_w", "conv_b", "w_down")
SLOT = SUBLANES * LANES


def _pack(parts):
    rows, offs, off = [], [], 0
    for a in parts:
        flat = a.reshape(-1).astype(F32)
        n = -(-flat.shape[0] // SLOT) * SLOT
        rows.append(jnp.pad(flat, (0, n - flat.shape[0])).reshape(n // LANES, LANES))
        offs.append(off)
        off += n // LANES
    return jnp.concatenate(rows, axis=0), offs


def _unpack(buf, off, like):
    n = like.size
    rows = -(-n // LANES)
    return buf[off:off + rows].reshape(-1)[:n].reshape(like.shape)


FIRST = ("w_in",)
REST = ("w_mix_out", "w_xq", "w_xkv", "w_xo", "w_up")
LAST = ("w_down",)


def _local_params(w):
    w_pool_bd = jnp.zeros((D_POOL, D_POOL), F32)
    for gi in range(4):
        w_pool_bd = w_pool_bd.at[64 * gi:64 * (gi + 1), 64 * gi:64 * (gi + 1)].set(w["w_pool"][0, gi])
    p = {n: w[n] for n in ("norm_mix_pre", "norm_mix_post", "norm_mem", "norm_xa_pre", "norm_xa_post", "norm_ffn_pre",
                           "norm_ffn_post")}
    p.update(
        bf_pad=jnp.pad(w["b_forget"], ((0, 0), (0, LANES - HEADS))),
        w_pool_bd=w_pool_bd.astype(BF16), w_pool_bd_t=w_pool_bd.T.astype(BF16), pool_scale=w["pool_scale"].reshape(1, D_POOL))
    return p


def _w_in_param(stacked):
    n, rows, cols = stacked.shape
    tr = PREP_TR

    def body(w_ref, o_ref):
        o_ref[...] = jnp.concatenate([w_ref[j] for j in range(n)] + [jnp.zeros((tr, D_IN_PAD - n * cols), BF16)], axis=1)

    return pl.pallas_call(
        body, name="w_in_whole", grid=(rows // tr,), in_specs=[pl.BlockSpec((n, tr, cols), lambda i: (0, i, 0))],
        out_specs=_row_spec(tr, D_IN_PAD), out_shape=jax.ShapeDtypeStruct((rows, D_IN_PAD), BF16),
        compiler_params=_params(("parallel",)),
    )(stacked)


def _rest_params(w, full, conv_w_full):
    cw2 = conv_w_full.reshape(3, 2, D_FF).transpose(1, 0, 2)
    cwb = jnp.concatenate([cw2, w["conv_b"].reshape(1, 2, D_FF).transpose(1, 0, 2), jnp.zeros((2, 4, D_FF), F32)], axis=1)
    return dict(w_mix_out=full["w_mix_out"].reshape(D, D), w_xq=full["w_xq"].reshape(D, D), w_xkv=full["w_xkv"],
                w_xo=full["w_xo"].reshape(D, D), w_up=full["w_up"], cwb=cwb)


def _whole_params(w, full, conv_w_full):
    p = _local_params(w)
    p.update(_rest_params(w, full, conv_w_full), w_in=_w_in_param(full["w_in"]), w_down=full["w_down"].reshape(D_FF, D))
    return p


def _halved(a):
    return a.reshape(a.shape[:-2] + (2, a.shape[-2] // 2, a.shape[-1]))


class _StepComm:
    def __init__(self, w, shard2d, conv_w, core_id, chip_id):
        self.w, self.shard2d, self.conv_w, self.core_id, self.chip_id = w, shard2d, conv_w, core_id, chip_id
        self.first, self.second = ("w_up", "w_down"), ("w_xq", "w_xkv", "w_xo")
        self.early = self.first + self.second
        self.late = ("w_in", "w_mix_out")

    def gather_first(self):
        return _all_gather_weights([_halved(self.shard2d[n].astype(BF16)) for n in FIRST], [])

    def first_landed(self, p, landed):
        p["w_in"] = _w_in_param(landed[0].reshape((N_CHIPS,) + self.shard2d["w_in"].shape))

    def gather_rest(self, p):
        return _all_gather_weights([_halved(self.shard2d[n].astype(BF16)) for n in REST], [self.conv_w.reshape(3, -1)])

    def weights_landed(self, p, landed):
        full = {n: a.reshape((N_CHIPS,) + self.shard2d[n].shape) for n, a in zip(REST, landed)}
        conv_w_full = jnp.transpose(landed[-1], (1, 0, 2)).reshape(3, 2 * D_FF)
        p.update(_rest_params(self.w, full, conv_w_full))

    def gather_last(self):
        return _all_gather_weights([_halved(self.shard2d[n].astype(BF16)) for n in LAST], [])

    def last_landed(self, p, landed):
        p["w_down"] = landed[0].reshape(D_FF, D)

    def _view(self, g, n):
        return _halved(g[n].reshape((N_CHIPS,) + self.shard2d[n].shape))

    def swap_first(self, g):
        return _swap_halves([self._view(g, n) for n in self.first])

    def first_swapped(self, landed):
        self.from_sibling = dict(zip(self.first, landed))

    def swap_second(self, g):
        return _swap_halves([self._view(g, n) for n in self.second])

    def second_swapped(self, landed):
        self.from_sibling.update(zip(self.second, landed))

    def scatter_early(self, g):
        self.partial = [_chip_sum("chip_sum_" + n, self.core_id, self._view(g, n), self.from_sibling[n]) for n in self.early]
        return _scatter_chips(self.partial)

    def scatter_landed(self, landed):
        self.received = list(landed)

    def swap_reduced_early(self):
        self.reduced = [_mesh_sum("mesh_sum_" + n, self.chip_id, r, own)
                        for n, r, own in zip(self.early, self.received, self.partial)]
        return _swap_reduced(self.reduced)

    def reduced_landed(self, landed):
        self.reduced_sibling = list(landed)

    def scatter_late(self, g):
        views = [g["w_in"], self._view(g, "w_mix_out")]
        from_sibling = _swap_halves(views).run("swap_halves_late")
        self.partial_late = [_chip_sum("chip_sum_" + n, self.core_id, view, other)
                             for n, view, other in zip(self.late, views, from_sibling)]
        return _scatter_chips(self.partial_late)

    def late_landed(self, landed):
        self.received_late = list(landed)


def kernel(x, mem, norm_mix_pre, norm_mix_post, w_in, b_forget, w_pool, pool_scale, w_mix_out, norm_mem, norm_xa_pre, norm_xa_post, w_xq, w_xkv, w_xo, norm_ffn_pre, norm_ffn_post, w_up, conv_w, conv_b, w_down, loss_target, m_norm_mix_pre, m_norm_mix_post, m_w_in, m_b_forget, m_w_pool, m_pool_scale, m_w_mix_out, m_norm_mem, m_norm_xa_pre, m_norm_xa_post, m_w_xq, m_w_xkv, m_w_xo, m_norm_ffn_pre, m_norm_ffn_post, m_w_up, m_conv_w, m_conv_b, m_w_down, v_norm_mix_pre, v_norm_mix_post, v_w_in, v_b_forget, v_w_pool, v_pool_scale, v_w_mix_out, v_norm_mem, v_norm_xa_pre, v_norm_xa_post, v_w_xq, v_w_xkv, v_w_xo, v_norm_ffn_pre, v_norm_ffn_post, v_w_up, v_conv_w, v_conv_b, v_w_down):
    w = dict(norm_mix_pre=norm_mix_pre, norm_mix_post=norm_mix_post, w_in=w_in, b_forget=b_forget, w_pool=w_pool,
             pool_scale=pool_scale, w_mix_out=w_mix_out, norm_mem=norm_mem, norm_xa_pre=norm_xa_pre, norm_xa_post=norm_xa_post,
             w_xq=w_xq, w_xkv=w_xkv, w_xo=w_xo, norm_ffn_pre=norm_ffn_pre, norm_ffn_post=norm_ffn_post, w_up=w_up,
             conv_w=conv_w, conv_b=conv_b, w_down=w_down)
    m = dict(norm_mix_pre=m_norm_mix_pre, norm_mix_post=m_norm_mix_post, w_in=m_w_in, b_forget=m_b_forget, w_pool=m_w_pool,
             pool_scale=m_pool_scale, w_mix_out=m_w_mix_out, norm_mem=m_norm_mem, norm_xa_pre=m_norm_xa_pre,
             norm_xa_post=m_norm_xa_post, w_xq=m_w_xq, w_xkv=m_w_xkv, w_xo=m_w_xo, norm_ffn_pre=m_norm_ffn_pre,
             norm_ffn_post=m_norm_ffn_post, w_up=m_w_up, conv_w=m_conv_w, conv_b=m_conv_b, w_down=m_w_down)
    v = dict(norm_mix_pre=v_norm_mix_pre, norm_mix_post=v_norm_mix_post, w_in=v_w_in, b_forget=v_b_forget, w_pool=v_w_pool,
             pool_scale=v_pool_scale, w_mix_out=v_w_mix_out, norm_mem=v_norm_mem, norm_xa_pre=v_norm_xa_pre,
             norm_xa_post=v_norm_xa_post, w_xq=v_w_xq, w_xkv=v_w_xkv, w_xo=v_w_xo, norm_ffn_pre=v_norm_ffn_pre,
             norm_ffn_post=v_norm_ffn_post, w_up=v_w_up, conv_w=v_conv_w, conv_b=v_conv_b, w_down=v_w_down)
    chip = 2 * lax.axis_index("x") + lax.axis_index("y")

    core_id = lax.axis_index("c").astype(jnp.int32).reshape(1)
    chip_id = chip.astype(jnp.int32).reshape(1)

    shard2d = {n: w[n][0] for n in BIG}
    p = _local_params(w)
    comm = _StepComm(w, shard2d, conv_w, core_id, chip_id)
    grad_x, g, loss_cols = _local_step(x[0], mem[0], loss_target[0], p, comm)

    reduced_late = [_mesh_sum("mesh_sum_" + n, chip_id, r, own)
                    for n, r, own in zip(comm.late, comm.received_late, comm.partial_late)]
    names = comm.late + comm.early
    reduced = reduced_late + comm.reduced
    reduced_sibling = list(_swap_reduced(reduced_late).run("swap_reduced_late")) + comm.reduced_sibling
    grads = {}

    gw_pool = jnp.stack([g["w_pool_full"][64 * gi:64 * (gi + 1), 64 * gi:64 * (gi + 1)] for gi in range(4)])
    dcwb = g["cwb"]
    g_conv_w = dcwb[:, 0:3, :].transpose(1, 0, 2).reshape(3, 2 * D_FF)
    g_conv_b = dcwb[:, 3, :].reshape(2 * D_FF)
    small_g = dict(norm_mix_pre=g["norm_mix_pre"], norm_mix_post=g["norm_mix_post"], b_forget=g["bf_pad"][:, :HEADS],
                   w_pool=gw_pool, pool_scale=g["pool_scale"], norm_mem=g["norm_mem"], norm_xa_pre=g["norm_xa_pre"],
                   norm_xa_post=g["norm_xa_post"], norm_ffn_pre=g["norm_ffn_pre"], norm_ffn_post=g["norm_ffn_post"],
                   conv_b=g_conv_b)
    local_buf, offs = _pack([small_g[n] for n in SMALL] + [g_conv_w, loss_cols])

    delta, new_m, new_v = {}, {}, {}
    for n, g_mine, g_sibling in zip(names, reduced, reduced_sibling):
        cols = shard2d[n].shape[1]
        if cols % LANES:
            outs = _adamw_halves_columns("adamw_" + n, core_id, jnp.transpose(w[n], (2, 0, 1)), g_mine[:cols, None, :],
                                         g_sibling[:cols, None, :], jnp.transpose(m[n], (2, 0, 1)), jnp.transpose(v[n], (2, 0, 1)))
            gn, d, nm, nv = (jnp.transpose(o, (1, 2, 0)) for o in outs)
        else:
            gn, d, nm, nv = (o[None] for o in _adamw_halves("adamw_" + n, core_id, shard2d[n], g_mine, g_sibling, m[n][0], v[n][0]))
        grads[n], delta[n], new_m[n], new_v[n] = gn, d, nm, nv
    place = (2 * chip + lax.axis_index("c")).astype(jnp.int32).reshape(1)
    buf = _sum_devices(place, _gather_small(local_buf).run("gather_small")[0], local_buf)
    for n, off in zip(SMALL, offs):
        grads[n] = _unpack(buf, off, w[n])
    g_conv_w = _unpack(buf, offs[len(SMALL)], g_conv_w)
    grads["conv_w"] = lax.dynamic_slice_in_dim(g_conv_w, chip * (2 * D_FF // N_CHIPS), 2 * D_FF // N_CHIPS, axis=1).reshape(conv_w.shape)
    loss = jnp.sum(_unpack(buf, offs[len(SMALL) + 1], loss_cols))
    small_names = SMALL + ("conv_w",)
    packed = [_pack([d[n] for n in small_names])[0] for d in (w, grads, m, v)]
    offs = _pack([w[n] for n in small_names])[1]
    d, nm, nv = _adamw("adamw_small", *packed)
    for n, off in zip(small_names, offs):
        delta[n], new_m[n], new_v[n] = _unpack(d, off, w[n]), _unpack(nm, off, w[n]), _unpack(nv, off, w[n])

    return (loss, grad_x[None], *[grads[n] for n in ORDER], *[delta[n] for n in ORDER], *[new_m[n] for n in ORDER],
            *[new_v[n] for n in ORDER])
```

```python
import functools

import jax
import jax.numpy as jnp
import numpy as np
from jax import lax
from jax.experimental import pallas as pl
from jax.experimental.pallas import tpu as pltpu

F32 = jnp.float32
BF16 = jnp.bfloat16
MESH = pl.DeviceIdType.MESH
ANY = pl.BlockSpec(memory_space=pl.ANY)
VMEM_SPEC = pl.BlockSpec(memory_space=pltpu.VMEM)

S = 4096
D = 1024
MEM = 256
D_POOL = 256
HEADS = 12
DH = 64
D_FOX = HEADS * DH
D_IN = D_POOL + 3 * D_FOX + HEADS
F_OFF = D_POOL + 3 * D_FOX
Q_OFF, K_OFF, V_OFF = D_POOL, D_POOL + D_FOX, D_POOL + 2 * D_FOX
XA_HEADS = 4
XA_DH = 256
D_FF = 4096
EPS = 1e-6
N_CHIPS = 4
ADAM_LR, ADAM_B1, ADAM_B2, ADAM_EPS, ADAM_WD, ADAM_STEP = 0.001, 0.9, 0.999, 1e-08, 0.01, 10

LANES = 128
SUBLANES = 8
D_IN_PAD = 21 * LANES
TR = 512
ROW_PIECES = 4
TILE_BYTES = 2 * 1024 * 1024
NEG = -1e30
VMEM_LIMIT = 52 * 1024 * 1024

NN = (((1,), (0,)), ((), ()))
NT = (((1,), (1,)), ((), ()))
TN = (((0,), (0,)), ((), ()))


def _dot(a, b, dims=NN):
    return lax.dot_general(a, b, dims, preferred_element_type=F32)


def _params(sem):
    return pltpu.CompilerParams(dimension_semantics=sem, vmem_limit_bytes=VMEM_LIMIT)


def _split3(x):
    hi = x.astype(BF16)
    r = x - hi.astype(F32)
    mid = r.astype(BF16)
    lo = (r - mid.astype(F32)).astype(BF16)
    return hi, mid, lo


def _split3_f32(x):
    hi = x.astype(BF16).astype(F32)
    r = x - hi
    mid = r.astype(BF16).astype(F32)
    return hi, mid, r - mid


def _lane_iota(shape):
    return lax.broadcasted_iota(jnp.int32, shape, len(shape) - 1)


def _row_iota(shape):
    return lax.broadcasted_iota(jnp.int32, shape, len(shape) - 2)


def _mm(name, a, b, a_spec, b_spec, out_shape, out_spec, grid, dims, acc_shape, ex=None):
    nk = grid[2]
    if ex is not None:
        return _mm_hosting(name, a, b, a_spec, b_spec, out_shape, out_spec, grid, dims, ex)

    def body(a_ref, b_ref, o_ref, *scr):
        p = _dot(a_ref[...], b_ref[...], dims)
        if nk == 1:
            o_ref[...] = p.astype(o_ref.dtype)
        else:
            acc = scr[0]
            k = pl.program_id(2)

            @pl.when(k == 0)
            def _():
                acc[...] = p

            @pl.when(k > 0)
            def _():
                acc[...] += p

            @pl.when(k == nk - 1)
            def _():
                o_ref[...] = acc[...].astype(o_ref.dtype)

    return pl.pallas_call(
        body, name=name, grid=grid, in_specs=[a_spec, b_spec], out_specs=out_spec, out_shape=out_shape,
        scratch_shapes=[pltpu.VMEM(acc_shape, F32)] if nk > 1 else [],
        compiler_params=_params(("parallel", "parallel", "arbitrary")),
    )(a, b)


def _mm_hosting(name, a, b, a_spec, b_spec, out_shape, out_spec, grid, dims, ex):
    assert grid[2] == 1
    n = len(ex.ins)

    def body(*refs):
        i, j = pl.program_id(0), pl.program_id(1)
        last = (i == grid[0] - 1) & (j == grid[1] - 1)
        (a_ref, b_ref), (o_ref,), _, begin, end = _hosted(ex, refs, 2, 1, (i == 0) & (j == 0), last, last)
        begin()
        o_ref[...] = _dot(a_ref[...], b_ref[...], dims).astype(o_ref.dtype)
        end()

    res = pl.pallas_call(
        body, name=name, grid=grid, in_specs=[a_spec, b_spec] + [ANY] * n, out_specs=[out_spec] + [ANY] * n,
        out_shape=[out_shape] + ex.out_shapes, scratch_shapes=ex.scratch(),
        compiler_params=_params(("arbitrary", "arbitrary", "arbitrary")),
    )(a, b, *ex.ins)
    return res[0], res[1:]


def _mm_nn(name, a, b, out_dtype, tm, tn):
    m, k = a.shape
    n = b.shape[1]
    return _mm(name, a, b, pl.BlockSpec((tm, k), lambda i, j, kk: (i, 0)), pl.BlockSpec((k, tn), lambda i, j, kk: (0, j)),
               jax.ShapeDtypeStruct((m, n), out_dtype), pl.BlockSpec((tm, tn), lambda i, j, kk: (i, j)),
               (m // tm, n // tn, 1), NN, (tm, tn))


def _mm_nt(name, a, b, out_dtype, tm, tn, ex=None):
    m, k = a.shape
    n = b.shape[0]
    return _mm(name, a, b, pl.BlockSpec((tm, k), lambda i, j, kk: (i, 0)), pl.BlockSpec((tn, k), lambda i, j, kk: (j, 0)),
               jax.ShapeDtypeStruct((m, n), out_dtype), pl.BlockSpec((tm, tn), lambda i, j, kk: (i, j)),
               (m // tm, n // tn, 1), NT, (tm, tn), ex)


def _mm_tn(name, a, b, tka, tn, ex=None):
    t, ka = a.shape
    n = b.shape[1]
    return _mm(name, a, b, pl.BlockSpec((t, tka), lambda i, j, kk: (0, i)), pl.BlockSpec((t, tn), lambda i, j, kk: (0, j)),
               jax.ShapeDtypeStruct((ka, n), F32), pl.BlockSpec((tka, tn), lambda i, j, kk: (i, j)),
               (ka // tka, n // tn, 1), TN, (tka, tn), ex)


def _d_h3(dhid, w_up, ex):
    tm = tn = 1024
    shard = 2 * D_FF // N_CHIPS
    per_plane = D_FF // shard
    grid = (S // tm, D // tn, N_CHIPS)
    n = len(ex.ins)

    def body(*refs):
        i, j, k = pl.program_id(0), pl.program_id(1), pl.program_id(2)
        first = (i == 0) & (j == 0) & (k == 0)
        last = (i == grid[0] - 1) & (j == grid[1] - 1) & (k == N_CHIPS - 1)
        (a_ref, b_ref), (o_ref,), (acc_ref,), begin, end = _hosted(ex, refs, 2, 1, first, first, last)
        begin()
        part = _dot(a_ref[...], b_ref[...], NT)

        @pl.when(k == 0)
        def _():
            acc_ref[...] = part

        @pl.when(k > 0)
        def _():
            acc_ref[...] += part

        @pl.when(k == N_CHIPS - 1)
        def _():
            o_ref[...] = acc_ref[...]

        end()

    res = pl.pallas_call(
        body, name="d_h3", grid=grid,
        in_specs=[pl.BlockSpec((None, tm, shard), lambda i, j, k: (k // per_plane, i, k % per_plane)),
                  pl.BlockSpec((None, tn, shard), lambda i, j, k: (k, j, 0))] + [ANY] * n,
        out_specs=[pl.BlockSpec((tm, tn), lambda i, j, k: (i, j))] + [ANY] * n,
        out_shape=[jax.ShapeDtypeStruct((S, D), F32)] + ex.out_shapes,
        scratch_shapes=[pltpu.VMEM((tm, tn), F32)] + ex.scratch(),
        compiler_params=_params(("arbitrary", "arbitrary", "arbitrary")),
    )(dhid, w_up, *ex.ins)
    return res[0], res[1:]


SHARD_IN = D_IN // N_CHIPS
SHARD_IN_PAD = -(-SHARD_IN // SUBLANES) * SUBLANES


def _dw_in(dproj, h1, ex):
    tk = 1024
    nk = S // tk
    half = D // 2
    starts = [SHARD_IN * j // LANES * LANES for j in range(N_CHIPS)]
    shifts = [SHARD_IN * j - s for j, s in enumerate(starts)]
    window = -(-(max(shifts) + SHARD_IN) // LANES) * LANES
    assert starts[-1] + window <= dproj.shape[1]
    n = len(ex.ins)

    def body(*refs):
        k = pl.program_id(0)
        (a_ref, b_ref), (o_ref,), _, begin, end = _hosted(ex, refs, 2, 1, k == 0, k == nk - 1, k == nk - 1)
        begin()

        @pl.when(k == 0)
        def _():
            o_ref[...] = jnp.zeros(o_ref.shape, F32)

        for j in range(N_CHIPS):
            win = a_ref[:, starts[j]:starts[j] + window]
            if shifts[j]:
                win = pltpu.roll(win, window - shifts[j], axis=1)
            part = _dot(win, b_ref[...], TN)
            for h in range(2):
                o_ref[j, h] += part[:SHARD_IN_PAD, h * half:(h + 1) * half]
        end()

    out_shape = (N_CHIPS, 2, SHARD_IN_PAD, half)
    res = pl.pallas_call(
        body, name="dw_in", grid=(nk,),
        in_specs=[pl.BlockSpec((tk, dproj.shape[1]), lambda k: (k, 0)), pl.BlockSpec((tk, D), lambda k: (k, 0))] + [ANY] * n,
        out_specs=[pl.BlockSpec(out_shape, lambda k: (0, 0, 0, 0))] + [ANY] * n,
        out_shape=[jax.ShapeDtypeStruct(out_shape, F32)] + ex.out_shapes,
        scratch_shapes=ex.scratch(),
        compiler_params=_params(("arbitrary",)),
    )(dproj, h1, *ex.ins)
    return res[0], res[1:]


def _rms(x, g):
    r = lax.rsqrt(jnp.mean(x * x, axis=-1, keepdims=True) + EPS)
    return x * r * g


def _rms_bwd(x, g, dy):
    r = lax.rsqrt(jnp.mean(x * x, axis=-1, keepdims=True) + EPS)
    xh = x * r
    dxh = dy * g
    dx = r * (dxh - xh * jnp.mean(dxh * xh, axis=-1, keepdims=True))
    return dx, jnp.sum(dy * xh, axis=0, keepdims=True)


def _row_spec(tr, width):
    return pl.BlockSpec((tr, width), lambda i: (i, 0))


def _vec_spec(width):
    return pl.BlockSpec((1, width), lambda i: (0, 0))


def _norm_fwd(name, x, g, ex=None):
    rows, width = x.shape
    tr = min(TR, rows)
    steps = rows // tr
    hosted = ex if ex is not None else _no_exchange()
    n = len(hosted.ins)

    def body(*refs):
        i = pl.program_id(0)
        (x_ref, g_ref), (h_ref,), _, begin, end = _hosted(hosted, refs, 2, 1, i == 0, i == steps - 1, i == steps - 1)
        begin()
        h_ref[...] = _rms(x_ref[...], g_ref[...]).astype(BF16)
        end()

    res = pl.pallas_call(
        body, name=name, grid=(steps,), in_specs=[_row_spec(tr, width), _vec_spec(width)] + [ANY] * n,
        out_specs=[_row_spec(tr, width)] + [ANY] * n,
        out_shape=[jax.ShapeDtypeStruct((rows, width), BF16)] + hosted.out_shapes, scratch_shapes=hosted.scratch(),
        compiler_params=_params(("arbitrary",)),
    )(x, g, *hosted.ins)
    return res[0] if ex is None else (res[0], res[1:])


def _proj_resid_norm(name, a, w, xp, g_post, g_pre, w_next=None):
    def body(a_ref, w_ref, xp_ref, gpost_ref, gpre_ref, *rest):
        y_ref, xn_ref, h_ref = rest[-3:] if w_next is None else rest[1:4]
        y = _dot(a_ref[...], w_ref[...])
        y_ref[...] = y
        xn = xp_ref[...] + _rms(y, gpost_ref[...])
        xn_ref[...] = xn
        h = _rms(xn, gpre_ref[...]).astype(BF16)
        h_ref[...] = h
        if w_next is not None:
            rest[4][...] = _dot(h, rest[0][...]).astype(BF16)

    mat = pl.BlockSpec((D, D), lambda i: (0, 0))
    more = [] if w_next is None else [w_next]
    return pl.pallas_call(
        body, name=name, grid=(S // TR,),
        in_specs=[_row_spec(TR, D), mat, _row_spec(TR, D), _vec_spec(D), _vec_spec(D)] + [mat] * len(more),
        out_specs=[_row_spec(TR, D)] * (3 + len(more)),
        out_shape=[jax.ShapeDtypeStruct((S, D), F32), jax.ShapeDtypeStruct((S, D), F32), jax.ShapeDtypeStruct((S, D), BF16)]
        + [jax.ShapeDtypeStruct((S, D), BF16)] * len(more),
        compiler_params=_params(("parallel",)),
    )(a, w, xp, g_post, g_pre, *more)


def _down_loss_bwd(act, w_down, x3, g_post, target):
    def body(a_ref, w_ref, x_ref, g_ref, t_ref, dres_ref, dy_ref, dg_ref, loss_ref):
        i = pl.program_id(0)

        @pl.when(i == 0)
        def _():
            dg_ref[...] = jnp.zeros_like(dg_ref)
            loss_ref[...] = jnp.zeros_like(loss_ref)

        g = g_ref[...]
        for r in range(ROW_PIECES):
            rows = slice(r * TR // ROW_PIECES, (r + 1) * TR // ROW_PIECES)
            y = _dot(a_ref[rows, :], w_ref[...])
            e = x_ref[rows, :] + _rms(y, g) - t_ref[rows, :]
            loss_ref[...] += jnp.sum(e * e, axis=0, keepdims=True) * (0.5 / D)
            dres = e * (1.0 / D)
            dres_ref[rows, :] = dres
            dy, dg = _rms_bwd(y, g, dres)
            dy_ref[rows, :] = dy.astype(BF16)
            dg_ref[...] += dg

    return pl.pallas_call(
        body, name="down_loss_bwd", grid=(S // TR,),
        in_specs=[_row_spec(TR, D_FF), pl.BlockSpec((D_FF, D), lambda i: (0, 0)), _row_spec(TR, D), _vec_spec(D),
                  _row_spec(TR, D)],
        out_specs=[_row_spec(TR, D), _row_spec(TR, D), _vec_spec(D), _vec_spec(D)],
        out_shape=[jax.ShapeDtypeStruct((S, D), F32), jax.ShapeDtypeStruct((S, D), BF16),
                   jax.ShapeDtypeStruct((1, D), F32), jax.ShapeDtypeStruct((1, D), F32)],
        compiler_params=_params(("arbitrary",)),
    )(act, w_down, x3, g_post, target)


def _mid_bwd(name, dres, xcur, g_pre, dh, yprev, g_post, w):
    def body(dres_ref, x_ref, gpre_ref, dh_ref, y_ref, gpost_ref, w_ref, dx_ref, dy_ref, da_ref, dgpre_ref, dgpost_ref):
        i = pl.program_id(0)

        @pl.when(i == 0)
        def _():
            dgpre_ref[...] = jnp.zeros_like(dgpre_ref)
            dgpost_ref[...] = jnp.zeros_like(dgpost_ref)

        dxn, dgpre = _rms_bwd(x_ref[...], gpre_ref[...], dh_ref[...])
        dx = dres_ref[...] + dxn
        dx_ref[...] = dx
        dy, dgpost = _rms_bwd(y_ref[...], gpost_ref[...], dx)
        dy = dy.astype(BF16)
        dy_ref[...] = dy
        da_ref[...] = _dot(dy, w_ref[...], NT).astype(BF16)
        dgpre_ref[...] += dgpre
        dgpost_ref[...] += dgpost

    return pl.pallas_call(
        body, name=name, grid=(S // TR,),
        in_specs=[_row_spec(TR, D), _row_spec(TR, D), _vec_spec(D), _row_spec(TR, D), _row_spec(TR, D), _vec_spec(D),
                  pl.BlockSpec((D, D), lambda i: (0, 0))],
        out_specs=[_row_spec(TR, D), _row_spec(TR, D), _row_spec(TR, D), _vec_spec(D), _vec_spec(D)],
        out_shape=[jax.ShapeDtypeStruct((S, D), F32), jax.ShapeDtypeStruct((S, D), BF16), jax.ShapeDtypeStruct((S, D), BF16),
                   jax.ShapeDtypeStruct((1, D), F32), jax.ShapeDtypeStruct((1, D), F32)],
        compiler_params=_params(("arbitrary",)),
    )(dres, xcur, g_pre, dh, yprev, g_post, w)


def _d_h1_first_bwd(dproj, w_in, dres, x, g, ex):
    nt = S // TR
    n = len(ex.ins)

    def body(*refs):
        i = pl.program_id(0)
        (dp_ref, w_ref, dres_ref, x_ref, g_ref), (dx_ref, dg_ref), _, begin, end = _hosted(
            ex, refs, 5, 2, i == 0, i == nt - 1, i == nt - 1)
        begin()

        @pl.when(i == 0)
        def _():
            dg_ref[...] = jnp.zeros_like(dg_ref)

        dxn, dg = _rms_bwd(x_ref[...], g_ref[...], _dot(dp_ref[...], w_ref[...], NT))
        dx_ref[...] = dres_ref[...] + dxn
        dg_ref[...] += dg
        end()

    res = pl.pallas_call(
        body, name="d_h1", grid=(nt,),
        in_specs=[_row_spec(TR, D_IN_PAD), pl.BlockSpec((D, D_IN_PAD), lambda i: (0, 0)), _row_spec(TR, D), _row_spec(TR, D),
                  _vec_spec(D)] + [ANY] * n,
        out_specs=[_row_spec(TR, D), _vec_spec(D)] + [ANY] * n,
        out_shape=[jax.ShapeDtypeStruct((S, D), F32), jax.ShapeDtypeStruct((1, D), F32)] + ex.out_shapes,
        scratch_shapes=ex.scratch(), compiler_params=_params(("arbitrary",)),
    )(dproj, w_in, dres, x, g, *ex.ins)
    return res[0], res[1], res[2:]


def _gain_bwd(name, x, g, dy):
    rows, width = x.shape

    def body(x_ref, g_ref, dy_ref, dg_ref):
        _, dg = _rms_bwd(x_ref[...], g_ref[...], dy_ref[...])
        dg_ref[...] = dg

    return pl.pallas_call(
        body, name=name, grid=(1,), in_specs=[_row_spec(rows, width), _vec_spec(width), _row_spec(rows, width)],
        out_specs=_vec_spec(width), out_shape=jax.ShapeDtypeStruct((1, width), F32),
        compiler_params=_params(("arbitrary",)),
    )(x, g, dy)


CUM_Q = DH
CUM_K = DH + 3
LSE_Q = DH + 6
BOTH_ONE = DH + 9
DEN_V = DH
DELTA = DH + 1
PREP_TR = 256
PIECE_LANES = 16
FOX_FWD_BLOCK = 1024
FOX_BWD_BLOCK = 512


def _at(lane_of_even_head, h):
    return (lane_of_even_head + DH * (h % 2)) % LANES


def _data_lanes(lane, h):
    return lane >= DH if h % 2 else lane < DH


def _pair_block(ref, off, h):
    base = ((off + DH * h) // LANES) * LANES
    return ref[:, base:base + LANES]


def _cumsum_rows(x, tri, carry):
    hi, mid, lo = _split3(x)
    return _dot(tri, hi) + _dot(tri, mid) + _dot(tri, lo) + carry


def _in_proj(h1, w_in, bf_pad):
    tr = TR

    place_q = np.zeros((LANES, HEADS * LANES), np.float32)
    place_k = np.zeros((LANES, HEADS * LANES), np.float32)
    for h in range(HEADS):
        for piece in range(3):
            place_q[PIECE_LANES * piece + h, LANES * h + _at(CUM_Q, h) + piece] = 1.0
            place_k[PIECE_LANES * piece + h, LANES * h + _at(CUM_K, h) + piece] = -1.0

    def body(h_ref, w_ref, bf_ref, pq_ref, pk_ref, qa_ref, ka_ref, va_ref, u_ref, z_ref, carry_ref):
        i = pl.program_id(0)

        @pl.when(i == 0)
        def _():
            carry_ref[...] = jnp.zeros_like(carry_ref)

        proj = _dot(h_ref[...], w_ref[...])
        u_ref[...] = proj[:, :D_POOL]
        z_ref[...] = proj[:, F_OFF:F_OFF + LANES]
        lane = _lane_iota((tr, LANES))
        z = proj[:, F_OFF:F_OFF + LANES] + bf_ref[...]
        log_f = jnp.minimum(z, 0.0) - jnp.log(1.0 + jnp.exp(-jnp.abs(z)))
        log_f = jnp.where(lane < HEADS, log_f, 0.0)
        tri = jnp.where(_row_iota((tr, tr)) >= _lane_iota((tr, tr)), 1.0, 0.0).astype(BF16)
        cum = _cumsum_rows(log_f, tri, carry_ref[0:1, :])
        carry_ref[0:1, :] = cum[tr - 1:tr, :]
        c_hi, c_mid, c_lo = _split3_f32(cum)
        pieces = (c_hi + pltpu.roll(c_mid, PIECE_LANES, 1) + pltpu.roll(c_lo, 2 * PIECE_LANES, 1)).astype(BF16)
        cum_q = _dot(pieces, pq_ref[...])
        cum_k = _dot(pieces, pk_ref[...])

        def between(first, h):
            return (lane >= _at(first, h)) & (lane < _at(first, h) + 3)

        ones_q = [jnp.where(between(CUM_K, h) | (lane == _at(BOTH_ONE, h)), 1.0, 0.0) for h in range(2)]
        ones_k = [jnp.where(between(CUM_Q, h) | between(LSE_Q, h) | (lane == _at(BOTH_ONE, h)), 1.0, 0.0) for h in range(2)]
        aug_v = [jnp.where(lane == _at(DEN_V, h), 1.0, jnp.where(between(DELTA, h), -1.0, 0.0)) for h in range(2)]
        for h in range(HEADS):
            mine = slice(LANES * h, LANES * (h + 1))
            data = _data_lanes(lane, h)
            qa_ref[h] = jnp.where(data, _pair_block(proj, Q_OFF, h) * (DH ** -0.5), cum_q[:, mine] + ones_q[h % 2]).astype(BF16)
            ka_ref[h] = jnp.where(data, _pair_block(proj, K_OFF, h), cum_k[:, mine] + ones_k[h % 2]).astype(BF16)
            va_ref[h] = jnp.where(data, _pair_block(proj, V_OFF, h), aug_v[h % 2]).astype(BF16)

    head_spec = pl.BlockSpec((HEADS, tr, LANES), lambda i: (0, i, 0))
    head_shape = jax.ShapeDtypeStruct((HEADS, S, LANES), BF16)
    place_spec = pl.BlockSpec(place_q.shape, lambda i: (0, 0))
    return pl.pallas_call(
        body, name="in_proj", grid=(S // tr,),
        in_specs=[_row_spec(tr, D), pl.BlockSpec((D, D_IN_PAD), lambda i: (0, 0)), _vec_spec(LANES), place_spec, place_spec],
        out_specs=[head_spec] * 3 + [_row_spec(tr, D_POOL), _row_spec(tr, LANES)],
        out_shape=[head_shape] * 3 + [jax.ShapeDtypeStruct((S, D_POOL), F32), jax.ShapeDtypeStruct((S, LANES), F32)],
        scratch_shapes=[pltpu.VMEM((SUBLANES, LANES), F32)], compiler_params=_params(("arbitrary",)),
    )(h1, w_in, bf_pad, jnp.asarray(place_q, BF16), jnp.asarray(place_k, BF16))


def _hosted(ex, refs, n_blocked_in, n_blocked_out, first, forward_at, last):
    n = len(ex.ins)
    own_in = refs[:n_blocked_in]
    ex_in = refs[n_blocked_in:n_blocked_in + n]
    own_out = refs[n_blocked_in + n:n_blocked_in + n + n_blocked_out]
    ex_out = refs[n_blocked_in + n + n_blocked_out:n_blocked_in + 2 * n + n_blocked_out]
    rest = refs[n_blocked_in + 2 * n + n_blocked_out:]
    args = (ex_in, ex_out, rest[-2], rest[-1])

    def begin():
        @pl.when(first)
        def _():
            ex.start(*args)

        @pl.when(forward_at)
        def _():
            ex.forward(*args)

    def end():
        @pl.when(last)
        def _():
            ex.finish(*args)

    return own_in, own_out, rest[:-2], begin, end


def _fox_fwd(qa, ka, va, ex):
    BQ = BK = FOX_FWD_BLOCK
    nq = S // BQ
    n_pairs = HEADS // 2

    def body(*refs):
        p_id, i = pl.program_id(0), pl.program_id(1)
        (qa_ref, ka_ref, va_ref), (y_ref, qab_ref), (m_scr, acc_scr), begin, end = _hosted(
            ex, refs, 3, 2, (p_id == 0) & (i == 0), (p_id == n_pairs - 1) & (i == 0), (p_id == n_pairs - 1) & (i == nq - 1))
        begin()
        lane = _lane_iota((BQ, LANES))
        causal = _row_iota((BQ, BK)) >= _lane_iota((BQ, BK))
        m_scr[...] = jnp.full_like(m_scr, NEG)
        acc_scr[...] = jnp.zeros_like(acc_scr)

        def step(j, masked):
            rows = pl.ds(pl.multiple_of(j * BK, BK), BK)
            for hh in range(2):
                s = _dot(qa_ref[hh], ka_ref[hh, rows, :], NT)
                if masked:
                    s = jnp.where(causal, s, NEG)
                m_prev = m_scr[hh]
                m_new = jnp.maximum(m_prev, jnp.max(s, axis=1, keepdims=True))
                p = jnp.exp(s - jnp.tile(m_new, (1, BK // LANES)))
                acc_scr[hh] = jnp.exp(m_prev - m_new) * acc_scr[hh] + _dot(p.astype(BF16), va_ref[hh, rows, :])
                m_scr[hh] = m_new

        def full_step(j, carry):
            step(j, False)
            return carry

        lax.fori_loop(0, i, full_step, 0)
        step(i, True)
        outs = []
        for hh in range(2):
            acc = acc_scr[hh]
            den_lane, lse_lane = _at(DEN_V, hh), _at(LSE_Q, hh)
            den = jnp.broadcast_to(acc[:, den_lane:den_lane + 1], (BQ, LANES))
            outs.append(acc * (1.0 / den))
            n_hi, n_mid, n_lo = _split3(-(m_scr[hh] + jnp.log(den)))
            qab_ref[hh] = jnp.where(lane == lse_lane, n_hi,
                                    jnp.where(lane == lse_lane + 1, n_mid, jnp.where(lane == lse_lane + 2, n_lo, qa_ref[hh])))
        y_ref[...] = jnp.where(lane < DH, outs[0], outs[1]).astype(BF16)
        end()

    pair_rows = pl.BlockSpec((2, BQ, LANES), lambda p, i: (p, i, 0))
    pair_all = pl.BlockSpec((2, S, LANES), lambda p, i: (p, 0, 0))
    n = len(ex.ins)
    res = pl.pallas_call(
        body, name="fox_fwd", grid=(n_pairs, nq), in_specs=[pair_rows, pair_all, pair_all] + [ANY] * n,
        out_specs=[pl.BlockSpec((BQ, LANES), lambda p, i: (i, D_POOL // LANES + p)), pair_rows] + [ANY] * n,
        out_shape=[jax.ShapeDtypeStruct((S, D), BF16), jax.ShapeDtypeStruct((HEADS, S, LANES), BF16)] + ex.out_shapes,
        scratch_shapes=[pltpu.VMEM((2, BQ, LANES), F32), pltpu.VMEM((2, BQ, LANES), F32)] + ex.scratch(),
        compiler_params=_params(("arbitrary", "arbitrary")),
    )(qa, ka, va, *ex.ins)
    return res[0], res[1], res[2:]


def _bwd_xa_mix(dqx, w_xq, dres, x2, g_pre, y1, g_post, w_mix_out, ycat, ex):
    steps = S // TR
    n = len(ex.ins)

    def body(*refs):
        i = pl.program_id(0)
        ((dq_ref, wq_ref, dres_ref, x_ref, gpre_ref, y_ref, gpost_ref, wm_ref, ycat_ref),
         (dx_ref, dy_ref, dgpre_ref, dgpost_ref, dp_ref, doa_ref), _, begin, end) = _hosted(
            ex, refs, 9, 6, i == 0, i == 0, i == steps - 1)
        begin()

        @pl.when(i == 0)
        def _():
            dgpre_ref[...] = jnp.zeros_like(dgpre_ref)
            dgpost_ref[...] = jnp.zeros_like(dgpost_ref)

        dxn, dgpre = _rms_bwd(x_ref[...], gpre_ref[...], _dot(dq_ref[...], wq_ref[...], NT))
        dx = dres_ref[...] + dxn
        dx_ref[...] = dx
        dy, dgpost = _rms_bwd(y_ref[...], gpost_ref[...], dx)
        dy = dy.astype(BF16)
        dy_ref[...] = dy
        dgpre_ref[...] += dgpre
        dgpost_ref[...] += dgpost

        d = _dot(dy, wm_ref[...], NT)
        dp_ref[...] = d[:, :D_POOL]
        lane = _lane_iota((TR, LANES))
        low = lane < DH
        for p in range(HEADS // 2):
            cols = slice(D_POOL + LANES * p, D_POOL + LANES * (p + 1))
            do = d[:, cols]
            prod = do * ycat_ref[:, cols].astype(F32)
            deltas = (jnp.sum(jnp.where(low, prod, 0.0), axis=1, keepdims=True),
                      jnp.sum(jnp.where(low, 0.0, prod), axis=1, keepdims=True))
            for hh in range(2):
                d_hi, d_mid, d_lo = _split3_f32(deltas[hh])
                dl = _at(DELTA, hh)
                aug = jnp.where(lane == dl, d_hi, jnp.where(lane == dl + 1, d_mid, jnp.where(lane == dl + 2, d_lo, 0.0)))
                doa_ref[2 * p + hh] = jnp.where(_data_lanes(lane, hh), do, aug).astype(BF16)
        end()

    mat = pl.BlockSpec((D, D), lambda i: (0, 0))
    res = pl.pallas_call(
        body, name="bwd_xa_mix", grid=(steps,),
        in_specs=[_row_spec(TR, D), mat, _row_spec(TR, D), _row_spec(TR, D), _vec_spec(D), _row_spec(TR, D), _vec_spec(D), mat,
                  _row_spec(TR, D)] + [ANY] * n,
        out_specs=[_row_spec(TR, D), _row_spec(TR, D), _vec_spec(D), _vec_spec(D), _row_spec(TR, D_POOL),
                   pl.BlockSpec((HEADS, TR, LANES), lambda i: (0, i, 0))] + [ANY] * n,
        out_shape=[jax.ShapeDtypeStruct((S, D), F32), jax.ShapeDtypeStruct((S, D), BF16), jax.ShapeDtypeStruct((1, D), F32),
                   jax.ShapeDtypeStruct((1, D), F32), jax.ShapeDtypeStruct((S, D_POOL), F32),
                   jax.ShapeDtypeStruct((HEADS, S, LANES), BF16)] + ex.out_shapes,
        scratch_shapes=ex.scratch(), compiler_params=_params(("arbitrary",)),
    )(dqx, w_xq, dres, x2, g_pre, y1, g_post, w_mix_out, ycat, *ex.ins)
    return res[:6], res[6:]


def _fox_bwd(qab, doa, ka, va, ex):
    BQ = BK = FOX_BWD_BLOCK
    nk = S // BK
    n_pairs = HEADS // 2

    def body(*refs):
        p_id, j = pl.program_id(0), pl.program_id(1)
        (qab_ref, doa_ref, ka_ref, va_ref), (dqa_ref, dka_ref, dva_ref), (dv_ref,), begin, end = _hosted(
            ex, refs, 4, 3, (p_id == 0) & (j == 0), (p_id == n_pairs - 1) & (j == 0), (p_id == n_pairs - 1) & (j == nk - 1))
        begin()

        @pl.when(j == 0)
        def _():
            dqa_ref[...] = jnp.zeros_like(dqa_ref)

        causal = _row_iota((BQ, BK)) >= _lane_iota((BQ, BK))
        dka_ref[...] = jnp.zeros_like(dka_ref)
        dv_ref[...] = jnp.zeros_like(dv_ref)

        def step(i, masked):
            rows = pl.ds(pl.multiple_of(i * BQ, BQ), BQ)
            for hh in range(2):
                kb = ka_ref[hh]
                q = qab_ref[hh, rows, :]
                do = doa_ref[hh, rows, :]
                s = _dot(q, kb, NT)
                if masked:
                    s = jnp.where(causal, s, NEG)
                p = jnp.exp(s)
                ds = p * _dot(do, va_ref[hh], NT)
                pb = p.astype(BF16)
                dsb = ds.astype(BF16)
                dv_ref[hh] += _dot(pb, do, TN)
                dka_ref[hh] += _dot(dsb, q, TN)
                dqa_ref[hh, rows, :] += _dot(dsb, kb)

        def full_step(i, carry):
            step(i, False)
            return carry

        step(j, True)
        lax.fori_loop(j + 1, nk, full_step, 0)
        dva_ref[...] = dv_ref[...].astype(BF16)
        end()

    pair_all = pl.BlockSpec((2, S, LANES), lambda p, j: (p, 0, 0))
    pair_rows = pl.BlockSpec((2, BK, LANES), lambda p, j: (p, j, 0))
    shape = jax.ShapeDtypeStruct((HEADS, S, LANES), F32)
    n = len(ex.ins)
    res = pl.pallas_call(
        body, name="fox_bwd", grid=(n_pairs, nk), in_specs=[pair_all, pair_all, pair_rows, pair_rows] + [ANY] * n,
        out_specs=[pair_all, pair_rows, pair_rows] + [ANY] * n,
        out_shape=[shape, shape, jax.ShapeDtypeStruct((HEADS, S, LANES), BF16)] + ex.out_shapes,
        scratch_shapes=[pltpu.VMEM((2, BK, LANES), F32)] + ex.scratch(), compiler_params=_params(("arbitrary", "arbitrary")),
    )(qab, doa, ka, va, *ex.ins)
    return res[0], res[1], res[2], res[3:]


def _fox_bwd_post(dqa, dka, dva, du, proj, bf_pad):
    tr = PREP_TR
    nt = S // tr

    pick = np.zeros((HEADS * LANES, LANES), np.float32)
    for h in range(HEADS):
        pick[LANES * h + _at(BOTH_ONE, h), h] = 1.0

    def body(dqa_ref, dka_ref, dva_ref, du_ref, z_ref, bf_ref, pick_ref, dp_ref, dbf_ref, carry_ref):
        i = pl.program_id(0)

        @pl.when(i == 0)
        def _():
            carry_ref[...] = jnp.zeros_like(carry_ref)
            dbf_ref[...] = jnp.zeros_like(dbf_ref)

        lane = _lane_iota((tr, LANES))
        diff = jnp.concatenate([dqa_ref[h] - dka_ref[h] for h in range(HEADS)], axis=1)
        hi = diff.astype(BF16)
        dcum = _dot(hi, pick_ref[...]) + _dot((diff - hi.astype(F32)).astype(BF16), pick_ref[...])
        tri =jnp.where(_lane_iota((tr, tr)) >= _row_iota((tr, tr)), 1.0, 0.0).astype(BF16)
        dlog_f = _cumsum_rows(dcum, tri, carry_ref[0:1, :])
        carry_ref[0:1, :] = dlog_f[0:1, :]
        z = z_ref[...] + bf_ref[...]
        df = jnp.where(lane < HEADS, dlog_f / (1.0 + jnp.exp(z)), 0.0)
        dbf_ref[...] += jnp.sum(df, axis=0, keepdims=True)

        dp_ref[:, 0:D_POOL] = du_ref[...].astype(BF16)
        low = lane < DH
        for ref, off, scale in ((dqa_ref, Q_OFF, DH ** -0.5), (dka_ref, K_OFF, 1.0), (dva_ref, V_OFF, 1.0)):
            for p in range(HEADS // 2):
                blk = jnp.where(low, ref[2 * p], ref[2 * p + 1])
                dp_ref[:, off + LANES * p:off + LANES * (p + 1)] = (blk * scale).astype(BF16)
        dp_ref[:, F_OFF:F_OFF + LANES] = df.astype(BF16)

    head_spec = pl.BlockSpec((HEADS, tr, LANES), lambda i: (0, nt - 1 - i, 0))
    return pl.pallas_call(
        body, name="fox_bwd_post", grid=(nt,),
        in_specs=[head_spec, head_spec, head_spec, pl.BlockSpec((tr, D_POOL), lambda i: (nt - 1 - i, 0)),
                  pl.BlockSpec((tr, LANES), lambda i: (nt - 1 - i, 0)), _vec_spec(LANES),
                  pl.BlockSpec(pick.shape, lambda i: (0, 0))],
        out_specs=[pl.BlockSpec((tr, D_IN_PAD), lambda i: (nt - 1 - i, 0)), _vec_spec(LANES)],
        out_shape=[jax.ShapeDtypeStruct((S, D_IN_PAD), BF16), jax.ShapeDtypeStruct((1, LANES), F32)],
        scratch_shapes=[pltpu.VMEM((SUBLANES, LANES), F32)],
        compiler_params=_params(("arbitrary",)),
    )(dqa, dka, dva, du, proj, bf_pad, jnp.asarray(pick, BF16))


POOL_HALO = 16


def _by_group(lane, a2, a4, a8, a16):
    return jnp.where(lane < 64, a2, jnp.where(lane < 128, a4, jnp.where(lane < 192, a8, a16)))


def _window_count(lane, t):
    return jnp.minimum(t + 1, _by_group(lane, 2, 4, 8, 16)).astype(F32)


def _pool_diff(u, halo, first, tile):
    n = TR + POOL_HALO
    ext = jnp.concatenate([jnp.where(first, 0.0, halo), u], axis=0)
    s2 = ext + pltpu.roll(ext, 1, 0)
    s4 = s2 + pltpu.roll(s2, 2, 0)
    s8 = s4 + pltpu.roll(s4, 4, 0)
    s16 = s8 + pltpu.roll(s8, 8, 0)
    lane = _lane_iota((n, D_POOL))
    win = _by_group(lane, s2, s4, s8, s16)[POOL_HALO:]
    lane = _lane_iota((TR, D_POOL))
    t = tile * TR + _row_iota((TR, D_POOL))
    return win / _window_count(lane, t) - u


def _prev_halo(rows, width, col):
    per = TR // rows
    return pl.BlockSpec((rows, width), lambda i: (jnp.maximum(i * per - 1, 0), col))


def _next_halo(rows, width, col):
    per = TR // rows
    return pl.BlockSpec((rows, width), lambda i: (jnp.minimum((i + 1) * per, S // rows - 1), col))


def _pool_fwd(proj, w_bd, ps, ycat):
    def body(u_ref, halo_ref, w_ref, ps_ref, ycat_ref, y_ref):
        i = pl.program_id(0)
        diff = _pool_diff(u_ref[...], halo_ref[...], i == 0, i)
        y_ref[...] = (_dot(diff.astype(BF16), w_ref[...]) * ps_ref[...]).astype(BF16)

    return pl.pallas_call(
        body, name="pool_fwd", grid=(S // TR,),
        in_specs=[_row_spec(TR, D_POOL), _prev_halo(POOL_HALO, D_POOL, 0),
                  pl.BlockSpec((D_POOL, D_POOL), lambda i: (0, 0)), _vec_spec(D_POOL), ANY],
        out_specs=_row_spec(TR, D_POOL), out_shape=jax.ShapeDtypeStruct((S, D), BF16), input_output_aliases={4: 0},
        compiler_params=_params(("parallel",)),
    )(proj, proj, w_bd, ps, ycat)


def _pool_bwd(proj, dycat, w_bd, w_bd_t, ps):
    nt = S // TR
    n = TR + POOL_HALO

    def body(u_ref, halo_ref, dy_ref, dyn_ref, w_ref, wt_ref, ps_ref, du_ref, dw_ref, dps_ref):
        i = pl.program_id(0)

        @pl.when(i == 0)
        def _():
            dw_ref[...] = jnp.zeros_like(dw_ref)
            dps_ref[...] = jnp.zeros_like(dps_ref)

        diff = _pool_diff(u_ref[...], halo_ref[...], i == 0, i).astype(BF16)
        dy = dy_ref[...]
        dps_ref[...] += jnp.sum(dy * _dot(diff, w_ref[...]), axis=0, keepdims=True)
        dy_ext = jnp.concatenate([dy, jnp.where(i == nt - 1, 0.0, dyn_ref[...])], axis=0)
        dmixed = (dy_ext * ps_ref[...]).astype(BF16)
        ddiff = _dot(dmixed, wt_ref[...])
        dw_ref[...] += _dot(diff, dmixed[:TR], TN)
        lane = _lane_iota((n, D_POOL))
        t = i * TR + _row_iota((n, D_POOL))
        e = ddiff / _window_count(lane, t)
        f2 = e + pltpu.roll(e, n - 1, 0)
        f4 = f2 + pltpu.roll(f2, n - 2, 0)
        f8 = f4 + pltpu.roll(f4, n - 4, 0)
        f16 = f8 + pltpu.roll(f8, n - 8, 0)
        du_ref[...] = _by_group(lane, f2, f4, f8, f16)[:TR] - ddiff[:TR]

    mat = pl.BlockSpec((D_POOL, D_POOL), lambda i: (0, 0))
    return pl.pallas_call(
        body, name="pool_bwd", grid=(nt,),
        in_specs=[_row_spec(TR, D_POOL), _prev_halo(POOL_HALO, D_POOL, 0), _row_spec(TR, D_POOL),
                  _next_halo(POOL_HALO, D_POOL, 0), mat, mat, _vec_spec(D_POOL)],
        out_specs=[_row_spec(TR, D_POOL), mat, _vec_spec(D_POOL)],
        out_shape=[jax.ShapeDtypeStruct((S, D_POOL), F32), jax.ShapeDtypeStruct((D_POOL, D_POOL), F32),
                   jax.ShapeDtypeStruct((1, D_POOL), F32)],
        compiler_params=_params(("arbitrary",)),
    )(proj, proj, dycat, dycat, w_bd, w_bd_t, ps)


def _xa_probs(q, k):
    s = _dot(q, k, NT) * (XA_DH ** -0.5)
    e = jnp.exp(s - jnp.max(s, axis=-1, keepdims=True))
    return e * (1.0 / jnp.sum(e, axis=-1, keepdims=True))


def _xattn_fwd(qx, kv):
    def body(q_ref, kv_ref, o_ref):
        for h in range(XA_HEADS):
            cols = slice(XA_DH * h, XA_DH * (h + 1))
            vcols = slice(D + XA_DH * h, D + XA_DH * (h + 1))
            p = _xa_probs(q_ref[:, cols], kv_ref[:, cols])
            o_ref[:, cols] = _dot(p.astype(BF16), kv_ref[:, vcols]).astype(BF16)

    return pl.pallas_call(
        body, name="xattn_fwd", grid=(S // TR,),
        in_specs=[_row_spec(TR, D), pl.BlockSpec((MEM, 2 * D), lambda i: (0, 0))],
        out_specs=_row_spec(TR, D), out_shape=jax.ShapeDtypeStruct((S, D), BF16),
        compiler_params=_params(("parallel",)),
    )(qx, kv)


def _xattn_bwd(qx, kv, dxo):
    def body(q_ref, kv_ref, do_ref, dq_ref, dkv_ref):
        i = pl.program_id(0)

        @pl.when(i == 0)
        def _():
            dkv_ref[...] = jnp.zeros_like(dkv_ref)

        for h in range(XA_HEADS):
            cols = slice(XA_DH * h, XA_DH * (h + 1))
            vcols = slice(D + XA_DH * h, D + XA_DH * (h + 1))
            q = q_ref[:, cols]
            k = kv_ref[:, cols]
            do = do_ref[:, cols]
            p = _xa_probs(q, k)
            dkv_ref[:, vcols] += _dot(p.astype(BF16), do, TN)
            dp = _dot(do, kv_ref[:, vcols], NT)
            ds = (p * (dp - jnp.sum(p * dp, axis=-1, keepdims=True)) * (XA_DH ** -0.5)).astype(BF16)
            dq_ref[:, cols] = _dot(ds, k).astype(BF16)
            dkv_ref[:, cols] += _dot(ds, q, TN)

    kv_spec = pl.BlockSpec((MEM, 2 * D), lambda i: (0, 0))
    return pl.pallas_call(
        body, name="xattn_bwd", grid=(S // TR,), in_specs=[_row_spec(TR, D), kv_spec, _row_spec(TR, D)],
        out_specs=[_row_spec(TR, D), kv_spec],
        out_shape=[jax.ShapeDtypeStruct((S, D), BF16), jax.ShapeDtypeStruct((MEM, 2 * D), F32)],
        compiler_params=_params(("arbitrary",)),
    )(qx, kv, dxo)


CONV_HALO = SUBLANES
TC = 512
TC_FWD = 1024
GELU_K = 0.7978845608028654
GELU_C = 0.044715


def _conv3(ext, w, rows):
    h0 = ext[CONV_HALO:CONV_HALO + rows]
    h1 = pltpu.roll(ext, 1, 0)[CONV_HALO:CONV_HALO + rows]
    h2 = pltpu.roll(ext, 2, 0)[CONV_HALO:CONV_HALO + rows]
    return w[2:3] * h0 + w[1:2] * h1 + w[0:1] * h2 + w[3:4], (h2, h1, h0)


def _conv_specs(tc):
    main = pl.BlockSpec((2, TR, tc), lambda j, i: (0, i, j))
    per = TR // CONV_HALO
    prev = pl.BlockSpec((2, CONV_HALO, tc), lambda j, i: (0, jnp.maximum(i * per - 1, 0), j))
    nxt = pl.BlockSpec((2, CONV_HALO, tc), lambda j, i: (0, jnp.minimum((i + 1) * per, S // CONV_HALO - 1), j))
    par = pl.BlockSpec((2, SUBLANES, tc), lambda j, i: (0, 0, j))
    return main, prev, nxt, par


def _convgate_fwd(hid, cwb):
    tc = TC_FWD

    def body(h_ref, hp_ref, w_ref, act_ref):
        i = pl.program_id(1)
        c = []
        for g in range(2):
            ext = jnp.concatenate([jnp.where(i == 0, 0.0, hp_ref[g]), h_ref[g]], axis=0)
            c.append(_conv3(ext, w_ref[g], TR)[0])
        gate, up = c
        act_ref[...] = (jax.nn.gelu(gate, approximate=True) * up).astype(BF16)

    main, prev, _, par = _conv_specs(tc)
    return pl.pallas_call(
        body, name="convgate_fwd", grid=(D_FF // tc, S // TR), in_specs=[main, prev, par],
        out_specs=pl.BlockSpec((TR, tc), lambda j, i: (i, j)), out_shape=jax.ShapeDtypeStruct((S, D_FF), BF16),
        compiler_params=_params(("parallel", "parallel")),
    )(hid, hid, cwb)


def _convgate_bwd(hid, dact, cwb):
    nr = S // TR
    n = TR + CONV_HALO

    def body(h_ref, hp_ref, hn_ref, da_ref, dan_ref, w_ref, dh_ref, dw_ref):
        i = pl.program_id(1)

        @pl.when(i == 0)
        def _():
            dw_ref[...] = jnp.zeros_like(dw_ref)

        da = jnp.concatenate([da_ref[...], jnp.where(i == nr - 1, 0.0, dan_ref[...])], axis=0)
        c, taps = [], []
        for g in range(2):
            ext = jnp.concatenate([jnp.where(i == 0, 0.0, hp_ref[g]), h_ref[g], hn_ref[g]], axis=0)
            cg, tg = _conv3(ext, w_ref[g], n)
            c.append(cg)
            taps.append(tg)
        gate, up = c
        th = jnp.tanh(GELU_K * (gate + GELU_C * gate * gate * gate))
        gelu = 0.5 * gate * (1.0 + th)
        dgelu = 0.5 * (1.0 + th) + 0.5 * gate * (1.0 - th * th) * GELU_K * (1.0 + 3.0 * GELU_C * gate * gate)
        for g, dc in enumerate((da * up * dgelu, da * gelu)):
            w = w_ref[g]
            dh = w[2:3] * dc[:TR] + w[1:2] * pltpu.roll(dc, n - 1, 0)[:TR] + w[0:1] * pltpu.roll(dc, n - 2, 0)[:TR]
            dh_ref[g] = dh.astype(BF16)
            dcm = dc[:TR]
            for r in range(3):
                dw_ref[g, r:r + 1, :] += jnp.sum(dcm * taps[g][r][:TR], axis=0, keepdims=True)
            dw_ref[g, 3:4, :] += jnp.sum(dcm, axis=0, keepdims=True)

    main, prev, nxt, par = _conv_specs(TC)
    per = TR // CONV_HALO
    return pl.pallas_call(
        body, name="convgate_bwd", grid=(D_FF // TC, nr),
        in_specs=[main, prev, nxt, pl.BlockSpec((TR, TC), lambda j, i: (i, j)),
                  pl.BlockSpec((CONV_HALO, TC), lambda j, i: (jnp.minimum((i + 1) * per, S // CONV_HALO - 1), j)), par],
        out_specs=[main, par],
        out_shape=[jax.ShapeDtypeStruct((2, S, D_FF), BF16), jax.ShapeDtypeStruct((2, SUBLANES, D_FF), F32)],
        compiler_params=_params(("parallel", "arbitrary")),
    )(hid, hid, hid, dact, dact, cwb)


def _adam_update(w, g, m, v):
    m = ADAM_B1 * m + (1.0 - ADAM_B1) * g
    v = ADAM_B2 * v + (1.0 - ADAM_B2) * (g * g)
    m_hat = m / (1.0 - ADAM_B1 ** ADAM_STEP)
    v_hat = v / (1.0 - ADAM_B2 ** ADAM_STEP)
    return -ADAM_LR * (m_hat / (jnp.sqrt(v_hat) + ADAM_EPS) + ADAM_WD * w), m, v


def _row_tile(rows, cols, itemsize=4, target=TILE_BYTES):
    tr = SUBLANES
    while rows % (2 * tr) == 0 and 2 * tr * cols * itemsize <= target:
        tr *= 2
    assert rows % tr == 0, (rows, tr)
    return rows if rows % (2 * tr) and 16 * tr * cols * itemsize < target else tr


def _adamw(name, w, g, m, v):
    rows, cols = w.shape
    tr = rows if rows * cols * 4 <= TILE_BYTES // 2 else _row_tile(rows, cols, target=TILE_BYTES // 2)

    def body(w_ref, g_ref, m_ref, v_ref, d_ref, nm_ref, nv_ref):
        d_ref[...], nm_ref[...], nv_ref[...] = _adam_update(w_ref[...], g_ref[...], m_ref[...], v_ref[...])

    spec = _row_spec(tr, cols)
    shape = jax.ShapeDtypeStruct((rows, cols), F32)
    return pl.pallas_call(
        body, name=name, grid=(rows // tr,), in_specs=[spec] * 4, out_specs=[spec] * 3, out_shape=[shape] * 3,
        compiler_params=_params(("parallel",)),
    )(w, g, m, v)


def _adamw_halves(name, core, w, g_mine, g_sibling, m, v):
    rows, cols = w.shape
    half = rows // 2
    tr = _row_tile(half, cols, target=TILE_BYTES // 2)
    per = half // tr

    def body(core_ref, w_ref, gm_ref, gs_ref, m_ref, v_ref, g_ref, d_ref, nm_ref, nv_ref):
        g = jnp.where(pl.program_id(0) // per == core_ref[0], gm_ref[...], gs_ref[...])
        g_ref[...] = g
        d_ref[...], nm_ref[...], nv_ref[...] = _adam_update(w_ref[...], g, m_ref[...], v_ref[...])

    spec = pl.BlockSpec((tr, cols), lambda i, core_ref: (i, 0))
    half_spec = pl.BlockSpec((tr, cols), lambda i, core_ref: (i % per, 0))
    shape = jax.ShapeDtypeStruct((rows, cols), F32)
    return pl.pallas_call(
        body, name=name, out_shape=[shape] * 4,
        grid_spec=pltpu.PrefetchScalarGridSpec(
            num_scalar_prefetch=1, grid=(rows // tr,), in_specs=[spec, half_spec, half_spec, spec, spec], out_specs=[spec] * 4),
        compiler_params=_params(("parallel",)),
    )(core, w, g_mine, g_sibling, m, v)


def _adamw_halves_columns(name, core, w, g_mine, g_sibling, m, v):
    cols, _, rows = w.shape
    tl = 2 * LANES
    per = rows // 2 // tl

    def body(core_ref, w_ref, gm_ref, gs_ref, m_ref, v_ref, g_ref, d_ref, nm_ref, nv_ref):
        g = jnp.where(pl.program_id(0) // per == core_ref[0], gm_ref[...], gs_ref[...])
        g_ref[...] = g
        d_ref[...], nm_ref[...], nv_ref[...] = _adam_update(w_ref[...], g, m_ref[...], v_ref[...])

    spec = pl.BlockSpec((cols, 1, tl), lambda i, core_ref: (0, 0, i))
    half_spec = pl.BlockSpec((cols, 1, tl), lambda i, core_ref: (0, 0, i % per))
    shape = jax.ShapeDtypeStruct((cols, 1, rows), F32)
    return pl.pallas_call(
        body, name=name, out_shape=[shape] * 4,
        grid_spec=pltpu.PrefetchScalarGridSpec(
            num_scalar_prefetch=1, grid=(rows // tl,), in_specs=[spec, half_spec, half_spec, spec, spec], out_specs=[spec] * 4),
        compiler_params=_params(("parallel",)),
    )(core, w, g_mine, g_sibling, m, v)


def _chip_sum(name, core, g, other):
    _, _, half, cols = g.shape
    tr = _row_tile(half, cols)

    def body(core_ref, g_ref, o_ref, p_ref):
        p_ref[...] = (g_ref[...] + o_ref[...]).astype(BF16)

    spec = pl.BlockSpec((None, tr, cols), lambda j, i, core_ref: (j, i, 0))
    return pl.pallas_call(
        body, name=name, out_shape=jax.ShapeDtypeStruct((N_CHIPS, half, cols), BF16),
        grid_spec=pltpu.PrefetchScalarGridSpec(
            num_scalar_prefetch=1, grid=(N_CHIPS, half // tr),
            in_specs=[pl.BlockSpec((None, None, tr, cols), lambda j, i, core_ref: (j, core_ref[0], i, 0)), spec],
            out_specs=spec),
        compiler_params=_params(("parallel", "parallel")),
    )(core, g, other)


def _mesh_sum(name, chip, received, own):
    _, half, cols = received.shape
    tr = _row_tile(half, cols, itemsize=2 * N_CHIPS)

    def body(chip_ref, r_ref, own_ref, o_ref):
        acc = None
        for j in range(N_CHIPS):
            term = jnp.where(chip_ref[0] == j, own_ref[...], r_ref[j]).astype(F32)
            acc = term if acc is None else acc + term
        o_ref[...] = acc

    return pl.pallas_call(
        body, name=name, out_shape=jax.ShapeDtypeStruct((half, cols), F32),
        grid_spec=pltpu.PrefetchScalarGridSpec(
            num_scalar_prefetch=1, grid=(half // tr,),
            in_specs=[pl.BlockSpec((N_CHIPS, tr, cols), lambda i, chip_ref: (0, i, 0)),
                      pl.BlockSpec((None, tr, cols), lambda i, chip_ref: (chip_ref[0], i, 0))],
            out_specs=pl.BlockSpec((tr, cols), lambda i, chip_ref: (i, 0))),
        compiler_params=_params(("parallel",)),
    )(chip, received, own)


CHIP_FLIPS = ((1, 0), (0, 1), (1, 1))


def _place():
    x, y, c = lax.axis_index("x"), lax.axis_index("y"), lax.axis_index("c")
    return x, y, c, 2 * x + y


def _remote(src, dst, sems_s, sems_r, k, dev):
    return pltpu.make_async_remote_copy(src_ref=src, dst_ref=dst, send_sem=sems_s.at[k], recv_sem=sems_r.at[k],
                                        device_id=dev, device_id_type=MESH)


class _Exchange:
    def __init__(self, ins, out_shapes, n_sems, start, forward, finish):
        self.ins, self.out_shapes, self.n_sems = list(ins), list(out_shapes), n_sems
        self.start, self.forward, self.finish = start, forward, finish

    def scratch(self):
        return [pltpu.SemaphoreType.DMA((self.n_sems,)), pltpu.SemaphoreType.DMA((self.n_sems,))]

    def run(self, name):
        n = len(self.ins)

        def body(*refs):
            args = (refs[:n], refs[n:2 * n]) + tuple(refs[2 * n:])
            self.start(*args)
            self.forward(*args)
            self.finish(*args)

        return pl.pallas_call(
            body, name=name, in_specs=[ANY] * n, out_specs=[ANY] * n, out_shape=self.out_shapes, scratch_shapes=self.scratch(),
        )(*self.ins)


def _all_gather_weights(halved, whole):
    nh, nw = len(halved), len(whole)
    n_arr = nh + nw

    def copies(ins, outs, sems_s, sems_r):
        x, y, c, me = _place()
        sibling = (x, y, 1 - c)
        own = [_remote(ins[k], outs[k].at[me], sems_s, sems_r, k, sibling) for k in range(n_arr)]
        first, passed = [], []
        for k in range(n_arr):
            for f, (fx, fy) in enumerate(CHIP_FLIPS):
                src, dst = (ins[k].at[c], outs[k].at[me, c]) if k < nh else (ins[k], outs[k].at[me])
                first.append(_remote(src, dst, sems_s, sems_r, n_arr + 3 * k + f, (x ^ fx, y ^ fy, c)))
        for k in range(nh):
            for f, (fx, fy) in enumerate(CHIP_FLIPS):
                landed = outs[k].at[2 * (x ^ fx) + (y ^ fy), c]
                passed.append(_remote(landed, landed, sems_s, sems_r, 4 * n_arr + 3 * k + f, sibling))
        return own, first, passed

    def start(*refs):
        own, first, _ = copies(*refs)
        for cp in own + first:
            cp.start()

    def forward(*refs):
        _, first, passed = copies(*refs)
        for arrived, cp in zip(first, passed):
            arrived.wait_recv()
            cp.start()

    def finish(*refs):
        own, first, passed = copies(*refs)
        for cp in first[3 * nh:] + passed + own:
            cp.wait_recv()
        for cp in first + passed + own:
            cp.wait_send()

    shapes = [jax.ShapeDtypeStruct((N_CHIPS,) + a.shape, a.dtype) for a in list(halved) + list(whole)]
    return _Exchange(list(halved) + list(whole), shapes, 7 * nh + 4 * nw, start, forward, finish)


def _swap_halves(gs):
    n = len(gs)

    def copies(ins, outs, sems_s, sems_r):
        x, y, c, _ = _place()
        return [_remote(ins[k].at[:, 1 - c], outs[k], sems_s, sems_r, k, (x, y, 1 - c)) for k in range(n)]

    def start(*refs):
        for cp in copies(*refs):
            cp.start()

    def finish(*refs):
        for cp in copies(*refs):
            cp.wait()

    shapes = [jax.ShapeDtypeStruct((g.shape[0],) + g.shape[2:], g.dtype) for g in gs]
    return _Exchange(gs, shapes, n, start, _no_copies, finish)


def _scatter_chips(ps):
    n = len(ps)

    def copies(ins, outs, sems_s, sems_r):
        x, y, c, me = _place()
        return [_remote(ins[k].at[2 * (x ^ fx) + (y ^ fy)], outs[k].at[me], sems_s, sems_r, 3 * k + f, (x ^ fx, y ^ fy, c))
                for k in range(n) for f, (fx, fy) in enumerate(CHIP_FLIPS)]

    def start(*refs):
        for cp in copies(*refs):
            cp.start()

    def forward(*refs):
        pass

    def finish(*refs):
        for cp in copies(*refs):
            cp.wait()

    shapes = [jax.ShapeDtypeStruct(p.shape, p.dtype) for p in ps]
    return _Exchange(ps, shapes, 3 * n, start, forward, finish)


def _swap_reduced(rs):
    n = len(rs)

    def copies(ins, outs, sems_s, sems_r):
        x, y, c, _ = _place()
        return [_remote(ins[k], outs[k], sems_s, sems_r, k, (x, y, 1 - c)) for k in range(n)]

    def start(*refs):
        for cp in copies(*refs):
            cp.start()

    def finish(*refs):
        for cp in copies(*refs):
            cp.wait()

    return _Exchange(rs, [jax.ShapeDtypeStruct(r.shape, r.dtype) for r in rs], n, start, _no_copies, finish)


N_DEV = 8


def _gather_small(buf):
    def copies(ins, outs, sems_s, sems_r):
        x, y, c, chip = _place()
        sibling = (x, y, 1 - c)
        own = _remote(ins[0], outs[0].at[2 * chip + c], sems_s, sems_r, 0, sibling)
        first = [_remote(ins[0], outs[0].at[2 * chip + c], sems_s, sems_r, 1 + f, (x ^ fx, y ^ fy, c))
                 for f, (fx, fy) in enumerate(CHIP_FLIPS)]
        passed = []
        for f, (fx, fy) in enumerate(CHIP_FLIPS):
            landed = outs[0].at[2 * (2 * (x ^ fx) + (y ^ fy)) + c]
            passed.append(_remote(landed, landed, sems_s, sems_r, 4 + f, sibling))
        return own, first, passed

    def start(*refs):
        own, first, _ = copies(*refs)
        for cp in [own] + first:
            cp.start()

    def forward(*refs):
        _, first, passed = copies(*refs)
        for arrived, cp in zip(first, passed):
            arrived.wait_recv()
            cp.start()

    def finish(*refs):
        own, first, passed = copies(*refs)
        for cp in passed + [own]:
            cp.wait_recv()
        for cp in first + passed + [own]:
            cp.wait_send()

    return _Exchange([buf], [jax.ShapeDtypeStruct((N_DEV,) + buf.shape, buf.dtype)], N_DEV - 1, start, forward, finish)


def _sum_devices(place, gathered, own):
    rows = own.shape[0]

    def body(place_ref, g_ref, own_ref, o_ref):
        acc = None
        for d in range(N_DEV):
            term = jnp.where(place_ref[0] == d, own_ref[...], g_ref[d])
            acc = term if acc is None else acc + term
        o_ref[...] = acc

    return pl.pallas_call(
        body, name="sum_devices", out_shape=jax.ShapeDtypeStruct((rows, LANES), F32),
        grid_spec=pltpu.PrefetchScalarGridSpec(
            num_scalar_prefetch=1, grid=(1,),
            in_specs=[pl.BlockSpec((N_DEV, rows, LANES), lambda i, place_ref: (0, 0, 0)),
                      pl.BlockSpec((rows, LANES), lambda i, place_ref: (0, 0))],
            out_specs=pl.BlockSpec((rows, LANES), lambda i, place_ref: (0, 0))),
        compiler_params=_params(("arbitrary",)),
    )(place, gathered, own)


def _no_copies(*refs):
    pass


def _no_exchange():
    return _Exchange([], [], 1, _no_copies, _no_copies, _no_copies)


class _NoComm:
    def gather_first(self):
        return _no_exchange()

    def first_landed(self, p, landed):
        pass

    def gather_rest(self, p):
        return _no_exchange()

    def weights_landed(self, p, landed):
        pass

    def gather_last(self):
        return _no_exchange()

    def last_landed(self, p, landed):
        pass

    def swap_first(self, g):
        return _no_exchange()

    def first_swapped(self, landed):
        pass

    def swap_second(self, g):
        return _no_exchange()

    def second_swapped(self, landed):
        pass

    def scatter_early(self, g):
        return _no_exchange()

    def scatter_landed(self, landed):
        pass

    def swap_reduced_early(self):
        return _no_exchange()

    def reduced_landed(self, landed):
        pass

    def scatter_late(self, g):
        return _no_exchange()

    def late_landed(self, landed):
        pass


def _local_step(x, mem, target, p, comm):
    h1, landed = _norm_fwd("norm_mix_pre", x, p["norm_mix_pre"], comm.gather_first())
    comm.first_landed(p, landed)
    qa, ka, va, u, z = _in_proj(h1, p["w_in"], p["bf_pad"])
    ycat, qab, landed = _fox_fwd(qa, ka, va, comm.gather_rest(p))
    comm.weights_landed(p, landed)
    ycat = _pool_fwd(u, p["w_pool_bd"], p["pool_scale"], ycat)
    y1, x2, h2, qx = _proj_resid_norm("mix_out", ycat, p["w_mix_out"], x, p["norm_mix_post"], p["norm_xa_pre"], p["w_xq"])
    mem_n = _norm_fwd("norm_mem", mem, p["norm_mem"])
    kv = _mm(
        "xkv", mem_n, p["w_xkv"], pl.BlockSpec((MEM, D), lambda i, j, k: (0, 0)),
        pl.BlockSpec((None, D, 512), lambda i, j, k: (j, 0, 0)), jax.ShapeDtypeStruct((MEM, 2 * D), BF16),
        pl.BlockSpec((MEM, 512), lambda i, j, k: (0, j)), (1, N_CHIPS, 1), NN, (MEM, 512))
    xo = _xattn_fwd(qx, kv)
    y2, x3, h3 = _proj_resid_norm("xo", xo, p["w_xo"], x2, p["norm_xa_post"], p["norm_ffn_pre"])
    hid, landed = _mm(
        "up_proj", h3, p["w_up"], pl.BlockSpec((2048, D), lambda i, j, k: (i, 0)),
        pl.BlockSpec((None, D, 1024), lambda i, j, k: (j // 2, 0, j % 2)), jax.ShapeDtypeStruct((2, S, D_FF), F32),
        pl.BlockSpec((None, 2048, 1024), lambda i, j, k: (j // 4, i, j % 4)), (S // 2048, 8, 1), NN, (2048, 1024),
        comm.gather_last())
    comm.last_landed(p, landed)
    act = _convgate_fwd(hid, p["cwb"])

    g = {}
    dres, dy3, g["norm_ffn_post"], loss_cols = _down_loss_bwd(act, p["w_down"], x3, p["norm_ffn_post"], target)
    dact = _mm_nt("d_act", dy3, p["w_down"], F32, 2048, 1024)
    g["w_down"] = _mm_tn("dw_down", act, dy3, 1024, 512)
    dhid, dcwb = _convgate_bwd(hid, dact, p["cwb"])
    g["w_up"] = _mm(
        "dw_up", h3, dhid, pl.BlockSpec((S, D), lambda i, j, k: (0, 0)),
        pl.BlockSpec((None, S, 512), lambda i, j, k: (j // 8, 0, j % 8)), jax.ShapeDtypeStruct((N_CHIPS, D, 2048), F32),
        pl.BlockSpec((None, D, 512), lambda i, j, k: (j // 4, 0, j % 4)), (1, 16, 1), TN, (D, 512))
    dh3, landed = _d_h3(dhid, p["w_up"], comm.swap_first(g))
    comm.first_swapped(landed)
    dres, dy2, dxo, g["norm_ffn_pre"], g["norm_xa_post"] = _mid_bwd(
        "bwd_ffn_xa", dres, x3, p["norm_ffn_pre"], dh3, y2, p["norm_xa_post"], p["w_xo"])
    g["w_xo"] = _mm_tn("dw_xo", xo, dy2, 1024, 512)
    dqx, dkv = _xattn_bwd(qx, kv, dxo)
    dkv = dkv.astype(BF16)
    g["w_xq"] = _mm_tn("dw_xq", h2, dqx, 1024, 512)
    dmem_n = _mm(
        "d_mem", dkv, p["w_xkv"], pl.BlockSpec((MEM, 512), lambda i, j, k: (0, k)),
        pl.BlockSpec((None, D, 512), lambda i, j, k: (k, 0, 0)), jax.ShapeDtypeStruct((MEM, D), F32),
        pl.BlockSpec((MEM, D), lambda i, j, k: (0, 0)), (1, 1, N_CHIPS), NT, (MEM, D))
    g["w_xkv"] = _mm(
        "dw_xkv", mem_n, dkv, pl.BlockSpec((MEM, D), lambda i, j, k: (0, 0)),
        pl.BlockSpec((MEM, 512), lambda i, j, k: (0, j)), jax.ShapeDtypeStruct((N_CHIPS, D, 512), F32),
        pl.BlockSpec((None, D, 512), lambda i, j, k: (j, 0, 0)), (1, N_CHIPS, 1), TN, (D, 512))
    g["norm_mem"] = _gain_bwd("dg_mem", mem, p["norm_mem"], dmem_n)
    (dres, dy1, g["norm_xa_pre"], g["norm_mix_post"], dy_pool, doa), landed = _bwd_xa_mix(
        dqx, p["w_xq"], dres, x2, p["norm_xa_pre"], y1, p["norm_mix_post"], p["w_mix_out"], ycat, comm.swap_second(g))
    comm.second_swapped(landed)
    g["w_mix_out"] = _mm_tn("dw_mix_out", ycat, dy1, 1024, 512)
    dqa, dka, dva, landed = _fox_bwd(qab, doa, ka, va, comm.scatter_early(g))
    comm.scatter_landed(landed)
    du, g["w_pool_full"], g["pool_scale"] = _pool_bwd(u, dy_pool, p["w_pool_bd"], p["w_pool_bd_t"], p["pool_scale"])
    dproj, g["bf_pad"] = _fox_bwd_post(dqa, dka, dva, du, z, p["bf_pad"])
    g["w_in"], landed = _dw_in(dproj, h1, comm.swap_reduced_early())
    comm.reduced_landed(landed)
    grad_x, g["norm_mix_pre"], landed = _d_h1_first_bwd(dproj, p["w_in"], dres, x, p["norm_mix_pre"], comm.scatter_late(g))
    comm.late_landed(landed)
    g["cwb"] = dcwb
    return grad_x, g, loss_cols


BIG = ("w_in", "w_mix_out", "w_xq", "w_xkv", "w_xo", "w_up", "w_down")
ROW_SHARDED = ("w_mix_out", "w_xq", "w_xo", "w_down")
SMALL = ("norm_mix_pre", "norm_mix_post", "b_forget", "w_pool", "pool_scale", "norm_mem", "norm_xa_pre", "norm_xa_post",
         "norm_ffn_pre", "norm_ffn_post", "conv_b")
ORDER = ("norm_mix_pre", "norm_mix_post", "w_in", "b_forget", "w_pool", "pool_scale", "w_mix_out", "norm_mem", "norm_xa_pre",
         "norm_xa_post", "w_xq", "w_xkv", "w_xo", "norm_ffn_pre", "norm_ffn_post", "w_up", "conv_w", "conv_b", "w_down")
SLOT = SUBLANES * LANES


def _pack(parts):
    rows, offs, off = [], [], 0
    for a in parts:
        flat = a.reshape(-1).astype(F32)
        n = -(-flat.shape[0] // SLOT) * SLOT
        rows.append(jnp.pad(flat, (0, n - flat.shape[0])).reshape(n // LANES, LANES))
        offs.append(off)
        off += n // LANES
    return jnp.concatenate(rows, axis=0), offs


def _unpack(buf, off, like):
    n = like.size
    rows = -(-n // LANES)
    return buf[off:off + rows].reshape(-1)[:n].reshape(like.shape)


FIRST = ("w_in",)
REST = ("w_mix_out", "w_xq", "w_xkv", "w_xo", "w_up")
LAST = ("w_down",)


def _local_params(w):
    w_pool_bd = jnp.zeros((D_POOL, D_POOL), F32)
    for gi in range(4):
        w_pool_bd = w_pool_bd.at[64 * gi:64 * (gi + 1), 64 * gi:64 * (gi + 1)].set(w["w_pool"][0, gi])
    p = {n: w[n] for n in ("norm_mix_pre", "norm_mix_post", "norm_mem", "norm_xa_pre", "norm_xa_post", "norm_ffn_pre",
                           "norm_ffn_post")}
    p.update(
        bf_pad=jnp.pad(w["b_forget"], ((0, 0), (0, LANES - HEADS))),
        w_pool_bd=w_pool_bd.astype(BF16), w_pool_bd_t=w_pool_bd.T.astype(BF16), pool_scale=w["pool_scale"].reshape(1, D_POOL))
    return p


def _w_in_param(stacked):
    n, rows, cols = stacked.shape
    tr = PREP_TR

    def body(w_ref, o_ref):
        o_ref[...] = jnp.concatenate([w_ref[j] for j in range(n)] + [jnp.zeros((tr, D_IN_PAD - n * cols), BF16)], axis=1)

    return pl.pallas_call(
        body, name="w_in_whole", grid=(rows // tr,), in_specs=[pl.BlockSpec((n, tr, cols), lambda i: (0, i, 0))],
        out_specs=_row_spec(tr, D_IN_PAD), out_shape=jax.ShapeDtypeStruct((rows, D_IN_PAD), BF16),
        compiler_params=_params(("parallel",)),
    )(stacked)


def _rest_params(w, full, conv_w_full):
    cw2 = conv_w_full.reshape(3, 2, D_FF).transpose(1, 0, 2)
    cwb = jnp.concatenate([cw2, w["conv_b"].reshape(1, 2, D_FF).transpose(1, 0, 2), jnp.zeros((2, 4, D_FF), F32)], axis=1)
    return dict(w_mix_out=full["w_mix_out"].reshape(D, D), w_xq=full["w_xq"].reshape(D, D), w_xkv=full["w_xkv"],
                w_xo=full["w_xo"].reshape(D, D), w_up=full["w_up"], cwb=cwb)


def _whole_params(w, full, conv_w_full):
    p = _local_params(w)
    p.update(_rest_params(w, full, conv_w_full), w_in=_w_in_param(full["w_in"]), w_down=full["w_down"].reshape(D_FF, D))
    return p


def _halved(a):
    return a.reshape(a.shape[:-2] + (2, a.shape[-2] // 2, a.shape[-1]))


class _StepComm:
    def __init__(self, w, shard2d, conv_w, core_id, chip_id):
        self.w, self.shard2d, self.conv_w, self.core_id, self.chip_id = w, shard2d, conv_w, core_id, chip_id
        self.first, self.second = ("w_up", "w_down"), ("w_xq", "w_xkv", "w_xo")
        self.early = self.first + self.second
        self.late = ("w_in", "w_mix_out")

    def gather_first(self):
        return _all_gather_weights([_halved(self.shard2d[n].astype(BF16)) for n in FIRST], [])

    def first_landed(self, p, landed):
        p["w_in"] = _w_in_param(landed[0].reshape((N_CHIPS,) + self.shard2d["w_in"].shape))

    def gather_rest(self, p):
        return _all_gather_weights([_halved(self.shard2d[n].astype(BF16)) for n in REST], [self.conv_w.reshape(3, -1)])

    def weights_landed(self, p, landed):
        full = {n: a.reshape((N_CHIPS,) + self.shard2d[n].shape) for n, a in zip(REST, landed)}
        conv_w_full = jnp.transpose(landed[-1], (1, 0, 2)).reshape(3, 2 * D_FF)
        p.update(_rest_params(self.w, full, conv_w_full))

    def gather_last(self):
        return _all_gather_weights([_halved(self.shard2d[n].astype(BF16)) for n in LAST], [])

    def last_landed(self, p, landed):
        p["w_down"] = landed[0].reshape(D_FF, D)

    def _view(self, g, n):
        return _halved(g[n].reshape((N_CHIPS,) + self.shard2d[n].shape))

    def swap_first(self, g):
        return _swap_halves([self._view(g, n) for n in self.first])

    def first_swapped(self, landed):
        self.from_sibling = dict(zip(self.first, landed))

    def swap_second(self, g):
        return _swap_halves([self._view(g, n) for n in self.second])

    def second_swapped(self, landed):
        self.from_sibling.update(zip(self.second, landed))

    def scatter_early(self, g):
        self.partial = [_chip_sum("chip_sum_" + n, self.core_id, self._view(g, n), self.from_sibling[n]) for n in self.early]
        return _scatter_chips(self.partial)

    def scatter_landed(self, landed):
        self.received = list(landed)

    def swap_reduced_early(self):
        self.reduced = [_mesh_sum("mesh_sum_" + n, self.chip_id, r, own)
                        for n, r, own in zip(self.early, self.received, self.partial)]
        return _swap_reduced(self.reduced)

    def reduced_landed(self, landed):
        self.reduced_sibling = list(landed)

    def scatter_late(self, g):
        views = [g["w_in"], self._view(g, "w_mix_out")]
        from_sibling = _swap_halves(views).run("swap_halves_late")
        self.partial_late = [_chip_sum("chip_sum_" + n, self.core_id, view, other)
                             for n, view, other in zip(self.late, views, from_sibling)]
        return _scatter_chips(self.partial_late)

    def late_landed(self, landed):
        self.received_late = list(landed)


def kernel(x, mem, norm_mix_pre, norm_mix_post, w_in, b_forget, w_pool, pool_scale, w_mix_out, norm_mem, norm_xa_pre, norm_xa_post, w_xq, w_xkv, w_xo, norm_ffn_pre, norm_ffn_post, w_up, conv_w, conv_b, w_down, loss_target, m_norm_mix_pre, m_norm_mix_post, m_w_in, m_b_forget, m_w_pool, m_pool_scale, m_w_mix_out, m_norm_mem, m_norm_xa_pre, m_norm_xa_post, m_w_xq, m_w_xkv, m_w_xo, m_norm_ffn_pre, m_norm_ffn_post, m_w_up, m_conv_w, m_conv_b, m_w_down, v_norm_mix_pre, v_norm_mix_post, v_w_in, v_b_forget, v_w_pool, v_pool_scale, v_w_mix_out, v_norm_mem, v_norm_xa_pre, v_norm_xa_post, v_w_xq, v_w_xkv, v_w_xo, v_norm_ffn_pre, v_norm_ffn_post, v_w_up, v_conv_w, v_conv_b, v_w_down):
    w = dict(norm_mix_pre=norm_mix_pre, norm_mix_post=norm_mix_post, w_in=w_in, b_forget=b_forget, w_pool=w_pool,
             pool_scale=pool_scale, w_mix_out=w_mix_out, norm_mem=norm_mem, norm_xa_pre=norm_xa_pre, norm_xa_post=norm_xa_post,
             w_xq=w_xq, w_xkv=w_xkv, w_xo=w_xo, norm_ffn_pre=norm_ffn_pre, norm_ffn_post=norm_ffn_post, w_up=w_up,
             conv_w=conv_w, conv_b=conv_b, w_down=w_down)
    m = dict(norm_mix_pre=m_norm_mix_pre, norm_mix_post=m_norm_mix_post, w_in=m_w_in, b_forget=m_b_forget, w_pool=m_w_pool,
             pool_scale=m_pool_scale, w_mix_out=m_w_mix_out, norm_mem=m_norm_mem, norm_xa_pre=m_norm_xa_pre,
             norm_xa_post=m_norm_xa_post, w_xq=m_w_xq, w_xkv=m_w_xkv, w_xo=m_w_xo, norm_ffn_pre=m_norm_ffn_pre,
             norm_ffn_post=m_norm_ffn_post, w_up=m_w_up, conv_w=m_conv_w, conv_b=m_conv_b, w_down=m_w_down)
    v = dict(norm_mix_pre=v_norm_mix_pre, norm_mix_post=v_norm_mix_post, w_in=v_w_in, b_forget=v_b_forget, w_pool=v_w_pool,
             pool_scale=v_pool_scale, w_mix_out=v_w_mix_out, norm_mem=v_norm_mem, norm_xa_pre=v_norm_xa_pre,
             norm_xa_post=v_norm_xa_post, w_xq=v_w_xq, w_xkv=v_w_xkv, w_xo=v_w_xo, norm_ffn_pre=v_norm_ffn_pre,
             norm_ffn_post=v_norm_ffn_post, w_up=v_w_up, conv_w=v_conv_w, conv_b=v_conv_b, w_down=v_w_down)
    chip = 2 * lax.axis_index("x") + lax.axis_index("y")

    core_id = lax.axis_index("c").astype(jnp.int32).reshape(1)
    chip_id = chip.astype(jnp.int32).reshape(1)

    shard2d = {n: w[n][0] for n in BIG}
    p = _local_params(w)
    comm = _StepComm(w, shard2d, conv_w, core_id, chip_id)
    grad_x, g, loss_cols = _local_step(x[0], mem[0], loss_target[0], p, comm)

    reduced_late = [_mesh_sum("mesh_sum_" + n, chip_id, r, own)
                    for n, r, own in zip(comm.late, comm.received_late, comm.partial_late)]
    names = comm.late + comm.early
    reduced = reduced_late + comm.reduced
    reduced_sibling = list(_swap_reduced(reduced_late).run("swap_reduced_late")) + comm.reduced_sibling
    grads = {}

    gw_pool = jnp.stack([g["w_pool_full"][64 * gi:64 * (gi + 1), 64 * gi:64 * (gi + 1)] for gi in range(4)])
    dcwb = g["cwb"]
    g_conv_w = dcwb[:, 0:3, :].transpose(1, 0, 2).reshape(3, 2 * D_FF)
    g_conv_b = dcwb[:, 3, :].reshape(2 * D_FF)
    small_g = dict(norm_mix_pre=g["norm_mix_pre"], norm_mix_post=g["norm_mix_post"], b_forget=g["bf_pad"][:, :HEADS],
                   w_pool=gw_pool, pool_scale=g["pool_scale"], norm_mem=g["norm_mem"], norm_xa_pre=g["norm_xa_pre"],
                   norm_xa_post=g["norm_xa_post"], norm_ffn_pre=g["norm_ffn_pre"], norm_ffn_post=g["norm_ffn_post"],
                   conv_b=g_conv_b)
    local_buf, offs = _pack([small_g[n] for n in SMALL] + [g_conv_w, loss_cols])

    delta, new_m, new_v = {}, {}, {}
    for n, g_mine, g_sibling in zip(names, reduced, reduced_sibling):
        cols = shard2d[n].shape[1]
        if cols % LANES:
            outs = _adamw_halves_columns("adamw_" + n, core_id, jnp.transpose(w[n], (2, 0, 1)), g_mine[:cols, None, :],
                                         g_sibling[:cols, None, :], jnp.transpose(m[n], (2, 0, 1)), jnp.transpose(v[n], (2, 0, 1)))
            gn, d, nm, nv = (jnp.transpose(o, (1, 2, 0)) for o in outs)
        else:
            gn, d, nm, nv = (o[None] for o in _adamw_halves("adamw_" + n, core_id, shard2d[n], g_mine, g_sibling, m[n][0], v[n][0]))
        grads[n], delta[n], new_m[n], new_v[n] = gn, d, nm, nv
    place = (2 * chip + lax.axis_index("c")).astype(jnp.int32).reshape(1)
    buf = _sum_devices(place, _gather_small(local_buf).run("gather_small")[0], local_buf)
    for n, off in zip(SMALL, offs):
        grads[n] = _unpack(buf, off, w[n])
    g_conv_w = _unpack(buf, offs[len(SMALL)], g_conv_w)
    grads["conv_w"] = lax.dynamic_slice_in_dim(g_conv_w, chip * (2 * D_FF // N_CHIPS), 2 * D_FF // N_CHIPS, axis=1).reshape(conv_w.shape)
    loss = jnp.sum(_unpack(buf, offs[len(SMALL) + 1], loss_cols))
    small_names = SMALL + ("conv_w",)
    packed = [_pack([d[n] for n in small_names])[0] for d in (w, grads, m, v)]
    offs = _pack([w[n] for n in small_names])[1]
    d, nm, nv = _adamw("adamw_small", *packed)
    for n, off in zip(small_names, offs):
        delta[n], new_m[n], new_v[n] = _unpack(d, off, w[n]), _unpack(nm, off, w[n]), _unpack(nv, off, w[n])

    return (loss, grad_x[None], *[grads[n] for n in ORDER], *[delta[n] for n in ORDER], *[new_m[n] for n in ORDER],
            *[new_v[n] for n in ORDER])
```

```python
import functools

import jax
import jax.numpy as jnp
import numpy as np
from jax import lax
from jax.experimental import pallas as pl
from jax.experimental.pallas import tpu as pltpu

F32 = jnp.float32
BF16 = jnp.bfloat16
MESH = pl.DeviceIdType.MESH
ANY = pl.BlockSpec(memory_space=pl.ANY)
VMEM_SPEC = pl.BlockSpec(memory_space=pltpu.VMEM)

S = 4096
D = 1024
MEM = 256
D_POOL = 256
HEADS = 12
DH = 64
D_FOX = HEADS * DH
D_IN = D_POOL + 3 * D_FOX + HEADS
F_OFF = D_POOL + 3 * D_FOX
Q_OFF, K_OFF, V_OFF = D_POOL, D_POOL + D_FOX, D_POOL + 2 * D_FOX
XA_HEADS = 4
XA_DH = 256
D_FF = 4096
EPS = 1e-6
N_CHIPS = 4
ADAM_LR, ADAM_B1, ADAM_B2, ADAM_EPS, ADAM_WD, ADAM_STEP = 0.001, 0.9, 0.999, 1e-08, 0.01, 10

LANES = 128
SUBLANES = 8
D_IN_PAD = 21 * LANES
TR = 512
ROW_PIECES = 4
TILE_BYTES = 2 * 1024 * 1024
NEG = -1e30
VMEM_LIMIT = 52 * 1024 * 1024

NN = (((1,), (0,)), ((), ()))
NT = (((1,), (1,)), ((), ()))
TN = (((0,), (0,)), ((), ()))


def _dot(a, b, dims=NN):
    return lax.dot_general(a, b, dims, preferred_element_type=F32)


def _params(sem):
    return pltpu.CompilerParams(dimension_semantics=sem, vmem_limit_bytes=VMEM_LIMIT)


def _split3(x):
    hi = x.astype(BF16)
    r = x - hi.astype(F32)
    mid = r.astype(BF16)
    lo = (r - mid.astype(F32)).astype(BF16)
    return hi, mid, lo


def _split3_f32(x):
    hi = x.astype(BF16).astype(F32)
    r = x - hi
    mid = r.astype(BF16).astype(F32)
    return hi, mid, r - mid


def _lane_iota(shape):
    return lax.broadcasted_iota(jnp.int32, shape, len(shape) - 1)


def _row_iota(shape):
    return lax.broadcasted_iota(jnp.int32, shape, len(shape) - 2)


def _mm(name, a, b, a_spec, b_spec, out_shape, out_spec, grid, dims, acc_shape, ex=None):
    nk = grid[2]
    if ex is not None:
        return _mm_hosting(name, a, b, a_spec, b_spec, out_shape, out_spec, grid, dims, ex)

    def body(a_ref, b_ref, o_ref, *scr):
        p = _dot(a_ref[...], b_ref[...], dims)
        if nk == 1:
            o_ref[...] = p.astype(o_ref.dtype)
        else:
            acc = scr[0]
            k = pl.program_id(2)

            @pl.when(k == 0)
            def _():
                acc[...] = p

            @pl.when(k > 0)
            def _():
                acc[...] += p

            @pl.when(k == nk - 1)
            def _():
                o_ref[...] = acc[...].astype(o_ref.dtype)

    return pl.pallas_call(
        body, name=name, grid=grid, in_specs=[a_spec, b_spec], out_specs=out_spec, out_shape=out_shape,
        scratch_shapes=[pltpu.VMEM(acc_shape, F32)] if nk > 1 else [],
        compiler_params=_params(("parallel", "parallel", "arbitrary")),
    )(a, b)


def _mm_hosting(name, a, b, a_spec, b_spec, out_shape, out_spec, grid, dims, ex):
    assert grid[2] == 1
    n = len(ex.ins)

    def body(*refs):
        i, j = pl.program_id(0), pl.program_id(1)
        last = (i == grid[0] - 1) & (j == grid[1] - 1)
        (a_ref, b_ref), (o_ref,), _, begin, end = _hosted(ex, refs, 2, 1, (i == 0) & (j == 0), last, last)
        begin()
        o_ref[...] = _dot(a_ref[...], b_ref[...], dims).astype(o_ref.dtype)
        end()

    res = pl.pallas_call(
        body, name=name, grid=grid, in_specs=[a_spec, b_spec] + [ANY] * n, out_specs=[out_spec] + [ANY] * n,
        out_shape=[out_shape] + ex.out_shapes, scratch_shapes=ex.scratch(),
        compiler_params=_params(("arbitrary", "arbitrary", "arbitrary")),
    )(a, b, *ex.ins)
    return res[0], res[1:]


def _mm_nn(name, a, b, out_dtype, tm, tn):
    m, k = a.shape
    n = b.shape[1]
    return _mm(name, a, b, pl.BlockSpec((tm, k), lambda i, j, kk: (i, 0)), pl.BlockSpec((k, tn), lambda i, j, kk: (0, j)),
               jax.ShapeDtypeStruct((m, n), out_dtype), pl.BlockSpec((tm, tn), lambda i, j, kk: (i, j)),
               (m // tm, n // tn, 1), NN, (tm, tn))


def _mm_nt(name, a, b, out_dtype, tm, tn, ex=None):
    m, k = a.shape
    n = b.shape[0]
    return _mm(name, a, b, pl.BlockSpec((tm, k), lambda i, j, kk: (i, 0)), pl.BlockSpec((tn, k), lambda i, j, kk: (j, 0)),
               jax.ShapeDtypeStruct((m, n), out_dtype), pl.BlockSpec((tm, tn), lambda i, j, kk: (i, j)),
               (m // tm, n // tn, 1), NT, (tm, tn), ex)


def _mm_tn(name, a, b, tka, tn, ex=None):
    t, ka = a.shape
    n = b.shape[1]
    return _mm(name, a, b, pl.BlockSpec((t, tka), lambda i, j, kk: (0, i)), pl.BlockSpec((t, tn), lambda i, j, kk: (0, j)),
               jax.ShapeDtypeStruct((ka, n), F32), pl.BlockSpec((tka, tn), lambda i, j, kk: (i, j)),
               (ka // tka, n // tn, 1), TN, (tka, tn), ex)


def _d_h3(dhid, w_up, ex):
    tm = tn = 1024
    shard = 2 * D_FF // N_CHIPS
    per_plane = D_FF // shard
    grid = (S // tm, D // tn, N_CHIPS)
    n = len(ex.ins)

    def body(*refs):
        i, j, k = pl.program_id(0), pl.program_id(1), pl.program_id(2)
        first = (i == 0) & (j == 0) & (k == 0)
        last = (i == grid[0] - 1) & (j == grid[1] - 1) & (k == N_CHIPS - 1)
        (a_ref, b_ref), (o_ref,), (acc_ref,), begin, end = _hosted(ex, refs, 2, 1, first, first, last)
        begin()
        part = _dot(a_ref[...], b_ref[...], NT)

        @pl.when(k == 0)
        def _():
            acc_ref[...] = part

        @pl.when(k > 0)
        def _():
            acc_ref[...] += part

        @pl.when(k == N_CHIPS - 1)
        def _():
            o_ref[...] = acc_ref[...]

        end()

    res = pl.pallas_call(
        body, name="d_h3", grid=grid,
        in_specs=[pl.BlockSpec((None, tm, shard), lambda i, j, k: (k // per_plane, i, k % per_plane)),
                  pl.BlockSpec((None, tn, shard), lambda i, j, k: (k, j, 0))] + [ANY] * n,
        out_specs=[pl.BlockSpec((tm, tn), lambda i, j, k: (i, j))] + [ANY] * n,
        out_shape=[jax.ShapeDtypeStruct((S, D), F32)] + ex.out_shapes,
        scratch_shapes=[pltpu.VMEM((tm, tn), F32)] + ex.scratch(),
        compiler_params=_params(("arbitrary", "arbitrary", "arbitrary")),
    )(dhid, w_up, *ex.ins)
    return res[0], res[1:]


SHARD_IN = D_IN // N_CHIPS
SHARD_IN_PAD = -(-SHARD_IN // SUBLANES) * SUBLANES


def _dw_in(dproj, h1, ex):
    tk = 1024
    nk = S // tk
    half = D // 2
    starts = [SHARD_IN * j // LANES * LANES for j in range(N_CHIPS)]
    shifts = [SHARD_IN * j - s for j, s in enumerate(starts)]
    window = -(-(max(shifts) + SHARD_IN) // LANES) * LANES
    assert starts[-1] + window <= dproj.shape[1]
    n = len(ex.ins)

    def body(*refs):
        k = pl.program_id(0)
        (a_ref, b_ref), (o_ref,), _, begin, end = _hosted(ex, refs, 2, 1, k == 0, k == nk - 1, k == nk - 1)
        begin()

        @pl.when(k == 0)
        def _():
            o_ref[...] = jnp.zeros(o_ref.shape, F32)

        for j in range(N_CHIPS):
            win = a_ref[:, starts[j]:starts[j] + window]
            if shifts[j]:
                win = pltpu.roll(win, window - shifts[j], axis=1)
            part = _dot(win, b_ref[...], TN)
            for h in range(2):
                o_ref[j, h] += part[:SHARD_IN_PAD, h * half:(h + 1) * half]
        end()

    out_shape = (N_CHIPS, 2, SHARD_IN_PAD, half)
    res = pl.pallas_call(
        body, name="dw_in", grid=(nk,),
        in_specs=[pl.BlockSpec((tk, dproj.shape[1]), lambda k: (k, 0)), pl.BlockSpec((tk, D), lambda k: (k, 0))] + [ANY] * n,
        out_specs=[pl.BlockSpec(out_shape, lambda k: (0, 0, 0, 0))] + [ANY] * n,
        out_shape=[jax.ShapeDtypeStruct(out_shape, F32)] + ex.out_shapes,
        scratch_shapes=ex.scratch(),
        compiler_params=_params(("arbitrary",)),
    )(dproj, h1, *ex.ins)
    return res[0], res[1:]


def _rms(x, g):
    r = lax.rsqrt(jnp.mean(x * x, axis=-1, keepdims=True) + EPS)
    return x * r * g


def _rms_bwd(x, g, dy):
    r = lax.rsqrt(jnp.mean(x * x, axis=-1, keepdims=True) + EPS)
    xh = x * r
    dxh = dy * g
    dx = r * (dxh - xh * jnp.mean(dxh * xh, axis=-1, keepdims=True))
    return dx, jnp.sum(dy * xh, axis=0, keepdims=True)


def _row_spec(tr, width):
    return pl.BlockSpec((tr, width), lambda i: (i, 0))


def _vec_spec(width):
    return pl.BlockSpec((1, width), lambda i: (0, 0))


def _norm_fwd(name, x, g, ex=None):
    rows, width = x.shape
    tr = min(TR, rows)
    steps = rows // tr
    hosted = ex if ex is not None else _no_exchange()
    n = len(hosted.ins)

    def body(*refs):
        i = pl.program_id(0)
        (x_ref, g_ref), (h_ref,), _, begin, end = _hosted(hosted, refs, 2, 1, i == 0, i == steps - 1, i == steps - 1)
        begin()
        h_ref[...] = _rms(x_ref[...], g_ref[...]).astype(BF16)
        end()

    res = pl.pallas_call(
        body, name=name, grid=(steps,), in_specs=[_row_spec(tr, width), _vec_spec(width)] + [ANY] * n,
        out_specs=[_row_spec(tr, width)] + [ANY] * n,
        out_shape=[jax.ShapeDtypeStruct((rows, width), BF16)] + hosted.out_shapes, scratch_shapes=hosted.scratch(),
        compiler_params=_params(("arbitrary",)),
    )(x, g, *hosted.ins)
    return res[0] if ex is None else (res[0], res[1:])


def _proj_resid_norm(name, a, w, xp, g_post, g_pre, w_next=None):
    def body(a_ref, w_ref, xp_ref, gpost_ref, gpre_ref, *rest):
        y_ref, xn_ref, h_ref = rest[-3:] if w_next is None else rest[1:4]
        y = _dot(a_ref[...], w_ref[...])
        y_ref[...] = y
        xn = xp_ref[...] + _rms(y, gpost_ref[...])
        xn_ref[...] = xn
        h = _rms(xn, gpre_ref[...]).astype(BF16)
        h_ref[...] = h
        if w_next is not None:
            rest[4][...] = _dot(h, rest[0][...]).astype(BF16)

    mat = pl.BlockSpec((D, D), lambda i: (0, 0))
    more = [] if w_next is None else [w_next]
    return pl.pallas_call(
        body, name=name, grid=(S // TR,),
        in_specs=[_row_spec(TR, D), mat, _row_spec(TR, D), _vec_spec(D), _vec_spec(D)] + [mat] * len(more),
        out_specs=[_row_spec(TR, D)] * (3 + len(more)),
        out_shape=[jax.ShapeDtypeStruct((S, D), F32), jax.ShapeDtypeStruct((S, D), F32), jax.ShapeDtypeStruct((S, D), BF16)]
        + [jax.ShapeDtypeStruct((S, D), BF16)] * len(more),
        compiler_params=_params(("parallel",)),
    )(a, w, xp, g_post, g_pre, *more)


def _down_loss_bwd(act, w_down, x3, g_post, target):
    def body(a_ref, w_ref, x_ref, g_ref, t_ref, dres_ref, dy_ref, dg_ref, loss_ref):
        i = pl.program_id(0)

        @pl.when(i == 0)
        def _():
            dg_ref[...] = jnp.zeros_like(dg_ref)
            loss_ref[...] = jnp.zeros_like(loss_ref)

        g = g_ref[...]
        for r in range(ROW_PIECES):
            rows = slice(r * TR // ROW_PIECES, (r + 1) * TR // ROW_PIECES)
            y = _dot(a_ref[rows, :], w_ref[...])
            e = x_ref[rows, :] + _rms(y, g) - t_ref[rows, :]
            loss_ref[...] += jnp.sum(e * e, axis=0, keepdims=True) * (0.5 / D)
            dres = e * (1.0 / D)
            dres_ref[rows, :] = dres
            dy, dg = _rms_bwd(y, g, dres)
            dy_ref[rows, :] = dy.astype(BF16)
            dg_ref[...] += dg

    return pl.pallas_call(
        body, name="down_loss_bwd", grid=(S // TR,),
        in_specs=[_row_spec(TR, D_FF), pl.BlockSpec((D_FF, D), lambda i: (0, 0)), _row_spec(TR, D), _vec_spec(D),
                  _row_spec(TR, D)],
        out_specs=[_row_spec(TR, D), _row_spec(TR, D), _vec_spec(D), _vec_spec(D)],
        out_shape=[jax.ShapeDtypeStruct((S, D), F32), jax.ShapeDtypeStruct((S, D), BF16),
                   jax.ShapeDtypeStruct((1, D), F32), jax.ShapeDtypeStruct((1, D), F32)],
        compiler_params=_params(("arbitrary",)),
    )(act, w_down, x3, g_post, target)


def _mid_bwd(name, dres, xcur, g_pre, dh, yprev, g_post, w):
    def body(dres_ref, x_ref, gpre_ref, dh_ref, y_ref, gpost_ref, w_ref, dx_ref, dy_ref, da_ref, dgpre_ref, dgpost_ref):
        i = pl.program_id(0)

        @pl.when(i == 0)
        def _():
            dgpre_ref[...] = jnp.zeros_like(dgpre_ref)
            dgpost_ref[...] = jnp.zeros_like(dgpost_ref)

        dxn, dgpre = _rms_bwd(x_ref[...], gpre_ref[...], dh_ref[...])
        dx = dres_ref[...] + dxn
        dx_ref[...] = dx
        dy, dgpost = _rms_bwd(y_ref[...], gpost_ref[...], dx)
        dy = dy.astype(BF16)
        dy_ref[...] = dy
        da_ref[...] = _dot(dy, w_ref[...], NT).astype(BF16)
        dgpre_ref[...] += dgpre
        dgpost_ref[...] += dgpost

    return pl.pallas_call(
        body, name=name, grid=(S // TR,),
        in_specs=[_row_spec(TR, D), _row_spec(TR, D), _vec_spec(D), _row_spec(TR, D), _row_spec(TR, D), _vec_spec(D),
                  pl.BlockSpec((D, D), lambda i: (0, 0))],
        out_specs=[_row_spec(TR, D), _row_spec(TR, D), _row_spec(TR, D), _vec_spec(D), _vec_spec(D)],
        out_shape=[jax.ShapeDtypeStruct((S, D), F32), jax.ShapeDtypeStruct((S, D), BF16), jax.ShapeDtypeStruct((S, D), BF16),
                   jax.ShapeDtypeStruct((1, D), F32), jax.ShapeDtypeStruct((1, D), F32)],
        compiler_params=_params(("arbitrary",)),
    )(dres, xcur, g_pre, dh, yprev, g_post, w)


def _d_h1_first_bwd(dproj, w_in, dres, x, g, ex):
    nt = S // TR
    n = len(ex.ins)

    def body(*refs):
        i = pl.program_id(0)
        (dp_ref, w_ref, dres_ref, x_ref, g_ref), (dx_ref, dg_ref), _, begin, end = _hosted(
            ex, refs, 5, 2, i == 0, i == nt - 1, i == nt - 1)
        begin()

        @pl.when(i == 0)
        def _():
            dg_ref[...] = jnp.zeros_like(dg_ref)

        dxn, dg = _rms_bwd(x_ref[...], g_ref[...], _dot(dp_ref[...], w_ref[...], NT))
        dx_ref[...] = dres_ref[...] + dxn
        dg_ref[...] += dg
        end()

    res = pl.pallas_call(
        body, name="d_h1", grid=(nt,),
        in_specs=[_row_spec(TR, D_IN_PAD), pl.BlockSpec((D, D_IN_PAD), lambda i: (0, 0)), _row_spec(TR, D), _row_spec(TR, D),
                  _vec_spec(D)] + [ANY] * n,
        out_specs=[_row_spec(TR, D), _vec_spec(D)] + [ANY] * n,
        out_shape=[jax.ShapeDtypeStruct((S, D), F32), jax.ShapeDtypeStruct((1, D), F32)] + ex.out_shapes,
        scratch_shapes=ex.scratch(), input_output_aliases={2: 0}, compiler_params=_params(("arbitrary",)),
    )(dproj, w_in, dres, x, g, *ex.ins)
    return res[0], res[1], res[2:]


def _gain_bwd(name, x, g, dy):
    rows, width = x.shape

    def body(x_ref, g_ref, dy_ref, dg_ref):
        _, dg = _rms_bwd(x_ref[...], g_ref[...], dy_ref[...])
        dg_ref[...] = dg

    return pl.pallas_call(
        body, name=name, grid=(1,), in_specs=[_row_spec(rows, width), _vec_spec(width), _row_spec(rows, width)],
        out_specs=_vec_spec(width), out_shape=jax.ShapeDtypeStruct((1, width), F32),
        compiler_params=_params(("arbitrary",)),
    )(x, g, dy)


CUM_Q = DH
CUM_K = DH + 3
LSE_Q = DH + 6
BOTH_ONE = DH + 9
DEN_V = DH
DELTA = DH + 1
PREP_TR = 256
PIECE_LANES = 16
FOX_FWD_BLOCK = 1024
FOX_BWD_BLOCK = 512


def _at(lane_of_even_head, h):
    return (lane_of_even_head + DH * (h % 2)) % LANES


def _data_lanes(lane, h):
    return lane >= DH if h % 2 else lane < DH


def _pair_block(ref, off, h):
    base = ((off + DH * h) // LANES) * LANES
    return ref[:, base:base + LANES]


def _cumsum_rows(x, tri, carry):
    hi, mid, lo = _split3(x)
    return _dot(tri, hi) + _dot(tri, mid) + _dot(tri, lo) + carry


def _in_proj(h1, w_in, bf_pad):
    tr = TR

    place_q = np.zeros((LANES, HEADS * LANES), np.float32)
    place_k = np.zeros((LANES, HEADS * LANES), np.float32)
    for h in range(HEADS):
        for piece in range(3):
            place_q[PIECE_LANES * piece + h, LANES * h + _at(CUM_Q, h) + piece] = 1.0
            place_k[PIECE_LANES * piece + h, LANES * h + _at(CUM_K, h) + piece] = -1.0

    def body(h_ref, w_ref, bf_ref, pq_ref, pk_ref, qa_ref, ka_ref, va_ref, u_ref, z_ref, carry_ref):
        i = pl.program_id(0)

        @pl.when(i == 0)
        def _():
            carry_ref[...] = jnp.zeros_like(carry_ref)

        proj = _dot(h_ref[...], w_ref[...])
        u_ref[...] = proj[:, :D_POOL]
        z_ref[...] = proj[:, F_OFF:F_OFF + LANES]
        lane = _lane_iota((tr, LANES))
        z = proj[:, F_OFF:F_OFF + LANES] + bf_ref[...]
        log_f = jnp.minimum(z, 0.0) - jnp.log(1.0 + jnp.exp(-jnp.abs(z)))
        log_f = jnp.where(lane < HEADS, log_f, 0.0)
        tri = jnp.where(_row_iota((tr, tr)) >= _lane_iota((tr, tr)), 1.0, 0.0).astype(BF16)
        cum = _cumsum_rows(log_f, tri, carry_ref[0:1, :])
        carry_ref[0:1, :] = cum[tr - 1:tr, :]
        c_hi, c_mid, c_lo = _split3_f32(cum)
        pieces = (c_hi + pltpu.roll(c_mid, PIECE_LANES, 1) + pltpu.roll(c_lo, 2 * PIECE_LANES, 1)).astype(BF16)
        cum_q = _dot(pieces, pq_ref[...])
        cum_k = _dot(pieces, pk_ref[...])

        def between(first, h):
            return (lane >= _at(first, h)) & (lane < _at(first, h) + 3)

        ones_q = [jnp.where(between(CUM_K, h) | (lane == _at(BOTH_ONE, h)), 1.0, 0.0) for h in range(2)]
        ones_k = [jnp.where(between(CUM_Q, h) | between(LSE_Q, h) | (lane == _at(BOTH_ONE, h)), 1.0, 0.0) for h in range(2)]
        aug_v = [jnp.where(lane == _at(DEN_V, h), 1.0, jnp.where(between(DELTA, h), -1.0, 0.0)) for h in range(2)]
        for h in range(HEADS):
            mine = slice(LANES * h, LANES * (h + 1))
            data = _data_lanes(lane, h)
            qa_ref[h] = jnp.where(data, _pair_block(proj, Q_OFF, h) * (DH ** -0.5), cum_q[:, mine] + ones_q[h % 2]).astype(BF16)
            ka_ref[h] = jnp.where(data, _pair_block(proj, K_OFF, h), cum_k[:, mine] + ones_k[h % 2]).astype(BF16)
            va_ref[h] = jnp.where(data, _pair_block(proj, V_OFF, h), aug_v[h % 2]).astype(BF16)

    head_spec = pl.BlockSpec((HEADS, tr, LANES), lambda i: (0, i, 0))
    head_shape = jax.ShapeDtypeStruct((HEADS, S, LANES), BF16)
    place_spec = pl.BlockSpec(place_q.shape, lambda i: (0, 0))
    return pl.pallas_call(
        body, name="in_proj", grid=(S // tr,),
        in_specs=[_row_spec(tr, D), pl.BlockSpec((D, D_IN_PAD), lambda i: (0, 0)), _vec_spec(LANES), place_spec, place_spec],
        out_specs=[head_spec] * 3 + [_row_spec(tr, D_POOL), _row_spec(tr, LANES)],
        out_shape=[head_shape] * 3 + [jax.ShapeDtypeStruct((S, D_POOL), F32), jax.ShapeDtypeStruct((S, LANES), F32)],
        scratch_shapes=[pltpu.VMEM((SUBLANES, LANES), F32)], compiler_params=_params(("arbitrary",)),
    )(h1, w_in, bf_pad, jnp.asarray(place_q, BF16), jnp.asarray(place_k, BF16))


def _hosted(ex, refs, n_blocked_in, n_blocked_out, first, forward_at, last):
    n = len(ex.ins)
    own_in = refs[:n_blocked_in]
    ex_in = refs[n_blocked_in:n_blocked_in + n]
    own_out = refs[n_blocked_in + n:n_blocked_in + n + n_blocked_out]
    ex_out = refs[n_blocked_in + n + n_blocked_out:n_blocked_in + 2 * n + n_blocked_out]
    rest = refs[n_blocked_in + 2 * n + n_blocked_out:]
    args = (ex_in, ex_out, rest[-2], rest[-1])

    def begin():
        @pl.when(first)
        def _():
            ex.start(*args)

        @pl.when(forward_at)
        def _():
            ex.forward(*args)

    def end():
        @pl.when(last)
        def _():
            ex.finish(*args)

    return own_in, own_out, rest[:-2], begin, end


def _fox_fwd(qa, ka, va, ex):
    BQ = BK = FOX_FWD_BLOCK
    nq = S // BQ
    n_pairs = HEADS // 2

    def body(*refs):
        p_id, i = pl.program_id(0), pl.program_id(1)
        (qa_ref, ka_ref, va_ref), (y_ref, qab_ref), (m_scr, acc_scr), begin, end = _hosted(
            ex, refs, 3, 2, (p_id == 0) & (i == 0), (p_id == n_pairs - 1) & (i == 0), (p_id == n_pairs - 1) & (i == nq - 1))
        begin()
        lane = _lane_iota((BQ, LANES))
        causal = _row_iota((BQ, BK)) >= _lane_iota((BQ, BK))
        m_scr[...] = jnp.full_like(m_scr, NEG)
        acc_scr[...] = jnp.zeros_like(acc_scr)

        def step(j, masked):
            rows = pl.ds(pl.multiple_of(j * BK, BK), BK)
            for hh in range(2):
                s = _dot(qa_ref[hh], ka_ref[hh, rows, :], NT)
                if masked:
                    s = jnp.where(causal, s, NEG)
                m_prev = m_scr[hh]
                m_new = jnp.maximum(m_prev, jnp.max(s, axis=1, keepdims=True))
                p = jnp.exp(s - jnp.tile(m_new, (1, BK // LANES)))
                acc_scr[hh] = jnp.exp(m_prev - m_new) * acc_scr[hh] + _dot(p.astype(BF16), va_ref[hh, rows, :])
                m_scr[hh] = m_new

        def full_step(j, carry):
            step(j, False)
            return carry

        lax.fori_loop(0, i, full_step, 0)
        step(i, True)
        outs = []
        for hh in range(2):
            acc = acc_scr[hh]
            den_lane, lse_lane = _at(DEN_V, hh), _at(LSE_Q, hh)
            den = jnp.broadcast_to(acc[:, den_lane:den_lane + 1], (BQ, LANES))
            outs.append(acc * (1.0 / den))
            n_hi, n_mid, n_lo = _split3(-(m_scr[hh] + jnp.log(den)))
            qab_ref[hh] = jnp.where(lane == lse_lane, n_hi,
                                    jnp.where(lane == lse_lane + 1, n_mid, jnp.where(lane == lse_lane + 2, n_lo, qa_ref[hh])))
        y_ref[...] = jnp.where(lane < DH, outs[0], outs[1]).astype(BF16)
        end()

    pair_rows = pl.BlockSpec((2, BQ, LANES), lambda p, i: (p, i, 0))
    pair_all = pl.BlockSpec((2, S, LANES), lambda p, i: (p, 0, 0))
    n = len(ex.ins)
    res = pl.pallas_call(
        body, name="fox_fwd", grid=(n_pairs, nq), in_specs=[pair_rows, pair_all, pair_all] + [ANY] * n,
        out_specs=[pl.BlockSpec((BQ, LANES), lambda p, i: (i, D_POOL // LANES + p)), pair_rows] + [ANY] * n,
        out_shape=[jax.ShapeDtypeStruct((S, D), BF16), jax.ShapeDtypeStruct((HEADS, S, LANES), BF16)] + ex.out_shapes,
        scratch_shapes=[pltpu.VMEM((2, BQ, LANES), F32), pltpu.VMEM((2, BQ, LANES), F32)] + ex.scratch(),
        compiler_params=_params(("arbitrary", "arbitrary")),
    )(qa, ka, va, *ex.ins)
    return res[0], res[1], res[2:]


def _bwd_xa_mix(dqx, w_xq, dres, x2, g_pre, y1, g_post, w_mix_out, ycat, ex):
    steps = S // TR
    n = len(ex.ins)

    def body(*refs):
        i = pl.program_id(0)
        ((dq_ref, wq_ref, dres_ref, x_ref, gpre_ref, y_ref, gpost_ref, wm_ref, ycat_ref),
         (dx_ref, dy_ref, dgpre_ref, dgpost_ref, dp_ref, doa_ref), _, begin, end) = _hosted(
            ex, refs, 9, 6, i == 0, i == 0, i == steps - 1)
        begin()

        @pl.when(i == 0)
        def _():
            dgpre_ref[...] = jnp.zeros_like(dgpre_ref)
            dgpost_ref[...] = jnp.zeros_like(dgpost_ref)

        dxn, dgpre = _rms_bwd(x_ref[...], gpre_ref[...], _dot(dq_ref[...], wq_ref[...], NT))
        dx = dres_ref[...] + dxn
        dx_ref[...] = dx
        dy, dgpost = _rms_bwd(y_ref[...], gpost_ref[...], dx)
        dy = dy.astype(BF16)
        dy_ref[...] = dy
        dgpre_ref[...] += dgpre
        dgpost_ref[...] += dgpost

        d = _dot(dy, wm_ref[...], NT)
        dp_ref[...] = d[:, :D_POOL]
        lane = _lane_iota((TR, LANES))
        low = lane < DH
        for p in range(HEADS // 2):
            cols = slice(D_POOL + LANES * p, D_POOL + LANES * (p + 1))
            do = d[:, cols]
            prod = do * ycat_ref[:, cols].astype(F32)
            deltas = (jnp.sum(jnp.where(low, prod, 0.0), axis=1, keepdims=True),
                      jnp.sum(jnp.where(low, 0.0, prod), axis=1, keepdims=True))
            for hh in range(2):
                d_hi, d_mid, d_lo = _split3_f32(deltas[hh])
                dl = _at(DELTA, hh)
                aug = jnp.where(lane == dl, d_hi, jnp.where(lane == dl + 1, d_mid, jnp.where(lane == dl + 2, d_lo, 0.0)))
                doa_ref[2 * p + hh] = jnp.where(_data_lanes(lane, hh), do, aug).astype(BF16)
        end()

    mat = pl.BlockSpec((D, D), lambda i: (0, 0))
    res = pl.pallas_call(
        body, name="bwd_xa_mix", grid=(steps,),
        in_specs=[_row_spec(TR, D), mat, _row_spec(TR, D), _row_spec(TR, D), _vec_spec(D), _row_spec(TR, D), _vec_spec(D), mat,
                  _row_spec(TR, D)] + [ANY] * n,
        out_specs=[_row_spec(TR, D), _row_spec(TR, D), _vec_spec(D), _vec_spec(D), _row_spec(TR, D_POOL),
                   pl.BlockSpec((HEADS, TR, LANES), lambda i: (0, i, 0))] + [ANY] * n,
        out_shape=[jax.ShapeDtypeStruct((S, D), F32), jax.ShapeDtypeStruct((S, D), BF16), jax.ShapeDtypeStruct((1, D), F32),
                   jax.ShapeDtypeStruct((1, D), F32), jax.ShapeDtypeStruct((S, D_POOL), F32),
                   jax.ShapeDtypeStruct((HEADS, S, LANES), BF16)] + ex.out_shapes,
        scratch_shapes=ex.scratch(), compiler_params=_params(("arbitrary",)),
    )(dqx, w_xq, dres, x2, g_pre, y1, g_post, w_mix_out, ycat, *ex.ins)
    return res[:6], res[6:]


def _fox_bwd(qab, doa, ka, va, ex):
    BQ = BK = FOX_BWD_BLOCK
    nk = S // BK
    n_pairs = HEADS // 2

    def body(*refs):
        p_id, j = pl.program_id(0), pl.program_id(1)
        (qab_ref, doa_ref, ka_ref, va_ref), (dqa_ref, dka_ref, dva_ref), (dv_ref,), begin, end = _hosted(
            ex, refs, 4, 3, (p_id == 0) & (j == 0), (p_id == n_pairs - 1) & (j == 0), (p_id == n_pairs - 1) & (j == nk - 1))
        begin()

        @pl.when(j == 0)
        def _():
            dqa_ref[...] = jnp.zeros_like(dqa_ref)

        causal = _row_iota((BQ, BK)) >= _lane_iota((BQ, BK))
        dka_ref[...] = jnp.zeros_like(dka_ref)
        dv_ref[...] = jnp.zeros_like(dv_ref)

        def step(i, masked):
            rows = pl.ds(pl.multiple_of(i * BQ, BQ), BQ)
            for hh in range(2):
                kb = ka_ref[hh]
                q = qab_ref[hh, rows, :]
                do = doa_ref[hh, rows, :]
                s = _dot(q, kb, NT)
                if masked:
                    s = jnp.where(causal, s, NEG)
                p = jnp.exp(s)
                ds = p * _dot(do, va_ref[hh], NT)
                pb = p.astype(BF16)
                dsb = ds.astype(BF16)
                dv_ref[hh] += _dot(pb, do, TN)
                dka_ref[hh] += _dot(dsb, q, TN)
                dqa_ref[hh, rows, :] += _dot(dsb, kb)

        def full_step(i, carry):
            step(i, False)
            return carry

        step(j, True)
        lax.fori_loop(j + 1, nk, full_step, 0)
        dva_ref[...] = dv_ref[...].astype(BF16)
        end()

    pair_all = pl.BlockSpec((2, S, LANES), lambda p, j: (p, 0, 0))
    pair_rows = pl.BlockSpec((2, BK, LANES), lambda p, j: (p, j, 0))
    shape = jax.ShapeDtypeStruct((HEADS, S, LANES), F32)
    n = len(ex.ins)
    res = pl.pallas_call(
        body, name="fox_bwd", grid=(n_pairs, nk), in_specs=[pair_all, pair_all, pair_rows, pair_rows] + [ANY] * n,
        out_specs=[pair_all, pair_rows, pair_rows] + [ANY] * n,
        out_shape=[shape, shape, jax.ShapeDtypeStruct((HEADS, S, LANES), BF16)] + ex.out_shapes,
        scratch_shapes=[pltpu.VMEM((2, BK, LANES), F32)] + ex.scratch(), compiler_params=_params(("arbitrary", "arbitrary")),
    )(qab, doa, ka, va, *ex.ins)
    return res[0], res[1], res[2], res[3:]


def _fox_bwd_post(dqa, dka, dva, du, proj, bf_pad):
    tr = PREP_TR
    nt = S // tr

    pick = np.zeros((HEADS * LANES, LANES), np.float32)
    for h in range(HEADS):
        pick[LANES * h + _at(BOTH_ONE, h), h] = 1.0

    def body(dqa_ref, dka_ref, dva_ref, du_ref, z_ref, bf_ref, pick_ref, dp_ref, dbf_ref, carry_ref):
        i = pl.program_id(0)

        @pl.when(i == 0)
        def _():
            carry_ref[...] = jnp.zeros_like(carry_ref)
            dbf_ref[...] = jnp.zeros_like(dbf_ref)

        lane = _lane_iota((tr, LANES))
        diff = jnp.concatenate([dqa_ref[h] - dka_ref[h] for h in range(HEADS)], axis=1)
        hi = diff.astype(BF16)
        dcum = _dot(hi, pick_ref[...]) + _dot((diff - hi.astype(F32)).astype(BF16), pick_ref[...])
        tri =jnp.where(_lane_iota((tr, tr)) >= _row_iota((tr, tr)), 1.0, 0.0).astype(BF16)
        dlog_f = _cumsum_rows(dcum, tri, carry_ref[0:1, :])
        carry_ref[0:1, :] = dlog_f[0:1, :]
        z = z_ref[...] + bf_ref[...]
        df = jnp.where(lane < HEADS, dlog_f / (1.0 + jnp.exp(z)), 0.0)
        dbf_ref[...] += jnp.sum(df, axis=0, keepdims=True)

        dp_ref[:, 0:D_POOL] = du_ref[...].astype(BF16)
        low = lane < DH
        for ref, off, scale in ((dqa_ref, Q_OFF, DH ** -0.5), (dka_ref, K_OFF, 1.0), (dva_ref, V_OFF, 1.0)):
            for p in range(HEADS // 2):
                blk = jnp.where(low, ref[2 * p], ref[2 * p + 1])
                dp_ref[:, off + LANES * p:off + LANES * (p + 1)] = (blk * scale).astype(BF16)
        dp_ref[:, F_OFF:F_OFF + LANES] = df.astype(BF16)

    head_spec = pl.BlockSpec((HEADS, tr, LANES), lambda i: (0, nt - 1 - i, 0))
    return pl.pallas_call(
        body, name="fox_bwd_post", grid=(nt,),
        in_specs=[head_spec, head_spec, head_spec, pl.BlockSpec((tr, D_POOL), lambda i: (nt - 1 - i, 0)),
                  pl.BlockSpec((tr, LANES), lambda i: (nt - 1 - i, 0)), _vec_spec(LANES),
                  pl.BlockSpec(pick.shape, lambda i: (0, 0))],
        out_specs=[pl.BlockSpec((tr, D_IN_PAD), lambda i: (nt - 1 - i, 0)), _vec_spec(LANES)],
        out_shape=[jax.ShapeDtypeStruct((S, D_IN_PAD), BF16), jax.ShapeDtypeStruct((1, LANES), F32)],
        scratch_shapes=[pltpu.VMEM((SUBLANES, LANES), F32)],
        compiler_params=_params(("arbitrary",)),
    )(dqa, dka, dva, du, proj, bf_pad, jnp.asarray(pick, BF16))


POOL_HALO = 16


def _by_group(lane, a2, a4, a8, a16):
    return jnp.where(lane < 64, a2, jnp.where(lane < 128, a4, jnp.where(lane < 192, a8, a16)))


def _window_count(lane, t):
    return jnp.minimum(t + 1, _by_group(lane, 2, 4, 8, 16)).astype(F32)


def _pool_diff(u, halo, first, tile):
    n = TR + POOL_HALO
    ext = jnp.concatenate([jnp.where(first, 0.0, halo), u], axis=0)
    s2 = ext + pltpu.roll(ext, 1, 0)
    s4 = s2 + pltpu.roll(s2, 2, 0)
    s8 = s4 + pltpu.roll(s4, 4, 0)
    s16 = s8 + pltpu.roll(s8, 8, 0)
    lane = _lane_iota((n, D_POOL))
    win = _by_group(lane, s2, s4, s8, s16)[POOL_HALO:]
    lane = _lane_iota((TR, D_POOL))
    t = tile * TR + _row_iota((TR, D_POOL))
    return win / _window_count(lane, t) - u


def _prev_halo(rows, width, col):
    per = TR // rows
    return pl.BlockSpec((rows, width), lambda i: (jnp.maximum(i * per - 1, 0), col))


def _next_halo(rows, width, col):
    per = TR // rows
    return pl.BlockSpec((rows, width), lambda i: (jnp.minimum((i + 1) * per, S // rows - 1), col))


def _pool_fwd(proj, w_bd, ps, ycat):
    def body(u_ref, halo_ref, w_ref, ps_ref, ycat_ref, y_ref):
        i = pl.program_id(0)
        diff = _pool_diff(u_ref[...], halo_ref[...], i == 0, i)
        y_ref[...] = (_dot(diff.astype(BF16), w_ref[...]) * ps_ref[...]).astype(BF16)

    return pl.pallas_call(
        body, name="pool_fwd", grid=(S // TR,),
        in_specs=[_row_spec(TR, D_POOL), _prev_halo(POOL_HALO, D_POOL, 0),
                  pl.BlockSpec((D_POOL, D_POOL), lambda i: (0, 0)), _vec_spec(D_POOL), ANY],
        out_specs=_row_spec(TR, D_POOL), out_shape=jax.ShapeDtypeStruct((S, D), BF16), input_output_aliases={4: 0},
        compiler_params=_params(("parallel",)),
    )(proj, proj, w_bd, ps, ycat)


def _pool_bwd(proj, dycat, w_bd, w_bd_t, ps):
    nt = S // TR
    n = TR + POOL_HALO

    def body(u_ref, halo_ref, dy_ref, dyn_ref, w_ref, wt_ref, ps_ref, du_ref, dw_ref, dps_ref):
        i = pl.program_id(0)

        @pl.when(i == 0)
        def _():
            dw_ref[...] = jnp.zeros_like(dw_ref)
            dps_ref[...] = jnp.zeros_like(dps_ref)

        diff = _pool_diff(u_ref[...], halo_ref[...], i == 0, i).astype(BF16)
        dy = dy_ref[...]
        dps_ref[...] += jnp.sum(dy * _dot(diff, w_ref[...]), axis=0, keepdims=True)
        dy_ext = jnp.concatenate([dy, jnp.where(i == nt - 1, 0.0, dyn_ref[...])], axis=0)
        dmixed = (dy_ext * ps_ref[...]).astype(BF16)
        ddiff = _dot(dmixed, wt_ref[...])
        dw_ref[...] += _dot(diff, dmixed[:TR], TN)
        lane = _lane_iota((n, D_POOL))
        t = i * TR + _row_iota((n, D_POOL))
        e = ddiff / _window_count(lane, t)
        f2 = e + pltpu.roll(e, n - 1, 0)
        f4 = f2 + pltpu.roll(f2, n - 2, 0)
        f8 = f4 + pltpu.roll(f4, n - 4, 0)
        f16 = f8 + pltpu.roll(f8, n - 8, 0)
        du_ref[...] = _by_group(lane, f2, f4, f8, f16)[:TR] - ddiff[:TR]

    mat = pl.BlockSpec((D_POOL, D_POOL), lambda i: (0, 0))
    return pl.pallas_call(
        body, name="pool_bwd", grid=(nt,),
        in_specs=[_row_spec(TR, D_POOL), _prev_halo(POOL_HALO, D_POOL, 0), _row_spec(TR, D_POOL),
                  _next_halo(POOL_HALO, D_POOL, 0), mat, mat, _vec_spec(D_POOL)],
        out_specs=[_row_spec(TR, D_POOL), mat, _vec_spec(D_POOL)],
        out_shape=[jax.ShapeDtypeStruct((S, D_POOL), F32), jax.ShapeDtypeStruct((D_POOL, D_POOL), F32),
                   jax.ShapeDtypeStruct((1, D_POOL), F32)],
        compiler_params=_params(("arbitrary",)),
    )(proj, proj, dycat, dycat, w_bd, w_bd_t, ps)


def _xa_probs(q, k):
    s = _dot(q, k, NT) * (XA_DH ** -0.5)
    e = jnp.exp(s - jnp.max(s, axis=-1, keepdims=True))
    return e * (1.0 / jnp.sum(e, axis=-1, keepdims=True))


def _xattn_fwd(qx, kv):
    def body(q_ref, kv_ref, o_ref):
        for h in range(XA_HEADS):
            cols = slice(XA_DH * h, XA_DH * (h + 1))
            vcols = slice(D + XA_DH * h, D + XA_DH * (h + 1))
            p = _xa_probs(q_ref[:, cols], kv_ref[:, cols])
            o_ref[:, cols] = _dot(p.astype(BF16), kv_ref[:, vcols]).astype(BF16)

    return pl.pallas_call(
        body, name="xattn_fwd", grid=(S // TR,),
        in_specs=[_row_spec(TR, D), pl.BlockSpec((MEM, 2 * D), lambda i: (0, 0))],
        out_specs=_row_spec(TR, D), out_shape=jax.ShapeDtypeStruct((S, D), BF16),
        compiler_params=_params(("parallel",)),
    )(qx, kv)


def _xattn_bwd(qx, kv, dxo):
    def body(q_ref, kv_ref, do_ref, dq_ref, dkv_ref):
        i = pl.program_id(0)

        @pl.when(i == 0)
        def _():
            dkv_ref[...] = jnp.zeros_like(dkv_ref)

        for h in range(XA_HEADS):
            cols = slice(XA_DH * h, XA_DH * (h + 1))
            vcols = slice(D + XA_DH * h, D + XA_DH * (h + 1))
            q = q_ref[:, cols]
            k = kv_ref[:, cols]
            do = do_ref[:, cols]
            p = _xa_probs(q, k)
            dkv_ref[:, vcols] += _dot(p.astype(BF16), do, TN)
            dp = _dot(do, kv_ref[:, vcols], NT)
            ds = (p * (dp - jnp.sum(p * dp, axis=-1, keepdims=True)) * (XA_DH ** -0.5)).astype(BF16)
            dq_ref[:, cols] = _dot(ds, k).astype(BF16)
            dkv_ref[:, cols] += _dot(ds, q, TN)

    kv_spec = pl.BlockSpec((MEM, 2 * D), lambda i: (0, 0))
    return pl.pallas_call(
        body, name="xattn_bwd", grid=(S // TR,), in_specs=[_row_spec(TR, D), kv_spec, _row_spec(TR, D)],
        out_specs=[_row_spec(TR, D), kv_spec],
        out_shape=[jax.ShapeDtypeStruct((S, D), BF16), jax.ShapeDtypeStruct((MEM, 2 * D), F32)],
        compiler_params=_params(("arbitrary",)),
    )(qx, kv, dxo)


CONV_HALO = SUBLANES
TC = 512
TC_FWD = 1024
GELU_K = 0.7978845608028654
GELU_C = 0.044715


def _conv3(ext, w, rows):
    h0 = ext[CONV_HALO:CONV_HALO + rows]
    h1 = pltpu.roll(ext, 1, 0)[CONV_HALO:CONV_HALO + rows]
    h2 = pltpu.roll(ext, 2, 0)[CONV_HALO:CONV_HALO + rows]
    return w[2:3] * h0 + w[1:2] * h1 + w[0:1] * h2 + w[3:4], (h2, h1, h0)


def _conv_specs(tc):
    main = pl.BlockSpec((2, TR, tc), lambda j, i: (0, i, j))
    per = TR // CONV_HALO
    prev = pl.BlockSpec((2, CONV_HALO, tc), lambda j, i: (0, jnp.maximum(i * per - 1, 0), j))
    nxt = pl.BlockSpec((2, CONV_HALO, tc), lambda j, i: (0, jnp.minimum((i + 1) * per, S // CONV_HALO - 1), j))
    par = pl.BlockSpec((2, SUBLANES, tc), lambda j, i: (0, 0, j))
    return main, prev, nxt, par


def _convgate_fwd(hid, cwb):
    tc = TC_FWD

    def body(h_ref, hp_ref, w_ref, act_ref):
        i = pl.program_id(1)
        c = []
        for g in range(2):
            ext = jnp.concatenate([jnp.where(i == 0, 0.0, hp_ref[g]), h_ref[g]], axis=0)
            c.append(_conv3(ext, w_ref[g], TR)[0])
        gate, up = c
        act_ref[...] = (jax.nn.gelu(gate, approximate=True) * up).astype(BF16)

    main, prev, _, par = _conv_specs(tc)
    return pl.pallas_call(
        body, name="convgate_fwd", grid=(D_FF // tc, S // TR), in_specs=[main, prev, par],
        out_specs=pl.BlockSpec((TR, tc), lambda j, i: (i, j)), out_shape=jax.ShapeDtypeStruct((S, D_FF), BF16),
        compiler_params=_params(("parallel", "parallel")),
    )(hid, hid, cwb)


def _convgate_bwd(hid, dact, cwb):
    nr = S // TR
    n = TR + CONV_HALO

    def body(h_ref, hp_ref, hn_ref, da_ref, dan_ref, w_ref, dh_ref, dw_ref):
        i = pl.program_id(1)

        @pl.when(i == 0)
        def _():
            dw_ref[...] = jnp.zeros_like(dw_ref)

        da = jnp.concatenate([da_ref[...], jnp.where(i == nr - 1, 0.0, dan_ref[...])], axis=0)
        c, taps = [], []
        for g in range(2):
            ext = jnp.concatenate([jnp.where(i == 0, 0.0, hp_ref[g]), h_ref[g], hn_ref[g]], axis=0)
            cg, tg = _conv3(ext, w_ref[g], n)
            c.append(cg)
            taps.append(tg)
        gate, up = c
        th = jnp.tanh(GELU_K * (gate + GELU_C * gate * gate * gate))
        gelu = 0.5 * gate * (1.0 + th)
        dgelu = 0.5 * (1.0 + th) + 0.5 * gate * (1.0 - th * th) * GELU_K * (1.0 + 3.0 * GELU_C * gate * gate)
        for g, dc in enumerate((da * up * dgelu, da * gelu)):
            w = w_ref[g]
            dh = w[2:3] * dc[:TR] + w[1:2] * pltpu.roll(dc, n - 1, 0)[:TR] + w[0:1] * pltpu.roll(dc, n - 2, 0)[:TR]
            dh_ref[g] = dh.astype(BF16)
            dcm = dc[:TR]
            for r in range(3):
                dw_ref[g, r:r + 1, :] += jnp.sum(dcm * taps[g][r][:TR], axis=0, keepdims=True)
            dw_ref[g, 3:4, :] += jnp.sum(dcm, axis=0, keepdims=True)

    main, prev, nxt, par = _conv_specs(TC)
    per = TR // CONV_HALO
    return pl.pallas_call(
        body, name="convgate_bwd", grid=(D_FF // TC, nr),
        in_specs=[main, prev, nxt, pl.BlockSpec((TR, TC), lambda j, i: (i, j)),
                  pl.BlockSpec((CONV_HALO, TC), lambda j, i: (jnp.minimum((i + 1) * per, S // CONV_HALO - 1), j)), par],
        out_specs=[main, par],
        out_shape=[jax.ShapeDtypeStruct((2, S, D_FF), BF16), jax.ShapeDtypeStruct((2, SUBLANES, D_FF), F32)],
        compiler_params=_params(("parallel", "arbitrary")),
    )(hid, hid, hid, dact, dact, cwb)


def _adam_update(w, g, m, v):
    m = ADAM_B1 * m + (1.0 - ADAM_B1) * g
    v = ADAM_B2 * v + (1.0 - ADAM_B2) * (g * g)
    m_hat = m / (1.0 - ADAM_B1 ** ADAM_STEP)
    v_hat = v / (1.0 - ADAM_B2 ** ADAM_STEP)
    return -ADAM_LR * (m_hat / (jnp.sqrt(v_hat) + ADAM_EPS) + ADAM_WD * w), m, v


def _row_tile(rows, cols, itemsize=4, target=TILE_BYTES):
    tr = SUBLANES
    while rows % (2 * tr) == 0 and 2 * tr * cols * itemsize <= target:
        tr *= 2
    assert rows % tr == 0, (rows, tr)
    return rows if rows % (2 * tr) and 16 * tr * cols * itemsize < target else tr


def _adamw(name, w, g, m, v):
    rows, cols = w.shape
    tr = rows if rows * cols * 4 <= TILE_BYTES // 2 else _row_tile(rows, cols, target=TILE_BYTES // 2)

    def body(w_ref, g_ref, m_ref, v_ref, d_ref, nm_ref, nv_ref):
        d_ref[...], nm_ref[...], nv_ref[...] = _adam_update(w_ref[...], g_ref[...], m_ref[...], v_ref[...])

    spec = _row_spec(tr, cols)
    shape = jax.ShapeDtypeStruct((rows, cols), F32)
    return pl.pallas_call(
        body, name=name, grid=(rows // tr,), in_specs=[spec] * 4, out_specs=[spec] * 3, out_shape=[shape] * 3,
        compiler_params=_params(("parallel",)),
    )(w, g, m, v)


def _adamw_halves(name, core, w, g_mine, g_sibling, m, v):
    rows, cols = w.shape
    half = rows // 2
    tr = _row_tile(half, cols, target=TILE_BYTES // 2)
    per = half // tr

    def body(core_ref, w_ref, gm_ref, gs_ref, m_ref, v_ref, g_ref, d_ref, nm_ref, nv_ref):
        g = jnp.where(pl.program_id(0) // per == core_ref[0], gm_ref[...], gs_ref[...])
        g_ref[...] = g
        d_ref[...], nm_ref[...], nv_ref[...] = _adam_update(w_ref[...], g, m_ref[...], v_ref[...])

    spec = pl.BlockSpec((tr, cols), lambda i, core_ref: (i, 0))
    half_spec = pl.BlockSpec((tr, cols), lambda i, core_ref: (i % per, 0))
    shape = jax.ShapeDtypeStruct((rows, cols), F32)
    return pl.pallas_call(
        body, name=name, out_shape=[shape] * 4,
        grid_spec=pltpu.PrefetchScalarGridSpec(
            num_scalar_prefetch=1, grid=(rows // tr,), in_specs=[spec, half_spec, half_spec, spec, spec], out_specs=[spec] * 4),
        compiler_params=_params(("parallel",)),
    )(core, w, g_mine, g_sibling, m, v)


def _adamw_halves_columns(name, core, w, g_mine, g_sibling, m, v):
    cols, _, rows = w.shape
    tl = 2 * LANES
    per = rows // 2 // tl

    def body(core_ref, w_ref, gm_ref, gs_ref, m_ref, v_ref, g_ref, d_ref, nm_ref, nv_ref):
        g = jnp.where(pl.program_id(0) // per == core_ref[0], gm_ref[...], gs_ref[...])
        g_ref[...] = g
        d_ref[...], nm_ref[...], nv_ref[...] = _adam_update(w_ref[...], g, m_ref[...], v_ref[...])

    spec = pl.BlockSpec((cols, 1, tl), lambda i, core_ref: (0, 0, i))
    half_spec = pl.BlockSpec((cols, 1, tl), lambda i, core_ref: (0, 0, i % per))
    shape = jax.ShapeDtypeStruct((cols, 1, rows), F32)
    return pl.pallas_call(
        body, name=name, out_shape=[shape] * 4,
        grid_spec=pltpu.PrefetchScalarGridSpec(
            num_scalar_prefetch=1, grid=(rows // tl,), in_specs=[spec, half_spec, half_spec, spec, spec], out_specs=[spec] * 4),
        compiler_params=_params(("parallel",)),
    )(core, w, g_mine, g_sibling, m, v)


def _chip_sum(name, core, g, other):
    _, _, half, cols = g.shape
    tr = _row_tile(half, cols)

    def body(core_ref, g_ref, o_ref, p_ref):
        p_ref[...] = (g_ref[...] + o_ref[...]).astype(BF16)

    spec = pl.BlockSpec((None, tr, cols), lambda j, i, core_ref: (j, i, 0))
    return pl.pallas_call(
        body, name=name, out_shape=jax.ShapeDtypeStruct((N_CHIPS, half, cols), BF16),
        grid_spec=pltpu.PrefetchScalarGridSpec(
            num_scalar_prefetch=1, grid=(N_CHIPS, half // tr),
            in_specs=[pl.BlockSpec((None, None, tr, cols), lambda j, i, core_ref: (j, core_ref[0], i, 0)), spec],
            out_specs=spec),
        compiler_params=_params(("parallel", "parallel")),
    )(core, g, other)


def _mesh_sum(name, chip, received, own):
    _, half, cols = received.shape
    tr = _row_tile(half, cols, itemsize=2 * N_CHIPS)

    def body(chip_ref, r_ref, own_ref, o_ref):
        acc = None
        for j in range(N_CHIPS):
            term = jnp.where(chip_ref[0] == j, own_ref[...], r_ref[j]).astype(F32)
            acc = term if acc is None else acc + term
        o_ref[...] = acc

    return pl.pallas_call(
        body, name=name, out_shape=jax.ShapeDtypeStruct((half, cols), F32),
        grid_spec=pltpu.PrefetchScalarGridSpec(
            num_scalar_prefetch=1, grid=(half // tr,),
            in_specs=[pl.BlockSpec((N_CHIPS, tr, cols), lambda i, chip_ref: (0, i, 0)),
                      pl.BlockSpec((None, tr, cols), lambda i, chip_ref: (chip_ref[0], i, 0))],
            out_specs=pl.BlockSpec((tr, cols), lambda i, chip_ref: (i, 0))),
        compiler_params=_params(("parallel",)),
    )(chip, received, own)


CHIP_FLIPS = ((1, 0), (0, 1), (1, 1))


def _place():
    x, y, c = lax.axis_index("x"), lax.axis_index("y"), lax.axis_index("c")
    return x, y, c, 2 * x + y


def _remote(src, dst, sems_s, sems_r, k, dev):
    return pltpu.make_async_remote_copy(src_ref=src, dst_ref=dst, send_sem=sems_s.at[k], recv_sem=sems_r.at[k],
                                        device_id=dev, device_id_type=MESH)


class _Exchange:
    def __init__(self, ins, out_shapes, n_sems, start, forward, finish):
        self.ins, self.out_shapes, self.n_sems = list(ins), list(out_shapes), n_sems
        self.start, self.forward, self.finish = start, forward, finish

    def scratch(self):
        return [pltpu.SemaphoreType.DMA((self.n_sems,)), pltpu.SemaphoreType.DMA((self.n_sems,))]

    def run(self, name):
        n = len(self.ins)

        def body(*refs):
            args = (refs[:n], refs[n:2 * n]) + tuple(refs[2 * n:])
            self.start(*args)
            self.forward(*args)
            self.finish(*args)

        return pl.pallas_call(
            body, name=name, in_specs=[ANY] * n, out_specs=[ANY] * n, out_shape=self.out_shapes, scratch_shapes=self.scratch(),
        )(*self.ins)


def _all_gather_weights(halved, whole):
    nh, nw = len(halved), len(whole)
    n_arr = nh + nw

    def copies(ins, outs, sems_s, sems_r):
        x, y, c, me = _place()
        sibling = (x, y, 1 - c)
        own = [_remote(ins[k], outs[k].at[me], sems_s, sems_r, k, sibling) for k in range(n_arr)]
        first, passed = [], []
        for k in range(n_arr):
            for f, (fx, fy) in enumerate(CHIP_FLIPS):
                src, dst = (ins[k].at[c], outs[k].at[me, c]) if k < nh else (ins[k], outs[k].at[me])
                first.append(_remote(src, dst, sems_s, sems_r, n_arr + 3 * k + f, (x ^ fx, y ^ fy, c)))
        for k in range(nh):
            for f, (fx, fy) in enumerate(CHIP_FLIPS):
                landed = outs[k].at[2 * (x ^ fx) + (y ^ fy), c]
                passed.append(_remote(landed, landed, sems_s, sems_r, 4 * n_arr + 3 * k + f, sibling))
        return own, first, passed

    def start(*refs):
        own, first, _ = copies(*refs)
        for cp in own + first:
            cp.start()

    def forward(*refs):
        _, first, passed = copies(*refs)
        for arrived, cp in zip(first, passed):
            arrived.wait_recv()
            cp.start()

    def finish(*refs):
        own, first, passed = copies(*refs)
        for cp in first[3 * nh:] + passed + own:
            cp.wait_recv()
        for cp in first + passed + own:
            cp.wait_send()

    shapes = [jax.ShapeDtypeStruct((N_CHIPS,) + a.shape, a.dtype) for a in list(halved) + list(whole)]
    return _Exchange(list(halved) + list(whole), shapes, 7 * nh + 4 * nw, start, forward, finish)


def _swap_halves(gs):
    n = len(gs)

    def copies(ins, outs, sems_s, sems_r):
        x, y, c, _ = _place()
        return [_remote(ins[k].at[:, 1 - c], outs[k], sems_s, sems_r, k, (x, y, 1 - c)) for k in range(n)]

    def start(*refs):
        for cp in copies(*refs):
            cp.start()

    def finish(*refs):
        for cp in copies(*refs):
            cp.wait()

    shapes = [jax.ShapeDtypeStruct((g.shape[0],) + g.shape[2:], g.dtype) for g in gs]
    return _Exchange(gs, shapes, n, start, _no_copies, finish)


def _scatter_chips(ps):
    n = len(ps)

    def copies(ins, outs, sems_s, sems_r):
        x, y, c, me = _place()
        return [_remote(ins[k].at[2 * (x ^ fx) + (y ^ fy)], outs[k].at[me], sems_s, sems_r, 3 * k + f, (x ^ fx, y ^ fy, c))
                for k in range(n) for f, (fx, fy) in enumerate(CHIP_FLIPS)]

    def start(*refs):
        for cp in copies(*refs):
            cp.start()

    def forward(*refs):
        pass

    def finish(*refs):
        for cp in copies(*refs):
            cp.wait()

    shapes = [jax.ShapeDtypeStruct(p.shape, p.dtype) for p in ps]
    return _Exchange(ps, shapes, 3 * n, start, forward, finish)


def _swap_reduced(rs):
    n = len(rs)

    def copies(ins, outs, sems_s, sems_r):
        x, y, c, _ = _place()
        return [_remote(ins[k], outs[k], sems_s, sems_r, k, (x, y, 1 - c)) for k in range(n)]

    def start(*refs):
        for cp in copies(*refs):
            cp.start()

    def finish(*refs):
        for cp in copies(*refs):
            cp.wait()

    return _Exchange(rs, [jax.ShapeDtypeStruct(r.shape, r.dtype) for r in rs], n, start, _no_copies, finish)


N_DEV = 8


def _gather_small(buf):
    def copies(ins, outs, sems_s, sems_r):
        x, y, c, chip = _place()
        sibling = (x, y, 1 - c)
        own = _remote(ins[0], outs[0].at[2 * chip + c], sems_s, sems_r, 0, sibling)
        first = [_remote(ins[0], outs[0].at[2 * chip + c], sems_s, sems_r, 1 + f, (x ^ fx, y ^ fy, c))
                 for f, (fx, fy) in enumerate(CHIP_FLIPS)]
        passed = []
        for f, (fx, fy) in enumerate(CHIP_FLIPS):
            landed = outs[0].at[2 * (2 * (x ^ fx) + (y ^ fy)) + c]
            passed.append(_remote(landed, landed, sems_s, sems_r, 4 + f, sibling))
        return own, first, passed

    def start(*refs):
        own, first, _ = copies(*refs)
        for cp in [own] + first:
            cp.start()

    def forward(*refs):
        _, first, passed = copies(*refs)
        for arrived, cp in zip(first, passed):
            arrived.wait_recv()
            cp.start()

    def finish(*refs):
        own, first, passed = copies(*refs)
        for cp in passed + [own]:
            cp.wait_recv()
        for cp in first + passed + [own]:
            cp.wait_send()

    return _Exchange([buf], [jax.ShapeDtypeStruct((N_DEV,) + buf.shape, buf.dtype)], N_DEV - 1, start, forward, finish)


def _sum_devices(place, gathered, own):
    rows = own.shape[0]

    def body(place_ref, g_ref, own_ref, o_ref):
        acc = None
        for d in range(N_DEV):
            term = jnp.where(place_ref[0] == d, own_ref[...], g_ref[d])
            acc = term if acc is None else acc + term
        o_ref[...] = acc

    return pl.pallas_call(
        body, name="sum_devices", out_shape=jax.ShapeDtypeStruct((rows, LANES), F32),
        grid_spec=pltpu.PrefetchScalarGridSpec(
            num_scalar_prefetch=1, grid=(1,),
            in_specs=[pl.BlockSpec((N_DEV, rows, LANES), lambda i, place_ref: (0, 0, 0)),
                      pl.BlockSpec((rows, LANES), lambda i, place_ref: (0, 0))],
            out_specs=pl.BlockSpec((rows, LANES), lambda i, place_ref: (0, 0))),
        compiler_params=_params(("arbitrary",)),
    )(place, gathered, own)


def _no_copies(*refs):
    pass


def _no_exchange():
    return _Exchange([], [], 1, _no_copies, _no_copies, _no_copies)


class _NoComm:
    def gather_first(self):
        return _no_exchange()

    def first_landed(self, p, landed):
        pass

    def gather_rest(self, p):
        return _no_exchange()

    def weights_landed(self, p, landed):
        pass

    def gather_last(self):
        return _no_exchange()

    def last_landed(self, p, landed):
        pass

    def swap_first(self, g):
        return _no_exchange()

    def first_swapped(self, landed):
        pass

    def swap_second(self, g):
        return _no_exchange()

    def second_swapped(self, landed):
        pass

    def scatter_early(self, g):
        return _no_exchange()

    def scatter_landed(self, landed):
        pass

    def swap_reduced_early(self):
        return _no_exchange()

    def reduced_landed(self, landed):
        pass

    def scatter_late(self, g):
        return _no_exchange()

    def late_landed(self, landed):
        pass


def _local_step(x, mem, target, p, comm):
    h1, landed = _norm_fwd("norm_mix_pre", x, p["norm_mix_pre"], comm.gather_first())
    comm.first_landed(p, landed)
    qa, ka, va, u, z = _in_proj(h1, p["w_in"], p["bf_pad"])
    ycat, qab, landed = _fox_fwd(qa, ka, va, comm.gather_rest(p))
    comm.weights_landed(p, landed)
    ycat = _pool_fwd(u, p["w_pool_bd"], p["pool_scale"], ycat)
    y1, x2, h2, qx = _proj_resid_norm("mix_out", ycat, p["w_mix_out"], x, p["norm_mix_post"], p["norm_xa_pre"], p["w_xq"])
    mem_n = _norm_fwd("norm_mem", mem, p["norm_mem"])
    kv = _mm(
        "xkv", mem_n, p["w_xkv"], pl.BlockSpec((MEM, D), lambda i, j, k: (0, 0)),
        pl.BlockSpec((None, D, 512), lambda i, j, k: (j, 0, 0)), jax.ShapeDtypeStruct((MEM, 2 * D), BF16),
        pl.BlockSpec((MEM, 512), lambda i, j, k: (0, j)), (1, N_CHIPS, 1), NN, (MEM, 512))
    xo = _xattn_fwd(qx, kv)
    y2, x3, h3 = _proj_resid_norm("xo", xo, p["w_xo"], x2, p["norm_xa_post"], p["norm_ffn_pre"])
    hid, landed = _mm(
        "up_proj", h3, p["w_up"], pl.BlockSpec((2048, D), lambda i, j, k: (i, 0)),
        pl.BlockSpec((None, D, 1024), lambda i, j, k: (j // 2, 0, j % 2)), jax.ShapeDtypeStruct((2, S, D_FF), F32),
        pl.BlockSpec((None, 2048, 1024), lambda i, j, k: (j // 4, i, j % 4)), (S // 2048, 8, 1), NN, (2048, 1024),
        comm.gather_last())
    comm.last_landed(p, landed)
    act = _convgate_fwd(hid, p["cwb"])

    g = {}
    dres, dy3, g["norm_ffn_post"], loss_cols = _down_loss_bwd(act, p["w_down"], x3, p["norm_ffn_post"], target)
    dact = _mm_nt("d_act", dy3, p["w_down"], F32, 2048, 1024)
    g["w_down"] = _mm_tn("dw_down", act, dy3, 1024, 512)
    dhid, dcwb = _convgate_bwd(hid, dact, p["cwb"])
    g["w_up"] = _mm(
        "dw_up", h3, dhid, pl.BlockSpec((S, D), lambda i, j, k: (0, 0)),
        pl.BlockSpec((None, S, 512), lambda i, j, k: (j // 8, 0, j % 8)), jax.ShapeDtypeStruct((N_CHIPS, D, 2048), F32),
        pl.BlockSpec((None, D, 512), lambda i, j, k: (j // 4, 0, j % 4)), (1, 16, 1), TN, (D, 512))
    dh3, landed = _d_h3(dhid, p["w_up"], comm.swap_first(g))
    comm.first_swapped(landed)
    dres, dy2, dxo, g["norm_ffn_pre"], g["norm_xa_post"] = _mid_bwd(
        "bwd_ffn_xa", dres, x3, p["norm_ffn_pre"], dh3, y2, p["norm_xa_post"], p["w_xo"])
    g["w_xo"] = _mm_tn("dw_xo", xo, dy2, 1024, 512)
    dqx, dkv = _xattn_bwd(qx, kv, dxo)
    dkv = dkv.astype(BF16)
    g["w_xq"] = _mm_tn("dw_xq", h2, dqx, 1024, 512)
    dmem_n = _mm(
        "d_mem", dkv, p["w_xkv"], pl.BlockSpec((MEM, 512), lambda i, j, k: (0, k)),
        pl.BlockSpec((None, D, 512), lambda i, j, k: (k, 0, 0)), jax.ShapeDtypeStruct((MEM, D), F32),
        pl.BlockSpec((MEM, D), lambda i, j, k: (0, 0)), (1, 1, N_CHIPS), NT, (MEM, D))
    g["w_xkv"] = _mm(
        "dw_xkv", mem_n, dkv, pl.BlockSpec((MEM, D), lambda i, j, k: (0, 0)),
        pl.BlockSpec((MEM, 512), lambda i, j, k: (0, j)), jax.ShapeDtypeStruct((N_CHIPS, D, 512), F32),
        pl.BlockSpec((None, D, 512), lambda i, j, k: (j, 0, 0)), (1, N_CHIPS, 1), TN, (D, 512))
    g["norm_mem"] = _gain_bwd("dg_mem", mem, p["norm_mem"], dmem_n)
    (dres, dy1, g["norm_xa_pre"], g["norm_mix_post"], dy_pool, doa), landed = _bwd_xa_mix(
        dqx, p["w_xq"], dres, x2, p["norm_xa_pre"], y1, p["norm_mix_post"], p["w_mix_out"], ycat, comm.swap_second(g))
    comm.second_swapped(landed)
    g["w_mix_out"] = _mm_tn("dw_mix_out", ycat, dy1, 1024, 512)
    dqa, dka, dva, landed = _fox_bwd(qab, doa, ka, va, comm.scatter_early(g))
    comm.scatter_landed(landed)
    du, g["w_pool_full"], g["pool_scale"] = _pool_bwd(u, dy_pool, p["w_pool_bd"], p["w_pool_bd_t"], p["pool_scale"])
    dproj, g["bf_pad"] = _fox_bwd_post(dqa, dka, dva, du, z, p["bf_pad"])
    g["w_in"], landed = _dw_in(dproj, h1, comm.swap_reduced_early())
    comm.reduced_landed(landed)
    grad_x, g["norm_mix_pre"], landed = _d_h1_first_bwd(dproj, p["w_in"], dres, x, p["norm_mix_pre"], comm.scatter_late(g))
    comm.late_landed(landed)
    g["cwb"] = dcwb
    return grad_x, g, loss_cols


BIG = ("w_in", "w_mix_out", "w_xq", "w_xkv", "w_xo", "w_up", "w_down")
ROW_SHARDED = ("w_mix_out", "w_xq", "w_xo", "w_down")
SMALL = ("norm_mix_pre", "norm_mix_post", "b_forget", "w_pool", "pool_scale", "norm_mem", "norm_xa_pre", "norm_xa_post",
         "norm_ffn_pre", "norm_ffn_post", "conv_b")
ORDER = ("norm_mix_pre", "norm_mix_post", "w_in", "b_forget", "w_pool", "pool_scale", "w_mix_out", "norm_mem", "norm_xa_pre",
         "norm_xa_post", "w_xq", "w_xkv", "w_xo", "norm_ffn_pre", "norm_ffn_post", "w_up", "conv_w", "conv_b", "w_down")
SLOT = SUBLANES * LANES


def _pack(parts):
    rows, offs, off = [], [], 0
    for a in parts:
        flat = a.reshape(-1).astype(F32)
        n = -(-flat.shape[0] // SLOT) * SLOT
        rows.append(jnp.pad(flat, (0, n - flat.shape[0])).reshape(n // LANES, LANES))
        offs.append(off)
        off += n // LANES
    return jnp.concatenate(rows, axis=0), offs


def _unpack(buf, off, like):
    n = like.size
    rows = -(-n // LANES)
    return buf[off:off + rows].reshape(-1)[:n].reshape(like.shape)


FIRST = ("w_in",)
REST = ("w_mix_out", "w_xq", "w_xkv", "w_xo", "w_up")
LAST = ("w_down",)


def _local_params(w):
    w_pool_bd = jnp.zeros((D_POOL, D_POOL), F32)
    for gi in range(4):
        w_pool_bd = w_pool_bd.at[64 * gi:64 * (gi + 1), 64 * gi:64 * (gi + 1)].set(w["w_pool"][0, gi])
    p = {n: w[n] for n in ("norm_mix_pre", "norm_mix_post", "norm_mem", "norm_xa_pre", "norm_xa_post", "norm_ffn_pre",
                           "norm_ffn_post")}
    p.update(
        bf_pad=jnp.pad(w["b_forget"], ((0, 0), (0, LANES - HEADS))),
        w_pool_bd=w_pool_bd.astype(BF16), w_pool_bd_t=w_pool_bd.T.astype(BF16), pool_scale=w["pool_scale"].reshape(1, D_POOL))
    return p


def _w_in_param(stacked):
    n, rows, cols = stacked.shape
    tr = PREP_TR

    def body(w_ref, o_ref):
        o_ref[...] = jnp.concatenate([w_ref[j] for j in range(n)] + [jnp.zeros((tr, D_IN_PAD - n * cols), BF16)], axis=1)

    return pl.pallas_call(
        body, name="w_in_whole", grid=(rows // tr,), in_specs=[pl.BlockSpec((n, tr, cols), lambda i: (0, i, 0))],
        out_specs=_row_spec(tr, D_IN_PAD), out_shape=jax.ShapeDtypeStruct((rows, D_IN_PAD), BF16),
        compiler_params=_params(("parallel",)),
    )(stacked)


def _rest_params(w, full, conv_w_full):
    cw2 = conv_w_full.reshape(3, 2, D_FF).transpose(1, 0, 2)
    cwb = jnp.concatenate([cw2, w["conv_b"].reshape(1, 2, D_FF).transpose(1, 0, 2), jnp.zeros((2, 4, D_FF), F32)], axis=1)
    return dict(w_mix_out=full["w_mix_out"].reshape(D, D), w_xq=full["w_xq"].reshape(D, D), w_xkv=full["w_xkv"],
                w_xo=full["w_xo"].reshape(D, D), w_up=full["w_up"], cwb=cwb)


def _whole_params(w, full, conv_w_full):
    p = _local_params(w)
    p.update(_rest_params(w, full, conv_w_full), w_in=_w_in_param(full["w_in"]), w_down=full["w_down"].reshape(D_FF, D))
    return p


def _halved(a):
    return a.reshape(a.shape[:-2] + (2, a.shape[-2] // 2, a.shape[-1]))


class _StepComm:
    def __init__(self, w, shard2d, conv_w, core_id, chip_id):
        self.w, self.shard2d, self.conv_w, self.core_id, self.chip_id = w, shard2d, conv_w, core_id, chip_id
        self.first, self.second = ("w_up", "w_down"), ("w_xq", "w_xkv", "w_xo")
        self.early = self.first + self.second
        self.late = ("w_in", "w_mix_out")

    def gather_first(self):
        return _all_gather_weights([_halved(self.shard2d[n].astype(BF16)) for n in FIRST], [])

    def first_landed(self, p, landed):
        p["w_in"] = _w_in_param(landed[0].reshape((N_CHIPS,) + self.shard2d["w_in"].shape))

    def gather_rest(self, p):
        return _all_gather_weights([_halved(self.shard2d[n].astype(BF16)) for n in REST], [self.conv_w.reshape(3, -1)])

    def weights_landed(self, p, landed):
        full = {n: a.reshape((N_CHIPS,) + self.shard2d[n].shape) for n, a in zip(REST, landed)}
        conv_w_full = jnp.transpose(landed[-1], (1, 0, 2)).reshape(3, 2 * D_FF)
        p.update(_rest_params(self.w, full, conv_w_full))

    def gather_last(self):
        return _all_gather_weights([_halved(self.shard2d[n].astype(BF16)) for n in LAST], [])

    def last_landed(self, p, landed):
        p["w_down"] = landed[0].reshape(D_FF, D)

    def _view(self, g, n):
        return _halved(g[n].reshape((N_CHIPS,) + self.shard2d[n].shape))

    def swap_first(self, g):
        return _swap_halves([self._view(g, n) for n in self.first])

    def first_swapped(self, landed):
        self.from_sibling = dict(zip(self.first, landed))

    def swap_second(self, g):
        return _swap_halves([self._view(g, n) for n in self.second])

    def second_swapped(self, landed):
        self.from_sibling.update(zip(self.second, landed))

    def scatter_early(self, g):
        self.partial = [_chip_sum("chip_sum_" + n, self.core_id, self._view(g, n), self.from_sibling[n]) for n in self.early]
        return _scatter_chips(self.partial)

    def scatter_landed(self, landed):
        self.received = list(landed)

    def swap_reduced_early(self):
        self.reduced = [_mesh_sum("mesh_sum_" + n, self.chip_id, r, own)
                        for n, r, own in zip(self.early, self.received, self.partial)]
        return _swap_reduced(self.reduced)

    def reduced_landed(self, landed):
        self.reduced_sibling = list(landed)

    def scatter_late(self, g):
        views = [g["w_in"], self._view(g, "w_mix_out")]
        from_sibling = _swap_halves(views).run("swap_halves_late")
        self.partial_late = [_chip_sum("chip_sum_" + n, self.core_id, view, other)
                             for n, view, other in zip(self.late, views, from_sibling)]
        return _scatter_chips(self.partial_late)

    def late_landed(self, landed):
        self.received_late = list(landed)


def kernel(x, mem, norm_mix_pre, norm_mix_post, w_in, b_forget, w_pool, pool_scale, w_mix_out, norm_mem, norm_xa_pre, norm_xa_post, w_xq, w_xkv, w_xo, norm_ffn_pre, norm_ffn_post, w_up, conv_w, conv_b, w_down, loss_target, m_norm_mix_pre, m_norm_mix_post, m_w_in, m_b_forget, m_w_pool, m_pool_scale, m_w_mix_out, m_norm_mem, m_norm_xa_pre, m_norm_xa_post, m_w_xq, m_w_xkv, m_w_xo, m_norm_ffn_pre, m_norm_ffn_post, m_w_up, m_conv_w, m_conv_b, m_w_down, v_norm_mix_pre, v_norm_mix_post, v_w_in, v_b_forget, v_w_pool, v_pool_scale, v_w_mix_out, v_norm_mem, v_norm_xa_pre, v_norm_xa_post, v_w_xq, v_w_xkv, v_w_xo, v_norm_ffn_pre, v_norm_ffn_post, v_w_up, v_conv_w, v_conv_b, v_w_down):
    w = dict(norm_mix_pre=norm_mix_pre, norm_mix_post=norm_mix_post, w_in=w_in, b_forget=b_forget, w_pool=w_pool,
             pool_scale=pool_scale, w_mix_out=w_mix_out, norm_mem=norm_mem, norm_xa_pre=norm_xa_pre, norm_xa_post=norm_xa_post,
             w_xq=w_xq, w_xkv=w_xkv, w_xo=w_xo, norm_ffn_pre=norm_ffn_pre, norm_ffn_post=norm_ffn_post, w_up=w_up,
             conv_w=conv_w, conv_b=conv_b, w_down=w_down)
    m = dict(norm_mix_pre=m_norm_mix_pre, norm_mix_post=m_norm_mix_post, w_in=m_w_in, b_forget=m_b_forget, w_pool=m_w_pool,
             pool_scale=m_pool_scale, w_mix_out=m_w_mix_out, norm_mem=m_norm_mem, norm_xa_pre=m_norm_xa_pre,
             norm_xa_post=m_norm_xa_post, w_xq=m_w_xq, w_xkv=m_w_xkv, w_xo=m_w_xo, norm_ffn_pre=m_norm_ffn_pre,
             norm_ffn_post=m_norm_ffn_post, w_up=m_w_up, conv_w=m_conv_w, conv_b=m_conv_b, w_down=m_w_down)
    v = dict(norm_mix_pre=v_norm_mix_pre, norm_mix_post=v_norm_mix_post, w_in=v_w_in, b_forget=v_b_forget, w_pool=v_w_pool,
             pool_scale=v_pool_scale, w_mix_out=v_w_mix_out, norm_mem=v_norm_mem, norm_xa_pre=v_norm_xa_pre,
             norm_xa_post=v_norm_xa_post, w_xq=v_w_xq, w_xkv=v_w_xkv, w_xo=v_w_xo, norm_ffn_pre=v_norm_ffn_pre,
             norm_ffn_post=v_norm_ffn_post, w_up=v_w_up, conv_w=v_conv_w, conv_b=v_conv_b, w_down=v_w_down)
    chip = 2 * lax.axis_index("x") + lax.axis_index("y")

    core_id = lax.axis_index("c").astype(jnp.int32).reshape(1)
    chip_id = chip.astype(jnp.int32).reshape(1)

    shard2d = {n: w[n][0] for n in BIG}
    p = _local_params(w)
    comm = _StepComm(w, shard2d, conv_w, core_id, chip_id)
    grad_x, g, loss_cols = _local_step(x[0], mem[0], loss_target[0], p, comm)

    reduced_late = [_mesh_sum("mesh_sum_" + n, chip_id, r, own)
                    for n, r, own in zip(comm.late, comm.received_late, comm.partial_late)]
    names = comm.late + comm.early
    reduced = reduced_late + comm.reduced
    reduced_sibling = list(_swap_reduced(reduced_late).run("swap_reduced_late")) + comm.reduced_sibling
    grads = {}

    gw_pool = jnp.stack([g["w_pool_full"][64 * gi:64 * (gi + 1), 64 * gi:64 * (gi + 1)] for gi in range(4)])
    dcwb = g["cwb"]
    g_conv_w = dcwb[:, 0:3, :].transpose(1, 0, 2).reshape(3, 2 * D_FF)
    g_conv_b = dcwb[:, 3, :].reshape(2 * D_FF)
    small_g = dict(norm_mix_pre=g["norm_mix_pre"], norm_mix_post=g["norm_mix_post"], b_forget=g["bf_pad"][:, :HEADS],
                   w_pool=gw_pool, pool_scale=g["pool_scale"], norm_mem=g["norm_mem"], norm_xa_pre=g["norm_xa_pre"],
                   norm_xa_post=g["norm_xa_post"], norm_ffn_pre=g["norm_ffn_pre"], norm_ffn_post=g["norm_ffn_post"],
                   conv_b=g_conv_b)
    local_buf, offs = _pack([small_g[n] for n in SMALL] + [g_conv_w, loss_cols])

    delta, new_m, new_v = {}, {}, {}
    for n, g_mine, g_sibling in zip(names, reduced, reduced_sibling):
        cols = shard2d[n].shape[1]
        if cols % LANES:
            outs = _adamw_halves_columns("adamw_" + n, core_id, jnp.transpose(w[n], (2, 0, 1)), g_mine[:cols, None, :],
                                         g_sibling[:cols, None, :], jnp.transpose(m[n], (2, 0, 1)), jnp.transpose(v[n], (2, 0, 1)))
            gn, d, nm, nv = (jnp.transpose(o, (1, 2, 0)) for o in outs)
        else:
            gn, d, nm, nv = (o[None] for o in _adamw_halves("adamw_" + n, core_id, shard2d[n], g_mine, g_sibling, m[n][0], v[n][0]))
        grads[n], delta[n], new_m[n], new_v[n] = gn, d, nm, nv
    place = (2 * chip + lax.axis_index("c")).astype(jnp.int32).reshape(1)
    buf = _sum_devices(place, _gather_small(local_buf).run("gather_small")[0], local_buf)
    for n, off in zip(SMALL, offs):
        grads[n] = _unpack(buf, off, w[n])
    g_conv_w = _unpack(buf, offs[len(SMALL)], g_conv_w)
    grads["conv_w"] = lax.dynamic_slice_in_dim(g_conv_w, chip * (2 * D_FF // N_CHIPS), 2 * D_FF // N_CHIPS, axis=1).reshape(conv_w.shape)
    loss = jnp.sum(_unpack(buf, offs[len(SMALL) + 1], loss_cols))
    small_names = SMALL + ("conv_w",)
    packed = [_pack([d[n] for n in small_names])[0] for d in (w, grads, m, v)]
    offs = _pack([w[n] for n in small_names])[1]
    d, nm, nv = _adamw("adamw_small", *packed)
    for n, off in zip(small_names, offs):
        delta[n], new_m[n], new_v[n] = _unpack(d, off, w[n]), _unpack(nm, off, w[n]), _unpack(nv, off, w[n])

    return (loss, grad_x[None], *[grads[n] for n in ORDER], *[delta[n] for n in ORDER], *[new_m[n] for n in ORDER],
            *[new_v[n] for n in ORDER])
```

```python
import functools

import jax
import jax.numpy as jnp
import numpy as np
from jax import lax
from jax.experimental import pallas as pl
from jax.experimental.pallas import tpu as pltpu

F32 = jnp.float32
BF16 = jnp.bfloat16
MESH = pl.DeviceIdType.MESH
ANY = pl.BlockSpec(memory_space=pl.ANY)
VMEM_SPEC = pl.BlockSpec(memory_space=pltpu.VMEM)

S = 4096
D = 1024
MEM = 256
D_POOL = 256
HEADS = 12
DH = 64
D_FOX = HEADS * DH
D_IN = D_POOL + 3 * D_FOX + HEADS
F_OFF = D_POOL + 3 * D_FOX
Q_OFF, K_OFF, V_OFF = D_POOL, D_POOL + D_FOX, D_POOL + 2 * D_FOX
XA_HEADS = 4
XA_DH = 256
D_FF = 4096
EPS = 1e-6
N_CHIPS = 4
ADAM_LR, ADAM_B1, ADAM_B2, ADAM_EPS, ADAM_WD, ADAM_STEP = 0.001, 0.9, 0.999, 1e-08, 0.01, 10

LANES = 128
SUBLANES = 8
D_IN_PAD = 21 * LANES
TR = 512
ROW_PIECES = 4
TILE_BYTES = 2 * 1024 * 1024
NEG = -1e30
VMEM_LIMIT = 52 * 1024 * 1024

NN = (((1,), (0,)), ((), ()))
NT = (((1,), (1,)), ((), ()))
TN = (((0,), (0,)), ((), ()))


def _dot(a, b, dims=NN):
    return lax.dot_general(a, b, dims, preferred_element_type=F32)


def _params(sem):
    return pltpu.CompilerParams(dimension_semantics=sem, vmem_limit_bytes=VMEM_LIMIT)


def _split3(x):
    hi = x.astype(BF16)
    r = x - hi.astype(F32)
    mid = r.astype(BF16)
    lo = (r - mid.astype(F32)).astype(BF16)
    return hi, mid, lo


def _split3_f32(x):
    hi = x.astype(BF16).astype(F32)
    r = x - hi
    mid = r.astype(BF16).astype(F32)
    return hi, mid, r - mid


def _lane_iota(shape):
    return lax.broadcasted_iota(jnp.int32, shape, len(shape) - 1)


def _row_iota(shape):
    return lax.broadcasted_iota(jnp.int32, shape, len(shape) - 2)


def _mm(name, a, b, a_spec, b_spec, out_shape, out_spec, grid, dims, acc_shape, ex=None):
    nk = grid[2]
    if ex is not None:
        return _mm_hosting(name, a, b, a_spec, b_spec, out_shape, out_spec, grid, dims, ex)

    def body(a_ref, b_ref, o_ref, *scr):
        p = _dot(a_ref[...], b_ref[...], dims)
        if nk == 1:
            o_ref[...] = p.astype(o_ref.dtype)
        else:
            acc = scr[0]
            k = pl.program_id(2)

            @pl.when(k == 0)
            def _():
                acc[...] = p

            @pl.when(k > 0)
            def _():
                acc[...] += p

            @pl.when(k == nk - 1)
            def _():
                o_ref[...] = acc[...].astype(o_ref.dtype)

    return pl.pallas_call(
        body, name=name, grid=grid, in_specs=[a_spec, b_spec], out_specs=out_spec, out_shape=out_shape,
        scratch_shapes=[pltpu.VMEM(acc_shape, F32)] if nk > 1 else [],
        compiler_params=_params(("parallel", "parallel", "arbitrary")),
    )(a, b)


def _mm_hosting(name, a, b, a_spec, b_spec, out_shape, out_spec, grid, dims, ex):
    assert grid[2] == 1
    n = len(ex.ins)

    def body(*refs):
        i, j = pl.program_id(0), pl.program_id(1)
        last = (i == grid[0] - 1) & (j == grid[1] - 1)
        (a_ref, b_ref), (o_ref,), _, begin, end = _hosted(ex, refs, 2, 1, (i == 0) & (j == 0), last, last)
        begin()
        o_ref[...] = _dot(a_ref[...], b_ref[...], dims).astype(o_ref.dtype)
        end()

    res = pl.pallas_call(
        body, name=name, grid=grid, in_specs=[a_spec, b_spec] + [ANY] * n, out_specs=[out_spec] + [ANY] * n,
        out_shape=[out_shape] + ex.out_shapes, scratch_shapes=ex.scratch(),
        compiler_params=_params(("arbitrary", "arbitrary", "arbitrary")),
    )(a, b, *ex.ins)
    return res[0], res[1:]


def _mm_nn(name, a, b, out_dtype, tm, tn):
    m, k = a.shape
    n = b.shape[1]
    return _mm(name, a, b, pl.BlockSpec((tm, k), lambda i, j, kk: (i, 0)), pl.BlockSpec((k, tn), lambda i, j, kk: (0, j)),
               jax.ShapeDtypeStruct((m, n), out_dtype), pl.BlockSpec((tm, tn), lambda i, j, kk: (i, j)),
               (m // tm, n // tn, 1), NN, (tm, tn))


def _mm_nt(name, a, b, out_dtype, tm, tn, ex=None):
    m, k = a.shape
    n = b.shape[0]
    return _mm(name, a, b, pl.BlockSpec((tm, k), lambda i, j, kk: (i, 0)), pl.BlockSpec((tn, k), lambda i, j, kk: (j, 0)),
               jax.ShapeDtypeStruct((m, n), out_dtype), pl.BlockSpec((tm, tn), lambda i, j, kk: (i, j)),
               (m // tm, n // tn, 1), NT, (tm, tn), ex)


def _mm_tn(name, a, b, tka, tn, ex=None):
    t, ka = a.shape
    n = b.shape[1]
    return _mm(name, a, b, pl.BlockSpec((t, tka), lambda i, j, kk: (0, i)), pl.BlockSpec((t, tn), lambda i, j, kk: (0, j)),
               jax.ShapeDtypeStruct((ka, n), F32), pl.BlockSpec((tka, tn), lambda i, j, kk: (i, j)),
               (ka // tka, n // tn, 1), TN, (tka, tn), ex)


def _d_h3(dhid, w_up, ex):
    tm = tn = 1024
    shard = 2 * D_FF // N_CHIPS
    per_plane = D_FF // shard
    grid = (S // tm, D // tn, N_CHIPS)
    n = len(ex.ins)

    def body(*refs):
        i, j, k = pl.program_id(0), pl.program_id(1), pl.program_id(2)
        first = (i == 0) & (j == 0) & (k == 0)
        last = (i == grid[0] - 1) & (j == grid[1] - 1) & (k == N_CHIPS - 1)
        (a_ref, b_ref), (o_ref,), (acc_ref,), begin, end = _hosted(ex, refs, 2, 1, first, first, last)
        begin()
        part = _dot(a_ref[...], b_ref[...], NT)

        @pl.when(k == 0)
        def _():
            acc_ref[...] = part

        @pl.when(k > 0)
        def _():
            acc_ref[...] += part

        @pl.when(k == N_CHIPS - 1)
        def _():
            o_ref[...] = acc_ref[...]

        end()

    res = pl.pallas_call(
        body, name="d_h3", grid=grid,
        in_specs=[pl.BlockSpec((None, tm, shard), lambda i, j, k: (k // per_plane, i, k % per_plane)),
                  pl.BlockSpec((None, tn, shard), lambda i, j, k: (k, j, 0))] + [ANY] * n,
        out_specs=[pl.BlockSpec((tm, tn), lambda i, j, k: (i, j))] + [ANY] * n,
        out_shape=[jax.ShapeDtypeStruct((S, D), F32)] + ex.out_shapes,
        scratch_shapes=[pltpu.VMEM((tm, tn), F32)] + ex.scratch(),
        compiler_params=_params(("arbitrary", "arbitrary", "arbitrary")),
    )(dhid, w_up, *ex.ins)
    return res[0], res[1:]


SHARD_IN = D_IN // N_CHIPS
SHARD_IN_PAD = -(-SHARD_IN // SUBLANES) * SUBLANES


def _dw_in(dproj, h1, ex):
    tk = 1024
    nk = S // tk
    half = D // 2
    starts = [SHARD_IN * j // LANES * LANES for j in range(N_CHIPS)]
    shifts = [SHARD_IN * j - s for j, s in enumerate(starts)]
    window = -(-(max(shifts) + SHARD_IN) // LANES) * LANES
    assert starts[-1] + window <= dproj.shape[1]
    n = len(ex.ins)

    def body(*refs):
        k = pl.program_id(0)
        (a_ref, b_ref), (o_ref,), _, begin, end = _hosted(ex, refs, 2, 1, k == 0, k == nk - 1, k == nk - 1)
        begin()

        @pl.when(k == 0)
        def _():
            o_ref[...] = jnp.zeros(o_ref.shape, F32)

        for j in range(N_CHIPS):
            win = a_ref[:, starts[j]:starts[j] + window]
            if shifts[j]:
                win = pltpu.roll(win, window - shifts[j], axis=1)
            part = _dot(win, b_ref[...], TN)
            for h in range(2):
                o_ref[j, h] += part[:SHARD_IN_PAD, h * half:(h + 1) * half]
        end()

    out_shape = (N_CHIPS, 2, SHARD_IN_PAD, half)
    res = pl.pallas_call(
        body, name="dw_in", grid=(nk,),
        in_specs=[pl.BlockSpec((tk, dproj.shape[1]), lambda k: (k, 0)), pl.BlockSpec((tk, D), lambda k: (k, 0))] + [ANY] * n,
        out_specs=[pl.BlockSpec(out_shape, lambda k: (0, 0, 0, 0))] + [ANY] * n,
        out_shape=[jax.ShapeDtypeStruct(out_shape, F32)] + ex.out_shapes,
        scratch_shapes=ex.scratch(),
        compiler_params=_params(("arbitrary",)),
    )(dproj, h1, *ex.ins)
    return res[0], res[1:]


def _rms(x, g):
    r = lax.rsqrt(jnp.mean(x * x, axis=-1, keepdims=True) + EPS)
    return x * r * g


def _rms_bwd(x, g, dy):
    r = lax.rsqrt(jnp.mean(x * x, axis=-1, keepdims=True) + EPS)
    xh = x * r
    dxh = dy * g
    dx = r * (dxh - xh * jnp.mean(dxh * xh, axis=-1, keepdims=True))
    return dx, jnp.sum(dy * xh, axis=0, keepdims=True)


def _row_spec(tr, width):
    return pl.BlockSpec((tr, width), lambda i: (i, 0))


def _vec_spec(width):
    return pl.BlockSpec((1, width), lambda i: (0, 0))


def _norm_fwd(name, x, g, ex=None):
    rows, width = x.shape
    tr = min(TR, rows)
    steps = rows // tr
    hosted = ex if ex is not None else _no_exchange()
    n = len(hosted.ins)

    def body(*refs):
        i = pl.program_id(0)
        (x_ref, g_ref), (h_ref,), _, begin, end = _hosted(hosted, refs, 2, 1, i == 0, i == steps - 1, i == steps - 1)
        begin()
        h_ref[...] = _rms(x_ref[...], g_ref[...]).astype(BF16)
        end()

    res = pl.pallas_call(
        body, name=name, grid=(steps,), in_specs=[_row_spec(tr, width), _vec_spec(width)] + [ANY] * n,
        out_specs=[_row_spec(tr, width)] + [ANY] * n,
        out_shape=[jax.ShapeDtypeStruct((rows, width), BF16)] + hosted.out_shapes, scratch_shapes=hosted.scratch(),
        compiler_params=_params(("arbitrary",)),
    )(x, g, *hosted.ins)
    return res[0] if ex is None else (res[0], res[1:])


def _proj_resid_norm(name, a, w, xp, g_post, g_pre, w_next=None):
    def body(a_ref, w_ref, xp_ref, gpost_ref, gpre_ref, *rest):
        y_ref, xn_ref, h_ref = rest[-3:] if w_next is None else rest[1:4]
        y = _dot(a_ref[...], w_ref[...])
        y_ref[...] = y
        xn = xp_ref[...] + _rms(y, gpost_ref[...])
        xn_ref[...] = xn
        h = _rms(xn, gpre_ref[...]).astype(BF16)
        h_ref[...] = h
        if w_next is not None:
            rest[4][...] = _dot(h, rest[0][...]).astype(BF16)

    mat = pl.BlockSpec((D, D), lambda i: (0, 0))
    more = [] if w_next is None else [w_next]
    return pl.pallas_call(
        body, name=name, grid=(S // TR,),
        in_specs=[_row_spec(TR, D), mat, _row_spec(TR, D), _vec_spec(D), _vec_spec(D)] + [mat] * len(more),
        out_specs=[_row_spec(TR, D)] * (3 + len(more)),
        out_shape=[jax.ShapeDtypeStruct((S, D), F32), jax.ShapeDtypeStruct((S, D), F32), jax.ShapeDtypeStruct((S, D), BF16)]
        + [jax.ShapeDtypeStruct((S, D), BF16)] * len(more),
        compiler_params=_params(("parallel",)),
    )(a, w, xp, g_post, g_pre, *more)


def _down_loss_bwd(act, w_down, x3, g_post, target):
    def body(a_ref, w_ref, x_ref, g_ref, t_ref, dres_ref, dy_ref, dg_ref, loss_ref):
        i = pl.program_id(0)

        @pl.when(i == 0)
        def _():
            dg_ref[...] = jnp.zeros_like(dg_ref)
            loss_ref[...] = jnp.zeros_like(loss_ref)

        g = g_ref[...]
        for r in range(ROW_PIECES):
            rows = slice(r * TR // ROW_PIECES, (r + 1) * TR // ROW_PIECES)
            y = _dot(a_ref[rows, :], w_ref[...])
            e = x_ref[rows, :] + _rms(y, g) - t_ref[rows, :]
            loss_ref[...] += jnp.sum(e * e, axis=0, keepdims=True) * (0.5 / D)
            dres = e * (1.0 / D)
            dres_ref[rows, :] = dres
            dy, dg = _rms_bwd(y, g, dres)
            dy_ref[rows, :] = dy.astype(BF16)
            dg_ref[...] += dg

    return pl.pallas_call(
        body, name="down_loss_bwd", grid=(S // TR,),
        in_specs=[_row_spec(TR, D_FF), pl.BlockSpec((D_FF, D), lambda i: (0, 0)), _row_spec(TR, D), _vec_spec(D),
                  _row_spec(TR, D)],
        out_specs=[_row_spec(TR, D), _row_spec(TR, D), _vec_spec(D), _vec_spec(D)],
        out_shape=[jax.ShapeDtypeStruct((S, D), F32), jax.ShapeDtypeStruct((S, D), BF16),
                   jax.ShapeDtypeStruct((1, D), F32), jax.ShapeDtypeStruct((1, D), F32)],
        compiler_params=_params(("arbitrary",)),
    )(act, w_down, x3, g_post, target)


def _mid_bwd(name, dres, xcur, g_pre, dh, yprev, g_post, w):
    def body(dres_ref, x_ref, gpre_ref, dh_ref, y_ref, gpost_ref, w_ref, dx_ref, dy_ref, da_ref, dgpre_ref, dgpost_ref):
        i = pl.program_id(0)

        @pl.when(i == 0)
        def _():
            dgpre_ref[...] = jnp.zeros_like(dgpre_ref)
            dgpost_ref[...] = jnp.zeros_like(dgpost_ref)

        dxn, dgpre = _rms_bwd(x_ref[...], gpre_ref[...], dh_ref[...])
        dx = dres_ref[...] + dxn
        dx_ref[...] = dx
        dy, dgpost = _rms_bwd(y_ref[...], gpost_ref[...], dx)
        dy = dy.astype(BF16)
        dy_ref[...] = dy
        da_ref[...] = _dot(dy, w_ref[...], NT).astype(BF16)
        dgpre_ref[...] += dgpre
        dgpost_ref[...] += dgpost

    return pl.pallas_call(
        body, name=name, grid=(S // TR,),
        in_specs=[_row_spec(TR, D), _row_spec(TR, D), _vec_spec(D), _row_spec(TR, D), _row_spec(TR, D), _vec_spec(D),
                  pl.BlockSpec((D, D), lambda i: (0, 0))],
        out_specs=[_row_spec(TR, D), _row_spec(TR, D), _row_spec(TR, D), _vec_spec(D), _vec_spec(D)],
        out_shape=[jax.ShapeDtypeStruct((S, D), F32), jax.ShapeDtypeStruct((S, D), BF16), jax.ShapeDtypeStruct((S, D), BF16),
                   jax.ShapeDtypeStruct((1, D), F32), jax.ShapeDtypeStruct((1, D), F32)],
        compiler_params=_params(("arbitrary",)),
    )(dres, xcur, g_pre, dh, yprev, g_post, w)


def _d_h1_first_bwd(dproj, w_in, dres, x, g, ex):
    nt = S // TR
    n = len(ex.ins)

    def body(*refs):
        i = pl.program_id(0)
        (dp_ref, w_ref, dres_ref, x_ref, g_ref), (dx_ref, dg_ref), _, begin, end = _hosted(
            ex, refs, 5, 2, i == 0, i == nt - 1, i == nt - 1)
        begin()

        @pl.when(i == 0)
        def _():
            dg_ref[...] = jnp.zeros_like(dg_ref)

        dxn, dg = _rms_bwd(x_ref[...], g_ref[...], _dot(dp_ref[...], w_ref[...], NT))
        dx_ref[...] = dres_ref[...] + dxn
        dg_ref[...] += dg
        end()

    res = pl.pallas_call(
        body, name="d_h1", grid=(nt,),
        in_specs=[_row_spec(TR, D_IN_PAD), pl.BlockSpec((D, D_IN_PAD), lambda i: (0, 0)), _row_spec(TR, D), _row_spec(TR, D),
                  _vec_spec(D)] + [ANY] * n,
        out_specs=[_row_spec(TR, D), _vec_spec(D)] + [ANY] * n,
        out_shape=[jax.ShapeDtypeStruct((S, D), F32), jax.ShapeDtypeStruct((1, D), F32)] + ex.out_shapes,
        scratch_shapes=ex.scratch(), compiler_params=_params(("arbitrary",)),
    )(dproj, w_in, dres, x, g, *ex.ins)
    return res[0], res[1], res[2:]


def _gain_bwd(name, x, g, dy):
    rows, width = x.shape

    def body(x_ref, g_ref, dy_ref, dg_ref):
        _, dg = _rms_bwd(x_ref[...], g_ref[...], dy_ref[...])
        dg_ref[...] = dg

    return pl.pallas_call(
        body, name=name, grid=(1,), in_specs=[_row_spec(rows, width), _vec_spec(width), _row_spec(rows, width)],
        out_specs=_vec_spec(width), out_shape=jax.ShapeDtypeStruct((1, width), F32),
        compiler_params=_params(("arbitrary",)),
    )(x, g, dy)


CUM_Q = DH
CUM_K = DH + 3
LSE_Q = DH + 6
BOTH_ONE = DH + 9
DEN_V = DH
DELTA = DH + 1
PREP_TR = 256
PIECE_LANES = 16
FOX_FWD_BLOCK = 1024
FOX_BWD_BLOCK = 512


def _at(lane_of_even_head, h):
    return (lane_of_even_head + DH * (h % 2)) % LANES


def _data_lanes(lane, h):
    return lane >= DH if h % 2 else lane < DH


def _pair_block(ref, off, h):
    base = ((off + DH * h) // LANES) * LANES
    return ref[:, base:base + LANES]


def _cumsum_rows(x, tri, carry):
    hi, mid, lo = _split3(x)
    return _dot(tri, hi) + _dot(tri, mid) + _dot(tri, lo) + carry


def _in_proj(h1, w_in, bf_pad):
    tr = TR

    place_q = np.zeros((LANES, HEADS * LANES), np.float32)
    place_k = np.zeros((LANES, HEADS * LANES), np.float32)
    for h in range(HEADS):
        for piece in range(3):
            place_q[PIECE_LANES * piece + h, LANES * h + _at(CUM_Q, h) + piece] = 1.0
            place_k[PIECE_LANES * piece + h, LANES * h + _at(CUM_K, h) + piece] = -1.0

    def body(h_ref, w_ref, bf_ref, pq_ref, pk_ref, qa_ref, ka_ref, va_ref, u_ref, z_ref, carry_ref):
        i = pl.program_id(0)

        @pl.when(i == 0)
        def _():
            carry_ref[...] = jnp.zeros_like(carry_ref)

        proj = _dot(h_ref[...], w_ref[...])
        u_ref[...] = proj[:, :D_POOL]
        z_ref[...] = proj[:, F_OFF:F_OFF + LANES]
        lane = _lane_iota((tr, LANES))
        z = proj[:, F_OFF:F_OFF + LANES] + bf_ref[...]
        log_f = jnp.minimum(z, 0.0) - jnp.log(1.0 + jnp.exp(-jnp.abs(z)))
        log_f = jnp.where(lane < HEADS, log_f, 0.0)
        tri = jnp.where(_row_iota((tr, tr)) >= _lane_iota((tr, tr)), 1.0, 0.0).astype(BF16)
        cum = _cumsum_rows(log_f, tri, carry_ref[0:1, :])
        carry_ref[0:1, :] = cum[tr - 1:tr, :]
        c_hi, c_mid, c_lo = _split3_f32(cum)
        pieces = (c_hi + pltpu.roll(c_mid, PIECE_LANES, 1) + pltpu.roll(c_lo, 2 * PIECE_LANES, 1)).astype(BF16)
        cum_q = _dot(pieces, pq_ref[...])
        cum_k = _dot(pieces, pk_ref[...])

        def between(first, h):
            return (lane >= _at(first, h)) & (lane < _at(first, h) + 3)

        ones_q = [jnp.where(between(CUM_K, h) | (lane == _at(BOTH_ONE, h)), 1.0, 0.0) for h in range(2)]
        ones_k = [jnp.where(between(CUM_Q, h) | between(LSE_Q, h) | (lane == _at(BOTH_ONE, h)), 1.0, 0.0) for h in range(2)]
        aug_v = [jnp.where(lane == _at(DEN_V, h), 1.0, jnp.where(between(DELTA, h), -1.0, 0.0)) for h in range(2)]
        for h in range(HEADS):
            mine = slice(LANES * h, LANES * (h + 1))
            data = _data_lanes(lane, h)
            qa_ref[h] = jnp.where(data, _pair_block(proj, Q_OFF, h) * (DH ** -0.5), cum_q[:, mine] + ones_q[h % 2]).astype(BF16)
            ka_ref[h] = jnp.where(data, _pair_block(proj, K_OFF, h), cum_k[:, mine] + ones_k[h % 2]).astype(BF16)
            va_ref[h] = jnp.where(data, _pair_block(proj, V_OFF, h), aug_v[h % 2]).astype(BF16)

    head_spec = pl.BlockSpec((HEADS, tr, LANES), lambda i: (0, i, 0))
    head_shape = jax.ShapeDtypeStruct((HEADS, S, LANES), BF16)
    place_spec = pl.BlockSpec(place_q.shape, lambda i: (0, 0))
    return pl.pallas_call(
        body, name="in_proj", grid=(S // tr,),
        in_specs=[_row_spec(tr, D), pl.BlockSpec((D, D_IN_PAD), lambda i: (0, 0)), _vec_spec(LANES), place_spec, place_spec],
        out_specs=[head_spec] * 3 + [_row_spec(tr, D_POOL), _row_spec(tr, LANES)],
        out_shape=[head_shape] * 3 + [jax.ShapeDtypeStruct((S, D_POOL), F32), jax.ShapeDtypeStruct((S, LANES), F32)],
        scratch_shapes=[pltpu.VMEM((SUBLANES, LANES), F32)], compiler_params=_params(("arbitrary",)),
    )(h1, w_in, bf_pad, jnp.asarray(place_q, BF16), jnp.asarray(place_k, BF16))


def _hosted(ex, refs, n_blocked_in, n_blocked_out, first, forward_at, last):
    n = len(ex.ins)
    own_in = refs[:n_blocked_in]
    ex_in = refs[n_blocked_in:n_blocked_in + n]
    own_out = refs[n_blocked_in + n:n_blocked_in + n + n_blocked_out]
    ex_out = refs[n_blocked_in + n + n_blocked_out:n_blocked_in + 2 * n + n_blocked_out]
    rest = refs[n_blocked_in + 2 * n + n_blocked_out:]
    args = (ex_in, ex_out, rest[-2], rest[-1])

    def begin():
        @pl.when(first)
        def _():
            ex.start(*args)

        @pl.when(forward_at)
        def _():
            ex.forward(*args)

    def end():
        @pl.when(last)
        def _():
            ex.finish(*args)

    return own_in, own_out, rest[:-2], begin, end


def _fox_fwd(qa, ka, va, ex):
    BQ = BK = FOX_FWD_BLOCK
    nq = S // BQ
    n_pairs = HEADS // 2

    def body(*refs):
        p_id, i = pl.program_id(0), pl.program_id(1)
        (qa_ref, ka_ref, va_ref), (y_ref, qab_ref), (m_scr, acc_scr), begin, end = _hosted(
            ex, refs, 3, 2, (p_id == 0) & (i == 0), (p_id == n_pairs - 1) & (i == 0), (p_id == n_pairs - 1) & (i == nq - 1))
        begin()
        lane = _lane_iota((BQ, LANES))
        causal = _row_iota((BQ, BK)) >= _lane_iota((BQ, BK))
        m_scr[...] = jnp.full_like(m_scr, NEG)
        acc_scr[...] = jnp.zeros_like(acc_scr)

        def step(j, masked):
            rows = pl.ds(pl.multiple_of(j * BK, BK), BK)
            for hh in range(2):
                s = _dot(qa_ref[hh], ka_ref[hh, rows, :], NT)
                if masked:
                    s = jnp.where(causal, s, NEG)
                m_prev = m_scr[hh]
                m_new = jnp.maximum(m_prev, jnp.max(s, axis=1, keepdims=True))
                p = jnp.exp(s - jnp.tile(m_new, (1, BK // LANES)))
                acc_scr[hh] = jnp.exp(m_prev - m_new) * acc_scr[hh] + _dot(p.astype(BF16), va_ref[hh, rows, :])
                m_scr[hh] = m_new

        def full_step(j, carry):
            step(j, False)
            return carry

        lax.fori_loop(0, i, full_step, 0)
        step(i, True)
        outs = []
        for hh in range(2):
            acc = acc_scr[hh]
            den_lane, lse_lane = _at(DEN_V, hh), _at(LSE_Q, hh)
            den = jnp.broadcast_to(acc[:, den_lane:den_lane + 1], (BQ, LANES))
            outs.append(acc * (1.0 / den))
            n_hi, n_mid, n_lo = _split3(-(m_scr[hh] + jnp.log(den)))
            qab_ref[hh] = jnp.where(lane == lse_lane, n_hi,
                                    jnp.where(lane == lse_lane + 1, n_mid, jnp.where(lane == lse_lane + 2, n_lo, qa_ref[hh])))
        y_ref[...] = jnp.where(lane < DH, outs[0], outs[1]).astype(BF16)
        end()

    pair_rows = pl.BlockSpec((2, BQ, LANES), lambda p, i: (p, i, 0))
    pair_all = pl.BlockSpec((2, S, LANES), lambda p, i: (p, 0, 0))
    n = len(ex.ins)
    res = pl.pallas_call(
        body, name="fox_fwd", grid=(n_pairs, nq), in_specs=[pair_rows, pair_all, pair_all] + [ANY] * n,
        out_specs=[pl.BlockSpec((BQ, LANES), lambda p, i: (i, D_POOL // LANES + p)), pair_rows] + [ANY] * n,
        out_shape=[jax.ShapeDtypeStruct((S, D), BF16), jax.ShapeDtypeStruct((HEADS, S, LANES), BF16)] + ex.out_shapes,
        scratch_shapes=[pltpu.VMEM((2, BQ, LANES), F32), pltpu.VMEM((2, BQ, LANES), F32)] + ex.scratch(),
        compiler_params=_params(("arbitrary", "arbitrary")),
    )(qa, ka, va, *ex.ins)
    return res[0], res[1], res[2:]


def _bwd_xa_mix(dqx, w_xq, dres, x2, g_pre, y1, g_post, w_mix_out, ycat, ex):
    steps = S // TR
    n = len(ex.ins)

    def body(*refs):
        i = pl.program_id(0)
        ((dq_ref, wq_ref, dres_ref, x_ref, gpre_ref, y_ref, gpost_ref, wm_ref, ycat_ref),
         (dx_ref, dy_ref, dgpre_ref, dgpost_ref, dp_ref, doa_ref), _, begin, end) = _hosted(
            ex, refs, 9, 6, i == 0, i == 0, i == steps - 1)
        begin()

        @pl.when(i == 0)
        def _():
            dgpre_ref[...] = jnp.zeros_like(dgpre_ref)
            dgpost_ref[...] = jnp.zeros_like(dgpost_ref)

        dxn, dgpre = _rms_bwd(x_ref[...], gpre_ref[...], _dot(dq_ref[...], wq_ref[...], NT))
        dx = dres_ref[...] + dxn
        dx_ref[...] = dx
        dy, dgpost = _rms_bwd(y_ref[...], gpost_ref[...], dx)
        dy = dy.astype(BF16)
        dy_ref[...] = dy
        dgpre_ref[...] += dgpre
        dgpost_ref[...] += dgpost

        d = _dot(dy, wm_ref[...], NT)
        dp_ref[...] = d[:, :D_POOL]
        lane = _lane_iota((TR, LANES))
        low = lane < DH
        for p in range(HEADS // 2):
            cols = slice(D_POOL + LANES * p, D_POOL + LANES * (p + 1))
            do = d[:, cols]
            prod = do * ycat_ref[:, cols].astype(F32)
            deltas = (jnp.sum(jnp.where(low, prod, 0.0), axis=1, keepdims=True),
                      jnp.sum(jnp.where(low, 0.0, prod), axis=1, keepdims=True))
            for hh in range(2):
                d_hi, d_mid, d_lo = _split3_f32(deltas[hh])
                dl = _at(DELTA, hh)
                aug = jnp.where(lane == dl, d_hi, jnp.where(lane == dl + 1, d_mid, jnp.where(lane == dl + 2, d_lo, 0.0)))
                doa_ref[2 * p + hh] = jnp.where(_data_lanes(lane, hh), do, aug).astype(BF16)
        end()

    mat = pl.BlockSpec((D, D), lambda i: (0, 0))
    res = pl.pallas_call(
        body, name="bwd_xa_mix", grid=(steps,),
        in_specs=[_row_spec(TR, D), mat, _row_spec(TR, D), _row_spec(TR, D), _vec_spec(D), _row_spec(TR, D), _vec_spec(D), mat,
                  _row_spec(TR, D)] + [ANY] * n,
        out_specs=[_row_spec(TR, D), _row_spec(TR, D), _vec_spec(D), _vec_spec(D), _row_spec(TR, D_POOL),
                   pl.BlockSpec((HEADS, TR, LANES), lambda i: (0, i, 0))] + [ANY] * n,
        out_shape=[jax.ShapeDtypeStruct((S, D), F32), jax.ShapeDtypeStruct((S, D), BF16), jax.ShapeDtypeStruct((1, D), F32),
                   jax.ShapeDtypeStruct((1, D), F32), jax.ShapeDtypeStruct((S, D_POOL), F32),
                   jax.ShapeDtypeStruct((HEADS, S, LANES), BF16)] + ex.out_shapes,
        scratch_shapes=ex.scratch(), compiler_params=_params(("arbitrary",)),
    )(dqx, w_xq, dres, x2, g_pre, y1, g_post, w_mix_out, ycat, *ex.ins)
    return res[:6], res[6:]


def _fox_bwd(qab, doa, ka, va, ex):
    BQ = BK = FOX_BWD_BLOCK
    nk = S // BK
    n_pairs = HEADS // 2

    def body(*refs):
        p_id, j = pl.program_id(0), pl.program_id(1)
        (qab_ref, doa_ref, ka_ref, va_ref), (dqa_ref, dka_ref, dva_ref), (dv_ref,), begin, end = _hosted(
            ex, refs, 4, 3, (p_id == 0) & (j == 0), (p_id == n_pairs - 1) & (j == 0), (p_id == n_pairs - 1) & (j == nk - 1))
        begin()

        @pl.when(j == 0)
        def _():
            dqa_ref[...] = jnp.zeros_like(dqa_ref)

        causal = _row_iota((BQ, BK)) >= _lane_iota((BQ, BK))
        dka_ref[...] = jnp.zeros_like(dka_ref)
        dv_ref[...] = jnp.zeros_like(dv_ref)

        def step(i, masked):
            rows = pl.ds(pl.multiple_of(i * BQ, BQ), BQ)
            for hh in range(2):
                kb = ka_ref[hh]
                q = qab_ref[hh, rows, :]
                do = doa_ref[hh, rows, :]
                s = _dot(q, kb, NT)
                if masked:
                    s = jnp.where(causal, s, NEG)
                p = jnp.exp(s)
                ds = p * _dot(do, va_ref[hh], NT)
                pb = p.astype(BF16)
                dsb = ds.astype(BF16)
                dv_ref[hh] += _dot(pb, do, TN)
                dka_ref[hh] += _dot(dsb, q, TN)
                dqa_ref[hh, rows, :] += _dot(dsb, kb)

        def full_step(i, carry):
            step(i, False)
            return carry

        step(j, True)
        lax.fori_loop(j + 1, nk, full_step, 0)
        dva_ref[...] = dv_ref[...].astype(BF16)
        end()

    pair_all = pl.BlockSpec((2, S, LANES), lambda p, j: (p, 0, 0))
    pair_rows = pl.BlockSpec((2, BK, LANES), lambda p, j: (p, j, 0))
    shape = jax.ShapeDtypeStruct((HEADS, S, LANES), F32)
    n = len(ex.ins)
    res = pl.pallas_call(
        body, name="fox_bwd", grid=(n_pairs, nk), in_specs=[pair_all, pair_all, pair_rows, pair_rows] + [ANY] * n,
        out_specs=[pair_all, pair_rows, pair_rows] + [ANY] * n,
        out_shape=[shape, shape, jax.ShapeDtypeStruct((HEADS, S, LANES), BF16)] + ex.out_shapes,
        scratch_shapes=[pltpu.VMEM((2, BK, LANES), F32)] + ex.scratch(), compiler_params=_params(("arbitrary", "arbitrary")),
    )(qab, doa, ka, va, *ex.ins)
    return res[0], res[1], res[2], res[3:]


def _fox_bwd_post(dqa, dka, dva, du, proj, bf_pad):
    tr = PREP_TR
    nt = S // tr

    pick = np.zeros((HEADS * LANES, LANES), np.float32)
    for h in range(HEADS):
        pick[LANES * h + _at(BOTH_ONE, h), h] = 1.0

    def body(dqa_ref, dka_ref, dva_ref, du_ref, z_ref, bf_ref, pick_ref, dp_ref, dbf_ref, carry_ref):
        i = pl.program_id(0)

        @pl.when(i == 0)
        def _():
            carry_ref[...] = jnp.zeros_like(carry_ref)
            dbf_ref[...] = jnp.zeros_like(dbf_ref)

        lane = _lane_iota((tr, LANES))
        diff = jnp.concatenate([dqa_ref[h] - dka_ref[h] for h in range(HEADS)], axis=1)
        hi = diff.astype(BF16)
        dcum = _dot(hi, pick_ref[...]) + _dot((diff - hi.astype(F32)).astype(BF16), pick_ref[...])
        tri =jnp.where(_lane_iota((tr, tr)) >= _row_iota((tr, tr)), 1.0, 0.0).astype(BF16)
        dlog_f = _cumsum_rows(dcum, tri, carry_ref[0:1, :])
        carry_ref[0:1, :] = dlog_f[0:1, :]
        z = z_ref[...] + bf_ref[...]
        df = jnp.where(lane < HEADS, dlog_f / (1.0 + jnp.exp(z)), 0.0)
        dbf_ref[...] += jnp.sum(df, axis=0, keepdims=True)

        dp_ref[:, 0:D_POOL] = du_ref[...].astype(BF16)
        low = lane < DH
        for ref, off, scale in ((dqa_ref, Q_OFF, DH ** -0.5), (dka_ref, K_OFF, 1.0), (dva_ref, V_OFF, 1.0)):
            for p in range(HEADS // 2):
                blk = jnp.where(low, ref[2 * p], ref[2 * p + 1])
                dp_ref[:, off + LANES * p:off + LANES * (p + 1)] = (blk * scale).astype(BF16)
        dp_ref[:, F_OFF:F_OFF + LANES] = df.astype(BF16)

    head_spec = pl.BlockSpec((HEADS, tr, LANES), lambda i: (0, nt - 1 - i, 0))
    return pl.pallas_call(
        body, name="fox_bwd_post", grid=(nt,),
        in_specs=[head_spec, head_spec, head_spec, pl.BlockSpec((tr, D_POOL), lambda i: (nt - 1 - i, 0)),
                  pl.BlockSpec((tr, LANES), lambda i: (nt - 1 - i, 0)), _vec_spec(LANES),
                  pl.BlockSpec(pick.shape, lambda i: (0, 0))],
        out_specs=[pl.BlockSpec((tr, D_IN_PAD), lambda i: (nt - 1 - i, 0)), _vec_spec(LANES)],
        out_shape=[jax.ShapeDtypeStruct((S, D_IN_PAD), BF16), jax.ShapeDtypeStruct((1, LANES), F32)],
        scratch_shapes=[pltpu.VMEM((SUBLANES, LANES), F32)],
        compiler_params=_params(("arbitrary",)),
    )(dqa, dka, dva, du, proj, bf_pad, jnp.asarray(pick, BF16))


POOL_HALO = 16


def _by_group(lane, a2, a4, a8, a16):
    return jnp.where(lane < 64, a2, jnp.where(lane < 128, a4, jnp.where(lane < 192, a8, a16)))


def _window_count(lane, t):
    return jnp.minimum(t + 1, _by_group(lane, 2, 4, 8, 16)).astype(F32)


def _pool_diff(u, halo, first, tile):
    n = TR + POOL_HALO
    ext = jnp.concatenate([jnp.where(first, 0.0, halo), u], axis=0)
    s2 = ext + pltpu.roll(ext, 1, 0)
    s4 = s2 + pltpu.roll(s2, 2, 0)
    s8 = s4 + pltpu.roll(s4, 4, 0)
    s16 = s8 + pltpu.roll(s8, 8, 0)
    lane = _lane_iota((n, D_POOL))
    win = _by_group(lane, s2, s4, s8, s16)[POOL_HALO:]
    lane = _lane_iota((TR, D_POOL))
    t = tile * TR + _row_iota((TR, D_POOL))
    return win / _window_count(lane, t) - u


def _prev_halo(rows, width, col):
    per = TR // rows
    return pl.BlockSpec((rows, width), lambda i: (jnp.maximum(i * per - 1, 0), col))


def _next_halo(rows, width, col):
    per = TR // rows
    return pl.BlockSpec((rows, width), lambda i: (jnp.minimum((i + 1) * per, S // rows - 1), col))


def _pool_fwd(proj, w_bd, ps, ycat):
    def body(u_ref, halo_ref, w_ref, ps_ref, ycat_ref, y_ref):
        i = pl.program_id(0)
        diff = _pool_diff(u_ref[...], halo_ref[...], i == 0, i)
        y_ref[...] = (_dot(diff.astype(BF16), w_ref[...]) * ps_ref[...]).astype(BF16)

    return pl.pallas_call(
        body, name="pool_fwd", grid=(S // TR,),
        in_specs=[_row_spec(TR, D_POOL), _prev_halo(POOL_HALO, D_POOL, 0),
                  pl.BlockSpec((D_POOL, D_POOL), lambda i: (0, 0)), _vec_spec(D_POOL), ANY],
        out_specs=_row_spec(TR, D_POOL), out_shape=jax.ShapeDtypeStruct((S, D), BF16), input_output_aliases={4: 0},
        compiler_params=_params(("parallel",)),
    )(proj, proj, w_bd, ps, ycat)


def _pool_bwd(proj, dycat, w_bd, w_bd_t, ps):
    nt = S // TR
    n = TR + POOL_HALO

    def body(u_ref, halo_ref, dy_ref, dyn_ref, w_ref, wt_ref, ps_ref, du_ref, dw_ref, dps_ref):
        i = pl.program_id(0)

        @pl.when(i == 0)
        def _():
            dw_ref[...] = jnp.zeros_like(dw_ref)
            dps_ref[...] = jnp.zeros_like(dps_ref)

        diff = _pool_diff(u_ref[...], halo_ref[...], i == 0, i).astype(BF16)
        dy = dy_ref[...]
        dps_ref[...] += jnp.sum(dy * _dot(diff, w_ref[...]), axis=0, keepdims=True)
        dy_ext = jnp.concatenate([dy, jnp.where(i == nt - 1, 0.0, dyn_ref[...])], axis=0)
        dmixed = (dy_ext * ps_ref[...]).astype(BF16)
        ddiff = _dot(dmixed, wt_ref[...])
        dw_ref[...] += _dot(diff, dmixed[:TR], TN)
        lane = _lane_iota((n, D_POOL))
        t = i * TR + _row_iota((n, D_POOL))
        e = ddiff / _window_count(lane, t)
        f2 = e + pltpu.roll(e, n - 1, 0)
        f4 = f2 + pltpu.roll(f2, n - 2, 0)
        f8 = f4 + pltpu.roll(f4, n - 4, 0)
        f16 = f8 + pltpu.roll(f8, n - 8, 0)
        du_ref[...] = _by_group(lane, f2, f4, f8, f16)[:TR] - ddiff[:TR]

    mat = pl.BlockSpec((D_POOL, D_POOL), lambda i: (0, 0))
    return pl.pallas_call(
        body, name="pool_bwd", grid=(nt,),
        in_specs=[_row_spec(TR, D_POOL), _prev_halo(POOL_HALO, D_POOL, 0), _row_spec(TR, D_POOL),
                  _next_halo(POOL_HALO, D_POOL, 0), mat, mat, _vec_spec(D_POOL)],
        out_specs=[_row_spec(TR, D_POOL), mat, _vec_spec(D_POOL)],
        out_shape=[jax.ShapeDtypeStruct((S, D_POOL), F32), jax.ShapeDtypeStruct((D_POOL, D_POOL), F32),
                   jax.ShapeDtypeStruct((1, D_POOL), F32)],
        compiler_params=_params(("arbitrary",)),
    )(proj, proj, dycat, dycat, w_bd, w_bd_t, ps)


XA_GROUP = 4


def _xa_probs(q, k):
    s = _dot(q, k, NT) * (XA_DH ** -0.5)
    e = jnp.exp(s - jnp.max(s, axis=-1, keepdims=True))
    return e * (1.0 / jnp.sum(e, axis=-1, keepdims=True))


def _xattn_fwd(qx, kv):
    def body(q_ref, kv_ref, o_ref):
        for h0 in range(0, XA_HEADS, XA_GROUP):
            cols = [slice(XA_DH * h, XA_DH * (h + 1)) for h in range(h0, h0 + XA_GROUP)]
            p = [_xa_probs(q_ref[:, c], kv_ref[:, c]) for c in cols]
            for ph, c in zip(p, cols):
                o_ref[:, c] = _dot(ph.astype(BF16), kv_ref[:, D + c.start:D + c.stop]).astype(BF16)

    return pl.pallas_call(
        body, name="xattn_fwd", grid=(S // TR,),
        in_specs=[_row_spec(TR, D), pl.BlockSpec((MEM, 2 * D), lambda i: (0, 0))],
        out_specs=_row_spec(TR, D), out_shape=jax.ShapeDtypeStruct((S, D), BF16),
        compiler_params=_params(("parallel",)),
    )(qx, kv)


def _xattn_bwd(qx, kv, dxo):
    def body(q_ref, kv_ref, do_ref, dq_ref, dkv_ref):
        i = pl.program_id(0)

        @pl.when(i == 0)
        def _():
            dkv_ref[...] = jnp.zeros_like(dkv_ref)

        for h0 in range(0, XA_HEADS, XA_GROUP):
            heads = range(h0, h0 + XA_GROUP)
            cols = [slice(XA_DH * h, XA_DH * (h + 1)) for h in heads]
            vcols = [slice(D + XA_DH * h, D + XA_DH * (h + 1)) for h in heads]
            q = [q_ref[:, c] for c in cols]
            k = [kv_ref[:, c] for c in cols]
            do = [do_ref[:, c] for c in cols]
            p = [_xa_probs(qh, kh) for qh, kh in zip(q, k)]
            dp = [_dot(doh, kv_ref[:, c], NT) for doh, c in zip(do, vcols)]
            ds = [(ph * (dph - jnp.sum(ph * dph, axis=-1, keepdims=True)) * (XA_DH ** -0.5)).astype(BF16) for ph, dph in zip(p, dp)]
            for a in range(XA_GROUP):
                dkv_ref[:, vcols[a]] += _dot(p[a].astype(BF16), do[a], TN)
                dq_ref[:, cols[a]] = _dot(ds[a], k[a]).astype(BF16)
                dkv_ref[:, cols[a]] += _dot(ds[a], q[a], TN)

    kv_spec = pl.BlockSpec((MEM, 2 * D), lambda i: (0, 0))
    return pl.pallas_call(
        body, name="xattn_bwd", grid=(S // TR,), in_specs=[_row_spec(TR, D), kv_spec, _row_spec(TR, D)],
        out_specs=[_row_spec(TR, D), kv_spec],
        out_shape=[jax.ShapeDtypeStruct((S, D), BF16), jax.ShapeDtypeStruct((MEM, 2 * D), F32)],
        compiler_params=_params(("arbitrary",)),
    )(qx, kv, dxo)


CONV_HALO = SUBLANES
TC = 512
TC_FWD = 1024
GELU_K = 0.7978845608028654
GELU_C = 0.044715


def _conv3(ext, w, rows):
    h0 = ext[CONV_HALO:CONV_HALO + rows]
    h1 = pltpu.roll(ext, 1, 0)[CONV_HALO:CONV_HALO + rows]
    h2 = pltpu.roll(ext, 2, 0)[CONV_HALO:CONV_HALO + rows]
    return w[2:3] * h0 + w[1:2] * h1 + w[0:1] * h2 + w[3:4], (h2, h1, h0)


def _conv_specs(tc):
    main = pl.BlockSpec((2, TR, tc), lambda j, i: (0, i, j))
    per = TR // CONV_HALO
    prev = pl.BlockSpec((2, CONV_HALO, tc), lambda j, i: (0, jnp.maximum(i * per - 1, 0), j))
    nxt = pl.BlockSpec((2, CONV_HALO, tc), lambda j, i: (0, jnp.minimum((i + 1) * per, S // CONV_HALO - 1), j))
    par = pl.BlockSpec((2, SUBLANES, tc), lambda j, i: (0, 0, j))
    return main, prev, nxt, par


def _convgate_fwd(hid, cwb):
    tc = TC_FWD

    def body(h_ref, hp_ref, w_ref, act_ref):
        i = pl.program_id(1)
        c = []
        for g in range(2):
            ext = jnp.concatenate([jnp.where(i == 0, 0.0, hp_ref[g]), h_ref[g]], axis=0)
            c.append(_conv3(ext, w_ref[g], TR)[0])
        gate, up = c
        act_ref[...] = (jax.nn.gelu(gate, approximate=True) * up).astype(BF16)

    main, prev, _, par = _conv_specs(tc)
    return pl.pallas_call(
        body, name="convgate_fwd", grid=(D_FF // tc, S // TR), in_specs=[main, prev, par],
        out_specs=pl.BlockSpec((TR, tc), lambda j, i: (i, j)), out_shape=jax.ShapeDtypeStruct((S, D_FF), BF16),
        compiler_params=_params(("parallel", "parallel")),
    )(hid, hid, cwb)


def _convgate_bwd(hid, dact, cwb):
    nr = S // TR
    n = TR + CONV_HALO

    def body(h_ref, hp_ref, hn_ref, da_ref, dan_ref, w_ref, dh_ref, dw_ref):
        i = pl.program_id(1)

        @pl.when(i == 0)
        def _():
            dw_ref[...] = jnp.zeros_like(dw_ref)

        da = jnp.concatenate([da_ref[...], jnp.where(i == nr - 1, 0.0, dan_ref[...])], axis=0)
        c, taps = [], []
        for g in range(2):
            ext = jnp.concatenate([jnp.where(i == 0, 0.0, hp_ref[g]), h_ref[g], hn_ref[g]], axis=0)
            cg, tg = _conv3(ext, w_ref[g], n)
            c.append(cg)
            taps.append(tg)
        gate, up = c
        th = jnp.tanh(GELU_K * (gate + GELU_C * gate * gate * gate))
        gelu = 0.5 * gate * (1.0 + th)
        dgelu = 0.5 * (1.0 + th) + 0.5 * gate * (1.0 - th * th) * GELU_K * (1.0 + 3.0 * GELU_C * gate * gate)
        for g, dc in enumerate((da * up * dgelu, da * gelu)):
            w = w_ref[g]
            dh = w[2:3] * dc[:TR] + w[1:2] * pltpu.roll(dc, n - 1, 0)[:TR] + w[0:1] * pltpu.roll(dc, n - 2, 0)[:TR]
            dh_ref[g] = dh.astype(BF16)
            dcm = dc[:TR]
            for r in range(3):
                dw_ref[g, r:r + 1, :] += jnp.sum(dcm * taps[g][r][:TR], axis=0, keepdims=True)
            dw_ref[g, 3:4, :] += jnp.sum(dcm, axis=0, keepdims=True)

    main, prev, nxt, par = _conv_specs(TC)
    per = TR // CONV_HALO
    return pl.pallas_call(
        body, name="convgate_bwd", grid=(D_FF // TC, nr),
        in_specs=[main, prev, nxt, pl.BlockSpec((TR, TC), lambda j, i: (i, j)),
                  pl.BlockSpec((CONV_HALO, TC), lambda j, i: (jnp.minimum((i + 1) * per, S // CONV_HALO - 1), j)), par],
        out_specs=[main, par],
        out_shape=[jax.ShapeDtypeStruct((2, S, D_FF), BF16), jax.ShapeDtypeStruct((2, SUBLANES, D_FF), F32)],
        compiler_params=_params(("parallel", "arbitrary")),
    )(hid, hid, hid, dact, dact, cwb)


def _adam_update(w, g, m, v):
    m = ADAM_B1 * m + (1.0 - ADAM_B1) * g
    v = ADAM_B2 * v + (1.0 - ADAM_B2) * (g * g)
    m_hat = m / (1.0 - ADAM_B1 ** ADAM_STEP)
    v_hat = v / (1.0 - ADAM_B2 ** ADAM_STEP)
    return -ADAM_LR * (m_hat / (jnp.sqrt(v_hat) + ADAM_EPS) + ADAM_WD * w), m, v


def _row_tile(rows, cols, itemsize=4, target=TILE_BYTES):
    tr = SUBLANES
    while rows % (2 * tr) == 0 and 2 * tr * cols * itemsize <= target:
        tr *= 2
    assert rows % tr == 0, (rows, tr)
    return rows if rows % (2 * tr) and 16 * tr * cols * itemsize < target else tr


def _adamw(name, w, g, m, v):
    rows, cols = w.shape
    tr = rows if rows * cols * 4 <= TILE_BYTES // 2 else _row_tile(rows, cols, target=TILE_BYTES // 2)

    def body(w_ref, g_ref, m_ref, v_ref, d_ref, nm_ref, nv_ref):
        d_ref[...], nm_ref[...], nv_ref[...] = _adam_update(w_ref[...], g_ref[...], m_ref[...], v_ref[...])

    spec = _row_spec(tr, cols)
    shape = jax.ShapeDtypeStruct((rows, cols), F32)
    return pl.pallas_call(
        body, name=name, grid=(rows // tr,), in_specs=[spec] * 4, out_specs=[spec] * 3, out_shape=[shape] * 3,
        compiler_params=_params(("parallel",)),
    )(w, g, m, v)


def _adamw_halves(name, core, w, g_mine, g_sibling, m, v):
    rows, cols = w.shape
    half = rows // 2
    tr = _row_tile(half, cols, target=TILE_BYTES // 2)
    per = half // tr

    def body(core_ref, w_ref, gm_ref, gs_ref, m_ref, v_ref, g_ref, d_ref, nm_ref, nv_ref):
        g = jnp.where(pl.program_id(0) // per == core_ref[0], gm_ref[...], gs_ref[...])
        g_ref[...] = g
        d_ref[...], nm_ref[...], nv_ref[...] = _adam_update(w_ref[...], g, m_ref[...], v_ref[...])

    spec = pl.BlockSpec((tr, cols), lambda i, core_ref: (i, 0))
    half_spec = pl.BlockSpec((tr, cols), lambda i, core_ref: (i % per, 0))
    shape = jax.ShapeDtypeStruct((rows, cols), F32)
    return pl.pallas_call(
        body, name=name, out_shape=[shape] * 4,
        grid_spec=pltpu.PrefetchScalarGridSpec(
            num_scalar_prefetch=1, grid=(rows // tr,), in_specs=[spec, half_spec, half_spec, spec, spec], out_specs=[spec] * 4),
        compiler_params=_params(("parallel",)),
    )(core, w, g_mine, g_sibling, m, v)


def _adamw_halves_columns(name, core, w, g_mine, g_sibling, m, v):
    cols, _, rows = w.shape
    tl = 2 * LANES
    per = rows // 2 // tl

    def body(core_ref, w_ref, gm_ref, gs_ref, m_ref, v_ref, g_ref, d_ref, nm_ref, nv_ref):
        g = jnp.where(pl.program_id(0) // per == core_ref[0], gm_ref[...], gs_ref[...])
        g_ref[...] = g
        d_ref[...], nm_ref[...], nv_ref[...] = _adam_update(w_ref[...], g, m_ref[...], v_ref[...])

    spec = pl.BlockSpec((cols, 1, tl), lambda i, core_ref: (0, 0, i))
    half_spec = pl.BlockSpec((cols, 1, tl), lambda i, core_ref: (0, 0, i % per))
    shape = jax.ShapeDtypeStruct((cols, 1, rows), F32)
    return pl.pallas_call(
        body, name=name, out_shape=[shape] * 4,
        grid_spec=pltpu.PrefetchScalarGridSpec(
            num_scalar_prefetch=1, grid=(rows // tl,), in_specs=[spec, half_spec, half_spec, spec, spec], out_specs=[spec] * 4),
        compiler_params=_params(("parallel",)),
    )(core, w, g_mine, g_sibling, m, v)


def _chip_sum(name, core, g, other):
    _, _, half, cols = g.shape
    tr = _row_tile(half, cols)

    def body(core_ref, g_ref, o_ref, p_ref):
        p_ref[...] = (g_ref[...] + o_ref[...]).astype(BF16)

    spec = pl.BlockSpec((None, tr, cols), lambda j, i, core_ref: (j, i, 0))
    return pl.pallas_call(
        body, name=name, out_shape=jax.ShapeDtypeStruct((N_CHIPS, half, cols), BF16),
        grid_spec=pltpu.PrefetchScalarGridSpec(
            num_scalar_prefetch=1, grid=(N_CHIPS, half // tr),
            in_specs=[pl.BlockSpec((None, None, tr, cols), lambda j, i, core_ref: (j, core_ref[0], i, 0)), spec],
            out_specs=spec),
        compiler_params=_params(("parallel", "parallel")),
    )(core, g, other)


def _mesh_sum(name, chip, received, own):
    _, half, cols = received.shape
    tr = _row_tile(half, cols, itemsize=2 * N_CHIPS)

    def body(chip_ref, r_ref, own_ref, o_ref):
        acc = None
        for j in range(N_CHIPS):
            term = jnp.where(chip_ref[0] == j, own_ref[...], r_ref[j]).astype(F32)
            acc = term if acc is None else acc + term
        o_ref[...] = acc

    return pl.pallas_call(
        body, name=name, out_shape=jax.ShapeDtypeStruct((half, cols), F32),
        grid_spec=pltpu.PrefetchScalarGridSpec(
            num_scalar_prefetch=1, grid=(half // tr,),
            in_specs=[pl.BlockSpec((N_CHIPS, tr, cols), lambda i, chip_ref: (0, i, 0)),
                      pl.BlockSpec((None, tr, cols), lambda i, chip_ref: (chip_ref[0], i, 0))],
            out_specs=pl.BlockSpec((tr, cols), lambda i, chip_ref: (i, 0))),
        compiler_params=_params(("parallel",)),
    )(chip, received, own)


CHIP_FLIPS = ((1, 0), (0, 1), (1, 1))


def _place():
    x, y, c = lax.axis_index("x"), lax.axis_index("y"), lax.axis_index("c")
    return x, y, c, 2 * x + y


def _remote(src, dst, sems_s, sems_r, k, dev):
    return pltpu.make_async_remote_copy(src_ref=src, dst_ref=dst, send_sem=sems_s.at[k], recv_sem=sems_r.at[k],
                                        device_id=dev, device_id_type=MESH)


class _Exchange:
    def __init__(self, ins, out_shapes, n_sems, start, forward, finish):
        self.ins, self.out_shapes, self.n_sems = list(ins), list(out_shapes), n_sems
        self.start, self.forward, self.finish = start, forward, finish

    def scratch(self):
        return [pltpu.SemaphoreType.DMA((self.n_sems,)), pltpu.SemaphoreType.DMA((self.n_sems,))]

    def run(self, name):
        n = len(self.ins)

        def body(*refs):
            args = (refs[:n], refs[n:2 * n]) + tuple(refs[2 * n:])
            self.start(*args)
            self.forward(*args)
            self.finish(*args)

        return pl.pallas_call(
            body, name=name, in_specs=[ANY] * n, out_specs=[ANY] * n, out_shape=self.out_shapes, scratch_shapes=self.scratch(),
        )(*self.ins)


def _all_gather_weights(halved, whole):
    nh, nw = len(halved), len(whole)
    n_arr = nh + nw

    def copies(ins, outs, sems_s, sems_r):
        x, y, c, me = _place()
        sibling = (x, y, 1 - c)
        own = [_remote(ins[k], outs[k].at[me], sems_s, sems_r, k, sibling) for k in range(n_arr)]
        first, passed = [], []
        for k in range(n_arr):
            for f, (fx, fy) in enumerate(CHIP_FLIPS):
                src, dst = (ins[k].at[c], outs[k].at[me, c]) if k < nh else (ins[k], outs[k].at[me])
                first.append(_remote(src, dst, sems_s, sems_r, n_arr + 3 * k + f, (x ^ fx, y ^ fy, c)))
        for k in range(nh):
            for f, (fx, fy) in enumerate(CHIP_FLIPS):
                landed = outs[k].at[2 * (x ^ fx) + (y ^ fy), c]
                passed.append(_remote(landed, landed, sems_s, sems_r, 4 * n_arr + 3 * k + f, sibling))
        return own, first, passed

    def start(*refs):
        own, first, _ = copies(*refs)
        for cp in own + first:
            cp.start()

    def forward(*refs):
        _, first, passed = copies(*refs)
        for arrived, cp in zip(first, passed):
            arrived.wait_recv()
            cp.start()

    def finish(*refs):
        own, first, passed = copies(*refs)
        for cp in first[3 * nh:] + passed + own:
            cp.wait_recv()
        for cp in first + passed + own:
            cp.wait_send()

    shapes = [jax.ShapeDtypeStruct((N_CHIPS,) + a.shape, a.dtype) for a in list(halved) + list(whole)]
    return _Exchange(list(halved) + list(whole), shapes, 7 * nh + 4 * nw, start, forward, finish)


def _swap_halves(gs):
    n = len(gs)

    def copies(ins, outs, sems_s, sems_r):
        x, y, c, _ = _place()
        return [_remote(ins[k].at[:, 1 - c], outs[k], sems_s, sems_r, k, (x, y, 1 - c)) for k in range(n)]

    def start(*refs):
        for cp in copies(*refs):
            cp.start()

    def finish(*refs):
        for cp in copies(*refs):
            cp.wait()

    shapes = [jax.ShapeDtypeStruct((g.shape[0],) + g.shape[2:], g.dtype) for g in gs]
    return _Exchange(gs, shapes, n, start, _no_copies, finish)


def _scatter_chips(ps):
    n = len(ps)

    def copies(ins, outs, sems_s, sems_r):
        x, y, c, me = _place()
        return [_remote(ins[k].at[2 * (x ^ fx) + (y ^ fy)], outs[k].at[me], sems_s, sems_r, 3 * k + f, (x ^ fx, y ^ fy, c))
                for k in range(n) for f, (fx, fy) in enumerate(CHIP_FLIPS)]

    def start(*refs):
        for cp in copies(*refs):
            cp.start()

    def forward(*refs):
        pass

    def finish(*refs):
        for cp in copies(*refs):
            cp.wait()

    shapes = [jax.ShapeDtypeStruct(p.shape, p.dtype) for p in ps]
    return _Exchange(ps, shapes, 3 * n, start, forward, finish)


def _swap_reduced(rs):
    n = len(rs)

    def copies(ins, outs, sems_s, sems_r):
        x, y, c, _ = _place()
        return [_remote(ins[k], outs[k], sems_s, sems_r, k, (x, y, 1 - c)) for k in range(n)]

    def start(*refs):
        for cp in copies(*refs):
            cp.start()

    def finish(*refs):
        for cp in copies(*refs):
            cp.wait()

    return _Exchange(rs, [jax.ShapeDtypeStruct(r.shape, r.dtype) for r in rs], n, start, _no_copies, finish)


N_DEV = 8


def _gather_small(buf):
    def copies(ins, outs, sems_s, sems_r):
        x, y, c, chip = _place()
        sibling = (x, y, 1 - c)
        own = _remote(ins[0], outs[0].at[2 * chip + c], sems_s, sems_r, 0, sibling)
        first = [_remote(ins[0], outs[0].at[2 * chip + c], sems_s, sems_r, 1 + f, (x ^ fx, y ^ fy, c))
                 for f, (fx, fy) in enumerate(CHIP_FLIPS)]
        passed = []
        for f, (fx, fy) in enumerate(CHIP_FLIPS):
            landed = outs[0].at[2 * (2 * (x ^ fx) + (y ^ fy)) + c]
            passed.append(_remote(landed, landed, sems_s, sems_r, 4 + f, sibling))
        return own, first, passed

    def start(*refs):
        own, first, _ = copies(*refs)
        for cp in [own] + first:
            cp.start()

    def forward(*refs):
        _, first, passed = copies(*refs)
        for arrived, cp in zip(first, passed):
            arrived.wait_recv()
            cp.start()

    def finish(*refs):
        own, first, passed = copies(*refs)
        for cp in passed + [own]:
            cp.wait_recv()
        for cp in first + passed + [own]:
            cp.wait_send()

    return _Exchange([buf], [jax.ShapeDtypeStruct((N_DEV,) + buf.shape, buf.dtype)], N_DEV - 1, start, forward, finish)


def _sum_devices(place, gathered, own):
    rows = own.shape[0]

    def body(place_ref, g_ref, own_ref, o_ref):
        acc = None
        for d in range(N_DEV):
            term = jnp.where(place_ref[0] == d, own_ref[...], g_ref[d])
            acc = term if acc is None else acc + term
        o_ref[...] = acc

    return pl.pallas_call(
        body, name="sum_devices", out_shape=jax.ShapeDtypeStruct((rows, LANES), F32),
        grid_spec=pltpu.PrefetchScalarGridSpec(
            num_scalar_prefetch=1, grid=(1,),
            in_specs=[pl.BlockSpec((N_DEV, rows, LANES), lambda i, place_ref: (0, 0, 0)),
                      pl.BlockSpec((rows, LANES), lambda i, place_ref: (0, 0))],
            out_specs=pl.BlockSpec((rows, LANES), lambda i, place_ref: (0, 0))),
        compiler_params=_params(("arbitrary",)),
    )(place, gathered, own)


def _no_copies(*refs):
    pass


def _no_exchange():
    return _Exchange([], [], 1, _no_copies, _no_copies, _no_copies)


class _NoComm:
    def gather_first(self):
        return _no_exchange()

    def first_landed(self, p, landed):
        pass

    def gather_rest(self, p):
        return _no_exchange()

    def weights_landed(self, p, landed):
        pass

    def gather_last(self):
        return _no_exchange()

    def last_landed(self, p, landed):
        pass

    def swap_first(self, g):
        return _no_exchange()

    def first_swapped(self, landed):
        pass

    def swap_second(self, g):
        return _no_exchange()

    def second_swapped(self, landed):
        pass

    def scatter_early(self, g):
        return _no_exchange()

    def scatter_landed(self, landed):
        pass

    def swap_reduced_early(self):
        return _no_exchange()

    def reduced_landed(self, landed):
        pass

    def scatter_late(self, g):
        return _no_exchange()

    def late_landed(self, landed):
        pass


def _local_step(x, mem, target, p, comm):
    h1, landed = _norm_fwd("norm_mix_pre", x, p["norm_mix_pre"], comm.gather_first())
    comm.first_landed(p, landed)
    qa, ka, va, u, z = _in_proj(h1, p["w_in"], p["bf_pad"])
    ycat, qab, landed = _fox_fwd(qa, ka, va, comm.gather_rest(p))
    comm.weights_landed(p, landed)
    ycat = _pool_fwd(u, p["w_pool_bd"], p["pool_scale"], ycat)
    y1, x2, h2, qx = _proj_resid_norm("mix_out", ycat, p["w_mix_out"], x, p["norm_mix_post"], p["norm_xa_pre"], p["w_xq"])
    mem_n = _norm_fwd("norm_mem", mem, p["norm_mem"])
    kv = _mm(
        "xkv", mem_n, p["w_xkv"], pl.BlockSpec((MEM, D), lambda i, j, k: (0, 0)),
        pl.BlockSpec((None, D, 512), lambda i, j, k: (j, 0, 0)), jax.ShapeDtypeStruct((MEM, 2 * D), BF16),
        pl.BlockSpec((MEM, 512), lambda i, j, k: (0, j)), (1, N_CHIPS, 1), NN, (MEM, 512))
    xo = _xattn_fwd(qx, kv)
    y2, x3, h3 = _proj_resid_norm("xo", xo, p["w_xo"], x2, p["norm_xa_post"], p["norm_ffn_pre"])
    hid, landed = _mm(
        "up_proj", h3, p["w_up"], pl.BlockSpec((2048, D), lambda i, j, k: (i, 0)),
        pl.BlockSpec((None, D, 1024), lambda i, j, k: (j // 2, 0, j % 2)), jax.ShapeDtypeStruct((2, S, D_FF), F32),
        pl.BlockSpec((None, 2048, 1024), lambda i, j, k: (j // 4, i, j % 4)), (S // 2048, 8, 1), NN, (2048, 1024),
        comm.gather_last())
    comm.last_landed(p, landed)
    act = _convgate_fwd(hid, p["cwb"])

    g = {}
    dres, dy3, g["norm_ffn_post"], loss_cols = _down_loss_bwd(act, p["w_down"], x3, p["norm_ffn_post"], target)
    dact = _mm_nt("d_act", dy3, p["w_down"], F32, 2048, 1024)
    g["w_down"] = _mm_tn("dw_down", act, dy3, 1024, 512)
    dhid, dcwb = _convgate_bwd(hid, dact, p["cwb"])
    g["w_up"] = _mm(
        "dw_up", h3, dhid, pl.BlockSpec((S, D), lambda i, j, k: (0, 0)),
        pl.BlockSpec((None, S, 512), lambda i, j, k: (j // 8, 0, j % 8)), jax.ShapeDtypeStruct((N_CHIPS, D, 2048), F32),
        pl.BlockSpec((None, D, 512), lambda i, j, k: (j // 4, 0, j % 4)), (1, 16, 1), TN, (D, 512))
    dh3, landed = _d_h3(dhid, p["w_up"], comm.swap_first(g))
    comm.first_swapped(landed)
    dres, dy2, dxo, g["norm_ffn_pre"], g["norm_xa_post"] = _mid_bwd(
        "bwd_ffn_xa", dres, x3, p["norm_ffn_pre"], dh3, y2, p["norm_xa_post"], p["w_xo"])
    g["w_xo"] = _mm_tn("dw_xo", xo, dy2, 1024, 512)
    dqx, dkv = _xattn_bwd(qx, kv, dxo)
    dkv = dkv.astype(BF16)
    g["w_xq"] = _mm_tn("dw_xq", h2, dqx, 1024, 512)
    dmem_n = _mm(
        "d_mem", dkv, p["w_xkv"], pl.BlockSpec((MEM, 512), lambda i, j, k: (0, k)),
        pl.BlockSpec((None, D, 512), lambda i, j, k: (k, 0, 0)), jax.ShapeDtypeStruct((MEM, D), F32),
        pl.BlockSpec((MEM, D), lambda i, j, k: (0, 0)), (1, 1, N_CHIPS), NT, (MEM, D))
    g["w_xkv"] = _mm(
        "dw_xkv", mem_n, dkv, pl.BlockSpec((MEM, D), lambda i, j, k: (0, 0)),
        pl.BlockSpec((MEM, 512), lambda i, j, k: (0, j)), jax.ShapeDtypeStruct((N_CHIPS, D, 512), F32),
        pl.BlockSpec((None, D, 512), lambda i, j, k: (j, 0, 0)), (1, N_CHIPS, 1), TN, (D, 512))
    g["norm_mem"] = _gain_bwd("dg_mem", mem, p["norm_mem"], dmem_n)
    (dres, dy1, g["norm_xa_pre"], g["norm_mix_post"], dy_pool, doa), landed = _bwd_xa_mix(
        dqx, p["w_xq"], dres, x2, p["norm_xa_pre"], y1, p["norm_mix_post"], p["w_mix_out"], ycat, comm.swap_second(g))
    comm.second_swapped(landed)
    g["w_mix_out"] = _mm_tn("dw_mix_out", ycat, dy1, 1024, 512)
    dqa, dka, dva, landed = _fox_bwd(qab, doa, ka, va, comm.scatter_early(g))
    comm.scatter_landed(landed)
    du, g["w_pool_full"], g["pool_scale"] = _pool_bwd(u, dy_pool, p["w_pool_bd"], p["w_pool_bd_t"], p["pool_scale"])
    dproj, g["bf_pad"] = _fox_bwd_post(dqa, dka, dva, du, z, p["bf_pad"])
    g["w_in"], landed = _dw_in(dproj, h1, comm.swap_reduced_early())
    comm.reduced_landed(landed)
    grad_x, g["norm_mix_pre"], landed = _d_h1_first_bwd(dproj, p["w_in"], dres, x, p["norm_mix_pre"], comm.scatter_late(g))
    comm.late_landed(landed)
    g["cwb"] = dcwb
    return grad_x, g, loss_cols


BIG = ("w_in", "w_mix_out", "w_xq", "w_xkv", "w_xo", "w_up", "w_down")
ROW_SHARDED = ("w_mix_out", "w_xq", "w_xo", "w_down")
SMALL = ("norm_mix_pre", "norm_mix_post", "b_forget", "w_pool", "pool_scale", "norm_mem", "norm_xa_pre", "norm_xa_post",
         "norm_ffn_pre", "norm_ffn_post", "conv_b")
ORDER = ("norm_mix_pre", "norm_mix_post", "w_in", "b_forget", "w_pool", "pool_scale", "w_mix_out", "norm_mem", "norm_xa_pre",
         "norm_xa_post", "w_xq", "w_xkv", "w_xo", "norm_ffn_pre", "norm_ffn_post", "w_up", "conv_w", "conv_b", "w_down")
SLOT = SUBLANES * LANES


def _pack(parts):
    rows, offs, off = [], [], 0
    for a in parts:
        flat = a.reshape(-1).astype(F32)
        n = -(-flat.shape[0] // SLOT) * SLOT
        rows.append(jnp.pad(flat, (0, n - flat.shape[0])).reshape(n // LANES, LANES))
        offs.append(off)
        off += n // LANES
    return jnp.concatenate(rows, axis=0), offs


def _unpack(buf, off, like):
    n = like.size
    rows = -(-n // LANES)
    return buf[off:off + rows].reshape(-1)[:n].reshape(like.shape)


FIRST = ("w_in",)
REST = ("w_mix_out", "w_xq", "w_xkv", "w_xo", "w_up")
LAST = ("w_down",)


def _local_params(w):
    w_pool_bd = jnp.zeros((D_POOL, D_POOL), F32)
    for gi in range(4):
        w_pool_bd = w_pool_bd.at[64 * gi:64 * (gi + 1), 64 * gi:64 * (gi + 1)].set(w["w_pool"][0, gi])
    p = {n: w[n] for n in ("norm_mix_pre", "norm_mix_post", "norm_mem", "norm_xa_pre", "norm_xa_post", "norm_ffn_pre",
                           "norm_ffn_post")}
    p.update(
        bf_pad=jnp.pad(w["b_forget"], ((0, 0), (0, LANES - HEADS))),
        w_pool_bd=w_pool_bd.astype(BF16), w_pool_bd_t=w_pool_bd.T.astype(BF16), pool_scale=w["pool_scale"].reshape(1, D_POOL))
    return p


def _w_in_param(stacked):
    n, rows, cols = stacked.shape
    tr = PREP_TR

    def body(w_ref, o_ref):
        o_ref[...] = jnp.concatenate([w_ref[j] for j in range(n)] + [jnp.zeros((tr, D_IN_PAD - n * cols), BF16)], axis=1)

    return pl.pallas_call(
        body, name="w_in_whole", grid=(rows // tr,), in_specs=[pl.BlockSpec((n, tr, cols), lambda i: (0, i, 0))],
        out_specs=_row_spec(tr, D_IN_PAD), out_shape=jax.ShapeDtypeStruct((rows, D_IN_PAD), BF16),
        compiler_params=_params(("parallel",)),
    )(stacked)


def _rest_params(w, full, conv_w_full):
    cw2 = conv_w_full.reshape(3, 2, D_FF).transpose(1, 0, 2)
    cwb = jnp.concatenate([cw2, w["conv_b"].reshape(1, 2, D_FF).transpose(1, 0, 2), jnp.zeros((2, 4, D_FF), F32)], axis=1)
    return dict(w_mix_out=full["w_mix_out"].reshape(D, D), w_xq=full["w_xq"].reshape(D, D), w_xkv=full["w_xkv"],
                w_xo=full["w_xo"].reshape(D, D), w_up=full["w_up"], cwb=cwb)


def _whole_params(w, full, conv_w_full):
    p = _local_params(w)
    p.update(_rest_params(w, full, conv_w_full), w_in=_w_in_param(full["w_in"]), w_down=full["w_down"].reshape(D_FF, D))
    return p


def _halved(a):
    return a.reshape(a.shape[:-2] + (2, a.shape[-2] // 2, a.shape[-1]))


class _StepComm:
    def __init__(self, w, shard2d, conv_w, core_id, chip_id):
        self.w, self.shard2d, self.conv_w, self.core_id, self.chip_id = w, shard2d, conv_w, core_id, chip_id
        self.first, self.second = ("w_up", "w_down"), ("w_xq", "w_xkv", "w_xo")
        self.early = self.first + self.second
        self.late = ("w_in", "w_mix_out")

    def gather_first(self):
        return _all_gather_weights([_halved(self.shard2d[n].astype(BF16)) for n in FIRST], [])

    def first_landed(self, p, landed):
        p["w_in"] = _w_in_param(landed[0].reshape((N_CHIPS,) + self.shard2d["w_in"].shape))

    def gather_rest(self, p):
        return _all_gather_weights([_halved(self.shard2d[n].astype(BF16)) for n in REST], [self.conv_w.reshape(3, -1)])

    def weights_landed(self, p, landed):
        full = {n: a.reshape((N_CHIPS,) + self.shard2d[n].shape) for n, a in zip(REST, landed)}
        conv_w_full = jnp.transpose(landed[-1], (1, 0, 2)).reshape(3, 2 * D_FF)
        p.update(_rest_params(self.w, full, conv_w_full))

    def gather_last(self):
        return _all_gather_weights([_halved(self.shard2d[n].astype(BF16)) for n in LAST], [])

    def last_landed(self, p, landed):
        p["w_down"] = landed[0].reshape(D_FF, D)

    def _view(self, g, n):
        return _halved(g[n].reshape((N_CHIPS,) + self.shard2d[n].shape))

    def swap_first(self, g):
        return _swap_halves([self._view(g, n) for n in self.first])

    def first_swapped(self, landed):
        self.from_sibling = dict(zip(self.first, landed))

    def swap_second(self, g):
        return _swap_halves([self._view(g, n) for n in self.second])

    def second_swapped(self, landed):
        self.from_sibling.update(zip(self.second, landed))

    def scatter_early(self, g):
        self.partial = [_chip_sum("chip_sum_" + n, self.core_id, self._view(g, n), self.from_sibling[n]) for n in self.early]
        return _scatter_chips(self.partial)

    def scatter_landed(self, landed):
        self.received = list(landed)

    def swap_reduced_early(self):
        self.reduced = [_mesh_sum("mesh_sum_" + n, self.chip_id, r, own)
                        for n, r, own in zip(self.early, self.received, self.partial)]
        return _swap_reduced(self.reduced)

    def reduced_landed(self, landed):
        self.reduced_sibling = list(landed)

    def scatter_late(self, g):
        views = [g["w_in"], self._view(g, "w_mix_out")]
        from_sibling = _swap_halves(views).run("swap_halves_late")
        self.partial_late = [_chip_sum("chip_sum_" + n, self.core_id, view, other)
                             for n, view, other in zip(self.late, views, from_sibling)]
        return _scatter_chips(self.partial_late)

    def late_landed(self, landed):
        self.received_late = list(landed)


def kernel(x, mem, norm_mix_pre, norm_mix_post, w_in, b_forget, w_pool, pool_scale, w_mix_out, norm_mem, norm_xa_pre, norm_xa_post, w_xq, w_xkv, w_xo, norm_ffn_pre, norm_ffn_post, w_up, conv_w, conv_b, w_down, loss_target, m_norm_mix_pre, m_norm_mix_post, m_w_in, m_b_forget, m_w_pool, m_pool_scale, m_w_mix_out, m_norm_mem, m_norm_xa_pre, m_norm_xa_post, m_w_xq, m_w_xkv, m_w_xo, m_norm_ffn_pre, m_norm_ffn_post, m_w_up, m_conv_w, m_conv_b, m_w_down, v_norm_mix_pre, v_norm_mix_post, v_w_in, v_b_forget, v_w_pool, v_pool_scale, v_w_mix_out, v_norm_mem, v_norm_xa_pre, v_norm_xa_post, v_w_xq, v_w_xkv, v_w_xo, v_norm_ffn_pre, v_norm_ffn_post, v_w_up, v_conv_w, v_conv_b, v_w_down):
    w = dict(norm_mix_pre=norm_mix_pre, norm_mix_post=norm_mix_post, w_in=w_in, b_forget=b_forget, w_pool=w_pool,
             pool_scale=pool_scale, w_mix_out=w_mix_out, norm_mem=norm_mem, norm_xa_pre=norm_xa_pre, norm_xa_post=norm_xa_post,
             w_xq=w_xq, w_xkv=w_xkv, w_xo=w_xo, norm_ffn_pre=norm_ffn_pre, norm_ffn_post=norm_ffn_post, w_up=w_up,
             conv_w=conv_w, conv_b=conv_b, w_down=w_down)
    m = dict(norm_mix_pre=m_norm_mix_pre, norm_mix_post=m_norm_mix_post, w_in=m_w_in, b_forget=m_b_forget, w_pool=m_w_pool,
             pool_scale=m_pool_scale, w_mix_out=m_w_mix_out, norm_mem=m_norm_mem, norm_xa_pre=m_norm_xa_pre,
             norm_xa_post=m_norm_xa_post, w_xq=m_w_xq, w_xkv=m_w_xkv, w_xo=m_w_xo, norm_ffn_pre=m_norm_ffn_pre,
             norm_ffn_post=m_norm_ffn_post, w_up=m_w_up, conv_w=m_conv_w, conv_b=m_conv_b, w_down=m_w_down)
    v = dict(norm_mix_pre=v_norm_mix_pre, norm_mix_post=v_norm_mix_post, w_in=v_w_in, b_forget=v_b_forget, w_pool=v_w_pool,
             pool_scale=v_pool_scale, w_mix_out=v_w_mix_out, norm_mem=v_norm_mem, norm_xa_pre=v_norm_xa_pre,
             norm_xa_post=v_norm_xa_post, w_xq=v_w_xq, w_xkv=v_w_xkv, w_xo=v_w_xo, norm_ffn_pre=v_norm_ffn_pre,
             norm_ffn_post=v_norm_ffn_post, w_up=v_w_up, conv_w=v_conv_w, conv_b=v_conv_b, w_down=v_w_down)
    chip = 2 * lax.axis_index("x") + lax.axis_index("y")

    core_id = lax.axis_index("c").astype(jnp.int32).reshape(1)
    chip_id = chip.astype(jnp.int32).reshape(1)

    shard2d = {n: w[n][0] for n in BIG}
    p = _local_params(w)
    comm = _StepComm(w, shard2d, conv_w, core_id, chip_id)
    grad_x, g, loss_cols = _local_step(x[0], mem[0], loss_target[0], p, comm)

    reduced_late = [_mesh_sum("mesh_sum_" + n, chip_id, r, own)
                    for n, r, own in zip(comm.late, comm.received_late, comm.partial_late)]
    names = comm.late + comm.early
    reduced = reduced_late + comm.reduced
    reduced_sibling = list(_swap_reduced(reduced_late).run("swap_reduced_late")) + comm.reduced_sibling
    grads = {}

    gw_pool = jnp.stack([g["w_pool_full"][64 * gi:64 * (gi + 1), 64 * gi:64 * (gi + 1)] for gi in range(4)])
    dcwb = g["cwb"]
    g_conv_w = dcwb[:, 0:3, :].transpose(1, 0, 2).reshape(3, 2 * D_FF)
    g_conv_b = dcwb[:, 3, :].reshape(2 * D_FF)
    small_g = dict(norm_mix_pre=g["norm_mix_pre"], norm_mix_post=g["norm_mix_post"], b_forget=g["bf_pad"][:, :HEADS],
                   w_pool=gw_pool, pool_scale=g["pool_scale"], norm_mem=g["norm_mem"], norm_xa_pre=g["norm_xa_pre"],
                   norm_xa_post=g["norm_xa_post"], norm_ffn_pre=g["norm_ffn_pre"], norm_ffn_post=g["norm_ffn_post"],
                   conv_b=g_conv_b)
    local_buf, offs = _pack([small_g[n] for n in SMALL] + [g_conv_w, loss_cols])

    delta, new_m, new_v = {}, {}, {}
    for n, g_mine, g_sibling in zip(names, reduced, reduced_sibling):
        cols = shard2d[n].shape[1]
        if cols % LANES:
            outs = _adamw_halves_columns("adamw_" + n, core_id, jnp.transpose(w[n], (2, 0, 1)), g_mine[:cols, None, :],
                                         g_sibling[:cols, None, :], jnp.transpose(m[n], (2, 0, 1)), jnp.transpose(v[n], (2, 0, 1)))
            gn, d, nm, nv = (jnp.transpose(o, (1, 2, 0)) for o in outs)
        else:
            gn, d, nm, nv = (o[None] for o in _adamw_halves("adamw_" + n, core_id, shard2d[n], g_mine, g_sibling, m[n][0], v[n][0]))
        grads[n], delta[n], new_m[n], new_v[n] = gn, d, nm, nv
    place = (2 * chip + lax.axis_index("c")).astype(jnp.int32).reshape(1)
    buf = _sum_devices(place, _gather_small(local_buf).run("gather_small")[0], local_buf)
    for n, off in zip(SMALL, offs):
        grads[n] = _unpack(buf, off, w[n])
    g_conv_w = _unpack(buf, offs[len(SMALL)], g_conv_w)
    grads["conv_w"] = lax.dynamic_slice_in_dim(g_conv_w, chip * (2 * D_FF // N_CHIPS), 2 * D_FF // N_CHIPS, axis=1).reshape(conv_w.shape)
    loss = jnp.sum(_unpack(buf, offs[len(SMALL) + 1], loss_cols))
    small_names = SMALL + ("conv_w",)
    packed = [_pack([d[n] for n in small_names])[0] for d in (w, grads, m, v)]
    offs = _pack([w[n] for n in small_names])[1]
    d, nm, nv = _adamw("adamw_small", *packed)
    for n, off in zip(small_names, offs):
        delta[n], new_m[n], new_v[n] = _unpack(d, off, w[n]), _unpack(nm, off, w[n]), _unpack(nv, off, w[n])

    return (loss, grad_x[None], *[grads[n] for n in ORDER], *[delta[n] for n in ORDER], *[new_m[n] for n in ORDER],
            *[new_v[n] for n in ORDER])
```

```python
import functools

import jax
import jax.numpy as jnp
import numpy as np
from jax import lax
from jax.experimental import pallas as pl
from jax.experimental.pallas import tpu as pltpu

F32 = jnp.float32
BF16 = jnp.bfloat16
MESH = pl.DeviceIdType.MESH
ANY = pl.BlockSpec(memory_space=pl.ANY)
VMEM_SPEC = pl.BlockSpec(memory_space=pltpu.VMEM)

S = 4096
D = 1024
MEM = 256
D_POOL = 256
HEADS = 12
DH = 64
D_FOX = HEADS * DH
D_IN = D_POOL + 3 * D_FOX + HEADS
F_OFF = D_POOL + 3 * D_FOX
Q_OFF, K_OFF, V_OFF = D_POOL, D_POOL + D_FOX, D_POOL + 2 * D_FOX
XA_HEADS = 4
XA_DH = 256
D_FF = 4096
EPS = 1e-6
N_CHIPS = 4
ADAM_LR, ADAM_B1, ADAM_B2, ADAM_EPS, ADAM_WD, ADAM_STEP = 0.001, 0.9, 0.999, 1e-08, 0.01, 10

LANES = 128
SUBLANES = 8
D_IN_PAD = 21 * LANES
TR = 512
RING = 3
ROW_PIECES = 4
TILE_BYTES = 2 * 1024 * 1024
NEG = -1e30
VMEM_LIMIT = 52 * 1024 * 1024

NN = (((1,), (0,)), ((), ()))
NT = (((1,), (1,)), ((), ()))
TN = (((0,), (0,)), ((), ()))


def _dot(a, b, dims=NN):
    return lax.dot_general(a, b, dims, preferred_element_type=F32)


def _params(sem):
    return pltpu.CompilerParams(dimension_semantics=sem, vmem_limit_bytes=VMEM_LIMIT)


def _split3(x):
    hi = x.astype(BF16)
    r = x - hi.astype(F32)
    mid = r.astype(BF16)
    lo = (r - mid.astype(F32)).astype(BF16)
    return hi, mid, lo


def _split3_f32(x):
    hi = x.astype(BF16).astype(F32)
    r = x - hi
    mid = r.astype(BF16).astype(F32)
    return hi, mid, r - mid


def _lane_iota(shape):
    return lax.broadcasted_iota(jnp.int32, shape, len(shape) - 1)


def _row_iota(shape):
    return lax.broadcasted_iota(jnp.int32, shape, len(shape) - 2)


def _mm(name, a, b, a_spec, b_spec, out_shape, out_spec, grid, dims, acc_shape, ex=None):
    nk = grid[2]
    if ex is not None:
        return _mm_hosting(name, a, b, a_spec, b_spec, out_shape, out_spec, grid, dims, ex)

    def body(a_ref, b_ref, o_ref, *scr):
        p = _dot(a_ref[...], b_ref[...], dims)
        if nk == 1:
            o_ref[...] = p.astype(o_ref.dtype)
        else:
            acc = scr[0]
            k = pl.program_id(2)

            @pl.when(k == 0)
            def _():
                acc[...] = p

            @pl.when(k > 0)
            def _():
                acc[...] += p

            @pl.when(k == nk - 1)
            def _():
                o_ref[...] = acc[...].astype(o_ref.dtype)

    return pl.pallas_call(
        body, name=name, grid=grid, in_specs=[a_spec, b_spec], out_specs=out_spec, out_shape=out_shape,
        scratch_shapes=[pltpu.VMEM(acc_shape, F32)] if nk > 1 else [],
        compiler_params=_params(("parallel", "parallel", "arbitrary")),
    )(a, b)


def _mm_hosting(name, a, b, a_spec, b_spec, out_shape, out_spec, grid, dims, ex):
    assert grid[2] == 1
    n = len(ex.ins)

    def body(*refs):
        i, j = pl.program_id(0), pl.program_id(1)
        last = (i == grid[0] - 1) & (j == grid[1] - 1)
        (a_ref, b_ref), (o_ref,), _, begin, end = _hosted(ex, refs, 2, 1, (i == 0) & (j == 0), last, last)
        begin()
        o_ref[...] = _dot(a_ref[...], b_ref[...], dims).astype(o_ref.dtype)
        end()

    res = pl.pallas_call(
        body, name=name, grid=grid, in_specs=[a_spec, b_spec] + [ANY] * n, out_specs=[out_spec] + [ANY] * n,
        out_shape=[out_shape] + ex.out_shapes, scratch_shapes=ex.scratch(),
        compiler_params=_params(("arbitrary", "arbitrary", "arbitrary")),
    )(a, b, *ex.ins)
    return res[0], res[1:]


def _mm_nn(name, a, b, out_dtype, tm, tn):
    m, k = a.shape
    n = b.shape[1]
    return _mm(name, a, b, pl.BlockSpec((tm, k), lambda i, j, kk: (i, 0)), pl.BlockSpec((k, tn), lambda i, j, kk: (0, j)),
               jax.ShapeDtypeStruct((m, n), out_dtype), pl.BlockSpec((tm, tn), lambda i, j, kk: (i, j)),
               (m // tm, n // tn, 1), NN, (tm, tn))


def _mm_nt(name, a, b, out_dtype, tm, tn, ex=None):
    m, k = a.shape
    n = b.shape[0]
    return _mm(name, a, b, pl.BlockSpec((tm, k), lambda i, j, kk: (i, 0)), pl.BlockSpec((tn, k), lambda i, j, kk: (j, 0)),
               jax.ShapeDtypeStruct((m, n), out_dtype), pl.BlockSpec((tm, tn), lambda i, j, kk: (i, j)),
               (m // tm, n // tn, 1), NT, (tm, tn), ex)


def _mm_tn(name, a, b, tka, tn, ex=None):
    t, ka = a.shape
    n = b.shape[1]
    return _mm(name, a, b, pl.BlockSpec((t, tka), lambda i, j, kk: (0, i)), pl.BlockSpec((t, tn), lambda i, j, kk: (0, j)),
               jax.ShapeDtypeStruct((ka, n), F32), pl.BlockSpec((tka, tn), lambda i, j, kk: (i, j)),
               (ka // tka, n // tn, 1), TN, (tka, tn), ex)


def _d_h3(dhid, w_up, ex):
    tm = tn = 1024
    shard = 2 * D_FF // N_CHIPS
    per_plane = D_FF // shard
    grid = (S // tm, D // tn, N_CHIPS)
    n = len(ex.ins)

    def body(*refs):
        i, j, k = pl.program_id(0), pl.program_id(1), pl.program_id(2)
        first = (i == 0) & (j == 0) & (k == 0)
        last = (i == grid[0] - 1) & (j == grid[1] - 1) & (k == N_CHIPS - 1)
        (a_ref, b_ref), (o_ref,), (acc_ref,), begin, end = _hosted(ex, refs, 2, 1, first, first, last)
        begin()
        part = _dot(a_ref[...], b_ref[...], NT)

        @pl.when(k == 0)
        def _():
            acc_ref[...] = part

        @pl.when(k > 0)
        def _():
            acc_ref[...] += part

        @pl.when(k == N_CHIPS - 1)
        def _():
            o_ref[...] = acc_ref[...]

        end()

    res = pl.pallas_call(
        body, name="d_h3", grid=grid,
        in_specs=[pl.BlockSpec((None, tm, shard), lambda i, j, k: (k // per_plane, i, k % per_plane)),
                  pl.BlockSpec((None, tn, shard), lambda i, j, k: (k, j, 0))] + [ANY] * n,
        out_specs=[pl.BlockSpec((tm, tn), lambda i, j, k: (i, j))] + [ANY] * n,
        out_shape=[jax.ShapeDtypeStruct((S, D), F32)] + ex.out_shapes,
        scratch_shapes=[pltpu.VMEM((tm, tn), F32)] + ex.scratch(),
        compiler_params=_params(("arbitrary", "arbitrary", "arbitrary")),
    )(dhid, w_up, *ex.ins)
    return res[0], res[1:]


SHARD_IN = D_IN // N_CHIPS
SHARD_IN_PAD = -(-SHARD_IN // SUBLANES) * SUBLANES


def _dw_in(dproj, h1, ex):
    tk = 1024
    nk = S // tk
    half = D // 2
    starts = [SHARD_IN * j // LANES * LANES for j in range(N_CHIPS)]
    shifts = [SHARD_IN * j - s for j, s in enumerate(starts)]
    window = -(-(max(shifts) + SHARD_IN) // LANES) * LANES
    assert starts[-1] + window <= dproj.shape[1]
    n = len(ex.ins)

    def body(*refs):
        k = pl.program_id(0)
        (a_ref, b_ref), (o_ref,), _, begin, end = _hosted(ex, refs, 2, 1, k == 0, k == nk - 1, k == nk - 1)
        begin()

        @pl.when(k == 0)
        def _():
            o_ref[...] = jnp.zeros(o_ref.shape, F32)

        for j in range(N_CHIPS):
            win = a_ref[:, starts[j]:starts[j] + window]
            if shifts[j]:
                win = pltpu.roll(win, window - shifts[j], axis=1)
            part = _dot(win, b_ref[...], TN)
            for h in range(2):
                o_ref[j, h] += part[:SHARD_IN_PAD, h * half:(h + 1) * half]
        end()

    out_shape = (N_CHIPS, 2, SHARD_IN_PAD, half)
    res = pl.pallas_call(
        body, name="dw_in", grid=(nk,),
        in_specs=[pl.BlockSpec((tk, dproj.shape[1]), lambda k: (k, 0)), pl.BlockSpec((tk, D), lambda k: (k, 0))] + [ANY] * n,
        out_specs=[pl.BlockSpec(out_shape, lambda k: (0, 0, 0, 0))] + [ANY] * n,
        out_shape=[jax.ShapeDtypeStruct(out_shape, F32)] + ex.out_shapes,
        scratch_shapes=ex.scratch(),
        compiler_params=_params(("arbitrary",)),
    )(dproj, h1, *ex.ins)
    return res[0], res[1:]


def _rms(x, g):
    r = lax.rsqrt(jnp.mean(x * x, axis=-1, keepdims=True) + EPS)
    return x * r * g


def _rms_bwd(x, g, dy):
    r = lax.rsqrt(jnp.mean(x * x, axis=-1, keepdims=True) + EPS)
    xh = x * r
    dxh = dy * g
    dx = r * (dxh - xh * jnp.mean(dxh * xh, axis=-1, keepdims=True))
    return dx, jnp.sum(dy * xh, axis=0, keepdims=True)


def _row_spec(tr, width):
    return pl.BlockSpec((tr, width), lambda i: (i, 0))


def _vec_spec(width):
    return pl.BlockSpec((1, width), lambda i: (0, 0))


def _norm_fwd(name, x, g, ex=None):
    rows, width = x.shape
    tr = min(TR, rows)
    steps = rows // tr
    hosted = ex if ex is not None else _no_exchange()
    n = len(hosted.ins)

    def body(*refs):
        i = pl.program_id(0)
        (x_ref, g_ref), (h_ref,), _, begin, end = _hosted(hosted, refs, 2, 1, i == 0, i == steps - 1, i == steps - 1)
        begin()
        h_ref[...] = _rms(x_ref[...], g_ref[...]).astype(BF16)
        end()

    res = pl.pallas_call(
        body, name=name, grid=(steps,), in_specs=[_row_spec(tr, width), _vec_spec(width)] + [ANY] * n,
        out_specs=[_row_spec(tr, width)] + [ANY] * n,
        out_shape=[jax.ShapeDtypeStruct((rows, width), BF16)] + hosted.out_shapes, scratch_shapes=hosted.scratch(),
        compiler_params=_params(("arbitrary",)),
    )(x, g, *hosted.ins)
    return res[0] if ex is None else (res[0], res[1:])


def _proj_resid_norm(name, a, w, xp, g_post, g_pre, w_next=None):
    nt = S // TR

    def body(a_ref, w_ref, xp_hbm, gpost_ref, gpre_ref, *rest):
        xp_buf, xp_sem = rest[-2:]
        rest = rest[:-2]
        y_ref, xn_ref, h_ref = rest[-3:] if w_next is None else rest[1:4]
        i = pl.program_id(0)

        def fetch(step):
            slot = step % RING
            return pltpu.make_async_copy(xp_hbm.at[pl.ds(pl.multiple_of(step * TR, TR), TR)], xp_buf.at[slot], xp_sem.at[slot])

        @pl.when(i == 0)
        def _():
            for step in range(RING - 1):
                fetch(step).start()

        @pl.when(i + RING - 1 < nt)
        def _():
            fetch(i + RING - 1).start()

        y = _dot(a_ref[...], w_ref[...])
        y_ref[...] = y
        fetch(i).wait()
        xn = xp_buf[i % RING] + _rms(y, gpost_ref[...])
        xn_ref[...] = xn
        h = _rms(xn, gpre_ref[...]).astype(BF16)
        h_ref[...] = h
        if w_next is not None:
            rest[4][...] = _dot(h, rest[0][...]).astype(BF16)

    mat = pl.BlockSpec((D, D), lambda i: (0, 0))
    more = [] if w_next is None else [w_next]
    return pl.pallas_call(
        body, name=name, grid=(nt,),
        in_specs=[_row_spec(TR, D), mat, ANY, _vec_spec(D), _vec_spec(D)] + [mat] * len(more),
        out_specs=[_row_spec(TR, D)] * (3 + len(more)),
        out_shape=[jax.ShapeDtypeStruct((S, D), F32), jax.ShapeDtypeStruct((S, D), F32), jax.ShapeDtypeStruct((S, D), BF16)]
        + [jax.ShapeDtypeStruct((S, D), BF16)] * len(more),
        scratch_shapes=[pltpu.VMEM((RING, TR, D), F32), pltpu.SemaphoreType.DMA((RING,))],
        compiler_params=_params(("arbitrary",)),
    )(a, w, xp, g_post, g_pre, *more)


def _down_loss_bwd(act, w_down, x3, g_post, target):
    def body(a_ref, w_ref, x_ref, g_ref, t_ref, dres_ref, dy_ref, dg_ref, loss_ref):
        i = pl.program_id(0)

        @pl.when(i == 0)
        def _():
            dg_ref[...] = jnp.zeros_like(dg_ref)
            loss_ref[...] = jnp.zeros_like(loss_ref)

        g = g_ref[...]
        for r in range(ROW_PIECES):
            rows = slice(r * TR // ROW_PIECES, (r + 1) * TR // ROW_PIECES)
            y = _dot(a_ref[rows, :], w_ref[...])
            e = x_ref[rows, :] + _rms(y, g) - t_ref[rows, :]
            loss_ref[...] += jnp.sum(e * e, axis=0, keepdims=True) * (0.5 / D)
            dres = e * (1.0 / D)
            dres_ref[rows, :] = dres
            dy, dg = _rms_bwd(y, g, dres)
            dy_ref[rows, :] = dy.astype(BF16)
            dg_ref[...] += dg

    return pl.pallas_call(
        body, name="down_loss_bwd", grid=(S // TR,),
        in_specs=[_row_spec(TR, D_FF), pl.BlockSpec((D_FF, D), lambda i: (0, 0)), _row_spec(TR, D), _vec_spec(D),
                  _row_spec(TR, D)],
        out_specs=[_row_spec(TR, D), _row_spec(TR, D), _vec_spec(D), _vec_spec(D)],
        out_shape=[jax.ShapeDtypeStruct((S, D), F32), jax.ShapeDtypeStruct((S, D), BF16),
                   jax.ShapeDtypeStruct((1, D), F32), jax.ShapeDtypeStruct((1, D), F32)],
        compiler_params=_params(("arbitrary",)),
    )(act, w_down, x3, g_post, target)


def _mid_bwd(name, dres, xcur, g_pre, dh, yprev, g_post, w):
    def body(dres_ref, x_ref, gpre_ref, dh_ref, y_ref, gpost_ref, w_ref, dx_ref, dy_ref, da_ref, dgpre_ref, dgpost_ref):
        i = pl.program_id(0)

        @pl.when(i == 0)
        def _():
            dgpre_ref[...] = jnp.zeros_like(dgpre_ref)
            dgpost_ref[...] = jnp.zeros_like(dgpost_ref)

        dxn, dgpre = _rms_bwd(x_ref[...], gpre_ref[...], dh_ref[...])
        dx = dres_ref[...] + dxn
        dx_ref[...] = dx
        dy, dgpost = _rms_bwd(y_ref[...], gpost_ref[...], dx)
        dy = dy.astype(BF16)
        dy_ref[...] = dy
        da_ref[...] = _dot(dy, w_ref[...], NT).astype(BF16)
        dgpre_ref[...] += dgpre
        dgpost_ref[...] += dgpost

    return pl.pallas_call(
        body, name=name, grid=(S // TR,),
        in_specs=[_row_spec(TR, D), _row_spec(TR, D), _vec_spec(D), _row_spec(TR, D), _row_spec(TR, D), _vec_spec(D),
                  pl.BlockSpec((D, D), lambda i: (0, 0))],
        out_specs=[_row_spec(TR, D), _row_spec(TR, D), _row_spec(TR, D), _vec_spec(D), _vec_spec(D)],
        out_shape=[jax.ShapeDtypeStruct((S, D), F32), jax.ShapeDtypeStruct((S, D), BF16), jax.ShapeDtypeStruct((S, D), BF16),
                   jax.ShapeDtypeStruct((1, D), F32), jax.ShapeDtypeStruct((1, D), F32)],
        compiler_params=_params(("arbitrary",)),
    )(dres, xcur, g_pre, dh, yprev, g_post, w)


def _d_h1_first_bwd(dproj, w_in, dres, x, g, ex):
    nt = S // TR
    n = len(ex.ins)

    def body(*refs):
        i = pl.program_id(0)
        (dp_ref, w_ref, dres_ref, x_ref, g_ref), (dx_ref, dg_ref), _, begin, end = _hosted(
            ex, refs, 5, 2, i == 0, i == nt - 1, i == nt - 1)
        begin()

        @pl.when(i == 0)
        def _():
            dg_ref[...] = jnp.zeros_like(dg_ref)

        dxn, dg = _rms_bwd(x_ref[...], g_ref[...], _dot(dp_ref[...], w_ref[...], NT))
        dx_ref[...] = dres_ref[...] + dxn
        dg_ref[...] += dg
        end()

    res = pl.pallas_call(
        body, name="d_h1", grid=(nt,),
        in_specs=[_row_spec(TR, D_IN_PAD), pl.BlockSpec((D, D_IN_PAD), lambda i: (0, 0)), _row_spec(TR, D), _row_spec(TR, D),
                  _vec_spec(D)] + [ANY] * n,
        out_specs=[_row_spec(TR, D), _vec_spec(D)] + [ANY] * n,
        out_shape=[jax.ShapeDtypeStruct((S, D), F32), jax.ShapeDtypeStruct((1, D), F32)] + ex.out_shapes,
        scratch_shapes=ex.scratch(), compiler_params=_params(("arbitrary",)),
    )(dproj, w_in, dres, x, g, *ex.ins)
    return res[0], res[1], res[2:]


def _gain_bwd(name, x, g, dy):
    rows, width = x.shape

    def body(x_ref, g_ref, dy_ref, dg_ref):
        _, dg = _rms_bwd(x_ref[...], g_ref[...], dy_ref[...])
        dg_ref[...] = dg

    return pl.pallas_call(
        body, name=name, grid=(1,), in_specs=[_row_spec(rows, width), _vec_spec(width), _row_spec(rows, width)],
        out_specs=_vec_spec(width), out_shape=jax.ShapeDtypeStruct((1, width), F32),
        compiler_params=_params(("arbitrary",)),
    )(x, g, dy)


CUM_Q = DH
CUM_K = DH + 3
LSE_Q = DH + 6
BOTH_ONE = DH + 9
DEN_V = DH
DELTA = DH + 1
PREP_TR = 256
PIECE_LANES = 16
FOX_FWD_BLOCK = 1024
FOX_BWD_BLOCK = 512


def _at(lane_of_even_head, h):
    return (lane_of_even_head + DH * (h % 2)) % LANES


def _data_lanes(lane, h):
    return lane >= DH if h % 2 else lane < DH


def _pair_block(ref, off, h):
    base = ((off + DH * h) // LANES) * LANES
    return ref[:, base:base + LANES]


def _cumsum_rows(x, tri, carry):
    hi, mid, lo = _split3(x)
    return _dot(tri, hi) + _dot(tri, mid) + _dot(tri, lo) + carry


def _in_proj(h1, w_in, bf_pad):
    tr = TR

    place_q = np.zeros((LANES, HEADS * LANES), np.float32)
    place_k = np.zeros((LANES, HEADS * LANES), np.float32)
    for h in range(HEADS):
        for piece in range(3):
            place_q[PIECE_LANES * piece + h, LANES * h + _at(CUM_Q, h) + piece] = 1.0
            place_k[PIECE_LANES * piece + h, LANES * h + _at(CUM_K, h) + piece] = -1.0

    def body(h_ref, w_ref, bf_ref, pq_ref, pk_ref, qa_ref, ka_ref, va_ref, u_ref, z_ref, carry_ref):
        i = pl.program_id(0)

        @pl.when(i == 0)
        def _():
            carry_ref[...] = jnp.zeros_like(carry_ref)

        proj = _dot(h_ref[...], w_ref[...])
        u_ref[...] = proj[:, :D_POOL]
        z_ref[...] = proj[:, F_OFF:F_OFF + LANES]
        lane = _lane_iota((tr, LANES))
        z = proj[:, F_OFF:F_OFF + LANES] + bf_ref[...]
        log_f = jnp.minimum(z, 0.0) - jnp.log(1.0 + jnp.exp(-jnp.abs(z)))
        log_f = jnp.where(lane < HEADS, log_f, 0.0)
        tri = jnp.where(_row_iota((tr, tr)) >= _lane_iota((tr, tr)), 1.0, 0.0).astype(BF16)
        cum = _cumsum_rows(log_f, tri, carry_ref[0:1, :])
        carry_ref[0:1, :] = cum[tr - 1:tr, :]
        c_hi, c_mid, c_lo = _split3_f32(cum)
        pieces = (c_hi + pltpu.roll(c_mid, PIECE_LANES, 1) + pltpu.roll(c_lo, 2 * PIECE_LANES, 1)).astype(BF16)
        cum_q = _dot(pieces, pq_ref[...])
        cum_k = _dot(pieces, pk_ref[...])

        def between(first, h):
            return (lane >= _at(first, h)) & (lane < _at(first, h) + 3)

        ones_q = [jnp.where(between(CUM_K, h) | (lane == _at(BOTH_ONE, h)), 1.0, 0.0) for h in range(2)]
        ones_k = [jnp.where(between(CUM_Q, h) | between(LSE_Q, h) | (lane == _at(BOTH_ONE, h)), 1.0, 0.0) for h in range(2)]
        aug_v = [jnp.where(lane == _at(DEN_V, h), 1.0, jnp.where(between(DELTA, h), -1.0, 0.0)) for h in range(2)]
        for h in range(HEADS):
            mine = slice(LANES * h, LANES * (h + 1))
            data = _data_lanes(lane, h)
            qa_ref[h] = jnp.where(data, _pair_block(proj, Q_OFF, h) * (DH ** -0.5), cum_q[:, mine] + ones_q[h % 2]).astype(BF16)
            ka_ref[h] = jnp.where(data, _pair_block(proj, K_OFF, h), cum_k[:, mine] + ones_k[h % 2]).astype(BF16)
            va_ref[h] = jnp.where(data, _pair_block(proj, V_OFF, h), aug_v[h % 2]).astype(BF16)

    head_spec = pl.BlockSpec((HEADS, tr, LANES), lambda i: (0, i, 0))
    head_shape = jax.ShapeDtypeStruct((HEADS, S, LANES), BF16)
    place_spec = pl.BlockSpec(place_q.shape, lambda i: (0, 0))
    return pl.pallas_call(
        body, name="in_proj", grid=(S // tr,),
        in_specs=[_row_spec(tr, D), pl.BlockSpec((D, D_IN_PAD), lambda i: (0, 0)), _vec_spec(LANES), place_spec, place_spec],
        out_specs=[head_spec] * 3 + [_row_spec(tr, D_POOL), _row_spec(tr, LANES)],
        out_shape=[head_shape] * 3 + [jax.ShapeDtypeStruct((S, D_POOL), F32), jax.ShapeDtypeStruct((S, LANES), F32)],
        scratch_shapes=[pltpu.VMEM((SUBLANES, LANES), F32)], compiler_params=_params(("arbitrary",)),
    )(h1, w_in, bf_pad, jnp.asarray(place_q, BF16), jnp.asarray(place_k, BF16))


def _hosted(ex, refs, n_blocked_in, n_blocked_out, first, forward_at, last):
    n = len(ex.ins)
    own_in = refs[:n_blocked_in]
    ex_in = refs[n_blocked_in:n_blocked_in + n]
    own_out = refs[n_blocked_in + n:n_blocked_in + n + n_blocked_out]
    ex_out = refs[n_blocked_in + n + n_blocked_out:n_blocked_in + 2 * n + n_blocked_out]
    rest = refs[n_blocked_in + 2 * n + n_blocked_out:]
    args = (ex_in, ex_out, rest[-2], rest[-1])

    def begin():
        @pl.when(first)
        def _():
            ex.start(*args)

        @pl.when(forward_at)
        def _():
            ex.forward(*args)

    def end():
        @pl.when(last)
        def _():
            ex.finish(*args)

    return own_in, own_out, rest[:-2], begin, end


def _fox_fwd(qa, ka, va, ex):
    BQ = BK = FOX_FWD_BLOCK
    nq = S // BQ
    n_pairs = HEADS // 2

    def body(*refs):
        p_id, i = pl.program_id(0), pl.program_id(1)
        (qa_ref, ka_ref, va_ref), (y_ref, qab_ref), (m_scr, acc_scr), begin, end = _hosted(
            ex, refs, 3, 2, (p_id == 0) & (i == 0), (p_id == n_pairs - 1) & (i == 0), (p_id == n_pairs - 1) & (i == nq - 1))
        begin()
        lane = _lane_iota((BQ, LANES))
        causal = _row_iota((BQ, BK)) >= _lane_iota((BQ, BK))
        m_scr[...] = jnp.full_like(m_scr, NEG)
        acc_scr[...] = jnp.zeros_like(acc_scr)

        def step(j, masked):
            rows = pl.ds(pl.multiple_of(j * BK, BK), BK)
            for hh in range(2):
                s = _dot(qa_ref[hh], ka_ref[hh, rows, :], NT)
                if masked:
                    s = jnp.where(causal, s, NEG)
                m_prev = m_scr[hh]
                m_new = jnp.maximum(m_prev, jnp.max(s, axis=1, keepdims=True))
                p = jnp.exp(s - jnp.tile(m_new, (1, BK // LANES)))
                acc_scr[hh] = jnp.exp(m_prev - m_new) * acc_scr[hh] + _dot(p.astype(BF16), va_ref[hh, rows, :])
                m_scr[hh] = m_new

        def full_step(j, carry):
            step(j, False)
            return carry

        lax.fori_loop(0, i, full_step, 0)
        step(i, True)
        outs = []
        for hh in range(2):
            acc = acc_scr[hh]
            den_lane, lse_lane = _at(DEN_V, hh), _at(LSE_Q, hh)
            den = jnp.broadcast_to(acc[:, den_lane:den_lane + 1], (BQ, LANES))
            outs.append(acc * (1.0 / den))
            n_hi, n_mid, n_lo = _split3(-(m_scr[hh] + jnp.log(den)))
            qab_ref[hh] = jnp.where(lane == lse_lane, n_hi,
                                    jnp.where(lane == lse_lane + 1, n_mid, jnp.where(lane == lse_lane + 2, n_lo, qa_ref[hh])))
        y_ref[...] = jnp.where(lane < DH, outs[0], outs[1]).astype(BF16)
        end()

    pair_rows = pl.BlockSpec((2, BQ, LANES), lambda p, i: (p, i, 0))
    pair_all = pl.BlockSpec((2, S, LANES), lambda p, i: (p, 0, 0))
    n = len(ex.ins)
    res = pl.pallas_call(
        body, name="fox_fwd", grid=(n_pairs, nq), in_specs=[pair_rows, pair_all, pair_all] + [ANY] * n,
        out_specs=[pl.BlockSpec((BQ, LANES), lambda p, i: (i, D_POOL // LANES + p)), pair_rows] + [ANY] * n,
        out_shape=[jax.ShapeDtypeStruct((S, D), BF16), jax.ShapeDtypeStruct((HEADS, S, LANES), BF16)] + ex.out_shapes,
        scratch_shapes=[pltpu.VMEM((2, BQ, LANES), F32), pltpu.VMEM((2, BQ, LANES), F32)] + ex.scratch(),
        compiler_params=_params(("arbitrary", "arbitrary")),
    )(qa, ka, va, *ex.ins)
    return res[0], res[1], res[2:]


def _bwd_xa_mix(dqx, w_xq, dres, x2, g_pre, y1, g_post, w_mix_out, ycat, ex):
    steps = S // TR
    n = len(ex.ins)

    def body(*refs):
        i = pl.program_id(0)
        ((dq_ref, wq_ref, dres_ref, x_ref, gpre_ref, y_ref, gpost_ref, wm_ref, ycat_ref),
         (dx_ref, dy_ref, dgpre_ref, dgpost_ref, dp_ref, doa_ref), _, begin, end) = _hosted(
            ex, refs, 9, 6, i == 0, i == 0, i == steps - 1)
        begin()

        @pl.when(i == 0)
        def _():
            dgpre_ref[...] = jnp.zeros_like(dgpre_ref)
            dgpost_ref[...] = jnp.zeros_like(dgpost_ref)

        dxn, dgpre = _rms_bwd(x_ref[...], gpre_ref[...], _dot(dq_ref[...], wq_ref[...], NT))
        dx = dres_ref[...] + dxn
        dx_ref[...] = dx
        dy, dgpost = _rms_bwd(y_ref[...], gpost_ref[...], dx)
        dy = dy.astype(BF16)
        dy_ref[...] = dy
        dgpre_ref[...] += dgpre
        dgpost_ref[...] += dgpost

        d = _dot(dy, wm_ref[...], NT)
        dp_ref[...] = d[:, :D_POOL]
        lane = _lane_iota((TR, LANES))
        low = lane < DH
        for p in range(HEADS // 2):
            cols = slice(D_POOL + LANES * p, D_POOL + LANES * (p + 1))
            do = d[:, cols]
            prod = do * ycat_ref[:, cols].astype(F32)
            deltas = (jnp.sum(jnp.where(low, prod, 0.0), axis=1, keepdims=True),
                      jnp.sum(jnp.where(low, 0.0, prod), axis=1, keepdims=True))
            for hh in range(2):
                d_hi, d_mid, d_lo = _split3_f32(deltas[hh])
                dl = _at(DELTA, hh)
                aug = jnp.where(lane == dl, d_hi, jnp.where(lane == dl + 1, d_mid, jnp.where(lane == dl + 2, d_lo, 0.0)))
                doa_ref[2 * p + hh] = jnp.where(_data_lanes(lane, hh), do, aug).astype(BF16)
        end()

    mat = pl.BlockSpec((D, D), lambda i: (0, 0))
    res = pl.pallas_call(
        body, name="bwd_xa_mix", grid=(steps,),
        in_specs=[_row_spec(TR, D), mat, _row_spec(TR, D), _row_spec(TR, D), _vec_spec(D), _row_spec(TR, D), _vec_spec(D), mat,
                  _row_spec(TR, D)] + [ANY] * n,
        out_specs=[_row_spec(TR, D), _row_spec(TR, D), _vec_spec(D), _vec_spec(D), _row_spec(TR, D_POOL),
                   pl.BlockSpec((HEADS, TR, LANES), lambda i: (0, i, 0))] + [ANY] * n,
        out_shape=[jax.ShapeDtypeStruct((S, D), F32), jax.ShapeDtypeStruct((S, D), BF16), jax.ShapeDtypeStruct((1, D), F32),
                   jax.ShapeDtypeStruct((1, D), F32), jax.ShapeDtypeStruct((S, D_POOL), F32),
                   jax.ShapeDtypeStruct((HEADS, S, LANES), BF16)] + ex.out_shapes,
        scratch_shapes=ex.scratch(), compiler_params=_params(("arbitrary",)),
    )(dqx, w_xq, dres, x2, g_pre, y1, g_post, w_mix_out, ycat, *ex.ins)
    return res[:6], res[6:]


def _fox_bwd(qab, doa, ka, va, ex):
    BQ = BK = FOX_BWD_BLOCK
    nk = S // BK
    n_pairs = HEADS // 2

    def body(*refs):
        p_id, j = pl.program_id(0), pl.program_id(1)
        (qab_ref, doa_ref, ka_ref, va_ref), (dqa_ref, dka_ref, dva_ref), (dv_ref,), begin, end = _hosted(
            ex, refs, 4, 3, (p_id == 0) & (j == 0), (p_id == n_pairs - 1) & (j == 0), (p_id == n_pairs - 1) & (j == nk - 1))
        begin()

        @pl.when(j == 0)
        def _():
            dqa_ref[...] = jnp.zeros_like(dqa_ref)

        causal = _row_iota((BQ, BK)) >= _lane_iota((BQ, BK))
        dka_ref[...] = jnp.zeros_like(dka_ref)
        dv_ref[...] = jnp.zeros_like(dv_ref)

        def step(i, masked):
            rows = pl.ds(pl.multiple_of(i * BQ, BQ), BQ)
            for hh in range(2):
                kb = ka_ref[hh]
                q = qab_ref[hh, rows, :]
                do = doa_ref[hh, rows, :]
                s = _dot(q, kb, NT)
                if masked:
                    s = jnp.where(causal, s, NEG)
                p = jnp.exp(s)
                ds = p * _dot(do, va_ref[hh], NT)
                pb = p.astype(BF16)
                dsb = ds.astype(BF16)
                dv_ref[hh] += _dot(pb, do, TN)
                dka_ref[hh] += _dot(dsb, q, TN)
                dqa_ref[hh, rows, :] += _dot(dsb, kb)

        def full_step(i, carry):
            step(i, False)
            return carry

        step(j, True)
        lax.fori_loop(j + 1, nk, full_step, 0)
        dva_ref[...] = dv_ref[...].astype(BF16)
        end()

    pair_all = pl.BlockSpec((2, S, LANES), lambda p, j: (p, 0, 0))
    pair_rows = pl.BlockSpec((2, BK, LANES), lambda p, j: (p, j, 0))
    shape = jax.ShapeDtypeStruct((HEADS, S, LANES), F32)
    n = len(ex.ins)
    res = pl.pallas_call(
        body, name="fox_bwd", grid=(n_pairs, nk), in_specs=[pair_all, pair_all, pair_rows, pair_rows] + [ANY] * n,
        out_specs=[pair_all, pair_rows, pair_rows] + [ANY] * n,
        out_shape=[shape, shape, jax.ShapeDtypeStruct((HEADS, S, LANES), BF16)] + ex.out_shapes,
        scratch_shapes=[pltpu.VMEM((2, BK, LANES), F32)] + ex.scratch(), compiler_params=_params(("arbitrary", "arbitrary")),
    )(qab, doa, ka, va, *ex.ins)
    return res[0], res[1], res[2], res[3:]


def _fox_bwd_post(dqa, dka, dva, du, proj, bf_pad):
    tr = PREP_TR
    nt = S // tr

    pick = np.zeros((HEADS * LANES, LANES), np.float32)
    for h in range(HEADS):
        pick[LANES * h + _at(BOTH_ONE, h), h] = 1.0

    def body(dqa_ref, dka_ref, dva_ref, du_ref, z_ref, bf_ref, pick_ref, dp_ref, dbf_ref, carry_ref):
        i = pl.program_id(0)

        @pl.when(i == 0)
        def _():
            carry_ref[...] = jnp.zeros_like(carry_ref)
            dbf_ref[...] = jnp.zeros_like(dbf_ref)

        lane = _lane_iota((tr, LANES))
        diff = jnp.concatenate([dqa_ref[h] - dka_ref[h] for h in range(HEADS)], axis=1)
        hi = diff.astype(BF16)
        dcum = _dot(hi, pick_ref[...]) + _dot((diff - hi.astype(F32)).astype(BF16), pick_ref[...])
        tri =jnp.where(_lane_iota((tr, tr)) >= _row_iota((tr, tr)), 1.0, 0.0).astype(BF16)
        dlog_f = _cumsum_rows(dcum, tri, carry_ref[0:1, :])
        carry_ref[0:1, :] = dlog_f[0:1, :]
        z = z_ref[...] + bf_ref[...]
        df = jnp.where(lane < HEADS, dlog_f / (1.0 + jnp.exp(z)), 0.0)
        dbf_ref[...] += jnp.sum(df, axis=0, keepdims=True)

        dp_ref[:, 0:D_POOL] = du_ref[...].astype(BF16)
        low = lane < DH
        for ref, off, scale in ((dqa_ref, Q_OFF, DH ** -0.5), (dka_ref, K_OFF, 1.0), (dva_ref, V_OFF, 1.0)):
            for p in range(HEADS // 2):
                blk = jnp.where(low, ref[2 * p], ref[2 * p + 1])
                dp_ref[:, off + LANES * p:off + LANES * (p + 1)] = (blk * scale).astype(BF16)
        dp_ref[:, F_OFF:F_OFF + LANES] = df.astype(BF16)

    head_spec = pl.BlockSpec((HEADS, tr, LANES), lambda i: (0, nt - 1 - i, 0))
    return pl.pallas_call(
        body, name="fox_bwd_post", grid=(nt,),
        in_specs=[head_spec, head_spec, head_spec, pl.BlockSpec((tr, D_POOL), lambda i: (nt - 1 - i, 0)),
                  pl.BlockSpec((tr, LANES), lambda i: (nt - 1 - i, 0)), _vec_spec(LANES),
                  pl.BlockSpec(pick.shape, lambda i: (0, 0))],
        out_specs=[pl.BlockSpec((tr, D_IN_PAD), lambda i: (nt - 1 - i, 0)), _vec_spec(LANES)],
        out_shape=[jax.ShapeDtypeStruct((S, D_IN_PAD), BF16), jax.ShapeDtypeStruct((1, LANES), F32)],
        scratch_shapes=[pltpu.VMEM((SUBLANES, LANES), F32)],
        compiler_params=_params(("arbitrary",)),
    )(dqa, dka, dva, du, proj, bf_pad, jnp.asarray(pick, BF16))


POOL_HALO = 16


def _by_group(lane, a2, a4, a8, a16):
    return jnp.where(lane < 64, a2, jnp.where(lane < 128, a4, jnp.where(lane < 192, a8, a16)))


def _window_count(lane, t):
    return jnp.minimum(t + 1, _by_group(lane, 2, 4, 8, 16)).astype(F32)


def _pool_diff(u, halo, first, tile):
    n = TR + POOL_HALO
    ext = jnp.concatenate([jnp.where(first, 0.0, halo), u], axis=0)
    s2 = ext + pltpu.roll(ext, 1, 0)
    s4 = s2 + pltpu.roll(s2, 2, 0)
    s8 = s4 + pltpu.roll(s4, 4, 0)
    s16 = s8 + pltpu.roll(s8, 8, 0)
    lane = _lane_iota((n, D_POOL))
    win = _by_group(lane, s2, s4, s8, s16)[POOL_HALO:]
    lane = _lane_iota((TR, D_POOL))
    t = tile * TR + _row_iota((TR, D_POOL))
    return win / _window_count(lane, t) - u


def _prev_halo(rows, width, col):
    per = TR // rows
    return pl.BlockSpec((rows, width), lambda i: (jnp.maximum(i * per - 1, 0), col))


def _next_halo(rows, width, col):
    per = TR // rows
    return pl.BlockSpec((rows, width), lambda i: (jnp.minimum((i + 1) * per, S // rows - 1), col))


def _pool_fwd(proj, w_bd, ps, ycat):
    def body(u_ref, halo_ref, w_ref, ps_ref, ycat_ref, y_ref):
        i = pl.program_id(0)
        diff = _pool_diff(u_ref[...], halo_ref[...], i == 0, i)
        y_ref[...] = (_dot(diff.astype(BF16), w_ref[...]) * ps_ref[...]).astype(BF16)

    return pl.pallas_call(
        body, name="pool_fwd", grid=(S // TR,),
        in_specs=[_row_spec(TR, D_POOL), _prev_halo(POOL_HALO, D_POOL, 0),
                  pl.BlockSpec((D_POOL, D_POOL), lambda i: (0, 0)), _vec_spec(D_POOL), ANY],
        out_specs=_row_spec(TR, D_POOL), out_shape=jax.ShapeDtypeStruct((S, D), BF16), input_output_aliases={4: 0},
        compiler_params=_params(("parallel",)),
    )(proj, proj, w_bd, ps, ycat)


def _pool_bwd(proj, dycat, w_bd, w_bd_t, ps):
    nt = S // TR
    n = TR + POOL_HALO

    def body(u_ref, halo_ref, dy_ref, dyn_ref, w_ref, wt_ref, ps_ref, du_ref, dw_ref, dps_ref):
        i = pl.program_id(0)

        @pl.when(i == 0)
        def _():
            dw_ref[...] = jnp.zeros_like(dw_ref)
            dps_ref[...] = jnp.zeros_like(dps_ref)

        diff = _pool_diff(u_ref[...], halo_ref[...], i == 0, i).astype(BF16)
        dy = dy_ref[...]
        dps_ref[...] += jnp.sum(dy * _dot(diff, w_ref[...]), axis=0, keepdims=True)
        dy_ext = jnp.concatenate([dy, jnp.where(i == nt - 1, 0.0, dyn_ref[...])], axis=0)
        dmixed = (dy_ext * ps_ref[...]).astype(BF16)
        ddiff = _dot(dmixed, wt_ref[...])
        dw_ref[...] += _dot(diff, dmixed[:TR], TN)
        lane = _lane_iota((n, D_POOL))
        t = i * TR + _row_iota((n, D_POOL))
        e = ddiff / _window_count(lane, t)
        f2 = e + pltpu.roll(e, n - 1, 0)
        f4 = f2 + pltpu.roll(f2, n - 2, 0)
        f8 = f4 + pltpu.roll(f4, n - 4, 0)
        f16 = f8 + pltpu.roll(f8, n - 8, 0)
        du_ref[...] = _by_group(lane, f2, f4, f8, f16)[:TR] - ddiff[:TR]

    mat = pl.BlockSpec((D_POOL, D_POOL), lambda i: (0, 0))
    return pl.pallas_call(
        body, name="pool_bwd", grid=(nt,),
        in_specs=[_row_spec(TR, D_POOL), _prev_halo(POOL_HALO, D_POOL, 0), _row_spec(TR, D_POOL),
                  _next_halo(POOL_HALO, D_POOL, 0), mat, mat, _vec_spec(D_POOL)],
        out_specs=[_row_spec(TR, D_POOL), mat, _vec_spec(D_POOL)],
        out_shape=[jax.ShapeDtypeStruct((S, D_POOL), F32), jax.ShapeDtypeStruct((D_POOL, D_POOL), F32),
                   jax.ShapeDtypeStruct((1, D_POOL), F32)],
        compiler_params=_params(("arbitrary",)),
    )(proj, proj, dycat, dycat, w_bd, w_bd_t, ps)


XA_GROUP = 4


def _xa_probs(q, k):
    s = _dot(q, k, NT) * (XA_DH ** -0.5)
    e = jnp.exp(s - jnp.max(s, axis=-1, keepdims=True))
    return e * (1.0 / jnp.sum(e, axis=-1, keepdims=True))


def _xattn_fwd(qx, kv):
    def body(q_ref, kv_ref, o_ref):
        for h0 in range(0, XA_HEADS, XA_GROUP):
            cols = [slice(XA_DH * h, XA_DH * (h + 1)) for h in range(h0, h0 + XA_GROUP)]
            p = [_xa_probs(q_ref[:, c], kv_ref[:, c]) for c in cols]
            for ph, c in zip(p, cols):
                o_ref[:, c] = _dot(ph.astype(BF16), kv_ref[:, D + c.start:D + c.stop]).astype(BF16)

    return pl.pallas_call(
        body, name="xattn_fwd", grid=(S // TR,),
        in_specs=[_row_spec(TR, D), pl.BlockSpec((MEM, 2 * D), lambda i: (0, 0))],
        out_specs=_row_spec(TR, D), out_shape=jax.ShapeDtypeStruct((S, D), BF16),
        compiler_params=_params(("parallel",)),
    )(qx, kv)


def _xattn_bwd(qx, kv, dxo):
    def body(q_ref, kv_ref, do_ref, dq_ref, dkv_ref):
        i = pl.program_id(0)

        @pl.when(i == 0)
        def _():
            dkv_ref[...] = jnp.zeros_like(dkv_ref)

        for h0 in range(0, XA_HEADS, XA_GROUP):
            heads = range(h0, h0 + XA_GROUP)
            cols = [slice(XA_DH * h, XA_DH * (h + 1)) for h in heads]
            vcols = [slice(D + XA_DH * h, D + XA_DH * (h + 1)) for h in heads]
            q = [q_ref[:, c] for c in cols]
            k = [kv_ref[:, c] for c in cols]
            do = [do_ref[:, c] for c in cols]
            p = [_xa_probs(qh, kh) for qh, kh in zip(q, k)]
            dp = [_dot(doh, kv_ref[:, c], NT) for doh, c in zip(do, vcols)]
            ds = [(ph * (dph - jnp.sum(ph * dph, axis=-1, keepdims=True)) * (XA_DH ** -0.5)).astype(BF16) for ph, dph in zip(p, dp)]
            for a in range(XA_GROUP):
                dkv_ref[:, vcols[a]] += _dot(p[a].astype(BF16), do[a], TN)
                dq_ref[:, cols[a]] = _dot(ds[a], k[a]).astype(BF16)
                dkv_ref[:, cols[a]] += _dot(ds[a], q[a], TN)

    kv_spec = pl.BlockSpec((MEM, 2 * D), lambda i: (0, 0))
    return pl.pallas_call(
        body, name="xattn_bwd", grid=(S // TR,), in_specs=[_row_spec(TR, D), kv_spec, _row_spec(TR, D)],
        out_specs=[_row_spec(TR, D), kv_spec],
        out_shape=[jax.ShapeDtypeStruct((S, D), BF16), jax.ShapeDtypeStruct((MEM, 2 * D), F32)],
        compiler_params=_params(("arbitrary",)),
    )(qx, kv, dxo)


CONV_HALO = SUBLANES
TC = 512
TC_FWD = 1024
GELU_K = 0.7978845608028654
GELU_C = 0.044715


def _conv3(ext, w, rows):
    h0 = ext[CONV_HALO:CONV_HALO + rows]
    h1 = pltpu.roll(ext, 1, 0)[CONV_HALO:CONV_HALO + rows]
    h2 = pltpu.roll(ext, 2, 0)[CONV_HALO:CONV_HALO + rows]
    return w[2:3] * h0 + w[1:2] * h1 + w[0:1] * h2 + w[3:4], (h2, h1, h0)


def _conv_specs(tc):
    main = pl.BlockSpec((2, TR, tc), lambda j, i: (0, i, j))
    per = TR // CONV_HALO
    prev = pl.BlockSpec((2, CONV_HALO, tc), lambda j, i: (0, jnp.maximum(i * per - 1, 0), j))
    nxt = pl.BlockSpec((2, CONV_HALO, tc), lambda j, i: (0, jnp.minimum((i + 1) * per, S // CONV_HALO - 1), j))
    par = pl.BlockSpec((2, SUBLANES, tc), lambda j, i: (0, 0, j))
    return main, prev, nxt, par


def _convgate_fwd(hid, cwb):
    tc = TC_FWD

    def body(h_ref, hp_ref, w_ref, act_ref):
        i = pl.program_id(1)
        c = []
        for g in range(2):
            ext = jnp.concatenate([jnp.where(i == 0, 0.0, hp_ref[g]), h_ref[g]], axis=0)
            c.append(_conv3(ext, w_ref[g], TR)[0])
        gate, up = c
        act_ref[...] = (jax.nn.gelu(gate, approximate=True) * up).astype(BF16)

    main, prev, _, par = _conv_specs(tc)
    return pl.pallas_call(
        body, name="convgate_fwd", grid=(D_FF // tc, S // TR), in_specs=[main, prev, par],
        out_specs=pl.BlockSpec((TR, tc), lambda j, i: (i, j)), out_shape=jax.ShapeDtypeStruct((S, D_FF), BF16),
        compiler_params=_params(("parallel", "parallel")),
    )(hid, hid, cwb)


def _convgate_bwd(hid, dact, cwb):
    nr = S // TR
    n = TR + CONV_HALO

    def body(h_ref, hp_ref, hn_ref, da_ref, dan_ref, w_ref, dh_ref, dw_ref):
        i = pl.program_id(1)

        @pl.when(i == 0)
        def _():
            dw_ref[...] = jnp.zeros_like(dw_ref)

        da = jnp.concatenate([da_ref[...], jnp.where(i == nr - 1, 0.0, dan_ref[...])], axis=0)
        c, taps = [], []
        for g in range(2):
            ext = jnp.concatenate([jnp.where(i == 0, 0.0, hp_ref[g]), h_ref[g], hn_ref[g]], axis=0)
            cg, tg = _conv3(ext, w_ref[g], n)
            c.append(cg)
            taps.append(tg)
        gate, up = c
        th = jnp.tanh(GELU_K * (gate + GELU_C * gate * gate * gate))
        gelu = 0.5 * gate * (1.0 + th)
        dgelu = 0.5 * (1.0 + th) + 0.5 * gate * (1.0 - th * th) * GELU_K * (1.0 + 3.0 * GELU_C * gate * gate)
        for g, dc in enumerate((da * up * dgelu, da * gelu)):
            w = w_ref[g]
            dh = w[2:3] * dc[:TR] + w[1:2] * pltpu.roll(dc, n - 1, 0)[:TR] + w[0:1] * pltpu.roll(dc, n - 2, 0)[:TR]
            dh_ref[g] = dh.astype(BF16)
            dcm = dc[:TR]
            for r in range(3):
                dw_ref[g, r:r + 1, :] += jnp.sum(dcm * taps[g][r][:TR], axis=0, keepdims=True)
            dw_ref[g, 3:4, :] += jnp.sum(dcm, axis=0, keepdims=True)

    main, prev, nxt, par = _conv_specs(TC)
    per = TR // CONV_HALO
    return pl.pallas_call(
        body, name="convgate_bwd", grid=(D_FF // TC, nr),
        in_specs=[main, prev, nxt, pl.BlockSpec((TR, TC), lambda j, i: (i, j)),
                  pl.BlockSpec((CONV_HALO, TC), lambda j, i: (jnp.minimum((i + 1) * per, S // CONV_HALO - 1), j)), par],
        out_specs=[main, par],
        out_shape=[jax.ShapeDtypeStruct((2, S, D_FF), BF16), jax.ShapeDtypeStruct((2, SUBLANES, D_FF), F32)],
        compiler_params=_params(("parallel", "arbitrary")),
    )(hid, hid, hid, dact, dact, cwb)


def _adam_update(w, g, m, v):
    m = ADAM_B1 * m + (1.0 - ADAM_B1) * g
    v = ADAM_B2 * v + (1.0 - ADAM_B2) * (g * g)
    m_hat = m / (1.0 - ADAM_B1 ** ADAM_STEP)
    v_hat = v / (1.0 - ADAM_B2 ** ADAM_STEP)
    return -ADAM_LR * (m_hat / (jnp.sqrt(v_hat) + ADAM_EPS) + ADAM_WD * w), m, v


def _row_tile(rows, cols, itemsize=4, target=TILE_BYTES):
    tr = SUBLANES
    while rows % (2 * tr) == 0 and 2 * tr * cols * itemsize <= target:
        tr *= 2
    assert rows % tr == 0, (rows, tr)
    return rows if rows % (2 * tr) and 16 * tr * cols * itemsize < target else tr


def _adamw(name, w, g, m, v):
    rows, cols = w.shape
    tr = rows if rows * cols * 4 <= TILE_BYTES // 2 else _row_tile(rows, cols, target=TILE_BYTES // 2)

    def body(w_ref, g_ref, m_ref, v_ref, d_ref, nm_ref, nv_ref):
        d_ref[...], nm_ref[...], nv_ref[...] = _adam_update(w_ref[...], g_ref[...], m_ref[...], v_ref[...])

    spec = _row_spec(tr, cols)
    shape = jax.ShapeDtypeStruct((rows, cols), F32)
    return pl.pallas_call(
        body, name=name, grid=(rows // tr,), in_specs=[spec] * 4, out_specs=[spec] * 3, out_shape=[shape] * 3,
        compiler_params=_params(("parallel",)),
    )(w, g, m, v)


def _adamw_halves(name, core, w, g_mine, g_sibling, m, v):
    rows, cols = w.shape
    half = rows // 2
    tr = _row_tile(half, cols, target=TILE_BYTES // 2)
    per = half // tr

    def body(core_ref, w_ref, gm_ref, gs_ref, m_ref, v_ref, g_ref, d_ref, nm_ref, nv_ref):
        g = jnp.where(pl.program_id(0) // per == core_ref[0], gm_ref[...], gs_ref[...])
        g_ref[...] = g
        d_ref[...], nm_ref[...], nv_ref[...] = _adam_update(w_ref[...], g, m_ref[...], v_ref[...])

    spec = pl.BlockSpec((tr, cols), lambda i, core_ref: (i, 0))
    half_spec = pl.BlockSpec((tr, cols), lambda i, core_ref: (i % per, 0))
    shape = jax.ShapeDtypeStruct((rows, cols), F32)
    return pl.pallas_call(
        body, name=name, out_shape=[shape] * 4,
        grid_spec=pltpu.PrefetchScalarGridSpec(
            num_scalar_prefetch=1, grid=(rows // tr,), in_specs=[spec, half_spec, half_spec, spec, spec], out_specs=[spec] * 4),
        compiler_params=_params(("parallel",)),
    )(core, w, g_mine, g_sibling, m, v)


def _adamw_halves_columns(name, core, w, g_mine, g_sibling, m, v):
    cols, _, rows = w.shape
    tl = 2 * LANES
    per = rows // 2 // tl

    def body(core_ref, w_ref, gm_ref, gs_ref, m_ref, v_ref, g_ref, d_ref, nm_ref, nv_ref):
        g = jnp.where(pl.program_id(0) // per == core_ref[0], gm_ref[...], gs_ref[...])
        g_ref[...] = g
        d_ref[...], nm_ref[...], nv_ref[...] = _adam_update(w_ref[...], g, m_ref[...], v_ref[...])

    spec = pl.BlockSpec((cols, 1, tl), lambda i, core_ref: (0, 0, i))
    half_spec = pl.BlockSpec((cols, 1, tl), lambda i, core_ref: (0, 0, i % per))
    shape = jax.ShapeDtypeStruct((cols, 1, rows), F32)
    return pl.pallas_call(
        body, name=name, out_shape=[shape] * 4,
        grid_spec=pltpu.PrefetchScalarGridSpec(
            num_scalar_prefetch=1, grid=(rows // tl,), in_specs=[spec, half_spec, half_spec, spec, spec], out_specs=[spec] * 4),
        compiler_params=_params(("parallel",)),
    )(core, w, g_mine, g_sibling, m, v)


def _chip_sum(name, core, g, other):
    _, _, half, cols = g.shape
    tr = _row_tile(half, cols)

    def body(core_ref, g_ref, o_ref, p_ref):
        p_ref[...] = (g_ref[...] + o_ref[...]).astype(BF16)

    spec = pl.BlockSpec((None, tr, cols), lambda j, i, core_ref: (j, i, 0))
    return pl.pallas_call(
        body, name=name, out_shape=jax.ShapeDtypeStruct((N_CHIPS, half, cols), BF16),
        grid_spec=pltpu.PrefetchScalarGridSpec(
            num_scalar_prefetch=1, grid=(N_CHIPS, half // tr),
            in_specs=[pl.BlockSpec((None, None, tr, cols), lambda j, i, core_ref: (j, core_ref[0], i, 0)), spec],
            out_specs=spec),
        compiler_params=_params(("parallel", "parallel")),
    )(core, g, other)


def _mesh_sum(name, chip, received, own):
    _, half, cols = received.shape
    tr = _row_tile(half, cols, itemsize=2 * N_CHIPS)

    def body(chip_ref, r_ref, own_ref, o_ref):
        acc = None
        for j in range(N_CHIPS):
            term = jnp.where(chip_ref[0] == j, own_ref[...], r_ref[j]).astype(F32)
            acc = term if acc is None else acc + term
        o_ref[...] = acc

    return pl.pallas_call(
        body, name=name, out_shape=jax.ShapeDtypeStruct((half, cols), F32),
        grid_spec=pltpu.PrefetchScalarGridSpec(
            num_scalar_prefetch=1, grid=(half // tr,),
            in_specs=[pl.BlockSpec((N_CHIPS, tr, cols), lambda i, chip_ref: (0, i, 0)),
                      pl.BlockSpec((None, tr, cols), lambda i, chip_ref: (chip_ref[0], i, 0))],
            out_specs=pl.BlockSpec((tr, cols), lambda i, chip_ref: (i, 0))),
        compiler_params=_params(("parallel",)),
    )(chip, received, own)


CHIP_FLIPS = ((1, 0), (0, 1), (1, 1))


def _place():
    x, y, c = lax.axis_index("x"), lax.axis_index("y"), lax.axis_index("c")
    return x, y, c, 2 * x + y


def _remote(src, dst, sems_s, sems_r, k, dev):
    return pltpu.make_async_remote_copy(src_ref=src, dst_ref=dst, send_sem=sems_s.at[k], recv_sem=sems_r.at[k],
                                        device_id=dev, device_id_type=MESH)


class _Exchange:
    def __init__(self, ins, out_shapes, n_sems, start, forward, finish):
        self.ins, self.out_shapes, self.n_sems = list(ins), list(out_shapes), n_sems
        self.start, self.forward, self.finish = start, forward, finish

    def scratch(self):
        return [pltpu.SemaphoreType.DMA((self.n_sems,)), pltpu.SemaphoreType.DMA((self.n_sems,))]

    def run(self, name):
        n = len(self.ins)

        def body(*refs):
            args = (refs[:n], refs[n:2 * n]) + tuple(refs[2 * n:])
            self.start(*args)
            self.forward(*args)
            self.finish(*args)

        return pl.pallas_call(
            body, name=name, in_specs=[ANY] * n, out_specs=[ANY] * n, out_shape=self.out_shapes, scratch_shapes=self.scratch(),
        )(*self.ins)


def _all_gather_weights(halved, whole):
    nh, nw = len(halved), len(whole)
    n_arr = nh + nw

    def copies(ins, outs, sems_s, sems_r):
        x, y, c, me = _place()
        sibling = (x, y, 1 - c)
        own = [_remote(ins[k], outs[k].at[me], sems_s, sems_r, k, sibling) for k in range(n_arr)]
        first, passed = [], []
        for k in range(n_arr):
            for f, (fx, fy) in enumerate(CHIP_FLIPS):
                src, dst = (ins[k].at[c], outs[k].at[me, c]) if k < nh else (ins[k], outs[k].at[me])
                first.append(_remote(src, dst, sems_s, sems_r, n_arr + 3 * k + f, (x ^ fx, y ^ fy, c)))
        for k in range(nh):
            for f, (fx, fy) in enumerate(CHIP_FLIPS):
                landed = outs[k].at[2 * (x ^ fx) + (y ^ fy), c]
                passed.append(_remote(landed, landed, sems_s, sems_r, 4 * n_arr + 3 * k + f, sibling))
        return own, first, passed

    def start(*refs):
        own, first, _ = copies(*refs)
        for cp in own + first:
            cp.start()

    def forward(*refs):
        _, first, passed = copies(*refs)
        for arrived, cp in zip(first, passed):
            arrived.wait_recv()
            cp.start()

    def finish(*refs):
        own, first, passed = copies(*refs)
        for cp in first[3 * nh:] + passed + own:
            cp.wait_recv()
        for cp in first + passed + own:
            cp.wait_send()

    shapes = [jax.ShapeDtypeStruct((N_CHIPS,) + a.shape, a.dtype) for a in list(halved) + list(whole)]
    return _Exchange(list(halved) + list(whole), shapes, 7 * nh + 4 * nw, start, forward, finish)


def _swap_halves(gs):
    n = len(gs)

    def copies(ins, outs, sems_s, sems_r):
        x, y, c, _ = _place()
        return [_remote(ins[k].at[:, 1 - c], outs[k], sems_s, sems_r, k, (x, y, 1 - c)) for k in range(n)]

    def start(*refs):
        for cp in copies(*refs):
            cp.start()

    def finish(*refs):
        for cp in copies(*refs):
            cp.wait()

    shapes = [jax.ShapeDtypeStruct((g.shape[0],) + g.shape[2:], g.dtype) for g in gs]
    return _Exchange(gs, shapes, n, start, _no_copies, finish)


def _scatter_chips(ps):
    n = len(ps)

    def copies(ins, outs, sems_s, sems_r):
        x, y, c, me = _place()
        return [_remote(ins[k].at[2 * (x ^ fx) + (y ^ fy)], outs[k].at[me], sems_s, sems_r, 3 * k + f, (x ^ fx, y ^ fy, c))
                for k in range(n) for f, (fx, fy) in enumerate(CHIP_FLIPS)]

    def start(*refs):
        for cp in copies(*refs):
            cp.start()

    def forward(*refs):
        pass

    def finish(*refs):
        for cp in copies(*refs):
            cp.wait()

    shapes = [jax.ShapeDtypeStruct(p.shape, p.dtype) for p in ps]
    return _Exchange(ps, shapes, 3 * n, start, forward, finish)


def _swap_reduced(rs):
    n = len(rs)

    def copies(ins, outs, sems_s, sems_r):
        x, y, c, _ = _place()
        return [_remote(ins[k], outs[k], sems_s, sems_r, k, (x, y, 1 - c)) for k in range(n)]

    def start(*refs):
        for cp in copies(*refs):
            cp.start()

    def finish(*refs):
        for cp in copies(*refs):
            cp.wait()

    return _Exchange(rs, [jax.ShapeDtypeStruct(r.shape, r.dtype) for r in rs], n, start, _no_copies, finish)


N_DEV = 8


def _gather_small(buf):
    def copies(ins, outs, sems_s, sems_r):
        x, y, c, chip = _place()
        sibling = (x, y, 1 - c)
        own = _remote(ins[0], outs[0].at[2 * chip + c], sems_s, sems_r, 0, sibling)
        first = [_remote(ins[0], outs[0].at[2 * chip + c], sems_s, sems_r, 1 + f, (x ^ fx, y ^ fy, c))
                 for f, (fx, fy) in enumerate(CHIP_FLIPS)]
        passed = []
        for f, (fx, fy) in enumerate(CHIP_FLIPS):
            landed = outs[0].at[2 * (2 * (x ^ fx) + (y ^ fy)) + c]
            passed.append(_remote(landed, landed, sems_s, sems_r, 4 + f, sibling))
        return own, first, passed

    def start(*refs):
        own, first, _ = copies(*refs)
        for cp in [own] + first:
            cp.start()

    def forward(*refs):
        _, first, passed = copies(*refs)
        for arrived, cp in zip(first, passed):
            arrived.wait_recv()
            cp.start()

    def finish(*refs):
        own, first, passed = copies(*refs)
        for cp in passed + [own]:
            cp.wait_recv()
        for cp in first + passed + [own]:
            cp.wait_send()

    return _Exchange([buf], [jax.ShapeDtypeStruct((N_DEV,) + buf.shape, buf.dtype)], N_DEV - 1, start, forward, finish)


def _sum_devices(place, gathered, own):
    rows = own.shape[0]

    def body(place_ref, g_ref, own_ref, o_ref):
        acc = None
        for d in range(N_DEV):
            term = jnp.where(place_ref[0] == d, own_ref[...], g_ref[d])
            acc = term if acc is None else acc + term
        o_ref[...] = acc

    return pl.pallas_call(
        body, name="sum_devices", out_shape=jax.ShapeDtypeStruct((rows, LANES), F32),
        grid_spec=pltpu.PrefetchScalarGridSpec(
            num_scalar_prefetch=1, grid=(1,),
            in_specs=[pl.BlockSpec((N_DEV, rows, LANES), lambda i, place_ref: (0, 0, 0)),
                      pl.BlockSpec((rows, LANES), lambda i, place_ref: (0, 0))],
            out_specs=pl.BlockSpec((rows, LANES), lambda i, place_ref: (0, 0))),
        compiler_params=_params(("arbitrary",)),
    )(place, gathered, own)


def _no_copies(*refs):
    pass


def _no_exchange():
    return _Exchange([], [], 1, _no_copies, _no_copies, _no_copies)


class _NoComm:
    def gather_first(self):
        return _no_exchange()

    def first_landed(self, p, landed):
        pass

    def gather_rest(self, p):
        return _no_exchange()

    def weights_landed(self, p, landed):
        pass

    def gather_last(self):
        return _no_exchange()

    def last_landed(self, p, landed):
        pass

    def swap_first(self, g):
        return _no_exchange()

    def first_swapped(self, landed):
        pass

    def swap_second(self, g):
        return _no_exchange()

    def second_swapped(self, landed):
        pass

    def scatter_early(self, g):
        return _no_exchange()

    def scatter_landed(self, landed):
        pass

    def swap_reduced_early(self):
        return _no_exchange()

    def reduced_landed(self, landed):
        pass

    def scatter_late(self, g):
        return _no_exchange()

    def late_landed(self, landed):
        pass


def _local_step(x, mem, target, p, comm):
    h1, landed = _norm_fwd("norm_mix_pre", x, p["norm_mix_pre"], comm.gather_first())
    comm.first_landed(p, landed)
    qa, ka, va, u, z = _in_proj(h1, p["w_in"], p["bf_pad"])
    ycat, qab, landed = _fox_fwd(qa, ka, va, comm.gather_rest(p))
    comm.weights_landed(p, landed)
    ycat = _pool_fwd(u, p["w_pool_bd"], p["pool_scale"], ycat)
    y1, x2, h2, qx = _proj_resid_norm("mix_out", ycat, p["w_mix_out"], x, p["norm_mix_post"], p["norm_xa_pre"], p["w_xq"])
    mem_n = _norm_fwd("norm_mem", mem, p["norm_mem"])
    kv = _mm(
        "xkv", mem_n, p["w_xkv"], pl.BlockSpec((MEM, D), lambda i, j, k: (0, 0)),
        pl.BlockSpec((None, D, 512), lambda i, j, k: (j, 0, 0)), jax.ShapeDtypeStruct((MEM, 2 * D), BF16),
        pl.BlockSpec((MEM, 512), lambda i, j, k: (0, j)), (1, N_CHIPS, 1), NN, (MEM, 512))
    xo = _xattn_fwd(qx, kv)
    y2, x3, h3 = _proj_resid_norm("xo", xo, p["w_xo"], x2, p["norm_xa_post"], p["norm_ffn_pre"])
    hid, landed = _mm(
        "up_proj", h3, p["w_up"], pl.BlockSpec((2048, D), lambda i, j, k: (i, 0)),
        pl.BlockSpec((None, D, 1024), lambda i, j, k: (j // 2, 0, j % 2)), jax.ShapeDtypeStruct((2, S, D_FF), F32),
        pl.BlockSpec((None, 2048, 1024), lambda i, j, k: (j // 4, i, j % 4)), (S // 2048, 8, 1), NN, (2048, 1024),
        comm.gather_last())
    comm.last_landed(p, landed)
    act = _convgate_fwd(hid, p["cwb"])

    g = {}
    dres, dy3, g["norm_ffn_post"], loss_cols = _down_loss_bwd(act, p["w_down"], x3, p["norm_ffn_post"], target)
    dact = _mm_nt("d_act", dy3, p["w_down"], F32, 2048, 1024)
    g["w_down"] = _mm_tn("dw_down", act, dy3, 1024, 512)
    dhid, dcwb = _convgate_bwd(hid, dact, p["cwb"])
    g["w_up"] = _mm(
        "dw_up", h3, dhid, pl.BlockSpec((S, D), lambda i, j, k: (0, 0)),
        pl.BlockSpec((None, S, 512), lambda i, j, k: (j // 8, 0, j % 8)), jax.ShapeDtypeStruct((N_CHIPS, D, 2048), F32),
        pl.BlockSpec((None, D, 512), lambda i, j, k: (j // 4, 0, j % 4)), (1, 16, 1), TN, (D, 512))
    dh3, landed = _d_h3(dhid, p["w_up"], comm.swap_first(g))
    comm.first_swapped(landed)
    dres, dy2, dxo, g["norm_ffn_pre"], g["norm_xa_post"] = _mid_bwd(
        "bwd_ffn_xa", dres, x3, p["norm_ffn_pre"], dh3, y2, p["norm_xa_post"], p["w_xo"])
    g["w_xo"] = _mm_tn("dw_xo", xo, dy2, 1024, 512)
    dqx, dkv = _xattn_bwd(qx, kv, dxo)
    dkv = dkv.astype(BF16)
    g["w_xq"] = _mm_tn("dw_xq", h2, dqx, 1024, 512)
    dmem_n = _mm(
        "d_mem", dkv, p["w_xkv"], pl.BlockSpec((MEM, 512), lambda i, j, k: (0, k)),
        pl.BlockSpec((None, D, 512), lambda i, j, k: (k, 0, 0)), jax.ShapeDtypeStruct((MEM, D), F32),
        pl.BlockSpec((MEM, D), lambda i, j, k: (0, 0)), (1, 1, N_CHIPS), NT, (MEM, D))
    g["w_xkv"] = _mm(
        "dw_xkv", mem_n, dkv, pl.BlockSpec((MEM, D), lambda i, j, k: (0, 0)),
        pl.BlockSpec((MEM, 512), lambda i, j, k: (0, j)), jax.ShapeDtypeStruct((N_CHIPS, D, 512), F32),
        pl.BlockSpec((None, D, 512), lambda i, j, k: (j, 0, 0)), (1, N_CHIPS, 1), TN, (D, 512))
    g["norm_mem"] = _gain_bwd("dg_mem", mem, p["norm_mem"], dmem_n)
    (dres, dy1, g["norm_xa_pre"], g["norm_mix_post"], dy_pool, doa), landed = _bwd_xa_mix(
        dqx, p["w_xq"], dres, x2, p["norm_xa_pre"], y1, p["norm_mix_post"], p["w_mix_out"], ycat, comm.swap_second(g))
    comm.second_swapped(landed)
    g["w_mix_out"] = _mm_tn("dw_mix_out", ycat, dy1, 1024, 512)
    dqa, dka, dva, landed = _fox_bwd(qab, doa, ka, va, comm.scatter_early(g))
    comm.scatter_landed(landed)
    du, g["w_pool_full"], g["pool_scale"] = _pool_bwd(u, dy_pool, p["w_pool_bd"], p["w_pool_bd_t"], p["pool_scale"])
    dproj, g["bf_pad"] = _fox_bwd_post(dqa, dka, dva, du, z, p["bf_pad"])
    g["w_in"], landed = _dw_in(dproj, h1, comm.swap_reduced_early())
    comm.reduced_landed(landed)
    grad_x, g["norm_mix_pre"], landed = _d_h1_first_bwd(dproj, p["w_in"], dres, x, p["norm_mix_pre"], comm.scatter_late(g))
    comm.late_landed(landed)
    g["cwb"] = dcwb
    return grad_x, g, loss_cols


BIG = ("w_in", "w_mix_out", "w_xq", "w_xkv", "w_xo", "w_up", "w_down")
ROW_SHARDED = ("w_mix_out", "w_xq", "w_xo", "w_down")
SMALL = ("norm_mix_pre", "norm_mix_post", "b_forget", "w_pool", "pool_scale", "norm_mem", "norm_xa_pre", "norm_xa_post",
         "norm_ffn_pre", "norm_ffn_post", "conv_b")
ORDER = ("norm_mix_pre", "norm_mix_post", "w_in", "b_forget", "w_pool", "pool_scale", "w_mix_out", "norm_mem", "norm_xa_pre",
         "norm_xa_post", "w_xq", "w_xkv", "w_xo", "norm_ffn_pre", "norm_ffn_post", "w_up", "conv_w", "conv_b", "w_down")
SLOT = SUBLANES * LANES


def _pack(parts):
    rows, offs, off = [], [], 0
    for a in parts:
        flat = a.reshape(-1).astype(F32)
        n = -(-flat.shape[0] // SLOT) * SLOT
        rows.append(jnp.pad(flat, (0, n - flat.shape[0])).reshape(n // LANES, LANES))
        offs.append(off)
        off += n // LANES
    return jnp.concatenate(rows, axis=0), offs


def _unpack(buf, off, like):
    n = like.size
    rows = -(-n // LANES)
    return buf[off:off + rows].reshape(-1)[:n].reshape(like.shape)


FIRST = ("w_in",)
REST = ("w_mix_out", "w_xq", "w_xkv", "w_xo", "w_up")
LAST = ("w_down",)


def _local_params(w):
    w_pool_bd = jnp.zeros((D_POOL, D_POOL), F32)
    for gi in range(4):
        w_pool_bd = w_pool_bd.at[64 * gi:64 * (gi + 1), 64 * gi:64 * (gi + 1)].set(w["w_pool"][0, gi])
    p = {n: w[n] for n in ("norm_mix_pre", "norm_mix_post", "norm_mem", "norm_xa_pre", "norm_xa_post", "norm_ffn_pre",
                           "norm_ffn_post")}
    p.update(
        bf_pad=jnp.pad(w["b_forget"], ((0, 0), (0, LANES - HEADS))),
        w_pool_bd=w_pool_bd.astype(BF16), w_pool_bd_t=w_pool_bd.T.astype(BF16), pool_scale=w["pool_scale"].reshape(1, D_POOL))
    return p


def _w_in_param(stacked):
    n, rows, cols = stacked.shape
    tr = PREP_TR

    def body(w_ref, o_ref):
        o_ref[...] = jnp.concatenate([w_ref[j] for j in range(n)] + [jnp.zeros((tr, D_IN_PAD - n * cols), BF16)], axis=1)

    return pl.pallas_call(
        body, name="w_in_whole", grid=(rows // tr,), in_specs=[pl.BlockSpec((n, tr, cols), lambda i: (0, i, 0))],
        out_specs=_row_spec(tr, D_IN_PAD), out_shape=jax.ShapeDtypeStruct((rows, D_IN_PAD), BF16),
        compiler_params=_params(("parallel",)),
    )(stacked)


def _rest_params(w, full, conv_w_full):
    cw2 = conv_w_full.reshape(3, 2, D_FF).transpose(1, 0, 2)
    cwb = jnp.concatenate([cw2, w["conv_b"].reshape(1, 2, D_FF).transpose(1, 0, 2), jnp.zeros((2, 4, D_FF), F32)], axis=1)
    return dict(w_mix_out=full["w_mix_out"].reshape(D, D), w_xq=full["w_xq"].reshape(D, D), w_xkv=full["w_xkv"],
                w_xo=full["w_xo"].reshape(D, D), w_up=full["w_up"], cwb=cwb)


def _whole_params(w, full, conv_w_full):
    p = _local_params(w)
    p.update(_rest_params(w, full, conv_w_full), w_in=_w_in_param(full["w_in"]), w_down=full["w_down"].reshape(D_FF, D))
    return p


def _halved(a):
    return a.reshape(a.shape[:-2] + (2, a.shape[-2] // 2, a.shape[-1]))


class _StepComm:
    def __init__(self, w, shard2d, conv_w, core_id, chip_id):
        self.w, self.shard2d, self.conv_w, self.core_id, self.chip_id = w, shard2d, conv_w, core_id, chip_id
        self.first, self.second = ("w_up", "w_down"), ("w_xq", "w_xkv", "w_xo")
        self.early = self.first + self.second
        self.late = ("w_in", "w_mix_out")

    def gather_first(self):
        return _all_gather_weights([_halved(self.shard2d[n].astype(BF16)) for n in FIRST], [])

    def first_landed(self, p, landed):
        p["w_in"] = _w_in_param(landed[0].reshape((N_CHIPS,) + self.shard2d["w_in"].shape))

    def gather_rest(self, p):
        return _all_gather_weights([_halved(self.shard2d[n].astype(BF16)) for n in REST], [self.conv_w.reshape(3, -1)])

    def weights_landed(self, p, landed):
        full = {n: a.reshape((N_CHIPS,) + self.shard2d[n].shape) for n, a in zip(REST, landed)}
        conv_w_full = jnp.transpose(landed[-1], (1, 0, 2)).reshape(3, 2 * D_FF)
        p.update(_rest_params(self.w, full, conv_w_full))

    def gather_last(self):
        return _all_gather_weights([_halved(self.shard2d[n].astype(BF16)) for n in LAST], [])

    def last_landed(self, p, landed):
        p["w_down"] = landed[0].reshape(D_FF, D)

    def _view(self, g, n):
        return _halved(g[n].reshape((N_CHIPS,) + self.shard2d[n].shape))

    def swap_first(self, g):
        return _swap_halves([self._view(g, n) for n in self.first])

    def first_swapped(self, landed):
        self.from_sibling = dict(zip(self.first, landed))

    def swap_second(self, g):
        return _swap_halves([self._view(g, n) for n in self.second])

    def second_swapped(self, landed):
        self.from_sibling.update(zip(self.second, landed))

    def scatter_early(self, g):
        self.partial = [_chip_sum("chip_sum_" + n, self.core_id, self._view(g, n), self.from_sibling[n]) for n in self.early]
        return _scatter_chips(self.partial)

    def scatter_landed(self, landed):
        self.received = list(landed)

    def swap_reduced_early(self):
        self.reduced = [_mesh_sum("mesh_sum_" + n, self.chip_id, r, own)
                        for n, r, own in zip(self.early, self.received, self.partial)]
        return _swap_reduced(self.reduced)

    def reduced_landed(self, landed):
        self.reduced_sibling = list(landed)

    def scatter_late(self, g):
        views = [g["w_in"], self._view(g, "w_mix_out")]
        from_sibling = _swap_halves(views).run("swap_halves_late")
        self.partial_late = [_chip_sum("chip_sum_" + n, self.core_id, view, other)
                             for n, view, other in zip(self.late, views, from_sibling)]
        return _scatter_chips(self.partial_late)

    def late_landed(self, landed):
        self.received_late = list(landed)


def kernel(x, mem, norm_mix_pre, norm_mix_post, w_in, b_forget, w_pool, pool_scale, w_mix_out, norm_mem, norm_xa_pre, norm_xa_post, w_xq, w_xkv, w_xo, norm_ffn_pre, norm_ffn_post, w_up, conv_w, conv_b, w_down, loss_target, m_norm_mix_pre, m_norm_mix_post, m_w_in, m_b_forget, m_w_pool, m_pool_scale, m_w_mix_out, m_norm_mem, m_norm_xa_pre, m_norm_xa_post, m_w_xq, m_w_xkv, m_w_xo, m_norm_ffn_pre, m_norm_ffn_post, m_w_up, m_conv_w, m_conv_b, m_w_down, v_norm_mix_pre, v_norm_mix_post, v_w_in, v_b_forget, v_w_pool, v_pool_scale, v_w_mix_out, v_norm_mem, v_norm_xa_pre, v_norm_xa_post, v_w_xq, v_w_xkv, v_w_xo, v_norm_ffn_pre, v_norm_ffn_post, v_w_up, v_conv_w, v_conv_b, v_w_down):
    w = dict(norm_mix_pre=norm_mix_pre, norm_mix_post=norm_mix_post, w_in=w_in, b_forget=b_forget, w_pool=w_pool,
             pool_scale=pool_scale, w_mix_out=w_mix_out, norm_mem=norm_mem, norm_xa_pre=norm_xa_pre, norm_xa_post=norm_xa_post,
             w_xq=w_xq, w_xkv=w_xkv, w_xo=w_xo, norm_ffn_pre=norm_ffn_pre, norm_ffn_post=norm_ffn_post, w_up=w_up,
             conv_w=conv_w, conv_b=conv_b, w_down=w_down)
    m = dict(norm_mix_pre=m_norm_mix_pre, norm_mix_post=m_norm_mix_post, w_in=m_w_in, b_forget=m_b_forget, w_pool=m_w_pool,
             pool_scale=m_pool_scale, w_mix_out=m_w_mix_out, norm_mem=m_norm_mem, norm_xa_pre=m_norm_xa_pre,
             norm_xa_post=m_norm_xa_post, w_xq=m_w_xq, w_xkv=m_w_xkv, w_xo=m_w_xo, norm_ffn_pre=m_norm_ffn_pre,
             norm_ffn_post=m_norm_ffn_post, w_up=m_w_up, conv_w=m_conv_w, conv_b=m_conv_b, w_down=m_w_down)
    v = dict(norm_mix_pre=v_norm_mix_pre, norm_mix_post=v_norm_mix_post, w_in=v_w_in, b_forget=v_b_forget, w_pool=v_w_pool,
             pool_scale=v_pool_scale, w_mix_out=v_w_mix_out, norm_mem=v_norm_mem, norm_xa_pre=v_norm_xa_pre,
             norm_xa_post=v_norm_xa_post, w_xq=v_w_xq, w_xkv=v_w_xkv, w_xo=v_w_xo, norm_ffn_pre=v_norm_ffn_pre,
             norm_ffn_post=v_norm_ffn_post, w_up=v_w_up, conv_w=v_conv_w, conv_b=v_conv_b, w_down=v_w_down)
    chip = 2 * lax.axis_index("x") + lax.axis_index("y")

    core_id = lax.axis_index("c").astype(jnp.int32).reshape(1)
    chip_id = chip.astype(jnp.int32).reshape(1)

    shard2d = {n: w[n][0] for n in BIG}
    p = _local_params(w)
    comm = _StepComm(w, shard2d, conv_w, core_id, chip_id)
    grad_x, g, loss_cols = _local_step(x[0], mem[0], loss_target[0], p, comm)

    reduced_late = [_mesh_sum("mesh_sum_" + n, chip_id, r, own)
                    for n, r, own in zip(comm.late, comm.received_late, comm.partial_late)]
    names = comm.late + comm.early
    reduced = reduced_late + comm.reduced
    reduced_sibling = list(_swap_reduced(reduced_late).run("swap_reduced_late")) + comm.reduced_sibling
    grads = {}

    gw_pool = jnp.stack([g["w_pool_full"][64 * gi:64 * (gi + 1), 64 * gi:64 * (gi + 1)] for gi in range(4)])
    dcwb = g["cwb"]
    g_conv_w = dcwb[:, 0:3, :].transpose(1, 0, 2).reshape(3, 2 * D_FF)
    g_conv_b = dcwb[:, 3, :].reshape(2 * D_FF)
    small_g = dict(norm_mix_pre=g["norm_mix_pre"], norm_mix_post=g["norm_mix_post"], b_forget=g["bf_pad"][:, :HEADS],
                   w_pool=gw_pool, pool_scale=g["pool_scale"], norm_mem=g["norm_mem"], norm_xa_pre=g["norm_xa_pre"],
                   norm_xa_post=g["norm_xa_post"], norm_ffn_pre=g["norm_ffn_pre"], norm_ffn_post=g["norm_ffn_post"],
                   conv_b=g_conv_b)
    local_buf, offs = _pack([small_g[n] for n in SMALL] + [g_conv_w, loss_cols])

    delta, new_m, new_v = {}, {}, {}
    for n, g_mine, g_sibling in zip(names, reduced, reduced_sibling):
        cols = shard2d[n].shape[1]
        if cols % LANES:
            outs = _adamw_halves_columns("adamw_" + n, core_id, jnp.transpose(w[n], (2, 0, 1)), g_mine[:cols, None, :],
                                         g_sibling[:cols, None, :], jnp.transpose(m[n], (2, 0, 1)), jnp.transpose(v[n], (2, 0, 1)))
            gn, d, nm, nv = (jnp.transpose(o, (1, 2, 0)) for o in outs)
        else:
            gn, d, nm, nv = (o[None] for o in _adamw_halves("adamw_" + n, core_id, shard2d[n], g_mine, g_sibling, m[n][0], v[n][0]))
        grads[n], delta[n], new_m[n], new_v[n] = gn, d, nm, nv
    place = (2 * chip + lax.axis_index("c")).astype(jnp.int32).reshape(1)
    buf = _sum_devices(place, _gather_small(local_buf).run("gather_small")[0], local_buf)
    for n, off in zip(SMALL, offs):
        grads[n] = _unpack(buf, off, w[n])
    g_conv_w = _unpack(buf, offs[len(SMALL)], g_conv_w)
    grads["conv_w"] = lax.dynamic_slice_in_dim(g_conv_w, chip * (2 * D_FF // N_CHIPS), 2 * D_FF // N_CHIPS, axis=1).reshape(conv_w.shape)
    loss = jnp.sum(_unpack(buf, offs[len(SMALL) + 1], loss_cols))
    small_names = SMALL + ("conv_w",)
    packed = [_pack([d[n] for n in small_names])[0] for d in (w, grads, m, v)]
    offs = _pack([w[n] for n in small_names])[1]
    d, nm, nv = _adamw("adamw_small", *packed)
    for n, off in zip(small_names, offs):
        delta[n], new_m[n], new_v[n] = _unpack(d, off, w[n]), _unpack(nm, off, w[n]), _unpack(nv, off, w[n])

    return (loss, grad_x[None], *[grads[n] for n in ORDER], *[delta[n] for n in ORDER], *[new_m[n] for n in ORDER],
            *[new_v[n] for n in ORDER])
```

```python
import functools

import jax
import jax.numpy as jnp
import numpy as np
from jax import lax
from jax.experimental import pallas as pl
from jax.experimental.pallas import tpu as pltpu

F32 = jnp.float32
BF16 = jnp.bfloat16
MESH = pl.DeviceIdType.MESH
ANY = pl.BlockSpec(memory_space=pl.ANY)
VMEM_SPEC = pl.BlockSpec(memory_space=pltpu.VMEM)

S = 4096
D = 1024
MEM = 256
D_POOL = 256
HEADS = 12
DH = 64
D_FOX = HEADS * DH
D_IN = D_POOL + 3 * D_FOX + HEADS
F_OFF = D_POOL + 3 * D_FOX
Q_OFF, K_OFF, V_OFF = D_POOL, D_POOL + D_FOX, D_POOL + 2 * D_FOX
XA_HEADS = 4
XA_DH = 256
D_FF = 4096
EPS = 1e-6
N_CHIPS = 4
ADAM_LR, ADAM_B1, ADAM_B2, ADAM_EPS, ADAM_WD, ADAM_STEP = 0.001, 0.9, 0.999, 1e-08, 0.01, 10

LANES = 128
SUBLANES = 8
D_IN_PAD = 21 * LANES
TR = 512
RING = 3
ROW_PIECES = 4
TILE_BYTES = 2 * 1024 * 1024
NEG = -1e30
VMEM_LIMIT = 52 * 1024 * 1024

NN = (((1,), (0,)), ((), ()))
NT = (((1,), (1,)), ((), ()))
TN = (((0,), (0,)), ((), ()))


def _dot(a, b, dims=NN):
    return lax.dot_general(a, b, dims, preferred_element_type=F32)


def _params(sem):
    return pltpu.CompilerParams(dimension_semantics=sem, vmem_limit_bytes=VMEM_LIMIT)


def _split3(x):
    hi = x.astype(BF16)
    r = x - hi.astype(F32)
    mid = r.astype(BF16)
    lo = (r - mid.astype(F32)).astype(BF16)
    return hi, mid, lo


def _split3_f32(x):
    hi = x.astype(BF16).astype(F32)
    r = x - hi
    mid = r.astype(BF16).astype(F32)
    return hi, mid, r - mid


def _lane_iota(shape):
    return lax.broadcasted_iota(jnp.int32, shape, len(shape) - 1)


def _row_iota(shape):
    return lax.broadcasted_iota(jnp.int32, shape, len(shape) - 2)


def _mm(name, a, b, a_spec, b_spec, out_shape, out_spec, grid, dims, acc_shape, ex=None):
    nk = grid[2]
    if ex is not None:
        return _mm_hosting(name, a, b, a_spec, b_spec, out_shape, out_spec, grid, dims, ex)

    def body(a_ref, b_ref, o_ref, *scr):
        p = _dot(a_ref[...], b_ref[...], dims)
        if nk == 1:
            o_ref[...] = p.astype(o_ref.dtype)
        else:
            acc = scr[0]
            k = pl.program_id(2)

            @pl.when(k == 0)
            def _():
                acc[...] = p

            @pl.when(k > 0)
            def _():
                acc[...] += p

            @pl.when(k == nk - 1)
            def _():
                o_ref[...] = acc[...].astype(o_ref.dtype)

    return pl.pallas_call(
        body, name=name, grid=grid, in_specs=[a_spec, b_spec], out_specs=out_spec, out_shape=out_shape,
        scratch_shapes=[pltpu.VMEM(acc_shape, F32)] if nk > 1 else [],
        compiler_params=_params(("parallel", "parallel", "arbitrary")),
    )(a, b)


def _mm_hosting(name, a, b, a_spec, b_spec, out_shape, out_spec, grid, dims, ex):
    assert grid[2] == 1
    n = len(ex.ins)

    def body(*refs):
        i, j = pl.program_id(0), pl.program_id(1)
        last = (i == grid[0] - 1) & (j == grid[1] - 1)
        (a_ref, b_ref), (o_ref,), _, begin, end = _hosted(ex, refs, 2, 1, (i == 0) & (j == 0), last, last)
        begin()
        o_ref[...] = _dot(a_ref[...], b_ref[...], dims).astype(o_ref.dtype)
        end()

    res = pl.pallas_call(
        body, name=name, grid=grid, in_specs=[a_spec, b_spec] + [ANY] * n, out_specs=[out_spec] + [ANY] * n,
        out_shape=[out_shape] + ex.out_shapes, scratch_shapes=ex.scratch(),
        compiler_params=_params(("arbitrary", "arbitrary", "arbitrary")),
    )(a, b, *ex.ins)
    return res[0], res[1:]


def _mm_nn(name, a, b, out_dtype, tm, tn):
    m, k = a.shape
    n = b.shape[1]
    return _mm(name, a, b, pl.BlockSpec((tm, k), lambda i, j, kk: (i, 0)), pl.BlockSpec((k, tn), lambda i, j, kk: (0, j)),
               jax.ShapeDtypeStruct((m, n), out_dtype), pl.BlockSpec((tm, tn), lambda i, j, kk: (i, j)),
               (m // tm, n // tn, 1), NN, (tm, tn))


def _mm_nt(name, a, b, out_dtype, tm, tn, ex=None):
    m, k = a.shape
    n = b.shape[0]
    return _mm(name, a, b, pl.BlockSpec((tm, k), lambda i, j, kk: (i, 0)), pl.BlockSpec((tn, k), lambda i, j, kk: (j, 0)),
               jax.ShapeDtypeStruct((m, n), out_dtype), pl.BlockSpec((tm, tn), lambda i, j, kk: (i, j)),
               (m // tm, n // tn, 1), NT, (tm, tn), ex)


def _mm_tn(name, a, b, tka, tn, ex=None):
    t, ka = a.shape
    n = b.shape[1]
    return _mm(name, a, b, pl.BlockSpec((t, tka), lambda i, j, kk: (0, i)), pl.BlockSpec((t, tn), lambda i, j, kk: (0, j)),
               jax.ShapeDtypeStruct((ka, n), F32), pl.BlockSpec((tka, tn), lambda i, j, kk: (i, j)),
               (ka // tka, n // tn, 1), TN, (tka, tn), ex)


def _d_h3(dhid, w_up, ex):
    tm = tn = 1024
    shard = 2 * D_FF // N_CHIPS
    per_plane = D_FF // shard
    grid = (S // tm, D // tn, N_CHIPS)
    n = len(ex.ins)

    def body(*refs):
        i, j, k = pl.program_id(0), pl.program_id(1), pl.program_id(2)
        first = (i == 0) & (j == 0) & (k == 0)
        last = (i == grid[0] - 1) & (j == grid[1] - 1) & (k == N_CHIPS - 1)
        (a_ref, b_ref), (o_ref,), (acc_ref,), begin, end = _hosted(ex, refs, 2, 1, first, first, last)
        begin()
        part = _dot(a_ref[...], b_ref[...], NT)

        @pl.when(k == 0)
        def _():
            acc_ref[...] = part

        @pl.when(k > 0)
        def _():
            acc_ref[...] += part

        @pl.when(k == N_CHIPS - 1)
        def _():
            o_ref[...] = acc_ref[...]

        end()

    res = pl.pallas_call(
        body, name="d_h3", grid=grid,
        in_specs=[pl.BlockSpec((None, tm, shard), lambda i, j, k: (k // per_plane, i, k % per_plane)),
                  pl.BlockSpec((None, tn, shard), lambda i, j, k: (k, j, 0))] + [ANY] * n,
        out_specs=[pl.BlockSpec((tm, tn), lambda i, j, k: (i, j))] + [ANY] * n,
        out_shape=[jax.ShapeDtypeStruct((S, D), F32)] + ex.out_shapes,
        scratch_shapes=[pltpu.VMEM((tm, tn), F32)] + ex.scratch(),
        compiler_params=_params(("arbitrary", "arbitrary", "arbitrary")),
    )(dhid, w_up, *ex.ins)
    return res[0], res[1:]


SHARD_IN = D_IN // N_CHIPS
SHARD_IN_PAD = -(-SHARD_IN // SUBLANES) * SUBLANES


def _dw_in(dproj, h1, ex):
    tk = 1024
    nk = S // tk
    half = D // 2
    starts = [SHARD_IN * j // LANES * LANES for j in range(N_CHIPS)]
    shifts = [SHARD_IN * j - s for j, s in enumerate(starts)]
    window = -(-(max(shifts) + SHARD_IN) // LANES) * LANES
    assert starts[-1] + window <= dproj.shape[1]
    n = len(ex.ins)

    def body(*refs):
        k = pl.program_id(0)
        (a_ref, b_ref), (o_ref,), _, begin, end = _hosted(ex, refs, 2, 1, k == 0, k == nk - 1, k == nk - 1)
        begin()

        @pl.when(k == 0)
        def _():
            o_ref[...] = jnp.zeros(o_ref.shape, F32)

        for j in range(N_CHIPS):
            win = a_ref[:, starts[j]:starts[j] + window]
            if shifts[j]:
                win = pltpu.roll(win, window - shifts[j], axis=1)
            part = _dot(win, b_ref[...], TN)
            for h in range(2):
                o_ref[j, h] += part[:SHARD_IN_PAD, h * half:(h + 1) * half]
        end()

    out_shape = (N_CHIPS, 2, SHARD_IN_PAD, half)
    res = pl.pallas_call(
        body, name="dw_in", grid=(nk,),
        in_specs=[pl.BlockSpec((tk, dproj.shape[1]), lambda k: (k, 0)), pl.BlockSpec((tk, D), lambda k: (k, 0))] + [ANY] * n,
        out_specs=[pl.BlockSpec(out_shape, lambda k: (0, 0, 0, 0))] + [ANY] * n,
        out_shape=[jax.ShapeDtypeStruct(out_shape, F32)] + ex.out_shapes,
        scratch_shapes=ex.scratch(),
        compiler_params=_params(("arbitrary",)),
    )(dproj, h1, *ex.ins)
    return res[0], res[1:]


def _rms(x, g):
    r = lax.rsqrt(jnp.mean(x * x, axis=-1, keepdims=True) + EPS)
    return x * r * g


def _rms_bwd(x, g, dy):
    r = lax.rsqrt(jnp.mean(x * x, axis=-1, keepdims=True) + EPS)
    xh = x * r
    dxh = dy * g
    dx = r * (dxh - xh * jnp.mean(dxh * xh, axis=-1, keepdims=True))
    return dx, jnp.sum(dy * xh, axis=0, keepdims=True)


def _row_spec(tr, width):
    return pl.BlockSpec((tr, width), lambda i: (i, 0))


def _vec_spec(width):
    return pl.BlockSpec((1, width), lambda i: (0, 0))


def _norm_fwd(name, x, g, ex=None):
    rows, width = x.shape
    tr = min(TR, rows)
    steps = rows // tr
    hosted = ex if ex is not None else _no_exchange()
    n = len(hosted.ins)

    def body(*refs):
        i = pl.program_id(0)
        (x_ref, g_ref), (h_ref,), _, begin, end = _hosted(hosted, refs, 2, 1, i == 0, i == steps - 1, i == steps - 1)
        begin()
        h_ref[...] = _rms(x_ref[...], g_ref[...]).astype(BF16)
        end()

    res = pl.pallas_call(
        body, name=name, grid=(steps,), in_specs=[_row_spec(tr, width), _vec_spec(width)] + [ANY] * n,
        out_specs=[_row_spec(tr, width)] + [ANY] * n,
        out_shape=[jax.ShapeDtypeStruct((rows, width), BF16)] + hosted.out_shapes, scratch_shapes=hosted.scratch(),
        compiler_params=_params(("arbitrary",)),
    )(x, g, *hosted.ins)
    return res[0] if ex is None else (res[0], res[1:])


def _ring_rows(hbm, buf, sem, i, nt):
    def fetch(step):
        slot = step % RING
        return pltpu.make_async_copy(hbm.at[pl.ds(pl.multiple_of(step * TR, TR), TR)], buf.at[slot], sem.at[slot])

    @pl.when(i == 0)
    def _():
        for step in range(RING - 1):
            fetch(step).start()

    @pl.when(i + RING - 1 < nt)
    def _():
        fetch(i + RING - 1).start()

    def read():
        fetch(i).wait()
        return buf[i % RING]

    return read


def _ring_scratch(width):
    return [pltpu.VMEM((RING, TR, width), F32), pltpu.SemaphoreType.DMA((RING,))]


def _proj_resid_norm(name, a, w, xp, g_post, g_pre, w_next=None):
    nt = S // TR

    def body(a_ref, w_ref, xp_hbm, gpost_ref, gpre_ref, *rest):
        xp_buf, xp_sem = rest[-2:]
        rest = rest[:-2]
        y_ref, xn_ref, h_ref = rest[-3:] if w_next is None else rest[1:4]
        i = pl.program_id(0)

        read_xp = _ring_rows(xp_hbm, xp_buf, xp_sem, i, nt)
        y = _dot(a_ref[...], w_ref[...])
        y_ref[...] = y
        xn = read_xp() + _rms(y, gpost_ref[...])
        xn_ref[...] = xn
        h = _rms(xn, gpre_ref[...]).astype(BF16)
        h_ref[...] = h
        if w_next is not None:
            rest[4][...] = _dot(h, rest[0][...]).astype(BF16)

    mat = pl.BlockSpec((D, D), lambda i: (0, 0))
    more = [] if w_next is None else [w_next]
    return pl.pallas_call(
        body, name=name, grid=(nt,),
        in_specs=[_row_spec(TR, D), mat, ANY, _vec_spec(D), _vec_spec(D)] + [mat] * len(more),
        out_specs=[_row_spec(TR, D)] * (3 + len(more)),
        out_shape=[jax.ShapeDtypeStruct((S, D), F32), jax.ShapeDtypeStruct((S, D), F32), jax.ShapeDtypeStruct((S, D), BF16)]
        + [jax.ShapeDtypeStruct((S, D), BF16)] * len(more),
        scratch_shapes=_ring_scratch(D), compiler_params=_params(("arbitrary",)),
    )(a, w, xp, g_post, g_pre, *more)


def _down_loss_bwd(act, w_down, x3, g_post, target):
    def body(a_ref, w_ref, x_ref, g_ref, t_ref, dres_ref, dy_ref, dg_ref, loss_ref):
        i = pl.program_id(0)

        @pl.when(i == 0)
        def _():
            dg_ref[...] = jnp.zeros_like(dg_ref)
            loss_ref[...] = jnp.zeros_like(loss_ref)

        g = g_ref[...]
        for r in range(ROW_PIECES):
            rows = slice(r * TR // ROW_PIECES, (r + 1) * TR // ROW_PIECES)
            y = _dot(a_ref[rows, :], w_ref[...])
            e = x_ref[rows, :] + _rms(y, g) - t_ref[rows, :]
            loss_ref[...] += jnp.sum(e * e, axis=0, keepdims=True) * (0.5 / D)
            dres = e * (1.0 / D)
            dres_ref[rows, :] = dres
            dy, dg = _rms_bwd(y, g, dres)
            dy_ref[rows, :] = dy.astype(BF16)
            dg_ref[...] += dg

    return pl.pallas_call(
        body, name="down_loss_bwd", grid=(S // TR,),
        in_specs=[_row_spec(TR, D_FF), pl.BlockSpec((D_FF, D), lambda i: (0, 0)), _row_spec(TR, D), _vec_spec(D),
                  _row_spec(TR, D)],
        out_specs=[_row_spec(TR, D), _row_spec(TR, D), _vec_spec(D), _vec_spec(D)],
        out_shape=[jax.ShapeDtypeStruct((S, D), F32), jax.ShapeDtypeStruct((S, D), BF16),
                   jax.ShapeDtypeStruct((1, D), F32), jax.ShapeDtypeStruct((1, D), F32)],
        compiler_params=_params(("arbitrary",)),
    )(act, w_down, x3, g_post, target)


def _mid_bwd(name, dres, xcur, g_pre, dh, yprev, g_post, w):
    def body(dres_hbm, x_ref, gpre_ref, dh_ref, y_ref, gpost_ref, w_ref, dx_ref, dy_ref, da_ref, dgpre_ref, dgpost_ref,
             dres_buf, dres_sem):
        i = pl.program_id(0)
        read_dres = _ring_rows(dres_hbm, dres_buf, dres_sem, i, S // TR)

        @pl.when(i == 0)
        def _():
            dgpre_ref[...] = jnp.zeros_like(dgpre_ref)
            dgpost_ref[...] = jnp.zeros_like(dgpost_ref)

        dxn, dgpre = _rms_bwd(x_ref[...], gpre_ref[...], dh_ref[...])
        dx = read_dres() + dxn
        dx_ref[...] = dx
        dy, dgpost = _rms_bwd(y_ref[...], gpost_ref[...], dx)
        dy = dy.astype(BF16)
        dy_ref[...] = dy
        da_ref[...] = _dot(dy, w_ref[...], NT).astype(BF16)
        dgpre_ref[...] += dgpre
        dgpost_ref[...] += dgpost

    return pl.pallas_call(
        body, name=name, grid=(S // TR,),
        in_specs=[ANY, _row_spec(TR, D), _vec_spec(D), _row_spec(TR, D), _row_spec(TR, D), _vec_spec(D),
                  pl.BlockSpec((D, D), lambda i: (0, 0))],
        out_specs=[_row_spec(TR, D), _row_spec(TR, D), _row_spec(TR, D), _vec_spec(D), _vec_spec(D)],
        out_shape=[jax.ShapeDtypeStruct((S, D), F32), jax.ShapeDtypeStruct((S, D), BF16), jax.ShapeDtypeStruct((S, D), BF16),
                   jax.ShapeDtypeStruct((1, D), F32), jax.ShapeDtypeStruct((1, D), F32)],
        scratch_shapes=_ring_scratch(D), compiler_params=_params(("arbitrary",)),
    )(dres, xcur, g_pre, dh, yprev, g_post, w)


def _d_h1_first_bwd(dproj, w_in, dres, x, g, ex):
    nt = S // TR
    n = len(ex.ins)

    def body(*refs):
        i = pl.program_id(0)
        (dp_ref, w_ref, dres_ref, x_ref, g_ref), (dx_ref, dg_ref), _, begin, end = _hosted(
            ex, refs, 5, 2, i == 0, i == nt - 1, i == nt - 1)
        begin()

        @pl.when(i == 0)
        def _():
            dg_ref[...] = jnp.zeros_like(dg_ref)

        dxn, dg = _rms_bwd(x_ref[...], g_ref[...], _dot(dp_ref[...], w_ref[...], NT))
        dx_ref[...] = dres_ref[...] + dxn
        dg_ref[...] += dg
        end()

    res = pl.pallas_call(
        body, name="d_h1", grid=(nt,),
        in_specs=[_row_spec(TR, D_IN_PAD), pl.BlockSpec((D, D_IN_PAD), lambda i: (0, 0)), _row_spec(TR, D), _row_spec(TR, D),
                  _vec_spec(D)] + [ANY] * n,
        out_specs=[_row_spec(TR, D), _vec_spec(D)] + [ANY] * n,
        out_shape=[jax.ShapeDtypeStruct((S, D), F32), jax.ShapeDtypeStruct((1, D), F32)] + ex.out_shapes,
        scratch_shapes=ex.scratch(), compiler_params=_params(("arbitrary",)),
    )(dproj, w_in, dres, x, g, *ex.ins)
    return res[0], res[1], res[2:]


def _gain_bwd(name, x, g, dy):
    rows, width = x.shape

    def body(x_ref, g_ref, dy_ref, dg_ref):
        _, dg = _rms_bwd(x_ref[...], g_ref[...], dy_ref[...])
        dg_ref[...] = dg

    return pl.pallas_call(
        body, name=name, grid=(1,), in_specs=[_row_spec(rows, width), _vec_spec(width), _row_spec(rows, width)],
        out_specs=_vec_spec(width), out_shape=jax.ShapeDtypeStruct((1, width), F32),
        compiler_params=_params(("arbitrary",)),
    )(x, g, dy)


CUM_Q = DH
CUM_K = DH + 3
LSE_Q = DH + 6
BOTH_ONE = DH + 9
DEN_V = DH
DELTA = DH + 1
PREP_TR = 256
PIECE_LANES = 16
FOX_FWD_BLOCK = 1024
FOX_BWD_BLOCK = 512


def _at(lane_of_even_head, h):
    return (lane_of_even_head + DH * (h % 2)) % LANES


def _data_lanes(lane, h):
    return lane >= DH if h % 2 else lane < DH


def _pair_block(ref, off, h):
    base = ((off + DH * h) // LANES) * LANES
    return ref[:, base:base + LANES]


def _cumsum_rows(x, tri, carry):
    hi, mid, lo = _split3(x)
    return _dot(tri, hi) + _dot(tri, mid) + _dot(tri, lo) + carry


def _in_proj(h1, w_in, bf_pad):
    tr = TR

    place_q = np.zeros((LANES, HEADS * LANES), np.float32)
    place_k = np.zeros((LANES, HEADS * LANES), np.float32)
    for h in range(HEADS):
        for piece in range(3):
            place_q[PIECE_LANES * piece + h, LANES * h + _at(CUM_Q, h) + piece] = 1.0
            place_k[PIECE_LANES * piece + h, LANES * h + _at(CUM_K, h) + piece] = -1.0

    def body(h_ref, w_ref, bf_ref, pq_ref, pk_ref, qa_ref, ka_ref, va_ref, u_ref, z_ref, carry_ref):
        i = pl.program_id(0)

        @pl.when(i == 0)
        def _():
            carry_ref[...] = jnp.zeros_like(carry_ref)

        proj = _dot(h_ref[...], w_ref[...])
        u_ref[...] = proj[:, :D_POOL]
        z_ref[...] = proj[:, F_OFF:F_OFF + LANES]
        lane = _lane_iota((tr, LANES))
        z = proj[:, F_OFF:F_OFF + LANES] + bf_ref[...]
        log_f = jnp.minimum(z, 0.0) - jnp.log(1.0 + jnp.exp(-jnp.abs(z)))
        log_f = jnp.where(lane < HEADS, log_f, 0.0)
        tri = jnp.where(_row_iota((tr, tr)) >= _lane_iota((tr, tr)), 1.0, 0.0).astype(BF16)
        cum = _cumsum_rows(log_f, tri, carry_ref[0:1, :])
        carry_ref[0:1, :] = cum[tr - 1:tr, :]
        c_hi, c_mid, c_lo = _split3_f32(cum)
        pieces = (c_hi + pltpu.roll(c_mid, PIECE_LANES, 1) + pltpu.roll(c_lo, 2 * PIECE_LANES, 1)).astype(BF16)
        cum_q = _dot(pieces, pq_ref[...])
        cum_k = _dot(pieces, pk_ref[...])

        def between(first, h):
            return (lane >= _at(first, h)) & (lane < _at(first, h) + 3)

        ones_q = [jnp.where(between(CUM_K, h) | (lane == _at(BOTH_ONE, h)), 1.0, 0.0) for h in range(2)]
        ones_k = [jnp.where(between(CUM_Q, h) | between(LSE_Q, h) | (lane == _at(BOTH_ONE, h)), 1.0, 0.0) for h in range(2)]
        aug_v = [jnp.where(lane == _at(DEN_V, h), 1.0, jnp.where(between(DELTA, h), -1.0, 0.0)) for h in range(2)]
        for h in range(HEADS):
            mine = slice(LANES * h, LANES * (h + 1))
            data = _data_lanes(lane, h)
            qa_ref[h] = jnp.where(data, _pair_block(proj, Q_OFF, h) * (DH ** -0.5), cum_q[:, mine] + ones_q[h % 2]).astype(BF16)
            ka_ref[h] = jnp.where(data, _pair_block(proj, K_OFF, h), cum_k[:, mine] + ones_k[h % 2]).astype(BF16)
            va_ref[h] = jnp.where(data, _pair_block(proj, V_OFF, h), aug_v[h % 2]).astype(BF16)

    head_spec = pl.BlockSpec((HEADS, tr, LANES), lambda i: (0, i, 0))
    head_shape = jax.ShapeDtypeStruct((HEADS, S, LANES), BF16)
    place_spec = pl.BlockSpec(place_q.shape, lambda i: (0, 0))
    return pl.pallas_call(
        body, name="in_proj", grid=(S // tr,),
        in_specs=[_row_spec(tr, D), pl.BlockSpec((D, D_IN_PAD), lambda i: (0, 0)), _vec_spec(LANES), place_spec, place_spec],
        out_specs=[head_spec] * 3 + [_row_spec(tr, D_POOL), _row_spec(tr, LANES)],
        out_shape=[head_shape] * 3 + [jax.ShapeDtypeStruct((S, D_POOL), F32), jax.ShapeDtypeStruct((S, LANES), F32)],
        scratch_shapes=[pltpu.VMEM((SUBLANES, LANES), F32)], compiler_params=_params(("arbitrary",)),
    )(h1, w_in, bf_pad, jnp.asarray(place_q, BF16), jnp.asarray(place_k, BF16))


def _hosted(ex, refs, n_blocked_in, n_blocked_out, first, forward_at, last):
    n = len(ex.ins)
    own_in = refs[:n_blocked_in]
    ex_in = refs[n_blocked_in:n_blocked_in + n]
    own_out = refs[n_blocked_in + n:n_blocked_in + n + n_blocked_out]
    ex_out = refs[n_blocked_in + n + n_blocked_out:n_blocked_in + 2 * n + n_blocked_out]
    rest = refs[n_blocked_in + 2 * n + n_blocked_out:]
    args = (ex_in, ex_out, rest[-2], rest[-1])

    def begin():
        @pl.when(first)
        def _():
            ex.start(*args)

        @pl.when(forward_at)
        def _():
            ex.forward(*args)

    def end():
        @pl.when(last)
        def _():
            ex.finish(*args)

    return own_in, own_out, rest[:-2], begin, end


def _fox_fwd(qa, ka, va, ex):
    BQ = BK = FOX_FWD_BLOCK
    nq = S // BQ
    n_pairs = HEADS // 2

    def body(*refs):
        p_id, i = pl.program_id(0), pl.program_id(1)
        (qa_ref, ka_ref, va_ref), (y_ref, qab_ref), (m_scr, acc_scr), begin, end = _hosted(
            ex, refs, 3, 2, (p_id == 0) & (i == 0), (p_id == n_pairs - 1) & (i == 0), (p_id == n_pairs - 1) & (i == nq - 1))
        begin()
        lane = _lane_iota((BQ, LANES))
        causal = _row_iota((BQ, BK)) >= _lane_iota((BQ, BK))
        m_scr[...] = jnp.full_like(m_scr, NEG)
        acc_scr[...] = jnp.zeros_like(acc_scr)

        def step(j, masked):
            rows = pl.ds(pl.multiple_of(j * BK, BK), BK)
            for hh in range(2):
                s = _dot(qa_ref[hh], ka_ref[hh, rows, :], NT)
                if masked:
                    s = jnp.where(causal, s, NEG)
                m_prev = m_scr[hh]
                m_new = jnp.maximum(m_prev, jnp.max(s, axis=1, keepdims=True))
                p = jnp.exp(s - jnp.tile(m_new, (1, BK // LANES)))
                acc_scr[hh] = jnp.exp(m_prev - m_new) * acc_scr[hh] + _dot(p.astype(BF16), va_ref[hh, rows, :])
                m_scr[hh] = m_new

        def full_step(j, carry):
            step(j, False)
            return carry

        lax.fori_loop(0, i, full_step, 0)
        step(i, True)
        outs = []
        for hh in range(2):
            acc = acc_scr[hh]
            den_lane, lse_lane = _at(DEN_V, hh), _at(LSE_Q, hh)
            den = jnp.broadcast_to(acc[:, den_lane:den_lane + 1], (BQ, LANES))
            outs.append(acc * (1.0 / den))
            n_hi, n_mid, n_lo = _split3(-(m_scr[hh] + jnp.log(den)))
            qab_ref[hh] = jnp.where(lane == lse_lane, n_hi,
                                    jnp.where(lane == lse_lane + 1, n_mid, jnp.where(lane == lse_lane + 2, n_lo, qa_ref[hh])))
        y_ref[...] = jnp.where(lane < DH, outs[0], outs[1]).astype(BF16)
        end()

    pair_rows = pl.BlockSpec((2, BQ, LANES), lambda p, i: (p, i, 0))
    pair_all = pl.BlockSpec((2, S, LANES), lambda p, i: (p, 0, 0))
    n = len(ex.ins)
    res = pl.pallas_call(
        body, name="fox_fwd", grid=(n_pairs, nq), in_specs=[pair_rows, pair_all, pair_all] + [ANY] * n,
        out_specs=[pl.BlockSpec((BQ, LANES), lambda p, i: (i, D_POOL // LANES + p)), pair_rows] + [ANY] * n,
        out_shape=[jax.ShapeDtypeStruct((S, D), BF16), jax.ShapeDtypeStruct((HEADS, S, LANES), BF16)] + ex.out_shapes,
        scratch_shapes=[pltpu.VMEM((2, BQ, LANES), F32), pltpu.VMEM((2, BQ, LANES), F32)] + ex.scratch(),
        compiler_params=_params(("arbitrary", "arbitrary")),
    )(qa, ka, va, *ex.ins)
    return res[0], res[1], res[2:]


def _bwd_xa_mix(dqx, w_xq, dres, x2, g_pre, y1, g_post, w_mix_out, ycat, ex):
    steps = S // TR
    n = len(ex.ins)

    def body(*refs):
        i = pl.program_id(0)
        ((dq_ref, wq_ref, dres_ref, x_ref, gpre_ref, y_ref, gpost_ref, wm_ref, ycat_ref),
         (dx_ref, dy_ref, dgpre_ref, dgpost_ref, dp_ref, doa_ref), _, begin, end) = _hosted(
            ex, refs, 9, 6, i == 0, i == 0, i == steps - 1)
        begin()

        @pl.when(i == 0)
        def _():
            dgpre_ref[...] = jnp.zeros_like(dgpre_ref)
            dgpost_ref[...] = jnp.zeros_like(dgpost_ref)

        dxn, dgpre = _rms_bwd(x_ref[...], gpre_ref[...], _dot(dq_ref[...], wq_ref[...], NT))
        dx = dres_ref[...] + dxn
        dx_ref[...] = dx
        dy, dgpost = _rms_bwd(y_ref[...], gpost_ref[...], dx)
        dy = dy.astype(BF16)
        dy_ref[...] = dy
        dgpre_ref[...] += dgpre
        dgpost_ref[...] += dgpost

        d = _dot(dy, wm_ref[...], NT)
        dp_ref[...] = d[:, :D_POOL]
        lane = _lane_iota((TR, LANES))
        low = lane < DH
        for p in range(HEADS // 2):
            cols = slice(D_POOL + LANES * p, D_POOL + LANES * (p + 1))
            do = d[:, cols]
            prod = do * ycat_ref[:, cols].astype(F32)
            deltas = (jnp.sum(jnp.where(low, prod, 0.0), axis=1, keepdims=True),
                      jnp.sum(jnp.where(low, 0.0, prod), axis=1, keepdims=True))
            for hh in range(2):
                d_hi, d_mid, d_lo = _split3_f32(deltas[hh])
                dl = _at(DELTA, hh)
                aug = jnp.where(lane == dl, d_hi, jnp.where(lane == dl + 1, d_mid, jnp.where(lane == dl + 2, d_lo, 0.0)))
                doa_ref[2 * p + hh] = jnp.where(_data_lanes(lane, hh), do, aug).astype(BF16)
        end()

    mat = pl.BlockSpec((D, D), lambda i: (0, 0))
    res = pl.pallas_call(
        body, name="bwd_xa_mix", grid=(steps,),
        in_specs=[_row_spec(TR, D), mat, _row_spec(TR, D), _row_spec(TR, D), _vec_spec(D), _row_spec(TR, D), _vec_spec(D), mat,
                  _row_spec(TR, D)] + [ANY] * n,
        out_specs=[_row_spec(TR, D), _row_spec(TR, D), _vec_spec(D), _vec_spec(D), _row_spec(TR, D_POOL),
                   pl.BlockSpec((HEADS, TR, LANES), lambda i: (0, i, 0))] + [ANY] * n,
        out_shape=[jax.ShapeDtypeStruct((S, D), F32), jax.ShapeDtypeStruct((S, D), BF16), jax.ShapeDtypeStruct((1, D), F32),
                   jax.ShapeDtypeStruct((1, D), F32), jax.ShapeDtypeStruct((S, D_POOL), F32),
                   jax.ShapeDtypeStruct((HEADS, S, LANES), BF16)] + ex.out_shapes,
        scratch_shapes=ex.scratch(), compiler_params=_params(("arbitrary",)),
    )(dqx, w_xq, dres, x2, g_pre, y1, g_post, w_mix_out, ycat, *ex.ins)
    return res[:6], res[6:]


def _fox_bwd(qab, doa, ka, va, ex):
    BQ = BK = FOX_BWD_BLOCK
    nk = S // BK
    n_pairs = HEADS // 2

    def body(*refs):
        p_id, j = pl.program_id(0), pl.program_id(1)
        (qab_ref, doa_ref, ka_ref, va_ref), (dqa_ref, dka_ref, dva_ref), (dv_ref,), begin, end = _hosted(
            ex, refs, 4, 3, (p_id == 0) & (j == 0), (p_id == n_pairs - 1) & (j == 0), (p_id == n_pairs - 1) & (j == nk - 1))
        begin()

        @pl.when(j == 0)
        def _():
            dqa_ref[...] = jnp.zeros_like(dqa_ref)

        causal = _row_iota((BQ, BK)) >= _lane_iota((BQ, BK))
        dka_ref[...] = jnp.zeros_like(dka_ref)
        dv_ref[...] = jnp.zeros_like(dv_ref)

        def step(i, masked):
            rows = pl.ds(pl.multiple_of(i * BQ, BQ), BQ)
            for hh in range(2):
                kb = ka_ref[hh]
                q = qab_ref[hh, rows, :]
                do = doa_ref[hh, rows, :]
                s = _dot(q, kb, NT)
                if masked:
                    s = jnp.where(causal, s, NEG)
                p = jnp.exp(s)
                ds = p * _dot(do, va_ref[hh], NT)
                pb = p.astype(BF16)
                dsb = ds.astype(BF16)
                dv_ref[hh] += _dot(pb, do, TN)
                dka_ref[hh] += _dot(dsb, q, TN)
                dqa_ref[hh, rows, :] += _dot(dsb, kb)

        def full_step(i, carry):
            step(i, False)
            return carry

        step(j, True)
        lax.fori_loop(j + 1, nk, full_step, 0)
        dva_ref[...] = dv_ref[...].astype(BF16)
        end()

    pair_all = pl.BlockSpec((2, S, LANES), lambda p, j: (p, 0, 0))
    pair_rows = pl.BlockSpec((2, BK, LANES), lambda p, j: (p, j, 0))
    shape = jax.ShapeDtypeStruct((HEADS, S, LANES), F32)
    n = len(ex.ins)
    res = pl.pallas_call(
        body, name="fox_bwd", grid=(n_pairs, nk), in_specs=[pair_all, pair_all, pair_rows, pair_rows] + [ANY] * n,
        out_specs=[pair_all, pair_rows, pair_rows] + [ANY] * n,
        out_shape=[shape, shape, jax.ShapeDtypeStruct((HEADS, S, LANES), BF16)] + ex.out_shapes,
        scratch_shapes=[pltpu.VMEM((2, BK, LANES), F32)] + ex.scratch(), compiler_params=_params(("arbitrary", "arbitrary")),
    )(qab, doa, ka, va, *ex.ins)
    return res[0], res[1], res[2], res[3:]


def _fox_bwd_post(dqa, dka, dva, du, proj, bf_pad):
    tr = PREP_TR
    nt = S // tr

    pick = np.zeros((HEADS * LANES, LANES), np.float32)
    for h in range(HEADS):
        pick[LANES * h + _at(BOTH_ONE, h), h] = 1.0

    def body(dqa_ref, dka_ref, dva_ref, du_ref, z_ref, bf_ref, pick_ref, dp_ref, dbf_ref, carry_ref):
        i = pl.program_id(0)

        @pl.when(i == 0)
        def _():
            carry_ref[...] = jnp.zeros_like(carry_ref)
            dbf_ref[...] = jnp.zeros_like(dbf_ref)

        lane = _lane_iota((tr, LANES))
        diff = jnp.concatenate([dqa_ref[h] - dka_ref[h] for h in range(HEADS)], axis=1)
        hi = diff.astype(BF16)
        dcum = _dot(hi, pick_ref[...]) + _dot((diff - hi.astype(F32)).astype(BF16), pick_ref[...])
        tri =jnp.where(_lane_iota((tr, tr)) >= _row_iota((tr, tr)), 1.0, 0.0).astype(BF16)
        dlog_f = _cumsum_rows(dcum, tri, carry_ref[0:1, :])
        carry_ref[0:1, :] = dlog_f[0:1, :]
        z = z_ref[...] + bf_ref[...]
        df = jnp.where(lane < HEADS, dlog_f / (1.0 + jnp.exp(z)), 0.0)
        dbf_ref[...] += jnp.sum(df, axis=0, keepdims=True)

        dp_ref[:, 0:D_POOL] = du_ref[...].astype(BF16)
        low = lane < DH
        for ref, off, scale in ((dqa_ref, Q_OFF, DH ** -0.5), (dka_ref, K_OFF, 1.0), (dva_ref, V_OFF, 1.0)):
            for p in range(HEADS // 2):
                blk = jnp.where(low, ref[2 * p], ref[2 * p + 1])
                dp_ref[:, off + LANES * p:off + LANES * (p + 1)] = (blk * scale).astype(BF16)
        dp_ref[:, F_OFF:F_OFF + LANES] = df.astype(BF16)

    head_spec = pl.BlockSpec((HEADS, tr, LANES), lambda i: (0, nt - 1 - i, 0))
    return pl.pallas_call(
        body, name="fox_bwd_post", grid=(nt,),
        in_specs=[head_spec, head_spec, head_spec, pl.BlockSpec((tr, D_POOL), lambda i: (nt - 1 - i, 0)),
                  pl.BlockSpec((tr, LANES), lambda i: (nt - 1 - i, 0)), _vec_spec(LANES),
                  pl.BlockSpec(pick.shape, lambda i: (0, 0))],
        out_specs=[pl.BlockSpec((tr, D_IN_PAD), lambda i: (nt - 1 - i, 0)), _vec_spec(LANES)],
        out_shape=[jax.ShapeDtypeStruct((S, D_IN_PAD), BF16), jax.ShapeDtypeStruct((1, LANES), F32)],
        scratch_shapes=[pltpu.VMEM((SUBLANES, LANES), F32)],
        compiler_params=_params(("arbitrary",)),
    )(dqa, dka, dva, du, proj, bf_pad, jnp.asarray(pick, BF16))


POOL_HALO = 16


def _by_group(lane, a2, a4, a8, a16):
    return jnp.where(lane < 64, a2, jnp.where(lane < 128, a4, jnp.where(lane < 192, a8, a16)))


def _window_count(lane, t):
    return jnp.minimum(t + 1, _by_group(lane, 2, 4, 8, 16)).astype(F32)


def _pool_diff(u, halo, first, tile):
    n = TR + POOL_HALO
    ext = jnp.concatenate([jnp.where(first, 0.0, halo), u], axis=0)
    s2 = ext + pltpu.roll(ext, 1, 0)
    s4 = s2 + pltpu.roll(s2, 2, 0)
    s8 = s4 + pltpu.roll(s4, 4, 0)
    s16 = s8 + pltpu.roll(s8, 8, 0)
    lane = _lane_iota((n, D_POOL))
    win = _by_group(lane, s2, s4, s8, s16)[POOL_HALO:]
    lane = _lane_iota((TR, D_POOL))
    t = tile * TR + _row_iota((TR, D_POOL))
    return win / _window_count(lane, t) - u


def _prev_halo(rows, width, col):
    per = TR // rows
    return pl.BlockSpec((rows, width), lambda i: (jnp.maximum(i * per - 1, 0), col))


def _next_halo(rows, width, col):
    per = TR // rows
    return pl.BlockSpec((rows, width), lambda i: (jnp.minimum((i + 1) * per, S // rows - 1), col))


def _pool_fwd(proj, w_bd, ps, ycat):
    def body(u_ref, halo_ref, w_ref, ps_ref, ycat_ref, y_ref):
        i = pl.program_id(0)
        diff = _pool_diff(u_ref[...], halo_ref[...], i == 0, i)
        y_ref[...] = (_dot(diff.astype(BF16), w_ref[...]) * ps_ref[...]).astype(BF16)

    return pl.pallas_call(
        body, name="pool_fwd", grid=(S // TR,),
        in_specs=[_row_spec(TR, D_POOL), _prev_halo(POOL_HALO, D_POOL, 0),
                  pl.BlockSpec((D_POOL, D_POOL), lambda i: (0, 0)), _vec_spec(D_POOL), ANY],
        out_specs=_row_spec(TR, D_POOL), out_shape=jax.ShapeDtypeStruct((S, D), BF16), input_output_aliases={4: 0},
        compiler_params=_params(("parallel",)),
    )(proj, proj, w_bd, ps, ycat)


def _pool_bwd(proj, dycat, w_bd, w_bd_t, ps):
    nt = S // TR
    n = TR + POOL_HALO

    def body(u_ref, halo_ref, dy_ref, dyn_ref, w_ref, wt_ref, ps_ref, du_ref, dw_ref, dps_ref):
        i = pl.program_id(0)

        @pl.when(i == 0)
        def _():
            dw_ref[...] = jnp.zeros_like(dw_ref)
            dps_ref[...] = jnp.zeros_like(dps_ref)

        diff = _pool_diff(u_ref[...], halo_ref[...], i == 0, i).astype(BF16)
        dy = dy_ref[...]
        dps_ref[...] += jnp.sum(dy * _dot(diff, w_ref[...]), axis=0, keepdims=True)
        dy_ext = jnp.concatenate([dy, jnp.where(i == nt - 1, 0.0, dyn_ref[...])], axis=0)
        dmixed = (dy_ext * ps_ref[...]).astype(BF16)
        ddiff = _dot(dmixed, wt_ref[...])
        dw_ref[...] += _dot(diff, dmixed[:TR], TN)
        lane = _lane_iota((n, D_POOL))
        t = i * TR + _row_iota((n, D_POOL))
        e = ddiff / _window_count(lane, t)
        f2 = e + pltpu.roll(e, n - 1, 0)
        f4 = f2 + pltpu.roll(f2, n - 2, 0)
        f8 = f4 + pltpu.roll(f4, n - 4, 0)
        f16 = f8 + pltpu.roll(f8, n - 8, 0)
        du_ref[...] = _by_group(lane, f2, f4, f8, f16)[:TR] - ddiff[:TR]

    mat = pl.BlockSpec((D_POOL, D_POOL), lambda i: (0, 0))
    return pl.pallas_call(
        body, name="pool_bwd", grid=(nt,),
        in_specs=[_row_spec(TR, D_POOL), _prev_halo(POOL_HALO, D_POOL, 0), _row_spec(TR, D_POOL),
                  _next_halo(POOL_HALO, D_POOL, 0), mat, mat, _vec_spec(D_POOL)],
        out_specs=[_row_spec(TR, D_POOL), mat, _vec_spec(D_POOL)],
        out_shape=[jax.ShapeDtypeStruct((S, D_POOL), F32), jax.ShapeDtypeStruct((D_POOL, D_POOL), F32),
                   jax.ShapeDtypeStruct((1, D_POOL), F32)],
        compiler_params=_params(("arbitrary",)),
    )(proj, proj, dycat, dycat, w_bd, w_bd_t, ps)


XA_GROUP = 4


def _xa_probs(q, k):
    s = _dot(q, k, NT) * (XA_DH ** -0.5)
    e = jnp.exp(s - jnp.max(s, axis=-1, keepdims=True))
    return e * (1.0 / jnp.sum(e, axis=-1, keepdims=True))


def _xattn_fwd(qx, kv):
    def body(q_ref, kv_ref, o_ref):
        for h0 in range(0, XA_HEADS, XA_GROUP):
            cols = [slice(XA_DH * h, XA_DH * (h + 1)) for h in range(h0, h0 + XA_GROUP)]
            p = [_xa_probs(q_ref[:, c], kv_ref[:, c]) for c in cols]
            for ph, c in zip(p, cols):
                o_ref[:, c] = _dot(ph.astype(BF16), kv_ref[:, D + c.start:D + c.stop]).astype(BF16)

    return pl.pallas_call(
        body, name="xattn_fwd", grid=(S // TR,),
        in_specs=[_row_spec(TR, D), pl.BlockSpec((MEM, 2 * D), lambda i: (0, 0))],
        out_specs=_row_spec(TR, D), out_shape=jax.ShapeDtypeStruct((S, D), BF16),
        compiler_params=_params(("parallel",)),
    )(qx, kv)


def _xattn_bwd(qx, kv, dxo):
    def body(q_ref, kv_ref, do_ref, dq_ref, dkv_ref):
        i = pl.program_id(0)

        @pl.when(i == 0)
        def _():
            dkv_ref[...] = jnp.zeros_like(dkv_ref)

        for h0 in range(0, XA_HEADS, XA_GROUP):
            heads = range(h0, h0 + XA_GROUP)
            cols = [slice(XA_DH * h, XA_DH * (h + 1)) for h in heads]
            vcols = [slice(D + XA_DH * h, D + XA_DH * (h + 1)) for h in heads]
            q = [q_ref[:, c] for c in cols]
            k = [kv_ref[:, c] for c in cols]
            do = [do_ref[:, c] for c in cols]
            p = [_xa_probs(qh, kh) for qh, kh in zip(q, k)]
            dp = [_dot(doh, kv_ref[:, c], NT) for doh, c in zip(do, vcols)]
            ds = [(ph * (dph - jnp.sum(ph * dph, axis=-1, keepdims=True)) * (XA_DH ** -0.5)).astype(BF16) for ph, dph in zip(p, dp)]
            for a in range(XA_GROUP):
                dkv_ref[:, vcols[a]] += _dot(p[a].astype(BF16), do[a], TN)
                dq_ref[:, cols[a]] = _dot(ds[a], k[a]).astype(BF16)
                dkv_ref[:, cols[a]] += _dot(ds[a], q[a], TN)

    kv_spec = pl.BlockSpec((MEM, 2 * D), lambda i: (0, 0))
    return pl.pallas_call(
        body, name="xattn_bwd", grid=(S // TR,), in_specs=[_row_spec(TR, D), kv_spec, _row_spec(TR, D)],
        out_specs=[_row_spec(TR, D), kv_spec],
        out_shape=[jax.ShapeDtypeStruct((S, D), BF16), jax.ShapeDtypeStruct((MEM, 2 * D), F32)],
        compiler_params=_params(("arbitrary",)),
    )(qx, kv, dxo)


CONV_HALO = SUBLANES
TC = 512
TC_FWD = 1024
GELU_K = 0.7978845608028654
GELU_C = 0.044715


def _conv3(ext, w, rows):
    h0 = ext[CONV_HALO:CONV_HALO + rows]
    h1 = pltpu.roll(ext, 1, 0)[CONV_HALO:CONV_HALO + rows]
    h2 = pltpu.roll(ext, 2, 0)[CONV_HALO:CONV_HALO + rows]
    return w[2:3] * h0 + w[1:2] * h1 + w[0:1] * h2 + w[3:4], (h2, h1, h0)


def _conv_specs(tc):
    main = pl.BlockSpec((2, TR, tc), lambda j, i: (0, i, j))
    per = TR // CONV_HALO
    prev = pl.BlockSpec((2, CONV_HALO, tc), lambda j, i: (0, jnp.maximum(i * per - 1, 0), j))
    nxt = pl.BlockSpec((2, CONV_HALO, tc), lambda j, i: (0, jnp.minimum((i + 1) * per, S // CONV_HALO - 1), j))
    par = pl.BlockSpec((2, SUBLANES, tc), lambda j, i: (0, 0, j))
    return main, prev, nxt, par


def _convgate_fwd(hid, cwb):
    tc = TC_FWD

    def body(h_ref, hp_ref, w_ref, act_ref):
        i = pl.program_id(1)
        c = []
        for g in range(2):
            ext = jnp.concatenate([jnp.where(i == 0, 0.0, hp_ref[g]), h_ref[g]], axis=0)
            c.append(_conv3(ext, w_ref[g], TR)[0])
        gate, up = c
        act_ref[...] = (jax.nn.gelu(gate, approximate=True) * up).astype(BF16)

    main, prev, _, par = _conv_specs(tc)
    return pl.pallas_call(
        body, name="convgate_fwd", grid=(D_FF // tc, S // TR), in_specs=[main, prev, par],
        out_specs=pl.BlockSpec((TR, tc), lambda j, i: (i, j)), out_shape=jax.ShapeDtypeStruct((S, D_FF), BF16),
        compiler_params=_params(("parallel", "parallel")),
    )(hid, hid, cwb)


def _convgate_bwd(hid, dact, cwb):
    nr = S // TR
    n = TR + CONV_HALO

    def body(h_ref, hp_ref, hn_ref, da_ref, dan_ref, w_ref, dh_ref, dw_ref):
        i = pl.program_id(1)

        @pl.when(i == 0)
        def _():
            dw_ref[...] = jnp.zeros_like(dw_ref)

        da = jnp.concatenate([da_ref[...], jnp.where(i == nr - 1, 0.0, dan_ref[...])], axis=0)
        c, taps = [], []
        for g in range(2):
            ext = jnp.concatenate([jnp.where(i == 0, 0.0, hp_ref[g]), h_ref[g], hn_ref[g]], axis=0)
            cg, tg = _conv3(ext, w_ref[g], n)
            c.append(cg)
            taps.append(tg)
        gate, up = c
        th = jnp.tanh(GELU_K * (gate + GELU_C * gate * gate * gate))
        gelu = 0.5 * gate * (1.0 + th)
        dgelu = 0.5 * (1.0 + th) + 0.5 * gate * (1.0 - th * th) * GELU_K * (1.0 + 3.0 * GELU_C * gate * gate)
        for g, dc in enumerate((da * up * dgelu, da * gelu)):
            w = w_ref[g]
            dh = w[2:3] * dc[:TR] + w[1:2] * pltpu.roll(dc, n - 1, 0)[:TR] + w[0:1] * pltpu.roll(dc, n - 2, 0)[:TR]
            dh_ref[g] = dh.astype(BF16)
            dcm = dc[:TR]
            for r in range(3):
                dw_ref[g, r:r + 1, :] += jnp.sum(dcm * taps[g][r][:TR], axis=0, keepdims=True)
            dw_ref[g, 3:4, :] += jnp.sum(dcm, axis=0, keepdims=True)

    main, prev, nxt, par = _conv_specs(TC)
    per = TR // CONV_HALO
    return pl.pallas_call(
        body, name="convgate_bwd", grid=(D_FF // TC, nr),
        in_specs=[main, prev, nxt, pl.BlockSpec((TR, TC), lambda j, i: (i, j)),
                  pl.BlockSpec((CONV_HALO, TC), lambda j, i: (jnp.minimum((i + 1) * per, S // CONV_HALO - 1), j)), par],
        out_specs=[main, par],
        out_shape=[jax.ShapeDtypeStruct((2, S, D_FF), BF16), jax.ShapeDtypeStruct((2, SUBLANES, D_FF), F32)],
        compiler_params=_params(("parallel", "arbitrary")),
    )(hid, hid, hid, dact, dact, cwb)


def _adam_update(w, g, m, v):
    m = ADAM_B1 * m + (1.0 - ADAM_B1) * g
    v = ADAM_B2 * v + (1.0 - ADAM_B2) * (g * g)
    m_hat = m / (1.0 - ADAM_B1 ** ADAM_STEP)
    v_hat = v / (1.0 - ADAM_B2 ** ADAM_STEP)
    return -ADAM_LR * (m_hat / (jnp.sqrt(v_hat) + ADAM_EPS) + ADAM_WD * w), m, v


def _row_tile(rows, cols, itemsize=4, target=TILE_BYTES):
    tr = SUBLANES
    while rows % (2 * tr) == 0 and 2 * tr * cols * itemsize <= target:
        tr *= 2
    assert rows % tr == 0, (rows, tr)
    return rows if rows % (2 * tr) and 16 * tr * cols * itemsize < target else tr


def _adamw(name, w, g, m, v):
    rows, cols = w.shape
    tr = rows if rows * cols * 4 <= TILE_BYTES // 2 else _row_tile(rows, cols, target=TILE_BYTES // 2)

    def body(w_ref, g_ref, m_ref, v_ref, d_ref, nm_ref, nv_ref):
        d_ref[...], nm_ref[...], nv_ref[...] = _adam_update(w_ref[...], g_ref[...], m_ref[...], v_ref[...])

    spec = _row_spec(tr, cols)
    shape = jax.ShapeDtypeStruct((rows, cols), F32)
    return pl.pallas_call(
        body, name=name, grid=(rows // tr,), in_specs=[spec] * 4, out_specs=[spec] * 3, out_shape=[shape] * 3,
        compiler_params=_params(("parallel",)),
    )(w, g, m, v)


def _adamw_halves(name, core, w, g_mine, g_sibling, m, v):
    rows, cols = w.shape
    half = rows // 2
    tr = _row_tile(half, cols, target=TILE_BYTES // 2)
    per = half // tr

    def body(core_ref, w_ref, gm_ref, gs_ref, m_ref, v_ref, g_ref, d_ref, nm_ref, nv_ref):
        g = jnp.where(pl.program_id(0) // per == core_ref[0], gm_ref[...], gs_ref[...])
        g_ref[...] = g
        d_ref[...], nm_ref[...], nv_ref[...] = _adam_update(w_ref[...], g, m_ref[...], v_ref[...])

    spec = pl.BlockSpec((tr, cols), lambda i, core_ref: (i, 0))
    half_spec = pl.BlockSpec((tr, cols), lambda i, core_ref: (i % per, 0))
    shape = jax.ShapeDtypeStruct((rows, cols), F32)
    return pl.pallas_call(
        body, name=name, out_shape=[shape] * 4,
        grid_spec=pltpu.PrefetchScalarGridSpec(
            num_scalar_prefetch=1, grid=(rows // tr,), in_specs=[spec, half_spec, half_spec, spec, spec], out_specs=[spec] * 4),
        compiler_params=_params(("parallel",)),
    )(core, w, g_mine, g_sibling, m, v)


def _adamw_halves_columns(name, core, w, g_mine, g_sibling, m, v):
    cols, _, rows = w.shape
    tl = 2 * LANES
    per = rows // 2 // tl

    def body(core_ref, w_ref, gm_ref, gs_ref, m_ref, v_ref, g_ref, d_ref, nm_ref, nv_ref):
        g = jnp.where(pl.program_id(0) // per == core_ref[0], gm_ref[...], gs_ref[...])
        g_ref[...] = g
        d_ref[...], nm_ref[...], nv_ref[...] = _adam_update(w_ref[...], g, m_ref[...], v_ref[...])

    spec = pl.BlockSpec((cols, 1, tl), lambda i, core_ref: (0, 0, i))
    half_spec = pl.BlockSpec((cols, 1, tl), lambda i, core_ref: (0, 0, i % per))
    shape = jax.ShapeDtypeStruct((cols, 1, rows), F32)
    return pl.pallas_call(
        body, name=name, out_shape=[shape] * 4,
        grid_spec=pltpu.PrefetchScalarGridSpec(
            num_scalar_prefetch=1, grid=(rows // tl,), in_specs=[spec, half_spec, half_spec, spec, spec], out_specs=[spec] * 4),
        compiler_params=_params(("parallel",)),
    )(core, w, g_mine, g_sibling, m, v)


def _chip_sum(name, core, g, other):
    _, _, half, cols = g.shape
    tr = _row_tile(half, cols)

    def body(core_ref, g_ref, o_ref, p_ref):
        p_ref[...] = (g_ref[...] + o_ref[...]).astype(BF16)

    spec = pl.BlockSpec((None, tr, cols), lambda j, i, core_ref: (j, i, 0))
    return pl.pallas_call(
        body, name=name, out_shape=jax.ShapeDtypeStruct((N_CHIPS, half, cols), BF16),
        grid_spec=pltpu.PrefetchScalarGridSpec(
            num_scalar_prefetch=1, grid=(N_CHIPS, half // tr),
            in_specs=[pl.BlockSpec((None, None, tr, cols), lambda j, i, core_ref: (j, core_ref[0], i, 0)), spec],
            out_specs=spec),
        compiler_params=_params(("parallel", "parallel")),
    )(core, g, other)


def _mesh_sum(name, chip, received, own):
    _, half, cols = received.shape
    tr = _row_tile(half, cols, itemsize=2 * N_CHIPS)

    def body(chip_ref, r_ref, own_ref, o_ref):
        acc = None
        for j in range(N_CHIPS):
            term = jnp.where(chip_ref[0] == j, own_ref[...], r_ref[j]).astype(F32)
            acc = term if acc is None else acc + term
        o_ref[...] = acc

    return pl.pallas_call(
        body, name=name, out_shape=jax.ShapeDtypeStruct((half, cols), F32),
        grid_spec=pltpu.PrefetchScalarGridSpec(
            num_scalar_prefetch=1, grid=(half // tr,),
            in_specs=[pl.BlockSpec((N_CHIPS, tr, cols), lambda i, chip_ref: (0, i, 0)),
                      pl.BlockSpec((None, tr, cols), lambda i, chip_ref: (chip_ref[0], i, 0))],
            out_specs=pl.BlockSpec((tr, cols), lambda i, chip_ref: (i, 0))),
        compiler_params=_params(("parallel",)),
    )(chip, received, own)


CHIP_FLIPS = ((1, 0), (0, 1), (1, 1))


def _place():
    x, y, c = lax.axis_index("x"), lax.axis_index("y"), lax.axis_index("c")
    return x, y, c, 2 * x + y


def _remote(src, dst, sems_s, sems_r, k, dev):
    return pltpu.make_async_remote_copy(src_ref=src, dst_ref=dst, send_sem=sems_s.at[k], recv_sem=sems_r.at[k],
                                        device_id=dev, device_id_type=MESH)


class _Exchange:
    def __init__(self, ins, out_shapes, n_sems, start, forward, finish):
        self.ins, self.out_shapes, self.n_sems = list(ins), list(out_shapes), n_sems
        self.start, self.forward, self.finish = start, forward, finish

    def scratch(self):
        return [pltpu.SemaphoreType.DMA((self.n_sems,)), pltpu.SemaphoreType.DMA((self.n_sems,))]

    def run(self, name):
        n = len(self.ins)

        def body(*refs):
            args = (refs[:n], refs[n:2 * n]) + tuple(refs[2 * n:])
            self.start(*args)
            self.forward(*args)
            self.finish(*args)

        return pl.pallas_call(
            body, name=name, in_specs=[ANY] * n, out_specs=[ANY] * n, out_shape=self.out_shapes, scratch_shapes=self.scratch(),
        )(*self.ins)


def _all_gather_weights(halved, whole):
    nh, nw = len(halved), len(whole)
    n_arr = nh + nw

    def copies(ins, outs, sems_s, sems_r):
        x, y, c, me = _place()
        sibling = (x, y, 1 - c)
        own = [_remote(ins[k], outs[k].at[me], sems_s, sems_r, k, sibling) for k in range(n_arr)]
        first, passed = [], []
        for k in range(n_arr):
            for f, (fx, fy) in enumerate(CHIP_FLIPS):
                src, dst = (ins[k].at[c], outs[k].at[me, c]) if k < nh else (ins[k], outs[k].at[me])
                first.append(_remote(src, dst, sems_s, sems_r, n_arr + 3 * k + f, (x ^ fx, y ^ fy, c)))
        for k in range(nh):
            for f, (fx, fy) in enumerate(CHIP_FLIPS):
                landed = outs[k].at[2 * (x ^ fx) + (y ^ fy), c]
                passed.append(_remote(landed, landed, sems_s, sems_r, 4 * n_arr + 3 * k + f, sibling))
        return own, first, passed

    def start(*refs):
        own, first, _ = copies(*refs)
        for cp in own + first:
            cp.start()

    def forward(*refs):
        _, first, passed = copies(*refs)
        for arrived, cp in zip(first, passed):
            arrived.wait_recv()
            cp.start()

    def finish(*refs):
        own, first, passed = copies(*refs)
        for cp in first[3 * nh:] + passed + own:
            cp.wait_recv()
        for cp in first + passed + own:
            cp.wait_send()

    shapes = [jax.ShapeDtypeStruct((N_CHIPS,) + a.shape, a.dtype) for a in list(halved) + list(whole)]
    return _Exchange(list(halved) + list(whole), shapes, 7 * nh + 4 * nw, start, forward, finish)


def _swap_halves(gs):
    n = len(gs)

    def copies(ins, outs, sems_s, sems_r):
        x, y, c, _ = _place()
        return [_remote(ins[k].at[:, 1 - c], outs[k], sems_s, sems_r, k, (x, y, 1 - c)) for k in range(n)]

    def start(*refs):
        for cp in copies(*refs):
            cp.start()

    def finish(*refs):
        for cp in copies(*refs):
            cp.wait()

    shapes = [jax.ShapeDtypeStruct((g.shape[0],) + g.shape[2:], g.dtype) for g in gs]
    return _Exchange(gs, shapes, n, start, _no_copies, finish)


def _scatter_chips(ps):
    n = len(ps)

    def copies(ins, outs, sems_s, sems_r):
        x, y, c, me = _place()
        return [_remote(ins[k].at[2 * (x ^ fx) + (y ^ fy)], outs[k].at[me], sems_s, sems_r, 3 * k + f, (x ^ fx, y ^ fy, c))
                for k in range(n) for f, (fx, fy) in enumerate(CHIP_FLIPS)]

    def start(*refs):
        for cp in copies(*refs):
            cp.start()

    def forward(*refs):
        pass

    def finish(*refs):
        for cp in copies(*refs):
            cp.wait()

    shapes = [jax.ShapeDtypeStruct(p.shape, p.dtype) for p in ps]
    return _Exchange(ps, shapes, 3 * n, start, forward, finish)


def _swap_reduced(rs):
    n = len(rs)

    def copies(ins, outs, sems_s, sems_r):
        x, y, c, _ = _place()
        return [_remote(ins[k], outs[k], sems_s, sems_r, k, (x, y, 1 - c)) for k in range(n)]

    def start(*refs):
        for cp in copies(*refs):
            cp.start()

    def finish(*refs):
        for cp in copies(*refs):
            cp.wait()

    return _Exchange(rs, [jax.ShapeDtypeStruct(r.shape, r.dtype) for r in rs], n, start, _no_copies, finish)


N_DEV = 8


def _gather_small(buf):
    def copies(ins, outs, sems_s, sems_r):
        x, y, c, chip = _place()
        sibling = (x, y, 1 - c)
        own = _remote(ins[0], outs[0].at[2 * chip + c], sems_s, sems_r, 0, sibling)
        first = [_remote(ins[0], outs[0].at[2 * chip + c], sems_s, sems_r, 1 + f, (x ^ fx, y ^ fy, c))
                 for f, (fx, fy) in enumerate(CHIP_FLIPS)]
        passed = []
        for f, (fx, fy) in enumerate(CHIP_FLIPS):
            landed = outs[0].at[2 * (2 * (x ^ fx) + (y ^ fy)) + c]
            passed.append(_remote(landed, landed, sems_s, sems_r, 4 + f, sibling))
        return own, first, passed

    def start(*refs):
        own, first, _ = copies(*refs)
        for cp in [own] + first:
            cp.start()

    def forward(*refs):
        _, first, passed = copies(*refs)
        for arrived, cp in zip(first, passed):
            arrived.wait_recv()
            cp.start()

    def finish(*refs):
        own, first, passed = copies(*refs)
        for cp in passed + [own]:
            cp.wait_recv()
        for cp in first + passed + [own]:
            cp.wait_send()

    return _Exchange([buf], [jax.ShapeDtypeStruct((N_DEV,) + buf.shape, buf.dtype)], N_DEV - 1, start, forward, finish)


def _sum_devices(place, gathered, own):
    rows = own.shape[0]

    def body(place_ref, g_ref, own_ref, o_ref):
        acc = None
        for d in range(N_DEV):
            term = jnp.where(place_ref[0] == d, own_ref[...], g_ref[d])
            acc = term if acc is None else acc + term
        o_ref[...] = acc

    return pl.pallas_call(
        body, name="sum_devices", out_shape=jax.ShapeDtypeStruct((rows, LANES), F32),
        grid_spec=pltpu.PrefetchScalarGridSpec(
            num_scalar_prefetch=1, grid=(1,),
            in_specs=[pl.BlockSpec((N_DEV, rows, LANES), lambda i, place_ref: (0, 0, 0)),
                      pl.BlockSpec((rows, LANES), lambda i, place_ref: (0, 0))],
            out_specs=pl.BlockSpec((rows, LANES), lambda i, place_ref: (0, 0))),
        compiler_params=_params(("arbitrary",)),
    )(place, gathered, own)


def _no_copies(*refs):
    pass


def _no_exchange():
    return _Exchange([], [], 1, _no_copies, _no_copies, _no_copies)


class _NoComm:
    def gather_first(self):
        return _no_exchange()

    def first_landed(self, p, landed):
        pass

    def gather_rest(self, p):
        return _no_exchange()

    def weights_landed(self, p, landed):
        pass

    def gather_last(self):
        return _no_exchange()

    def last_landed(self, p, landed):
        pass

    def swap_first(self, g):
        return _no_exchange()

    def first_swapped(self, landed):
        pass

    def swap_second(self, g):
        return _no_exchange()

    def second_swapped(self, landed):
        pass

    def scatter_early(self, g):
        return _no_exchange()

    def scatter_landed(self, landed):
        pass

    def swap_reduced_early(self):
        return _no_exchange()

    def reduced_landed(self, landed):
        pass

    def scatter_late(self, g):
        return _no_exchange()

    def late_landed(self, landed):
        pass


def _local_step(x, mem, target, p, comm):
    h1, landed = _norm_fwd("norm_mix_pre", x, p["norm_mix_pre"], comm.gather_first())
    comm.first_landed(p, landed)
    qa, ka, va, u, z = _in_proj(h1, p["w_in"], p["bf_pad"])
    ycat, qab, landed = _fox_fwd(qa, ka, va, comm.gather_rest(p))
    comm.weights_landed(p, landed)
    ycat = _pool_fwd(u, p["w_pool_bd"], p["pool_scale"], ycat)
    y1, x2, h2, qx = _proj_resid_norm("mix_out", ycat, p["w_mix_out"], x, p["norm_mix_post"], p["norm_xa_pre"], p["w_xq"])
    mem_n = _norm_fwd("norm_mem", mem, p["norm_mem"])
    kv = _mm(
        "xkv", mem_n, p["w_xkv"], pl.BlockSpec((MEM, D), lambda i, j, k: (0, 0)),
        pl.BlockSpec((None, D, 512), lambda i, j, k: (j, 0, 0)), jax.ShapeDtypeStruct((MEM, 2 * D), BF16),
        pl.BlockSpec((MEM, 512), lambda i, j, k: (0, j)), (1, N_CHIPS, 1), NN, (MEM, 512))
    xo = _xattn_fwd(qx, kv)
    y2, x3, h3 = _proj_resid_norm("xo", xo, p["w_xo"], x2, p["norm_xa_post"], p["norm_ffn_pre"])
    hid, landed = _mm(
        "up_proj", h3, p["w_up"], pl.BlockSpec((2048, D), lambda i, j, k: (i, 0)),
        pl.BlockSpec((None, D, 1024), lambda i, j, k: (j // 2, 0, j % 2)), jax.ShapeDtypeStruct((2, S, D_FF), F32),
        pl.BlockSpec((None, 2048, 1024), lambda i, j, k: (j // 4, i, j % 4)), (S // 2048, 8, 1), NN, (2048, 1024),
        comm.gather_last())
    comm.last_landed(p, landed)
    act = _convgate_fwd(hid, p["cwb"])

    g = {}
    dres, dy3, g["norm_ffn_post"], loss_cols = _down_loss_bwd(act, p["w_down"], x3, p["norm_ffn_post"], target)
    dact = _mm_nt("d_act", dy3, p["w_down"], F32, 2048, 1024)
    g["w_down"] = _mm_tn("dw_down", act, dy3, 1024, 512)
    dhid, dcwb = _convgate_bwd(hid, dact, p["cwb"])
    g["w_up"] = _mm(
        "dw_up", h3, dhid, pl.BlockSpec((S, D), lambda i, j, k: (0, 0)),
        pl.BlockSpec((None, S, 512), lambda i, j, k: (j // 8, 0, j % 8)), jax.ShapeDtypeStruct((N_CHIPS, D, 2048), F32),
        pl.BlockSpec((None, D, 512), lambda i, j, k: (j // 4, 0, j % 4)), (1, 16, 1), TN, (D, 512))
    dh3, landed = _d_h3(dhid, p["w_up"], comm.swap_first(g))
    comm.first_swapped(landed)
    dres, dy2, dxo, g["norm_ffn_pre"], g["norm_xa_post"] = _mid_bwd(
        "bwd_ffn_xa", dres, x3, p["norm_ffn_pre"], dh3, y2, p["norm_xa_post"], p["w_xo"])
    g["w_xo"] = _mm_tn("dw_xo", xo, dy2, 1024, 512)
    dqx, dkv = _xattn_bwd(qx, kv, dxo)
    dkv = dkv.astype(BF16)
    g["w_xq"] = _mm_tn("dw_xq", h2, dqx, 1024, 512)
    dmem_n = _mm(
        "d_mem", dkv, p["w_xkv"], pl.BlockSpec((MEM, 512), lambda i, j, k: (0, k)),
        pl.BlockSpec((None, D, 512), lambda i, j, k: (k, 0, 0)), jax.ShapeDtypeStruct((MEM, D), F32),
        pl.BlockSpec((MEM, D), lambda i, j, k: (0, 0)), (1, 1, N_CHIPS), NT, (MEM, D))
    g["w_xkv"] = _mm(
        "dw_xkv", mem_n, dkv, pl.BlockSpec((MEM, D), lambda i, j, k: (0, 0)),
        pl.BlockSpec((MEM, 512), lambda i, j, k: (0, j)), jax.ShapeDtypeStruct((N_CHIPS, D, 512), F32),
        pl.BlockSpec((None, D, 512), lambda i, j, k: (j, 0, 0)), (1, N_CHIPS, 1), TN, (D, 512))
    g["norm_mem"] = _gain_bwd("dg_mem", mem, p["norm_mem"], dmem_n)
    (dres, dy1, g["norm_xa_pre"], g["norm_mix_post"], dy_pool, doa), landed = _bwd_xa_mix(
        dqx, p["w_xq"], dres, x2, p["norm_xa_pre"], y1, p["norm_mix_post"], p["w_mix_out"], ycat, comm.swap_second(g))
    comm.second_swapped(landed)
    g["w_mix_out"] = _mm_tn("dw_mix_out", ycat, dy1, 1024, 512)
    dqa, dka, dva, landed = _fox_bwd(qab, doa, ka, va, comm.scatter_early(g))
    comm.scatter_landed(landed)
    du, g["w_pool_full"], g["pool_scale"] = _pool_bwd(u, dy_pool, p["w_pool_bd"], p["w_pool_bd_t"], p["pool_scale"])
    dproj, g["bf_pad"] = _fox_bwd_post(dqa, dka, dva, du, z, p["bf_pad"])
    g["w_in"], landed = _dw_in(dproj, h1, comm.swap_reduced_early())
    comm.reduced_landed(landed)
    grad_x, g["norm_mix_pre"], landed = _d_h1_first_bwd(dproj, p["w_in"], dres, x, p["norm_mix_pre"], comm.scatter_late(g))
    comm.late_landed(landed)
    g["cwb"] = dcwb
    return grad_x, g, loss_cols


BIG = ("w_in", "w_mix_out", "w_xq", "w_xkv", "w_xo", "w_up", "w_down")
ROW_SHARDED = ("w_mix_out", "w_xq", "w_xo", "w_down")
SMALL = ("norm_mix_pre", "norm_mix_post", "b_forget", "w_pool", "pool_scale", "norm_mem", "norm_xa_pre", "norm_xa_post",
         "norm_ffn_pre", "norm_ffn_post", "conv_b")
ORDER = ("norm_mix_pre", "norm_mix_post", "w_in", "b_forget", "w_pool", "pool_scale", "w_mix_out", "norm_mem", "norm_xa_pre",
         "norm_xa_post", "w_xq", "w_xkv", "w_xo", "norm_ffn_pre", "norm_ffn_post", "w_up", "conv_w", "conv_b", "w_down")
SLOT = SUBLANES * LANES


def _pack(parts):
    rows, offs, off = [], [], 0
    for a in parts:
        flat = a.reshape(-1).astype(F32)
        n = -(-flat.shape[0] // SLOT) * SLOT
        rows.append(jnp.pad(flat, (0, n - flat.shape[0])).reshape(n // LANES, LANES))
        offs.append(off)
        off += n // LANES
    return jnp.concatenate(rows, axis=0), offs


def _unpack(buf, off, like):
    n = like.size
    rows = -(-n // LANES)
    return buf[off:off + rows].reshape(-1)[:n].reshape(like.shape)


FIRST = ("w_in",)
REST = ("w_mix_out", "w_xq", "w_xkv", "w_xo", "w_up")
LAST = ("w_down",)


def _local_params(w):
    w_pool_bd = jnp.zeros((D_POOL, D_POOL), F32)
    for gi in range(4):
        w_pool_bd = w_pool_bd.at[64 * gi:64 * (gi + 1), 64 * gi:64 * (gi + 1)].set(w["w_pool"][0, gi])
    p = {n: w[n] for n in ("norm_mix_pre", "norm_mix_post", "norm_mem", "norm_xa_pre", "norm_xa_post", "norm_ffn_pre",
                           "norm_ffn_post")}
    p.update(
        bf_pad=jnp.pad(w["b_forget"], ((0, 0), (0, LANES - HEADS))),
        w_pool_bd=w_pool_bd.astype(BF16), w_pool_bd_t=w_pool_bd.T.astype(BF16), pool_scale=w["pool_scale"].reshape(1, D_POOL))
    return p


def _w_in_param(stacked):
    n, rows, cols = stacked.shape
    tr = PREP_TR

    def body(w_ref, o_ref):
        o_ref[...] = jnp.concatenate([w_ref[j] for j in range(n)] + [jnp.zeros((tr, D_IN_PAD - n * cols), BF16)], axis=1)

    return pl.pallas_call(
        body, name="w_in_whole", grid=(rows // tr,), in_specs=[pl.BlockSpec((n, tr, cols), lambda i: (0, i, 0))],
        out_specs=_row_spec(tr, D_IN_PAD), out_shape=jax.ShapeDtypeStruct((rows, D_IN_PAD), BF16),
        compiler_params=_params(("parallel",)),
    )(stacked)


def _rest_params(w, full, conv_w_full):
    cw2 = conv_w_full.reshape(3, 2, D_FF).transpose(1, 0, 2)
    cwb = jnp.concatenate([cw2, w["conv_b"].reshape(1, 2, D_FF).transpose(1, 0, 2), jnp.zeros((2, 4, D_FF), F32)], axis=1)
    return dict(w_mix_out=full["w_mix_out"].reshape(D, D), w_xq=full["w_xq"].reshape(D, D), w_xkv=full["w_xkv"],
                w_xo=full["w_xo"].reshape(D, D), w_up=full["w_up"], cwb=cwb)


def _whole_params(w, full, conv_w_full):
    p = _local_params(w)
    p.update(_rest_params(w, full, conv_w_full), w_in=_w_in_param(full["w_in"]), w_down=full["w_down"].reshape(D_FF, D))
    return p


def _halved(a):
    return a.reshape(a.shape[:-2] + (2, a.shape[-2] // 2, a.shape[-1]))


class _StepComm:
    def __init__(self, w, shard2d, conv_w, core_id, chip_id):
        self.w, self.shard2d, self.conv_w, self.core_id, self.chip_id = w, shard2d, conv_w, core_id, chip_id
        self.first, self.second = ("w_up", "w_down"), ("w_xq", "w_xkv", "w_xo")
        self.early = self.first + self.second
        self.late = ("w_in", "w_mix_out")

    def gather_first(self):
        return _all_gather_weights([_halved(self.shard2d[n].astype(BF16)) for n in FIRST], [])

    def first_landed(self, p, landed):
        p["w_in"] = _w_in_param(landed[0].reshape((N_CHIPS,) + self.shard2d["w_in"].shape))

    def gather_rest(self, p):
        return _all_gather_weights([_halved(self.shard2d[n].astype(BF16)) for n in REST], [self.conv_w.reshape(3, -1)])

    def weights_landed(self, p, landed):
        full = {n: a.reshape((N_CHIPS,) + self.shard2d[n].shape) for n, a in zip(REST, landed)}
        conv_w_full = jnp.transpose(landed[-1], (1, 0, 2)).reshape(3, 2 * D_FF)
        p.update(_rest_params(self.w, full, conv_w_full))

    def gather_last(self):
        return _all_gather_weights([_halved(self.shard2d[n].astype(BF16)) for n in LAST], [])

    def last_landed(self, p, landed):
        p["w_down"] = landed[0].reshape(D_FF, D)

    def _view(self, g, n):
        return _halved(g[n].reshape((N_CHIPS,) + self.shard2d[n].shape))

    def swap_first(self, g):
        return _swap_halves([self._view(g, n) for n in self.first])

    def first_swapped(self, landed):
        self.from_sibling = dict(zip(self.first, landed))

    def swap_second(self, g):
        return _swap_halves([self._view(g, n) for n in self.second])

    def second_swapped(self, landed):
        self.from_sibling.update(zip(self.second, landed))

    def scatter_early(self, g):
        self.partial = [_chip_sum("chip_sum_" + n, self.core_id, self._view(g, n), self.from_sibling[n]) for n in self.early]
        return _scatter_chips(self.partial)

    def scatter_landed(self, landed):
        self.received = list(landed)

    def swap_reduced_early(self):
        self.reduced = [_mesh_sum("mesh_sum_" + n, self.chip_id, r, own)
                        for n, r, own in zip(self.early, self.received, self.partial)]
        return _swap_reduced(self.reduced)

    def reduced_landed(self, landed):
        self.reduced_sibling = list(landed)

    def scatter_late(self, g):
        views = [g["w_in"], self._view(g, "w_mix_out")]
        from_sibling = _swap_halves(views).run("swap_halves_late")
        self.partial_late = [_chip_sum("chip_sum_" + n, self.core_id, view, other)
                             for n, view, other in zip(self.late, views, from_sibling)]
        return _scatter_chips(self.partial_late)

    def late_landed(self, landed):
        self.received_late = list(landed)


def kernel(x, mem, norm_mix_pre, norm_mix_post, w_in, b_forget, w_pool, pool_scale, w_mix_out, norm_mem, norm_xa_pre, norm_xa_post, w_xq, w_xkv, w_xo, norm_ffn_pre, norm_ffn_post, w_up, conv_w, conv_b, w_down, loss_target, m_norm_mix_pre, m_norm_mix_post, m_w_in, m_b_forget, m_w_pool, m_pool_scale, m_w_mix_out, m_norm_mem, m_norm_xa_pre, m_norm_xa_post, m_w_xq, m_w_xkv, m_w_xo, m_norm_ffn_pre, m_norm_ffn_post, m_w_up, m_conv_w, m_conv_b, m_w_down, v_norm_mix_pre, v_norm_mix_post, v_w_in, v_b_forget, v_w_pool, v_pool_scale, v_w_mix_out, v_norm_mem, v_norm_xa_pre, v_norm_xa_post, v_w_xq, v_w_xkv, v_w_xo, v_norm_ffn_pre, v_norm_ffn_post, v_w_up, v_conv_w, v_conv_b, v_w_down):
    w = dict(norm_mix_pre=norm_mix_pre, norm_mix_post=norm_mix_post, w_in=w_in, b_forget=b_forget, w_pool=w_pool,
             pool_scale=pool_scale, w_mix_out=w_mix_out, norm_mem=norm_mem, norm_xa_pre=norm_xa_pre, norm_xa_post=norm_xa_post,
             w_xq=w_xq, w_xkv=w_xkv, w_xo=w_xo, norm_ffn_pre=norm_ffn_pre, norm_ffn_post=norm_ffn_post, w_up=w_up,
             conv_w=conv_w, conv_b=conv_b, w_down=w_down)
    m = dict(norm_mix_pre=m_norm_mix_pre, norm_mix_post=m_norm_mix_post, w_in=m_w_in, b_forget=m_b_forget, w_pool=m_w_pool,
             pool_scale=m_pool_scale, w_mix_out=m_w_mix_out, norm_mem=m_norm_mem, norm_xa_pre=m_norm_xa_pre,
             norm_xa_post=m_norm_xa_post, w_xq=m_w_xq, w_xkv=m_w_xkv, w_xo=m_w_xo, norm_ffn_pre=m_norm_ffn_pre,
             norm_ffn_post=m_norm_ffn_post, w_up=m_w_up, conv_w=m_conv_w, conv_b=m_conv_b, w_down=m_w_down)
    v = dict(norm_mix_pre=v_norm_mix_pre, norm_mix_post=v_norm_mix_post, w_in=v_w_in, b_forget=v_b_forget, w_pool=v_w_pool,
             pool_scale=v_pool_scale, w_mix_out=v_w_mix_out, norm_mem=v_norm_mem, norm_xa_pre=v_norm_xa_pre,
             norm_xa_post=v_norm_xa_post, w_xq=v_w_xq, w_xkv=v_w_xkv, w_xo=v_w_xo, norm_ffn_pre=v_norm_ffn_pre,
             norm_ffn_post=v_norm_ffn_post, w_up=v_w_up, conv_w=v_conv_w, conv_b=v_conv_b, w_down=v_w_down)
    chip = 2 * lax.axis_index("x") + lax.axis_index("y")

    core_id = lax.axis_index("c").astype(jnp.int32).reshape(1)
    chip_id = chip.astype(jnp.int32).reshape(1)

    shard2d = {n: w[n][0] for n in BIG}
    p = _local_params(w)
    comm = _StepComm(w, shard2d, conv_w, core_id, chip_id)
    grad_x, g, loss_cols = _local_step(x[0], mem[0], loss_target[0], p, comm)

    reduced_late = [_mesh_sum("mesh_sum_" + n, chip_id, r, own)
                    for n, r, own in zip(comm.late, comm.received_late, comm.partial_late)]
    names = comm.late + comm.early
    reduced = reduced_late + comm.reduced
    reduced_sibling = list(_swap_reduced(reduced_late).run("swap_reduced_late")) + comm.reduced_sibling
    grads = {}

    gw_pool = jnp.stack([g["w_pool_full"][64 * gi:64 * (gi + 1), 64 * gi:64 * (gi + 1)] for gi in range(4)])
    dcwb = g["cwb"]
    g_conv_w = dcwb[:, 0:3, :].transpose(1, 0, 2).reshape(3, 2 * D_FF)
    g_conv_b = dcwb[:, 3, :].reshape(2 * D_FF)
    small_g = dict(norm_mix_pre=g["norm_mix_pre"], norm_mix_post=g["norm_mix_post"], b_forget=g["bf_pad"][:, :HEADS],
                   w_pool=gw_pool, pool_scale=g["pool_scale"], norm_mem=g["norm_mem"], norm_xa_pre=g["norm_xa_pre"],
                   norm_xa_post=g["norm_xa_post"], norm_ffn_pre=g["norm_ffn_pre"], norm_ffn_post=g["norm_ffn_post"],
                   conv_b=g_conv_b)
    local_buf, offs = _pack([small_g[n] for n in SMALL] + [g_conv_w, loss_cols])

    delta, new_m, new_v = {}, {}, {}
    for n, g_mine, g_sibling in zip(names, reduced, reduced_sibling):
        cols = shard2d[n].shape[1]
        if cols % LANES:
            outs = _adamw_halves_columns("adamw_" + n, core_id, jnp.transpose(w[n], (2, 0, 1)), g_mine[:cols, None, :],
                                         g_sibling[:cols, None, :], jnp.transpose(m[n], (2, 0, 1)), jnp.transpose(v[n], (2, 0, 1)))
            gn, d, nm, nv = (jnp.transpose(o, (1, 2, 0)) for o in outs)
        else:
            gn, d, nm, nv = (o[None] for o in _adamw_halves("adamw_" + n, core_id, shard2d[n], g_mine, g_sibling, m[n][0], v[n][0]))
        grads[n], delta[n], new_m[n], new_v[n] = gn, d, nm, nv
    place = (2 * chip + lax.axis_index("c")).astype(jnp.int32).reshape(1)
    buf = _sum_devices(place, _gather_small(local_buf).run("gather_small")[0], local_buf)
    for n, off in zip(SMALL, offs):
        grads[n] = _unpack(buf, off, w[n])
    g_conv_w = _unpack(buf, offs[len(SMALL)], g_conv_w)
    grads["conv_w"] = lax.dynamic_slice_in_dim(g_conv_w, chip * (2 * D_FF // N_CHIPS), 2 * D_FF // N_CHIPS, axis=1).reshape(conv_w.shape)
    loss = jnp.sum(_unpack(buf, offs[len(SMALL) + 1], loss_cols))
    small_names = SMALL + ("conv_w",)
    packed = [_pack([d[n] for n in small_names])[0] for d in (w, grads, m, v)]
    offs = _pack([w[n] for n in small_names])[1]
    d, nm, nv = _adamw("adamw_small", *packed)
    for n, off in zip(small_names, offs):
        delta[n], new_m[n], new_v[n] = _unpack(d, off, w[n]), _unpack(nm, off, w[n]), _unpack(nv, off, w[n])

    return (loss, grad_x[None], *[grads[n] for n in ORDER], *[delta[n] for n in ORDER], *[new_m[n] for n in ORDER],
            *[new_v[n] for n in ORDER])
```
